```python
import jax, jax.numpy as jnp
from jax import lax
import numpy as np

D_MODEL = 1024
BATCH = 8
SEQ = 8192
DEPTH = 1

HEAD_DIM = 64
N_Q_HEADS = 8
N_KV_HEADS = 2
Q_PER_KV = N_Q_HEADS // N_KV_HEADS
ATTN_WIDTH = N_Q_HEADS * HEAD_DIM
KV_WIDTH = N_KV_HEADS * HEAD_DIM
WINDOW = 128
BLOCK = 128
N_BUCKETS = 32
MAX_DISTANCE = 128

SSM_HEAD_DIM = 64
SSM_HEADS = 8
SSM_GROUPS = 2
HEADS_PER_GROUP = SSM_HEADS // SSM_GROUPS
SSM_WIDTH = SSM_HEADS * SSM_HEAD_DIM
D_STATE = 128
CONV_K = 4
CHUNK = 128
XBC_WIDTH = SSM_WIDTH + 2 * SSM_GROUPS * D_STATE

MIX_WIDTH = ATTN_WIDTH + SSM_WIDTH
IN_WIDTH = ATTN_WIDTH + 2 * KV_WIDTH + SSM_WIDTH + XBC_WIDTH + SSM_HEADS
D_FF = -(-8 * D_MODEL // (3 * 256)) * 256
EPS = 1e-6

kernel_name = "hymba_swa_sink_ssd_adaln_block"


def rmsnorm(x, g):
    xf = x.astype(jnp.float32)
    y = xf * lax.rsqrt(jnp.mean(xf * xf, axis=-1, keepdims=True) + EPS)
    return (y * g.astype(jnp.float32)).astype(x.dtype)


def t5_buckets(dist):
    n = np.maximum(dist, 0)
    max_exact = N_BUCKETS // 2
    large = max_exact + (np.log(np.maximum(n, 1) / max_exact) / np.log(MAX_DISTANCE / max_exact)
                         * (N_BUCKETS - max_exact)).astype(np.int32)
    large = np.minimum(large, N_BUCKETS - 1)
    return np.where(n < max_exact, n, large).astype(np.int32)


def sliding_window_attention(q, k, v, sinks, rel_bias):
    b, s, _ = q.shape
    nb = s // BLOCK
    qb = q.reshape(b, nb, BLOCK, N_KV_HEADS, Q_PER_KV, HEAD_DIM)

    def band(t):
        t = t.reshape(b, s, N_KV_HEADS, HEAD_DIM)
        t = jnp.pad(t, ((0, 0), (BLOCK, 0), (0, 0), (0, 0)))
        t = t.reshape(b, nb + 1, BLOCK, N_KV_HEADS, HEAD_DIM)
        return jnp.concatenate([t[:, :-1], t[:, 1:]], axis=2)

    kb, vb = band(k), band(v)
    dist = np.arange(BLOCK)[:, None] + BLOCK - np.arange(2 * BLOCK)[None, :]
    key_pos = np.arange(nb)[:, None] * BLOCK - BLOCK + np.arange(2 * BLOCK)[None, :]
    mask = ((dist >= 0) & (dist < WINDOW))[None] & (key_pos >= 0)[:, None, :]
    mask = mask.reshape(nb, 1, 1, BLOCK, 2 * BLOCK)
    bias = rel_bias.astype(jnp.float32)[t5_buckets(dist)]
    bias = jnp.transpose(bias, (2, 0, 1)).reshape(N_KV_HEADS, Q_PER_KV, BLOCK, 2 * BLOCK)

    scores = jnp.einsum("bnqkgd,bnskd->bnkgqs", qb, kb).astype(jnp.float32)
    scores = scores * (HEAD_DIM ** -0.5) + bias
    scores = jnp.where(mask, scores, -jnp.inf)
    sink = sinks.astype(jnp.float32).reshape(N_KV_HEADS, Q_PER_KV, 1, 1)
    m = jnp.maximum(jnp.max(scores, axis=-1, keepdims=True), sink)
    p = jnp.exp(scores - m)
    denom = jnp.sum(p, axis=-1, keepdims=True) + jnp.exp(sink - m)
    out = jnp.einsum("bnkgqs,bnskd->bnkgqd", p, vb.astype(jnp.float32)) / denom
    out = jnp.transpose(out, (0, 1, 4, 2, 3, 5)).reshape(b, s, ATTN_WIDTH)
    return out.astype(q.dtype)


def ssd_scan(xs, dt, A, Bm, Cm, D_skip):
    b, s = xs.shape[:2]
    nc = s // CHUNK
    xs = xs.astype(jnp.float32)
    xdt = xs * dt[..., None]
    xc = xdt.reshape(b, nc, CHUNK, SSM_GROUPS, HEADS_PER_GROUP, SSM_HEAD_DIM)
    Bc = Bm.astype(jnp.float32).reshape(b, nc, CHUNK, SSM_GROUPS, D_STATE)
    Cc = Cm.astype(jnp.float32).reshape(b, nc, CHUNK, SSM_GROUPS, D_STATE)
    dtA = (dt * A).reshape(b, nc, CHUNK, SSM_GROUPS, HEADS_PER_GROUP)
    Acs = jnp.cumsum(jnp.moveaxis(dtA, 2, -1), axis=-1)

    causal = np.tril(np.ones((CHUNK, CHUNK), dtype=bool))
    seg = Acs[..., :, None] - Acs[..., None, :]
    Lmat = jnp.exp(jnp.where(causal, seg, -jnp.inf))
    CB = jnp.einsum("bclgn,bcsgn->bcgls", Cc, Bc)
    W = CB[:, :, :, None] * Lmat
    y_diag = jnp.einsum("bcgrls,bcsgrp->bclgrp", W, xc)

    decay_states = jnp.exp(Acs[..., -1:] - Acs)
    states = jnp.einsum("bclgn,bcgrl,bclgrp->bcgrpn", Bc, decay_states, xc)
    chunk_decay = jnp.exp(Acs[..., -1])

    def step(h, inp):
        s_c, d_c = inp
        return h * d_c[..., None, None] + s_c, h

    h0 = jnp.zeros((b, SSM_GROUPS, HEADS_PER_GROUP, SSM_HEAD_DIM, D_STATE), jnp.float32)
    _, prev = lax.scan(step, h0, (jnp.moveaxis(states, 1, 0), jnp.moveaxis(chunk_decay, 1, 0)))
    prev = jnp.moveaxis(prev, 0, 1)
    y_off = jnp.einsum("bclgn,bcgrpn,bcgrl->bclgrp", Cc, prev, jnp.exp(Acs))

    y = (y_diag + y_off).reshape(b, s, SSM_GROUPS, HEADS_PER_GROUP, SSM_HEAD_DIM)
    y = y + D_skip.astype(jnp.float32)[:, :, None] * xs
    return y


def hybrid_mixer(h, w_in, conv_w, conv_b, dt_bias, A_log, D_skip, sinks,
                 attn_out_norm, ssm_out_norm, w_o, rel_bias):
    b, s, _ = h.shape
    proj = h @ w_in
    o1 = ATTN_WIDTH
    o2 = o1 + KV_WIDTH
    o3 = o2 + KV_WIDTH
    o4 = o3 + SSM_WIDTH
    o5 = o4 + XBC_WIDTH
    q, k, v = proj[..., :o1], proj[..., o1:o2], proj[..., o2:o3]
    z, xbc, dt_raw = proj[..., o3:o4], proj[..., o4:o5], proj[..., o5:]

    y_attn = sliding_window_attention(q, k, v, sinks, rel_bias)
    y_attn = rmsnorm(y_attn, attn_out_norm)

    xbc = lax.conv_general_dilated(xbc, conv_w[:, None, :], window_strides=(1,),
                                   padding=[(CONV_K - 1, 0)],
                                   dimension_numbers=("NWC", "WIO", "NWC"),
                                   feature_group_count=XBC_WIDTH)
    xbc = jax.nn.silu(xbc + conv_b)
    xs = xbc[..., :SSM_WIDTH].reshape(b, s, SSM_GROUPS, HEADS_PER_GROUP, SSM_HEAD_DIM)
    Bm = xbc[..., SSM_WIDTH:SSM_WIDTH + SSM_GROUPS * D_STATE].reshape(b, s, SSM_GROUPS, D_STATE)
    Cm = xbc[..., SSM_WIDTH + SSM_GROUPS * D_STATE:].reshape(b, s, SSM_GROUPS, D_STATE)
    dt = jax.nn.softplus(dt_raw.astype(jnp.float32) + dt_bias.astype(jnp.float32))
    dt = dt.reshape(b, s, SSM_GROUPS, HEADS_PER_GROUP)
    A = -jnp.exp(A_log.astype(jnp.float32)).reshape(SSM_GROUPS, HEADS_PER_GROUP)
    y_ssm = ssd_scan(xs, dt, A, Bm, Cm, D_skip.reshape(SSM_GROUPS, HEADS_PER_GROUP))
    y_ssm = y_ssm.reshape(b, s, SSM_GROUPS, SSM_WIDTH // SSM_GROUPS)
    gz = jax.nn.silu(z.astype(jnp.float32)).reshape(b, s, SSM_GROUPS, SSM_WIDTH // SSM_GROUPS)
    y_ssm = rmsnorm(y_ssm * gz, ssm_out_norm.reshape(SSM_GROUPS, SSM_WIDTH // SSM_GROUPS))
    y_ssm = y_ssm.reshape(b, s, SSM_WIDTH).astype(h.dtype)

    return jnp.concatenate([y_attn, y_ssm], axis=-1) @ w_o


def swiglu(h, w_gate_up, w_down):
    gu = h @ w_gate_up
    g, u = gu[..., :D_FF], gu[..., D_FF:]
    return (jax.nn.silu(g) * u) @ w_down


def _fwd_setup_inputs(seed: int = 0) -> dict:
    key = jax.random.key(seed)
    ks = jax.random.split(key, 24)
    f32 = jnp.float32
    nrm = lambda k, shape, sc: jax.random.normal(k, shape, f32) * sc
    dt = jnp.exp(jax.random.uniform(ks[8], (DEPTH, SSM_HEADS), f32)
                 * (jnp.log(0.1) - jnp.log(0.001)) + jnp.log(0.001))
    return {
        "x": nrm(ks[0], (BATCH, SEQ, D_MODEL), 1.0),
        "c": nrm(ks[1], (BATCH, D_MODEL), 1.0),
        "ada_w": nrm(ks[2], (DEPTH, D_MODEL, 6 * D_MODEL), D_MODEL ** -0.5),
        "ada_b": nrm(ks[3], (DEPTH, 6 * D_MODEL), 0.01),
        "norm1": 1.0 + nrm(ks[4], (DEPTH, D_MODEL), 0.01),
        "w_in": nrm(ks[5], (DEPTH, D_MODEL, IN_WIDTH), D_MODEL ** -0.5),
        "conv_w": nrm(ks[6], (DEPTH, CONV_K, XBC_WIDTH), CONV_K ** -0.5),
        "conv_b": nrm(ks[7], (DEPTH, XBC_WIDTH), 0.01),
        "dt_bias": dt + jnp.log(-jnp.expm1(-dt)),
        "A_log": jnp.log(jax.random.uniform(ks[9], (DEPTH, SSM_HEADS), f32, 1.0, 16.0)),
        "D_skip": 1.0 + nrm(ks[10], (DEPTH, SSM_HEADS), 0.1),
        "sinks": nrm(ks[11], (DEPTH, N_Q_HEADS), 0.5),
        "attn_out_norm": 1.0 + nrm(ks[12], (DEPTH, ATTN_WIDTH), 0.01),
        "ssm_out_norm": 1.0 + nrm(ks[13], (DEPTH, SSM_WIDTH), 0.01),
        "w_o": nrm(ks[14], (DEPTH, MIX_WIDTH, D_MODEL), MIX_WIDTH ** -0.5),
        "norm2": 1.0 + nrm(ks[15], (DEPTH, D_MODEL), 0.01),
        "w_gate_up": nrm(ks[16], (DEPTH, D_MODEL, 2 * D_FF), D_MODEL ** -0.5),
        "w_down": nrm(ks[17], (DEPTH, D_FF, D_MODEL), D_FF ** -0.5),
        "rel_bias": nrm(ks[18], (N_BUCKETS, N_Q_HEADS), 0.5),
        "final_norm": 1.0 + nrm(ks[19], (D_MODEL,), 0.01),
    }


def _fwd_reference(x, c, ada_w, ada_b, norm1, w_in, conv_w, conv_b, dt_bias, A_log, D_skip,
              sinks, attn_out_norm, ssm_out_norm, w_o, norm2, w_gate_up, w_down,
              rel_bias, final_norm):
    cond = jax.nn.silu(c)
    for l in range(DEPTH):
        mod = (cond @ ada_w[l] + ada_b[l])[:, None, :]
        shift1, scale1, gate1, shift2, scale2, gate2 = jnp.split(mod, 6, axis=-1)
        h = rmsnorm(x, norm1[l]) * (1.0 + scale1) + shift1
        x = x + gate1 * hybrid_mixer(h, w_in[l], conv_w[l], conv_b[l], dt_bias[l], A_log[l],
                                     D_skip[l], sinks[l], attn_out_norm[l], ssm_out_norm[l],
                                     w_o[l], rel_bias)
        h = rmsnorm(x, norm2[l]) * (1.0 + scale2) + shift2
        x = x + gate2 * swiglu(h, w_gate_up[l], w_down[l])
    return rmsnorm(x, final_norm)


import jax as _jax
import jax.numpy as _jnp

TWIN_FORMAT = 'train_step'
FWD_PARAMS = ['x', 'c', 'ada_w', 'ada_b', 'norm1', 'w_in', 'conv_w', 'conv_b', 'dt_bias', 'A_log', 'D_skip', 'sinks', 'attn_out_norm', 'ssm_out_norm', 'w_o', 'norm2', 'w_gate_up', 'w_down', 'rel_bias', 'final_norm']
TWIN_WEIGHTS = ['ada_w', 'ada_b', 'norm1', 'w_in', 'conv_w', 'conv_b', 'dt_bias', 'A_log', 'D_skip', 'sinks', 'attn_out_norm', 'ssm_out_norm', 'w_o', 'norm2', 'w_gate_up', 'w_down', 'rel_bias', 'final_norm']
TWIN_DIFF_INPUT = 'x'
TWIN_INPUTS = ['x', 'c', 'ada_w', 'ada_b', 'norm1', 'w_in', 'conv_w', 'conv_b', 'dt_bias', 'A_log', 'D_skip', 'sinks', 'attn_out_norm', 'ssm_out_norm', 'w_o', 'norm2', 'w_gate_up', 'w_down', 'rel_bias', 'final_norm', 'loss_target', 'm_ada_w', 'm_ada_b', 'm_norm1', 'm_w_in', 'm_conv_w', 'm_conv_b', 'm_dt_bias', 'm_A_log', 'm_D_skip', 'm_sinks', 'm_attn_out_norm', 'm_ssm_out_norm', 'm_w_o', 'm_norm2', 'm_w_gate_up', 'm_w_down', 'm_rel_bias', 'm_final_norm', 'v_ada_w', 'v_ada_b', 'v_norm1', 'v_w_in', 'v_conv_w', 'v_conv_b', 'v_dt_bias', 'v_A_log', 'v_D_skip', 'v_sinks', 'v_attn_out_norm', 'v_ssm_out_norm', 'v_w_o', 'v_norm2', 'v_w_gate_up', 'v_w_down', 'v_rel_bias', 'v_final_norm']
TWIN_OUTPUTS = ['loss', 'grad_x', 'grad_ada_w', 'grad_ada_b', 'grad_norm1', 'grad_w_in', 'grad_conv_w', 'grad_conv_b', 'grad_dt_bias', 'grad_A_log', 'grad_D_skip', 'grad_sinks', 'grad_attn_out_norm', 'grad_ssm_out_norm', 'grad_w_o', 'grad_norm2', 'grad_w_gate_up', 'grad_w_down', 'grad_rel_bias', 'grad_final_norm', 'delta_ada_w', 'delta_ada_b', 'delta_norm1', 'delta_w_in', 'delta_conv_w', 'delta_conv_b', 'delta_dt_bias', 'delta_A_log', 'delta_D_skip', 'delta_sinks', 'delta_attn_out_norm', 'delta_ssm_out_norm', 'delta_w_o', 'delta_norm2', 'delta_w_gate_up', 'delta_w_down', 'delta_rel_bias', 'delta_final_norm', 'new_m_ada_w', 'new_m_ada_b', 'new_m_norm1', 'new_m_w_in', 'new_m_conv_w', 'new_m_conv_b', 'new_m_dt_bias', 'new_m_A_log', 'new_m_D_skip', 'new_m_sinks', 'new_m_attn_out_norm', 'new_m_ssm_out_norm', 'new_m_w_o', 'new_m_norm2', 'new_m_w_gate_up', 'new_m_w_down', 'new_m_rel_bias', 'new_m_final_norm', 'new_v_ada_w', 'new_v_ada_b', 'new_v_norm1', 'new_v_w_in', 'new_v_conv_w', 'new_v_conv_b', 'new_v_dt_bias', 'new_v_A_log', 'new_v_D_skip', 'new_v_sinks', 'new_v_attn_out_norm', 'new_v_ssm_out_norm', 'new_v_w_o', 'new_v_norm2', 'new_v_w_gate_up', 'new_v_w_down', 'new_v_rel_bias', 'new_v_final_norm']
TWIN_LEAF_KINDS = {'loss': 'loss', 'grad_x': 'grad_x', 'grad_ada_w': 'grad_w', 'grad_ada_b': 'grad_w', 'grad_norm1': 'grad_w', 'grad_w_in': 'grad_w', 'grad_conv_w': 'grad_w', 'grad_conv_b': 'grad_w', 'grad_dt_bias': 'grad_w', 'grad_A_log': 'grad_w', 'grad_D_skip': 'grad_w', 'grad_sinks': 'grad_w', 'grad_attn_out_norm': 'grad_w', 'grad_ssm_out_norm': 'grad_w', 'grad_w_o': 'grad_w', 'grad_norm2': 'grad_w', 'grad_w_gate_up': 'grad_w', 'grad_w_down': 'grad_w', 'grad_rel_bias': 'grad_w', 'grad_final_norm': 'grad_w', 'delta_ada_w': 'delta_w', 'delta_ada_b': 'delta_w', 'delta_norm1': 'delta_w', 'delta_w_in': 'delta_w', 'delta_conv_w': 'delta_w', 'delta_conv_b': 'delta_w', 'delta_dt_bias': 'delta_w', 'delta_A_log': 'delta_w', 'delta_D_skip': 'delta_w', 'delta_sinks': 'delta_w', 'delta_attn_out_norm': 'delta_w', 'delta_ssm_out_norm': 'delta_w', 'delta_w_o': 'delta_w', 'delta_norm2': 'delta_w', 'delta_w_gate_up': 'delta_w', 'delta_w_down': 'delta_w', 'delta_rel_bias': 'delta_w', 'delta_final_norm': 'delta_w', 'new_m_ada_w': 'new_m', 'new_m_ada_b': 'new_m', 'new_m_norm1': 'new_m', 'new_m_w_in': 'new_m', 'new_m_conv_w': 'new_m', 'new_m_conv_b': 'new_m', 'new_m_dt_bias': 'new_m', 'new_m_A_log': 'new_m', 'new_m_D_skip': 'new_m', 'new_m_sinks': 'new_m', 'new_m_attn_out_norm': 'new_m', 'new_m_ssm_out_norm': 'new_m', 'new_m_w_o': 'new_m', 'new_m_norm2': 'new_m', 'new_m_w_gate_up': 'new_m', 'new_m_w_down': 'new_m', 'new_m_rel_bias': 'new_m', 'new_m_final_norm': 'new_m', 'new_v_ada_w': 'new_v', 'new_v_ada_b': 'new_v', 'new_v_norm1': 'new_v', 'new_v_w_in': 'new_v', 'new_v_conv_w': 'new_v', 'new_v_conv_b': 'new_v', 'new_v_dt_bias': 'new_v', 'new_v_A_log': 'new_v', 'new_v_D_skip': 'new_v', 'new_v_sinks': 'new_v', 'new_v_attn_out_norm': 'new_v', 'new_v_ssm_out_norm': 'new_v', 'new_v_w_o': 'new_v', 'new_v_norm2': 'new_v', 'new_v_w_gate_up': 'new_v', 'new_v_w_down': 'new_v', 'new_v_rel_bias': 'new_v', 'new_v_final_norm': 'new_v'}


def _forward(args):
    return _fwd_reference(*[args[k] for k in FWD_PARAMS])


def _output_shape():
    def fwd():
        inp = _fwd_setup_inputs(0)
        return _fwd_reference(*[inp[k] for k in FWD_PARAMS])
    out = _jax.eval_shape(fwd)
    return out.shape, out.dtype

N_MICROBATCH = 1
ADAM_LR = 0.001
ADAM_B1 = 0.9
ADAM_B2 = 0.999
ADAM_EPS = 1e-08
ADAM_WD = 0.01
ADAM_STEP = 10
PER_EXAMPLE_BATCH_AXIS = {'x': 0, 'c': 0, 'loss_target': 0}
SHARED_INPUTS = []
_WEIGHT_DTYPES = {'ada_w': _jnp.float32, 'ada_b': _jnp.float32, 'norm1': _jnp.float32, 'w_in': _jnp.float32, 'conv_w': _jnp.float32, 'conv_b': _jnp.float32, 'dt_bias': _jnp.float32, 'A_log': _jnp.float32, 'D_skip': _jnp.float32, 'sinks': _jnp.float32, 'attn_out_norm': _jnp.float32, 'ssm_out_norm': _jnp.float32, 'w_o': _jnp.float32, 'norm2': _jnp.float32, 'w_gate_up': _jnp.float32, 'w_down': _jnp.float32, 'rel_bias': _jnp.float32, 'final_norm': _jnp.float32}
MOMENT_SCALE = {'ada_w': 1.154847e-01, 'ada_b': 1.985076e-01, 'norm1': 1.645043e-01, 'w_in': 1.319986e-01, 'conv_w': 1.027328e-01, 'conv_b': 9.435971e-02, 'dt_bias': 2.228592e-01, 'A_log': 3.862165e-01, 'D_skip': 6.525035e-01, 'sinks': 3.959654e-02, 'attn_out_norm': 1.443879e-01, 'ssm_out_norm': 1.346260e-01, 'w_o': 1.441731e-01, 'norm2': 1.618102e-01, 'w_gate_up': 7.625506e-02, 'w_down': 1.245055e-01, 'rel_bias': 8.108249e-02, 'final_norm': 6.521534e+01}


def _to_microbatches(a, axis):
    t = _jnp.moveaxis(a, axis, 0)
    t = t.reshape((N_MICROBATCH, t.shape[0] // N_MICROBATCH) + t.shape[1:])
    return _jnp.moveaxis(t, 1, axis + 1)


def setup_inputs(seed: int = 0) -> dict:
    inp = _fwd_setup_inputs(seed)
    key = _jax.random.fold_in(_jax.random.key(seed), 7919)
    shape, _ = _output_shape()
    out = dict(inp)
    out["loss_target"] = _jax.random.normal(_jax.random.fold_in(key, 0), shape, _jnp.float32)
    for i, name in enumerate(TWIN_WEIGHTS):
        w = inp[name].astype(_jnp.float32)
        if MOMENT_SCALE is None:
            s = _jnp.sqrt(_jnp.mean(_jnp.square(w)) + 1e-30)
        else:
            s = MOMENT_SCALE[name]
        km, kv = _jax.random.split(_jax.random.fold_in(key, i + 1))
        out[name] = w
        out["m_" + name] = s * _jax.random.normal(km, w.shape, _jnp.float32)
        out["v_" + name] = (s * s) * _jax.random.uniform(kv, w.shape, _jnp.float32, 0.5, 1.5)
    if N_MICROBATCH > 1:
        for name, axis in PER_EXAMPLE_BATCH_AXIS.items():
            out[name] = _to_microbatches(out[name], axis)
    return {'x': out['x'], 'c': out['c'], 'ada_w': out['ada_w'], 'ada_b': out['ada_b'], 'norm1': out['norm1'], 'w_in': out['w_in'], 'conv_w': out['conv_w'], 'conv_b': out['conv_b'], 'dt_bias': out['dt_bias'], 'A_log': out['A_log'], 'D_skip': out['D_skip'], 'sinks': out['sinks'], 'attn_out_norm': out['attn_out_norm'], 'ssm_out_norm': out['ssm_out_norm'], 'w_o': out['w_o'], 'norm2': out['norm2'], 'w_gate_up': out['w_gate_up'], 'w_down': out['w_down'], 'rel_bias': out['rel_bias'], 'final_norm': out['final_norm'], 'loss_target': out['loss_target'], 'm_ada_w': out['m_ada_w'], 'm_ada_b': out['m_ada_b'], 'm_norm1': out['m_norm1'], 'm_w_in': out['m_w_in'], 'm_conv_w': out['m_conv_w'], 'm_conv_b': out['m_conv_b'], 'm_dt_bias': out['m_dt_bias'], 'm_A_log': out['m_A_log'], 'm_D_skip': out['m_D_skip'], 'm_sinks': out['m_sinks'], 'm_attn_out_norm': out['m_attn_out_norm'], 'm_ssm_out_norm': out['m_ssm_out_norm'], 'm_w_o': out['m_w_o'], 'm_norm2': out['m_norm2'], 'm_w_gate_up': out['m_w_gate_up'], 'm_w_down': out['m_w_down'], 'm_rel_bias': out['m_rel_bias'], 'm_final_norm': out['m_final_norm'], 'v_ada_w': out['v_ada_w'], 'v_ada_b': out['v_ada_b'], 'v_norm1': out['v_norm1'], 'v_w_in': out['v_w_in'], 'v_conv_w': out['v_conv_w'], 'v_conv_b': out['v_conv_b'], 'v_dt_bias': out['v_dt_bias'], 'v_A_log': out['v_A_log'], 'v_D_skip': out['v_D_skip'], 'v_sinks': out['v_sinks'], 'v_attn_out_norm': out['v_attn_out_norm'], 'v_ssm_out_norm': out['v_ssm_out_norm'], 'v_w_o': out['v_w_o'], 'v_norm2': out['v_norm2'], 'v_w_gate_up': out['v_w_gate_up'], 'v_w_down': out['v_w_down'], 'v_rel_bias': out['v_rel_bias'], 'v_final_norm': out['v_final_norm']}


def _loss(weights, diff, rest, loss_target):
    with _jax.named_scope("forward"):
        args = {**rest, TWIN_DIFF_INPUT: diff, **{k: w.astype(_WEIGHT_DTYPES[k]) for k, w in weights.items()}}
        y = _forward(args)
    with _jax.named_scope("loss_head"):
        err = _jnp.square(y.astype(_jnp.float32) - loss_target)
        return 0.5 * _jnp.sum(_jnp.mean(err, axis=-1)) if err.ndim else 0.5 * err


def _adamw(w, g, m, v):
    m = ADAM_B1 * m + (1.0 - ADAM_B1) * g
    v = ADAM_B2 * v + (1.0 - ADAM_B2) * _jnp.square(g)
    m_hat = m / (1.0 - ADAM_B1 ** ADAM_STEP)
    v_hat = v / (1.0 - ADAM_B2 ** ADAM_STEP)
    delta = -ADAM_LR * (m_hat / (_jnp.sqrt(v_hat) + ADAM_EPS) + ADAM_WD * w)
    return delta, m, v


def reference(x, c, ada_w, ada_b, norm1, w_in, conv_w, conv_b, dt_bias, A_log, D_skip, sinks, attn_out_norm, ssm_out_norm, w_o, norm2, w_gate_up, w_down, rel_bias, final_norm, loss_target, m_ada_w, m_ada_b, m_norm1, m_w_in, m_conv_w, m_conv_b, m_dt_bias, m_A_log, m_D_skip, m_sinks, m_attn_out_norm, m_ssm_out_norm, m_w_o, m_norm2, m_w_gate_up, m_w_down, m_rel_bias, m_final_norm, v_ada_w, v_ada_b, v_norm1, v_w_in, v_conv_w, v_conv_b, v_dt_bias, v_A_log, v_D_skip, v_sinks, v_attn_out_norm, v_ssm_out_norm, v_w_o, v_norm2, v_w_gate_up, v_w_down, v_rel_bias, v_final_norm):
    given = dict(x=x, c=c, ada_w=ada_w, ada_b=ada_b, norm1=norm1, w_in=w_in, conv_w=conv_w, conv_b=conv_b, dt_bias=dt_bias, A_log=A_log, D_skip=D_skip, sinks=sinks, attn_out_norm=attn_out_norm, ssm_out_norm=ssm_out_norm, w_o=w_o, norm2=norm2, w_gate_up=w_gate_up, w_down=w_down, rel_bias=rel_bias, final_norm=final_norm, loss_target=loss_target, m_ada_w=m_ada_w, m_ada_b=m_ada_b, m_norm1=m_norm1, m_w_in=m_w_in, m_conv_w=m_conv_w, m_conv_b=m_conv_b, m_dt_bias=m_dt_bias, m_A_log=m_A_log, m_D_skip=m_D_skip, m_sinks=m_sinks, m_attn_out_norm=m_attn_out_norm, m_ssm_out_norm=m_ssm_out_norm, m_w_o=m_w_o, m_norm2=m_norm2, m_w_gate_up=m_w_gate_up, m_w_down=m_w_down, m_rel_bias=m_rel_bias, m_final_norm=m_final_norm, v_ada_w=v_ada_w, v_ada_b=v_ada_b, v_norm1=v_norm1, v_w_in=v_w_in, v_conv_w=v_conv_w, v_conv_b=v_conv_b, v_dt_bias=v_dt_bias, v_A_log=v_A_log, v_D_skip=v_D_skip, v_sinks=v_sinks, v_attn_out_norm=v_attn_out_norm, v_ssm_out_norm=v_ssm_out_norm, v_w_o=v_w_o, v_norm2=v_norm2, v_w_gate_up=v_w_gate_up, v_w_down=v_w_down, v_rel_bias=v_rel_bias, v_final_norm=v_final_norm)
    weights = {n: given[n] for n in TWIN_WEIGHTS}
    shared = {n: given[n] for n in SHARED_INPUTS}
    per_example = {n: given[n] for n in ['x', 'c']}
    grad_fn = _jax.value_and_grad(_loss, argnums=(0, 1))

    def one_microbatch(ex, loss_target):
        ex = dict(ex)
        diff = ex.pop(TWIN_DIFF_INPUT)
        return grad_fn(weights, diff, {**shared, **ex}, loss_target)

    if N_MICROBATCH == 1:
        loss, (grad_w, grad_x) = one_microbatch(per_example, given["loss_target"])
    else:
        def body(carry, xs):
            loss_sum, grad_sum = carry
            l_k, (gw_k, gx_k) = one_microbatch(xs[0], xs[1])
            with _jax.named_scope("update"):
                return (loss_sum + l_k, _jax.tree.map(_jnp.add, grad_sum, gw_k)), gx_k

        init = (_jnp.zeros((), _jnp.float32), _jax.tree.map(_jnp.zeros_like, weights))
        (loss, grad_w), grad_x = _jax.lax.scan(body, init, (per_example, given["loss_target"]))
    with _jax.named_scope("update"):
        delta_w, new_m, new_v = {}, {}, {}
        for n in TWIN_WEIGHTS:
            delta_w[n], new_m[n], new_v[n] = _adamw(weights[n], grad_w[n], given["m_" + n], given["v_" + n])
    return (loss, grad_x, *[grad_w[n] for n in TWIN_WEIGHTS], *[delta_w[n] for n in TWIN_WEIGHTS],
            *[new_m[n] for n in TWIN_WEIGHTS], *[new_v[n] for n in TWIN_WEIGHTS])
```

```python
import functools

import numpy as np
import jax
import jax.numpy as jnp
from jax import lax
from jax.experimental import pallas as pl
from jax.experimental.pallas import tpu as pltpu

F32 = jnp.float32
MXU_DTYPE = jnp.bfloat16
WIRE_DTYPE = jnp.bfloat16
HI = lax.Precision.HIGHEST
MESH = pl.DeviceIdType.MESH
N_DEV = 8

D_MODEL = 1024
ATTN_W = 512
KV_W = 128
SSM_W = 512
XBC_W = 1024
N_HEADS = 8
D_STATE = 128
D_FF = 2816
IN_W = 2312
IN_PAD = 2432
BLK = 128
N_BUCKETS = 32
EPS = 1e-6
LANE = 128
HALF = 64

ADAM_LR, ADAM_B1, ADAM_B2, ADAM_EPS, ADAM_WD, ADAM_STEP = 0.001, 0.9, 0.999, 1e-08, 0.01, 10

VMEM_BIG = 56 * 1024 * 1024


def _cparams(vmem=None):
    if vmem is None:
        return pltpu.CompilerParams()
    return pltpu.CompilerParams(vmem_limit_bytes=vmem)


def _mm(a, b):
    return jnp.dot(a.astype(MXU_DTYPE), b.astype(MXU_DTYPE), preferred_element_type=F32)


def _mm_nt(a, b):
    return lax.dot_general(a.astype(MXU_DTYPE), b.astype(MXU_DTYPE), (((1,), (1,)), ((), ())),
                           preferred_element_type=F32)


def _mm_tn(a, b):
    return lax.dot_general(a.astype(MXU_DTYPE), b.astype(MXU_DTYPE), (((0,), (0,)), ((), ())),
                           preferred_element_type=F32)


def _mm_hi(a, b):
    return jnp.dot(a, b, precision=HI, preferred_element_type=F32)


def _silu(x):
    return x * jax.nn.sigmoid(x)


def _softplus(x):
    return jnp.maximum(x, 0.0) + jnp.log1p(jnp.exp(-jnp.abs(x)))


def _rms(x, g, n):
    return x * lax.rsqrt(jnp.sum(x * x, axis=-1, keepdims=True) * (1.0 / n) + EPS) * g


def _modnorm(x, g, scale, shift):
    return _rms(x, g, x.shape[-1]) * (1.0 + scale) + shift


def _lane_iota(shape):
    return lax.broadcasted_iota(jnp.int32, shape, len(shape) - 1)


def _split_pair(t):
    lane = _lane_iota(t.shape)
    lo = jnp.where(lane < HALF, t, 0.0)
    hi = pltpu.roll(jnp.where(lane >= HALF, t, 0.0), HALF, 1)
    return lo, hi


def _join_pair(lo, hi):
    lane = _lane_iota(lo.shape)
    return jnp.where(lane < HALF, lo, pltpu.roll(hi, HALF, 1))


def _split_heads(t, n_pairs):
    out = []
    for p in range(n_pairs):
        out.extend(_split_pair(t[:, p * LANE:(p + 1) * LANE]))
    return out


def _join_heads(hs):
    return jnp.concatenate([_join_pair(hs[2 * p], hs[2 * p + 1]) for p in range(len(hs) // 2)], axis=1)


def _t5_bucket_table():
    dist = np.arange(BLK)[:, None] + BLK - np.arange(2 * BLK)[None, :]
    n = np.maximum(dist, 0)
    max_exact = N_BUCKETS // 2
    large = max_exact + (np.log(np.maximum(n, 1) / max_exact) / np.log(128 / max_exact)
                         * (N_BUCKETS - max_exact)).astype(np.int32)
    large = np.minimum(large, N_BUCKETS - 1)
    return np.where(n < max_exact, n, large).astype(np.int32)


def _my_pos():
    return lax.axis_index("x"), lax.axis_index("y"), lax.axis_index("c")


def _peer(k):
    x, y, c = _my_pos()
    return (1 - x if k & 4 else x, 1 - y if k & 2 else y, 1 - c if k & 1 else c)


def _lin(pos):
    return 4 * pos[0] + 2 * pos[1] + pos[2]


def _exchange(arrs, scatter, name):
    n = len(arrs)
    if scatter:
        out_shape = [jax.ShapeDtypeStruct(a.shape, a.dtype) for a in arrs]
    else:
        out_shape = [jax.ShapeDtypeStruct((N_DEV,) + a.shape, a.dtype) for a in arrs]

    def body(*refs):
        ins, outs = refs[:n], refs[n:2 * n]
        local_sem, send_sem, recv_sem = refs[2 * n:]
        me = _lin(_my_pos())
        local, remote = [], []
        for a in range(n):
            src = ins[a].at[me] if scatter else ins[a]
            cp = pltpu.make_async_copy(src, outs[a].at[me], local_sem.at[a])
            cp.start()
            local.append(cp)
        for k in range(1, N_DEV):
            peer = _peer(k)
            for a in range(n):
                src = ins[a].at[_lin(peer)] if scatter else ins[a]
                cp = pltpu.make_async_remote_copy(src, outs[a].at[me], send_sem.at[a, k - 1], recv_sem.at[a, k - 1],
                                                  device_id=peer, device_id_type=MESH)
                cp.start()
                remote.append(cp)
        for cp in local:
            cp.wait()
        for cp in remote:
            cp.wait_send()
            cp.wait_recv()

    hbm = pl.BlockSpec(memory_space=pltpu.HBM)
    return pl.pallas_call(
        body, name=name, out_shape=out_shape, in_specs=[hbm] * n, out_specs=[hbm] * n,
        scratch_shapes=[pltpu.SemaphoreType.DMA((n,)), pltpu.SemaphoreType.DMA((n, N_DEV - 1)),
                        pltpu.SemaphoreType.DMA((n, N_DEV - 1))],
    )(*arrs)


def _mod_exchange(c, ada_w, ada_b8):
    chunk = ada_w.shape[1]

    def body(c_ref, w_ref, b_ref, out_ref, cbuf, part, s1, r1, s2, r2):
        me = _lin(_my_pos())
        first = []
        for k in range(1, N_DEV):
            cp = pltpu.make_async_remote_copy(c_ref, cbuf.at[me], s1.at[k - 1], r1.at[k - 1],
                                              device_id=_peer(k), device_id_type=MESH)
            cp.start()
            first.append(cp)
        cbuf[me] = c_ref[...]
        for cp in first:
            cp.wait_send()
            cp.wait_recv()
        cond = _silu(jnp.concatenate([cbuf[i] for i in range(N_DEV)], axis=0))
        mod = _mm_hi(cond, w_ref[...]) + b_ref[pl.ds(me, 1), :]
        for j in range(N_DEV):
            part[j] = mod[j:j + 1, :]
        second = []
        for k in range(1, N_DEV):
            peer = _peer(k)
            cp = pltpu.make_async_remote_copy(part.at[_lin(peer)], out_ref.at[me], s2.at[k - 1], r2.at[k - 1],
                                              device_id=peer, device_id_type=MESH)
            cp.start()
            second.append(cp)
        out_ref[me] = part[me]
        for cp in second:
            cp.wait_send()
            cp.wait_recv()

    vm = pl.BlockSpec(memory_space=pltpu.VMEM)
    return pl.pallas_call(
        body, name="mod_exchange", out_shape=jax.ShapeDtypeStruct((N_DEV, 1, chunk), F32),
        in_specs=[vm, vm, vm], out_specs=vm,
        scratch_shapes=[pltpu.VMEM((N_DEV, 1, D_MODEL), F32), pltpu.VMEM((N_DEV, 1, chunk), F32)]
        + [pltpu.SemaphoreType.DMA((N_DEV - 1,))] * 4,
    )(c, ada_w, ada_b8)


def _row(i):
    return (i, 0)


def _fixed(i):
    return (0, 0)


def _in_proj_fwd(x, norm1, scale1, shift1, w_in, tm):
    S = x.shape[0]

    def body(x_ref, n_ref, sc_ref, sh_ref, w_ref, qkv_ref, z_ref, xbc_ref, dt_ref):
        h = _modnorm(x_ref[...], n_ref[...], sc_ref[...], sh_ref[...])
        p = _mm(h, w_ref[...])
        qkv_ref[...] = p[:, :768].astype(qkv_ref.dtype)
        z_ref[...] = p[:, 768:1280]
        xbc_ref[...] = p[:, 1280:2304]
        dt_ref[...] = p[:, 2304:IN_PAD]

    vec = pl.BlockSpec((1, D_MODEL), _fixed)
    return pl.pallas_call(
        body, name="in_proj_fwd", grid=(S // tm,),
        in_specs=[pl.BlockSpec((tm, D_MODEL), _row), vec, vec, vec, pl.BlockSpec((D_MODEL, IN_PAD), _fixed)],
        out_specs=[pl.BlockSpec((tm, 768), _row), pl.BlockSpec((tm, SSM_W), _row),
                   pl.BlockSpec((tm, XBC_W), _row), pl.BlockSpec((tm, LANE), _row)],
        out_shape=[jax.ShapeDtypeStruct((S, 768), MXU_DTYPE), jax.ShapeDtypeStruct((S, SSM_W), F32),
                   jax.ShapeDtypeStruct((S, XBC_W), F32), jax.ShapeDtypeStruct((S, LANE), F32)],
        compiler_params=_cparams(VMEM_BIG),
    )(x, norm1, scale1, shift1, w_in)


def _in_proj_bwd(x, dx1, dq, dkv, dz, dxbc, ddt, norm1, scale1, shift1, w_in, tm):
    S = x.shape[0]

    def body(x_ref, dx1_ref, dq_ref, dkv_ref, dz_ref, dxbc_ref, ddt_ref, n_ref, sc_ref, sh_ref, w_ref,
             gx_ref, h_ref, dp_ref, acc_ref):
        @pl.when(pl.program_id(0) == 0)
        def _():
            acc_ref[...] = jnp.zeros_like(acc_ref)

        h, vjp = jax.vjp(_modnorm, x_ref[...], n_ref[...], sc_ref[...], sh_ref[...])
        dp = jnp.concatenate([dq_ref[...].astype(MXU_DTYPE), dkv_ref[...].astype(MXU_DTYPE),
                              dz_ref[...].astype(MXU_DTYPE), dxbc_ref[...].astype(MXU_DTYPE),
                              ddt_ref[...].astype(MXU_DTYPE)], axis=1)
        dh = _mm_nt(dp, w_ref[...])
        dx, dn, dsc, dsh = vjp(dh)
        gx_ref[...] = dx1_ref[...] + dx
        h_ref[...] = h.astype(h_ref.dtype)
        dp_ref[...] = dp
        acc_ref[0:1, :] += dn
        acc_ref[1:2, :] += dsc
        acc_ref[2:3, :] += dsh

    vec = pl.BlockSpec((1, D_MODEL), _fixed)
    return pl.pallas_call(
        body, name="in_proj_bwd", grid=(S // tm,),
        in_specs=[pl.BlockSpec((tm, D_MODEL), _row), pl.BlockSpec((tm, D_MODEL), _row),
                  pl.BlockSpec((tm, ATTN_W), _row), pl.BlockSpec((tm, 2 * KV_W), _row),
                  pl.BlockSpec((tm, SSM_W), _row), pl.BlockSpec((tm, XBC_W), _row), pl.BlockSpec((tm, LANE), _row),
                  vec, vec, vec, pl.BlockSpec((D_MODEL, IN_PAD), _fixed)],
        out_specs=[pl.BlockSpec((tm, D_MODEL), _row), pl.BlockSpec((tm, D_MODEL), _row),
                   pl.BlockSpec((tm, IN_PAD), _row), pl.BlockSpec((8, D_MODEL), _fixed)],
        out_shape=[jax.ShapeDtypeStruct((S, D_MODEL), F32), jax.ShapeDtypeStruct((S, D_MODEL), MXU_DTYPE),
                   jax.ShapeDtypeStruct((S, IN_PAD), MXU_DTYPE), jax.ShapeDtypeStruct((8, D_MODEL), F32)],
        compiler_params=_cparams(VMEM_BIG),
    )(x, dx1, dq, dkv, dz, dxbc, ddt, norm1, scale1, shift1, w_in)


def _out_stage(ya, ys0, ys1, z0, z1, an, sn0, sn1):
    half = SSM_W // 2
    a = _rms(ya, an, ATTN_W)
    g0 = _rms(ys0 * _silu(z0), sn0, half)
    g1 = _rms(ys1 * _silu(z1), sn1, half)
    return jnp.concatenate([a, g0, g1], axis=1)


def _out_stage_args(ya_ref, ys_ref, z_ref, an_ref, sn_ref):
    half = SSM_W // 2
    return (ya_ref[...], ys_ref[:, :half], ys_ref[:, half:], z_ref[:, :half], z_ref[:, half:],
            an_ref[...], sn_ref[:, :half], sn_ref[:, half:])


def _out_proj_fwd(x, ya, ys, z, an, sn, gate1, w_o, tm):
    S = x.shape[0]

    def body(x_ref, ya_ref, ys_ref, z_ref, an_ref, sn_ref, g_ref, w_ref, x1_ref):
        u = _out_stage(*_out_stage_args(ya_ref, ys_ref, z_ref, an_ref, sn_ref))
        x1_ref[...] = x_ref[...] + g_ref[...] * _mm(u, w_ref[...])

    half = pl.BlockSpec((tm, ATTN_W), _row)
    hvec = pl.BlockSpec((1, ATTN_W), _fixed)
    return pl.pallas_call(
        body, name="out_proj_fwd", grid=(S // tm,),
        in_specs=[pl.BlockSpec((tm, D_MODEL), _row), half, half, half, hvec, hvec,
                  pl.BlockSpec((1, D_MODEL), _fixed), pl.BlockSpec((D_MODEL, D_MODEL), _fixed)],
        out_specs=pl.BlockSpec((tm, D_MODEL), _row),
        out_shape=jax.ShapeDtypeStruct((S, D_MODEL), F32),
        compiler_params=_cparams(VMEM_BIG),
    )(x, ya, ys, z, an, sn, gate1, w_o)


def _out_proj_bwd(dx1, ya, ys, z, an, sn, gate1, w_o, tm):
    S = dx1.shape[0]

    def body(dx1_ref, ya_ref, ys_ref, z_ref, an_ref, sn_ref, g_ref, w_ref,
             dya_ref, dys_ref, dz_ref, u_ref, dmix_ref, acc_ref):
        @pl.when(pl.program_id(0) == 0)
        def _():
            acc_ref[...] = jnp.zeros_like(acc_ref)

        u, vjp = jax.vjp(_out_stage, *_out_stage_args(ya_ref, ys_ref, z_ref, an_ref, sn_ref))
        dx1 = dx1_ref[...]
        mix = _mm(u, w_ref[...])
        dmix = dx1 * g_ref[...]
        du = _mm_nt(dmix, w_ref[...])
        dya, dys0, dys1, dz0, dz1, dan, dsn0, dsn1 = vjp(du)
        dya_ref[...] = dya
        dys_ref[...] = jnp.concatenate([dys0, dys1], axis=1)
        dz_ref[...] = jnp.concatenate([dz0, dz1], axis=1)
        u_ref[...] = u.astype(u_ref.dtype)
        dmix_ref[...] = dmix.astype(dmix_ref.dtype)
        acc_ref[0:1, :] += jnp.sum(dx1 * mix, axis=0, keepdims=True)
        acc_ref[1:2, :] += jnp.concatenate([dan, dsn0, dsn1], axis=1)

    half = pl.BlockSpec((tm, ATTN_W), _row)
    hvec = pl.BlockSpec((1, ATTN_W), _fixed)
    full = pl.BlockSpec((tm, D_MODEL), _row)
    return pl.pallas_call(
        body, name="out_proj_bwd", grid=(S // tm,),
        in_specs=[full, half, half, half, hvec, hvec,
                  pl.BlockSpec((1, D_MODEL), _fixed), pl.BlockSpec((D_MODEL, D_MODEL), _fixed)],
        out_specs=[half, half, half, full, full, pl.BlockSpec((8, D_MODEL), _fixed)],
        out_shape=[jax.ShapeDtypeStruct((S, ATTN_W), F32)] * 3
        + [jax.ShapeDtypeStruct((S, D_MODEL), MXU_DTYPE)] * 2 + [jax.ShapeDtypeStruct((8, D_MODEL), F32)],
        compiler_params=_cparams(VMEM_BIG),
    )(dx1, ya, ys, z, an, sn, gate1, w_o)


def _loss_rows(x2, fn, tgt):
    y = _rms(x2, fn, D_MODEL)
    per_row = jnp.sum(jnp.square(y - tgt), axis=1, keepdims=True)
    return jnp.sum(per_row, axis=0, keepdims=True) * (0.5 / D_MODEL)


def _mlp_loss(x1, tgt, norm2, scale2, shift2, gate2, fnorm, w_gu, w_d, tm):
    S = x1.shape[0]

    def body(x1_ref, t_ref, n_ref, sc_ref, sh_ref, g_ref, fn_ref, wgu_hbm, wd_hbm,
             dx1_ref, h_ref, dgu_ref, act_ref, dmlp_ref, acc_ref, wgu, wd):
        @pl.when(pl.program_id(0) == 0)
        def _():
            acc_ref[...] = jnp.zeros_like(acc_ref)
            pltpu.sync_copy(wgu_hbm, wgu)
            pltpu.sync_copy(wd_hbm, wd)

        x1 = x1_ref[...]
        gate2 = g_ref[...]
        h, vjp_h = jax.vjp(_modnorm, x1, n_ref[...], sc_ref[...], sh_ref[...])
        hb = h.astype(MXU_DTYPE)
        gu = _mm(hb, wgu[...])
        g, u = gu[:, :D_FF], gu[:, D_FF:]
        sg = jax.nn.sigmoid(g)
        silu_g = g * sg
        act = (silu_g * u).astype(MXU_DTYPE)
        mlp = _mm(act, wd[...])
        x2 = x1 + gate2 * mlp
        loss, vjp_loss = jax.vjp(_loss_rows, x2, fn_ref[...], t_ref[...])
        dx2, dfn, _ = vjp_loss(jnp.ones((1, 1), F32))
        dmlp = (dx2 * gate2).astype(MXU_DTYPE)
        dact = _mm_nt(dmlp, wd[...])
        dg = dact * u * (sg * (1.0 + g * (1.0 - sg)))
        du = dact * silu_g
        dgu = jnp.concatenate([dg, du], axis=1).astype(MXU_DTYPE)
        dh = _mm_nt(dgu, wgu[...])
        dx, dn, dsc, dsh = vjp_h(dh)
        dx1_ref[...] = dx2 + dx
        h_ref[...] = hb
        dgu_ref[...] = dgu
        act_ref[...] = act
        dmlp_ref[...] = dmlp
        acc_ref[0:1, :] += dn
        acc_ref[1:2, :] += dsc
        acc_ref[2:3, :] += dsh
        acc_ref[3:4, :] += jnp.sum(dx2 * mlp, axis=0, keepdims=True)
        acc_ref[4:5, :] += dfn
        acc_ref[5:6, :] += jnp.broadcast_to(loss, (1, D_MODEL))

    full = pl.BlockSpec((tm, D_MODEL), _row)
    vec = pl.BlockSpec((1, D_MODEL), _fixed)
    anyspec = pl.BlockSpec(memory_space=pl.ANY)
    return pl.pallas_call(
        body, name="mlp_loss", grid=(S // tm,),
        in_specs=[full, full, vec, vec, vec, vec, vec, anyspec, anyspec],
        out_specs=[full, full, pl.BlockSpec((tm, 2 * D_FF), _row), pl.BlockSpec((tm, D_FF), _row), full,
                   pl.BlockSpec((8, D_MODEL), _fixed)],
        out_shape=[jax.ShapeDtypeStruct((S, D_MODEL), F32), jax.ShapeDtypeStruct((S, D_MODEL), MXU_DTYPE),
                   jax.ShapeDtypeStruct((S, 2 * D_FF), MXU_DTYPE), jax.ShapeDtypeStruct((S, D_FF), MXU_DTYPE),
                   jax.ShapeDtypeStruct((S, D_MODEL), MXU_DTYPE), jax.ShapeDtypeStruct((8, D_MODEL), F32)],
        scratch_shapes=[pltpu.VMEM((D_MODEL, 2 * D_FF), MXU_DTYPE), pltpu.VMEM((D_FF, D_MODEL), MXU_DTYPE)],
        compiler_params=_cparams(VMEM_BIG),
    )(x1, tgt, norm2, scale2, shift2, gate2, fnorm, w_gu, w_d)


def _wgrad(a, g, tn, ts, name):
    S, K = a.shape
    N = g.shape[1]
    ns = S // ts

    def body(a_ref, g_ref, o_ref):
        @pl.when(pl.program_id(1) == 0)
        def _():
            o_ref[...] = jnp.zeros_like(o_ref)

        o_ref[...] += _mm_tn(a_ref[...], g_ref[...])

    return pl.pallas_call(
        body, name=name, grid=(N // tn, ns),
        in_specs=[pl.BlockSpec((ts, K), lambda j, s: (s, 0)), pl.BlockSpec((ts, tn), lambda j, s: (s, j))],
        out_specs=pl.BlockSpec((K, tn), lambda j, s: (0, j)),
        out_shape=jax.ShapeDtypeStruct((K, N), F32),
        compiler_params=_cparams(VMEM_BIG),
    )(a, g)


def _attn_heads(q_heads, k_pads, v_pads, bias_heads, sink_cols, mask):
    outs = []
    for h in range(N_HEADS):
        kv = h // 4
        s = _mm_nt(q_heads[h], k_pads[kv]) * (HALF ** -0.5) + bias_heads[h]
        s = jnp.where(mask, s, -1e30)
        m = lax.stop_gradient(jnp.maximum(jnp.max(s, axis=-1, keepdims=True), sink_cols[h]))
        p = jnp.exp(s - m)
        den = jnp.sum(p, axis=-1, keepdims=True) + jnp.exp(sink_cols[h] - m)
        outs.append(_mm(p, v_pads[kv]) / den)
    return tuple(outs)


def _attn_mask(first):
    i = lax.broadcasted_iota(jnp.int32, (BLK, 2 * BLK), 0)
    j = lax.broadcasted_iota(jnp.int32, (BLK, 2 * BLK), 1)
    return (j > i) & (j <= i + BLK) & (j >= jnp.where(first, BLK, 0))


def _attn_operands(q_ref, kvp_ref, kvc_ref, bias_ref, sinks_ref):
    q_heads = _split_heads(q_ref[...].astype(F32), 4)
    kvp = kvp_ref[...].astype(F32)
    kvc = kvc_ref[...].astype(F32)
    kp, kc = _split_pair(kvp[:, :LANE]), _split_pair(kvc[:, :LANE])
    vp, vc = _split_pair(kvp[:, LANE:]), _split_pair(kvc[:, LANE:])
    k_pads = [jnp.concatenate([kp[g], kc[g]], axis=0) for g in range(2)]
    v_pads = [jnp.concatenate([vp[g], vc[g]], axis=0) for g in range(2)]
    bias_heads = [bias_ref[h] for h in range(N_HEADS)]
    sink_cols = [jnp.full((BLK, 1), sinks_ref[h], F32) for h in range(N_HEADS)]
    return q_heads, k_pads, v_pads, bias_heads, sink_cols


def _build_bias(bucket_ref, relb_ref, bias_ref):
    bk = bucket_ref[...]
    for h in range(N_HEADS):
        acc = jnp.zeros((BLK, 2 * BLK), F32)
        for b in range(N_BUCKETS):
            acc = jnp.where(bk == b, relb_ref[b, h], acc)
        bias_ref[h] = acc


def _attn_fwd(qkv, buckets, rel_bias, sinks):
    S = qkv.shape[0]
    nb = S // BLK

    def body(q_ref, kvp_ref, kvc_ref, bk_ref, relb_ref, sinks_ref, y_ref, bias_ref):
        i = pl.program_id(0)

        @pl.when(i == 0)
        def _():
            _build_bias(bk_ref, relb_ref, bias_ref)

        ops = _attn_operands(q_ref, kvp_ref, kvc_ref, bias_ref, sinks_ref)
        outs = _attn_heads(*ops, mask=_attn_mask(i == 0))
        y_ref[...] = _join_heads(outs)

    smem = pl.BlockSpec(memory_space=pltpu.SMEM)
    return pl.pallas_call(
        body, name="attn_fwd", grid=(nb,),
        in_specs=[pl.BlockSpec((BLK, ATTN_W), _row),
                  pl.BlockSpec((BLK, 2 * KV_W), lambda i: (jnp.maximum(i - 1, 0), 2)),
                  pl.BlockSpec((BLK, 2 * KV_W), lambda i: (i, 2)),
                  pl.BlockSpec((BLK, 2 * BLK), _fixed), smem, smem],
        out_specs=pl.BlockSpec((BLK, ATTN_W), _row),
        out_shape=jax.ShapeDtypeStruct((S, ATTN_W), F32),
        scratch_shapes=[pltpu.VMEM((N_HEADS, BLK, 2 * BLK), F32)],
    )(qkv, qkv, qkv, buckets, rel_bias, sinks)


def _attn_bwd(qkv, dy, buckets, rel_bias, sinks):
    S = qkv.shape[0]
    nb = S // BLK

    def body(q_ref, kvp_ref, kvc_ref, dy_ref, bk_ref, relb_ref, sinks_ref,
             dq_ref, dkv_ref, dsink_ref, drel_ref, bias_ref, dbias_ref, carry_ref):
        i = pl.program_id(0)
        blk = nb - 1 - i

        @pl.when(i == 0)
        def _():
            _build_bias(bk_ref, relb_ref, bias_ref)
            dbias_ref[...] = jnp.zeros_like(dbias_ref)
            carry_ref[...] = jnp.zeros_like(carry_ref)
            dsink_ref[...] = jnp.zeros_like(dsink_ref)
            drel_ref[...] = jnp.zeros_like(drel_ref)

        ops = _attn_operands(q_ref, kvp_ref, kvc_ref, bias_ref, sinks_ref)
        _, vjp = jax.vjp(functools.partial(_attn_heads, mask=_attn_mask(blk == 0)), *ops)
        dq_heads, dk_pads, dv_pads, dbias_heads, dsink_cols = vjp(tuple(_split_heads(dy_ref[...], 4)))
        dq_ref[...] = _join_heads(dq_heads)
        dk_prev = _join_pair(dk_pads[0][:BLK], dk_pads[1][:BLK])
        dk_cur = _join_pair(dk_pads[0][BLK:], dk_pads[1][BLK:])
        dv_prev = _join_pair(dv_pads[0][:BLK], dv_pads[1][:BLK])
        dv_cur = _join_pair(dv_pads[0][BLK:], dv_pads[1][BLK:])
        dkv_ref[...] = jnp.concatenate([dk_cur, dv_cur], axis=1) + carry_ref[...]
        carry_ref[...] = jnp.concatenate([dk_prev, dv_prev], axis=1)
        row = lax.broadcasted_iota(jnp.int32, (N_HEADS, LANE), 0)
        dsink = jnp.zeros((N_HEADS, LANE), F32)
        for h in range(N_HEADS):
            dbias_ref[h] += dbias_heads[h]
            dsink = dsink + jnp.where(row == h, jnp.sum(dsink_cols[h], axis=0, keepdims=True), 0.0)
        dsink_ref[...] += dsink

        @pl.when(i == nb - 1)
        def _():
            bk = bk_ref[...]
            r = lax.broadcasted_iota(jnp.int32, (N_BUCKETS, LANE), 0)
            l = lax.broadcasted_iota(jnp.int32, (N_BUCKETS, LANE), 1)
            res = jnp.zeros((N_BUCKETS, LANE), F32)
            for h in range(N_HEADS):
                db = dbias_ref[h]
                for b in range(N_BUCKETS):
                    v = jnp.sum(jnp.sum(jnp.where(bk == b, db, 0.0), axis=1, keepdims=True), axis=0, keepdims=True)
                    res = res + jnp.where((r == b) & (l == h), v, 0.0)
            drel_ref[...] = res

    smem = pl.BlockSpec(memory_space=pltpu.SMEM)
    rev = lambda i: (nb - 1 - i, 0)
    return pl.pallas_call(
        body, name="attn_bwd", grid=(nb,),
        in_specs=[pl.BlockSpec((BLK, ATTN_W), rev),
                  pl.BlockSpec((BLK, 2 * KV_W), lambda i: (jnp.maximum(nb - 2 - i, 0), 2)),
                  pl.BlockSpec((BLK, 2 * KV_W), lambda i: (nb - 1 - i, 2)),
                  pl.BlockSpec((BLK, ATTN_W), rev),
                  pl.BlockSpec((BLK, 2 * BLK), _fixed), smem, smem],
        out_specs=[pl.BlockSpec((BLK, ATTN_W), rev), pl.BlockSpec((BLK, 2 * KV_W), rev),
                   pl.BlockSpec((N_HEADS, LANE), _fixed), pl.BlockSpec((N_BUCKETS, LANE), _fixed)],
        out_shape=[jax.ShapeDtypeStruct((S, ATTN_W), F32), jax.ShapeDtypeStruct((S, 2 * KV_W), F32),
                   jax.ShapeDtypeStruct((N_HEADS, LANE), F32), jax.ShapeDtypeStruct((N_BUCKETS, LANE), F32)],
        scratch_shapes=[pltpu.VMEM((N_HEADS, BLK, 2 * BLK), F32), pltpu.VMEM((N_HEADS, BLK, 2 * BLK), F32),
                        pltpu.VMEM((BLK, 2 * KV_W), F32)],
    )(qkv, qkv, qkv, dy, buckets, rel_bias, sinks)


def _ssd_consts():
    r = lax.broadcasted_iota(jnp.int32, (BLK, BLK), 0)
    c = lax.broadcasted_iota(jnp.int32, (BLK, BLK), 1)
    causal = c <= r
    tri = causal.astype(F32)
    last = r == BLK - 1
    expand = [(r == h).astype(F32) for h in range(N_HEADS)]
    return causal, tri, last, expand


def _ssd_chunk(pre_heads, pre_b, pre_c, dt_raw, prev, dtb_row, alog_row, d_rows, consts):
    causal, tri, last, expand = consts
    xs = [_silu(t) for t in pre_heads]
    bg = [_silu(t) for t in pre_b]
    cg = [_silu(t) for t in pre_c]
    dt = _softplus(dt_raw + dtb_row)
    acs = _mm_hi(tri, dt * (-jnp.exp(alog_row)))
    cb = [_mm_nt(cg[g], bg[g]) for g in range(2)]
    ys, hs = [], []
    for h in range(N_HEADS):
        g = h // 4
        dt_b = _mm_hi(dt, expand[h])
        a_b = _mm_hi(acs, expand[h])
        lmat = jnp.exp(jnp.where(causal, a_b - a_b.T, -1e30))
        xdt = xs[h] * dt_b
        a_last = jnp.sum(jnp.where(last, a_b, 0.0), axis=0, keepdims=True)
        y = _mm(cb[g] * lmat, xdt) + _mm(cg[g], prev[h]) * jnp.exp(a_b) + d_rows[h] * xs[h]
        st = _mm_tn(bg[g], xdt * jnp.exp(a_last - a_b))
        ys.append(y)
        hs.append(prev[h] * jnp.exp(a_last) + st)
    return tuple(ys), tuple(hs)


def _conv_pre(ext_ref, halo, blk, cw_ref, cb_ref):
    ext_ref[0:8, :] = halo
    ext_ref[8:8 + BLK, :] = blk
    pre = cb_ref[...] + cw_ref[0:1, :] * ext_ref[pl.ds(5, BLK), :]
    for k in range(1, 4):
        pre = pre + cw_ref[k:k + 1, :] * ext_ref[pl.ds(5 + k, BLK), :]
    return pre


def _ssd_split(pre):
    heads = _split_heads(pre[:, :SSM_W], 4)
    pb = [pre[:, SSM_W + g * D_STATE:SSM_W + (g + 1) * D_STATE] for g in range(2)]
    pc = [pre[:, SSM_W + 2 * D_STATE + g * D_STATE:SSM_W + 2 * D_STATE + (g + 1) * D_STATE] for g in range(2)]
    return heads, pb, pc


def _ssd_fwd(xbc, dt_raw, conv_w, conv_b, dtb_row, alog_row, d_exp):
    S = xbc.shape[0]
    nc = S // BLK

    def body(xbc_ref, halo_ref, dt_ref, cw_ref, cb_ref, dtb_ref, alog_ref, d_ref, y_ref, prev_ref, ext_ref, state_ref):
        i = pl.program_id(0)

        @pl.when(i == 0)
        def _():
            state_ref[...] = jnp.zeros_like(state_ref)

        halo = halo_ref[...] * jnp.where(i > 0, 1.0, 0.0)
        pre = _conv_pre(ext_ref, halo, xbc_ref[...], cw_ref, cb_ref)
        heads, pb, pc = _ssd_split(pre)
        prev = [state_ref[h] for h in range(N_HEADS)]
        for h in range(N_HEADS):
            prev_ref[0, h] = prev[h]
        d_rows = [d_ref[h:h + 1, :] for h in range(N_HEADS)]
        ys, hs = _ssd_chunk(heads, pb, pc, dt_ref[...], prev, dtb_ref[...], alog_ref[...], d_rows, _ssd_consts())
        for h in range(N_HEADS):
            state_ref[h] = hs[h]
        y_ref[...] = _join_heads(ys)

    vec = pl.BlockSpec((1, LANE), _fixed)
    return pl.pallas_call(
        body, name="ssd_fwd", grid=(nc,),
        in_specs=[pl.BlockSpec((BLK, XBC_W), _row),
                  pl.BlockSpec((8, XBC_W), lambda i: (jnp.maximum(i * (BLK // 8) - 1, 0), 0)),
                  pl.BlockSpec((BLK, LANE), _row),
                  pl.BlockSpec((4, XBC_W), _fixed), pl.BlockSpec((1, XBC_W), _fixed), vec, vec,
                  pl.BlockSpec((N_HEADS, LANE), _fixed)],
        out_specs=[pl.BlockSpec((BLK, SSM_W), _row),
                   pl.BlockSpec((1, N_HEADS, D_STATE, LANE), lambda i: (i, 0, 0, 0))],
        out_shape=[jax.ShapeDtypeStruct((S, SSM_W), F32), jax.ShapeDtypeStruct((nc, N_HEADS, D_STATE, LANE), F32)],
        scratch_shapes=[pltpu.VMEM((8 + BLK, XBC_W), F32), pltpu.VMEM((N_HEADS, D_STATE, LANE), F32)],
    )(xbc, xbc, dt_raw, conv_w, conv_b, dtb_row, alog_row, d_exp)


def _ssd_bwd(xbc, dt_raw, prev_states, dy, conv_w, conv_b, dtb_row, alog_row, d_exp):
    S = xbc.shape[0]
    nc = S // BLK

    def body(xbc_ref, halo_ref, dt_ref, prev_ref, dy_ref, cw_ref, cb_ref, dtb_ref, alog_ref, d_ref,
             dxbc_ref, ddt_ref, dcw_ref, dvec_ref, dd_ref, ext_ref, dpe_ref, gstate_ref, ghalo_ref):
        i = pl.program_id(0)
        c = nc - 1 - i

        @pl.when(i == 0)
        def _():
            gstate_ref[...] = jnp.zeros_like(gstate_ref)
            ghalo_ref[...] = jnp.zeros_like(ghalo_ref)
            dcw_ref[...] = jnp.zeros_like(dcw_ref)
            dvec_ref[...] = jnp.zeros_like(dvec_ref)
            dd_ref[...] = jnp.zeros_like(dd_ref)
            dpe_ref[...] = jnp.zeros_like(dpe_ref)

        halo = halo_ref[...] * jnp.where(c > 0, 1.0, 0.0)
        pre = _conv_pre(ext_ref, halo, xbc_ref[...], cw_ref, cb_ref)
        heads, pb, pc = _ssd_split(pre)
        prev = [prev_ref[0, h] for h in range(N_HEADS)]
        d_rows = [d_ref[h:h + 1, :] for h in range(N_HEADS)]
        _, vjp = jax.vjp(functools.partial(_ssd_chunk, consts=_ssd_consts()),
                         heads, pb, pc, dt_ref[...], prev, dtb_ref[...], alog_ref[...], d_rows)
        dys = tuple(_split_heads(dy_ref[...], 4))
        dhs = tuple(gstate_ref[h] for h in range(N_HEADS))
        dheads, dpb, dpc, ddt, dprev, ddtb, dalog, dd_rows = vjp((dys, dhs))
        for h in range(N_HEADS):
            gstate_ref[h] = dprev[h]
            dd_ref[h:h + 1, :] += dd_rows[h]
        ddt_ref[...] = ddt
        dvec_ref[0:1, :] += ddtb
        dvec_ref[1:2, :] += dalog
        dpre = jnp.concatenate([_join_heads(dheads)] + list(dpb) + list(dpc), axis=1)
        dpe_ref[8:8 + BLK, :] = dpre
        dext = cw_ref[0:1, :] * dpe_ref[pl.ds(3, 8 + BLK), :]
        dcw_ref[0:1, :] += jnp.sum(dpre * ext_ref[pl.ds(5, BLK), :], axis=0, keepdims=True)
        for k in range(1, 4):
            dext = dext + cw_ref[k:k + 1, :] * dpe_ref[pl.ds(3 - k, 8 + BLK), :]
            dcw_ref[k:k + 1, :] += jnp.sum(dpre * ext_ref[pl.ds(5 + k, BLK), :], axis=0, keepdims=True)
        dcw_ref[4:5, :] += jnp.sum(dpre, axis=0, keepdims=True)
        dxbc_ref[...] = dext[8:, :]
        dxbc_ref[BLK - 8:BLK, :] += ghalo_ref[...]
        ghalo_ref[...] = dext[:8, :]

    vec = pl.BlockSpec((1, LANE), _fixed)
    rev = lambda i: (nc - 1 - i, 0)
    return pl.pallas_call(
        body, name="ssd_bwd", grid=(nc,),
        in_specs=[pl.BlockSpec((BLK, XBC_W), rev),
                  pl.BlockSpec((8, XBC_W), lambda i: (jnp.maximum((nc - 1 - i) * (BLK // 8) - 1, 0), 0)),
                  pl.BlockSpec((BLK, LANE), rev),
                  pl.BlockSpec((1, N_HEADS, D_STATE, LANE), lambda i: (nc - 1 - i, 0, 0, 0)),
                  pl.BlockSpec((BLK, SSM_W), rev),
                  pl.BlockSpec((4, XBC_W), _fixed), pl.BlockSpec((1, XBC_W), _fixed), vec, vec,
                  pl.BlockSpec((N_HEADS, LANE), _fixed)],
        out_specs=[pl.BlockSpec((BLK, XBC_W), rev), pl.BlockSpec((BLK, LANE), rev),
                   pl.BlockSpec((8, XBC_W), _fixed), pl.BlockSpec((8, LANE), _fixed),
                   pl.BlockSpec((N_HEADS, LANE), _fixed)],
        out_shape=[jax.ShapeDtypeStruct((S, XBC_W), F32), jax.ShapeDtypeStruct((S, LANE), F32),
                   jax.ShapeDtypeStruct((8, XBC_W), F32), jax.ShapeDtypeStruct((8, LANE), F32),
                   jax.ShapeDtypeStruct((N_HEADS, LANE), F32)],
        scratch_shapes=[pltpu.VMEM((8 + BLK, XBC_W), F32), pltpu.VMEM((16 + BLK, XBC_W), F32),
                        pltpu.VMEM((N_HEADS, D_STATE, LANE), F32), pltpu.VMEM((8, XBC_W), F32)],
        compiler_params=_cparams(VMEM_BIG),
    )(xbc, xbc, dt_raw, prev_states, dy, conv_w, conv_b, dtb_row, alog_row, d_exp)


def _adamw_math(w, g, m, v):
    m = ADAM_B1 * m + (1.0 - ADAM_B1) * g
    v = ADAM_B2 * v + (1.0 - ADAM_B2) * jnp.square(g)
    m_hat = m / (1.0 - ADAM_B1 ** ADAM_STEP)
    v_hat = v / (1.0 - ADAM_B2 ** ADAM_STEP)
    delta = -ADAM_LR * (m_hat / (jnp.sqrt(v_hat) + ADAM_EPS) + ADAM_WD * w)
    return delta, m, v


def _reduce_adamw(parts, w, m, v, name):
    R, C = w.shape
    tr = max(t for t in range(16, 257, 16) if R % t == 0)

    def body(p_ref, w_ref, m_ref, v_ref, g_ref, d_ref, nm_ref, nv_ref):
        g = p_ref[0].astype(F32)
        for i in range(1, N_DEV):
            g = g + p_ref[i].astype(F32)
        d, nm, nv = _adamw_math(w_ref[...], g, m_ref[...], v_ref[...])
        g_ref[...] = g
        d_ref[...] = d
        nm_ref[...] = nm
        nv_ref[...] = nv

    blk = pl.BlockSpec((tr, C), _row)
    return pl.pallas_call(
        body, name=name, grid=(R // tr,),
        in_specs=[pl.BlockSpec((N_DEV, tr, C), lambda i: (0, i, 0)), blk, blk, blk],
        out_specs=[blk] * 4, out_shape=[jax.ShapeDtypeStruct((R, C), F32)] * 4,
    )(parts, w, m, v)


def _adamw_small(parts, w, m, v, name):
    summed = parts.ndim == 3

    def body(p_ref, w_ref, m_ref, v_ref, g_ref, d_ref, nm_ref, nv_ref):
        if summed:
            g = p_ref[0]
            for i in range(1, N_DEV):
                g = g + p_ref[i]
        else:
            g = p_ref[...]
        d, nm, nv = _adamw_math(w_ref[...], g, m_ref[...], v_ref[...])
        g_ref[...] = g
        d_ref[...] = d
        nm_ref[...] = nm
        nv_ref[...] = nv

    return pl.pallas_call(body, name=name, out_shape=[jax.ShapeDtypeStruct(w.shape, F32)] * 4)(parts, w, m, v)


def _ada_w_update(c_all, dmod_mine, w, m, v):
    def body(c_ref, dm_ref, w_ref, m_ref, v_ref, g_ref, d_ref, nm_ref, nv_ref):
        g = lax.dot_general(_silu(c_ref[...]), dm_ref[...], (((0,), (0,)), ((), ())),
                            precision=HI, preferred_element_type=F32)
        d, nm, nv = _adamw_math(w_ref[...], g, m_ref[...], v_ref[...])
        g_ref[...] = g
        d_ref[...] = d
        nm_ref[...] = nm
        nv_ref[...] = nv

    return pl.pallas_call(body, name="ada_w_update", out_shape=[jax.ShapeDtypeStruct(w.shape, F32)] * 4,
                          compiler_params=_cparams(VMEM_BIG))(c_all, dmod_mine, w, m, v)


_SMALL = (("ada_b", 6144), ("norm1", 1024), ("conv_b", 1024), ("dt_bias", 8), ("A_log", 8), ("D_skip", 8),
          ("sinks", 8), ("attn_out_norm", 512), ("ssm_out_norm", 512), ("norm2", 1024), ("rel_bias", 256),
          ("final_norm", 1024))


def _pack_rows(pieces):
    rows = []
    for p in pieces:
        flat = p.reshape(-1).astype(F32)
        n = -(-flat.shape[0] // 1024) * 1024
        rows.append(jnp.pad(flat, (0, n - flat.shape[0])).reshape(n // LANE, LANE))
    return jnp.concatenate(rows, axis=0)


def _unpack_rows(packed, sizes):
    out, r = [], 0
    for n in sizes:
        nr = -(-n // 1024) * 8
        out.append(packed[r:r + nr].reshape(-1)[:n])
        r += nr
    return out


def _local_step(x, tgt, mod, w_in, w_o, w_gu, w_d, conv_w, p):
    S = x.shape[0]
    tm = min(512, S)
    tmm = min(256, S)
    shift1, scale1, gate1, shift2, scale2, gate2 = [mod[i:i + 1] for i in range(6)]
    buckets = jnp.asarray(_t5_bucket_table())
    dtb_row = jnp.pad(p["dt_bias"], ((0, 0), (0, LANE - N_HEADS)))
    alog_row = jnp.pad(p["A_log"], ((0, 0), (0, LANE - N_HEADS)))
    d_exp = jnp.broadcast_to(p["D_skip"].reshape(N_HEADS, 1), (N_HEADS, LANE))
    sinks = p["sinks"].reshape(N_HEADS)

    qkv, z, xbc, dt_raw = _in_proj_fwd(x, p["norm1"], scale1, shift1, w_in, tm)
    ya = _attn_fwd(qkv, buckets, p["rel_bias"], sinks)
    ys, prev_states = _ssd_fwd(xbc, dt_raw, conv_w, p["conv_b"], dtb_row, alog_row, d_exp)
    x1 = _out_proj_fwd(x, ya, ys, z, p["attn_out_norm"], p["ssm_out_norm"], gate1, w_o, tm)
    dx1, h2, dgu, act, dmlp, acc2 = _mlp_loss(x1, tgt, p["norm2"], scale2, shift2, gate2, p["final_norm"],
                                              w_gu, w_d, tmm)
    g_w_gu = _wgrad(h2, dgu, 512, tm, "wgrad_gate_up")
    g_w_d = _wgrad(act, dmlp, 512, tm, "wgrad_down")
    dya, dys, dz, u, dmix, acc1 = _out_proj_bwd(dx1, ya, ys, z, p["attn_out_norm"], p["ssm_out_norm"], gate1, w_o, tm)
    g_w_o = _wgrad(u, dmix, 512, tm, "wgrad_out")
    dq, dkv, dsink, drel = _attn_bwd(qkv, dya, buckets, p["rel_bias"], sinks)
    dxbc, ddt, dcw, dvec, dd = _ssd_bwd(xbc, dt_raw, prev_states, dys, conv_w, p["conv_b"], dtb_row, alog_row, d_exp)
    gx, h1, dproj, acc0 = _in_proj_bwd(x, dx1, dq, dkv, dz, dxbc, ddt, p["norm1"], scale1, shift1, w_in, tm)
    g_w_in = _wgrad(h1, dproj, IN_PAD, tm, "wgrad_in")

    dmod = jnp.concatenate([acc0[2:3], acc0[1:2], acc1[0:1], acc2[2:3], acc2[1:2], acc2[3:4]], axis=0)
    small = {
        "dmod": dmod, "norm1": acc0[0], "conv_b": dcw[4], "dt_bias": dvec[0, :N_HEADS], "A_log": dvec[1, :N_HEADS],
        "D_skip": jnp.sum(dd, axis=1), "sinks": dsink[:, 0], "attn_out_norm": acc1[1, :ATTN_W],
        "ssm_out_norm": acc1[1, ATTN_W:], "norm2": acc2[0], "rel_bias": drel[:, :N_HEADS], "final_norm": acc2[4],
        "conv_w": dcw[:4], "loss": acc2[5, :1],
    }
    return gx, (g_w_in, g_w_o, g_w_gu, g_w_d), small


def kernel(x, c, ada_w, ada_b, norm1, w_in, conv_w, conv_b, dt_bias, A_log, D_skip, sinks, attn_out_norm, ssm_out_norm, w_o, norm2, w_gate_up, w_down, rel_bias, final_norm, loss_target, m_ada_w, m_ada_b, m_norm1, m_w_in, m_conv_w, m_conv_b, m_dt_bias, m_A_log, m_D_skip, m_sinks, m_attn_out_norm, m_ssm_out_norm, m_w_o, m_norm2, m_w_gate_up, m_w_down, m_rel_bias, m_final_norm, v_ada_w, v_ada_b, v_norm1, v_w_in, v_conv_w, v_conv_b, v_dt_bias, v_A_log, v_D_skip, v_sinks, v_attn_out_norm, v_ssm_out_norm, v_w_o, v_norm2, v_w_gate_up, v_w_down, v_rel_bias, v_final_norm):
    given = dict(ada_b=ada_b, norm1=norm1, conv_b=conv_b, dt_bias=dt_bias, A_log=A_log, D_skip=D_skip, sinks=sinks,
                 attn_out_norm=attn_out_norm, ssm_out_norm=ssm_out_norm, norm2=norm2, rel_bias=rel_bias,
                 final_norm=final_norm)
    moments_m = dict(ada_b=m_ada_b, norm1=m_norm1, conv_b=m_conv_b, dt_bias=m_dt_bias, A_log=m_A_log, D_skip=m_D_skip,
                     sinks=m_sinks, attn_out_norm=m_attn_out_norm, ssm_out_norm=m_ssm_out_norm, norm2=m_norm2,
                     rel_bias=m_rel_bias, final_norm=m_final_norm)
    moments_v = dict(ada_b=v_ada_b, norm1=v_norm1, conv_b=v_conv_b, dt_bias=v_dt_bias, A_log=v_A_log, D_skip=v_D_skip,
                     sinks=v_sinks, attn_out_norm=v_attn_out_norm, ssm_out_norm=v_ssm_out_norm, norm2=v_norm2,
                     rel_bias=v_rel_bias, final_norm=v_final_norm)
    me = _lin(_my_pos())
    S = x.shape[1]
    xs, tgt = x.reshape(S, D_MODEL), loss_target.reshape(S, D_MODEL)
    ada_w2 = ada_w[0]
    chunk = ada_w2.shape[1]

    mod = _mod_exchange(c, ada_w2, ada_b.reshape(N_DEV, chunk)).reshape(6, D_MODEL)

    g_in, g_o, g_gu, g_d, g_cw = _exchange(
        [w_in[0].astype(WIRE_DTYPE), w_o[0].astype(WIRE_DTYPE), w_gate_up[0].astype(WIRE_DTYPE),
         w_down[0].astype(WIRE_DTYPE), conv_w[0]], scatter=False, name="gather_weights")
    w_in_full = jnp.transpose(g_in, (1, 0, 2)).reshape(D_MODEL, IN_W)
    w_in_full = jnp.pad(w_in_full, ((0, 0), (0, IN_PAD - IN_W)))
    w_gu_full = jnp.transpose(g_gu, (1, 0, 2)).reshape(D_MODEL, 2 * D_FF)
    w_o_full = g_o.reshape(D_MODEL, D_MODEL)
    w_d_full = g_d.reshape(D_FF, D_MODEL)
    conv_w_full = jnp.transpose(g_cw, (1, 0, 2)).reshape(4, XBC_W)

    p = {k: (v if k == "rel_bias" else v.reshape(1, -1)) for k, v in given.items()}
    gx, (gw_in, gw_o, gw_gu, gw_d), small = _local_step(xs, tgt, mod, w_in_full, w_o_full, w_gu_full, w_d_full,
                                                        conv_w_full, p)

    n_in, n_gu = IN_W // N_DEV, 2 * D_FF // N_DEV
    parts = [
        jnp.transpose(gw_in[:, :IN_W].reshape(D_MODEL, N_DEV, n_in), (1, 0, 2)).astype(WIRE_DTYPE),
        gw_o.reshape(N_DEV, D_MODEL // N_DEV, D_MODEL).astype(WIRE_DTYPE),
        jnp.transpose(gw_gu.reshape(D_MODEL, N_DEV, n_gu), (1, 0, 2)).astype(WIRE_DTYPE),
        gw_d.reshape(N_DEV, D_FF // N_DEV, D_MODEL).astype(WIRE_DTYPE),
    ]
    r_in, r_o, r_gu, r_d = _exchange(parts, scatter=True, name="scatter_grads")

    names = [n for n, _ in _SMALL]
    sizes = [s for _, s in _SMALL]
    packed = _pack_rows([small["dmod"]] + [small[n] for n in names[1:]] + [small["conv_w"], small["loss"], c])
    (gathered,) = _exchange([packed], scatter=False, name="gather_small")
    n_rows = packed.shape[0]
    pw = _pack_rows([given[n] for n in names] + [jnp.zeros((4 * XBC_W + 1024 + 1024,), F32)])
    pm = _pack_rows([moments_m[n] for n in names] + [jnp.zeros((4 * XBC_W + 1024 + 1024,), F32)])
    pv = _pack_rows([moments_v[n] for n in names] + [jnp.ones((4 * XBC_W + 1024 + 1024,), F32)])
    sg, sd, sm, sv = _adamw_small(gathered, pw, pm, pv, "adamw_small")
    all_sizes = sizes + [4 * XBC_W, 1, 1024]
    g_small = _unpack_rows(sg, all_sizes)
    d_small = _unpack_rows(sd, all_sizes)
    m_small = _unpack_rows(sm, all_sizes)
    v_small = _unpack_rows(sv, all_sizes)
    loss = g_small[len(names) + 1][0]
    conv_w_grad_full = g_small[len(names)].reshape(4, XBC_W)

    offs = np.cumsum([0] + [-(-s // 1024) * 8 for s in all_sizes])
    c_all = gathered[:, offs[len(names) + 2]:offs[len(names) + 2] + 8, :].reshape(N_DEV, D_MODEL)
    dmod_all = gathered[:, :48, :].reshape(N_DEV, 6 * D_MODEL)
    dmod_mine = lax.dynamic_slice(dmod_all, (0, me * chunk), (N_DEV, chunk))
    ada = _ada_w_update(c_all, dmod_mine, ada_w2, m_ada_w[0], v_ada_w[0])

    big = {
        "ada_w": ada,
        "w_in": _reduce_adamw(r_in, w_in[0], m_w_in[0], v_w_in[0], "adamw_w_in"),
        "w_o": _reduce_adamw(r_o, w_o[0], m_w_o[0], v_w_o[0], "adamw_w_o"),
        "w_gate_up": _reduce_adamw(r_gu, w_gate_up[0], m_w_gate_up[0], v_w_gate_up[0], "adamw_w_gate_up"),
        "w_down": _reduce_adamw(r_d, w_down[0], m_w_down[0], v_w_down[0], "adamw_w_down"),
    }
    n_cw = XBC_W // N_DEV
    cw_grad_mine = lax.dynamic_slice(conv_w_grad_full, (0, me * n_cw), (4, n_cw))
    big["conv_w"] = _adamw_small(cw_grad_mine, conv_w[0], m_conv_w[0], v_conv_w[0], "adamw_conv_w")

    order = ['ada_w', 'ada_b', 'norm1', 'w_in', 'conv_w', 'conv_b', 'dt_bias', 'A_log', 'D_skip', 'sinks',
             'attn_out_norm', 'ssm_out_norm', 'w_o', 'norm2', 'w_gate_up', 'w_down', 'rel_bias', 'final_norm']
    shapes = dict(ada_w=ada_w.shape, ada_b=ada_b.shape, norm1=norm1.shape, w_in=w_in.shape, conv_w=conv_w.shape,
                  conv_b=conv_b.shape, dt_bias=dt_bias.shape, A_log=A_log.shape, D_skip=D_skip.shape,
                  sinks=sinks.shape, attn_out_norm=attn_out_norm.shape, ssm_out_norm=ssm_out_norm.shape,
                  w_o=w_o.shape, norm2=norm2.shape, w_gate_up=w_gate_up.shape, w_down=w_down.shape,
                  rel_bias=rel_bias.shape, final_norm=final_norm.shape)
    outs = [[], [], [], []]
    for name in order:
        for kind in range(4):
            if name in big:
                val = big[name][kind]
            else:
                val = (g_small, d_small, m_small, v_small)[kind][names.index(name)]
            outs[kind].append(val.reshape(shapes[name]))
    return (loss, gx.reshape(x.shape), *outs[0], *outs[1], *outs[2], *outs[3])
```

```python
import functools

import numpy as np
import jax
import jax.numpy as jnp
from jax import lax
from jax.experimental import pallas as pl
from jax.experimental.pallas import tpu as pltpu

F32 = jnp.float32
MXU_DTYPE = jnp.bfloat16
WIRE_DTYPE = jnp.bfloat16
HI = lax.Precision.HIGHEST
MESH = pl.DeviceIdType.MESH
N_DEV = 8

D_MODEL = 1024
ATTN_W = 512
KV_W = 128
SSM_W = 512
XBC_W = 1024
N_HEADS = 8
D_STATE = 128
D_FF = 2816
IN_W = 2312
IN_PAD = 2432
BLK = 128
N_BUCKETS = 32
EPS = 1e-6
LANE = 128
HALF = 64

ADAM_LR, ADAM_B1, ADAM_B2, ADAM_EPS, ADAM_WD, ADAM_STEP = 0.001, 0.9, 0.999, 1e-08, 0.01, 10

VMEM_BIG = 56 * 1024 * 1024


def _cparams(vmem=None):
    if vmem is None:
        return pltpu.CompilerParams()
    return pltpu.CompilerParams(vmem_limit_bytes=vmem)


def _mm(a, b):
    return jnp.dot(a.astype(MXU_DTYPE), b.astype(MXU_DTYPE), preferred_element_type=F32)


def _mm_nt(a, b):
    return lax.dot_general(a.astype(MXU_DTYPE), b.astype(MXU_DTYPE), (((1,), (1,)), ((), ())),
                           preferred_element_type=F32)


def _mm_tn(a, b):
    return lax.dot_general(a.astype(MXU_DTYPE), b.astype(MXU_DTYPE), (((0,), (0,)), ((), ())),
                           preferred_element_type=F32)


def _mm_hi(a, b):
    return jnp.dot(a, b, precision=HI, preferred_element_type=F32)


def _silu(x):
    return x * jax.nn.sigmoid(x)


def _softplus(x):
    return jnp.maximum(x, 0.0) + jnp.log1p(jnp.exp(-jnp.abs(x)))


def _rms(x, g, n):
    return x * lax.rsqrt(jnp.sum(x * x, axis=-1, keepdims=True) * (1.0 / n) + EPS) * g


def _modnorm(x, g, scale, shift):
    return _rms(x, g, x.shape[-1]) * (1.0 + scale) + shift


def _lane_iota(shape):
    return lax.broadcasted_iota(jnp.int32, shape, len(shape) - 1)


def _split_pair(t):
    lane = _lane_iota(t.shape)
    lo = jnp.where(lane < HALF, t, 0.0)
    hi = pltpu.roll(jnp.where(lane >= HALF, t, 0.0), HALF, 1)
    return lo, hi


def _join_pair(lo, hi):
    lane = _lane_iota(lo.shape)
    return jnp.where(lane < HALF, lo, pltpu.roll(hi, HALF, 1))


def _split_heads(t, n_pairs):
    out = []
    for p in range(n_pairs):
        out.extend(_split_pair(t[:, p * LANE:(p + 1) * LANE]))
    return out


def _join_heads(hs):
    return jnp.concatenate([_join_pair(hs[2 * p], hs[2 * p + 1]) for p in range(len(hs) // 2)], axis=1)


def _t5_bucket_table():
    dist = np.arange(BLK)[:, None] + BLK - np.arange(2 * BLK)[None, :]
    n = np.maximum(dist, 0)
    max_exact = N_BUCKETS // 2
    large = max_exact + (np.log(np.maximum(n, 1) / max_exact) / np.log(128 / max_exact)
                         * (N_BUCKETS - max_exact)).astype(np.int32)
    large = np.minimum(large, N_BUCKETS - 1)
    return np.where(n < max_exact, n, large).astype(np.int32)


def _my_pos():
    return lax.axis_index("x"), lax.axis_index("y"), lax.axis_index("c")


def _peer(k):
    x, y, c = _my_pos()
    return (1 - x if k & 4 else x, 1 - y if k & 2 else y, 1 - c if k & 1 else c)


def _lin(pos):
    return 4 * pos[0] + 2 * pos[1] + pos[2]


def _xchg_copies(ins, outs, sems, scatter):
    local_sem, send_sem, recv_sem = sems
    me = _lin(_my_pos())
    local, remote = [], []
    for a in range(len(ins)):
        src = ins[a].at[me] if scatter else ins[a]
        local.append(pltpu.make_async_copy(src, outs[a].at[me], local_sem.at[a]))
    for k in range(1, N_DEV):
        peer = _peer(k)
        for a in range(len(ins)):
            src = ins[a].at[_lin(peer)] if scatter else ins[a]
            remote.append(pltpu.make_async_remote_copy(src, outs[a].at[me], send_sem.at[a, k - 1],
                                                       recv_sem.at[a, k - 1], device_id=peer, device_id_type=MESH))
    return local, remote


def _xchg_start(ins, outs, sems, scatter):
    local, remote = _xchg_copies(ins, outs, sems, scatter)
    for cp in local + remote:
        cp.start()


def _xchg_wait(ins, outs, sems, scatter):
    local, remote = _xchg_copies(ins, outs, sems, scatter)
    for cp in local:
        cp.wait()
    for cp in remote:
        cp.wait_send()
        cp.wait_recv()


def _xchg_shapes(arrs, scatter):
    n = len(arrs)
    if scatter:
        out_shape = [jax.ShapeDtypeStruct(a.shape, a.dtype) for a in arrs]
    else:
        out_shape = [jax.ShapeDtypeStruct((N_DEV,) + a.shape, a.dtype) for a in arrs]
    sems = [pltpu.SemaphoreType.DMA((n,)), pltpu.SemaphoreType.DMA((n, N_DEV - 1)),
            pltpu.SemaphoreType.DMA((n, N_DEV - 1))]
    return out_shape, sems


def _exchange(arrs, scatter, name):
    n = len(arrs)
    out_shape, sems = _xchg_shapes(arrs, scatter)

    def body(*refs):
        ins, outs, s = refs[:n], refs[n:2 * n], refs[2 * n:]
        _xchg_start(ins, outs, s, scatter)
        _xchg_wait(ins, outs, s, scatter)

    hbm = pl.BlockSpec(memory_space=pltpu.HBM)
    return pl.pallas_call(body, name=name, out_shape=out_shape, in_specs=[hbm] * n, out_specs=[hbm] * n,
                          scratch_shapes=sems)(*arrs)


def _hosted_call(body, name, n_steps, in_specs, out_specs, out_shape, scratch_shapes, args, xchg, cparams):
    arrs, scatter = xchg
    n, n_in, n_out, n_scr = len(arrs), len(in_specs), len(out_specs), len(scratch_shapes)
    x_shape, x_sems = _xchg_shapes(arrs, scatter)

    def hosted(*refs):
        ins, refs = refs[:n_in], refs[n_in:]
        x_in, refs = refs[:n], refs[n:]
        outs, refs = refs[:n_out], refs[n_out:]
        x_out, refs = refs[:n], refs[n:]
        scr, sems = refs[:n_scr], refs[n_scr:]

        @pl.when(pl.program_id(0) == 0)
        def _():
            _xchg_start(x_in, x_out, sems, scatter)

        body(*ins, *outs, *scr)

        @pl.when(pl.program_id(0) == n_steps - 1)
        def _():
            _xchg_wait(x_in, x_out, sems, scatter)

    hbm = pl.BlockSpec(memory_space=pltpu.HBM)
    res = pl.pallas_call(
        hosted, name=name, grid=(n_steps,), in_specs=list(in_specs) + [hbm] * n,
        out_specs=list(out_specs) + [hbm] * n, out_shape=list(out_shape) + x_shape,
        scratch_shapes=list(scratch_shapes) + x_sems, compiler_params=cparams,
    )(*args, *arrs)
    return res[:n_out], res[n_out:]


def _mod_exchange(c, ada_w, ada_b8):
    chunk = ada_w.shape[1]

    def body(c_ref, w_ref, b_ref, out_ref, cbuf, part, s1, r1, s2, r2):
        me = _lin(_my_pos())
        first = []
        for k in range(1, N_DEV):
            cp = pltpu.make_async_remote_copy(c_ref, cbuf.at[me], s1.at[k - 1], r1.at[k - 1],
                                              device_id=_peer(k), device_id_type=MESH)
            cp.start()
            first.append(cp)
        cbuf[me] = c_ref[...]
        for cp in first:
            cp.wait_send()
            cp.wait_recv()
        cond = _silu(jnp.concatenate([cbuf[i] for i in range(N_DEV)], axis=0))
        mod = _mm_hi(cond, w_ref[...]) + b_ref[pl.ds(me, 1), :]
        for j in range(N_DEV):
            part[j] = mod[j:j + 1, :]
        second = []
        for k in range(1, N_DEV):
            peer = _peer(k)
            cp = pltpu.make_async_remote_copy(part.at[_lin(peer)], out_ref.at[me], s2.at[k - 1], r2.at[k - 1],
                                              device_id=peer, device_id_type=MESH)
            cp.start()
            second.append(cp)
        out_ref[me] = part[me]
        for cp in second:
            cp.wait_send()
            cp.wait_recv()

    vm = pl.BlockSpec(memory_space=pltpu.VMEM)
    return pl.pallas_call(
        body, name="mod_exchange", out_shape=jax.ShapeDtypeStruct((N_DEV, 1, chunk), F32),
        in_specs=[vm, vm, vm], out_specs=vm,
        scratch_shapes=[pltpu.VMEM((N_DEV, 1, D_MODEL), F32), pltpu.VMEM((N_DEV, 1, chunk), F32)]
        + [pltpu.SemaphoreType.DMA((N_DEV - 1,))] * 4,
    )(c, ada_w, ada_b8)


def _row(i):
    return (i, 0)


def _fixed(i):
    return (0, 0)


def _in_proj_fwd(x, norm1, scale1, shift1, w_in, tm):
    S = x.shape[0]

    def body(x_ref, n_ref, sc_ref, sh_ref, w_ref, qkv_ref, z_ref, xbc_ref, dt_ref):
        h = _modnorm(x_ref[...], n_ref[...], sc_ref[...], sh_ref[...])
        p = _mm(h, w_ref[...])
        qkv_ref[...] = p[:, :768].astype(qkv_ref.dtype)
        z_ref[...] = p[:, 768:1280]
        xbc_ref[...] = p[:, 1280:2304]
        dt_ref[...] = p[:, 2304:IN_PAD]

    vec = pl.BlockSpec((1, D_MODEL), _fixed)
    return pl.pallas_call(
        body, name="in_proj_fwd", grid=(S // tm,),
        in_specs=[pl.BlockSpec((tm, D_MODEL), _row), vec, vec, vec, pl.BlockSpec((D_MODEL, IN_PAD), _fixed)],
        out_specs=[pl.BlockSpec((tm, 768), _row), pl.BlockSpec((tm, SSM_W), _row),
                   pl.BlockSpec((tm, XBC_W), _row), pl.BlockSpec((tm, LANE), _row)],
        out_shape=[jax.ShapeDtypeStruct((S, 768), MXU_DTYPE), jax.ShapeDtypeStruct((S, SSM_W), F32),
                   jax.ShapeDtypeStruct((S, XBC_W), F32), jax.ShapeDtypeStruct((S, LANE), F32)],
        compiler_params=_cparams(VMEM_BIG),
    )(x, norm1, scale1, shift1, w_in)


def _in_proj_bwd(x, dx1, dq, dkv, dz, dxbc, ddt, norm1, scale1, shift1, w_in, tm):
    S = x.shape[0]

    def body(x_ref, dx1_ref, dq_ref, dkv_ref, dz_ref, dxbc_ref, ddt_ref, n_ref, sc_ref, sh_ref, w_ref,
             gx_ref, h_ref, dp_ref, acc_ref):
        @pl.when(pl.program_id(0) == 0)
        def _():
            acc_ref[...] = jnp.zeros_like(acc_ref)

        h, vjp = jax.vjp(_modnorm, x_ref[...], n_ref[...], sc_ref[...], sh_ref[...])
        dp = jnp.concatenate([dq_ref[...].astype(MXU_DTYPE), dkv_ref[...].astype(MXU_DTYPE),
                              dz_ref[...].astype(MXU_DTYPE), dxbc_ref[...].astype(MXU_DTYPE),
                              ddt_ref[...].astype(MXU_DTYPE)], axis=1)
        dh = _mm_nt(dp, w_ref[...])
        dx, dn, dsc, dsh = vjp(dh)
        gx_ref[...] = dx1_ref[...] + dx
        h_ref[...] = h.astype(h_ref.dtype)
        dp_ref[...] = dp
        acc_ref[0:1, :] += dn
        acc_ref[1:2, :] += dsc
        acc_ref[2:3, :] += dsh

    vec = pl.BlockSpec((1, D_MODEL), _fixed)
    return pl.pallas_call(
        body, name="in_proj_bwd", grid=(S // tm,),
        in_specs=[pl.BlockSpec((tm, D_MODEL), _row), pl.BlockSpec((tm, D_MODEL), _row),
                  pl.BlockSpec((tm, ATTN_W), _row), pl.BlockSpec((tm, 2 * KV_W), _row),
                  pl.BlockSpec((tm, SSM_W), _row), pl.BlockSpec((tm, XBC_W), _row), pl.BlockSpec((tm, LANE), _row),
                  vec, vec, vec, pl.BlockSpec((D_MODEL, IN_PAD), _fixed)],
        out_specs=[pl.BlockSpec((tm, D_MODEL), _row), pl.BlockSpec((tm, D_MODEL), _row),
                   pl.BlockSpec((tm, IN_PAD), _row), pl.BlockSpec((8, D_MODEL), _fixed)],
        out_shape=[jax.ShapeDtypeStruct((S, D_MODEL), F32), jax.ShapeDtypeStruct((S, D_MODEL), MXU_DTYPE),
                   jax.ShapeDtypeStruct((S, IN_PAD), MXU_DTYPE), jax.ShapeDtypeStruct((8, D_MODEL), F32)],
        compiler_params=_cparams(VMEM_BIG),
    )(x, dx1, dq, dkv, dz, dxbc, ddt, norm1, scale1, shift1, w_in)


def _out_stage(ya, ys0, ys1, z0, z1, an, sn0, sn1):
    half = SSM_W // 2
    a = _rms(ya, an, ATTN_W)
    g0 = _rms(ys0 * _silu(z0), sn0, half)
    g1 = _rms(ys1 * _silu(z1), sn1, half)
    return jnp.concatenate([a, g0, g1], axis=1)


def _out_stage_args(ya_ref, ys_ref, z_ref, an_ref, sn_ref):
    half = SSM_W // 2
    return (ya_ref[...], ys_ref[:, :half], ys_ref[:, half:], z_ref[:, :half], z_ref[:, half:],
            an_ref[...], sn_ref[:, :half], sn_ref[:, half:])


def _out_proj_fwd(x, ya, ys, z, an, sn, gate1, w_o, tm):
    S = x.shape[0]

    def body(x_ref, ya_ref, ys_ref, z_ref, an_ref, sn_ref, g_ref, w_ref, x1_ref):
        u = _out_stage(*_out_stage_args(ya_ref, ys_ref, z_ref, an_ref, sn_ref))
        x1_ref[...] = x_ref[...] + g_ref[...] * _mm(u, w_ref[...])

    half = pl.BlockSpec((tm, ATTN_W), _row)
    hvec = pl.BlockSpec((1, ATTN_W), _fixed)
    return pl.pallas_call(
        body, name="out_proj_fwd", grid=(S // tm,),
        in_specs=[pl.BlockSpec((tm, D_MODEL), _row), half, half, half, hvec, hvec,
                  pl.BlockSpec((1, D_MODEL), _fixed), pl.BlockSpec((D_MODEL, D_MODEL), _fixed)],
        out_specs=pl.BlockSpec((tm, D_MODEL), _row),
        out_shape=jax.ShapeDtypeStruct((S, D_MODEL), F32),
        compiler_params=_cparams(VMEM_BIG),
    )(x, ya, ys, z, an, sn, gate1, w_o)


def _out_proj_bwd(dx1, ya, ys, z, an, sn, gate1, w_o, tm):
    S = dx1.shape[0]

    def body(dx1_ref, ya_ref, ys_ref, z_ref, an_ref, sn_ref, g_ref, w_ref,
             dya_ref, dys_ref, dz_ref, u_ref, dmix_ref, acc_ref):
        @pl.when(pl.program_id(0) == 0)
        def _():
            acc_ref[...] = jnp.zeros_like(acc_ref)

        u, vjp = jax.vjp(_out_stage, *_out_stage_args(ya_ref, ys_ref, z_ref, an_ref, sn_ref))
        dx1 = dx1_ref[...]
        mix = _mm(u, w_ref[...])
        dmix = dx1 * g_ref[...]
        du = _mm_nt(dmix, w_ref[...])
        dya, dys0, dys1, dz0, dz1, dan, dsn0, dsn1 = vjp(du)
        dya_ref[...] = dya
        dys_ref[...] = jnp.concatenate([dys0, dys1], axis=1)
        dz_ref[...] = jnp.concatenate([dz0, dz1], axis=1)
        u_ref[...] = u.astype(u_ref.dtype)
        dmix_ref[...] = dmix.astype(dmix_ref.dtype)
        acc_ref[0:1, :] += jnp.sum(dx1 * mix, axis=0, keepdims=True)
        acc_ref[1:2, :] += jnp.concatenate([dan, dsn0, dsn1], axis=1)

    half = pl.BlockSpec((tm, ATTN_W), _row)
    hvec = pl.BlockSpec((1, ATTN_W), _fixed)
    full = pl.BlockSpec((tm, D_MODEL), _row)
    return pl.pallas_call(
        body, name="out_proj_bwd", grid=(S // tm,),
        in_specs=[full, half, half, half, hvec, hvec,
                  pl.BlockSpec((1, D_MODEL), _fixed), pl.BlockSpec((D_MODEL, D_MODEL), _fixed)],
        out_specs=[half, half, half, full, full, pl.BlockSpec((8, D_MODEL), _fixed)],
        out_shape=[jax.ShapeDtypeStruct((S, ATTN_W), F32)] * 3
        + [jax.ShapeDtypeStruct((S, D_MODEL), MXU_DTYPE)] * 2 + [jax.ShapeDtypeStruct((8, D_MODEL), F32)],
        compiler_params=_cparams(VMEM_BIG),
    )(dx1, ya, ys, z, an, sn, gate1, w_o)


def _loss_rows(x2, fn, tgt):
    y = _rms(x2, fn, D_MODEL)
    per_row = jnp.sum(jnp.square(y - tgt), axis=1, keepdims=True)
    return jnp.sum(per_row, axis=0, keepdims=True) * (0.5 / D_MODEL)


def _mlp_loss(x1, tgt, norm2, scale2, shift2, gate2, fnorm, w_gu, w_d, tm):
    S = x1.shape[0]

    def body(x1_ref, t_ref, n_ref, sc_ref, sh_ref, g_ref, fn_ref, wgu_hbm, wd_hbm,
             dx1_ref, h_ref, dgu_ref, act_ref, dmlp_ref, acc_ref, wgu, wd):
        @pl.when(pl.program_id(0) == 0)
        def _():
            acc_ref[...] = jnp.zeros_like(acc_ref)
            pltpu.sync_copy(wgu_hbm, wgu)
            pltpu.sync_copy(wd_hbm, wd)

        x1 = x1_ref[...]
        gate2 = g_ref[...]
        h, vjp_h = jax.vjp(_modnorm, x1, n_ref[...], sc_ref[...], sh_ref[...])
        hb = h.astype(MXU_DTYPE)
        gu = _mm(hb, wgu[...])
        g, u = gu[:, :D_FF], gu[:, D_FF:]
        sg = jax.nn.sigmoid(g)
        silu_g = g * sg
        act = (silu_g * u).astype(MXU_DTYPE)
        mlp = _mm(act, wd[...])
        x2 = x1 + gate2 * mlp
        loss, vjp_loss = jax.vjp(_loss_rows, x2, fn_ref[...], t_ref[...])
        dx2, dfn, _ = vjp_loss(jnp.ones((1, 1), F32))
        dmlp = (dx2 * gate2).astype(MXU_DTYPE)
        dact = _mm_nt(dmlp, wd[...])
        dg = dact * u * (sg * (1.0 + g * (1.0 - sg)))
        du = dact * silu_g
        dgu = jnp.concatenate([dg, du], axis=1).astype(MXU_DTYPE)
        dh = _mm_nt(dgu, wgu[...])
        dx, dn, dsc, dsh = vjp_h(dh)
        dx1_ref[...] = dx2 + dx
        h_ref[...] = hb
        dgu_ref[...] = dgu
        act_ref[...] = act
        dmlp_ref[...] = dmlp
        acc_ref[0:1, :] += dn
        acc_ref[1:2, :] += dsc
        acc_ref[2:3, :] += dsh
        acc_ref[3:4, :] += jnp.sum(dx2 * mlp, axis=0, keepdims=True)
        acc_ref[4:5, :] += dfn
        acc_ref[5:6, :] += jnp.broadcast_to(loss, (1, D_MODEL))

    full = pl.BlockSpec((tm, D_MODEL), _row)
    vec = pl.BlockSpec((1, D_MODEL), _fixed)
    anyspec = pl.BlockSpec(memory_space=pl.ANY)
    return pl.pallas_call(
        body, name="mlp_loss", grid=(S // tm,),
        in_specs=[full, full, vec, vec, vec, vec, vec, anyspec, anyspec],
        out_specs=[full, full, pl.BlockSpec((tm, 2 * D_FF), _row), pl.BlockSpec((tm, D_FF), _row), full,
                   pl.BlockSpec((8, D_MODEL), _fixed)],
        out_shape=[jax.ShapeDtypeStruct((S, D_MODEL), F32), jax.ShapeDtypeStruct((S, D_MODEL), MXU_DTYPE),
                   jax.ShapeDtypeStruct((S, 2 * D_FF), MXU_DTYPE), jax.ShapeDtypeStruct((S, D_FF), MXU_DTYPE),
                   jax.ShapeDtypeStruct((S, D_MODEL), MXU_DTYPE), jax.ShapeDtypeStruct((8, D_MODEL), F32)],
        scratch_shapes=[pltpu.VMEM((D_MODEL, 2 * D_FF), MXU_DTYPE), pltpu.VMEM((D_FF, D_MODEL), MXU_DTYPE)],
        compiler_params=_cparams(VMEM_BIG),
    )(x1, tgt, norm2, scale2, shift2, gate2, fnorm, w_gu, w_d)


def _wgrad(a, g, tn, ts, name):
    S, K = a.shape
    N = g.shape[1]
    ns = S // ts

    def body(a_ref, g_ref, o_ref):
        @pl.when(pl.program_id(1) == 0)
        def _():
            o_ref[...] = jnp.zeros_like(o_ref)

        o_ref[...] += _mm_tn(a_ref[...], g_ref[...])

    return pl.pallas_call(
        body, name=name, grid=(N // tn, ns),
        in_specs=[pl.BlockSpec((ts, K), lambda j, s: (s, 0)), pl.BlockSpec((ts, tn), lambda j, s: (s, j))],
        out_specs=pl.BlockSpec((K, tn), lambda j, s: (0, j)),
        out_shape=jax.ShapeDtypeStruct((K, N), F32),
        compiler_params=_cparams(VMEM_BIG),
    )(a, g)


def _attn_heads(q_heads, k_pads, v_pads, bias_heads, sink_cols, mask):
    outs = []
    for h in range(N_HEADS):
        kv = h // 4
        s = _mm_nt(q_heads[h], k_pads[kv]) * (HALF ** -0.5) + bias_heads[h]
        s = jnp.where(mask, s, -1e30)
        m = lax.stop_gradient(jnp.maximum(jnp.max(s, axis=-1, keepdims=True), sink_cols[h]))
        p = jnp.exp(s - m)
        den = jnp.sum(p, axis=-1, keepdims=True) + jnp.exp(sink_cols[h] - m)
        outs.append(_mm(p, v_pads[kv]) / den)
    return tuple(outs)


def _attn_mask(first):
    i = lax.broadcasted_iota(jnp.int32, (BLK, 2 * BLK), 0)
    j = lax.broadcasted_iota(jnp.int32, (BLK, 2 * BLK), 1)
    return (j > i) & (j <= i + BLK) & (j >= jnp.where(first, BLK, 0))


def _attn_operands(q_ref, kvp_ref, kvc_ref, bias_ref, sinks_ref):
    q_heads = _split_heads(q_ref[...].astype(F32), 4)
    kvp = kvp_ref[...].astype(F32)
    kvc = kvc_ref[...].astype(F32)
    kp, kc = _split_pair(kvp[:, :LANE]), _split_pair(kvc[:, :LANE])
    vp, vc = _split_pair(kvp[:, LANE:]), _split_pair(kvc[:, LANE:])
    k_pads = [jnp.concatenate([kp[g], kc[g]], axis=0) for g in range(2)]
    v_pads = [jnp.concatenate([vp[g], vc[g]], axis=0) for g in range(2)]
    bias_heads = [bias_ref[h] for h in range(N_HEADS)]
    sink_cols = [jnp.full((BLK, 1), sinks_ref[h], F32) for h in range(N_HEADS)]
    return q_heads, k_pads, v_pads, bias_heads, sink_cols


def _build_bias(bucket_ref, relb_ref, bias_ref):
    bk = bucket_ref[...]
    for h in range(N_HEADS):
        acc = jnp.zeros((BLK, 2 * BLK), F32)
        for b in range(N_BUCKETS):
            acc = jnp.where(bk == b, relb_ref[b, h], acc)
        bias_ref[h] = acc


def _attn_fwd(qkv, buckets, rel_bias, sinks, xchg):
    S = qkv.shape[0]
    nb = S // BLK

    def body(q_ref, kvp_ref, kvc_ref, bk_ref, relb_ref, sinks_ref, y_ref, bias_ref):
        i = pl.program_id(0)

        @pl.when(i == 0)
        def _():
            _build_bias(bk_ref, relb_ref, bias_ref)

        ops = _attn_operands(q_ref, kvp_ref, kvc_ref, bias_ref, sinks_ref)
        outs = _attn_heads(*ops, mask=_attn_mask(i == 0))
        y_ref[...] = _join_heads(outs)

    smem = pl.BlockSpec(memory_space=pltpu.SMEM)
    return _hosted_call(
        body, "attn_fwd", nb,
        in_specs=[pl.BlockSpec((BLK, ATTN_W), _row),
                  pl.BlockSpec((BLK, 2 * KV_W), lambda i: (jnp.maximum(i - 1, 0), 2)),
                  pl.BlockSpec((BLK, 2 * KV_W), lambda i: (i, 2)),
                  pl.BlockSpec((BLK, 2 * BLK), _fixed), smem, smem],
        out_specs=[pl.BlockSpec((BLK, ATTN_W), _row)],
        out_shape=[jax.ShapeDtypeStruct((S, ATTN_W), F32)],
        scratch_shapes=[pltpu.VMEM((N_HEADS, BLK, 2 * BLK), F32)],
        args=(qkv, qkv, qkv, buckets, rel_bias, sinks), xchg=xchg, cparams=_cparams(),
    )


def _attn_bwd(qkv, dy, buckets, rel_bias, sinks, xchg):
    S = qkv.shape[0]
    nb = S // BLK

    def body(q_ref, kvp_ref, kvc_ref, dy_ref, bk_ref, relb_ref, sinks_ref,
             dq_ref, dkv_ref, dsink_ref, drel_ref, bias_ref, dbias_ref, carry_ref):
        i = pl.program_id(0)
        blk = nb - 1 - i

        @pl.when(i == 0)
        def _():
            _build_bias(bk_ref, relb_ref, bias_ref)
            dbias_ref[...] = jnp.zeros_like(dbias_ref)
            carry_ref[...] = jnp.zeros_like(carry_ref)
            dsink_ref[...] = jnp.zeros_like(dsink_ref)
            drel_ref[...] = jnp.zeros_like(drel_ref)

        ops = _attn_operands(q_ref, kvp_ref, kvc_ref, bias_ref, sinks_ref)
        _, vjp = jax.vjp(functools.partial(_attn_heads, mask=_attn_mask(blk == 0)), *ops)
        dq_heads, dk_pads, dv_pads, dbias_heads, dsink_cols = vjp(tuple(_split_heads(dy_ref[...], 4)))
        dq_ref[...] = _join_heads(dq_heads)
        dk_prev = _join_pair(dk_pads[0][:BLK], dk_pads[1][:BLK])
        dk_cur = _join_pair(dk_pads[0][BLK:], dk_pads[1][BLK:])
        dv_prev = _join_pair(dv_pads[0][:BLK], dv_pads[1][:BLK])
        dv_cur = _join_pair(dv_pads[0][BLK:], dv_pads[1][BLK:])
        dkv_ref[...] = jnp.concatenate([dk_cur, dv_cur], axis=1) + carry_ref[...]
        carry_ref[...] = jnp.concatenate([dk_prev, dv_prev], axis=1)
        row = lax.broadcasted_iota(jnp.int32, (N_HEADS, LANE), 0)
        dsink = jnp.zeros((N_HEADS, LANE), F32)
        for h in range(N_HEADS):
            dbias_ref[h] += dbias_heads[h]
            dsink = dsink + jnp.where(row == h, jnp.sum(dsink_cols[h], axis=0, keepdims=True), 0.0)
        dsink_ref[...] += dsink

        @pl.when(i == nb - 1)
        def _():
            bk = bk_ref[...]
            r = lax.broadcasted_iota(jnp.int32, (N_BUCKETS, LANE), 0)
            l = lax.broadcasted_iota(jnp.int32, (N_BUCKETS, LANE), 1)
            res = jnp.zeros((N_BUCKETS, LANE), F32)
            for h in range(N_HEADS):
                db = dbias_ref[h]
                for b in range(N_BUCKETS):
                    v = jnp.sum(jnp.sum(jnp.where(bk == b, db, 0.0), axis=1, keepdims=True), axis=0, keepdims=True)
                    res = res + jnp.where((r == b) & (l == h), v, 0.0)
            drel_ref[...] = res

    smem = pl.BlockSpec(memory_space=pltpu.SMEM)
    rev = lambda i: (nb - 1 - i, 0)
    return _hosted_call(
        body, "attn_bwd", nb,
        in_specs=[pl.BlockSpec((BLK, ATTN_W), rev),
                  pl.BlockSpec((BLK, 2 * KV_W), lambda i: (jnp.maximum(nb - 2 - i, 0), 2)),
                  pl.BlockSpec((BLK, 2 * KV_W), lambda i: (nb - 1 - i, 2)),
                  pl.BlockSpec((BLK, ATTN_W), rev),
                  pl.BlockSpec((BLK, 2 * BLK), _fixed), smem, smem],
        out_specs=[pl.BlockSpec((BLK, ATTN_W), rev), pl.BlockSpec((BLK, 2 * KV_W), rev),
                   pl.BlockSpec((N_HEADS, LANE), _fixed), pl.BlockSpec((N_BUCKETS, LANE), _fixed)],
        out_shape=[jax.ShapeDtypeStruct((S, ATTN_W), F32), jax.ShapeDtypeStruct((S, 2 * KV_W), F32),
                   jax.ShapeDtypeStruct((N_HEADS, LANE), F32), jax.ShapeDtypeStruct((N_BUCKETS, LANE), F32)],
        scratch_shapes=[pltpu.VMEM((N_HEADS, BLK, 2 * BLK), F32), pltpu.VMEM((N_HEADS, BLK, 2 * BLK), F32),
                        pltpu.VMEM((BLK, 2 * KV_W), F32)],
        args=(qkv, qkv, qkv, dy, buckets, rel_bias, sinks), xchg=xchg, cparams=_cparams(),
    )


def _ssd_consts():
    r = lax.broadcasted_iota(jnp.int32, (BLK, BLK), 0)
    c = lax.broadcasted_iota(jnp.int32, (BLK, BLK), 1)
    causal = c <= r
    tri = causal.astype(F32)
    last = r == BLK - 1
    expand = [(r == h).astype(F32) for h in range(N_HEADS)]
    return causal, tri, last, expand


def _ssd_chunk(pre_heads, pre_b, pre_c, dt_raw, prev, dtb_row, alog_row, d_rows, consts):
    causal, tri, last, expand = consts
    xs = [_silu(t) for t in pre_heads]
    bg = [_silu(t) for t in pre_b]
    cg = [_silu(t) for t in pre_c]
    dt = _softplus(dt_raw + dtb_row)
    acs = _mm_hi(tri, dt * (-jnp.exp(alog_row)))
    cb = [_mm_nt(cg[g], bg[g]) for g in range(2)]
    ys, hs = [], []
    for h in range(N_HEADS):
        g = h // 4
        dt_b = _mm_hi(dt, expand[h])
        a_b = _mm_hi(acs, expand[h])
        lmat = jnp.exp(jnp.where(causal, a_b - a_b.T, -1e30))
        xdt = xs[h] * dt_b
        a_last = jnp.sum(jnp.where(last, a_b, 0.0), axis=0, keepdims=True)
        y = _mm(cb[g] * lmat, xdt) + _mm(cg[g], prev[h]) * jnp.exp(a_b) + d_rows[h] * xs[h]
        st = _mm_tn(bg[g], xdt * jnp.exp(a_last - a_b))
        ys.append(y)
        hs.append(prev[h] * jnp.exp(a_last) + st)
    return tuple(ys), tuple(hs)


def _conv_pre(ext_ref, halo, blk, cw_ref, cb_ref):
    ext_ref[0:8, :] = halo
    ext_ref[8:8 + BLK, :] = blk
    pre = cb_ref[...] + cw_ref[0:1, :] * ext_ref[pl.ds(5, BLK), :]
    for k in range(1, 4):
        pre = pre + cw_ref[k:k + 1, :] * ext_ref[pl.ds(5 + k, BLK), :]
    return pre


def _ssd_split(pre):
    heads = _split_heads(pre[:, :SSM_W], 4)
    pb = [pre[:, SSM_W + g * D_STATE:SSM_W + (g + 1) * D_STATE] for g in range(2)]
    pc = [pre[:, SSM_W + 2 * D_STATE + g * D_STATE:SSM_W + 2 * D_STATE + (g + 1) * D_STATE] for g in range(2)]
    return heads, pb, pc


def _ssd_fwd(xbc, dt_raw, conv_w, conv_b, dtb_row, alog_row, d_exp, xchg):
    S = xbc.shape[0]
    nc = S // BLK

    def body(xbc_ref, halo_ref, dt_ref, cw_ref, cb_ref, dtb_ref, alog_ref, d_ref, y_ref, prev_ref, ext_ref, state_ref):
        i = pl.program_id(0)

        @pl.when(i == 0)
        def _():
            state_ref[...] = jnp.zeros_like(state_ref)

        halo = halo_ref[...] * jnp.where(i > 0, 1.0, 0.0)
        pre = _conv_pre(ext_ref, halo, xbc_ref[...], cw_ref, cb_ref)
        heads, pb, pc = _ssd_split(pre)
        prev = [state_ref[h] for h in range(N_HEADS)]
        for h in range(N_HEADS):
            prev_ref[0, h] = prev[h]
        d_rows = [d_ref[h:h + 1, :] for h in range(N_HEADS)]
        ys, hs = _ssd_chunk(heads, pb, pc, dt_ref[...], prev, dtb_ref[...], alog_ref[...], d_rows, _ssd_consts())
        for h in range(N_HEADS):
            state_ref[h] = hs[h]
        y_ref[...] = _join_heads(ys)

    vec = pl.BlockSpec((1, LANE), _fixed)
    return _hosted_call(
        body, "ssd_fwd", nc,
        in_specs=[pl.BlockSpec((BLK, XBC_W), _row),
                  pl.BlockSpec((8, XBC_W), lambda i: (jnp.maximum(i * (BLK // 8) - 1, 0), 0)),
                  pl.BlockSpec((BLK, LANE), _row),
                  pl.BlockSpec((4, XBC_W), _fixed), pl.BlockSpec((1, XBC_W), _fixed), vec, vec,
                  pl.BlockSpec((N_HEADS, LANE), _fixed)],
        out_specs=[pl.BlockSpec((BLK, SSM_W), _row),
                   pl.BlockSpec((1, N_HEADS, D_STATE, LANE), lambda i: (i, 0, 0, 0))],
        out_shape=[jax.ShapeDtypeStruct((S, SSM_W), F32), jax.ShapeDtypeStruct((nc, N_HEADS, D_STATE, LANE), F32)],
        scratch_shapes=[pltpu.VMEM((8 + BLK, XBC_W), F32), pltpu.VMEM((N_HEADS, D_STATE, LANE), F32)],
        args=(xbc, xbc, dt_raw, conv_w, conv_b, dtb_row, alog_row, d_exp), xchg=xchg, cparams=_cparams(),
    )


def _ssd_bwd(xbc, dt_raw, prev_states, dy, conv_w, conv_b, dtb_row, alog_row, d_exp, xchg):
    S = xbc.shape[0]
    nc = S // BLK

    def body(xbc_ref, halo_ref, dt_ref, prev_ref, dy_ref, cw_ref, cb_ref, dtb_ref, alog_ref, d_ref,
             dxbc_ref, ddt_ref, dcw_ref, dvec_ref, dd_ref, ext_ref, dpe_ref, gstate_ref, ghalo_ref):
        i = pl.program_id(0)
        c = nc - 1 - i

        @pl.when(i == 0)
        def _():
            gstate_ref[...] = jnp.zeros_like(gstate_ref)
            ghalo_ref[...] = jnp.zeros_like(ghalo_ref)
            dcw_ref[...] = jnp.zeros_like(dcw_ref)
            dvec_ref[...] = jnp.zeros_like(dvec_ref)
            dd_ref[...] = jnp.zeros_like(dd_ref)
            dpe_ref[...] = jnp.zeros_like(dpe_ref)

        halo = halo_ref[...] * jnp.where(c > 0, 1.0, 0.0)
        pre = _conv_pre(ext_ref, halo, xbc_ref[...], cw_ref, cb_ref)
        heads, pb, pc = _ssd_split(pre)
        prev = [prev_ref[0, h] for h in range(N_HEADS)]
        d_rows = [d_ref[h:h + 1, :] for h in range(N_HEADS)]
        _, vjp = jax.vjp(functools.partial(_ssd_chunk, consts=_ssd_consts()),
                         heads, pb, pc, dt_ref[...], prev, dtb_ref[...], alog_ref[...], d_rows)
        dys = tuple(_split_heads(dy_ref[...], 4))
        dhs = tuple(gstate_ref[h] for h in range(N_HEADS))
        dheads, dpb, dpc, ddt, dprev, ddtb, dalog, dd_rows = vjp((dys, dhs))
        for h in range(N_HEADS):
            gstate_ref[h] = dprev[h]
            dd_ref[h:h + 1, :] += dd_rows[h]
        ddt_ref[...] = ddt
        dvec_ref[0:1, :] += ddtb
        dvec_ref[1:2, :] += dalog
        dpre = jnp.concatenate([_join_heads(dheads)] + list(dpb) + list(dpc), axis=1)
        dpe_ref[8:8 + BLK, :] = dpre
        dext = cw_ref[0:1, :] * dpe_ref[pl.ds(3, 8 + BLK), :]
        dcw_ref[0:1, :] += jnp.sum(dpre * ext_ref[pl.ds(5, BLK), :], axis=0, keepdims=True)
        for k in range(1, 4):
            dext = dext + cw_ref[k:k + 1, :] * dpe_ref[pl.ds(3 - k, 8 + BLK), :]
            dcw_ref[k:k + 1, :] += jnp.sum(dpre * ext_ref[pl.ds(5 + k, BLK), :], axis=0, keepdims=True)
        dcw_ref[4:5, :] += jnp.sum(dpre, axis=0, keepdims=True)
        dxbc_ref[...] = dext[8:, :]
        dxbc_ref[BLK - 8:BLK, :] += ghalo_ref[...]
        ghalo_ref[...] = dext[:8, :]

    vec = pl.BlockSpec((1, LANE), _fixed)
    rev = lambda i: (nc - 1 - i, 0)
    return _hosted_call(
        body, "ssd_bwd", nc,
        in_specs=[pl.BlockSpec((BLK, XBC_W), rev),
                  pl.BlockSpec((8, XBC_W), lambda i: (jnp.maximum((nc - 1 - i) * (BLK // 8) - 1, 0), 0)),
                  pl.BlockSpec((BLK, LANE), rev),
                  pl.BlockSpec((1, N_HEADS, D_STATE, LANE), lambda i: (nc - 1 - i, 0, 0, 0)),
                  pl.BlockSpec((BLK, SSM_W), rev),
                  pl.BlockSpec((4, XBC_W), _fixed), pl.BlockSpec((1, XBC_W), _fixed), vec, vec,
                  pl.BlockSpec((N_HEADS, LANE), _fixed)],
        out_specs=[pl.BlockSpec((BLK, XBC_W), rev), pl.BlockSpec((BLK, LANE), rev),
                   pl.BlockSpec((8, XBC_W), _fixed), pl.BlockSpec((8, LANE), _fixed),
                   pl.BlockSpec((N_HEADS, LANE), _fixed)],
        out_shape=[jax.ShapeDtypeStruct((S, XBC_W), F32), jax.ShapeDtypeStruct((S, LANE), F32),
                   jax.ShapeDtypeStruct((8, XBC_W), F32), jax.ShapeDtypeStruct((8, LANE), F32),
                   jax.ShapeDtypeStruct((N_HEADS, LANE), F32)],
        scratch_shapes=[pltpu.VMEM((8 + BLK, XBC_W), F32), pltpu.VMEM((16 + BLK, XBC_W), F32),
                        pltpu.VMEM((N_HEADS, D_STATE, LANE), F32), pltpu.VMEM((8, XBC_W), F32)],
        args=(xbc, xbc, dt_raw, prev_states, dy, conv_w, conv_b, dtb_row, alog_row, d_exp), xchg=xchg,
        cparams=_cparams(VMEM_BIG),
    )


def _adamw_math(w, g, m, v):
    m = ADAM_B1 * m + (1.0 - ADAM_B1) * g
    v = ADAM_B2 * v + (1.0 - ADAM_B2) * jnp.square(g)
    m_hat = m / (1.0 - ADAM_B1 ** ADAM_STEP)
    v_hat = v / (1.0 - ADAM_B2 ** ADAM_STEP)
    delta = -ADAM_LR * (m_hat / (jnp.sqrt(v_hat) + ADAM_EPS) + ADAM_WD * w)
    return delta, m, v


def _reduce_adamw(parts, w, m, v, name):
    R, C = w.shape
    tr = max(t for t in range(16, 257, 16) if R % t == 0)

    def body(p_ref, w_ref, m_ref, v_ref, g_ref, d_ref, nm_ref, nv_ref):
        g = p_ref[0].astype(F32)
        for i in range(1, N_DEV):
            g = g + p_ref[i].astype(F32)
        d, nm, nv = _adamw_math(w_ref[...], g, m_ref[...], v_ref[...])
        g_ref[...] = g
        d_ref[...] = d
        nm_ref[...] = nm
        nv_ref[...] = nv

    blk = pl.BlockSpec((tr, C), _row)
    return pl.pallas_call(
        body, name=name, grid=(R // tr,),
        in_specs=[pl.BlockSpec((N_DEV, tr, C), lambda i: (0, i, 0)), blk, blk, blk],
        out_specs=[blk] * 4, out_shape=[jax.ShapeDtypeStruct((R, C), F32)] * 4,
    )(parts, w, m, v)


def _adamw_small(parts, w, m, v, name):
    summed = parts.ndim == 3

    def body(p_ref, w_ref, m_ref, v_ref, g_ref, d_ref, nm_ref, nv_ref):
        if summed:
            g = p_ref[0]
            for i in range(1, N_DEV):
                g = g + p_ref[i]
        else:
            g = p_ref[...]
        d, nm, nv = _adamw_math(w_ref[...], g, m_ref[...], v_ref[...])
        g_ref[...] = g
        d_ref[...] = d
        nm_ref[...] = nm
        nv_ref[...] = nv

    return pl.pallas_call(body, name=name, out_shape=[jax.ShapeDtypeStruct(w.shape, F32)] * 4)(parts, w, m, v)


def _ada_w_update(c_all, dmod_mine, w, m, v):
    def body(c_ref, dm_ref, w_ref, m_ref, v_ref, g_ref, d_ref, nm_ref, nv_ref):
        g = lax.dot_general(_silu(c_ref[...]), dm_ref[...], (((0,), (0,)), ((), ())),
                            precision=HI, preferred_element_type=F32)
        d, nm, nv = _adamw_math(w_ref[...], g, m_ref[...], v_ref[...])
        g_ref[...] = g
        d_ref[...] = d
        nm_ref[...] = nm
        nv_ref[...] = nv

    return pl.pallas_call(body, name="ada_w_update", out_shape=[jax.ShapeDtypeStruct(w.shape, F32)] * 4,
                          compiler_params=_cparams(VMEM_BIG))(c_all, dmod_mine, w, m, v)


_SMALL = (("ada_b", 6144), ("norm1", 1024), ("conv_b", 1024), ("dt_bias", 8), ("A_log", 8), ("D_skip", 8),
          ("sinks", 8), ("attn_out_norm", 512), ("ssm_out_norm", 512), ("norm2", 1024), ("rel_bias", 256),
          ("final_norm", 1024))


def _pack_rows(pieces):
    rows = []
    for p in pieces:
        flat = p.reshape(-1).astype(F32)
        n = -(-flat.shape[0] // 1024) * 1024
        rows.append(jnp.pad(flat, (0, n - flat.shape[0])).reshape(n // LANE, LANE))
    return jnp.concatenate(rows, axis=0)


def _unpack_rows(packed, sizes):
    out, r = [], 0
    for n in sizes:
        nr = -(-n // 1024) * 8
        out.append(packed[r:r + nr].reshape(-1)[:n])
        r += nr
    return out


def _col_shards(g, n):
    return jnp.transpose(g.reshape(g.shape[0], N_DEV, n), (1, 0, 2)).astype(WIRE_DTYPE)


def _local_step(x, tgt, mod, w_in, conv_w, w_o_mine, w_gu_mine, w_d_mine, p):
    S = x.shape[0]
    tm = min(512, S)
    tmm = min(256, S)
    shift1, scale1, gate1, shift2, scale2, gate2 = [mod[i:i + 1] for i in range(6)]
    buckets = jnp.asarray(_t5_bucket_table())
    dtb_row = jnp.pad(p["dt_bias"], ((0, 0), (0, LANE - N_HEADS)))
    alog_row = jnp.pad(p["A_log"], ((0, 0), (0, LANE - N_HEADS)))
    d_exp = jnp.broadcast_to(p["D_skip"].reshape(N_HEADS, 1), (N_HEADS, LANE))
    sinks = p["sinks"].reshape(N_HEADS)

    qkv, z, xbc, dt_raw = _in_proj_fwd(x, p["norm1"], scale1, shift1, w_in, tm)
    (ya,), (g_d,) = _attn_fwd(qkv, buckets, p["rel_bias"], sinks, ([w_d_mine], False))
    (ys, prev_states), (g_gu, g_o) = _ssd_fwd(xbc, dt_raw, conv_w, p["conv_b"], dtb_row, alog_row, d_exp,
                                              ([w_gu_mine, w_o_mine], False))
    w_gu = jnp.transpose(g_gu, (1, 0, 2)).reshape(D_MODEL, 2 * D_FF)
    w_o = g_o.reshape(D_MODEL, D_MODEL)
    w_d = g_d.reshape(D_FF, D_MODEL)
    x1 = _out_proj_fwd(x, ya, ys, z, p["attn_out_norm"], p["ssm_out_norm"], gate1, w_o, tm)
    dx1, h2, dgu, act, dmlp, acc2 = _mlp_loss(x1, tgt, p["norm2"], scale2, shift2, gate2, p["final_norm"],
                                              w_gu, w_d, tmm)
    g_w_gu = _wgrad(h2, dgu, 512, tm, "wgrad_gate_up")
    g_w_d = _wgrad(act, dmlp, 512, tm, "wgrad_down")
    dya, dys, dz, u, dmix, acc1 = _out_proj_bwd(dx1, ya, ys, z, p["attn_out_norm"], p["ssm_out_norm"], gate1, w_o, tm)
    g_w_o = _wgrad(u, dmix, 512, tm, "wgrad_out")
    (dq, dkv, dsink, drel), (r_gu,) = _attn_bwd(qkv, dya, buckets, p["rel_bias"], sinks,
                                                ([_col_shards(g_w_gu, 2 * D_FF // N_DEV)], True))
    (dxbc, ddt, dcw, dvec, dd), (r_d, r_o) = _ssd_bwd(
        xbc, dt_raw, prev_states, dys, conv_w, p["conv_b"], dtb_row, alog_row, d_exp,
        ([g_w_d.reshape(N_DEV, D_FF // N_DEV, D_MODEL).astype(WIRE_DTYPE),
          g_w_o.reshape(N_DEV, D_MODEL // N_DEV, D_MODEL).astype(WIRE_DTYPE)], True))
    gx, h1, dproj, acc0 = _in_proj_bwd(x, dx1, dq, dkv, dz, dxbc, ddt, p["norm1"], scale1, shift1, w_in, tm)
    g_w_in = _wgrad(h1, dproj, IN_PAD, tm, "wgrad_in")

    dmod = jnp.concatenate([acc0[2:3], acc0[1:2], acc1[0:1], acc2[2:3], acc2[1:2], acc2[3:4]], axis=0)
    small = {
        "dmod": dmod, "norm1": acc0[0], "conv_b": dcw[4], "dt_bias": dvec[0, :N_HEADS], "A_log": dvec[1, :N_HEADS],
        "D_skip": jnp.sum(dd, axis=1), "sinks": dsink[:, 0], "attn_out_norm": acc1[1, :ATTN_W],
        "ssm_out_norm": acc1[1, ATTN_W:], "norm2": acc2[0], "rel_bias": drel[:, :N_HEADS], "final_norm": acc2[4],
        "conv_w": dcw[:4], "loss": acc2[5, :1],
    }
    return gx, g_w_in, (r_o, r_gu, r_d), small


def kernel(x, c, ada_w, ada_b, norm1, w_in, conv_w, conv_b, dt_bias, A_log, D_skip, sinks, attn_out_norm, ssm_out_norm, w_o, norm2, w_gate_up, w_down, rel_bias, final_norm, loss_target, m_ada_w, m_ada_b, m_norm1, m_w_in, m_conv_w, m_conv_b, m_dt_bias, m_A_log, m_D_skip, m_sinks, m_attn_out_norm, m_ssm_out_norm, m_w_o, m_norm2, m_w_gate_up, m_w_down, m_rel_bias, m_final_norm, v_ada_w, v_ada_b, v_norm1, v_w_in, v_conv_w, v_conv_b, v_dt_bias, v_A_log, v_D_skip, v_sinks, v_attn_out_norm, v_ssm_out_norm, v_w_o, v_norm2, v_w_gate_up, v_w_down, v_rel_bias, v_final_norm):
    given = dict(ada_b=ada_b, norm1=norm1, conv_b=conv_b, dt_bias=dt_bias, A_log=A_log, D_skip=D_skip, sinks=sinks,
                 attn_out_norm=attn_out_norm, ssm_out_norm=ssm_out_norm, norm2=norm2, rel_bias=rel_bias,
                 final_norm=final_norm)
    moments_m = dict(ada_b=m_ada_b, norm1=m_norm1, conv_b=m_conv_b, dt_bias=m_dt_bias, A_log=m_A_log, D_skip=m_D_skip,
                     sinks=m_sinks, attn_out_norm=m_attn_out_norm, ssm_out_norm=m_ssm_out_norm, norm2=m_norm2,
                     rel_bias=m_rel_bias, final_norm=m_final_norm)
    moments_v = dict(ada_b=v_ada_b, norm1=v_norm1, conv_b=v_conv_b, dt_bias=v_dt_bias, A_log=v_A_log, D_skip=v_D_skip,
                     sinks=v_sinks, attn_out_norm=v_attn_out_norm, ssm_out_norm=v_ssm_out_norm, norm2=v_norm2,
                     rel_bias=v_rel_bias, final_norm=v_final_norm)
    me = _lin(_my_pos())
    S = x.shape[1]
    xs, tgt = x.reshape(S, D_MODEL), loss_target.reshape(S, D_MODEL)
    ada_w2 = ada_w[0]
    chunk = ada_w2.shape[1]

    mod = _mod_exchange(c, ada_w2, ada_b.reshape(N_DEV, chunk)).reshape(6, D_MODEL)

    g_in, g_cw = _exchange([w_in[0].astype(WIRE_DTYPE), conv_w[0]], scatter=False, name="gather_w_in")
    w_in_full = jnp.transpose(g_in, (1, 0, 2)).reshape(D_MODEL, IN_W)
    w_in_full = jnp.pad(w_in_full, ((0, 0), (0, IN_PAD - IN_W)))
    conv_w_full = jnp.transpose(g_cw, (1, 0, 2)).reshape(4, XBC_W)

    p = {k: (v if k == "rel_bias" else v.reshape(1, -1)) for k, v in given.items()}
    gx, gw_in, (r_o, r_gu, r_d), small = _local_step(
        xs, tgt, mod, w_in_full, conv_w_full, w_o[0].astype(WIRE_DTYPE), w_gate_up[0].astype(WIRE_DTYPE),
        w_down[0].astype(WIRE_DTYPE), p)

    (r_in,) = _exchange([_col_shards(gw_in[:, :IN_W], IN_W // N_DEV)], scatter=True, name="scatter_w_in")

    names = [n for n, _ in _SMALL]
    sizes = [s for _, s in _SMALL]
    packed = _pack_rows([small["dmod"]] + [small[n] for n in names[1:]] + [small["conv_w"], small["loss"], c])
    (gathered,) = _exchange([packed], scatter=False, name="gather_small")
    n_rows = packed.shape[0]
    pw = _pack_rows([given[n] for n in names] + [jnp.zeros((4 * XBC_W + 1024 + 1024,), F32)])
    pm = _pack_rows([moments_m[n] for n in names] + [jnp.zeros((4 * XBC_W + 1024 + 1024,), F32)])
    pv = _pack_rows([moments_v[n] for n in names] + [jnp.ones((4 * XBC_W + 1024 + 1024,), F32)])
    sg, sd, sm, sv = _adamw_small(gathered, pw, pm, pv, "adamw_small")
    all_sizes = sizes + [4 * XBC_W, 1, 1024]
    g_small = _unpack_rows(sg, all_sizes)
    d_small = _unpack_rows(sd, all_sizes)
    m_small = _unpack_rows(sm, all_sizes)
    v_small = _unpack_rows(sv, all_sizes)
    loss = g_small[len(names) + 1][0]
    conv_w_grad_full = g_small[len(names)].reshape(4, XBC_W)

    offs = np.cumsum([0] + [-(-s // 1024) * 8 for s in all_sizes])
    c_all = gathered[:, offs[len(names) + 2]:offs[len(names) + 2] + 8, :].reshape(N_DEV, D_MODEL)
    dmod_all = gathered[:, :48, :].reshape(N_DEV, 6 * D_MODEL)
    dmod_mine = lax.dynamic_slice(dmod_all, (0, me * chunk), (N_DEV, chunk))
    ada = _ada_w_update(c_all, dmod_mine, ada_w2, m_ada_w[0], v_ada_w[0])

    big = {
        "ada_w": ada,
        "w_in": _reduce_adamw(r_in, w_in[0], m_w_in[0], v_w_in[0], "adamw_w_in"),
        "w_o": _reduce_adamw(r_o, w_o[0], m_w_o[0], v_w_o[0], "adamw_w_o"),
        "w_gate_up": _reduce_adamw(r_gu, w_gate_up[0], m_w_gate_up[0], v_w_gate_up[0], "adamw_w_gate_up"),
        "w_down": _reduce_adamw(r_d, w_down[0], m_w_down[0], v_w_down[0], "adamw_w_down"),
    }
    n_cw = XBC_W // N_DEV
    cw_grad_mine = lax.dynamic_slice(conv_w_grad_full, (0, me * n_cw), (4, n_cw))
    big["conv_w"] = _adamw_small(cw_grad_mine, conv_w[0], m_conv_w[0], v_conv_w[0], "adamw_conv_w")

    order = ['ada_w', 'ada_b', 'norm1', 'w_in', 'conv_w', 'conv_b', 'dt_bias', 'A_log', 'D_skip', 'sinks',
             'attn_out_norm', 'ssm_out_norm', 'w_o', 'norm2', 'w_gate_up', 'w_down', 'rel_bias', 'final_norm']
    shapes = dict(ada_w=ada_w.shape, ada_b=ada_b.shape, norm1=norm1.shape, w_in=w_in.shape, conv_w=conv_w.shape,
                  conv_b=conv_b.shape, dt_bias=dt_bias.shape, A_log=A_log.shape, D_skip=D_skip.shape,
                  sinks=sinks.shape, attn_out_norm=attn_out_norm.shape, ssm_out_norm=ssm_out_norm.shape,
                  w_o=w_o.shape, norm2=norm2.shape, w_gate_up=w_gate_up.shape, w_down=w_down.shape,
                  rel_bias=rel_bias.shape, final_norm=final_norm.shape)
    outs = [[], [], [], []]
    for name in order:
        for kind in range(4):
            if name in big:
                val = big[name][kind]
            else:
                val = (g_small, d_small, m_small, v_small)[kind][names.index(name)]
            outs[kind].append(val.reshape(shapes[name]))
    return (loss, gx.reshape(x.shape), *outs[0], *outs[1], *outs[2], *outs[3])
```

```python
import functools

import numpy as np
import jax
import jax.numpy as jnp
from jax import lax
from jax.experimental import pallas as pl
from jax.experimental.pallas import tpu as pltpu

F32 = jnp.float32
MXU_DTYPE = jnp.bfloat16
WIRE_DTYPE = jnp.bfloat16
HI = lax.Precision.HIGHEST
MESH = pl.DeviceIdType.MESH
N_DEV = 8

D_MODEL = 1024
ATTN_W = 512
KV_W = 128
SSM_W = 512
XBC_W = 1024
N_HEADS = 8
D_STATE = 128
D_FF = 2816
IN_W = 2312
IN_PAD = 2432
BLK = 128
N_BUCKETS = 32
EPS = 1e-6
LANE = 128
HALF = 64

ADAM_LR, ADAM_B1, ADAM_B2, ADAM_EPS, ADAM_WD, ADAM_STEP = 0.001, 0.9, 0.999, 1e-08, 0.01, 10

VMEM_BIG = 56 * 1024 * 1024


def _cparams(vmem=None):
    if vmem is None:
        return pltpu.CompilerParams()
    return pltpu.CompilerParams(vmem_limit_bytes=vmem)


def _mm(a, b):
    return jnp.dot(a.astype(MXU_DTYPE), b.astype(MXU_DTYPE), preferred_element_type=F32)


def _mm_nt(a, b):
    return lax.dot_general(a.astype(MXU_DTYPE), b.astype(MXU_DTYPE), (((1,), (1,)), ((), ())),
                           preferred_element_type=F32)


def _mm_tn(a, b):
    return lax.dot_general(a.astype(MXU_DTYPE), b.astype(MXU_DTYPE), (((0,), (0,)), ((), ())),
                           preferred_element_type=F32)


def _mm_hi(a, b):
    return jnp.dot(a, b, precision=HI, preferred_element_type=F32)


def _silu(x):
    return x * jax.nn.sigmoid(x)


def _softplus(x):
    return jnp.maximum(x, 0.0) + jnp.log1p(jnp.exp(-jnp.abs(x)))


def _rms(x, g, n):
    return x * lax.rsqrt(jnp.sum(x * x, axis=-1, keepdims=True) * (1.0 / n) + EPS) * g


def _modnorm(x, g, scale, shift):
    return _rms(x, g, x.shape[-1]) * (1.0 + scale) + shift


def _lane_iota(shape):
    return lax.broadcasted_iota(jnp.int32, shape, len(shape) - 1)


def _split_pair(t):
    lane = _lane_iota(t.shape)
    lo = jnp.where(lane < HALF, t, 0.0)
    hi = pltpu.roll(jnp.where(lane >= HALF, t, 0.0), HALF, 1)
    return lo, hi


def _join_pair(lo, hi):
    lane = _lane_iota(lo.shape)
    return jnp.where(lane < HALF, lo, pltpu.roll(hi, HALF, 1))


def _split_heads(t, n_pairs):
    out = []
    for p in range(n_pairs):
        out.extend(_split_pair(t[:, p * LANE:(p + 1) * LANE]))
    return out


def _join_heads(hs):
    return jnp.concatenate([_join_pair(hs[2 * p], hs[2 * p + 1]) for p in range(len(hs) // 2)], axis=1)


def _t5_bucket_table():
    dist = np.arange(BLK)[:, None] + BLK - np.arange(2 * BLK)[None, :]
    n = np.maximum(dist, 0)
    max_exact = N_BUCKETS // 2
    large = max_exact + (np.log(np.maximum(n, 1) / max_exact) / np.log(128 / max_exact)
                         * (N_BUCKETS - max_exact)).astype(np.int32)
    large = np.minimum(large, N_BUCKETS - 1)
    return np.where(n < max_exact, n, large).astype(np.int32)


def _my_pos():
    return lax.axis_index("x"), lax.axis_index("y"), lax.axis_index("c")


def _peer(k):
    x, y, c = _my_pos()
    return (1 - x if k & 4 else x, 1 - y if k & 2 else y, 1 - c if k & 1 else c)


def _lin(pos):
    return 4 * pos[0] + 2 * pos[1] + pos[2]


def _xchg_copies(ins, outs, sems, scatter):
    local_sem, send_sem, recv_sem = sems
    me = _lin(_my_pos())
    local, remote = [], []
    for a in range(len(ins)):
        src = ins[a].at[me] if scatter else ins[a]
        local.append(pltpu.make_async_copy(src, outs[a].at[me], local_sem.at[a]))
    for k in range(1, N_DEV):
        peer = _peer(k)
        for a in range(len(ins)):
            src = ins[a].at[_lin(peer)] if scatter else ins[a]
            remote.append(pltpu.make_async_remote_copy(src, outs[a].at[me], send_sem.at[a, k - 1],
                                                       recv_sem.at[a, k - 1], device_id=peer, device_id_type=MESH))
    return local, remote


def _xchg_start(ins, outs, sems, scatter):
    local, remote = _xchg_copies(ins, outs, sems, scatter)
    for cp in local + remote:
        cp.start()


def _xchg_wait(ins, outs, sems, scatter):
    local, remote = _xchg_copies(ins, outs, sems, scatter)
    for cp in local:
        cp.wait()
    for cp in remote:
        cp.wait_send()
        cp.wait_recv()


def _xchg_shapes(arrs, scatter):
    n = len(arrs)
    if scatter:
        out_shape = [jax.ShapeDtypeStruct(a.shape, a.dtype) for a in arrs]
    else:
        out_shape = [jax.ShapeDtypeStruct((N_DEV,) + a.shape, a.dtype) for a in arrs]
    sems = [pltpu.SemaphoreType.DMA((n,)), pltpu.SemaphoreType.DMA((n, N_DEV - 1)),
            pltpu.SemaphoreType.DMA((n, N_DEV - 1))]
    return out_shape, sems


def _exchange(arrs, scatter, name):
    n = len(arrs)
    out_shape, sems = _xchg_shapes(arrs, scatter)

    def body(*refs):
        ins, outs, s = refs[:n], refs[n:2 * n], refs[2 * n:]
        _xchg_start(ins, outs, s, scatter)
        _xchg_wait(ins, outs, s, scatter)

    hbm = pl.BlockSpec(memory_space=pltpu.HBM)
    return pl.pallas_call(body, name=name, out_shape=out_shape, in_specs=[hbm] * n, out_specs=[hbm] * n,
                          scratch_shapes=sems)(*arrs)


def _hosted_call(body, name, n_steps, in_specs, out_specs, out_shape, scratch_shapes, args, xchg, cparams):
    arrs, scatter = xchg
    n, n_in, n_out, n_scr = len(arrs), len(in_specs), len(out_specs), len(scratch_shapes)
    x_shape, x_sems = _xchg_shapes(arrs, scatter)

    def hosted(*refs):
        ins, refs = refs[:n_in], refs[n_in:]
        x_in, refs = refs[:n], refs[n:]
        outs, refs = refs[:n_out], refs[n_out:]
        x_out, refs = refs[:n], refs[n:]
        scr, sems = refs[:n_scr], refs[n_scr:]

        @pl.when(pl.program_id(0) == 0)
        def _():
            _xchg_start(x_in, x_out, sems, scatter)

        body(*ins, *outs, *scr)

        @pl.when(pl.program_id(0) == n_steps - 1)
        def _():
            _xchg_wait(x_in, x_out, sems, scatter)

    hbm = pl.BlockSpec(memory_space=pltpu.HBM)
    res = pl.pallas_call(
        hosted, name=name, grid=(n_steps,), in_specs=list(in_specs) + [hbm] * n,
        out_specs=list(out_specs) + [hbm] * n, out_shape=list(out_shape) + x_shape,
        scratch_shapes=list(scratch_shapes) + x_sems, compiler_params=cparams,
    )(*args, *arrs)
    return res[:n_out], res[n_out:]


def _mod_exchange(c, ada_w, ada_b8):
    chunk = ada_w.shape[1]

    def body(c_ref, w_ref, b_ref, out_ref, cbuf, part, s1, r1, s2, r2):
        me = _lin(_my_pos())
        first = []
        for k in range(1, N_DEV):
            cp = pltpu.make_async_remote_copy(c_ref, cbuf.at[me], s1.at[k - 1], r1.at[k - 1],
                                              device_id=_peer(k), device_id_type=MESH)
            cp.start()
            first.append(cp)
        cbuf[me] = c_ref[...]
        for cp in first:
            cp.wait_send()
            cp.wait_recv()
        cond = _silu(jnp.concatenate([cbuf[i] for i in range(N_DEV)], axis=0))
        mod = _mm_hi(cond, w_ref[...]) + b_ref[pl.ds(me, 1), :]
        for j in range(N_DEV):
            part[j] = mod[j:j + 1, :]
        second = []
        for k in range(1, N_DEV):
            peer = _peer(k)
            cp = pltpu.make_async_remote_copy(part.at[_lin(peer)], out_ref.at[me], s2.at[k - 1], r2.at[k - 1],
                                              device_id=peer, device_id_type=MESH)
            cp.start()
            second.append(cp)
        out_ref[me] = part[me]
        for cp in second:
            cp.wait_send()
            cp.wait_recv()

    vm = pl.BlockSpec(memory_space=pltpu.VMEM)
    return pl.pallas_call(
        body, name="mod_exchange", out_shape=jax.ShapeDtypeStruct((N_DEV, 1, chunk), F32),
        in_specs=[vm, vm, vm], out_specs=vm,
        scratch_shapes=[pltpu.VMEM((N_DEV, 1, D_MODEL), F32), pltpu.VMEM((N_DEV, 1, chunk), F32)]
        + [pltpu.SemaphoreType.DMA((N_DEV - 1,))] * 4,
    )(c, ada_w, ada_b8)


def _row(i):
    return (i, 0)


def _fixed(i):
    return (0, 0)


def _in_proj_fwd(x, norm1, scale1, shift1, w_in, tm):
    S = x.shape[0]

    def body(x_ref, n_ref, sc_ref, sh_ref, w_ref, qkv_ref, z_ref, xbc_ref, dt_ref):
        h = _modnorm(x_ref[...], n_ref[...], sc_ref[...], sh_ref[...])
        p = _mm_nt(h, w_ref[...])
        qkv_ref[...] = p[:, :768].astype(qkv_ref.dtype)
        z_ref[...] = p[:, 768:1280]
        xbc_ref[...] = p[:, 1280:2304]
        dt_ref[...] = p[:, 2304:IN_PAD]

    vec = pl.BlockSpec((1, D_MODEL), _fixed)
    return pl.pallas_call(
        body, name="in_proj_fwd", grid=(S // tm,),
        in_specs=[pl.BlockSpec((tm, D_MODEL), _row), vec, vec, vec, pl.BlockSpec((IN_PAD, D_MODEL), _fixed)],
        out_specs=[pl.BlockSpec((tm, 768), _row), pl.BlockSpec((tm, SSM_W), _row),
                   pl.BlockSpec((tm, XBC_W), _row), pl.BlockSpec((tm, LANE), _row)],
        out_shape=[jax.ShapeDtypeStruct((S, 768), MXU_DTYPE), jax.ShapeDtypeStruct((S, SSM_W), F32),
                   jax.ShapeDtypeStruct((S, XBC_W), F32), jax.ShapeDtypeStruct((S, LANE), F32)],
        compiler_params=_cparams(VMEM_BIG),
    )(x, norm1, scale1, shift1, w_in)


def _in_proj_bwd(x, dx1, dq, dkv, dz, dxbc, ddt, norm1, scale1, shift1, w_in, tm):
    S = x.shape[0]

    def body(x_ref, dx1_ref, dq_ref, dkv_ref, dz_ref, dxbc_ref, ddt_ref, n_ref, sc_ref, sh_ref, w_ref,
             gx_ref, h_ref, dp_ref, acc_ref):
        @pl.when(pl.program_id(0) == 0)
        def _():
            acc_ref[...] = jnp.zeros_like(acc_ref)

        h, vjp = jax.vjp(_modnorm, x_ref[...], n_ref[...], sc_ref[...], sh_ref[...])
        dp = jnp.concatenate([dq_ref[...].astype(MXU_DTYPE), dkv_ref[...].astype(MXU_DTYPE),
                              dz_ref[...].astype(MXU_DTYPE), dxbc_ref[...].astype(MXU_DTYPE),
                              ddt_ref[...].astype(MXU_DTYPE)], axis=1)
        dh = _mm(dp, w_ref[...])
        dx, dn, dsc, dsh = vjp(dh)
        gx_ref[...] = dx1_ref[...] + dx
        h_ref[...] = h.astype(h_ref.dtype)
        dp_ref[...] = dp
        acc_ref[0:1, :] += dn
        acc_ref[1:2, :] += dsc
        acc_ref[2:3, :] += dsh

    vec = pl.BlockSpec((1, D_MODEL), _fixed)
    return pl.pallas_call(
        body, name="in_proj_bwd", grid=(S // tm,),
        in_specs=[pl.BlockSpec((tm, D_MODEL), _row), pl.BlockSpec((tm, D_MODEL), _row),
                  pl.BlockSpec((tm, ATTN_W), _row), pl.BlockSpec((tm, 2 * KV_W), _row),
                  pl.BlockSpec((tm, SSM_W), _row), pl.BlockSpec((tm, XBC_W), _row), pl.BlockSpec((tm, LANE), _row),
                  vec, vec, vec, pl.BlockSpec((IN_PAD, D_MODEL), _fixed)],
        out_specs=[pl.BlockSpec((tm, D_MODEL), _row), pl.BlockSpec((tm, D_MODEL), _row),
                   pl.BlockSpec((tm, IN_PAD), _row), pl.BlockSpec((8, D_MODEL), _fixed)],
        out_shape=[jax.ShapeDtypeStruct((S, D_MODEL), F32), jax.ShapeDtypeStruct((S, D_MODEL), MXU_DTYPE),
                   jax.ShapeDtypeStruct((S, IN_PAD), MXU_DTYPE), jax.ShapeDtypeStruct((8, D_MODEL), F32)],
        compiler_params=_cparams(VMEM_BIG),
    )(x, dx1, dq, dkv, dz, dxbc, ddt, norm1, scale1, shift1, w_in)


def _out_stage(ya, ys0, ys1, z0, z1, an, sn0, sn1):
    half = SSM_W // 2
    a = _rms(ya, an, ATTN_W)
    g0 = _rms(ys0 * _silu(z0), sn0, half)
    g1 = _rms(ys1 * _silu(z1), sn1, half)
    return jnp.concatenate([a, g0, g1], axis=1)


def _out_stage_args(ya_ref, ys_ref, z_ref, an_ref, sn_ref):
    half = SSM_W // 2
    return (ya_ref[...], ys_ref[:, :half], ys_ref[:, half:], z_ref[:, :half], z_ref[:, half:],
            an_ref[...], sn_ref[:, :half], sn_ref[:, half:])


def _out_proj_fwd(x, ya, ys, z, an, sn, gate1, w_o, tm):
    S = x.shape[0]

    def body(x_ref, ya_ref, ys_ref, z_ref, an_ref, sn_ref, g_ref, w_ref, x1_ref):
        u = _out_stage(*_out_stage_args(ya_ref, ys_ref, z_ref, an_ref, sn_ref))
        x1_ref[...] = x_ref[...] + g_ref[...] * _mm(u, w_ref[...])

    half = pl.BlockSpec((tm, ATTN_W), _row)
    hvec = pl.BlockSpec((1, ATTN_W), _fixed)
    return pl.pallas_call(
        body, name="out_proj_fwd", grid=(S // tm,),
        in_specs=[pl.BlockSpec((tm, D_MODEL), _row), half, half, half, hvec, hvec,
                  pl.BlockSpec((1, D_MODEL), _fixed), pl.BlockSpec((D_MODEL, D_MODEL), _fixed)],
        out_specs=pl.BlockSpec((tm, D_MODEL), _row),
        out_shape=jax.ShapeDtypeStruct((S, D_MODEL), F32),
        compiler_params=_cparams(VMEM_BIG),
    )(x, ya, ys, z, an, sn, gate1, w_o)


def _out_proj_bwd(dx1, ya, ys, z, an, sn, gate1, w_o, tm):
    S = dx1.shape[0]

    def body(dx1_ref, ya_ref, ys_ref, z_ref, an_ref, sn_ref, g_ref, w_ref,
             dya_ref, dys_ref, dz_ref, u_ref, dmix_ref, acc_ref):
        @pl.when(pl.program_id(0) == 0)
        def _():
            acc_ref[...] = jnp.zeros_like(acc_ref)

        u, vjp = jax.vjp(_out_stage, *_out_stage_args(ya_ref, ys_ref, z_ref, an_ref, sn_ref))
        dx1 = dx1_ref[...]
        mix = _mm(u, w_ref[...])
        dmix = dx1 * g_ref[...]
        du = _mm_nt(dmix, w_ref[...])
        dya, dys0, dys1, dz0, dz1, dan, dsn0, dsn1 = vjp(du)
        dya_ref[...] = dya
        dys_ref[...] = jnp.concatenate([dys0, dys1], axis=1)
        dz_ref[...] = jnp.concatenate([dz0, dz1], axis=1)
        u_ref[...] = u.astype(u_ref.dtype)
        dmix_ref[...] = dmix.astype(dmix_ref.dtype)
        acc_ref[0:1, :] += jnp.sum(dx1 * mix, axis=0, keepdims=True)
        acc_ref[1:2, :] += jnp.concatenate([dan, dsn0, dsn1], axis=1)

    half = pl.BlockSpec((tm, ATTN_W), _row)
    hvec = pl.BlockSpec((1, ATTN_W), _fixed)
    full = pl.BlockSpec((tm, D_MODEL), _row)
    return pl.pallas_call(
        body, name="out_proj_bwd", grid=(S // tm,),
        in_specs=[full, half, half, half, hvec, hvec,
                  pl.BlockSpec((1, D_MODEL), _fixed), pl.BlockSpec((D_MODEL, D_MODEL), _fixed)],
        out_specs=[half, half, half, full, full, pl.BlockSpec((8, D_MODEL), _fixed)],
        out_shape=[jax.ShapeDtypeStruct((S, ATTN_W), F32)] * 3
        + [jax.ShapeDtypeStruct((S, D_MODEL), MXU_DTYPE)] * 2 + [jax.ShapeDtypeStruct((8, D_MODEL), F32)],
        compiler_params=_cparams(VMEM_BIG),
    )(dx1, ya, ys, z, an, sn, gate1, w_o)


def _loss_rows(x2, fn, tgt):
    y = _rms(x2, fn, D_MODEL)
    per_row = jnp.sum(jnp.square(y - tgt), axis=1, keepdims=True)
    return jnp.sum(per_row, axis=0, keepdims=True) * (0.5 / D_MODEL)


def _mlp_loss(x1, tgt, norm2, scale2, shift2, gate2, fnorm, w_gu, w_d, tm):
    S = x1.shape[0]

    def body(x1_ref, t_ref, n_ref, sc_ref, sh_ref, g_ref, fn_ref, wgu_hbm, wd_hbm,
             dx1_ref, h_ref, dgu_ref, act_ref, dmlp_ref, acc_ref, wgu, wd):
        @pl.when(pl.program_id(0) == 0)
        def _():
            acc_ref[...] = jnp.zeros_like(acc_ref)
            pltpu.sync_copy(wgu_hbm, wgu)
            pltpu.sync_copy(wd_hbm, wd)

        x1 = x1_ref[...]
        gate2 = g_ref[...]
        h, vjp_h = jax.vjp(_modnorm, x1, n_ref[...], sc_ref[...], sh_ref[...])
        hb = h.astype(MXU_DTYPE)
        gu = _mm_nt(hb, wgu[...])
        g, u = gu[:, :D_FF], gu[:, D_FF:]
        sg = jax.nn.sigmoid(g)
        silu_g = g * sg
        act = (silu_g * u).astype(MXU_DTYPE)
        mlp = _mm(act, wd[...])
        x2 = x1 + gate2 * mlp
        loss, vjp_loss = jax.vjp(_loss_rows, x2, fn_ref[...], t_ref[...])
        dx2, dfn, _ = vjp_loss(jnp.ones((1, 1), F32))
        dmlp = (dx2 * gate2).astype(MXU_DTYPE)
        dact = _mm_nt(dmlp, wd[...])
        dg = dact * u * (sg * (1.0 + g * (1.0 - sg)))
        du = dact * silu_g
        dgu = jnp.concatenate([dg, du], axis=1).astype(MXU_DTYPE)
        dh = _mm(dgu, wgu[...])
        dx, dn, dsc, dsh = vjp_h(dh)
        dx1_ref[...] = dx2 + dx
        h_ref[...] = hb
        dgu_ref[...] = dgu
        act_ref[...] = act
        dmlp_ref[...] = dmlp
        acc_ref[0:1, :] += dn
        acc_ref[1:2, :] += dsc
        acc_ref[2:3, :] += dsh
        acc_ref[3:4, :] += jnp.sum(dx2 * mlp, axis=0, keepdims=True)
        acc_ref[4:5, :] += dfn
        acc_ref[5:6, :] += jnp.broadcast_to(loss, (1, D_MODEL))

    full = pl.BlockSpec((tm, D_MODEL), _row)
    vec = pl.BlockSpec((1, D_MODEL), _fixed)
    anyspec = pl.BlockSpec(memory_space=pl.ANY)
    return pl.pallas_call(
        body, name="mlp_loss", grid=(S // tm,),
        in_specs=[full, full, vec, vec, vec, vec, vec, anyspec, anyspec],
        out_specs=[full, full, pl.BlockSpec((tm, 2 * D_FF), _row), pl.BlockSpec((tm, D_FF), _row), full,
                   pl.BlockSpec((8, D_MODEL), _fixed)],
        out_shape=[jax.ShapeDtypeStruct((S, D_MODEL), F32), jax.ShapeDtypeStruct((S, D_MODEL), MXU_DTYPE),
                   jax.ShapeDtypeStruct((S, 2 * D_FF), MXU_DTYPE), jax.ShapeDtypeStruct((S, D_FF), MXU_DTYPE),
                   jax.ShapeDtypeStruct((S, D_MODEL), MXU_DTYPE), jax.ShapeDtypeStruct((8, D_MODEL), F32)],
        scratch_shapes=[pltpu.VMEM((2 * D_FF, D_MODEL), MXU_DTYPE), pltpu.VMEM((D_FF, D_MODEL), MXU_DTYPE)],
        compiler_params=_cparams(VMEM_BIG),
    )(x1, tgt, norm2, scale2, shift2, gate2, fnorm, w_gu, w_d)


def _wgrad(a, g, tk, ts, name):
    S, K = a.shape
    N = g.shape[1]
    ns = S // ts

    def body(a_ref, g_ref, o_ref, acc_ref):
        s = pl.program_id(1)

        @pl.when(s == 0)
        def _():
            acc_ref[...] = jnp.zeros_like(acc_ref)

        acc_ref[...] += _mm_tn(a_ref[...], g_ref[...])

        @pl.when(s == ns - 1)
        def _():
            o_ref[...] = acc_ref[...].astype(o_ref.dtype)

    return pl.pallas_call(
        body, name=name, grid=(K // tk, ns),
        in_specs=[pl.BlockSpec((ts, tk), lambda j, s: (s, j)), pl.BlockSpec((ts, N), lambda j, s: (s, 0))],
        out_specs=pl.BlockSpec((tk, N), lambda j, s: (j, 0)),
        out_shape=jax.ShapeDtypeStruct((K, N), WIRE_DTYPE),
        scratch_shapes=[pltpu.VMEM((tk, N), F32)],
        compiler_params=_cparams(VMEM_BIG),
    )(a, g)


def _attn_heads(q_heads, k_pads, v_pads, bias_heads, sink_cols, mask):
    outs = []
    for h in range(N_HEADS):
        kv = h // 4
        s = _mm_nt(q_heads[h], k_pads[kv]) * (HALF ** -0.5) + bias_heads[h]
        s = jnp.where(mask, s, -1e30)
        m = lax.stop_gradient(jnp.maximum(jnp.max(s, axis=-1, keepdims=True), sink_cols[h]))
        p = jnp.exp(s - m)
        den = jnp.sum(p, axis=-1, keepdims=True) + jnp.exp(sink_cols[h] - m)
        outs.append(_mm(p, v_pads[kv]) / den)
    return tuple(outs)


def _attn_mask(first):
    i = lax.broadcasted_iota(jnp.int32, (BLK, 2 * BLK), 0)
    j = lax.broadcasted_iota(jnp.int32, (BLK, 2 * BLK), 1)
    return (j > i) & (j <= i + BLK) & (j >= jnp.where(first, BLK, 0))


def _attn_operands(q_ref, kvp_ref, kvc_ref, bias_ref, sinks_ref):
    q_heads = _split_heads(q_ref[...].astype(F32), 4)
    kvp = kvp_ref[...].astype(F32)
    kvc = kvc_ref[...].astype(F32)
    kp, kc = _split_pair(kvp[:, :LANE]), _split_pair(kvc[:, :LANE])
    vp, vc = _split_pair(kvp[:, LANE:]), _split_pair(kvc[:, LANE:])
    k_pads = [jnp.concatenate([kp[g], kc[g]], axis=0) for g in range(2)]
    v_pads = [jnp.concatenate([vp[g], vc[g]], axis=0) for g in range(2)]
    bias_heads = [bias_ref[h] for h in range(N_HEADS)]
    sink_cols = [jnp.full((BLK, 1), sinks_ref[h], F32) for h in range(N_HEADS)]
    return q_heads, k_pads, v_pads, bias_heads, sink_cols


def _build_bias(bucket_ref, relb_ref, bias_ref):
    bk = bucket_ref[...]
    for h in range(N_HEADS):
        acc = jnp.zeros((BLK, 2 * BLK), F32)
        for b in range(N_BUCKETS):
            acc = jnp.where(bk == b, relb_ref[b, h], acc)
        bias_ref[h] = acc


def _attn_fwd(qkv, buckets, rel_bias, sinks, xchg):
    S = qkv.shape[0]
    nb = S // BLK

    def body(q_ref, kvp_ref, kvc_ref, bk_ref, relb_ref, sinks_ref, y_ref, bias_ref):
        i = pl.program_id(0)

        @pl.when(i == 0)
        def _():
            _build_bias(bk_ref, relb_ref, bias_ref)

        ops = _attn_operands(q_ref, kvp_ref, kvc_ref, bias_ref, sinks_ref)
        outs = _attn_heads(*ops, mask=_attn_mask(i == 0))
        y_ref[...] = _join_heads(outs)

    smem = pl.BlockSpec(memory_space=pltpu.SMEM)
    return _hosted_call(
        body, "attn_fwd", nb,
        in_specs=[pl.BlockSpec((BLK, ATTN_W), _row),
                  pl.BlockSpec((BLK, 2 * KV_W), lambda i: (jnp.maximum(i - 1, 0), 2)),
                  pl.BlockSpec((BLK, 2 * KV_W), lambda i: (i, 2)),
                  pl.BlockSpec((BLK, 2 * BLK), _fixed), smem, smem],
        out_specs=[pl.BlockSpec((BLK, ATTN_W), _row)],
        out_shape=[jax.ShapeDtypeStruct((S, ATTN_W), F32)],
        scratch_shapes=[pltpu.VMEM((N_HEADS, BLK, 2 * BLK), F32)],
        args=(qkv, qkv, qkv, buckets, rel_bias, sinks), xchg=xchg, cparams=_cparams(),
    )


def _attn_bwd(qkv, dy, buckets, rel_bias, sinks, xchg):
    S = qkv.shape[0]
    nb = S // BLK

    def body(q_ref, kvp_ref, kvc_ref, dy_ref, bk_ref, relb_ref, sinks_ref,
             dq_ref, dkv_ref, dsink_ref, drel_ref, bias_ref, dbias_ref, carry_ref):
        i = pl.program_id(0)
        blk = nb - 1 - i

        @pl.when(i == 0)
        def _():
            _build_bias(bk_ref, relb_ref, bias_ref)
            dbias_ref[...] = jnp.zeros_like(dbias_ref)
            carry_ref[...] = jnp.zeros_like(carry_ref)
            dsink_ref[...] = jnp.zeros_like(dsink_ref)
            drel_ref[...] = jnp.zeros_like(drel_ref)

        ops = _attn_operands(q_ref, kvp_ref, kvc_ref, bias_ref, sinks_ref)
        _, vjp = jax.vjp(functools.partial(_attn_heads, mask=_attn_mask(blk == 0)), *ops)
        dq_heads, dk_pads, dv_pads, dbias_heads, dsink_cols = vjp(tuple(_split_heads(dy_ref[...], 4)))
        dq_ref[...] = _join_heads(dq_heads)
        dk_prev = _join_pair(dk_pads[0][:BLK], dk_pads[1][:BLK])
        dk_cur = _join_pair(dk_pads[0][BLK:], dk_pads[1][BLK:])
        dv_prev = _join_pair(dv_pads[0][:BLK], dv_pads[1][:BLK])
        dv_cur = _join_pair(dv_pads[0][BLK:], dv_pads[1][BLK:])
        dkv_ref[...] = jnp.concatenate([dk_cur, dv_cur], axis=1) + carry_ref[...]
        carry_ref[...] = jnp.concatenate([dk_prev, dv_prev], axis=1)
        row = lax.broadcasted_iota(jnp.int32, (N_HEADS, LANE), 0)
        dsink = jnp.zeros((N_HEADS, LANE), F32)
        for h in range(N_HEADS):
            dbias_ref[h] += dbias_heads[h]
            dsink = dsink + jnp.where(row == h, jnp.sum(dsink_cols[h], axis=0, keepdims=True), 0.0)
        dsink_ref[...] += dsink

        @pl.when(i == nb - 1)
        def _():
            bk = bk_ref[...]
            r = lax.broadcasted_iota(jnp.int32, (N_BUCKETS, LANE), 0)
            l = lax.broadcasted_iota(jnp.int32, (N_BUCKETS, LANE), 1)
            res = jnp.zeros((N_BUCKETS, LANE), F32)
            for h in range(N_HEADS):
                db = dbias_ref[h]
                for b in range(N_BUCKETS):
                    v = jnp.sum(jnp.sum(jnp.where(bk == b, db, 0.0), axis=1, keepdims=True), axis=0, keepdims=True)
                    res = res + jnp.where((r == b) & (l == h), v, 0.0)
            drel_ref[...] = res

    smem = pl.BlockSpec(memory_space=pltpu.SMEM)
    rev = lambda i: (nb - 1 - i, 0)
    return _hosted_call(
        body, "attn_bwd", nb,
        in_specs=[pl.BlockSpec((BLK, ATTN_W), rev),
                  pl.BlockSpec((BLK, 2 * KV_W), lambda i: (jnp.maximum(nb - 2 - i, 0), 2)),
                  pl.BlockSpec((BLK, 2 * KV_W), lambda i: (nb - 1 - i, 2)),
                  pl.BlockSpec((BLK, ATTN_W), rev),
                  pl.BlockSpec((BLK, 2 * BLK), _fixed), smem, smem],
        out_specs=[pl.BlockSpec((BLK, ATTN_W), rev), pl.BlockSpec((BLK, 2 * KV_W), rev),
                   pl.BlockSpec((N_HEADS, LANE), _fixed), pl.BlockSpec((N_BUCKETS, LANE), _fixed)],
        out_shape=[jax.ShapeDtypeStruct((S, ATTN_W), F32), jax.ShapeDtypeStruct((S, 2 * KV_W), F32),
                   jax.ShapeDtypeStruct((N_HEADS, LANE), F32), jax.ShapeDtypeStruct((N_BUCKETS, LANE), F32)],
        scratch_shapes=[pltpu.VMEM((N_HEADS, BLK, 2 * BLK), F32), pltpu.VMEM((N_HEADS, BLK, 2 * BLK), F32),
                        pltpu.VMEM((BLK, 2 * KV_W), F32)],
        args=(qkv, qkv, qkv, dy, buckets, rel_bias, sinks), xchg=xchg, cparams=_cparams(),
    )


def _ssd_consts():
    r = lax.broadcasted_iota(jnp.int32, (BLK, BLK), 0)
    c = lax.broadcasted_iota(jnp.int32, (BLK, BLK), 1)
    causal = c <= r
    tri = causal.astype(F32)
    last = r == BLK - 1
    expand = [(r == h).astype(F32) for h in range(N_HEADS)]
    return causal, tri, last, expand


def _ssd_chunk(pre_heads, pre_b, pre_c, dt_raw, prev, dtb_row, alog_row, d_rows, consts):
    causal, tri, last, expand = consts
    xs = [_silu(t) for t in pre_heads]
    bg = [_silu(t) for t in pre_b]
    cg = [_silu(t) for t in pre_c]
    dt = _softplus(dt_raw + dtb_row)
    acs = _mm_hi(tri, dt * (-jnp.exp(alog_row)))
    cb = [_mm_nt(cg[g], bg[g]) for g in range(2)]
    ys, hs = [], []
    for h in range(N_HEADS):
        g = h // 4
        dt_b = _mm_hi(dt, expand[h])
        a_b = _mm_hi(acs, expand[h])
        lmat = jnp.exp(jnp.where(causal, a_b - a_b.T, -1e30))
        xdt = xs[h] * dt_b
        a_last = jnp.sum(jnp.where(last, a_b, 0.0), axis=0, keepdims=True)
        y = _mm(cb[g] * lmat, xdt) + _mm(cg[g], prev[h]) * jnp.exp(a_b) + d_rows[h] * xs[h]
        st = _mm_tn(bg[g], xdt * jnp.exp(a_last - a_b))
        ys.append(y)
        hs.append(prev[h] * jnp.exp(a_last) + st)
    return tuple(ys), tuple(hs)


def _conv_pre(ext_ref, halo, blk, cw_ref, cb_ref):
    ext_ref[0:8, :] = halo
    ext_ref[8:8 + BLK, :] = blk
    pre = cb_ref[...] + cw_ref[0:1, :] * ext_ref[pl.ds(5, BLK), :]
    for k in range(1, 4):
        pre = pre + cw_ref[k:k + 1, :] * ext_ref[pl.ds(5 + k, BLK), :]
    return pre


def _ssd_split(pre):
    heads = _split_heads(pre[:, :SSM_W], 4)
    pb = [pre[:, SSM_W + g * D_STATE:SSM_W + (g + 1) * D_STATE] for g in range(2)]
    pc = [pre[:, SSM_W + 2 * D_STATE + g * D_STATE:SSM_W + 2 * D_STATE + (g + 1) * D_STATE] for g in range(2)]
    return heads, pb, pc


def _ssd_fwd(xbc, dt_raw, conv_w, conv_b, dtb_row, alog_row, d_exp, xchg):
    S = xbc.shape[0]
    nc = S // BLK

    def body(xbc_ref, halo_ref, dt_ref, cw_ref, cb_ref, dtb_ref, alog_ref, d_ref, y_ref, prev_ref, ext_ref, state_ref):
        i = pl.program_id(0)

        @pl.when(i == 0)
        def _():
            state_ref[...] = jnp.zeros_like(state_ref)

        halo = halo_ref[...] * jnp.where(i > 0, 1.0, 0.0)
        pre = _conv_pre(ext_ref, halo, xbc_ref[...], cw_ref, cb_ref)
        heads, pb, pc = _ssd_split(pre)
        prev = [state_ref[h] for h in range(N_HEADS)]
        for h in range(N_HEADS):
            prev_ref[0, h] = prev[h]
        d_rows = [d_ref[h:h + 1, :] for h in range(N_HEADS)]
        ys, hs = _ssd_chunk(heads, pb, pc, dt_ref[...], prev, dtb_ref[...], alog_ref[...], d_rows, _ssd_consts())
        for h in range(N_HEADS):
            state_ref[h] = hs[h]
        y_ref[...] = _join_heads(ys)

    vec = pl.BlockSpec((1, LANE), _fixed)
    return _hosted_call(
        body, "ssd_fwd", nc,
        in_specs=[pl.BlockSpec((BLK, XBC_W), _row),
                  pl.BlockSpec((8, XBC_W), lambda i: (jnp.maximum(i * (BLK // 8) - 1, 0), 0)),
                  pl.BlockSpec((BLK, LANE), _row),
                  pl.BlockSpec((4, XBC_W), _fixed), pl.BlockSpec((1, XBC_W), _fixed), vec, vec,
                  pl.BlockSpec((N_HEADS, LANE), _fixed)],
        out_specs=[pl.BlockSpec((BLK, SSM_W), _row),
                   pl.BlockSpec((1, N_HEADS, D_STATE, LANE), lambda i: (i, 0, 0, 0))],
        out_shape=[jax.ShapeDtypeStruct((S, SSM_W), F32), jax.ShapeDtypeStruct((nc, N_HEADS, D_STATE, LANE), F32)],
        scratch_shapes=[pltpu.VMEM((8 + BLK, XBC_W), F32), pltpu.VMEM((N_HEADS, D_STATE, LANE), F32)],
        args=(xbc, xbc, dt_raw, conv_w, conv_b, dtb_row, alog_row, d_exp), xchg=xchg, cparams=_cparams(),
    )


def _ssd_bwd(xbc, dt_raw, prev_states, dy, conv_w, conv_b, dtb_row, alog_row, d_exp, xchg):
    S = xbc.shape[0]
    nc = S // BLK

    def body(xbc_ref, halo_ref, dt_ref, prev_ref, dy_ref, cw_ref, cb_ref, dtb_ref, alog_ref, d_ref,
             dxbc_ref, ddt_ref, dcw_ref, dvec_ref, dd_ref, ext_ref, dpe_ref, gstate_ref, ghalo_ref):
        i = pl.program_id(0)
        c = nc - 1 - i

        @pl.when(i == 0)
        def _():
            gstate_ref[...] = jnp.zeros_like(gstate_ref)
            ghalo_ref[...] = jnp.zeros_like(ghalo_ref)
            dcw_ref[...] = jnp.zeros_like(dcw_ref)
            dvec_ref[...] = jnp.zeros_like(dvec_ref)
            dd_ref[...] = jnp.zeros_like(dd_ref)
            dpe_ref[...] = jnp.zeros_like(dpe_ref)

        halo = halo_ref[...] * jnp.where(c > 0, 1.0, 0.0)
        pre = _conv_pre(ext_ref, halo, xbc_ref[...], cw_ref, cb_ref)
        heads, pb, pc = _ssd_split(pre)
        prev = [prev_ref[0, h] for h in range(N_HEADS)]
        d_rows = [d_ref[h:h + 1, :] for h in range(N_HEADS)]
        _, vjp = jax.vjp(functools.partial(_ssd_chunk, consts=_ssd_consts()),
                         heads, pb, pc, dt_ref[...], prev, dtb_ref[...], alog_ref[...], d_rows)
        dys = tuple(_split_heads(dy_ref[...], 4))
        dhs = tuple(gstate_ref[h] for h in range(N_HEADS))
        dheads, dpb, dpc, ddt, dprev, ddtb, dalog, dd_rows = vjp((dys, dhs))
        for h in range(N_HEADS):
            gstate_ref[h] = dprev[h]
            dd_ref[h:h + 1, :] += dd_rows[h]
        ddt_ref[...] = ddt
        dvec_ref[0:1, :] += ddtb
        dvec_ref[1:2, :] += dalog
        dpre = jnp.concatenate([_join_heads(dheads)] + list(dpb) + list(dpc), axis=1)
        dpe_ref[8:8 + BLK, :] = dpre
        dext = cw_ref[0:1, :] * dpe_ref[pl.ds(3, 8 + BLK), :]
        dcw_ref[0:1, :] += jnp.sum(dpre * ext_ref[pl.ds(5, BLK), :], axis=0, keepdims=True)
        for k in range(1, 4):
            dext = dext + cw_ref[k:k + 1, :] * dpe_ref[pl.ds(3 - k, 8 + BLK), :]
            dcw_ref[k:k + 1, :] += jnp.sum(dpre * ext_ref[pl.ds(5 + k, BLK), :], axis=0, keepdims=True)
        dcw_ref[4:5, :] += jnp.sum(dpre, axis=0, keepdims=True)
        dxbc_ref[...] = dext[8:, :]
        dxbc_ref[BLK - 8:BLK, :] += ghalo_ref[...]
        ghalo_ref[...] = dext[:8, :]

    vec = pl.BlockSpec((1, LANE), _fixed)
    rev = lambda i: (nc - 1 - i, 0)
    return _hosted_call(
        body, "ssd_bwd", nc,
        in_specs=[pl.BlockSpec((BLK, XBC_W), rev),
                  pl.BlockSpec((8, XBC_W), lambda i: (jnp.maximum((nc - 1 - i) * (BLK // 8) - 1, 0), 0)),
                  pl.BlockSpec((BLK, LANE), rev),
                  pl.BlockSpec((1, N_HEADS, D_STATE, LANE), lambda i: (nc - 1 - i, 0, 0, 0)),
                  pl.BlockSpec((BLK, SSM_W), rev),
                  pl.BlockSpec((4, XBC_W), _fixed), pl.BlockSpec((1, XBC_W), _fixed), vec, vec,
                  pl.BlockSpec((N_HEADS, LANE), _fixed)],
        out_specs=[pl.BlockSpec((BLK, XBC_W), rev), pl.BlockSpec((BLK, LANE), rev),
                   pl.BlockSpec((8, XBC_W), _fixed), pl.BlockSpec((8, LANE), _fixed),
                   pl.BlockSpec((N_HEADS, LANE), _fixed)],
        out_shape=[jax.ShapeDtypeStruct((S, XBC_W), F32), jax.ShapeDtypeStruct((S, LANE), F32),
                   jax.ShapeDtypeStruct((8, XBC_W), F32), jax.ShapeDtypeStruct((8, LANE), F32),
                   jax.ShapeDtypeStruct((N_HEADS, LANE), F32)],
        scratch_shapes=[pltpu.VMEM((8 + BLK, XBC_W), F32), pltpu.VMEM((16 + BLK, XBC_W), F32),
                        pltpu.VMEM((N_HEADS, D_STATE, LANE), F32), pltpu.VMEM((8, XBC_W), F32)],
        args=(xbc, xbc, dt_raw, prev_states, dy, conv_w, conv_b, dtb_row, alog_row, d_exp), xchg=xchg,
        cparams=_cparams(VMEM_BIG),
    )


def _adamw_math(w, g, m, v):
    m = ADAM_B1 * m + (1.0 - ADAM_B1) * g
    v = ADAM_B2 * v + (1.0 - ADAM_B2) * jnp.square(g)
    m_hat = m / (1.0 - ADAM_B1 ** ADAM_STEP)
    v_hat = v / (1.0 - ADAM_B2 ** ADAM_STEP)
    delta = -ADAM_LR * (m_hat / (jnp.sqrt(v_hat) + ADAM_EPS) + ADAM_WD * w)
    return delta, m, v


def _reduce_adamw(parts, w, m, v, name):
    R, C = w.shape

    def body(p_ref, w_ref, m_ref, v_ref, g_ref, d_ref, nm_ref, nv_ref):
        g = p_ref[0].astype(F32)
        for i in range(1, N_DEV):
            g = g + p_ref[i].astype(F32)
        d, nm, nv = _adamw_math(w_ref[...], g, m_ref[...], v_ref[...])
        g_ref[...] = g
        d_ref[...] = d
        nm_ref[...] = nm
        nv_ref[...] = nv

    if R % 16 == 0:
        tr = max(t for t in range(16, 257, 16) if R % t == 0)
        n, blk, pblk = R // tr, pl.BlockSpec((tr, C), _row), pl.BlockSpec((N_DEV, tr, C), lambda i: (0, i, 0))
    else:
        tl = 256
        n, blk, pblk = C // tl, pl.BlockSpec((R, tl), lambda i: (0, i)), pl.BlockSpec((N_DEV, R, tl),
                                                                                      lambda i: (0, 0, i))
    return pl.pallas_call(
        body, name=name, grid=(n,), in_specs=[pblk, blk, blk, blk],
        out_specs=[blk] * 4, out_shape=[jax.ShapeDtypeStruct((R, C), F32)] * 4,
    )(parts, w, m, v)


_SMALL_NAMES = ("ada_b", "norm1", "conv_w", "conv_b", "dt_bias", "A_log", "D_skip", "sinks", "attn_out_norm",
                "ssm_out_norm", "norm2", "rel_bias", "final_norm")
N_MOD = 6 * D_MODEL


def _mod_row(a0, a1, a2):
    return jnp.concatenate([a0[2:3], a0[1:2], a1[0:1], a2[2:3], a2[1:2], a2[3:4]], axis=1)


def _small_update(gathered, params):
    n_g = len(gathered)
    flat = [a for name in _SMALL_NAMES for a in params[name]]

    def body(*refs):
        a0_ref, a1_ref, a2_ref, cw_ref, dv_ref, dd_ref, ds_ref, dr_ref, c_ref = refs[:n_g]
        wmv = refs[n_g:n_g + len(flat)]
        outs = refs[n_g + len(flat):]

        def total(ref):
            t = ref[0]
            for i in range(1, N_DEV):
                t = t + ref[i]
            return t

        t0, t1, t2, tcw, tdv, tdd, tds, tdr = [total(r) for r in (a0_ref, a1_ref, a2_ref, cw_ref, dv_ref, dd_ref,
                                                                   ds_ref, dr_ref)]
        r8 = lax.broadcasted_iota(jnp.int32, (N_HEADS, LANE), 0)
        l8 = lax.broadcasted_iota(jnp.int32, (N_HEADS, LANE), 1)

        def diag_row(t):
            return jnp.sum(jnp.where(r8 == l8, t, 0.0), axis=0, keepdims=True)[:, :N_HEADS]

        me = _lin(_my_pos())
        n_cw = XBC_W // N_DEV
        cw_mine = jnp.zeros((4, n_cw), F32)
        for j in range(N_DEV):
            cw_mine = cw_mine + tcw[0:4, j * n_cw:(j + 1) * n_cw] * jnp.where(me == j, 1.0, 0.0)
        grads = {
            "ada_b": _mod_row(t0, t1, t2), "norm1": t0[0:1], "conv_w": cw_mine, "conv_b": tcw[4:5],
            "dt_bias": tdv[0:1, :N_HEADS], "A_log": tdv[1:2, :N_HEADS],
            "D_skip": diag_row(jnp.broadcast_to(jnp.sum(tdd, axis=1, keepdims=True), (N_HEADS, LANE))),
            "sinks": diag_row(tds), "attn_out_norm": t1[1:2, :ATTN_W], "ssm_out_norm": t1[1:2, ATTN_W:],
            "norm2": t2[0:1], "rel_bias": tdr[:, :N_HEADS], "final_norm": t2[4:5],
        }
        for k, name in enumerate(_SMALL_NAMES):
            w_ref, m_ref, v_ref = wmv[3 * k:3 * k + 3]
            g = grads[name]
            d, nm, nv = _adamw_math(w_ref[...], g, m_ref[...], v_ref[...])
            for o, val in zip(outs[4 * k:4 * k + 4], (g, d, nm, nv)):
                o[...] = val
        loss_ref, call_ref, dmod_ref = outs[4 * len(_SMALL_NAMES):]
        loss_ref[...] = t2[5:6, 0:1]
        call_ref[...] = jnp.concatenate([c_ref[i] for i in range(N_DEV)], axis=0)
        dmod_ref[...] = jnp.concatenate([_mod_row(a0_ref[i], a1_ref[i], a2_ref[i]) for i in range(N_DEV)], axis=0)

    out_shape = [jax.ShapeDtypeStruct(params[name][0].shape, F32) for name in _SMALL_NAMES for _ in range(4)]
    out_shape += [jax.ShapeDtypeStruct((1, 1), F32), jax.ShapeDtypeStruct((N_DEV, D_MODEL), F32),
                  jax.ShapeDtypeStruct((N_DEV, N_MOD), F32)]
    res = pl.pallas_call(body, name="small_update", out_shape=out_shape)(*gathered, *flat)
    upd = {name: res[4 * k:4 * k + 4] for k, name in enumerate(_SMALL_NAMES)}
    loss, c_all, dmod_all = res[4 * len(_SMALL_NAMES):]
    return upd, loss, c_all, dmod_all


def _ada_w_update(c_all, dmod_all, w, m, v):
    chunk = w.shape[1]

    def body(c_ref, dm_ref, w_ref, m_ref, v_ref, g_ref, d_ref, nm_ref, nv_ref):
        me = _lin(_my_pos())
        dm = jnp.zeros((N_DEV, chunk), F32)
        for j in range(N_DEV):
            dm = dm + dm_ref[:, j * chunk:(j + 1) * chunk] * jnp.where(me == j, 1.0, 0.0)
        g = lax.dot_general(_silu(c_ref[...]), dm, (((0,), (0,)), ((), ())), precision=HI,
                            preferred_element_type=F32)
        d, nm, nv = _adamw_math(w_ref[...], g, m_ref[...], v_ref[...])
        g_ref[...] = g
        d_ref[...] = d
        nm_ref[...] = nm
        nv_ref[...] = nv

    return pl.pallas_call(body, name="ada_w_update", out_shape=[jax.ShapeDtypeStruct(w.shape, F32)] * 4,
                          compiler_params=_cparams(VMEM_BIG))(c_all, dmod_all, w, m, v)


def _local_step(x, tgt, mod, w_in, conv_w, w_o_mine, w_gu_mine, w_d_mine, p):
    S = x.shape[0]
    tm = min(512, S)
    tmm = min(256, S)
    shift1, scale1, gate1, shift2, scale2, gate2 = [mod[i:i + 1] for i in range(6)]
    buckets = jnp.asarray(_t5_bucket_table())
    dtb_row = jnp.pad(p["dt_bias"], ((0, 0), (0, LANE - N_HEADS)))
    alog_row = jnp.pad(p["A_log"], ((0, 0), (0, LANE - N_HEADS)))
    d_exp = jnp.broadcast_to(p["D_skip"].reshape(N_HEADS, 1), (N_HEADS, LANE))
    sinks = p["sinks"].reshape(N_HEADS)

    qkv, z, xbc, dt_raw = _in_proj_fwd(x, p["norm1"], scale1, shift1, w_in, tm)
    (ya,), (g_d,) = _attn_fwd(qkv, buckets, p["rel_bias"], sinks, ([w_d_mine], False))
    (ys, prev_states), (g_gu, g_o) = _ssd_fwd(xbc, dt_raw, conv_w, p["conv_b"], dtb_row, alog_row, d_exp,
                                              ([w_gu_mine, w_o_mine], False))
    w_gu = g_gu.reshape(2 * D_FF, D_MODEL)
    w_o = g_o.reshape(D_MODEL, D_MODEL)
    w_d = g_d.reshape(D_FF, D_MODEL)
    x1 = _out_proj_fwd(x, ya, ys, z, p["attn_out_norm"], p["ssm_out_norm"], gate1, w_o, tm)
    dx1, h2, dgu, act, dmlp, acc2 = _mlp_loss(x1, tgt, p["norm2"], scale2, shift2, gate2, p["final_norm"],
                                              w_gu, w_d, tmm)
    g_w_gu = _wgrad(dgu, h2, 2 * D_FF // 4, tm, "wgrad_gate_up")
    g_w_d = _wgrad(act, dmlp, D_FF // 2, tm, "wgrad_down")
    dya, dys, dz, u, dmix, acc1 = _out_proj_bwd(dx1, ya, ys, z, p["attn_out_norm"], p["ssm_out_norm"], gate1, w_o, tm)
    g_w_o = _wgrad(u, dmix, D_MODEL, tm, "wgrad_out")
    (dq, dkv, dsink, drel), (r_gu,) = _attn_bwd(qkv, dya, buckets, p["rel_bias"], sinks,
                                                ([g_w_gu.reshape(N_DEV, 2 * D_FF // N_DEV, D_MODEL)], True))
    (dxbc, ddt, dcw, dvec, dd), (r_d, r_o) = _ssd_bwd(
        xbc, dt_raw, prev_states, dys, conv_w, p["conv_b"], dtb_row, alog_row, d_exp,
        ([g_w_d.reshape(N_DEV, D_FF // N_DEV, D_MODEL), g_w_o.reshape(N_DEV, D_MODEL // N_DEV, D_MODEL)], True))
    gx, h1, dproj, acc0 = _in_proj_bwd(x, dx1, dq, dkv, dz, dxbc, ddt, p["norm1"], scale1, shift1, w_in, tm)
    g_w_in = _wgrad(dproj, h1, IN_PAD, tm, "wgrad_in")
    return gx, g_w_in, (r_o, r_gu, r_d), (acc0, acc1, acc2, dcw, dvec, dd, dsink, drel)


def kernel(x, c, ada_w, ada_b, norm1, w_in, conv_w, conv_b, dt_bias, A_log, D_skip, sinks, attn_out_norm, ssm_out_norm, w_o, norm2, w_gate_up, w_down, rel_bias, final_norm, loss_target, m_ada_w, m_ada_b, m_norm1, m_w_in, m_conv_w, m_conv_b, m_dt_bias, m_A_log, m_D_skip, m_sinks, m_attn_out_norm, m_ssm_out_norm, m_w_o, m_norm2, m_w_gate_up, m_w_down, m_rel_bias, m_final_norm, v_ada_w, v_ada_b, v_norm1, v_w_in, v_conv_w, v_conv_b, v_dt_bias, v_A_log, v_D_skip, v_sinks, v_attn_out_norm, v_ssm_out_norm, v_w_o, v_norm2, v_w_gate_up, v_w_down, v_rel_bias, v_final_norm):
    two_d = lambda a: a if a.ndim == 2 else a.reshape(-1, a.shape[-1])
    small_params = dict(
        ada_b=(ada_b, m_ada_b, v_ada_b), norm1=(norm1, m_norm1, v_norm1), conv_w=(conv_w, m_conv_w, v_conv_w),
        conv_b=(conv_b, m_conv_b, v_conv_b), dt_bias=(dt_bias, m_dt_bias, v_dt_bias), A_log=(A_log, m_A_log, v_A_log),
        D_skip=(D_skip, m_D_skip, v_D_skip), sinks=(sinks, m_sinks, v_sinks),
        attn_out_norm=(attn_out_norm, m_attn_out_norm, v_attn_out_norm),
        ssm_out_norm=(ssm_out_norm, m_ssm_out_norm, v_ssm_out_norm), norm2=(norm2, m_norm2, v_norm2),
        rel_bias=(rel_bias, m_rel_bias, v_rel_bias), final_norm=(final_norm, m_final_norm, v_final_norm))
    small_params = {k: tuple(two_d(a) for a in v) for k, v in small_params.items()}
    S = x.shape[1]
    xs, tgt = x.reshape(S, D_MODEL), loss_target.reshape(S, D_MODEL)
    ada_w2 = ada_w[0]
    chunk = ada_w2.shape[1]
    t_in = [jnp.transpose(a[0]) for a in (w_in, m_w_in, v_w_in)]
    t_gu = [jnp.transpose(a[0]) for a in (w_gate_up, m_w_gate_up, v_w_gate_up)]

    mod = _mod_exchange(c, ada_w2, ada_b.reshape(N_DEV, chunk)).reshape(6, D_MODEL)

    g_in, g_cw = _exchange([t_in[0].astype(WIRE_DTYPE), conv_w[0]], scatter=False, name="gather_w_in")
    w_in_full = jnp.pad(g_in.reshape(IN_W, D_MODEL), ((0, IN_PAD - IN_W), (0, 0)))
    conv_w_full = jnp.transpose(g_cw, (1, 0, 2)).reshape(4, XBC_W)

    p = {k: v[0] for k, v in small_params.items()}
    gx, gw_in, (r_o, r_gu, r_d), blocks = _local_step(
        xs, tgt, mod, w_in_full, conv_w_full, w_o[0].astype(WIRE_DTYPE), t_gu[0].astype(WIRE_DTYPE),
        w_down[0].astype(WIRE_DTYPE), p)

    (r_in,) = _exchange([gw_in[:IN_W].reshape(N_DEV, IN_W // N_DEV, D_MODEL)], scatter=True, name="scatter_w_in")

    gathered = _exchange(list(blocks) + [c], scatter=False, name="gather_small")
    small, loss, c_all, dmod_all = _small_update(gathered, small_params)

    big = {
        "ada_w": _ada_w_update(c_all, dmod_all, ada_w2, m_ada_w[0], v_ada_w[0]),
        "w_in": [jnp.transpose(a) for a in _reduce_adamw(r_in, *t_in, "adamw_w_in")],
        "w_o": _reduce_adamw(r_o, w_o[0], m_w_o[0], v_w_o[0], "adamw_w_o"),
        "w_gate_up": [jnp.transpose(a) for a in _reduce_adamw(r_gu, *t_gu, "adamw_w_gate_up")],
        "w_down": _reduce_adamw(r_d, w_down[0], m_w_down[0], v_w_down[0], "adamw_w_down"),
    }
    big.update(small)

    order = ['ada_w', 'ada_b', 'norm1', 'w_in', 'conv_w', 'conv_b', 'dt_bias', 'A_log', 'D_skip', 'sinks',
             'attn_out_norm', 'ssm_out_norm', 'w_o', 'norm2', 'w_gate_up', 'w_down', 'rel_bias', 'final_norm']
    shapes = dict(ada_w=ada_w.shape, ada_b=ada_b.shape, norm1=norm1.shape, w_in=w_in.shape, conv_w=conv_w.shape,
                  conv_b=conv_b.shape, dt_bias=dt_bias.shape, A_log=A_log.shape, D_skip=D_skip.shape,
                  sinks=sinks.shape, attn_out_norm=attn_out_norm.shape, ssm_out_norm=ssm_out_norm.shape,
                  w_o=w_o.shape, norm2=norm2.shape, w_gate_up=w_gate_up.shape, w_down=w_down.shape,
                  rel_bias=rel_bias.shape, final_norm=final_norm.shape)
    outs = [[], [], [], []]
    for name in order:
        for kind in range(4):
            outs[kind].append(big[name][kind].reshape(shapes[name]))
    return (loss.reshape(()), gx.reshape(x.shape), *outs[0], *outs[1], *outs[2], *outs[3])
```

```python
import functools

import numpy as np
import jax
import jax.numpy as jnp
from jax import lax
from jax.experimental import pallas as pl
from jax.experimental.pallas import tpu as pltpu

F32 = jnp.float32
MXU_DTYPE = jnp.bfloat16
WIRE_DTYPE = jnp.bfloat16
HI = lax.Precision.HIGHEST
MESH = pl.DeviceIdType.MESH
N_DEV = 8

D_MODEL = 1024
ATTN_W = 512
KV_W = 128
SSM_W = 512
XBC_W = 1024
N_HEADS = 8
D_STATE = 128
D_FF = 2816
IN_W = 2312
IN_PAD = 2432
BLK = 128
N_BUCKETS = 32
EPS = 1e-6
LANE = 128
HALF = 64

ADAM_LR, ADAM_B1, ADAM_B2, ADAM_EPS, ADAM_WD, ADAM_STEP = 0.001, 0.9, 0.999, 1e-08, 0.01, 10

VMEM_BIG = 56 * 1024 * 1024


def _cparams(vmem=None):
    if vmem is None:
        return pltpu.CompilerParams()
    return pltpu.CompilerParams(vmem_limit_bytes=vmem)


def _mm(a, b):
    return jnp.dot(a.astype(MXU_DTYPE), b.astype(MXU_DTYPE), preferred_element_type=F32)


def _mm_nt(a, b):
    return lax.dot_general(a.astype(MXU_DTYPE), b.astype(MXU_DTYPE), (((1,), (1,)), ((), ())),
                           preferred_element_type=F32)


def _mm_tn(a, b):
    return lax.dot_general(a.astype(MXU_DTYPE), b.astype(MXU_DTYPE), (((0,), (0,)), ((), ())),
                           preferred_element_type=F32)


def _mm_hi(a, b):
    return jnp.dot(a, b, precision=HI, preferred_element_type=F32)


def _silu(x):
    return x * jax.nn.sigmoid(x)


def _softplus(x):
    return jnp.maximum(x, 0.0) + jnp.log1p(jnp.exp(-jnp.abs(x)))


def _rms(x, g, n):
    return x * lax.rsqrt(jnp.sum(x * x, axis=-1, keepdims=True) * (1.0 / n) + EPS) * g


def _modnorm(x, g, scale, shift):
    return _rms(x, g, x.shape[-1]) * (1.0 + scale) + shift


def _lane_iota(shape):
    return lax.broadcasted_iota(jnp.int32, shape, len(shape) - 1)


def _split_pair(t):
    lane = _lane_iota(t.shape)
    lo = jnp.where(lane < HALF, t, 0.0)
    hi = pltpu.roll(jnp.where(lane >= HALF, t, 0.0), HALF, 1)
    return lo, hi


def _join_pair(lo, hi):
    lane = _lane_iota(lo.shape)
    return jnp.where(lane < HALF, lo, pltpu.roll(hi, HALF, 1))


def _split_heads(t, n_pairs):
    out = []
    for p in range(n_pairs):
        out.extend(_split_pair(t[:, p * LANE:(p + 1) * LANE]))
    return out


def _join_heads(hs):
    return jnp.concatenate([_join_pair(hs[2 * p], hs[2 * p + 1]) for p in range(len(hs) // 2)], axis=1)


def _t5_bucket_table():
    dist = np.arange(BLK)[:, None] + BLK - np.arange(2 * BLK)[None, :]
    n = np.maximum(dist, 0)
    max_exact = N_BUCKETS // 2
    large = max_exact + (np.log(np.maximum(n, 1) / max_exact) / np.log(128 / max_exact)
                         * (N_BUCKETS - max_exact)).astype(np.int32)
    large = np.minimum(large, N_BUCKETS - 1)
    return np.where(n < max_exact, n, large).astype(np.int32)


def _my_pos():
    return lax.axis_index("x"), lax.axis_index("y"), lax.axis_index("c")


def _peer(k):
    x, y, c = _my_pos()
    return (1 - x if k & 4 else x, 1 - y if k & 2 else y, 1 - c if k & 1 else c)


def _lin(pos):
    return 4 * pos[0] + 2 * pos[1] + pos[2]


def _xchg_copies(ins, outs, sems, scatter):
    local_sem, send_sem, recv_sem = sems
    me = _lin(_my_pos())
    local, remote = [], []
    for a in range(len(ins)):
        src = ins[a].at[me] if scatter else ins[a]
        local.append(pltpu.make_async_copy(src, outs[a].at[me], local_sem.at[a]))
    for k in range(1, N_DEV):
        peer = _peer(k)
        for a in range(len(ins)):
            src = ins[a].at[_lin(peer)] if scatter else ins[a]
            remote.append(pltpu.make_async_remote_copy(src, outs[a].at[me], send_sem.at[a, k - 1],
                                                       recv_sem.at[a, k - 1], device_id=peer, device_id_type=MESH))
    return local, remote


def _xchg_start(ins, outs, sems, scatter):
    local, remote = _xchg_copies(ins, outs, sems, scatter)
    for cp in local + remote:
        cp.start()


def _xchg_wait(ins, outs, sems, scatter):
    local, remote = _xchg_copies(ins, outs, sems, scatter)
    for cp in local:
        cp.wait()
    for cp in remote:
        cp.wait_send()
        cp.wait_recv()


def _xchg_shapes(arrs, scatter):
    n = len(arrs)
    if scatter:
        out_shape = [jax.ShapeDtypeStruct(a.shape, a.dtype) for a in arrs]
    else:
        out_shape = [jax.ShapeDtypeStruct((N_DEV,) + a.shape, a.dtype) for a in arrs]
    sems = [pltpu.SemaphoreType.DMA((n,)), pltpu.SemaphoreType.DMA((n, N_DEV - 1)),
            pltpu.SemaphoreType.DMA((n, N_DEV - 1))]
    return out_shape, sems


def _exchange(arrs, scatter, name):
    n = len(arrs)
    out_shape, sems = _xchg_shapes(arrs, scatter)

    def body(*refs):
        ins, outs, s = refs[:n], refs[n:2 * n], refs[2 * n:]
        _xchg_start(ins, outs, s, scatter)
        _xchg_wait(ins, outs, s, scatter)

    hbm = pl.BlockSpec(memory_space=pltpu.HBM)
    return pl.pallas_call(body, name=name, out_shape=out_shape, in_specs=[hbm] * n, out_specs=[hbm] * n,
                          scratch_shapes=sems)(*arrs)


def _hosted_call(body, name, n_steps, in_specs, out_specs, out_shape, scratch_shapes, args, xchg, cparams):
    arrs, scatter = xchg
    n, n_in, n_out, n_scr = len(arrs), len(in_specs), len(out_specs), len(scratch_shapes)
    x_shape, x_sems = _xchg_shapes(arrs, scatter)

    def hosted(*refs):
        ins, refs = refs[:n_in], refs[n_in:]
        x_in, refs = refs[:n], refs[n:]
        outs, refs = refs[:n_out], refs[n_out:]
        x_out, refs = refs[:n], refs[n:]
        scr, sems = refs[:n_scr], refs[n_scr:]

        @pl.when(pl.program_id(0) == 0)
        def _():
            _xchg_start(x_in, x_out, sems, scatter)

        body(*ins, *outs, *scr)

        @pl.when(pl.program_id(0) == n_steps - 1)
        def _():
            _xchg_wait(x_in, x_out, sems, scatter)

    hbm = pl.BlockSpec(memory_space=pltpu.HBM)
    res = pl.pallas_call(
        hosted, name=name, grid=(n_steps,), in_specs=list(in_specs) + [hbm] * n,
        out_specs=list(out_specs) + [hbm] * n, out_shape=list(out_shape) + x_shape,
        scratch_shapes=list(scratch_shapes) + x_sems, compiler_params=cparams,
    )(*args, *arrs)
    return res[:n_out], res[n_out:]


def _mod_exchange(c, ada_w, ada_b8):
    chunk = ada_w.shape[1]

    def body(c_ref, w_ref, b_ref, out_ref, cbuf, part, s1, r1, s2, r2):
        me = _lin(_my_pos())
        first = []
        for k in range(1, N_DEV):
            cp = pltpu.make_async_remote_copy(c_ref, cbuf.at[me], s1.at[k - 1], r1.at[k - 1],
                                              device_id=_peer(k), device_id_type=MESH)
            cp.start()
            first.append(cp)
        cbuf[me] = c_ref[...]
        for cp in first:
            cp.wait_send()
            cp.wait_recv()
        cond = _silu(jnp.concatenate([cbuf[i] for i in range(N_DEV)], axis=0))
        mod = _mm_hi(cond, w_ref[...]) + b_ref[pl.ds(me, 1), :]
        for j in range(N_DEV):
            part[j] = mod[j:j + 1, :]
        second = []
        for k in range(1, N_DEV):
            peer = _peer(k)
            cp = pltpu.make_async_remote_copy(part.at[_lin(peer)], out_ref.at[me], s2.at[k - 1], r2.at[k - 1],
                                              device_id=peer, device_id_type=MESH)
            cp.start()
            second.append(cp)
        out_ref[me] = part[me]
        for cp in second:
            cp.wait_send()
            cp.wait_recv()

    vm = pl.BlockSpec(memory_space=pltpu.VMEM)
    return pl.pallas_call(
        body, name="mod_exchange", out_shape=jax.ShapeDtypeStruct((N_DEV, 1, chunk), F32),
        in_specs=[vm, vm, vm], out_specs=vm,
        scratch_shapes=[pltpu.VMEM((N_DEV, 1, D_MODEL), F32), pltpu.VMEM((N_DEV, 1, chunk), F32)]
        + [pltpu.SemaphoreType.DMA((N_DEV - 1,))] * 4,
    )(c, ada_w, ada_b8)


def _row(i):
    return (i, 0)


def _fixed(i):
    return (0, 0)


def _in_proj_fwd(x, norm1, scale1, shift1, w_in, tm):
    S = x.shape[0]

    def body(x_ref, n_ref, sc_ref, sh_ref, w_ref, qkv_ref, z_ref, xbc_ref, dt_ref):
        h = _modnorm(x_ref[...], n_ref[...], sc_ref[...], sh_ref[...])
        p = _mm_nt(h, w_ref[...])
        qkv_ref[...] = p[:, :768].astype(qkv_ref.dtype)
        z_ref[...] = p[:, 768:1280]
        xbc_ref[...] = p[:, 1280:2304]
        dt_ref[...] = p[:, 2304:IN_PAD]

    vec = pl.BlockSpec((1, D_MODEL), _fixed)
    return pl.pallas_call(
        body, name="in_proj_fwd", grid=(S // tm,),
        in_specs=[pl.BlockSpec((tm, D_MODEL), _row), vec, vec, vec, pl.BlockSpec((IN_PAD, D_MODEL), _fixed)],
        out_specs=[pl.BlockSpec((tm, 768), _row), pl.BlockSpec((tm, SSM_W), _row),
                   pl.BlockSpec((tm, XBC_W), _row), pl.BlockSpec((tm, LANE), _row)],
        out_shape=[jax.ShapeDtypeStruct((S, 768), MXU_DTYPE), jax.ShapeDtypeStruct((S, SSM_W), F32),
                   jax.ShapeDtypeStruct((S, XBC_W), F32), jax.ShapeDtypeStruct((S, LANE), F32)],
        compiler_params=_cparams(VMEM_BIG),
    )(x, norm1, scale1, shift1, w_in)


def _in_proj_bwd(x, dx1, dq, dkv, dz, dxbc, ddt, norm1, scale1, shift1, w_in, tm):
    S = x.shape[0]

    def body(x_ref, dx1_ref, dq_ref, dkv_ref, dz_ref, dxbc_ref, ddt_ref, n_ref, sc_ref, sh_ref, w_ref,
             gx_ref, h_ref, dp_ref, acc_ref):
        @pl.when(pl.program_id(0) == 0)
        def _():
            acc_ref[...] = jnp.zeros_like(acc_ref)

        h, vjp = jax.vjp(_modnorm, x_ref[...], n_ref[...], sc_ref[...], sh_ref[...])
        dp = jnp.concatenate([dq_ref[...].astype(MXU_DTYPE), dkv_ref[...].astype(MXU_DTYPE),
                              dz_ref[...].astype(MXU_DTYPE), dxbc_ref[...].astype(MXU_DTYPE),
                              ddt_ref[...].astype(MXU_DTYPE)], axis=1)
        dh = _mm(dp, w_ref[...])
        dx, dn, dsc, dsh = vjp(dh)
        gx_ref[...] = dx1_ref[...] + dx
        h_ref[...] = h.astype(h_ref.dtype)
        dp_ref[...] = dp
        acc_ref[0:1, :] += dn
        acc_ref[1:2, :] += dsc
        acc_ref[2:3, :] += dsh

    vec = pl.BlockSpec((1, D_MODEL), _fixed)
    return pl.pallas_call(
        body, name="in_proj_bwd", grid=(S // tm,),
        in_specs=[pl.BlockSpec((tm, D_MODEL), _row), pl.BlockSpec((tm, D_MODEL), _row),
                  pl.BlockSpec((tm, ATTN_W), _row), pl.BlockSpec((tm, 2 * KV_W), _row),
                  pl.BlockSpec((tm, SSM_W), _row), pl.BlockSpec((tm, XBC_W), _row), pl.BlockSpec((tm, LANE), _row),
                  vec, vec, vec, pl.BlockSpec((IN_PAD, D_MODEL), _fixed)],
        out_specs=[pl.BlockSpec((tm, D_MODEL), _row), pl.BlockSpec((tm, D_MODEL), _row),
                   pl.BlockSpec((tm, IN_PAD), _row), pl.BlockSpec((8, D_MODEL), _fixed)],
        out_shape=[jax.ShapeDtypeStruct((S, D_MODEL), F32), jax.ShapeDtypeStruct((S, D_MODEL), MXU_DTYPE),
                   jax.ShapeDtypeStruct((S, IN_PAD), MXU_DTYPE), jax.ShapeDtypeStruct((8, D_MODEL), F32)],
        compiler_params=_cparams(VMEM_BIG),
    )(x, dx1, dq, dkv, dz, dxbc, ddt, norm1, scale1, shift1, w_in)


def _out_stage(ya, ys0, ys1, z0, z1, an, sn0, sn1):
    half = SSM_W // 2
    a = _rms(ya, an, ATTN_W)
    g0 = _rms(ys0 * _silu(z0), sn0, half)
    g1 = _rms(ys1 * _silu(z1), sn1, half)
    return jnp.concatenate([a, g0, g1], axis=1)


def _out_stage_args(ya_ref, ys_ref, z_ref, an_ref, sn_ref):
    half = SSM_W // 2
    return (ya_ref[...], ys_ref[:, :half], ys_ref[:, half:], z_ref[:, :half], z_ref[:, half:],
            an_ref[...], sn_ref[:, :half], sn_ref[:, half:])


def _out_proj_fwd(x, ya, ys, z, an, sn, gate1, w_o, tm):
    S = x.shape[0]

    def body(x_ref, ya_ref, ys_ref, z_ref, an_ref, sn_ref, g_ref, w_ref, x1_ref):
        u = _out_stage(*_out_stage_args(ya_ref, ys_ref, z_ref, an_ref, sn_ref))
        x1_ref[...] = x_ref[...] + g_ref[...] * _mm(u, w_ref[...])

    half = pl.BlockSpec((tm, ATTN_W), _row)
    hvec = pl.BlockSpec((1, ATTN_W), _fixed)
    return pl.pallas_call(
        body, name="out_proj_fwd", grid=(S // tm,),
        in_specs=[pl.BlockSpec((tm, D_MODEL), _row), half, half, half, hvec, hvec,
                  pl.BlockSpec((1, D_MODEL), _fixed), pl.BlockSpec((D_MODEL, D_MODEL), _fixed)],
        out_specs=pl.BlockSpec((tm, D_MODEL), _row),
        out_shape=jax.ShapeDtypeStruct((S, D_MODEL), F32),
        compiler_params=_cparams(VMEM_BIG),
    )(x, ya, ys, z, an, sn, gate1, w_o)


def _out_proj_bwd(dx1, ya, ys, z, an, sn, gate1, w_o, tm):
    S = dx1.shape[0]

    def body(dx1_ref, ya_ref, ys_ref, z_ref, an_ref, sn_ref, g_ref, w_ref,
             dya_ref, dys_ref, dz_ref, u_ref, dmix_ref, acc_ref):
        @pl.when(pl.program_id(0) == 0)
        def _():
            acc_ref[...] = jnp.zeros_like(acc_ref)

        u, vjp = jax.vjp(_out_stage, *_out_stage_args(ya_ref, ys_ref, z_ref, an_ref, sn_ref))
        dx1 = dx1_ref[...]
        mix = _mm(u, w_ref[...])
        dmix = dx1 * g_ref[...]
        du = _mm_nt(dmix, w_ref[...])
        dya, dys0, dys1, dz0, dz1, dan, dsn0, dsn1 = vjp(du)
        dya_ref[...] = dya
        dys_ref[...] = jnp.concatenate([dys0, dys1], axis=1)
        dz_ref[...] = jnp.concatenate([dz0, dz1], axis=1)
        u_ref[...] = u.astype(u_ref.dtype)
        dmix_ref[...] = dmix.astype(dmix_ref.dtype)
        acc_ref[0:1, :] += jnp.sum(dx1 * mix, axis=0, keepdims=True)
        acc_ref[1:2, :] += jnp.concatenate([dan, dsn0, dsn1], axis=1)

    half = pl.BlockSpec((tm, ATTN_W), _row)
    hvec = pl.BlockSpec((1, ATTN_W), _fixed)
    full = pl.BlockSpec((tm, D_MODEL), _row)
    return pl.pallas_call(
        body, name="out_proj_bwd", grid=(S // tm,),
        in_specs=[full, half, half, half, hvec, hvec,
                  pl.BlockSpec((1, D_MODEL), _fixed), pl.BlockSpec((D_MODEL, D_MODEL), _fixed)],
        out_specs=[half, half, half, full, full, pl.BlockSpec((8, D_MODEL), _fixed)],
        out_shape=[jax.ShapeDtypeStruct((S, ATTN_W), F32)] * 3
        + [jax.ShapeDtypeStruct((S, D_MODEL), MXU_DTYPE)] * 2 + [jax.ShapeDtypeStruct((8, D_MODEL), F32)],
        compiler_params=_cparams(VMEM_BIG),
    )(dx1, ya, ys, z, an, sn, gate1, w_o)


def _loss_rows(x2, fn, tgt):
    y = _rms(x2, fn, D_MODEL)
    per_row = jnp.sum(jnp.square(y - tgt), axis=1, keepdims=True)
    return jnp.sum(per_row, axis=0, keepdims=True) * (0.5 / D_MODEL)


def _mlp_loss(x1, tgt, norm2, scale2, shift2, gate2, fnorm, w_gu, w_d, tm):
    S = x1.shape[0]

    def body(x1_ref, t_ref, n_ref, sc_ref, sh_ref, g_ref, fn_ref, wgu_hbm, wd_hbm,
             dx1_ref, h_ref, dgu_ref, act_ref, dmlp_ref, acc_ref, wgu, wd):
        @pl.when(pl.program_id(0) == 0)
        def _():
            acc_ref[...] = jnp.zeros_like(acc_ref)
            pltpu.sync_copy(wgu_hbm, wgu)
            pltpu.sync_copy(wd_hbm, wd)

        x1 = x1_ref[...]
        gate2 = g_ref[...]
        h, vjp_h = jax.vjp(_modnorm, x1, n_ref[...], sc_ref[...], sh_ref[...])
        hb = h.astype(MXU_DTYPE)
        gu = _mm_nt(hb, wgu[...])
        g, u = gu[:, :D_FF], gu[:, D_FF:]
        sg = jax.nn.sigmoid(g)
        silu_g = g * sg
        act = (silu_g * u).astype(MXU_DTYPE)
        mlp = _mm(act, wd[...])
        x2 = x1 + gate2 * mlp
        loss, vjp_loss = jax.vjp(_loss_rows, x2, fn_ref[...], t_ref[...])
        dx2, dfn, _ = vjp_loss(jnp.ones((1, 1), F32))
        dmlp = (dx2 * gate2).astype(MXU_DTYPE)
        dact = _mm_nt(dmlp, wd[...])
        dg = dact * u * (sg * (1.0 + g * (1.0 - sg)))
        du = dact * silu_g
        dgu = jnp.concatenate([dg, du], axis=1).astype(MXU_DTYPE)
        dh = _mm(dgu, wgu[...])
        dx, dn, dsc, dsh = vjp_h(dh)
        dx1_ref[...] = dx2 + dx
        h_ref[...] = hb
        dgu_ref[...] = dgu
        act_ref[...] = act
        dmlp_ref[...] = dmlp
        acc_ref[0:1, :] += dn
        acc_ref[1:2, :] += dsc
        acc_ref[2:3, :] += dsh
        acc_ref[3:4, :] += jnp.sum(dx2 * mlp, axis=0, keepdims=True)
        acc_ref[4:5, :] += dfn
        acc_ref[5:6, :] += jnp.broadcast_to(loss, (1, D_MODEL))

    full = pl.BlockSpec((tm, D_MODEL), _row)
    vec = pl.BlockSpec((1, D_MODEL), _fixed)
    anyspec = pl.BlockSpec(memory_space=pl.ANY)
    return pl.pallas_call(
        body, name="mlp_loss", grid=(S // tm,),
        in_specs=[full, full, vec, vec, vec, vec, vec, anyspec, anyspec],
        out_specs=[full, full, pl.BlockSpec((tm, 2 * D_FF), _row), pl.BlockSpec((tm, D_FF), _row), full,
                   pl.BlockSpec((8, D_MODEL), _fixed)],
        out_shape=[jax.ShapeDtypeStruct((S, D_MODEL), F32), jax.ShapeDtypeStruct((S, D_MODEL), MXU_DTYPE),
                   jax.ShapeDtypeStruct((S, 2 * D_FF), MXU_DTYPE), jax.ShapeDtypeStruct((S, D_FF), MXU_DTYPE),
                   jax.ShapeDtypeStruct((S, D_MODEL), MXU_DTYPE), jax.ShapeDtypeStruct((8, D_MODEL), F32)],
        scratch_shapes=[pltpu.VMEM((2 * D_FF, D_MODEL), MXU_DTYPE), pltpu.VMEM((D_FF, D_MODEL), MXU_DTYPE)],
        compiler_params=_cparams(VMEM_BIG),
    )(x1, tgt, norm2, scale2, shift2, gate2, fnorm, w_gu, w_d)


def _wgrad(a, g, tk, ts, name):
    S, K = a.shape
    N = g.shape[1]
    ns = S // ts

    def body(a_ref, g_ref, o_ref, acc_ref):
        s = pl.program_id(1)

        @pl.when(s == 0)
        def _():
            acc_ref[...] = jnp.zeros_like(acc_ref)

        acc_ref[...] += _mm_tn(a_ref[...], g_ref[...])

        @pl.when(s == ns - 1)
        def _():
            o_ref[...] = acc_ref[...].astype(o_ref.dtype)

    return pl.pallas_call(
        body, name=name, grid=(K // tk, ns),
        in_specs=[pl.BlockSpec((ts, tk), lambda j, s: (s, j)), pl.BlockSpec((ts, N), lambda j, s: (s, 0))],
        out_specs=pl.BlockSpec((tk, N), lambda j, s: (j, 0)),
        out_shape=jax.ShapeDtypeStruct((K, N), WIRE_DTYPE),
        scratch_shapes=[pltpu.VMEM((tk, N), F32)],
        compiler_params=_cparams(VMEM_BIG),
    )(a, g)


def _attn_heads(q_heads, k_pads, v_pads, bias_heads, sink_cols, mask):
    outs = []
    for h in range(N_HEADS):
        kv = h // 4
        s = _mm_nt(q_heads[h], k_pads[kv]) * (HALF ** -0.5) + bias_heads[h]
        s = jnp.where(mask, s, -1e30)
        m = lax.stop_gradient(jnp.maximum(jnp.max(s, axis=-1, keepdims=True), sink_cols[h]))
        p = jnp.exp(s - m)
        den = jnp.sum(p, axis=-1, keepdims=True) + jnp.exp(sink_cols[h] - m)
        outs.append(_mm(p, v_pads[kv]) / den)
    return tuple(outs)


def _attn_mask(first):
    i = lax.broadcasted_iota(jnp.int32, (BLK, 2 * BLK), 0)
    j = lax.broadcasted_iota(jnp.int32, (BLK, 2 * BLK), 1)
    return (j > i) & (j <= i + BLK) & (j >= jnp.where(first, BLK, 0))


def _attn_operands(q_ref, kvp_ref, kvc_ref, bias_ref, sinks_ref):
    q_heads = _split_heads(q_ref[...].astype(F32), 4)
    kvp = kvp_ref[...].astype(F32)
    kvc = kvc_ref[...].astype(F32)
    kp, kc = _split_pair(kvp[:, :LANE]), _split_pair(kvc[:, :LANE])
    vp, vc = _split_pair(kvp[:, LANE:]), _split_pair(kvc[:, LANE:])
    k_pads = [jnp.concatenate([kp[g], kc[g]], axis=0) for g in range(2)]
    v_pads = [jnp.concatenate([vp[g], vc[g]], axis=0) for g in range(2)]
    bias_heads = [bias_ref[h] for h in range(N_HEADS)]
    sink_cols = [jnp.full((BLK, 1), sinks_ref[h], F32) for h in range(N_HEADS)]
    return q_heads, k_pads, v_pads, bias_heads, sink_cols


def _build_bias(bucket_ref, relb_ref, bias_ref):
    bk = bucket_ref[...]
    for h in range(N_HEADS):
        acc = jnp.zeros((BLK, 2 * BLK), F32)
        for b in range(N_BUCKETS):
            acc = jnp.where(bk == b, relb_ref[b, h], acc)
        bias_ref[h] = acc


def _attn_fwd(qkv, buckets, rel_bias, sinks, xchg):
    S = qkv.shape[0]
    nb = S // BLK

    def body(q_ref, kvp_ref, kvc_ref, bk_ref, relb_ref, sinks_ref, y_ref, bias_ref):
        i = pl.program_id(0)

        @pl.when(i == 0)
        def _():
            _build_bias(bk_ref, relb_ref, bias_ref)

        ops = _attn_operands(q_ref, kvp_ref, kvc_ref, bias_ref, sinks_ref)
        outs = _attn_heads(*ops, mask=_attn_mask(i == 0))
        y_ref[...] = _join_heads(outs)

    smem = pl.BlockSpec(memory_space=pltpu.SMEM)
    return _hosted_call(
        body, "attn_fwd", nb,
        in_specs=[pl.BlockSpec((BLK, ATTN_W), _row),
                  pl.BlockSpec((BLK, 2 * KV_W), lambda i: (jnp.maximum(i - 1, 0), 2)),
                  pl.BlockSpec((BLK, 2 * KV_W), lambda i: (i, 2)),
                  pl.BlockSpec((BLK, 2 * BLK), _fixed), smem, smem],
        out_specs=[pl.BlockSpec((BLK, ATTN_W), _row)],
        out_shape=[jax.ShapeDtypeStruct((S, ATTN_W), F32)],
        scratch_shapes=[pltpu.VMEM((N_HEADS, BLK, 2 * BLK), F32)],
        args=(qkv, qkv, qkv, buckets, rel_bias, sinks), xchg=xchg, cparams=_cparams(),
    )


def _attn_bwd(qkv, dy, buckets, rel_bias, sinks, xchg):
    S = qkv.shape[0]
    nb = S // BLK

    def body(q_ref, kvp_ref, kvc_ref, dy_ref, bk_ref, relb_ref, sinks_ref,
             dq_ref, dkv_ref, dsink_ref, drel_ref, bias_ref, dbias_ref, carry_ref):
        i = pl.program_id(0)
        blk = nb - 1 - i

        @pl.when(i == 0)
        def _():
            _build_bias(bk_ref, relb_ref, bias_ref)
            dbias_ref[...] = jnp.zeros_like(dbias_ref)
            carry_ref[...] = jnp.zeros_like(carry_ref)
            dsink_ref[...] = jnp.zeros_like(dsink_ref)
            drel_ref[...] = jnp.zeros_like(drel_ref)

        ops = _attn_operands(q_ref, kvp_ref, kvc_ref, bias_ref, sinks_ref)
        _, vjp = jax.vjp(functools.partial(_attn_heads, mask=_attn_mask(blk == 0)), *ops)
        dq_heads, dk_pads, dv_pads, dbias_heads, dsink_cols = vjp(tuple(_split_heads(dy_ref[...], 4)))
        dq_ref[...] = _join_heads(dq_heads)
        dk_prev = _join_pair(dk_pads[0][:BLK], dk_pads[1][:BLK])
        dk_cur = _join_pair(dk_pads[0][BLK:], dk_pads[1][BLK:])
        dv_prev = _join_pair(dv_pads[0][:BLK], dv_pads[1][:BLK])
        dv_cur = _join_pair(dv_pads[0][BLK:], dv_pads[1][BLK:])
        dkv_ref[...] = jnp.concatenate([dk_cur, dv_cur], axis=1) + carry_ref[...]
        carry_ref[...] = jnp.concatenate([dk_prev, dv_prev], axis=1)
        row = lax.broadcasted_iota(jnp.int32, (N_HEADS, LANE), 0)
        dsink = jnp.zeros((N_HEADS, LANE), F32)
        for h in range(N_HEADS):
            dbias_ref[h] += dbias_heads[h]
            dsink = dsink + jnp.where(row == h, jnp.sum(dsink_cols[h], axis=0, keepdims=True), 0.0)
        dsink_ref[...] += dsink

        @pl.when(i == nb - 1)
        def _():
            bk = bk_ref[...]
            r = lax.broadcasted_iota(jnp.int32, (N_BUCKETS, LANE), 0)
            l = lax.broadcasted_iota(jnp.int32, (N_BUCKETS, LANE), 1)
            res = jnp.zeros((N_BUCKETS, LANE), F32)
            for h in range(N_HEADS):
                db = dbias_ref[h]
                for b in range(N_BUCKETS):
                    v = jnp.sum(jnp.sum(jnp.where(bk == b, db, 0.0), axis=1, keepdims=True), axis=0, keepdims=True)
                    res = res + jnp.where((r == b) & (l == h), v, 0.0)
            drel_ref[...] = res

    smem = pl.BlockSpec(memory_space=pltpu.SMEM)
    rev = lambda i: (nb - 1 - i, 0)
    return _hosted_call(
        body, "attn_bwd", nb,
        in_specs=[pl.BlockSpec((BLK, ATTN_W), rev),
                  pl.BlockSpec((BLK, 2 * KV_W), lambda i: (jnp.maximum(nb - 2 - i, 0), 2)),
                  pl.BlockSpec((BLK, 2 * KV_W), lambda i: (nb - 1 - i, 2)),
                  pl.BlockSpec((BLK, ATTN_W), rev),
                  pl.BlockSpec((BLK, 2 * BLK), _fixed), smem, smem],
        out_specs=[pl.BlockSpec((BLK, ATTN_W), rev), pl.BlockSpec((BLK, 2 * KV_W), rev),
                   pl.BlockSpec((N_HEADS, LANE), _fixed), pl.BlockSpec((N_BUCKETS, LANE), _fixed)],
        out_shape=[jax.ShapeDtypeStruct((S, ATTN_W), F32), jax.ShapeDtypeStruct((S, 2 * KV_W), F32),
                   jax.ShapeDtypeStruct((N_HEADS, LANE), F32), jax.ShapeDtypeStruct((N_BUCKETS, LANE), F32)],
        scratch_shapes=[pltpu.VMEM((N_HEADS, BLK, 2 * BLK), F32), pltpu.VMEM((N_HEADS, BLK, 2 * BLK), F32),
                        pltpu.VMEM((BLK, 2 * KV_W), F32)],
        args=(qkv, qkv, qkv, dy, buckets, rel_bias, sinks), xchg=xchg, cparams=_cparams(),
    )


def _ssd_consts():
    r = lax.broadcasted_iota(jnp.int32, (BLK, BLK), 0)
    c = lax.broadcasted_iota(jnp.int32, (BLK, BLK), 1)
    causal = c <= r
    upper = (r <= c).astype(F32)
    last = r == BLK - 1
    head = lax.broadcasted_iota(jnp.int32, (N_HEADS, BLK), 0)
    return causal, upper, last, head


def _ssd_chunk(xs, bg, cg, dt_raw_t, prev, dtb, alog, d_rows, consts):
    causal, upper, last, head = consts
    dt_t = _softplus(dt_raw_t + dtb)
    acs_t = _mm_hi(dt_t * (-jnp.exp(alog)), upper)
    cb = [_mm_nt(cg[g], bg[g]) for g in range(2)]
    ys, hs = [], []
    for h in range(N_HEADS):
        g = h // 4
        dt_row = jnp.sum(jnp.where(head == h, dt_t, 0.0), axis=0, keepdims=True)
        a_row = jnp.sum(jnp.where(head == h, acs_t, 0.0), axis=0, keepdims=True)
        a_rb = jnp.broadcast_to(a_row, (BLK, BLK))
        a_b = a_rb.T
        a_last = jnp.sum(jnp.where(last, a_b, 0.0), axis=0, keepdims=True)
        w = cb[g] * jnp.exp(jnp.where(causal, a_b - a_rb, -1e30)) * dt_row
        f_b = jnp.broadcast_to(dt_row * jnp.exp(a_last - a_row), (BLK, BLK)).T
        y = _mm(w, xs[h]) + _mm(cg[g], prev[h]) * jnp.exp(a_b) + d_rows[h] * xs[h]
        st = _mm_tn(bg[g], xs[h] * f_b)
        ys.append(y)
        hs.append(prev[h] * jnp.exp(a_last) + st)
    return tuple(ys), tuple(hs)


def _dt_rows(dt_blk):
    return dt_blk.T[:N_HEADS]


def _silu_grad(x):
    s = jax.nn.sigmoid(x)
    return s * (1.0 + x * (1.0 - s))


def _conv_pre(ext_ref, halo, blk, cw_ref, cb_ref):
    ext_ref[0:8, :] = halo
    ext_ref[8:8 + BLK, :] = blk
    pre = cb_ref[...] + cw_ref[0:1, :] * ext_ref[pl.ds(5, BLK), :]
    for k in range(1, 4):
        pre = pre + cw_ref[k:k + 1, :] * ext_ref[pl.ds(5 + k, BLK), :]
    return pre


def _ssd_split(pre):
    heads = _split_heads(pre[:, :SSM_W], 4)
    pb = [pre[:, SSM_W + g * D_STATE:SSM_W + (g + 1) * D_STATE] for g in range(2)]
    pc = [pre[:, SSM_W + 2 * D_STATE + g * D_STATE:SSM_W + 2 * D_STATE + (g + 1) * D_STATE] for g in range(2)]
    return heads, pb, pc


def _ssd_fwd(xbc, dt_raw, conv_w, conv_b, dtb_row, alog_row, d_exp, xchg):
    S = xbc.shape[0]
    nc = S // BLK

    def body(xbc_ref, halo_ref, dt_ref, cw_ref, cb_ref, dtb_ref, alog_ref, d_ref, y_ref, prev_ref, ext_ref, state_ref):
        i = pl.program_id(0)

        @pl.when(i == 0)
        def _():
            state_ref[...] = jnp.zeros_like(state_ref)

        halo = halo_ref[...] * jnp.where(i > 0, 1.0, 0.0)
        pre = _conv_pre(ext_ref, halo, xbc_ref[...], cw_ref, cb_ref)
        heads, pb, pc = _ssd_split(_silu(pre))
        prev = [state_ref[h] for h in range(N_HEADS)]
        for h in range(N_HEADS):
            prev_ref[0, h] = prev[h]
        d_rows = [d_ref[h:h + 1, :] for h in range(N_HEADS)]
        ys, hs = _ssd_chunk(heads, pb, pc, _dt_rows(dt_ref[...]), prev, dtb_ref[...], alog_ref[...], d_rows,
                            _ssd_consts())
        for h in range(N_HEADS):
            state_ref[h] = hs[h]
        y_ref[...] = _join_heads(ys)

    vec = pl.BlockSpec((N_HEADS, LANE), _fixed)
    return _hosted_call(
        body, "ssd_fwd", nc,
        in_specs=[pl.BlockSpec((BLK, XBC_W), _row),
                  pl.BlockSpec((8, XBC_W), lambda i: (jnp.maximum(i * (BLK // 8) - 1, 0), 0)),
                  pl.BlockSpec((BLK, LANE), _row),
                  pl.BlockSpec((4, XBC_W), _fixed), pl.BlockSpec((1, XBC_W), _fixed), vec, vec,
                  pl.BlockSpec((N_HEADS, LANE), _fixed)],
        out_specs=[pl.BlockSpec((BLK, SSM_W), _row),
                   pl.BlockSpec((1, N_HEADS, D_STATE, LANE), lambda i: (i, 0, 0, 0))],
        out_shape=[jax.ShapeDtypeStruct((S, SSM_W), F32), jax.ShapeDtypeStruct((nc, N_HEADS, D_STATE, LANE), F32)],
        scratch_shapes=[pltpu.VMEM((8 + BLK, XBC_W), F32), pltpu.VMEM((N_HEADS, D_STATE, LANE), F32)],
        args=(xbc, xbc, dt_raw, conv_w, conv_b, dtb_row, alog_row, d_exp), xchg=xchg, cparams=_cparams(),
    )


def _ssd_bwd(xbc, dt_raw, prev_states, dy, conv_w, conv_b, dtb_row, alog_row, d_exp, xchg):
    S = xbc.shape[0]
    nc = S // BLK

    def body(xbc_ref, halo_ref, dt_ref, prev_ref, dy_ref, cw_ref, cb_ref, dtb_ref, alog_ref, d_ref,
             dxbc_ref, ddt_ref, dcw_ref, dvec_ref, dd_ref, ext_ref, dpe_ref, gstate_ref, ghalo_ref):
        i = pl.program_id(0)
        c = nc - 1 - i

        @pl.when(i == 0)
        def _():
            gstate_ref[...] = jnp.zeros_like(gstate_ref)
            ghalo_ref[...] = jnp.zeros_like(ghalo_ref)
            dcw_ref[...] = jnp.zeros_like(dcw_ref)
            dvec_ref[...] = jnp.zeros_like(dvec_ref)
            dd_ref[...] = jnp.zeros_like(dd_ref)
            dpe_ref[...] = jnp.zeros_like(dpe_ref)

        halo = halo_ref[...] * jnp.where(c > 0, 1.0, 0.0)
        pre = _conv_pre(ext_ref, halo, xbc_ref[...], cw_ref, cb_ref)
        heads, pb, pc = _ssd_split(_silu(pre))
        prev = [prev_ref[0, h] for h in range(N_HEADS)]
        d_rows = [d_ref[h:h + 1, :] for h in range(N_HEADS)]
        _, vjp = jax.vjp(functools.partial(_ssd_chunk, consts=_ssd_consts()),
                         heads, pb, pc, _dt_rows(dt_ref[...]), prev, dtb_ref[...], alog_ref[...], d_rows)
        dys = tuple(_split_heads(dy_ref[...], 4))
        dhs = tuple(gstate_ref[h] for h in range(N_HEADS))
        dheads, dpb, dpc, ddt_t, dprev, ddtb, dalog, dd_rows = vjp((dys, dhs))
        for h in range(N_HEADS):
            gstate_ref[h] = dprev[h]
            dd_ref[h:h + 1, :] += dd_rows[h]
        ddt_ref[...] = jnp.concatenate([ddt_t, jnp.zeros((BLK - N_HEADS, BLK), F32)], axis=0).T
        dvec_ref[0:N_HEADS, :] += ddtb
        dvec_ref[N_HEADS:, :] += dalog
        dpre = jnp.concatenate([_join_heads(dheads)] + list(dpb) + list(dpc), axis=1) * _silu_grad(pre)
        dpe_ref[8:8 + BLK, :] = dpre
        dext = cw_ref[0:1, :] * dpe_ref[pl.ds(3, 8 + BLK), :]
        dcw_ref[0:1, :] += jnp.sum(dpre * ext_ref[pl.ds(5, BLK), :], axis=0, keepdims=True)
        for k in range(1, 4):
            dext = dext + cw_ref[k:k + 1, :] * dpe_ref[pl.ds(3 - k, 8 + BLK), :]
            dcw_ref[k:k + 1, :] += jnp.sum(dpre * ext_ref[pl.ds(5 + k, BLK), :], axis=0, keepdims=True)
        dcw_ref[4:5, :] += jnp.sum(dpre, axis=0, keepdims=True)
        dxbc_ref[...] = dext[8:, :]
        dxbc_ref[BLK - 8:BLK, :] += ghalo_ref[...]
        ghalo_ref[...] = dext[:8, :]

    vec = pl.BlockSpec((N_HEADS, LANE), _fixed)
    rev = lambda i: (nc - 1 - i, 0)
    return _hosted_call(
        body, "ssd_bwd", nc,
        in_specs=[pl.BlockSpec((BLK, XBC_W), rev),
                  pl.BlockSpec((8, XBC_W), lambda i: (jnp.maximum((nc - 1 - i) * (BLK // 8) - 1, 0), 0)),
                  pl.BlockSpec((BLK, LANE), rev),
                  pl.BlockSpec((1, N_HEADS, D_STATE, LANE), lambda i: (nc - 1 - i, 0, 0, 0)),
                  pl.BlockSpec((BLK, SSM_W), rev),
                  pl.BlockSpec((4, XBC_W), _fixed), pl.BlockSpec((1, XBC_W), _fixed), vec, vec,
                  pl.BlockSpec((N_HEADS, LANE), _fixed)],
        out_specs=[pl.BlockSpec((BLK, XBC_W), rev), pl.BlockSpec((BLK, LANE), rev),
                   pl.BlockSpec((8, XBC_W), _fixed), pl.BlockSpec((2 * N_HEADS, LANE), _fixed),
                   pl.BlockSpec((N_HEADS, LANE), _fixed)],
        out_shape=[jax.ShapeDtypeStruct((S, XBC_W), F32), jax.ShapeDtypeStruct((S, LANE), F32),
                   jax.ShapeDtypeStruct((8, XBC_W), F32), jax.ShapeDtypeStruct((2 * N_HEADS, LANE), F32),
                   jax.ShapeDtypeStruct((N_HEADS, LANE), F32)],
        scratch_shapes=[pltpu.VMEM((8 + BLK, XBC_W), F32), pltpu.VMEM((16 + BLK, XBC_W), F32),
                        pltpu.VMEM((N_HEADS, D_STATE, LANE), F32), pltpu.VMEM((8, XBC_W), F32)],
        args=(xbc, xbc, dt_raw, prev_states, dy, conv_w, conv_b, dtb_row, alog_row, d_exp), xchg=xchg,
        cparams=_cparams(VMEM_BIG),
    )


def _adamw_math(w, g, m, v):
    m = ADAM_B1 * m + (1.0 - ADAM_B1) * g
    v = ADAM_B2 * v + (1.0 - ADAM_B2) * jnp.square(g)
    m_hat = m / (1.0 - ADAM_B1 ** ADAM_STEP)
    v_hat = v / (1.0 - ADAM_B2 ** ADAM_STEP)
    delta = -ADAM_LR * (m_hat / (jnp.sqrt(v_hat) + ADAM_EPS) + ADAM_WD * w)
    return delta, m, v


def _reduce_adamw(parts, w, m, v, name):
    R, C = w.shape

    def body(p_ref, w_ref, m_ref, v_ref, g_ref, d_ref, nm_ref, nv_ref):
        g = p_ref[0].astype(F32)
        for i in range(1, N_DEV):
            g = g + p_ref[i].astype(F32)
        d, nm, nv = _adamw_math(w_ref[...], g, m_ref[...], v_ref[...])
        g_ref[...] = g
        d_ref[...] = d
        nm_ref[...] = nm
        nv_ref[...] = nv

    if R % 16 == 0:
        tr = max(t for t in range(16, 257, 16) if R % t == 0)
        n, blk, pblk = R // tr, pl.BlockSpec((tr, C), _row), pl.BlockSpec((N_DEV, tr, C), lambda i: (0, i, 0))
    else:
        tl = 256
        n, blk, pblk = C // tl, pl.BlockSpec((R, tl), lambda i: (0, i)), pl.BlockSpec((N_DEV, R, tl),
                                                                                      lambda i: (0, 0, i))
    return pl.pallas_call(
        body, name=name, grid=(n,), in_specs=[pblk, blk, blk, blk],
        out_specs=[blk] * 4, out_shape=[jax.ShapeDtypeStruct((R, C), F32)] * 4,
    )(parts, w, m, v)


_SMALL_NAMES = ("ada_b", "norm1", "conv_w", "conv_b", "dt_bias", "A_log", "D_skip", "sinks", "attn_out_norm",
                "ssm_out_norm", "norm2", "rel_bias", "final_norm")
N_MOD = 6 * D_MODEL


def _mod_row(a0, a1, a2):
    return jnp.concatenate([a0[2:3], a0[1:2], a1[0:1], a2[2:3], a2[1:2], a2[3:4]], axis=1)


def _small_update(gathered, params):
    n_g = len(gathered)
    flat = [a for name in _SMALL_NAMES for a in params[name]]

    def body(*refs):
        a0_ref, a1_ref, a2_ref, cw_ref, dv_ref, dd_ref, ds_ref, dr_ref, c_ref = refs[:n_g]
        wmv = refs[n_g:n_g + len(flat)]
        outs = refs[n_g + len(flat):]

        def total(ref):
            t = ref[0]
            for i in range(1, N_DEV):
                t = t + ref[i]
            return t

        t0, t1, t2, tcw, tdv, tdd, tds, tdr = [total(r) for r in (a0_ref, a1_ref, a2_ref, cw_ref, dv_ref, dd_ref,
                                                                   ds_ref, dr_ref)]
        r8 = lax.broadcasted_iota(jnp.int32, (N_HEADS, LANE), 0)
        l8 = lax.broadcasted_iota(jnp.int32, (N_HEADS, LANE), 1)

        def diag_row(t):
            return jnp.sum(jnp.where(r8 == l8, t, 0.0), axis=0, keepdims=True)[:, :N_HEADS]

        def lane_sums(t):
            return diag_row(jnp.broadcast_to(jnp.sum(t, axis=1, keepdims=True), (N_HEADS, LANE)))

        me = _lin(_my_pos())
        n_cw = XBC_W // N_DEV
        cw_mine = jnp.zeros((4, n_cw), F32)
        for j in range(N_DEV):
            cw_mine = cw_mine + tcw[0:4, j * n_cw:(j + 1) * n_cw] * jnp.where(me == j, 1.0, 0.0)
        grads = {
            "ada_b": _mod_row(t0, t1, t2), "norm1": t0[0:1], "conv_w": cw_mine, "conv_b": tcw[4:5],
            "dt_bias": lane_sums(tdv[:N_HEADS]), "A_log": lane_sums(tdv[N_HEADS:]), "D_skip": lane_sums(tdd),
            "sinks": diag_row(tds), "attn_out_norm": t1[1:2, :ATTN_W], "ssm_out_norm": t1[1:2, ATTN_W:],
            "norm2": t2[0:1], "rel_bias": tdr[:, :N_HEADS], "final_norm": t2[4:5],
        }
        for k, name in enumerate(_SMALL_NAMES):
            w_ref, m_ref, v_ref = wmv[3 * k:3 * k + 3]
            g = grads[name]
            d, nm, nv = _adamw_math(w_ref[...], g, m_ref[...], v_ref[...])
            for o, val in zip(outs[4 * k:4 * k + 4], (g, d, nm, nv)):
                o[...] = val
        loss_ref, call_ref, dmod_ref = outs[4 * len(_SMALL_NAMES):]
        loss_ref[...] = t2[5:6, 0:1]
        call_ref[...] = jnp.concatenate([c_ref[i] for i in range(N_DEV)], axis=0)
        dmod_ref[...] = jnp.concatenate([_mod_row(a0_ref[i], a1_ref[i], a2_ref[i]) for i in range(N_DEV)], axis=0)

    out_shape = [jax.ShapeDtypeStruct(params[name][0].shape, F32) for name in _SMALL_NAMES for _ in range(4)]
    out_shape += [jax.ShapeDtypeStruct((1, 1), F32), jax.ShapeDtypeStruct((N_DEV, D_MODEL), F32),
                  jax.ShapeDtypeStruct((N_DEV, N_MOD), F32)]
    res = pl.pallas_call(body, name="small_update", out_shape=out_shape)(*gathered, *flat)
    upd = {name: res[4 * k:4 * k + 4] for k, name in enumerate(_SMALL_NAMES)}
    loss, c_all, dmod_all = res[4 * len(_SMALL_NAMES):]
    return upd, loss, c_all, dmod_all


def _ada_w_update(c_all, dmod_all, w, m, v):
    chunk = w.shape[1]

    def body(c_ref, dm_ref, w_ref, m_ref, v_ref, g_ref, d_ref, nm_ref, nv_ref):
        me = _lin(_my_pos())
        dm = jnp.zeros((N_DEV, chunk), F32)
        for j in range(N_DEV):
            dm = dm + dm_ref[:, j * chunk:(j + 1) * chunk] * jnp.where(me == j, 1.0, 0.0)
        g = lax.dot_general(_silu(c_ref[...]), dm, (((0,), (0,)), ((), ())), precision=HI,
                            preferred_element_type=F32)
        d, nm, nv = _adamw_math(w_ref[...], g, m_ref[...], v_ref[...])
        g_ref[...] = g
        d_ref[...] = d
        nm_ref[...] = nm
        nv_ref[...] = nv

    return pl.pallas_call(body, name="ada_w_update", out_shape=[jax.ShapeDtypeStruct(w.shape, F32)] * 4,
                          compiler_params=_cparams(VMEM_BIG))(c_all, dmod_all, w, m, v)


def _local_step(x, tgt, mod, w_in, conv_w, w_o_mine, w_gu_mine, w_d_mine, p):
    S = x.shape[0]
    tm = min(512, S)
    tmm = min(256, S)
    shift1, scale1, gate1, shift2, scale2, gate2 = [mod[i:i + 1] for i in range(6)]
    buckets = jnp.asarray(_t5_bucket_table())
    per_head = lambda a: jnp.broadcast_to(a.reshape(N_HEADS, 1), (N_HEADS, LANE))
    dtb_row, alog_row, d_exp = per_head(p["dt_bias"]), per_head(p["A_log"]), per_head(p["D_skip"])
    sinks = p["sinks"].reshape(N_HEADS)

    qkv, z, xbc, dt_raw = _in_proj_fwd(x, p["norm1"], scale1, shift1, w_in, tm)
    (ya,), (g_d,) = _attn_fwd(qkv, buckets, p["rel_bias"], sinks, ([w_d_mine], False))
    (ys, prev_states), (g_gu, g_o) = _ssd_fwd(xbc, dt_raw, conv_w, p["conv_b"], dtb_row, alog_row, d_exp,
                                              ([w_gu_mine, w_o_mine], False))
    w_gu = g_gu.reshape(2 * D_FF, D_MODEL)
    w_o = g_o.reshape(D_MODEL, D_MODEL)
    w_d = g_d.reshape(D_FF, D_MODEL)
    x1 = _out_proj_fwd(x, ya, ys, z, p["attn_out_norm"], p["ssm_out_norm"], gate1, w_o, tm)
    dx1, h2, dgu, act, dmlp, acc2 = _mlp_loss(x1, tgt, p["norm2"], scale2, shift2, gate2, p["final_norm"],
                                              w_gu, w_d, tmm)
    g_w_gu = _wgrad(dgu, h2, 2 * D_FF // 4, tm, "wgrad_gate_up")
    g_w_d = _wgrad(act, dmlp, D_FF // 2, tm, "wgrad_down")
    dya, dys, dz, u, dmix, acc1 = _out_proj_bwd(dx1, ya, ys, z, p["attn_out_norm"], p["ssm_out_norm"], gate1, w_o, tm)
    g_w_o = _wgrad(u, dmix, D_MODEL, tm, "wgrad_out")
    (dq, dkv, dsink, drel), (r_gu,) = _attn_bwd(qkv, dya, buckets, p["rel_bias"], sinks,
                                                ([g_w_gu.reshape(N_DEV, 2 * D_FF // N_DEV, D_MODEL)], True))
    (dxbc, ddt, dcw, dvec, dd), (r_d, r_o) = _ssd_bwd(
        xbc, dt_raw, prev_states, dys, conv_w, p["conv_b"], dtb_row, alog_row, d_exp,
        ([g_w_d.reshape(N_DEV, D_FF // N_DEV, D_MODEL), g_w_o.reshape(N_DEV, D_MODEL // N_DEV, D_MODEL)], True))
    gx, h1, dproj, acc0 = _in_proj_bwd(x, dx1, dq, dkv, dz, dxbc, ddt, p["norm1"], scale1, shift1, w_in, tm)
    g_w_in = _wgrad(dproj, h1, IN_PAD, tm, "wgrad_in")
    return gx, g_w_in, (r_o, r_gu, r_d), (acc0, acc1, acc2, dcw, dvec, dd, dsink, drel)


def kernel(x, c, ada_w, ada_b, norm1, w_in, conv_w, conv_b, dt_bias, A_log, D_skip, sinks, attn_out_norm, ssm_out_norm, w_o, norm2, w_gate_up, w_down, rel_bias, final_norm, loss_target, m_ada_w, m_ada_b, m_norm1, m_w_in, m_conv_w, m_conv_b, m_dt_bias, m_A_log, m_D_skip, m_sinks, m_attn_out_norm, m_ssm_out_norm, m_w_o, m_norm2, m_w_gate_up, m_w_down, m_rel_bias, m_final_norm, v_ada_w, v_ada_b, v_norm1, v_w_in, v_conv_w, v_conv_b, v_dt_bias, v_A_log, v_D_skip, v_sinks, v_attn_out_norm, v_ssm_out_norm, v_w_o, v_norm2, v_w_gate_up, v_w_down, v_rel_bias, v_final_norm):
    two_d = lambda a: a if a.ndim == 2 else a.reshape(-1, a.shape[-1])
    small_params = dict(
        ada_b=(ada_b, m_ada_b, v_ada_b), norm1=(norm1, m_norm1, v_norm1), conv_w=(conv_w, m_conv_w, v_conv_w),
        conv_b=(conv_b, m_conv_b, v_conv_b), dt_bias=(dt_bias, m_dt_bias, v_dt_bias), A_log=(A_log, m_A_log, v_A_log),
        D_skip=(D_skip, m_D_skip, v_D_skip), sinks=(sinks, m_sinks, v_sinks),
        attn_out_norm=(attn_out_norm, m_attn_out_norm, v_attn_out_norm),
        ssm_out_norm=(ssm_out_norm, m_ssm_out_norm, v_ssm_out_norm), norm2=(norm2, m_norm2, v_norm2),
        rel_bias=(rel_bias, m_rel_bias, v_rel_bias), final_norm=(final_norm, m_final_norm, v_final_norm))
    small_params = {k: tuple(two_d(a) for a in v) for k, v in small_params.items()}
    S = x.shape[1]
    xs, tgt = x.reshape(S, D_MODEL), loss_target.reshape(S, D_MODEL)
    ada_w2 = ada_w[0]
    chunk = ada_w2.shape[1]
    t_in = [jnp.transpose(a[0]) for a in (w_in, m_w_in, v_w_in)]
    t_gu = [jnp.transpose(a[0]) for a in (w_gate_up, m_w_gate_up, v_w_gate_up)]

    mod = _mod_exchange(c, ada_w2, ada_b.reshape(N_DEV, chunk)).reshape(6, D_MODEL)

    g_in, g_cw = _exchange([t_in[0].astype(WIRE_DTYPE), conv_w[0]], scatter=False, name="gather_w_in")
    w_in_full = jnp.pad(g_in.reshape(IN_W, D_MODEL), ((0, IN_PAD - IN_W), (0, 0)))
    conv_w_full = jnp.transpose(g_cw, (1, 0, 2)).reshape(4, XBC_W)

    p = {k: v[0] for k, v in small_params.items()}
    gx, gw_in, (r_o, r_gu, r_d), blocks = _local_step(
        xs, tgt, mod, w_in_full, conv_w_full, w_o[0].astype(WIRE_DTYPE), t_gu[0].astype(WIRE_DTYPE),
        w_down[0].astype(WIRE_DTYPE), p)

    (r_in,) = _exchange([gw_in[:IN_W].reshape(N_DEV, IN_W // N_DEV, D_MODEL)], scatter=True, name="scatter_w_in")

    gathered = _exchange(list(blocks) + [c], scatter=False, name="gather_small")
    small, loss, c_all, dmod_all = _small_update(gathered, small_params)

    big = {
        "ada_w": _ada_w_update(c_all, dmod_all, ada_w2, m_ada_w[0], v_ada_w[0]),
        "w_in": [jnp.transpose(a) for a in _reduce_adamw(r_in, *t_in, "adamw_w_in")],
        "w_o": _reduce_adamw(r_o, w_o[0], m_w_o[0], v_w_o[0], "adamw_w_o"),
        "w_gate_up": [jnp.transpose(a) for a in _reduce_adamw(r_gu, *t_gu, "adamw_w_gate_up")],
        "w_down": _reduce_adamw(r_d, w_down[0], m_w_down[0], v_w_down[0], "adamw_w_down"),
    }
    big.update(small)

    order = ['ada_w', 'ada_b', 'norm1', 'w_in', 'conv_w', 'conv_b', 'dt_bias', 'A_log', 'D_skip', 'sinks',
             'attn_out_norm', 'ssm_out_norm', 'w_o', 'norm2', 'w_gate_up', 'w_down', 'rel_bias', 'final_norm']
    shapes = dict(ada_w=ada_w.shape, ada_b=ada_b.shape, norm1=norm1.shape, w_in=w_in.shape, conv_w=conv_w.shape,
                  conv_b=conv_b.shape, dt_bias=dt_bias.shape, A_log=A_log.shape, D_skip=D_skip.shape,
                  sinks=sinks.shape, attn_out_norm=attn_out_norm.shape, ssm_out_norm=ssm_out_norm.shape,
                  w_o=w_o.shape, norm2=norm2.shape, w_gate_up=w_gate_up.shape, w_down=w_down.shape,
                  rel_bias=rel_bias.shape, final_norm=final_norm.shape)
    outs = [[], [], [], []]
    for name in order:
        for kind in range(4):
            outs[kind].append(big[name][kind].reshape(shapes[name]))
    return (loss.reshape(()), gx.reshape(x.shape), *outs[0], *outs[1], *outs[2], *outs[3])
```

```python
import functools

import numpy as np
import jax
import jax.numpy as jnp
from jax import lax
from jax.experimental import pallas as pl
from jax.experimental.pallas import tpu as pltpu

F32 = jnp.float32
MXU_DTYPE = jnp.bfloat16
WIRE_DTYPE = jnp.bfloat16
HI = lax.Precision.HIGHEST
MESH = pl.DeviceIdType.MESH
N_DEV = 8

D_MODEL = 1024
ATTN_W = 512
KV_W = 128
SSM_W = 512
XBC_W = 1024
N_HEADS = 8
D_STATE = 128
D_FF = 2816
IN_W = 2312
IN_PAD = 2432
BLK = 128
N_BUCKETS = 32
EPS = 1e-6
LANE = 128
HALF = 64

ADAM_LR, ADAM_B1, ADAM_B2, ADAM_EPS, ADAM_WD, ADAM_STEP = 0.001, 0.9, 0.999, 1e-08, 0.01, 10

VMEM_BIG = 56 * 1024 * 1024


def _cparams(vmem=None):
    if vmem is None:
        return pltpu.CompilerParams()
    return pltpu.CompilerParams(vmem_limit_bytes=vmem)


def _mm(a, b):
    return jnp.dot(a.astype(MXU_DTYPE), b.astype(MXU_DTYPE), preferred_element_type=F32)


def _mm_nt(a, b):
    return lax.dot_general(a.astype(MXU_DTYPE), b.astype(MXU_DTYPE), (((1,), (1,)), ((), ())),
                           preferred_element_type=F32)


def _mm_tn(a, b):
    return lax.dot_general(a.astype(MXU_DTYPE), b.astype(MXU_DTYPE), (((0,), (0,)), ((), ())),
                           preferred_element_type=F32)


def _mm_hi(a, b):
    return jnp.dot(a, b, precision=HI, preferred_element_type=F32)


def _silu(x):
    return x * jax.nn.sigmoid(x)


def _softplus(x):
    return jnp.maximum(x, 0.0) + jnp.log1p(jnp.exp(-jnp.abs(x)))


def _rms(x, g, n):
    return x * lax.rsqrt(jnp.sum(x * x, axis=-1, keepdims=True) * (1.0 / n) + EPS) * g


def _modnorm(x, g, scale, shift):
    return _rms(x, g, x.shape[-1]) * (1.0 + scale) + shift


def _lane_iota(shape):
    return lax.broadcasted_iota(jnp.int32, shape, len(shape) - 1)


def _split_pair(t):
    lane = _lane_iota(t.shape)
    lo = jnp.where(lane < HALF, t, 0.0)
    hi = pltpu.roll(jnp.where(lane >= HALF, t, 0.0), HALF, 1)
    return lo, hi


def _join_pair(lo, hi):
    lane = _lane_iota(lo.shape)
    return jnp.where(lane < HALF, lo, pltpu.roll(hi, HALF, 1))


def _split_heads(t, n_pairs):
    out = []
    for p in range(n_pairs):
        out.extend(_split_pair(t[:, p * LANE:(p + 1) * LANE]))
    return out


def _join_heads(hs):
    return jnp.concatenate([_join_pair(hs[2 * p], hs[2 * p + 1]) for p in range(len(hs) // 2)], axis=1)


def _t5_bucket_table():
    dist = np.arange(BLK)[:, None] + BLK - np.arange(2 * BLK)[None, :]
    n = np.maximum(dist, 0)
    max_exact = N_BUCKETS // 2
    large = max_exact + (np.log(np.maximum(n, 1) / max_exact) / np.log(128 / max_exact)
                         * (N_BUCKETS - max_exact)).astype(np.int32)
    large = np.minimum(large, N_BUCKETS - 1)
    return np.where(n < max_exact, n, large).astype(np.int32)


def _my_pos():
    return lax.axis_index("x"), lax.axis_index("y"), lax.axis_index("c")


def _peer(k):
    x, y, c = _my_pos()
    return (1 - x if k & 4 else x, 1 - y if k & 2 else y, 1 - c if k & 1 else c)


def _lin(pos):
    return 4 * pos[0] + 2 * pos[1] + pos[2]


def _xchg_copies(ins, outs, sems, scatter):
    local_sem, send_sem, recv_sem = sems
    me = _lin(_my_pos())
    local, remote = [], []
    for a in range(len(ins)):
        src = ins[a].at[me] if scatter else ins[a]
        local.append(pltpu.make_async_copy(src, outs[a].at[me], local_sem.at[a]))
    for k in range(1, N_DEV):
        peer = _peer(k)
        for a in range(len(ins)):
            src = ins[a].at[_lin(peer)] if scatter else ins[a]
            remote.append(pltpu.make_async_remote_copy(src, outs[a].at[me], send_sem.at[a, k - 1],
                                                       recv_sem.at[a, k - 1], device_id=peer, device_id_type=MESH))
    return local, remote


def _xchg_start(ins, outs, sems, scatter):
    local, remote = _xchg_copies(ins, outs, sems, scatter)
    for cp in local + remote:
        cp.start()


def _xchg_wait(ins, outs, sems, scatter):
    local, remote = _xchg_copies(ins, outs, sems, scatter)
    for cp in local:
        cp.wait()
    for cp in remote:
        cp.wait_send()
        cp.wait_recv()


def _xchg_shapes(arrs, scatter):
    n = len(arrs)
    if scatter:
        out_shape = [jax.ShapeDtypeStruct(a.shape, a.dtype) for a in arrs]
    else:
        out_shape = [jax.ShapeDtypeStruct((N_DEV,) + a.shape, a.dtype) for a in arrs]
    sems = [pltpu.SemaphoreType.DMA((n,)), pltpu.SemaphoreType.DMA((n, N_DEV - 1)),
            pltpu.SemaphoreType.DMA((n, N_DEV - 1))]
    return out_shape, sems


def _exchange(arrs, scatter, name):
    n = len(arrs)
    out_shape, sems = _xchg_shapes(arrs, scatter)

    def body(*refs):
        ins, outs, s = refs[:n], refs[n:2 * n], refs[2 * n:]
        _xchg_start(ins, outs, s, scatter)
        _xchg_wait(ins, outs, s, scatter)

    hbm = pl.BlockSpec(memory_space=pltpu.HBM)
    return pl.pallas_call(body, name=name, out_shape=out_shape, in_specs=[hbm] * n, out_specs=[hbm] * n,
                          scratch_shapes=sems)(*arrs)


def _hosted_call(body, name, n_steps, in_specs, out_specs, out_shape, scratch_shapes, args, xchg, cparams):
    arrs, scatter = xchg
    n, n_in, n_out, n_scr = len(arrs), len(in_specs), len(out_specs), len(scratch_shapes)
    x_shape, x_sems = _xchg_shapes(arrs, scatter)

    def hosted(*refs):
        ins, refs = refs[:n_in], refs[n_in:]
        x_in, refs = refs[:n], refs[n:]
        outs, refs = refs[:n_out], refs[n_out:]
        x_out, refs = refs[:n], refs[n:]
        scr, sems = refs[:n_scr], refs[n_scr:]

        @pl.when(pl.program_id(0) == 0)
        def _():
            _xchg_start(x_in, x_out, sems, scatter)

        body(*ins, *outs, *scr)

        @pl.when(pl.program_id(0) == n_steps - 1)
        def _():
            _xchg_wait(x_in, x_out, sems, scatter)

    hbm = pl.BlockSpec(memory_space=pltpu.HBM)
    res = pl.pallas_call(
        hosted, name=name, grid=(n_steps,), in_specs=list(in_specs) + [hbm] * n,
        out_specs=list(out_specs) + [hbm] * n, out_shape=list(out_shape) + x_shape,
        scratch_shapes=list(scratch_shapes) + x_sems, compiler_params=cparams,
    )(*args, *arrs)
    return res[:n_out], res[n_out:]


def _mod_exchange(c, ada_w, ada_b8):
    chunk = ada_w.shape[1]

    def body(c_ref, w_ref, b_ref, out_ref, cbuf, part, s1, r1, s2, r2):
        me = _lin(_my_pos())
        first = []
        for k in range(1, N_DEV):
            cp = pltpu.make_async_remote_copy(c_ref, cbuf.at[me], s1.at[k - 1], r1.at[k - 1],
                                              device_id=_peer(k), device_id_type=MESH)
            cp.start()
            first.append(cp)
        cbuf[me] = c_ref[...]
        for cp in first:
            cp.wait_send()
            cp.wait_recv()
        cond = _silu(jnp.concatenate([cbuf[i] for i in range(N_DEV)], axis=0))
        mod = _mm_hi(cond, w_ref[...]) + b_ref[pl.ds(me, 1), :]
        for j in range(N_DEV):
            part[j] = mod[j:j + 1, :]
        second = []
        for k in range(1, N_DEV):
            peer = _peer(k)
            cp = pltpu.make_async_remote_copy(part.at[_lin(peer)], out_ref.at[me], s2.at[k - 1], r2.at[k - 1],
                                              device_id=peer, device_id_type=MESH)
            cp.start()
            second.append(cp)
        out_ref[me] = part[me]
        for cp in second:
            cp.wait_send()
            cp.wait_recv()

    vm = pl.BlockSpec(memory_space=pltpu.VMEM)
    return pl.pallas_call(
        body, name="mod_exchange", out_shape=jax.ShapeDtypeStruct((N_DEV, 1, chunk), F32),
        in_specs=[vm, vm, vm], out_specs=vm,
        scratch_shapes=[pltpu.VMEM((N_DEV, 1, D_MODEL), F32), pltpu.VMEM((N_DEV, 1, chunk), F32)]
        + [pltpu.SemaphoreType.DMA((N_DEV - 1,))] * 4,
    )(c, ada_w, ada_b8)


def _row(i):
    return (i, 0)


def _fixed(i):
    return (0, 0)


def _in_proj_fwd(x, norm1, scale1, shift1, w_in, tm):
    S = x.shape[0]

    def body(x_ref, n_ref, sc_ref, sh_ref, w_ref, qkv_ref, z_ref, xbc_ref, dt_ref):
        h = _modnorm(x_ref[...], n_ref[...], sc_ref[...], sh_ref[...])
        p = _mm_nt(h, w_ref[...])
        qkv_ref[...] = p[:, :768].astype(qkv_ref.dtype)
        z_ref[...] = p[:, 768:1280]
        xbc_ref[...] = p[:, 1280:2304]
        dt_ref[...] = p[:, 2304:IN_PAD]

    vec = pl.BlockSpec((1, D_MODEL), _fixed)
    return pl.pallas_call(
        body, name="in_proj_fwd", grid=(S // tm,),
        in_specs=[pl.BlockSpec((tm, D_MODEL), _row), vec, vec, vec, pl.BlockSpec((IN_PAD, D_MODEL), _fixed)],
        out_specs=[pl.BlockSpec((tm, 768), _row), pl.BlockSpec((tm, SSM_W), _row),
                   pl.BlockSpec((tm, XBC_W), _row), pl.BlockSpec((tm, LANE), _row)],
        out_shape=[jax.ShapeDtypeStruct((S, 768), MXU_DTYPE), jax.ShapeDtypeStruct((S, SSM_W), F32),
                   jax.ShapeDtypeStruct((S, XBC_W), F32), jax.ShapeDtypeStruct((S, LANE), F32)],
        compiler_params=_cparams(VMEM_BIG),
    )(x, norm1, scale1, shift1, w_in)


def _in_proj_bwd(x, dx1, dq, dkv, dz, dxbc, ddt, norm1, scale1, shift1, w_in, tm):
    S = x.shape[0]

    def body(x_ref, dx1_ref, dq_ref, dkv_ref, dz_ref, dxbc_ref, ddt_ref, n_ref, sc_ref, sh_ref, w_ref,
             gx_ref, h_ref, dp_ref, acc_ref):
        @pl.when(pl.program_id(0) == 0)
        def _():
            acc_ref[...] = jnp.zeros_like(acc_ref)

        h, vjp = jax.vjp(_modnorm, x_ref[...], n_ref[...], sc_ref[...], sh_ref[...])
        dp = jnp.concatenate([dq_ref[...].astype(MXU_DTYPE), dkv_ref[...].astype(MXU_DTYPE),
                              dz_ref[...].astype(MXU_DTYPE), dxbc_ref[...].astype(MXU_DTYPE),
                              ddt_ref[...].astype(MXU_DTYPE)], axis=1)
        dh = _mm(dp, w_ref[...])
        dx, dn, dsc, dsh = vjp(dh)
        gx_ref[...] = dx1_ref[...] + dx
        h_ref[...] = h.astype(h_ref.dtype)
        dp_ref[...] = dp
        acc_ref[0:1, :] += dn
        acc_ref[1:2, :] += dsc
        acc_ref[2:3, :] += dsh

    vec = pl.BlockSpec((1, D_MODEL), _fixed)
    return pl.pallas_call(
        body, name="in_proj_bwd", grid=(S // tm,),
        in_specs=[pl.BlockSpec((tm, D_MODEL), _row), pl.BlockSpec((tm, D_MODEL), _row),
                  pl.BlockSpec((tm, ATTN_W), _row), pl.BlockSpec((tm, 2 * KV_W), _row),
                  pl.BlockSpec((tm, SSM_W), _row), pl.BlockSpec((tm, XBC_W), _row), pl.BlockSpec((tm, LANE), _row),
                  vec, vec, vec, pl.BlockSpec((IN_PAD, D_MODEL), _fixed)],
        out_specs=[pl.BlockSpec((tm, D_MODEL), _row), pl.BlockSpec((tm, D_MODEL), _row),
                   pl.BlockSpec((tm, IN_PAD), _row), pl.BlockSpec((8, D_MODEL), _fixed)],
        out_shape=[jax.ShapeDtypeStruct((S, D_MODEL), F32), jax.ShapeDtypeStruct((S, D_MODEL), MXU_DTYPE),
                   jax.ShapeDtypeStruct((S, IN_PAD), MXU_DTYPE), jax.ShapeDtypeStruct((8, D_MODEL), F32)],
        compiler_params=_cparams(VMEM_BIG),
    )(x, dx1, dq, dkv, dz, dxbc, ddt, norm1, scale1, shift1, w_in)


def _out_stage(ya, ys0, ys1, z0, z1, an, sn0, sn1):
    half = SSM_W // 2
    a = _rms(ya, an, ATTN_W)
    g0 = _rms(ys0 * _silu(z0), sn0, half)
    g1 = _rms(ys1 * _silu(z1), sn1, half)
    return jnp.concatenate([a, g0, g1], axis=1)


def _out_stage_args(ya_ref, ys_ref, z_ref, an_ref, sn_ref):
    half = SSM_W // 2
    return (ya_ref[...], ys_ref[:, :half], ys_ref[:, half:], z_ref[:, :half], z_ref[:, half:],
            an_ref[...], sn_ref[:, :half], sn_ref[:, half:])


def _out_proj_fwd(x, ya, ys, z, an, sn, gate1, w_o, tm):
    S = x.shape[0]

    def body(x_ref, ya_ref, ys_ref, z_ref, an_ref, sn_ref, g_ref, w_ref, x1_ref):
        u = _out_stage(*_out_stage_args(ya_ref, ys_ref, z_ref, an_ref, sn_ref))
        x1_ref[...] = x_ref[...] + g_ref[...] * _mm(u, w_ref[...])

    half = pl.BlockSpec((tm, ATTN_W), _row)
    hvec = pl.BlockSpec((1, ATTN_W), _fixed)
    return pl.pallas_call(
        body, name="out_proj_fwd", grid=(S // tm,),
        in_specs=[pl.BlockSpec((tm, D_MODEL), _row), half, half, half, hvec, hvec,
                  pl.BlockSpec((1, D_MODEL), _fixed), pl.BlockSpec((D_MODEL, D_MODEL), _fixed)],
        out_specs=pl.BlockSpec((tm, D_MODEL), _row),
        out_shape=jax.ShapeDtypeStruct((S, D_MODEL), F32),
        compiler_params=_cparams(VMEM_BIG),
    )(x, ya, ys, z, an, sn, gate1, w_o)


def _out_proj_bwd(dx1, ya, ys, z, an, sn, gate1, w_o, tm):
    S = dx1.shape[0]

    def body(dx1_ref, ya_ref, ys_ref, z_ref, an_ref, sn_ref, g_ref, w_ref,
             dya_ref, dys_ref, dz_ref, u_ref, dmix_ref, acc_ref):
        @pl.when(pl.program_id(0) == 0)
        def _():
            acc_ref[...] = jnp.zeros_like(acc_ref)

        u, vjp = jax.vjp(_out_stage, *_out_stage_args(ya_ref, ys_ref, z_ref, an_ref, sn_ref))
        dx1 = dx1_ref[...]
        mix = _mm(u, w_ref[...])
        dmix = dx1 * g_ref[...]
        du = _mm_nt(dmix, w_ref[...])
        dya, dys0, dys1, dz0, dz1, dan, dsn0, dsn1 = vjp(du)
        dya_ref[...] = dya
        dys_ref[...] = jnp.concatenate([dys0, dys1], axis=1)
        dz_ref[...] = jnp.concatenate([dz0, dz1], axis=1)
        u_ref[...] = u.astype(u_ref.dtype)
        dmix_ref[...] = dmix.astype(dmix_ref.dtype)
        acc_ref[0:1, :] += jnp.sum(dx1 * mix, axis=0, keepdims=True)
        acc_ref[1:2, :] += jnp.concatenate([dan, dsn0, dsn1], axis=1)

    half = pl.BlockSpec((tm, ATTN_W), _row)
    hvec = pl.BlockSpec((1, ATTN_W), _fixed)
    full = pl.BlockSpec((tm, D_MODEL), _row)
    return pl.pallas_call(
        body, name="out_proj_bwd", grid=(S // tm,),
        in_specs=[full, half, half, half, hvec, hvec,
                  pl.BlockSpec((1, D_MODEL), _fixed), pl.BlockSpec((D_MODEL, D_MODEL), _fixed)],
        out_specs=[half, half, half, full, full, pl.BlockSpec((8, D_MODEL), _fixed)],
        out_shape=[jax.ShapeDtypeStruct((S, ATTN_W), F32)] * 3
        + [jax.ShapeDtypeStruct((S, D_MODEL), MXU_DTYPE)] * 2 + [jax.ShapeDtypeStruct((8, D_MODEL), F32)],
        compiler_params=_cparams(VMEM_BIG),
    )(dx1, ya, ys, z, an, sn, gate1, w_o)


def _loss_rows(x2, fn, tgt):
    y = _rms(x2, fn, D_MODEL)
    per_row = jnp.sum(jnp.square(y - tgt), axis=1, keepdims=True)
    return jnp.sum(per_row, axis=0, keepdims=True) * (0.5 / D_MODEL)


def _mlp_loss(x1, tgt, norm2, scale2, shift2, gate2, fnorm, w_gu, w_d, tm):
    S = x1.shape[0]

    def body(x1_ref, t_ref, n_ref, sc_ref, sh_ref, g_ref, fn_ref, wgu_hbm, wd_hbm,
             dx1_ref, h_ref, dgu_ref, act_ref, dmlp_ref, acc_ref, wgu, wd):
        @pl.when(pl.program_id(0) == 0)
        def _():
            acc_ref[...] = jnp.zeros_like(acc_ref)
            pltpu.sync_copy(wgu_hbm, wgu)
            pltpu.sync_copy(wd_hbm, wd)

        x1 = x1_ref[...]
        gate2 = g_ref[...]
        h, vjp_h = jax.vjp(_modnorm, x1, n_ref[...], sc_ref[...], sh_ref[...])
        hb = h.astype(MXU_DTYPE)
        gu = _mm_nt(hb, wgu[...])
        g, u = gu[:, :D_FF], gu[:, D_FF:]
        sg = jax.nn.sigmoid(g)
        silu_g = g * sg
        act = (silu_g * u).astype(MXU_DTYPE)
        mlp = _mm(act, wd[...])
        x2 = x1 + gate2 * mlp
        loss, vjp_loss = jax.vjp(_loss_rows, x2, fn_ref[...], t_ref[...])
        dx2, dfn, _ = vjp_loss(jnp.ones((1, 1), F32))
        dmlp = (dx2 * gate2).astype(MXU_DTYPE)
        dact = _mm_nt(dmlp, wd[...])
        dg = dact * u * (sg * (1.0 + g * (1.0 - sg)))
        du = dact * silu_g
        dgu = jnp.concatenate([dg, du], axis=1).astype(MXU_DTYPE)
        dh = _mm(dgu, wgu[...])
        dx, dn, dsc, dsh = vjp_h(dh)
        dx1_ref[...] = dx2 + dx
        h_ref[...] = hb
        dgu_ref[...] = dgu
        act_ref[...] = act
        dmlp_ref[...] = dmlp
        acc_ref[0:1, :] += dn
        acc_ref[1:2, :] += dsc
        acc_ref[2:3, :] += dsh
        acc_ref[3:4, :] += jnp.sum(dx2 * mlp, axis=0, keepdims=True)
        acc_ref[4:5, :] += dfn
        acc_ref[5:6, :] += jnp.broadcast_to(loss, (1, D_MODEL))

    full = pl.BlockSpec((tm, D_MODEL), _row)
    vec = pl.BlockSpec((1, D_MODEL), _fixed)
    anyspec = pl.BlockSpec(memory_space=pl.ANY)
    return pl.pallas_call(
        body, name="mlp_loss", grid=(S // tm,),
        in_specs=[full, full, vec, vec, vec, vec, vec, anyspec, anyspec],
        out_specs=[full, full, pl.BlockSpec((tm, 2 * D_FF), _row), pl.BlockSpec((tm, D_FF), _row), full,
                   pl.BlockSpec((8, D_MODEL), _fixed)],
        out_shape=[jax.ShapeDtypeStruct((S, D_MODEL), F32), jax.ShapeDtypeStruct((S, D_MODEL), MXU_DTYPE),
                   jax.ShapeDtypeStruct((S, 2 * D_FF), MXU_DTYPE), jax.ShapeDtypeStruct((S, D_FF), MXU_DTYPE),
                   jax.ShapeDtypeStruct((S, D_MODEL), MXU_DTYPE), jax.ShapeDtypeStruct((8, D_MODEL), F32)],
        scratch_shapes=[pltpu.VMEM((2 * D_FF, D_MODEL), MXU_DTYPE), pltpu.VMEM((D_FF, D_MODEL), MXU_DTYPE)],
        compiler_params=_cparams(VMEM_BIG),
    )(x1, tgt, norm2, scale2, shift2, gate2, fnorm, w_gu, w_d)


def _wgrad(a, g, tk, ts, name):
    S, K = a.shape
    N = g.shape[1]
    ns = S // ts

    def body(a_ref, g_ref, o_ref, acc_ref):
        s = pl.program_id(1)

        @pl.when(s == 0)
        def _():
            acc_ref[...] = jnp.zeros_like(acc_ref)

        acc_ref[...] += _mm_tn(a_ref[...], g_ref[...])

        @pl.when(s == ns - 1)
        def _():
            o_ref[...] = acc_ref[...].astype(o_ref.dtype)

    return pl.pallas_call(
        body, name=name, grid=(K // tk, ns),
        in_specs=[pl.BlockSpec((ts, tk), lambda j, s: (s, j)), pl.BlockSpec((ts, N), lambda j, s: (s, 0))],
        out_specs=pl.BlockSpec((tk, N), lambda j, s: (j, 0)),
        out_shape=jax.ShapeDtypeStruct((K, N), WIRE_DTYPE),
        scratch_shapes=[pltpu.VMEM((tk, N), F32)],
        compiler_params=_cparams(VMEM_BIG),
    )(a, g)


MASKED = -1e30
QK_SCALE = HALF ** -0.5


def _attn_bias(buckets, rel_bias):
    def body(bk_ref, relb_ref, out_ref):
        bk = bk_ref[...]
        i = lax.broadcasted_iota(jnp.int32, (BLK, 2 * BLK), 0)
        j = lax.broadcasted_iota(jnp.int32, (BLK, 2 * BLK), 1)
        window = (j > i) & (j <= i + BLK)
        for h in range(N_HEADS):
            acc = jnp.zeros((BLK, 2 * BLK), F32)
            for b in range(N_BUCKETS):
                acc = jnp.where(bk == b, relb_ref[b, h], acc)
            out_ref[0, h] = jnp.where(window, acc, MASKED)
            out_ref[1, h] = jnp.where(window & (j >= BLK), acc, MASKED)

    return pl.pallas_call(
        body, name="attn_bias", out_shape=jax.ShapeDtypeStruct((2, N_HEADS, BLK, 2 * BLK), F32),
        in_specs=[pl.BlockSpec(memory_space=pltpu.VMEM), pl.BlockSpec(memory_space=pltpu.SMEM)],
    )(buckets, rel_bias)


def _attn_kv(kvp_ref, kvc_ref):
    kvp = kvp_ref[...].astype(F32)
    kvc = kvc_ref[...].astype(F32)
    kp, kc = _split_pair(kvp[:, :LANE]), _split_pair(kvc[:, :LANE])
    vp, vc = _split_pair(kvp[:, LANE:]), _split_pair(kvc[:, LANE:])
    k_pads = [jnp.concatenate([kp[g], kc[g]], axis=0).astype(MXU_DTYPE) for g in range(2)]
    v_pads = [jnp.concatenate([vp[g], vc[g]], axis=0).astype(MXU_DTYPE) for g in range(2)]
    return k_pads, v_pads


def _attn_probs(qs, k_pad, bias, sink):
    s = _mm_nt(qs, k_pad) + bias
    m = jnp.maximum(jnp.max(s, axis=-1, keepdims=True), sink)
    p = jnp.exp(s - m)
    den = jnp.sum(p, axis=-1, keepdims=True) + jnp.exp(sink - m)
    return p, m, 1.0 / den


def _attn_fwd(qkv, bias, sinks, xchg):
    S = qkv.shape[0]
    nb = S // BLK

    def body(q_ref, kvp_ref, kvc_ref, bias_ref, sinks_ref, y_ref):
        first = jnp.where(pl.program_id(0) == 0, 1, 0)
        q_heads = _split_heads(q_ref[...].astype(F32) * QK_SCALE, 4)
        k_pads, v_pads = _attn_kv(kvp_ref, kvc_ref)
        outs = []
        for h in range(N_HEADS):
            kv = h // 4
            p, _, rinv = _attn_probs(q_heads[h].astype(MXU_DTYPE), k_pads[kv], bias_ref[first, h], sinks_ref[h])
            outs.append(_mm(p, v_pads[kv]) * rinv)
        y_ref[...] = _join_heads(outs)

    smem = pl.BlockSpec(memory_space=pltpu.SMEM)
    return _hosted_call(
        body, "attn_fwd", nb,
        in_specs=[pl.BlockSpec((BLK, ATTN_W), _row),
                  pl.BlockSpec((BLK, 2 * KV_W), lambda i: (jnp.maximum(i - 1, 0), 2)),
                  pl.BlockSpec((BLK, 2 * KV_W), lambda i: (i, 2)),
                  pl.BlockSpec((2, N_HEADS, BLK, 2 * BLK), lambda i: (0, 0, 0, 0)), smem],
        out_specs=[pl.BlockSpec((BLK, ATTN_W), _row)],
        out_shape=[jax.ShapeDtypeStruct((S, ATTN_W), F32)],
        scratch_shapes=[],
        args=(qkv, qkv, qkv, bias, sinks), xchg=xchg, cparams=_cparams(),
    )


def _attn_bwd(qkv, y, dy, bias, sinks, xchg):
    S = qkv.shape[0]
    nb = S // BLK

    def body(q_ref, kvp_ref, kvc_ref, y_ref, dy_ref, bias_ref, sinks_ref, dq_ref, dkv_ref, dbias_ref, dsk_ref, carry_ref):
        i = pl.program_id(0)

        @pl.when(i == 0)
        def _():
            dbias_ref[...] = jnp.zeros_like(dbias_ref)
            dsk_ref[...] = jnp.zeros_like(dsk_ref)
            carry_ref[...] = jnp.zeros_like(carry_ref)

        first = jnp.where(i == nb - 1, 1, 0)
        q_heads = _split_heads(q_ref[...].astype(F32) * QK_SCALE, 4)
        k_pads, v_pads = _attn_kv(kvp_ref, kvc_ref)
        y_heads = _split_heads(y_ref[...], 4)
        dy_heads = _split_heads(dy_ref[...], 4)
        dq_heads, dk_pads, dv_pads = [], [], []
        for kv in range(2):
            ds_l, q_l, p_l, t_l = [], [], [], []
            for h in range(4 * kv, 4 * kv + 4):
                qs = q_heads[h].astype(MXU_DTYPE)
                p, m, rinv = _attn_probs(qs, k_pads[kv], bias_ref[first, h], sinks_ref[h])
                t = dy_heads[h] * rinv
                delta = jnp.sum(t * y_heads[h], axis=-1, keepdims=True)
                ds = p * (_mm_nt(t, v_pads[kv]) - delta)
                dbias_ref[h] += ds
                dsk_ref[h] -= jnp.exp(sinks_ref[h] - m) * delta
                ds = ds.astype(MXU_DTYPE)
                dq_heads.append(_mm(ds, k_pads[kv]) * QK_SCALE)
                ds_l.append(ds)
                q_l.append(qs)
                p_l.append(p.astype(MXU_DTYPE))
                t_l.append(t.astype(MXU_DTYPE))
            dk_pads.append(_mm_tn(jnp.concatenate(ds_l, axis=0), jnp.concatenate(q_l, axis=0)))
            dv_pads.append(_mm_tn(jnp.concatenate(p_l, axis=0), jnp.concatenate(t_l, axis=0)))
        dq_ref[...] = _join_heads(dq_heads)
        dk_prev = _join_pair(dk_pads[0][:BLK], dk_pads[1][:BLK])
        dk_cur = _join_pair(dk_pads[0][BLK:], dk_pads[1][BLK:])
        dv_prev = _join_pair(dv_pads[0][:BLK], dv_pads[1][:BLK])
        dv_cur = _join_pair(dv_pads[0][BLK:], dv_pads[1][BLK:])
        dkv_ref[...] = jnp.concatenate([dk_cur, dv_cur], axis=1) + carry_ref[...]
        carry_ref[...] = jnp.concatenate([dk_prev, dv_prev], axis=1)

    smem = pl.BlockSpec(memory_space=pltpu.SMEM)
    rev = lambda i: (nb - 1 - i, 0)
    return _hosted_call(
        body, "attn_bwd", nb,
        in_specs=[pl.BlockSpec((BLK, ATTN_W), rev),
                  pl.BlockSpec((BLK, 2 * KV_W), lambda i: (jnp.maximum(nb - 2 - i, 0), 2)),
                  pl.BlockSpec((BLK, 2 * KV_W), lambda i: (nb - 1 - i, 2)),
                  pl.BlockSpec((BLK, ATTN_W), rev), pl.BlockSpec((BLK, ATTN_W), rev),
                  pl.BlockSpec((2, N_HEADS, BLK, 2 * BLK), lambda i: (0, 0, 0, 0)), smem],
        out_specs=[pl.BlockSpec((BLK, ATTN_W), rev), pl.BlockSpec((BLK, 2 * KV_W), rev),
                   pl.BlockSpec((N_HEADS, BLK, 2 * BLK), lambda i: (0, 0, 0)),
                   pl.BlockSpec((N_HEADS, BLK, 1), lambda i: (0, 0, 0))],
        out_shape=[jax.ShapeDtypeStruct((S, ATTN_W), F32), jax.ShapeDtypeStruct((S, 2 * KV_W), F32),
                   jax.ShapeDtypeStruct((N_HEADS, BLK, 2 * BLK), F32), jax.ShapeDtypeStruct((N_HEADS, BLK, 1), F32)],
        scratch_shapes=[pltpu.VMEM((BLK, 2 * KV_W), F32)],
        args=(qkv, qkv, qkv, y, dy, bias, sinks), xchg=xchg, cparams=_cparams(),
    )


def _attn_finish(dbias, dsk, buckets):
    def body(db_ref, dsk_ref, bk_ref, drel_ref, dsink_ref):
        bk = bk_ref[...]
        r = lax.broadcasted_iota(jnp.int32, (N_BUCKETS, LANE), 0)
        l = lax.broadcasted_iota(jnp.int32, (N_BUCKETS, LANE), 1)
        row = lax.broadcasted_iota(jnp.int32, (N_HEADS, LANE), 0)
        res = jnp.zeros((N_BUCKETS, LANE), F32)
        dsink = jnp.zeros((N_HEADS, LANE), F32)
        for h in range(N_HEADS):
            db = db_ref[h]
            for b in range(N_BUCKETS):
                v = jnp.sum(jnp.sum(jnp.where(bk == b, db, 0.0), axis=1, keepdims=True), axis=0, keepdims=True)
                res = res + jnp.where((r == b) & (l == h), v, 0.0)
            dsink = dsink + jnp.where(row == h, jnp.sum(dsk_ref[h], axis=0, keepdims=True), 0.0)
        drel_ref[...] = res
        dsink_ref[...] = dsink

    return pl.pallas_call(body, name="attn_finish",
                          out_shape=[jax.ShapeDtypeStruct((N_BUCKETS, LANE), F32),
                                     jax.ShapeDtypeStruct((N_HEADS, LANE), F32)])(dbias, dsk, buckets)


def _ssd_consts():
    r = lax.broadcasted_iota(jnp.int32, (BLK, BLK), 0)
    c = lax.broadcasted_iota(jnp.int32, (BLK, BLK), 1)
    causal = c <= r
    upper = (r <= c).astype(F32)
    last = r == BLK - 1
    head = lax.broadcasted_iota(jnp.int32, (N_HEADS, BLK), 0)
    return causal, upper, last, head


def _ssd_chunk(xs, bg, cg, dt_raw_t, prev, dtb, alog, d_rows, consts):
    causal, upper, last, head = consts
    dt_t = _softplus(dt_raw_t + dtb)
    acs_t = _mm_hi(dt_t * (-jnp.exp(alog)), upper)
    cb = [_mm_nt(cg[g], bg[g]) for g in range(2)]
    ys, hs = [], []
    for h in range(N_HEADS):
        g = h // 4
        dt_row = jnp.sum(jnp.where(head == h, dt_t, 0.0), axis=0, keepdims=True)
        a_row = jnp.sum(jnp.where(head == h, acs_t, 0.0), axis=0, keepdims=True)
        a_rb = jnp.broadcast_to(a_row, (BLK, BLK))
        a_b = a_rb.T
        a_last = jnp.sum(jnp.where(last, a_b, 0.0), axis=0, keepdims=True)
        w = cb[g] * jnp.exp(jnp.where(causal, a_b - a_rb, -1e30)) * dt_row
        f_b = jnp.broadcast_to(dt_row * jnp.exp(a_last - a_row), (BLK, BLK)).T
        y = _mm(w, xs[h]) + _mm(cg[g], prev[h]) * jnp.exp(a_b) + d_rows[h] * xs[h]
        st = _mm_tn(bg[g], xs[h] * f_b)
        ys.append(y)
        hs.append(prev[h] * jnp.exp(a_last) + st)
    return tuple(ys), tuple(hs)


def _dt_rows(dt_blk):
    return dt_blk.T[:N_HEADS]


def _silu_grad(x):
    s = jax.nn.sigmoid(x)
    return s * (1.0 + x * (1.0 - s))


def _conv_pre(ext_ref, halo, blk, cw_ref, cb_ref):
    ext_ref[0:8, :] = halo
    ext_ref[8:8 + BLK, :] = blk
    pre = cb_ref[...] + cw_ref[0:1, :] * ext_ref[pl.ds(5, BLK), :]
    for k in range(1, 4):
        pre = pre + cw_ref[k:k + 1, :] * ext_ref[pl.ds(5 + k, BLK), :]
    return pre


def _ssd_split(pre):
    heads = _split_heads(pre[:, :SSM_W], 4)
    pb = [pre[:, SSM_W + g * D_STATE:SSM_W + (g + 1) * D_STATE] for g in range(2)]
    pc = [pre[:, SSM_W + 2 * D_STATE + g * D_STATE:SSM_W + 2 * D_STATE + (g + 1) * D_STATE] for g in range(2)]
    return heads, pb, pc


def _ssd_fwd(xbc, dt_raw, conv_w, conv_b, dtb_row, alog_row, d_exp, xchg):
    S = xbc.shape[0]
    nc = S // BLK

    def body(xbc_ref, halo_ref, dt_ref, cw_ref, cb_ref, dtb_ref, alog_ref, d_ref, y_ref, prev_ref, ext_ref, state_ref):
        i = pl.program_id(0)

        @pl.when(i == 0)
        def _():
            state_ref[...] = jnp.zeros_like(state_ref)

        halo = halo_ref[...] * jnp.where(i > 0, 1.0, 0.0)
        pre = _conv_pre(ext_ref, halo, xbc_ref[...], cw_ref, cb_ref)
        heads, pb, pc = _ssd_split(_silu(pre))
        prev = [state_ref[h] for h in range(N_HEADS)]
        for h in range(N_HEADS):
            prev_ref[0, h] = prev[h]
        d_rows = [d_ref[h:h + 1, :] for h in range(N_HEADS)]
        ys, hs = _ssd_chunk(heads, pb, pc, _dt_rows(dt_ref[...]), prev, dtb_ref[...], alog_ref[...], d_rows,
                            _ssd_consts())
        for h in range(N_HEADS):
            state_ref[h] = hs[h]
        y_ref[...] = _join_heads(ys)

    vec = pl.BlockSpec((N_HEADS, LANE), _fixed)
    return _hosted_call(
        body, "ssd_fwd", nc,
        in_specs=[pl.BlockSpec((BLK, XBC_W), _row),
                  pl.BlockSpec((8, XBC_W), lambda i: (jnp.maximum(i * (BLK // 8) - 1, 0), 0)),
                  pl.BlockSpec((BLK, LANE), _row),
                  pl.BlockSpec((4, XBC_W), _fixed), pl.BlockSpec((1, XBC_W), _fixed), vec, vec,
                  pl.BlockSpec((N_HEADS, LANE), _fixed)],
        out_specs=[pl.BlockSpec((BLK, SSM_W), _row),
                   pl.BlockSpec((1, N_HEADS, D_STATE, LANE), lambda i: (i, 0, 0, 0))],
        out_shape=[jax.ShapeDtypeStruct((S, SSM_W), F32), jax.ShapeDtypeStruct((nc, N_HEADS, D_STATE, LANE), F32)],
        scratch_shapes=[pltpu.VMEM((8 + BLK, XBC_W), F32), pltpu.VMEM((N_HEADS, D_STATE, LANE), F32)],
        args=(xbc, xbc, dt_raw, conv_w, conv_b, dtb_row, alog_row, d_exp), xchg=xchg, cparams=_cparams(),
    )


def _ssd_bwd(xbc, dt_raw, prev_states, dy, conv_w, conv_b, dtb_row, alog_row, d_exp, xchg):
    S = xbc.shape[0]
    nc = S // BLK

    def body(xbc_ref, halo_ref, dt_ref, prev_ref, dy_ref, cw_ref, cb_ref, dtb_ref, alog_ref, d_ref,
             dxbc_ref, ddt_ref, dcw_ref, dvec_ref, dd_ref, ext_ref, dpe_ref, gstate_ref, ghalo_ref):
        i = pl.program_id(0)
        c = nc - 1 - i

        @pl.when(i == 0)
        def _():
            gstate_ref[...] = jnp.zeros_like(gstate_ref)
            ghalo_ref[...] = jnp.zeros_like(ghalo_ref)
            dcw_ref[...] = jnp.zeros_like(dcw_ref)
            dvec_ref[...] = jnp.zeros_like(dvec_ref)
            dd_ref[...] = jnp.zeros_like(dd_ref)
            dpe_ref[...] = jnp.zeros_like(dpe_ref)

        halo = halo_ref[...] * jnp.where(c > 0, 1.0, 0.0)
        pre = _conv_pre(ext_ref, halo, xbc_ref[...], cw_ref, cb_ref)
        heads, pb, pc = _ssd_split(_silu(pre))
        prev = [prev_ref[0, h] for h in range(N_HEADS)]
        d_rows = [d_ref[h:h + 1, :] for h in range(N_HEADS)]
        _, vjp = jax.vjp(functools.partial(_ssd_chunk, consts=_ssd_consts()),
                         heads, pb, pc, _dt_rows(dt_ref[...]), prev, dtb_ref[...], alog_ref[...], d_rows)
        dys = tuple(_split_heads(dy_ref[...], 4))
        dhs = tuple(gstate_ref[h] for h in range(N_HEADS))
        dheads, dpb, dpc, ddt_t, dprev, ddtb, dalog, dd_rows = vjp((dys, dhs))
        for h in range(N_HEADS):
            gstate_ref[h] = dprev[h]
            dd_ref[h:h + 1, :] += dd_rows[h]
        ddt_ref[...] = jnp.concatenate([ddt_t, jnp.zeros((BLK - N_HEADS, BLK), F32)], axis=0).T
        dvec_ref[0:N_HEADS, :] += ddtb
        dvec_ref[N_HEADS:, :] += dalog
        dpre = jnp.concatenate([_join_heads(dheads)] + list(dpb) + list(dpc), axis=1) * _silu_grad(pre)
        dpe_ref[8:8 + BLK, :] = dpre
        dext = cw_ref[0:1, :] * dpe_ref[pl.ds(3, 8 + BLK), :]
        dcw_ref[0:1, :] += jnp.sum(dpre * ext_ref[pl.ds(5, BLK), :], axis=0, keepdims=True)
        for k in range(1, 4):
            dext = dext + cw_ref[k:k + 1, :] * dpe_ref[pl.ds(3 - k, 8 + BLK), :]
            dcw_ref[k:k + 1, :] += jnp.sum(dpre * ext_ref[pl.ds(5 + k, BLK), :], axis=0, keepdims=True)
        dcw_ref[4:5, :] += jnp.sum(dpre, axis=0, keepdims=True)
        dxbc_ref[...] = dext[8:, :]
        dxbc_ref[BLK - 8:BLK, :] += ghalo_ref[...]
        ghalo_ref[...] = dext[:8, :]

    vec = pl.BlockSpec((N_HEADS, LANE), _fixed)
    rev = lambda i: (nc - 1 - i, 0)
    return _hosted_call(
        body, "ssd_bwd", nc,
        in_specs=[pl.BlockSpec((BLK, XBC_W), rev),
                  pl.BlockSpec((8, XBC_W), lambda i: (jnp.maximum((nc - 1 - i) * (BLK // 8) - 1, 0), 0)),
                  pl.BlockSpec((BLK, LANE), rev),
                  pl.BlockSpec((1, N_HEADS, D_STATE, LANE), lambda i: (nc - 1 - i, 0, 0, 0)),
                  pl.BlockSpec((BLK, SSM_W), rev),
                  pl.BlockSpec((4, XBC_W), _fixed), pl.BlockSpec((1, XBC_W), _fixed), vec, vec,
                  pl.BlockSpec((N_HEADS, LANE), _fixed)],
        out_specs=[pl.BlockSpec((BLK, XBC_W), rev), pl.BlockSpec((BLK, LANE), rev),
                   pl.BlockSpec((8, XBC_W), _fixed), pl.BlockSpec((2 * N_HEADS, LANE), _fixed),
                   pl.BlockSpec((N_HEADS, LANE), _fixed)],
        out_shape=[jax.ShapeDtypeStruct((S, XBC_W), F32), jax.ShapeDtypeStruct((S, LANE), F32),
                   jax.ShapeDtypeStruct((8, XBC_W), F32), jax.ShapeDtypeStruct((2 * N_HEADS, LANE), F32),
                   jax.ShapeDtypeStruct((N_HEADS, LANE), F32)],
        scratch_shapes=[pltpu.VMEM((8 + BLK, XBC_W), F32), pltpu.VMEM((16 + BLK, XBC_W), F32),
                        pltpu.VMEM((N_HEADS, D_STATE, LANE), F32), pltpu.VMEM((8, XBC_W), F32)],
        args=(xbc, xbc, dt_raw, prev_states, dy, conv_w, conv_b, dtb_row, alog_row, d_exp), xchg=xchg,
        cparams=_cparams(VMEM_BIG),
    )


def _adamw_math(w, g, m, v):
    m = ADAM_B1 * m + (1.0 - ADAM_B1) * g
    v = ADAM_B2 * v + (1.0 - ADAM_B2) * jnp.square(g)
    m_hat = m / (1.0 - ADAM_B1 ** ADAM_STEP)
    v_hat = v / (1.0 - ADAM_B2 ** ADAM_STEP)
    delta = -ADAM_LR * (m_hat / (jnp.sqrt(v_hat) + ADAM_EPS) + ADAM_WD * w)
    return delta, m, v


def _reduce_adamw(parts, w, m, v, name):
    R, C = w.shape

    def body(p_ref, w_ref, m_ref, v_ref, g_ref, d_ref, nm_ref, nv_ref):
        g = p_ref[0].astype(F32)
        for i in range(1, N_DEV):
            g = g + p_ref[i].astype(F32)
        d, nm, nv = _adamw_math(w_ref[...], g, m_ref[...], v_ref[...])
        g_ref[...] = g
        d_ref[...] = d
        nm_ref[...] = nm
        nv_ref[...] = nv

    if R % 16 == 0:
        tr = max(t for t in range(16, 257, 16) if R % t == 0)
        n, blk, pblk = R // tr, pl.BlockSpec((tr, C), _row), pl.BlockSpec((N_DEV, tr, C), lambda i: (0, i, 0))
    else:
        tl = 256
        n, blk, pblk = C // tl, pl.BlockSpec((R, tl), lambda i: (0, i)), pl.BlockSpec((N_DEV, R, tl),
                                                                                      lambda i: (0, 0, i))
    return pl.pallas_call(
        body, name=name, grid=(n,), in_specs=[pblk, blk, blk, blk],
        out_specs=[blk] * 4, out_shape=[jax.ShapeDtypeStruct((R, C), F32)] * 4,
    )(parts, w, m, v)


_SMALL_NAMES = ("ada_b", "norm1", "conv_w", "conv_b", "dt_bias", "A_log", "D_skip", "sinks", "attn_out_norm",
                "ssm_out_norm", "norm2", "rel_bias", "final_norm")
N_MOD = 6 * D_MODEL


def _mod_row(a0, a1, a2):
    return jnp.concatenate([a0[2:3], a0[1:2], a1[0:1], a2[2:3], a2[1:2], a2[3:4]], axis=1)


def _small_update(gathered, params):
    n_g = len(gathered)
    flat = [a for name in _SMALL_NAMES for a in params[name]]

    def body(*refs):
        a0_ref, a1_ref, a2_ref, cw_ref, dv_ref, dd_ref, ds_ref, dr_ref, c_ref = refs[:n_g]
        wmv = refs[n_g:n_g + len(flat)]
        outs = refs[n_g + len(flat):]

        def total(ref):
            t = ref[0]
            for i in range(1, N_DEV):
                t = t + ref[i]
            return t

        t0, t1, t2, tcw, tdv, tdd, tds, tdr = [total(r) for r in (a0_ref, a1_ref, a2_ref, cw_ref, dv_ref, dd_ref,
                                                                   ds_ref, dr_ref)]
        r8 = lax.broadcasted_iota(jnp.int32, (N_HEADS, LANE), 0)
        l8 = lax.broadcasted_iota(jnp.int32, (N_HEADS, LANE), 1)

        def diag_row(t):
            return jnp.sum(jnp.where(r8 == l8, t, 0.0), axis=0, keepdims=True)[:, :N_HEADS]

        def lane_sums(t):
            return diag_row(jnp.broadcast_to(jnp.sum(t, axis=1, keepdims=True), (N_HEADS, LANE)))

        me = _lin(_my_pos())
        n_cw = XBC_W // N_DEV
        cw_mine = jnp.zeros((4, n_cw), F32)
        for j in range(N_DEV):
            cw_mine = cw_mine + tcw[0:4, j * n_cw:(j + 1) * n_cw] * jnp.where(me == j, 1.0, 0.0)
        grads = {
            "ada_b": _mod_row(t0, t1, t2), "norm1": t0[0:1], "conv_w": cw_mine, "conv_b": tcw[4:5],
            "dt_bias": lane_sums(tdv[:N_HEADS]), "A_log": lane_sums(tdv[N_HEADS:]), "D_skip": lane_sums(tdd),
            "sinks": diag_row(tds), "attn_out_norm": t1[1:2, :ATTN_W], "ssm_out_norm": t1[1:2, ATTN_W:],
            "norm2": t2[0:1], "rel_bias": tdr[:, :N_HEADS], "final_norm": t2[4:5],
        }
        for k, name in enumerate(_SMALL_NAMES):
            w_ref, m_ref, v_ref = wmv[3 * k:3 * k + 3]
            g = grads[name]
            d, nm, nv = _adamw_math(w_ref[...], g, m_ref[...], v_ref[...])
            for o, val in zip(outs[4 * k:4 * k + 4], (g, d, nm, nv)):
                o[...] = val
        loss_ref, call_ref, dmod_ref = outs[4 * len(_SMALL_NAMES):]
        loss_ref[...] = t2[5:6, 0:1]
        call_ref[...] = jnp.concatenate([c_ref[i] for i in range(N_DEV)], axis=0)
        dmod_ref[...] = jnp.concatenate([_mod_row(a0_ref[i], a1_ref[i], a2_ref[i]) for i in range(N_DEV)], axis=0)

    out_shape = [jax.ShapeDtypeStruct(params[name][0].shape, F32) for name in _SMALL_NAMES for _ in range(4)]
    out_shape += [jax.ShapeDtypeStruct((1, 1), F32), jax.ShapeDtypeStruct((N_DEV, D_MODEL), F32),
                  jax.ShapeDtypeStruct((N_DEV, N_MOD), F32)]
    res = pl.pallas_call(body, name="small_update", out_shape=out_shape)(*gathered, *flat)
    upd = {name: res[4 * k:4 * k + 4] for k, name in enumerate(_SMALL_NAMES)}
    loss, c_all, dmod_all = res[4 * len(_SMALL_NAMES):]
    return upd, loss, c_all, dmod_all


def _ada_w_update(c_all, dmod_all, w, m, v):
    chunk = w.shape[1]

    def body(c_ref, dm_ref, w_ref, m_ref, v_ref, g_ref, d_ref, nm_ref, nv_ref):
        me = _lin(_my_pos())
        dm = jnp.zeros((N_DEV, chunk), F32)
        for j in range(N_DEV):
            dm = dm + dm_ref[:, j * chunk:(j + 1) * chunk] * jnp.where(me == j, 1.0, 0.0)
        g = lax.dot_general(_silu(c_ref[...]), dm, (((0,), (0,)), ((), ())), precision=HI,
                            preferred_element_type=F32)
        d, nm, nv = _adamw_math(w_ref[...], g, m_ref[...], v_ref[...])
        g_ref[...] = g
        d_ref[...] = d
        nm_ref[...] = nm
        nv_ref[...] = nv

    return pl.pallas_call(body, name="ada_w_update", out_shape=[jax.ShapeDtypeStruct(w.shape, F32)] * 4,
                          compiler_params=_cparams(VMEM_BIG))(c_all, dmod_all, w, m, v)


def _local_step(x, tgt, mod, w_in, conv_w, w_o_mine, w_gu_mine, w_d_mine, p):
    S = x.shape[0]
    tm = min(512, S)
    tmm = min(256, S)
    shift1, scale1, gate1, shift2, scale2, gate2 = [mod[i:i + 1] for i in range(6)]
    buckets = jnp.asarray(_t5_bucket_table())
    per_head = lambda a: jnp.broadcast_to(a.reshape(N_HEADS, 1), (N_HEADS, LANE))
    dtb_row, alog_row, d_exp = per_head(p["dt_bias"]), per_head(p["A_log"]), per_head(p["D_skip"])
    sinks = p["sinks"].reshape(N_HEADS)

    qkv, z, xbc, dt_raw = _in_proj_fwd(x, p["norm1"], scale1, shift1, w_in, tm)
    bias = _attn_bias(buckets, p["rel_bias"])
    (ya,), (g_d,) = _attn_fwd(qkv, bias, sinks, ([w_d_mine], False))
    (ys, prev_states), (g_gu, g_o) = _ssd_fwd(xbc, dt_raw, conv_w, p["conv_b"], dtb_row, alog_row, d_exp,
                                              ([w_gu_mine, w_o_mine], False))
    w_gu = g_gu.reshape(2 * D_FF, D_MODEL)
    w_o = g_o.reshape(D_MODEL, D_MODEL)
    w_d = g_d.reshape(D_FF, D_MODEL)
    x1 = _out_proj_fwd(x, ya, ys, z, p["attn_out_norm"], p["ssm_out_norm"], gate1, w_o, tm)
    dx1, h2, dgu, act, dmlp, acc2 = _mlp_loss(x1, tgt, p["norm2"], scale2, shift2, gate2, p["final_norm"],
                                              w_gu, w_d, tmm)
    g_w_gu = _wgrad(dgu, h2, 2 * D_FF // 4, tm, "wgrad_gate_up")
    g_w_d = _wgrad(act, dmlp, D_FF // 2, tm, "wgrad_down")
    dya, dys, dz, u, dmix, acc1 = _out_proj_bwd(dx1, ya, ys, z, p["attn_out_norm"], p["ssm_out_norm"], gate1, w_o, tm)
    g_w_o = _wgrad(u, dmix, D_MODEL, tm, "wgrad_out")
    (dq, dkv, dbias, dsk), (r_gu,) = _attn_bwd(qkv, ya, dya, bias, sinks,
                                               ([g_w_gu.reshape(N_DEV, 2 * D_FF // N_DEV, D_MODEL)], True))
    drel, dsink = _attn_finish(dbias, dsk, buckets)
    (dxbc, ddt, dcw, dvec, dd), (r_d, r_o) = _ssd_bwd(
        xbc, dt_raw, prev_states, dys, conv_w, p["conv_b"], dtb_row, alog_row, d_exp,
        ([g_w_d.reshape(N_DEV, D_FF // N_DEV, D_MODEL), g_w_o.reshape(N_DEV, D_MODEL // N_DEV, D_MODEL)], True))
    gx, h1, dproj, acc0 = _in_proj_bwd(x, dx1, dq, dkv, dz, dxbc, ddt, p["norm1"], scale1, shift1, w_in, tm)
    g_w_in = _wgrad(dproj, h1, IN_PAD, tm, "wgrad_in")
    return gx, g_w_in, (r_o, r_gu, r_d), (acc0, acc1, acc2, dcw, dvec, dd, dsink, drel)


def kernel(x, c, ada_w, ada_b, norm1, w_in, conv_w, conv_b, dt_bias, A_log, D_skip, sinks, attn_out_norm, ssm_out_norm, w_o, norm2, w_gate_up, w_down, rel_bias, final_norm, loss_target, m_ada_w, m_ada_b, m_norm1, m_w_in, m_conv_w, m_conv_b, m_dt_bias, m_A_log, m_D_skip, m_sinks, m_attn_out_norm, m_ssm_out_norm, m_w_o, m_norm2, m_w_gate_up, m_w_down, m_rel_bias, m_final_norm, v_ada_w, v_ada_b, v_norm1, v_w_in, v_conv_w, v_conv_b, v_dt_bias, v_A_log, v_D_skip, v_sinks, v_attn_out_norm, v_ssm_out_norm, v_w_o, v_norm2, v_w_gate_up, v_w_down, v_rel_bias, v_final_norm):
    two_d = lambda a: a if a.ndim == 2 else a.reshape(-1, a.shape[-1])
    small_params = dict(
        ada_b=(ada_b, m_ada_b, v_ada_b), norm1=(norm1, m_norm1, v_norm1), conv_w=(conv_w, m_conv_w, v_conv_w),
        conv_b=(conv_b, m_conv_b, v_conv_b), dt_bias=(dt_bias, m_dt_bias, v_dt_bias), A_log=(A_log, m_A_log, v_A_log),
        D_skip=(D_skip, m_D_skip, v_D_skip), sinks=(sinks, m_sinks, v_sinks),
        attn_out_norm=(attn_out_norm, m_attn_out_norm, v_attn_out_norm),
        ssm_out_norm=(ssm_out_norm, m_ssm_out_norm, v_ssm_out_norm), norm2=(norm2, m_norm2, v_norm2),
        rel_bias=(rel_bias, m_rel_bias, v_rel_bias), final_norm=(final_norm, m_final_norm, v_final_norm))
    small_params = {k: tuple(two_d(a) for a in v) for k, v in small_params.items()}
    S = x.shape[1]
    xs, tgt = x.reshape(S, D_MODEL), loss_target.reshape(S, D_MODEL)
    ada_w2 = ada_w[0]
    chunk = ada_w2.shape[1]
    t_in = [jnp.transpose(a[0]) for a in (w_in, m_w_in, v_w_in)]
    t_gu = [jnp.transpose(a[0]) for a in (w_gate_up, m_w_gate_up, v_w_gate_up)]

    mod = _mod_exchange(c, ada_w2, ada_b.reshape(N_DEV, chunk)).reshape(6, D_MODEL)

    g_in, g_cw = _exchange([t_in[0].astype(WIRE_DTYPE), conv_w[0]], scatter=False, name="gather_w_in")
    w_in_full = jnp.pad(g_in.reshape(IN_W, D_MODEL), ((0, IN_PAD - IN_W), (0, 0)))
    conv_w_full = jnp.transpose(g_cw, (1, 0, 2)).reshape(4, XBC_W)

    p = {k: v[0] for k, v in small_params.items()}
    gx, gw_in, (r_o, r_gu, r_d), blocks = _local_step(
        xs, tgt, mod, w_in_full, conv_w_full, w_o[0].astype(WIRE_DTYPE), t_gu[0].astype(WIRE_DTYPE),
        w_down[0].astype(WIRE_DTYPE), p)

    (r_in,) = _exchange([gw_in[:IN_W].reshape(N_DEV, IN_W // N_DEV, D_MODEL)], scatter=True, name="scatter_w_in")

    gathered = _exchange(list(blocks) + [c], scatter=False, name="gather_small")
    small, loss, c_all, dmod_all = _small_update(gathered, small_params)

    big = {
        "ada_w": _ada_w_update(c_all, dmod_all, ada_w2, m_ada_w[0], v_ada_w[0]),
        "w_in": [jnp.transpose(a) for a in _reduce_adamw(r_in, *t_in, "adamw_w_in")],
        "w_o": _reduce_adamw(r_o, w_o[0], m_w_o[0], v_w_o[0], "adamw_w_o"),
        "w_gate_up": [jnp.transpose(a) for a in _reduce_adamw(r_gu, *t_gu, "adamw_w_gate_up")],
        "w_down": _reduce_adamw(r_d, w_down[0], m_w_down[0], v_w_down[0], "adamw_w_down"),
    }
    big.update(small)

    order = ['ada_w', 'ada_b', 'norm1', 'w_in', 'conv_w', 'conv_b', 'dt_bias', 'A_log', 'D_skip', 'sinks',
             'attn_out_norm', 'ssm_out_norm', 'w_o', 'norm2', 'w_gate_up', 'w_down', 'rel_bias', 'final_norm']
    shapes = dict(ada_w=ada_w.shape, ada_b=ada_b.shape, norm1=norm1.shape, w_in=w_in.shape, conv_w=conv_w.shape,
                  conv_b=conv_b.shape, dt_bias=dt_bias.shape, A_log=A_log.shape, D_skip=D_skip.shape,
                  sinks=sinks.shape, attn_out_norm=attn_out_norm.shape, ssm_out_norm=ssm_out_norm.shape,
                  w_o=w_o.shape, norm2=norm2.shape, w_gate_up=w_gate_up.shape, w_down=w_down.shape,
                  rel_bias=rel_bias.shape, final_norm=final_norm.shape)
    outs = [[], [], [], []]
    for name in order:
        for kind in range(4):
            outs[kind].append(big[name][kind].reshape(shapes[name]))
    return (loss.reshape(()), gx.reshape(x.shape), *outs[0], *outs[1], *outs[2], *outs[3])
```

```python
import functools

import numpy as np
import jax
import jax.numpy as jnp
from jax import lax
from jax.experimental import pallas as pl
from jax.experimental.pallas import tpu as pltpu

F32 = jnp.float32
MXU_DTYPE = jnp.bfloat16
WIRE_DTYPE = jnp.bfloat16
HI = lax.Precision.HIGHEST
MESH = pl.DeviceIdType.MESH
N_DEV = 8

D_MODEL = 1024
ATTN_W = 512
KV_W = 128
SSM_W = 512
XBC_W = 1024
N_HEADS = 8
D_STATE = 128
D_FF = 2816
IN_W = 2312
IN_PAD = 2432
BLK = 128
N_BUCKETS = 32
EPS = 1e-6
LANE = 128
HALF = 64

ADAM_LR, ADAM_B1, ADAM_B2, ADAM_EPS, ADAM_WD, ADAM_STEP = 0.001, 0.9, 0.999, 1e-08, 0.01, 10

VMEM_BIG = 56 * 1024 * 1024


def _cparams(vmem=None):
    if vmem is None:
        return pltpu.CompilerParams()
    return pltpu.CompilerParams(vmem_limit_bytes=vmem)


def _mm(a, b):
    return jnp.dot(a.astype(MXU_DTYPE), b.astype(MXU_DTYPE), preferred_element_type=F32)


def _mm_nt(a, b):
    return lax.dot_general(a.astype(MXU_DTYPE), b.astype(MXU_DTYPE), (((1,), (1,)), ((), ())),
                           preferred_element_type=F32)


def _mm_tn(a, b):
    return lax.dot_general(a.astype(MXU_DTYPE), b.astype(MXU_DTYPE), (((0,), (0,)), ((), ())),
                           preferred_element_type=F32)


def _mm_hi(a, b):
    return jnp.dot(a, b, precision=HI, preferred_element_type=F32)


def _silu(x):
    return x * jax.nn.sigmoid(x)


def _softplus(x):
    return jnp.maximum(x, 0.0) + jnp.log1p(jnp.exp(-jnp.abs(x)))


def _rms(x, g, n):
    return x * lax.rsqrt(jnp.sum(x * x, axis=-1, keepdims=True) * (1.0 / n) + EPS) * g


def _modnorm(x, g, scale, shift):
    return _rms(x, g, x.shape[-1]) * (1.0 + scale) + shift


def _lane_iota(shape):
    return lax.broadcasted_iota(jnp.int32, shape, len(shape) - 1)


def _split_pair(t):
    lane = _lane_iota(t.shape)
    lo = jnp.where(lane < HALF, t, 0.0)
    hi = pltpu.roll(jnp.where(lane >= HALF, t, 0.0), HALF, 1)
    return lo, hi


def _join_pair(lo, hi):
    lane = _lane_iota(lo.shape)
    return jnp.where(lane < HALF, lo, pltpu.roll(hi, HALF, 1))


def _split_heads(t, n_pairs):
    out = []
    for p in range(n_pairs):
        out.extend(_split_pair(t[:, p * LANE:(p + 1) * LANE]))
    return out


def _join_heads(hs):
    return jnp.concatenate([_join_pair(hs[2 * p], hs[2 * p + 1]) for p in range(len(hs) // 2)], axis=1)


def _t5_bucket_table():
    dist = np.arange(BLK)[:, None] + BLK - np.arange(2 * BLK)[None, :]
    n = np.maximum(dist, 0)
    max_exact = N_BUCKETS // 2
    large = max_exact + (np.log(np.maximum(n, 1) / max_exact) / np.log(128 / max_exact)
                         * (N_BUCKETS - max_exact)).astype(np.int32)
    large = np.minimum(large, N_BUCKETS - 1)
    return np.where(n < max_exact, n, large).astype(np.int32)


def _my_pos():
    return lax.axis_index("x"), lax.axis_index("y"), lax.axis_index("c")


def _peer(k):
    x, y, c = _my_pos()
    return (1 - x if k & 4 else x, 1 - y if k & 2 else y, 1 - c if k & 1 else c)


def _lin(pos):
    return 4 * pos[0] + 2 * pos[1] + pos[2]


def _xchg_copies(ins, outs, sems, scatter):
    local_sem, send_sem, recv_sem = sems
    me = _lin(_my_pos())
    local, remote = [], []
    for a in range(len(ins)):
        src = ins[a].at[me] if scatter else ins[a]
        local.append(pltpu.make_async_copy(src, outs[a].at[me], local_sem.at[a]))
    for k in range(1, N_DEV):
        peer = _peer(k)
        for a in range(len(ins)):
            src = ins[a].at[_lin(peer)] if scatter else ins[a]
            remote.append(pltpu.make_async_remote_copy(src, outs[a].at[me], send_sem.at[a, k - 1],
                                                       recv_sem.at[a, k - 1], device_id=peer, device_id_type=MESH))
    return local, remote


def _xchg_start(ins, outs, sems, scatter):
    local, remote = _xchg_copies(ins, outs, sems, scatter)
    for cp in local + remote:
        cp.start()


def _xchg_wait(ins, outs, sems, scatter):
    local, remote = _xchg_copies(ins, outs, sems, scatter)
    for cp in local:
        cp.wait()
    for cp in remote:
        cp.wait_send()
        cp.wait_recv()


def _xchg_shapes(arrs, scatter):
    n = len(arrs)
    if scatter:
        out_shape = [jax.ShapeDtypeStruct(a.shape, a.dtype) for a in arrs]
    else:
        out_shape = [jax.ShapeDtypeStruct((N_DEV,) + a.shape, a.dtype) for a in arrs]
    sems = [pltpu.SemaphoreType.DMA((n,)), pltpu.SemaphoreType.DMA((n, N_DEV - 1)),
            pltpu.SemaphoreType.DMA((n, N_DEV - 1))]
    return out_shape, sems


def _exchange(arrs, scatter, name):
    n = len(arrs)
    out_shape, sems = _xchg_shapes(arrs, scatter)

    def body(*refs):
        ins, outs, s = refs[:n], refs[n:2 * n], refs[2 * n:]
        _xchg_start(ins, outs, s, scatter)
        _xchg_wait(ins, outs, s, scatter)

    hbm = pl.BlockSpec(memory_space=pltpu.HBM)
    return pl.pallas_call(body, name=name, out_shape=out_shape, in_specs=[hbm] * n, out_specs=[hbm] * n,
                          scratch_shapes=sems)(*arrs)


def _hosted_call(body, name, n_steps, in_specs, out_specs, out_shape, scratch_shapes, args, xchg, cparams):
    arrs, scatter = xchg
    n, n_in, n_out, n_scr = len(arrs), len(in_specs), len(out_specs), len(scratch_shapes)
    x_shape, x_sems = _xchg_shapes(arrs, scatter)

    def hosted(*refs):
        ins, refs = refs[:n_in], refs[n_in:]
        x_in, refs = refs[:n], refs[n:]
        outs, refs = refs[:n_out], refs[n_out:]
        x_out, refs = refs[:n], refs[n:]
        scr, sems = refs[:n_scr], refs[n_scr:]

        @pl.when(pl.program_id(0) == 0)
        def _():
            _xchg_start(x_in, x_out, sems, scatter)

        body(*ins, *outs, *scr)

        @pl.when(pl.program_id(0) == n_steps - 1)
        def _():
            _xchg_wait(x_in, x_out, sems, scatter)

    hbm = pl.BlockSpec(memory_space=pltpu.HBM)
    res = pl.pallas_call(
        hosted, name=name, grid=(n_steps,), in_specs=list(in_specs) + [hbm] * n,
        out_specs=list(out_specs) + [hbm] * n, out_shape=list(out_shape) + x_shape,
        scratch_shapes=list(scratch_shapes) + x_sems, compiler_params=cparams,
    )(*args, *arrs)
    return res[:n_out], res[n_out:]


def _mod_exchange(c, ada_w, ada_b8):
    chunk = ada_w.shape[1]

    def body(c_ref, w_ref, b_ref, out_ref, cbuf, part, s1, r1, s2, r2):
        me = _lin(_my_pos())
        first = []
        for k in range(1, N_DEV):
            cp = pltpu.make_async_remote_copy(c_ref, cbuf.at[me], s1.at[k - 1], r1.at[k - 1],
                                              device_id=_peer(k), device_id_type=MESH)
            cp.start()
            first.append(cp)
        cbuf[me] = c_ref[...]
        for cp in first:
            cp.wait_send()
            cp.wait_recv()
        cond = _silu(jnp.concatenate([cbuf[i] for i in range(N_DEV)], axis=0))
        mod = _mm_hi(cond, w_ref[...]) + b_ref[pl.ds(me, 1), :]
        for j in range(N_DEV):
            part[j] = mod[j:j + 1, :]
        second = []
        for k in range(1, N_DEV):
            peer = _peer(k)
            cp = pltpu.make_async_remote_copy(part.at[_lin(peer)], out_ref.at[me], s2.at[k - 1], r2.at[k - 1],
                                              device_id=peer, device_id_type=MESH)
            cp.start()
            second.append(cp)
        out_ref[me] = part[me]
        for cp in second:
            cp.wait_send()
            cp.wait_recv()

    vm = pl.BlockSpec(memory_space=pltpu.VMEM)
    return pl.pallas_call(
        body, name="mod_exchange", out_shape=jax.ShapeDtypeStruct((N_DEV, 1, chunk), F32),
        in_specs=[vm, vm, vm], out_specs=vm,
        scratch_shapes=[pltpu.VMEM((N_DEV, 1, D_MODEL), F32), pltpu.VMEM((N_DEV, 1, chunk), F32)]
        + [pltpu.SemaphoreType.DMA((N_DEV - 1,))] * 4,
    )(c, ada_w, ada_b8)


def _row(i):
    return (i, 0)


def _fixed(i):
    return (0, 0)


def _in_proj_fwd(x, norm1, scale1, shift1, w_in, tm):
    S = x.shape[0]

    def body(x_ref, n_ref, sc_ref, sh_ref, w_ref, qkv_ref, z_ref, xbc_ref, dt_ref):
        h = _modnorm(x_ref[...], n_ref[...], sc_ref[...], sh_ref[...])
        p = _mm_nt(h, w_ref[...])
        qkv_ref[...] = p[:, :768].astype(qkv_ref.dtype)
        z_ref[...] = p[:, 768:1280]
        xbc_ref[...] = p[:, 1280:2304]
        dt_ref[...] = p[:, 2304:IN_PAD]

    vec = pl.BlockSpec((1, D_MODEL), _fixed)
    return pl.pallas_call(
        body, name="in_proj_fwd", grid=(S // tm,),
        in_specs=[pl.BlockSpec((tm, D_MODEL), _row), vec, vec, vec, pl.BlockSpec((IN_PAD, D_MODEL), _fixed)],
        out_specs=[pl.BlockSpec((tm, 768), _row), pl.BlockSpec((tm, SSM_W), _row),
                   pl.BlockSpec((tm, XBC_W), _row), pl.BlockSpec((tm, LANE), _row)],
        out_shape=[jax.ShapeDtypeStruct((S, 768), MXU_DTYPE), jax.ShapeDtypeStruct((S, SSM_W), F32),
                   jax.ShapeDtypeStruct((S, XBC_W), F32), jax.ShapeDtypeStruct((S, LANE), F32)],
        compiler_params=_cparams(VMEM_BIG),
    )(x, norm1, scale1, shift1, w_in)


def _in_proj_bwd(x, dx1, dq, dkv, dz, dxbc, ddt, norm1, scale1, shift1, w_in, tm):
    S = x.shape[0]

    def body(x_ref, dx1_ref, dq_ref, dkv_ref, dz_ref, dxbc_ref, ddt_ref, n_ref, sc_ref, sh_ref, w_ref,
             gx_ref, h_ref, dp_ref, acc_ref):
        @pl.when(pl.program_id(0) == 0)
        def _():
            acc_ref[...] = jnp.zeros_like(acc_ref)

        h, vjp = jax.vjp(_modnorm, x_ref[...], n_ref[...], sc_ref[...], sh_ref[...])
        dp = jnp.concatenate([dq_ref[...].astype(MXU_DTYPE), dkv_ref[...].astype(MXU_DTYPE),
                              dz_ref[...].astype(MXU_DTYPE), dxbc_ref[...].astype(MXU_DTYPE),
                              ddt_ref[...].astype(MXU_DTYPE)], axis=1)
        dh = _mm(dp, w_ref[...])
        dx, dn, dsc, dsh = vjp(dh)
        gx_ref[...] = dx1_ref[...] + dx
        h_ref[...] = h.astype(h_ref.dtype)
        dp_ref[...] = dp
        acc_ref[0:1, :] += dn
        acc_ref[1:2, :] += dsc
        acc_ref[2:3, :] += dsh

    vec = pl.BlockSpec((1, D_MODEL), _fixed)
    return pl.pallas_call(
        body, name="in_proj_bwd", grid=(S // tm,),
        in_specs=[pl.BlockSpec((tm, D_MODEL), _row), pl.BlockSpec((tm, D_MODEL), _row),
                  pl.BlockSpec((tm, ATTN_W), _row), pl.BlockSpec((tm, 2 * KV_W), _row),
                  pl.BlockSpec((tm, SSM_W), _row), pl.BlockSpec((tm, XBC_W), _row), pl.BlockSpec((tm, LANE), _row),
                  vec, vec, vec, pl.BlockSpec((IN_PAD, D_MODEL), _fixed)],
        out_specs=[pl.BlockSpec((tm, D_MODEL), _row), pl.BlockSpec((tm, D_MODEL), _row),
                   pl.BlockSpec((tm, IN_PAD), _row), pl.BlockSpec((8, D_MODEL), _fixed)],
        out_shape=[jax.ShapeDtypeStruct((S, D_MODEL), F32), jax.ShapeDtypeStruct((S, D_MODEL), MXU_DTYPE),
                   jax.ShapeDtypeStruct((S, IN_PAD), MXU_DTYPE), jax.ShapeDtypeStruct((8, D_MODEL), F32)],
        compiler_params=_cparams(VMEM_BIG),
    )(x, dx1, dq, dkv, dz, dxbc, ddt, norm1, scale1, shift1, w_in)


def _out_stage(ya, ys0, ys1, z0, z1, an, sn0, sn1):
    half = SSM_W // 2
    a = _rms(ya, an, ATTN_W)
    g0 = _rms(ys0 * _silu(z0), sn0, half)
    g1 = _rms(ys1 * _silu(z1), sn1, half)
    return jnp.concatenate([a, g0, g1], axis=1)


def _out_stage_args(ya_ref, ys_ref, z_ref, an_ref, sn_ref):
    half = SSM_W // 2
    return (ya_ref[...], ys_ref[:, :half], ys_ref[:, half:], z_ref[:, :half], z_ref[:, half:],
            an_ref[...], sn_ref[:, :half], sn_ref[:, half:])


def _out_proj_fwd(x, ya, ys, z, an, sn, gate1, w_o, tm):
    S = x.shape[0]

    def body(x_ref, ya_ref, ys_ref, z_ref, an_ref, sn_ref, g_ref, w_ref, x1_ref):
        u = _out_stage(*_out_stage_args(ya_ref, ys_ref, z_ref, an_ref, sn_ref))
        x1_ref[...] = x_ref[...] + g_ref[...] * _mm(u, w_ref[...])

    half = pl.BlockSpec((tm, ATTN_W), _row)
    hvec = pl.BlockSpec((1, ATTN_W), _fixed)
    return pl.pallas_call(
        body, name="out_proj_fwd", grid=(S // tm,),
        in_specs=[pl.BlockSpec((tm, D_MODEL), _row), half, half, half, hvec, hvec,
                  pl.BlockSpec((1, D_MODEL), _fixed), pl.BlockSpec((D_MODEL, D_MODEL), _fixed)],
        out_specs=pl.BlockSpec((tm, D_MODEL), _row),
        out_shape=jax.ShapeDtypeStruct((S, D_MODEL), F32),
        compiler_params=_cparams(VMEM_BIG),
    )(x, ya, ys, z, an, sn, gate1, w_o)


def _out_proj_bwd(dx1, ya, ys, z, an, sn, gate1, w_o, tm):
    S = dx1.shape[0]

    def body(dx1_ref, ya_ref, ys_ref, z_ref, an_ref, sn_ref, g_ref, w_ref,
             dya_ref, dys_ref, dz_ref, u_ref, dmix_ref, acc_ref):
        @pl.when(pl.program_id(0) == 0)
        def _():
            acc_ref[...] = jnp.zeros_like(acc_ref)

        u, vjp = jax.vjp(_out_stage, *_out_stage_args(ya_ref, ys_ref, z_ref, an_ref, sn_ref))
        dx1 = dx1_ref[...]
        mix = _mm(u, w_ref[...])
        dmix = dx1 * g_ref[...]
        du = _mm_nt(dmix, w_ref[...])
        dya, dys0, dys1, dz0, dz1, dan, dsn0, dsn1 = vjp(du)
        dya_ref[...] = dya
        dys_ref[...] = jnp.concatenate([dys0, dys1], axis=1)
        dz_ref[...] = jnp.concatenate([dz0, dz1], axis=1)
        u_ref[...] = u.astype(u_ref.dtype)
        dmix_ref[...] = dmix.astype(dmix_ref.dtype)
        acc_ref[0:1, :] += jnp.sum(dx1 * mix, axis=0, keepdims=True)
        acc_ref[1:2, :] += jnp.concatenate([dan, dsn0, dsn1], axis=1)

    half = pl.BlockSpec((tm, ATTN_W), _row)
    hvec = pl.BlockSpec((1, ATTN_W), _fixed)
    full = pl.BlockSpec((tm, D_MODEL), _row)
    return pl.pallas_call(
        body, name="out_proj_bwd", grid=(S // tm,),
        in_specs=[full, half, half, half, hvec, hvec,
                  pl.BlockSpec((1, D_MODEL), _fixed), pl.BlockSpec((D_MODEL, D_MODEL), _fixed)],
        out_specs=[half, half, half, full, full, pl.BlockSpec((8, D_MODEL), _fixed)],
        out_shape=[jax.ShapeDtypeStruct((S, ATTN_W), F32)] * 3
        + [jax.ShapeDtypeStruct((S, D_MODEL), MXU_DTYPE)] * 2 + [jax.ShapeDtypeStruct((8, D_MODEL), F32)],
        compiler_params=_cparams(VMEM_BIG),
    )(dx1, ya, ys, z, an, sn, gate1, w_o)


def _loss_rows(x2, fn, tgt):
    y = _rms(x2, fn, D_MODEL)
    per_row = jnp.sum(jnp.square(y - tgt), axis=1, keepdims=True)
    return jnp.sum(per_row, axis=0, keepdims=True) * (0.5 / D_MODEL)


def _mlp_loss(x1, tgt, norm2, scale2, shift2, gate2, fnorm, w_gu, w_d, tm):
    S = x1.shape[0]

    def body(x1_ref, t_ref, n_ref, sc_ref, sh_ref, g_ref, fn_ref, wgu_hbm, wd_hbm,
             dx1_ref, h_ref, dgu_ref, act_ref, dmlp_ref, acc_ref, wgu, wd):
        @pl.when(pl.program_id(0) == 0)
        def _():
            acc_ref[...] = jnp.zeros_like(acc_ref)
            pltpu.sync_copy(wgu_hbm, wgu)
            pltpu.sync_copy(wd_hbm, wd)

        x1 = x1_ref[...]
        gate2 = g_ref[...]
        h, vjp_h = jax.vjp(_modnorm, x1, n_ref[...], sc_ref[...], sh_ref[...])
        hb = h.astype(MXU_DTYPE)
        gu = _mm_nt(hb, wgu[...])
        g, u = gu[:, :D_FF], gu[:, D_FF:]
        sg = jax.nn.sigmoid(g)
        silu_g = g * sg
        act = (silu_g * u).astype(MXU_DTYPE)
        mlp = _mm(act, wd[...])
        x2 = x1 + gate2 * mlp
        loss, vjp_loss = jax.vjp(_loss_rows, x2, fn_ref[...], t_ref[...])
        dx2, dfn, _ = vjp_loss(jnp.ones((1, 1), F32))
        dmlp = (dx2 * gate2).astype(MXU_DTYPE)
        dact = _mm_nt(dmlp, wd[...])
        dg = dact * u * (sg * (1.0 + g * (1.0 - sg)))
        du = dact * silu_g
        dgu = jnp.concatenate([dg, du], axis=1).astype(MXU_DTYPE)
        dh = _mm(dgu, wgu[...])
        dx, dn, dsc, dsh = vjp_h(dh)
        dx1_ref[...] = dx2 + dx
        h_ref[...] = hb
        dgu_ref[...] = dgu
        act_ref[...] = act
        dmlp_ref[...] = dmlp
        acc_ref[0:1, :] += dn
        acc_ref[1:2, :] += dsc
        acc_ref[2:3, :] += dsh
        acc_ref[3:4, :] += jnp.sum(dx2 * mlp, axis=0, keepdims=True)
        acc_ref[4:5, :] += dfn
        acc_ref[5:6, :] += jnp.broadcast_to(loss, (1, D_MODEL))

    full = pl.BlockSpec((tm, D_MODEL), _row)
    vec = pl.BlockSpec((1, D_MODEL), _fixed)
    anyspec = pl.BlockSpec(memory_space=pl.ANY)
    return pl.pallas_call(
        body, name="mlp_loss", grid=(S // tm,),
        in_specs=[full, full, vec, vec, vec, vec, vec, anyspec, anyspec],
        out_specs=[full, full, pl.BlockSpec((tm, 2 * D_FF), _row), pl.BlockSpec((tm, D_FF), _row), full,
                   pl.BlockSpec((8, D_MODEL), _fixed)],
        out_shape=[jax.ShapeDtypeStruct((S, D_MODEL), F32), jax.ShapeDtypeStruct((S, D_MODEL), MXU_DTYPE),
                   jax.ShapeDtypeStruct((S, 2 * D_FF), MXU_DTYPE), jax.ShapeDtypeStruct((S, D_FF), MXU_DTYPE),
                   jax.ShapeDtypeStruct((S, D_MODEL), MXU_DTYPE), jax.ShapeDtypeStruct((8, D_MODEL), F32)],
        scratch_shapes=[pltpu.VMEM((2 * D_FF, D_MODEL), MXU_DTYPE), pltpu.VMEM((D_FF, D_MODEL), MXU_DTYPE)],
        compiler_params=_cparams(VMEM_BIG),
    )(x1, tgt, norm2, scale2, shift2, gate2, fnorm, w_gu, w_d)


def _wgrad(a, g, tk, ts, name):
    S, K = a.shape
    N = g.shape[1]
    ns = S // ts

    def body(a_ref, g_ref, o_ref, acc_ref):
        s = pl.program_id(1)

        @pl.when(s == 0)
        def _():
            acc_ref[...] = jnp.zeros_like(acc_ref)

        acc_ref[...] += _mm_tn(a_ref[...], g_ref[...])

        @pl.when(s == ns - 1)
        def _():
            o_ref[...] = acc_ref[...].astype(o_ref.dtype)

    return pl.pallas_call(
        body, name=name, grid=(K // tk, ns),
        in_specs=[pl.BlockSpec((ts, tk), lambda j, s: (s, j)), pl.BlockSpec((ts, N), lambda j, s: (s, 0))],
        out_specs=pl.BlockSpec((tk, N), lambda j, s: (j, 0)),
        out_shape=jax.ShapeDtypeStruct((K, N), WIRE_DTYPE),
        scratch_shapes=[pltpu.VMEM((tk, N), F32)],
        compiler_params=_cparams(VMEM_BIG),
    )(a, g)


MASKED = -1e30
QK_SCALE = HALF ** -0.5


def _attn_bias(buckets, rel_bias):
    def body(bk_ref, relb_ref, out_ref):
        bk = bk_ref[...]
        i = lax.broadcasted_iota(jnp.int32, (BLK, 2 * BLK), 0)
        j = lax.broadcasted_iota(jnp.int32, (BLK, 2 * BLK), 1)
        window = (j > i) & (j <= i + BLK)
        for h in range(N_HEADS):
            acc = jnp.zeros((BLK, 2 * BLK), F32)
            for b in range(N_BUCKETS):
                acc = jnp.where(bk == b, relb_ref[b, h], acc)
            out_ref[0, h] = jnp.where(window, acc, MASKED)
            out_ref[1, h] = jnp.where(window & (j >= BLK), acc, MASKED)

    return pl.pallas_call(
        body, name="attn_bias", out_shape=jax.ShapeDtypeStruct((2, N_HEADS, BLK, 2 * BLK), F32),
        in_specs=[pl.BlockSpec(memory_space=pltpu.VMEM), pl.BlockSpec(memory_space=pltpu.SMEM)],
    )(buckets, rel_bias)


def _attn_kv(kvp_ref, kvc_ref):
    kvp = kvp_ref[...].astype(F32)
    kvc = kvc_ref[...].astype(F32)
    kp, kc = _split_pair(kvp[:, :LANE]), _split_pair(kvc[:, :LANE])
    vp, vc = _split_pair(kvp[:, LANE:]), _split_pair(kvc[:, LANE:])
    k_pads = [jnp.concatenate([kp[g], kc[g]], axis=0).astype(MXU_DTYPE) for g in range(2)]
    v_pads = [jnp.concatenate([vp[g], vc[g]], axis=0).astype(MXU_DTYPE) for g in range(2)]
    return k_pads, v_pads


def _attn_fwd(qkv, bias, sinks, xchg):
    S = qkv.shape[0]
    nb = S // BLK

    def body(q_ref, kvp_ref, kvc_ref, bias_ref, sinks_ref, y_ref):
        first = jnp.where(pl.program_id(0) == 0, 1, 0)
        q_heads = _split_heads(q_ref[...].astype(F32) * QK_SCALE, 4)
        k_pads, v_pads = _attn_kv(kvp_ref, kvc_ref)
        heads = range(N_HEADS)
        s = [_mm_nt(q_heads[h].astype(MXU_DTYPE), k_pads[h // 4]) + bias_ref[first, h] for h in heads]
        m = [jnp.maximum(jnp.max(s[h], axis=-1, keepdims=True), sinks_ref[h]) for h in heads]
        p = [jnp.exp(s[h] - m[h]) for h in heads]
        rinv = [1.0 / (jnp.sum(p[h], axis=-1, keepdims=True) + jnp.exp(sinks_ref[h] - m[h])) for h in heads]
        y_ref[...] = _join_heads([_mm(p[h], v_pads[h // 4]) * rinv[h] for h in heads])

    smem = pl.BlockSpec(memory_space=pltpu.SMEM)
    return _hosted_call(
        body, "attn_fwd", nb,
        in_specs=[pl.BlockSpec((BLK, ATTN_W), _row),
                  pl.BlockSpec((BLK, 2 * KV_W), lambda i: (jnp.maximum(i - 1, 0), 2)),
                  pl.BlockSpec((BLK, 2 * KV_W), lambda i: (i, 2)),
                  pl.BlockSpec((2, N_HEADS, BLK, 2 * BLK), lambda i: (0, 0, 0, 0)), smem],
        out_specs=[pl.BlockSpec((BLK, ATTN_W), _row)],
        out_shape=[jax.ShapeDtypeStruct((S, ATTN_W), F32)],
        scratch_shapes=[],
        args=(qkv, qkv, qkv, bias, sinks), xchg=xchg, cparams=_cparams(),
    )


def _attn_bwd(qkv, y, dy, bias, sinks, xchg):
    S = qkv.shape[0]
    nb = S // BLK

    def body(q_ref, kvp_ref, kvc_ref, y_ref, dy_ref, bias_ref, sinks_ref, dq_ref, dkv_ref, dbias_ref, dsk_ref, carry_ref):
        i = pl.program_id(0)

        @pl.when(i == 0)
        def _():
            dbias_ref[...] = jnp.zeros_like(dbias_ref)
            dsk_ref[...] = jnp.zeros_like(dsk_ref)
            carry_ref[...] = jnp.zeros_like(carry_ref)

        first = jnp.where(i == nb - 1, 1, 0)
        q_heads = _split_heads(q_ref[...].astype(F32) * QK_SCALE, 4)
        k_pads, v_pads = _attn_kv(kvp_ref, kvc_ref)
        y_heads = _split_heads(y_ref[...], 4)
        dy_heads = _split_heads(dy_ref[...], 4)
        heads = range(N_HEADS)
        qs = [q_heads[h].astype(MXU_DTYPE) for h in heads]
        s = [_mm_nt(qs[h], k_pads[h // 4]) + bias_ref[first, h] for h in heads]
        m = [jnp.maximum(jnp.max(s[h], axis=-1, keepdims=True), sinks_ref[h]) for h in heads]
        p = [jnp.exp(s[h] - m[h]) for h in heads]
        esink = [jnp.exp(sinks_ref[h] - m[h]) for h in heads]
        rinv = [1.0 / (jnp.sum(p[h], axis=-1, keepdims=True) + esink[h]) for h in heads]
        t = [dy_heads[h] * rinv[h] for h in heads]
        delta = [jnp.sum(t[h] * y_heads[h], axis=-1, keepdims=True) for h in heads]
        tb = [t[h].astype(MXU_DTYPE) for h in heads]
        dp = [_mm_nt(tb[h], v_pads[h // 4]) for h in heads]
        ds = [p[h] * (dp[h] - delta[h]) for h in heads]
        for h in heads:
            dbias_ref[h] += ds[h]
            dsk_ref[h] -= esink[h] * delta[h]
        dsb = [ds[h].astype(MXU_DTYPE) for h in heads]
        pb = [p[h].astype(MXU_DTYPE) for h in heads]
        dq_heads = [_mm(dsb[h], k_pads[h // 4]) * QK_SCALE for h in heads]
        dk_pads = [_mm_tn(jnp.concatenate(dsb[4 * g:4 * g + 4], axis=0), jnp.concatenate(qs[4 * g:4 * g + 4], axis=0))
                   for g in range(2)]
        dv_pads = [_mm_tn(jnp.concatenate(pb[4 * g:4 * g + 4], axis=0), jnp.concatenate(tb[4 * g:4 * g + 4], axis=0))
                   for g in range(2)]
        dq_ref[...] = _join_heads(dq_heads)
        dk_prev = _join_pair(dk_pads[0][:BLK], dk_pads[1][:BLK])
        dk_cur = _join_pair(dk_pads[0][BLK:], dk_pads[1][BLK:])
        dv_prev = _join_pair(dv_pads[0][:BLK], dv_pads[1][:BLK])
        dv_cur = _join_pair(dv_pads[0][BLK:], dv_pads[1][BLK:])
        dkv_ref[...] = jnp.concatenate([dk_cur, dv_cur], axis=1) + carry_ref[...]
        carry_ref[...] = jnp.concatenate([dk_prev, dv_prev], axis=1)

    smem = pl.BlockSpec(memory_space=pltpu.SMEM)
    rev = lambda i: (nb - 1 - i, 0)
    return _hosted_call(
        body, "attn_bwd", nb,
        in_specs=[pl.BlockSpec((BLK, ATTN_W), rev),
                  pl.BlockSpec((BLK, 2 * KV_W), lambda i: (jnp.maximum(nb - 2 - i, 0), 2)),
                  pl.BlockSpec((BLK, 2 * KV_W), lambda i: (nb - 1 - i, 2)),
                  pl.BlockSpec((BLK, ATTN_W), rev), pl.BlockSpec((BLK, ATTN_W), rev),
                  pl.BlockSpec((2, N_HEADS, BLK, 2 * BLK), lambda i: (0, 0, 0, 0)), smem],
        out_specs=[pl.BlockSpec((BLK, ATTN_W), rev), pl.BlockSpec((BLK, 2 * KV_W), rev),
                   pl.BlockSpec((N_HEADS, BLK, 2 * BLK), lambda i: (0, 0, 0)),
                   pl.BlockSpec((N_HEADS, BLK, 1), lambda i: (0, 0, 0))],
        out_shape=[jax.ShapeDtypeStruct((S, ATTN_W), F32), jax.ShapeDtypeStruct((S, 2 * KV_W), F32),
                   jax.ShapeDtypeStruct((N_HEADS, BLK, 2 * BLK), F32), jax.ShapeDtypeStruct((N_HEADS, BLK, 1), F32)],
        scratch_shapes=[pltpu.VMEM((BLK, 2 * KV_W), F32)],
        args=(qkv, qkv, qkv, y, dy, bias, sinks), xchg=xchg, cparams=_cparams(),
    )


def _attn_finish(dbias, dsk, buckets):
    def body(db_ref, dsk_ref, bk_ref, drel_ref, dsink_ref):
        bk = bk_ref[...]
        r = lax.broadcasted_iota(jnp.int32, (N_BUCKETS, LANE), 0)
        l = lax.broadcasted_iota(jnp.int32, (N_BUCKETS, LANE), 1)
        row = lax.broadcasted_iota(jnp.int32, (N_HEADS, LANE), 0)
        res = jnp.zeros((N_BUCKETS, LANE), F32)
        dsink = jnp.zeros((N_HEADS, LANE), F32)
        for h in range(N_HEADS):
            db = db_ref[h]
            for b in range(N_BUCKETS):
                v = jnp.sum(jnp.sum(jnp.where(bk == b, db, 0.0), axis=1, keepdims=True), axis=0, keepdims=True)
                res = res + jnp.where((r == b) & (l == h), v, 0.0)
            dsink = dsink + jnp.where(row == h, jnp.sum(dsk_ref[h], axis=0, keepdims=True), 0.0)
        drel_ref[...] = res
        dsink_ref[...] = dsink

    return pl.pallas_call(body, name="attn_finish",
                          out_shape=[jax.ShapeDtypeStruct((N_BUCKETS, LANE), F32),
                                     jax.ShapeDtypeStruct((N_HEADS, LANE), F32)])(dbias, dsk, buckets)


def _ssd_consts():
    r = lax.broadcasted_iota(jnp.int32, (BLK, BLK), 0)
    c = lax.broadcasted_iota(jnp.int32, (BLK, BLK), 1)
    causal = c <= r
    upper = (r <= c).astype(F32)
    last = r == BLK - 1
    head = lax.broadcasted_iota(jnp.int32, (N_HEADS, BLK), 0)
    return causal, upper, last, head


def _ssd_chunk(xs, bg, cg, dt_raw_t, prev, dtb, alog, d_rows, consts):
    causal, upper, last, head = consts
    dt_t = _softplus(dt_raw_t + dtb)
    acs_t = _mm_hi(dt_t * (-jnp.exp(alog)), upper)
    cb = [_mm_nt(cg[g], bg[g]) for g in range(2)]
    heads = range(N_HEADS)
    dt_row = [jnp.sum(jnp.where(head == h, dt_t, 0.0), axis=0, keepdims=True) for h in heads]
    a_row = [jnp.sum(jnp.where(head == h, acs_t, 0.0), axis=0, keepdims=True) for h in heads]
    a_rb = [jnp.broadcast_to(a_row[h], (BLK, BLK)) for h in heads]
    a_b = [a_rb[h].T for h in heads]
    a_last = [jnp.sum(jnp.where(last, a_b[h], 0.0), axis=0, keepdims=True) for h in heads]
    w = [cb[h // 4] * jnp.exp(jnp.where(causal, a_b[h] - a_rb[h], -1e30)) * dt_row[h] for h in heads]
    f_b = [jnp.broadcast_to(dt_row[h] * jnp.exp(a_last[h] - a_row[h]), (BLK, BLK)).T for h in heads]
    y_in = [_mm(w[h], xs[h]) for h in heads]
    y_off = [_mm(cg[h // 4], prev[h]) * jnp.exp(a_b[h]) for h in heads]
    st = [_mm_tn(bg[h // 4], xs[h] * f_b[h]) for h in heads]
    ys = [y_in[h] + y_off[h] + d_rows[h] * xs[h] for h in heads]
    hs = [prev[h] * jnp.exp(a_last[h]) + st[h] for h in heads]
    return tuple(ys), tuple(hs)


def _dt_rows(dt_blk):
    return dt_blk.T[:N_HEADS]


def _silu_grad(x):
    s = jax.nn.sigmoid(x)
    return s * (1.0 + x * (1.0 - s))


def _conv_pre(ext_ref, halo, blk, cw_ref, cb_ref):
    ext_ref[0:8, :] = halo
    ext_ref[8:8 + BLK, :] = blk
    pre = cb_ref[...] + cw_ref[0:1, :] * ext_ref[pl.ds(5, BLK), :]
    for k in range(1, 4):
        pre = pre + cw_ref[k:k + 1, :] * ext_ref[pl.ds(5 + k, BLK), :]
    return pre


def _ssd_split(pre):
    heads = _split_heads(pre[:, :SSM_W], 4)
    pb = [pre[:, SSM_W + g * D_STATE:SSM_W + (g + 1) * D_STATE] for g in range(2)]
    pc = [pre[:, SSM_W + 2 * D_STATE + g * D_STATE:SSM_W + 2 * D_STATE + (g + 1) * D_STATE] for g in range(2)]
    return heads, pb, pc


def _ssd_fwd(xbc, dt_raw, conv_w, conv_b, dtb_row, alog_row, d_exp, xchg):
    S = xbc.shape[0]
    nc = S // BLK

    def body(xbc_ref, halo_ref, dt_ref, cw_ref, cb_ref, dtb_ref, alog_ref, d_ref, y_ref, prev_ref, ext_ref, state_ref):
        i = pl.program_id(0)

        @pl.when(i == 0)
        def _():
            state_ref[...] = jnp.zeros_like(state_ref)

        halo = halo_ref[...] * jnp.where(i > 0, 1.0, 0.0)
        pre = _conv_pre(ext_ref, halo, xbc_ref[...], cw_ref, cb_ref)
        heads, pb, pc = _ssd_split(_silu(pre))
        prev = [state_ref[h] for h in range(N_HEADS)]
        for h in range(N_HEADS):
            prev_ref[0, h] = prev[h]
        d_rows = [d_ref[h:h + 1, :] for h in range(N_HEADS)]
        ys, hs = _ssd_chunk(heads, pb, pc, _dt_rows(dt_ref[...]), prev, dtb_ref[...], alog_ref[...], d_rows,
                            _ssd_consts())
        for h in range(N_HEADS):
            state_ref[h] = hs[h]
        y_ref[...] = _join_heads(ys)

    vec = pl.BlockSpec((N_HEADS, LANE), _fixed)
    return _hosted_call(
        body, "ssd_fwd", nc,
        in_specs=[pl.BlockSpec((BLK, XBC_W), _row),
                  pl.BlockSpec((8, XBC_W), lambda i: (jnp.maximum(i * (BLK // 8) - 1, 0), 0)),
                  pl.BlockSpec((BLK, LANE), _row),
                  pl.BlockSpec((4, XBC_W), _fixed), pl.BlockSpec((1, XBC_W), _fixed), vec, vec,
                  pl.BlockSpec((N_HEADS, LANE), _fixed)],
        out_specs=[pl.BlockSpec((BLK, SSM_W), _row),
                   pl.BlockSpec((1, N_HEADS, D_STATE, LANE), lambda i: (i, 0, 0, 0))],
        out_shape=[jax.ShapeDtypeStruct((S, SSM_W), F32), jax.ShapeDtypeStruct((nc, N_HEADS, D_STATE, LANE), F32)],
        scratch_shapes=[pltpu.VMEM((8 + BLK, XBC_W), F32), pltpu.VMEM((N_HEADS, D_STATE, LANE), F32)],
        args=(xbc, xbc, dt_raw, conv_w, conv_b, dtb_row, alog_row, d_exp), xchg=xchg, cparams=_cparams(),
    )


def _ssd_bwd(xbc, dt_raw, prev_states, dy, conv_w, conv_b, dtb_row, alog_row, d_exp, xchg):
    S = xbc.shape[0]
    nc = S // BLK

    def body(xbc_ref, halo_ref, dt_ref, prev_ref, dy_ref, cw_ref, cb_ref, dtb_ref, alog_ref, d_ref,
             dxbc_ref, ddt_ref, dcw_ref, dvec_ref, dd_ref, ext_ref, dpe_ref, gstate_ref, ghalo_ref):
        i = pl.program_id(0)
        c = nc - 1 - i

        @pl.when(i == 0)
        def _():
            gstate_ref[...] = jnp.zeros_like(gstate_ref)
            ghalo_ref[...] = jnp.zeros_like(ghalo_ref)
            dcw_ref[...] = jnp.zeros_like(dcw_ref)
            dvec_ref[...] = jnp.zeros_like(dvec_ref)
            dd_ref[...] = jnp.zeros_like(dd_ref)
            dpe_ref[...] = jnp.zeros_like(dpe_ref)

        halo = halo_ref[...] * jnp.where(c > 0, 1.0, 0.0)
        pre = _conv_pre(ext_ref, halo, xbc_ref[...], cw_ref, cb_ref)
        heads, pb, pc = _ssd_split(_silu(pre))
        prev = [prev_ref[0, h] for h in range(N_HEADS)]
        d_rows = [d_ref[h:h + 1, :] for h in range(N_HEADS)]
        _, vjp = jax.vjp(functools.partial(_ssd_chunk, consts=_ssd_consts()),
                         heads, pb, pc, _dt_rows(dt_ref[...]), prev, dtb_ref[...], alog_ref[...], d_rows)
        dys = tuple(_split_heads(dy_ref[...], 4))
        dhs = tuple(gstate_ref[h] for h in range(N_HEADS))
        dheads, dpb, dpc, ddt_t, dprev, ddtb, dalog, dd_rows = vjp((dys, dhs))
        for h in range(N_HEADS):
            gstate_ref[h] = dprev[h]
            dd_ref[h:h + 1, :] += dd_rows[h]
        ddt_ref[...] = jnp.concatenate([ddt_t, jnp.zeros((BLK - N_HEADS, BLK), F32)], axis=0).T
        dvec_ref[0:N_HEADS, :] += ddtb
        dvec_ref[N_HEADS:, :] += dalog
        dpre = jnp.concatenate([_join_heads(dheads)] + list(dpb) + list(dpc), axis=1) * _silu_grad(pre)
        dpe_ref[8:8 + BLK, :] = dpre
        dext = cw_ref[0:1, :] * dpe_ref[pl.ds(3, 8 + BLK), :]
        dcw_ref[0:1, :] += jnp.sum(dpre * ext_ref[pl.ds(5, BLK), :], axis=0, keepdims=True)
        for k in range(1, 4):
            dext = dext + cw_ref[k:k + 1, :] * dpe_ref[pl.ds(3 - k, 8 + BLK), :]
            dcw_ref[k:k + 1, :] += jnp.sum(dpre * ext_ref[pl.ds(5 + k, BLK), :], axis=0, keepdims=True)
        dcw_ref[4:5, :] += jnp.sum(dpre, axis=0, keepdims=True)
        dxbc_ref[...] = dext[8:, :]
        dxbc_ref[BLK - 8:BLK, :] += ghalo_ref[...]
        ghalo_ref[...] = dext[:8, :]

    vec = pl.BlockSpec((N_HEADS, LANE), _fixed)
    rev = lambda i: (nc - 1 - i, 0)
    return _hosted_call(
        body, "ssd_bwd", nc,
        in_specs=[pl.BlockSpec((BLK, XBC_W), rev),
                  pl.BlockSpec((8, XBC_W), lambda i: (jnp.maximum((nc - 1 - i) * (BLK // 8) - 1, 0), 0)),
                  pl.BlockSpec((BLK, LANE), rev),
                  pl.BlockSpec((1, N_HEADS, D_STATE, LANE), lambda i: (nc - 1 - i, 0, 0, 0)),
                  pl.BlockSpec((BLK, SSM_W), rev),
                  pl.BlockSpec((4, XBC_W), _fixed), pl.BlockSpec((1, XBC_W), _fixed), vec, vec,
                  pl.BlockSpec((N_HEADS, LANE), _fixed)],
        out_specs=[pl.BlockSpec((BLK, XBC_W), rev), pl.BlockSpec((BLK, LANE), rev),
                   pl.BlockSpec((8, XBC_W), _fixed), pl.BlockSpec((2 * N_HEADS, LANE), _fixed),
                   pl.BlockSpec((N_HEADS, LANE), _fixed)],
        out_shape=[jax.ShapeDtypeStruct((S, XBC_W), F32), jax.ShapeDtypeStruct((S, LANE), F32),
                   jax.ShapeDtypeStruct((8, XBC_W), F32), jax.ShapeDtypeStruct((2 * N_HEADS, LANE), F32),
                   jax.ShapeDtypeStruct((N_HEADS, LANE), F32)],
        scratch_shapes=[pltpu.VMEM((8 + BLK, XBC_W), F32), pltpu.VMEM((16 + BLK, XBC_W), F32),
                        pltpu.VMEM((N_HEADS, D_STATE, LANE), F32), pltpu.VMEM((8, XBC_W), F32)],
        args=(xbc, xbc, dt_raw, prev_states, dy, conv_w, conv_b, dtb_row, alog_row, d_exp), xchg=xchg,
        cparams=_cparams(VMEM_BIG),
    )


def _adamw_math(w, g, m, v):
    m = ADAM_B1 * m + (1.0 - ADAM_B1) * g
    v = ADAM_B2 * v + (1.0 - ADAM_B2) * jnp.square(g)
    m_hat = m / (1.0 - ADAM_B1 ** ADAM_STEP)
    v_hat = v / (1.0 - ADAM_B2 ** ADAM_STEP)
    delta = -ADAM_LR * (m_hat / (jnp.sqrt(v_hat) + ADAM_EPS) + ADAM_WD * w)
    return delta, m, v


def _reduce_adamw(parts, w, m, v, name):
    R, C = w.shape

    def body(p_ref, w_ref, m_ref, v_ref, g_ref, d_ref, nm_ref, nv_ref):
        g = p_ref[0].astype(F32)
        for i in range(1, N_DEV):
            g = g + p_ref[i].astype(F32)
        d, nm, nv = _adamw_math(w_ref[...], g, m_ref[...], v_ref[...])
        g_ref[...] = g
        d_ref[...] = d
        nm_ref[...] = nm
        nv_ref[...] = nv

    if R % 16 == 0:
        tr = max(t for t in range(16, 257, 16) if R % t == 0)
        n, blk, pblk = R // tr, pl.BlockSpec((tr, C), _row), pl.BlockSpec((N_DEV, tr, C), lambda i: (0, i, 0))
    else:
        tl = 256
        n, blk, pblk = C // tl, pl.BlockSpec((R, tl), lambda i: (0, i)), pl.BlockSpec((N_DEV, R, tl),
                                                                                      lambda i: (0, 0, i))
    return pl.pallas_call(
        body, name=name, grid=(n,), in_specs=[pblk, blk, blk, blk],
        out_specs=[blk] * 4, out_shape=[jax.ShapeDtypeStruct((R, C), F32)] * 4,
    )(parts, w, m, v)


_SMALL_NAMES = ("ada_b", "norm1", "conv_w", "conv_b", "dt_bias", "A_log", "D_skip", "sinks", "attn_out_norm",
                "ssm_out_norm", "norm2", "rel_bias", "final_norm")
N_MOD = 6 * D_MODEL


def _mod_row(a0, a1, a2):
    return jnp.concatenate([a0[2:3], a0[1:2], a1[0:1], a2[2:3], a2[1:2], a2[3:4]], axis=1)


def _small_update(gathered, params):
    n_g = len(gathered)
    flat = [a for name in _SMALL_NAMES for a in params[name]]

    def body(*refs):
        a0_ref, a1_ref, a2_ref, cw_ref, dv_ref, dd_ref, ds_ref, dr_ref, c_ref = refs[:n_g]
        wmv = refs[n_g:n_g + len(flat)]
        outs = refs[n_g + len(flat):]

        def total(ref):
            t = ref[0]
            for i in range(1, N_DEV):
                t = t + ref[i]
            return t

        t0, t1, t2, tcw, tdv, tdd, tds, tdr = [total(r) for r in (a0_ref, a1_ref, a2_ref, cw_ref, dv_ref, dd_ref,
                                                                   ds_ref, dr_ref)]
        r8 = lax.broadcasted_iota(jnp.int32, (N_HEADS, LANE), 0)
        l8 = lax.broadcasted_iota(jnp.int32, (N_HEADS, LANE), 1)

        def diag_row(t):
            return jnp.sum(jnp.where(r8 == l8, t, 0.0), axis=0, keepdims=True)[:, :N_HEADS]

        def lane_sums(t):
            return diag_row(jnp.broadcast_to(jnp.sum(t, axis=1, keepdims=True), (N_HEADS, LANE)))

        me = _lin(_my_pos())
        n_cw = XBC_W // N_DEV
        cw_mine = jnp.zeros((4, n_cw), F32)
        for j in range(N_DEV):
            cw_mine = cw_mine + tcw[0:4, j * n_cw:(j + 1) * n_cw] * jnp.where(me == j, 1.0, 0.0)
        grads = {
            "ada_b": _mod_row(t0, t1, t2), "norm1": t0[0:1], "conv_w": cw_mine, "conv_b": tcw[4:5],
            "dt_bias": lane_sums(tdv[:N_HEADS]), "A_log": lane_sums(tdv[N_HEADS:]), "D_skip": lane_sums(tdd),
            "sinks": diag_row(tds), "attn_out_norm": t1[1:2, :ATTN_W], "ssm_out_norm": t1[1:2, ATTN_W:],
            "norm2": t2[0:1], "rel_bias": tdr[:, :N_HEADS], "final_norm": t2[4:5],
        }
        for k, name in enumerate(_SMALL_NAMES):
            w_ref, m_ref, v_ref = wmv[3 * k:3 * k + 3]
            g = grads[name]
            d, nm, nv = _adamw_math(w_ref[...], g, m_ref[...], v_ref[...])
            for o, val in zip(outs[4 * k:4 * k + 4], (g, d, nm, nv)):
                o[...] = val
        loss_ref, call_ref, dmod_ref = outs[4 * len(_SMALL_NAMES):]
        loss_ref[...] = t2[5:6, 0:1]
        call_ref[...] = jnp.concatenate([c_ref[i] for i in range(N_DEV)], axis=0)
        dmod_ref[...] = jnp.concatenate([_mod_row(a0_ref[i], a1_ref[i], a2_ref[i]) for i in range(N_DEV)], axis=0)

    out_shape = [jax.ShapeDtypeStruct(params[name][0].shape, F32) for name in _SMALL_NAMES for _ in range(4)]
    out_shape += [jax.ShapeDtypeStruct((1, 1), F32), jax.ShapeDtypeStruct((N_DEV, D_MODEL), F32),
                  jax.ShapeDtypeStruct((N_DEV, N_MOD), F32)]
    res = pl.pallas_call(body, name="small_update", out_shape=out_shape)(*gathered, *flat)
    upd = {name: res[4 * k:4 * k + 4] for k, name in enumerate(_SMALL_NAMES)}
    loss, c_all, dmod_all = res[4 * len(_SMALL_NAMES):]
    return upd, loss, c_all, dmod_all


def _ada_w_update(c_all, dmod_all, w, m, v):
    chunk = w.shape[1]

    def body(c_ref, dm_ref, w_ref, m_ref, v_ref, g_ref, d_ref, nm_ref, nv_ref):
        me = _lin(_my_pos())
        dm = jnp.zeros((N_DEV, chunk), F32)
        for j in range(N_DEV):
            dm = dm + dm_ref[:, j * chunk:(j + 1) * chunk] * jnp.where(me == j, 1.0, 0.0)
        g = lax.dot_general(_silu(c_ref[...]), dm, (((0,), (0,)), ((), ())), precision=HI,
                            preferred_element_type=F32)
        d, nm, nv = _adamw_math(w_ref[...], g, m_ref[...], v_ref[...])
        g_ref[...] = g
        d_ref[...] = d
        nm_ref[...] = nm
        nv_ref[...] = nv

    return pl.pallas_call(body, name="ada_w_update", out_shape=[jax.ShapeDtypeStruct(w.shape, F32)] * 4,
                          compiler_params=_cparams(VMEM_BIG))(c_all, dmod_all, w, m, v)


def _local_step(x, tgt, mod, w_in, conv_w, w_o_mine, w_gu_mine, w_d_mine, p):
    S = x.shape[0]
    tm = min(512, S)
    tmm = min(256, S)
    shift1, scale1, gate1, shift2, scale2, gate2 = [mod[i:i + 1] for i in range(6)]
    buckets = jnp.asarray(_t5_bucket_table())
    per_head = lambda a: jnp.broadcast_to(a.reshape(N_HEADS, 1), (N_HEADS, LANE))
    dtb_row, alog_row, d_exp = per_head(p["dt_bias"]), per_head(p["A_log"]), per_head(p["D_skip"])
    sinks = p["sinks"].reshape(N_HEADS)

    qkv, z, xbc, dt_raw = _in_proj_fwd(x, p["norm1"], scale1, shift1, w_in, tm)
    bias = _attn_bias(buckets, p["rel_bias"])
    (ya,), (g_d,) = _attn_fwd(qkv, bias, sinks, ([w_d_mine], False))
    (ys, prev_states), (g_gu, g_o) = _ssd_fwd(xbc, dt_raw, conv_w, p["conv_b"], dtb_row, alog_row, d_exp,
                                              ([w_gu_mine, w_o_mine], False))
    w_gu = g_gu.reshape(2 * D_FF, D_MODEL)
    w_o = g_o.reshape(D_MODEL, D_MODEL)
    w_d = g_d.reshape(D_FF, D_MODEL)
    x1 = _out_proj_fwd(x, ya, ys, z, p["attn_out_norm"], p["ssm_out_norm"], gate1, w_o, tm)
    dx1, h2, dgu, act, dmlp, acc2 = _mlp_loss(x1, tgt, p["norm2"], scale2, shift2, gate2, p["final_norm"],
                                              w_gu, w_d, tmm)
    g_w_gu = _wgrad(dgu, h2, 2 * D_FF // 4, tm, "wgrad_gate_up")
    g_w_d = _wgrad(act, dmlp, D_FF // 2, tm, "wgrad_down")
    dya, dys, dz, u, dmix, acc1 = _out_proj_bwd(dx1, ya, ys, z, p["attn_out_norm"], p["ssm_out_norm"], gate1, w_o, tm)
    g_w_o = _wgrad(u, dmix, D_MODEL, tm, "wgrad_out")
    (dq, dkv, dbias, dsk), (r_gu,) = _attn_bwd(qkv, ya, dya, bias, sinks,
                                               ([g_w_gu.reshape(N_DEV, 2 * D_FF // N_DEV, D_MODEL)], True))
    drel, dsink = _attn_finish(dbias, dsk, buckets)
    (dxbc, ddt, dcw, dvec, dd), (r_d, r_o) = _ssd_bwd(
        xbc, dt_raw, prev_states, dys, conv_w, p["conv_b"], dtb_row, alog_row, d_exp,
        ([g_w_d.reshape(N_DEV, D_FF // N_DEV, D_MODEL), g_w_o.reshape(N_DEV, D_MODEL // N_DEV, D_MODEL)], True))
    gx, h1, dproj, acc0 = _in_proj_bwd(x, dx1, dq, dkv, dz, dxbc, ddt, p["norm1"], scale1, shift1, w_in, tm)
    g_w_in = _wgrad(dproj, h1, IN_PAD, tm, "wgrad_in")
    return gx, g_w_in, (r_o, r_gu, r_d), (acc0, acc1, acc2, dcw, dvec, dd, dsink, drel)


def kernel(x, c, ada_w, ada_b, norm1, w_in, conv_w, conv_b, dt_bias, A_log, D_skip, sinks, attn_out_norm, ssm_out_norm, w_o, norm2, w_gate_up, w_down, rel_bias, final_norm, loss_target, m_ada_w, m_ada_b, m_norm1, m_w_in, m_conv_w, m_conv_b, m_dt_bias, m_A_log, m_D_skip, m_sinks, m_attn_out_norm, m_ssm_out_norm, m_w_o, m_norm2, m_w_gate_up, m_w_down, m_rel_bias, m_final_norm, v_ada_w, v_ada_b, v_norm1, v_w_in, v_conv_w, v_conv_b, v_dt_bias, v_A_log, v_D_skip, v_sinks, v_attn_out_norm, v_ssm_out_norm, v_w_o, v_norm2, v_w_gate_up, v_w_down, v_rel_bias, v_final_norm):
    two_d = lambda a: a if a.ndim == 2 else a.reshape(-1, a.shape[-1])
    small_params = dict(
        ada_b=(ada_b, m_ada_b, v_ada_b), norm1=(norm1, m_norm1, v_norm1), conv_w=(conv_w, m_conv_w, v_conv_w),
        conv_b=(conv_b, m_conv_b, v_conv_b), dt_bias=(dt_bias, m_dt_bias, v_dt_bias), A_log=(A_log, m_A_log, v_A_log),
        D_skip=(D_skip, m_D_skip, v_D_skip), sinks=(sinks, m_sinks, v_sinks),
        attn_out_norm=(attn_out_norm, m_attn_out_norm, v_attn_out_norm),
        ssm_out_norm=(ssm_out_norm, m_ssm_out_norm, v_ssm_out_norm), norm2=(norm2, m_norm2, v_norm2),
        rel_bias=(rel_bias, m_rel_bias, v_rel_bias), final_norm=(final_norm, m_final_norm, v_final_norm))
    small_params = {k: tuple(two_d(a) for a in v) for k, v in small_params.items()}
    S = x.shape[1]
    xs, tgt = x.reshape(S, D_MODEL), loss_target.reshape(S, D_MODEL)
    ada_w2 = ada_w[0]
    chunk = ada_w2.shape[1]
    t_in = [jnp.transpose(a[0]) for a in (w_in, m_w_in, v_w_in)]
    t_gu = [jnp.transpose(a[0]) for a in (w_gate_up, m_w_gate_up, v_w_gate_up)]

    mod = _mod_exchange(c, ada_w2, ada_b.reshape(N_DEV, chunk)).reshape(6, D_MODEL)

    g_in, g_cw = _exchange([t_in[0].astype(WIRE_DTYPE), conv_w[0]], scatter=False, name="gather_w_in")
    w_in_full = jnp.pad(g_in.reshape(IN_W, D_MODEL), ((0, IN_PAD - IN_W), (0, 0)))
    conv_w_full = jnp.transpose(g_cw, (1, 0, 2)).reshape(4, XBC_W)

    p = {k: v[0] for k, v in small_params.items()}
    gx, gw_in, (r_o, r_gu, r_d), blocks = _local_step(
        xs, tgt, mod, w_in_full, conv_w_full, w_o[0].astype(WIRE_DTYPE), t_gu[0].astype(WIRE_DTYPE),
        w_down[0].astype(WIRE_DTYPE), p)

    (r_in,) = _exchange([gw_in[:IN_W].reshape(N_DEV, IN_W // N_DEV, D_MODEL)], scatter=True, name="scatter_w_in")

    gathered = _exchange(list(blocks) + [c], scatter=False, name="gather_small")
    small, loss, c_all, dmod_all = _small_update(gathered, small_params)

    big = {
        "ada_w": _ada_w_update(c_all, dmod_all, ada_w2, m_ada_w[0], v_ada_w[0]),
        "w_in": [jnp.transpose(a) for a in _reduce_adamw(r_in, *t_in, "adamw_w_in")],
        "w_o": _reduce_adamw(r_o, w_o[0], m_w_o[0], v_w_o[0], "adamw_w_o"),
        "w_gate_up": [jnp.transpose(a) for a in _reduce_adamw(r_gu, *t_gu, "adamw_w_gate_up")],
        "w_down": _reduce_adamw(r_d, w_down[0], m_w_down[0], v_w_down[0], "adamw_w_down"),
    }
    big.update(small)

    order = ['ada_w', 'ada_b', 'norm1', 'w_in', 'conv_w', 'conv_b', 'dt_bias', 'A_log', 'D_skip', 'sinks',
             'attn_out_norm', 'ssm_out_norm', 'w_o', 'norm2', 'w_gate_up', 'w_down', 'rel_bias', 'final_norm']
    shapes = dict(ada_w=ada_w.shape, ada_b=ada_b.shape, norm1=norm1.shape, w_in=w_in.shape, conv_w=conv_w.shape,
                  conv_b=conv_b.shape, dt_bias=dt_bias.shape, A_log=A_log.shape, D_skip=D_skip.shape,
                  sinks=sinks.shape, attn_out_norm=attn_out_norm.shape, ssm_out_norm=ssm_out_norm.shape,
                  w_o=w_o.shape, norm2=norm2.shape, w_gate_up=w_gate_up.shape, w_down=w_down.shape,
                  rel_bias=rel_bias.shape, final_norm=final_norm.shape)
    outs = [[], [], [], []]
    for name in order:
        for kind in range(4):
            outs[kind].append(big[name][kind].reshape(shapes[name]))
    return (loss.reshape(()), gx.reshape(x.shape), *outs[0], *outs[1], *outs[2], *outs[3])
```

```python
import functools

import numpy as np
import jax
import jax.numpy as jnp
from jax import lax
from jax.experimental import pallas as pl
from jax.experimental.pallas import tpu as pltpu

F32 = jnp.float32
MXU_DTYPE = jnp.bfloat16
WIRE_DTYPE = jnp.bfloat16
HI = lax.Precision.HIGHEST
MESH = pl.DeviceIdType.MESH
N_DEV = 8

D_MODEL = 1024
ATTN_W = 512
KV_W = 128
SSM_W = 512
XBC_W = 1024
N_HEADS = 8
D_STATE = 128
D_FF = 2816
IN_W = 2312
IN_PAD = 2432
BLK = 128
N_BUCKETS = 32
EPS = 1e-6
LANE = 128
HALF = 64

ADAM_LR, ADAM_B1, ADAM_B2, ADAM_EPS, ADAM_WD, ADAM_STEP = 0.001, 0.9, 0.999, 1e-08, 0.01, 10

VMEM_BIG = 56 * 1024 * 1024


def _cparams(vmem=None):
    if vmem is None:
        return pltpu.CompilerParams()
    return pltpu.CompilerParams(vmem_limit_bytes=vmem)


def _mm(a, b):
    return jnp.dot(a.astype(MXU_DTYPE), b.astype(MXU_DTYPE), preferred_element_type=F32)


def _mm_nt(a, b):
    return lax.dot_general(a.astype(MXU_DTYPE), b.astype(MXU_DTYPE), (((1,), (1,)), ((), ())),
                           preferred_element_type=F32)


def _mm_tn(a, b):
    return lax.dot_general(a.astype(MXU_DTYPE), b.astype(MXU_DTYPE), (((0,), (0,)), ((), ())),
                           preferred_element_type=F32)


def _mm_hi(a, b):
    return jnp.dot(a, b, precision=HI, preferred_element_type=F32)


def _silu(x):
    return x * jax.nn.sigmoid(x)


def _softplus(x):
    return jnp.maximum(x, 0.0) + jnp.log1p(jnp.exp(-jnp.abs(x)))


def _rms(x, g, n):
    return x * lax.rsqrt(jnp.sum(x * x, axis=-1, keepdims=True) * (1.0 / n) + EPS) * g


def _modnorm(x, g, scale, shift):
    return _rms(x, g, x.shape[-1]) * (1.0 + scale) + shift


def _lane_iota(shape):
    return lax.broadcasted_iota(jnp.int32, shape, len(shape) - 1)


def _split_pair(t):
    lane = _lane_iota(t.shape)
    lo = jnp.where(lane < HALF, t, 0.0)
    hi = pltpu.roll(jnp.where(lane >= HALF, t, 0.0), HALF, 1)
    return lo, hi


def _join_pair(lo, hi):
    lane = _lane_iota(lo.shape)
    return jnp.where(lane < HALF, lo, pltpu.roll(hi, HALF, 1))


def _split_heads(t, n_pairs):
    out = []
    for p in range(n_pairs):
        out.extend(_split_pair(t[:, p * LANE:(p + 1) * LANE]))
    return out


def _join_heads(hs):
    return jnp.concatenate([_join_pair(hs[2 * p], hs[2 * p + 1]) for p in range(len(hs) // 2)], axis=1)


def _t5_bucket_table():
    dist = np.arange(BLK)[:, None] + BLK - np.arange(2 * BLK)[None, :]
    n = np.maximum(dist, 0)
    max_exact = N_BUCKETS // 2
    large = max_exact + (np.log(np.maximum(n, 1) / max_exact) / np.log(128 / max_exact)
                         * (N_BUCKETS - max_exact)).astype(np.int32)
    large = np.minimum(large, N_BUCKETS - 1)
    return np.where(n < max_exact, n, large).astype(np.int32)


def _my_pos():
    return lax.axis_index("x"), lax.axis_index("y"), lax.axis_index("c")


def _peer(k):
    x, y, c = _my_pos()
    return (1 - x if k & 4 else x, 1 - y if k & 2 else y, 1 - c if k & 1 else c)


def _lin(pos):
    return 4 * pos[0] + 2 * pos[1] + pos[2]


def _xchg_copies(ins, outs, sems, scatter):
    local_sem, send_sem, recv_sem = sems
    me = _lin(_my_pos())
    local, remote = [], []
    for a in range(len(ins)):
        src = ins[a].at[me] if scatter else ins[a]
        local.append(pltpu.make_async_copy(src, outs[a].at[me], local_sem.at[a]))
    for k in range(1, N_DEV):
        peer = _peer(k)
        for a in range(len(ins)):
            src = ins[a].at[_lin(peer)] if scatter else ins[a]
            remote.append(pltpu.make_async_remote_copy(src, outs[a].at[me], send_sem.at[a, k - 1],
                                                       recv_sem.at[a, k - 1], device_id=peer, device_id_type=MESH))
    return local, remote


def _xchg_start(ins, outs, sems, scatter):
    local, remote = _xchg_copies(ins, outs, sems, scatter)
    for cp in local + remote:
        cp.start()


def _xchg_wait(ins, outs, sems, scatter):
    local, remote = _xchg_copies(ins, outs, sems, scatter)
    for cp in local:
        cp.wait()
    for cp in remote:
        cp.wait_send()
        cp.wait_recv()


def _xchg_shapes(arrs, scatter):
    n = len(arrs)
    if scatter:
        out_shape = [jax.ShapeDtypeStruct(a.shape, a.dtype) for a in arrs]
    else:
        out_shape = [jax.ShapeDtypeStruct((N_DEV,) + a.shape, a.dtype) for a in arrs]
    sems = [pltpu.SemaphoreType.DMA((n,)), pltpu.SemaphoreType.DMA((n, N_DEV - 1)),
            pltpu.SemaphoreType.DMA((n, N_DEV - 1))]
    return out_shape, sems


def _exchange(arrs, scatter, name):
    n = len(arrs)
    out_shape, sems = _xchg_shapes(arrs, scatter)

    def body(*refs):
        ins, outs, s = refs[:n], refs[n:2 * n], refs[2 * n:]
        _xchg_start(ins, outs, s, scatter)
        _xchg_wait(ins, outs, s, scatter)

    hbm = pl.BlockSpec(memory_space=pltpu.HBM)
    return pl.pallas_call(body, name=name, out_shape=out_shape, in_specs=[hbm] * n, out_specs=[hbm] * n,
                          scratch_shapes=sems)(*arrs)


def _hosted_call(body, name, n_steps, in_specs, out_specs, out_shape, scratch_shapes, args, xchg, cparams):
    arrs, scatter = xchg
    n, n_in, n_out, n_scr = len(arrs), len(in_specs), len(out_specs), len(scratch_shapes)
    x_shape, x_sems = _xchg_shapes(arrs, scatter)

    def hosted(*refs):
        ins, refs = refs[:n_in], refs[n_in:]
        x_in, refs = refs[:n], refs[n:]
        outs, refs = refs[:n_out], refs[n_out:]
        x_out, refs = refs[:n], refs[n:]
        scr, sems = refs[:n_scr], refs[n_scr:]

        @pl.when(pl.program_id(0) == 0)
        def _():
            _xchg_start(x_in, x_out, sems, scatter)

        body(*ins, *outs, *scr)

        @pl.when(pl.program_id(0) == n_steps - 1)
        def _():
            _xchg_wait(x_in, x_out, sems, scatter)

    hbm = pl.BlockSpec(memory_space=pltpu.HBM)
    res = pl.pallas_call(
        hosted, name=name, grid=(n_steps,), in_specs=list(in_specs) + [hbm] * n,
        out_specs=list(out_specs) + [hbm] * n, out_shape=list(out_shape) + x_shape,
        scratch_shapes=list(scratch_shapes) + x_sems, compiler_params=cparams,
    )(*args, *arrs)
    return res[:n_out], res[n_out:]


def _mod_exchange(c, ada_w, ada_b8):
    chunk = ada_w.shape[1]

    def body(c_ref, w_ref, b_ref, out_ref, cbuf, part, s1, r1, s2, r2):
        me = _lin(_my_pos())
        first = []
        for k in range(1, N_DEV):
            cp = pltpu.make_async_remote_copy(c_ref, cbuf.at[me], s1.at[k - 1], r1.at[k - 1],
                                              device_id=_peer(k), device_id_type=MESH)
            cp.start()
            first.append(cp)
        cbuf[me] = c_ref[...]
        for cp in first:
            cp.wait_send()
            cp.wait_recv()
        cond = _silu(jnp.concatenate([cbuf[i] for i in range(N_DEV)], axis=0))
        mod = _mm_hi(cond, w_ref[...]) + b_ref[pl.ds(me, 1), :]
        for j in range(N_DEV):
            part[j] = mod[j:j + 1, :]
        second = []
        for k in range(1, N_DEV):
            peer = _peer(k)
            cp = pltpu.make_async_remote_copy(part.at[_lin(peer)], out_ref.at[me], s2.at[k - 1], r2.at[k - 1],
                                              device_id=peer, device_id_type=MESH)
            cp.start()
            second.append(cp)
        out_ref[me] = part[me]
        for cp in second:
            cp.wait_send()
            cp.wait_recv()

    vm = pl.BlockSpec(memory_space=pltpu.VMEM)
    return pl.pallas_call(
        body, name="mod_exchange", out_shape=jax.ShapeDtypeStruct((N_DEV, 1, chunk), F32),
        in_specs=[vm, vm, vm], out_specs=vm,
        scratch_shapes=[pltpu.VMEM((N_DEV, 1, D_MODEL), F32), pltpu.VMEM((N_DEV, 1, chunk), F32)]
        + [pltpu.SemaphoreType.DMA((N_DEV - 1,))] * 4,
    )(c, ada_w, ada_b8)


def _row(i):
    return (i, 0)


def _fixed(i):
    return (0, 0)


def _in_proj_fwd(x, norm1, scale1, shift1, w_in, tm, xchg):
    S = x.shape[0]

    def body(x_ref, n_ref, sc_ref, sh_ref, w_ref, qkv_ref, z_ref, xbc_ref, dt_ref):
        h = _modnorm(x_ref[...], n_ref[...], sc_ref[...], sh_ref[...])
        p = _mm_nt(h, w_ref[...])
        qkv_ref[...] = p[:, :768].astype(qkv_ref.dtype)
        z_ref[...] = p[:, 768:1280]
        xbc_ref[...] = p[:, 1280:2304]
        dt_ref[...] = p[:, 2304:IN_PAD]

    vec = pl.BlockSpec((1, D_MODEL), _fixed)
    return _hosted_call(
        body, "in_proj_fwd", S // tm,
        in_specs=[pl.BlockSpec((tm, D_MODEL), _row), vec, vec, vec, pl.BlockSpec((IN_PAD, D_MODEL), _fixed)],
        out_specs=[pl.BlockSpec((tm, 768), _row), pl.BlockSpec((tm, SSM_W), _row),
                   pl.BlockSpec((tm, XBC_W), _row), pl.BlockSpec((tm, LANE), _row)],
        out_shape=[jax.ShapeDtypeStruct((S, 768), MXU_DTYPE), jax.ShapeDtypeStruct((S, SSM_W), F32),
                   jax.ShapeDtypeStruct((S, XBC_W), F32), jax.ShapeDtypeStruct((S, LANE), F32)],
        scratch_shapes=[], args=(x, norm1, scale1, shift1, w_in), xchg=xchg, cparams=_cparams(VMEM_BIG),
    )


def _in_proj_bwd(x, dx1, dq, dkv, dz, dxbc, ddt, norm1, scale1, shift1, w_in, tm):
    S = x.shape[0]

    def body(x_ref, dx1_ref, dq_ref, dkv_ref, dz_ref, dxbc_ref, ddt_ref, n_ref, sc_ref, sh_ref, w_ref,
             gx_ref, h_ref, dp_ref, acc_ref):
        @pl.when(pl.program_id(0) == 0)
        def _():
            acc_ref[...] = jnp.zeros_like(acc_ref)

        h, vjp = jax.vjp(_modnorm, x_ref[...], n_ref[...], sc_ref[...], sh_ref[...])
        dp = jnp.concatenate([dq_ref[...].astype(MXU_DTYPE), dkv_ref[...].astype(MXU_DTYPE),
                              dz_ref[...].astype(MXU_DTYPE), dxbc_ref[...].astype(MXU_DTYPE),
                              ddt_ref[...].astype(MXU_DTYPE)], axis=1)
        dh = _mm(dp, w_ref[...])
        dx, dn, dsc, dsh = vjp(dh)
        gx_ref[...] = dx1_ref[...] + dx
        h_ref[...] = h.astype(h_ref.dtype)
        dp_ref[...] = dp
        acc_ref[0:1, :] += dn
        acc_ref[1:2, :] += dsc
        acc_ref[2:3, :] += dsh

    vec = pl.BlockSpec((1, D_MODEL), _fixed)
    return pl.pallas_call(
        body, name="in_proj_bwd", grid=(S // tm,),
        in_specs=[pl.BlockSpec((tm, D_MODEL), _row), pl.BlockSpec((tm, D_MODEL), _row),
                  pl.BlockSpec((tm, ATTN_W), _row), pl.BlockSpec((tm, 2 * KV_W), _row),
                  pl.BlockSpec((tm, SSM_W), _row), pl.BlockSpec((tm, XBC_W), _row), pl.BlockSpec((tm, LANE), _row),
                  vec, vec, vec, pl.BlockSpec((IN_PAD, D_MODEL), _fixed)],
        out_specs=[pl.BlockSpec((tm, D_MODEL), _row), pl.BlockSpec((tm, D_MODEL), _row),
                   pl.BlockSpec((tm, IN_PAD), _row), pl.BlockSpec((8, D_MODEL), _fixed)],
        out_shape=[jax.ShapeDtypeStruct((S, D_MODEL), F32), jax.ShapeDtypeStruct((S, D_MODEL), MXU_DTYPE),
                   jax.ShapeDtypeStruct((S, IN_PAD), MXU_DTYPE), jax.ShapeDtypeStruct((8, D_MODEL), F32)],
        compiler_params=_cparams(VMEM_BIG),
    )(x, dx1, dq, dkv, dz, dxbc, ddt, norm1, scale1, shift1, w_in)


def _out_stage(ya, ys0, ys1, z0, z1, an, sn0, sn1):
    half = SSM_W // 2
    a = _rms(ya, an, ATTN_W)
    g0 = _rms(ys0 * _silu(z0), sn0, half)
    g1 = _rms(ys1 * _silu(z1), sn1, half)
    return jnp.concatenate([a, g0, g1], axis=1)


def _out_stage_args(ya_ref, ys_ref, z_ref, an_ref, sn_ref):
    half = SSM_W // 2
    return (ya_ref[...], ys_ref[:, :half], ys_ref[:, half:], z_ref[:, :half], z_ref[:, half:],
            an_ref[...], sn_ref[:, :half], sn_ref[:, half:])


def _out_proj_fwd(x, ya, ys, z, an, sn, gate1, w_o, tm):
    S = x.shape[0]

    def body(x_ref, ya_ref, ys_ref, z_ref, an_ref, sn_ref, g_ref, w_ref, x1_ref):
        u = _out_stage(*_out_stage_args(ya_ref, ys_ref, z_ref, an_ref, sn_ref))
        x1_ref[...] = x_ref[...] + g_ref[...] * _mm(u, w_ref[...])

    half = pl.BlockSpec((tm, ATTN_W), _row)
    hvec = pl.BlockSpec((1, ATTN_W), _fixed)
    return pl.pallas_call(
        body, name="out_proj_fwd", grid=(S // tm,),
        in_specs=[pl.BlockSpec((tm, D_MODEL), _row), half, half, half, hvec, hvec,
                  pl.BlockSpec((1, D_MODEL), _fixed), pl.BlockSpec((D_MODEL, D_MODEL), _fixed)],
        out_specs=pl.BlockSpec((tm, D_MODEL), _row),
        out_shape=jax.ShapeDtypeStruct((S, D_MODEL), F32),
        compiler_params=_cparams(VMEM_BIG),
    )(x, ya, ys, z, an, sn, gate1, w_o)


def _out_proj_bwd(dx1, ya, ys, z, an, sn, gate1, w_o, tm):
    S = dx1.shape[0]

    def body(dx1_ref, ya_ref, ys_ref, z_ref, an_ref, sn_ref, g_ref, w_ref,
             dya_ref, dys_ref, dz_ref, u_ref, dmix_ref, acc_ref):
        @pl.when(pl.program_id(0) == 0)
        def _():
            acc_ref[...] = jnp.zeros_like(acc_ref)

        u, vjp = jax.vjp(_out_stage, *_out_stage_args(ya_ref, ys_ref, z_ref, an_ref, sn_ref))
        dx1 = dx1_ref[...]
        mix = _mm(u, w_ref[...])
        dmix = dx1 * g_ref[...]
        du = _mm_nt(dmix, w_ref[...])
        dya, dys0, dys1, dz0, dz1, dan, dsn0, dsn1 = vjp(du)
        dya_ref[...] = dya
        dys_ref[...] = jnp.concatenate([dys0, dys1], axis=1)
        dz_ref[...] = jnp.concatenate([dz0, dz1], axis=1)
        u_ref[...] = u.astype(u_ref.dtype)
        dmix_ref[...] = dmix.astype(dmix_ref.dtype)
        acc_ref[0:1, :] += jnp.sum(dx1 * mix, axis=0, keepdims=True)
        acc_ref[1:2, :] += jnp.concatenate([dan, dsn0, dsn1], axis=1)

    half = pl.BlockSpec((tm, ATTN_W), _row)
    hvec = pl.BlockSpec((1, ATTN_W), _fixed)
    full = pl.BlockSpec((tm, D_MODEL), _row)
    return pl.pallas_call(
        body, name="out_proj_bwd", grid=(S // tm,),
        in_specs=[full, half, half, half, hvec, hvec,
                  pl.BlockSpec((1, D_MODEL), _fixed), pl.BlockSpec((D_MODEL, D_MODEL), _fixed)],
        out_specs=[half, half, half, full, full, pl.BlockSpec((8, D_MODEL), _fixed)],
        out_shape=[jax.ShapeDtypeStruct((S, ATTN_W), F32)] * 3
        + [jax.ShapeDtypeStruct((S, D_MODEL), MXU_DTYPE)] * 2 + [jax.ShapeDtypeStruct((8, D_MODEL), F32)],
        compiler_params=_cparams(VMEM_BIG),
    )(dx1, ya, ys, z, an, sn, gate1, w_o)


def _loss_rows(x2, fn, tgt):
    y = _rms(x2, fn, D_MODEL)
    per_row = jnp.sum(jnp.square(y - tgt), axis=1, keepdims=True)
    return jnp.sum(per_row, axis=0, keepdims=True) * (0.5 / D_MODEL)


def _mlp_loss(x1, tgt, norm2, scale2, shift2, gate2, fnorm, w_gu, w_d, tm):
    S = x1.shape[0]
    n_gu = 2 * D_FF // N_DEV
    half = n_gu // 2

    def body(x1_ref, t_ref, n_ref, sc_ref, sh_ref, g_ref, fn_ref, wgu_a, wgu_b, wd_hbm,
             dx1_ref, h_ref, dgu_ref, act_ref, dmlp_ref, acc_ref, wgu, wd, wsem):
        @pl.when(pl.program_id(0) == 0)
        def _():
            acc_ref[...] = jnp.zeros_like(acc_ref)
            copies = [pltpu.make_async_copy(wd_hbm, wd, wsem.at[2 * N_DEV])]
            for j in range(N_DEV):
                copies.append(pltpu.make_async_copy(wgu_a.at[j], wgu.at[pl.ds(j * n_gu, half)], wsem.at[2 * j]))
                copies.append(pltpu.make_async_copy(wgu_b.at[j], wgu.at[pl.ds(j * n_gu + half, half)],
                                                    wsem.at[2 * j + 1]))
            for cp in copies:
                cp.start()
            for cp in copies:
                cp.wait()

        x1 = x1_ref[...]
        gate2 = g_ref[...]
        h, vjp_h = jax.vjp(_modnorm, x1, n_ref[...], sc_ref[...], sh_ref[...])
        hb = h.astype(MXU_DTYPE)
        gu = _mm_nt(hb, wgu[...])
        g, u = gu[:, :D_FF], gu[:, D_FF:]
        sg = jax.nn.sigmoid(g)
        silu_g = g * sg
        act = (silu_g * u).astype(MXU_DTYPE)
        mlp = _mm(act, wd[...])
        x2 = x1 + gate2 * mlp
        loss, vjp_loss = jax.vjp(_loss_rows, x2, fn_ref[...], t_ref[...])
        dx2, dfn, _ = vjp_loss(jnp.ones((1, 1), F32))
        dmlp = (dx2 * gate2).astype(MXU_DTYPE)
        dact = _mm_nt(dmlp, wd[...])
        dg = dact * u * (sg * (1.0 + g * (1.0 - sg)))
        du = dact * silu_g
        dgu = jnp.concatenate([dg, du], axis=1).astype(MXU_DTYPE)
        dh = _mm(dgu, wgu[...])
        dx, dn, dsc, dsh = vjp_h(dh)
        dx1_ref[...] = dx2 + dx
        h_ref[...] = hb
        dgu_ref[...] = dgu
        act_ref[...] = act
        dmlp_ref[...] = dmlp
        acc_ref[0:1, :] += dn
        acc_ref[1:2, :] += dsc
        acc_ref[2:3, :] += dsh
        acc_ref[3:4, :] += jnp.sum(dx2 * mlp, axis=0, keepdims=True)
        acc_ref[4:5, :] += dfn
        acc_ref[5:6, :] += jnp.broadcast_to(loss, (1, D_MODEL))

    full = pl.BlockSpec((tm, D_MODEL), _row)
    vec = pl.BlockSpec((1, D_MODEL), _fixed)
    anyspec = pl.BlockSpec(memory_space=pl.ANY)
    return pl.pallas_call(
        body, name="mlp_loss", grid=(S // tm,),
        in_specs=[full, full, vec, vec, vec, vec, vec, anyspec, anyspec, anyspec],
        out_specs=[full, full, pl.BlockSpec((tm, 2 * D_FF), _row), pl.BlockSpec((tm, D_FF), _row), full,
                   pl.BlockSpec((8, D_MODEL), _fixed)],
        out_shape=[jax.ShapeDtypeStruct((S, D_MODEL), F32), jax.ShapeDtypeStruct((S, D_MODEL), MXU_DTYPE),
                   jax.ShapeDtypeStruct((S, 2 * D_FF), MXU_DTYPE), jax.ShapeDtypeStruct((S, D_FF), MXU_DTYPE),
                   jax.ShapeDtypeStruct((S, D_MODEL), MXU_DTYPE), jax.ShapeDtypeStruct((8, D_MODEL), F32)],
        scratch_shapes=[pltpu.VMEM((2 * D_FF, D_MODEL), MXU_DTYPE), pltpu.VMEM((D_FF, D_MODEL), MXU_DTYPE),
                        pltpu.SemaphoreType.DMA((2 * N_DEV + 1,))],
        compiler_params=_cparams(VMEM_BIG),
    )(x1, tgt, norm2, scale2, shift2, gate2, fnorm, w_gu[0], w_gu[1], w_d)


def _wgrad(a, g, tk, ts, name):
    S, K = a.shape
    N = g.shape[1]
    ns = S // ts

    def body(a_ref, g_ref, o_ref, acc_ref):
        s = pl.program_id(1)

        @pl.when(s == 0)
        def _():
            acc_ref[...] = jnp.zeros_like(acc_ref)

        acc_ref[...] += _mm_tn(a_ref[...], g_ref[...])

        @pl.when(s == ns - 1)
        def _():
            o_ref[...] = acc_ref[...].astype(o_ref.dtype)

    return pl.pallas_call(
        body, name=name, grid=(K // tk, ns),
        in_specs=[pl.BlockSpec((ts, tk), lambda j, s: (s, j)), pl.BlockSpec((ts, N), lambda j, s: (s, 0))],
        out_specs=pl.BlockSpec((tk, N), lambda j, s: (j, 0)),
        out_shape=jax.ShapeDtypeStruct((K, N), WIRE_DTYPE),
        scratch_shapes=[pltpu.VMEM((tk, N), F32)],
        compiler_params=_cparams(VMEM_BIG),
    )(a, g)


MASKED = -1e30
QK_SCALE = HALF ** -0.5


def _attn_bias(buckets, rel_bias):
    def body(bk_ref, relb_ref, out_ref):
        bk = bk_ref[...]
        i = lax.broadcasted_iota(jnp.int32, (BLK, 2 * BLK), 0)
        j = lax.broadcasted_iota(jnp.int32, (BLK, 2 * BLK), 1)
        window = (j > i) & (j <= i + BLK)
        for h in range(N_HEADS):
            acc = jnp.zeros((BLK, 2 * BLK), F32)
            for b in range(N_BUCKETS):
                acc = jnp.where(bk == b, relb_ref[b, h], acc)
            out_ref[0, h] = jnp.where(window, acc, MASKED)
            out_ref[1, h] = jnp.where(window & (j >= BLK), acc, MASKED)

    return pl.pallas_call(
        body, name="attn_bias", out_shape=jax.ShapeDtypeStruct((2, N_HEADS, BLK, 2 * BLK), F32),
        in_specs=[pl.BlockSpec(memory_space=pltpu.VMEM), pl.BlockSpec(memory_space=pltpu.SMEM)],
    )(buckets, rel_bias)


def _attn_kv(kvp_ref, kvc_ref):
    kvp = kvp_ref[...].astype(F32)
    kvc = kvc_ref[...].astype(F32)
    kp, kc = _split_pair(kvp[:, :LANE]), _split_pair(kvc[:, :LANE])
    vp, vc = _split_pair(kvp[:, LANE:]), _split_pair(kvc[:, LANE:])
    k_pads = [jnp.concatenate([kp[g], kc[g]], axis=0).astype(MXU_DTYPE) for g in range(2)]
    v_pads = [jnp.concatenate([vp[g], vc[g]], axis=0).astype(MXU_DTYPE) for g in range(2)]
    return k_pads, v_pads


def _attn_fwd(qkv, bias, sinks, xchg):
    S = qkv.shape[0]
    nb = S // BLK

    def body(q_ref, kvp_ref, kvc_ref, bias_ref, sinks_ref, y_ref):
        first = jnp.where(pl.program_id(0) == 0, 1, 0)
        q_heads = _split_heads(q_ref[...].astype(F32) * QK_SCALE, 4)
        k_pads, v_pads = _attn_kv(kvp_ref, kvc_ref)
        heads = range(N_HEADS)
        s = [_mm_nt(q_heads[h].astype(MXU_DTYPE), k_pads[h // 4]) + bias_ref[first, h] for h in heads]
        m = [jnp.maximum(jnp.max(s[h], axis=-1, keepdims=True), sinks_ref[h]) for h in heads]
        p = [jnp.exp(s[h] - m[h]) for h in heads]
        rinv = [1.0 / (jnp.sum(p[h], axis=-1, keepdims=True) + jnp.exp(sinks_ref[h] - m[h])) for h in heads]
        y_ref[...] = _join_heads([_mm(p[h], v_pads[h // 4]) * rinv[h] for h in heads])

    smem = pl.BlockSpec(memory_space=pltpu.SMEM)
    return _hosted_call(
        body, "attn_fwd", nb,
        in_specs=[pl.BlockSpec((BLK, ATTN_W), _row),
                  pl.BlockSpec((BLK, 2 * KV_W), lambda i: (jnp.maximum(i - 1, 0), 2)),
                  pl.BlockSpec((BLK, 2 * KV_W), lambda i: (i, 2)),
                  pl.BlockSpec((2, N_HEADS, BLK, 2 * BLK), lambda i: (0, 0, 0, 0)), smem],
        out_specs=[pl.BlockSpec((BLK, ATTN_W), _row)],
        out_shape=[jax.ShapeDtypeStruct((S, ATTN_W), F32)],
        scratch_shapes=[],
        args=(qkv, qkv, qkv, bias, sinks), xchg=xchg, cparams=_cparams(),
    )


def _attn_bwd(qkv, y, dy, bias, sinks, xchg):
    S = qkv.shape[0]
    nb = S // BLK

    def body(q_ref, kvp_ref, kvc_ref, y_ref, dy_ref, bias_ref, sinks_ref, dq_ref, dkv_ref, dbias_ref, dsk_ref, carry_ref):
        i = pl.program_id(0)

        @pl.when(i == 0)
        def _():
            dbias_ref[...] = jnp.zeros_like(dbias_ref)
            dsk_ref[...] = jnp.zeros_like(dsk_ref)
            carry_ref[...] = jnp.zeros_like(carry_ref)

        first = jnp.where(i == nb - 1, 1, 0)
        q_heads = _split_heads(q_ref[...].astype(F32) * QK_SCALE, 4)
        k_pads, v_pads = _attn_kv(kvp_ref, kvc_ref)
        y_heads = _split_heads(y_ref[...], 4)
        dy_heads = _split_heads(dy_ref[...], 4)
        heads = range(N_HEADS)
        qs = [q_heads[h].astype(MXU_DTYPE) for h in heads]
        s = [_mm_nt(qs[h], k_pads[h // 4]) + bias_ref[first, h] for h in heads]
        m = [jnp.maximum(jnp.max(s[h], axis=-1, keepdims=True), sinks_ref[h]) for h in heads]
        p = [jnp.exp(s[h] - m[h]) for h in heads]
        esink = [jnp.exp(sinks_ref[h] - m[h]) for h in heads]
        rinv = [1.0 / (jnp.sum(p[h], axis=-1, keepdims=True) + esink[h]) for h in heads]
        t = [dy_heads[h] * rinv[h] for h in heads]
        delta = [jnp.sum(t[h] * y_heads[h], axis=-1, keepdims=True) for h in heads]
        tb = [t[h].astype(MXU_DTYPE) for h in heads]
        dp = [_mm_nt(tb[h], v_pads[h // 4]) for h in heads]
        ds = [p[h] * (dp[h] - delta[h]) for h in heads]
        for h in heads:
            dbias_ref[h] += ds[h]
            dsk_ref[h] -= esink[h] * delta[h]
        dsb = [ds[h].astype(MXU_DTYPE) for h in heads]
        pb = [p[h].astype(MXU_DTYPE) for h in heads]
        dq_heads = [_mm(dsb[h], k_pads[h // 4]) * QK_SCALE for h in heads]
        dk_pads = [_mm_tn(jnp.concatenate(dsb[4 * g:4 * g + 4], axis=0), jnp.concatenate(qs[4 * g:4 * g + 4], axis=0))
                   for g in range(2)]
        dv_pads = [_mm_tn(jnp.concatenate(pb[4 * g:4 * g + 4], axis=0), jnp.concatenate(tb[4 * g:4 * g + 4], axis=0))
                   for g in range(2)]
        dq_ref[...] = _join_heads(dq_heads)
        dk_prev = _join_pair(dk_pads[0][:BLK], dk_pads[1][:BLK])
        dk_cur = _join_pair(dk_pads[0][BLK:], dk_pads[1][BLK:])
        dv_prev = _join_pair(dv_pads[0][:BLK], dv_pads[1][:BLK])
        dv_cur = _join_pair(dv_pads[0][BLK:], dv_pads[1][BLK:])
        dkv_ref[...] = jnp.concatenate([dk_cur, dv_cur], axis=1) + carry_ref[...]
        carry_ref[...] = jnp.concatenate([dk_prev, dv_prev], axis=1)

    smem = pl.BlockSpec(memory_space=pltpu.SMEM)
    rev = lambda i: (nb - 1 - i, 0)
    return _hosted_call(
        body, "attn_bwd", nb,
        in_specs=[pl.BlockSpec((BLK, ATTN_W), rev),
                  pl.BlockSpec((BLK, 2 * KV_W), lambda i: (jnp.maximum(nb - 2 - i, 0), 2)),
                  pl.BlockSpec((BLK, 2 * KV_W), lambda i: (nb - 1 - i, 2)),
                  pl.BlockSpec((BLK, ATTN_W), rev), pl.BlockSpec((BLK, ATTN_W), rev),
                  pl.BlockSpec((2, N_HEADS, BLK, 2 * BLK), lambda i: (0, 0, 0, 0)), smem],
        out_specs=[pl.BlockSpec((BLK, ATTN_W), rev), pl.BlockSpec((BLK, 2 * KV_W), rev),
                   pl.BlockSpec((N_HEADS, BLK, 2 * BLK), lambda i: (0, 0, 0)),
                   pl.BlockSpec((N_HEADS, BLK, 1), lambda i: (0, 0, 0))],
        out_shape=[jax.ShapeDtypeStruct((S, ATTN_W), F32), jax.ShapeDtypeStruct((S, 2 * KV_W), F32),
                   jax.ShapeDtypeStruct((N_HEADS, BLK, 2 * BLK), F32), jax.ShapeDtypeStruct((N_HEADS, BLK, 1), F32)],
        scratch_shapes=[pltpu.VMEM((BLK, 2 * KV_W), F32)],
        args=(qkv, qkv, qkv, y, dy, bias, sinks), xchg=xchg, cparams=_cparams(),
    )


def _attn_finish(dbias, dsk, buckets):
    def body(db_ref, dsk_ref, bk_ref, drel_ref, dsink_ref):
        bk = bk_ref[...]
        r = lax.broadcasted_iota(jnp.int32, (N_BUCKETS, LANE), 0)
        l = lax.broadcasted_iota(jnp.int32, (N_BUCKETS, LANE), 1)
        row = lax.broadcasted_iota(jnp.int32, (N_HEADS, LANE), 0)
        res = jnp.zeros((N_BUCKETS, LANE), F32)
        dsink = jnp.zeros((N_HEADS, LANE), F32)
        for h in range(N_HEADS):
            db = db_ref[h]
            for b in range(N_BUCKETS):
                v = jnp.sum(jnp.sum(jnp.where(bk == b, db, 0.0), axis=1, keepdims=True), axis=0, keepdims=True)
                res = res + jnp.where((r == b) & (l == h), v, 0.0)
            dsink = dsink + jnp.where(row == h, jnp.sum(dsk_ref[h], axis=0, keepdims=True), 0.0)
        drel_ref[...] = res
        dsink_ref[...] = dsink

    return pl.pallas_call(body, name="attn_finish",
                          out_shape=[jax.ShapeDtypeStruct((N_BUCKETS, LANE), F32),
                                     jax.ShapeDtypeStruct((N_HEADS, LANE), F32)])(dbias, dsk, buckets)


def _ssd_consts():
    r = lax.broadcasted_iota(jnp.int32, (BLK, BLK), 0)
    c = lax.broadcasted_iota(jnp.int32, (BLK, BLK), 1)
    causal = c <= r
    upper = (r <= c).astype(F32)
    last = r == BLK - 1
    head = lax.broadcasted_iota(jnp.int32, (N_HEADS, BLK), 0)
    return causal, upper, last, head


def _ssd_chunk(xs, bg, cg, dt_raw_t, prev, dtb, alog, d_rows, consts):
    causal, upper, last, head = consts
    dt_t = _softplus(dt_raw_t + dtb)
    acs_t = _mm_hi(dt_t * (-jnp.exp(alog)), upper)
    cb = [_mm_nt(cg[g], bg[g]) for g in range(2)]
    heads = range(N_HEADS)
    dt_row = [jnp.sum(jnp.where(head == h, dt_t, 0.0), axis=0, keepdims=True) for h in heads]
    a_row = [jnp.sum(jnp.where(head == h, acs_t, 0.0), axis=0, keepdims=True) for h in heads]
    a_rb = [jnp.broadcast_to(a_row[h], (BLK, BLK)) for h in heads]
    a_b = [a_rb[h].T for h in heads]
    a_last = [jnp.sum(jnp.where(last, a_b[h], 0.0), axis=0, keepdims=True) for h in heads]
    w = [cb[h // 4] * jnp.exp(jnp.where(causal, a_b[h] - a_rb[h], -1e30)) * dt_row[h] for h in heads]
    f_b = [jnp.broadcast_to(dt_row[h] * jnp.exp(a_last[h] - a_row[h]), (BLK, BLK)).T for h in heads]
    y_in = [_mm(w[h], xs[h]) for h in heads]
    y_off = [_mm(cg[h // 4], prev[h]) * jnp.exp(a_b[h]) for h in heads]
    st = [_mm_tn(bg[h // 4], xs[h] * f_b[h]) for h in heads]
    ys = [y_in[h] + y_off[h] + d_rows[h] * xs[h] for h in heads]
    hs = [prev[h] * jnp.exp(a_last[h]) + st[h] for h in heads]
    return tuple(ys), tuple(hs)


def _dt_rows(dt_blk):
    return dt_blk.T[:N_HEADS]


def _silu_grad(x):
    s = jax.nn.sigmoid(x)
    return s * (1.0 + x * (1.0 - s))


def _conv_pre(ext_ref, halo, blk, cw_ref, cb_ref):
    ext_ref[0:8, :] = halo
    ext_ref[8:8 + BLK, :] = blk
    pre = cb_ref[...] + cw_ref[0:1, :] * ext_ref[pl.ds(5, BLK), :]
    for k in range(1, 4):
        pre = pre + cw_ref[k:k + 1, :] * ext_ref[pl.ds(5 + k, BLK), :]
    return pre


def _ssd_split(pre):
    heads = _split_heads(pre[:, :SSM_W], 4)
    pb = [pre[:, SSM_W + g * D_STATE:SSM_W + (g + 1) * D_STATE] for g in range(2)]
    pc = [pre[:, SSM_W + 2 * D_STATE + g * D_STATE:SSM_W + 2 * D_STATE + (g + 1) * D_STATE] for g in range(2)]
    return heads, pb, pc


def _ssd_fwd(xbc, dt_raw, conv_w, conv_b, dtb_row, alog_row, d_exp, xchg):
    S = xbc.shape[0]
    nc = S // BLK

    def body(xbc_ref, halo_ref, dt_ref, cw_ref, cb_ref, dtb_ref, alog_ref, d_ref, y_ref, prev_ref, ext_ref, state_ref):
        i = pl.program_id(0)

        @pl.when(i == 0)
        def _():
            state_ref[...] = jnp.zeros_like(state_ref)

        halo = halo_ref[...] * jnp.where(i > 0, 1.0, 0.0)
        pre = _conv_pre(ext_ref, halo, xbc_ref[...], cw_ref, cb_ref)
        heads, pb, pc = _ssd_split(_silu(pre))
        prev = [state_ref[h] for h in range(N_HEADS)]
        for h in range(N_HEADS):
            prev_ref[0, h] = prev[h]
        d_rows = [d_ref[h:h + 1, :] for h in range(N_HEADS)]
        ys, hs = _ssd_chunk(heads, pb, pc, _dt_rows(dt_ref[...]), prev, dtb_ref[...], alog_ref[...], d_rows,
                            _ssd_consts())
        for h in range(N_HEADS):
            state_ref[h] = hs[h]
        y_ref[...] = _join_heads(ys)

    vec = pl.BlockSpec((N_HEADS, LANE), _fixed)
    return _hosted_call(
        body, "ssd_fwd", nc,
        in_specs=[pl.BlockSpec((BLK, XBC_W), _row),
                  pl.BlockSpec((8, XBC_W), lambda i: (jnp.maximum(i * (BLK // 8) - 1, 0), 0)),
                  pl.BlockSpec((BLK, LANE), _row),
                  pl.BlockSpec((4, XBC_W), _fixed), pl.BlockSpec((1, XBC_W), _fixed), vec, vec,
                  pl.BlockSpec((N_HEADS, LANE), _fixed)],
        out_specs=[pl.BlockSpec((BLK, SSM_W), _row),
                   pl.BlockSpec((1, N_HEADS, D_STATE, LANE), lambda i: (i, 0, 0, 0))],
        out_shape=[jax.ShapeDtypeStruct((S, SSM_W), F32), jax.ShapeDtypeStruct((nc, N_HEADS, D_STATE, LANE), F32)],
        scratch_shapes=[pltpu.VMEM((8 + BLK, XBC_W), F32), pltpu.VMEM((N_HEADS, D_STATE, LANE), F32)],
        args=(xbc, xbc, dt_raw, conv_w, conv_b, dtb_row, alog_row, d_exp), xchg=xchg, cparams=_cparams(),
    )


def _ssd_bwd(xbc, dt_raw, prev_states, dy, conv_w, conv_b, dtb_row, alog_row, d_exp, xchg):
    S = xbc.shape[0]
    nc = S // BLK

    def body(xbc_ref, halo_ref, dt_ref, prev_ref, dy_ref, cw_ref, cb_ref, dtb_ref, alog_ref, d_ref,
             dxbc_ref, ddt_ref, dcw_ref, dvec_ref, dd_ref, ext_ref, dpe_ref, gstate_ref, ghalo_ref):
        i = pl.program_id(0)
        c = nc - 1 - i

        @pl.when(i == 0)
        def _():
            gstate_ref[...] = jnp.zeros_like(gstate_ref)
            ghalo_ref[...] = jnp.zeros_like(ghalo_ref)
            dcw_ref[...] = jnp.zeros_like(dcw_ref)
            dvec_ref[...] = jnp.zeros_like(dvec_ref)
            dd_ref[...] = jnp.zeros_like(dd_ref)
            dpe_ref[...] = jnp.zeros_like(dpe_ref)

        halo = halo_ref[...] * jnp.where(c > 0, 1.0, 0.0)
        pre = _conv_pre(ext_ref, halo, xbc_ref[...], cw_ref, cb_ref)
        heads, pb, pc = _ssd_split(_silu(pre))
        prev = [prev_ref[0, h] for h in range(N_HEADS)]
        d_rows = [d_ref[h:h + 1, :] for h in range(N_HEADS)]
        _, vjp = jax.vjp(functools.partial(_ssd_chunk, consts=_ssd_consts()),
                         heads, pb, pc, _dt_rows(dt_ref[...]), prev, dtb_ref[...], alog_ref[...], d_rows)
        dys = tuple(_split_heads(dy_ref[...], 4))
        dhs = tuple(gstate_ref[h] for h in range(N_HEADS))
        dheads, dpb, dpc, ddt_t, dprev, ddtb, dalog, dd_rows = vjp((dys, dhs))
        for h in range(N_HEADS):
            gstate_ref[h] = dprev[h]
            dd_ref[h:h + 1, :] += dd_rows[h]
        ddt_ref[...] = jnp.concatenate([ddt_t, jnp.zeros((BLK - N_HEADS, BLK), F32)], axis=0).T
        dvec_ref[0:N_HEADS, :] += ddtb
        dvec_ref[N_HEADS:, :] += dalog
        dpre = jnp.concatenate([_join_heads(dheads)] + list(dpb) + list(dpc), axis=1) * _silu_grad(pre)
        dpe_ref[8:8 + BLK, :] = dpre
        dext = cw_ref[0:1, :] * dpe_ref[pl.ds(3, 8 + BLK), :]
        dcw_ref[0:1, :] += jnp.sum(dpre * ext_ref[pl.ds(5, BLK), :], axis=0, keepdims=True)
        for k in range(1, 4):
            dext = dext + cw_ref[k:k + 1, :] * dpe_ref[pl.ds(3 - k, 8 + BLK), :]
            dcw_ref[k:k + 1, :] += jnp.sum(dpre * ext_ref[pl.ds(5 + k, BLK), :], axis=0, keepdims=True)
        dcw_ref[4:5, :] += jnp.sum(dpre, axis=0, keepdims=True)
        dxbc_ref[...] = dext[8:, :]
        dxbc_ref[BLK - 8:BLK, :] += ghalo_ref[...]
        ghalo_ref[...] = dext[:8, :]

    vec = pl.BlockSpec((N_HEADS, LANE), _fixed)
    rev = lambda i: (nc - 1 - i, 0)
    return _hosted_call(
        body, "ssd_bwd", nc,
        in_specs=[pl.BlockSpec((BLK, XBC_W), rev),
                  pl.BlockSpec((8, XBC_W), lambda i: (jnp.maximum((nc - 1 - i) * (BLK // 8) - 1, 0), 0)),
                  pl.BlockSpec((BLK, LANE), rev),
                  pl.BlockSpec((1, N_HEADS, D_STATE, LANE), lambda i: (nc - 1 - i, 0, 0, 0)),
                  pl.BlockSpec((BLK, SSM_W), rev),
                  pl.BlockSpec((4, XBC_W), _fixed), pl.BlockSpec((1, XBC_W), _fixed), vec, vec,
                  pl.BlockSpec((N_HEADS, LANE), _fixed)],
        out_specs=[pl.BlockSpec((BLK, XBC_W), rev), pl.BlockSpec((BLK, LANE), rev),
                   pl.BlockSpec((8, XBC_W), _fixed), pl.BlockSpec((2 * N_HEADS, LANE), _fixed),
                   pl.BlockSpec((N_HEADS, LANE), _fixed)],
        out_shape=[jax.ShapeDtypeStruct((S, XBC_W), F32), jax.ShapeDtypeStruct((S, LANE), F32),
                   jax.ShapeDtypeStruct((8, XBC_W), F32), jax.ShapeDtypeStruct((2 * N_HEADS, LANE), F32),
                   jax.ShapeDtypeStruct((N_HEADS, LANE), F32)],
        scratch_shapes=[pltpu.VMEM((8 + BLK, XBC_W), F32), pltpu.VMEM((16 + BLK, XBC_W), F32),
                        pltpu.VMEM((N_HEADS, D_STATE, LANE), F32), pltpu.VMEM((8, XBC_W), F32)],
        args=(xbc, xbc, dt_raw, prev_states, dy, conv_w, conv_b, dtb_row, alog_row, d_exp), xchg=xchg,
        cparams=_cparams(VMEM_BIG),
    )


def _adamw_math(w, g, m, v):
    m = ADAM_B1 * m + (1.0 - ADAM_B1) * g
    v = ADAM_B2 * v + (1.0 - ADAM_B2) * jnp.square(g)
    m_hat = m / (1.0 - ADAM_B1 ** ADAM_STEP)
    v_hat = v / (1.0 - ADAM_B2 ** ADAM_STEP)
    delta = -ADAM_LR * (m_hat / (jnp.sqrt(v_hat) + ADAM_EPS) + ADAM_WD * w)
    return delta, m, v


def _reduce_adamw(parts, w, m, v, name):
    R, C = w.shape

    def body(p_ref, w_ref, m_ref, v_ref, g_ref, d_ref, nm_ref, nv_ref):
        g = p_ref[0].astype(F32)
        for i in range(1, N_DEV):
            g = g + p_ref[i].astype(F32)
        d, nm, nv = _adamw_math(w_ref[...], g, m_ref[...], v_ref[...])
        g_ref[...] = g
        d_ref[...] = d
        nm_ref[...] = nm
        nv_ref[...] = nv

    if R % 16 == 0:
        tr = max(t for t in range(16, 257, 16) if R % t == 0)
        n, blk, pblk = R // tr, pl.BlockSpec((tr, C), _row), pl.BlockSpec((N_DEV, tr, C), lambda i: (0, i, 0))
    else:
        tl = 256
        n, blk, pblk = C // tl, pl.BlockSpec((R, tl), lambda i: (0, i)), pl.BlockSpec((N_DEV, R, tl),
                                                                                      lambda i: (0, 0, i))
    return pl.pallas_call(
        body, name=name, grid=(n,), in_specs=[pblk, blk, blk, blk],
        out_specs=[blk] * 4, out_shape=[jax.ShapeDtypeStruct((R, C), F32)] * 4,
    )(parts, w, m, v)


_SMALL_NAMES = ("ada_b", "norm1", "conv_w", "conv_b", "dt_bias", "A_log", "D_skip", "sinks", "attn_out_norm",
                "ssm_out_norm", "norm2", "rel_bias", "final_norm")
N_MOD = 6 * D_MODEL


def _mod_row(a0, a1, a2):
    return jnp.concatenate([a0[2:3], a0[1:2], a1[0:1], a2[2:3], a2[1:2], a2[3:4]], axis=1)


def _small_update(gathered, params):
    n_g = len(gathered)
    flat = [a for name in _SMALL_NAMES for a in params[name]]

    def body(*refs):
        a0_ref, a1_ref, a2_ref, cw_ref, dv_ref, dd_ref, ds_ref, dr_ref, c_ref = refs[:n_g]
        wmv = refs[n_g:n_g + len(flat)]
        outs = refs[n_g + len(flat):]

        def total(ref):
            t = ref[0]
            for i in range(1, N_DEV):
                t = t + ref[i]
            return t

        t0, t1, t2, tcw, tdv, tdd, tds, tdr = [total(r) for r in (a0_ref, a1_ref, a2_ref, cw_ref, dv_ref, dd_ref,
                                                                   ds_ref, dr_ref)]
        r8 = lax.broadcasted_iota(jnp.int32, (N_HEADS, LANE), 0)
        l8 = lax.broadcasted_iota(jnp.int32, (N_HEADS, LANE), 1)

        def diag_row(t):
            return jnp.sum(jnp.where(r8 == l8, t, 0.0), axis=0, keepdims=True)[:, :N_HEADS]

        def lane_sums(t):
            return diag_row(jnp.broadcast_to(jnp.sum(t, axis=1, keepdims=True), (N_HEADS, LANE)))

        me = _lin(_my_pos())
        n_cw = XBC_W // N_DEV
        cw_mine = jnp.zeros((4, n_cw), F32)
        for j in range(N_DEV):
            cw_mine = cw_mine + tcw[0:4, j * n_cw:(j + 1) * n_cw] * jnp.where(me == j, 1.0, 0.0)
        grads = {
            "ada_b": _mod_row(t0, t1, t2), "norm1": t0[0:1], "conv_w": cw_mine, "conv_b": tcw[4:5],
            "dt_bias": lane_sums(tdv[:N_HEADS]), "A_log": lane_sums(tdv[N_HEADS:]), "D_skip": lane_sums(tdd),
            "sinks": diag_row(tds), "attn_out_norm": t1[1:2, :ATTN_W], "ssm_out_norm": t1[1:2, ATTN_W:],
            "norm2": t2[0:1], "rel_bias": tdr[:, :N_HEADS], "final_norm": t2[4:5],
        }
        for k, name in enumerate(_SMALL_NAMES):
            w_ref, m_ref, v_ref = wmv[3 * k:3 * k + 3]
            g = grads[name]
            d, nm, nv = _adamw_math(w_ref[...], g, m_ref[...], v_ref[...])
            for o, val in zip(outs[4 * k:4 * k + 4], (g, d, nm, nv)):
                o[...] = val
        loss_ref, call_ref, dmod_ref = outs[4 * len(_SMALL_NAMES):]
        loss_ref[...] = t2[5:6, 0:1]
        call_ref[...] = jnp.concatenate([c_ref[i] for i in range(N_DEV)], axis=0)
        dmod_ref[...] = jnp.concatenate([_mod_row(a0_ref[i], a1_ref[i], a2_ref[i]) for i in range(N_DEV)], axis=0)

    out_shape = [jax.ShapeDtypeStruct(params[name][0].shape, F32) for name in _SMALL_NAMES for _ in range(4)]
    out_shape += [jax.ShapeDtypeStruct((1, 1), F32), jax.ShapeDtypeStruct((N_DEV, D_MODEL), F32),
                  jax.ShapeDtypeStruct((N_DEV, N_MOD), F32)]
    res = pl.pallas_call(body, name="small_update", out_shape=out_shape)(*gathered, *flat)
    upd = {name: res[4 * k:4 * k + 4] for k, name in enumerate(_SMALL_NAMES)}
    loss, c_all, dmod_all = res[4 * len(_SMALL_NAMES):]
    return upd, loss, c_all, dmod_all


def _ada_w_update(c_all, dmod_all, w, m, v):
    chunk = w.shape[1]

    def body(c_ref, dm_ref, w_ref, m_ref, v_ref, g_ref, d_ref, nm_ref, nv_ref):
        me = _lin(_my_pos())
        dm = jnp.zeros((N_DEV, chunk), F32)
        for j in range(N_DEV):
            dm = dm + dm_ref[:, j * chunk:(j + 1) * chunk] * jnp.where(me == j, 1.0, 0.0)
        g = lax.dot_general(_silu(c_ref[...]), dm, (((0,), (0,)), ((), ())), precision=HI,
                            preferred_element_type=F32)
        d, nm, nv = _adamw_math(w_ref[...], g, m_ref[...], v_ref[...])
        g_ref[...] = g
        d_ref[...] = d
        nm_ref[...] = nm
        nv_ref[...] = nv

    return pl.pallas_call(body, name="ada_w_update", out_shape=[jax.ShapeDtypeStruct(w.shape, F32)] * 4,
                          compiler_params=_cparams(VMEM_BIG))(c_all, dmod_all, w, m, v)


def _local_step(x, tgt, mod, w_in, conv_w, w_o_mine, w_gu_mine, w_d_mine, p):
    S = x.shape[0]
    tm = min(512, S)
    tmm = min(256, S)
    shift1, scale1, gate1, shift2, scale2, gate2 = [mod[i:i + 1] for i in range(6)]
    buckets = jnp.asarray(_t5_bucket_table())
    per_head = lambda a: jnp.broadcast_to(a.reshape(N_HEADS, 1), (N_HEADS, LANE))
    dtb_row, alog_row, d_exp = per_head(p["dt_bias"]), per_head(p["A_log"]), per_head(p["D_skip"])
    sinks = p["sinks"].reshape(N_HEADS)

    half = w_gu_mine.shape[0] // 2
    (qkv, z, xbc, dt_raw), (g_d,) = _in_proj_fwd(x, p["norm1"], scale1, shift1, w_in, tm, ([w_d_mine], False))
    bias = _attn_bias(buckets, p["rel_bias"])
    (ya,), (g_gu_a,) = _attn_fwd(qkv, bias, sinks, ([w_gu_mine[:half]], False))
    (ys, prev_states), (g_gu_b, g_o) = _ssd_fwd(xbc, dt_raw, conv_w, p["conv_b"], dtb_row, alog_row, d_exp,
                                                ([w_gu_mine[half:], w_o_mine], False))
    w_o = g_o.reshape(D_MODEL, D_MODEL)
    w_d = g_d.reshape(D_FF, D_MODEL)
    x1 = _out_proj_fwd(x, ya, ys, z, p["attn_out_norm"], p["ssm_out_norm"], gate1, w_o, tm)
    dx1, h2, dgu, act, dmlp, acc2 = _mlp_loss(x1, tgt, p["norm2"], scale2, shift2, gate2, p["final_norm"],
                                              (g_gu_a, g_gu_b), w_d, tmm)
    g_w_gu = _wgrad(dgu, h2, 2 * D_FF // 4, tm, "wgrad_gate_up")
    g_w_d = _wgrad(act, dmlp, D_FF // 2, tm, "wgrad_down")
    dya, dys, dz, u, dmix, acc1 = _out_proj_bwd(dx1, ya, ys, z, p["attn_out_norm"], p["ssm_out_norm"], gate1, w_o, tm)
    g_w_o = _wgrad(u, dmix, D_MODEL, tm, "wgrad_out")
    (dq, dkv, dbias, dsk), (r_d,) = _attn_bwd(qkv, ya, dya, bias, sinks,
                                              ([g_w_d.reshape(N_DEV, D_FF // N_DEV, D_MODEL)], True))
    drel, dsink = _attn_finish(dbias, dsk, buckets)
    (dxbc, ddt, dcw, dvec, dd), (r_gu, r_o) = _ssd_bwd(
        xbc, dt_raw, prev_states, dys, conv_w, p["conv_b"], dtb_row, alog_row, d_exp,
        ([g_w_gu.reshape(N_DEV, 2 * D_FF // N_DEV, D_MODEL), g_w_o.reshape(N_DEV, D_MODEL // N_DEV, D_MODEL)], True))
    gx, h1, dproj, acc0 = _in_proj_bwd(x, dx1, dq, dkv, dz, dxbc, ddt, p["norm1"], scale1, shift1, w_in, tm)
    g_w_in = _wgrad(dproj, h1, IN_PAD, tm, "wgrad_in")
    return gx, g_w_in, (r_o, r_gu, r_d), (acc0, acc1, acc2, dcw, dvec, dd, dsink, drel)


def kernel(x, c, ada_w, ada_b, norm1, w_in, conv_w, conv_b, dt_bias, A_log, D_skip, sinks, attn_out_norm, ssm_out_norm, w_o, norm2, w_gate_up, w_down, rel_bias, final_norm, loss_target, m_ada_w, m_ada_b, m_norm1, m_w_in, m_conv_w, m_conv_b, m_dt_bias, m_A_log, m_D_skip, m_sinks, m_attn_out_norm, m_ssm_out_norm, m_w_o, m_norm2, m_w_gate_up, m_w_down, m_rel_bias, m_final_norm, v_ada_w, v_ada_b, v_norm1, v_w_in, v_conv_w, v_conv_b, v_dt_bias, v_A_log, v_D_skip, v_sinks, v_attn_out_norm, v_ssm_out_norm, v_w_o, v_norm2, v_w_gate_up, v_w_down, v_rel_bias, v_final_norm):
    two_d = lambda a: a if a.ndim == 2 else a.reshape(-1, a.shape[-1])
    small_params = dict(
        ada_b=(ada_b, m_ada_b, v_ada_b), norm1=(norm1, m_norm1, v_norm1), conv_w=(conv_w, m_conv_w, v_conv_w),
        conv_b=(conv_b, m_conv_b, v_conv_b), dt_bias=(dt_bias, m_dt_bias, v_dt_bias), A_log=(A_log, m_A_log, v_A_log),
        D_skip=(D_skip, m_D_skip, v_D_skip), sinks=(sinks, m_sinks, v_sinks),
        attn_out_norm=(attn_out_norm, m_attn_out_norm, v_attn_out_norm),
        ssm_out_norm=(ssm_out_norm, m_ssm_out_norm, v_ssm_out_norm), norm2=(norm2, m_norm2, v_norm2),
        rel_bias=(rel_bias, m_rel_bias, v_rel_bias), final_norm=(final_norm, m_final_norm, v_final_norm))
    small_params = {k: tuple(two_d(a) for a in v) for k, v in small_params.items()}
    S = x.shape[1]
    xs, tgt = x.reshape(S, D_MODEL), loss_target.reshape(S, D_MODEL)
    ada_w2 = ada_w[0]
    chunk = ada_w2.shape[1]
    t_in = [jnp.transpose(a[0]) for a in (w_in, m_w_in, v_w_in)]
    t_gu = [jnp.transpose(a[0]) for a in (w_gate_up, m_w_gate_up, v_w_gate_up)]

    mod = _mod_exchange(c, ada_w2, ada_b.reshape(N_DEV, chunk)).reshape(6, D_MODEL)

    g_in, g_cw = _exchange([t_in[0].astype(WIRE_DTYPE), conv_w[0]], scatter=False, name="gather_w_in")
    w_in_full = jnp.pad(g_in.reshape(IN_W, D_MODEL), ((0, IN_PAD - IN_W), (0, 0)))
    conv_w_full = jnp.transpose(g_cw, (1, 0, 2)).reshape(4, XBC_W)

    p = {k: v[0] for k, v in small_params.items()}
    gx, gw_in, (r_o, r_gu, r_d), blocks = _local_step(
        xs, tgt, mod, w_in_full, conv_w_full, w_o[0].astype(WIRE_DTYPE), t_gu[0].astype(WIRE_DTYPE),
        w_down[0].astype(WIRE_DTYPE), p)

    (r_in,) = _exchange([gw_in[:IN_W].reshape(N_DEV, IN_W // N_DEV, D_MODEL)], scatter=True, name="scatter_w_in")

    gathered = _exchange(list(blocks) + [c], scatter=False, name="gather_small")
    small, loss, c_all, dmod_all = _small_update(gathered, small_params)

    big = {
        "ada_w": _ada_w_update(c_all, dmod_all, ada_w2, m_ada_w[0], v_ada_w[0]),
        "w_in": [jnp.transpose(a) for a in _reduce_adamw(r_in, *t_in, "adamw_w_in")],
        "w_o": _reduce_adamw(r_o, w_o[0], m_w_o[0], v_w_o[0], "adamw_w_o"),
        "w_gate_up": [jnp.transpose(a) for a in _reduce_adamw(r_gu, *t_gu, "adamw_w_gate_up")],
        "w_down": _reduce_adamw(r_d, w_down[0], m_w_down[0], v_w_down[0], "adamw_w_down"),
    }
    big.update(small)

    order = ['ada_w', 'ada_b', 'norm1', 'w_in', 'conv_w', 'conv_b', 'dt_bias', 'A_log', 'D_skip', 'sinks',
             'attn_out_norm', 'ssm_out_norm', 'w_o', 'norm2', 'w_gate_up', 'w_down', 'rel_bias', 'final_norm']
    shapes = dict(ada_w=ada_w.shape, ada_b=ada_b.shape, norm1=norm1.shape, w_in=w_in.shape, conv_w=conv_w.shape,
                  conv_b=conv_b.shape, dt_bias=dt_bias.shape, A_log=A_log.shape, D_skip=D_skip.shape,
                  sinks=sinks.shape, attn_out_norm=attn_out_norm.shape, ssm_out_norm=ssm_out_norm.shape,
                  w_o=w_o.shape, norm2=norm2.shape, w_gate_up=w_gate_up.shape, w_down=w_down.shape,
                  rel_bias=rel_bias.shape, final_norm=final_norm.shape)
    outs = [[], [], [], []]
    for name in order:
        for kind in range(4):
            outs[kind].append(big[name][kind].reshape(shapes[name]))
    return (loss.reshape(()), gx.reshape(x.shape), *outs[0], *outs[1], *outs[2], *outs[3])
```

```python
import functools

import numpy as np
import jax
import jax.numpy as jnp
from jax import lax
from jax.experimental import pallas as pl
from jax.experimental.pallas import tpu as pltpu

F32 = jnp.float32
MXU_DTYPE = jnp.bfloat16
WIRE_DTYPE = jnp.bfloat16
HI = lax.Precision.HIGHEST
MESH = pl.DeviceIdType.MESH
N_DEV = 8

D_MODEL = 1024
ATTN_W = 512
KV_W = 128
SSM_W = 512
XBC_W = 1024
N_HEADS = 8
D_STATE = 128
D_FF = 2816
IN_W = 2312
IN_PAD = 2432
BLK = 128
N_BUCKETS = 32
EPS = 1e-6
LANE = 128
HALF = 64

ADAM_LR, ADAM_B1, ADAM_B2, ADAM_EPS, ADAM_WD, ADAM_STEP = 0.001, 0.9, 0.999, 1e-08, 0.01, 10

VMEM_BIG = 56 * 1024 * 1024


def _cparams(vmem=None):
    if vmem is None:
        return pltpu.CompilerParams()
    return pltpu.CompilerParams(vmem_limit_bytes=vmem)


def _mm(a, b):
    return jnp.dot(a.astype(MXU_DTYPE), b.astype(MXU_DTYPE), preferred_element_type=F32)


def _mm_nt(a, b):
    return lax.dot_general(a.astype(MXU_DTYPE), b.astype(MXU_DTYPE), (((1,), (1,)), ((), ())),
                           preferred_element_type=F32)


def _mm_tn(a, b):
    return lax.dot_general(a.astype(MXU_DTYPE), b.astype(MXU_DTYPE), (((0,), (0,)), ((), ())),
                           preferred_element_type=F32)


def _mm_hi(a, b):
    return jnp.dot(a, b, precision=HI, preferred_element_type=F32)


def _silu(x):
    return x * jax.nn.sigmoid(x)


def _softplus(x):
    return jnp.maximum(x, 0.0) + jnp.log1p(jnp.exp(-jnp.abs(x)))


def _rms(x, g, n):
    return x * lax.rsqrt(jnp.sum(x * x, axis=-1, keepdims=True) * (1.0 / n) + EPS) * g


def _modnorm(x, g, scale, shift):
    return _rms(x, g, x.shape[-1]) * (1.0 + scale) + shift


def _lane_iota(shape):
    return lax.broadcasted_iota(jnp.int32, shape, len(shape) - 1)


def _split_pair(t):
    lane = _lane_iota(t.shape)
    lo = jnp.where(lane < HALF, t, 0.0)
    hi = pltpu.roll(jnp.where(lane >= HALF, t, 0.0), HALF, 1)
    return lo, hi


def _join_pair(lo, hi):
    lane = _lane_iota(lo.shape)
    return jnp.where(lane < HALF, lo, pltpu.roll(hi, HALF, 1))


def _split_heads(t, n_pairs):
    out = []
    for p in range(n_pairs):
        out.extend(_split_pair(t[:, p * LANE:(p + 1) * LANE]))
    return out


def _join_heads(hs):
    return jnp.concatenate([_join_pair(hs[2 * p], hs[2 * p + 1]) for p in range(len(hs) // 2)], axis=1)


def _t5_bucket_table():
    dist = np.arange(BLK)[:, None] + BLK - np.arange(2 * BLK)[None, :]
    n = np.maximum(dist, 0)
    max_exact = N_BUCKETS // 2
    large = max_exact + (np.log(np.maximum(n, 1) / max_exact) / np.log(128 / max_exact)
                         * (N_BUCKETS - max_exact)).astype(np.int32)
    large = np.minimum(large, N_BUCKETS - 1)
    return np.where(n < max_exact, n, large).astype(np.int32)


def _my_pos():
    return lax.axis_index("x"), lax.axis_index("y"), lax.axis_index("c")


def _peer(k):
    x, y, c = _my_pos()
    return (1 - x if k & 4 else x, 1 - y if k & 2 else y, 1 - c if k & 1 else c)


def _lin(pos):
    return 4 * pos[0] + 2 * pos[1] + pos[2]


def _xchg_copies(ins, outs, sems, scatter):
    local_sem, send_sem, recv_sem = sems
    me = _lin(_my_pos())
    local, remote = [], []
    for a in range(len(ins)):
        src = ins[a].at[me] if scatter else ins[a]
        local.append(pltpu.make_async_copy(src, outs[a].at[me], local_sem.at[a]))
    for k in range(1, N_DEV):
        peer = _peer(k)
        for a in range(len(ins)):
            src = ins[a].at[_lin(peer)] if scatter else ins[a]
            remote.append(pltpu.make_async_remote_copy(src, outs[a].at[me], send_sem.at[a, k - 1],
                                                       recv_sem.at[a, k - 1], device_id=peer, device_id_type=MESH))
    return local, remote


def _xchg_start(ins, outs, sems, scatter):
    local, remote = _xchg_copies(ins, outs, sems, scatter)
    for cp in local + remote:
        cp.start()


def _xchg_wait(ins, outs, sems, scatter):
    local, remote = _xchg_copies(ins, outs, sems, scatter)
    for cp in local:
        cp.wait()
    for cp in remote:
        cp.wait_send()
        cp.wait_recv()


def _xchg_shapes(arrs, scatter):
    n = len(arrs)
    if scatter:
        out_shape = [jax.ShapeDtypeStruct(a.shape, a.dtype) for a in arrs]
    else:
        out_shape = [jax.ShapeDtypeStruct((N_DEV,) + a.shape, a.dtype) for a in arrs]
    sems = [pltpu.SemaphoreType.DMA((n,)), pltpu.SemaphoreType.DMA((n, N_DEV - 1)),
            pltpu.SemaphoreType.DMA((n, N_DEV - 1))]
    return out_shape, sems


def _exchange(arrs, scatter, name):
    n = len(arrs)
    out_shape, sems = _xchg_shapes(arrs, scatter)

    def body(*refs):
        ins, outs, s = refs[:n], refs[n:2 * n], refs[2 * n:]
        _xchg_start(ins, outs, s, scatter)
        _xchg_wait(ins, outs, s, scatter)

    hbm = pl.BlockSpec(memory_space=pltpu.HBM)
    return pl.pallas_call(body, name=name, out_shape=out_shape, in_specs=[hbm] * n, out_specs=[hbm] * n,
                          scratch_shapes=sems)(*arrs)


def _hosted_call(body, name, grid, in_specs, out_specs, out_shape, scratch_shapes, args, xchg, cparams):
    arrs, scatter = xchg
    grid = (grid,) if isinstance(grid, int) else tuple(grid)
    n, n_in, n_out, n_scr = len(arrs), len(in_specs), len(out_specs), len(scratch_shapes)
    x_shape, x_sems = _xchg_shapes(arrs, scatter)

    def hosted(*refs):
        ins, refs = refs[:n_in], refs[n_in:]
        x_in, refs = refs[:n], refs[n:]
        outs, refs = refs[:n_out], refs[n_out:]
        x_out, refs = refs[:n], refs[n:]
        scr, sems = refs[:n_scr], refs[n_scr:]
        step = pl.program_id(0)
        for d in range(1, len(grid)):
            step = step * grid[d] + pl.program_id(d)

        @pl.when(step == 0)
        def _():
            _xchg_start(x_in, x_out, sems, scatter)

        body(*ins, *outs, *scr)

        @pl.when(step == int(np.prod(grid)) - 1)
        def _():
            _xchg_wait(x_in, x_out, sems, scatter)

    hbm = pl.BlockSpec(memory_space=pltpu.HBM)
    res = pl.pallas_call(
        hosted, name=name, grid=grid, in_specs=list(in_specs) + [hbm] * n,
        out_specs=list(out_specs) + [hbm] * n, out_shape=list(out_shape) + x_shape,
        scratch_shapes=list(scratch_shapes) + x_sems, compiler_params=cparams,
    )(*args, *arrs)
    return res[:n_out], res[n_out:]


def _mod_exchange(c, ada_w, ada_b8):
    chunk = ada_w.shape[1]

    def body(c_ref, w_ref, b_ref, out_ref, cbuf, part, s1, r1, s2, r2):
        me = _lin(_my_pos())
        first = []
        for k in range(1, N_DEV):
            cp = pltpu.make_async_remote_copy(c_ref, cbuf.at[me], s1.at[k - 1], r1.at[k - 1],
                                              device_id=_peer(k), device_id_type=MESH)
            cp.start()
            first.append(cp)
        cbuf[me] = c_ref[...]
        for cp in first:
            cp.wait_send()
            cp.wait_recv()
        cond = _silu(jnp.concatenate([cbuf[i] for i in range(N_DEV)], axis=0))
        mod = _mm_hi(cond, w_ref[...]) + b_ref[pl.ds(me, 1), :]
        for j in range(N_DEV):
            part[j] = mod[j:j + 1, :]
        second = []
        for k in range(1, N_DEV):
            peer = _peer(k)
            cp = pltpu.make_async_remote_copy(part.at[_lin(peer)], out_ref.at[me], s2.at[k - 1], r2.at[k - 1],
                                              device_id=peer, device_id_type=MESH)
            cp.start()
            second.append(cp)
        out_ref[me] = part[me]
        for cp in second:
            cp.wait_send()
            cp.wait_recv()

    vm = pl.BlockSpec(memory_space=pltpu.VMEM)
    return pl.pallas_call(
        body, name="mod_exchange", out_shape=jax.ShapeDtypeStruct((N_DEV, 1, chunk), F32),
        in_specs=[vm, vm, vm], out_specs=vm,
        scratch_shapes=[pltpu.VMEM((N_DEV, 1, D_MODEL), F32), pltpu.VMEM((N_DEV, 1, chunk), F32)]
        + [pltpu.SemaphoreType.DMA((N_DEV - 1,))] * 4,
    )(c, ada_w, ada_b8)


def _row(i):
    return (i, 0)


def _fixed(i):
    return (0, 0)


def _in_proj_fwd(x, norm1, scale1, shift1, w_in, tm, xchg):
    S = x.shape[0]

    def body(x_ref, n_ref, sc_ref, sh_ref, w_ref, qkv_ref, z_ref, xbc_ref, dt_ref):
        h = _modnorm(x_ref[...], n_ref[...], sc_ref[...], sh_ref[...])
        p = _mm_nt(h, w_ref[...])
        qkv_ref[...] = p[:, :768].astype(qkv_ref.dtype)
        z_ref[...] = p[:, 768:1280]
        xbc_ref[...] = p[:, 1280:2304]
        dt_ref[...] = p[:, 2304:IN_PAD]

    vec = pl.BlockSpec((1, D_MODEL), _fixed)
    return _hosted_call(
        body, "in_proj_fwd", S // tm,
        in_specs=[pl.BlockSpec((tm, D_MODEL), _row), vec, vec, vec, pl.BlockSpec((IN_PAD, D_MODEL), _fixed)],
        out_specs=[pl.BlockSpec((tm, 768), _row), pl.BlockSpec((tm, SSM_W), _row),
                   pl.BlockSpec((tm, XBC_W), _row), pl.BlockSpec((tm, LANE), _row)],
        out_shape=[jax.ShapeDtypeStruct((S, 768), MXU_DTYPE), jax.ShapeDtypeStruct((S, SSM_W), F32),
                   jax.ShapeDtypeStruct((S, XBC_W), F32), jax.ShapeDtypeStruct((S, LANE), F32)],
        scratch_shapes=[], args=(x, norm1, scale1, shift1, w_in), xchg=xchg, cparams=_cparams(VMEM_BIG),
    )


def _in_proj_bwd(x, dx1, dq, dkv, dz, dxbc, ddt, norm1, scale1, shift1, w_in, tm):
    S = x.shape[0]

    def body(x_ref, dx1_ref, dq_ref, dkv_ref, dz_ref, dxbc_ref, ddt_ref, n_ref, sc_ref, sh_ref, w_ref,
             gx_ref, h_ref, dp_ref, acc_ref):
        @pl.when(pl.program_id(0) == 0)
        def _():
            acc_ref[...] = jnp.zeros_like(acc_ref)

        h, vjp = jax.vjp(_modnorm, x_ref[...], n_ref[...], sc_ref[...], sh_ref[...])
        dp = jnp.concatenate([dq_ref[...].astype(MXU_DTYPE), dkv_ref[...].astype(MXU_DTYPE),
                              dz_ref[...].astype(MXU_DTYPE), dxbc_ref[...].astype(MXU_DTYPE),
                              ddt_ref[...].astype(MXU_DTYPE)], axis=1)
        dh = _mm(dp, w_ref[...])
        dx, dn, dsc, dsh = vjp(dh)
        gx_ref[...] = dx1_ref[...] + dx
        h_ref[...] = h.astype(h_ref.dtype)
        dp_ref[...] = dp
        acc_ref[0:1, :] += dn
        acc_ref[1:2, :] += dsc
        acc_ref[2:3, :] += dsh

    vec = pl.BlockSpec((1, D_MODEL), _fixed)
    return pl.pallas_call(
        body, name="in_proj_bwd", grid=(S // tm,),
        in_specs=[pl.BlockSpec((tm, D_MODEL), _row), pl.BlockSpec((tm, D_MODEL), _row),
                  pl.BlockSpec((tm, ATTN_W), _row), pl.BlockSpec((tm, 2 * KV_W), _row),
                  pl.BlockSpec((tm, SSM_W), _row), pl.BlockSpec((tm, XBC_W), _row), pl.BlockSpec((tm, LANE), _row),
                  vec, vec, vec, pl.BlockSpec((IN_PAD, D_MODEL), _fixed)],
        out_specs=[pl.BlockSpec((tm, D_MODEL), _row), pl.BlockSpec((tm, D_MODEL), _row),
                   pl.BlockSpec((tm, IN_PAD), _row), pl.BlockSpec((8, D_MODEL), _fixed)],
        out_shape=[jax.ShapeDtypeStruct((S, D_MODEL), F32), jax.ShapeDtypeStruct((S, D_MODEL), MXU_DTYPE),
                   jax.ShapeDtypeStruct((S, IN_PAD), MXU_DTYPE), jax.ShapeDtypeStruct((8, D_MODEL), F32)],
        compiler_params=_cparams(VMEM_BIG),
    )(x, dx1, dq, dkv, dz, dxbc, ddt, norm1, scale1, shift1, w_in)


def _out_stage(ya, ys0, ys1, z0, z1, an, sn0, sn1):
    half = SSM_W // 2
    a = _rms(ya, an, ATTN_W)
    g0 = _rms(ys0 * _silu(z0), sn0, half)
    g1 = _rms(ys1 * _silu(z1), sn1, half)
    return jnp.concatenate([a, g0, g1], axis=1)


def _out_stage_args(ya_ref, ys_ref, z_ref, an_ref, sn_ref):
    half = SSM_W // 2
    return (ya_ref[...], ys_ref[:, :half], ys_ref[:, half:], z_ref[:, :half], z_ref[:, half:],
            an_ref[...], sn_ref[:, :half], sn_ref[:, half:])


def _out_proj_fwd(x, ya, ys, z, an, sn, gate1, w_o, tm):
    S = x.shape[0]

    def body(x_ref, ya_ref, ys_ref, z_ref, an_ref, sn_ref, g_ref, w_ref, x1_ref):
        u = _out_stage(*_out_stage_args(ya_ref, ys_ref, z_ref, an_ref, sn_ref))
        x1_ref[...] = x_ref[...] + g_ref[...] * _mm(u, w_ref[...])

    half = pl.BlockSpec((tm, ATTN_W), _row)
    hvec = pl.BlockSpec((1, ATTN_W), _fixed)
    return pl.pallas_call(
        body, name="out_proj_fwd", grid=(S // tm,),
        in_specs=[pl.BlockSpec((tm, D_MODEL), _row), half, half, half, hvec, hvec,
                  pl.BlockSpec((1, D_MODEL), _fixed), pl.BlockSpec((D_MODEL, D_MODEL), _fixed)],
        out_specs=pl.BlockSpec((tm, D_MODEL), _row),
        out_shape=jax.ShapeDtypeStruct((S, D_MODEL), F32),
        compiler_params=_cparams(VMEM_BIG),
    )(x, ya, ys, z, an, sn, gate1, w_o)


def _out_proj_bwd(dx1, ya, ys, z, an, sn, gate1, w_o, tm):
    S = dx1.shape[0]

    def body(dx1_ref, ya_ref, ys_ref, z_ref, an_ref, sn_ref, g_ref, w_ref,
             dya_ref, dys_ref, dz_ref, u_ref, dmix_ref, acc_ref):
        @pl.when(pl.program_id(0) == 0)
        def _():
            acc_ref[...] = jnp.zeros_like(acc_ref)

        u, vjp = jax.vjp(_out_stage, *_out_stage_args(ya_ref, ys_ref, z_ref, an_ref, sn_ref))
        dx1 = dx1_ref[...]
        mix = _mm(u, w_ref[...])
        dmix = dx1 * g_ref[...]
        du = _mm_nt(dmix, w_ref[...])
        dya, dys0, dys1, dz0, dz1, dan, dsn0, dsn1 = vjp(du)
        dya_ref[...] = dya
        dys_ref[...] = jnp.concatenate([dys0, dys1], axis=1)
        dz_ref[...] = jnp.concatenate([dz0, dz1], axis=1)
        u_ref[...] = u.astype(u_ref.dtype)
        dmix_ref[...] = dmix.astype(dmix_ref.dtype)
        acc_ref[0:1, :] += jnp.sum(dx1 * mix, axis=0, keepdims=True)
        acc_ref[1:2, :] += jnp.concatenate([dan, dsn0, dsn1], axis=1)

    half = pl.BlockSpec((tm, ATTN_W), _row)
    hvec = pl.BlockSpec((1, ATTN_W), _fixed)
    full = pl.BlockSpec((tm, D_MODEL), _row)
    return pl.pallas_call(
        body, name="out_proj_bwd", grid=(S // tm,),
        in_specs=[full, half, half, half, hvec, hvec,
                  pl.BlockSpec((1, D_MODEL), _fixed), pl.BlockSpec((D_MODEL, D_MODEL), _fixed)],
        out_specs=[half, half, half, full, full, pl.BlockSpec((8, D_MODEL), _fixed)],
        out_shape=[jax.ShapeDtypeStruct((S, ATTN_W), F32)] * 3
        + [jax.ShapeDtypeStruct((S, D_MODEL), MXU_DTYPE)] * 2 + [jax.ShapeDtypeStruct((8, D_MODEL), F32)],
        compiler_params=_cparams(VMEM_BIG),
    )(dx1, ya, ys, z, an, sn, gate1, w_o)


def _loss_rows(x2, fn, tgt):
    y = _rms(x2, fn, D_MODEL)
    per_row = jnp.sum(jnp.square(y - tgt), axis=1, keepdims=True)
    return jnp.sum(per_row, axis=0, keepdims=True) * (0.5 / D_MODEL)


def _mlp_loss(x1, tgt, norm2, scale2, shift2, gate2, fnorm, w_gu, w_d, tm):
    S = x1.shape[0]
    n_gu = 2 * D_FF // N_DEV
    half = n_gu // 2

    def body(x1_ref, t_ref, n_ref, sc_ref, sh_ref, g_ref, fn_ref, wgu_a, wgu_b, wd_hbm,
             dx1_ref, h_ref, dgu_ref, act_ref, dmlp_ref, acc_ref, wgu, wd, wsem):
        @pl.when(pl.program_id(0) == 0)
        def _():
            acc_ref[...] = jnp.zeros_like(acc_ref)
            copies = [pltpu.make_async_copy(wd_hbm, wd, wsem.at[2 * N_DEV])]
            for j in range(N_DEV):
                copies.append(pltpu.make_async_copy(wgu_a.at[j], wgu.at[pl.ds(j * n_gu, half)], wsem.at[2 * j]))
                copies.append(pltpu.make_async_copy(wgu_b.at[j], wgu.at[pl.ds(j * n_gu + half, half)],
                                                    wsem.at[2 * j + 1]))
            for cp in copies:
                cp.start()
            for cp in copies:
                cp.wait()

        x1 = x1_ref[...]
        gate2 = g_ref[...]
        h, vjp_h = jax.vjp(_modnorm, x1, n_ref[...], sc_ref[...], sh_ref[...])
        hb = h.astype(MXU_DTYPE)
        gu = _mm_nt(hb, wgu[...])
        g, u = gu[:, :D_FF], gu[:, D_FF:]
        sg = jax.nn.sigmoid(g)
        silu_g = g * sg
        act = (silu_g * u).astype(MXU_DTYPE)
        mlp = _mm(act, wd[...])
        x2 = x1 + gate2 * mlp
        loss, vjp_loss = jax.vjp(_loss_rows, x2, fn_ref[...], t_ref[...])
        dx2, dfn, _ = vjp_loss(jnp.ones((1, 1), F32))
        dmlp = (dx2 * gate2).astype(MXU_DTYPE)
        dact = _mm_nt(dmlp, wd[...])
        dg = dact * u * (sg * (1.0 + g * (1.0 - sg)))
        du = dact * silu_g
        dgu = jnp.concatenate([dg, du], axis=1).astype(MXU_DTYPE)
        dh = _mm(dgu, wgu[...])
        dx, dn, dsc, dsh = vjp_h(dh)
        dx1_ref[...] = dx2 + dx
        h_ref[...] = hb
        dgu_ref[...] = dgu
        act_ref[...] = act
        dmlp_ref[...] = dmlp
        acc_ref[0:1, :] += dn
        acc_ref[1:2, :] += dsc
        acc_ref[2:3, :] += dsh
        acc_ref[3:4, :] += jnp.sum(dx2 * mlp, axis=0, keepdims=True)
        acc_ref[4:5, :] += dfn
        acc_ref[5:6, :] += jnp.broadcast_to(loss, (1, D_MODEL))

    full = pl.BlockSpec((tm, D_MODEL), _row)
    vec = pl.BlockSpec((1, D_MODEL), _fixed)
    anyspec = pl.BlockSpec(memory_space=pl.ANY)
    return pl.pallas_call(
        body, name="mlp_loss", grid=(S // tm,),
        in_specs=[full, full, vec, vec, vec, vec, vec, anyspec, anyspec, anyspec],
        out_specs=[full, full, pl.BlockSpec((tm, 2 * D_FF), _row), pl.BlockSpec((tm, D_FF), _row), full,
                   pl.BlockSpec((8, D_MODEL), _fixed)],
        out_shape=[jax.ShapeDtypeStruct((S, D_MODEL), F32), jax.ShapeDtypeStruct((S, D_MODEL), MXU_DTYPE),
                   jax.ShapeDtypeStruct((S, 2 * D_FF), MXU_DTYPE), jax.ShapeDtypeStruct((S, D_FF), MXU_DTYPE),
                   jax.ShapeDtypeStruct((S, D_MODEL), MXU_DTYPE), jax.ShapeDtypeStruct((8, D_MODEL), F32)],
        scratch_shapes=[pltpu.VMEM((2 * D_FF, D_MODEL), MXU_DTYPE), pltpu.VMEM((D_FF, D_MODEL), MXU_DTYPE),
                        pltpu.SemaphoreType.DMA((2 * N_DEV + 1,))],
        compiler_params=_cparams(VMEM_BIG),
    )(x1, tgt, norm2, scale2, shift2, gate2, fnorm, w_gu[0], w_gu[1], w_d)


def _wgrad(a, g, tk, ts, name, xchg=None):
    S, K = a.shape
    N = g.shape[1]
    ns = S // ts

    def body(a_ref, g_ref, o_ref, acc_ref):
        s = pl.program_id(1)

        @pl.when(s == 0)
        def _():
            acc_ref[...] = jnp.zeros_like(acc_ref)

        acc_ref[...] += _mm_tn(a_ref[...], g_ref[...])

        @pl.when(s == ns - 1)
        def _():
            o_ref[...] = acc_ref[...].astype(o_ref.dtype)

    in_specs = [pl.BlockSpec((ts, tk), lambda j, s: (s, j)), pl.BlockSpec((ts, N), lambda j, s: (s, 0))]
    out_spec = pl.BlockSpec((tk, N), lambda j, s: (j, 0))
    out_shape = jax.ShapeDtypeStruct((K, N), WIRE_DTYPE)
    scratch = [pltpu.VMEM((tk, N), F32)]
    if xchg is None:
        return pl.pallas_call(body, name=name, grid=(K // tk, ns), in_specs=in_specs, out_specs=out_spec,
                              out_shape=out_shape, scratch_shapes=scratch, compiler_params=_cparams(VMEM_BIG))(a, g)
    (out,), x_out = _hosted_call(body, name, (K // tk, ns), in_specs, [out_spec], [out_shape], scratch, (a, g), xchg,
                                 _cparams(VMEM_BIG))
    return out, x_out


MASKED = -1e30
QK_SCALE = HALF ** -0.5


def _attn_bias(buckets, rel_bias):
    def body(bk_ref, relb_ref, out_ref):
        bk = bk_ref[...]
        i = lax.broadcasted_iota(jnp.int32, (BLK, 2 * BLK), 0)
        j = lax.broadcasted_iota(jnp.int32, (BLK, 2 * BLK), 1)
        window = (j > i) & (j <= i + BLK)
        for h in range(N_HEADS):
            acc = jnp.zeros((BLK, 2 * BLK), F32)
            for b in range(N_BUCKETS):
                acc = jnp.where(bk == b, relb_ref[b, h], acc)
            out_ref[0, h] = jnp.where(window, acc, MASKED)
            out_ref[1, h] = jnp.where(window & (j >= BLK), acc, MASKED)

    return pl.pallas_call(
        body, name="attn_bias", out_shape=jax.ShapeDtypeStruct((2, N_HEADS, BLK, 2 * BLK), F32),
        in_specs=[pl.BlockSpec(memory_space=pltpu.VMEM), pl.BlockSpec(memory_space=pltpu.SMEM)],
    )(buckets, rel_bias)


def _attn_kv(kvp_ref, kvc_ref):
    kvp = kvp_ref[...].astype(F32)
    kvc = kvc_ref[...].astype(F32)
    kp, kc = _split_pair(kvp[:, :LANE]), _split_pair(kvc[:, :LANE])
    vp, vc = _split_pair(kvp[:, LANE:]), _split_pair(kvc[:, LANE:])
    k_pads = [jnp.concatenate([kp[g], kc[g]], axis=0).astype(MXU_DTYPE) for g in range(2)]
    v_pads = [jnp.concatenate([vp[g], vc[g]], axis=0).astype(MXU_DTYPE) for g in range(2)]
    return k_pads, v_pads


def _attn_fwd(qkv, bias, sinks, xchg):
    S = qkv.shape[0]
    nb = S // BLK

    def body(q_ref, kvp_ref, kvc_ref, bias_ref, sinks_ref, y_ref):
        first = jnp.where(pl.program_id(0) == 0, 1, 0)
        q_heads = _split_heads(q_ref[...].astype(F32) * QK_SCALE, 4)
        k_pads, v_pads = _attn_kv(kvp_ref, kvc_ref)
        heads = range(N_HEADS)
        s = [_mm_nt(q_heads[h].astype(MXU_DTYPE), k_pads[h // 4]) + bias_ref[first, h] for h in heads]
        m = [jnp.maximum(jnp.max(s[h], axis=-1, keepdims=True), sinks_ref[h]) for h in heads]
        p = [jnp.exp(s[h] - m[h]) for h in heads]
        rinv = [1.0 / (jnp.sum(p[h], axis=-1, keepdims=True) + jnp.exp(sinks_ref[h] - m[h])) for h in heads]
        y_ref[...] = _join_heads([_mm(p[h], v_pads[h // 4]) * rinv[h] for h in heads])

    smem = pl.BlockSpec(memory_space=pltpu.SMEM)
    return _hosted_call(
        body, "attn_fwd", nb,
        in_specs=[pl.BlockSpec((BLK, ATTN_W), _row),
                  pl.BlockSpec((BLK, 2 * KV_W), lambda i: (jnp.maximum(i - 1, 0), 2)),
                  pl.BlockSpec((BLK, 2 * KV_W), lambda i: (i, 2)),
                  pl.BlockSpec((2, N_HEADS, BLK, 2 * BLK), lambda i: (0, 0, 0, 0)), smem],
        out_specs=[pl.BlockSpec((BLK, ATTN_W), _row)],
        out_shape=[jax.ShapeDtypeStruct((S, ATTN_W), F32)],
        scratch_shapes=[],
        args=(qkv, qkv, qkv, bias, sinks), xchg=xchg, cparams=_cparams(),
    )


def _attn_bwd(qkv, y, dy, bias, sinks, xchg):
    S = qkv.shape[0]
    nb = S // BLK

    def body(q_ref, kvp_ref, kvc_ref, y_ref, dy_ref, bias_ref, sinks_ref, dq_ref, dkv_ref, dbias_ref, dsk_ref, carry_ref):
        i = pl.program_id(0)

        @pl.when(i == 0)
        def _():
            dbias_ref[...] = jnp.zeros_like(dbias_ref)
            dsk_ref[...] = jnp.zeros_like(dsk_ref)
            carry_ref[...] = jnp.zeros_like(carry_ref)

        first = jnp.where(i == nb - 1, 1, 0)
        q_heads = _split_heads(q_ref[...].astype(F32) * QK_SCALE, 4)
        k_pads, v_pads = _attn_kv(kvp_ref, kvc_ref)
        y_heads = _split_heads(y_ref[...], 4)
        dy_heads = _split_heads(dy_ref[...], 4)
        heads = range(N_HEADS)
        qs = [q_heads[h].astype(MXU_DTYPE) for h in heads]
        s = [_mm_nt(qs[h], k_pads[h // 4]) + bias_ref[first, h] for h in heads]
        m = [jnp.maximum(jnp.max(s[h], axis=-1, keepdims=True), sinks_ref[h]) for h in heads]
        p = [jnp.exp(s[h] - m[h]) for h in heads]
        esink = [jnp.exp(sinks_ref[h] - m[h]) for h in heads]
        rinv = [1.0 / (jnp.sum(p[h], axis=-1, keepdims=True) + esink[h]) for h in heads]
        t = [dy_heads[h] * rinv[h] for h in heads]
        delta = [jnp.sum(t[h] * y_heads[h], axis=-1, keepdims=True) for h in heads]
        tb = [t[h].astype(MXU_DTYPE) for h in heads]
        dp = [_mm_nt(tb[h], v_pads[h // 4]) for h in heads]
        ds = [p[h] * (dp[h] - delta[h]) for h in heads]
        for h in heads:
            dbias_ref[h] += ds[h]
            dsk_ref[h] -= esink[h] * delta[h]
        dsb = [ds[h].astype(MXU_DTYPE) for h in heads]
        pb = [p[h].astype(MXU_DTYPE) for h in heads]
        dq_heads = [_mm(dsb[h], k_pads[h // 4]) * QK_SCALE for h in heads]
        dk_pads = [_mm_tn(jnp.concatenate(dsb[4 * g:4 * g + 4], axis=0), jnp.concatenate(qs[4 * g:4 * g + 4], axis=0))
                   for g in range(2)]
        dv_pads = [_mm_tn(jnp.concatenate(pb[4 * g:4 * g + 4], axis=0), jnp.concatenate(tb[4 * g:4 * g + 4], axis=0))
                   for g in range(2)]
        dq_ref[...] = _join_heads(dq_heads)
        dk_prev = _join_pair(dk_pads[0][:BLK], dk_pads[1][:BLK])
        dk_cur = _join_pair(dk_pads[0][BLK:], dk_pads[1][BLK:])
        dv_prev = _join_pair(dv_pads[0][:BLK], dv_pads[1][:BLK])
        dv_cur = _join_pair(dv_pads[0][BLK:], dv_pads[1][BLK:])
        dkv_ref[...] = jnp.concatenate([dk_cur, dv_cur], axis=1) + carry_ref[...]
        carry_ref[...] = jnp.concatenate([dk_prev, dv_prev], axis=1)

    smem = pl.BlockSpec(memory_space=pltpu.SMEM)
    rev = lambda i: (nb - 1 - i, 0)
    return _hosted_call(
        body, "attn_bwd", nb,
        in_specs=[pl.BlockSpec((BLK, ATTN_W), rev),
                  pl.BlockSpec((BLK, 2 * KV_W), lambda i: (jnp.maximum(nb - 2 - i, 0), 2)),
                  pl.BlockSpec((BLK, 2 * KV_W), lambda i: (nb - 1 - i, 2)),
                  pl.BlockSpec((BLK, ATTN_W), rev), pl.BlockSpec((BLK, ATTN_W), rev),
                  pl.BlockSpec((2, N_HEADS, BLK, 2 * BLK), lambda i: (0, 0, 0, 0)), smem],
        out_specs=[pl.BlockSpec((BLK, ATTN_W), rev), pl.BlockSpec((BLK, 2 * KV_W), rev),
                   pl.BlockSpec((N_HEADS, BLK, 2 * BLK), lambda i: (0, 0, 0)),
                   pl.BlockSpec((N_HEADS, BLK, 1), lambda i: (0, 0, 0))],
        out_shape=[jax.ShapeDtypeStruct((S, ATTN_W), F32), jax.ShapeDtypeStruct((S, 2 * KV_W), F32),
                   jax.ShapeDtypeStruct((N_HEADS, BLK, 2 * BLK), F32), jax.ShapeDtypeStruct((N_HEADS, BLK, 1), F32)],
        scratch_shapes=[pltpu.VMEM((BLK, 2 * KV_W), F32)],
        args=(qkv, qkv, qkv, y, dy, bias, sinks), xchg=xchg, cparams=_cparams(),
    )


def _attn_finish(dbias, dsk, buckets):
    def body(db_ref, dsk_ref, bk_ref, drel_ref, dsink_ref):
        bk = bk_ref[...]
        r = lax.broadcasted_iota(jnp.int32, (N_BUCKETS, LANE), 0)
        l = lax.broadcasted_iota(jnp.int32, (N_BUCKETS, LANE), 1)
        row = lax.broadcasted_iota(jnp.int32, (N_HEADS, LANE), 0)
        res = jnp.zeros((N_BUCKETS, LANE), F32)
        dsink = jnp.zeros((N_HEADS, LANE), F32)
        for h in range(N_HEADS):
            db = db_ref[h]
            for b in range(N_BUCKETS):
                v = jnp.sum(jnp.sum(jnp.where(bk == b, db, 0.0), axis=1, keepdims=True), axis=0, keepdims=True)
                res = res + jnp.where((r == b) & (l == h), v, 0.0)
            dsink = dsink + jnp.where(row == h, jnp.sum(dsk_ref[h], axis=0, keepdims=True), 0.0)
        drel_ref[...] = res
        dsink_ref[...] = dsink

    return pl.pallas_call(body, name="attn_finish",
                          out_shape=[jax.ShapeDtypeStruct((N_BUCKETS, LANE), F32),
                                     jax.ShapeDtypeStruct((N_HEADS, LANE), F32)])(dbias, dsk, buckets)


def _ssd_consts():
    r = lax.broadcasted_iota(jnp.int32, (BLK, BLK), 0)
    c = lax.broadcasted_iota(jnp.int32, (BLK, BLK), 1)
    causal = c <= r
    upper = (r <= c).astype(F32)
    last = r == BLK - 1
    head = lax.broadcasted_iota(jnp.int32, (N_HEADS, BLK), 0)
    return causal, upper, last, head


def _ssd_chunk(xs, bg, cg, dt_raw_t, prev, dtb, alog, d_rows, consts):
    causal, upper, last, head = consts
    dt_t = _softplus(dt_raw_t + dtb)
    acs_t = _mm_hi(dt_t * (-jnp.exp(alog)), upper)
    cb = [_mm_nt(cg[g], bg[g]) for g in range(2)]
    heads = range(N_HEADS)
    dt_row = [jnp.sum(jnp.where(head == h, dt_t, 0.0), axis=0, keepdims=True) for h in heads]
    a_row = [jnp.sum(jnp.where(head == h, acs_t, 0.0), axis=0, keepdims=True) for h in heads]
    a_rb = [jnp.broadcast_to(a_row[h], (BLK, BLK)) for h in heads]
    a_b = [a_rb[h].T for h in heads]
    a_last = [jnp.sum(jnp.where(last, a_b[h], 0.0), axis=0, keepdims=True) for h in heads]
    w = [cb[h // 4] * jnp.exp(jnp.where(causal, a_b[h] - a_rb[h], -1e30)) * dt_row[h] for h in heads]
    f_b = [jnp.broadcast_to(dt_row[h] * jnp.exp(a_last[h] - a_row[h]), (BLK, BLK)).T for h in heads]
    y_in = [_mm(w[h], xs[h]) for h in heads]
    y_off = [_mm(cg[h // 4], prev[h]) * jnp.exp(a_b[h]) for h in heads]
    st = [_mm_tn(bg[h // 4], xs[h] * f_b[h]) for h in heads]
    ys = [y_in[h] + y_off[h] + d_rows[h] * xs[h] for h in heads]
    hs = [prev[h] * jnp.exp(a_last[h]) + st[h] for h in heads]
    return tuple(ys), tuple(hs)


def _dt_rows(dt_blk):
    return dt_blk.T[:N_HEADS]


def _silu_grad(x):
    s = jax.nn.sigmoid(x)
    return s * (1.0 + x * (1.0 - s))


def _conv_pre(halo, blk, cw_ref, cb_ref):
    ext = jnp.concatenate([halo, blk], axis=0)
    taps = [pltpu.roll(ext, 3 - k, 0)[8:] for k in range(3)] + [blk]
    pre = cb_ref[...] + cw_ref[0:1, :] * taps[0]
    for k in range(1, 4):
        pre = pre + cw_ref[k:k + 1, :] * taps[k]
    return pre, taps


def _ssd_split(pre):
    heads = _split_heads(pre[:, :SSM_W], 4)
    pb = [pre[:, SSM_W + g * D_STATE:SSM_W + (g + 1) * D_STATE] for g in range(2)]
    pc = [pre[:, SSM_W + 2 * D_STATE + g * D_STATE:SSM_W + 2 * D_STATE + (g + 1) * D_STATE] for g in range(2)]
    return heads, pb, pc


def _ssd_fwd(xbc, dt_raw, conv_w, conv_b, dtb_row, alog_row, d_exp, xchg):
    S = xbc.shape[0]
    nc = S // BLK

    def body(xbc_ref, halo_ref, dt_ref, cw_ref, cb_ref, dtb_ref, alog_ref, d_ref, y_ref, prev_ref, state_ref):
        i = pl.program_id(0)

        @pl.when(i == 0)
        def _():
            state_ref[...] = jnp.zeros_like(state_ref)

        halo = halo_ref[...] * jnp.where(i > 0, 1.0, 0.0)
        pre, _ = _conv_pre(halo, xbc_ref[...], cw_ref, cb_ref)
        heads, pb, pc = _ssd_split(_silu(pre))
        prev = [state_ref[h] for h in range(N_HEADS)]
        for h in range(N_HEADS):
            prev_ref[0, h] = prev[h]
        d_rows = [d_ref[h:h + 1, :] for h in range(N_HEADS)]
        ys, hs = _ssd_chunk(heads, pb, pc, _dt_rows(dt_ref[...]), prev, dtb_ref[...], alog_ref[...], d_rows,
                            _ssd_consts())
        for h in range(N_HEADS):
            state_ref[h] = hs[h]
        y_ref[...] = _join_heads(ys)

    vec = pl.BlockSpec((N_HEADS, LANE), _fixed)
    return _hosted_call(
        body, "ssd_fwd", nc,
        in_specs=[pl.BlockSpec((BLK, XBC_W), _row),
                  pl.BlockSpec((8, XBC_W), lambda i: (jnp.maximum(i * (BLK // 8) - 1, 0), 0)),
                  pl.BlockSpec((BLK, LANE), _row),
                  pl.BlockSpec((4, XBC_W), _fixed), pl.BlockSpec((1, XBC_W), _fixed), vec, vec,
                  pl.BlockSpec((N_HEADS, LANE), _fixed)],
        out_specs=[pl.BlockSpec((BLK, SSM_W), _row),
                   pl.BlockSpec((1, N_HEADS, D_STATE, LANE), lambda i: (i, 0, 0, 0))],
        out_shape=[jax.ShapeDtypeStruct((S, SSM_W), F32), jax.ShapeDtypeStruct((nc, N_HEADS, D_STATE, LANE), F32)],
        scratch_shapes=[pltpu.VMEM((N_HEADS, D_STATE, LANE), F32)],
        args=(xbc, xbc, dt_raw, conv_w, conv_b, dtb_row, alog_row, d_exp), xchg=xchg, cparams=_cparams(),
    )


def _ssd_bwd(xbc, dt_raw, prev_states, dy, conv_w, conv_b, dtb_row, alog_row, d_exp, xchg):
    S = xbc.shape[0]
    nc = S // BLK

    def body(xbc_ref, halo_ref, dt_ref, prev_ref, dy_ref, cw_ref, cb_ref, dtb_ref, alog_ref, d_ref,
             dxbc_ref, ddt_ref, dcw_ref, dvec_ref, dd_ref, gstate_ref, ghalo_ref):
        i = pl.program_id(0)
        c = nc - 1 - i

        @pl.when(i == 0)
        def _():
            gstate_ref[...] = jnp.zeros_like(gstate_ref)
            ghalo_ref[...] = jnp.zeros_like(ghalo_ref)
            dcw_ref[...] = jnp.zeros_like(dcw_ref)
            dvec_ref[...] = jnp.zeros_like(dvec_ref)
            dd_ref[...] = jnp.zeros_like(dd_ref)

        halo = halo_ref[...] * jnp.where(c > 0, 1.0, 0.0)
        pre, taps = _conv_pre(halo, xbc_ref[...], cw_ref, cb_ref)
        heads, pb, pc = _ssd_split(_silu(pre))
        prev = [prev_ref[0, h] for h in range(N_HEADS)]
        d_rows = [d_ref[h:h + 1, :] for h in range(N_HEADS)]
        _, vjp = jax.vjp(functools.partial(_ssd_chunk, consts=_ssd_consts()),
                         heads, pb, pc, _dt_rows(dt_ref[...]), prev, dtb_ref[...], alog_ref[...], d_rows)
        dys = tuple(_split_heads(dy_ref[...], 4))
        dhs = tuple(gstate_ref[h] for h in range(N_HEADS))
        dheads, dpb, dpc, ddt_t, dprev, ddtb, dalog, dd_rows = vjp((dys, dhs))
        for h in range(N_HEADS):
            gstate_ref[h] = dprev[h]
            dd_ref[h:h + 1, :] += dd_rows[h]
        ddt_ref[...] = jnp.concatenate([ddt_t, jnp.zeros((BLK - N_HEADS, BLK), F32)], axis=0).T
        dvec_ref[0:N_HEADS, :] += ddtb
        dvec_ref[N_HEADS:, :] += dalog
        dpre = jnp.concatenate([_join_heads(dheads)] + list(dpb) + list(dpc), axis=1) * _silu_grad(pre)
        zeros8 = jnp.zeros((8, XBC_W), F32)
        dpe = jnp.concatenate([zeros8, dpre, zeros8], axis=0)
        n_ext = 16 + BLK
        dext = cw_ref[3:4, :] * dpe[:8 + BLK]
        dcw_ref[3:4, :] += jnp.sum(dpre * taps[3], axis=0, keepdims=True)
        for k in range(3):
            dext = dext + cw_ref[k:k + 1, :] * pltpu.roll(dpe, n_ext - (3 - k), 0)[:8 + BLK]
            dcw_ref[k:k + 1, :] += jnp.sum(dpre * taps[k], axis=0, keepdims=True)
        dcw_ref[4:5, :] += jnp.sum(dpre, axis=0, keepdims=True)
        dxbc_ref[...] = dext[8:, :]
        dxbc_ref[BLK - 8:BLK, :] += ghalo_ref[...]
        ghalo_ref[...] = dext[:8, :]

    vec = pl.BlockSpec((N_HEADS, LANE), _fixed)
    rev = lambda i: (nc - 1 - i, 0)
    return _hosted_call(
        body, "ssd_bwd", nc,
        in_specs=[pl.BlockSpec((BLK, XBC_W), rev),
                  pl.BlockSpec((8, XBC_W), lambda i: (jnp.maximum((nc - 1 - i) * (BLK // 8) - 1, 0), 0)),
                  pl.BlockSpec((BLK, LANE), rev),
                  pl.BlockSpec((1, N_HEADS, D_STATE, LANE), lambda i: (nc - 1 - i, 0, 0, 0)),
                  pl.BlockSpec((BLK, SSM_W), rev),
                  pl.BlockSpec((4, XBC_W), _fixed), pl.BlockSpec((1, XBC_W), _fixed), vec, vec,
                  pl.BlockSpec((N_HEADS, LANE), _fixed)],
        out_specs=[pl.BlockSpec((BLK, XBC_W), rev), pl.BlockSpec((BLK, LANE), rev),
                   pl.BlockSpec((8, XBC_W), _fixed), pl.BlockSpec((2 * N_HEADS, LANE), _fixed),
                   pl.BlockSpec((N_HEADS, LANE), _fixed)],
        out_shape=[jax.ShapeDtypeStruct((S, XBC_W), F32), jax.ShapeDtypeStruct((S, LANE), F32),
                   jax.ShapeDtypeStruct((8, XBC_W), F32), jax.ShapeDtypeStruct((2 * N_HEADS, LANE), F32),
                   jax.ShapeDtypeStruct((N_HEADS, LANE), F32)],
        scratch_shapes=[pltpu.VMEM((N_HEADS, D_STATE, LANE), F32), pltpu.VMEM((8, XBC_W), F32)],
        args=(xbc, xbc, dt_raw, prev_states, dy, conv_w, conv_b, dtb_row, alog_row, d_exp), xchg=xchg,
        cparams=_cparams(VMEM_BIG),
    )


def _adamw_math(w, g, m, v):
    m = ADAM_B1 * m + (1.0 - ADAM_B1) * g
    v = ADAM_B2 * v + (1.0 - ADAM_B2) * jnp.square(g)
    m_hat = m / (1.0 - ADAM_B1 ** ADAM_STEP)
    v_hat = v / (1.0 - ADAM_B2 ** ADAM_STEP)
    delta = -ADAM_LR * (m_hat / (jnp.sqrt(v_hat) + ADAM_EPS) + ADAM_WD * w)
    return delta, m, v


def _reduce_adamw(parts, w, m, v, name):
    R, C = w.shape

    def body(p_ref, w_ref, m_ref, v_ref, g_ref, d_ref, nm_ref, nv_ref):
        g = p_ref[0].astype(F32)
        for i in range(1, N_DEV):
            g = g + p_ref[i].astype(F32)
        d, nm, nv = _adamw_math(w_ref[...], g, m_ref[...], v_ref[...])
        g_ref[...] = g
        d_ref[...] = d
        nm_ref[...] = nm
        nv_ref[...] = nv

    if R % 16 == 0:
        tr = max(t for t in range(16, 257, 16) if R % t == 0)
        n, blk, pblk = R // tr, pl.BlockSpec((tr, C), _row), pl.BlockSpec((N_DEV, tr, C), lambda i: (0, i, 0))
    else:
        tl = 256
        n, blk, pblk = C // tl, pl.BlockSpec((R, tl), lambda i: (0, i)), pl.BlockSpec((N_DEV, R, tl),
                                                                                      lambda i: (0, 0, i))
    return pl.pallas_call(
        body, name=name, grid=(n,), in_specs=[pblk, blk, blk, blk],
        out_specs=[blk] * 4, out_shape=[jax.ShapeDtypeStruct((R, C), F32)] * 4,
    )(parts, w, m, v)


def _reduce_adamw_hosting(parts_list, wmv_list, name, xchg):
    n_arr = len(parts_list)
    C = wmv_list[0][0].shape[1]
    tl = 256

    def body(*refs):
        p_refs, wmv_refs, o_refs = refs[:n_arr], refs[n_arr:4 * n_arr], refs[4 * n_arr:]
        for k in range(n_arr):
            g = p_refs[k][0].astype(F32)
            for i in range(1, N_DEV):
                g = g + p_refs[k][i].astype(F32)
            w_ref, m_ref, v_ref = wmv_refs[3 * k:3 * k + 3]
            d, nm, nv = _adamw_math(w_ref[...], g, m_ref[...], v_ref[...])
            for o, val in zip(o_refs[4 * k:4 * k + 4], (g, d, nm, nv)):
                o[...] = val

    in_specs = [pl.BlockSpec((N_DEV, w.shape[0], tl), lambda i: (0, 0, i)) for w, _, _ in wmv_list]
    in_specs += [pl.BlockSpec((w.shape[0], tl), lambda i: (0, i)) for w, _, _ in wmv_list for _ in range(3)]
    out_specs = [pl.BlockSpec((w.shape[0], tl), lambda i: (0, i)) for w, _, _ in wmv_list for _ in range(4)]
    out_shape = [jax.ShapeDtypeStruct(w.shape, F32) for w, _, _ in wmv_list for _ in range(4)]
    args = list(parts_list) + [a for wmv in wmv_list for a in wmv]
    outs, x_out = _hosted_call(body, name, C // tl, in_specs, out_specs, out_shape, [], args, xchg,
                               _cparams(VMEM_BIG))
    return [outs[4 * k:4 * k + 4] for k in range(n_arr)], x_out


_SMALL_NAMES = ("ada_b", "norm1", "conv_w", "conv_b", "dt_bias", "A_log", "D_skip", "sinks", "attn_out_norm",
                "ssm_out_norm", "norm2", "rel_bias", "final_norm")
N_MOD = 6 * D_MODEL


def _mod_row(a0, a1, a2):
    return jnp.concatenate([a0[2:3], a0[1:2], a1[0:1], a2[2:3], a2[1:2], a2[3:4]], axis=1)


def _small_update(gathered, params):
    n_g = len(gathered)
    flat = [a for name in _SMALL_NAMES for a in params[name]]

    def body(*refs):
        a0_ref, a1_ref, a2_ref, cw_ref, dv_ref, dd_ref, ds_ref, dr_ref, c_ref = refs[:n_g]
        wmv = refs[n_g:n_g + len(flat)]
        outs = refs[n_g + len(flat):]

        def total(ref):
            t = ref[0]
            for i in range(1, N_DEV):
                t = t + ref[i]
            return t

        t0, t1, t2, tcw, tdv, tdd, tds, tdr = [total(r) for r in (a0_ref, a1_ref, a2_ref, cw_ref, dv_ref, dd_ref,
                                                                   ds_ref, dr_ref)]
        r8 = lax.broadcasted_iota(jnp.int32, (N_HEADS, LANE), 0)
        l8 = lax.broadcasted_iota(jnp.int32, (N_HEADS, LANE), 1)

        def diag_row(t):
            return jnp.sum(jnp.where(r8 == l8, t, 0.0), axis=0, keepdims=True)[:, :N_HEADS]

        def lane_sums(t):
            return diag_row(jnp.broadcast_to(jnp.sum(t, axis=1, keepdims=True), (N_HEADS, LANE)))

        me = _lin(_my_pos())
        n_cw = XBC_W // N_DEV
        cw_mine = jnp.zeros((4, n_cw), F32)
        for j in range(N_DEV):
            cw_mine = cw_mine + tcw[0:4, j * n_cw:(j + 1) * n_cw] * jnp.where(me == j, 1.0, 0.0)
        grads = {
            "ada_b": _mod_row(t0, t1, t2), "norm1": t0[0:1], "conv_w": cw_mine, "conv_b": tcw[4:5],
            "dt_bias": lane_sums(tdv[:N_HEADS]), "A_log": lane_sums(tdv[N_HEADS:]), "D_skip": lane_sums(tdd),
            "sinks": diag_row(tds), "attn_out_norm": t1[1:2, :ATTN_W], "ssm_out_norm": t1[1:2, ATTN_W:],
            "norm2": t2[0:1], "rel_bias": tdr[:, :N_HEADS], "final_norm": t2[4:5],
        }
        for k, name in enumerate(_SMALL_NAMES):
            w_ref, m_ref, v_ref = wmv[3 * k:3 * k + 3]
            g = grads[name]
            d, nm, nv = _adamw_math(w_ref[...], g, m_ref[...], v_ref[...])
            for o, val in zip(outs[4 * k:4 * k + 4], (g, d, nm, nv)):
                o[...] = val
        loss_ref, call_ref, dmod_ref = outs[4 * len(_SMALL_NAMES):]
        loss_ref[...] = t2[5:6, 0:1]
        call_ref[...] = jnp.concatenate([c_ref[i] for i in range(N_DEV)], axis=0)
        dmod_ref[...] = jnp.concatenate([_mod_row(a0_ref[i], a1_ref[i], a2_ref[i]) for i in range(N_DEV)], axis=0)

    out_shape = [jax.ShapeDtypeStruct(params[name][0].shape, F32) for name in _SMALL_NAMES for _ in range(4)]
    out_shape += [jax.ShapeDtypeStruct((1, 1), F32), jax.ShapeDtypeStruct((N_DEV, D_MODEL), F32),
                  jax.ShapeDtypeStruct((N_DEV, N_MOD), F32)]
    res = pl.pallas_call(body, name="small_update", out_shape=out_shape)(*gathered, *flat)
    upd = {name: res[4 * k:4 * k + 4] for k, name in enumerate(_SMALL_NAMES)}
    loss, c_all, dmod_all = res[4 * len(_SMALL_NAMES):]
    return upd, loss, c_all, dmod_all


def _ada_w_update(c_all, dmod_all, w, m, v):
    chunk = w.shape[1]

    def body(c_ref, dm_ref, w_ref, m_ref, v_ref, g_ref, d_ref, nm_ref, nv_ref):
        me = _lin(_my_pos())
        dm = jnp.zeros((N_DEV, chunk), F32)
        for j in range(N_DEV):
            dm = dm + dm_ref[:, j * chunk:(j + 1) * chunk] * jnp.where(me == j, 1.0, 0.0)
        g = lax.dot_general(_silu(c_ref[...]), dm, (((0,), (0,)), ((), ())), precision=HI,
                            preferred_element_type=F32)
        d, nm, nv = _adamw_math(w_ref[...], g, m_ref[...], v_ref[...])
        g_ref[...] = g
        d_ref[...] = d
        nm_ref[...] = nm
        nv_ref[...] = nv

    return pl.pallas_call(body, name="ada_w_update", out_shape=[jax.ShapeDtypeStruct(w.shape, F32)] * 4,
                          compiler_params=_cparams(VMEM_BIG))(c_all, dmod_all, w, m, v)


def _local_step(x, tgt, c, mod, w_in, conv_w, w_o_mine, w_gu_mine, w_d_mine, p):
    S = x.shape[0]
    tm = min(512, S)
    tmm = min(256, S)
    tw = min(2048, S)
    shift1, scale1, gate1, shift2, scale2, gate2 = [mod[i:i + 1] for i in range(6)]
    buckets = jnp.asarray(_t5_bucket_table())
    per_head = lambda a: jnp.broadcast_to(a.reshape(N_HEADS, 1), (N_HEADS, LANE))
    dtb_row, alog_row, d_exp = per_head(p["dt_bias"]), per_head(p["A_log"]), per_head(p["D_skip"])
    sinks = p["sinks"].reshape(N_HEADS)

    half = w_gu_mine.shape[0] // 2
    (qkv, z, xbc, dt_raw), (g_d,) = _in_proj_fwd(x, p["norm1"], scale1, shift1, w_in, tm, ([w_d_mine], False))
    bias = _attn_bias(buckets, p["rel_bias"])
    (ya,), (g_gu_a,) = _attn_fwd(qkv, bias, sinks, ([w_gu_mine[:half]], False))
    (ys, prev_states), (g_gu_b, g_o) = _ssd_fwd(xbc, dt_raw, conv_w, p["conv_b"], dtb_row, alog_row, d_exp,
                                                ([w_gu_mine[half:], w_o_mine], False))
    w_o = g_o.reshape(D_MODEL, D_MODEL)
    w_d = g_d.reshape(D_FF, D_MODEL)
    x1 = _out_proj_fwd(x, ya, ys, z, p["attn_out_norm"], p["ssm_out_norm"], gate1, w_o, tm)
    dx1, h2, dgu, act, dmlp, acc2 = _mlp_loss(x1, tgt, p["norm2"], scale2, shift2, gate2, p["final_norm"],
                                              (g_gu_a, g_gu_b), w_d, tmm)
    g_w_gu = _wgrad(dgu, h2, 2 * D_FF // 4, tw, "wgrad_gate_up")
    g_w_d = _wgrad(act, dmlp, D_FF // 2, tw, "wgrad_down")
    dya, dys, dz, u, dmix, acc1 = _out_proj_bwd(dx1, ya, ys, z, p["attn_out_norm"], p["ssm_out_norm"], gate1, w_o, tm)
    g_w_o = _wgrad(u, dmix, D_MODEL, tw, "wgrad_out")
    (dq, dkv, dbias, dsk), (r_d,) = _attn_bwd(qkv, ya, dya, bias, sinks,
                                              ([g_w_d.reshape(N_DEV, D_FF // N_DEV, D_MODEL)], True))
    drel, dsink = _attn_finish(dbias, dsk, buckets)
    (dxbc, ddt, dcw, dvec, dd), (r_gu, r_o) = _ssd_bwd(
        xbc, dt_raw, prev_states, dys, conv_w, p["conv_b"], dtb_row, alog_row, d_exp,
        ([g_w_gu.reshape(N_DEV, 2 * D_FF // N_DEV, D_MODEL), g_w_o.reshape(N_DEV, D_MODEL // N_DEV, D_MODEL)], True))
    gx, h1, dproj, acc0 = _in_proj_bwd(x, dx1, dq, dkv, dz, dxbc, ddt, p["norm1"], scale1, shift1, w_in, tm)
    g_w_in, gathered = _wgrad(dproj, h1, IN_PAD, tw, "wgrad_in",
                              ([acc0, acc1, acc2, dcw, dvec, dd, dsink, drel, c], False))
    return gx, g_w_in, (r_o, r_gu, r_d), gathered


def kernel(x, c, ada_w, ada_b, norm1, w_in, conv_w, conv_b, dt_bias, A_log, D_skip, sinks, attn_out_norm, ssm_out_norm, w_o, norm2, w_gate_up, w_down, rel_bias, final_norm, loss_target, m_ada_w, m_ada_b, m_norm1, m_w_in, m_conv_w, m_conv_b, m_dt_bias, m_A_log, m_D_skip, m_sinks, m_attn_out_norm, m_ssm_out_norm, m_w_o, m_norm2, m_w_gate_up, m_w_down, m_rel_bias, m_final_norm, v_ada_w, v_ada_b, v_norm1, v_w_in, v_conv_w, v_conv_b, v_dt_bias, v_A_log, v_D_skip, v_sinks, v_attn_out_norm, v_ssm_out_norm, v_w_o, v_norm2, v_w_gate_up, v_w_down, v_rel_bias, v_final_norm):
    two_d = lambda a: a if a.ndim == 2 else a.reshape(-1, a.shape[-1])
    small_params = dict(
        ada_b=(ada_b, m_ada_b, v_ada_b), norm1=(norm1, m_norm1, v_norm1), conv_w=(conv_w, m_conv_w, v_conv_w),
        conv_b=(conv_b, m_conv_b, v_conv_b), dt_bias=(dt_bias, m_dt_bias, v_dt_bias), A_log=(A_log, m_A_log, v_A_log),
        D_skip=(D_skip, m_D_skip, v_D_skip), sinks=(sinks, m_sinks, v_sinks),
        attn_out_norm=(attn_out_norm, m_attn_out_norm, v_attn_out_norm),
        ssm_out_norm=(ssm_out_norm, m_ssm_out_norm, v_ssm_out_norm), norm2=(norm2, m_norm2, v_norm2),
        rel_bias=(rel_bias, m_rel_bias, v_rel_bias), final_norm=(final_norm, m_final_norm, v_final_norm))
    small_params = {k: tuple(two_d(a) for a in v) for k, v in small_params.items()}
    S = x.shape[1]
    xs, tgt = x.reshape(S, D_MODEL), loss_target.reshape(S, D_MODEL)
    ada_w2 = ada_w[0]
    chunk = ada_w2.shape[1]
    t_in = [jnp.transpose(a[0]) for a in (w_in, m_w_in, v_w_in)]
    t_gu = [jnp.transpose(a[0]) for a in (w_gate_up, m_w_gate_up, v_w_gate_up)]

    mod = _mod_exchange(c, ada_w2, ada_b.reshape(N_DEV, chunk)).reshape(6, D_MODEL)

    g_in, g_cw = _exchange([t_in[0].astype(WIRE_DTYPE), conv_w[0]], scatter=False, name="gather_w_in")
    w_in_full = jnp.pad(g_in.reshape(IN_W, D_MODEL), ((0, IN_PAD - IN_W), (0, 0)))
    conv_w_full = jnp.transpose(g_cw, (1, 0, 2)).reshape(4, XBC_W)

    p = {k: v[0] for k, v in small_params.items()}
    gx, gw_in, (r_o, r_gu, r_d), gathered = _local_step(
        xs, tgt, c, mod, w_in_full, conv_w_full, w_o[0].astype(WIRE_DTYPE), t_gu[0].astype(WIRE_DTYPE),
        w_down[0].astype(WIRE_DTYPE), p)

    (u_gu, u_d, u_o), (r_in,) = _reduce_adamw_hosting(
        [r_gu, r_d, r_o], [tuple(t_gu), (w_down[0], m_w_down[0], v_w_down[0]), (w_o[0], m_w_o[0], v_w_o[0])],
        "adamw_big", ([gw_in[:IN_W].reshape(N_DEV, IN_W // N_DEV, D_MODEL)], True))

    small, loss, c_all, dmod_all = _small_update(gathered, small_params)

    big = {
        "ada_w": _ada_w_update(c_all, dmod_all, ada_w2, m_ada_w[0], v_ada_w[0]),
        "w_in": [jnp.transpose(a) for a in _reduce_adamw(r_in, *t_in, "adamw_w_in")],
        "w_o": u_o,
        "w_gate_up": [jnp.transpose(a) for a in u_gu],
        "w_down": u_d,
    }
    big.update(small)

    order = ['ada_w', 'ada_b', 'norm1', 'w_in', 'conv_w', 'conv_b', 'dt_bias', 'A_log', 'D_skip', 'sinks',
             'attn_out_norm', 'ssm_out_norm', 'w_o', 'norm2', 'w_gate_up', 'w_down', 'rel_bias', 'final_norm']
    shapes = dict(ada_w=ada_w.shape, ada_b=ada_b.shape, norm1=norm1.shape, w_in=w_in.shape, conv_w=conv_w.shape,
                  conv_b=conv_b.shape, dt_bias=dt_bias.shape, A_log=A_log.shape, D_skip=D_skip.shape,
                  sinks=sinks.shape, attn_out_norm=attn_out_norm.shape, ssm_out_norm=ssm_out_norm.shape,
                  w_o=w_o.shape, norm2=norm2.shape, w_gate_up=w_gate_up.shape, w_down=w_down.shape,
                  rel_bias=rel_bias.shape, final_norm=final_norm.shape)
    outs = [[], [], [], []]
    for name in order:
        for kind in range(4):
            outs[kind].append(big[name][kind].reshape(shapes[name]))
    return (loss.reshape(()), gx.reshape(x.shape), *outs[0], *outs[1], *outs[2], *outs[3])
```

```python
import functools

import numpy as np
import jax
import jax.numpy as jnp
from jax import lax
from jax.experimental import pallas as pl
from jax.experimental.pallas import tpu as pltpu

F32 = jnp.float32
MXU_DTYPE = jnp.bfloat16
WIRE_DTYPE = jnp.bfloat16
HI = lax.Precision.HIGHEST
MESH = pl.DeviceIdType.MESH
N_DEV = 8

D_MODEL = 1024
ATTN_W = 512
KV_W = 128
SSM_W = 512
XBC_W = 1024
N_HEADS = 8
D_STATE = 128
D_FF = 2816
IN_W = 2312
IN_PAD = 2432
BLK = 128
N_BUCKETS = 32
EPS = 1e-6
LANE = 128
HALF = 64

ADAM_LR, ADAM_B1, ADAM_B2, ADAM_EPS, ADAM_WD, ADAM_STEP = 0.001, 0.9, 0.999, 1e-08, 0.01, 10

VMEM_BIG = 56 * 1024 * 1024


def _cparams(vmem=None):
    if vmem is None:
        return pltpu.CompilerParams()
    return pltpu.CompilerParams(vmem_limit_bytes=vmem)


def _mm(a, b):
    return jnp.dot(a.astype(MXU_DTYPE), b.astype(MXU_DTYPE), preferred_element_type=F32)


def _mm_nt(a, b):
    return lax.dot_general(a.astype(MXU_DTYPE), b.astype(MXU_DTYPE), (((1,), (1,)), ((), ())),
                           preferred_element_type=F32)


def _mm_tn(a, b):
    return lax.dot_general(a.astype(MXU_DTYPE), b.astype(MXU_DTYPE), (((0,), (0,)), ((), ())),
                           preferred_element_type=F32)


def _mm_hi(a, b):
    return jnp.dot(a, b, precision=HI, preferred_element_type=F32)


def _silu(x):
    return x * jax.nn.sigmoid(x)


def _softplus(x):
    return jnp.maximum(x, 0.0) + jnp.log1p(jnp.exp(-jnp.abs(x)))


def _rms(x, g, n):
    return x * lax.rsqrt(jnp.sum(x * x, axis=-1, keepdims=True) * (1.0 / n) + EPS) * g


def _modnorm(x, g, scale, shift):
    return _rms(x, g, x.shape[-1]) * (1.0 + scale) + shift


def _lane_iota(shape):
    return lax.broadcasted_iota(jnp.int32, shape, len(shape) - 1)


def _split_pair(t):
    lane = _lane_iota(t.shape)
    lo = jnp.where(lane < HALF, t, 0.0)
    hi = pltpu.roll(jnp.where(lane >= HALF, t, 0.0), HALF, 1)
    return lo, hi


def _join_pair(lo, hi):
    lane = _lane_iota(lo.shape)
    return jnp.where(lane < HALF, lo, pltpu.roll(hi, HALF, 1))


def _split_heads(t, n_pairs):
    out = []
    for p in range(n_pairs):
        out.extend(_split_pair(t[:, p * LANE:(p + 1) * LANE]))
    return out


def _join_heads(hs):
    return jnp.concatenate([_join_pair(hs[2 * p], hs[2 * p + 1]) for p in range(len(hs) // 2)], axis=1)


def _t5_bucket_table():
    dist = np.arange(BLK)[:, None] + BLK - np.arange(2 * BLK)[None, :]
    n = np.maximum(dist, 0)
    max_exact = N_BUCKETS // 2
    large = max_exact + (np.log(np.maximum(n, 1) / max_exact) / np.log(128 / max_exact)
                         * (N_BUCKETS - max_exact)).astype(np.int32)
    large = np.minimum(large, N_BUCKETS - 1)
    return np.where(n < max_exact, n, large).astype(np.int32)


def _my_pos():
    return lax.axis_index("x"), lax.axis_index("y"), lax.axis_index("c")


def _peer(k):
    x, y, c = _my_pos()
    return (1 - x if k & 4 else x, 1 - y if k & 2 else y, 1 - c if k & 1 else c)


def _lin(pos):
    return 4 * pos[0] + 2 * pos[1] + pos[2]


def _xchg_copies(ins, outs, sems, scatter):
    local_sem, send_sem, recv_sem = sems
    me = _lin(_my_pos())
    local, remote = [], []
    for a in range(len(ins)):
        src = ins[a].at[me] if scatter else ins[a]
        local.append(pltpu.make_async_copy(src, outs[a].at[me], local_sem.at[a]))
    for k in range(1, N_DEV):
        peer = _peer(k)
        for a in range(len(ins)):
            src = ins[a].at[_lin(peer)] if scatter else ins[a]
            remote.append(pltpu.make_async_remote_copy(src, outs[a].at[me], send_sem.at[a, k - 1],
                                                       recv_sem.at[a, k - 1], device_id=peer, device_id_type=MESH))
    return local, remote


def _xchg_start(ins, outs, sems, scatter):
    local, remote = _xchg_copies(ins, outs, sems, scatter)
    for cp in local + remote:
        cp.start()


def _xchg_wait(ins, outs, sems, scatter):
    local, remote = _xchg_copies(ins, outs, sems, scatter)
    for cp in local:
        cp.wait()
    for cp in remote:
        cp.wait_send()
        cp.wait_recv()


def _xchg_shapes(arrs, scatter):
    n = len(arrs)
    if scatter:
        out_shape = [jax.ShapeDtypeStruct(a.shape, a.dtype) for a in arrs]
    else:
        out_shape = [jax.ShapeDtypeStruct((N_DEV,) + a.shape, a.dtype) for a in arrs]
    sems = [pltpu.SemaphoreType.DMA((n,)), pltpu.SemaphoreType.DMA((n, N_DEV - 1)),
            pltpu.SemaphoreType.DMA((n, N_DEV - 1))]
    return out_shape, sems


def _exchange(arrs, scatter, name):
    n = len(arrs)
    out_shape, sems = _xchg_shapes(arrs, scatter)

    def body(*refs):
        ins, outs, s = refs[:n], refs[n:2 * n], refs[2 * n:]
        _xchg_start(ins, outs, s, scatter)
        _xchg_wait(ins, outs, s, scatter)

    hbm = pl.BlockSpec(memory_space=pltpu.HBM)
    return pl.pallas_call(body, name=name, out_shape=out_shape, in_specs=[hbm] * n, out_specs=[hbm] * n,
                          scratch_shapes=sems)(*arrs)


def _hosted_call(body, name, grid, in_specs, out_specs, out_shape, scratch_shapes, args, xchg, cparams):
    arrs, scatter = xchg
    grid = (grid,) if isinstance(grid, int) else tuple(grid)
    n, n_in, n_out, n_scr = len(arrs), len(in_specs), len(out_specs), len(scratch_shapes)
    x_shape, x_sems = _xchg_shapes(arrs, scatter)

    def hosted(*refs):
        ins, refs = refs[:n_in], refs[n_in:]
        x_in, refs = refs[:n], refs[n:]
        outs, refs = refs[:n_out], refs[n_out:]
        x_out, refs = refs[:n], refs[n:]
        scr, sems = refs[:n_scr], refs[n_scr:]
        step = pl.program_id(0)
        for d in range(1, len(grid)):
            step = step * grid[d] + pl.program_id(d)

        @pl.when(step == 0)
        def _():
            _xchg_start(x_in, x_out, sems, scatter)

        body(*ins, *outs, *scr)

        @pl.when(step == int(np.prod(grid)) - 1)
        def _():
            _xchg_wait(x_in, x_out, sems, scatter)

    hbm = pl.BlockSpec(memory_space=pltpu.HBM)
    res = pl.pallas_call(
        hosted, name=name, grid=grid, in_specs=list(in_specs) + [hbm] * n,
        out_specs=list(out_specs) + [hbm] * n, out_shape=list(out_shape) + x_shape,
        scratch_shapes=list(scratch_shapes) + x_sems, compiler_params=cparams,
    )(*args, *arrs)
    return res[:n_out], res[n_out:]


def _mod_exchange(c, ada_w, ada_b8):
    chunk = ada_w.shape[1]

    def body(c_ref, w_ref, b_ref, out_ref, cbuf, part, s1, r1, s2, r2):
        me = _lin(_my_pos())
        first = []
        for k in range(1, N_DEV):
            cp = pltpu.make_async_remote_copy(c_ref, cbuf.at[me], s1.at[k - 1], r1.at[k - 1],
                                              device_id=_peer(k), device_id_type=MESH)
            cp.start()
            first.append(cp)
        cbuf[me] = c_ref[...]
        for cp in first:
            cp.wait_send()
            cp.wait_recv()
        cond = _silu(jnp.concatenate([cbuf[i] for i in range(N_DEV)], axis=0))
        mod = _mm_hi(cond, w_ref[...]) + b_ref[pl.ds(me, 1), :]
        for j in range(N_DEV):
            part[j] = mod[j:j + 1, :]
        second = []
        for k in range(1, N_DEV):
            peer = _peer(k)
            cp = pltpu.make_async_remote_copy(part.at[_lin(peer)], out_ref.at[me], s2.at[k - 1], r2.at[k - 1],
                                              device_id=peer, device_id_type=MESH)
            cp.start()
            second.append(cp)
        out_ref[me] = part[me]
        for cp in second:
            cp.wait_send()
            cp.wait_recv()

    vm = pl.BlockSpec(memory_space=pltpu.VMEM)
    return pl.pallas_call(
        body, name="mod_exchange", out_shape=jax.ShapeDtypeStruct((N_DEV, 1, chunk), F32),
        in_specs=[vm, vm, vm], out_specs=vm,
        scratch_shapes=[pltpu.VMEM((N_DEV, 1, D_MODEL), F32), pltpu.VMEM((N_DEV, 1, chunk), F32)]
        + [pltpu.SemaphoreType.DMA((N_DEV - 1,))] * 4,
    )(c, ada_w, ada_b8)


def _row(i):
    return (i, 0)


def _fixed(i):
    return (0, 0)


def _in_proj_fwd(x, norm1, scale1, shift1, w_in, tm, xchg):
    S = x.shape[0]

    def body(x_ref, n_ref, sc_ref, sh_ref, w_ref, qkv_ref, z_ref, xbc_ref, dt_ref):
        h = _modnorm(x_ref[...], n_ref[...], sc_ref[...], sh_ref[...])
        p = _mm_nt(h, w_ref[...])
        qkv_ref[...] = p[:, :768].astype(qkv_ref.dtype)
        z_ref[...] = p[:, 768:1280]
        xbc_ref[...] = p[:, 1280:2304]
        dt_ref[...] = p[:, 2304:IN_PAD]

    vec = pl.BlockSpec((1, D_MODEL), _fixed)
    return _hosted_call(
        body, "in_proj_fwd", S // tm,
        in_specs=[pl.BlockSpec((tm, D_MODEL), _row), vec, vec, vec, pl.BlockSpec((IN_PAD, D_MODEL), _fixed)],
        out_specs=[pl.BlockSpec((tm, 768), _row), pl.BlockSpec((tm, SSM_W), _row),
                   pl.BlockSpec((tm, XBC_W), _row), pl.BlockSpec((tm, LANE), _row)],
        out_shape=[jax.ShapeDtypeStruct((S, 768), MXU_DTYPE), jax.ShapeDtypeStruct((S, SSM_W), F32),
                   jax.ShapeDtypeStruct((S, XBC_W), F32), jax.ShapeDtypeStruct((S, LANE), F32)],
        scratch_shapes=[], args=(x, norm1, scale1, shift1, w_in), xchg=xchg, cparams=_cparams(VMEM_BIG),
    )


def _in_proj_bwd(x, dx1, dq, dkv, dz, dxbc, ddt, norm1, scale1, shift1, w_in, tm):
    S = x.shape[0]

    def body(x_ref, dx1_ref, dq_ref, dkv_ref, dz_ref, dxbc_ref, ddt_ref, n_ref, sc_ref, sh_ref, w_ref,
             gx_ref, h_ref, dp_ref, acc_ref):
        @pl.when(pl.program_id(0) == 0)
        def _():
            acc_ref[...] = jnp.zeros_like(acc_ref)

        h, vjp = jax.vjp(_modnorm, x_ref[...], n_ref[...], sc_ref[...], sh_ref[...])
        dp = jnp.concatenate([dq_ref[...].astype(MXU_DTYPE), dkv_ref[...].astype(MXU_DTYPE),
                              dz_ref[...].astype(MXU_DTYPE), dxbc_ref[...].astype(MXU_DTYPE),
                              ddt_ref[...].astype(MXU_DTYPE)], axis=1)
        dh = _mm(dp, w_ref[...])
        dx, dn, dsc, dsh = vjp(dh)
        gx_ref[...] = dx1_ref[...] + dx
        h_ref[...] = h.astype(h_ref.dtype)
        dp_ref[...] = dp
        acc_ref[0:1, :] += dn
        acc_ref[1:2, :] += dsc
        acc_ref[2:3, :] += dsh

    vec = pl.BlockSpec((1, D_MODEL), _fixed)
    return pl.pallas_call(
        body, name="in_proj_bwd", grid=(S // tm,),
        in_specs=[pl.BlockSpec((tm, D_MODEL), _row), pl.BlockSpec((tm, D_MODEL), _row),
                  pl.BlockSpec((tm, ATTN_W), _row), pl.BlockSpec((tm, 2 * KV_W), _row),
                  pl.BlockSpec((tm, SSM_W), _row), pl.BlockSpec((tm, XBC_W), _row), pl.BlockSpec((tm, LANE), _row),
                  vec, vec, vec, pl.BlockSpec((IN_PAD, D_MODEL), _fixed)],
        out_specs=[pl.BlockSpec((tm, D_MODEL), _row), pl.BlockSpec((tm, D_MODEL), _row),
                   pl.BlockSpec((tm, IN_PAD), _row), pl.BlockSpec((8, D_MODEL), _fixed)],
        out_shape=[jax.ShapeDtypeStruct((S, D_MODEL), F32), jax.ShapeDtypeStruct((S, D_MODEL), MXU_DTYPE),
                   jax.ShapeDtypeStruct((S, IN_PAD), MXU_DTYPE), jax.ShapeDtypeStruct((8, D_MODEL), F32)],
        compiler_params=_cparams(VMEM_BIG),
    )(x, dx1, dq, dkv, dz, dxbc, ddt, norm1, scale1, shift1, w_in)


def _out_stage(ya, ys0, ys1, z0, z1, an, sn0, sn1):
    half = SSM_W // 2
    a = _rms(ya, an, ATTN_W)
    g0 = _rms(ys0 * _silu(z0), sn0, half)
    g1 = _rms(ys1 * _silu(z1), sn1, half)
    return jnp.concatenate([a, g0, g1], axis=1)


def _out_stage_args(ya_ref, ys_ref, z_ref, an_ref, sn_ref):
    half = SSM_W // 2
    return (ya_ref[...], ys_ref[:, :half], ys_ref[:, half:], z_ref[:, :half], z_ref[:, half:],
            an_ref[...], sn_ref[:, :half], sn_ref[:, half:])


def _out_proj_fwd(x, ya, ys, z, an, sn, gate1, w_o, tm):
    S = x.shape[0]

    def body(x_ref, ya_ref, ys_ref, z_ref, an_ref, sn_ref, g_ref, w_ref, x1_ref):
        u = _out_stage(*_out_stage_args(ya_ref, ys_ref, z_ref, an_ref, sn_ref))
        x1_ref[...] = x_ref[...] + g_ref[...] * _mm(u, w_ref[...])

    half = pl.BlockSpec((tm, ATTN_W), _row)
    hvec = pl.BlockSpec((1, ATTN_W), _fixed)
    return pl.pallas_call(
        body, name="out_proj_fwd", grid=(S // tm,),
        in_specs=[pl.BlockSpec((tm, D_MODEL), _row), half, half, half, hvec, hvec,
                  pl.BlockSpec((1, D_MODEL), _fixed), pl.BlockSpec((D_MODEL, D_MODEL), _fixed)],
        out_specs=pl.BlockSpec((tm, D_MODEL), _row),
        out_shape=jax.ShapeDtypeStruct((S, D_MODEL), F32),
        compiler_params=_cparams(VMEM_BIG),
    )(x, ya, ys, z, an, sn, gate1, w_o)


def _out_proj_bwd(dx1, ya, ys, z, an, sn, gate1, w_o, tm):
    S = dx1.shape[0]

    def body(dx1_ref, ya_ref, ys_ref, z_ref, an_ref, sn_ref, g_ref, w_ref,
             dya_ref, dys_ref, dz_ref, u_ref, dmix_ref, acc_ref):
        @pl.when(pl.program_id(0) == 0)
        def _():
            acc_ref[...] = jnp.zeros_like(acc_ref)

        u, vjp = jax.vjp(_out_stage, *_out_stage_args(ya_ref, ys_ref, z_ref, an_ref, sn_ref))
        dx1 = dx1_ref[...]
        mix = _mm(u, w_ref[...])
        dmix = dx1 * g_ref[...]
        du = _mm_nt(dmix, w_ref[...])
        dya, dys0, dys1, dz0, dz1, dan, dsn0, dsn1 = vjp(du)
        dya_ref[...] = dya
        dys_ref[...] = jnp.concatenate([dys0, dys1], axis=1)
        dz_ref[...] = jnp.concatenate([dz0, dz1], axis=1)
        u_ref[...] = u.astype(u_ref.dtype)
        dmix_ref[...] = dmix.astype(dmix_ref.dtype)
        acc_ref[0:1, :] += jnp.sum(dx1 * mix, axis=0, keepdims=True)
        acc_ref[1:2, :] += jnp.concatenate([dan, dsn0, dsn1], axis=1)

    half = pl.BlockSpec((tm, ATTN_W), _row)
    hvec = pl.BlockSpec((1, ATTN_W), _fixed)
    full = pl.BlockSpec((tm, D_MODEL), _row)
    return pl.pallas_call(
        body, name="out_proj_bwd", grid=(S // tm,),
        in_specs=[full, half, half, half, hvec, hvec,
                  pl.BlockSpec((1, D_MODEL), _fixed), pl.BlockSpec((D_MODEL, D_MODEL), _fixed)],
        out_specs=[half, half, half, full, full, pl.BlockSpec((8, D_MODEL), _fixed)],
        out_shape=[jax.ShapeDtypeStruct((S, ATTN_W), F32)] * 3
        + [jax.ShapeDtypeStruct((S, D_MODEL), MXU_DTYPE)] * 2 + [jax.ShapeDtypeStruct((8, D_MODEL), F32)],
        compiler_params=_cparams(VMEM_BIG),
    )(dx1, ya, ys, z, an, sn, gate1, w_o)


def _loss_rows(x2, fn, tgt):
    y = _rms(x2, fn, D_MODEL)
    per_row = jnp.sum(jnp.square(y - tgt), axis=1, keepdims=True)
    return jnp.sum(per_row, axis=0, keepdims=True) * (0.5 / D_MODEL)


def _mlp_loss(x1, tgt, norm2, scale2, shift2, gate2, fnorm, w_gu, w_d, tm):
    S = x1.shape[0]
    n_gu = 2 * D_FF // N_DEV
    half = n_gu // 2

    def body(x1_ref, t_ref, n_ref, sc_ref, sh_ref, g_ref, fn_ref, wgu_a, wgu_b, wd_hbm,
             dx1_ref, h_ref, dgu_ref, act_ref, dmlp_ref, acc_ref, wgu, wd, wsem):
        @pl.when(pl.program_id(0) == 0)
        def _():
            acc_ref[...] = jnp.zeros_like(acc_ref)
            copies = [pltpu.make_async_copy(wd_hbm, wd, wsem.at[2 * N_DEV])]
            for j in range(N_DEV):
                copies.append(pltpu.make_async_copy(wgu_a.at[j], wgu.at[pl.ds(j * n_gu, half)], wsem.at[2 * j]))
                copies.append(pltpu.make_async_copy(wgu_b.at[j], wgu.at[pl.ds(j * n_gu + half, half)],
                                                    wsem.at[2 * j + 1]))
            for cp in copies:
                cp.start()
            for cp in copies:
                cp.wait()

        x1 = x1_ref[...]
        gate2 = g_ref[...]
        h, vjp_h = jax.vjp(_modnorm, x1, n_ref[...], sc_ref[...], sh_ref[...])
        hb = h.astype(MXU_DTYPE)
        gu = _mm_nt(hb, wgu[...])
        g, u = gu[:, :D_FF], gu[:, D_FF:]
        sg = jax.nn.sigmoid(g)
        silu_g = g * sg
        act = (silu_g * u).astype(MXU_DTYPE)
        mlp = _mm(act, wd[...])
        x2 = x1 + gate2 * mlp
        loss, vjp_loss = jax.vjp(_loss_rows, x2, fn_ref[...], t_ref[...])
        dx2, dfn, _ = vjp_loss(jnp.ones((1, 1), F32))
        dmlp = (dx2 * gate2).astype(MXU_DTYPE)
        dact = _mm_nt(dmlp, wd[...])
        dg = dact * u * (sg * (1.0 + g * (1.0 - sg)))
        du = dact * silu_g
        dgu = jnp.concatenate([dg, du], axis=1).astype(MXU_DTYPE)
        dh = _mm(dgu, wgu[...])
        dx, dn, dsc, dsh = vjp_h(dh)
        dx1_ref[...] = dx2 + dx
        h_ref[...] = hb
        dgu_ref[...] = dgu
        act_ref[...] = act
        dmlp_ref[...] = dmlp
        acc_ref[0:1, :] += dn
        acc_ref[1:2, :] += dsc
        acc_ref[2:3, :] += dsh
        acc_ref[3:4, :] += jnp.sum(dx2 * mlp, axis=0, keepdims=True)
        acc_ref[4:5, :] += dfn
        acc_ref[5:6, :] += jnp.broadcast_to(loss, (1, D_MODEL))

    full = pl.BlockSpec((tm, D_MODEL), _row)
    vec = pl.BlockSpec((1, D_MODEL), _fixed)
    anyspec = pl.BlockSpec(memory_space=pl.ANY)
    return pl.pallas_call(
        body, name="mlp_loss", grid=(S // tm,),
        in_specs=[full, full, vec, vec, vec, vec, vec, anyspec, anyspec, anyspec],
        out_specs=[full, full, pl.BlockSpec((tm, 2 * D_FF), _row), pl.BlockSpec((tm, D_FF), _row), full,
                   pl.BlockSpec((8, D_MODEL), _fixed)],
        out_shape=[jax.ShapeDtypeStruct((S, D_MODEL), F32), jax.ShapeDtypeStruct((S, D_MODEL), MXU_DTYPE),
                   jax.ShapeDtypeStruct((S, 2 * D_FF), MXU_DTYPE), jax.ShapeDtypeStruct((S, D_FF), MXU_DTYPE),
                   jax.ShapeDtypeStruct((S, D_MODEL), MXU_DTYPE), jax.ShapeDtypeStruct((8, D_MODEL), F32)],
        scratch_shapes=[pltpu.VMEM((2 * D_FF, D_MODEL), MXU_DTYPE), pltpu.VMEM((D_FF, D_MODEL), MXU_DTYPE),
                        pltpu.SemaphoreType.DMA((2 * N_DEV + 1,))],
        compiler_params=_cparams(VMEM_BIG),
    )(x1, tgt, norm2, scale2, shift2, gate2, fnorm, w_gu[0], w_gu[1], w_d)


def _wgrad(a, g, tk, ts, name, xchg=None):
    S, K = a.shape
    N = g.shape[1]
    ns = S // ts

    def body(a_ref, g_ref, o_ref, acc_ref):
        s = pl.program_id(1)

        @pl.when(s == 0)
        def _():
            acc_ref[...] = jnp.zeros_like(acc_ref)

        acc_ref[...] += _mm_tn(a_ref[...], g_ref[...])

        @pl.when(s == ns - 1)
        def _():
            o_ref[...] = acc_ref[...].astype(o_ref.dtype)

    in_specs = [pl.BlockSpec((ts, tk), lambda j, s: (s, j)), pl.BlockSpec((ts, N), lambda j, s: (s, 0))]
    out_spec = pl.BlockSpec((tk, N), lambda j, s: (j, 0))
    out_shape = jax.ShapeDtypeStruct((K, N), WIRE_DTYPE)
    scratch = [pltpu.VMEM((tk, N), F32)]
    if xchg is None:
        return pl.pallas_call(body, name=name, grid=(K // tk, ns), in_specs=in_specs, out_specs=out_spec,
                              out_shape=out_shape, scratch_shapes=scratch, compiler_params=_cparams(VMEM_BIG))(a, g)
    (out,), x_out = _hosted_call(body, name, (K // tk, ns), in_specs, [out_spec], [out_shape], scratch, (a, g), xchg,
                                 _cparams(VMEM_BIG))
    return out, x_out


MASKED = -1e30
QK_SCALE = HALF ** -0.5


def _attn_bias(buckets, rel_bias):
    def body(bk_ref, relb_ref, out_ref):
        bk = bk_ref[...]
        i = lax.broadcasted_iota(jnp.int32, (BLK, 2 * BLK), 0)
        j = lax.broadcasted_iota(jnp.int32, (BLK, 2 * BLK), 1)
        window = (j > i) & (j <= i + BLK)
        for h in range(N_HEADS):
            acc = jnp.zeros((BLK, 2 * BLK), F32)
            for b in range(N_BUCKETS):
                acc = jnp.where(bk == b, relb_ref[b, h], acc)
            out_ref[0, h] = jnp.where(window, acc, MASKED)
            out_ref[1, h] = jnp.where(window & (j >= BLK), acc, MASKED)

    return pl.pallas_call(
        body, name="attn_bias", out_shape=jax.ShapeDtypeStruct((2, N_HEADS, BLK, 2 * BLK), F32),
        in_specs=[pl.BlockSpec(memory_space=pltpu.VMEM), pl.BlockSpec(memory_space=pltpu.SMEM)],
    )(buckets, rel_bias)


def _attn_kv(kvp_ref, kvc_ref):
    kvp = kvp_ref[...].astype(F32)
    kvc = kvc_ref[...].astype(F32)
    kp, kc = _split_pair(kvp[:, :LANE]), _split_pair(kvc[:, :LANE])
    vp, vc = _split_pair(kvp[:, LANE:]), _split_pair(kvc[:, LANE:])
    k_pads = [jnp.concatenate([kp[g], kc[g]], axis=0).astype(MXU_DTYPE) for g in range(2)]
    v_pads = [jnp.concatenate([vp[g], vc[g]], axis=0).astype(MXU_DTYPE) for g in range(2)]
    return k_pads, v_pads


def _attn_fwd(qkv, bias, sinks, xchg):
    S = qkv.shape[0]
    nb = S // BLK

    def body(q_ref, kvp_ref, kvc_ref, bias_ref, sinks_ref, y_ref):
        first = jnp.where(pl.program_id(0) == 0, 1, 0)
        q_heads = _split_heads(q_ref[...].astype(F32) * QK_SCALE, 4)
        k_pads, v_pads = _attn_kv(kvp_ref, kvc_ref)
        heads = range(N_HEADS)
        s = [_mm_nt(q_heads[h].astype(MXU_DTYPE), k_pads[h // 4]) + bias_ref[first, h] for h in heads]
        m = [jnp.maximum(jnp.max(s[h], axis=-1, keepdims=True), sinks_ref[h]) for h in heads]
        p = [jnp.exp(s[h] - m[h]) for h in heads]
        rinv = [1.0 / (jnp.sum(p[h], axis=-1, keepdims=True) + jnp.exp(sinks_ref[h] - m[h])) for h in heads]
        y_ref[...] = _join_heads([_mm(p[h], v_pads[h // 4]) * rinv[h] for h in heads])

    smem = pl.BlockSpec(memory_space=pltpu.SMEM)
    return _hosted_call(
        body, "attn_fwd", nb,
        in_specs=[pl.BlockSpec((BLK, ATTN_W), _row),
                  pl.BlockSpec((BLK, 2 * KV_W), lambda i: (jnp.maximum(i - 1, 0), 2)),
                  pl.BlockSpec((BLK, 2 * KV_W), lambda i: (i, 2)),
                  pl.BlockSpec((2, N_HEADS, BLK, 2 * BLK), lambda i: (0, 0, 0, 0)), smem],
        out_specs=[pl.BlockSpec((BLK, ATTN_W), _row)],
        out_shape=[jax.ShapeDtypeStruct((S, ATTN_W), F32)],
        scratch_shapes=[],
        args=(qkv, qkv, qkv, bias, sinks), xchg=xchg, cparams=_cparams(),
    )


def _attn_bwd(qkv, y, dy, bias, sinks, xchg):
    S = qkv.shape[0]
    nb = S // BLK

    def body(q_ref, kvp_ref, kvc_ref, y_ref, dy_ref, bias_ref, sinks_ref, dq_ref, dkv_ref, dbias_ref, dsk_ref, carry_ref):
        i = pl.program_id(0)

        @pl.when(i == 0)
        def _():
            dbias_ref[...] = jnp.zeros_like(dbias_ref)
            dsk_ref[...] = jnp.zeros_like(dsk_ref)
            carry_ref[...] = jnp.zeros_like(carry_ref)

        first = jnp.where(i == nb - 1, 1, 0)
        q_heads = _split_heads(q_ref[...].astype(F32) * QK_SCALE, 4)
        k_pads, v_pads = _attn_kv(kvp_ref, kvc_ref)
        y_heads = _split_heads(y_ref[...], 4)
        dy_heads = _split_heads(dy_ref[...], 4)
        heads = range(N_HEADS)
        qs = [q_heads[h].astype(MXU_DTYPE) for h in heads]
        s = [_mm_nt(qs[h], k_pads[h // 4]) + bias_ref[first, h] for h in heads]
        m = [jnp.maximum(jnp.max(s[h], axis=-1, keepdims=True), sinks_ref[h]) for h in heads]
        p = [jnp.exp(s[h] - m[h]) for h in heads]
        esink = [jnp.exp(sinks_ref[h] - m[h]) for h in heads]
        rinv = [1.0 / (jnp.sum(p[h], axis=-1, keepdims=True) + esink[h]) for h in heads]
        t = [dy_heads[h] * rinv[h] for h in heads]
        delta = [jnp.sum(t[h] * y_heads[h], axis=-1, keepdims=True) for h in heads]
        tb = [t[h].astype(MXU_DTYPE) for h in heads]
        dp = [_mm_nt(tb[h], v_pads[h // 4]) for h in heads]
        ds = [p[h] * (dp[h] - delta[h]) for h in heads]
        for h in heads:
            dbias_ref[h] += ds[h]
            dsk_ref[h] -= esink[h] * delta[h]
        dsb = [ds[h].astype(MXU_DTYPE) for h in heads]
        pb = [p[h].astype(MXU_DTYPE) for h in heads]
        dq_heads = [_mm(dsb[h], k_pads[h // 4]) * QK_SCALE for h in heads]
        dk_pads = [_mm_tn(jnp.concatenate(dsb[4 * g:4 * g + 4], axis=0), jnp.concatenate(qs[4 * g:4 * g + 4], axis=0))
                   for g in range(2)]
        dv_pads = [_mm_tn(jnp.concatenate(pb[4 * g:4 * g + 4], axis=0), jnp.concatenate(tb[4 * g:4 * g + 4], axis=0))
                   for g in range(2)]
        dq_ref[...] = _join_heads(dq_heads)
        dk_prev = _join_pair(dk_pads[0][:BLK], dk_pads[1][:BLK])
        dk_cur = _join_pair(dk_pads[0][BLK:], dk_pads[1][BLK:])
        dv_prev = _join_pair(dv_pads[0][:BLK], dv_pads[1][:BLK])
        dv_cur = _join_pair(dv_pads[0][BLK:], dv_pads[1][BLK:])
        dkv_ref[...] = jnp.concatenate([dk_cur, dv_cur], axis=1) + carry_ref[...]
        carry_ref[...] = jnp.concatenate([dk_prev, dv_prev], axis=1)

    smem = pl.BlockSpec(memory_space=pltpu.SMEM)
    rev = lambda i: (nb - 1 - i, 0)
    return _hosted_call(
        body, "attn_bwd", nb,
        in_specs=[pl.BlockSpec((BLK, ATTN_W), rev),
                  pl.BlockSpec((BLK, 2 * KV_W), lambda i: (jnp.maximum(nb - 2 - i, 0), 2)),
                  pl.BlockSpec((BLK, 2 * KV_W), lambda i: (nb - 1 - i, 2)),
                  pl.BlockSpec((BLK, ATTN_W), rev), pl.BlockSpec((BLK, ATTN_W), rev),
                  pl.BlockSpec((2, N_HEADS, BLK, 2 * BLK), lambda i: (0, 0, 0, 0)), smem],
        out_specs=[pl.BlockSpec((BLK, ATTN_W), rev), pl.BlockSpec((BLK, 2 * KV_W), rev),
                   pl.BlockSpec((N_HEADS, BLK, 2 * BLK), lambda i: (0, 0, 0)),
                   pl.BlockSpec((N_HEADS, BLK, 1), lambda i: (0, 0, 0))],
        out_shape=[jax.ShapeDtypeStruct((S, ATTN_W), F32), jax.ShapeDtypeStruct((S, 2 * KV_W), F32),
                   jax.ShapeDtypeStruct((N_HEADS, BLK, 2 * BLK), F32), jax.ShapeDtypeStruct((N_HEADS, BLK, 1), F32)],
        scratch_shapes=[pltpu.VMEM((BLK, 2 * KV_W), F32)],
        args=(qkv, qkv, qkv, y, dy, bias, sinks), xchg=xchg, cparams=_cparams(),
    )


def _attn_finish(dbias, dsk, buckets):
    def body(db_ref, dsk_ref, bk_ref, drel_ref, dsink_ref):
        bk = bk_ref[...]
        r = lax.broadcasted_iota(jnp.int32, (N_BUCKETS, LANE), 0)
        l = lax.broadcasted_iota(jnp.int32, (N_BUCKETS, LANE), 1)
        row = lax.broadcasted_iota(jnp.int32, (N_HEADS, LANE), 0)
        res = jnp.zeros((N_BUCKETS, LANE), F32)
        dsink = jnp.zeros((N_HEADS, LANE), F32)
        for h in range(N_HEADS):
            db = db_ref[h]
            for b in range(N_BUCKETS):
                v = jnp.sum(jnp.sum(jnp.where(bk == b, db, 0.0), axis=1, keepdims=True), axis=0, keepdims=True)
                res = res + jnp.where((r == b) & (l == h), v, 0.0)
            dsink = dsink + jnp.where(row == h, jnp.sum(dsk_ref[h], axis=0, keepdims=True), 0.0)
        drel_ref[...] = res
        dsink_ref[...] = dsink

    return pl.pallas_call(body, name="attn_finish",
                          out_shape=[jax.ShapeDtypeStruct((N_BUCKETS, LANE), F32),
                                     jax.ShapeDtypeStruct((N_HEADS, LANE), F32)])(dbias, dsk, buckets)


def _ssd_consts():
    r = lax.broadcasted_iota(jnp.int32, (BLK, BLK), 0)
    c = lax.broadcasted_iota(jnp.int32, (BLK, BLK), 1)
    causal = c <= r
    upper = (r <= c).astype(F32)
    last = r == BLK - 1
    head = lax.broadcasted_iota(jnp.int32, (N_HEADS, BLK), 0)
    return causal, upper, last, head


def _ssd_chunk(xs, bg, cg, dt_raw_t, prev, dtb, alog, d_rows, consts):
    causal, upper, last, head = consts
    dt_t = _softplus(dt_raw_t + dtb)
    acs_t = _mm_hi(dt_t * (-jnp.exp(alog)), upper)
    cb = [_mm_nt(cg[g], bg[g]) for g in range(2)]
    heads = range(N_HEADS)
    dt_row = [jnp.sum(jnp.where(head == h, dt_t, 0.0), axis=0, keepdims=True) for h in heads]
    a_row = [jnp.sum(jnp.where(head == h, acs_t, 0.0), axis=0, keepdims=True) for h in heads]
    a_rb = [jnp.broadcast_to(a_row[h], (BLK, BLK)) for h in heads]
    a_b = [a_rb[h].T for h in heads]
    a_last = [jnp.sum(jnp.where(last, a_b[h], 0.0), axis=0, keepdims=True) for h in heads]
    w = [cb[h // 4] * jnp.exp(jnp.where(causal, a_b[h] - a_rb[h], -1e30)) * dt_row[h] for h in heads]
    f_b = [jnp.broadcast_to(dt_row[h] * jnp.exp(a_last[h] - a_row[h]), (BLK, BLK)).T for h in heads]
    y_in = [_mm(w[h], xs[h]) for h in heads]
    y_off = [_mm(cg[h // 4], prev[h]) * jnp.exp(a_b[h]) for h in heads]
    st = [_mm_tn(bg[h // 4], xs[h] * f_b[h]) for h in heads]
    ys = [y_in[h] + y_off[h] + d_rows[h] * xs[h] for h in heads]
    hs = [prev[h] * jnp.exp(a_last[h]) + st[h] for h in heads]
    return tuple(ys), tuple(hs)


def _ssd_chunk_bwd(xs, bg, cg, dt_raw_t, prev, dtb, alog, d_rows, dys, dhs, consts):
    causal, upper, last, head = consts
    heads, groups = range(N_HEADS), range(2)
    lane = _lane_iota((BLK, BLK))
    lane_row = _lane_iota((1, BLK))
    pre_dt = dt_raw_t + dtb
    dt_t = _softplus(pre_dt)
    a_neg = -jnp.exp(alog)
    acs_t = _mm_hi(dt_t * a_neg, upper)
    pick = lambda t, h: jnp.sum(jnp.where(head == h, t, 0.0), axis=0, keepdims=True)
    full_sum = lambda t: jnp.sum(jnp.sum(t, axis=1, keepdims=True), axis=0, keepdims=True)
    dt_row = [pick(dt_t, h) for h in heads]
    a_row = [pick(acs_t, h) for h in heads]
    a_rb = [jnp.broadcast_to(a_row[h], (BLK, BLK)) for h in heads]
    a_b = [a_rb[h].T for h in heads]
    a_last = [jnp.sum(jnp.where(last, a_b[h], 0.0), axis=0, keepdims=True) for h in heads]
    lm = [jnp.exp(jnp.where(causal, a_b[h] - a_rb[h], -1e30)) for h in heads]
    cgb = [cg[g].astype(MXU_DTYPE) for g in groups]
    bgb = [bg[g].astype(MXU_DTYPE) for g in groups]
    cb = [_mm_nt(cgb[g], bgb[g]) for g in groups]
    u = [cb[h // 4] * lm[h] for h in heads]
    w = [(u[h] * dt_row[h]).astype(MXU_DTYPE) for h in heads]
    e_row = [jnp.exp(a_last[h] - a_row[h]) for h in heads]
    f_row = [dt_row[h] * e_row[h] for h in heads]
    f_b = [jnp.broadcast_to(f_row[h], (BLK, BLK)).T for h in heads]
    e_b = [jnp.exp(a_b[h]) for h in heads]
    el = [jnp.exp(a_last[h]) for h in heads]
    xb = [xs[h].astype(MXU_DTYPE) for h in heads]
    dyb = [dys[h].astype(MXU_DTYPE) for h in heads]
    prevb = [prev[h].astype(MXU_DTYPE) for h in heads]
    dstb = [dhs[h].astype(MXU_DTYPE) for h in heads]
    gmat = [_mm(cgb[h // 4], prevb[h]) for h in heads]
    dw = [_mm_nt(dyb[h], xb[h]) for h in heads]
    dg = [dys[h] * e_b[h] for h in heads]
    dgb = [dg[h].astype(MXU_DTYPE) for h in heads]
    dxf = [_mm(bgb[h // 4], dstb[h]) for h in heads]
    xfb = [(xs[h] * f_b[h]).astype(MXU_DTYPE) for h in heads]
    dxs = [_mm_tn(w[h], dyb[h]) + d_rows[h] * dys[h] + f_b[h] * dxf[h] for h in heads]
    dd_rows = [jnp.sum(dys[h] * xs[h], axis=0, keepdims=True) for h in heads]
    dprev = [_mm_tn(cgb[h // 4], dgb[h]) + dhs[h] * el[h] for h in heads]
    dcg_h = [_mm_nt(dgb[h], prevb[h]) for h in heads]
    dbg_h = [_mm_nt(xfb[h], dstb[h]) for h in heads]
    zt = [dw[h] * u[h] for h in heads]
    dseg = [zt[h] * dt_row[h] for h in heads]
    dcb_h = [dw[h] * lm[h] * dt_row[h] for h in heads]
    dcb = [(dcb_h[4 * g] + dcb_h[4 * g + 1] + dcb_h[4 * g + 2] + dcb_h[4 * g + 3]).astype(MXU_DTYPE) for g in groups]
    dcg = [dcg_h[4 * g] + dcg_h[4 * g + 1] + dcg_h[4 * g + 2] + dcg_h[4 * g + 3] + _mm(dcb[g], bgb[g]) for g in groups]
    dbg = [dbg_h[4 * g] + dbg_h[4 * g + 1] + dbg_h[4 * g + 2] + dbg_h[4 * g + 3] + _mm_tn(dcb[g], cgb[g])
           for g in groups]
    r1 = [jnp.sum(dg[h] * gmat[h] + dseg[h], axis=1, keepdims=True) for h in heads]
    r2 = [jnp.sum(dxf[h] * xs[h], axis=1, keepdims=True) for h in heads]
    tt = [jnp.where(lane < HALF, jnp.broadcast_to(r1[h], (BLK, BLK)), jnp.broadcast_to(r2[h], (BLK, BLK))).T
          for h in heads]
    r1_row = [tt[h][0:1, :] for h in heads]
    r2_row = [tt[h][HALF:HALF + 1, :] for h in heads]
    d_el = [full_sum(dhs[h] * prev[h]) for h in heads]
    da_last = [jnp.sum(r2_row[h] * f_row[h], axis=1, keepdims=True) + el[h] * d_el[h] for h in heads]
    da_row = [r1_row[h] - jnp.sum(dseg[h], axis=0, keepdims=True) - r2_row[h] * f_row[h]
              + jnp.where(lane_row == BLK - 1, da_last[h], 0.0) for h in heads]
    ddt_row = [jnp.sum(zt[h], axis=0, keepdims=True) + r2_row[h] * e_row[h] for h in heads]
    da_t = jnp.zeros((N_HEADS, BLK), F32)
    ddt_t = jnp.zeros((N_HEADS, BLK), F32)
    for h in heads:
        da_t = jnp.where(head == h, da_row[h], da_t)
        ddt_t = jnp.where(head == h, ddt_row[h], ddt_t)
    d_dta = _mm_hi(da_t, causal.astype(F32))
    dalog = d_dta * dt_t * a_neg
    draw = (ddt_t + d_dta * a_neg) * jax.nn.sigmoid(pre_dt)
    return dxs, dbg, dcg, draw, dprev, draw, dalog, dd_rows


def _dt_rows(dt_blk):
    return dt_blk.T[:N_HEADS]


def _silu_grad(x):
    s = jax.nn.sigmoid(x)
    return s * (1.0 + x * (1.0 - s))


def _conv_pre(halo, blk, cw_ref, cb_ref):
    ext = jnp.concatenate([halo, blk], axis=0)
    taps = [pltpu.roll(ext, 3 - k, 0)[8:] for k in range(3)] + [blk]
    pre = cb_ref[...] + cw_ref[0:1, :] * taps[0]
    for k in range(1, 4):
        pre = pre + cw_ref[k:k + 1, :] * taps[k]
    return pre, taps


def _ssd_split(pre):
    heads = _split_heads(pre[:, :SSM_W], 4)
    pb = [pre[:, SSM_W + g * D_STATE:SSM_W + (g + 1) * D_STATE] for g in range(2)]
    pc = [pre[:, SSM_W + 2 * D_STATE + g * D_STATE:SSM_W + 2 * D_STATE + (g + 1) * D_STATE] for g in range(2)]
    return heads, pb, pc


def _ssd_fwd(xbc, dt_raw, conv_w, conv_b, dtb_row, alog_row, d_exp, xchg):
    S = xbc.shape[0]
    nc = S // BLK

    def body(xbc_ref, halo_ref, dt_ref, cw_ref, cb_ref, dtb_ref, alog_ref, d_ref, y_ref, prev_ref, state_ref):
        i = pl.program_id(0)

        @pl.when(i == 0)
        def _():
            state_ref[...] = jnp.zeros_like(state_ref)

        halo = halo_ref[...] * jnp.where(i > 0, 1.0, 0.0)
        pre, _ = _conv_pre(halo, xbc_ref[...], cw_ref, cb_ref)
        heads, pb, pc = _ssd_split(_silu(pre))
        prev = [state_ref[h] for h in range(N_HEADS)]
        for h in range(N_HEADS):
            prev_ref[0, h] = prev[h]
        d_rows = [d_ref[h:h + 1, :] for h in range(N_HEADS)]
        ys, hs = _ssd_chunk(heads, pb, pc, _dt_rows(dt_ref[...]), prev, dtb_ref[...], alog_ref[...], d_rows,
                            _ssd_consts())
        for h in range(N_HEADS):
            state_ref[h] = hs[h]
        y_ref[...] = _join_heads(ys)

    vec = pl.BlockSpec((N_HEADS, LANE), _fixed)
    return _hosted_call(
        body, "ssd_fwd", nc,
        in_specs=[pl.BlockSpec((BLK, XBC_W), _row),
                  pl.BlockSpec((8, XBC_W), lambda i: (jnp.maximum(i * (BLK // 8) - 1, 0), 0)),
                  pl.BlockSpec((BLK, LANE), _row),
                  pl.BlockSpec((4, XBC_W), _fixed), pl.BlockSpec((1, XBC_W), _fixed), vec, vec,
                  pl.BlockSpec((N_HEADS, LANE), _fixed)],
        out_specs=[pl.BlockSpec((BLK, SSM_W), _row),
                   pl.BlockSpec((1, N_HEADS, D_STATE, LANE), lambda i: (i, 0, 0, 0))],
        out_shape=[jax.ShapeDtypeStruct((S, SSM_W), F32), jax.ShapeDtypeStruct((nc, N_HEADS, D_STATE, LANE), F32)],
        scratch_shapes=[pltpu.VMEM((N_HEADS, D_STATE, LANE), F32)],
        args=(xbc, xbc, dt_raw, conv_w, conv_b, dtb_row, alog_row, d_exp), xchg=xchg, cparams=_cparams(),
    )


def _ssd_bwd(xbc, dt_raw, prev_states, dy, conv_w, conv_b, dtb_row, alog_row, d_exp, xchg):
    S = xbc.shape[0]
    nc = S // BLK

    def body(xbc_ref, halo_ref, dt_ref, prev_ref, dy_ref, cw_ref, cb_ref, dtb_ref, alog_ref, d_ref,
             dxbc_ref, ddt_ref, dcw_ref, dvec_ref, dd_ref, gstate_ref, ghalo_ref):
        i = pl.program_id(0)
        c = nc - 1 - i

        @pl.when(i == 0)
        def _():
            gstate_ref[...] = jnp.zeros_like(gstate_ref)
            ghalo_ref[...] = jnp.zeros_like(ghalo_ref)
            dcw_ref[...] = jnp.zeros_like(dcw_ref)
            dvec_ref[...] = jnp.zeros_like(dvec_ref)
            dd_ref[...] = jnp.zeros_like(dd_ref)

        halo = halo_ref[...] * jnp.where(c > 0, 1.0, 0.0)
        pre, taps = _conv_pre(halo, xbc_ref[...], cw_ref, cb_ref)
        heads, pb, pc = _ssd_split(_silu(pre))
        prev = [prev_ref[0, h] for h in range(N_HEADS)]
        d_rows = [d_ref[h:h + 1, :] for h in range(N_HEADS)]
        dys = _split_heads(dy_ref[...], 4)
        dhs = [gstate_ref[h] for h in range(N_HEADS)]
        dheads, dpb, dpc, ddt_t, dprev, ddtb, dalog, dd_rows = _ssd_chunk_bwd(
            heads, pb, pc, _dt_rows(dt_ref[...]), prev, dtb_ref[...], alog_ref[...], d_rows, dys, dhs, _ssd_consts())
        for h in range(N_HEADS):
            gstate_ref[h] = dprev[h]
            dd_ref[h:h + 1, :] += dd_rows[h]
        ddt_ref[...] = jnp.concatenate([ddt_t, jnp.zeros((BLK - N_HEADS, BLK), F32)], axis=0).T
        dvec_ref[0:N_HEADS, :] += ddtb
        dvec_ref[N_HEADS:, :] += dalog
        dpre = jnp.concatenate([_join_heads(dheads)] + list(dpb) + list(dpc), axis=1) * _silu_grad(pre)
        zeros8 = jnp.zeros((8, XBC_W), F32)
        dpe = jnp.concatenate([zeros8, dpre, zeros8], axis=0)
        n_ext = 16 + BLK
        dext = cw_ref[3:4, :] * dpe[:8 + BLK]
        dcw_ref[3:4, :] += jnp.sum(dpre * taps[3], axis=0, keepdims=True)
        for k in range(3):
            dext = dext + cw_ref[k:k + 1, :] * pltpu.roll(dpe, n_ext - (3 - k), 0)[:8 + BLK]
            dcw_ref[k:k + 1, :] += jnp.sum(dpre * taps[k], axis=0, keepdims=True)
        dcw_ref[4:5, :] += jnp.sum(dpre, axis=0, keepdims=True)
        dxbc_ref[...] = dext[8:, :]
        dxbc_ref[BLK - 8:BLK, :] += ghalo_ref[...]
        ghalo_ref[...] = dext[:8, :]

    vec = pl.BlockSpec((N_HEADS, LANE), _fixed)
    rev = lambda i: (nc - 1 - i, 0)
    return _hosted_call(
        body, "ssd_bwd", nc,
        in_specs=[pl.BlockSpec((BLK, XBC_W), rev),
                  pl.BlockSpec((8, XBC_W), lambda i: (jnp.maximum((nc - 1 - i) * (BLK // 8) - 1, 0), 0)),
                  pl.BlockSpec((BLK, LANE), rev),
                  pl.BlockSpec((1, N_HEADS, D_STATE, LANE), lambda i: (nc - 1 - i, 0, 0, 0)),
                  pl.BlockSpec((BLK, SSM_W), rev),
                  pl.BlockSpec((4, XBC_W), _fixed), pl.BlockSpec((1, XBC_W), _fixed), vec, vec,
                  pl.BlockSpec((N_HEADS, LANE), _fixed)],
        out_specs=[pl.BlockSpec((BLK, XBC_W), rev), pl.BlockSpec((BLK, LANE), rev),
                   pl.BlockSpec((8, XBC_W), _fixed), pl.BlockSpec((2 * N_HEADS, LANE), _fixed),
                   pl.BlockSpec((N_HEADS, LANE), _fixed)],
        out_shape=[jax.ShapeDtypeStruct((S, XBC_W), F32), jax.ShapeDtypeStruct((S, LANE), F32),
                   jax.ShapeDtypeStruct((8, XBC_W), F32), jax.ShapeDtypeStruct((2 * N_HEADS, LANE), F32),
                   jax.ShapeDtypeStruct((N_HEADS, LANE), F32)],
        scratch_shapes=[pltpu.VMEM((N_HEADS, D_STATE, LANE), F32), pltpu.VMEM((8, XBC_W), F32)],
        args=(xbc, xbc, dt_raw, prev_states, dy, conv_w, conv_b, dtb_row, alog_row, d_exp), xchg=xchg,
        cparams=_cparams(VMEM_BIG),
    )


def _adamw_math(w, g, m, v):
    m = ADAM_B1 * m + (1.0 - ADAM_B1) * g
    v = ADAM_B2 * v + (1.0 - ADAM_B2) * jnp.square(g)
    m_hat = m / (1.0 - ADAM_B1 ** ADAM_STEP)
    v_hat = v / (1.0 - ADAM_B2 ** ADAM_STEP)
    delta = -ADAM_LR * (m_hat / (jnp.sqrt(v_hat) + ADAM_EPS) + ADAM_WD * w)
    return delta, m, v


def _reduce_adamw(parts, w, m, v, name):
    R, C = w.shape

    def body(p_ref, w_ref, m_ref, v_ref, g_ref, d_ref, nm_ref, nv_ref):
        g = p_ref[0].astype(F32)
        for i in range(1, N_DEV):
            g = g + p_ref[i].astype(F32)
        d, nm, nv = _adamw_math(w_ref[...], g, m_ref[...], v_ref[...])
        g_ref[...] = g
        d_ref[...] = d
        nm_ref[...] = nm
        nv_ref[...] = nv

    if R % 16 == 0:
        tr = max(t for t in range(16, 257, 16) if R % t == 0)
        n, blk, pblk = R // tr, pl.BlockSpec((tr, C), _row), pl.BlockSpec((N_DEV, tr, C), lambda i: (0, i, 0))
    else:
        tl = 256
        n, blk, pblk = C // tl, pl.BlockSpec((R, tl), lambda i: (0, i)), pl.BlockSpec((N_DEV, R, tl),
                                                                                      lambda i: (0, 0, i))
    return pl.pallas_call(
        body, name=name, grid=(n,), in_specs=[pblk, blk, blk, blk],
        out_specs=[blk] * 4, out_shape=[jax.ShapeDtypeStruct((R, C), F32)] * 4,
    )(parts, w, m, v)


def _reduce_adamw_hosting(parts_list, wmv_list, name, xchg):
    n_arr = len(parts_list)
    C = wmv_list[0][0].shape[1]
    tl = 256

    def body(*refs):
        p_refs, wmv_refs, o_refs = refs[:n_arr], refs[n_arr:4 * n_arr], refs[4 * n_arr:]
        for k in range(n_arr):
            g = p_refs[k][0].astype(F32)
            for i in range(1, N_DEV):
                g = g + p_refs[k][i].astype(F32)
            w_ref, m_ref, v_ref = wmv_refs[3 * k:3 * k + 3]
            d, nm, nv = _adamw_math(w_ref[...], g, m_ref[...], v_ref[...])
            for o, val in zip(o_refs[4 * k:4 * k + 4], (g, d, nm, nv)):
                o[...] = val

    in_specs = [pl.BlockSpec((N_DEV, w.shape[0], tl), lambda i: (0, 0, i)) for w, _, _ in wmv_list]
    in_specs += [pl.BlockSpec((w.shape[0], tl), lambda i: (0, i)) for w, _, _ in wmv_list for _ in range(3)]
    out_specs = [pl.BlockSpec((w.shape[0], tl), lambda i: (0, i)) for w, _, _ in wmv_list for _ in range(4)]
    out_shape = [jax.ShapeDtypeStruct(w.shape, F32) for w, _, _ in wmv_list for _ in range(4)]
    args = list(parts_list) + [a for wmv in wmv_list for a in wmv]
    outs, x_out = _hosted_call(body, name, C // tl, in_specs, out_specs, out_shape, [], args, xchg,
                               _cparams(VMEM_BIG))
    return [outs[4 * k:4 * k + 4] for k in range(n_arr)], x_out


_SMALL_NAMES = ("ada_b", "norm1", "conv_w", "conv_b", "dt_bias", "A_log", "D_skip", "sinks", "attn_out_norm",
                "ssm_out_norm", "norm2", "rel_bias", "final_norm")
N_MOD = 6 * D_MODEL


def _mod_row(a0, a1, a2):
    return jnp.concatenate([a0[2:3], a0[1:2], a1[0:1], a2[2:3], a2[1:2], a2[3:4]], axis=1)


def _small_update(gathered, params):
    n_g = len(gathered)
    flat = [a for name in _SMALL_NAMES for a in params[name]]

    def body(*refs):
        a0_ref, a1_ref, a2_ref, cw_ref, dv_ref, dd_ref, ds_ref, dr_ref, c_ref = refs[:n_g]
        wmv = refs[n_g:n_g + len(flat)]
        outs = refs[n_g + len(flat):]

        def total(ref):
            t = ref[0]
            for i in range(1, N_DEV):
                t = t + ref[i]
            return t

        t0, t1, t2, tcw, tdv, tdd, tds, tdr = [total(r) for r in (a0_ref, a1_ref, a2_ref, cw_ref, dv_ref, dd_ref,
                                                                   ds_ref, dr_ref)]
        r8 = lax.broadcasted_iota(jnp.int32, (N_HEADS, LANE), 0)
        l8 = lax.broadcasted_iota(jnp.int32, (N_HEADS, LANE), 1)

        def diag_row(t):
            return jnp.sum(jnp.where(r8 == l8, t, 0.0), axis=0, keepdims=True)[:, :N_HEADS]

        def lane_sums(t):
            return diag_row(jnp.broadcast_to(jnp.sum(t, axis=1, keepdims=True), (N_HEADS, LANE)))

        me = _lin(_my_pos())
        n_cw = XBC_W // N_DEV
        cw_mine = jnp.zeros((4, n_cw), F32)
        for j in range(N_DEV):
            cw_mine = cw_mine + tcw[0:4, j * n_cw:(j + 1) * n_cw] * jnp.where(me == j, 1.0, 0.0)
        grads = {
            "ada_b": _mod_row(t0, t1, t2), "norm1": t0[0:1], "conv_w": cw_mine, "conv_b": tcw[4:5],
            "dt_bias": lane_sums(tdv[:N_HEADS]), "A_log": lane_sums(tdv[N_HEADS:]), "D_skip": lane_sums(tdd),
            "sinks": diag_row(tds), "attn_out_norm": t1[1:2, :ATTN_W], "ssm_out_norm": t1[1:2, ATTN_W:],
            "norm2": t2[0:1], "rel_bias": tdr[:, :N_HEADS], "final_norm": t2[4:5],
        }
        for k, name in enumerate(_SMALL_NAMES):
            w_ref, m_ref, v_ref = wmv[3 * k:3 * k + 3]
            g = grads[name]
            d, nm, nv = _adamw_math(w_ref[...], g, m_ref[...], v_ref[...])
            for o, val in zip(outs[4 * k:4 * k + 4], (g, d, nm, nv)):
                o[...] = val
        loss_ref, call_ref, dmod_ref = outs[4 * len(_SMALL_NAMES):]
        loss_ref[...] = t2[5:6, 0:1]
        call_ref[...] = jnp.concatenate([c_ref[i] for i in range(N_DEV)], axis=0)
        dmod_ref[...] = jnp.concatenate([_mod_row(a0_ref[i], a1_ref[i], a2_ref[i]) for i in range(N_DEV)], axis=0)

    out_shape = [jax.ShapeDtypeStruct(params[name][0].shape, F32) for name in _SMALL_NAMES for _ in range(4)]
    out_shape += [jax.ShapeDtypeStruct((1, 1), F32), jax.ShapeDtypeStruct((N_DEV, D_MODEL), F32),
                  jax.ShapeDtypeStruct((N_DEV, N_MOD), F32)]
    res = pl.pallas_call(body, name="small_update", out_shape=out_shape)(*gathered, *flat)
    upd = {name: res[4 * k:4 * k + 4] for k, name in enumerate(_SMALL_NAMES)}
    loss, c_all, dmod_all = res[4 * len(_SMALL_NAMES):]
    return upd, loss, c_all, dmod_all


def _ada_w_update(c_all, dmod_all, w, m, v):
    chunk = w.shape[1]

    def body(c_ref, dm_ref, w_ref, m_ref, v_ref, g_ref, d_ref, nm_ref, nv_ref):
        me = _lin(_my_pos())
        dm = jnp.zeros((N_DEV, chunk), F32)
        for j in range(N_DEV):
            dm = dm + dm_ref[:, j * chunk:(j + 1) * chunk] * jnp.where(me == j, 1.0, 0.0)
        g = lax.dot_general(_silu(c_ref[...]), dm, (((0,), (0,)), ((), ())), precision=HI,
                            preferred_element_type=F32)
        d, nm, nv = _adamw_math(w_ref[...], g, m_ref[...], v_ref[...])
        g_ref[...] = g
        d_ref[...] = d
        nm_ref[...] = nm
        nv_ref[...] = nv

    return pl.pallas_call(body, name="ada_w_update", out_shape=[jax.ShapeDtypeStruct(w.shape, F32)] * 4,
                          compiler_params=_cparams(VMEM_BIG))(c_all, dmod_all, w, m, v)


def _local_step(x, tgt, c, mod, w_in, conv_w, w_o_mine, w_gu_mine, w_d_mine, p):
    S = x.shape[0]
    tm = min(512, S)
    tmm = min(256, S)
    tw = min(2048, S)
    shift1, scale1, gate1, shift2, scale2, gate2 = [mod[i:i + 1] for i in range(6)]
    buckets = jnp.asarray(_t5_bucket_table())
    per_head = lambda a: jnp.broadcast_to(a.reshape(N_HEADS, 1), (N_HEADS, LANE))
    dtb_row, alog_row, d_exp = per_head(p["dt_bias"]), per_head(p["A_log"]), per_head(p["D_skip"])
    sinks = p["sinks"].reshape(N_HEADS)

    half = w_gu_mine.shape[0] // 2
    (qkv, z, xbc, dt_raw), (g_d,) = _in_proj_fwd(x, p["norm1"], scale1, shift1, w_in, tm, ([w_d_mine], False))
    bias = _attn_bias(buckets, p["rel_bias"])
    (ya,), (g_gu_a,) = _attn_fwd(qkv, bias, sinks, ([w_gu_mine[:half]], False))
    (ys, prev_states), (g_gu_b, g_o) = _ssd_fwd(xbc, dt_raw, conv_w, p["conv_b"], dtb_row, alog_row, d_exp,
                                                ([w_gu_mine[half:], w_o_mine], False))
    w_o = g_o.reshape(D_MODEL, D_MODEL)
    w_d = g_d.reshape(D_FF, D_MODEL)
    x1 = _out_proj_fwd(x, ya, ys, z, p["attn_out_norm"], p["ssm_out_norm"], gate1, w_o, tm)
    dx1, h2, dgu, act, dmlp, acc2 = _mlp_loss(x1, tgt, p["norm2"], scale2, shift2, gate2, p["final_norm"],
                                              (g_gu_a, g_gu_b), w_d, tmm)
    g_w_gu = _wgrad(dgu, h2, 2 * D_FF // 4, tw, "wgrad_gate_up")
    g_w_d = _wgrad(act, dmlp, D_FF // 2, tw, "wgrad_down")
    dya, dys, dz, u, dmix, acc1 = _out_proj_bwd(dx1, ya, ys, z, p["attn_out_norm"], p["ssm_out_norm"], gate1, w_o, tm)
    g_w_o = _wgrad(u, dmix, D_MODEL, tw, "wgrad_out")
    (dq, dkv, dbias, dsk), (r_d,) = _attn_bwd(qkv, ya, dya, bias, sinks,
                                              ([g_w_d.reshape(N_DEV, D_FF // N_DEV, D_MODEL)], True))
    drel, dsink = _attn_finish(dbias, dsk, buckets)
    (dxbc, ddt, dcw, dvec, dd), (r_gu, r_o) = _ssd_bwd(
        xbc, dt_raw, prev_states, dys, conv_w, p["conv_b"], dtb_row, alog_row, d_exp,
        ([g_w_gu.reshape(N_DEV, 2 * D_FF // N_DEV, D_MODEL), g_w_o.reshape(N_DEV, D_MODEL // N_DEV, D_MODEL)], True))
    gx, h1, dproj, acc0 = _in_proj_bwd(x, dx1, dq, dkv, dz, dxbc, ddt, p["norm1"], scale1, shift1, w_in, tm)
    g_w_in, gathered = _wgrad(dproj, h1, IN_PAD, tw, "wgrad_in",
                              ([acc0, acc1, acc2, dcw, dvec, dd, dsink, drel, c], False))
    return gx, g_w_in, (r_o, r_gu, r_d), gathered


def kernel(x, c, ada_w, ada_b, norm1, w_in, conv_w, conv_b, dt_bias, A_log, D_skip, sinks, attn_out_norm, ssm_out_norm, w_o, norm2, w_gate_up, w_down, rel_bias, final_norm, loss_target, m_ada_w, m_ada_b, m_norm1, m_w_in, m_conv_w, m_conv_b, m_dt_bias, m_A_log, m_D_skip, m_sinks, m_attn_out_norm, m_ssm_out_norm, m_w_o, m_norm2, m_w_gate_up, m_w_down, m_rel_bias, m_final_norm, v_ada_w, v_ada_b, v_norm1, v_w_in, v_conv_w, v_conv_b, v_dt_bias, v_A_log, v_D_skip, v_sinks, v_attn_out_norm, v_ssm_out_norm, v_w_o, v_norm2, v_w_gate_up, v_w_down, v_rel_bias, v_final_norm):
    two_d = lambda a: a if a.ndim == 2 else a.reshape(-1, a.shape[-1])
    small_params = dict(
        ada_b=(ada_b, m_ada_b, v_ada_b), norm1=(norm1, m_norm1, v_norm1), conv_w=(conv_w, m_conv_w, v_conv_w),
        conv_b=(conv_b, m_conv_b, v_conv_b), dt_bias=(dt_bias, m_dt_bias, v_dt_bias), A_log=(A_log, m_A_log, v_A_log),
        D_skip=(D_skip, m_D_skip, v_D_skip), sinks=(sinks, m_sinks, v_sinks),
        attn_out_norm=(attn_out_norm, m_attn_out_norm, v_attn_out_norm),
        ssm_out_norm=(ssm_out_norm, m_ssm_out_norm, v_ssm_out_norm), norm2=(norm2, m_norm2, v_norm2),
        rel_bias=(rel_bias, m_rel_bias, v_rel_bias), final_norm=(final_norm, m_final_norm, v_final_norm))
    small_params = {k: tuple(two_d(a) for a in v) for k, v in small_params.items()}
    S = x.shape[1]
    xs, tgt = x.reshape(S, D_MODEL), loss_target.reshape(S, D_MODEL)
    ada_w2 = ada_w[0]
    chunk = ada_w2.shape[1]
    t_in = [jnp.transpose(a[0]) for a in (w_in, m_w_in, v_w_in)]
    t_gu = [jnp.transpose(a[0]) for a in (w_gate_up, m_w_gate_up, v_w_gate_up)]

    mod = _mod_exchange(c, ada_w2, ada_b.reshape(N_DEV, chunk)).reshape(6, D_MODEL)

    g_in, g_cw = _exchange([t_in[0].astype(WIRE_DTYPE), conv_w[0]], scatter=False, name="gather_w_in")
    w_in_full = jnp.pad(g_in.reshape(IN_W, D_MODEL), ((0, IN_PAD - IN_W), (0, 0)))
    conv_w_full = jnp.transpose(g_cw, (1, 0, 2)).reshape(4, XBC_W)

    p = {k: v[0] for k, v in small_params.items()}
    gx, gw_in, (r_o, r_gu, r_d), gathered = _local_step(
        xs, tgt, c, mod, w_in_full, conv_w_full, w_o[0].astype(WIRE_DTYPE), t_gu[0].astype(WIRE_DTYPE),
        w_down[0].astype(WIRE_DTYPE), p)

    (u_gu, u_d, u_o), (r_in,) = _reduce_adamw_hosting(
        [r_gu, r_d, r_o], [tuple(t_gu), (w_down[0], m_w_down[0], v_w_down[0]), (w_o[0], m_w_o[0], v_w_o[0])],
        "adamw_big", ([gw_in[:IN_W].reshape(N_DEV, IN_W // N_DEV, D_MODEL)], True))

    small, loss, c_all, dmod_all = _small_update(gathered, small_params)

    big = {
        "ada_w": _ada_w_update(c_all, dmod_all, ada_w2, m_ada_w[0], v_ada_w[0]),
        "w_in": [jnp.transpose(a) for a in _reduce_adamw(r_in, *t_in, "adamw_w_in")],
        "w_o": u_o,
        "w_gate_up": [jnp.transpose(a) for a in u_gu],
        "w_down": u_d,
    }
    big.update(small)

    order = ['ada_w', 'ada_b', 'norm1', 'w_in', 'conv_w', 'conv_b', 'dt_bias', 'A_log', 'D_skip', 'sinks',
             'attn_out_norm', 'ssm_out_norm', 'w_o', 'norm2', 'w_gate_up', 'w_down', 'rel_bias', 'final_norm']
    shapes = dict(ada_w=ada_w.shape, ada_b=ada_b.shape, norm1=norm1.shape, w_in=w_in.shape, conv_w=conv_w.shape,
                  conv_b=conv_b.shape, dt_bias=dt_bias.shape, A_log=A_log.shape, D_skip=D_skip.shape,
                  sinks=sinks.shape, attn_out_norm=attn_out_norm.shape, ssm_out_norm=ssm_out_norm.shape,
                  w_o=w_o.shape, norm2=norm2.shape, w_gate_up=w_gate_up.shape, w_down=w_down.shape,
                  rel_bias=rel_bias.shape, final_norm=final_norm.shape)
    outs = [[], [], [], []]
    for name in order:
        for kind in range(4):
            outs[kind].append(big[name][kind].reshape(shapes[name]))
    return (loss.reshape(()), gx.reshape(x.shape), *outs[0], *outs[1], *outs[2], *outs[3])
```

```python
import functools

import numpy as np
import jax
import jax.numpy as jnp
from jax import lax
from jax.experimental import pallas as pl
from jax.experimental.pallas import tpu as pltpu

F32 = jnp.float32
MXU_DTYPE = jnp.bfloat16
WIRE_DTYPE = jnp.bfloat16
HI = lax.Precision.HIGHEST
MESH = pl.DeviceIdType.MESH
N_DEV = 8

D_MODEL = 1024
ATTN_W = 512
KV_W = 128
SSM_W = 512
XBC_W = 1024
N_HEADS = 8
D_STATE = 128
D_FF = 2816
IN_W = 2312
IN_PAD = 2432
BLK = 128
N_BUCKETS = 32
EPS = 1e-6
LANE = 128
HALF = 64

ADAM_LR, ADAM_B1, ADAM_B2, ADAM_EPS, ADAM_WD, ADAM_STEP = 0.001, 0.9, 0.999, 1e-08, 0.01, 10

VMEM_BIG = 56 * 1024 * 1024
WD_CUT = 256
WGU_CUTS = (304, 608)


def _cparams(vmem=None):
    if vmem is None:
        return pltpu.CompilerParams()
    return pltpu.CompilerParams(vmem_limit_bytes=vmem)


def _mm(a, b):
    return jnp.dot(a.astype(MXU_DTYPE), b.astype(MXU_DTYPE), preferred_element_type=F32)


def _mm_nt(a, b):
    return lax.dot_general(a.astype(MXU_DTYPE), b.astype(MXU_DTYPE), (((1,), (1,)), ((), ())),
                           preferred_element_type=F32)


def _mm_tn(a, b):
    return lax.dot_general(a.astype(MXU_DTYPE), b.astype(MXU_DTYPE), (((0,), (0,)), ((), ())),
                           preferred_element_type=F32)


def _mm_hi(a, b):
    return jnp.dot(a, b, precision=HI, preferred_element_type=F32)


def _silu(x):
    return x * jax.nn.sigmoid(x)


def _softplus(x):
    return jnp.maximum(x, 0.0) + jnp.log1p(jnp.exp(-jnp.abs(x)))


def _rms(x, g, n):
    return x * lax.rsqrt(jnp.sum(x * x, axis=-1, keepdims=True) * (1.0 / n) + EPS) * g


def _modnorm(x, g, scale, shift):
    return _rms(x, g, x.shape[-1]) * (1.0 + scale) + shift


def _lane_iota(shape):
    return lax.broadcasted_iota(jnp.int32, shape, len(shape) - 1)


def _split_pair(t):
    lane = _lane_iota(t.shape)
    lo = jnp.where(lane < HALF, t, 0.0)
    hi = pltpu.roll(jnp.where(lane >= HALF, t, 0.0), HALF, 1)
    return lo, hi


def _join_pair(lo, hi):
    lane = _lane_iota(lo.shape)
    return jnp.where(lane < HALF, lo, pltpu.roll(hi, HALF, 1))


def _split_heads(t, n_pairs):
    out = []
    for p in range(n_pairs):
        out.extend(_split_pair(t[:, p * LANE:(p + 1) * LANE]))
    return out


def _join_heads(hs):
    return jnp.concatenate([_join_pair(hs[2 * p], hs[2 * p + 1]) for p in range(len(hs) // 2)], axis=1)


def _t5_bucket_table():
    dist = np.arange(BLK)[:, None] + BLK - np.arange(2 * BLK)[None, :]
    n = np.maximum(dist, 0)
    max_exact = N_BUCKETS // 2
    large = max_exact + (np.log(np.maximum(n, 1) / max_exact) / np.log(128 / max_exact)
                         * (N_BUCKETS - max_exact)).astype(np.int32)
    large = np.minimum(large, N_BUCKETS - 1)
    return np.where(n < max_exact, n, large).astype(np.int32)


def _my_pos():
    return lax.axis_index("x"), lax.axis_index("y"), lax.axis_index("c")


def _peer(k):
    x, y, c = _my_pos()
    return (1 - x if k & 4 else x, 1 - y if k & 2 else y, 1 - c if k & 1 else c)


def _lin(pos):
    return 4 * pos[0] + 2 * pos[1] + pos[2]


def _xchg_copies(ins, outs, sems, scatter):
    local_sem, send_sem, recv_sem = sems
    me = _lin(_my_pos())
    local, remote = [], []
    for a in range(len(ins)):
        src = ins[a].at[me] if scatter else ins[a]
        local.append(pltpu.make_async_copy(src, outs[a].at[me], local_sem.at[a]))
    for k in range(1, N_DEV):
        peer = _peer(k)
        for a in range(len(ins)):
            src = ins[a].at[_lin(peer)] if scatter else ins[a]
            remote.append(pltpu.make_async_remote_copy(src, outs[a].at[me], send_sem.at[a, k - 1],
                                                       recv_sem.at[a, k - 1], device_id=peer, device_id_type=MESH))
    return local, remote


def _xchg_start(ins, outs, sems, scatter):
    local, remote = _xchg_copies(ins, outs, sems, scatter)
    for cp in local + remote:
        cp.start()


def _xchg_wait(ins, outs, sems, scatter):
    local, remote = _xchg_copies(ins, outs, sems, scatter)
    for cp in local:
        cp.wait()
    for cp in remote:
        cp.wait_send()
        cp.wait_recv()


def _xchg_shapes(arrs, scatter):
    n = len(arrs)
    if scatter:
        out_shape = [jax.ShapeDtypeStruct(a.shape, a.dtype) for a in arrs]
    else:
        out_shape = [jax.ShapeDtypeStruct((N_DEV,) + a.shape, a.dtype) for a in arrs]
    sems = [pltpu.SemaphoreType.DMA((n,)), pltpu.SemaphoreType.DMA((n, N_DEV - 1)),
            pltpu.SemaphoreType.DMA((n, N_DEV - 1))]
    return out_shape, sems


def _exchange(arrs, scatter, name):
    n = len(arrs)
    out_shape, sems = _xchg_shapes(arrs, scatter)

    def body(*refs):
        ins, outs, s = refs[:n], refs[n:2 * n], refs[2 * n:]
        _xchg_start(ins, outs, s, scatter)
        _xchg_wait(ins, outs, s, scatter)

    hbm = pl.BlockSpec(memory_space=pltpu.HBM)
    return pl.pallas_call(body, name=name, out_shape=out_shape, in_specs=[hbm] * n, out_specs=[hbm] * n,
                          scratch_shapes=sems)(*arrs)


def _hosted_call(body, name, grid, in_specs, out_specs, out_shape, scratch_shapes, args, xchg, cparams):
    arrs, scatter = xchg
    grid = (grid,) if isinstance(grid, int) else tuple(grid)
    n, n_in, n_out, n_scr = len(arrs), len(in_specs), len(out_specs), len(scratch_shapes)
    x_shape, x_sems = _xchg_shapes(arrs, scatter)

    def hosted(*refs):
        ins, refs = refs[:n_in], refs[n_in:]
        x_in, refs = refs[:n], refs[n:]
        outs, refs = refs[:n_out], refs[n_out:]
        x_out, refs = refs[:n], refs[n:]
        scr, sems = refs[:n_scr], refs[n_scr:]
        step = pl.program_id(0)
        for d in range(1, len(grid)):
            step = step * grid[d] + pl.program_id(d)

        @pl.when(step == 0)
        def _():
            _xchg_start(x_in, x_out, sems, scatter)

        body(*ins, *outs, *scr)

        @pl.when(step == int(np.prod(grid)) - 1)
        def _():
            _xchg_wait(x_in, x_out, sems, scatter)

    hbm = pl.BlockSpec(memory_space=pltpu.HBM)
    res = pl.pallas_call(
        hosted, name=name, grid=grid, in_specs=list(in_specs) + [hbm] * n,
        out_specs=list(out_specs) + [hbm] * n, out_shape=list(out_shape) + x_shape,
        scratch_shapes=list(scratch_shapes) + x_sems, compiler_params=cparams,
    )(*args, *arrs)
    return res[:n_out], res[n_out:]


def _mod_exchange(c, ada_w, ada_b8):
    chunk = ada_w.shape[1]

    def body(c_ref, w_ref, b_ref, out_ref, cbuf, part, s1, r1, s2, r2):
        me = _lin(_my_pos())
        first = []
        for k in range(1, N_DEV):
            cp = pltpu.make_async_remote_copy(c_ref, cbuf.at[me], s1.at[k - 1], r1.at[k - 1],
                                              device_id=_peer(k), device_id_type=MESH)
            cp.start()
            first.append(cp)
        cbuf[me] = c_ref[...]
        for cp in first:
            cp.wait_send()
            cp.wait_recv()
        cond = _silu(jnp.concatenate([cbuf[i] for i in range(N_DEV)], axis=0))
        mod = _mm_hi(cond, w_ref[...]) + b_ref[pl.ds(me, 1), :]
        for j in range(N_DEV):
            part[j] = mod[j:j + 1, :]
        second = []
        for k in range(1, N_DEV):
            peer = _peer(k)
            cp = pltpu.make_async_remote_copy(part.at[_lin(peer)], out_ref.at[me], s2.at[k - 1], r2.at[k - 1],
                                              device_id=peer, device_id_type=MESH)
            cp.start()
            second.append(cp)
        out_ref[me] = part[me]
        for cp in second:
            cp.wait_send()
            cp.wait_recv()

    vm = pl.BlockSpec(memory_space=pltpu.VMEM)
    return pl.pallas_call(
        body, name="mod_exchange", out_shape=jax.ShapeDtypeStruct((N_DEV, 1, chunk), F32),
        in_specs=[vm, vm, vm], out_specs=vm,
        scratch_shapes=[pltpu.VMEM((N_DEV, 1, D_MODEL), F32), pltpu.VMEM((N_DEV, 1, chunk), F32)]
        + [pltpu.SemaphoreType.DMA((N_DEV - 1,))] * 4,
    )(c, ada_w, ada_b8)


def _row(i):
    return (i, 0)


def _fixed(i):
    return (0, 0)


def _in_proj_fwd(x, norm1, scale1, shift1, w_in, tm, xchg):
    S = x.shape[0]

    def body(x_ref, n_ref, sc_ref, sh_ref, w_ref, qkv_ref, z_ref, xbc_ref, dt_ref):
        h = _modnorm(x_ref[...], n_ref[...], sc_ref[...], sh_ref[...])
        p = _mm_nt(h, w_ref[...])
        qkv_ref[...] = p[:, :768].astype(qkv_ref.dtype)
        z_ref[...] = p[:, 768:1280]
        xbc_ref[...] = p[:, 1280:2304]
        dt_ref[...] = p[:, 2304:IN_PAD]

    vec = pl.BlockSpec((1, D_MODEL), _fixed)
    return _hosted_call(
        body, "in_proj_fwd", S // tm,
        in_specs=[pl.BlockSpec((tm, D_MODEL), _row), vec, vec, vec, pl.BlockSpec((IN_PAD, D_MODEL), _fixed)],
        out_specs=[pl.BlockSpec((tm, 768), _row), pl.BlockSpec((tm, SSM_W), _row),
                   pl.BlockSpec((tm, XBC_W), _row), pl.BlockSpec((tm, LANE), _row)],
        out_shape=[jax.ShapeDtypeStruct((S, 768), MXU_DTYPE), jax.ShapeDtypeStruct((S, SSM_W), F32),
                   jax.ShapeDtypeStruct((S, XBC_W), F32), jax.ShapeDtypeStruct((S, LANE), F32)],
        scratch_shapes=[], args=(x, norm1, scale1, shift1, w_in), xchg=xchg, cparams=_cparams(VMEM_BIG),
    )


def _in_proj_bwd(x, dx1, dq, dkv, dz, dxbc, ddt, norm1, scale1, shift1, w_in, tm):
    S = x.shape[0]

    def body(x_ref, dx1_ref, dq_ref, dkv_ref, dz_ref, dxbc_ref, ddt_ref, n_ref, sc_ref, sh_ref, w_ref,
             gx_ref, h_ref, dp_ref, acc_ref):
        @pl.when(pl.program_id(0) == 0)
        def _():
            acc_ref[...] = jnp.zeros_like(acc_ref)

        h, vjp = jax.vjp(_modnorm, x_ref[...], n_ref[...], sc_ref[...], sh_ref[...])
        dp = jnp.concatenate([dq_ref[...].astype(MXU_DTYPE), dkv_ref[...].astype(MXU_DTYPE),
                              dz_ref[...].astype(MXU_DTYPE), dxbc_ref[...].astype(MXU_DTYPE),
                              ddt_ref[...].astype(MXU_DTYPE)], axis=1)
        dh = _mm(dp, w_ref[...])
        dx, dn, dsc, dsh = vjp(dh)
        gx_ref[...] = dx1_ref[...] + dx
        h_ref[...] = h.astype(h_ref.dtype)
        dp_ref[...] = dp
        acc_ref[0:1, :] += dn
        acc_ref[1:2, :] += dsc
        acc_ref[2:3, :] += dsh

    vec = pl.BlockSpec((1, D_MODEL), _fixed)
    return pl.pallas_call(
        body, name="in_proj_bwd", grid=(S // tm,),
        in_specs=[pl.BlockSpec((tm, D_MODEL), _row), pl.BlockSpec((tm, D_MODEL), _row),
                  pl.BlockSpec((tm, ATTN_W), _row), pl.BlockSpec((tm, 2 * KV_W), _row),
                  pl.BlockSpec((tm, SSM_W), _row), pl.BlockSpec((tm, XBC_W), _row), pl.BlockSpec((tm, LANE), _row),
                  vec, vec, vec, pl.BlockSpec((IN_PAD, D_MODEL), _fixed)],
        out_specs=[pl.BlockSpec((tm, D_MODEL), _row), pl.BlockSpec((tm, D_MODEL), _row),
                   pl.BlockSpec((tm, IN_PAD), _row), pl.BlockSpec((8, D_MODEL), _fixed)],
        out_shape=[jax.ShapeDtypeStruct((S, D_MODEL), F32), jax.ShapeDtypeStruct((S, D_MODEL), MXU_DTYPE),
                   jax.ShapeDtypeStruct((S, IN_PAD), MXU_DTYPE), jax.ShapeDtypeStruct((8, D_MODEL), F32)],
        compiler_params=_cparams(VMEM_BIG),
    )(x, dx1, dq, dkv, dz, dxbc, ddt, norm1, scale1, shift1, w_in)


def _out_stage(ya, ys0, ys1, z0, z1, an, sn0, sn1):
    half = SSM_W // 2
    a = _rms(ya, an, ATTN_W)
    g0 = _rms(ys0 * _silu(z0), sn0, half)
    g1 = _rms(ys1 * _silu(z1), sn1, half)
    return jnp.concatenate([a, g0, g1], axis=1)


def _out_stage_args(ya_ref, ys_ref, z_ref, an_ref, sn_ref):
    half = SSM_W // 2
    return (ya_ref[...], ys_ref[:, :half], ys_ref[:, half:], z_ref[:, :half], z_ref[:, half:],
            an_ref[...], sn_ref[:, :half], sn_ref[:, half:])


def _out_proj_fwd(x, ya, ys, z, an, sn, gate1, w_o, tm, xchg):
    S = x.shape[0]

    def body(x_ref, ya_ref, ys_ref, z_ref, an_ref, sn_ref, g_ref, w_ref, x1_ref):
        u = _out_stage(*_out_stage_args(ya_ref, ys_ref, z_ref, an_ref, sn_ref))
        x1_ref[...] = x_ref[...] + g_ref[...] * _mm(u, w_ref[...])

    half = pl.BlockSpec((tm, ATTN_W), _row)
    hvec = pl.BlockSpec((1, ATTN_W), _fixed)
    (x1,), x_out = _hosted_call(
        body, "out_proj_fwd", S // tm,
        in_specs=[pl.BlockSpec((tm, D_MODEL), _row), half, half, half, hvec, hvec,
                  pl.BlockSpec((1, D_MODEL), _fixed), pl.BlockSpec((D_MODEL, D_MODEL), _fixed)],
        out_specs=[pl.BlockSpec((tm, D_MODEL), _row)],
        out_shape=[jax.ShapeDtypeStruct((S, D_MODEL), F32)],
        scratch_shapes=[], args=(x, ya, ys, z, an, sn, gate1, w_o), xchg=xchg, cparams=_cparams(VMEM_BIG),
    )
    return x1, x_out


def _out_proj_bwd(dx1, ya, ys, z, an, sn, gate1, w_o, tm):
    S = dx1.shape[0]

    def body(dx1_ref, ya_ref, ys_ref, z_ref, an_ref, sn_ref, g_ref, w_ref,
             dya_ref, dys_ref, dz_ref, u_ref, dmix_ref, acc_ref):
        @pl.when(pl.program_id(0) == 0)
        def _():
            acc_ref[...] = jnp.zeros_like(acc_ref)

        u, vjp = jax.vjp(_out_stage, *_out_stage_args(ya_ref, ys_ref, z_ref, an_ref, sn_ref))
        dx1 = dx1_ref[...]
        mix = _mm(u, w_ref[...])
        dmix = dx1 * g_ref[...]
        du = _mm_nt(dmix, w_ref[...])
        dya, dys0, dys1, dz0, dz1, dan, dsn0, dsn1 = vjp(du)
        dya_ref[...] = dya
        dys_ref[...] = jnp.concatenate([dys0, dys1], axis=1)
        dz_ref[...] = jnp.concatenate([dz0, dz1], axis=1)
        u_ref[...] = u.astype(u_ref.dtype)
        dmix_ref[...] = dmix.astype(dmix_ref.dtype)
        acc_ref[0:1, :] += jnp.sum(dx1 * mix, axis=0, keepdims=True)
        acc_ref[1:2, :] += jnp.concatenate([dan, dsn0, dsn1], axis=1)

    half = pl.BlockSpec((tm, ATTN_W), _row)
    hvec = pl.BlockSpec((1, ATTN_W), _fixed)
    full = pl.BlockSpec((tm, D_MODEL), _row)
    return pl.pallas_call(
        body, name="out_proj_bwd", grid=(S // tm,),
        in_specs=[full, half, half, half, hvec, hvec,
                  pl.BlockSpec((1, D_MODEL), _fixed), pl.BlockSpec((D_MODEL, D_MODEL), _fixed)],
        out_specs=[half, half, half, full, full, pl.BlockSpec((8, D_MODEL), _fixed)],
        out_shape=[jax.ShapeDtypeStruct((S, ATTN_W), F32)] * 3
        + [jax.ShapeDtypeStruct((S, D_MODEL), MXU_DTYPE)] * 2 + [jax.ShapeDtypeStruct((8, D_MODEL), F32)],
        compiler_params=_cparams(VMEM_BIG),
    )(dx1, ya, ys, z, an, sn, gate1, w_o)


def _loss_rows(x2, fn, tgt):
    y = _rms(x2, fn, D_MODEL)
    per_row = jnp.sum(jnp.square(y - tgt), axis=1, keepdims=True)
    return jnp.sum(per_row, axis=0, keepdims=True) * (0.5 / D_MODEL)


def _mlp_loss(x1, tgt, norm2, scale2, shift2, gate2, fnorm, w_gu, w_d, tm):
    S = x1.shape[0]
    n_pieces = len(w_gu) + len(w_d)

    def body(*refs):
        x1_ref, t_ref, n_ref, sc_ref, sh_ref, g_ref, fn_ref = refs[:7]
        piece_refs = refs[7:7 + n_pieces]
        dx1_ref, h_ref, dgu_ref, act_ref, dmlp_ref, acc_ref, wgu, wd, wsem = refs[7 + n_pieces:]

        @pl.when(pl.program_id(0) == 0)
        def _():
            acc_ref[...] = jnp.zeros_like(acc_ref)
            copies = []
            for dst, pieces in ((wgu, piece_refs[:len(w_gu)]), (wd, piece_refs[len(w_gu):])):
                shard = sum(p.shape[1] for p in pieces)
                off = 0
                for p in pieces:
                    for j in range(N_DEV):
                        copies.append(pltpu.make_async_copy(p.at[j], dst.at[pl.ds(j * shard + off, p.shape[1])],
                                                            wsem.at[len(copies)]))
                    off += p.shape[1]
            for cp in copies:
                cp.start()
            for cp in copies:
                cp.wait()

        x1 = x1_ref[...]
        gate2 = g_ref[...]
        h, vjp_h = jax.vjp(_modnorm, x1, n_ref[...], sc_ref[...], sh_ref[...])
        hb = h.astype(MXU_DTYPE)
        gu = _mm_nt(hb, wgu[...])
        g, u = gu[:, :D_FF], gu[:, D_FF:]
        sg = jax.nn.sigmoid(g)
        silu_g = g * sg
        act = (silu_g * u).astype(MXU_DTYPE)
        mlp = _mm(act, wd[...])
        x2 = x1 + gate2 * mlp
        loss, vjp_loss = jax.vjp(_loss_rows, x2, fn_ref[...], t_ref[...])
        dx2, dfn, _ = vjp_loss(jnp.ones((1, 1), F32))
        dmlp = (dx2 * gate2).astype(MXU_DTYPE)
        dact = _mm_nt(dmlp, wd[...])
        dg = dact * u * (sg * (1.0 + g * (1.0 - sg)))
        du = dact * silu_g
        dgu = jnp.concatenate([dg, du], axis=1).astype(MXU_DTYPE)
        dh = _mm(dgu, wgu[...])
        dx, dn, dsc, dsh = vjp_h(dh)
        dx1_ref[...] = dx2 + dx
        h_ref[...] = hb
        dgu_ref[...] = dgu
        act_ref[...] = act
        dmlp_ref[...] = dmlp
        acc_ref[0:1, :] += dn
        acc_ref[1:2, :] += dsc
        acc_ref[2:3, :] += dsh
        acc_ref[3:4, :] += jnp.sum(dx2 * mlp, axis=0, keepdims=True)
        acc_ref[4:5, :] += dfn
        acc_ref[5:6, :] += jnp.broadcast_to(loss, (1, D_MODEL))

    full = pl.BlockSpec((tm, D_MODEL), _row)
    vec = pl.BlockSpec((1, D_MODEL), _fixed)
    anyspec = pl.BlockSpec(memory_space=pl.ANY)
    return pl.pallas_call(
        body, name="mlp_loss", grid=(S // tm,),
        in_specs=[full, full, vec, vec, vec, vec, vec] + [anyspec] * n_pieces,
        out_specs=[full, full, pl.BlockSpec((tm, 2 * D_FF), _row), pl.BlockSpec((tm, D_FF), _row), full,
                   pl.BlockSpec((8, D_MODEL), _fixed)],
        out_shape=[jax.ShapeDtypeStruct((S, D_MODEL), F32), jax.ShapeDtypeStruct((S, D_MODEL), MXU_DTYPE),
                   jax.ShapeDtypeStruct((S, 2 * D_FF), MXU_DTYPE), jax.ShapeDtypeStruct((S, D_FF), MXU_DTYPE),
                   jax.ShapeDtypeStruct((S, D_MODEL), MXU_DTYPE), jax.ShapeDtypeStruct((8, D_MODEL), F32)],
        scratch_shapes=[pltpu.VMEM((2 * D_FF, D_MODEL), MXU_DTYPE), pltpu.VMEM((D_FF, D_MODEL), MXU_DTYPE),
                        pltpu.SemaphoreType.DMA((N_DEV * n_pieces,))],
        compiler_params=_cparams(VMEM_BIG),
    )(x1, tgt, norm2, scale2, shift2, gate2, fnorm, *w_gu, *w_d)


def _wgrad(a, g, tk, ts, name, xchg=None, g_cols=None):
    S, K = a.shape
    N, col = (g.shape[1], 0) if g_cols is None else g_cols
    ns = S // ts

    def body(a_ref, g_ref, o_ref, acc_ref):
        s = pl.program_id(1)

        @pl.when(s == 0)
        def _():
            acc_ref[...] = jnp.zeros_like(acc_ref)

        acc_ref[...] += _mm_tn(a_ref[...], g_ref[...])

        @pl.when(s == ns - 1)
        def _():
            o_ref[...] = acc_ref[...].astype(o_ref.dtype)

    in_specs = [pl.BlockSpec((ts, tk), lambda j, s: (s, j)), pl.BlockSpec((ts, N), lambda j, s: (s, col))]
    out_spec = pl.BlockSpec((tk, N), lambda j, s: (j, 0))
    out_shape = jax.ShapeDtypeStruct((K, N), WIRE_DTYPE)
    scratch = [pltpu.VMEM((tk, N), F32)]
    if xchg is None:
        return pl.pallas_call(body, name=name, grid=(K // tk, ns), in_specs=in_specs, out_specs=out_spec,
                              out_shape=out_shape, scratch_shapes=scratch, compiler_params=_cparams(VMEM_BIG))(a, g)
    (out,), x_out = _hosted_call(body, name, (K // tk, ns), in_specs, [out_spec], [out_shape], scratch, (a, g), xchg,
                                 _cparams(VMEM_BIG))
    return out, x_out


MASKED = -1e30
QK_SCALE = HALF ** -0.5


def _attn_bias(buckets, rel_bias):
    def body(bk_ref, relb_ref, out_ref):
        bk = bk_ref[...]
        i = lax.broadcasted_iota(jnp.int32, (BLK, 2 * BLK), 0)
        j = lax.broadcasted_iota(jnp.int32, (BLK, 2 * BLK), 1)
        window = (j > i) & (j <= i + BLK)
        for h in range(N_HEADS):
            acc = jnp.zeros((BLK, 2 * BLK), F32)
            for b in range(N_BUCKETS):
                acc = jnp.where(bk == b, relb_ref[b, h], acc)
            out_ref[0, h] = jnp.where(window, acc, MASKED)
            out_ref[1, h] = jnp.where(window & (j >= BLK), acc, MASKED)

    return pl.pallas_call(
        body, name="attn_bias", out_shape=jax.ShapeDtypeStruct((2, N_HEADS, BLK, 2 * BLK), F32),
        in_specs=[pl.BlockSpec(memory_space=pltpu.VMEM), pl.BlockSpec(memory_space=pltpu.SMEM)],
    )(buckets, rel_bias)


def _attn_kv(kvp_ref, kvc_ref):
    kvp = kvp_ref[...].astype(F32)
    kvc = kvc_ref[...].astype(F32)
    kp, kc = _split_pair(kvp[:, :LANE]), _split_pair(kvc[:, :LANE])
    vp, vc = _split_pair(kvp[:, LANE:]), _split_pair(kvc[:, LANE:])
    k_pads = [jnp.concatenate([kp[g], kc[g]], axis=0).astype(MXU_DTYPE) for g in range(2)]
    v_pads = [jnp.concatenate([vp[g], vc[g]], axis=0).astype(MXU_DTYPE) for g in range(2)]
    return k_pads, v_pads


def _attn_fwd(qkv, bias, sinks, xchg):
    S = qkv.shape[0]
    nb = S // BLK

    def body(q_ref, kvp_ref, kvc_ref, bias_ref, sinks_ref, y_ref):
        first = jnp.where(pl.program_id(0) == 0, 1, 0)
        q_heads = _split_heads(q_ref[...].astype(F32) * QK_SCALE, 4)
        k_pads, v_pads = _attn_kv(kvp_ref, kvc_ref)
        heads = range(N_HEADS)
        s = [_mm_nt(q_heads[h].astype(MXU_DTYPE), k_pads[h // 4]) + bias_ref[first, h] for h in heads]
        m = [jnp.maximum(jnp.max(s[h], axis=-1, keepdims=True), sinks_ref[h]) for h in heads]
        p = [jnp.exp(s[h] - m[h]) for h in heads]
        rinv = [1.0 / (jnp.sum(p[h], axis=-1, keepdims=True) + jnp.exp(sinks_ref[h] - m[h])) for h in heads]
        y_ref[...] = _join_heads([_mm(p[h], v_pads[h // 4]) * rinv[h] for h in heads])

    smem = pl.BlockSpec(memory_space=pltpu.SMEM)
    return _hosted_call(
        body, "attn_fwd", nb,
        in_specs=[pl.BlockSpec((BLK, ATTN_W), _row),
                  pl.BlockSpec((BLK, 2 * KV_W), lambda i: (jnp.maximum(i - 1, 0), 2)),
                  pl.BlockSpec((BLK, 2 * KV_W), lambda i: (i, 2)),
                  pl.BlockSpec((2, N_HEADS, BLK, 2 * BLK), lambda i: (0, 0, 0, 0)), smem],
        out_specs=[pl.BlockSpec((BLK, ATTN_W), _row)],
        out_shape=[jax.ShapeDtypeStruct((S, ATTN_W), F32)],
        scratch_shapes=[],
        args=(qkv, qkv, qkv, bias, sinks), xchg=xchg, cparams=_cparams(),
    )


def _attn_bwd(qkv, y, dy, bias, sinks, xchg):
    S = qkv.shape[0]
    nb = S // BLK

    def body(q_ref, kvp_ref, kvc_ref, y_ref, dy_ref, bias_ref, sinks_ref, dq_ref, dkv_ref, dbias_ref, dsk_ref, carry_ref):
        i = pl.program_id(0)

        @pl.when(i == 0)
        def _():
            dbias_ref[...] = jnp.zeros_like(dbias_ref)
            dsk_ref[...] = jnp.zeros_like(dsk_ref)
            carry_ref[...] = jnp.zeros_like(carry_ref)

        first = jnp.where(i == nb - 1, 1, 0)
        q_heads = _split_heads(q_ref[...].astype(F32) * QK_SCALE, 4)
        k_pads, v_pads = _attn_kv(kvp_ref, kvc_ref)
        y_heads = _split_heads(y_ref[...], 4)
        dy_heads = _split_heads(dy_ref[...], 4)
        heads = range(N_HEADS)
        qs = [q_heads[h].astype(MXU_DTYPE) for h in heads]
        s = [_mm_nt(qs[h], k_pads[h // 4]) + bias_ref[first, h] for h in heads]
        m = [jnp.maximum(jnp.max(s[h], axis=-1, keepdims=True), sinks_ref[h]) for h in heads]
        p = [jnp.exp(s[h] - m[h]) for h in heads]
        esink = [jnp.exp(sinks_ref[h] - m[h]) for h in heads]
        rinv = [1.0 / (jnp.sum(p[h], axis=-1, keepdims=True) + esink[h]) for h in heads]
        t = [dy_heads[h] * rinv[h] for h in heads]
        delta = [jnp.sum(t[h] * y_heads[h], axis=-1, keepdims=True) for h in heads]
        tb = [t[h].astype(MXU_DTYPE) for h in heads]
        dp = [_mm_nt(tb[h], v_pads[h // 4]) for h in heads]
        ds = [p[h] * (dp[h] - delta[h]) for h in heads]
        for h in heads:
            dbias_ref[h] += ds[h]
            dsk_ref[h] -= esink[h] * delta[h]
        dsb = [ds[h].astype(MXU_DTYPE) for h in heads]
        pb = [p[h].astype(MXU_DTYPE) for h in heads]
        dq_heads = [_mm(dsb[h], k_pads[h // 4]) * QK_SCALE for h in heads]
        dk_pads = [_mm_tn(jnp.concatenate(dsb[4 * g:4 * g + 4], axis=0), jnp.concatenate(qs[4 * g:4 * g + 4], axis=0))
                   for g in range(2)]
        dv_pads = [_mm_tn(jnp.concatenate(pb[4 * g:4 * g + 4], axis=0), jnp.concatenate(tb[4 * g:4 * g + 4], axis=0))
                   for g in range(2)]
        dq_ref[...] = _join_heads(dq_heads)
        dk_prev = _join_pair(dk_pads[0][:BLK], dk_pads[1][:BLK])
        dk_cur = _join_pair(dk_pads[0][BLK:], dk_pads[1][BLK:])
        dv_prev = _join_pair(dv_pads[0][:BLK], dv_pads[1][:BLK])
        dv_cur = _join_pair(dv_pads[0][BLK:], dv_pads[1][BLK:])
        dkv_ref[...] = jnp.concatenate([dk_cur, dv_cur], axis=1) + carry_ref[...]
        carry_ref[...] = jnp.concatenate([dk_prev, dv_prev], axis=1)

    smem = pl.BlockSpec(memory_space=pltpu.SMEM)
    rev = lambda i: (nb - 1 - i, 0)
    return _hosted_call(
        body, "attn_bwd", nb,
        in_specs=[pl.BlockSpec((BLK, ATTN_W), rev),
                  pl.BlockSpec((BLK, 2 * KV_W), lambda i: (jnp.maximum(nb - 2 - i, 0), 2)),
                  pl.BlockSpec((BLK, 2 * KV_W), lambda i: (nb - 1 - i, 2)),
                  pl.BlockSpec((BLK, ATTN_W), rev), pl.BlockSpec((BLK, ATTN_W), rev),
                  pl.BlockSpec((2, N_HEADS, BLK, 2 * BLK), lambda i: (0, 0, 0, 0)), smem],
        out_specs=[pl.BlockSpec((BLK, ATTN_W), rev), pl.BlockSpec((BLK, 2 * KV_W), rev),
                   pl.BlockSpec((N_HEADS, BLK, 2 * BLK), lambda i: (0, 0, 0)),
                   pl.BlockSpec((N_HEADS, BLK, 1), lambda i: (0, 0, 0))],
        out_shape=[jax.ShapeDtypeStruct((S, ATTN_W), F32), jax.ShapeDtypeStruct((S, 2 * KV_W), F32),
                   jax.ShapeDtypeStruct((N_HEADS, BLK, 2 * BLK), F32), jax.ShapeDtypeStruct((N_HEADS, BLK, 1), F32)],
        scratch_shapes=[pltpu.VMEM((BLK, 2 * KV_W), F32)],
        args=(qkv, qkv, qkv, y, dy, bias, sinks), xchg=xchg, cparams=_cparams(),
    )


def _attn_finish(dbias, dsk, buckets):
    def body(db_ref, dsk_ref, bk_ref, drel_ref, dsink_ref):
        bk = bk_ref[...]
        r = lax.broadcasted_iota(jnp.int32, (N_BUCKETS, LANE), 0)
        l = lax.broadcasted_iota(jnp.int32, (N_BUCKETS, LANE), 1)
        row = lax.broadcasted_iota(jnp.int32, (N_HEADS, LANE), 0)
        res = jnp.zeros((N_BUCKETS, LANE), F32)
        dsink = jnp.zeros((N_HEADS, LANE), F32)
        for h in range(N_HEADS):
            db = db_ref[h]
            for b in range(N_BUCKETS):
                v = jnp.sum(jnp.sum(jnp.where(bk == b, db, 0.0), axis=1, keepdims=True), axis=0, keepdims=True)
                res = res + jnp.where((r == b) & (l == h), v, 0.0)
            dsink = dsink + jnp.where(row == h, jnp.sum(dsk_ref[h], axis=0, keepdims=True), 0.0)
        drel_ref[...] = res
        dsink_ref[...] = dsink

    return pl.pallas_call(body, name="attn_finish",
                          out_shape=[jax.ShapeDtypeStruct((N_BUCKETS, LANE), F32),
                                     jax.ShapeDtypeStruct((N_HEADS, LANE), F32)])(dbias, dsk, buckets)


def _ssd_consts():
    r = lax.broadcasted_iota(jnp.int32, (BLK, BLK), 0)
    c = lax.broadcasted_iota(jnp.int32, (BLK, BLK), 1)
    causal = c <= r
    upper = (r <= c).astype(F32)
    last = r == BLK - 1
    head = lax.broadcasted_iota(jnp.int32, (N_HEADS, BLK), 0)
    return causal, upper, last, head


def _ssd_chunk(xs, bg, cg, dt_raw_t, prev, dtb, alog, d_rows, consts):
    causal, upper, last, head = consts
    dt_t = _softplus(dt_raw_t + dtb)
    acs_t = _mm_hi(dt_t * (-jnp.exp(alog)), upper)
    cb = [_mm_nt(cg[g], bg[g]) for g in range(2)]
    heads = range(N_HEADS)
    dt_row = [jnp.sum(jnp.where(head == h, dt_t, 0.0), axis=0, keepdims=True) for h in heads]
    a_row = [jnp.sum(jnp.where(head == h, acs_t, 0.0), axis=0, keepdims=True) for h in heads]
    a_rb = [jnp.broadcast_to(a_row[h], (BLK, BLK)) for h in heads]
    a_b = [a_rb[h].T for h in heads]
    a_last = [jnp.sum(jnp.where(last, a_b[h], 0.0), axis=0, keepdims=True) for h in heads]
    w = [cb[h // 4] * jnp.exp(jnp.where(causal, a_b[h] - a_rb[h], -1e30)) * dt_row[h] for h in heads]
    f_b = [jnp.broadcast_to(dt_row[h] * jnp.exp(a_last[h] - a_row[h]), (BLK, BLK)).T for h in heads]
    y_in = [_mm(w[h], xs[h]) for h in heads]
    y_off = [_mm(cg[h // 4], prev[h]) * jnp.exp(a_b[h]) for h in heads]
    st = [_mm_tn(bg[h // 4], xs[h] * f_b[h]) for h in heads]
    ys = [y_in[h] + y_off[h] + d_rows[h] * xs[h] for h in heads]
    hs = [prev[h] * jnp.exp(a_last[h]) + st[h] for h in heads]
    return tuple(ys), tuple(hs)


def _ssd_chunk_bwd(xs, bg, cg, dt_raw_t, prev, dtb, alog, d_rows, dys, dhs, consts):
    causal, upper, last, head = consts
    heads, groups = range(N_HEADS), range(2)
    lane = _lane_iota((BLK, BLK))
    lane_row = _lane_iota((1, BLK))
    pre_dt = dt_raw_t + dtb
    dt_t = _softplus(pre_dt)
    a_neg = -jnp.exp(alog)
    acs_t = _mm_hi(dt_t * a_neg, upper)
    pick = lambda t, h: jnp.sum(jnp.where(head == h, t, 0.0), axis=0, keepdims=True)
    full_sum = lambda t: jnp.sum(jnp.sum(t, axis=1, keepdims=True), axis=0, keepdims=True)
    dt_row = [pick(dt_t, h) for h in heads]
    a_row = [pick(acs_t, h) for h in heads]
    a_rb = [jnp.broadcast_to(a_row[h], (BLK, BLK)) for h in heads]
    a_b = [a_rb[h].T for h in heads]
    a_last = [jnp.sum(jnp.where(last, a_b[h], 0.0), axis=0, keepdims=True) for h in heads]
    lm = [jnp.exp(jnp.where(causal, a_b[h] - a_rb[h], -1e30)) for h in heads]
    cgb = [cg[g].astype(MXU_DTYPE) for g in groups]
    bgb = [bg[g].astype(MXU_DTYPE) for g in groups]
    cb = [_mm_nt(cgb[g], bgb[g]) for g in groups]
    u = [cb[h // 4] * lm[h] for h in heads]
    w = [(u[h] * dt_row[h]).astype(MXU_DTYPE) for h in heads]
    e_row = [jnp.exp(a_last[h] - a_row[h]) for h in heads]
    f_row = [dt_row[h] * e_row[h] for h in heads]
    f_b = [jnp.broadcast_to(f_row[h], (BLK, BLK)).T for h in heads]
    e_b = [jnp.exp(a_b[h]) for h in heads]
    el = [jnp.exp(a_last[h]) for h in heads]
    xb = [xs[h].astype(MXU_DTYPE) for h in heads]
    dyb = [dys[h].astype(MXU_DTYPE) for h in heads]
    prevb = [prev[h].astype(MXU_DTYPE) for h in heads]
    dstb = [dhs[h].astype(MXU_DTYPE) for h in heads]
    gmat = [_mm(cgb[h // 4], prevb[h]) for h in heads]
    dw = [_mm_nt(dyb[h], xb[h]) for h in heads]
    dg = [dys[h] * e_b[h] for h in heads]
    dgb = [dg[h].astype(MXU_DTYPE) for h in heads]
    dxf = [_mm(bgb[h // 4], dstb[h]) for h in heads]
    xfb = [(xs[h] * f_b[h]).astype(MXU_DTYPE) for h in heads]
    dxs = [_mm_tn(w[h], dyb[h]) + d_rows[h] * dys[h] + f_b[h] * dxf[h] for h in heads]
    dd_rows = [jnp.sum(dys[h] * xs[h], axis=0, keepdims=True) for h in heads]
    dprev = [_mm_tn(cgb[h // 4], dgb[h]) + dhs[h] * el[h] for h in heads]
    dcg_h = [_mm_nt(dgb[h], prevb[h]) for h in heads]
    dbg_h = [_mm_nt(xfb[h], dstb[h]) for h in heads]
    zt = [dw[h] * u[h] for h in heads]
    dseg = [zt[h] * dt_row[h] for h in heads]
    dcb_h = [dw[h] * lm[h] * dt_row[h] for h in heads]
    dcb = [(dcb_h[4 * g] + dcb_h[4 * g + 1] + dcb_h[4 * g + 2] + dcb_h[4 * g + 3]).astype(MXU_DTYPE) for g in groups]
    dcg = [dcg_h[4 * g] + dcg_h[4 * g + 1] + dcg_h[4 * g + 2] + dcg_h[4 * g + 3] + _mm(dcb[g], bgb[g]) for g in groups]
    dbg = [dbg_h[4 * g] + dbg_h[4 * g + 1] + dbg_h[4 * g + 2] + dbg_h[4 * g + 3] + _mm_tn(dcb[g], cgb[g])
           for g in groups]
    r1 = [jnp.sum(dg[h] * gmat[h] + dseg[h], axis=1, keepdims=True) for h in heads]
    r2 = [jnp.sum(dxf[h] * xs[h], axis=1, keepdims=True) for h in heads]
    tt = [jnp.where(lane < HALF, jnp.broadcast_to(r1[h], (BLK, BLK)), jnp.broadcast_to(r2[h], (BLK, BLK))).T
          for h in heads]
    r1_row = [tt[h][0:1, :] for h in heads]
    r2_row = [tt[h][HALF:HALF + 1, :] for h in heads]
    d_el = [full_sum(dhs[h] * prev[h]) for h in heads]
    da_last = [jnp.sum(r2_row[h] * f_row[h], axis=1, keepdims=True) + el[h] * d_el[h] for h in heads]
    da_row = [r1_row[h] - jnp.sum(dseg[h], axis=0, keepdims=True) - r2_row[h] * f_row[h]
              + jnp.where(lane_row == BLK - 1, da_last[h], 0.0) for h in heads]
    ddt_row = [jnp.sum(zt[h], axis=0, keepdims=True) + r2_row[h] * e_row[h] for h in heads]
    da_t = jnp.zeros((N_HEADS, BLK), F32)
    ddt_t = jnp.zeros((N_HEADS, BLK), F32)
    for h in heads:
        da_t = jnp.where(head == h, da_row[h], da_t)
        ddt_t = jnp.where(head == h, ddt_row[h], ddt_t)
    d_dta = _mm_hi(da_t, causal.astype(F32))
    dalog = d_dta * dt_t * a_neg
    draw = (ddt_t + d_dta * a_neg) * jax.nn.sigmoid(pre_dt)
    return dxs, dbg, dcg, draw, dprev, draw, dalog, dd_rows


def _dt_rows(dt_blk):
    return dt_blk.T[:N_HEADS]


def _silu_grad(x):
    s = jax.nn.sigmoid(x)
    return s * (1.0 + x * (1.0 - s))


def _conv_pre(halo, blk, cw_ref, cb_ref):
    ext = jnp.concatenate([halo, blk], axis=0)
    taps = [pltpu.roll(ext, 3 - k, 0)[8:] for k in range(3)] + [blk]
    pre = cb_ref[...] + cw_ref[0:1, :] * taps[0]
    for k in range(1, 4):
        pre = pre + cw_ref[k:k + 1, :] * taps[k]
    return pre, taps


def _ssd_split(pre):
    heads = _split_heads(pre[:, :SSM_W], 4)
    pb = [pre[:, SSM_W + g * D_STATE:SSM_W + (g + 1) * D_STATE] for g in range(2)]
    pc = [pre[:, SSM_W + 2 * D_STATE + g * D_STATE:SSM_W + 2 * D_STATE + (g + 1) * D_STATE] for g in range(2)]
    return heads, pb, pc


def _ssd_fwd(xbc, dt_raw, conv_w, conv_b, dtb_row, alog_row, d_exp, xchg):
    S = xbc.shape[0]
    nc = S // BLK

    def body(xbc_ref, halo_ref, dt_ref, cw_ref, cb_ref, dtb_ref, alog_ref, d_ref, y_ref, prev_ref, state_ref):
        i = pl.program_id(0)

        @pl.when(i == 0)
        def _():
            state_ref[...] = jnp.zeros_like(state_ref)

        halo = halo_ref[...] * jnp.where(i > 0, 1.0, 0.0)
        pre, _ = _conv_pre(halo, xbc_ref[...], cw_ref, cb_ref)
        heads, pb, pc = _ssd_split(_silu(pre))
        prev = [state_ref[h] for h in range(N_HEADS)]
        for h in range(N_HEADS):
            prev_ref[0, h] = prev[h]
        d_rows = [d_ref[h:h + 1, :] for h in range(N_HEADS)]
        ys, hs = _ssd_chunk(heads, pb, pc, _dt_rows(dt_ref[...]), prev, dtb_ref[...], alog_ref[...], d_rows,
                            _ssd_consts())
        for h in range(N_HEADS):
            state_ref[h] = hs[h]
        y_ref[...] = _join_heads(ys)

    vec = pl.BlockSpec((N_HEADS, LANE), _fixed)
    return _hosted_call(
        body, "ssd_fwd", nc,
        in_specs=[pl.BlockSpec((BLK, XBC_W), _row),
                  pl.BlockSpec((8, XBC_W), lambda i: (jnp.maximum(i * (BLK // 8) - 1, 0), 0)),
                  pl.BlockSpec((BLK, LANE), _row),
                  pl.BlockSpec((4, XBC_W), _fixed), pl.BlockSpec((1, XBC_W), _fixed), vec, vec,
                  pl.BlockSpec((N_HEADS, LANE), _fixed)],
        out_specs=[pl.BlockSpec((BLK, SSM_W), _row),
                   pl.BlockSpec((1, N_HEADS, D_STATE, LANE), lambda i: (i, 0, 0, 0))],
        out_shape=[jax.ShapeDtypeStruct((S, SSM_W), F32), jax.ShapeDtypeStruct((nc, N_HEADS, D_STATE, LANE), F32)],
        scratch_shapes=[pltpu.VMEM((N_HEADS, D_STATE, LANE), F32)],
        args=(xbc, xbc, dt_raw, conv_w, conv_b, dtb_row, alog_row, d_exp), xchg=xchg, cparams=_cparams(),
    )


def _ssd_bwd(xbc, dt_raw, prev_states, dy, conv_w, conv_b, dtb_row, alog_row, d_exp, xchg):
    S = xbc.shape[0]
    nc = S // BLK

    def body(xbc_ref, halo_ref, dt_ref, prev_ref, dy_ref, cw_ref, cb_ref, dtb_ref, alog_ref, d_ref,
             dxbc_ref, ddt_ref, dcw_ref, dvec_ref, dd_ref, gstate_ref, ghalo_ref):
        i = pl.program_id(0)
        c = nc - 1 - i

        @pl.when(i == 0)
        def _():
            gstate_ref[...] = jnp.zeros_like(gstate_ref)
            ghalo_ref[...] = jnp.zeros_like(ghalo_ref)
            dcw_ref[...] = jnp.zeros_like(dcw_ref)
            dvec_ref[...] = jnp.zeros_like(dvec_ref)
            dd_ref[...] = jnp.zeros_like(dd_ref)

        halo = halo_ref[...] * jnp.where(c > 0, 1.0, 0.0)
        pre, taps = _conv_pre(halo, xbc_ref[...], cw_ref, cb_ref)
        heads, pb, pc = _ssd_split(_silu(pre))
        prev = [prev_ref[0, h] for h in range(N_HEADS)]
        d_rows = [d_ref[h:h + 1, :] for h in range(N_HEADS)]
        dys = _split_heads(dy_ref[...], 4)
        dhs = [gstate_ref[h] for h in range(N_HEADS)]
        dheads, dpb, dpc, ddt_t, dprev, ddtb, dalog, dd_rows = _ssd_chunk_bwd(
            heads, pb, pc, _dt_rows(dt_ref[...]), prev, dtb_ref[...], alog_ref[...], d_rows, dys, dhs, _ssd_consts())
        for h in range(N_HEADS):
            gstate_ref[h] = dprev[h]
            dd_ref[h:h + 1, :] += dd_rows[h]
        ddt_ref[...] = jnp.concatenate([ddt_t, jnp.zeros((BLK - N_HEADS, BLK), F32)], axis=0).T
        dvec_ref[0:N_HEADS, :] += ddtb
        dvec_ref[N_HEADS:, :] += dalog
        dpre = jnp.concatenate([_join_heads(dheads)] + list(dpb) + list(dpc), axis=1) * _silu_grad(pre)
        zeros8 = jnp.zeros((8, XBC_W), F32)
        dpe = jnp.concatenate([zeros8, dpre, zeros8], axis=0)
        n_ext = 16 + BLK
        dext = cw_ref[3:4, :] * dpe[:8 + BLK]
        dcw_ref[3:4, :] += jnp.sum(dpre * taps[3], axis=0, keepdims=True)
        for k in range(3):
            dext = dext + cw_ref[k:k + 1, :] * pltpu.roll(dpe, n_ext - (3 - k), 0)[:8 + BLK]
            dcw_ref[k:k + 1, :] += jnp.sum(dpre * taps[k], axis=0, keepdims=True)
        dcw_ref[4:5, :] += jnp.sum(dpre, axis=0, keepdims=True)
        dxbc_ref[...] = dext[8:, :]
        dxbc_ref[BLK - 8:BLK, :] += ghalo_ref[...]
        ghalo_ref[...] = dext[:8, :]

    vec = pl.BlockSpec((N_HEADS, LANE), _fixed)
    rev = lambda i: (nc - 1 - i, 0)
    return _hosted_call(
        body, "ssd_bwd", nc,
        in_specs=[pl.BlockSpec((BLK, XBC_W), rev),
                  pl.BlockSpec((8, XBC_W), lambda i: (jnp.maximum((nc - 1 - i) * (BLK // 8) - 1, 0), 0)),
                  pl.BlockSpec((BLK, LANE), rev),
                  pl.BlockSpec((1, N_HEADS, D_STATE, LANE), lambda i: (nc - 1 - i, 0, 0, 0)),
                  pl.BlockSpec((BLK, SSM_W), rev),
                  pl.BlockSpec((4, XBC_W), _fixed), pl.BlockSpec((1, XBC_W), _fixed), vec, vec,
                  pl.BlockSpec((N_HEADS, LANE), _fixed)],
        out_specs=[pl.BlockSpec((BLK, XBC_W), rev), pl.BlockSpec((BLK, LANE), rev),
                   pl.BlockSpec((8, XBC_W), _fixed), pl.BlockSpec((2 * N_HEADS, LANE), _fixed),
                   pl.BlockSpec((N_HEADS, LANE), _fixed)],
        out_shape=[jax.ShapeDtypeStruct((S, XBC_W), F32), jax.ShapeDtypeStruct((S, LANE), F32),
                   jax.ShapeDtypeStruct((8, XBC_W), F32), jax.ShapeDtypeStruct((2 * N_HEADS, LANE), F32),
                   jax.ShapeDtypeStruct((N_HEADS, LANE), F32)],
        scratch_shapes=[pltpu.VMEM((N_HEADS, D_STATE, LANE), F32), pltpu.VMEM((8, XBC_W), F32)],
        args=(xbc, xbc, dt_raw, prev_states, dy, conv_w, conv_b, dtb_row, alog_row, d_exp), xchg=xchg,
        cparams=_cparams(VMEM_BIG),
    )


def _adamw_math(w, g, m, v):
    m = ADAM_B1 * m + (1.0 - ADAM_B1) * g
    v = ADAM_B2 * v + (1.0 - ADAM_B2) * jnp.square(g)
    m_hat = m / (1.0 - ADAM_B1 ** ADAM_STEP)
    v_hat = v / (1.0 - ADAM_B2 ** ADAM_STEP)
    delta = -ADAM_LR * (m_hat / (jnp.sqrt(v_hat) + ADAM_EPS) + ADAM_WD * w)
    return delta, m, v


def _reduce_adamw(parts, w, m, v, name):
    R, C = w.shape

    def body(p_ref, w_ref, m_ref, v_ref, g_ref, d_ref, nm_ref, nv_ref):
        g = p_ref[0].astype(F32)
        for i in range(1, N_DEV):
            g = g + p_ref[i].astype(F32)
        d, nm, nv = _adamw_math(w_ref[...], g, m_ref[...], v_ref[...])
        g_ref[...] = g
        d_ref[...] = d
        nm_ref[...] = nm
        nv_ref[...] = nv

    if R % 16 == 0:
        tr = max(t for t in range(16, 257, 16) if R % t == 0)
        n, blk, pblk = R // tr, pl.BlockSpec((tr, C), _row), pl.BlockSpec((N_DEV, tr, C), lambda i: (0, i, 0))
    else:
        tl = 256
        n, blk, pblk = C // tl, pl.BlockSpec((R, tl), lambda i: (0, i)), pl.BlockSpec((N_DEV, R, tl),
                                                                                      lambda i: (0, 0, i))
    return pl.pallas_call(
        body, name=name, grid=(n,), in_specs=[pblk, blk, blk, blk],
        out_specs=[blk] * 4, out_shape=[jax.ShapeDtypeStruct((R, C), F32)] * 4,
    )(parts, w, m, v)


def _reduce_adamw_hosting(parts_list, wmv_list, name, xchg):
    n_arr = len(parts_list)
    C = wmv_list[0][0].shape[1]
    tl = 256

    def body(*refs):
        p_refs, wmv_refs, o_refs = refs[:n_arr], refs[n_arr:4 * n_arr], refs[4 * n_arr:]
        for k in range(n_arr):
            g = p_refs[k][0].astype(F32)
            for i in range(1, N_DEV):
                g = g + p_refs[k][i].astype(F32)
            w_ref, m_ref, v_ref = wmv_refs[3 * k:3 * k + 3]
            d, nm, nv = _adamw_math(w_ref[...], g, m_ref[...], v_ref[...])
            for o, val in zip(o_refs[4 * k:4 * k + 4], (g, d, nm, nv)):
                o[...] = val

    in_specs = [pl.BlockSpec((N_DEV, w.shape[0], tl), lambda i: (0, 0, i)) for w, _, _ in wmv_list]
    in_specs += [pl.BlockSpec((w.shape[0], tl), lambda i: (0, i)) for w, _, _ in wmv_list for _ in range(3)]
    out_specs = [pl.BlockSpec((w.shape[0], tl), lambda i: (0, i)) for w, _, _ in wmv_list for _ in range(4)]
    out_shape = [jax.ShapeDtypeStruct(w.shape, F32) for w, _, _ in wmv_list for _ in range(4)]
    args = list(parts_list) + [a for wmv in wmv_list for a in wmv]
    outs, x_out = _hosted_call(body, name, C // tl, in_specs, out_specs, out_shape, [], args, xchg,
                               _cparams(VMEM_BIG))
    return [outs[4 * k:4 * k + 4] for k in range(n_arr)], x_out


_SMALL_NAMES = ("ada_b", "norm1", "conv_w", "conv_b", "dt_bias", "A_log", "D_skip", "sinks", "attn_out_norm",
                "ssm_out_norm", "norm2", "rel_bias", "final_norm")
N_MOD = 6 * D_MODEL


def _mod_row(a0, a1, a2):
    return jnp.concatenate([a0[2:3], a0[1:2], a1[0:1], a2[2:3], a2[1:2], a2[3:4]], axis=1)


def _small_update(gathered, params):
    n_g = len(gathered)
    flat = [a for name in _SMALL_NAMES for a in params[name]]

    def body(*refs):
        a0_ref, a1_ref, a2_ref, cw_ref, dv_ref, dd_ref, ds_ref, dr_ref, c_ref = refs[:n_g]
        wmv = refs[n_g:n_g + len(flat)]
        outs = refs[n_g + len(flat):]

        def total(ref):
            t = ref[0]
            for i in range(1, N_DEV):
                t = t + ref[i]
            return t

        t0, t1, t2, tcw, tdv, tdd, tds, tdr = [total(r) for r in (a0_ref, a1_ref, a2_ref, cw_ref, dv_ref, dd_ref,
                                                                   ds_ref, dr_ref)]
        r8 = lax.broadcasted_iota(jnp.int32, (N_HEADS, LANE), 0)
        l8 = lax.broadcasted_iota(jnp.int32, (N_HEADS, LANE), 1)

        def diag_row(t):
            return jnp.sum(jnp.where(r8 == l8, t, 0.0), axis=0, keepdims=True)[:, :N_HEADS]

        def lane_sums(t):
            return diag_row(jnp.broadcast_to(jnp.sum(t, axis=1, keepdims=True), (N_HEADS, LANE)))

        me = _lin(_my_pos())
        n_cw = XBC_W // N_DEV
        cw_mine = jnp.zeros((4, n_cw), F32)
        for j in range(N_DEV):
            cw_mine = cw_mine + tcw[0:4, j * n_cw:(j + 1) * n_cw] * jnp.where(me == j, 1.0, 0.0)
        grads = {
            "ada_b": _mod_row(t0, t1, t2), "norm1": t0[0:1], "conv_w": cw_mine, "conv_b": tcw[4:5],
            "dt_bias": lane_sums(tdv[:N_HEADS]), "A_log": lane_sums(tdv[N_HEADS:]), "D_skip": lane_sums(tdd),
            "sinks": diag_row(tds), "attn_out_norm": t1[1:2, :ATTN_W], "ssm_out_norm": t1[1:2, ATTN_W:],
            "norm2": t2[0:1], "rel_bias": tdr[:, :N_HEADS], "final_norm": t2[4:5],
        }
        for k, name in enumerate(_SMALL_NAMES):
            w_ref, m_ref, v_ref = wmv[3 * k:3 * k + 3]
            g = grads[name]
            d, nm, nv = _adamw_math(w_ref[...], g, m_ref[...], v_ref[...])
            for o, val in zip(outs[4 * k:4 * k + 4], (g, d, nm, nv)):
                o[...] = val
        loss_ref, call_ref, dmod_ref = outs[4 * len(_SMALL_NAMES):]
        loss_ref[...] = t2[5:6, 0:1]
        call_ref[...] = jnp.concatenate([c_ref[i] for i in range(N_DEV)], axis=0)
        dmod_ref[...] = jnp.concatenate([_mod_row(a0_ref[i], a1_ref[i], a2_ref[i]) for i in range(N_DEV)], axis=0)

    out_shape = [jax.ShapeDtypeStruct(params[name][0].shape, F32) for name in _SMALL_NAMES for _ in range(4)]
    out_shape += [jax.ShapeDtypeStruct((1, 1), F32), jax.ShapeDtypeStruct((N_DEV, D_MODEL), F32),
                  jax.ShapeDtypeStruct((N_DEV, N_MOD), F32)]
    res = pl.pallas_call(body, name="small_update", out_shape=out_shape)(*gathered, *flat)
    upd = {name: res[4 * k:4 * k + 4] for k, name in enumerate(_SMALL_NAMES)}
    loss, c_all, dmod_all = res[4 * len(_SMALL_NAMES):]
    return upd, loss, c_all, dmod_all


def _ada_w_update(c_all, dmod_all, w, m, v):
    chunk = w.shape[1]

    def body(c_ref, dm_ref, w_ref, m_ref, v_ref, g_ref, d_ref, nm_ref, nv_ref):
        me = _lin(_my_pos())
        dm = jnp.zeros((N_DEV, chunk), F32)
        for j in range(N_DEV):
            dm = dm + dm_ref[:, j * chunk:(j + 1) * chunk] * jnp.where(me == j, 1.0, 0.0)
        g = lax.dot_general(_silu(c_ref[...]), dm, (((0,), (0,)), ((), ())), precision=HI,
                            preferred_element_type=F32)
        d, nm, nv = _adamw_math(w_ref[...], g, m_ref[...], v_ref[...])
        g_ref[...] = g
        d_ref[...] = d
        nm_ref[...] = nm
        nv_ref[...] = nv

    return pl.pallas_call(body, name="ada_w_update", out_shape=[jax.ShapeDtypeStruct(w.shape, F32)] * 4,
                          compiler_params=_cparams(VMEM_BIG))(c_all, dmod_all, w, m, v)


def _local_step(x, tgt, c, mod, w_in, conv_w, w_o_mine, w_gu_mine, w_d_mine, p):
    S = x.shape[0]
    tm = min(512, S)
    tmm = min(256, S)
    tw = min(2048, S)
    shift1, scale1, gate1, shift2, scale2, gate2 = [mod[i:i + 1] for i in range(6)]
    buckets = jnp.asarray(_t5_bucket_table())
    per_head = lambda a: jnp.broadcast_to(a.reshape(N_HEADS, 1), (N_HEADS, LANE))
    dtb_row, alog_row, d_exp = per_head(p["dt_bias"]), per_head(p["A_log"]), per_head(p["D_skip"])
    sinks = p["sinks"].reshape(N_HEADS)

    d_cut, gu_cut = WD_CUT, WGU_CUTS
    (qkv, z, xbc, dt_raw), (g_d_a,) = _in_proj_fwd(x, p["norm1"], scale1, shift1, w_in, tm,
                                                   ([w_d_mine[:d_cut]], False))
    bias = _attn_bias(buckets, p["rel_bias"])
    (ya,), (g_gu_a,) = _attn_fwd(qkv, bias, sinks, ([w_gu_mine[:gu_cut[0]]], False))
    (ys, prev_states), (g_gu_b, g_o) = _ssd_fwd(xbc, dt_raw, conv_w, p["conv_b"], dtb_row, alog_row, d_exp,
                                                ([w_gu_mine[gu_cut[0]:gu_cut[1]], w_o_mine], False))
    w_o = g_o.reshape(D_MODEL, D_MODEL)
    x1, (g_gu_c, g_d_b) = _out_proj_fwd(x, ya, ys, z, p["attn_out_norm"], p["ssm_out_norm"], gate1, w_o, tm,
                                        ([w_gu_mine[gu_cut[1]:], w_d_mine[d_cut:]], False))
    dx1, h2, dgu, act, dmlp, acc2 = _mlp_loss(x1, tgt, p["norm2"], scale2, shift2, gate2, p["final_norm"],
                                              (g_gu_a, g_gu_b, g_gu_c), (g_d_a, g_d_b), tmm)
    g_w_gu = _wgrad(dgu, h2, 2 * D_FF // 4, tw, "wgrad_gate_up")
    g_w_d = _wgrad(act, dmlp, D_FF // 2, tw, "wgrad_down")
    dya, dys, dz, u, dmix, acc1 = _out_proj_bwd(dx1, ya, ys, z, p["attn_out_norm"], p["ssm_out_norm"], gate1, w_o, tm)
    g_w_o = _wgrad(u, dmix, D_MODEL, tw, "wgrad_out")
    (dq, dkv, dbias, dsk), (r_d,) = _attn_bwd(qkv, ya, dya, bias, sinks,
                                              ([g_w_d.reshape(N_DEV, D_FF // N_DEV, D_MODEL)], True))
    drel, dsink = _attn_finish(dbias, dsk, buckets)
    (dxbc, ddt, dcw, dvec, dd), (r_gu, r_o) = _ssd_bwd(
        xbc, dt_raw, prev_states, dys, conv_w, p["conv_b"], dtb_row, alog_row, d_exp,
        ([g_w_gu.reshape(N_DEV, 2 * D_FF // N_DEV, D_MODEL), g_w_o.reshape(N_DEV, D_MODEL // N_DEV, D_MODEL)], True))
    gx, h1, dproj, acc0 = _in_proj_bwd(x, dx1, dq, dkv, dz, dxbc, ddt, p["norm1"], scale1, shift1, w_in, tm)
    half = D_MODEL // 2
    slots = lambda g: g[:IN_W].reshape(N_DEV, IN_W // N_DEV, half)
    g_in_a, gathered = _wgrad(dproj, h1, IN_PAD, tw, "wgrad_in_a",
                              ([acc0, acc1, acc2, dcw, dvec, dd, dsink, drel, c], False), g_cols=(half, 0))
    g_in_b, (r_in_a,) = _wgrad(dproj, h1, IN_PAD, tw, "wgrad_in_b", ([slots(g_in_a)], True), g_cols=(half, 1))
    return gx, (r_in_a, slots(g_in_b)), (r_o, r_gu, r_d), gathered


def kernel(x, c, ada_w, ada_b, norm1, w_in, conv_w, conv_b, dt_bias, A_log, D_skip, sinks, attn_out_norm, ssm_out_norm, w_o, norm2, w_gate_up, w_down, rel_bias, final_norm, loss_target, m_ada_w, m_ada_b, m_norm1, m_w_in, m_conv_w, m_conv_b, m_dt_bias, m_A_log, m_D_skip, m_sinks, m_attn_out_norm, m_ssm_out_norm, m_w_o, m_norm2, m_w_gate_up, m_w_down, m_rel_bias, m_final_norm, v_ada_w, v_ada_b, v_norm1, v_w_in, v_conv_w, v_conv_b, v_dt_bias, v_A_log, v_D_skip, v_sinks, v_attn_out_norm, v_ssm_out_norm, v_w_o, v_norm2, v_w_gate_up, v_w_down, v_rel_bias, v_final_norm):
    two_d = lambda a: a if a.ndim == 2 else a.reshape(-1, a.shape[-1])
    small_params = dict(
        ada_b=(ada_b, m_ada_b, v_ada_b), norm1=(norm1, m_norm1, v_norm1), conv_w=(conv_w, m_conv_w, v_conv_w),
        conv_b=(conv_b, m_conv_b, v_conv_b), dt_bias=(dt_bias, m_dt_bias, v_dt_bias), A_log=(A_log, m_A_log, v_A_log),
        D_skip=(D_skip, m_D_skip, v_D_skip), sinks=(sinks, m_sinks, v_sinks),
        attn_out_norm=(attn_out_norm, m_attn_out_norm, v_attn_out_norm),
        ssm_out_norm=(ssm_out_norm, m_ssm_out_norm, v_ssm_out_norm), norm2=(norm2, m_norm2, v_norm2),
        rel_bias=(rel_bias, m_rel_bias, v_rel_bias), final_norm=(final_norm, m_final_norm, v_final_norm))
    small_params = {k: tuple(two_d(a) for a in v) for k, v in small_params.items()}
    S = x.shape[1]
    xs, tgt = x.reshape(S, D_MODEL), loss_target.reshape(S, D_MODEL)
    ada_w2 = ada_w[0]
    chunk = ada_w2.shape[1]
    t_in = [jnp.transpose(a[0]) for a in (w_in, m_w_in, v_w_in)]
    t_gu = [jnp.transpose(a[0]) for a in (w_gate_up, m_w_gate_up, v_w_gate_up)]

    mod = _mod_exchange(c, ada_w2, ada_b.reshape(N_DEV, chunk)).reshape(6, D_MODEL)

    g_in, g_cw = _exchange([t_in[0].astype(WIRE_DTYPE), conv_w[0]], scatter=False, name="gather_w_in")
    w_in_full = jnp.pad(g_in.reshape(IN_W, D_MODEL), ((0, IN_PAD - IN_W), (0, 0)))
    conv_w_full = jnp.transpose(g_cw, (1, 0, 2)).reshape(4, XBC_W)

    p = {k: v[0] for k, v in small_params.items()}
    gx, (r_in_a, gw_in_b), (r_o, r_gu, r_d), gathered = _local_step(
        xs, tgt, c, mod, w_in_full, conv_w_full, w_o[0].astype(WIRE_DTYPE), t_gu[0].astype(WIRE_DTYPE),
        w_down[0].astype(WIRE_DTYPE), p)

    (u_gu, u_d, u_o), (r_in_b,) = _reduce_adamw_hosting(
        [r_gu, r_d, r_o], [tuple(t_gu), (w_down[0], m_w_down[0], v_w_down[0]), (w_o[0], m_w_o[0], v_w_o[0])],
        "adamw_big", ([gw_in_b], True))
    r_in = jnp.concatenate([r_in_a, r_in_b], axis=2)

    small, loss, c_all, dmod_all = _small_update(gathered, small_params)

    big = {
        "ada_w": _ada_w_update(c_all, dmod_all, ada_w2, m_ada_w[0], v_ada_w[0]),
        "w_in": [jnp.transpose(a) for a in _reduce_adamw(r_in, *t_in, "adamw_w_in")],
        "w_o": u_o,
        "w_gate_up": [jnp.transpose(a) for a in u_gu],
        "w_down": u_d,
    }
    big.update(small)

    order = ['ada_w', 'ada_b', 'norm1', 'w_in', 'conv_w', 'conv_b', 'dt_bias', 'A_log', 'D_skip', 'sinks',
             'attn_out_norm', 'ssm_out_norm', 'w_o', 'norm2', 'w_gate_up', 'w_down', 'rel_bias', 'final_norm']
    shapes = dict(ada_w=ada_w.shape, ada_b=ada_b.shape, norm1=norm1.shape, w_in=w_in.shape, conv_w=conv_w.shape,
                  conv_b=conv_b.shape, dt_bias=dt_bias.shape, A_log=A_log.shape, D_skip=D_skip.shape,
                  sinks=sinks.shape, attn_out_norm=attn_out_norm.shape, ssm_out_norm=ssm_out_norm.shape,
                  w_o=w_o.shape, norm2=norm2.shape, w_gate_up=w_gate_up.shape, w_down=w_down.shape,
                  rel_bias=rel_bias.shape, final_norm=final_norm.shape)
    outs = [[], [], [], []]
    for name in order:
        for kind in range(4):
            outs[kind].append(big[name][kind].reshape(shapes[name]))
    return (loss.reshape(()), gx.reshape(x.shape), *outs[0], *outs[1], *outs[2], *outs[3])
```

```python
import functools

import numpy as np
import jax
import jax.numpy as jnp
from jax import lax
from jax.experimental import pallas as pl
from jax.experimental.pallas import tpu as pltpu

F32 = jnp.float32
MXU_DTYPE = jnp.bfloat16
WIRE_DTYPE = jnp.bfloat16
HI = lax.Precision.HIGHEST
MESH = pl.DeviceIdType.MESH
N_DEV = 8

D_MODEL = 1024
ATTN_W = 512
KV_W = 128
SSM_W = 512
XBC_W = 1024
N_HEADS = 8
D_STATE = 128
D_FF = 2816
IN_W = 2312
IN_PAD = 2432
BLK = 128
N_BUCKETS = 32
EPS = 1e-6
LANE = 128
HALF = 64

ADAM_LR, ADAM_B1, ADAM_B2, ADAM_EPS, ADAM_WD, ADAM_STEP = 0.001, 0.9, 0.999, 1e-08, 0.01, 10

VMEM_BIG = 56 * 1024 * 1024
WD_CUT = 256
WGU_CUTS = (304, 608)


def _cparams(vmem=None):
    if vmem is None:
        return pltpu.CompilerParams()
    return pltpu.CompilerParams(vmem_limit_bytes=vmem)


def _mm(a, b):
    return jnp.dot(a.astype(MXU_DTYPE), b.astype(MXU_DTYPE), preferred_element_type=F32)


def _mm_nt(a, b):
    return lax.dot_general(a.astype(MXU_DTYPE), b.astype(MXU_DTYPE), (((1,), (1,)), ((), ())),
                           preferred_element_type=F32)


def _mm_tn(a, b):
    return lax.dot_general(a.astype(MXU_DTYPE), b.astype(MXU_DTYPE), (((0,), (0,)), ((), ())),
                           preferred_element_type=F32)


def _mm_hi(a, b):
    return jnp.dot(a, b, precision=HI, preferred_element_type=F32)


def _silu(x):
    return x * jax.nn.sigmoid(x)


def _softplus(x):
    return jnp.maximum(x, 0.0) + jnp.log1p(jnp.exp(-jnp.abs(x)))


def _rms(x, g, n):
    return x * lax.rsqrt(jnp.sum(x * x, axis=-1, keepdims=True) * (1.0 / n) + EPS) * g


def _modnorm(x, g, scale, shift):
    return _rms(x, g, x.shape[-1]) * (1.0 + scale) + shift


def _modnorm_parts(x):
    r = lax.rsqrt(jnp.sum(x * x, axis=-1, keepdims=True) * (1.0 / x.shape[-1]) + EPS)
    return r, x * r


def _modnorm_bwd(r, xhat, g, scale, dy):
    dyg = dy * (g * (1.0 + scale))
    c = jnp.sum(dyg * xhat, axis=-1, keepdims=True) * (1.0 / xhat.shape[-1])
    dx = r * (dyg - xhat * c)
    ct = jnp.sum(dy * xhat, axis=0, keepdims=True)
    return dx, ct * (1.0 + scale), ct * g, jnp.sum(dy, axis=0, keepdims=True)


def _lane_iota(shape):
    return lax.broadcasted_iota(jnp.int32, shape, len(shape) - 1)


def _split_pair(t):
    lane = _lane_iota(t.shape)
    lo = jnp.where(lane < HALF, t, 0.0)
    hi = pltpu.roll(jnp.where(lane >= HALF, t, 0.0), HALF, 1)
    return lo, hi


def _join_pair(lo, hi):
    lane = _lane_iota(lo.shape)
    return jnp.where(lane < HALF, lo, pltpu.roll(hi, HALF, 1))


def _split_heads(t, n_pairs):
    out = []
    for p in range(n_pairs):
        out.extend(_split_pair(t[:, p * LANE:(p + 1) * LANE]))
    return out


def _join_heads(hs):
    return jnp.concatenate([_join_pair(hs[2 * p], hs[2 * p + 1]) for p in range(len(hs) // 2)], axis=1)


def _t5_bucket_table():
    dist = np.arange(BLK)[:, None] + BLK - np.arange(2 * BLK)[None, :]
    n = np.maximum(dist, 0)
    max_exact = N_BUCKETS // 2
    large = max_exact + (np.log(np.maximum(n, 1) / max_exact) / np.log(128 / max_exact)
                         * (N_BUCKETS - max_exact)).astype(np.int32)
    large = np.minimum(large, N_BUCKETS - 1)
    return np.where(n < max_exact, n, large).astype(np.int32)


def _my_pos():
    return lax.axis_index("x"), lax.axis_index("y"), lax.axis_index("c")


def _peer(k):
    x, y, c = _my_pos()
    return (1 - x if k & 4 else x, 1 - y if k & 2 else y, 1 - c if k & 1 else c)


def _lin(pos):
    return 4 * pos[0] + 2 * pos[1] + pos[2]


def _xchg_copies(ins, outs, sems, scatter):
    local_sem, send_sem, recv_sem = sems
    me = _lin(_my_pos())
    local, remote = [], []
    for a in range(len(ins)):
        src = ins[a].at[me] if scatter else ins[a]
        local.append(pltpu.make_async_copy(src, outs[a].at[me], local_sem.at[a]))
    for k in range(1, N_DEV):
        peer = _peer(k)
        for a in range(len(ins)):
            src = ins[a].at[_lin(peer)] if scatter else ins[a]
            remote.append(pltpu.make_async_remote_copy(src, outs[a].at[me], send_sem.at[a, k - 1],
                                                       recv_sem.at[a, k - 1], device_id=peer, device_id_type=MESH))
    return local, remote


def _xchg_start(ins, outs, sems, scatter):
    local, remote = _xchg_copies(ins, outs, sems, scatter)
    for cp in local + remote:
        cp.start()


def _xchg_wait(ins, outs, sems, scatter):
    local, remote = _xchg_copies(ins, outs, sems, scatter)
    for cp in local:
        cp.wait()
    for cp in remote:
        cp.wait_send()
        cp.wait_recv()


def _xchg_shapes(arrs, scatter):
    n = len(arrs)
    if scatter:
        out_shape = [jax.ShapeDtypeStruct(a.shape, a.dtype) for a in arrs]
    else:
        out_shape = [jax.ShapeDtypeStruct((N_DEV,) + a.shape, a.dtype) for a in arrs]
    sems = [pltpu.SemaphoreType.DMA((n,)), pltpu.SemaphoreType.DMA((n, N_DEV - 1)),
            pltpu.SemaphoreType.DMA((n, N_DEV - 1))]
    return out_shape, sems


def _gather_two_level(arrs, name):
    n = len(arrs)
    out_shape = [jax.ShapeDtypeStruct((N_DEV,) + a.shape, a.dtype) for a in arrs]
    chips = (2, 4, 6)

    def body(*refs):
        ins, outs = refs[:n], refs[n:2 * n]
        local_sem, send_sem, recv_sem, fsend_sem, frecv_sem = refs[2 * n:]
        me = _lin(_my_pos())
        sibling = _peer(1)

        def direct(a, k):
            return pltpu.make_async_remote_copy(ins[a], outs[a].at[me], send_sem.at[a, k], recv_sem.at[a, k],
                                                device_id=_peer(k), device_id_type=MESH)

        def handed_on(a, j, origin):
            slot = outs[a].at[origin]
            return pltpu.make_async_remote_copy(slot, slot, fsend_sem.at[a, j], frecv_sem.at[a, j],
                                                device_id=sibling, device_id_type=MESH)

        local = [pltpu.make_async_copy(ins[a], outs[a].at[me], local_sem.at[a]) for a in range(n)]
        first = [direct(a, k) for k in (1,) + chips for a in range(n)]
        for cp in local + first:
            cp.start()
        passed = []
        for j, k in enumerate(chips):
            for a in range(n):
                direct(a, k).wait_recv()
                cp = handed_on(a, j, _lin(_peer(k)))
                cp.start()
                passed.append(cp)
        for a in range(n):
            direct(a, 1).wait_recv()
            for j, k in enumerate(chips):
                handed_on(a, j, _lin(_peer(k ^ 1))).wait_recv()
        for cp in local:
            cp.wait()
        for cp in first + passed:
            cp.wait_send()

    hbm = pl.BlockSpec(memory_space=pltpu.HBM)
    dma = pltpu.SemaphoreType.DMA
    return pl.pallas_call(body, name=name, out_shape=out_shape, in_specs=[hbm] * n, out_specs=[hbm] * n,
                          scratch_shapes=[dma((n,)), dma((n, N_DEV)), dma((n, N_DEV)), dma((n, 3)), dma((n, 3))],
                          )(*arrs)


def _hosted_call(body, name, grid, in_specs, out_specs, out_shape, scratch_shapes, args, xchg, cparams):
    arrs, scatter = xchg
    grid = (grid,) if isinstance(grid, int) else tuple(grid)
    n, n_in, n_out, n_scr = len(arrs), len(in_specs), len(out_specs), len(scratch_shapes)
    x_shape, x_sems = _xchg_shapes(arrs, scatter)

    def hosted(*refs):
        ins, refs = refs[:n_in], refs[n_in:]
        x_in, refs = refs[:n], refs[n:]
        outs, refs = refs[:n_out], refs[n_out:]
        x_out, refs = refs[:n], refs[n:]
        scr, sems = refs[:n_scr], refs[n_scr:]
        step = pl.program_id(0)
        for d in range(1, len(grid)):
            step = step * grid[d] + pl.program_id(d)

        @pl.when(step == 0)
        def _():
            _xchg_start(x_in, x_out, sems, scatter)

        body(*ins, *outs, *scr)

        @pl.when(step == int(np.prod(grid)) - 1)
        def _():
            _xchg_wait(x_in, x_out, sems, scatter)

    hbm = pl.BlockSpec(memory_space=pltpu.HBM)
    res = pl.pallas_call(
        hosted, name=name, grid=grid, in_specs=list(in_specs) + [hbm] * n,
        out_specs=list(out_specs) + [hbm] * n, out_shape=list(out_shape) + x_shape,
        scratch_shapes=list(scratch_shapes) + x_sems, compiler_params=cparams,
    )(*args, *arrs)
    return res[:n_out], res[n_out:]


def _mod_exchange(c, ada_w, ada_b8):
    chunk = ada_w.shape[1]

    def body(c_ref, w_ref, b_ref, out_ref, cbuf, part, s1, r1, s2, r2):
        me = _lin(_my_pos())
        first = []
        for k in range(1, N_DEV):
            cp = pltpu.make_async_remote_copy(c_ref, cbuf.at[me], s1.at[k - 1], r1.at[k - 1],
                                              device_id=_peer(k), device_id_type=MESH)
            cp.start()
            first.append(cp)
        cbuf[me] = c_ref[...]
        for cp in first:
            cp.wait_send()
            cp.wait_recv()
        cond = _silu(jnp.concatenate([cbuf[i] for i in range(N_DEV)], axis=0))
        mod = _mm_hi(cond, w_ref[...]) + b_ref[pl.ds(me, 1), :]
        for j in range(N_DEV):
            part[j] = mod[j:j + 1, :]
        second = []
        for k in range(1, N_DEV):
            peer = _peer(k)
            cp = pltpu.make_async_remote_copy(part.at[_lin(peer)], out_ref.at[me], s2.at[k - 1], r2.at[k - 1],
                                              device_id=peer, device_id_type=MESH)
            cp.start()
            second.append(cp)
        out_ref[me] = part[me]
        for cp in second:
            cp.wait_send()
            cp.wait_recv()

    vm = pl.BlockSpec(memory_space=pltpu.VMEM)
    return pl.pallas_call(
        body, name="mod_exchange", out_shape=jax.ShapeDtypeStruct((N_DEV, 1, chunk), F32),
        in_specs=[vm, vm, vm], out_specs=vm,
        scratch_shapes=[pltpu.VMEM((N_DEV, 1, D_MODEL), F32), pltpu.VMEM((N_DEV, 1, chunk), F32)]
        + [pltpu.SemaphoreType.DMA((N_DEV - 1,))] * 4,
    )(c, ada_w, ada_b8)


def _row(i):
    return (i, 0)


def _fixed(i):
    return (0, 0)


def _in_proj_fwd(x, norm1, scale1, shift1, w_in, tm, xchg):
    S = x.shape[0]

    def body(x_ref, n_ref, sc_ref, sh_ref, w_ref, qkv_ref, z_ref, xbc_ref, dt_ref):
        h = _modnorm(x_ref[...], n_ref[...], sc_ref[...], sh_ref[...])
        p = _mm_nt(h, w_ref[...])
        qkv_ref[...] = p[:, :768].astype(qkv_ref.dtype)
        z_ref[...] = p[:, 768:1280]
        xbc_ref[...] = p[:, 1280:2304]
        dt_ref[...] = p[:, 2304:IN_PAD]

    vec = pl.BlockSpec((1, D_MODEL), _fixed)
    return _hosted_call(
        body, "in_proj_fwd", S // tm,
        in_specs=[pl.BlockSpec((tm, D_MODEL), _row), vec, vec, vec, pl.BlockSpec((IN_PAD, D_MODEL), _fixed)],
        out_specs=[pl.BlockSpec((tm, 768), _row), pl.BlockSpec((tm, SSM_W), _row),
                   pl.BlockSpec((tm, XBC_W), _row), pl.BlockSpec((tm, LANE), _row)],
        out_shape=[jax.ShapeDtypeStruct((S, 768), MXU_DTYPE), jax.ShapeDtypeStruct((S, SSM_W), F32),
                   jax.ShapeDtypeStruct((S, XBC_W), F32), jax.ShapeDtypeStruct((S, LANE), F32)],
        scratch_shapes=[], args=(x, norm1, scale1, shift1, w_in), xchg=xchg, cparams=_cparams(VMEM_BIG),
    )


def _in_proj_bwd(x, dx1, dq, dkv, dz, dxbc, ddt, norm1, scale1, shift1, w_in, tm):
    S = x.shape[0]

    def body(x_ref, dx1_ref, dq_ref, dkv_ref, dz_ref, dxbc_ref, ddt_ref, n_ref, sc_ref, sh_ref, w_ref,
             gx_ref, h_ref, dp_ref, acc_ref):
        @pl.when(pl.program_id(0) == 0)
        def _():
            acc_ref[...] = jnp.zeros_like(acc_ref)

        dp = jnp.concatenate([dq_ref[...].astype(MXU_DTYPE), dkv_ref[...].astype(MXU_DTYPE),
                              dz_ref[...].astype(MXU_DTYPE), dxbc_ref[...].astype(MXU_DTYPE),
                              ddt_ref[...].astype(MXU_DTYPE)], axis=1)
        dh = _mm(dp, w_ref[...])
        r, xhat = _modnorm_parts(x_ref[...])
        dx, dn, dsc, dsh = _modnorm_bwd(r, xhat, n_ref[...], sc_ref[...], dh)
        gx_ref[...] = dx1_ref[...] + dx
        h_ref[...] = (xhat * n_ref[...] * (1.0 + sc_ref[...]) + sh_ref[...]).astype(h_ref.dtype)
        dp_ref[...] = dp
        acc_ref[0:1, :] += dn
        acc_ref[1:2, :] += dsc
        acc_ref[2:3, :] += dsh

    vec = pl.BlockSpec((1, D_MODEL), _fixed)
    return pl.pallas_call(
        body, name="in_proj_bwd", grid=(S // tm,),
        in_specs=[pl.BlockSpec((tm, D_MODEL), _row), pl.BlockSpec((tm, D_MODEL), _row),
                  pl.BlockSpec((tm, ATTN_W), _row), pl.BlockSpec((tm, 2 * KV_W), _row),
                  pl.BlockSpec((tm, SSM_W), _row), pl.BlockSpec((tm, XBC_W), _row), pl.BlockSpec((tm, LANE), _row),
                  vec, vec, vec, pl.BlockSpec((IN_PAD, D_MODEL), _fixed)],
        out_specs=[pl.BlockSpec((tm, D_MODEL), _row), pl.BlockSpec((tm, D_MODEL), _row),
                   pl.BlockSpec((tm, IN_PAD), _row), pl.BlockSpec((8, D_MODEL), _fixed)],
        out_shape=[jax.ShapeDtypeStruct((S, D_MODEL), F32), jax.ShapeDtypeStruct((S, D_MODEL), MXU_DTYPE),
                   jax.ShapeDtypeStruct((S, IN_PAD), MXU_DTYPE), jax.ShapeDtypeStruct((8, D_MODEL), F32)],
        compiler_params=_cparams(VMEM_BIG),
    )(x, dx1, dq, dkv, dz, dxbc, ddt, norm1, scale1, shift1, w_in)


def _out_stage(ya, ys0, ys1, z0, z1, an, sn0, sn1):
    half = SSM_W // 2
    a = _rms(ya, an, ATTN_W)
    g0 = _rms(ys0 * _silu(z0), sn0, half)
    g1 = _rms(ys1 * _silu(z1), sn1, half)
    return jnp.concatenate([a, g0, g1], axis=1)


def _out_stage_args(ya_ref, ys_ref, z_ref, an_ref, sn_ref):
    half = SSM_W // 2
    return (ya_ref[...], ys_ref[:, :half], ys_ref[:, half:], z_ref[:, :half], z_ref[:, half:],
            an_ref[...], sn_ref[:, :half], sn_ref[:, half:])


def _out_proj_fwd(x, ya, ys, z, an, sn, gate1, w_o, tm, xchg):
    S = x.shape[0]

    def body(x_ref, ya_ref, ys_ref, z_ref, an_ref, sn_ref, g_ref, w_ref, x1_ref):
        u = _out_stage(*_out_stage_args(ya_ref, ys_ref, z_ref, an_ref, sn_ref))
        x1_ref[...] = x_ref[...] + g_ref[...] * _mm(u, w_ref[...])

    half = pl.BlockSpec((tm, ATTN_W), _row)
    hvec = pl.BlockSpec((1, ATTN_W), _fixed)
    (x1,), x_out = _hosted_call(
        body, "out_proj_fwd", S // tm,
        in_specs=[pl.BlockSpec((tm, D_MODEL), _row), half, half, half, hvec, hvec,
                  pl.BlockSpec((1, D_MODEL), _fixed), pl.BlockSpec((D_MODEL, D_MODEL), _fixed)],
        out_specs=[pl.BlockSpec((tm, D_MODEL), _row)],
        out_shape=[jax.ShapeDtypeStruct((S, D_MODEL), F32)],
        scratch_shapes=[], args=(x, ya, ys, z, an, sn, gate1, w_o), xchg=xchg, cparams=_cparams(VMEM_BIG),
    )
    return x1, x_out


def _out_proj_bwd(dx1, ya, ys, z, an, sn, gate1, w_o, tm):
    S = dx1.shape[0]

    def body(dx1_ref, ya_ref, ys_ref, z_ref, an_ref, sn_ref, g_ref, w_ref,
             dya_ref, dys_ref, dz_ref, u_ref, dmix_ref, acc_ref):
        @pl.when(pl.program_id(0) == 0)
        def _():
            acc_ref[...] = jnp.zeros_like(acc_ref)

        u, vjp = jax.vjp(_out_stage, *_out_stage_args(ya_ref, ys_ref, z_ref, an_ref, sn_ref))
        dx1 = dx1_ref[...]
        mix = _mm(u, w_ref[...])
        dmix = dx1 * g_ref[...]
        du = _mm_nt(dmix, w_ref[...])
        dya, dys0, dys1, dz0, dz1, dan, dsn0, dsn1 = vjp(du)
        dya_ref[...] = dya
        dys_ref[...] = jnp.concatenate([dys0, dys1], axis=1)
        dz_ref[...] = jnp.concatenate([dz0, dz1], axis=1)
        u_ref[...] = u.astype(u_ref.dtype)
        dmix_ref[...] = dmix.astype(dmix_ref.dtype)
        acc_ref[0:1, :] += jnp.sum(dx1 * mix, axis=0, keepdims=True)
        acc_ref[1:2, :] += jnp.concatenate([dan, dsn0, dsn1], axis=1)

    half = pl.BlockSpec((tm, ATTN_W), _row)
    hvec = pl.BlockSpec((1, ATTN_W), _fixed)
    full = pl.BlockSpec((tm, D_MODEL), _row)
    return pl.pallas_call(
        body, name="out_proj_bwd", grid=(S // tm,),
        in_specs=[full, half, half, half, hvec, hvec,
                  pl.BlockSpec((1, D_MODEL), _fixed), pl.BlockSpec((D_MODEL, D_MODEL), _fixed)],
        out_specs=[half, half, half, full, full, pl.BlockSpec((8, D_MODEL), _fixed)],
        out_shape=[jax.ShapeDtypeStruct((S, ATTN_W), F32)] * 3
        + [jax.ShapeDtypeStruct((S, D_MODEL), MXU_DTYPE)] * 2 + [jax.ShapeDtypeStruct((8, D_MODEL), F32)],
        compiler_params=_cparams(VMEM_BIG),
    )(dx1, ya, ys, z, an, sn, gate1, w_o)


def _loss_rows(x2, fn, tgt):
    y = _rms(x2, fn, D_MODEL)
    per_row = jnp.sum(jnp.square(y - tgt), axis=1, keepdims=True)
    return jnp.sum(per_row, axis=0, keepdims=True) * (0.5 / D_MODEL)


def _mlp_loss(x1, tgt, norm2, scale2, shift2, gate2, fnorm, w_gu, w_d, tm):
    S = x1.shape[0]
    n_pieces = len(w_gu) + len(w_d)

    def body(*refs):
        x1_ref, t_ref, n_ref, sc_ref, sh_ref, g_ref, fn_ref = refs[:7]
        piece_refs = refs[7:7 + n_pieces]
        dx1_ref, h_ref, dgu_ref, act_ref, dmlp_ref, acc_ref, wgu, wd, wsem = refs[7 + n_pieces:]

        @pl.when(pl.program_id(0) == 0)
        def _():
            acc_ref[...] = jnp.zeros_like(acc_ref)
            copies = []
            for dst, pieces in ((wgu, piece_refs[:len(w_gu)]), (wd, piece_refs[len(w_gu):])):
                shard = sum(p.shape[1] for p in pieces)
                off = 0
                for p in pieces:
                    for j in range(N_DEV):
                        copies.append(pltpu.make_async_copy(p.at[j], dst.at[pl.ds(j * shard + off, p.shape[1])],
                                                            wsem.at[len(copies)]))
                    off += p.shape[1]
            for cp in copies:
                cp.start()
            for cp in copies:
                cp.wait()

        x1 = x1_ref[...]
        gate2 = g_ref[...]
        h, vjp_h = jax.vjp(_modnorm, x1, n_ref[...], sc_ref[...], sh_ref[...])
        hb = h.astype(MXU_DTYPE)
        gu = _mm_nt(hb, wgu[...])
        g, u = gu[:, :D_FF], gu[:, D_FF:]
        sg = jax.nn.sigmoid(g)
        silu_g = g * sg
        act = (silu_g * u).astype(MXU_DTYPE)
        mlp = _mm(act, wd[...])
        x2 = x1 + gate2 * mlp
        loss, vjp_loss = jax.vjp(_loss_rows, x2, fn_ref[...], t_ref[...])
        dx2, dfn, _ = vjp_loss(jnp.ones((1, 1), F32))
        dmlp = (dx2 * gate2).astype(MXU_DTYPE)
        dact = _mm_nt(dmlp, wd[...])
        dg = dact * u * (sg * (1.0 + g * (1.0 - sg)))
        du = dact * silu_g
        dgu = jnp.concatenate([dg, du], axis=1).astype(MXU_DTYPE)
        dh = _mm(dgu, wgu[...])
        dx, dn, dsc, dsh = vjp_h(dh)
        dx1_ref[...] = dx2 + dx
        h_ref[...] = hb
        dgu_ref[...] = dgu
        act_ref[...] = act
        dmlp_ref[...] = dmlp
        acc_ref[0:1, :] += dn
        acc_ref[1:2, :] += dsc
        acc_ref[2:3, :] += dsh
        acc_ref[3:4, :] += jnp.sum(dx2 * mlp, axis=0, keepdims=True)
        acc_ref[4:5, :] += dfn
        acc_ref[5:6, :] += jnp.broadcast_to(loss, (1, D_MODEL))

    full = pl.BlockSpec((tm, D_MODEL), _row)
    vec = pl.BlockSpec((1, D_MODEL), _fixed)
    anyspec = pl.BlockSpec(memory_space=pl.ANY)
    return pl.pallas_call(
        body, name="mlp_loss", grid=(S // tm,),
        in_specs=[full, full, vec, vec, vec, vec, vec] + [anyspec] * n_pieces,
        out_specs=[full, full, pl.BlockSpec((tm, 2 * D_FF), _row), pl.BlockSpec((tm, D_FF), _row), full,
                   pl.BlockSpec((8, D_MODEL), _fixed)],
        out_shape=[jax.ShapeDtypeStruct((S, D_MODEL), F32), jax.ShapeDtypeStruct((S, D_MODEL), MXU_DTYPE),
                   jax.ShapeDtypeStruct((S, 2 * D_FF), MXU_DTYPE), jax.ShapeDtypeStruct((S, D_FF), MXU_DTYPE),
                   jax.ShapeDtypeStruct((S, D_MODEL), MXU_DTYPE), jax.ShapeDtypeStruct((8, D_MODEL), F32)],
        scratch_shapes=[pltpu.VMEM((2 * D_FF, D_MODEL), MXU_DTYPE), pltpu.VMEM((D_FF, D_MODEL), MXU_DTYPE),
                        pltpu.SemaphoreType.DMA((N_DEV * n_pieces,))],
        compiler_params=_cparams(VMEM_BIG),
    )(x1, tgt, norm2, scale2, shift2, gate2, fnorm, *w_gu, *w_d)


def _wgrad(a, g, tk, ts, name, xchg=None, g_cols=None):
    S, K = a.shape
    N, col = (g.shape[1], 0) if g_cols is None else g_cols
    ns = S // ts

    def body(a_ref, g_ref, o_ref, acc_ref):
        s = pl.program_id(1)

        @pl.when(s == 0)
        def _():
            acc_ref[...] = jnp.zeros_like(acc_ref)

        acc_ref[...] += _mm_tn(a_ref[...], g_ref[...])

        @pl.when(s == ns - 1)
        def _():
            o_ref[...] = acc_ref[...].astype(o_ref.dtype)

    in_specs = [pl.BlockSpec((ts, tk), lambda j, s: (s, j)), pl.BlockSpec((ts, N), lambda j, s: (s, col))]
    out_spec = pl.BlockSpec((tk, N), lambda j, s: (j, 0))
    out_shape = jax.ShapeDtypeStruct((K, N), WIRE_DTYPE)
    scratch = [pltpu.VMEM((tk, N), F32)]
    if xchg is None:
        return pl.pallas_call(body, name=name, grid=(K // tk, ns), in_specs=in_specs, out_specs=out_spec,
                              out_shape=out_shape, scratch_shapes=scratch, compiler_params=_cparams(VMEM_BIG))(a, g)
    (out,), x_out = _hosted_call(body, name, (K // tk, ns), in_specs, [out_spec], [out_shape], scratch, (a, g), xchg,
                                 _cparams(VMEM_BIG))
    return out, x_out


MASKED = -1e30
QK_SCALE = HALF ** -0.5


def _attn_bias(buckets, rel_bias):
    def body(bk_ref, relb_ref, out_ref):
        bk = bk_ref[...]
        i = lax.broadcasted_iota(jnp.int32, (BLK, 2 * BLK), 0)
        j = lax.broadcasted_iota(jnp.int32, (BLK, 2 * BLK), 1)
        window = (j > i) & (j <= i + BLK)
        for h in range(N_HEADS):
            acc = jnp.zeros((BLK, 2 * BLK), F32)
            for b in range(N_BUCKETS):
                acc = jnp.where(bk == b, relb_ref[b, h], acc)
            out_ref[0, h] = jnp.where(window, acc, MASKED)
            out_ref[1, h] = jnp.where(window & (j >= BLK), acc, MASKED)

    return pl.pallas_call(
        body, name="attn_bias", out_shape=jax.ShapeDtypeStruct((2, N_HEADS, BLK, 2 * BLK), F32),
        in_specs=[pl.BlockSpec(memory_space=pltpu.VMEM), pl.BlockSpec(memory_space=pltpu.SMEM)],
    )(buckets, rel_bias)


def _attn_kv(kvp_ref, kvc_ref):
    kvp = kvp_ref[...].astype(F32)
    kvc = kvc_ref[...].astype(F32)
    kp, kc = _split_pair(kvp[:, :LANE]), _split_pair(kvc[:, :LANE])
    vp, vc = _split_pair(kvp[:, LANE:]), _split_pair(kvc[:, LANE:])
    k_pads = [jnp.concatenate([kp[g], kc[g]], axis=0).astype(MXU_DTYPE) for g in range(2)]
    v_pads = [jnp.concatenate([vp[g], vc[g]], axis=0).astype(MXU_DTYPE) for g in range(2)]
    return k_pads, v_pads


def _attn_fwd(qkv, bias, sinks, xchg):
    S = qkv.shape[0]
    nb = S // BLK

    def body(q_ref, kvp_ref, kvc_ref, bias_ref, sinks_ref, y_ref):
        first = jnp.where(pl.program_id(0) == 0, 1, 0)
        q_heads = _split_heads(q_ref[...].astype(F32) * QK_SCALE, 4)
        k_pads, v_pads = _attn_kv(kvp_ref, kvc_ref)
        heads = range(N_HEADS)
        s = [_mm_nt(q_heads[h].astype(MXU_DTYPE), k_pads[h // 4]) + bias_ref[first, h] for h in heads]
        m = [jnp.maximum(jnp.max(s[h], axis=-1, keepdims=True), sinks_ref[h]) for h in heads]
        p = [jnp.exp(s[h] - m[h]) for h in heads]
        rinv = [1.0 / (jnp.sum(p[h], axis=-1, keepdims=True) + jnp.exp(sinks_ref[h] - m[h])) for h in heads]
        y_ref[...] = _join_heads([_mm(p[h], v_pads[h // 4]) * rinv[h] for h in heads])

    smem = pl.BlockSpec(memory_space=pltpu.SMEM)
    return _hosted_call(
        body, "attn_fwd", nb,
        in_specs=[pl.BlockSpec((BLK, ATTN_W), _row),
                  pl.BlockSpec((BLK, 2 * KV_W), lambda i: (jnp.maximum(i - 1, 0), 2)),
                  pl.BlockSpec((BLK, 2 * KV_W), lambda i: (i, 2)),
                  pl.BlockSpec((2, N_HEADS, BLK, 2 * BLK), lambda i: (0, 0, 0, 0)), smem],
        out_specs=[pl.BlockSpec((BLK, ATTN_W), _row)],
        out_shape=[jax.ShapeDtypeStruct((S, ATTN_W), F32)],
        scratch_shapes=[],
        args=(qkv, qkv, qkv, bias, sinks), xchg=xchg, cparams=_cparams(),
    )


def _attn_bwd(qkv, y, dy, bias, sinks, xchg):
    S = qkv.shape[0]
    nb = S // BLK

    def body(q_ref, kvp_ref, kvc_ref, y_ref, dy_ref, bias_ref, sinks_ref, dq_ref, dkv_ref, dbias_ref, dsk_ref, carry_ref):
        i = pl.program_id(0)

        @pl.when(i == 0)
        def _():
            dbias_ref[...] = jnp.zeros_like(dbias_ref)
            dsk_ref[...] = jnp.zeros_like(dsk_ref)
            carry_ref[...] = jnp.zeros_like(carry_ref)

        first = jnp.where(i == nb - 1, 1, 0)
        q_heads = _split_heads(q_ref[...].astype(F32) * QK_SCALE, 4)
        k_pads, v_pads = _attn_kv(kvp_ref, kvc_ref)
        y_heads = _split_heads(y_ref[...], 4)
        dy_heads = _split_heads(dy_ref[...], 4)
        heads = range(N_HEADS)
        qs = [q_heads[h].astype(MXU_DTYPE) for h in heads]
        s = [_mm_nt(qs[h], k_pads[h // 4]) + bias_ref[first, h] for h in heads]
        m = [jnp.maximum(jnp.max(s[h], axis=-1, keepdims=True), sinks_ref[h]) for h in heads]
        p = [jnp.exp(s[h] - m[h]) for h in heads]
        esink = [jnp.exp(sinks_ref[h] - m[h]) for h in heads]
        rinv = [1.0 / (jnp.sum(p[h], axis=-1, keepdims=True) + esink[h]) for h in heads]
        t = [dy_heads[h] * rinv[h] for h in heads]
        delta = [jnp.sum(t[h] * y_heads[h], axis=-1, keepdims=True) for h in heads]
        tb = [t[h].astype(MXU_DTYPE) for h in heads]
        dp = [_mm_nt(tb[h], v_pads[h // 4]) for h in heads]
        ds = [p[h] * (dp[h] - delta[h]) for h in heads]
        for h in heads:
            dbias_ref[h] += ds[h]
            dsk_ref[h] -= esink[h] * delta[h]
        dsb = [ds[h].astype(MXU_DTYPE) for h in heads]
        pb = [p[h].astype(MXU_DTYPE) for h in heads]
        dq_heads = [_mm(dsb[h], k_pads[h // 4]) * QK_SCALE for h in heads]
        dk_pads = [_mm_tn(jnp.concatenate(dsb[4 * g:4 * g + 4], axis=0), jnp.concatenate(qs[4 * g:4 * g + 4], axis=0))
                   for g in range(2)]
        dv_pads = [_mm_tn(jnp.concatenate(pb[4 * g:4 * g + 4], axis=0), jnp.concatenate(tb[4 * g:4 * g + 4], axis=0))
                   for g in range(2)]
        dq_ref[...] = _join_heads(dq_heads)
        dk_prev = _join_pair(dk_pads[0][:BLK], dk_pads[1][:BLK])
        dk_cur = _join_pair(dk_pads[0][BLK:], dk_pads[1][BLK:])
        dv_prev = _join_pair(dv_pads[0][:BLK], dv_pads[1][:BLK])
        dv_cur = _join_pair(dv_pads[0][BLK:], dv_pads[1][BLK:])
        dkv_ref[...] = jnp.concatenate([dk_cur, dv_cur], axis=1) + carry_ref[...]
        carry_ref[...] = jnp.concatenate([dk_prev, dv_prev], axis=1)

    smem = pl.BlockSpec(memory_space=pltpu.SMEM)
    rev = lambda i: (nb - 1 - i, 0)
    return _hosted_call(
        body, "attn_bwd", nb,
        in_specs=[pl.BlockSpec((BLK, ATTN_W), rev),
                  pl.BlockSpec((BLK, 2 * KV_W), lambda i: (jnp.maximum(nb - 2 - i, 0), 2)),
                  pl.BlockSpec((BLK, 2 * KV_W), lambda i: (nb - 1 - i, 2)),
                  pl.BlockSpec((BLK, ATTN_W), rev), pl.BlockSpec((BLK, ATTN_W), rev),
                  pl.BlockSpec((2, N_HEADS, BLK, 2 * BLK), lambda i: (0, 0, 0, 0)), smem],
        out_specs=[pl.BlockSpec((BLK, ATTN_W), rev), pl.BlockSpec((BLK, 2 * KV_W), rev),
                   pl.BlockSpec((N_HEADS, BLK, 2 * BLK), lambda i: (0, 0, 0)),
                   pl.BlockSpec((N_HEADS, BLK, 1), lambda i: (0, 0, 0))],
        out_shape=[jax.ShapeDtypeStruct((S, ATTN_W), F32), jax.ShapeDtypeStruct((S, 2 * KV_W), F32),
                   jax.ShapeDtypeStruct((N_HEADS, BLK, 2 * BLK), F32), jax.ShapeDtypeStruct((N_HEADS, BLK, 1), F32)],
        scratch_shapes=[pltpu.VMEM((BLK, 2 * KV_W), F32)],
        args=(qkv, qkv, qkv, y, dy, bias, sinks), xchg=xchg, cparams=_cparams(),
    )


def _attn_finish(dbias, dsk, buckets):
    def body(db_ref, dsk_ref, bk_ref, drel_ref, dsink_ref):
        bk = bk_ref[...]
        r = lax.broadcasted_iota(jnp.int32, (N_BUCKETS, LANE), 0)
        l = lax.broadcasted_iota(jnp.int32, (N_BUCKETS, LANE), 1)
        row = lax.broadcasted_iota(jnp.int32, (N_HEADS, LANE), 0)
        res = jnp.zeros((N_BUCKETS, LANE), F32)
        dsink = jnp.zeros((N_HEADS, LANE), F32)
        for h in range(N_HEADS):
            db = db_ref[h]
            for b in range(N_BUCKETS):
                v = jnp.sum(jnp.sum(jnp.where(bk == b, db, 0.0), axis=1, keepdims=True), axis=0, keepdims=True)
                res = res + jnp.where((r == b) & (l == h), v, 0.0)
            dsink = dsink + jnp.where(row == h, jnp.sum(dsk_ref[h], axis=0, keepdims=True), 0.0)
        drel_ref[...] = res
        dsink_ref[...] = dsink

    return pl.pallas_call(body, name="attn_finish",
                          out_shape=[jax.ShapeDtypeStruct((N_BUCKETS, LANE), F32),
                                     jax.ShapeDtypeStruct((N_HEADS, LANE), F32)])(dbias, dsk, buckets)


def _ssd_consts():
    r = lax.broadcasted_iota(jnp.int32, (BLK, BLK), 0)
    c = lax.broadcasted_iota(jnp.int32, (BLK, BLK), 1)
    causal = c <= r
    upper = (r <= c).astype(F32)
    last = r == BLK - 1
    head = lax.broadcasted_iota(jnp.int32, (N_HEADS, BLK), 0)
    return causal, upper, last, head


def _ssd_chunk(xs, bg, cg, dt_raw_t, prev, dtb, alog, d_rows, consts):
    causal, upper, last, head = consts
    dt_t = _softplus(dt_raw_t + dtb)
    acs_t = _mm_hi(dt_t * (-jnp.exp(alog)), upper)
    cb = [_mm_nt(cg[g], bg[g]) for g in range(2)]
    heads = range(N_HEADS)
    dt_row = [jnp.sum(jnp.where(head == h, dt_t, 0.0), axis=0, keepdims=True) for h in heads]
    a_row = [jnp.sum(jnp.where(head == h, acs_t, 0.0), axis=0, keepdims=True) for h in heads]
    a_rb = [jnp.broadcast_to(a_row[h], (BLK, BLK)) for h in heads]
    a_b = [a_rb[h].T for h in heads]
    a_last = [jnp.sum(jnp.where(last, a_b[h], 0.0), axis=0, keepdims=True) for h in heads]
    w = [cb[h // 4] * jnp.exp(jnp.where(causal, a_b[h] - a_rb[h], -1e30)) * dt_row[h] for h in heads]
    f_b = [jnp.broadcast_to(dt_row[h] * jnp.exp(a_last[h] - a_row[h]), (BLK, BLK)).T for h in heads]
    y_in = [_mm(w[h], xs[h]) for h in heads]
    y_off = [_mm(cg[h // 4], prev[h]) * jnp.exp(a_b[h]) for h in heads]
    st = [_mm_tn(bg[h // 4], xs[h] * f_b[h]) for h in heads]
    ys = [y_in[h] + y_off[h] + d_rows[h] * xs[h] for h in heads]
    hs = [prev[h] * jnp.exp(a_last[h]) + st[h] for h in heads]
    return tuple(ys), tuple(hs)


def _ssd_chunk_bwd(xs, bg, cg, dt_raw_t, prev, dtb, alog, d_rows, dys, dhs, consts):
    causal, upper, last, head = consts
    heads, groups = range(N_HEADS), range(2)
    lane = _lane_iota((BLK, BLK))
    lane_row = _lane_iota((1, BLK))
    pre_dt = dt_raw_t + dtb
    dt_t = _softplus(pre_dt)
    a_neg = -jnp.exp(alog)
    acs_t = _mm_hi(dt_t * a_neg, upper)
    pick = lambda t, h: jnp.sum(jnp.where(head == h, t, 0.0), axis=0, keepdims=True)
    full_sum = lambda t: jnp.sum(jnp.sum(t, axis=1, keepdims=True), axis=0, keepdims=True)
    dt_row = [pick(dt_t, h) for h in heads]
    a_row = [pick(acs_t, h) for h in heads]
    a_rb = [jnp.broadcast_to(a_row[h], (BLK, BLK)) for h in heads]
    a_b = [a_rb[h].T for h in heads]
    a_last = [jnp.sum(jnp.where(last, a_b[h], 0.0), axis=0, keepdims=True) for h in heads]
    lm = [jnp.exp(jnp.where(causal, a_b[h] - a_rb[h], -1e30)) for h in heads]
    cgb = [cg[g].astype(MXU_DTYPE) for g in groups]
    bgb = [bg[g].astype(MXU_DTYPE) for g in groups]
    cb = [_mm_nt(cgb[g], bgb[g]) for g in groups]
    u = [cb[h // 4] * lm[h] for h in heads]
    w = [(u[h] * dt_row[h]).astype(MXU_DTYPE) for h in heads]
    e_row = [jnp.exp(a_last[h] - a_row[h]) for h in heads]
    f_row = [dt_row[h] * e_row[h] for h in heads]
    f_b = [jnp.broadcast_to(f_row[h], (BLK, BLK)).T for h in heads]
    e_b = [jnp.exp(a_b[h]) for h in heads]
    el = [jnp.exp(a_last[h]) for h in heads]
    xb = [xs[h].astype(MXU_DTYPE) for h in heads]
    dyb = [dys[h].astype(MXU_DTYPE) for h in heads]
    prevb = [prev[h].astype(MXU_DTYPE) for h in heads]
    dstb = [dhs[h].astype(MXU_DTYPE) for h in heads]
    gmat = [_mm(cgb[h // 4], prevb[h]) for h in heads]
    dw = [_mm_nt(dyb[h], xb[h]) for h in heads]
    dg = [dys[h] * e_b[h] for h in heads]
    dgb = [dg[h].astype(MXU_DTYPE) for h in heads]
    dxf = [_mm(bgb[h // 4], dstb[h]) for h in heads]
    xfb = [(xs[h] * f_b[h]).astype(MXU_DTYPE) for h in heads]
    dxs = [_mm_tn(w[h], dyb[h]) + d_rows[h] * dys[h] + f_b[h] * dxf[h] for h in heads]
    dd_rows = [jnp.sum(dys[h] * xs[h], axis=0, keepdims=True) for h in heads]
    dprev = [_mm_tn(cgb[h // 4], dgb[h]) + dhs[h] * el[h] for h in heads]
    dcg_h = [_mm_nt(dgb[h], prevb[h]) for h in heads]
    dbg_h = [_mm_nt(xfb[h], dstb[h]) for h in heads]
    zt = [dw[h] * u[h] for h in heads]
    dseg = [zt[h] * dt_row[h] for h in heads]
    dcb_h = [dw[h] * lm[h] * dt_row[h] for h in heads]
    dcb = [(dcb_h[4 * g] + dcb_h[4 * g + 1] + dcb_h[4 * g + 2] + dcb_h[4 * g + 3]).astype(MXU_DTYPE) for g in groups]
    dcg = [dcg_h[4 * g] + dcg_h[4 * g + 1] + dcg_h[4 * g + 2] + dcg_h[4 * g + 3] + _mm(dcb[g], bgb[g]) for g in groups]
    dbg = [dbg_h[4 * g] + dbg_h[4 * g + 1] + dbg_h[4 * g + 2] + dbg_h[4 * g + 3] + _mm_tn(dcb[g], cgb[g])
           for g in groups]
    r1 = [jnp.sum(dg[h] * gmat[h] + dseg[h], axis=1, keepdims=True) for h in heads]
    r2 = [jnp.sum(dxf[h] * xs[h], axis=1, keepdims=True) for h in heads]
    tt = [jnp.where(lane < HALF, jnp.broadcast_to(r1[h], (BLK, BLK)), jnp.broadcast_to(r2[h], (BLK, BLK))).T
          for h in heads]
    r1_row = [tt[h][0:1, :] for h in heads]
    r2_row = [tt[h][HALF:HALF + 1, :] for h in heads]
    d_el = [full_sum(dhs[h] * prev[h]) for h in heads]
    da_last = [jnp.sum(r2_row[h] * f_row[h], axis=1, keepdims=True) + el[h] * d_el[h] for h in heads]
    da_row = [r1_row[h] - jnp.sum(dseg[h], axis=0, keepdims=True) - r2_row[h] * f_row[h]
              + jnp.where(lane_row == BLK - 1, da_last[h], 0.0) for h in heads]
    ddt_row = [jnp.sum(zt[h], axis=0, keepdims=True) + r2_row[h] * e_row[h] for h in heads]
    da_t = jnp.zeros((N_HEADS, BLK), F32)
    ddt_t = jnp.zeros((N_HEADS, BLK), F32)
    for h in heads:
        da_t = jnp.where(head == h, da_row[h], da_t)
        ddt_t = jnp.where(head == h, ddt_row[h], ddt_t)
    d_dta = _mm_hi(da_t, causal.astype(F32))
    dalog = d_dta * dt_t * a_neg
    draw = (ddt_t + d_dta * a_neg) * jax.nn.sigmoid(pre_dt)
    return dxs, dbg, dcg, draw, dprev, draw, dalog, dd_rows


def _dt_rows(dt_blk):
    return dt_blk.T[:N_HEADS]


def _silu_grad(x):
    s = jax.nn.sigmoid(x)
    return s * (1.0 + x * (1.0 - s))


def _conv_pre(halo, blk, cw_ref, cb_ref):
    ext = jnp.concatenate([halo, blk], axis=0)
    taps = [pltpu.roll(ext, 3 - k, 0)[8:] for k in range(3)] + [blk]
    pre = cb_ref[...] + cw_ref[0:1, :] * taps[0]
    for k in range(1, 4):
        pre = pre + cw_ref[k:k + 1, :] * taps[k]
    return pre, taps


def _ssd_split(pre):
    heads = _split_heads(pre[:, :SSM_W], 4)
    pb = [pre[:, SSM_W + g * D_STATE:SSM_W + (g + 1) * D_STATE] for g in range(2)]
    pc = [pre[:, SSM_W + 2 * D_STATE + g * D_STATE:SSM_W + 2 * D_STATE + (g + 1) * D_STATE] for g in range(2)]
    return heads, pb, pc


def _ssd_fwd(xbc, dt_raw, conv_w, conv_b, dtb_row, alog_row, d_exp, xchg):
    S = xbc.shape[0]
    nc = S // BLK

    def body(xbc_ref, halo_ref, dt_ref, cw_ref, cb_ref, dtb_ref, alog_ref, d_ref, y_ref, prev_ref, state_ref):
        i = pl.program_id(0)

        @pl.when(i == 0)
        def _():
            state_ref[...] = jnp.zeros_like(state_ref)

        halo = halo_ref[...] * jnp.where(i > 0, 1.0, 0.0)
        pre, _ = _conv_pre(halo, xbc_ref[...], cw_ref, cb_ref)
        heads, pb, pc = _ssd_split(_silu(pre))
        prev = [state_ref[h] for h in range(N_HEADS)]
        for h in range(N_HEADS):
            prev_ref[0, h] = prev[h]
        d_rows = [d_ref[h:h + 1, :] for h in range(N_HEADS)]
        ys, hs = _ssd_chunk(heads, pb, pc, _dt_rows(dt_ref[...]), prev, dtb_ref[...], alog_ref[...], d_rows,
                            _ssd_consts())
        for h in range(N_HEADS):
            state_ref[h] = hs[h]
        y_ref[...] = _join_heads(ys)

    vec = pl.BlockSpec((N_HEADS, LANE), _fixed)
    return _hosted_call(
        body, "ssd_fwd", nc,
        in_specs=[pl.BlockSpec((BLK, XBC_W), _row),
                  pl.BlockSpec((8, XBC_W), lambda i: (jnp.maximum(i * (BLK // 8) - 1, 0), 0)),
                  pl.BlockSpec((BLK, LANE), _row),
                  pl.BlockSpec((4, XBC_W), _fixed), pl.BlockSpec((1, XBC_W), _fixed), vec, vec,
                  pl.BlockSpec((N_HEADS, LANE), _fixed)],
        out_specs=[pl.BlockSpec((BLK, SSM_W), _row),
                   pl.BlockSpec((1, N_HEADS, D_STATE, LANE), lambda i: (i, 0, 0, 0))],
        out_shape=[jax.ShapeDtypeStruct((S, SSM_W), F32), jax.ShapeDtypeStruct((nc, N_HEADS, D_STATE, LANE), F32)],
        scratch_shapes=[pltpu.VMEM((N_HEADS, D_STATE, LANE), F32)],
        args=(xbc, xbc, dt_raw, conv_w, conv_b, dtb_row, alog_row, d_exp), xchg=xchg, cparams=_cparams(),
    )


def _ssd_bwd(xbc, dt_raw, prev_states, dy, conv_w, conv_b, dtb_row, alog_row, d_exp, xchg):
    S = xbc.shape[0]
    nc = S // BLK

    def body(xbc_ref, halo_ref, dt_ref, prev_ref, dy_ref, cw_ref, cb_ref, dtb_ref, alog_ref, d_ref,
             dxbc_ref, ddt_ref, dcw_ref, dvec_ref, dd_ref, gstate_ref, ghalo_ref):
        i = pl.program_id(0)
        c = nc - 1 - i

        @pl.when(i == 0)
        def _():
            gstate_ref[...] = jnp.zeros_like(gstate_ref)
            ghalo_ref[...] = jnp.zeros_like(ghalo_ref)
            dcw_ref[...] = jnp.zeros_like(dcw_ref)
            dvec_ref[...] = jnp.zeros_like(dvec_ref)
            dd_ref[...] = jnp.zeros_like(dd_ref)

        halo = halo_ref[...] * jnp.where(c > 0, 1.0, 0.0)
        pre, taps = _conv_pre(halo, xbc_ref[...], cw_ref, cb_ref)
        heads, pb, pc = _ssd_split(_silu(pre))
        prev = [prev_ref[0, h] for h in range(N_HEADS)]
        d_rows = [d_ref[h:h + 1, :] for h in range(N_HEADS)]
        dys = _split_heads(dy_ref[...], 4)
        dhs = [gstate_ref[h] for h in range(N_HEADS)]
        dheads, dpb, dpc, ddt_t, dprev, ddtb, dalog, dd_rows = _ssd_chunk_bwd(
            heads, pb, pc, _dt_rows(dt_ref[...]), prev, dtb_ref[...], alog_ref[...], d_rows, dys, dhs, _ssd_consts())
        for h in range(N_HEADS):
            gstate_ref[h] = dprev[h]
            dd_ref[h:h + 1, :] += dd_rows[h]
        ddt_ref[...] = jnp.concatenate([ddt_t, jnp.zeros((BLK - N_HEADS, BLK), F32)], axis=0).T
        dvec_ref[0:N_HEADS, :] += ddtb
        dvec_ref[N_HEADS:, :] += dalog
        dpre = jnp.concatenate([_join_heads(dheads)] + list(dpb) + list(dpc), axis=1) * _silu_grad(pre)
        zeros8 = jnp.zeros((8, XBC_W), F32)
        dpe = jnp.concatenate([zeros8, dpre, zeros8], axis=0)
        n_ext = 16 + BLK
        dext = cw_ref[3:4, :] * dpe[:8 + BLK]
        dcw_ref[3:4, :] += jnp.sum(dpre * taps[3], axis=0, keepdims=True)
        for k in range(3):
            dext = dext + cw_ref[k:k + 1, :] * pltpu.roll(dpe, n_ext - (3 - k), 0)[:8 + BLK]
            dcw_ref[k:k + 1, :] += jnp.sum(dpre * taps[k], axis=0, keepdims=True)
        dcw_ref[4:5, :] += jnp.sum(dpre, axis=0, keepdims=True)
        dxbc_ref[...] = dext[8:, :]
        dxbc_ref[BLK - 8:BLK, :] += ghalo_ref[...]
        ghalo_ref[...] = dext[:8, :]

    vec = pl.BlockSpec((N_HEADS, LANE), _fixed)
    rev = lambda i: (nc - 1 - i, 0)
    return _hosted_call(
        body, "ssd_bwd", nc,
        in_specs=[pl.BlockSpec((BLK, XBC_W), rev),
                  pl.BlockSpec((8, XBC_W), lambda i: (jnp.maximum((nc - 1 - i) * (BLK // 8) - 1, 0), 0)),
                  pl.BlockSpec((BLK, LANE), rev),
                  pl.BlockSpec((1, N_HEADS, D_STATE, LANE), lambda i: (nc - 1 - i, 0, 0, 0)),
                  pl.BlockSpec((BLK, SSM_W), rev),
                  pl.BlockSpec((4, XBC_W), _fixed), pl.BlockSpec((1, XBC_W), _fixed), vec, vec,
                  pl.BlockSpec((N_HEADS, LANE), _fixed)],
        out_specs=[pl.BlockSpec((BLK, XBC_W), rev), pl.BlockSpec((BLK, LANE), rev),
                   pl.BlockSpec((8, XBC_W), _fixed), pl.BlockSpec((2 * N_HEADS, LANE), _fixed),
                   pl.BlockSpec((N_HEADS, LANE), _fixed)],
        out_shape=[jax.ShapeDtypeStruct((S, XBC_W), F32), jax.ShapeDtypeStruct((S, LANE), F32),
                   jax.ShapeDtypeStruct((8, XBC_W), F32), jax.ShapeDtypeStruct((2 * N_HEADS, LANE), F32),
                   jax.ShapeDtypeStruct((N_HEADS, LANE), F32)],
        scratch_shapes=[pltpu.VMEM((N_HEADS, D_STATE, LANE), F32), pltpu.VMEM((8, XBC_W), F32)],
        args=(xbc, xbc, dt_raw, prev_states, dy, conv_w, conv_b, dtb_row, alog_row, d_exp), xchg=xchg,
        cparams=_cparams(VMEM_BIG),
    )


def _adamw_math(w, g, m, v):
    m = ADAM_B1 * m + (1.0 - ADAM_B1) * g
    v = ADAM_B2 * v + (1.0 - ADAM_B2) * jnp.square(g)
    m_hat = m / (1.0 - ADAM_B1 ** ADAM_STEP)
    v_hat = v / (1.0 - ADAM_B2 ** ADAM_STEP)
    delta = -ADAM_LR * (m_hat / (jnp.sqrt(v_hat) + ADAM_EPS) + ADAM_WD * w)
    return delta, m, v


def _reduce_adamw(parts, w, m, v, name):
    R, C = w.shape

    def body(p_ref, w_ref, m_ref, v_ref, g_ref, d_ref, nm_ref, nv_ref):
        g = p_ref[0].astype(F32)
        for i in range(1, N_DEV):
            g = g + p_ref[i].astype(F32)
        d, nm, nv = _adamw_math(w_ref[...], g, m_ref[...], v_ref[...])
        g_ref[...] = g
        d_ref[...] = d
        nm_ref[...] = nm
        nv_ref[...] = nv

    if R % 16 == 0:
        tr = max(t for t in range(16, 257, 16) if R % t == 0)
        n, blk, pblk = R // tr, pl.BlockSpec((tr, C), _row), pl.BlockSpec((N_DEV, tr, C), lambda i: (0, i, 0))
    else:
        tl = 256
        n, blk, pblk = C // tl, pl.BlockSpec((R, tl), lambda i: (0, i)), pl.BlockSpec((N_DEV, R, tl),
                                                                                      lambda i: (0, 0, i))
    return pl.pallas_call(
        body, name=name, grid=(n,), in_specs=[pblk, blk, blk, blk],
        out_specs=[blk] * 4, out_shape=[jax.ShapeDtypeStruct((R, C), F32)] * 4,
    )(parts, w, m, v)


def _reduce_adamw_hosting(parts_list, wmv_list, name, xchg):
    n_arr = len(parts_list)
    C = wmv_list[0][0].shape[1]
    tl = 256

    def body(*refs):
        p_refs, wmv_refs, o_refs = refs[:n_arr], refs[n_arr:4 * n_arr], refs[4 * n_arr:]
        for k in range(n_arr):
            g = p_refs[k][0].astype(F32)
            for i in range(1, N_DEV):
                g = g + p_refs[k][i].astype(F32)
            w_ref, m_ref, v_ref = wmv_refs[3 * k:3 * k + 3]
            d, nm, nv = _adamw_math(w_ref[...], g, m_ref[...], v_ref[...])
            for o, val in zip(o_refs[4 * k:4 * k + 4], (g, d, nm, nv)):
                o[...] = val

    in_specs = [pl.BlockSpec((N_DEV, w.shape[0], tl), lambda i: (0, 0, i)) for w, _, _ in wmv_list]
    in_specs += [pl.BlockSpec((w.shape[0], tl), lambda i: (0, i)) for w, _, _ in wmv_list for _ in range(3)]
    out_specs = [pl.BlockSpec((w.shape[0], tl), lambda i: (0, i)) for w, _, _ in wmv_list for _ in range(4)]
    out_shape = [jax.ShapeDtypeStruct(w.shape, F32) for w, _, _ in wmv_list for _ in range(4)]
    args = list(parts_list) + [a for wmv in wmv_list for a in wmv]
    outs, x_out = _hosted_call(body, name, C // tl, in_specs, out_specs, out_shape, [], args, xchg,
                               _cparams(VMEM_BIG))
    return [outs[4 * k:4 * k + 4] for k in range(n_arr)], x_out


_SMALL_NAMES = ("ada_b", "norm1", "conv_w", "conv_b", "dt_bias", "A_log", "D_skip", "sinks", "attn_out_norm",
                "ssm_out_norm", "norm2", "rel_bias", "final_norm")
N_MOD = 6 * D_MODEL


def _mod_row(a0, a1, a2):
    return jnp.concatenate([a0[2:3], a0[1:2], a1[0:1], a2[2:3], a2[1:2], a2[3:4]], axis=1)


def _small_update(gathered, params):
    n_g = len(gathered)
    flat = [a for name in _SMALL_NAMES for a in params[name]]

    def body(*refs):
        a0_ref, a1_ref, a2_ref, cw_ref, dv_ref, dd_ref, ds_ref, dr_ref, c_ref = refs[:n_g]
        wmv = refs[n_g:n_g + len(flat)]
        outs = refs[n_g + len(flat):]

        def total(ref):
            t = ref[0]
            for i in range(1, N_DEV):
                t = t + ref[i]
            return t

        t0, t1, t2, tcw, tdv, tdd, tds, tdr = [total(r) for r in (a0_ref, a1_ref, a2_ref, cw_ref, dv_ref, dd_ref,
                                                                   ds_ref, dr_ref)]
        r8 = lax.broadcasted_iota(jnp.int32, (N_HEADS, LANE), 0)
        l8 = lax.broadcasted_iota(jnp.int32, (N_HEADS, LANE), 1)

        def diag_row(t):
            return jnp.sum(jnp.where(r8 == l8, t, 0.0), axis=0, keepdims=True)[:, :N_HEADS]

        def lane_sums(t):
            return diag_row(jnp.broadcast_to(jnp.sum(t, axis=1, keepdims=True), (N_HEADS, LANE)))

        me = _lin(_my_pos())
        n_cw = XBC_W // N_DEV
        cw_mine = jnp.zeros((4, n_cw), F32)
        for j in range(N_DEV):
            cw_mine = cw_mine + tcw[0:4, j * n_cw:(j + 1) * n_cw] * jnp.where(me == j, 1.0, 0.0)
        grads = {
            "ada_b": _mod_row(t0, t1, t2), "norm1": t0[0:1], "conv_w": cw_mine, "conv_b": tcw[4:5],
            "dt_bias": lane_sums(tdv[:N_HEADS]), "A_log": lane_sums(tdv[N_HEADS:]), "D_skip": lane_sums(tdd),
            "sinks": diag_row(tds), "attn_out_norm": t1[1:2, :ATTN_W], "ssm_out_norm": t1[1:2, ATTN_W:],
            "norm2": t2[0:1], "rel_bias": tdr[:, :N_HEADS], "final_norm": t2[4:5],
        }
        for k, name in enumerate(_SMALL_NAMES):
            w_ref, m_ref, v_ref = wmv[3 * k:3 * k + 3]
            g = grads[name]
            d, nm, nv = _adamw_math(w_ref[...], g, m_ref[...], v_ref[...])
            for o, val in zip(outs[4 * k:4 * k + 4], (g, d, nm, nv)):
                o[...] = val
        loss_ref, call_ref, dmod_ref = outs[4 * len(_SMALL_NAMES):]
        loss_ref[...] = t2[5:6, 0:1]
        call_ref[...] = jnp.concatenate([c_ref[i] for i in range(N_DEV)], axis=0)
        dmod_ref[...] = jnp.concatenate([_mod_row(a0_ref[i], a1_ref[i], a2_ref[i]) for i in range(N_DEV)], axis=0)

    out_shape = [jax.ShapeDtypeStruct(params[name][0].shape, F32) for name in _SMALL_NAMES for _ in range(4)]
    out_shape += [jax.ShapeDtypeStruct((1, 1), F32), jax.ShapeDtypeStruct((N_DEV, D_MODEL), F32),
                  jax.ShapeDtypeStruct((N_DEV, N_MOD), F32)]
    res = pl.pallas_call(body, name="small_update", out_shape=out_shape)(*gathered, *flat)
    upd = {name: res[4 * k:4 * k + 4] for k, name in enumerate(_SMALL_NAMES)}
    loss, c_all, dmod_all = res[4 * len(_SMALL_NAMES):]
    return upd, loss, c_all, dmod_all


def _ada_w_update(c_all, dmod_all, w, m, v):
    chunk = w.shape[1]

    def body(c_ref, dm_ref, w_ref, m_ref, v_ref, g_ref, d_ref, nm_ref, nv_ref):
        me = _lin(_my_pos())
        dm = jnp.zeros((N_DEV, chunk), F32)
        for j in range(N_DEV):
            dm = dm + dm_ref[:, j * chunk:(j + 1) * chunk] * jnp.where(me == j, 1.0, 0.0)
        g = lax.dot_general(_silu(c_ref[...]), dm, (((0,), (0,)), ((), ())), precision=HI,
                            preferred_element_type=F32)
        d, nm, nv = _adamw_math(w_ref[...], g, m_ref[...], v_ref[...])
        g_ref[...] = g
        d_ref[...] = d
        nm_ref[...] = nm
        nv_ref[...] = nv

    return pl.pallas_call(body, name="ada_w_update", out_shape=[jax.ShapeDtypeStruct(w.shape, F32)] * 4,
                          compiler_params=_cparams(VMEM_BIG))(c_all, dmod_all, w, m, v)


def _local_step(x, tgt, c, mod, w_in, conv_w, w_o_mine, w_gu_mine, w_d_mine, p):
    S = x.shape[0]
    tm = min(512, S)
    tmm = min(256, S)
    tw = min(2048, S)
    shift1, scale1, gate1, shift2, scale2, gate2 = [mod[i:i + 1] for i in range(6)]
    buckets = jnp.asarray(_t5_bucket_table())
    per_head = lambda a: jnp.broadcast_to(a.reshape(N_HEADS, 1), (N_HEADS, LANE))
    dtb_row, alog_row, d_exp = per_head(p["dt_bias"]), per_head(p["A_log"]), per_head(p["D_skip"])
    sinks = p["sinks"].reshape(N_HEADS)

    d_cut, gu_cut = WD_CUT, WGU_CUTS
    (qkv, z, xbc, dt_raw), (g_d_a,) = _in_proj_fwd(x, p["norm1"], scale1, shift1, w_in, tm,
                                                   ([w_d_mine[:d_cut]], False))
    bias = _attn_bias(buckets, p["rel_bias"])
    (ya,), (g_gu_a,) = _attn_fwd(qkv, bias, sinks, ([w_gu_mine[:gu_cut[0]]], False))
    (ys, prev_states), (g_gu_b, g_o) = _ssd_fwd(xbc, dt_raw, conv_w, p["conv_b"], dtb_row, alog_row, d_exp,
                                                ([w_gu_mine[gu_cut[0]:gu_cut[1]], w_o_mine], False))
    w_o = g_o.reshape(D_MODEL, D_MODEL)
    x1, (g_gu_c, g_d_b) = _out_proj_fwd(x, ya, ys, z, p["attn_out_norm"], p["ssm_out_norm"], gate1, w_o, tm,
                                        ([w_gu_mine[gu_cut[1]:], w_d_mine[d_cut:]], False))
    dx1, h2, dgu, act, dmlp, acc2 = _mlp_loss(x1, tgt, p["norm2"], scale2, shift2, gate2, p["final_norm"],
                                              (g_gu_a, g_gu_b, g_gu_c), (g_d_a, g_d_b), tmm)
    g_w_gu = _wgrad(dgu, h2, 2 * D_FF // 4, tw, "wgrad_gate_up")
    g_w_d = _wgrad(act, dmlp, D_FF // 2, tw, "wgrad_down")
    dya, dys, dz, u, dmix, acc1 = _out_proj_bwd(dx1, ya, ys, z, p["attn_out_norm"], p["ssm_out_norm"], gate1, w_o, tm)
    g_w_o = _wgrad(u, dmix, D_MODEL, tw, "wgrad_out")
    (dq, dkv, dbias, dsk), (r_d,) = _attn_bwd(qkv, ya, dya, bias, sinks,
                                              ([g_w_d.reshape(N_DEV, D_FF // N_DEV, D_MODEL)], True))
    drel, dsink = _attn_finish(dbias, dsk, buckets)
    (dxbc, ddt, dcw, dvec, dd), (r_gu, r_o) = _ssd_bwd(
        xbc, dt_raw, prev_states, dys, conv_w, p["conv_b"], dtb_row, alog_row, d_exp,
        ([g_w_gu.reshape(N_DEV, 2 * D_FF // N_DEV, D_MODEL), g_w_o.reshape(N_DEV, D_MODEL // N_DEV, D_MODEL)], True))
    gx, h1, dproj, acc0 = _in_proj_bwd(x, dx1, dq, dkv, dz, dxbc, ddt, p["norm1"], scale1, shift1, w_in, tm)
    half = D_MODEL // 2
    slots = lambda g: g[:IN_W].reshape(N_DEV, IN_W // N_DEV, half)
    g_in_a, gathered = _wgrad(dproj, h1, IN_PAD, tw, "wgrad_in_a",
                              ([acc0, acc1, acc2, dcw, dvec, dd, dsink, drel, c], False), g_cols=(half, 0))
    g_in_b, (r_in_a,) = _wgrad(dproj, h1, IN_PAD, tw, "wgrad_in_b", ([slots(g_in_a)], True), g_cols=(half, 1))
    return gx, (r_in_a, slots(g_in_b)), (r_o, r_gu, r_d), gathered


def kernel(x, c, ada_w, ada_b, norm1, w_in, conv_w, conv_b, dt_bias, A_log, D_skip, sinks, attn_out_norm, ssm_out_norm, w_o, norm2, w_gate_up, w_down, rel_bias, final_norm, loss_target, m_ada_w, m_ada_b, m_norm1, m_w_in, m_conv_w, m_conv_b, m_dt_bias, m_A_log, m_D_skip, m_sinks, m_attn_out_norm, m_ssm_out_norm, m_w_o, m_norm2, m_w_gate_up, m_w_down, m_rel_bias, m_final_norm, v_ada_w, v_ada_b, v_norm1, v_w_in, v_conv_w, v_conv_b, v_dt_bias, v_A_log, v_D_skip, v_sinks, v_attn_out_norm, v_ssm_out_norm, v_w_o, v_norm2, v_w_gate_up, v_w_down, v_rel_bias, v_final_norm):
    two_d = lambda a: a if a.ndim == 2 else a.reshape(-1, a.shape[-1])
    small_params = dict(
        ada_b=(ada_b, m_ada_b, v_ada_b), norm1=(norm1, m_norm1, v_norm1), conv_w=(conv_w, m_conv_w, v_conv_w),
        conv_b=(conv_b, m_conv_b, v_conv_b), dt_bias=(dt_bias, m_dt_bias, v_dt_bias), A_log=(A_log, m_A_log, v_A_log),
        D_skip=(D_skip, m_D_skip, v_D_skip), sinks=(sinks, m_sinks, v_sinks),
        attn_out_norm=(attn_out_norm, m_attn_out_norm, v_attn_out_norm),
        ssm_out_norm=(ssm_out_norm, m_ssm_out_norm, v_ssm_out_norm), norm2=(norm2, m_norm2, v_norm2),
        rel_bias=(rel_bias, m_rel_bias, v_rel_bias), final_norm=(final_norm, m_final_norm, v_final_norm))
    small_params = {k: tuple(two_d(a) for a in v) for k, v in small_params.items()}
    S = x.shape[1]
    xs, tgt = x.reshape(S, D_MODEL), loss_target.reshape(S, D_MODEL)
    ada_w2 = ada_w[0]
    chunk = ada_w2.shape[1]
    t_in = [jnp.transpose(a[0]) for a in (w_in, m_w_in, v_w_in)]
    t_gu = [jnp.transpose(a[0]) for a in (w_gate_up, m_w_gate_up, v_w_gate_up)]

    mod = _mod_exchange(c, ada_w2, ada_b.reshape(N_DEV, chunk)).reshape(6, D_MODEL)

    g_in, g_cw = _gather_two_level([t_in[0].astype(WIRE_DTYPE), conv_w[0]], "gather_w_in")
    w_in_full = jnp.pad(g_in.reshape(IN_W, D_MODEL), ((0, IN_PAD - IN_W), (0, 0)))
    conv_w_full = jnp.transpose(g_cw, (1, 0, 2)).reshape(4, XBC_W)

    p = {k: v[0] for k, v in small_params.items()}
    gx, (r_in_a, gw_in_b), (r_o, r_gu, r_d), gathered = _local_step(
        xs, tgt, c, mod, w_in_full, conv_w_full, w_o[0].astype(WIRE_DTYPE), t_gu[0].astype(WIRE_DTYPE),
        w_down[0].astype(WIRE_DTYPE), p)

    (u_gu, u_d, u_o), (r_in_b,) = _reduce_adamw_hosting(
        [r_gu, r_d, r_o], [tuple(t_gu), (w_down[0], m_w_down[0], v_w_down[0]), (w_o[0], m_w_o[0], v_w_o[0])],
        "adamw_big", ([gw_in_b], True))
    r_in = jnp.concatenate([r_in_a, r_in_b], axis=2)

    small, loss, c_all, dmod_all = _small_update(gathered, small_params)

    big = {
        "ada_w": _ada_w_update(c_all, dmod_all, ada_w2, m_ada_w[0], v_ada_w[0]),
        "w_in": [jnp.transpose(a) for a in _reduce_adamw(r_in, *t_in, "adamw_w_in")],
        "w_o": u_o,
        "w_gate_up": [jnp.transpose(a) for a in u_gu],
        "w_down": u_d,
    }
    big.update(small)

    order = ['ada_w', 'ada_b', 'norm1', 'w_in', 'conv_w', 'conv_b', 'dt_bias', 'A_log', 'D_skip', 'sinks',
             'attn_out_norm', 'ssm_out_norm', 'w_o', 'norm2', 'w_gate_up', 'w_down', 'rel_bias', 'final_norm']
    shapes = dict(ada_w=ada_w.shape, ada_b=ada_b.shape, norm1=norm1.shape, w_in=w_in.shape, conv_w=conv_w.shape,
                  conv_b=conv_b.shape, dt_bias=dt_bias.shape, A_log=A_log.shape, D_skip=D_skip.shape,
                  sinks=sinks.shape, attn_out_norm=attn_out_norm.shape, ssm_out_norm=ssm_out_norm.shape,
                  w_o=w_o.shape, norm2=norm2.shape, w_gate_up=w_gate_up.shape, w_down=w_down.shape,
                  rel_bias=rel_bias.shape, final_norm=final_norm.shape)
    outs = [[], [], [], []]
    for name in order:
        for kind in range(4):
            outs[kind].append(big[name][kind].reshape(shapes[name]))
    return (loss.reshape(()), gx.reshape(x.shape), *outs[0], *outs[1], *outs[2], *outs[3])
```

```python
import functools

import numpy as np
import jax
import jax.numpy as jnp
from jax import lax
from jax.experimental import pallas as pl
from jax.experimental.pallas import tpu as pltpu

F32 = jnp.float32
MXU_DTYPE = jnp.bfloat16
WIRE_DTYPE = jnp.bfloat16
HI = lax.Precision.HIGHEST
MESH = pl.DeviceIdType.MESH
N_DEV = 8

D_MODEL = 1024
ATTN_W = 512
KV_W = 128
SSM_W = 512
XBC_W = 1024
N_HEADS = 8
D_STATE = 128
D_FF = 2816
IN_W = 2312
IN_PAD = 2432
BLK = 128
N_BUCKETS = 32
EPS = 1e-6
LANE = 128
HALF = 64

ADAM_LR, ADAM_B1, ADAM_B2, ADAM_EPS, ADAM_WD, ADAM_STEP = 0.001, 0.9, 0.999, 1e-08, 0.01, 10

VMEM_BIG = 56 * 1024 * 1024
WD_CUT = 256
WGU_CUTS = (304, 608)


def _cparams(vmem=None):
    if vmem is None:
        return pltpu.CompilerParams()
    return pltpu.CompilerParams(vmem_limit_bytes=vmem)


def _mm(a, b):
    return jnp.dot(a.astype(MXU_DTYPE), b.astype(MXU_DTYPE), preferred_element_type=F32)


def _mm_nt(a, b):
    return lax.dot_general(a.astype(MXU_DTYPE), b.astype(MXU_DTYPE), (((1,), (1,)), ((), ())),
                           preferred_element_type=F32)


def _mm_tn(a, b):
    return lax.dot_general(a.astype(MXU_DTYPE), b.astype(MXU_DTYPE), (((0,), (0,)), ((), ())),
                           preferred_element_type=F32)


def _mm_hi(a, b):
    return jnp.dot(a, b, precision=HI, preferred_element_type=F32)


def _silu(x):
    return x * jax.nn.sigmoid(x)


def _softplus(x):
    return jnp.maximum(x, 0.0) + jnp.log1p(jnp.exp(-jnp.abs(x)))


def _rms(x, g, n):
    return x * lax.rsqrt(jnp.sum(x * x, axis=-1, keepdims=True) * (1.0 / n) + EPS) * g


def _modnorm(x, g, scale, shift):
    return _rms(x, g, x.shape[-1]) * (1.0 + scale) + shift


def _modnorm_parts(x):
    r = lax.rsqrt(jnp.sum(x * x, axis=-1, keepdims=True) * (1.0 / x.shape[-1]) + EPS)
    return r, x * r


def _modnorm_bwd(r, xhat, g, scale, dy):
    dyg = dy * (g * (1.0 + scale))
    c = jnp.sum(dyg * xhat, axis=-1, keepdims=True) * (1.0 / xhat.shape[-1])
    dx = r * (dyg - xhat * c)
    ct = jnp.sum(dy * xhat, axis=0, keepdims=True)
    return dx, ct * (1.0 + scale), ct * g, jnp.sum(dy, axis=0, keepdims=True)


def _lane_iota(shape):
    return lax.broadcasted_iota(jnp.int32, shape, len(shape) - 1)


def _split_pair(t):
    lane = _lane_iota(t.shape)
    lo = jnp.where(lane < HALF, t, 0.0)
    hi = pltpu.roll(jnp.where(lane >= HALF, t, 0.0), HALF, 1)
    return lo, hi


def _join_pair(lo, hi):
    lane = _lane_iota(lo.shape)
    return jnp.where(lane < HALF, lo, pltpu.roll(hi, HALF, 1))


def _split_heads(t, n_pairs):
    out = []
    for p in range(n_pairs):
        out.extend(_split_pair(t[:, p * LANE:(p + 1) * LANE]))
    return out


def _join_heads(hs):
    return jnp.concatenate([_join_pair(hs[2 * p], hs[2 * p + 1]) for p in range(len(hs) // 2)], axis=1)


def _t5_bucket_table():
    dist = np.arange(BLK)[:, None] + BLK - np.arange(2 * BLK)[None, :]
    n = np.maximum(dist, 0)
    max_exact = N_BUCKETS // 2
    large = max_exact + (np.log(np.maximum(n, 1) / max_exact) / np.log(128 / max_exact)
                         * (N_BUCKETS - max_exact)).astype(np.int32)
    large = np.minimum(large, N_BUCKETS - 1)
    return np.where(n < max_exact, n, large).astype(np.int32)


def _my_pos():
    return lax.axis_index("x"), lax.axis_index("y"), lax.axis_index("c")


def _peer(k):
    x, y, c = _my_pos()
    return (1 - x if k & 4 else x, 1 - y if k & 2 else y, 1 - c if k & 1 else c)


def _lin(pos):
    return 4 * pos[0] + 2 * pos[1] + pos[2]


def _xchg_copies(ins, outs, sems, scatter):
    local_sem, send_sem, recv_sem = sems
    me = _lin(_my_pos())
    local, remote = [], []
    for a in range(len(ins)):
        src = ins[a].at[me] if scatter else ins[a]
        local.append(pltpu.make_async_copy(src, outs[a].at[me], local_sem.at[a]))
    for k in range(1, N_DEV):
        peer = _peer(k)
        for a in range(len(ins)):
            src = ins[a].at[_lin(peer)] if scatter else ins[a]
            remote.append(pltpu.make_async_remote_copy(src, outs[a].at[me], send_sem.at[a, k - 1],
                                                       recv_sem.at[a, k - 1], device_id=peer, device_id_type=MESH))
    return local, remote


def _xchg_start(ins, outs, sems, scatter):
    local, remote = _xchg_copies(ins, outs, sems, scatter)
    for cp in local + remote:
        cp.start()


def _xchg_wait(ins, outs, sems, scatter):
    local, remote = _xchg_copies(ins, outs, sems, scatter)
    for cp in local:
        cp.wait()
    for cp in remote:
        cp.wait_send()
        cp.wait_recv()


def _xchg_shapes(arrs, scatter):
    n = len(arrs)
    if scatter:
        out_shape = [jax.ShapeDtypeStruct(a.shape, a.dtype) for a in arrs]
    else:
        out_shape = [jax.ShapeDtypeStruct((N_DEV,) + a.shape, a.dtype) for a in arrs]
    sems = [pltpu.SemaphoreType.DMA((n,)), pltpu.SemaphoreType.DMA((n, N_DEV - 1)),
            pltpu.SemaphoreType.DMA((n, N_DEV - 1))]
    return out_shape, sems


def _mod_and_gather(c, ada_w, ada_b8, arrs):
    n = len(arrs)
    chunk = ada_w.shape[1]
    out_shape = [jax.ShapeDtypeStruct((N_DEV, 1, chunk), F32)]
    out_shape += [jax.ShapeDtypeStruct((N_DEV,) + a.shape, a.dtype) for a in arrs]
    chips = (2, 4, 6)

    def modulation(c_ref, w_ref, b_ref, out_ref, cbuf, part, s1, r1, s2, r2):
        me = _lin(_my_pos())
        first = []
        for k in range(1, N_DEV):
            cp = pltpu.make_async_remote_copy(c_ref, cbuf.at[me], s1.at[k - 1], r1.at[k - 1],
                                              device_id=_peer(k), device_id_type=MESH)
            cp.start()
            first.append(cp)
        cbuf[me] = c_ref[...]
        for cp in first:
            cp.wait_send()
            cp.wait_recv()
        cond = _silu(jnp.concatenate([cbuf[i] for i in range(N_DEV)], axis=0))
        mod = _mm_hi(cond, w_ref[...]) + b_ref[pl.ds(me, 1), :]
        for j in range(N_DEV):
            part[j] = mod[j:j + 1, :]
        second = []
        for k in range(1, N_DEV):
            peer = _peer(k)
            cp = pltpu.make_async_remote_copy(part.at[_lin(peer)], out_ref.at[me], s2.at[k - 1], r2.at[k - 1],
                                              device_id=peer, device_id_type=MESH)
            cp.start()
            second.append(cp)
        out_ref[me] = part[me]
        for cp in second:
            cp.wait_send()
            cp.wait_recv()

    def body(*refs):
        c_ref, w_ref, b_ref = refs[:3]
        ins = refs[3:3 + n]
        mod_ref = refs[3 + n]
        outs = refs[4 + n:4 + 2 * n]
        cbuf, part, s1, r1, s2, r2, local_sem, send_sem, recv_sem, fsend_sem, frecv_sem = refs[4 + 2 * n:]
        me = _lin(_my_pos())
        sibling = _peer(1)

        def direct(a, k):
            return pltpu.make_async_remote_copy(ins[a], outs[a].at[me], send_sem.at[a, k], recv_sem.at[a, k],
                                                device_id=_peer(k), device_id_type=MESH)

        def handed_on(a, j, origin):
            slot = outs[a].at[origin]
            return pltpu.make_async_remote_copy(slot, slot, fsend_sem.at[a, j], frecv_sem.at[a, j],
                                                device_id=sibling, device_id_type=MESH)

        local = [pltpu.make_async_copy(ins[a], outs[a].at[me], local_sem.at[a]) for a in range(n)]
        first = [direct(a, k) for k in (1,) + chips for a in range(n)]
        for cp in local + first:
            cp.start()
        modulation(c_ref, w_ref, b_ref, mod_ref, cbuf, part, s1, r1, s2, r2)
        passed = []
        for j, k in enumerate(chips):
            for a in range(n):
                direct(a, k).wait_recv()
                cp = handed_on(a, j, _lin(_peer(k)))
                cp.start()
                passed.append(cp)
        for a in range(n):
            direct(a, 1).wait_recv()
            for j, k in enumerate(chips):
                handed_on(a, j, _lin(_peer(k ^ 1))).wait_recv()
        for cp in local:
            cp.wait()
        for cp in first + passed:
            cp.wait_send()

    hbm = pl.BlockSpec(memory_space=pltpu.HBM)
    vm = pl.BlockSpec(memory_space=pltpu.VMEM)
    dma = pltpu.SemaphoreType.DMA
    res = pl.pallas_call(
        body, name="mod_and_gather", out_shape=out_shape, in_specs=[vm, vm, vm] + [hbm] * n,
        out_specs=[vm] + [hbm] * n,
        scratch_shapes=[pltpu.VMEM((N_DEV, 1, D_MODEL), F32), pltpu.VMEM((N_DEV, 1, chunk), F32)]
        + [dma((N_DEV - 1,))] * 4 + [dma((n,)), dma((n, N_DEV)), dma((n, N_DEV)), dma((n, 3)), dma((n, 3))],
    )(c, ada_w, ada_b8, *arrs)
    return res[0], res[1:]


def _hosted_call(body, name, grid, in_specs, out_specs, out_shape, scratch_shapes, args, xchg, cparams):
    arrs, scatter = xchg
    grid = (grid,) if isinstance(grid, int) else tuple(grid)
    n, n_in, n_out, n_scr = len(arrs), len(in_specs), len(out_specs), len(scratch_shapes)
    x_shape, x_sems = _xchg_shapes(arrs, scatter)

    def hosted(*refs):
        ins, refs = refs[:n_in], refs[n_in:]
        x_in, refs = refs[:n], refs[n:]
        outs, refs = refs[:n_out], refs[n_out:]
        x_out, refs = refs[:n], refs[n:]
        scr, sems = refs[:n_scr], refs[n_scr:]
        step = pl.program_id(0)
        for d in range(1, len(grid)):
            step = step * grid[d] + pl.program_id(d)

        @pl.when(step == 0)
        def _():
            _xchg_start(x_in, x_out, sems, scatter)

        body(*ins, *outs, *scr)

        @pl.when(step == int(np.prod(grid)) - 1)
        def _():
            _xchg_wait(x_in, x_out, sems, scatter)

    hbm = pl.BlockSpec(memory_space=pltpu.HBM)
    res = pl.pallas_call(
        hosted, name=name, grid=grid, in_specs=list(in_specs) + [hbm] * n,
        out_specs=list(out_specs) + [hbm] * n, out_shape=list(out_shape) + x_shape,
        scratch_shapes=list(scratch_shapes) + x_sems, compiler_params=cparams,
    )(*args, *arrs)
    return res[:n_out], res[n_out:]


def _row(i):
    return (i, 0)


def _fixed(i):
    return (0, 0)


def _in_proj_fwd(x, norm1, scale1, shift1, w_in, tm, xchg):
    S = x.shape[0]

    def body(x_ref, n_ref, sc_ref, sh_ref, w_ref, qkv_ref, z_ref, xbc_ref, dt_ref):
        h = _modnorm(x_ref[...], n_ref[...], sc_ref[...], sh_ref[...])
        p = _mm_nt(h, w_ref[...])
        qkv_ref[...] = p[:, :768].astype(qkv_ref.dtype)
        z_ref[...] = p[:, 768:1280]
        xbc_ref[...] = p[:, 1280:2304]
        dt_ref[...] = p[:, 2304:IN_PAD]

    vec = pl.BlockSpec((1, D_MODEL), _fixed)
    return _hosted_call(
        body, "in_proj_fwd", S // tm,
        in_specs=[pl.BlockSpec((tm, D_MODEL), _row), vec, vec, vec, pl.BlockSpec((IN_PAD, D_MODEL), _fixed)],
        out_specs=[pl.BlockSpec((tm, 768), _row), pl.BlockSpec((tm, SSM_W), _row),
                   pl.BlockSpec((tm, XBC_W), _row), pl.BlockSpec((tm, LANE), _row)],
        out_shape=[jax.ShapeDtypeStruct((S, 768), MXU_DTYPE), jax.ShapeDtypeStruct((S, SSM_W), F32),
                   jax.ShapeDtypeStruct((S, XBC_W), F32), jax.ShapeDtypeStruct((S, LANE), F32)],
        scratch_shapes=[], args=(x, norm1, scale1, shift1, w_in), xchg=xchg, cparams=_cparams(VMEM_BIG),
    )


def _in_proj_bwd(x, dx1, dq, dkv, dz, dxbc, ddt, norm1, scale1, shift1, w_in, tm):
    S = x.shape[0]

    def body(x_ref, dx1_ref, dq_ref, dkv_ref, dz_ref, dxbc_ref, ddt_ref, n_ref, sc_ref, sh_ref, w_ref,
             gx_ref, h_ref, dp_ref, acc_ref):
        @pl.when(pl.program_id(0) == 0)
        def _():
            acc_ref[...] = jnp.zeros_like(acc_ref)

        dp = jnp.concatenate([dq_ref[...].astype(MXU_DTYPE), dkv_ref[...].astype(MXU_DTYPE),
                              dz_ref[...].astype(MXU_DTYPE), dxbc_ref[...].astype(MXU_DTYPE),
                              ddt_ref[...].astype(MXU_DTYPE)], axis=1)
        dh = _mm(dp, w_ref[...])
        r, xhat = _modnorm_parts(x_ref[...])
        dx, dn, dsc, dsh = _modnorm_bwd(r, xhat, n_ref[...], sc_ref[...], dh)
        gx_ref[...] = dx1_ref[...] + dx
        h_ref[...] = (xhat * n_ref[...] * (1.0 + sc_ref[...]) + sh_ref[...]).astype(h_ref.dtype)
        dp_ref[...] = dp
        acc_ref[0:1, :] += dn
        acc_ref[1:2, :] += dsc
        acc_ref[2:3, :] += dsh

    vec = pl.BlockSpec((1, D_MODEL), _fixed)
    return pl.pallas_call(
        body, name="in_proj_bwd", grid=(S // tm,),
        in_specs=[pl.BlockSpec((tm, D_MODEL), _row), pl.BlockSpec((tm, D_MODEL), _row),
                  pl.BlockSpec((tm, ATTN_W), _row), pl.BlockSpec((tm, 2 * KV_W), _row),
                  pl.BlockSpec((tm, SSM_W), _row), pl.BlockSpec((tm, XBC_W), _row), pl.BlockSpec((tm, LANE), _row),
                  vec, vec, vec, pl.BlockSpec((IN_PAD, D_MODEL), _fixed)],
        out_specs=[pl.BlockSpec((tm, D_MODEL), _row), pl.BlockSpec((tm, D_MODEL), _row),
                   pl.BlockSpec((tm, IN_PAD), _row), pl.BlockSpec((8, D_MODEL), _fixed)],
        out_shape=[jax.ShapeDtypeStruct((S, D_MODEL), F32), jax.ShapeDtypeStruct((S, D_MODEL), MXU_DTYPE),
                   jax.ShapeDtypeStruct((S, IN_PAD), MXU_DTYPE), jax.ShapeDtypeStruct((8, D_MODEL), F32)],
        compiler_params=_cparams(VMEM_BIG),
    )(x, dx1, dq, dkv, dz, dxbc, ddt, norm1, scale1, shift1, w_in)


def _out_stage(ya, ys0, ys1, z0, z1, an, sn0, sn1):
    half = SSM_W // 2
    a = _rms(ya, an, ATTN_W)
    g0 = _rms(ys0 * _silu(z0), sn0, half)
    g1 = _rms(ys1 * _silu(z1), sn1, half)
    return jnp.concatenate([a, g0, g1], axis=1)


def _out_stage_args(ya_ref, ys_ref, z_ref, an_ref, sn_ref):
    half = SSM_W // 2
    return (ya_ref[...], ys_ref[:, :half], ys_ref[:, half:], z_ref[:, :half], z_ref[:, half:],
            an_ref[...], sn_ref[:, :half], sn_ref[:, half:])


def _out_proj_fwd(x, ya, ys, z, an, sn, gate1, w_o, tm, xchg):
    S = x.shape[0]

    def body(x_ref, ya_ref, ys_ref, z_ref, an_ref, sn_ref, g_ref, w_ref, x1_ref):
        u = _out_stage(*_out_stage_args(ya_ref, ys_ref, z_ref, an_ref, sn_ref))
        x1_ref[...] = x_ref[...] + g_ref[...] * _mm(u, w_ref[...])

    half = pl.BlockSpec((tm, ATTN_W), _row)
    hvec = pl.BlockSpec((1, ATTN_W), _fixed)
    (x1,), x_out = _hosted_call(
        body, "out_proj_fwd", S // tm,
        in_specs=[pl.BlockSpec((tm, D_MODEL), _row), half, half, half, hvec, hvec,
                  pl.BlockSpec((1, D_MODEL), _fixed), pl.BlockSpec((D_MODEL, D_MODEL), _fixed)],
        out_specs=[pl.BlockSpec((tm, D_MODEL), _row)],
        out_shape=[jax.ShapeDtypeStruct((S, D_MODEL), F32)],
        scratch_shapes=[], args=(x, ya, ys, z, an, sn, gate1, w_o), xchg=xchg, cparams=_cparams(VMEM_BIG),
    )
    return x1, x_out


def _out_proj_bwd(dx1, ya, ys, z, an, sn, gate1, w_o, tm):
    S = dx1.shape[0]

    def body(dx1_ref, ya_ref, ys_ref, z_ref, an_ref, sn_ref, g_ref, w_ref,
             dya_ref, dys_ref, dz_ref, u_ref, dmix_ref, acc_ref):
        @pl.when(pl.program_id(0) == 0)
        def _():
            acc_ref[...] = jnp.zeros_like(acc_ref)

        u, vjp = jax.vjp(_out_stage, *_out_stage_args(ya_ref, ys_ref, z_ref, an_ref, sn_ref))
        dx1 = dx1_ref[...]
        mix = _mm(u, w_ref[...])
        dmix = dx1 * g_ref[...]
        du = _mm_nt(dmix, w_ref[...])
        dya, dys0, dys1, dz0, dz1, dan, dsn0, dsn1 = vjp(du)
        dya_ref[...] = dya
        dys_ref[...] = jnp.concatenate([dys0, dys1], axis=1)
        dz_ref[...] = jnp.concatenate([dz0, dz1], axis=1)
        u_ref[...] = u.astype(u_ref.dtype)
        dmix_ref[...] = dmix.astype(dmix_ref.dtype)
        acc_ref[0:1, :] += jnp.sum(dx1 * mix, axis=0, keepdims=True)
        acc_ref[1:2, :] += jnp.concatenate([dan, dsn0, dsn1], axis=1)

    half = pl.BlockSpec((tm, ATTN_W), _row)
    hvec = pl.BlockSpec((1, ATTN_W), _fixed)
    full = pl.BlockSpec((tm, D_MODEL), _row)
    return pl.pallas_call(
        body, name="out_proj_bwd", grid=(S // tm,),
        in_specs=[full, half, half, half, hvec, hvec,
                  pl.BlockSpec((1, D_MODEL), _fixed), pl.BlockSpec((D_MODEL, D_MODEL), _fixed)],
        out_specs=[half, half, half, full, full, pl.BlockSpec((8, D_MODEL), _fixed)],
        out_shape=[jax.ShapeDtypeStruct((S, ATTN_W), F32)] * 3
        + [jax.ShapeDtypeStruct((S, D_MODEL), MXU_DTYPE)] * 2 + [jax.ShapeDtypeStruct((8, D_MODEL), F32)],
        compiler_params=_cparams(VMEM_BIG),
    )(dx1, ya, ys, z, an, sn, gate1, w_o)


def _loss_rows(x2, fn, tgt):
    y = _rms(x2, fn, D_MODEL)
    per_row = jnp.sum(jnp.square(y - tgt), axis=1, keepdims=True)
    return jnp.sum(per_row, axis=0, keepdims=True) * (0.5 / D_MODEL)


def _mlp_loss(x1, tgt, norm2, scale2, shift2, gate2, fnorm, w_gu, w_d, tm):
    S = x1.shape[0]
    n_pieces = len(w_gu) + len(w_d)

    def body(*refs):
        x1_ref, t_ref, n_ref, sc_ref, sh_ref, g_ref, fn_ref = refs[:7]
        piece_refs = refs[7:7 + n_pieces]
        dx1_ref, h_ref, dgu_ref, act_ref, dmlp_ref, acc_ref, wgu, wd, wsem = refs[7 + n_pieces:]

        @pl.when(pl.program_id(0) == 0)
        def _():
            acc_ref[...] = jnp.zeros_like(acc_ref)
            copies = []
            for dst, pieces in ((wgu, piece_refs[:len(w_gu)]), (wd, piece_refs[len(w_gu):])):
                shard = sum(p.shape[1] for p in pieces)
                off = 0
                for p in pieces:
                    for j in range(N_DEV):
                        copies.append(pltpu.make_async_copy(p.at[j], dst.at[pl.ds(j * shard + off, p.shape[1])],
                                                            wsem.at[len(copies)]))
                    off += p.shape[1]
            for cp in copies:
                cp.start()
            for cp in copies:
                cp.wait()

        x1 = x1_ref[...]
        gate2 = g_ref[...]
        h, vjp_h = jax.vjp(_modnorm, x1, n_ref[...], sc_ref[...], sh_ref[...])
        hb = h.astype(MXU_DTYPE)
        gu = _mm_nt(hb, wgu[...])
        g, u = gu[:, :D_FF], gu[:, D_FF:]
        sg = jax.nn.sigmoid(g)
        silu_g = g * sg
        act = (silu_g * u).astype(MXU_DTYPE)
        mlp = _mm(act, wd[...])
        x2 = x1 + gate2 * mlp
        loss, vjp_loss = jax.vjp(_loss_rows, x2, fn_ref[...], t_ref[...])
        dx2, dfn, _ = vjp_loss(jnp.ones((1, 1), F32))
        dmlp = (dx2 * gate2).astype(MXU_DTYPE)
        dact = _mm_nt(dmlp, wd[...])
        dg = dact * u * (sg * (1.0 + g * (1.0 - sg)))
        du = dact * silu_g
        dgu = jnp.concatenate([dg, du], axis=1).astype(MXU_DTYPE)
        dh = _mm(dgu, wgu[...])
        dx, dn, dsc, dsh = vjp_h(dh)
        dx1_ref[...] = dx2 + dx
        h_ref[...] = hb
        dgu_ref[...] = dgu
        act_ref[...] = act
        dmlp_ref[...] = dmlp
        acc_ref[0:1, :] += dn
        acc_ref[1:2, :] += dsc
        acc_ref[2:3, :] += dsh
        acc_ref[3:4, :] += jnp.sum(dx2 * mlp, axis=0, keepdims=True)
        acc_ref[4:5, :] += dfn
        acc_ref[5:6, :] += jnp.broadcast_to(loss, (1, D_MODEL))

    full = pl.BlockSpec((tm, D_MODEL), _row)
    vec = pl.BlockSpec((1, D_MODEL), _fixed)
    anyspec = pl.BlockSpec(memory_space=pl.ANY)
    return pl.pallas_call(
        body, name="mlp_loss", grid=(S // tm,),
        in_specs=[full, full, vec, vec, vec, vec, vec] + [anyspec] * n_pieces,
        out_specs=[full, full, pl.BlockSpec((tm, 2 * D_FF), _row), pl.BlockSpec((tm, D_FF), _row), full,
                   pl.BlockSpec((8, D_MODEL), _fixed)],
        out_shape=[jax.ShapeDtypeStruct((S, D_MODEL), F32), jax.ShapeDtypeStruct((S, D_MODEL), MXU_DTYPE),
                   jax.ShapeDtypeStruct((S, 2 * D_FF), MXU_DTYPE), jax.ShapeDtypeStruct((S, D_FF), MXU_DTYPE),
                   jax.ShapeDtypeStruct((S, D_MODEL), MXU_DTYPE), jax.ShapeDtypeStruct((8, D_MODEL), F32)],
        scratch_shapes=[pltpu.VMEM((2 * D_FF, D_MODEL), MXU_DTYPE), pltpu.VMEM((D_FF, D_MODEL), MXU_DTYPE),
                        pltpu.SemaphoreType.DMA((N_DEV * n_pieces,))],
        compiler_params=_cparams(VMEM_BIG),
    )(x1, tgt, norm2, scale2, shift2, gate2, fnorm, *w_gu, *w_d)


def _wgrad(a, g, tk, ts, name, xchg=None, g_cols=None):
    S, K = a.shape
    N, col = (g.shape[1], 0) if g_cols is None else g_cols
    ns = S // ts

    def body(a_ref, g_ref, o_ref, acc_ref):
        s = pl.program_id(1)

        @pl.when(s == 0)
        def _():
            acc_ref[...] = jnp.zeros_like(acc_ref)

        acc_ref[...] += _mm_tn(a_ref[...], g_ref[...])

        @pl.when(s == ns - 1)
        def _():
            o_ref[...] = acc_ref[...].astype(o_ref.dtype)

    in_specs = [pl.BlockSpec((ts, tk), lambda j, s: (s, j)), pl.BlockSpec((ts, N), lambda j, s: (s, col))]
    out_spec = pl.BlockSpec((tk, N), lambda j, s: (j, 0))
    out_shape = jax.ShapeDtypeStruct((K, N), WIRE_DTYPE)
    scratch = [pltpu.VMEM((tk, N), F32)]
    if xchg is None:
        return pl.pallas_call(body, name=name, grid=(K // tk, ns), in_specs=in_specs, out_specs=out_spec,
                              out_shape=out_shape, scratch_shapes=scratch, compiler_params=_cparams(VMEM_BIG))(a, g)
    (out,), x_out = _hosted_call(body, name, (K // tk, ns), in_specs, [out_spec], [out_shape], scratch, (a, g), xchg,
                                 _cparams(VMEM_BIG))
    return out, x_out


MASKED = -1e30
QK_SCALE = HALF ** -0.5


def _attn_bias(buckets, rel_bias):
    def body(bk_ref, relb_ref, out_ref):
        bk = bk_ref[...]
        i = lax.broadcasted_iota(jnp.int32, (BLK, 2 * BLK), 0)
        j = lax.broadcasted_iota(jnp.int32, (BLK, 2 * BLK), 1)
        window = (j > i) & (j <= i + BLK)
        for h in range(N_HEADS):
            acc = jnp.zeros((BLK, 2 * BLK), F32)
            for b in range(N_BUCKETS):
                acc = jnp.where(bk == b, relb_ref[b, h], acc)
            out_ref[0, h] = jnp.where(window, acc, MASKED)
            out_ref[1, h] = jnp.where(window & (j >= BLK), acc, MASKED)

    return pl.pallas_call(
        body, name="attn_bias", out_shape=jax.ShapeDtypeStruct((2, N_HEADS, BLK, 2 * BLK), F32),
        in_specs=[pl.BlockSpec(memory_space=pltpu.VMEM), pl.BlockSpec(memory_space=pltpu.SMEM)],
    )(buckets, rel_bias)


def _attn_kv(kvp_ref, kvc_ref):
    kvp = kvp_ref[...].astype(F32)
    kvc = kvc_ref[...].astype(F32)
    kp, kc = _split_pair(kvp[:, :LANE]), _split_pair(kvc[:, :LANE])
    vp, vc = _split_pair(kvp[:, LANE:]), _split_pair(kvc[:, LANE:])
    k_pads = [jnp.concatenate([kp[g], kc[g]], axis=0).astype(MXU_DTYPE) for g in range(2)]
    v_pads = [jnp.concatenate([vp[g], vc[g]], axis=0).astype(MXU_DTYPE) for g in range(2)]
    return k_pads, v_pads


def _attn_fwd(qkv, bias, sinks, xchg):
    S = qkv.shape[0]
    nb = S // BLK

    def body(q_ref, kvp_ref, kvc_ref, bias_ref, sinks_ref, y_ref):
        first = jnp.where(pl.program_id(0) == 0, 1, 0)
        q_heads = _split_heads(q_ref[...].astype(F32) * QK_SCALE, 4)
        k_pads, v_pads = _attn_kv(kvp_ref, kvc_ref)
        heads = range(N_HEADS)
        s = [_mm_nt(q_heads[h].astype(MXU_DTYPE), k_pads[h // 4]) + bias_ref[first, h] for h in heads]
        m = [jnp.maximum(jnp.max(s[h], axis=-1, keepdims=True), sinks_ref[h]) for h in heads]
        p = [jnp.exp(s[h] - m[h]) for h in heads]
        rinv = [1.0 / (jnp.sum(p[h], axis=-1, keepdims=True) + jnp.exp(sinks_ref[h] - m[h])) for h in heads]
        y_ref[...] = _join_heads([_mm(p[h], v_pads[h // 4]) * rinv[h] for h in heads])

    smem = pl.BlockSpec(memory_space=pltpu.SMEM)
    return _hosted_call(
        body, "attn_fwd", nb,
        in_specs=[pl.BlockSpec((BLK, ATTN_W), _row),
                  pl.BlockSpec((BLK, 2 * KV_W), lambda i: (jnp.maximum(i - 1, 0), 2)),
                  pl.BlockSpec((BLK, 2 * KV_W), lambda i: (i, 2)),
                  pl.BlockSpec((2, N_HEADS, BLK, 2 * BLK), lambda i: (0, 0, 0, 0)), smem],
        out_specs=[pl.BlockSpec((BLK, ATTN_W), _row)],
        out_shape=[jax.ShapeDtypeStruct((S, ATTN_W), F32)],
        scratch_shapes=[],
        args=(qkv, qkv, qkv, bias, sinks), xchg=xchg, cparams=_cparams(),
    )


def _attn_bwd(qkv, y, dy, bias, sinks, xchg):
    S = qkv.shape[0]
    nb = S // BLK

    def body(q_ref, kvp_ref, kvc_ref, y_ref, dy_ref, bias_ref, sinks_ref, dq_ref, dkv_ref, dbias_ref, dsk_ref, carry_ref):
        i = pl.program_id(0)

        @pl.when(i == 0)
        def _():
            dbias_ref[...] = jnp.zeros_like(dbias_ref)
            dsk_ref[...] = jnp.zeros_like(dsk_ref)
            carry_ref[...] = jnp.zeros_like(carry_ref)

        first = jnp.where(i == nb - 1, 1, 0)
        q_heads = _split_heads(q_ref[...].astype(F32) * QK_SCALE, 4)
        k_pads, v_pads = _attn_kv(kvp_ref, kvc_ref)
        y_heads = _split_heads(y_ref[...], 4)
        dy_heads = _split_heads(dy_ref[...], 4)
        heads = range(N_HEADS)
        qs = [q_heads[h].astype(MXU_DTYPE) for h in heads]
        s = [_mm_nt(qs[h], k_pads[h // 4]) + bias_ref[first, h] for h in heads]
        m = [jnp.maximum(jnp.max(s[h], axis=-1, keepdims=True), sinks_ref[h]) for h in heads]
        p = [jnp.exp(s[h] - m[h]) for h in heads]
        esink = [jnp.exp(sinks_ref[h] - m[h]) for h in heads]
        rinv = [1.0 / (jnp.sum(p[h], axis=-1, keepdims=True) + esink[h]) for h in heads]
        t = [dy_heads[h] * rinv[h] for h in heads]
        delta = [jnp.sum(t[h] * y_heads[h], axis=-1, keepdims=True) for h in heads]
        tb = [t[h].astype(MXU_DTYPE) for h in heads]
        dp = [_mm_nt(tb[h], v_pads[h // 4]) for h in heads]
        ds = [p[h] * (dp[h] - delta[h]) for h in heads]
        for h in heads:
            dbias_ref[h] += ds[h]
            dsk_ref[h] -= esink[h] * delta[h]
        dsb = [ds[h].astype(MXU_DTYPE) for h in heads]
        pb = [p[h].astype(MXU_DTYPE) for h in heads]
        dq_heads = [_mm(dsb[h], k_pads[h // 4]) * QK_SCALE for h in heads]
        dk_pads = [_mm_tn(jnp.concatenate(dsb[4 * g:4 * g + 4], axis=0), jnp.concatenate(qs[4 * g:4 * g + 4], axis=0))
                   for g in range(2)]
        dv_pads = [_mm_tn(jnp.concatenate(pb[4 * g:4 * g + 4], axis=0), jnp.concatenate(tb[4 * g:4 * g + 4], axis=0))
                   for g in range(2)]
        dq_ref[...] = _join_heads(dq_heads)
        dk_prev = _join_pair(dk_pads[0][:BLK], dk_pads[1][:BLK])
        dk_cur = _join_pair(dk_pads[0][BLK:], dk_pads[1][BLK:])
        dv_prev = _join_pair(dv_pads[0][:BLK], dv_pads[1][:BLK])
        dv_cur = _join_pair(dv_pads[0][BLK:], dv_pads[1][BLK:])
        dkv_ref[...] = jnp.concatenate([dk_cur, dv_cur], axis=1) + carry_ref[...]
        carry_ref[...] = jnp.concatenate([dk_prev, dv_prev], axis=1)

    smem = pl.BlockSpec(memory_space=pltpu.SMEM)
    rev = lambda i: (nb - 1 - i, 0)
    return _hosted_call(
        body, "attn_bwd", nb,
        in_specs=[pl.BlockSpec((BLK, ATTN_W), rev),
                  pl.BlockSpec((BLK, 2 * KV_W), lambda i: (jnp.maximum(nb - 2 - i, 0), 2)),
                  pl.BlockSpec((BLK, 2 * KV_W), lambda i: (nb - 1 - i, 2)),
                  pl.BlockSpec((BLK, ATTN_W), rev), pl.BlockSpec((BLK, ATTN_W), rev),
                  pl.BlockSpec((2, N_HEADS, BLK, 2 * BLK), lambda i: (0, 0, 0, 0)), smem],
        out_specs=[pl.BlockSpec((BLK, ATTN_W), rev), pl.BlockSpec((BLK, 2 * KV_W), rev),
                   pl.BlockSpec((N_HEADS, BLK, 2 * BLK), lambda i: (0, 0, 0)),
                   pl.BlockSpec((N_HEADS, BLK, 1), lambda i: (0, 0, 0))],
        out_shape=[jax.ShapeDtypeStruct((S, ATTN_W), F32), jax.ShapeDtypeStruct((S, 2 * KV_W), F32),
                   jax.ShapeDtypeStruct((N_HEADS, BLK, 2 * BLK), F32), jax.ShapeDtypeStruct((N_HEADS, BLK, 1), F32)],
        scratch_shapes=[pltpu.VMEM((BLK, 2 * KV_W), F32)],
        args=(qkv, qkv, qkv, y, dy, bias, sinks), xchg=xchg, cparams=_cparams(),
    )


def _attn_finish(dbias, dsk, buckets):
    def body(db_ref, dsk_ref, bk_ref, drel_ref, dsink_ref):
        bk = bk_ref[...]
        r = lax.broadcasted_iota(jnp.int32, (N_BUCKETS, LANE), 0)
        l = lax.broadcasted_iota(jnp.int32, (N_BUCKETS, LANE), 1)
        row = lax.broadcasted_iota(jnp.int32, (N_HEADS, LANE), 0)
        res = jnp.zeros((N_BUCKETS, LANE), F32)
        dsink = jnp.zeros((N_HEADS, LANE), F32)
        for h in range(N_HEADS):
            db = db_ref[h]
            for b in range(N_BUCKETS):
                v = jnp.sum(jnp.sum(jnp.where(bk == b, db, 0.0), axis=1, keepdims=True), axis=0, keepdims=True)
                res = res + jnp.where((r == b) & (l == h), v, 0.0)
            dsink = dsink + jnp.where(row == h, jnp.sum(dsk_ref[h], axis=0, keepdims=True), 0.0)
        drel_ref[...] = res
        dsink_ref[...] = dsink

    return pl.pallas_call(body, name="attn_finish",
                          out_shape=[jax.ShapeDtypeStruct((N_BUCKETS, LANE), F32),
                                     jax.ShapeDtypeStruct((N_HEADS, LANE), F32)])(dbias, dsk, buckets)


def _ssd_consts():
    r = lax.broadcasted_iota(jnp.int32, (BLK, BLK), 0)
    c = lax.broadcasted_iota(jnp.int32, (BLK, BLK), 1)
    causal = c <= r
    upper = (r <= c).astype(F32)
    last = r == BLK - 1
    head = lax.broadcasted_iota(jnp.int32, (N_HEADS, BLK), 0)
    return causal, upper, last, head


def _ssd_chunk(xs, bg, cg, dt_raw_t, prev, dtb, alog, d_rows, consts):
    causal, upper, last, head = consts
    dt_t = _softplus(dt_raw_t + dtb)
    acs_t = _mm_hi(dt_t * (-jnp.exp(alog)), upper)
    cb = [_mm_nt(cg[g], bg[g]) for g in range(2)]
    heads = range(N_HEADS)
    dt_row = [jnp.sum(jnp.where(head == h, dt_t, 0.0), axis=0, keepdims=True) for h in heads]
    a_row = [jnp.sum(jnp.where(head == h, acs_t, 0.0), axis=0, keepdims=True) for h in heads]
    a_rb = [jnp.broadcast_to(a_row[h], (BLK, BLK)) for h in heads]
    a_b = [a_rb[h].T for h in heads]
    a_last = [jnp.sum(jnp.where(last, a_b[h], 0.0), axis=0, keepdims=True) for h in heads]
    w = [cb[h // 4] * jnp.exp(jnp.where(causal, a_b[h] - a_rb[h], -1e30)) * dt_row[h] for h in heads]
    f_b = [jnp.broadcast_to(dt_row[h] * jnp.exp(a_last[h] - a_row[h]), (BLK, BLK)).T for h in heads]
    y_in = [_mm(w[h], xs[h]) for h in heads]
    y_off = [_mm(cg[h // 4], prev[h]) * jnp.exp(a_b[h]) for h in heads]
    st = [_mm_tn(bg[h // 4], xs[h] * f_b[h]) for h in heads]
    ys = [y_in[h] + y_off[h] + d_rows[h] * xs[h] for h in heads]
    hs = [prev[h] * jnp.exp(a_last[h]) + st[h] for h in heads]
    return tuple(ys), tuple(hs)


def _ssd_chunk_bwd(xs, bg, cg, dt_raw_t, prev, dtb, alog, d_rows, dys, dhs, consts):
    causal, upper, last, head = consts
    heads, groups = range(N_HEADS), range(2)
    lane = _lane_iota((BLK, BLK))
    lane_row = _lane_iota((1, BLK))
    pre_dt = dt_raw_t + dtb
    dt_t = _softplus(pre_dt)
    a_neg = -jnp.exp(alog)
    acs_t = _mm_hi(dt_t * a_neg, upper)
    pick = lambda t, h: jnp.sum(jnp.where(head == h, t, 0.0), axis=0, keepdims=True)
    full_sum = lambda t: jnp.sum(jnp.sum(t, axis=1, keepdims=True), axis=0, keepdims=True)
    dt_row = [pick(dt_t, h) for h in heads]
    a_row = [pick(acs_t, h) for h in heads]
    a_rb = [jnp.broadcast_to(a_row[h], (BLK, BLK)) for h in heads]
    a_b = [a_rb[h].T for h in heads]
    a_last = [jnp.sum(jnp.where(last, a_b[h], 0.0), axis=0, keepdims=True) for h in heads]
    lm = [jnp.exp(jnp.where(causal, a_b[h] - a_rb[h], -1e30)) for h in heads]
    cgb = [cg[g].astype(MXU_DTYPE) for g in groups]
    bgb = [bg[g].astype(MXU_DTYPE) for g in groups]
    cb = [_mm_nt(cgb[g], bgb[g]) for g in groups]
    u = [cb[h // 4] * lm[h] for h in heads]
    w = [(u[h] * dt_row[h]).astype(MXU_DTYPE) for h in heads]
    e_row = [jnp.exp(a_last[h] - a_row[h]) for h in heads]
    f_row = [dt_row[h] * e_row[h] for h in heads]
    f_b = [jnp.broadcast_to(f_row[h], (BLK, BLK)).T for h in heads]
    e_b = [jnp.exp(a_b[h]) for h in heads]
    el = [jnp.exp(a_last[h]) for h in heads]
    xb = [xs[h].astype(MXU_DTYPE) for h in heads]
    dyb = [dys[h].astype(MXU_DTYPE) for h in heads]
    prevb = [prev[h].astype(MXU_DTYPE) for h in heads]
    dstb = [dhs[h].astype(MXU_DTYPE) for h in heads]
    gmat = [_mm(cgb[h // 4], prevb[h]) for h in heads]
    dw = [_mm_nt(dyb[h], xb[h]) for h in heads]
    dg = [dys[h] * e_b[h] for h in heads]
    dgb = [dg[h].astype(MXU_DTYPE) for h in heads]
    dxf = [_mm(bgb[h // 4], dstb[h]) for h in heads]
    xfb = [(xs[h] * f_b[h]).astype(MXU_DTYPE) for h in heads]
    dxs = [_mm_tn(w[h], dyb[h]) + d_rows[h] * dys[h] + f_b[h] * dxf[h] for h in heads]
    dd_rows = [jnp.sum(dys[h] * xs[h], axis=0, keepdims=True) for h in heads]
    dprev = [_mm_tn(cgb[h // 4], dgb[h]) + dhs[h] * el[h] for h in heads]
    dcg_h = [_mm_nt(dgb[h], prevb[h]) for h in heads]
    dbg_h = [_mm_nt(xfb[h], dstb[h]) for h in heads]
    zt = [dw[h] * u[h] for h in heads]
    dseg = [zt[h] * dt_row[h] for h in heads]
    dcb_h = [dw[h] * lm[h] * dt_row[h] for h in heads]
    dcb = [(dcb_h[4 * g] + dcb_h[4 * g + 1] + dcb_h[4 * g + 2] + dcb_h[4 * g + 3]).astype(MXU_DTYPE) for g in groups]
    dcg = [dcg_h[4 * g] + dcg_h[4 * g + 1] + dcg_h[4 * g + 2] + dcg_h[4 * g + 3] + _mm(dcb[g], bgb[g]) for g in groups]
    dbg = [dbg_h[4 * g] + dbg_h[4 * g + 1] + dbg_h[4 * g + 2] + dbg_h[4 * g + 3] + _mm_tn(dcb[g], cgb[g])
           for g in groups]
    r1 = [jnp.sum(dg[h] * gmat[h] + dseg[h], axis=1, keepdims=True) for h in heads]
    r2 = [jnp.sum(dxf[h] * xs[h], axis=1, keepdims=True) for h in heads]
    tt = [jnp.where(lane < HALF, jnp.broadcast_to(r1[h], (BLK, BLK)), jnp.broadcast_to(r2[h], (BLK, BLK))).T
          for h in heads]
    r1_row = [tt[h][0:1, :] for h in heads]
    r2_row = [tt[h][HALF:HALF + 1, :] for h in heads]
    d_el = [full_sum(dhs[h] * prev[h]) for h in heads]
    da_last = [jnp.sum(r2_row[h] * f_row[h], axis=1, keepdims=True) + el[h] * d_el[h] for h in heads]
    da_row = [r1_row[h] - jnp.sum(dseg[h], axis=0, keepdims=True) - r2_row[h] * f_row[h]
              + jnp.where(lane_row == BLK - 1, da_last[h], 0.0) for h in heads]
    ddt_row = [jnp.sum(zt[h], axis=0, keepdims=True) + r2_row[h] * e_row[h] for h in heads]
    da_t = jnp.zeros((N_HEADS, BLK), F32)
    ddt_t = jnp.zeros((N_HEADS, BLK), F32)
    for h in heads:
        da_t = jnp.where(head == h, da_row[h], da_t)
        ddt_t = jnp.where(head == h, ddt_row[h], ddt_t)
    d_dta = _mm_hi(da_t, causal.astype(F32))
    dalog = d_dta * dt_t * a_neg
    draw = (ddt_t + d_dta * a_neg) * jax.nn.sigmoid(pre_dt)
    return dxs, dbg, dcg, draw, dprev, draw, dalog, dd_rows


def _dt_rows(dt_blk):
    return dt_blk.T[:N_HEADS]


def _silu_grad(x):
    s = jax.nn.sigmoid(x)
    return s * (1.0 + x * (1.0 - s))


def _conv_pre(halo, blk, cw_ref, cb_ref):
    ext = jnp.concatenate([halo, blk], axis=0)
    taps = [pltpu.roll(ext, 3 - k, 0)[8:] for k in range(3)] + [blk]
    pre = cb_ref[...] + cw_ref[0:1, :] * taps[0]
    for k in range(1, 4):
        pre = pre + cw_ref[k:k + 1, :] * taps[k]
    return pre, taps


def _ssd_split(pre):
    heads = _split_heads(pre[:, :SSM_W], 4)
    pb = [pre[:, SSM_W + g * D_STATE:SSM_W + (g + 1) * D_STATE] for g in range(2)]
    pc = [pre[:, SSM_W + 2 * D_STATE + g * D_STATE:SSM_W + 2 * D_STATE + (g + 1) * D_STATE] for g in range(2)]
    return heads, pb, pc


def _ssd_fwd(xbc, dt_raw, conv_w, conv_b, dtb_row, alog_row, d_exp, xchg):
    S = xbc.shape[0]
    nc = S // BLK

    def body(xbc_ref, halo_ref, dt_ref, cw_ref, cb_ref, dtb_ref, alog_ref, d_ref, y_ref, prev_ref, state_ref):
        i = pl.program_id(0)

        @pl.when(i == 0)
        def _():
            state_ref[...] = jnp.zeros_like(state_ref)

        halo = halo_ref[...] * jnp.where(i > 0, 1.0, 0.0)
        pre, _ = _conv_pre(halo, xbc_ref[...], cw_ref, cb_ref)
        heads, pb, pc = _ssd_split(_silu(pre))
        prev = [state_ref[h] for h in range(N_HEADS)]
        for h in range(N_HEADS):
            prev_ref[0, h] = prev[h]
        d_rows = [d_ref[h:h + 1, :] for h in range(N_HEADS)]
        ys, hs = _ssd_chunk(heads, pb, pc, _dt_rows(dt_ref[...]), prev, dtb_ref[...], alog_ref[...], d_rows,
                            _ssd_consts())
        for h in range(N_HEADS):
            state_ref[h] = hs[h]
        y_ref[...] = _join_heads(ys)

    vec = pl.BlockSpec((N_HEADS, LANE), _fixed)
    return _hosted_call(
        body, "ssd_fwd", nc,
        in_specs=[pl.BlockSpec((BLK, XBC_W), _row),
                  pl.BlockSpec((8, XBC_W), lambda i: (jnp.maximum(i * (BLK // 8) - 1, 0), 0)),
                  pl.BlockSpec((BLK, LANE), _row),
                  pl.BlockSpec((4, XBC_W), _fixed), pl.BlockSpec((1, XBC_W), _fixed), vec, vec,
                  pl.BlockSpec((N_HEADS, LANE), _fixed)],
        out_specs=[pl.BlockSpec((BLK, SSM_W), _row),
                   pl.BlockSpec((1, N_HEADS, D_STATE, LANE), lambda i: (i, 0, 0, 0))],
        out_shape=[jax.ShapeDtypeStruct((S, SSM_W), F32), jax.ShapeDtypeStruct((nc, N_HEADS, D_STATE, LANE), F32)],
        scratch_shapes=[pltpu.VMEM((N_HEADS, D_STATE, LANE), F32)],
        args=(xbc, xbc, dt_raw, conv_w, conv_b, dtb_row, alog_row, d_exp), xchg=xchg, cparams=_cparams(),
    )


def _ssd_bwd(xbc, dt_raw, prev_states, dy, conv_w, conv_b, dtb_row, alog_row, d_exp, xchg):
    S = xbc.shape[0]
    nc = S // BLK

    def body(xbc_ref, halo_ref, dt_ref, prev_ref, dy_ref, cw_ref, cb_ref, dtb_ref, alog_ref, d_ref,
             dxbc_ref, ddt_ref, dcw_ref, dvec_ref, dd_ref, gstate_ref, ghalo_ref):
        i = pl.program_id(0)
        c = nc - 1 - i

        @pl.when(i == 0)
        def _():
            gstate_ref[...] = jnp.zeros_like(gstate_ref)
            ghalo_ref[...] = jnp.zeros_like(ghalo_ref)
            dcw_ref[...] = jnp.zeros_like(dcw_ref)
            dvec_ref[...] = jnp.zeros_like(dvec_ref)
            dd_ref[...] = jnp.zeros_like(dd_ref)

        halo = halo_ref[...] * jnp.where(c > 0, 1.0, 0.0)
        pre, taps = _conv_pre(halo, xbc_ref[...], cw_ref, cb_ref)
        heads, pb, pc = _ssd_split(_silu(pre))
        prev = [prev_ref[0, h] for h in range(N_HEADS)]
        d_rows = [d_ref[h:h + 1, :] for h in range(N_HEADS)]
        dys = _split_heads(dy_ref[...], 4)
        dhs = [gstate_ref[h] for h in range(N_HEADS)]
        dheads, dpb, dpc, ddt_t, dprev, ddtb, dalog, dd_rows = _ssd_chunk_bwd(
            heads, pb, pc, _dt_rows(dt_ref[...]), prev, dtb_ref[...], alog_ref[...], d_rows, dys, dhs, _ssd_consts())
        for h in range(N_HEADS):
            gstate_ref[h] = dprev[h]
            dd_ref[h:h + 1, :] += dd_rows[h]
        ddt_ref[...] = jnp.concatenate([ddt_t, jnp.zeros((BLK - N_HEADS, BLK), F32)], axis=0).T
        dvec_ref[0:N_HEADS, :] += ddtb
        dvec_ref[N_HEADS:, :] += dalog
        dpre = jnp.concatenate([_join_heads(dheads)] + list(dpb) + list(dpc), axis=1) * _silu_grad(pre)
        zeros8 = jnp.zeros((8, XBC_W), F32)
        dpe = jnp.concatenate([zeros8, dpre, zeros8], axis=0)
        n_ext = 16 + BLK
        dext = cw_ref[3:4, :] * dpe[:8 + BLK]
        dcw_ref[3:4, :] += jnp.sum(dpre * taps[3], axis=0, keepdims=True)
        for k in range(3):
            dext = dext + cw_ref[k:k + 1, :] * pltpu.roll(dpe, n_ext - (3 - k), 0)[:8 + BLK]
            dcw_ref[k:k + 1, :] += jnp.sum(dpre * taps[k], axis=0, keepdims=True)
        dcw_ref[4:5, :] += jnp.sum(dpre, axis=0, keepdims=True)
        dxbc_ref[...] = dext[8:, :]
        dxbc_ref[BLK - 8:BLK, :] += ghalo_ref[...]
        ghalo_ref[...] = dext[:8, :]

    vec = pl.BlockSpec((N_HEADS, LANE), _fixed)
    rev = lambda i: (nc - 1 - i, 0)
    return _hosted_call(
        body, "ssd_bwd", nc,
        in_specs=[pl.BlockSpec((BLK, XBC_W), rev),
                  pl.BlockSpec((8, XBC_W), lambda i: (jnp.maximum((nc - 1 - i) * (BLK // 8) - 1, 0), 0)),
                  pl.BlockSpec((BLK, LANE), rev),
                  pl.BlockSpec((1, N_HEADS, D_STATE, LANE), lambda i: (nc - 1 - i, 0, 0, 0)),
                  pl.BlockSpec((BLK, SSM_W), rev),
                  pl.BlockSpec((4, XBC_W), _fixed), pl.BlockSpec((1, XBC_W), _fixed), vec, vec,
                  pl.BlockSpec((N_HEADS, LANE), _fixed)],
        out_specs=[pl.BlockSpec((BLK, XBC_W), rev), pl.BlockSpec((BLK, LANE), rev),
                   pl.BlockSpec((8, XBC_W), _fixed), pl.BlockSpec((2 * N_HEADS, LANE), _fixed),
                   pl.BlockSpec((N_HEADS, LANE), _fixed)],
        out_shape=[jax.ShapeDtypeStruct((S, XBC_W), F32), jax.ShapeDtypeStruct((S, LANE), F32),
                   jax.ShapeDtypeStruct((8, XBC_W), F32), jax.ShapeDtypeStruct((2 * N_HEADS, LANE), F32),
                   jax.ShapeDtypeStruct((N_HEADS, LANE), F32)],
        scratch_shapes=[pltpu.VMEM((N_HEADS, D_STATE, LANE), F32), pltpu.VMEM((8, XBC_W), F32)],
        args=(xbc, xbc, dt_raw, prev_states, dy, conv_w, conv_b, dtb_row, alog_row, d_exp), xchg=xchg,
        cparams=_cparams(VMEM_BIG),
    )


def _adamw_math(w, g, m, v):
    m = ADAM_B1 * m + (1.0 - ADAM_B1) * g
    v = ADAM_B2 * v + (1.0 - ADAM_B2) * jnp.square(g)
    m_hat = m / (1.0 - ADAM_B1 ** ADAM_STEP)
    v_hat = v / (1.0 - ADAM_B2 ** ADAM_STEP)
    delta = -ADAM_LR * (m_hat / (jnp.sqrt(v_hat) + ADAM_EPS) + ADAM_WD * w)
    return delta, m, v


def _reduce_adamw(parts, w, m, v, name):
    R, C = w.shape

    def body(p_ref, w_ref, m_ref, v_ref, g_ref, d_ref, nm_ref, nv_ref):
        g = p_ref[0].astype(F32)
        for i in range(1, N_DEV):
            g = g + p_ref[i].astype(F32)
        d, nm, nv = _adamw_math(w_ref[...], g, m_ref[...], v_ref[...])
        g_ref[...] = g
        d_ref[...] = d
        nm_ref[...] = nm
        nv_ref[...] = nv

    if R % 16 == 0:
        tr = max(t for t in range(16, 257, 16) if R % t == 0)
        n, blk, pblk = R // tr, pl.BlockSpec((tr, C), _row), pl.BlockSpec((N_DEV, tr, C), lambda i: (0, i, 0))
    else:
        tl = 256
        n, blk, pblk = C // tl, pl.BlockSpec((R, tl), lambda i: (0, i)), pl.BlockSpec((N_DEV, R, tl),
                                                                                      lambda i: (0, 0, i))
    return pl.pallas_call(
        body, name=name, grid=(n,), in_specs=[pblk, blk, blk, blk],
        out_specs=[blk] * 4, out_shape=[jax.ShapeDtypeStruct((R, C), F32)] * 4,
    )(parts, w, m, v)


def _reduce_adamw_hosting(parts_list, wmv_list, name, xchg):
    n_arr = len(parts_list)
    C = wmv_list[0][0].shape[1]
    tl = 256

    def body(*refs):
        p_refs, wmv_refs, o_refs = refs[:n_arr], refs[n_arr:4 * n_arr], refs[4 * n_arr:]
        for k in range(n_arr):
            g = p_refs[k][0].astype(F32)
            for i in range(1, N_DEV):
                g = g + p_refs[k][i].astype(F32)
            w_ref, m_ref, v_ref = wmv_refs[3 * k:3 * k + 3]
            d, nm, nv = _adamw_math(w_ref[...], g, m_ref[...], v_ref[...])
            for o, val in zip(o_refs[4 * k:4 * k + 4], (g, d, nm, nv)):
                o[...] = val

    in_specs = [pl.BlockSpec((N_DEV, w.shape[0], tl), lambda i: (0, 0, i)) for w, _, _ in wmv_list]
    in_specs += [pl.BlockSpec((w.shape[0], tl), lambda i: (0, i)) for w, _, _ in wmv_list for _ in range(3)]
    out_specs = [pl.BlockSpec((w.shape[0], tl), lambda i: (0, i)) for w, _, _ in wmv_list for _ in range(4)]
    out_shape = [jax.ShapeDtypeStruct(w.shape, F32) for w, _, _ in wmv_list for _ in range(4)]
    args = list(parts_list) + [a for wmv in wmv_list for a in wmv]
    outs, x_out = _hosted_call(body, name, C // tl, in_specs, out_specs, out_shape, [], args, xchg,
                               _cparams(VMEM_BIG))
    return [outs[4 * k:4 * k + 4] for k in range(n_arr)], x_out


_SMALL_NAMES = ("ada_b", "norm1", "conv_w", "conv_b", "dt_bias", "A_log", "D_skip", "sinks", "attn_out_norm",
                "ssm_out_norm", "norm2", "rel_bias", "final_norm")
N_MOD = 6 * D_MODEL


def _mod_row(a0, a1, a2):
    return jnp.concatenate([a0[2:3], a0[1:2], a1[0:1], a2[2:3], a2[1:2], a2[3:4]], axis=1)


def _small_update(gathered, params):
    n_g = len(gathered)
    flat = [a for name in _SMALL_NAMES for a in params[name]]

    def body(*refs):
        a0_ref, a1_ref, a2_ref, cw_ref, dv_ref, dd_ref, ds_ref, dr_ref, c_ref = refs[:n_g]
        wmv = refs[n_g:n_g + len(flat)]
        outs = refs[n_g + len(flat):]

        def total(ref):
            t = ref[0]
            for i in range(1, N_DEV):
                t = t + ref[i]
            return t

        t0, t1, t2, tcw, tdv, tdd, tds, tdr = [total(r) for r in (a0_ref, a1_ref, a2_ref, cw_ref, dv_ref, dd_ref,
                                                                   ds_ref, dr_ref)]
        r8 = lax.broadcasted_iota(jnp.int32, (N_HEADS, LANE), 0)
        l8 = lax.broadcasted_iota(jnp.int32, (N_HEADS, LANE), 1)

        def diag_row(t):
            return jnp.sum(jnp.where(r8 == l8, t, 0.0), axis=0, keepdims=True)[:, :N_HEADS]

        def lane_sums(t):
            return diag_row(jnp.broadcast_to(jnp.sum(t, axis=1, keepdims=True), (N_HEADS, LANE)))

        me = _lin(_my_pos())
        n_cw = XBC_W // N_DEV
        cw_mine = jnp.zeros((4, n_cw), F32)
        for j in range(N_DEV):
            cw_mine = cw_mine + tcw[0:4, j * n_cw:(j + 1) * n_cw] * jnp.where(me == j, 1.0, 0.0)
        grads = {
            "ada_b": _mod_row(t0, t1, t2), "norm1": t0[0:1], "conv_w": cw_mine, "conv_b": tcw[4:5],
            "dt_bias": lane_sums(tdv[:N_HEADS]), "A_log": lane_sums(tdv[N_HEADS:]), "D_skip": lane_sums(tdd),
            "sinks": diag_row(tds), "attn_out_norm": t1[1:2, :ATTN_W], "ssm_out_norm": t1[1:2, ATTN_W:],
            "norm2": t2[0:1], "rel_bias": tdr[:, :N_HEADS], "final_norm": t2[4:5],
        }
        for k, name in enumerate(_SMALL_NAMES):
            w_ref, m_ref, v_ref = wmv[3 * k:3 * k + 3]
            g = grads[name]
            d, nm, nv = _adamw_math(w_ref[...], g, m_ref[...], v_ref[...])
            for o, val in zip(outs[4 * k:4 * k + 4], (g, d, nm, nv)):
                o[...] = val
        loss_ref, call_ref, dmod_ref = outs[4 * len(_SMALL_NAMES):]
        loss_ref[...] = t2[5:6, 0:1]
        call_ref[...] = jnp.concatenate([c_ref[i] for i in range(N_DEV)], axis=0)
        dmod_ref[...] = jnp.concatenate([_mod_row(a0_ref[i], a1_ref[i], a2_ref[i]) for i in range(N_DEV)], axis=0)

    out_shape = [jax.ShapeDtypeStruct(params[name][0].shape, F32) for name in _SMALL_NAMES for _ in range(4)]
    out_shape += [jax.ShapeDtypeStruct((1, 1), F32), jax.ShapeDtypeStruct((N_DEV, D_MODEL), F32),
                  jax.ShapeDtypeStruct((N_DEV, N_MOD), F32)]
    res = pl.pallas_call(body, name="small_update", out_shape=out_shape)(*gathered, *flat)
    upd = {name: res[4 * k:4 * k + 4] for k, name in enumerate(_SMALL_NAMES)}
    loss, c_all, dmod_all = res[4 * len(_SMALL_NAMES):]
    return upd, loss, c_all, dmod_all


def _ada_w_update(c_all, dmod_all, w, m, v):
    chunk = w.shape[1]

    def body(c_ref, dm_ref, w_ref, m_ref, v_ref, g_ref, d_ref, nm_ref, nv_ref):
        me = _lin(_my_pos())
        dm = jnp.zeros((N_DEV, chunk), F32)
        for j in range(N_DEV):
            dm = dm + dm_ref[:, j * chunk:(j + 1) * chunk] * jnp.where(me == j, 1.0, 0.0)
        g = lax.dot_general(_silu(c_ref[...]), dm, (((0,), (0,)), ((), ())), precision=HI,
                            preferred_element_type=F32)
        d, nm, nv = _adamw_math(w_ref[...], g, m_ref[...], v_ref[...])
        g_ref[...] = g
        d_ref[...] = d
        nm_ref[...] = nm
        nv_ref[...] = nv

    tr = 256
    blk = pl.BlockSpec((tr, chunk), _row)
    return pl.pallas_call(
        body, name="ada_w_update", grid=(w.shape[0] // tr,),
        in_specs=[pl.BlockSpec((N_DEV, tr), lambda i: (0, i)), pl.BlockSpec(dmod_all.shape, _fixed), blk, blk, blk],
        out_specs=[blk] * 4, out_shape=[jax.ShapeDtypeStruct(w.shape, F32)] * 4,
    )(c_all, dmod_all, w, m, v)


def _local_step(x, tgt, c, mod, w_in, conv_w, w_o_mine, w_gu_mine, w_d_mine, p):
    S = x.shape[0]
    tm = min(512, S)
    tmm = min(256, S)
    tw = min(2048, S)
    shift1, scale1, gate1, shift2, scale2, gate2 = [mod[i:i + 1] for i in range(6)]
    buckets = jnp.asarray(_t5_bucket_table())
    per_head = lambda a: jnp.broadcast_to(a.reshape(N_HEADS, 1), (N_HEADS, LANE))
    dtb_row, alog_row, d_exp = per_head(p["dt_bias"]), per_head(p["A_log"]), per_head(p["D_skip"])
    sinks = p["sinks"].reshape(N_HEADS)

    d_cut, gu_cut = WD_CUT, WGU_CUTS
    (qkv, z, xbc, dt_raw), (g_d_a,) = _in_proj_fwd(x, p["norm1"], scale1, shift1, w_in, tm,
                                                   ([w_d_mine[:d_cut]], False))
    bias = _attn_bias(buckets, p["rel_bias"])
    (ya,), (g_gu_a,) = _attn_fwd(qkv, bias, sinks, ([w_gu_mine[:gu_cut[0]]], False))
    (ys, prev_states), (g_gu_b, g_o) = _ssd_fwd(xbc, dt_raw, conv_w, p["conv_b"], dtb_row, alog_row, d_exp,
                                                ([w_gu_mine[gu_cut[0]:gu_cut[1]], w_o_mine], False))
    w_o = g_o.reshape(D_MODEL, D_MODEL)
    x1, (g_gu_c, g_d_b) = _out_proj_fwd(x, ya, ys, z, p["attn_out_norm"], p["ssm_out_norm"], gate1, w_o, tm,
                                        ([w_gu_mine[gu_cut[1]:], w_d_mine[d_cut:]], False))
    dx1, h2, dgu, act, dmlp, acc2 = _mlp_loss(x1, tgt, p["norm2"], scale2, shift2, gate2, p["final_norm"],
                                              (g_gu_a, g_gu_b, g_gu_c), (g_d_a, g_d_b), tmm)
    g_w_gu = _wgrad(dgu, h2, 2 * D_FF // 4, tw, "wgrad_gate_up")
    g_w_d = _wgrad(act, dmlp, D_FF // 2, tw, "wgrad_down")
    dya, dys, dz, u, dmix, acc1 = _out_proj_bwd(dx1, ya, ys, z, p["attn_out_norm"], p["ssm_out_norm"], gate1, w_o, tm)
    g_w_o = _wgrad(u, dmix, D_MODEL, tw, "wgrad_out")
    (dq, dkv, dbias, dsk), (r_d,) = _attn_bwd(qkv, ya, dya, bias, sinks,
                                              ([g_w_d.reshape(N_DEV, D_FF // N_DEV, D_MODEL)], True))
    drel, dsink = _attn_finish(dbias, dsk, buckets)
    (dxbc, ddt, dcw, dvec, dd), (r_gu, r_o) = _ssd_bwd(
        xbc, dt_raw, prev_states, dys, conv_w, p["conv_b"], dtb_row, alog_row, d_exp,
        ([g_w_gu.reshape(N_DEV, 2 * D_FF // N_DEV, D_MODEL), g_w_o.reshape(N_DEV, D_MODEL // N_DEV, D_MODEL)], True))
    gx, h1, dproj, acc0 = _in_proj_bwd(x, dx1, dq, dkv, dz, dxbc, ddt, p["norm1"], scale1, shift1, w_in, tm)
    half = D_MODEL // 2
    slots = lambda g: g[:IN_W].reshape(N_DEV, IN_W // N_DEV, half)
    g_in_a, gathered = _wgrad(dproj, h1, IN_PAD, tw, "wgrad_in_a",
                              ([acc0, acc1, acc2, dcw, dvec, dd, dsink, drel, c], False), g_cols=(half, 0))
    g_in_b, (r_in_a,) = _wgrad(dproj, h1, IN_PAD, tw, "wgrad_in_b", ([slots(g_in_a)], True), g_cols=(half, 1))
    return gx, (r_in_a, slots(g_in_b)), (r_o, r_gu, r_d), gathered


def kernel(x, c, ada_w, ada_b, norm1, w_in, conv_w, conv_b, dt_bias, A_log, D_skip, sinks, attn_out_norm, ssm_out_norm, w_o, norm2, w_gate_up, w_down, rel_bias, final_norm, loss_target, m_ada_w, m_ada_b, m_norm1, m_w_in, m_conv_w, m_conv_b, m_dt_bias, m_A_log, m_D_skip, m_sinks, m_attn_out_norm, m_ssm_out_norm, m_w_o, m_norm2, m_w_gate_up, m_w_down, m_rel_bias, m_final_norm, v_ada_w, v_ada_b, v_norm1, v_w_in, v_conv_w, v_conv_b, v_dt_bias, v_A_log, v_D_skip, v_sinks, v_attn_out_norm, v_ssm_out_norm, v_w_o, v_norm2, v_w_gate_up, v_w_down, v_rel_bias, v_final_norm):
    two_d = lambda a: a if a.ndim == 2 else a.reshape(-1, a.shape[-1])
    small_params = dict(
        ada_b=(ada_b, m_ada_b, v_ada_b), norm1=(norm1, m_norm1, v_norm1), conv_w=(conv_w, m_conv_w, v_conv_w),
        conv_b=(conv_b, m_conv_b, v_conv_b), dt_bias=(dt_bias, m_dt_bias, v_dt_bias), A_log=(A_log, m_A_log, v_A_log),
        D_skip=(D_skip, m_D_skip, v_D_skip), sinks=(sinks, m_sinks, v_sinks),
        attn_out_norm=(attn_out_norm, m_attn_out_norm, v_attn_out_norm),
        ssm_out_norm=(ssm_out_norm, m_ssm_out_norm, v_ssm_out_norm), norm2=(norm2, m_norm2, v_norm2),
        rel_bias=(rel_bias, m_rel_bias, v_rel_bias), final_norm=(final_norm, m_final_norm, v_final_norm))
    small_params = {k: tuple(two_d(a) for a in v) for k, v in small_params.items()}
    S = x.shape[1]
    xs, tgt = x.reshape(S, D_MODEL), loss_target.reshape(S, D_MODEL)
    ada_w2 = ada_w[0]
    chunk = ada_w2.shape[1]
    t_in = [jnp.transpose(a[0]) for a in (w_in, m_w_in, v_w_in)]
    t_gu = [jnp.transpose(a[0]) for a in (w_gate_up, m_w_gate_up, v_w_gate_up)]

    mod, (g_in, g_cw) = _mod_and_gather(c, ada_w2, ada_b.reshape(N_DEV, chunk), [t_in[0].astype(WIRE_DTYPE), conv_w[0]])
    mod = mod.reshape(6, D_MODEL)
    w_in_full = jnp.pad(g_in.reshape(IN_W, D_MODEL), ((0, IN_PAD - IN_W), (0, 0)))
    conv_w_full = jnp.transpose(g_cw, (1, 0, 2)).reshape(4, XBC_W)

    p = {k: v[0] for k, v in small_params.items()}
    gx, (r_in_a, gw_in_b), (r_o, r_gu, r_d), gathered = _local_step(
        xs, tgt, c, mod, w_in_full, conv_w_full, w_o[0].astype(WIRE_DTYPE), t_gu[0].astype(WIRE_DTYPE),
        w_down[0].astype(WIRE_DTYPE), p)

    (u_gu, u_d, u_o), (r_in_b,) = _reduce_adamw_hosting(
        [r_gu, r_d, r_o], [tuple(t_gu), (w_down[0], m_w_down[0], v_w_down[0]), (w_o[0], m_w_o[0], v_w_o[0])],
        "adamw_big", ([gw_in_b], True))
    r_in = jnp.concatenate([r_in_a, r_in_b], axis=2)

    small, loss, c_all, dmod_all = _small_update(gathered, small_params)

    big = {
        "ada_w": _ada_w_update(c_all, dmod_all, ada_w2, m_ada_w[0], v_ada_w[0]),
        "w_in": [jnp.transpose(a) for a in _reduce_adamw(r_in, *t_in, "adamw_w_in")],
        "w_o": u_o,
        "w_gate_up": [jnp.transpose(a) for a in u_gu],
        "w_down": u_d,
    }
    big.update(small)

    order = ['ada_w', 'ada_b', 'norm1', 'w_in', 'conv_w', 'conv_b', 'dt_bias', 'A_log', 'D_skip', 'sinks',
             'attn_out_norm', 'ssm_out_norm', 'w_o', 'norm2', 'w_gate_up', 'w_down', 'rel_bias', 'final_norm']
    shapes = dict(ada_w=ada_w.shape, ada_b=ada_b.shape, norm1=norm1.shape, w_in=w_in.shape, conv_w=conv_w.shape,
                  conv_b=conv_b.shape, dt_bias=dt_bias.shape, A_log=A_log.shape, D_skip=D_skip.shape,
                  sinks=sinks.shape, attn_out_norm=attn_out_norm.shape, ssm_out_norm=ssm_out_norm.shape,
                  w_o=w_o.shape, norm2=norm2.shape, w_gate_up=w_gate_up.shape, w_down=w_down.shape,
                  rel_bias=rel_bias.shape, final_norm=final_norm.shape)
    outs = [[], [], [], []]
    for name in order:
        for kind in range(4):
            outs[kind].append(big[name][kind].reshape(shapes[name]))
    return (loss.reshape(()), gx.reshape(x.shape), *outs[0], *outs[1], *outs[2], *outs[3])
```

```python
import functools

import numpy as np
import jax
import jax.numpy as jnp
from jax import lax
from jax.experimental import pallas as pl
from jax.experimental.pallas import tpu as pltpu

F32 = jnp.float32
MXU_DTYPE = jnp.bfloat16
WIRE_DTYPE = jnp.bfloat16
HI = lax.Precision.HIGHEST
MESH = pl.DeviceIdType.MESH
N_DEV = 8

D_MODEL = 1024
ATTN_W = 512
KV_W = 128
SSM_W = 512
XBC_W = 1024
N_HEADS = 8
D_STATE = 128
D_FF = 2816
IN_W = 2312
IN_PAD = 2432
BLK = 128
N_BUCKETS = 32
EPS = 1e-6
LANE = 128
HALF = 64

ADAM_LR, ADAM_B1, ADAM_B2, ADAM_EPS, ADAM_WD, ADAM_STEP = 0.001, 0.9, 0.999, 1e-08, 0.01, 10

VMEM_BIG = 56 * 1024 * 1024
WD_CUT = 256
WGU_CUTS = (304, 608)


def _cparams(vmem=None):
    if vmem is None:
        return pltpu.CompilerParams()
    return pltpu.CompilerParams(vmem_limit_bytes=vmem)


def _mm(a, b):
    return jnp.dot(a.astype(MXU_DTYPE), b.astype(MXU_DTYPE), preferred_element_type=F32)


def _mm_nt(a, b):
    return lax.dot_general(a.astype(MXU_DTYPE), b.astype(MXU_DTYPE), (((1,), (1,)), ((), ())),
                           preferred_element_type=F32)


def _mm_tn(a, b):
    return lax.dot_general(a.astype(MXU_DTYPE), b.astype(MXU_DTYPE), (((0,), (0,)), ((), ())),
                           preferred_element_type=F32)


def _mm_hi(a, b):
    return jnp.dot(a, b, precision=HI, preferred_element_type=F32)


def _silu(x):
    return x * jax.nn.sigmoid(x)


def _softplus(x):
    return jnp.maximum(x, 0.0) + jnp.log1p(jnp.exp(-jnp.abs(x)))


def _rms(x, g, n):
    return x * lax.rsqrt(jnp.sum(x * x, axis=-1, keepdims=True) * (1.0 / n) + EPS) * g


def _modnorm(x, g, scale, shift):
    return _rms(x, g, x.shape[-1]) * (1.0 + scale) + shift


def _modnorm_parts(x):
    r = lax.rsqrt(jnp.sum(x * x, axis=-1, keepdims=True) * (1.0 / x.shape[-1]) + EPS)
    return r, x * r


def _modnorm_bwd(r, xhat, g, scale, dy):
    dyg = dy * (g * (1.0 + scale))
    c = jnp.sum(dyg * xhat, axis=-1, keepdims=True) * (1.0 / xhat.shape[-1])
    dx = r * (dyg - xhat * c)
    ct = jnp.sum(dy * xhat, axis=0, keepdims=True)
    return dx, ct * (1.0 + scale), ct * g, jnp.sum(dy, axis=0, keepdims=True)


def _lane_iota(shape):
    return lax.broadcasted_iota(jnp.int32, shape, len(shape) - 1)


def _split_pair(t):
    lane = _lane_iota(t.shape)
    lo = jnp.where(lane < HALF, t, 0.0)
    hi = pltpu.roll(jnp.where(lane >= HALF, t, 0.0), HALF, 1)
    return lo, hi


def _join_pair(lo, hi):
    lane = _lane_iota(lo.shape)
    return jnp.where(lane < HALF, lo, pltpu.roll(hi, HALF, 1))


def _split_heads(t, n_pairs):
    out = []
    for p in range(n_pairs):
        out.extend(_split_pair(t[:, p * LANE:(p + 1) * LANE]))
    return out


def _join_heads(hs):
    return jnp.concatenate([_join_pair(hs[2 * p], hs[2 * p + 1]) for p in range(len(hs) // 2)], axis=1)


def _t5_bucket_table():
    dist = np.arange(BLK)[:, None] + BLK - np.arange(2 * BLK)[None, :]
    n = np.maximum(dist, 0)
    max_exact = N_BUCKETS // 2
    large = max_exact + (np.log(np.maximum(n, 1) / max_exact) / np.log(128 / max_exact)
                         * (N_BUCKETS - max_exact)).astype(np.int32)
    large = np.minimum(large, N_BUCKETS - 1)
    return np.where(n < max_exact, n, large).astype(np.int32)


def _my_pos():
    return lax.axis_index("x"), lax.axis_index("y"), lax.axis_index("c")


def _peer(k):
    x, y, c = _my_pos()
    return (1 - x if k & 4 else x, 1 - y if k & 2 else y, 1 - c if k & 1 else c)


def _lin(pos):
    return 4 * pos[0] + 2 * pos[1] + pos[2]


def _xchg_copies(ins, outs, sems, scatter):
    local_sem, send_sem, recv_sem = sems
    me = _lin(_my_pos())
    local, remote = [], []
    for a in range(len(ins)):
        src = ins[a].at[me] if scatter else ins[a]
        local.append(pltpu.make_async_copy(src, outs[a].at[me], local_sem.at[a]))
    for k in range(1, N_DEV):
        peer = _peer(k)
        for a in range(len(ins)):
            src = ins[a].at[_lin(peer)] if scatter else ins[a]
            remote.append(pltpu.make_async_remote_copy(src, outs[a].at[me], send_sem.at[a, k - 1],
                                                       recv_sem.at[a, k - 1], device_id=peer, device_id_type=MESH))
    return local, remote


def _xchg_start(ins, outs, sems, scatter):
    local, remote = _xchg_copies(ins, outs, sems, scatter)
    for cp in local + remote:
        cp.start()


def _xchg_wait(ins, outs, sems, scatter):
    local, remote = _xchg_copies(ins, outs, sems, scatter)
    for cp in local:
        cp.wait()
    for cp in remote:
        cp.wait_send()
        cp.wait_recv()


def _xchg_shapes(arrs, scatter):
    n = len(arrs)
    if scatter:
        out_shape = [jax.ShapeDtypeStruct(a.shape, a.dtype) for a in arrs]
    else:
        out_shape = [jax.ShapeDtypeStruct((N_DEV,) + a.shape, a.dtype) for a in arrs]
    sems = [pltpu.SemaphoreType.DMA((n,)), pltpu.SemaphoreType.DMA((n, N_DEV - 1)),
            pltpu.SemaphoreType.DMA((n, N_DEV - 1))]
    return out_shape, sems


def _mod_and_gather(c, ada_w, ada_b8, arrs):
    n = len(arrs)
    chunk = ada_w.shape[1]
    out_shape = [jax.ShapeDtypeStruct((N_DEV, 1, chunk), F32)]
    out_shape += [jax.ShapeDtypeStruct((N_DEV,) + a.shape, a.dtype) for a in arrs]
    chips = (2, 4, 6)

    def modulation(c_ref, w_ref, b_ref, out_ref, cbuf, part, s1, r1, s2, r2):
        me = _lin(_my_pos())
        first = []
        for k in range(1, N_DEV):
            cp = pltpu.make_async_remote_copy(c_ref, cbuf.at[me], s1.at[k - 1], r1.at[k - 1],
                                              device_id=_peer(k), device_id_type=MESH)
            cp.start()
            first.append(cp)
        cbuf[me] = c_ref[...]
        for cp in first:
            cp.wait_send()
            cp.wait_recv()
        cond = _silu(jnp.concatenate([cbuf[i] for i in range(N_DEV)], axis=0))
        mod = _mm_hi(cond, w_ref[...]) + b_ref[pl.ds(me, 1), :]
        for j in range(N_DEV):
            part[j] = mod[j:j + 1, :]
        second = []
        for k in range(1, N_DEV):
            peer = _peer(k)
            cp = pltpu.make_async_remote_copy(part.at[_lin(peer)], out_ref.at[me], s2.at[k - 1], r2.at[k - 1],
                                              device_id=peer, device_id_type=MESH)
            cp.start()
            second.append(cp)
        out_ref[me] = part[me]
        for cp in second:
            cp.wait_send()
            cp.wait_recv()

    def body(*refs):
        c_ref, w_ref, b_ref = refs[:3]
        ins = refs[3:3 + n]
        mod_ref = refs[3 + n]
        outs = refs[4 + n:4 + 2 * n]
        cbuf, part, s1, r1, s2, r2, local_sem, send_sem, recv_sem, fsend_sem, frecv_sem = refs[4 + 2 * n:]
        me = _lin(_my_pos())
        sibling = _peer(1)

        def direct(a, k):
            return pltpu.make_async_remote_copy(ins[a], outs[a].at[me], send_sem.at[a, k], recv_sem.at[a, k],
                                                device_id=_peer(k), device_id_type=MESH)

        def handed_on(a, j, origin):
            slot = outs[a].at[origin]
            return pltpu.make_async_remote_copy(slot, slot, fsend_sem.at[a, j], frecv_sem.at[a, j],
                                                device_id=sibling, device_id_type=MESH)

        local = [pltpu.make_async_copy(ins[a], outs[a].at[me], local_sem.at[a]) for a in range(n)]
        first = [direct(a, k) for k in (1,) + chips for a in range(n)]
        for cp in local + first:
            cp.start()
        modulation(c_ref, w_ref, b_ref, mod_ref, cbuf, part, s1, r1, s2, r2)
        passed = []
        for j, k in enumerate(chips):
            for a in range(n):
                direct(a, k).wait_recv()
                cp = handed_on(a, j, _lin(_peer(k)))
                cp.start()
                passed.append(cp)
        for a in range(n):
            direct(a, 1).wait_recv()
            for j, k in enumerate(chips):
                handed_on(a, j, _lin(_peer(k ^ 1))).wait_recv()
        for cp in local:
            cp.wait()
        for cp in first + passed:
            cp.wait_send()

    hbm = pl.BlockSpec(memory_space=pltpu.HBM)
    vm = pl.BlockSpec(memory_space=pltpu.VMEM)
    dma = pltpu.SemaphoreType.DMA
    res = pl.pallas_call(
        body, name="mod_and_gather", out_shape=out_shape, in_specs=[vm, vm, vm] + [hbm] * n,
        out_specs=[vm] + [hbm] * n,
        scratch_shapes=[pltpu.VMEM((N_DEV, 1, D_MODEL), F32), pltpu.VMEM((N_DEV, 1, chunk), F32)]
        + [dma((N_DEV - 1,))] * 4 + [dma((n,)), dma((n, N_DEV)), dma((n, N_DEV)), dma((n, 3)), dma((n, 3))],
    )(c, ada_w, ada_b8, *arrs)
    return res[0], res[1:]


def _hosted_call(body, name, grid, in_specs, out_specs, out_shape, scratch_shapes, args, xchg, cparams):
    arrs, scatter = xchg
    grid = (grid,) if isinstance(grid, int) else tuple(grid)
    n, n_in, n_out, n_scr = len(arrs), len(in_specs), len(out_specs), len(scratch_shapes)
    x_shape, x_sems = _xchg_shapes(arrs, scatter)

    def hosted(*refs):
        ins, refs = refs[:n_in], refs[n_in:]
        x_in, refs = refs[:n], refs[n:]
        outs, refs = refs[:n_out], refs[n_out:]
        x_out, refs = refs[:n], refs[n:]
        scr, sems = refs[:n_scr], refs[n_scr:]
        step = pl.program_id(0)
        for d in range(1, len(grid)):
            step = step * grid[d] + pl.program_id(d)

        @pl.when(step == 0)
        def _():
            _xchg_start(x_in, x_out, sems, scatter)

        body(*ins, *outs, *scr)

        @pl.when(step == int(np.prod(grid)) - 1)
        def _():
            _xchg_wait(x_in, x_out, sems, scatter)

    hbm = pl.BlockSpec(memory_space=pltpu.HBM)
    res = pl.pallas_call(
        hosted, name=name, grid=grid, in_specs=list(in_specs) + [hbm] * n,
        out_specs=list(out_specs) + [hbm] * n, out_shape=list(out_shape) + x_shape,
        scratch_shapes=list(scratch_shapes) + x_sems, compiler_params=cparams,
    )(*args, *arrs)
    return res[:n_out], res[n_out:]


def _row(i):
    return (i, 0)


def _fixed(i):
    return (0, 0)


def _in_proj_fwd(x, norm1, scale1, shift1, w_in, tm, xchg):
    S = x.shape[0]

    def body(x_ref, n_ref, sc_ref, sh_ref, w_ref, qkv_ref, z_ref, xbc_ref, dt_ref):
        h = _modnorm(x_ref[...], n_ref[...], sc_ref[...], sh_ref[...])
        p = _mm_nt(h, w_ref[...])
        qkv_ref[...] = p[:, :768].astype(qkv_ref.dtype)
        z_ref[...] = p[:, 768:1280]
        xbc_ref[...] = p[:, 1280:2304]
        dt_ref[...] = p[:, 2304:IN_PAD]

    vec = pl.BlockSpec((1, D_MODEL), _fixed)
    return _hosted_call(
        body, "in_proj_fwd", S // tm,
        in_specs=[pl.BlockSpec((tm, D_MODEL), _row), vec, vec, vec, pl.BlockSpec((IN_PAD, D_MODEL), _fixed)],
        out_specs=[pl.BlockSpec((tm, 768), _row), pl.BlockSpec((tm, SSM_W), _row),
                   pl.BlockSpec((tm, XBC_W), _row), pl.BlockSpec((tm, LANE), _row)],
        out_shape=[jax.ShapeDtypeStruct((S, 768), MXU_DTYPE), jax.ShapeDtypeStruct((S, SSM_W), F32),
                   jax.ShapeDtypeStruct((S, XBC_W), F32), jax.ShapeDtypeStruct((S, LANE), F32)],
        scratch_shapes=[], args=(x, norm1, scale1, shift1, w_in), xchg=xchg, cparams=_cparams(VMEM_BIG),
    )


def _in_proj_bwd(x, dx1, dq, dkv, dz, dxbc, ddt, norm1, scale1, shift1, w_in, tm):
    S = x.shape[0]

    def body(x_ref, dx1_ref, dq_ref, dkv_ref, dz_ref, dxbc_ref, ddt_ref, n_ref, sc_ref, sh_ref, w_ref,
             gx_ref, h_ref, acc_ref):
        @pl.when(pl.program_id(0) == 0)
        def _():
            acc_ref[...] = jnp.zeros_like(acc_ref)

        dp = jnp.concatenate([dq_ref[...], dkv_ref[...], dz_ref[...], dxbc_ref[...], ddt_ref[...]], axis=1)
        dh = _mm(dp, w_ref[...])
        r, xhat = _modnorm_parts(x_ref[...])
        dx, dn, dsc, dsh = _modnorm_bwd(r, xhat, n_ref[...], sc_ref[...], dh)
        gx_ref[...] = dx1_ref[...] + dx
        h_ref[...] = (xhat * n_ref[...] * (1.0 + sc_ref[...]) + sh_ref[...]).astype(h_ref.dtype)
        acc_ref[0:1, :] += dn
        acc_ref[1:2, :] += dsc
        acc_ref[2:3, :] += dsh

    vec = pl.BlockSpec((1, D_MODEL), _fixed)
    return pl.pallas_call(
        body, name="in_proj_bwd", grid=(S // tm,),
        in_specs=[pl.BlockSpec((tm, D_MODEL), _row), pl.BlockSpec((tm, D_MODEL), _row),
                  pl.BlockSpec((tm, ATTN_W), _row), pl.BlockSpec((tm, 2 * KV_W), _row),
                  pl.BlockSpec((tm, SSM_W), _row), pl.BlockSpec((tm, XBC_W), _row), pl.BlockSpec((tm, LANE), _row),
                  vec, vec, vec, pl.BlockSpec((IN_PAD, D_MODEL), _fixed)],
        out_specs=[pl.BlockSpec((tm, D_MODEL), _row), pl.BlockSpec((tm, D_MODEL), _row),
                   pl.BlockSpec((8, D_MODEL), _fixed)],
        out_shape=[jax.ShapeDtypeStruct((S, D_MODEL), F32), jax.ShapeDtypeStruct((S, D_MODEL), MXU_DTYPE),
                   jax.ShapeDtypeStruct((8, D_MODEL), F32)],
        compiler_params=_cparams(VMEM_BIG),
    )(x, dx1, dq, dkv, dz, dxbc, ddt, norm1, scale1, shift1, w_in)


def _out_stage(ya, ys0, ys1, z0, z1, an, sn0, sn1):
    half = SSM_W // 2
    a = _rms(ya, an, ATTN_W)
    g0 = _rms(ys0 * _silu(z0), sn0, half)
    g1 = _rms(ys1 * _silu(z1), sn1, half)
    return jnp.concatenate([a, g0, g1], axis=1)


def _out_stage_args(ya_ref, ys_ref, z_ref, an_ref, sn_ref):
    half = SSM_W // 2
    return (ya_ref[...], ys_ref[:, :half], ys_ref[:, half:], z_ref[:, :half], z_ref[:, half:],
            an_ref[...], sn_ref[:, :half], sn_ref[:, half:])


def _out_proj_fwd(x, ya, ys, z, an, sn, gate1, w_o, tm, xchg):
    S = x.shape[0]

    def body(x_ref, ya_ref, ys_ref, z_ref, an_ref, sn_ref, g_ref, w_ref, x1_ref):
        u = _out_stage(*_out_stage_args(ya_ref, ys_ref, z_ref, an_ref, sn_ref))
        x1_ref[...] = x_ref[...] + g_ref[...] * _mm(u, w_ref[...])

    half = pl.BlockSpec((tm, ATTN_W), _row)
    hvec = pl.BlockSpec((1, ATTN_W), _fixed)
    (x1,), x_out = _hosted_call(
        body, "out_proj_fwd", S // tm,
        in_specs=[pl.BlockSpec((tm, D_MODEL), _row), half, half, half, hvec, hvec,
                  pl.BlockSpec((1, D_MODEL), _fixed), pl.BlockSpec((D_MODEL, D_MODEL), _fixed)],
        out_specs=[pl.BlockSpec((tm, D_MODEL), _row)],
        out_shape=[jax.ShapeDtypeStruct((S, D_MODEL), F32)],
        scratch_shapes=[], args=(x, ya, ys, z, an, sn, gate1, w_o), xchg=xchg, cparams=_cparams(VMEM_BIG),
    )
    return x1, x_out


def _out_proj_bwd(dx1, ya, ys, z, an, sn, gate1, w_o, tm):
    S = dx1.shape[0]

    def body(dx1_ref, ya_ref, ys_ref, z_ref, an_ref, sn_ref, g_ref, w_ref,
             dya_ref, dys_ref, dz_ref, u_ref, dmix_ref, acc_ref):
        @pl.when(pl.program_id(0) == 0)
        def _():
            acc_ref[...] = jnp.zeros_like(acc_ref)

        u, vjp = jax.vjp(_out_stage, *_out_stage_args(ya_ref, ys_ref, z_ref, an_ref, sn_ref))
        dx1 = dx1_ref[...]
        mix = _mm(u, w_ref[...])
        dmix = dx1 * g_ref[...]
        du = _mm_nt(dmix, w_ref[...])
        dya, dys0, dys1, dz0, dz1, dan, dsn0, dsn1 = vjp(du)
        dya_ref[...] = dya
        dys_ref[...] = jnp.concatenate([dys0, dys1], axis=1)
        dz_ref[...] = jnp.concatenate([dz0, dz1], axis=1).astype(dz_ref.dtype)
        u_ref[...] = u.astype(u_ref.dtype)
        dmix_ref[...] = dmix.astype(dmix_ref.dtype)
        acc_ref[0:1, :] += jnp.sum(dx1 * mix, axis=0, keepdims=True)
        acc_ref[1:2, :] += jnp.concatenate([dan, dsn0, dsn1], axis=1)

    half = pl.BlockSpec((tm, ATTN_W), _row)
    hvec = pl.BlockSpec((1, ATTN_W), _fixed)
    full = pl.BlockSpec((tm, D_MODEL), _row)
    return pl.pallas_call(
        body, name="out_proj_bwd", grid=(S // tm,),
        in_specs=[full, half, half, half, hvec, hvec,
                  pl.BlockSpec((1, D_MODEL), _fixed), pl.BlockSpec((D_MODEL, D_MODEL), _fixed)],
        out_specs=[half, half, half, full, full, pl.BlockSpec((8, D_MODEL), _fixed)],
        out_shape=[jax.ShapeDtypeStruct((S, ATTN_W), F32)] * 2 + [jax.ShapeDtypeStruct((S, ATTN_W), MXU_DTYPE)]
        + [jax.ShapeDtypeStruct((S, D_MODEL), MXU_DTYPE)] * 2 + [jax.ShapeDtypeStruct((8, D_MODEL), F32)],
        compiler_params=_cparams(VMEM_BIG),
    )(dx1, ya, ys, z, an, sn, gate1, w_o)


def _loss_rows(x2, fn, tgt):
    y = _rms(x2, fn, D_MODEL)
    per_row = jnp.sum(jnp.square(y - tgt), axis=1, keepdims=True)
    return jnp.sum(per_row, axis=0, keepdims=True) * (0.5 / D_MODEL)


def _mlp_loss(x1, tgt, norm2, scale2, shift2, gate2, fnorm, w_gu, w_d, tm):
    S = x1.shape[0]
    n_pieces = len(w_gu) + len(w_d)

    def body(*refs):
        x1_ref, t_ref, n_ref, sc_ref, sh_ref, g_ref, fn_ref = refs[:7]
        piece_refs = refs[7:7 + n_pieces]
        dx1_ref, h_ref, dgu_ref, act_ref, dmlp_ref, acc_ref, wgu, wd, wsem = refs[7 + n_pieces:]

        @pl.when(pl.program_id(0) == 0)
        def _():
            acc_ref[...] = jnp.zeros_like(acc_ref)
            copies = []
            for dst, pieces in ((wgu, piece_refs[:len(w_gu)]), (wd, piece_refs[len(w_gu):])):
                shard = sum(p.shape[1] for p in pieces)
                off = 0
                for p in pieces:
                    for j in range(N_DEV):
                        copies.append(pltpu.make_async_copy(p.at[j], dst.at[pl.ds(j * shard + off, p.shape[1])],
                                                            wsem.at[len(copies)]))
                    off += p.shape[1]
            for cp in copies:
                cp.start()
            for cp in copies:
                cp.wait()

        x1 = x1_ref[...]
        gate2 = g_ref[...]
        h, vjp_h = jax.vjp(_modnorm, x1, n_ref[...], sc_ref[...], sh_ref[...])
        hb = h.astype(MXU_DTYPE)
        gu = _mm_nt(hb, wgu[...])
        g, u = gu[:, :D_FF], gu[:, D_FF:]
        sg = jax.nn.sigmoid(g)
        silu_g = g * sg
        act = (silu_g * u).astype(MXU_DTYPE)
        mlp = _mm(act, wd[...])
        x2 = x1 + gate2 * mlp
        loss, vjp_loss = jax.vjp(_loss_rows, x2, fn_ref[...], t_ref[...])
        dx2, dfn, _ = vjp_loss(jnp.ones((1, 1), F32))
        dmlp = (dx2 * gate2).astype(MXU_DTYPE)
        dact = _mm_nt(dmlp, wd[...])
        dg = dact * u * (sg * (1.0 + g * (1.0 - sg)))
        du = dact * silu_g
        dgu = jnp.concatenate([dg, du], axis=1).astype(MXU_DTYPE)
        dh = _mm(dgu, wgu[...])
        dx, dn, dsc, dsh = vjp_h(dh)
        dx1_ref[...] = dx2 + dx
        h_ref[...] = hb
        dgu_ref[...] = dgu
        act_ref[...] = act
        dmlp_ref[...] = dmlp
        acc_ref[0:1, :] += dn
        acc_ref[1:2, :] += dsc
        acc_ref[2:3, :] += dsh
        acc_ref[3:4, :] += jnp.sum(dx2 * mlp, axis=0, keepdims=True)
        acc_ref[4:5, :] += dfn
        acc_ref[5:6, :] += jnp.broadcast_to(loss, (1, D_MODEL))

    full = pl.BlockSpec((tm, D_MODEL), _row)
    vec = pl.BlockSpec((1, D_MODEL), _fixed)
    anyspec = pl.BlockSpec(memory_space=pl.ANY)
    return pl.pallas_call(
        body, name="mlp_loss", grid=(S // tm,),
        in_specs=[full, full, vec, vec, vec, vec, vec] + [anyspec] * n_pieces,
        out_specs=[full, full, pl.BlockSpec((tm, 2 * D_FF), _row), pl.BlockSpec((tm, D_FF), _row), full,
                   pl.BlockSpec((8, D_MODEL), _fixed)],
        out_shape=[jax.ShapeDtypeStruct((S, D_MODEL), F32), jax.ShapeDtypeStruct((S, D_MODEL), MXU_DTYPE),
                   jax.ShapeDtypeStruct((S, 2 * D_FF), MXU_DTYPE), jax.ShapeDtypeStruct((S, D_FF), MXU_DTYPE),
                   jax.ShapeDtypeStruct((S, D_MODEL), MXU_DTYPE), jax.ShapeDtypeStruct((8, D_MODEL), F32)],
        scratch_shapes=[pltpu.VMEM((2 * D_FF, D_MODEL), MXU_DTYPE), pltpu.VMEM((D_FF, D_MODEL), MXU_DTYPE),
                        pltpu.SemaphoreType.DMA((N_DEV * n_pieces,))],
        compiler_params=_cparams(VMEM_BIG),
    )(x1, tgt, norm2, scale2, shift2, gate2, fnorm, *w_gu, *w_d)


def _wgrad(a, g, tk, ts, name, xchg=None, g_cols=None):
    pieces = list(a) if isinstance(a, (list, tuple)) else [a]
    S = pieces[0].shape[0]
    K = sum(p.shape[1] for p in pieces)
    assert len(pieces) == 1 or tk == K
    N, col = (g.shape[1], 0) if g_cols is None else g_cols
    ns = S // ts
    n_a = len(pieces)

    def body(*refs):
        a_refs, (g_ref, o_ref, acc_ref) = refs[:n_a], refs[n_a:]
        s = pl.program_id(1)

        @pl.when(s == 0)
        def _():
            acc_ref[...] = jnp.zeros_like(acc_ref)

        a_blk = a_refs[0][...] if n_a == 1 else jnp.concatenate([r[...] for r in a_refs], axis=1)
        acc_ref[...] += _mm_tn(a_blk, g_ref[...])

        @pl.when(s == ns - 1)
        def _():
            o_ref[...] = acc_ref[...].astype(o_ref.dtype)

    if n_a == 1:
        in_specs = [pl.BlockSpec((ts, tk), lambda j, s: (s, j))]
    else:
        in_specs = [pl.BlockSpec((ts, p.shape[1]), lambda j, s: (s, 0)) for p in pieces]
    in_specs.append(pl.BlockSpec((ts, N), lambda j, s: (s, col)))
    out_spec = pl.BlockSpec((tk, N), lambda j, s: (j, 0))
    out_shape = jax.ShapeDtypeStruct((K, N), WIRE_DTYPE)
    scratch = [pltpu.VMEM((tk, N), F32)]
    args = (*pieces, g)
    if xchg is None:
        return pl.pallas_call(body, name=name, grid=(K // tk, ns), in_specs=in_specs, out_specs=out_spec,
                              out_shape=out_shape, scratch_shapes=scratch, compiler_params=_cparams(VMEM_BIG))(*args)
    (out,), x_out = _hosted_call(body, name, (K // tk, ns), in_specs, [out_spec], [out_shape], scratch, args, xchg,
                                 _cparams(VMEM_BIG))
    return out, x_out


MASKED = -1e30
QK_SCALE = HALF ** -0.5


def _attn_bias(buckets, rel_bias):
    def body(bk_ref, relb_ref, out_ref):
        bk = bk_ref[...]
        i = lax.broadcasted_iota(jnp.int32, (BLK, 2 * BLK), 0)
        j = lax.broadcasted_iota(jnp.int32, (BLK, 2 * BLK), 1)
        window = (j > i) & (j <= i + BLK)
        for h in range(N_HEADS):
            acc = jnp.zeros((BLK, 2 * BLK), F32)
            for b in range(N_BUCKETS):
                acc = jnp.where(bk == b, relb_ref[b, h], acc)
            out_ref[0, h] = jnp.where(window, acc, MASKED)
            out_ref[1, h] = jnp.where(window & (j >= BLK), acc, MASKED)

    return pl.pallas_call(
        body, name="attn_bias", out_shape=jax.ShapeDtypeStruct((2, N_HEADS, BLK, 2 * BLK), F32),
        in_specs=[pl.BlockSpec(memory_space=pltpu.VMEM), pl.BlockSpec(memory_space=pltpu.SMEM)],
    )(buckets, rel_bias)


def _attn_kv(kvp_ref, kvc_ref):
    kvp = kvp_ref[...].astype(F32)
    kvc = kvc_ref[...].astype(F32)
    kp, kc = _split_pair(kvp[:, :LANE]), _split_pair(kvc[:, :LANE])
    vp, vc = _split_pair(kvp[:, LANE:]), _split_pair(kvc[:, LANE:])
    k_pads = [jnp.concatenate([kp[g], kc[g]], axis=0).astype(MXU_DTYPE) for g in range(2)]
    v_pads = [jnp.concatenate([vp[g], vc[g]], axis=0).astype(MXU_DTYPE) for g in range(2)]
    return k_pads, v_pads


def _attn_fwd(qkv, bias, sinks, xchg):
    S = qkv.shape[0]
    nb = S // BLK

    def body(q_ref, kvp_ref, kvc_ref, bias_ref, sinks_ref, y_ref):
        first = jnp.where(pl.program_id(0) == 0, 1, 0)
        q_heads = _split_heads(q_ref[...].astype(F32) * QK_SCALE, 4)
        k_pads, v_pads = _attn_kv(kvp_ref, kvc_ref)
        heads = range(N_HEADS)
        s = [_mm_nt(q_heads[h].astype(MXU_DTYPE), k_pads[h // 4]) + bias_ref[first, h] for h in heads]
        m = [jnp.maximum(jnp.max(s[h], axis=-1, keepdims=True), sinks_ref[h]) for h in heads]
        p = [jnp.exp(s[h] - m[h]) for h in heads]
        rinv = [1.0 / (jnp.sum(p[h], axis=-1, keepdims=True) + jnp.exp(sinks_ref[h] - m[h])) for h in heads]
        y_ref[...] = _join_heads([_mm(p[h], v_pads[h // 4]) * rinv[h] for h in heads])

    smem = pl.BlockSpec(memory_space=pltpu.SMEM)
    return _hosted_call(
        body, "attn_fwd", nb,
        in_specs=[pl.BlockSpec((BLK, ATTN_W), _row),
                  pl.BlockSpec((BLK, 2 * KV_W), lambda i: (jnp.maximum(i - 1, 0), 2)),
                  pl.BlockSpec((BLK, 2 * KV_W), lambda i: (i, 2)),
                  pl.BlockSpec((2, N_HEADS, BLK, 2 * BLK), lambda i: (0, 0, 0, 0)), smem],
        out_specs=[pl.BlockSpec((BLK, ATTN_W), _row)],
        out_shape=[jax.ShapeDtypeStruct((S, ATTN_W), F32)],
        scratch_shapes=[],
        args=(qkv, qkv, qkv, bias, sinks), xchg=xchg, cparams=_cparams(),
    )


def _attn_bwd(qkv, y, dy, bias, sinks, xchg):
    S = qkv.shape[0]
    nb = S // BLK

    def body(q_ref, kvp_ref, kvc_ref, y_ref, dy_ref, bias_ref, sinks_ref, dq_ref, dkv_ref, dbias_ref, dsk_ref, carry_ref):
        i = pl.program_id(0)

        @pl.when(i == 0)
        def _():
            dbias_ref[...] = jnp.zeros_like(dbias_ref)
            dsk_ref[...] = jnp.zeros_like(dsk_ref)
            carry_ref[...] = jnp.zeros_like(carry_ref)

        first = jnp.where(i == nb - 1, 1, 0)
        q_heads = _split_heads(q_ref[...].astype(F32) * QK_SCALE, 4)
        k_pads, v_pads = _attn_kv(kvp_ref, kvc_ref)
        y_heads = _split_heads(y_ref[...], 4)
        dy_heads = _split_heads(dy_ref[...], 4)
        heads = range(N_HEADS)
        qs = [q_heads[h].astype(MXU_DTYPE) for h in heads]
        s = [_mm_nt(qs[h], k_pads[h // 4]) + bias_ref[first, h] for h in heads]
        m = [jnp.maximum(jnp.max(s[h], axis=-1, keepdims=True), sinks_ref[h]) for h in heads]
        p = [jnp.exp(s[h] - m[h]) for h in heads]
        esink = [jnp.exp(sinks_ref[h] - m[h]) for h in heads]
        rinv = [1.0 / (jnp.sum(p[h], axis=-1, keepdims=True) + esink[h]) for h in heads]
        t = [dy_heads[h] * rinv[h] for h in heads]
        delta = [jnp.sum(t[h] * y_heads[h], axis=-1, keepdims=True) for h in heads]
        tb = [t[h].astype(MXU_DTYPE) for h in heads]
        dp = [_mm_nt(tb[h], v_pads[h // 4]) for h in heads]
        ds = [p[h] * (dp[h] - delta[h]) for h in heads]
        for h in heads:
            dbias_ref[h] += ds[h]
            dsk_ref[h] -= esink[h] * delta[h]
        dsb = [ds[h].astype(MXU_DTYPE) for h in heads]
        pb = [p[h].astype(MXU_DTYPE) for h in heads]
        dq_heads = [_mm(dsb[h], k_pads[h // 4]) * QK_SCALE for h in heads]
        dk_pads = [_mm_tn(jnp.concatenate(dsb[4 * g:4 * g + 4], axis=0), jnp.concatenate(qs[4 * g:4 * g + 4], axis=0))
                   for g in range(2)]
        dv_pads = [_mm_tn(jnp.concatenate(pb[4 * g:4 * g + 4], axis=0), jnp.concatenate(tb[4 * g:4 * g + 4], axis=0))
                   for g in range(2)]
        dq_ref[...] = _join_heads(dq_heads).astype(dq_ref.dtype)
        dk_prev = _join_pair(dk_pads[0][:BLK], dk_pads[1][:BLK])
        dk_cur = _join_pair(dk_pads[0][BLK:], dk_pads[1][BLK:])
        dv_prev = _join_pair(dv_pads[0][:BLK], dv_pads[1][:BLK])
        dv_cur = _join_pair(dv_pads[0][BLK:], dv_pads[1][BLK:])
        dkv_ref[...] = (jnp.concatenate([dk_cur, dv_cur], axis=1) + carry_ref[...]).astype(dkv_ref.dtype)
        carry_ref[...] = jnp.concatenate([dk_prev, dv_prev], axis=1)

    smem = pl.BlockSpec(memory_space=pltpu.SMEM)
    rev = lambda i: (nb - 1 - i, 0)
    return _hosted_call(
        body, "attn_bwd", nb,
        in_specs=[pl.BlockSpec((BLK, ATTN_W), rev),
                  pl.BlockSpec((BLK, 2 * KV_W), lambda i: (jnp.maximum(nb - 2 - i, 0), 2)),
                  pl.BlockSpec((BLK, 2 * KV_W), lambda i: (nb - 1 - i, 2)),
                  pl.BlockSpec((BLK, ATTN_W), rev), pl.BlockSpec((BLK, ATTN_W), rev),
                  pl.BlockSpec((2, N_HEADS, BLK, 2 * BLK), lambda i: (0, 0, 0, 0)), smem],
        out_specs=[pl.BlockSpec((BLK, ATTN_W), rev), pl.BlockSpec((BLK, 2 * KV_W), rev),
                   pl.BlockSpec((N_HEADS, BLK, 2 * BLK), lambda i: (0, 0, 0)),
                   pl.BlockSpec((N_HEADS, BLK, 1), lambda i: (0, 0, 0))],
        out_shape=[jax.ShapeDtypeStruct((S, ATTN_W), MXU_DTYPE), jax.ShapeDtypeStruct((S, 2 * KV_W), MXU_DTYPE),
                   jax.ShapeDtypeStruct((N_HEADS, BLK, 2 * BLK), F32), jax.ShapeDtypeStruct((N_HEADS, BLK, 1), F32)],
        scratch_shapes=[pltpu.VMEM((BLK, 2 * KV_W), F32)],
        args=(qkv, qkv, qkv, y, dy, bias, sinks), xchg=xchg, cparams=_cparams(),
    )


def _attn_finish(dbias, dsk, buckets):
    def body(db_ref, dsk_ref, bk_ref, drel_ref, dsink_ref):
        bk = bk_ref[...]
        r = lax.broadcasted_iota(jnp.int32, (N_BUCKETS, LANE), 0)
        l = lax.broadcasted_iota(jnp.int32, (N_BUCKETS, LANE), 1)
        row = lax.broadcasted_iota(jnp.int32, (N_HEADS, LANE), 0)
        res = jnp.zeros((N_BUCKETS, LANE), F32)
        dsink = jnp.zeros((N_HEADS, LANE), F32)
        for h in range(N_HEADS):
            db = db_ref[h]
            for b in range(N_BUCKETS):
                v = jnp.sum(jnp.sum(jnp.where(bk == b, db, 0.0), axis=1, keepdims=True), axis=0, keepdims=True)
                res = res + jnp.where((r == b) & (l == h), v, 0.0)
            dsink = dsink + jnp.where(row == h, jnp.sum(dsk_ref[h], axis=0, keepdims=True), 0.0)
        drel_ref[...] = res
        dsink_ref[...] = dsink

    return pl.pallas_call(body, name="attn_finish",
                          out_shape=[jax.ShapeDtypeStruct((N_BUCKETS, LANE), F32),
                                     jax.ShapeDtypeStruct((N_HEADS, LANE), F32)])(dbias, dsk, buckets)


def _ssd_consts():
    r = lax.broadcasted_iota(jnp.int32, (BLK, BLK), 0)
    c = lax.broadcasted_iota(jnp.int32, (BLK, BLK), 1)
    causal = c <= r
    upper = (r <= c).astype(F32)
    last = r == BLK - 1
    head = lax.broadcasted_iota(jnp.int32, (N_HEADS, BLK), 0)
    return causal, upper, last, head


def _ssd_chunk(xs, bg, cg, dt_raw_t, prev, dtb, alog, d_rows, consts):
    causal, upper, last, head = consts
    dt_t = _softplus(dt_raw_t + dtb)
    acs_t = _mm_hi(dt_t * (-jnp.exp(alog)), upper)
    cb = [_mm_nt(cg[g], bg[g]) for g in range(2)]
    heads = range(N_HEADS)
    dt_row = [jnp.sum(jnp.where(head == h, dt_t, 0.0), axis=0, keepdims=True) for h in heads]
    a_row = [jnp.sum(jnp.where(head == h, acs_t, 0.0), axis=0, keepdims=True) for h in heads]
    a_rb = [jnp.broadcast_to(a_row[h], (BLK, BLK)) for h in heads]
    a_b = [a_rb[h].T for h in heads]
    a_last = [jnp.sum(jnp.where(last, a_b[h], 0.0), axis=0, keepdims=True) for h in heads]
    w = [cb[h // 4] * jnp.exp(jnp.where(causal, a_b[h] - a_rb[h], -1e30)) * dt_row[h] for h in heads]
    f_b = [jnp.broadcast_to(dt_row[h] * jnp.exp(a_last[h] - a_row[h]), (BLK, BLK)).T for h in heads]
    y_in = [_mm(w[h], xs[h]) for h in heads]
    y_off = [_mm(cg[h // 4], prev[h]) * jnp.exp(a_b[h]) for h in heads]
    st = [_mm_tn(bg[h // 4], xs[h] * f_b[h]) for h in heads]
    ys = [y_in[h] + y_off[h] + d_rows[h] * xs[h] for h in heads]
    hs = [prev[h] * jnp.exp(a_last[h]) + st[h] for h in heads]
    return tuple(ys), tuple(hs)


def _ssd_chunk_bwd(xs, bg, cg, dt_raw_t, prev, dtb, alog, d_rows, dys, dhs, consts):
    causal, upper, last, head = consts
    heads, groups = range(N_HEADS), range(2)
    lane = _lane_iota((BLK, BLK))
    lane_row = _lane_iota((1, BLK))
    pre_dt = dt_raw_t + dtb
    dt_t = _softplus(pre_dt)
    a_neg = -jnp.exp(alog)
    acs_t = _mm_hi(dt_t * a_neg, upper)
    pick = lambda t, h: jnp.sum(jnp.where(head == h, t, 0.0), axis=0, keepdims=True)
    full_sum = lambda t: jnp.sum(jnp.sum(t, axis=1, keepdims=True), axis=0, keepdims=True)
    dt_row = [pick(dt_t, h) for h in heads]
    a_row = [pick(acs_t, h) for h in heads]
    a_rb = [jnp.broadcast_to(a_row[h], (BLK, BLK)) for h in heads]
    a_b = [a_rb[h].T for h in heads]
    a_last = [jnp.sum(jnp.where(last, a_b[h], 0.0), axis=0, keepdims=True) for h in heads]
    lm = [jnp.exp(jnp.where(causal, a_b[h] - a_rb[h], -1e30)) for h in heads]
    cgb = [cg[g].astype(MXU_DTYPE) for g in groups]
    bgb = [bg[g].astype(MXU_DTYPE) for g in groups]
    cb = [_mm_nt(cgb[g], bgb[g]) for g in groups]
    u = [cb[h // 4] * lm[h] for h in heads]
    w = [(u[h] * dt_row[h]).astype(MXU_DTYPE) for h in heads]
    e_row = [jnp.exp(a_last[h] - a_row[h]) for h in heads]
    f_row = [dt_row[h] * e_row[h] for h in heads]
    f_b = [jnp.broadcast_to(f_row[h], (BLK, BLK)).T for h in heads]
    e_b = [jnp.exp(a_b[h]) for h in heads]
    el = [jnp.exp(a_last[h]) for h in heads]
    xb = [xs[h].astype(MXU_DTYPE) for h in heads]
    dyb = [dys[h].astype(MXU_DTYPE) for h in heads]
    prevb = [prev[h].astype(MXU_DTYPE) for h in heads]
    dstb = [dhs[h].astype(MXU_DTYPE) for h in heads]
    gmat = [_mm(cgb[h // 4], prevb[h]) for h in heads]
    dw = [_mm_nt(dyb[h], xb[h]) for h in heads]
    dg = [dys[h] * e_b[h] for h in heads]
    dgb = [dg[h].astype(MXU_DTYPE) for h in heads]
    dxf = [_mm(bgb[h // 4], dstb[h]) for h in heads]
    xfb = [(xs[h] * f_b[h]).astype(MXU_DTYPE) for h in heads]
    dxs = [_mm_tn(w[h], dyb[h]) + d_rows[h] * dys[h] + f_b[h] * dxf[h] for h in heads]
    dd_rows = [jnp.sum(dys[h] * xs[h], axis=0, keepdims=True) for h in heads]
    dprev = [_mm_tn(cgb[h // 4], dgb[h]) + dhs[h] * el[h] for h in heads]
    dcg_h = [_mm_nt(dgb[h], prevb[h]) for h in heads]
    dbg_h = [_mm_nt(xfb[h], dstb[h]) for h in heads]
    zt = [dw[h] * u[h] for h in heads]
    dseg = [zt[h] * dt_row[h] for h in heads]
    dcb_h = [dw[h] * lm[h] * dt_row[h] for h in heads]
    dcb = [(dcb_h[4 * g] + dcb_h[4 * g + 1] + dcb_h[4 * g + 2] + dcb_h[4 * g + 3]).astype(MXU_DTYPE) for g in groups]
    dcg = [dcg_h[4 * g] + dcg_h[4 * g + 1] + dcg_h[4 * g + 2] + dcg_h[4 * g + 3] + _mm(dcb[g], bgb[g]) for g in groups]
    dbg = [dbg_h[4 * g] + dbg_h[4 * g + 1] + dbg_h[4 * g + 2] + dbg_h[4 * g + 3] + _mm_tn(dcb[g], cgb[g])
           for g in groups]
    r1 = [jnp.sum(dg[h] * gmat[h] + dseg[h], axis=1, keepdims=True) for h in heads]
    r2 = [jnp.sum(dxf[h] * xs[h], axis=1, keepdims=True) for h in heads]
    tt = [jnp.where(lane < HALF, jnp.broadcast_to(r1[h], (BLK, BLK)), jnp.broadcast_to(r2[h], (BLK, BLK))).T
          for h in heads]
    r1_row = [tt[h][0:1, :] for h in heads]
    r2_row = [tt[h][HALF:HALF + 1, :] for h in heads]
    d_el = [full_sum(dhs[h] * prev[h]) for h in heads]
    da_last = [jnp.sum(r2_row[h] * f_row[h], axis=1, keepdims=True) + el[h] * d_el[h] for h in heads]
    da_row = [r1_row[h] - jnp.sum(dseg[h], axis=0, keepdims=True) - r2_row[h] * f_row[h]
              + jnp.where(lane_row == BLK - 1, da_last[h], 0.0) for h in heads]
    ddt_row = [jnp.sum(zt[h], axis=0, keepdims=True) + r2_row[h] * e_row[h] for h in heads]
    da_t = jnp.zeros((N_HEADS, BLK), F32)
    ddt_t = jnp.zeros((N_HEADS, BLK), F32)
    for h in heads:
        da_t = jnp.where(head == h, da_row[h], da_t)
        ddt_t = jnp.where(head == h, ddt_row[h], ddt_t)
    d_dta = _mm_hi(da_t, causal.astype(F32))
    dalog = d_dta * dt_t * a_neg
    draw = (ddt_t + d_dta * a_neg) * jax.nn.sigmoid(pre_dt)
    return dxs, dbg, dcg, draw, dprev, draw, dalog, dd_rows


def _dt_rows(dt_blk):
    return dt_blk.T[:N_HEADS]


def _silu_grad(x):
    s = jax.nn.sigmoid(x)
    return s * (1.0 + x * (1.0 - s))


def _conv_pre(halo, blk, cw_ref, cb_ref):
    ext = jnp.concatenate([halo, blk], axis=0)
    taps = [pltpu.roll(ext, 3 - k, 0)[8:] for k in range(3)] + [blk]
    pre = cb_ref[...] + cw_ref[0:1, :] * taps[0]
    for k in range(1, 4):
        pre = pre + cw_ref[k:k + 1, :] * taps[k]
    return pre, taps


def _ssd_split(pre):
    heads = _split_heads(pre[:, :SSM_W], 4)
    pb = [pre[:, SSM_W + g * D_STATE:SSM_W + (g + 1) * D_STATE] for g in range(2)]
    pc = [pre[:, SSM_W + 2 * D_STATE + g * D_STATE:SSM_W + 2 * D_STATE + (g + 1) * D_STATE] for g in range(2)]
    return heads, pb, pc


def _ssd_fwd(xbc, dt_raw, conv_w, conv_b, dtb_row, alog_row, d_exp, xchg):
    S = xbc.shape[0]
    nc = S // BLK

    def body(xbc_ref, halo_ref, dt_ref, cw_ref, cb_ref, dtb_ref, alog_ref, d_ref, y_ref, prev_ref, state_ref):
        i = pl.program_id(0)

        @pl.when(i == 0)
        def _():
            state_ref[...] = jnp.zeros_like(state_ref)

        halo = halo_ref[...] * jnp.where(i > 0, 1.0, 0.0)
        pre, _ = _conv_pre(halo, xbc_ref[...], cw_ref, cb_ref)
        heads, pb, pc = _ssd_split(_silu(pre))
        prev = [state_ref[h] for h in range(N_HEADS)]
        for h in range(N_HEADS):
            prev_ref[0, h] = prev[h]
        d_rows = [d_ref[h:h + 1, :] for h in range(N_HEADS)]
        ys, hs = _ssd_chunk(heads, pb, pc, _dt_rows(dt_ref[...]), prev, dtb_ref[...], alog_ref[...], d_rows,
                            _ssd_consts())
        for h in range(N_HEADS):
            state_ref[h] = hs[h]
        y_ref[...] = _join_heads(ys)

    vec = pl.BlockSpec((N_HEADS, LANE), _fixed)
    return _hosted_call(
        body, "ssd_fwd", nc,
        in_specs=[pl.BlockSpec((BLK, XBC_W), _row),
                  pl.BlockSpec((8, XBC_W), lambda i: (jnp.maximum(i * (BLK // 8) - 1, 0), 0)),
                  pl.BlockSpec((BLK, LANE), _row),
                  pl.BlockSpec((4, XBC_W), _fixed), pl.BlockSpec((1, XBC_W), _fixed), vec, vec,
                  pl.BlockSpec((N_HEADS, LANE), _fixed)],
        out_specs=[pl.BlockSpec((BLK, SSM_W), _row),
                   pl.BlockSpec((1, N_HEADS, D_STATE, LANE), lambda i: (i, 0, 0, 0))],
        out_shape=[jax.ShapeDtypeStruct((S, SSM_W), F32), jax.ShapeDtypeStruct((nc, N_HEADS, D_STATE, LANE), F32)],
        scratch_shapes=[pltpu.VMEM((N_HEADS, D_STATE, LANE), F32)],
        args=(xbc, xbc, dt_raw, conv_w, conv_b, dtb_row, alog_row, d_exp), xchg=xchg, cparams=_cparams(),
    )


def _ssd_bwd(xbc, dt_raw, prev_states, dy, conv_w, conv_b, dtb_row, alog_row, d_exp, xchg):
    S = xbc.shape[0]
    nc = S // BLK

    def body(xbc_ref, halo_ref, dt_ref, prev_ref, dy_ref, cw_ref, cb_ref, dtb_ref, alog_ref, d_ref,
             dxbc_ref, ddt_ref, dcw_ref, dvec_ref, dd_ref, gstate_ref, ghalo_ref):
        i = pl.program_id(0)
        c = nc - 1 - i

        @pl.when(i == 0)
        def _():
            gstate_ref[...] = jnp.zeros_like(gstate_ref)
            ghalo_ref[...] = jnp.zeros_like(ghalo_ref)
            dcw_ref[...] = jnp.zeros_like(dcw_ref)
            dvec_ref[...] = jnp.zeros_like(dvec_ref)
            dd_ref[...] = jnp.zeros_like(dd_ref)

        halo = halo_ref[...] * jnp.where(c > 0, 1.0, 0.0)
        pre, taps = _conv_pre(halo, xbc_ref[...], cw_ref, cb_ref)
        heads, pb, pc = _ssd_split(_silu(pre))
        prev = [prev_ref[0, h] for h in range(N_HEADS)]
        d_rows = [d_ref[h:h + 1, :] for h in range(N_HEADS)]
        dys = _split_heads(dy_ref[...], 4)
        dhs = [gstate_ref[h] for h in range(N_HEADS)]
        dheads, dpb, dpc, ddt_t, dprev, ddtb, dalog, dd_rows = _ssd_chunk_bwd(
            heads, pb, pc, _dt_rows(dt_ref[...]), prev, dtb_ref[...], alog_ref[...], d_rows, dys, dhs, _ssd_consts())
        for h in range(N_HEADS):
            gstate_ref[h] = dprev[h]
            dd_ref[h:h + 1, :] += dd_rows[h]
        ddt_ref[...] = jnp.concatenate([ddt_t, jnp.zeros((BLK - N_HEADS, BLK), F32)], axis=0).T.astype(ddt_ref.dtype)
        dvec_ref[0:N_HEADS, :] += ddtb
        dvec_ref[N_HEADS:, :] += dalog
        dpre = jnp.concatenate([_join_heads(dheads)] + list(dpb) + list(dpc), axis=1) * _silu_grad(pre)
        zeros8 = jnp.zeros((8, XBC_W), F32)
        dpe = jnp.concatenate([zeros8, dpre, zeros8], axis=0)
        n_ext = 16 + BLK
        dext = cw_ref[3:4, :] * dpe[:8 + BLK]
        dcw_ref[3:4, :] += jnp.sum(dpre * taps[3], axis=0, keepdims=True)
        for k in range(3):
            dext = dext + cw_ref[k:k + 1, :] * pltpu.roll(dpe, n_ext - (3 - k), 0)[:8 + BLK]
            dcw_ref[k:k + 1, :] += jnp.sum(dpre * taps[k], axis=0, keepdims=True)
        dcw_ref[4:5, :] += jnp.sum(dpre, axis=0, keepdims=True)
        dxbc_ref[...] = jnp.concatenate([dext[8:BLK], dext[BLK:] + ghalo_ref[...]], axis=0).astype(dxbc_ref.dtype)
        ghalo_ref[...] = dext[:8, :]

    vec = pl.BlockSpec((N_HEADS, LANE), _fixed)
    rev = lambda i: (nc - 1 - i, 0)
    return _hosted_call(
        body, "ssd_bwd", nc,
        in_specs=[pl.BlockSpec((BLK, XBC_W), rev),
                  pl.BlockSpec((8, XBC_W), lambda i: (jnp.maximum((nc - 1 - i) * (BLK // 8) - 1, 0), 0)),
                  pl.BlockSpec((BLK, LANE), rev),
                  pl.BlockSpec((1, N_HEADS, D_STATE, LANE), lambda i: (nc - 1 - i, 0, 0, 0)),
                  pl.BlockSpec((BLK, SSM_W), rev),
                  pl.BlockSpec((4, XBC_W), _fixed), pl.BlockSpec((1, XBC_W), _fixed), vec, vec,
                  pl.BlockSpec((N_HEADS, LANE), _fixed)],
        out_specs=[pl.BlockSpec((BLK, XBC_W), rev), pl.BlockSpec((BLK, LANE), rev),
                   pl.BlockSpec((8, XBC_W), _fixed), pl.BlockSpec((2 * N_HEADS, LANE), _fixed),
                   pl.BlockSpec((N_HEADS, LANE), _fixed)],
        out_shape=[jax.ShapeDtypeStruct((S, XBC_W), MXU_DTYPE), jax.ShapeDtypeStruct((S, LANE), MXU_DTYPE),
                   jax.ShapeDtypeStruct((8, XBC_W), F32), jax.ShapeDtypeStruct((2 * N_HEADS, LANE), F32),
                   jax.ShapeDtypeStruct((N_HEADS, LANE), F32)],
        scratch_shapes=[pltpu.VMEM((N_HEADS, D_STATE, LANE), F32), pltpu.VMEM((8, XBC_W), F32)],
        args=(xbc, xbc, dt_raw, prev_states, dy, conv_w, conv_b, dtb_row, alog_row, d_exp), xchg=xchg,
        cparams=_cparams(VMEM_BIG),
    )


def _adamw_math(w, g, m, v):
    m = ADAM_B1 * m + (1.0 - ADAM_B1) * g
    v = ADAM_B2 * v + (1.0 - ADAM_B2) * jnp.square(g)
    m_hat = m / (1.0 - ADAM_B1 ** ADAM_STEP)
    v_hat = v / (1.0 - ADAM_B2 ** ADAM_STEP)
    delta = -ADAM_LR * (m_hat / (jnp.sqrt(v_hat) + ADAM_EPS) + ADAM_WD * w)
    return delta, m, v


def _reduce_adamw(parts, w, m, v, name):
    R, C = w.shape

    def body(p_ref, w_ref, m_ref, v_ref, g_ref, d_ref, nm_ref, nv_ref):
        g = p_ref[0].astype(F32)
        for i in range(1, N_DEV):
            g = g + p_ref[i].astype(F32)
        d, nm, nv = _adamw_math(w_ref[...], g, m_ref[...], v_ref[...])
        g_ref[...] = g
        d_ref[...] = d
        nm_ref[...] = nm
        nv_ref[...] = nv

    if R % 16 == 0:
        tr = max(t for t in range(16, 257, 16) if R % t == 0)
        n, blk, pblk = R // tr, pl.BlockSpec((tr, C), _row), pl.BlockSpec((N_DEV, tr, C), lambda i: (0, i, 0))
    else:
        tl = 256
        n, blk, pblk = C // tl, pl.BlockSpec((R, tl), lambda i: (0, i)), pl.BlockSpec((N_DEV, R, tl),
                                                                                      lambda i: (0, 0, i))
    return pl.pallas_call(
        body, name=name, grid=(n,), in_specs=[pblk, blk, blk, blk],
        out_specs=[blk] * 4, out_shape=[jax.ShapeDtypeStruct((R, C), F32)] * 4,
    )(parts, w, m, v)


def _reduce_adamw_hosting(parts_list, wmv_list, name, xchg):
    n_arr = len(parts_list)
    C = wmv_list[0][0].shape[1]
    tl = 256

    def body(*refs):
        p_refs, wmv_refs, o_refs = refs[:n_arr], refs[n_arr:4 * n_arr], refs[4 * n_arr:]
        for k in range(n_arr):
            g = p_refs[k][0].astype(F32)
            for i in range(1, N_DEV):
                g = g + p_refs[k][i].astype(F32)
            w_ref, m_ref, v_ref = wmv_refs[3 * k:3 * k + 3]
            d, nm, nv = _adamw_math(w_ref[...], g, m_ref[...], v_ref[...])
            for o, val in zip(o_refs[4 * k:4 * k + 4], (g, d, nm, nv)):
                o[...] = val

    in_specs = [pl.BlockSpec((N_DEV, w.shape[0], tl), lambda i: (0, 0, i)) for w, _, _ in wmv_list]
    in_specs += [pl.BlockSpec((w.shape[0], tl), lambda i: (0, i)) for w, _, _ in wmv_list for _ in range(3)]
    out_specs = [pl.BlockSpec((w.shape[0], tl), lambda i: (0, i)) for w, _, _ in wmv_list for _ in range(4)]
    out_shape = [jax.ShapeDtypeStruct(w.shape, F32) for w, _, _ in wmv_list for _ in range(4)]
    args = list(parts_list) + [a for wmv in wmv_list for a in wmv]
    outs, x_out = _hosted_call(body, name, C // tl, in_specs, out_specs, out_shape, [], args, xchg,
                               _cparams(VMEM_BIG))
    return [outs[4 * k:4 * k + 4] for k in range(n_arr)], x_out


_SMALL_NAMES = ("ada_b", "norm1", "conv_w", "conv_b", "dt_bias", "A_log", "D_skip", "sinks", "attn_out_norm",
                "ssm_out_norm", "norm2", "rel_bias", "final_norm")
N_MOD = 6 * D_MODEL


def _mod_row(a0, a1, a2):
    return jnp.concatenate([a0[2:3], a0[1:2], a1[0:1], a2[2:3], a2[1:2], a2[3:4]], axis=1)


def _small_update(gathered, params):
    n_g = len(gathered)
    flat = [a for name in _SMALL_NAMES for a in params[name]]

    def body(*refs):
        a0_ref, a1_ref, a2_ref, cw_ref, dv_ref, dd_ref, ds_ref, dr_ref, c_ref = refs[:n_g]
        wmv = refs[n_g:n_g + len(flat)]
        outs = refs[n_g + len(flat):]

        def total(ref):
            t = ref[0]
            for i in range(1, N_DEV):
                t = t + ref[i]
            return t

        t0, t1, t2, tcw, tdv, tdd, tds, tdr = [total(r) for r in (a0_ref, a1_ref, a2_ref, cw_ref, dv_ref, dd_ref,
                                                                   ds_ref, dr_ref)]
        r8 = lax.broadcasted_iota(jnp.int32, (N_HEADS, LANE), 0)
        l8 = lax.broadcasted_iota(jnp.int32, (N_HEADS, LANE), 1)

        def diag_row(t):
            return jnp.sum(jnp.where(r8 == l8, t, 0.0), axis=0, keepdims=True)[:, :N_HEADS]

        def lane_sums(t):
            return diag_row(jnp.broadcast_to(jnp.sum(t, axis=1, keepdims=True), (N_HEADS, LANE)))

        me = _lin(_my_pos())
        n_cw = XBC_W // N_DEV
        cw_mine = jnp.zeros((4, n_cw), F32)
        for j in range(N_DEV):
            cw_mine = cw_mine + tcw[0:4, j * n_cw:(j + 1) * n_cw] * jnp.where(me == j, 1.0, 0.0)
        grads = {
            "ada_b": _mod_row(t0, t1, t2), "norm1": t0[0:1], "conv_w": cw_mine, "conv_b": tcw[4:5],
            "dt_bias": lane_sums(tdv[:N_HEADS]), "A_log": lane_sums(tdv[N_HEADS:]), "D_skip": lane_sums(tdd),
            "sinks": diag_row(tds), "attn_out_norm": t1[1:2, :ATTN_W], "ssm_out_norm": t1[1:2, ATTN_W:],
            "norm2": t2[0:1], "rel_bias": tdr[:, :N_HEADS], "final_norm": t2[4:5],
        }
        for k, name in enumerate(_SMALL_NAMES):
            w_ref, m_ref, v_ref = wmv[3 * k:3 * k + 3]
            g = grads[name]
            d, nm, nv = _adamw_math(w_ref[...], g, m_ref[...], v_ref[...])
            for o, val in zip(outs[4 * k:4 * k + 4], (g, d, nm, nv)):
                o[...] = val
        loss_ref, call_ref, dmod_ref = outs[4 * len(_SMALL_NAMES):]
        loss_ref[...] = t2[5:6, 0:1]
        call_ref[...] = jnp.concatenate([c_ref[i] for i in range(N_DEV)], axis=0)
        dmod_ref[...] = jnp.concatenate([_mod_row(a0_ref[i], a1_ref[i], a2_ref[i]) for i in range(N_DEV)], axis=0)

    out_shape = [jax.ShapeDtypeStruct(params[name][0].shape, F32) for name in _SMALL_NAMES for _ in range(4)]
    out_shape += [jax.ShapeDtypeStruct((1, 1), F32), jax.ShapeDtypeStruct((N_DEV, D_MODEL), F32),
                  jax.ShapeDtypeStruct((N_DEV, N_MOD), F32)]
    res = pl.pallas_call(body, name="small_update", out_shape=out_shape)(*gathered, *flat)
    upd = {name: res[4 * k:4 * k + 4] for k, name in enumerate(_SMALL_NAMES)}
    loss, c_all, dmod_all = res[4 * len(_SMALL_NAMES):]
    return upd, loss, c_all, dmod_all


def _ada_w_update(c_all, dmod_all, w, m, v):
    chunk = w.shape[1]

    def body(c_ref, dm_ref, w_ref, m_ref, v_ref, g_ref, d_ref, nm_ref, nv_ref):
        me = _lin(_my_pos())
        dm = jnp.zeros((N_DEV, chunk), F32)
        for j in range(N_DEV):
            dm = dm + dm_ref[:, j * chunk:(j + 1) * chunk] * jnp.where(me == j, 1.0, 0.0)
        g = lax.dot_general(_silu(c_ref[...]), dm, (((0,), (0,)), ((), ())), precision=HI,
                            preferred_element_type=F32)
        d, nm, nv = _adamw_math(w_ref[...], g, m_ref[...], v_ref[...])
        g_ref[...] = g
        d_ref[...] = d
        nm_ref[...] = nm
        nv_ref[...] = nv

    tr = 256
    blk = pl.BlockSpec((tr, chunk), _row)
    return pl.pallas_call(
        body, name="ada_w_update", grid=(w.shape[0] // tr,),
        in_specs=[pl.BlockSpec((N_DEV, tr), lambda i: (0, i)), pl.BlockSpec(dmod_all.shape, _fixed), blk, blk, blk],
        out_specs=[blk] * 4, out_shape=[jax.ShapeDtypeStruct(w.shape, F32)] * 4,
    )(c_all, dmod_all, w, m, v)


def _local_step(x, tgt, c, mod, w_in, conv_w, w_o_mine, w_gu_mine, w_d_mine, p):
    S = x.shape[0]
    tm = min(512, S)
    tmm = min(256, S)
    tw = min(2048, S)
    shift1, scale1, gate1, shift2, scale2, gate2 = [mod[i:i + 1] for i in range(6)]
    buckets = jnp.asarray(_t5_bucket_table())
    per_head = lambda a: jnp.broadcast_to(a.reshape(N_HEADS, 1), (N_HEADS, LANE))
    dtb_row, alog_row, d_exp = per_head(p["dt_bias"]), per_head(p["A_log"]), per_head(p["D_skip"])
    sinks = p["sinks"].reshape(N_HEADS)

    d_cut, gu_cut = WD_CUT, WGU_CUTS
    (qkv, z, xbc, dt_raw), (g_d_a,) = _in_proj_fwd(x, p["norm1"], scale1, shift1, w_in, tm,
                                                   ([w_d_mine[:d_cut]], False))
    bias = _attn_bias(buckets, p["rel_bias"])
    (ya,), (g_gu_a,) = _attn_fwd(qkv, bias, sinks, ([w_gu_mine[:gu_cut[0]]], False))
    (ys, prev_states), (g_gu_b, g_o) = _ssd_fwd(xbc, dt_raw, conv_w, p["conv_b"], dtb_row, alog_row, d_exp,
                                                ([w_gu_mine[gu_cut[0]:gu_cut[1]], w_o_mine], False))
    w_o = g_o.reshape(D_MODEL, D_MODEL)
    x1, (g_gu_c, g_d_b) = _out_proj_fwd(x, ya, ys, z, p["attn_out_norm"], p["ssm_out_norm"], gate1, w_o, tm,
                                        ([w_gu_mine[gu_cut[1]:], w_d_mine[d_cut:]], False))
    dx1, h2, dgu, act, dmlp, acc2 = _mlp_loss(x1, tgt, p["norm2"], scale2, shift2, gate2, p["final_norm"],
                                              (g_gu_a, g_gu_b, g_gu_c), (g_d_a, g_d_b), tmm)
    g_w_gu = _wgrad(dgu, h2, 2 * D_FF // 4, tw, "wgrad_gate_up")
    g_w_d = _wgrad(act, dmlp, D_FF // 2, tw, "wgrad_down")
    dya, dys, dz, u, dmix, acc1 = _out_proj_bwd(dx1, ya, ys, z, p["attn_out_norm"], p["ssm_out_norm"], gate1, w_o, tm)
    g_w_o = _wgrad(u, dmix, D_MODEL, tw, "wgrad_out")
    (dq, dkv, dbias, dsk), (r_d,) = _attn_bwd(qkv, ya, dya, bias, sinks,
                                              ([g_w_d.reshape(N_DEV, D_FF // N_DEV, D_MODEL)], True))
    drel, dsink = _attn_finish(dbias, dsk, buckets)
    (dxbc, ddt, dcw, dvec, dd), (r_gu, r_o) = _ssd_bwd(
        xbc, dt_raw, prev_states, dys, conv_w, p["conv_b"], dtb_row, alog_row, d_exp,
        ([g_w_gu.reshape(N_DEV, 2 * D_FF // N_DEV, D_MODEL), g_w_o.reshape(N_DEV, D_MODEL // N_DEV, D_MODEL)], True))
    gx, h1, acc0 = _in_proj_bwd(x, dx1, dq, dkv, dz, dxbc, ddt, p["norm1"], scale1, shift1, w_in, tm)
    dproj = (dq, dkv, dz, dxbc, ddt)
    half = D_MODEL // 2
    slots = lambda g: g[:IN_W].reshape(N_DEV, IN_W // N_DEV, half)
    g_in_a, gathered = _wgrad(dproj, h1, IN_PAD, tw, "wgrad_in_a",
                              ([acc0, acc1, acc2, dcw, dvec, dd, dsink, drel, c], False), g_cols=(half, 0))
    g_in_b, (r_in_a,) = _wgrad(dproj, h1, IN_PAD, tw, "wgrad_in_b", ([slots(g_in_a)], True), g_cols=(half, 1))
    return gx, (r_in_a, slots(g_in_b)), (r_o, r_gu, r_d), gathered


def kernel(x, c, ada_w, ada_b, norm1, w_in, conv_w, conv_b, dt_bias, A_log, D_skip, sinks, attn_out_norm, ssm_out_norm, w_o, norm2, w_gate_up, w_down, rel_bias, final_norm, loss_target, m_ada_w, m_ada_b, m_norm1, m_w_in, m_conv_w, m_conv_b, m_dt_bias, m_A_log, m_D_skip, m_sinks, m_attn_out_norm, m_ssm_out_norm, m_w_o, m_norm2, m_w_gate_up, m_w_down, m_rel_bias, m_final_norm, v_ada_w, v_ada_b, v_norm1, v_w_in, v_conv_w, v_conv_b, v_dt_bias, v_A_log, v_D_skip, v_sinks, v_attn_out_norm, v_ssm_out_norm, v_w_o, v_norm2, v_w_gate_up, v_w_down, v_rel_bias, v_final_norm):
    two_d = lambda a: a if a.ndim == 2 else a.reshape(-1, a.shape[-1])
    small_params = dict(
        ada_b=(ada_b, m_ada_b, v_ada_b), norm1=(norm1, m_norm1, v_norm1), conv_w=(conv_w, m_conv_w, v_conv_w),
        conv_b=(conv_b, m_conv_b, v_conv_b), dt_bias=(dt_bias, m_dt_bias, v_dt_bias), A_log=(A_log, m_A_log, v_A_log),
        D_skip=(D_skip, m_D_skip, v_D_skip), sinks=(sinks, m_sinks, v_sinks),
        attn_out_norm=(attn_out_norm, m_attn_out_norm, v_attn_out_norm),
        ssm_out_norm=(ssm_out_norm, m_ssm_out_norm, v_ssm_out_norm), norm2=(norm2, m_norm2, v_norm2),
        rel_bias=(rel_bias, m_rel_bias, v_rel_bias), final_norm=(final_norm, m_final_norm, v_final_norm))
    small_params = {k: tuple(two_d(a) for a in v) for k, v in small_params.items()}
    S = x.shape[1]
    xs, tgt = x.reshape(S, D_MODEL), loss_target.reshape(S, D_MODEL)
    ada_w2 = ada_w[0]
    chunk = ada_w2.shape[1]
    t_in = [jnp.transpose(a[0]) for a in (w_in, m_w_in, v_w_in)]
    t_gu = [jnp.transpose(a[0]) for a in (w_gate_up, m_w_gate_up, v_w_gate_up)]

    mod, (g_in, g_cw) = _mod_and_gather(c, ada_w2, ada_b.reshape(N_DEV, chunk), [t_in[0].astype(WIRE_DTYPE), conv_w[0]])
    mod = mod.reshape(6, D_MODEL)
    w_in_full = jnp.pad(g_in.reshape(IN_W, D_MODEL), ((0, IN_PAD - IN_W), (0, 0)))
    conv_w_full = jnp.transpose(g_cw, (1, 0, 2)).reshape(4, XBC_W)

    p = {k: v[0] for k, v in small_params.items()}
    gx, (r_in_a, gw_in_b), (r_o, r_gu, r_d), gathered = _local_step(
        xs, tgt, c, mod, w_in_full, conv_w_full, w_o[0].astype(WIRE_DTYPE), t_gu[0].astype(WIRE_DTYPE),
        w_down[0].astype(WIRE_DTYPE), p)

    (u_gu, u_d, u_o), (r_in_b,) = _reduce_adamw_hosting(
        [r_gu, r_d, r_o], [tuple(t_gu), (w_down[0], m_w_down[0], v_w_down[0]), (w_o[0], m_w_o[0], v_w_o[0])],
        "adamw_big", ([gw_in_b], True))
    r_in = jnp.concatenate([r_in_a, r_in_b], axis=2)

    small, loss, c_all, dmod_all = _small_update(gathered, small_params)

    big = {
        "ada_w": _ada_w_update(c_all, dmod_all, ada_w2, m_ada_w[0], v_ada_w[0]),
        "w_in": [jnp.transpose(a) for a in _reduce_adamw(r_in, *t_in, "adamw_w_in")],
        "w_o": u_o,
        "w_gate_up": [jnp.transpose(a) for a in u_gu],
        "w_down": u_d,
    }
    big.update(small)

    order = ['ada_w', 'ada_b', 'norm1', 'w_in', 'conv_w', 'conv_b', 'dt_bias', 'A_log', 'D_skip', 'sinks',
             'attn_out_norm', 'ssm_out_norm', 'w_o', 'norm2', 'w_gate_up', 'w_down', 'rel_bias', 'final_norm']
    shapes = dict(ada_w=ada_w.shape, ada_b=ada_b.shape, norm1=norm1.shape, w_in=w_in.shape, conv_w=conv_w.shape,
                  conv_b=conv_b.shape, dt_bias=dt_bias.shape, A_log=A_log.shape, D_skip=D_skip.shape,
                  sinks=sinks.shape, attn_out_norm=attn_out_norm.shape, ssm_out_norm=ssm_out_norm.shape,
                  w_o=w_o.shape, norm2=norm2.shape, w_gate_up=w_gate_up.shape, w_down=w_down.shape,
                  rel_bias=rel_bias.shape, final_norm=final_norm.shape)
    outs = [[], [], [], []]
    for name in order:
        for kind in range(4):
            outs[kind].append(big[name][kind].reshape(shapes[name]))
    return (loss.reshape(()), gx.reshape(x.shape), *outs[0], *outs[1], *outs[2], *outs[3])
```

```python
import functools

import numpy as np
import jax
import jax.numpy as jnp
from jax import lax
from jax.experimental import pallas as pl
from jax.experimental.pallas import tpu as pltpu

F32 = jnp.float32
MXU_DTYPE = jnp.bfloat16
WIRE_DTYPE = jnp.bfloat16
HI = lax.Precision.HIGHEST
MESH = pl.DeviceIdType.MESH
N_DEV = 8

D_MODEL = 1024
ATTN_W = 512
KV_W = 128
SSM_W = 512
XBC_W = 1024
N_HEADS = 8
D_STATE = 128
D_FF = 2816
IN_W = 2312
IN_PAD = 2432
BLK = 128
N_BUCKETS = 32
EPS = 1e-6
LANE = 128
HALF = 64

ADAM_LR, ADAM_B1, ADAM_B2, ADAM_EPS, ADAM_WD, ADAM_STEP = 0.001, 0.9, 0.999, 1e-08, 0.01, 10

VMEM_BIG = 56 * 1024 * 1024
WD_CUT = 256
WGU_CUTS = (304, 608)


def _cparams(vmem=None):
    if vmem is None:
        return pltpu.CompilerParams()
    return pltpu.CompilerParams(vmem_limit_bytes=vmem)


def _mm(a, b):
    return jnp.dot(a.astype(MXU_DTYPE), b.astype(MXU_DTYPE), preferred_element_type=F32)


def _mm_nt(a, b):
    return lax.dot_general(a.astype(MXU_DTYPE), b.astype(MXU_DTYPE), (((1,), (1,)), ((), ())),
                           preferred_element_type=F32)


def _mm_tn(a, b):
    return lax.dot_general(a.astype(MXU_DTYPE), b.astype(MXU_DTYPE), (((0,), (0,)), ((), ())),
                           preferred_element_type=F32)


def _mm_hi(a, b):
    return jnp.dot(a, b, precision=HI, preferred_element_type=F32)


def _silu(x):
    return x * jax.nn.sigmoid(x)


def _softplus(x):
    return jnp.maximum(x, 0.0) + jnp.log1p(jnp.exp(-jnp.abs(x)))


def _rms(x, g, n):
    return x * lax.rsqrt(jnp.sum(x * x, axis=-1, keepdims=True) * (1.0 / n) + EPS) * g


def _modnorm(x, g, scale, shift):
    return _rms(x, g, x.shape[-1]) * (1.0 + scale) + shift


def _modnorm_parts(x):
    r = lax.rsqrt(jnp.sum(x * x, axis=-1, keepdims=True) * (1.0 / x.shape[-1]) + EPS)
    return r, x * r


def _modnorm_bwd(r, xhat, g, scale, dy):
    dyg = dy * (g * (1.0 + scale))
    c = jnp.sum(dyg * xhat, axis=-1, keepdims=True) * (1.0 / xhat.shape[-1])
    dx = r * (dyg - xhat * c)
    ct = jnp.sum(dy * xhat, axis=0, keepdims=True)
    return dx, ct * (1.0 + scale), ct * g, jnp.sum(dy, axis=0, keepdims=True)


def _lane_iota(shape):
    return lax.broadcasted_iota(jnp.int32, shape, len(shape) - 1)


def _split_pair(t):
    lane = _lane_iota(t.shape)
    lo = jnp.where(lane < HALF, t, 0.0)
    hi = pltpu.roll(jnp.where(lane >= HALF, t, 0.0), HALF, 1)
    return lo, hi


def _join_pair(lo, hi):
    lane = _lane_iota(lo.shape)
    return jnp.where(lane < HALF, lo, pltpu.roll(hi, HALF, 1))


def _split_heads(t, n_pairs):
    out = []
    for p in range(n_pairs):
        out.extend(_split_pair(t[:, p * LANE:(p + 1) * LANE]))
    return out


def _join_heads(hs):
    return jnp.concatenate([_join_pair(hs[2 * p], hs[2 * p + 1]) for p in range(len(hs) // 2)], axis=1)


def _t5_bucket_table():
    dist = np.arange(BLK)[:, None] + BLK - np.arange(2 * BLK)[None, :]
    n = np.maximum(dist, 0)
    max_exact = N_BUCKETS // 2
    large = max_exact + (np.log(np.maximum(n, 1) / max_exact) / np.log(128 / max_exact)
                         * (N_BUCKETS - max_exact)).astype(np.int32)
    large = np.minimum(large, N_BUCKETS - 1)
    return np.where(n < max_exact, n, large).astype(np.int32)


def _my_pos():
    return lax.axis_index("x"), lax.axis_index("y"), lax.axis_index("c")


def _peer(k):
    x, y, c = _my_pos()
    return (1 - x if k & 4 else x, 1 - y if k & 2 else y, 1 - c if k & 1 else c)


def _lin(pos):
    return 4 * pos[0] + 2 * pos[1] + pos[2]


def _xchg_copies(ins, outs, sems, scatter):
    local_sem, send_sem, recv_sem = sems
    me = _lin(_my_pos())
    local, remote = [], []
    for a in range(len(ins)):
        src = ins[a].at[me] if scatter else ins[a]
        local.append(pltpu.make_async_copy(src, outs[a].at[me], local_sem.at[a]))
    for k in range(1, N_DEV):
        peer = _peer(k)
        for a in range(len(ins)):
            src = ins[a].at[_lin(peer)] if scatter else ins[a]
            remote.append(pltpu.make_async_remote_copy(src, outs[a].at[me], send_sem.at[a, k - 1],
                                                       recv_sem.at[a, k - 1], device_id=peer, device_id_type=MESH))
    return local, remote


def _xchg_start(ins, outs, sems, scatter):
    local, remote = _xchg_copies(ins, outs, sems, scatter)
    for cp in local + remote:
        cp.start()


def _xchg_wait(ins, outs, sems, scatter):
    local, remote = _xchg_copies(ins, outs, sems, scatter)
    for cp in local:
        cp.wait()
    for cp in remote:
        cp.wait_send()
        cp.wait_recv()


def _xchg_shapes(arrs, scatter):
    n = len(arrs)
    if scatter:
        out_shape = [jax.ShapeDtypeStruct(a.shape, a.dtype) for a in arrs]
    else:
        out_shape = [jax.ShapeDtypeStruct((N_DEV,) + a.shape, a.dtype) for a in arrs]
    sems = [pltpu.SemaphoreType.DMA((n,)), pltpu.SemaphoreType.DMA((n, N_DEV - 1)),
            pltpu.SemaphoreType.DMA((n, N_DEV - 1))]
    return out_shape, sems


_CHIPS = (2, 4, 6)


def _g2_sems(n):
    dma = pltpu.SemaphoreType.DMA
    return [dma((n,)), dma((n, N_DEV)), dma((n, N_DEV)), dma((n, len(_CHIPS))), dma((n, len(_CHIPS)))]


class _TwoLevelGather:
    def __init__(self, ins, outs, sems):
        self.ins, self.outs = ins, outs
        self.local_sem, self.send_sem, self.recv_sem, self.fsend_sem, self.frecv_sem = sems
        self.n = len(ins)

    def _direct(self, a, k):
        return pltpu.make_async_remote_copy(self.ins[a], self.outs[a].at[_lin(_my_pos())], self.send_sem.at[a, k],
                                            self.recv_sem.at[a, k], device_id=_peer(k), device_id_type=MESH)

    def _handed_on(self, a, j, origin):
        slot = self.outs[a].at[origin]
        return pltpu.make_async_remote_copy(slot, slot, self.fsend_sem.at[a, j], self.frecv_sem.at[a, j],
                                            device_id=_peer(1), device_id_type=MESH)

    def _local(self, a):
        return pltpu.make_async_copy(self.ins[a], self.outs[a].at[_lin(_my_pos())], self.local_sem.at[a])

    def start(self):
        for a in range(self.n):
            self._local(a).start()
        for k in (1,) + _CHIPS:
            for a in range(self.n):
                self._direct(a, k).start()

    def forward(self):
        for j, k in enumerate(_CHIPS):
            for a in range(self.n):
                self._direct(a, k).wait_recv()
                self._handed_on(a, j, _lin(_peer(k))).start()

    def finish(self):
        for a in range(self.n):
            self._direct(a, 1).wait_recv()
            for j, k in enumerate(_CHIPS):
                self._handed_on(a, j, _lin(_peer(k ^ 1))).wait_recv()
            self._local(a).wait()
            for k in (1,) + _CHIPS:
                self._direct(a, k).wait_send()
            for j, k in enumerate(_CHIPS):
                self._handed_on(a, j, _lin(_peer(k))).wait_send()


def _mod_and_gather(c, ada_w, ada_b8, arrs):
    n = len(arrs)
    chunk = ada_w.shape[1]
    out_shape = [jax.ShapeDtypeStruct((N_DEV, 1, chunk), F32)]
    out_shape += [jax.ShapeDtypeStruct((N_DEV,) + a.shape, a.dtype) for a in arrs]

    def modulation(c_ref, w_ref, b_ref, out_ref, cbuf, part, s1, r1, s2, r2):
        me = _lin(_my_pos())
        first = []
        for k in range(1, N_DEV):
            cp = pltpu.make_async_remote_copy(c_ref, cbuf.at[me], s1.at[k - 1], r1.at[k - 1],
                                              device_id=_peer(k), device_id_type=MESH)
            cp.start()
            first.append(cp)
        cbuf[me] = c_ref[...]
        for cp in first:
            cp.wait_send()
            cp.wait_recv()
        cond = _silu(jnp.concatenate([cbuf[i] for i in range(N_DEV)], axis=0))
        mod = _mm_hi(cond, w_ref[...]) + b_ref[pl.ds(me, 1), :]
        for j in range(N_DEV):
            part[j] = mod[j:j + 1, :]
        second = []
        for k in range(1, N_DEV):
            peer = _peer(k)
            cp = pltpu.make_async_remote_copy(part.at[_lin(peer)], out_ref.at[me], s2.at[k - 1], r2.at[k - 1],
                                              device_id=peer, device_id_type=MESH)
            cp.start()
            second.append(cp)
        out_ref[me] = part[me]
        for cp in second:
            cp.wait_send()
            cp.wait_recv()

    def body(*refs):
        c_ref, w_ref, b_ref = refs[:3]
        ins = refs[3:3 + n]
        mod_ref = refs[3 + n]
        outs = refs[4 + n:4 + 2 * n]
        cbuf, part, s1, r1, s2, r2 = refs[4 + 2 * n:10 + 2 * n]
        gather = _TwoLevelGather(ins, outs, refs[10 + 2 * n:])
        gather.start()
        modulation(c_ref, w_ref, b_ref, mod_ref, cbuf, part, s1, r1, s2, r2)
        gather.forward()
        gather.finish()

    hbm = pl.BlockSpec(memory_space=pltpu.HBM)
    vm = pl.BlockSpec(memory_space=pltpu.VMEM)
    dma = pltpu.SemaphoreType.DMA
    res = pl.pallas_call(
        body, name="mod_and_gather", out_shape=out_shape, in_specs=[vm, vm, vm] + [hbm] * n,
        out_specs=[vm] + [hbm] * n,
        scratch_shapes=[pltpu.VMEM((N_DEV, 1, D_MODEL), F32), pltpu.VMEM((N_DEV, 1, chunk), F32)]
        + [dma((N_DEV - 1,))] * 4 + _g2_sems(n),
    )(c, ada_w, ada_b8, *arrs)
    return res[0], res[1:]


def _hosted_call(body, name, grid, in_specs, out_specs, out_shape, scratch_shapes, args, xchg, cparams):
    arrs, scatter = xchg
    grid = (grid,) if isinstance(grid, int) else tuple(grid)
    n, n_in, n_out, n_scr = len(arrs), len(in_specs), len(out_specs), len(scratch_shapes)
    two_level = scatter == "two-level"
    x_shape, x_sems = _xchg_shapes(arrs, False if two_level else scatter)
    if two_level:
        x_sems = _g2_sems(n)
    n_steps = int(np.prod(grid))

    def hosted(*refs):
        ins, refs = refs[:n_in], refs[n_in:]
        x_in, refs = refs[:n], refs[n:]
        outs, refs = refs[:n_out], refs[n_out:]
        x_out, refs = refs[:n], refs[n:]
        scr, sems = refs[:n_scr], refs[n_scr:]
        step = pl.program_id(0)
        for d in range(1, len(grid)):
            step = step * grid[d] + pl.program_id(d)

        @pl.when(step == 0)
        def _():
            if two_level:
                _TwoLevelGather(x_in, x_out, sems).start()
            else:
                _xchg_start(x_in, x_out, sems, scatter)

        if two_level:
            @pl.when(step == (2 * n_steps) // 3)
            def _():
                _TwoLevelGather(x_in, x_out, sems).forward()

        body(*ins, *outs, *scr)

        @pl.when(step == n_steps - 1)
        def _():
            if two_level:
                _TwoLevelGather(x_in, x_out, sems).finish()
            else:
                _xchg_wait(x_in, x_out, sems, scatter)

    hbm = pl.BlockSpec(memory_space=pltpu.HBM)
    res = pl.pallas_call(
        hosted, name=name, grid=grid, in_specs=list(in_specs) + [hbm] * n,
        out_specs=list(out_specs) + [hbm] * n, out_shape=list(out_shape) + x_shape,
        scratch_shapes=list(scratch_shapes) + x_sems, compiler_params=cparams,
    )(*args, *arrs)
    return res[:n_out], res[n_out:]


def _row(i):
    return (i, 0)


def _fixed(i):
    return (0, 0)


def _in_proj_fwd(x, norm1, scale1, shift1, w_in, tm, xchg):
    S = x.shape[0]

    def body(x_ref, n_ref, sc_ref, sh_ref, w_ref, qkv_ref, z_ref, xbc_ref, dt_ref):
        h = _modnorm(x_ref[...], n_ref[...], sc_ref[...], sh_ref[...])
        p = _mm_nt(h, w_ref[...])
        qkv_ref[...] = p[:, :768].astype(qkv_ref.dtype)
        z_ref[...] = p[:, 768:1280]
        xbc_ref[...] = p[:, 1280:2304]
        dt_ref[...] = p[:, 2304:IN_PAD]

    vec = pl.BlockSpec((1, D_MODEL), _fixed)
    return _hosted_call(
        body, "in_proj_fwd", S // tm,
        in_specs=[pl.BlockSpec((tm, D_MODEL), _row), vec, vec, vec, pl.BlockSpec((IN_PAD, D_MODEL), _fixed)],
        out_specs=[pl.BlockSpec((tm, 768), _row), pl.BlockSpec((tm, SSM_W), _row),
                   pl.BlockSpec((tm, XBC_W), _row), pl.BlockSpec((tm, LANE), _row)],
        out_shape=[jax.ShapeDtypeStruct((S, 768), MXU_DTYPE), jax.ShapeDtypeStruct((S, SSM_W), F32),
                   jax.ShapeDtypeStruct((S, XBC_W), F32), jax.ShapeDtypeStruct((S, LANE), F32)],
        scratch_shapes=[], args=(x, norm1, scale1, shift1, w_in), xchg=xchg, cparams=_cparams(VMEM_BIG),
    )


def _in_proj_bwd(x, dx1, dq, dkv, dz, dxbc, ddt, norm1, scale1, shift1, w_in, tm):
    S = x.shape[0]

    def body(x_ref, dx1_ref, dq_ref, dkv_ref, dz_ref, dxbc_ref, ddt_ref, n_ref, sc_ref, sh_ref, w_ref,
             gx_ref, h_ref, acc_ref):
        @pl.when(pl.program_id(0) == 0)
        def _():
            acc_ref[...] = jnp.zeros_like(acc_ref)

        dp = jnp.concatenate([dq_ref[...], dkv_ref[...], dz_ref[...], dxbc_ref[...], ddt_ref[...]], axis=1)
        dh = _mm(dp, w_ref[...])
        r, xhat = _modnorm_parts(x_ref[...])
        dx, dn, dsc, dsh = _modnorm_bwd(r, xhat, n_ref[...], sc_ref[...], dh)
        gx_ref[...] = dx1_ref[...] + dx
        h_ref[...] = (xhat * n_ref[...] * (1.0 + sc_ref[...]) + sh_ref[...]).astype(h_ref.dtype)
        acc_ref[0:1, :] += dn
        acc_ref[1:2, :] += dsc
        acc_ref[2:3, :] += dsh

    vec = pl.BlockSpec((1, D_MODEL), _fixed)
    return pl.pallas_call(
        body, name="in_proj_bwd", grid=(S // tm,),
        in_specs=[pl.BlockSpec((tm, D_MODEL), _row), pl.BlockSpec((tm, D_MODEL), _row),
                  pl.BlockSpec((tm, ATTN_W), _row), pl.BlockSpec((tm, 2 * KV_W), _row),
                  pl.BlockSpec((tm, SSM_W), _row), pl.BlockSpec((tm, XBC_W), _row), pl.BlockSpec((tm, LANE), _row),
                  vec, vec, vec, pl.BlockSpec((IN_PAD, D_MODEL), _fixed)],
        out_specs=[pl.BlockSpec((tm, D_MODEL), _row), pl.BlockSpec((tm, D_MODEL), _row),
                   pl.BlockSpec((8, D_MODEL), _fixed)],
        out_shape=[jax.ShapeDtypeStruct((S, D_MODEL), F32), jax.ShapeDtypeStruct((S, D_MODEL), MXU_DTYPE),
                   jax.ShapeDtypeStruct((8, D_MODEL), F32)],
        compiler_params=_cparams(VMEM_BIG),
    )(x, dx1, dq, dkv, dz, dxbc, ddt, norm1, scale1, shift1, w_in)


def _out_stage(ya, ys0, ys1, z0, z1, an, sn0, sn1):
    half = SSM_W // 2
    a = _rms(ya, an, ATTN_W)
    g0 = _rms(ys0 * _silu(z0), sn0, half)
    g1 = _rms(ys1 * _silu(z1), sn1, half)
    return jnp.concatenate([a, g0, g1], axis=1)


def _out_stage_args(ya_ref, ys_ref, z_ref, an_ref, sn_ref):
    half = SSM_W // 2
    return (ya_ref[...], ys_ref[:, :half], ys_ref[:, half:], z_ref[:, :half], z_ref[:, half:],
            an_ref[...], sn_ref[:, :half], sn_ref[:, half:])


def _out_proj_fwd(x, ya, ys, z, an, sn, gate1, w_o, tm, xchg):
    S = x.shape[0]

    def body(x_ref, ya_ref, ys_ref, z_ref, an_ref, sn_ref, g_ref, w_ref, x1_ref):
        u = _out_stage(*_out_stage_args(ya_ref, ys_ref, z_ref, an_ref, sn_ref))
        x1_ref[...] = x_ref[...] + g_ref[...] * _mm(u, w_ref[...])

    half = pl.BlockSpec((tm, ATTN_W), _row)
    hvec = pl.BlockSpec((1, ATTN_W), _fixed)
    (x1,), x_out = _hosted_call(
        body, "out_proj_fwd", S // tm,
        in_specs=[pl.BlockSpec((tm, D_MODEL), _row), half, half, half, hvec, hvec,
                  pl.BlockSpec((1, D_MODEL), _fixed), pl.BlockSpec((D_MODEL, D_MODEL), _fixed)],
        out_specs=[pl.BlockSpec((tm, D_MODEL), _row)],
        out_shape=[jax.ShapeDtypeStruct((S, D_MODEL), F32)],
        scratch_shapes=[], args=(x, ya, ys, z, an, sn, gate1, w_o), xchg=xchg, cparams=_cparams(VMEM_BIG),
    )
    return x1, x_out


def _out_proj_bwd(dx1, ya, ys, z, an, sn, gate1, w_o, tm):
    S = dx1.shape[0]

    def body(dx1_ref, ya_ref, ys_ref, z_ref, an_ref, sn_ref, g_ref, w_ref,
             dya_ref, dys_ref, dz_ref, u_ref, dmix_ref, acc_ref):
        @pl.when(pl.program_id(0) == 0)
        def _():
            acc_ref[...] = jnp.zeros_like(acc_ref)

        u, vjp = jax.vjp(_out_stage, *_out_stage_args(ya_ref, ys_ref, z_ref, an_ref, sn_ref))
        dx1 = dx1_ref[...]
        mix = _mm(u, w_ref[...])
        dmix = dx1 * g_ref[...]
        du = _mm_nt(dmix, w_ref[...])
        dya, dys0, dys1, dz0, dz1, dan, dsn0, dsn1 = vjp(du)
        dya_ref[...] = dya
        dys_ref[...] = jnp.concatenate([dys0, dys1], axis=1)
        dz_ref[...] = jnp.concatenate([dz0, dz1], axis=1).astype(dz_ref.dtype)
        u_ref[...] = u.astype(u_ref.dtype)
        dmix_ref[...] = dmix.astype(dmix_ref.dtype)
        acc_ref[0:1, :] += jnp.sum(dx1 * mix, axis=0, keepdims=True)
        acc_ref[1:2, :] += jnp.concatenate([dan, dsn0, dsn1], axis=1)

    half = pl.BlockSpec((tm, ATTN_W), _row)
    hvec = pl.BlockSpec((1, ATTN_W), _fixed)
    full = pl.BlockSpec((tm, D_MODEL), _row)
    return pl.pallas_call(
        body, name="out_proj_bwd", grid=(S // tm,),
        in_specs=[full, half, half, half, hvec, hvec,
                  pl.BlockSpec((1, D_MODEL), _fixed), pl.BlockSpec((D_MODEL, D_MODEL), _fixed)],
        out_specs=[half, half, half, full, full, pl.BlockSpec((8, D_MODEL), _fixed)],
        out_shape=[jax.ShapeDtypeStruct((S, ATTN_W), F32)] * 2 + [jax.ShapeDtypeStruct((S, ATTN_W), MXU_DTYPE)]
        + [jax.ShapeDtypeStruct((S, D_MODEL), MXU_DTYPE)] * 2 + [jax.ShapeDtypeStruct((8, D_MODEL), F32)],
        compiler_params=_cparams(VMEM_BIG),
    )(dx1, ya, ys, z, an, sn, gate1, w_o)


def _loss_rows(x2, fn, tgt):
    y = _rms(x2, fn, D_MODEL)
    per_row = jnp.sum(jnp.square(y - tgt), axis=1, keepdims=True)
    return jnp.sum(per_row, axis=0, keepdims=True) * (0.5 / D_MODEL)


def _mlp_loss(x1, tgt, norm2, scale2, shift2, gate2, fnorm, w_gu, w_d, tm):
    S = x1.shape[0]
    n_pieces = len(w_gu) + len(w_d)

    def body(*refs):
        x1_ref, t_ref, n_ref, sc_ref, sh_ref, g_ref, fn_ref = refs[:7]
        piece_refs = refs[7:7 + n_pieces]
        dx1_ref, h_ref, dgu_ref, act_ref, dmlp_ref, acc_ref, wgu, wd, wsem = refs[7 + n_pieces:]

        @pl.when(pl.program_id(0) == 0)
        def _():
            acc_ref[...] = jnp.zeros_like(acc_ref)
            copies = []
            for dst, pieces in ((wgu, piece_refs[:len(w_gu)]), (wd, piece_refs[len(w_gu):])):
                shard = sum(p.shape[1] for p in pieces)
                off = 0
                for p in pieces:
                    for j in range(N_DEV):
                        copies.append(pltpu.make_async_copy(p.at[j], dst.at[pl.ds(j * shard + off, p.shape[1])],
                                                            wsem.at[len(copies)]))
                    off += p.shape[1]
            for cp in copies:
                cp.start()
            for cp in copies:
                cp.wait()

        x1 = x1_ref[...]
        gate2 = g_ref[...]
        h, vjp_h = jax.vjp(_modnorm, x1, n_ref[...], sc_ref[...], sh_ref[...])
        hb = h.astype(MXU_DTYPE)
        gu = _mm_nt(hb, wgu[...])
        g, u = gu[:, :D_FF], gu[:, D_FF:]
        sg = jax.nn.sigmoid(g)
        silu_g = g * sg
        act = (silu_g * u).astype(MXU_DTYPE)
        mlp = _mm(act, wd[...])
        x2 = x1 + gate2 * mlp
        loss, vjp_loss = jax.vjp(_loss_rows, x2, fn_ref[...], t_ref[...])
        dx2, dfn, _ = vjp_loss(jnp.ones((1, 1), F32))
        dmlp = (dx2 * gate2).astype(MXU_DTYPE)
        dact = _mm_nt(dmlp, wd[...])
        dg = dact * u * (sg * (1.0 + g * (1.0 - sg)))
        du = dact * silu_g
        dgu = jnp.concatenate([dg, du], axis=1).astype(MXU_DTYPE)
        dh = _mm(dgu, wgu[...])
        dx, dn, dsc, dsh = vjp_h(dh)
        dx1_ref[...] = dx2 + dx
        h_ref[...] = hb
        dgu_ref[...] = dgu
        act_ref[...] = act
        dmlp_ref[...] = dmlp
        acc_ref[0:1, :] += dn
        acc_ref[1:2, :] += dsc
        acc_ref[2:3, :] += dsh
        acc_ref[3:4, :] += jnp.sum(dx2 * mlp, axis=0, keepdims=True)
        acc_ref[4:5, :] += dfn
        acc_ref[5:6, :] += jnp.broadcast_to(loss, (1, D_MODEL))

    full = pl.BlockSpec((tm, D_MODEL), _row)
    vec = pl.BlockSpec((1, D_MODEL), _fixed)
    anyspec = pl.BlockSpec(memory_space=pl.ANY)
    return pl.pallas_call(
        body, name="mlp_loss", grid=(S // tm,),
        in_specs=[full, full, vec, vec, vec, vec, vec] + [anyspec] * n_pieces,
        out_specs=[full, full, pl.BlockSpec((tm, 2 * D_FF), _row), pl.BlockSpec((tm, D_FF), _row), full,
                   pl.BlockSpec((8, D_MODEL), _fixed)],
        out_shape=[jax.ShapeDtypeStruct((S, D_MODEL), F32), jax.ShapeDtypeStruct((S, D_MODEL), MXU_DTYPE),
                   jax.ShapeDtypeStruct((S, 2 * D_FF), MXU_DTYPE), jax.ShapeDtypeStruct((S, D_FF), MXU_DTYPE),
                   jax.ShapeDtypeStruct((S, D_MODEL), MXU_DTYPE), jax.ShapeDtypeStruct((8, D_MODEL), F32)],
        scratch_shapes=[pltpu.VMEM((2 * D_FF, D_MODEL), MXU_DTYPE), pltpu.VMEM((D_FF, D_MODEL), MXU_DTYPE),
                        pltpu.SemaphoreType.DMA((N_DEV * n_pieces,))],
        compiler_params=_cparams(VMEM_BIG),
    )(x1, tgt, norm2, scale2, shift2, gate2, fnorm, *w_gu, *w_d)


def _wgrad(a, g, tk, ts, name, xchg=None, g_cols=None):
    pieces = list(a) if isinstance(a, (list, tuple)) else [a]
    S = pieces[0].shape[0]
    K = sum(p.shape[1] for p in pieces)
    assert len(pieces) == 1 or tk == K
    N, col = (g.shape[1], 0) if g_cols is None else g_cols
    ns = S // ts
    n_a = len(pieces)

    def body(*refs):
        a_refs, (g_ref, o_ref, acc_ref) = refs[:n_a], refs[n_a:]
        s = pl.program_id(1)

        @pl.when(s == 0)
        def _():
            acc_ref[...] = jnp.zeros_like(acc_ref)

        a_blk = a_refs[0][...] if n_a == 1 else jnp.concatenate([r[...] for r in a_refs], axis=1)
        acc_ref[...] += _mm_tn(a_blk, g_ref[...])

        @pl.when(s == ns - 1)
        def _():
            o_ref[...] = acc_ref[...].astype(o_ref.dtype)

    if n_a == 1:
        in_specs = [pl.BlockSpec((ts, tk), lambda j, s: (s, j))]
    else:
        in_specs = [pl.BlockSpec((ts, p.shape[1]), lambda j, s: (s, 0)) for p in pieces]
    in_specs.append(pl.BlockSpec((ts, N), lambda j, s: (s, col)))
    out_spec = pl.BlockSpec((tk, N), lambda j, s: (j, 0))
    out_shape = jax.ShapeDtypeStruct((K, N), WIRE_DTYPE)
    scratch = [pltpu.VMEM((tk, N), F32)]
    args = (*pieces, g)
    if xchg is None:
        return pl.pallas_call(body, name=name, grid=(K // tk, ns), in_specs=in_specs, out_specs=out_spec,
                              out_shape=out_shape, scratch_shapes=scratch, compiler_params=_cparams(VMEM_BIG))(*args)
    (out,), x_out = _hosted_call(body, name, (K // tk, ns), in_specs, [out_spec], [out_shape], scratch, args, xchg,
                                 _cparams(VMEM_BIG))
    return out, x_out


MASKED = -1e30
QK_SCALE = HALF ** -0.5


def _attn_bias(buckets, rel_bias):
    def body(bk_ref, relb_ref, out_ref):
        bk = bk_ref[...]
        i = lax.broadcasted_iota(jnp.int32, (BLK, 2 * BLK), 0)
        j = lax.broadcasted_iota(jnp.int32, (BLK, 2 * BLK), 1)
        window = (j > i) & (j <= i + BLK)
        for h in range(N_HEADS):
            acc = jnp.zeros((BLK, 2 * BLK), F32)
            for b in range(N_BUCKETS):
                acc = jnp.where(bk == b, relb_ref[b, h], acc)
            out_ref[0, h] = jnp.where(window, acc, MASKED)
            out_ref[1, h] = jnp.where(window & (j >= BLK), acc, MASKED)

    return pl.pallas_call(
        body, name="attn_bias", out_shape=jax.ShapeDtypeStruct((2, N_HEADS, BLK, 2 * BLK), F32),
        in_specs=[pl.BlockSpec(memory_space=pltpu.VMEM), pl.BlockSpec(memory_space=pltpu.SMEM)],
    )(buckets, rel_bias)


def _attn_kv(kvp_ref, kvc_ref):
    kvp = kvp_ref[...].astype(F32)
    kvc = kvc_ref[...].astype(F32)
    kp, kc = _split_pair(kvp[:, :LANE]), _split_pair(kvc[:, :LANE])
    vp, vc = _split_pair(kvp[:, LANE:]), _split_pair(kvc[:, LANE:])
    k_pads = [jnp.concatenate([kp[g], kc[g]], axis=0).astype(MXU_DTYPE) for g in range(2)]
    v_pads = [jnp.concatenate([vp[g], vc[g]], axis=0).astype(MXU_DTYPE) for g in range(2)]
    return k_pads, v_pads


def _attn_fwd(qkv, bias, sinks, xchg):
    S = qkv.shape[0]
    nb = S // BLK

    def body(q_ref, kvp_ref, kvc_ref, bias_ref, sinks_ref, y_ref):
        first = jnp.where(pl.program_id(0) == 0, 1, 0)
        q_heads = _split_heads(q_ref[...].astype(F32) * QK_SCALE, 4)
        k_pads, v_pads = _attn_kv(kvp_ref, kvc_ref)
        heads = range(N_HEADS)
        s = [_mm_nt(q_heads[h].astype(MXU_DTYPE), k_pads[h // 4]) + bias_ref[first, h] for h in heads]
        m = [jnp.maximum(jnp.max(s[h], axis=-1, keepdims=True), sinks_ref[h]) for h in heads]
        p = [jnp.exp(s[h] - m[h]) for h in heads]
        rinv = [1.0 / (jnp.sum(p[h], axis=-1, keepdims=True) + jnp.exp(sinks_ref[h] - m[h])) for h in heads]
        y_ref[...] = _join_heads([_mm(p[h], v_pads[h // 4]) * rinv[h] for h in heads])

    smem = pl.BlockSpec(memory_space=pltpu.SMEM)
    return _hosted_call(
        body, "attn_fwd", nb,
        in_specs=[pl.BlockSpec((BLK, ATTN_W), _row),
                  pl.BlockSpec((BLK, 2 * KV_W), lambda i: (jnp.maximum(i - 1, 0), 2)),
                  pl.BlockSpec((BLK, 2 * KV_W), lambda i: (i, 2)),
                  pl.BlockSpec((2, N_HEADS, BLK, 2 * BLK), lambda i: (0, 0, 0, 0)), smem],
        out_specs=[pl.BlockSpec((BLK, ATTN_W), _row)],
        out_shape=[jax.ShapeDtypeStruct((S, ATTN_W), F32)],
        scratch_shapes=[],
        args=(qkv, qkv, qkv, bias, sinks), xchg=xchg, cparams=_cparams(),
    )


def _attn_bwd(qkv, y, dy, bias, sinks, xchg):
    S = qkv.shape[0]
    nb = S // BLK

    def body(q_ref, kvp_ref, kvc_ref, y_ref, dy_ref, bias_ref, sinks_ref, dq_ref, dkv_ref, dbias_ref, dsk_ref, carry_ref):
        i = pl.program_id(0)

        @pl.when(i == 0)
        def _():
            dbias_ref[...] = jnp.zeros_like(dbias_ref)
            dsk_ref[...] = jnp.zeros_like(dsk_ref)
            carry_ref[...] = jnp.zeros_like(carry_ref)

        first = jnp.where(i == nb - 1, 1, 0)
        q_heads = _split_heads(q_ref[...].astype(F32) * QK_SCALE, 4)
        k_pads, v_pads = _attn_kv(kvp_ref, kvc_ref)
        y_heads = _split_heads(y_ref[...], 4)
        dy_heads = _split_heads(dy_ref[...], 4)
        heads = range(N_HEADS)
        qs = [q_heads[h].astype(MXU_DTYPE) for h in heads]
        s = [_mm_nt(qs[h], k_pads[h // 4]) + bias_ref[first, h] for h in heads]
        m = [jnp.maximum(jnp.max(s[h], axis=-1, keepdims=True), sinks_ref[h]) for h in heads]
        p = [jnp.exp(s[h] - m[h]) for h in heads]
        esink = [jnp.exp(sinks_ref[h] - m[h]) for h in heads]
        rinv = [1.0 / (jnp.sum(p[h], axis=-1, keepdims=True) + esink[h]) for h in heads]
        t = [dy_heads[h] * rinv[h] for h in heads]
        delta = [jnp.sum(t[h] * y_heads[h], axis=-1, keepdims=True) for h in heads]
        tb = [t[h].astype(MXU_DTYPE) for h in heads]
        dp = [_mm_nt(tb[h], v_pads[h // 4]) for h in heads]
        ds = [p[h] * (dp[h] - delta[h]) for h in heads]
        for h in heads:
            dbias_ref[h] += ds[h]
            dsk_ref[h] -= esink[h] * delta[h]
        dsb = [ds[h].astype(MXU_DTYPE) for h in heads]
        pb = [p[h].astype(MXU_DTYPE) for h in heads]
        dq_heads = [_mm(dsb[h], k_pads[h // 4]) * QK_SCALE for h in heads]
        dk_pads = [_mm_tn(jnp.concatenate(dsb[4 * g:4 * g + 4], axis=0), jnp.concatenate(qs[4 * g:4 * g + 4], axis=0))
                   for g in range(2)]
        dv_pads = [_mm_tn(jnp.concatenate(pb[4 * g:4 * g + 4], axis=0), jnp.concatenate(tb[4 * g:4 * g + 4], axis=0))
                   for g in range(2)]
        dq_ref[...] = _join_heads(dq_heads).astype(dq_ref.dtype)
        dk_prev = _join_pair(dk_pads[0][:BLK], dk_pads[1][:BLK])
        dk_cur = _join_pair(dk_pads[0][BLK:], dk_pads[1][BLK:])
        dv_prev = _join_pair(dv_pads[0][:BLK], dv_pads[1][:BLK])
        dv_cur = _join_pair(dv_pads[0][BLK:], dv_pads[1][BLK:])
        dkv_ref[...] = (jnp.concatenate([dk_cur, dv_cur], axis=1) + carry_ref[...]).astype(dkv_ref.dtype)
        carry_ref[...] = jnp.concatenate([dk_prev, dv_prev], axis=1)

    smem = pl.BlockSpec(memory_space=pltpu.SMEM)
    rev = lambda i: (nb - 1 - i, 0)
    return _hosted_call(
        body, "attn_bwd", nb,
        in_specs=[pl.BlockSpec((BLK, ATTN_W), rev),
                  pl.BlockSpec((BLK, 2 * KV_W), lambda i: (jnp.maximum(nb - 2 - i, 0), 2)),
                  pl.BlockSpec((BLK, 2 * KV_W), lambda i: (nb - 1 - i, 2)),
                  pl.BlockSpec((BLK, ATTN_W), rev), pl.BlockSpec((BLK, ATTN_W), rev),
                  pl.BlockSpec((2, N_HEADS, BLK, 2 * BLK), lambda i: (0, 0, 0, 0)), smem],
        out_specs=[pl.BlockSpec((BLK, ATTN_W), rev), pl.BlockSpec((BLK, 2 * KV_W), rev),
                   pl.BlockSpec((N_HEADS, BLK, 2 * BLK), lambda i: (0, 0, 0)),
                   pl.BlockSpec((N_HEADS, BLK, 1), lambda i: (0, 0, 0))],
        out_shape=[jax.ShapeDtypeStruct((S, ATTN_W), MXU_DTYPE), jax.ShapeDtypeStruct((S, 2 * KV_W), MXU_DTYPE),
                   jax.ShapeDtypeStruct((N_HEADS, BLK, 2 * BLK), F32), jax.ShapeDtypeStruct((N_HEADS, BLK, 1), F32)],
        scratch_shapes=[pltpu.VMEM((BLK, 2 * KV_W), F32)],
        args=(qkv, qkv, qkv, y, dy, bias, sinks), xchg=xchg, cparams=_cparams(),
    )


def _attn_finish(dbias, dsk, buckets):
    def body(db_ref, dsk_ref, bk_ref, drel_ref, dsink_ref):
        bk = bk_ref[...]
        r = lax.broadcasted_iota(jnp.int32, (N_BUCKETS, LANE), 0)
        l = lax.broadcasted_iota(jnp.int32, (N_BUCKETS, LANE), 1)
        row = lax.broadcasted_iota(jnp.int32, (N_HEADS, LANE), 0)
        res = jnp.zeros((N_BUCKETS, LANE), F32)
        dsink = jnp.zeros((N_HEADS, LANE), F32)
        for h in range(N_HEADS):
            db = db_ref[h]
            for b in range(N_BUCKETS):
                v = jnp.sum(jnp.sum(jnp.where(bk == b, db, 0.0), axis=1, keepdims=True), axis=0, keepdims=True)
                res = res + jnp.where((r == b) & (l == h), v, 0.0)
            dsink = dsink + jnp.where(row == h, jnp.sum(dsk_ref[h], axis=0, keepdims=True), 0.0)
        drel_ref[...] = res
        dsink_ref[...] = dsink

    return pl.pallas_call(body, name="attn_finish",
                          out_shape=[jax.ShapeDtypeStruct((N_BUCKETS, LANE), F32),
                                     jax.ShapeDtypeStruct((N_HEADS, LANE), F32)])(dbias, dsk, buckets)


def _ssd_consts():
    r = lax.broadcasted_iota(jnp.int32, (BLK, BLK), 0)
    c = lax.broadcasted_iota(jnp.int32, (BLK, BLK), 1)
    causal = c <= r
    upper = (r <= c).astype(F32)
    last = r == BLK - 1
    head = lax.broadcasted_iota(jnp.int32, (N_HEADS, BLK), 0)
    return causal, upper, last, head


def _ssd_chunk(xs, bg, cg, dt_raw_t, prev, dtb, alog, d_rows, consts):
    causal, upper, last, head = consts
    dt_t = _softplus(dt_raw_t + dtb)
    acs_t = _mm_hi(dt_t * (-jnp.exp(alog)), upper)
    cb = [_mm_nt(cg[g], bg[g]) for g in range(2)]
    heads = range(N_HEADS)
    dt_row = [jnp.sum(jnp.where(head == h, dt_t, 0.0), axis=0, keepdims=True) for h in heads]
    a_row = [jnp.sum(jnp.where(head == h, acs_t, 0.0), axis=0, keepdims=True) for h in heads]
    a_rb = [jnp.broadcast_to(a_row[h], (BLK, BLK)) for h in heads]
    a_b = [a_rb[h].T for h in heads]
    a_last = [jnp.sum(jnp.where(last, a_b[h], 0.0), axis=0, keepdims=True) for h in heads]
    w = [cb[h // 4] * jnp.exp(jnp.where(causal, a_b[h] - a_rb[h], -1e30)) * dt_row[h] for h in heads]
    f_b = [jnp.broadcast_to(dt_row[h] * jnp.exp(a_last[h] - a_row[h]), (BLK, BLK)).T for h in heads]
    y_in = [_mm(w[h], xs[h]) for h in heads]
    y_off = [_mm(cg[h // 4], prev[h]) * jnp.exp(a_b[h]) for h in heads]
    st = [_mm_tn(bg[h // 4], xs[h] * f_b[h]) for h in heads]
    ys = [y_in[h] + y_off[h] + d_rows[h] * xs[h] for h in heads]
    hs = [prev[h] * jnp.exp(a_last[h]) + st[h] for h in heads]
    return tuple(ys), tuple(hs)


def _ssd_chunk_bwd(xs, bg, cg, dt_raw_t, prev, dtb, alog, d_rows, dys, dhs, consts):
    causal, upper, last, head = consts
    heads, groups = range(N_HEADS), range(2)
    lane = _lane_iota((BLK, BLK))
    lane_row = _lane_iota((1, BLK))
    pre_dt = dt_raw_t + dtb
    dt_t = _softplus(pre_dt)
    a_neg = -jnp.exp(alog)
    acs_t = _mm_hi(dt_t * a_neg, upper)
    pick = lambda t, h: jnp.sum(jnp.where(head == h, t, 0.0), axis=0, keepdims=True)
    full_sum = lambda t: jnp.sum(jnp.sum(t, axis=1, keepdims=True), axis=0, keepdims=True)
    dt_row = [pick(dt_t, h) for h in heads]
    a_row = [pick(acs_t, h) for h in heads]
    a_rb = [jnp.broadcast_to(a_row[h], (BLK, BLK)) for h in heads]
    a_b = [a_rb[h].T for h in heads]
    a_last = [jnp.sum(jnp.where(last, a_b[h], 0.0), axis=0, keepdims=True) for h in heads]
    lm = [jnp.exp(jnp.where(causal, a_b[h] - a_rb[h], -1e30)) for h in heads]
    cgb = [cg[g].astype(MXU_DTYPE) for g in groups]
    bgb = [bg[g].astype(MXU_DTYPE) for g in groups]
    cb = [_mm_nt(cgb[g], bgb[g]) for g in groups]
    u = [cb[h // 4] * lm[h] for h in heads]
    w = [(u[h] * dt_row[h]).astype(MXU_DTYPE) for h in heads]
    e_row = [jnp.exp(a_last[h] - a_row[h]) for h in heads]
    f_row = [dt_row[h] * e_row[h] for h in heads]
    f_b = [jnp.broadcast_to(f_row[h], (BLK, BLK)).T for h in heads]
    e_b = [jnp.exp(a_b[h]) for h in heads]
    el = [jnp.exp(a_last[h]) for h in heads]
    xb = [xs[h].astype(MXU_DTYPE) for h in heads]
    dyb = [dys[h].astype(MXU_DTYPE) for h in heads]
    prevb = [prev[h].astype(MXU_DTYPE) for h in heads]
    dstb = [dhs[h].astype(MXU_DTYPE) for h in heads]
    gmat = [_mm(cgb[h // 4], prevb[h]) for h in heads]
    dw = [_mm_nt(dyb[h], xb[h]) for h in heads]
    dg = [dys[h] * e_b[h] for h in heads]
    dgb = [dg[h].astype(MXU_DTYPE) for h in heads]
    dxf = [_mm(bgb[h // 4], dstb[h]) for h in heads]
    xfb = [(xs[h] * f_b[h]).astype(MXU_DTYPE) for h in heads]
    dxs = [_mm_tn(w[h], dyb[h]) + d_rows[h] * dys[h] + f_b[h] * dxf[h] for h in heads]
    dd_rows = [jnp.sum(dys[h] * xs[h], axis=0, keepdims=True) for h in heads]
    dprev = [_mm_tn(cgb[h // 4], dgb[h]) + dhs[h] * el[h] for h in heads]
    dcg_h = [_mm_nt(dgb[h], prevb[h]) for h in heads]
    dbg_h = [_mm_nt(xfb[h], dstb[h]) for h in heads]
    zt = [dw[h] * u[h] for h in heads]
    dseg = [zt[h] * dt_row[h] for h in heads]
    dcb_h = [dw[h] * lm[h] * dt_row[h] for h in heads]
    dcb = [(dcb_h[4 * g] + dcb_h[4 * g + 1] + dcb_h[4 * g + 2] + dcb_h[4 * g + 3]).astype(MXU_DTYPE) for g in groups]
    dcg = [dcg_h[4 * g] + dcg_h[4 * g + 1] + dcg_h[4 * g + 2] + dcg_h[4 * g + 3] + _mm(dcb[g], bgb[g]) for g in groups]
    dbg = [dbg_h[4 * g] + dbg_h[4 * g + 1] + dbg_h[4 * g + 2] + dbg_h[4 * g + 3] + _mm_tn(dcb[g], cgb[g])
           for g in groups]
    r1 = [jnp.sum(dg[h] * gmat[h] + dseg[h], axis=1, keepdims=True) for h in heads]
    r2 = [jnp.sum(dxf[h] * xs[h], axis=1, keepdims=True) for h in heads]
    tt = [jnp.where(lane < HALF, jnp.broadcast_to(r1[h], (BLK, BLK)), jnp.broadcast_to(r2[h], (BLK, BLK))).T
          for h in heads]
    r1_row = [tt[h][0:1, :] for h in heads]
    r2_row = [tt[h][HALF:HALF + 1, :] for h in heads]
    d_el = [full_sum(dhs[h] * prev[h]) for h in heads]
    da_last = [jnp.sum(r2_row[h] * f_row[h], axis=1, keepdims=True) + el[h] * d_el[h] for h in heads]
    da_row = [r1_row[h] - jnp.sum(dseg[h], axis=0, keepdims=True) - r2_row[h] * f_row[h]
              + jnp.where(lane_row == BLK - 1, da_last[h], 0.0) for h in heads]
    ddt_row = [jnp.sum(zt[h], axis=0, keepdims=True) + r2_row[h] * e_row[h] for h in heads]
    da_t = jnp.zeros((N_HEADS, BLK), F32)
    ddt_t = jnp.zeros((N_HEADS, BLK), F32)
    for h in heads:
        da_t = jnp.where(head == h, da_row[h], da_t)
        ddt_t = jnp.where(head == h, ddt_row[h], ddt_t)
    d_dta = _mm_hi(da_t, causal.astype(F32))
    dalog = d_dta * dt_t * a_neg
    draw = (ddt_t + d_dta * a_neg) * jax.nn.sigmoid(pre_dt)
    return dxs, dbg, dcg, draw, dprev, draw, dalog, dd_rows


def _dt_rows(dt_blk):
    return dt_blk.T[:N_HEADS]


def _silu_grad(x):
    s = jax.nn.sigmoid(x)
    return s * (1.0 + x * (1.0 - s))


def _conv_pre(halo, blk, cw_ref, cb_ref):
    ext = jnp.concatenate([halo, blk], axis=0)
    taps = [pltpu.roll(ext, 3 - k, 0)[8:] for k in range(3)] + [blk]
    pre = cb_ref[...] + cw_ref[0:1, :] * taps[0]
    for k in range(1, 4):
        pre = pre + cw_ref[k:k + 1, :] * taps[k]
    return pre, taps


def _ssd_split(pre):
    heads = _split_heads(pre[:, :SSM_W], 4)
    pb = [pre[:, SSM_W + g * D_STATE:SSM_W + (g + 1) * D_STATE] for g in range(2)]
    pc = [pre[:, SSM_W + 2 * D_STATE + g * D_STATE:SSM_W + 2 * D_STATE + (g + 1) * D_STATE] for g in range(2)]
    return heads, pb, pc


def _ssd_fwd(xbc, dt_raw, conv_w, conv_b, dtb_row, alog_row, d_exp, xchg):
    S = xbc.shape[0]
    nc = S // BLK

    def body(xbc_ref, halo_ref, dt_ref, cw_ref, cb_ref, dtb_ref, alog_ref, d_ref, y_ref, prev_ref, state_ref):
        i = pl.program_id(0)

        @pl.when(i == 0)
        def _():
            state_ref[...] = jnp.zeros_like(state_ref)

        halo = halo_ref[...] * jnp.where(i > 0, 1.0, 0.0)
        pre, _ = _conv_pre(halo, xbc_ref[...], cw_ref, cb_ref)
        heads, pb, pc = _ssd_split(_silu(pre))
        prev = [state_ref[h] for h in range(N_HEADS)]
        for h in range(N_HEADS):
            prev_ref[0, h] = prev[h]
        d_rows = [d_ref[h:h + 1, :] for h in range(N_HEADS)]
        ys, hs = _ssd_chunk(heads, pb, pc, _dt_rows(dt_ref[...]), prev, dtb_ref[...], alog_ref[...], d_rows,
                            _ssd_consts())
        for h in range(N_HEADS):
            state_ref[h] = hs[h]
        y_ref[...] = _join_heads(ys)

    vec = pl.BlockSpec((N_HEADS, LANE), _fixed)
    return _hosted_call(
        body, "ssd_fwd", nc,
        in_specs=[pl.BlockSpec((BLK, XBC_W), _row),
                  pl.BlockSpec((8, XBC_W), lambda i: (jnp.maximum(i * (BLK // 8) - 1, 0), 0)),
                  pl.BlockSpec((BLK, LANE), _row),
                  pl.BlockSpec((4, XBC_W), _fixed), pl.BlockSpec((1, XBC_W), _fixed), vec, vec,
                  pl.BlockSpec((N_HEADS, LANE), _fixed)],
        out_specs=[pl.BlockSpec((BLK, SSM_W), _row),
                   pl.BlockSpec((1, N_HEADS, D_STATE, LANE), lambda i: (i, 0, 0, 0))],
        out_shape=[jax.ShapeDtypeStruct((S, SSM_W), F32), jax.ShapeDtypeStruct((nc, N_HEADS, D_STATE, LANE), F32)],
        scratch_shapes=[pltpu.VMEM((N_HEADS, D_STATE, LANE), F32)],
        args=(xbc, xbc, dt_raw, conv_w, conv_b, dtb_row, alog_row, d_exp), xchg=xchg, cparams=_cparams(),
    )


def _ssd_bwd(xbc, dt_raw, prev_states, dy, conv_w, conv_b, dtb_row, alog_row, d_exp, xchg):
    S = xbc.shape[0]
    nc = S // BLK

    def body(xbc_ref, halo_ref, dt_ref, prev_ref, dy_ref, cw_ref, cb_ref, dtb_ref, alog_ref, d_ref,
             dxbc_ref, ddt_ref, dcw_ref, dvec_ref, dd_ref, gstate_ref, ghalo_ref):
        i = pl.program_id(0)
        c = nc - 1 - i

        @pl.when(i == 0)
        def _():
            gstate_ref[...] = jnp.zeros_like(gstate_ref)
            ghalo_ref[...] = jnp.zeros_like(ghalo_ref)
            dcw_ref[...] = jnp.zeros_like(dcw_ref)
            dvec_ref[...] = jnp.zeros_like(dvec_ref)
            dd_ref[...] = jnp.zeros_like(dd_ref)

        halo = halo_ref[...] * jnp.where(c > 0, 1.0, 0.0)
        pre, taps = _conv_pre(halo, xbc_ref[...], cw_ref, cb_ref)
        heads, pb, pc = _ssd_split(_silu(pre))
        prev = [prev_ref[0, h] for h in range(N_HEADS)]
        d_rows = [d_ref[h:h + 1, :] for h in range(N_HEADS)]
        dys = _split_heads(dy_ref[...], 4)
        dhs = [gstate_ref[h] for h in range(N_HEADS)]
        dheads, dpb, dpc, ddt_t, dprev, ddtb, dalog, dd_rows = _ssd_chunk_bwd(
            heads, pb, pc, _dt_rows(dt_ref[...]), prev, dtb_ref[...], alog_ref[...], d_rows, dys, dhs, _ssd_consts())
        for h in range(N_HEADS):
            gstate_ref[h] = dprev[h]
            dd_ref[h:h + 1, :] += dd_rows[h]
        ddt_ref[...] = jnp.concatenate([ddt_t, jnp.zeros((BLK - N_HEADS, BLK), F32)], axis=0).T.astype(ddt_ref.dtype)
        dvec_ref[0:N_HEADS, :] += ddtb
        dvec_ref[N_HEADS:, :] += dalog
        dpre = jnp.concatenate([_join_heads(dheads)] + list(dpb) + list(dpc), axis=1) * _silu_grad(pre)
        zeros8 = jnp.zeros((8, XBC_W), F32)
        dpe = jnp.concatenate([zeros8, dpre, zeros8], axis=0)
        n_ext = 16 + BLK
        dext = cw_ref[3:4, :] * dpe[:8 + BLK]
        dcw_ref[3:4, :] += jnp.sum(dpre * taps[3], axis=0, keepdims=True)
        for k in range(3):
            dext = dext + cw_ref[k:k + 1, :] * pltpu.roll(dpe, n_ext - (3 - k), 0)[:8 + BLK]
            dcw_ref[k:k + 1, :] += jnp.sum(dpre * taps[k], axis=0, keepdims=True)
        dcw_ref[4:5, :] += jnp.sum(dpre, axis=0, keepdims=True)
        dxbc_ref[...] = jnp.concatenate([dext[8:BLK], dext[BLK:] + ghalo_ref[...]], axis=0).astype(dxbc_ref.dtype)
        ghalo_ref[...] = dext[:8, :]

    vec = pl.BlockSpec((N_HEADS, LANE), _fixed)
    rev = lambda i: (nc - 1 - i, 0)
    return _hosted_call(
        body, "ssd_bwd", nc,
        in_specs=[pl.BlockSpec((BLK, XBC_W), rev),
                  pl.BlockSpec((8, XBC_W), lambda i: (jnp.maximum((nc - 1 - i) * (BLK // 8) - 1, 0), 0)),
                  pl.BlockSpec((BLK, LANE), rev),
                  pl.BlockSpec((1, N_HEADS, D_STATE, LANE), lambda i: (nc - 1 - i, 0, 0, 0)),
                  pl.BlockSpec((BLK, SSM_W), rev),
                  pl.BlockSpec((4, XBC_W), _fixed), pl.BlockSpec((1, XBC_W), _fixed), vec, vec,
                  pl.BlockSpec((N_HEADS, LANE), _fixed)],
        out_specs=[pl.BlockSpec((BLK, XBC_W), rev), pl.BlockSpec((BLK, LANE), rev),
                   pl.BlockSpec((8, XBC_W), _fixed), pl.BlockSpec((2 * N_HEADS, LANE), _fixed),
                   pl.BlockSpec((N_HEADS, LANE), _fixed)],
        out_shape=[jax.ShapeDtypeStruct((S, XBC_W), MXU_DTYPE), jax.ShapeDtypeStruct((S, LANE), MXU_DTYPE),
                   jax.ShapeDtypeStruct((8, XBC_W), F32), jax.ShapeDtypeStruct((2 * N_HEADS, LANE), F32),
                   jax.ShapeDtypeStruct((N_HEADS, LANE), F32)],
        scratch_shapes=[pltpu.VMEM((N_HEADS, D_STATE, LANE), F32), pltpu.VMEM((8, XBC_W), F32)],
        args=(xbc, xbc, dt_raw, prev_states, dy, conv_w, conv_b, dtb_row, alog_row, d_exp), xchg=xchg,
        cparams=_cparams(VMEM_BIG),
    )


def _adamw_math(w, g, m, v):
    m = ADAM_B1 * m + (1.0 - ADAM_B1) * g
    v = ADAM_B2 * v + (1.0 - ADAM_B2) * jnp.square(g)
    m_hat = m / (1.0 - ADAM_B1 ** ADAM_STEP)
    v_hat = v / (1.0 - ADAM_B2 ** ADAM_STEP)
    delta = -ADAM_LR * (m_hat / (jnp.sqrt(v_hat) + ADAM_EPS) + ADAM_WD * w)
    return delta, m, v


def _reduce_adamw(parts, w, m, v, name):
    R, C = w.shape

    def body(p_ref, w_ref, m_ref, v_ref, g_ref, d_ref, nm_ref, nv_ref):
        g = p_ref[0].astype(F32)
        for i in range(1, N_DEV):
            g = g + p_ref[i].astype(F32)
        d, nm, nv = _adamw_math(w_ref[...], g, m_ref[...], v_ref[...])
        g_ref[...] = g
        d_ref[...] = d
        nm_ref[...] = nm
        nv_ref[...] = nv

    if R % 16 == 0:
        tr = max(t for t in range(16, 257, 16) if R % t == 0)
        n, blk, pblk = R // tr, pl.BlockSpec((tr, C), _row), pl.BlockSpec((N_DEV, tr, C), lambda i: (0, i, 0))
    else:
        tl = 256
        n, blk, pblk = C // tl, pl.BlockSpec((R, tl), lambda i: (0, i)), pl.BlockSpec((N_DEV, R, tl),
                                                                                      lambda i: (0, 0, i))
    return pl.pallas_call(
        body, name=name, grid=(n,), in_specs=[pblk, blk, blk, blk],
        out_specs=[blk] * 4, out_shape=[jax.ShapeDtypeStruct((R, C), F32)] * 4,
    )(parts, w, m, v)


def _reduce_adamw_hosting(parts_list, wmv_list, name, xchg):
    n_arr = len(parts_list)
    C = wmv_list[0][0].shape[1]
    tl = 256

    def body(*refs):
        p_refs, wmv_refs, o_refs = refs[:n_arr], refs[n_arr:4 * n_arr], refs[4 * n_arr:]
        for k in range(n_arr):
            g = p_refs[k][0].astype(F32)
            for i in range(1, N_DEV):
                g = g + p_refs[k][i].astype(F32)
            w_ref, m_ref, v_ref = wmv_refs[3 * k:3 * k + 3]
            d, nm, nv = _adamw_math(w_ref[...], g, m_ref[...], v_ref[...])
            for o, val in zip(o_refs[4 * k:4 * k + 4], (g, d, nm, nv)):
                o[...] = val

    in_specs = [pl.BlockSpec((N_DEV, w.shape[0], tl), lambda i: (0, 0, i)) for w, _, _ in wmv_list]
    in_specs += [pl.BlockSpec((w.shape[0], tl), lambda i: (0, i)) for w, _, _ in wmv_list for _ in range(3)]
    out_specs = [pl.BlockSpec((w.shape[0], tl), lambda i: (0, i)) for w, _, _ in wmv_list for _ in range(4)]
    out_shape = [jax.ShapeDtypeStruct(w.shape, F32) for w, _, _ in wmv_list for _ in range(4)]
    args = list(parts_list) + [a for wmv in wmv_list for a in wmv]
    outs, x_out = _hosted_call(body, name, C // tl, in_specs, out_specs, out_shape, [], args, xchg,
                               _cparams(VMEM_BIG))
    return [outs[4 * k:4 * k + 4] for k in range(n_arr)], x_out


_SMALL_NAMES = ("ada_b", "norm1", "conv_w", "conv_b", "dt_bias", "A_log", "D_skip", "sinks", "attn_out_norm",
                "ssm_out_norm", "norm2", "rel_bias", "final_norm")
N_MOD = 6 * D_MODEL


def _mod_row(a0, a1, a2):
    return jnp.concatenate([a0[2:3], a0[1:2], a1[0:1], a2[2:3], a2[1:2], a2[3:4]], axis=1)


def _small_update(gathered, params):
    n_g = len(gathered)
    flat = [a for name in _SMALL_NAMES for a in params[name]]

    def body(*refs):
        a0_ref, a1_ref, a2_ref, cw_ref, dv_ref, dd_ref, ds_ref, dr_ref, c_ref = refs[:n_g]
        wmv = refs[n_g:n_g + len(flat)]
        outs = refs[n_g + len(flat):]

        def total(ref):
            t = ref[0]
            for i in range(1, N_DEV):
                t = t + ref[i]
            return t

        t0, t1, t2, tcw, tdv, tdd, tds, tdr = [total(r) for r in (a0_ref, a1_ref, a2_ref, cw_ref, dv_ref, dd_ref,
                                                                   ds_ref, dr_ref)]
        r8 = lax.broadcasted_iota(jnp.int32, (N_HEADS, LANE), 0)
        l8 = lax.broadcasted_iota(jnp.int32, (N_HEADS, LANE), 1)

        def diag_row(t):
            return jnp.sum(jnp.where(r8 == l8, t, 0.0), axis=0, keepdims=True)[:, :N_HEADS]

        def lane_sums(t):
            return diag_row(jnp.broadcast_to(jnp.sum(t, axis=1, keepdims=True), (N_HEADS, LANE)))

        me = _lin(_my_pos())
        n_cw = XBC_W // N_DEV
        cw_mine = jnp.zeros((4, n_cw), F32)
        for j in range(N_DEV):
            cw_mine = cw_mine + tcw[0:4, j * n_cw:(j + 1) * n_cw] * jnp.where(me == j, 1.0, 0.0)
        grads = {
            "ada_b": _mod_row(t0, t1, t2), "norm1": t0[0:1], "conv_w": cw_mine, "conv_b": tcw[4:5],
            "dt_bias": lane_sums(tdv[:N_HEADS]), "A_log": lane_sums(tdv[N_HEADS:]), "D_skip": lane_sums(tdd),
            "sinks": diag_row(tds), "attn_out_norm": t1[1:2, :ATTN_W], "ssm_out_norm": t1[1:2, ATTN_W:],
            "norm2": t2[0:1], "rel_bias": tdr[:, :N_HEADS], "final_norm": t2[4:5],
        }
        for k, name in enumerate(_SMALL_NAMES):
            w_ref, m_ref, v_ref = wmv[3 * k:3 * k + 3]
            g = grads[name]
            d, nm, nv = _adamw_math(w_ref[...], g, m_ref[...], v_ref[...])
            for o, val in zip(outs[4 * k:4 * k + 4], (g, d, nm, nv)):
                o[...] = val
        loss_ref, call_ref, dmod_ref = outs[4 * len(_SMALL_NAMES):]
        loss_ref[...] = t2[5:6, 0:1]
        call_ref[...] = jnp.concatenate([c_ref[i] for i in range(N_DEV)], axis=0)
        dmod_ref[...] = jnp.concatenate([_mod_row(a0_ref[i], a1_ref[i], a2_ref[i]) for i in range(N_DEV)], axis=0)

    out_shape = [jax.ShapeDtypeStruct(params[name][0].shape, F32) for name in _SMALL_NAMES for _ in range(4)]
    out_shape += [jax.ShapeDtypeStruct((1, 1), F32), jax.ShapeDtypeStruct((N_DEV, D_MODEL), F32),
                  jax.ShapeDtypeStruct((N_DEV, N_MOD), F32)]
    res = pl.pallas_call(body, name="small_update", out_shape=out_shape)(*gathered, *flat)
    upd = {name: res[4 * k:4 * k + 4] for k, name in enumerate(_SMALL_NAMES)}
    loss, c_all, dmod_all = res[4 * len(_SMALL_NAMES):]
    return upd, loss, c_all, dmod_all


def _ada_w_update(c_all, dmod_all, w, m, v):
    chunk = w.shape[1]

    def body(c_ref, dm_ref, w_ref, m_ref, v_ref, g_ref, d_ref, nm_ref, nv_ref):
        me = _lin(_my_pos())
        dm = jnp.zeros((N_DEV, chunk), F32)
        for j in range(N_DEV):
            dm = dm + dm_ref[:, j * chunk:(j + 1) * chunk] * jnp.where(me == j, 1.0, 0.0)
        g = lax.dot_general(_silu(c_ref[...]), dm, (((0,), (0,)), ((), ())), precision=HI,
                            preferred_element_type=F32)
        d, nm, nv = _adamw_math(w_ref[...], g, m_ref[...], v_ref[...])
        g_ref[...] = g
        d_ref[...] = d
        nm_ref[...] = nm
        nv_ref[...] = nv

    tr = 256
    blk = pl.BlockSpec((tr, chunk), _row)
    return pl.pallas_call(
        body, name="ada_w_update", grid=(w.shape[0] // tr,),
        in_specs=[pl.BlockSpec((N_DEV, tr), lambda i: (0, i)), pl.BlockSpec(dmod_all.shape, _fixed), blk, blk, blk],
        out_specs=[blk] * 4, out_shape=[jax.ShapeDtypeStruct(w.shape, F32)] * 4,
    )(c_all, dmod_all, w, m, v)


def _local_step(x, tgt, c, mod, w_in, conv_w, w_o_mine, w_gu_mine, w_d_mine, p):
    S = x.shape[0]
    tm = min(512, S)
    tmm = min(256, S)
    tw = min(2048, S)
    shift1, scale1, gate1, shift2, scale2, gate2 = [mod[i:i + 1] for i in range(6)]
    buckets = jnp.asarray(_t5_bucket_table())
    per_head = lambda a: jnp.broadcast_to(a.reshape(N_HEADS, 1), (N_HEADS, LANE))
    dtb_row, alog_row, d_exp = per_head(p["dt_bias"]), per_head(p["A_log"]), per_head(p["D_skip"])
    sinks = p["sinks"].reshape(N_HEADS)

    d_cut, gu_cut = WD_CUT, WGU_CUTS
    (qkv, z, xbc, dt_raw), (g_d_a,) = _in_proj_fwd(x, p["norm1"], scale1, shift1, w_in, tm,
                                                   ([w_d_mine[:d_cut]], "two-level"))
    bias = _attn_bias(buckets, p["rel_bias"])
    (ya,), (g_gu_a,) = _attn_fwd(qkv, bias, sinks, ([w_gu_mine[:gu_cut[0]]], "two-level"))
    (ys, prev_states), (g_gu_b, g_o) = _ssd_fwd(xbc, dt_raw, conv_w, p["conv_b"], dtb_row, alog_row, d_exp,
                                                ([w_gu_mine[gu_cut[0]:gu_cut[1]], w_o_mine], "two-level"))
    w_o = g_o.reshape(D_MODEL, D_MODEL)
    x1, (g_gu_c, g_d_b) = _out_proj_fwd(x, ya, ys, z, p["attn_out_norm"], p["ssm_out_norm"], gate1, w_o, tm,
                                        ([w_gu_mine[gu_cut[1]:], w_d_mine[d_cut:]], "two-level"))
    dx1, h2, dgu, act, dmlp, acc2 = _mlp_loss(x1, tgt, p["norm2"], scale2, shift2, gate2, p["final_norm"],
                                              (g_gu_a, g_gu_b, g_gu_c), (g_d_a, g_d_b), tmm)
    g_w_gu = _wgrad(dgu, h2, 2 * D_FF // 4, tw, "wgrad_gate_up")
    g_w_d = _wgrad(act, dmlp, D_FF // 2, tw, "wgrad_down")
    dya, dys, dz, u, dmix, acc1 = _out_proj_bwd(dx1, ya, ys, z, p["attn_out_norm"], p["ssm_out_norm"], gate1, w_o, tm)
    g_w_o = _wgrad(u, dmix, D_MODEL, tw, "wgrad_out")
    (dq, dkv, dbias, dsk), (r_d,) = _attn_bwd(qkv, ya, dya, bias, sinks,
                                              ([g_w_d.reshape(N_DEV, D_FF // N_DEV, D_MODEL)], True))
    drel, dsink = _attn_finish(dbias, dsk, buckets)
    (dxbc, ddt, dcw, dvec, dd), (r_gu, r_o) = _ssd_bwd(
        xbc, dt_raw, prev_states, dys, conv_w, p["conv_b"], dtb_row, alog_row, d_exp,
        ([g_w_gu.reshape(N_DEV, 2 * D_FF // N_DEV, D_MODEL), g_w_o.reshape(N_DEV, D_MODEL // N_DEV, D_MODEL)], True))
    gx, h1, acc0 = _in_proj_bwd(x, dx1, dq, dkv, dz, dxbc, ddt, p["norm1"], scale1, shift1, w_in, tm)
    dproj = (dq, dkv, dz, dxbc, ddt)
    half = D_MODEL // 2
    slots = lambda g: g[:IN_W].reshape(N_DEV, IN_W // N_DEV, half)
    g_in_a, gathered = _wgrad(dproj, h1, IN_PAD, tw, "wgrad_in_a",
                              ([acc0, acc1, acc2, dcw, dvec, dd, dsink, drel, c], False), g_cols=(half, 0))
    g_in_b, (r_in_a,) = _wgrad(dproj, h1, IN_PAD, tw, "wgrad_in_b", ([slots(g_in_a)], True), g_cols=(half, 1))
    return gx, (r_in_a, slots(g_in_b)), (r_o, r_gu, r_d), gathered


def kernel(x, c, ada_w, ada_b, norm1, w_in, conv_w, conv_b, dt_bias, A_log, D_skip, sinks, attn_out_norm, ssm_out_norm, w_o, norm2, w_gate_up, w_down, rel_bias, final_norm, loss_target, m_ada_w, m_ada_b, m_norm1, m_w_in, m_conv_w, m_conv_b, m_dt_bias, m_A_log, m_D_skip, m_sinks, m_attn_out_norm, m_ssm_out_norm, m_w_o, m_norm2, m_w_gate_up, m_w_down, m_rel_bias, m_final_norm, v_ada_w, v_ada_b, v_norm1, v_w_in, v_conv_w, v_conv_b, v_dt_bias, v_A_log, v_D_skip, v_sinks, v_attn_out_norm, v_ssm_out_norm, v_w_o, v_norm2, v_w_gate_up, v_w_down, v_rel_bias, v_final_norm):
    two_d = lambda a: a if a.ndim == 2 else a.reshape(-1, a.shape[-1])
    small_params = dict(
        ada_b=(ada_b, m_ada_b, v_ada_b), norm1=(norm1, m_norm1, v_norm1), conv_w=(conv_w, m_conv_w, v_conv_w),
        conv_b=(conv_b, m_conv_b, v_conv_b), dt_bias=(dt_bias, m_dt_bias, v_dt_bias), A_log=(A_log, m_A_log, v_A_log),
        D_skip=(D_skip, m_D_skip, v_D_skip), sinks=(sinks, m_sinks, v_sinks),
        attn_out_norm=(attn_out_norm, m_attn_out_norm, v_attn_out_norm),
        ssm_out_norm=(ssm_out_norm, m_ssm_out_norm, v_ssm_out_norm), norm2=(norm2, m_norm2, v_norm2),
        rel_bias=(rel_bias, m_rel_bias, v_rel_bias), final_norm=(final_norm, m_final_norm, v_final_norm))
    small_params = {k: tuple(two_d(a) for a in v) for k, v in small_params.items()}
    S = x.shape[1]
    xs, tgt = x.reshape(S, D_MODEL), loss_target.reshape(S, D_MODEL)
    ada_w2 = ada_w[0]
    chunk = ada_w2.shape[1]
    t_in = [jnp.transpose(a[0]) for a in (w_in, m_w_in, v_w_in)]
    t_gu = [jnp.transpose(a[0]) for a in (w_gate_up, m_w_gate_up, v_w_gate_up)]

    mod, (g_in, g_cw) = _mod_and_gather(c, ada_w2, ada_b.reshape(N_DEV, chunk), [t_in[0].astype(WIRE_DTYPE), conv_w[0]])
    mod = mod.reshape(6, D_MODEL)
    w_in_full = jnp.pad(g_in.reshape(IN_W, D_MODEL), ((0, IN_PAD - IN_W), (0, 0)))
    conv_w_full = jnp.transpose(g_cw, (1, 0, 2)).reshape(4, XBC_W)

    p = {k: v[0] for k, v in small_params.items()}
    gx, (r_in_a, gw_in_b), (r_o, r_gu, r_d), gathered = _local_step(
        xs, tgt, c, mod, w_in_full, conv_w_full, w_o[0].astype(WIRE_DTYPE), t_gu[0].astype(WIRE_DTYPE),
        w_down[0].astype(WIRE_DTYPE), p)

    (u_gu, u_d, u_o), (r_in_b,) = _reduce_adamw_hosting(
        [r_gu, r_d, r_o], [tuple(t_gu), (w_down[0], m_w_down[0], v_w_down[0]), (w_o[0], m_w_o[0], v_w_o[0])],
        "adamw_big", ([gw_in_b], True))
    r_in = jnp.concatenate([r_in_a, r_in_b], axis=2)

    small, loss, c_all, dmod_all = _small_update(gathered, small_params)

    big = {
        "ada_w": _ada_w_update(c_all, dmod_all, ada_w2, m_ada_w[0], v_ada_w[0]),
        "w_in": [jnp.transpose(a) for a in _reduce_adamw(r_in, *t_in, "adamw_w_in")],
        "w_o": u_o,
        "w_gate_up": [jnp.transpose(a) for a in u_gu],
        "w_down": u_d,
    }
    big.update(small)

    order = ['ada_w', 'ada_b', 'norm1', 'w_in', 'conv_w', 'conv_b', 'dt_bias', 'A_log', 'D_skip', 'sinks',
             'attn_out_norm', 'ssm_out_norm', 'w_o', 'norm2', 'w_gate_up', 'w_down', 'rel_bias', 'final_norm']
    shapes = dict(ada_w=ada_w.shape, ada_b=ada_b.shape, norm1=norm1.shape, w_in=w_in.shape, conv_w=conv_w.shape,
                  conv_b=conv_b.shape, dt_bias=dt_bias.shape, A_log=A_log.shape, D_skip=D_skip.shape,
                  sinks=sinks.shape, attn_out_norm=attn_out_norm.shape, ssm_out_norm=ssm_out_norm.shape,
                  w_o=w_o.shape, norm2=norm2.shape, w_gate_up=w_gate_up.shape, w_down=w_down.shape,
                  rel_bias=rel_bias.shape, final_norm=final_norm.shape)
    outs = [[], [], [], []]
    for name in order:
        for kind in range(4):
            outs[kind].append(big[name][kind].reshape(shapes[name]))
    return (loss.reshape(()), gx.reshape(x.shape), *outs[0], *outs[1], *outs[2], *outs[3])
```

```python
import functools

import numpy as np
import jax
import jax.numpy as jnp
from jax import lax
from jax.experimental import pallas as pl
from jax.experimental.pallas import tpu as pltpu

F32 = jnp.float32
MXU_DTYPE = jnp.bfloat16
WIRE_DTYPE = jnp.bfloat16
HI = lax.Precision.HIGHEST
MESH = pl.DeviceIdType.MESH
N_DEV = 8

D_MODEL = 1024
ATTN_W = 512
KV_W = 128
SSM_W = 512
XBC_W = 1024
N_HEADS = 8
D_STATE = 128
D_FF = 2816
IN_W = 2312
IN_PAD = 2432
BLK = 128
N_BUCKETS = 32
EPS = 1e-6
LANE = 128
HALF = 64

ADAM_LR, ADAM_B1, ADAM_B2, ADAM_EPS, ADAM_WD, ADAM_STEP = 0.001, 0.9, 0.999, 1e-08, 0.01, 10

VMEM_BIG = 56 * 1024 * 1024
WD_CUT = 256
WGU_CUTS = (304, 608)


def _cparams(vmem=None):
    if vmem is None:
        return pltpu.CompilerParams()
    return pltpu.CompilerParams(vmem_limit_bytes=vmem)


def _mm(a, b):
    return jnp.dot(a.astype(MXU_DTYPE), b.astype(MXU_DTYPE), preferred_element_type=F32)


def _mm_nt(a, b):
    return lax.dot_general(a.astype(MXU_DTYPE), b.astype(MXU_DTYPE), (((1,), (1,)), ((), ())),
                           preferred_element_type=F32)


def _mm_tn(a, b):
    return lax.dot_general(a.astype(MXU_DTYPE), b.astype(MXU_DTYPE), (((0,), (0,)), ((), ())),
                           preferred_element_type=F32)


def _mm_hi(a, b):
    return jnp.dot(a, b, precision=HI, preferred_element_type=F32)


def _silu(x):
    return x * jax.nn.sigmoid(x)


def _softplus(x):
    return jnp.maximum(x, 0.0) + jnp.log1p(jnp.exp(-jnp.abs(x)))


def _rms(x, g, n):
    return x * lax.rsqrt(jnp.sum(x * x, axis=-1, keepdims=True) * (1.0 / n) + EPS) * g


def _modnorm(x, g, scale, shift):
    return _rms(x, g, x.shape[-1]) * (1.0 + scale) + shift


def _modnorm_parts(x):
    r = lax.rsqrt(jnp.sum(x * x, axis=-1, keepdims=True) * (1.0 / x.shape[-1]) + EPS)
    return r, x * r


def _modnorm_bwd(r, xhat, g, scale, dy):
    dyg = dy * (g * (1.0 + scale))
    c = jnp.sum(dyg * xhat, axis=-1, keepdims=True) * (1.0 / xhat.shape[-1])
    dx = r * (dyg - xhat * c)
    ct = jnp.sum(dy * xhat, axis=0, keepdims=True)
    return dx, ct * (1.0 + scale), ct * g, jnp.sum(dy, axis=0, keepdims=True)


def _lane_iota(shape):
    return lax.broadcasted_iota(jnp.int32, shape, len(shape) - 1)


def _split_pair(t):
    lane = _lane_iota(t.shape)
    lo = jnp.where(lane < HALF, t, 0.0)
    hi = pltpu.roll(jnp.where(lane >= HALF, t, 0.0), HALF, 1)
    return lo, hi


def _join_pair(lo, hi):
    lane = _lane_iota(lo.shape)
    return jnp.where(lane < HALF, lo, pltpu.roll(hi, HALF, 1))


def _split_heads(t, n_pairs):
    out = []
    for p in range(n_pairs):
        out.extend(_split_pair(t[:, p * LANE:(p + 1) * LANE]))
    return out


def _join_heads(hs):
    return jnp.concatenate([_join_pair(hs[2 * p], hs[2 * p + 1]) for p in range(len(hs) // 2)], axis=1)


def _t5_bucket_table():
    dist = np.arange(BLK)[:, None] + BLK - np.arange(2 * BLK)[None, :]
    n = np.maximum(dist, 0)
    max_exact = N_BUCKETS // 2
    large = max_exact + (np.log(np.maximum(n, 1) / max_exact) / np.log(128 / max_exact)
                         * (N_BUCKETS - max_exact)).astype(np.int32)
    large = np.minimum(large, N_BUCKETS - 1)
    return np.where(n < max_exact, n, large).astype(np.int32)


def _my_pos():
    return lax.axis_index("x"), lax.axis_index("y"), lax.axis_index("c")


def _peer(k):
    x, y, c = _my_pos()
    return (1 - x if k & 4 else x, 1 - y if k & 2 else y, 1 - c if k & 1 else c)


def _lin(pos):
    return 4 * pos[0] + 2 * pos[1] + pos[2]


def _xchg_copies(ins, outs, sems, scatter):
    local_sem, send_sem, recv_sem = sems
    me = _lin(_my_pos())
    local, remote = [], []
    for a in range(len(ins)):
        src = ins[a].at[me] if scatter else ins[a]
        local.append(pltpu.make_async_copy(src, outs[a].at[me], local_sem.at[a]))
    for k in range(1, N_DEV):
        peer = _peer(k)
        for a in range(len(ins)):
            src = ins[a].at[_lin(peer)] if scatter else ins[a]
            remote.append(pltpu.make_async_remote_copy(src, outs[a].at[me], send_sem.at[a, k - 1],
                                                       recv_sem.at[a, k - 1], device_id=peer, device_id_type=MESH))
    return local, remote


def _xchg_start(ins, outs, sems, scatter):
    local, remote = _xchg_copies(ins, outs, sems, scatter)
    for cp in local + remote:
        cp.start()


def _xchg_wait(ins, outs, sems, scatter):
    local, remote = _xchg_copies(ins, outs, sems, scatter)
    for cp in local:
        cp.wait()
    for cp in remote:
        cp.wait_send()
        cp.wait_recv()


def _xchg_shapes(arrs, scatter):
    n = len(arrs)
    if scatter:
        out_shape = [jax.ShapeDtypeStruct(a.shape, a.dtype) for a in arrs]
    else:
        out_shape = [jax.ShapeDtypeStruct((N_DEV,) + a.shape, a.dtype) for a in arrs]
    sems = [pltpu.SemaphoreType.DMA((n,)), pltpu.SemaphoreType.DMA((n, N_DEV - 1)),
            pltpu.SemaphoreType.DMA((n, N_DEV - 1))]
    return out_shape, sems


_CHIPS = (2, 4, 6)


def _g2_sems(n):
    dma = pltpu.SemaphoreType.DMA
    return [dma((n,)), dma((n, N_DEV)), dma((n, N_DEV)), dma((n, len(_CHIPS))), dma((n, len(_CHIPS)))]


class _TwoLevelGather:
    def __init__(self, ins, outs, sems):
        self.ins, self.outs = ins, outs
        self.local_sem, self.send_sem, self.recv_sem, self.fsend_sem, self.frecv_sem = sems
        self.n = len(ins)

    def _direct(self, a, k):
        return pltpu.make_async_remote_copy(self.ins[a], self.outs[a].at[_lin(_my_pos())], self.send_sem.at[a, k],
                                            self.recv_sem.at[a, k], device_id=_peer(k), device_id_type=MESH)

    def _handed_on(self, a, j, origin):
        slot = self.outs[a].at[origin]
        return pltpu.make_async_remote_copy(slot, slot, self.fsend_sem.at[a, j], self.frecv_sem.at[a, j],
                                            device_id=_peer(1), device_id_type=MESH)

    def _local(self, a):
        return pltpu.make_async_copy(self.ins[a], self.outs[a].at[_lin(_my_pos())], self.local_sem.at[a])

    def start(self):
        for a in range(self.n):
            self._local(a).start()
        for k in (1,) + _CHIPS:
            for a in range(self.n):
                self._direct(a, k).start()

    def forward(self):
        for j, k in enumerate(_CHIPS):
            for a in range(self.n):
                self._direct(a, k).wait_recv()
                self._handed_on(a, j, _lin(_peer(k))).start()

    def finish(self):
        for a in range(self.n):
            self._direct(a, 1).wait_recv()
            for j, k in enumerate(_CHIPS):
                self._handed_on(a, j, _lin(_peer(k ^ 1))).wait_recv()
            self._local(a).wait()
            for k in (1,) + _CHIPS:
                self._direct(a, k).wait_send()
            for j, k in enumerate(_CHIPS):
                self._handed_on(a, j, _lin(_peer(k))).wait_send()


def _mod_and_gather(c, ada_w, ada_b8, arrs):
    n = len(arrs)
    chunk = ada_w.shape[1]
    out_shape = [jax.ShapeDtypeStruct((N_DEV, 1, chunk), F32)]
    out_shape += [jax.ShapeDtypeStruct((N_DEV,) + a.shape, a.dtype) for a in arrs]

    def modulation(c_ref, w_ref, b_ref, out_ref, cbuf, part, s1, r1, s2, r2):
        me = _lin(_my_pos())
        first = []
        for k in range(1, N_DEV):
            cp = pltpu.make_async_remote_copy(c_ref, cbuf.at[me], s1.at[k - 1], r1.at[k - 1],
                                              device_id=_peer(k), device_id_type=MESH)
            cp.start()
            first.append(cp)
        cbuf[me] = c_ref[...]
        for cp in first:
            cp.wait_send()
            cp.wait_recv()
        cond = _silu(jnp.concatenate([cbuf[i] for i in range(N_DEV)], axis=0))
        mod = _mm_hi(cond, w_ref[...]) + b_ref[pl.ds(me, 1), :]
        for j in range(N_DEV):
            part[j] = mod[j:j + 1, :]
        second = []
        for k in range(1, N_DEV):
            peer = _peer(k)
            cp = pltpu.make_async_remote_copy(part.at[_lin(peer)], out_ref.at[me], s2.at[k - 1], r2.at[k - 1],
                                              device_id=peer, device_id_type=MESH)
            cp.start()
            second.append(cp)
        out_ref[me] = part[me]
        for cp in second:
            cp.wait_send()
            cp.wait_recv()

    def body(*refs):
        c_ref, w_ref, b_ref = refs[:3]
        ins = refs[3:3 + n]
        mod_ref = refs[3 + n]
        outs = refs[4 + n:4 + 2 * n]
        cbuf, part, s1, r1, s2, r2 = refs[4 + 2 * n:10 + 2 * n]
        gather = _TwoLevelGather(ins, outs, refs[10 + 2 * n:])
        gather.start()
        modulation(c_ref, w_ref, b_ref, mod_ref, cbuf, part, s1, r1, s2, r2)
        gather.forward()
        gather.finish()

    hbm = pl.BlockSpec(memory_space=pltpu.HBM)
    vm = pl.BlockSpec(memory_space=pltpu.VMEM)
    dma = pltpu.SemaphoreType.DMA
    res = pl.pallas_call(
        body, name="mod_and_gather", out_shape=out_shape, in_specs=[vm, vm, vm] + [hbm] * n,
        out_specs=[vm] + [hbm] * n,
        scratch_shapes=[pltpu.VMEM((N_DEV, 1, D_MODEL), F32), pltpu.VMEM((N_DEV, 1, chunk), F32)]
        + [dma((N_DEV - 1,))] * 4 + _g2_sems(n),
    )(c, ada_w, ada_b8, *arrs)
    return res[0], res[1:]


def _hosted_call(body, name, grid, in_specs, out_specs, out_shape, scratch_shapes, args, xchg, cparams):
    arrs, scatter = xchg
    grid = (grid,) if isinstance(grid, int) else tuple(grid)
    n, n_in, n_out, n_scr = len(arrs), len(in_specs), len(out_specs), len(scratch_shapes)
    x_shape, x_sems = _xchg_shapes(arrs, scatter)
    n_steps = int(np.prod(grid))

    def hosted(*refs):
        ins, refs = refs[:n_in], refs[n_in:]
        x_in, refs = refs[:n], refs[n:]
        outs, refs = refs[:n_out], refs[n_out:]
        x_out, refs = refs[:n], refs[n:]
        scr, sems = refs[:n_scr], refs[n_scr:]
        step = pl.program_id(0)
        for d in range(1, len(grid)):
            step = step * grid[d] + pl.program_id(d)

        @pl.when(step == 0)
        def _():
            _xchg_start(x_in, x_out, sems, scatter)

        body(*ins, *outs, *scr)

        @pl.when(step == n_steps - 1)
        def _():
            _xchg_wait(x_in, x_out, sems, scatter)

    hbm = pl.BlockSpec(memory_space=pltpu.HBM)
    res = pl.pallas_call(
        hosted, name=name, grid=grid, in_specs=list(in_specs) + [hbm] * n,
        out_specs=list(out_specs) + [hbm] * n, out_shape=list(out_shape) + x_shape,
        scratch_shapes=list(scratch_shapes) + x_sems, compiler_params=cparams,
    )(*args, *arrs)
    return res[:n_out], res[n_out:]


def _row(i):
    return (i, 0)


def _fixed(i):
    return (0, 0)


def _in_proj_fwd(x, norm1, scale1, shift1, w_in, tm, xchg):
    S = x.shape[0]

    def body(x_ref, n_ref, sc_ref, sh_ref, w_ref, qkv_ref, z_ref, xbc_ref, dt_ref):
        h = _modnorm(x_ref[...], n_ref[...], sc_ref[...], sh_ref[...])
        p = _mm_nt(h, w_ref[...])
        qkv_ref[...] = p[:, :768].astype(qkv_ref.dtype)
        z_ref[...] = p[:, 768:1280]
        xbc_ref[...] = p[:, 1280:2304]
        dt_ref[...] = p[:, 2304:IN_PAD]

    vec = pl.BlockSpec((1, D_MODEL), _fixed)
    return _hosted_call(
        body, "in_proj_fwd", S // tm,
        in_specs=[pl.BlockSpec((tm, D_MODEL), _row), vec, vec, vec, pl.BlockSpec((IN_PAD, D_MODEL), _fixed)],
        out_specs=[pl.BlockSpec((tm, 768), _row), pl.BlockSpec((tm, SSM_W), _row),
                   pl.BlockSpec((tm, XBC_W), _row), pl.BlockSpec((tm, LANE), _row)],
        out_shape=[jax.ShapeDtypeStruct((S, 768), MXU_DTYPE), jax.ShapeDtypeStruct((S, SSM_W), F32),
                   jax.ShapeDtypeStruct((S, XBC_W), F32), jax.ShapeDtypeStruct((S, LANE), F32)],
        scratch_shapes=[], args=(x, norm1, scale1, shift1, w_in), xchg=xchg, cparams=_cparams(VMEM_BIG),
    )


def _in_proj_bwd(x, dx1, dq, dkv, dz, dxbc, ddt, norm1, scale1, shift1, w_in, tm):
    S = x.shape[0]

    def body(x_ref, dx1_ref, dq_ref, dkv_ref, dz_ref, dxbc_ref, ddt_ref, n_ref, sc_ref, sh_ref, w_ref,
             gx_ref, h_ref, acc_ref):
        @pl.when(pl.program_id(0) == 0)
        def _():
            acc_ref[...] = jnp.zeros_like(acc_ref)

        dp = jnp.concatenate([dq_ref[...], dkv_ref[...], dz_ref[...], dxbc_ref[...], ddt_ref[...]], axis=1)
        dh = _mm(dp, w_ref[...])
        r, xhat = _modnorm_parts(x_ref[...])
        dx, dn, dsc, dsh = _modnorm_bwd(r, xhat, n_ref[...], sc_ref[...], dh)
        gx_ref[...] = dx1_ref[...] + dx
        h_ref[...] = (xhat * n_ref[...] * (1.0 + sc_ref[...]) + sh_ref[...]).astype(h_ref.dtype)
        acc_ref[0:1, :] += dn
        acc_ref[1:2, :] += dsc
        acc_ref[2:3, :] += dsh

    vec = pl.BlockSpec((1, D_MODEL), _fixed)
    return pl.pallas_call(
        body, name="in_proj_bwd", grid=(S // tm,),
        in_specs=[pl.BlockSpec((tm, D_MODEL), _row), pl.BlockSpec((tm, D_MODEL), _row),
                  pl.BlockSpec((tm, ATTN_W), _row), pl.BlockSpec((tm, 2 * KV_W), _row),
                  pl.BlockSpec((tm, SSM_W), _row), pl.BlockSpec((tm, XBC_W), _row), pl.BlockSpec((tm, LANE), _row),
                  vec, vec, vec, pl.BlockSpec((IN_PAD, D_MODEL), _fixed)],
        out_specs=[pl.BlockSpec((tm, D_MODEL), _row), pl.BlockSpec((tm, D_MODEL), _row),
                   pl.BlockSpec((8, D_MODEL), _fixed)],
        out_shape=[jax.ShapeDtypeStruct((S, D_MODEL), F32), jax.ShapeDtypeStruct((S, D_MODEL), MXU_DTYPE),
                   jax.ShapeDtypeStruct((8, D_MODEL), F32)],
        compiler_params=_cparams(VMEM_BIG),
    )(x, dx1, dq, dkv, dz, dxbc, ddt, norm1, scale1, shift1, w_in)


def _out_stage(ya, ys0, ys1, z0, z1, an, sn0, sn1):
    half = SSM_W // 2
    a = _rms(ya, an, ATTN_W)
    g0 = _rms(ys0 * _silu(z0), sn0, half)
    g1 = _rms(ys1 * _silu(z1), sn1, half)
    return jnp.concatenate([a, g0, g1], axis=1)


def _out_stage_args(ya_ref, ys_ref, z_ref, an_ref, sn_ref):
    half = SSM_W // 2
    return (ya_ref[...], ys_ref[:, :half], ys_ref[:, half:], z_ref[:, :half], z_ref[:, half:],
            an_ref[...], sn_ref[:, :half], sn_ref[:, half:])


def _out_proj_fwd(x, ya, ys, z, an, sn, gate1, w_o, tm, xchg):
    S = x.shape[0]

    def body(x_ref, ya_ref, ys_ref, z_ref, an_ref, sn_ref, g_ref, w_ref, x1_ref):
        u = _out_stage(*_out_stage_args(ya_ref, ys_ref, z_ref, an_ref, sn_ref))
        x1_ref[...] = x_ref[...] + g_ref[...] * _mm(u, w_ref[...])

    half = pl.BlockSpec((tm, ATTN_W), _row)
    hvec = pl.BlockSpec((1, ATTN_W), _fixed)
    (x1,), x_out = _hosted_call(
        body, "out_proj_fwd", S // tm,
        in_specs=[pl.BlockSpec((tm, D_MODEL), _row), half, half, half, hvec, hvec,
                  pl.BlockSpec((1, D_MODEL), _fixed), pl.BlockSpec((D_MODEL, D_MODEL), _fixed)],
        out_specs=[pl.BlockSpec((tm, D_MODEL), _row)],
        out_shape=[jax.ShapeDtypeStruct((S, D_MODEL), F32)],
        scratch_shapes=[], args=(x, ya, ys, z, an, sn, gate1, w_o), xchg=xchg, cparams=_cparams(VMEM_BIG),
    )
    return x1, x_out


def _out_proj_bwd(dx1, ya, ys, z, an, sn, gate1, w_o, tm):
    S = dx1.shape[0]

    def body(dx1_ref, ya_ref, ys_ref, z_ref, an_ref, sn_ref, g_ref, w_ref,
             dya_ref, dys_ref, dz_ref, u_ref, dmix_ref, acc_ref):
        @pl.when(pl.program_id(0) == 0)
        def _():
            acc_ref[...] = jnp.zeros_like(acc_ref)

        u, vjp = jax.vjp(_out_stage, *_out_stage_args(ya_ref, ys_ref, z_ref, an_ref, sn_ref))
        dx1 = dx1_ref[...]
        mix = _mm(u, w_ref[...])
        dmix = dx1 * g_ref[...]
        du = _mm_nt(dmix, w_ref[...])
        dya, dys0, dys1, dz0, dz1, dan, dsn0, dsn1 = vjp(du)
        dya_ref[...] = dya
        dys_ref[...] = jnp.concatenate([dys0, dys1], axis=1)
        dz_ref[...] = jnp.concatenate([dz0, dz1], axis=1).astype(dz_ref.dtype)
        u_ref[...] = u.astype(u_ref.dtype)
        dmix_ref[...] = dmix.astype(dmix_ref.dtype)
        acc_ref[0:1, :] += jnp.sum(dx1 * mix, axis=0, keepdims=True)
        acc_ref[1:2, :] += jnp.concatenate([dan, dsn0, dsn1], axis=1)

    half = pl.BlockSpec((tm, ATTN_W), _row)
    hvec = pl.BlockSpec((1, ATTN_W), _fixed)
    full = pl.BlockSpec((tm, D_MODEL), _row)
    return pl.pallas_call(
        body, name="out_proj_bwd", grid=(S // tm,),
        in_specs=[full, half, half, half, hvec, hvec,
                  pl.BlockSpec((1, D_MODEL), _fixed), pl.BlockSpec((D_MODEL, D_MODEL), _fixed)],
        out_specs=[half, half, half, full, full, pl.BlockSpec((8, D_MODEL), _fixed)],
        out_shape=[jax.ShapeDtypeStruct((S, ATTN_W), F32)] * 2 + [jax.ShapeDtypeStruct((S, ATTN_W), MXU_DTYPE)]
        + [jax.ShapeDtypeStruct((S, D_MODEL), MXU_DTYPE)] * 2 + [jax.ShapeDtypeStruct((8, D_MODEL), F32)],
        compiler_params=_cparams(VMEM_BIG),
    )(dx1, ya, ys, z, an, sn, gate1, w_o)


def _loss_rows(x2, fn, tgt):
    y = _rms(x2, fn, D_MODEL)
    per_row = jnp.sum(jnp.square(y - tgt), axis=1, keepdims=True)
    return jnp.sum(per_row, axis=0, keepdims=True) * (0.5 / D_MODEL)


def _mlp_loss(x1, tgt, norm2, scale2, shift2, gate2, fnorm, w_gu, w_d, tm):
    S = x1.shape[0]
    n_pieces = len(w_gu) + len(w_d)

    def body(*refs):
        x1_ref, t_ref, n_ref, sc_ref, sh_ref, g_ref, fn_ref = refs[:7]
        piece_refs = refs[7:7 + n_pieces]
        dx1_ref, h_ref, dgu_ref, act_ref, dmlp_ref, acc_ref, wgu, wd, wsem = refs[7 + n_pieces:]

        @pl.when(pl.program_id(0) == 0)
        def _():
            acc_ref[...] = jnp.zeros_like(acc_ref)
            copies = []
            for dst, pieces in ((wgu, piece_refs[:len(w_gu)]), (wd, piece_refs[len(w_gu):])):
                shard = sum(p.shape[1] for p in pieces)
                off = 0
                for p in pieces:
                    for j in range(N_DEV):
                        copies.append(pltpu.make_async_copy(p.at[j], dst.at[pl.ds(j * shard + off, p.shape[1])],
                                                            wsem.at[len(copies)]))
                    off += p.shape[1]
            for cp in copies:
                cp.start()
            for cp in copies:
                cp.wait()

        x1 = x1_ref[...]
        gate2 = g_ref[...]
        h, vjp_h = jax.vjp(_modnorm, x1, n_ref[...], sc_ref[...], sh_ref[...])
        hb = h.astype(MXU_DTYPE)
        gu = _mm_nt(hb, wgu[...])
        g, u = gu[:, :D_FF], gu[:, D_FF:]
        sg = jax.nn.sigmoid(g)
        silu_g = g * sg
        act = (silu_g * u).astype(MXU_DTYPE)
        mlp = _mm(act, wd[...])
        x2 = x1 + gate2 * mlp
        loss, vjp_loss = jax.vjp(_loss_rows, x2, fn_ref[...], t_ref[...])
        dx2, dfn, _ = vjp_loss(jnp.ones((1, 1), F32))
        dmlp = (dx2 * gate2).astype(MXU_DTYPE)
        dact = _mm_nt(dmlp, wd[...])
        dg = dact * u * (sg * (1.0 + g * (1.0 - sg)))
        du = dact * silu_g
        dgu = jnp.concatenate([dg, du], axis=1).astype(MXU_DTYPE)
        dh = _mm(dgu, wgu[...])
        dx, dn, dsc, dsh = vjp_h(dh)
        dx1_ref[...] = dx2 + dx
        h_ref[...] = hb
        dgu_ref[...] = dgu
        act_ref[...] = act
        dmlp_ref[...] = dmlp
        acc_ref[0:1, :] += dn
        acc_ref[1:2, :] += dsc
        acc_ref[2:3, :] += dsh
        acc_ref[3:4, :] += jnp.sum(dx2 * mlp, axis=0, keepdims=True)
        acc_ref[4:5, :] += dfn
        acc_ref[5:6, :] += jnp.broadcast_to(loss, (1, D_MODEL))

    full = pl.BlockSpec((tm, D_MODEL), _row)
    vec = pl.BlockSpec((1, D_MODEL), _fixed)
    anyspec = pl.BlockSpec(memory_space=pl.ANY)
    return pl.pallas_call(
        body, name="mlp_loss", grid=(S // tm,),
        in_specs=[full, full, vec, vec, vec, vec, vec] + [anyspec] * n_pieces,
        out_specs=[full, full, pl.BlockSpec((tm, 2 * D_FF), _row), pl.BlockSpec((tm, D_FF), _row), full,
                   pl.BlockSpec((8, D_MODEL), _fixed)],
        out_shape=[jax.ShapeDtypeStruct((S, D_MODEL), F32), jax.ShapeDtypeStruct((S, D_MODEL), MXU_DTYPE),
                   jax.ShapeDtypeStruct((S, 2 * D_FF), MXU_DTYPE), jax.ShapeDtypeStruct((S, D_FF), MXU_DTYPE),
                   jax.ShapeDtypeStruct((S, D_MODEL), MXU_DTYPE), jax.ShapeDtypeStruct((8, D_MODEL), F32)],
        scratch_shapes=[pltpu.VMEM((2 * D_FF, D_MODEL), MXU_DTYPE), pltpu.VMEM((D_FF, D_MODEL), MXU_DTYPE),
                        pltpu.SemaphoreType.DMA((N_DEV * n_pieces,))],
        compiler_params=_cparams(VMEM_BIG),
    )(x1, tgt, norm2, scale2, shift2, gate2, fnorm, *w_gu, *w_d)


def _wgrad(a, g, tk, ts, name, xchg=None, g_cols=None):
    pieces = list(a) if isinstance(a, (list, tuple)) else [a]
    S = pieces[0].shape[0]
    K = sum(p.shape[1] for p in pieces)
    assert len(pieces) == 1 or tk == K
    N, col = (g.shape[1], 0) if g_cols is None else g_cols
    ns = S // ts
    n_a = len(pieces)

    def body(*refs):
        a_refs, (g_ref, o_ref, acc_ref) = refs[:n_a], refs[n_a:]
        s = pl.program_id(1)

        @pl.when(s == 0)
        def _():
            acc_ref[...] = jnp.zeros_like(acc_ref)

        a_blk = a_refs[0][...] if n_a == 1 else jnp.concatenate([r[...] for r in a_refs], axis=1)
        acc_ref[...] += _mm_tn(a_blk, g_ref[...])

        @pl.when(s == ns - 1)
        def _():
            o_ref[...] = acc_ref[...].astype(o_ref.dtype)

    if n_a == 1:
        in_specs = [pl.BlockSpec((ts, tk), lambda j, s: (s, j))]
    else:
        in_specs = [pl.BlockSpec((ts, p.shape[1]), lambda j, s: (s, 0)) for p in pieces]
    in_specs.append(pl.BlockSpec((ts, N), lambda j, s: (s, col)))
    out_spec = pl.BlockSpec((tk, N), lambda j, s: (j, 0))
    out_shape = jax.ShapeDtypeStruct((K, N), WIRE_DTYPE)
    scratch = [pltpu.VMEM((tk, N), F32)]
    args = (*pieces, g)
    if xchg is None:
        return pl.pallas_call(body, name=name, grid=(K // tk, ns), in_specs=in_specs, out_specs=out_spec,
                              out_shape=out_shape, scratch_shapes=scratch, compiler_params=_cparams(VMEM_BIG))(*args)
    (out,), x_out = _hosted_call(body, name, (K // tk, ns), in_specs, [out_spec], [out_shape], scratch, args, xchg,
                                 _cparams(VMEM_BIG))
    return out, x_out


ATTN_BLOCKS_PER_STEP = 4
MASKED = -1e30
QK_SCALE = HALF ** -0.5


def _attn_bias(buckets, rel_bias):
    def body(bk_ref, relb_ref, out_ref):
        bk = bk_ref[...]
        i = lax.broadcasted_iota(jnp.int32, (BLK, 2 * BLK), 0)
        j = lax.broadcasted_iota(jnp.int32, (BLK, 2 * BLK), 1)
        window = (j > i) & (j <= i + BLK)
        for h in range(N_HEADS):
            acc = jnp.zeros((BLK, 2 * BLK), F32)
            for b in range(N_BUCKETS):
                acc = jnp.where(bk == b, relb_ref[b, h], acc)
            out_ref[0, h] = jnp.where(window, acc, MASKED)
            out_ref[1, h] = jnp.where(window & (j >= BLK), acc, MASKED)

    return pl.pallas_call(
        body, name="attn_bias", out_shape=jax.ShapeDtypeStruct((2, N_HEADS, BLK, 2 * BLK), F32),
        in_specs=[pl.BlockSpec(memory_space=pltpu.VMEM), pl.BlockSpec(memory_space=pltpu.SMEM)],
    )(buckets, rel_bias)


def _attn_fwd(qkv, bias, sinks, xchg):
    S = qkv.shape[0]
    nb = S // BLK

    nq = ATTN_BLOCKS_PER_STEP if nb % ATTN_BLOCKS_PER_STEP == 0 else 1
    rows = nq * BLK

    def body(q_ref, kvp_ref, kvc_ref, bias_ref, sinks_ref, y_ref):
        i = pl.program_id(0)
        q = q_ref[...].astype(F32) * QK_SCALE
        kv = jnp.concatenate([kvp_ref[...], kvc_ref[...]], axis=0).astype(F32)
        k_lo, k_hi = _split_pair(kv[:, :LANE])
        v_lo, v_hi = _split_pair(kv[:, LANE:])
        bands = [[t[b * BLK:(b + 2) * BLK].astype(MXU_DTYPE) for t in (k_lo, k_hi, v_lo, v_hi)] for b in range(nq)]
        q_heads = [_split_heads(q[b * BLK:(b + 1) * BLK], 4) for b in range(nq)]
        first = [jnp.where(i == 0, 1, 0) if b == 0 else 0 for b in range(nq)]
        items = [(b, h) for b in range(nq) for h in range(N_HEADS)]
        s = [_mm_nt(q_heads[b][h].astype(MXU_DTYPE), bands[b][h // 4]) + bias_ref[first[b], h] for b, h in items]
        m = [jnp.maximum(jnp.max(s[n], axis=-1, keepdims=True), sinks_ref[h]) for n, (b, h) in enumerate(items)]
        p = [jnp.exp(s[n] - m[n]) for n in range(len(items))]
        rinv = [1.0 / (jnp.sum(p[n], axis=-1, keepdims=True) + jnp.exp(sinks_ref[h] - m[n]))
                for n, (b, h) in enumerate(items)]
        out = [_mm(p[n], bands[b][2 + h // 4]) * rinv[n] for n, (b, h) in enumerate(items)]
        y_ref[...] = jnp.concatenate([_join_heads(out[b * N_HEADS:(b + 1) * N_HEADS]) for b in range(nq)], axis=0)

    smem = pl.BlockSpec(memory_space=pltpu.SMEM)
    return _hosted_call(
        body, "attn_fwd", nb // nq,
        in_specs=[pl.BlockSpec((rows, ATTN_W), _row),
                  pl.BlockSpec((BLK, 2 * KV_W), lambda i: (jnp.maximum(i * nq - 1, 0), 2)),
                  pl.BlockSpec((rows, 2 * KV_W), lambda i: (i, 2)),
                  pl.BlockSpec((2, N_HEADS, BLK, 2 * BLK), lambda i: (0, 0, 0, 0)), smem],
        out_specs=[pl.BlockSpec((rows, ATTN_W), _row)],
        out_shape=[jax.ShapeDtypeStruct((S, ATTN_W), F32)],
        scratch_shapes=[],
        args=(qkv, qkv, qkv, bias, sinks), xchg=xchg, cparams=_cparams(),
    )


def _attn_bwd(qkv, y, dy, bias, sinks, xchg):
    S = qkv.shape[0]
    nb = S // BLK
    nq = ATTN_BLOCKS_PER_STEP if nb % ATTN_BLOCKS_PER_STEP == 0 else 1
    rows, n_steps = nq * BLK, nb // nq

    def body(q_ref, kvp_ref, kvc_ref, y_ref, dy_ref, bias_ref, sinks_ref, dq_ref, dkv_ref, dbias_ref, dsk_ref, carry_ref):
        i = pl.program_id(0)

        @pl.when(i == 0)
        def _():
            dbias_ref[...] = jnp.zeros_like(dbias_ref)
            dsk_ref[...] = jnp.zeros_like(dsk_ref)
            carry_ref[...] = jnp.zeros_like(carry_ref)

        q = q_ref[...].astype(F32) * QK_SCALE
        kv = jnp.concatenate([kvp_ref[...], kvc_ref[...]], axis=0).astype(F32)
        k_lo, k_hi = _split_pair(kv[:, :LANE])
        v_lo, v_hi = _split_pair(kv[:, LANE:])
        bands = [[t[b * BLK:(b + 2) * BLK].astype(MXU_DTYPE) for t in (k_lo, k_hi, v_lo, v_hi)] for b in range(nq)]
        rows_of = lambda ref, b: ref[b * BLK:(b + 1) * BLK, :]
        first = [jnp.where(i == n_steps - 1, 1, 0) if b == 0 else 0 for b in range(nq)]
        items = [(b, h) for b in range(nq) for h in range(N_HEADS)]
        at = lambda b, h: b * N_HEADS + h
        q_heads = [hd for b in range(nq) for hd in _split_heads(q[b * BLK:(b + 1) * BLK], 4)]
        y_heads = [hd for b in range(nq) for hd in _split_heads(rows_of(y_ref, b), 4)]
        dy_heads = [hd for b in range(nq) for hd in _split_heads(rows_of(dy_ref, b), 4)]
        qs = [q_heads[n].astype(MXU_DTYPE) for n in range(len(items))]
        s = [_mm_nt(qs[at(b, h)], bands[b][h // 4]) + bias_ref[first[b], h] for b, h in items]
        m = [jnp.maximum(jnp.max(s[at(b, h)], axis=-1, keepdims=True), sinks_ref[h]) for b, h in items]
        p = [jnp.exp(s[n] - m[n]) for n in range(len(items))]
        esink = [jnp.exp(sinks_ref[h] - m[at(b, h)]) for b, h in items]
        rinv = [1.0 / (jnp.sum(p[n], axis=-1, keepdims=True) + esink[n]) for n in range(len(items))]
        t = [dy_heads[n] * rinv[n] for n in range(len(items))]
        delta = [jnp.sum(t[n] * y_heads[n], axis=-1, keepdims=True) for n in range(len(items))]
        tb = [t[n].astype(MXU_DTYPE) for n in range(len(items))]
        dp = [_mm_nt(tb[at(b, h)], bands[b][2 + h // 4]) for b, h in items]
        ds = [p[n] * (dp[n] - delta[n]) for n in range(len(items))]
        for h in range(N_HEADS):
            ds_h, dsk_h = ds[at(0, h)], esink[at(0, h)] * delta[at(0, h)]
            for b in range(1, nq):
                ds_h = ds_h + ds[at(b, h)]
                dsk_h = dsk_h + esink[at(b, h)] * delta[at(b, h)]
            dbias_ref[h] += ds_h
            dsk_ref[h] -= dsk_h
        dsb = [ds[n].astype(MXU_DTYPE) for n in range(len(items))]
        pb = [p[n].astype(MXU_DTYPE) for n in range(len(items))]
        dq_heads = [_mm(dsb[at(b, h)], bands[b][h // 4]) * QK_SCALE for b, h in items]
        grp = lambda lst, b, g: jnp.concatenate(lst[at(b, 4 * g):at(b, 4 * g) + 4], axis=0)
        dk_pads = [[_mm_tn(grp(dsb, b, g), grp(qs, b, g)) for g in range(2)] for b in range(nq)]
        dv_pads = [[_mm_tn(grp(pb, b, g), grp(tb, b, g)) for g in range(2)] for b in range(nq)]
        dq_ref[...] = jnp.concatenate([_join_heads(dq_heads[b * N_HEADS:(b + 1) * N_HEADS]) for b in range(nq)],
                                      axis=0).astype(dq_ref.dtype)
        part = lambda b, lo: jnp.concatenate(
            [_join_pair(d[b][0][lo:lo + BLK], d[b][1][lo:lo + BLK]) for d in (dk_pads, dv_pads)], axis=1)
        dkv = [part(b, BLK) + (part(b + 1, 0) if b + 1 < nq else carry_ref[...]) for b in range(nq)]
        dkv_ref[...] = jnp.concatenate(dkv, axis=0).astype(dkv_ref.dtype)
        carry_ref[...] = part(0, 0)

    smem = pl.BlockSpec(memory_space=pltpu.SMEM)
    rev = lambda i: (n_steps - 1 - i, 0)
    return _hosted_call(
        body, "attn_bwd", n_steps,
        in_specs=[pl.BlockSpec((rows, ATTN_W), rev),
                  pl.BlockSpec((BLK, 2 * KV_W), lambda i: (jnp.maximum((n_steps - 1 - i) * nq - 1, 0), 2)),
                  pl.BlockSpec((rows, 2 * KV_W), lambda i: (n_steps - 1 - i, 2)),
                  pl.BlockSpec((rows, ATTN_W), rev), pl.BlockSpec((rows, ATTN_W), rev),
                  pl.BlockSpec((2, N_HEADS, BLK, 2 * BLK), lambda i: (0, 0, 0, 0)), smem],
        out_specs=[pl.BlockSpec((rows, ATTN_W), rev), pl.BlockSpec((rows, 2 * KV_W), rev),
                   pl.BlockSpec((N_HEADS, BLK, 2 * BLK), lambda i: (0, 0, 0)),
                   pl.BlockSpec((N_HEADS, BLK, 1), lambda i: (0, 0, 0))],
        out_shape=[jax.ShapeDtypeStruct((S, ATTN_W), MXU_DTYPE), jax.ShapeDtypeStruct((S, 2 * KV_W), MXU_DTYPE),
                   jax.ShapeDtypeStruct((N_HEADS, BLK, 2 * BLK), F32), jax.ShapeDtypeStruct((N_HEADS, BLK, 1), F32)],
        scratch_shapes=[pltpu.VMEM((BLK, 2 * KV_W), F32)],
        args=(qkv, qkv, qkv, y, dy, bias, sinks), xchg=xchg, cparams=_cparams(),
    )


def _attn_finish(dbias, dsk, buckets):
    def body(db_ref, dsk_ref, bk_ref, drel_ref, dsink_ref):
        bk = bk_ref[...]
        r = lax.broadcasted_iota(jnp.int32, (N_BUCKETS, LANE), 0)
        l = lax.broadcasted_iota(jnp.int32, (N_BUCKETS, LANE), 1)
        row = lax.broadcasted_iota(jnp.int32, (N_HEADS, LANE), 0)
        res = jnp.zeros((N_BUCKETS, LANE), F32)
        dsink = jnp.zeros((N_HEADS, LANE), F32)
        for h in range(N_HEADS):
            db = db_ref[h]
            for b in range(N_BUCKETS):
                v = jnp.sum(jnp.sum(jnp.where(bk == b, db, 0.0), axis=1, keepdims=True), axis=0, keepdims=True)
                res = res + jnp.where((r == b) & (l == h), v, 0.0)
            dsink = dsink + jnp.where(row == h, jnp.sum(dsk_ref[h], axis=0, keepdims=True), 0.0)
        drel_ref[...] = res
        dsink_ref[...] = dsink

    return pl.pallas_call(body, name="attn_finish",
                          out_shape=[jax.ShapeDtypeStruct((N_BUCKETS, LANE), F32),
                                     jax.ShapeDtypeStruct((N_HEADS, LANE), F32)])(dbias, dsk, buckets)


def _ssd_consts():
    r = lax.broadcasted_iota(jnp.int32, (BLK, BLK), 0)
    c = lax.broadcasted_iota(jnp.int32, (BLK, BLK), 1)
    causal = c <= r
    upper = (r <= c).astype(F32)
    last = r == BLK - 1
    head = lax.broadcasted_iota(jnp.int32, (N_HEADS, BLK), 0)
    return causal, upper, last, head


def _ssd_chunk(xs, bg, cg, dt_raw_t, prev, dtb, alog, d_rows, consts):
    causal, upper, last, head = consts
    dt_t = _softplus(dt_raw_t + dtb)
    acs_t = _mm_hi(dt_t * (-jnp.exp(alog)), upper)
    cb = [_mm_nt(cg[g], bg[g]) for g in range(2)]
    heads = range(N_HEADS)
    dt_row = [jnp.sum(jnp.where(head == h, dt_t, 0.0), axis=0, keepdims=True) for h in heads]
    a_row = [jnp.sum(jnp.where(head == h, acs_t, 0.0), axis=0, keepdims=True) for h in heads]
    a_rb = [jnp.broadcast_to(a_row[h], (BLK, BLK)) for h in heads]
    a_b = [a_rb[h].T for h in heads]
    a_last = [jnp.sum(jnp.where(last, a_b[h], 0.0), axis=0, keepdims=True) for h in heads]
    w = [cb[h // 4] * jnp.exp(jnp.where(causal, a_b[h] - a_rb[h], -1e30)) * dt_row[h] for h in heads]
    f_b = [jnp.broadcast_to(dt_row[h] * jnp.exp(a_last[h] - a_row[h]), (BLK, BLK)).T for h in heads]
    y_in = [_mm(w[h], xs[h]) for h in heads]
    y_off = [_mm(cg[h // 4], prev[h]) * jnp.exp(a_b[h]) for h in heads]
    st = [_mm_tn(bg[h // 4], xs[h] * f_b[h]) for h in heads]
    ys = [y_in[h] + y_off[h] + d_rows[h] * xs[h] for h in heads]
    hs = [prev[h] * jnp.exp(a_last[h]) + st[h] for h in heads]
    return tuple(ys), tuple(hs)


def _ssd_chunk_bwd(xs, bg, cg, dt_raw_t, prev, dtb, alog, d_rows, dys, dhs, consts):
    causal, upper, last, head = consts
    heads, groups = range(N_HEADS), range(2)
    lane = _lane_iota((BLK, BLK))
    lane_row = _lane_iota((1, BLK))
    pre_dt = dt_raw_t + dtb
    dt_t = _softplus(pre_dt)
    a_neg = -jnp.exp(alog)
    acs_t = _mm_hi(dt_t * a_neg, upper)
    pick = lambda t, h: jnp.sum(jnp.where(head == h, t, 0.0), axis=0, keepdims=True)
    full_sum = lambda t: jnp.sum(jnp.sum(t, axis=1, keepdims=True), axis=0, keepdims=True)
    dt_row = [pick(dt_t, h) for h in heads]
    a_row = [pick(acs_t, h) for h in heads]
    a_rb = [jnp.broadcast_to(a_row[h], (BLK, BLK)) for h in heads]
    a_b = [a_rb[h].T for h in heads]
    a_last = [jnp.sum(jnp.where(last, a_b[h], 0.0), axis=0, keepdims=True) for h in heads]
    lm = [jnp.exp(jnp.where(causal, a_b[h] - a_rb[h], -1e30)) for h in heads]
    cgb = [cg[g].astype(MXU_DTYPE) for g in groups]
    bgb = [bg[g].astype(MXU_DTYPE) for g in groups]
    cb = [_mm_nt(cgb[g], bgb[g]) for g in groups]
    u = [cb[h // 4] * lm[h] for h in heads]
    w = [(u[h] * dt_row[h]).astype(MXU_DTYPE) for h in heads]
    e_row = [jnp.exp(a_last[h] - a_row[h]) for h in heads]
    f_row = [dt_row[h] * e_row[h] for h in heads]
    f_b = [jnp.broadcast_to(f_row[h], (BLK, BLK)).T for h in heads]
    e_b = [jnp.exp(a_b[h]) for h in heads]
    el = [jnp.exp(a_last[h]) for h in heads]
    xb = [xs[h].astype(MXU_DTYPE) for h in heads]
    dyb = [dys[h].astype(MXU_DTYPE) for h in heads]
    prevb = [prev[h].astype(MXU_DTYPE) for h in heads]
    dstb = [dhs[h].astype(MXU_DTYPE) for h in heads]
    gmat = [_mm(cgb[h // 4], prevb[h]) for h in heads]
    dw = [_mm_nt(dyb[h], xb[h]) for h in heads]
    dg = [dys[h] * e_b[h] for h in heads]
    dgb = [dg[h].astype(MXU_DTYPE) for h in heads]
    dxf = [_mm(bgb[h // 4], dstb[h]) for h in heads]
    xfb = [(xs[h] * f_b[h]).astype(MXU_DTYPE) for h in heads]
    dxs = [_mm_tn(w[h], dyb[h]) + d_rows[h] * dys[h] + f_b[h] * dxf[h] for h in heads]
    dd_rows = [jnp.sum(dys[h] * xs[h], axis=0, keepdims=True) for h in heads]
    dprev = [_mm_tn(cgb[h // 4], dgb[h]) + dhs[h] * el[h] for h in heads]
    dcg_h = [_mm_nt(dgb[h], prevb[h]) for h in heads]
    dbg_h = [_mm_nt(xfb[h], dstb[h]) for h in heads]
    zt = [dw[h] * u[h] for h in heads]
    dseg = [zt[h] * dt_row[h] for h in heads]
    dcb_h = [dw[h] * lm[h] * dt_row[h] for h in heads]
    dcb = [(dcb_h[4 * g] + dcb_h[4 * g + 1] + dcb_h[4 * g + 2] + dcb_h[4 * g + 3]).astype(MXU_DTYPE) for g in groups]
    dcg = [dcg_h[4 * g] + dcg_h[4 * g + 1] + dcg_h[4 * g + 2] + dcg_h[4 * g + 3] + _mm(dcb[g], bgb[g]) for g in groups]
    dbg = [dbg_h[4 * g] + dbg_h[4 * g + 1] + dbg_h[4 * g + 2] + dbg_h[4 * g + 3] + _mm_tn(dcb[g], cgb[g])
           for g in groups]
    r1 = [jnp.sum(dg[h] * gmat[h] + dseg[h], axis=1, keepdims=True) for h in heads]
    r2 = [jnp.sum(dxf[h] * xs[h], axis=1, keepdims=True) for h in heads]
    tt = [jnp.where(lane < HALF, jnp.broadcast_to(r1[h], (BLK, BLK)), jnp.broadcast_to(r2[h], (BLK, BLK))).T
          for h in heads]
    r1_row = [tt[h][0:1, :] for h in heads]
    r2_row = [tt[h][HALF:HALF + 1, :] for h in heads]
    d_el = [full_sum(dhs[h] * prev[h]) for h in heads]
    da_last = [jnp.sum(r2_row[h] * f_row[h], axis=1, keepdims=True) + el[h] * d_el[h] for h in heads]
    da_row = [r1_row[h] - jnp.sum(dseg[h], axis=0, keepdims=True) - r2_row[h] * f_row[h]
              + jnp.where(lane_row == BLK - 1, da_last[h], 0.0) for h in heads]
    ddt_row = [jnp.sum(zt[h], axis=0, keepdims=True) + r2_row[h] * e_row[h] for h in heads]
    da_t = jnp.zeros((N_HEADS, BLK), F32)
    ddt_t = jnp.zeros((N_HEADS, BLK), F32)
    for h in heads:
        da_t = jnp.where(head == h, da_row[h], da_t)
        ddt_t = jnp.where(head == h, ddt_row[h], ddt_t)
    d_dta = _mm_hi(da_t, causal.astype(F32))
    dalog = d_dta * dt_t * a_neg
    draw = (ddt_t + d_dta * a_neg) * jax.nn.sigmoid(pre_dt)
    return dxs, dbg, dcg, draw, dprev, draw, dalog, dd_rows


def _dt_rows(dt_blk):
    return dt_blk.T[:N_HEADS]


def _silu_grad(x):
    s = jax.nn.sigmoid(x)
    return s * (1.0 + x * (1.0 - s))


def _conv_pre(halo, blk, cw_ref, cb_ref):
    ext = jnp.concatenate([halo, blk], axis=0)
    taps = [pltpu.roll(ext, 3 - k, 0)[8:] for k in range(3)] + [blk]
    pre = cb_ref[...] + cw_ref[0:1, :] * taps[0]
    for k in range(1, 4):
        pre = pre + cw_ref[k:k + 1, :] * taps[k]
    return pre, taps


def _ssd_split(pre):
    heads = _split_heads(pre[:, :SSM_W], 4)
    pb = [pre[:, SSM_W + g * D_STATE:SSM_W + (g + 1) * D_STATE] for g in range(2)]
    pc = [pre[:, SSM_W + 2 * D_STATE + g * D_STATE:SSM_W + 2 * D_STATE + (g + 1) * D_STATE] for g in range(2)]
    return heads, pb, pc


def _ssd_fwd(xbc, dt_raw, conv_w, conv_b, dtb_row, alog_row, d_exp, xchg):
    S = xbc.shape[0]
    nc = S // BLK

    def body(xbc_ref, halo_ref, dt_ref, cw_ref, cb_ref, dtb_ref, alog_ref, d_ref, y_ref, prev_ref, state_ref):
        i = pl.program_id(0)

        @pl.when(i == 0)
        def _():
            state_ref[...] = jnp.zeros_like(state_ref)

        halo = halo_ref[...] * jnp.where(i > 0, 1.0, 0.0)
        pre, _ = _conv_pre(halo, xbc_ref[...], cw_ref, cb_ref)
        heads, pb, pc = _ssd_split(_silu(pre))
        prev = [state_ref[h] for h in range(N_HEADS)]
        for h in range(N_HEADS):
            prev_ref[0, h] = prev[h]
        d_rows = [d_ref[h:h + 1, :] for h in range(N_HEADS)]
        ys, hs = _ssd_chunk(heads, pb, pc, _dt_rows(dt_ref[...]), prev, dtb_ref[...], alog_ref[...], d_rows,
                            _ssd_consts())
        for h in range(N_HEADS):
            state_ref[h] = hs[h]
        y_ref[...] = _join_heads(ys)

    vec = pl.BlockSpec((N_HEADS, LANE), _fixed)
    return _hosted_call(
        body, "ssd_fwd", nc,
        in_specs=[pl.BlockSpec((BLK, XBC_W), _row),
                  pl.BlockSpec((8, XBC_W), lambda i: (jnp.maximum(i * (BLK // 8) - 1, 0), 0)),
                  pl.BlockSpec((BLK, LANE), _row),
                  pl.BlockSpec((4, XBC_W), _fixed), pl.BlockSpec((1, XBC_W), _fixed), vec, vec,
                  pl.BlockSpec((N_HEADS, LANE), _fixed)],
        out_specs=[pl.BlockSpec((BLK, SSM_W), _row),
                   pl.BlockSpec((1, N_HEADS, D_STATE, LANE), lambda i: (i, 0, 0, 0))],
        out_shape=[jax.ShapeDtypeStruct((S, SSM_W), F32), jax.ShapeDtypeStruct((nc, N_HEADS, D_STATE, LANE), F32)],
        scratch_shapes=[pltpu.VMEM((N_HEADS, D_STATE, LANE), F32)],
        args=(xbc, xbc, dt_raw, conv_w, conv_b, dtb_row, alog_row, d_exp), xchg=xchg, cparams=_cparams(),
    )


def _ssd_bwd(xbc, dt_raw, prev_states, dy, conv_w, conv_b, dtb_row, alog_row, d_exp, xchg):
    S = xbc.shape[0]
    nc = S // BLK

    def body(xbc_ref, halo_ref, dt_ref, prev_ref, dy_ref, cw_ref, cb_ref, dtb_ref, alog_ref, d_ref,
             dxbc_ref, ddt_ref, dcw_ref, dvec_ref, dd_ref, gstate_ref, ghalo_ref):
        i = pl.program_id(0)
        c = nc - 1 - i

        @pl.when(i == 0)
        def _():
            gstate_ref[...] = jnp.zeros_like(gstate_ref)
            ghalo_ref[...] = jnp.zeros_like(ghalo_ref)
            dcw_ref[...] = jnp.zeros_like(dcw_ref)
            dvec_ref[...] = jnp.zeros_like(dvec_ref)
            dd_ref[...] = jnp.zeros_like(dd_ref)

        halo = halo_ref[...] * jnp.where(c > 0, 1.0, 0.0)
        pre, taps = _conv_pre(halo, xbc_ref[...], cw_ref, cb_ref)
        heads, pb, pc = _ssd_split(_silu(pre))
        prev = [prev_ref[0, h] for h in range(N_HEADS)]
        d_rows = [d_ref[h:h + 1, :] for h in range(N_HEADS)]
        dys = _split_heads(dy_ref[...], 4)
        dhs = [gstate_ref[h] for h in range(N_HEADS)]
        dheads, dpb, dpc, ddt_t, dprev, ddtb, dalog, dd_rows = _ssd_chunk_bwd(
            heads, pb, pc, _dt_rows(dt_ref[...]), prev, dtb_ref[...], alog_ref[...], d_rows, dys, dhs, _ssd_consts())
        for h in range(N_HEADS):
            gstate_ref[h] = dprev[h]
            dd_ref[h:h + 1, :] += dd_rows[h]
        ddt_ref[...] = jnp.concatenate([ddt_t, jnp.zeros((BLK - N_HEADS, BLK), F32)], axis=0).T.astype(ddt_ref.dtype)
        dvec_ref[0:N_HEADS, :] += ddtb
        dvec_ref[N_HEADS:, :] += dalog
        dpre = jnp.concatenate([_join_heads(dheads)] + list(dpb) + list(dpc), axis=1) * _silu_grad(pre)
        zeros8 = jnp.zeros((8, XBC_W), F32)
        dpe = jnp.concatenate([zeros8, dpre, zeros8], axis=0)
        n_ext = 16 + BLK
        dext = cw_ref[3:4, :] * dpe[:8 + BLK]
        dcw_ref[3:4, :] += jnp.sum(dpre * taps[3], axis=0, keepdims=True)
        for k in range(3):
            dext = dext + cw_ref[k:k + 1, :] * pltpu.roll(dpe, n_ext - (3 - k), 0)[:8 + BLK]
            dcw_ref[k:k + 1, :] += jnp.sum(dpre * taps[k], axis=0, keepdims=True)
        dcw_ref[4:5, :] += jnp.sum(dpre, axis=0, keepdims=True)
        dxbc_ref[...] = jnp.concatenate([dext[8:BLK], dext[BLK:] + ghalo_ref[...]], axis=0).astype(dxbc_ref.dtype)
        ghalo_ref[...] = dext[:8, :]

    vec = pl.BlockSpec((N_HEADS, LANE), _fixed)
    rev = lambda i: (nc - 1 - i, 0)
    return _hosted_call(
        body, "ssd_bwd", nc,
        in_specs=[pl.BlockSpec((BLK, XBC_W), rev),
                  pl.BlockSpec((8, XBC_W), lambda i: (jnp.maximum((nc - 1 - i) * (BLK // 8) - 1, 0), 0)),
                  pl.BlockSpec((BLK, LANE), rev),
                  pl.BlockSpec((1, N_HEADS, D_STATE, LANE), lambda i: (nc - 1 - i, 0, 0, 0)),
                  pl.BlockSpec((BLK, SSM_W), rev),
                  pl.BlockSpec((4, XBC_W), _fixed), pl.BlockSpec((1, XBC_W), _fixed), vec, vec,
                  pl.BlockSpec((N_HEADS, LANE), _fixed)],
        out_specs=[pl.BlockSpec((BLK, XBC_W), rev), pl.BlockSpec((BLK, LANE), rev),
                   pl.BlockSpec((8, XBC_W), _fixed), pl.BlockSpec((2 * N_HEADS, LANE), _fixed),
                   pl.BlockSpec((N_HEADS, LANE), _fixed)],
        out_shape=[jax.ShapeDtypeStruct((S, XBC_W), MXU_DTYPE), jax.ShapeDtypeStruct((S, LANE), MXU_DTYPE),
                   jax.ShapeDtypeStruct((8, XBC_W), F32), jax.ShapeDtypeStruct((2 * N_HEADS, LANE), F32),
                   jax.ShapeDtypeStruct((N_HEADS, LANE), F32)],
        scratch_shapes=[pltpu.VMEM((N_HEADS, D_STATE, LANE), F32), pltpu.VMEM((8, XBC_W), F32)],
        args=(xbc, xbc, dt_raw, prev_states, dy, conv_w, conv_b, dtb_row, alog_row, d_exp), xchg=xchg,
        cparams=_cparams(VMEM_BIG),
    )


def _adamw_math(w, g, m, v):
    m = ADAM_B1 * m + (1.0 - ADAM_B1) * g
    v = ADAM_B2 * v + (1.0 - ADAM_B2) * jnp.square(g)
    m_hat = m / (1.0 - ADAM_B1 ** ADAM_STEP)
    v_hat = v / (1.0 - ADAM_B2 ** ADAM_STEP)
    delta = -ADAM_LR * (m_hat / (jnp.sqrt(v_hat) + ADAM_EPS) + ADAM_WD * w)
    return delta, m, v


def _reduce_adamw(parts, w, m, v, name):
    R, C = w.shape

    def body(p_ref, w_ref, m_ref, v_ref, g_ref, d_ref, nm_ref, nv_ref):
        g = p_ref[0].astype(F32)
        for i in range(1, N_DEV):
            g = g + p_ref[i].astype(F32)
        d, nm, nv = _adamw_math(w_ref[...], g, m_ref[...], v_ref[...])
        g_ref[...] = g
        d_ref[...] = d
        nm_ref[...] = nm
        nv_ref[...] = nv

    if R % 16 == 0:
        tr = max(t for t in range(16, 257, 16) if R % t == 0)
        n, blk, pblk = R // tr, pl.BlockSpec((tr, C), _row), pl.BlockSpec((N_DEV, tr, C), lambda i: (0, i, 0))
    else:
        tl = 256
        n, blk, pblk = C // tl, pl.BlockSpec((R, tl), lambda i: (0, i)), pl.BlockSpec((N_DEV, R, tl),
                                                                                      lambda i: (0, 0, i))
    return pl.pallas_call(
        body, name=name, grid=(n,), in_specs=[pblk, blk, blk, blk],
        out_specs=[blk] * 4, out_shape=[jax.ShapeDtypeStruct((R, C), F32)] * 4,
    )(parts, w, m, v)


def _reduce_adamw_hosting(parts_list, wmv_list, name, xchg):
    n_arr = len(parts_list)
    C = wmv_list[0][0].shape[1]
    tl = 256

    def body(*refs):
        p_refs, wmv_refs, o_refs = refs[:n_arr], refs[n_arr:4 * n_arr], refs[4 * n_arr:]
        for k in range(n_arr):
            g = p_refs[k][0].astype(F32)
            for i in range(1, N_DEV):
                g = g + p_refs[k][i].astype(F32)
            w_ref, m_ref, v_ref = wmv_refs[3 * k:3 * k + 3]
            d, nm, nv = _adamw_math(w_ref[...], g, m_ref[...], v_ref[...])
            for o, val in zip(o_refs[4 * k:4 * k + 4], (g, d, nm, nv)):
                o[...] = val

    in_specs = [pl.BlockSpec((N_DEV, w.shape[0], tl), lambda i: (0, 0, i)) for w, _, _ in wmv_list]
    in_specs += [pl.BlockSpec((w.shape[0], tl), lambda i: (0, i)) for w, _, _ in wmv_list for _ in range(3)]
    out_specs = [pl.BlockSpec((w.shape[0], tl), lambda i: (0, i)) for w, _, _ in wmv_list for _ in range(4)]
    out_shape = [jax.ShapeDtypeStruct(w.shape, F32) for w, _, _ in wmv_list for _ in range(4)]
    args = list(parts_list) + [a for wmv in wmv_list for a in wmv]
    outs, x_out = _hosted_call(body, name, C // tl, in_specs, out_specs, out_shape, [], args, xchg,
                               _cparams(VMEM_BIG))
    return [outs[4 * k:4 * k + 4] for k in range(n_arr)], x_out


_SMALL_NAMES = ("ada_b", "norm1", "conv_w", "conv_b", "dt_bias", "A_log", "D_skip", "sinks", "attn_out_norm",
                "ssm_out_norm", "norm2", "rel_bias", "final_norm")
N_MOD = 6 * D_MODEL


def _mod_row(a0, a1, a2):
    return jnp.concatenate([a0[2:3], a0[1:2], a1[0:1], a2[2:3], a2[1:2], a2[3:4]], axis=1)


def _small_update(gathered, params):
    n_g = len(gathered)
    flat = [a for name in _SMALL_NAMES for a in params[name]]

    def body(*refs):
        a0_ref, a1_ref, a2_ref, cw_ref, dv_ref, dd_ref, ds_ref, dr_ref, c_ref = refs[:n_g]
        wmv = refs[n_g:n_g + len(flat)]
        outs = refs[n_g + len(flat):]

        def total(ref):
            t = ref[0]
            for i in range(1, N_DEV):
                t = t + ref[i]
            return t

        t0, t1, t2, tcw, tdv, tdd, tds, tdr = [total(r) for r in (a0_ref, a1_ref, a2_ref, cw_ref, dv_ref, dd_ref,
                                                                   ds_ref, dr_ref)]
        r8 = lax.broadcasted_iota(jnp.int32, (N_HEADS, LANE), 0)
        l8 = lax.broadcasted_iota(jnp.int32, (N_HEADS, LANE), 1)

        def diag_row(t):
            return jnp.sum(jnp.where(r8 == l8, t, 0.0), axis=0, keepdims=True)[:, :N_HEADS]

        def lane_sums(t):
            return diag_row(jnp.broadcast_to(jnp.sum(t, axis=1, keepdims=True), (N_HEADS, LANE)))

        me = _lin(_my_pos())
        n_cw = XBC_W // N_DEV
        cw_mine = jnp.zeros((4, n_cw), F32)
        for j in range(N_DEV):
            cw_mine = cw_mine + tcw[0:4, j * n_cw:(j + 1) * n_cw] * jnp.where(me == j, 1.0, 0.0)
        grads = {
            "ada_b": _mod_row(t0, t1, t2), "norm1": t0[0:1], "conv_w": cw_mine, "conv_b": tcw[4:5],
            "dt_bias": lane_sums(tdv[:N_HEADS]), "A_log": lane_sums(tdv[N_HEADS:]), "D_skip": lane_sums(tdd),
            "sinks": diag_row(tds), "attn_out_norm": t1[1:2, :ATTN_W], "ssm_out_norm": t1[1:2, ATTN_W:],
            "norm2": t2[0:1], "rel_bias": tdr[:, :N_HEADS], "final_norm": t2[4:5],
        }
        for k, name in enumerate(_SMALL_NAMES):
            w_ref, m_ref, v_ref = wmv[3 * k:3 * k + 3]
            g = grads[name]
            d, nm, nv = _adamw_math(w_ref[...], g, m_ref[...], v_ref[...])
            for o, val in zip(outs[4 * k:4 * k + 4], (g, d, nm, nv)):
                o[...] = val
        loss_ref, call_ref, dmod_ref = outs[4 * len(_SMALL_NAMES):]
        loss_ref[...] = t2[5:6, 0:1]
        call_ref[...] = jnp.concatenate([c_ref[i] for i in range(N_DEV)], axis=0)
        dmod_ref[...] = jnp.concatenate([_mod_row(a0_ref[i], a1_ref[i], a2_ref[i]) for i in range(N_DEV)], axis=0)

    out_shape = [jax.ShapeDtypeStruct(params[name][0].shape, F32) for name in _SMALL_NAMES for _ in range(4)]
    out_shape += [jax.ShapeDtypeStruct((1, 1), F32), jax.ShapeDtypeStruct((N_DEV, D_MODEL), F32),
                  jax.ShapeDtypeStruct((N_DEV, N_MOD), F32)]
    res = pl.pallas_call(body, name="small_update", out_shape=out_shape)(*gathered, *flat)
    upd = {name: res[4 * k:4 * k + 4] for k, name in enumerate(_SMALL_NAMES)}
    loss, c_all, dmod_all = res[4 * len(_SMALL_NAMES):]
    return upd, loss, c_all, dmod_all


def _ada_w_update(c_all, dmod_all, w, m, v):
    chunk = w.shape[1]

    def body(c_ref, dm_ref, w_ref, m_ref, v_ref, g_ref, d_ref, nm_ref, nv_ref):
        me = _lin(_my_pos())
        dm = jnp.zeros((N_DEV, chunk), F32)
        for j in range(N_DEV):
            dm = dm + dm_ref[:, j * chunk:(j + 1) * chunk] * jnp.where(me == j, 1.0, 0.0)
        g = lax.dot_general(_silu(c_ref[...]), dm, (((0,), (0,)), ((), ())), precision=HI,
                            preferred_element_type=F32)
        d, nm, nv = _adamw_math(w_ref[...], g, m_ref[...], v_ref[...])
        g_ref[...] = g
        d_ref[...] = d
        nm_ref[...] = nm
        nv_ref[...] = nv

    tr = 256
    blk = pl.BlockSpec((tr, chunk), _row)
    return pl.pallas_call(
        body, name="ada_w_update", grid=(w.shape[0] // tr,),
        in_specs=[pl.BlockSpec((N_DEV, tr), lambda i: (0, i)), pl.BlockSpec(dmod_all.shape, _fixed), blk, blk, blk],
        out_specs=[blk] * 4, out_shape=[jax.ShapeDtypeStruct(w.shape, F32)] * 4,
    )(c_all, dmod_all, w, m, v)


def _local_step(x, tgt, c, mod, w_in, conv_w, w_o_mine, w_gu_mine, w_d_mine, p):
    S = x.shape[0]
    tm = min(512, S)
    tmm = min(256, S)
    tw = min(2048, S)
    shift1, scale1, gate1, shift2, scale2, gate2 = [mod[i:i + 1] for i in range(6)]
    buckets = jnp.asarray(_t5_bucket_table())
    per_head = lambda a: jnp.broadcast_to(a.reshape(N_HEADS, 1), (N_HEADS, LANE))
    dtb_row, alog_row, d_exp = per_head(p["dt_bias"]), per_head(p["A_log"]), per_head(p["D_skip"])
    sinks = p["sinks"].reshape(N_HEADS)

    d_cut, gu_cut = WD_CUT, WGU_CUTS
    (qkv, z, xbc, dt_raw), (g_d_a,) = _in_proj_fwd(x, p["norm1"], scale1, shift1, w_in, tm,
                                                   ([w_d_mine[:d_cut]], False))
    bias = _attn_bias(buckets, p["rel_bias"])
    (ya,), (g_gu_a,) = _attn_fwd(qkv, bias, sinks, ([w_gu_mine[:gu_cut[0]]], False))
    (ys, prev_states), (g_gu_b, g_o) = _ssd_fwd(xbc, dt_raw, conv_w, p["conv_b"], dtb_row, alog_row, d_exp,
                                                ([w_gu_mine[gu_cut[0]:gu_cut[1]], w_o_mine], False))
    w_o = g_o.reshape(D_MODEL, D_MODEL)
    x1, (g_gu_c, g_d_b) = _out_proj_fwd(x, ya, ys, z, p["attn_out_norm"], p["ssm_out_norm"], gate1, w_o, tm,
                                        ([w_gu_mine[gu_cut[1]:], w_d_mine[d_cut:]], False))
    dx1, h2, dgu, act, dmlp, acc2 = _mlp_loss(x1, tgt, p["norm2"], scale2, shift2, gate2, p["final_norm"],
                                              (g_gu_a, g_gu_b, g_gu_c), (g_d_a, g_d_b), tmm)
    g_w_gu = _wgrad(dgu, h2, 2 * D_FF // 4, tw, "wgrad_gate_up")
    g_w_d = _wgrad(act, dmlp, D_FF // 2, tw, "wgrad_down")
    dya, dys, dz, u, dmix, acc1 = _out_proj_bwd(dx1, ya, ys, z, p["attn_out_norm"], p["ssm_out_norm"], gate1, w_o, tm)
    g_w_o = _wgrad(u, dmix, D_MODEL, tw, "wgrad_out")
    (dq, dkv, dbias, dsk), (r_d,) = _attn_bwd(qkv, ya, dya, bias, sinks,
                                              ([g_w_d.reshape(N_DEV, D_FF // N_DEV, D_MODEL)], True))
    drel, dsink = _attn_finish(dbias, dsk, buckets)
    (dxbc, ddt, dcw, dvec, dd), (r_gu, r_o) = _ssd_bwd(
        xbc, dt_raw, prev_states, dys, conv_w, p["conv_b"], dtb_row, alog_row, d_exp,
        ([g_w_gu.reshape(N_DEV, 2 * D_FF // N_DEV, D_MODEL), g_w_o.reshape(N_DEV, D_MODEL // N_DEV, D_MODEL)], True))
    gx, h1, acc0 = _in_proj_bwd(x, dx1, dq, dkv, dz, dxbc, ddt, p["norm1"], scale1, shift1, w_in, tm)
    dproj = (dq, dkv, dz, dxbc, ddt)
    half = D_MODEL // 2
    slots = lambda g: g[:IN_W].reshape(N_DEV, IN_W // N_DEV, half)
    g_in_a, gathered = _wgrad(dproj, h1, IN_PAD, tw, "wgrad_in_a",
                              ([acc0, acc1, acc2, dcw, dvec, dd, dsink, drel, c], False), g_cols=(half, 0))
    g_in_b, (r_in_a,) = _wgrad(dproj, h1, IN_PAD, tw, "wgrad_in_b", ([slots(g_in_a)], True), g_cols=(half, 1))
    return gx, (r_in_a, slots(g_in_b)), (r_o, r_gu, r_d), gathered


def kernel(x, c, ada_w, ada_b, norm1, w_in, conv_w, conv_b, dt_bias, A_log, D_skip, sinks, attn_out_norm, ssm_out_norm, w_o, norm2, w_gate_up, w_down, rel_bias, final_norm, loss_target, m_ada_w, m_ada_b, m_norm1, m_w_in, m_conv_w, m_conv_b, m_dt_bias, m_A_log, m_D_skip, m_sinks, m_attn_out_norm, m_ssm_out_norm, m_w_o, m_norm2, m_w_gate_up, m_w_down, m_rel_bias, m_final_norm, v_ada_w, v_ada_b, v_norm1, v_w_in, v_conv_w, v_conv_b, v_dt_bias, v_A_log, v_D_skip, v_sinks, v_attn_out_norm, v_ssm_out_norm, v_w_o, v_norm2, v_w_gate_up, v_w_down, v_rel_bias, v_final_norm):
    two_d = lambda a: a if a.ndim == 2 else a.reshape(-1, a.shape[-1])
    small_params = dict(
        ada_b=(ada_b, m_ada_b, v_ada_b), norm1=(norm1, m_norm1, v_norm1), conv_w=(conv_w, m_conv_w, v_conv_w),
        conv_b=(conv_b, m_conv_b, v_conv_b), dt_bias=(dt_bias, m_dt_bias, v_dt_bias), A_log=(A_log, m_A_log, v_A_log),
        D_skip=(D_skip, m_D_skip, v_D_skip), sinks=(sinks, m_sinks, v_sinks),
        attn_out_norm=(attn_out_norm, m_attn_out_norm, v_attn_out_norm),
        ssm_out_norm=(ssm_out_norm, m_ssm_out_norm, v_ssm_out_norm), norm2=(norm2, m_norm2, v_norm2),
        rel_bias=(rel_bias, m_rel_bias, v_rel_bias), final_norm=(final_norm, m_final_norm, v_final_norm))
    small_params = {k: tuple(two_d(a) for a in v) for k, v in small_params.items()}
    S = x.shape[1]
    xs, tgt = x.reshape(S, D_MODEL), loss_target.reshape(S, D_MODEL)
    ada_w2 = ada_w[0]
    chunk = ada_w2.shape[1]
    t_in = [jnp.transpose(a[0]) for a in (w_in, m_w_in, v_w_in)]
    t_gu = [jnp.transpose(a[0]) for a in (w_gate_up, m_w_gate_up, v_w_gate_up)]

    mod, (g_in, g_cw) = _mod_and_gather(c, ada_w2, ada_b.reshape(N_DEV, chunk), [t_in[0].astype(WIRE_DTYPE), conv_w[0]])
    mod = mod.reshape(6, D_MODEL)
    w_in_full = jnp.pad(g_in.reshape(IN_W, D_MODEL), ((0, IN_PAD - IN_W), (0, 0)))
    conv_w_full = jnp.transpose(g_cw, (1, 0, 2)).reshape(4, XBC_W)

    p = {k: v[0] for k, v in small_params.items()}
    gx, (r_in_a, gw_in_b), (r_o, r_gu, r_d), gathered = _local_step(
        xs, tgt, c, mod, w_in_full, conv_w_full, w_o[0].astype(WIRE_DTYPE), t_gu[0].astype(WIRE_DTYPE),
        w_down[0].astype(WIRE_DTYPE), p)

    (u_gu, u_d, u_o), (r_in_b,) = _reduce_adamw_hosting(
        [r_gu, r_d, r_o], [tuple(t_gu), (w_down[0], m_w_down[0], v_w_down[0]), (w_o[0], m_w_o[0], v_w_o[0])],
        "adamw_big", ([gw_in_b], True))
    r_in = jnp.concatenate([r_in_a, r_in_b], axis=2)

    small, loss, c_all, dmod_all = _small_update(gathered, small_params)

    big = {
        "ada_w": _ada_w_update(c_all, dmod_all, ada_w2, m_ada_w[0], v_ada_w[0]),
        "w_in": [jnp.transpose(a) for a in _reduce_adamw(r_in, *t_in, "adamw_w_in")],
        "w_o": u_o,
        "w_gate_up": [jnp.transpose(a) for a in u_gu],
        "w_down": u_d,
    }
    big.update(small)

    order = ['ada_w', 'ada_b', 'norm1', 'w_in', 'conv_w', 'conv_b', 'dt_bias', 'A_log', 'D_skip', 'sinks',
             'attn_out_norm', 'ssm_out_norm', 'w_o', 'norm2', 'w_gate_up', 'w_down', 'rel_bias', 'final_norm']
    shapes = dict(ada_w=ada_w.shape, ada_b=ada_b.shape, norm1=norm1.shape, w_in=w_in.shape, conv_w=conv_w.shape,
                  conv_b=conv_b.shape, dt_bias=dt_bias.shape, A_log=A_log.shape, D_skip=D_skip.shape,
                  sinks=sinks.shape, attn_out_norm=attn_out_norm.shape, ssm_out_norm=ssm_out_norm.shape,
                  w_o=w_o.shape, norm2=norm2.shape, w_gate_up=w_gate_up.shape, w_down=w_down.shape,
                  rel_bias=rel_bias.shape, final_norm=final_norm.shape)
    outs = [[], [], [], []]
    for name in order:
        for kind in range(4):
            outs[kind].append(big[name][kind].reshape(shapes[name]))
    return (loss.reshape(()), gx.reshape(x.shape), *outs[0], *outs[1], *outs[2], *outs[3])
```

```python
import functools

import numpy as np
import jax
import jax.numpy as jnp
from jax import lax
from jax.experimental import pallas as pl
from jax.experimental.pallas import tpu as pltpu

F32 = jnp.float32
MXU_DTYPE = jnp.bfloat16
WIRE_DTYPE = jnp.bfloat16
HI = lax.Precision.HIGHEST
MESH = pl.DeviceIdType.MESH
N_DEV = 8

D_MODEL = 1024
ATTN_W = 512
KV_W = 128
SSM_W = 512
XBC_W = 1024
N_HEADS = 8
D_STATE = 128
D_FF = 2816
IN_W = 2312
IN_PAD = 2432
BLK = 128
N_BUCKETS = 32
EPS = 1e-6
LANE = 128
HALF = 64

ADAM_LR, ADAM_B1, ADAM_B2, ADAM_EPS, ADAM_WD, ADAM_STEP = 0.001, 0.9, 0.999, 1e-08, 0.01, 10

VMEM_BIG = 56 * 1024 * 1024
WD_CUT = 288
WGU_CUTS = (240, 496)


def _cparams(vmem=None):
    if vmem is None:
        return pltpu.CompilerParams()
    return pltpu.CompilerParams(vmem_limit_bytes=vmem)


def _mm(a, b):
    return jnp.dot(a.astype(MXU_DTYPE), b.astype(MXU_DTYPE), preferred_element_type=F32)


def _mm_nt(a, b):
    return lax.dot_general(a.astype(MXU_DTYPE), b.astype(MXU_DTYPE), (((1,), (1,)), ((), ())),
                           preferred_element_type=F32)


def _mm_tn(a, b):
    return lax.dot_general(a.astype(MXU_DTYPE), b.astype(MXU_DTYPE), (((0,), (0,)), ((), ())),
                           preferred_element_type=F32)


def _mm_hi(a, b):
    return jnp.dot(a, b, precision=HI, preferred_element_type=F32)


def _silu(x):
    return x * jax.nn.sigmoid(x)


def _softplus(x):
    return jnp.maximum(x, 0.0) + jnp.log1p(jnp.exp(-jnp.abs(x)))


def _rms(x, g, n):
    return x * lax.rsqrt(jnp.sum(x * x, axis=-1, keepdims=True) * (1.0 / n) + EPS) * g


def _modnorm(x, g, scale, shift):
    return _rms(x, g, x.shape[-1]) * (1.0 + scale) + shift


def _modnorm_parts(x):
    r = lax.rsqrt(jnp.sum(x * x, axis=-1, keepdims=True) * (1.0 / x.shape[-1]) + EPS)
    return r, x * r


def _modnorm_bwd(r, xhat, g, scale, dy):
    dyg = dy * (g * (1.0 + scale))
    c = jnp.sum(dyg * xhat, axis=-1, keepdims=True) * (1.0 / xhat.shape[-1])
    dx = r * (dyg - xhat * c)
    ct = jnp.sum(dy * xhat, axis=0, keepdims=True)
    return dx, ct * (1.0 + scale), ct * g, jnp.sum(dy, axis=0, keepdims=True)


def _lane_iota(shape):
    return lax.broadcasted_iota(jnp.int32, shape, len(shape) - 1)


def _split_pair(t):
    lane = _lane_iota(t.shape)
    lo = jnp.where(lane < HALF, t, 0.0)
    hi = pltpu.roll(jnp.where(lane >= HALF, t, 0.0), HALF, 1)
    return lo, hi


def _join_pair(lo, hi):
    lane = _lane_iota(lo.shape)
    return jnp.where(lane < HALF, lo, pltpu.roll(hi, HALF, 1))


def _split_heads(t, n_pairs):
    out = []
    for p in range(n_pairs):
        out.extend(_split_pair(t[:, p * LANE:(p + 1) * LANE]))
    return out


def _join_heads(hs):
    return jnp.concatenate([_join_pair(hs[2 * p], hs[2 * p + 1]) for p in range(len(hs) // 2)], axis=1)


def _t5_bucket_table():
    dist = np.arange(BLK)[:, None] + BLK - np.arange(2 * BLK)[None, :]
    n = np.maximum(dist, 0)
    max_exact = N_BUCKETS // 2
    large = max_exact + (np.log(np.maximum(n, 1) / max_exact) / np.log(128 / max_exact)
                         * (N_BUCKETS - max_exact)).astype(np.int32)
    large = np.minimum(large, N_BUCKETS - 1)
    return np.where(n < max_exact, n, large).astype(np.int32)


def _my_pos():
    return lax.axis_index("x"), lax.axis_index("y"), lax.axis_index("c")


def _peer(k):
    x, y, c = _my_pos()
    return (1 - x if k & 4 else x, 1 - y if k & 2 else y, 1 - c if k & 1 else c)


def _lin(pos):
    return 4 * pos[0] + 2 * pos[1] + pos[2]


def _xchg_copies(ins, outs, sems, scatter):
    local_sem, send_sem, recv_sem = sems
    me = _lin(_my_pos())
    local, remote = [], []
    for a in range(len(ins)):
        src = ins[a].at[me] if scatter else ins[a]
        local.append(pltpu.make_async_copy(src, outs[a].at[me], local_sem.at[a]))
    for k in range(1, N_DEV):
        peer = _peer(k)
        for a in range(len(ins)):
            src = ins[a].at[_lin(peer)] if scatter else ins[a]
            remote.append(pltpu.make_async_remote_copy(src, outs[a].at[me], send_sem.at[a, k - 1],
                                                       recv_sem.at[a, k - 1], device_id=peer, device_id_type=MESH))
    return local, remote


def _xchg_start(ins, outs, sems, scatter):
    local, remote = _xchg_copies(ins, outs, sems, scatter)
    for cp in local + remote:
        cp.start()


def _xchg_wait(ins, outs, sems, scatter):
    local, remote = _xchg_copies(ins, outs, sems, scatter)
    for cp in local:
        cp.wait()
    for cp in remote:
        cp.wait_send()
        cp.wait_recv()


def _xchg_shapes(arrs, scatter):
    n = len(arrs)
    if scatter:
        out_shape = [jax.ShapeDtypeStruct(a.shape, a.dtype) for a in arrs]
    else:
        out_shape = [jax.ShapeDtypeStruct((N_DEV,) + a.shape, a.dtype) for a in arrs]
    sems = [pltpu.SemaphoreType.DMA((n,)), pltpu.SemaphoreType.DMA((n, N_DEV - 1)),
            pltpu.SemaphoreType.DMA((n, N_DEV - 1))]
    return out_shape, sems


_CHIPS = (2, 4, 6)


def _g2_sems(n):
    dma = pltpu.SemaphoreType.DMA
    return [dma((n,)), dma((n, N_DEV)), dma((n, N_DEV)), dma((n, len(_CHIPS))), dma((n, len(_CHIPS)))]


class _TwoLevelGather:
    def __init__(self, ins, outs, sems):
        self.ins, self.outs = ins, outs
        self.local_sem, self.send_sem, self.recv_sem, self.fsend_sem, self.frecv_sem = sems
        self.n = len(ins)

    def _direct(self, a, k):
        return pltpu.make_async_remote_copy(self.ins[a], self.outs[a].at[_lin(_my_pos())], self.send_sem.at[a, k],
                                            self.recv_sem.at[a, k], device_id=_peer(k), device_id_type=MESH)

    def _handed_on(self, a, j, origin):
        slot = self.outs[a].at[origin]
        return pltpu.make_async_remote_copy(slot, slot, self.fsend_sem.at[a, j], self.frecv_sem.at[a, j],
                                            device_id=_peer(1), device_id_type=MESH)

    def _local(self, a):
        return pltpu.make_async_copy(self.ins[a], self.outs[a].at[_lin(_my_pos())], self.local_sem.at[a])

    def start(self):
        for a in range(self.n):
            self._local(a).start()
        for k in (1,) + _CHIPS:
            for a in range(self.n):
                self._direct(a, k).start()

    def forward(self):
        for j, k in enumerate(_CHIPS):
            for a in range(self.n):
                self._direct(a, k).wait_recv()
                self._handed_on(a, j, _lin(_peer(k))).start()

    def finish(self):
        for a in range(self.n):
            self._direct(a, 1).wait_recv()
            for j, k in enumerate(_CHIPS):
                self._handed_on(a, j, _lin(_peer(k ^ 1))).wait_recv()
            self._local(a).wait()
            for k in (1,) + _CHIPS:
                self._direct(a, k).wait_send()
            for j, k in enumerate(_CHIPS):
                self._handed_on(a, j, _lin(_peer(k))).wait_send()


def _mod_and_gather(c, ada_w, ada_b8, arrs):
    n = len(arrs)
    chunk = ada_w.shape[1]
    out_shape = [jax.ShapeDtypeStruct((N_DEV, 1, chunk), F32)]
    out_shape += [jax.ShapeDtypeStruct((N_DEV,) + a.shape, a.dtype) for a in arrs]

    def modulation(c_ref, w_ref, b_ref, out_ref, cbuf, part, s1, r1, s2, r2):
        me = _lin(_my_pos())
        first = []
        for k in range(1, N_DEV):
            cp = pltpu.make_async_remote_copy(c_ref, cbuf.at[me], s1.at[k - 1], r1.at[k - 1],
                                              device_id=_peer(k), device_id_type=MESH)
            cp.start()
            first.append(cp)
        cbuf[me] = c_ref[...]
        for cp in first:
            cp.wait_send()
            cp.wait_recv()
        cond = _silu(jnp.concatenate([cbuf[i] for i in range(N_DEV)], axis=0))
        mod = _mm_hi(cond, w_ref[...]) + b_ref[pl.ds(me, 1), :]
        for j in range(N_DEV):
            part[j] = mod[j:j + 1, :]
        second = []
        for k in range(1, N_DEV):
            peer = _peer(k)
            cp = pltpu.make_async_remote_copy(part.at[_lin(peer)], out_ref.at[me], s2.at[k - 1], r2.at[k - 1],
                                              device_id=peer, device_id_type=MESH)
            cp.start()
            second.append(cp)
        out_ref[me] = part[me]
        for cp in second:
            cp.wait_send()
            cp.wait_recv()

    def body(*refs):
        c_ref, w_ref, b_ref = refs[:3]
        ins = refs[3:3 + n]
        mod_ref = refs[3 + n]
        outs = refs[4 + n:4 + 2 * n]
        cbuf, part, s1, r1, s2, r2 = refs[4 + 2 * n:10 + 2 * n]
        gather = _TwoLevelGather(ins, outs, refs[10 + 2 * n:])
        gather.start()
        modulation(c_ref, w_ref, b_ref, mod_ref, cbuf, part, s1, r1, s2, r2)
        gather.forward()
        gather.finish()

    hbm = pl.BlockSpec(memory_space=pltpu.HBM)
    vm = pl.BlockSpec(memory_space=pltpu.VMEM)
    dma = pltpu.SemaphoreType.DMA
    res = pl.pallas_call(
        body, name="mod_and_gather", out_shape=out_shape, in_specs=[vm, vm, vm] + [hbm] * n,
        out_specs=[vm] + [hbm] * n,
        scratch_shapes=[pltpu.VMEM((N_DEV, 1, D_MODEL), F32), pltpu.VMEM((N_DEV, 1, chunk), F32)]
        + [dma((N_DEV - 1,))] * 4 + _g2_sems(n),
    )(c, ada_w, ada_b8, *arrs)
    return res[0], res[1:]


def _hosted_call(body, name, grid, in_specs, out_specs, out_shape, scratch_shapes, args, xchg, cparams):
    arrs, scatter = xchg
    grid = (grid,) if isinstance(grid, int) else tuple(grid)
    n, n_in, n_out, n_scr = len(arrs), len(in_specs), len(out_specs), len(scratch_shapes)
    two_level = scatter == "two-level"
    x_shape, x_sems = _xchg_shapes(arrs, False if two_level else scatter)
    if two_level:
        x_sems = _g2_sems(n)
    n_steps = int(np.prod(grid))

    def hosted(*refs):
        ins, refs = refs[:n_in], refs[n_in:]
        x_in, refs = refs[:n], refs[n:]
        outs, refs = refs[:n_out], refs[n_out:]
        x_out, refs = refs[:n], refs[n:]
        scr, sems = refs[:n_scr], refs[n_scr:]
        step = pl.program_id(0)
        for d in range(1, len(grid)):
            step = step * grid[d] + pl.program_id(d)

        @pl.when(step == 0)
        def _():
            if two_level:
                _TwoLevelGather(x_in, x_out, sems).start()
            else:
                _xchg_start(x_in, x_out, sems, scatter)

        if two_level:
            @pl.when(step == (2 * n_steps) // 3)
            def _():
                _TwoLevelGather(x_in, x_out, sems).forward()

        body(*ins, *outs, *scr)

        @pl.when(step == n_steps - 1)
        def _():
            if two_level:
                _TwoLevelGather(x_in, x_out, sems).finish()
            else:
                _xchg_wait(x_in, x_out, sems, scatter)

    hbm = pl.BlockSpec(memory_space=pltpu.HBM)
    res = pl.pallas_call(
        hosted, name=name, grid=grid, in_specs=list(in_specs) + [hbm] * n,
        out_specs=list(out_specs) + [hbm] * n, out_shape=list(out_shape) + x_shape,
        scratch_shapes=list(scratch_shapes) + x_sems, compiler_params=cparams,
    )(*args, *arrs)
    return res[:n_out], res[n_out:]


def _row(i):
    return (i, 0)


def _fixed(i):
    return (0, 0)


def _in_proj_fwd(x, norm1, scale1, shift1, w_in, tm, xchg):
    S = x.shape[0]

    def body(x_ref, n_ref, sc_ref, sh_ref, w_ref, qkv_ref, z_ref, xbc_ref, dt_ref):
        h = _modnorm(x_ref[...], n_ref[...], sc_ref[...], sh_ref[...])
        p = _mm_nt(h, w_ref[...])
        qkv_ref[...] = p[:, :768].astype(qkv_ref.dtype)
        z_ref[...] = p[:, 768:1280]
        xbc_ref[...] = p[:, 1280:2304]
        dt_ref[...] = p[:, 2304:IN_PAD]

    vec = pl.BlockSpec((1, D_MODEL), _fixed)
    return _hosted_call(
        body, "in_proj_fwd", S // tm,
        in_specs=[pl.BlockSpec((tm, D_MODEL), _row), vec, vec, vec, pl.BlockSpec((IN_PAD, D_MODEL), _fixed)],
        out_specs=[pl.BlockSpec((tm, 768), _row), pl.BlockSpec((tm, SSM_W), _row),
                   pl.BlockSpec((tm, XBC_W), _row), pl.BlockSpec((tm, LANE), _row)],
        out_shape=[jax.ShapeDtypeStruct((S, 768), MXU_DTYPE), jax.ShapeDtypeStruct((S, SSM_W), F32),
                   jax.ShapeDtypeStruct((S, XBC_W), F32), jax.ShapeDtypeStruct((S, LANE), F32)],
        scratch_shapes=[], args=(x, norm1, scale1, shift1, w_in), xchg=xchg, cparams=_cparams(VMEM_BIG),
    )


def _in_proj_bwd(x, dx1, dq, dkv, dz, dxbc, ddt, norm1, scale1, shift1, w_in, tm):
    S = x.shape[0]

    def body(x_ref, dx1_ref, dq_ref, dkv_ref, dz_ref, dxbc_ref, ddt_ref, n_ref, sc_ref, sh_ref, w_ref,
             gx_ref, h_ref, acc_ref):
        @pl.when(pl.program_id(0) == 0)
        def _():
            acc_ref[...] = jnp.zeros_like(acc_ref)

        dp = jnp.concatenate([dq_ref[...], dkv_ref[...], dz_ref[...], dxbc_ref[...], ddt_ref[...]], axis=1)
        dh = _mm(dp, w_ref[...])
        r, xhat = _modnorm_parts(x_ref[...])
        dx, dn, dsc, dsh = _modnorm_bwd(r, xhat, n_ref[...], sc_ref[...], dh)
        gx_ref[...] = dx1_ref[...] + dx
        h_ref[...] = (xhat * n_ref[...] * (1.0 + sc_ref[...]) + sh_ref[...]).astype(h_ref.dtype)
        acc_ref[0:1, :] += dn
        acc_ref[1:2, :] += dsc
        acc_ref[2:3, :] += dsh

    vec = pl.BlockSpec((1, D_MODEL), _fixed)
    return pl.pallas_call(
        body, name="in_proj_bwd", grid=(S // tm,),
        in_specs=[pl.BlockSpec((tm, D_MODEL), _row), pl.BlockSpec((tm, D_MODEL), _row),
                  pl.BlockSpec((tm, ATTN_W), _row), pl.BlockSpec((tm, 2 * KV_W), _row),
                  pl.BlockSpec((tm, SSM_W), _row), pl.BlockSpec((tm, XBC_W), _row), pl.BlockSpec((tm, LANE), _row),
                  vec, vec, vec, pl.BlockSpec((IN_PAD, D_MODEL), _fixed)],
        out_specs=[pl.BlockSpec((tm, D_MODEL), _row), pl.BlockSpec((tm, D_MODEL), _row),
                   pl.BlockSpec((8, D_MODEL), _fixed)],
        out_shape=[jax.ShapeDtypeStruct((S, D_MODEL), F32), jax.ShapeDtypeStruct((S, D_MODEL), MXU_DTYPE),
                   jax.ShapeDtypeStruct((8, D_MODEL), F32)],
        compiler_params=_cparams(VMEM_BIG),
    )(x, dx1, dq, dkv, dz, dxbc, ddt, norm1, scale1, shift1, w_in)


def _out_stage(ya, ys0, ys1, z0, z1, an, sn0, sn1):
    half = SSM_W // 2
    a = _rms(ya, an, ATTN_W)
    g0 = _rms(ys0 * _silu(z0), sn0, half)
    g1 = _rms(ys1 * _silu(z1), sn1, half)
    return jnp.concatenate([a, g0, g1], axis=1)


def _out_stage_args(ya_ref, ys_ref, z_ref, an_ref, sn_ref):
    half = SSM_W // 2
    return (ya_ref[...], ys_ref[:, :half], ys_ref[:, half:], z_ref[:, :half], z_ref[:, half:],
            an_ref[...], sn_ref[:, :half], sn_ref[:, half:])


def _out_proj_fwd(x, ya, ys, z, an, sn, gate1, w_o, tm, xchg):
    S = x.shape[0]

    def body(x_ref, ya_ref, ys_ref, z_ref, an_ref, sn_ref, g_ref, w_ref, x1_ref):
        u = _out_stage(*_out_stage_args(ya_ref, ys_ref, z_ref, an_ref, sn_ref))
        x1_ref[...] = x_ref[...] + g_ref[...] * _mm(u, w_ref[...])

    half = pl.BlockSpec((tm, ATTN_W), _row)
    hvec = pl.BlockSpec((1, ATTN_W), _fixed)
    (x1,), x_out = _hosted_call(
        body, "out_proj_fwd", S // tm,
        in_specs=[pl.BlockSpec((tm, D_MODEL), _row), half, half, half, hvec, hvec,
                  pl.BlockSpec((1, D_MODEL), _fixed), pl.BlockSpec((D_MODEL, D_MODEL), _fixed)],
        out_specs=[pl.BlockSpec((tm, D_MODEL), _row)],
        out_shape=[jax.ShapeDtypeStruct((S, D_MODEL), F32)],
        scratch_shapes=[], args=(x, ya, ys, z, an, sn, gate1, w_o), xchg=xchg, cparams=_cparams(VMEM_BIG),
    )
    return x1, x_out


def _out_proj_bwd(dx1, ya, ys, z, an, sn, gate1, w_o, tm, xchg):
    S = dx1.shape[0]

    def body(dx1_ref, ya_ref, ys_ref, z_ref, an_ref, sn_ref, g_ref, w_ref,
             dya_ref, dys_ref, dz_ref, u_ref, dmix_ref, acc_ref):
        @pl.when(pl.program_id(0) == 0)
        def _():
            acc_ref[...] = jnp.zeros_like(acc_ref)

        u, vjp = jax.vjp(_out_stage, *_out_stage_args(ya_ref, ys_ref, z_ref, an_ref, sn_ref))
        dx1 = dx1_ref[...]
        mix = _mm(u, w_ref[...])
        dmix = dx1 * g_ref[...]
        du = _mm_nt(dmix, w_ref[...])
        dya, dys0, dys1, dz0, dz1, dan, dsn0, dsn1 = vjp(du)
        dya_ref[...] = dya
        dys_ref[...] = jnp.concatenate([dys0, dys1], axis=1)
        dz_ref[...] = jnp.concatenate([dz0, dz1], axis=1).astype(dz_ref.dtype)
        u_ref[...] = u.astype(u_ref.dtype)
        dmix_ref[...] = dmix.astype(dmix_ref.dtype)
        acc_ref[0:1, :] += jnp.sum(dx1 * mix, axis=0, keepdims=True)
        acc_ref[1:2, :] += jnp.concatenate([dan, dsn0, dsn1], axis=1)

    half = pl.BlockSpec((tm, ATTN_W), _row)
    hvec = pl.BlockSpec((1, ATTN_W), _fixed)
    full = pl.BlockSpec((tm, D_MODEL), _row)
    return _hosted_call(
        body, "out_proj_bwd", S // tm,
        in_specs=[full, half, half, half, hvec, hvec,
                  pl.BlockSpec((1, D_MODEL), _fixed), pl.BlockSpec((D_MODEL, D_MODEL), _fixed)],
        out_specs=[half, half, half, full, full, pl.BlockSpec((8, D_MODEL), _fixed)],
        out_shape=[jax.ShapeDtypeStruct((S, ATTN_W), F32)] * 2 + [jax.ShapeDtypeStruct((S, ATTN_W), MXU_DTYPE)]
        + [jax.ShapeDtypeStruct((S, D_MODEL), MXU_DTYPE)] * 2 + [jax.ShapeDtypeStruct((8, D_MODEL), F32)],
        scratch_shapes=[], args=(dx1, ya, ys, z, an, sn, gate1, w_o), xchg=xchg, cparams=_cparams(VMEM_BIG),
    )


def _loss_rows(x2, fn, tgt):
    y = _rms(x2, fn, D_MODEL)
    per_row = jnp.sum(jnp.square(y - tgt), axis=1, keepdims=True)
    return jnp.sum(per_row, axis=0, keepdims=True) * (0.5 / D_MODEL)


def _mlp_loss(x1, tgt, norm2, scale2, shift2, gate2, fnorm, w_gu, w_d, tm):
    S = x1.shape[0]
    n_pieces = len(w_gu) + len(w_d)

    def body(*refs):
        x1_ref, t_ref, n_ref, sc_ref, sh_ref, g_ref, fn_ref = refs[:7]
        piece_refs = refs[7:7 + n_pieces]
        dx1_ref, h_ref, dgu_ref, act_ref, dmlp_ref, acc_ref, wgu, wd, wsem = refs[7 + n_pieces:]

        @pl.when(pl.program_id(0) == 0)
        def _():
            acc_ref[...] = jnp.zeros_like(acc_ref)
            copies = []
            for dst, pieces in ((wgu, piece_refs[:len(w_gu)]), (wd, piece_refs[len(w_gu):])):
                shard = sum(p.shape[1] for p in pieces)
                off = 0
                for p in pieces:
                    for j in range(N_DEV):
                        copies.append(pltpu.make_async_copy(p.at[j], dst.at[pl.ds(j * shard + off, p.shape[1])],
                                                            wsem.at[len(copies)]))
                    off += p.shape[1]
            for cp in copies:
                cp.start()
            for cp in copies:
                cp.wait()

        x1 = x1_ref[...]
        gate2 = g_ref[...]
        h, vjp_h = jax.vjp(_modnorm, x1, n_ref[...], sc_ref[...], sh_ref[...])
        hb = h.astype(MXU_DTYPE)
        gu = _mm_nt(hb, wgu[...])
        g, u = gu[:, :D_FF], gu[:, D_FF:]
        sg = jax.nn.sigmoid(g)
        silu_g = g * sg
        act = (silu_g * u).astype(MXU_DTYPE)
        mlp = _mm(act, wd[...])
        x2 = x1 + gate2 * mlp
        loss, vjp_loss = jax.vjp(_loss_rows, x2, fn_ref[...], t_ref[...])
        dx2, dfn, _ = vjp_loss(jnp.ones((1, 1), F32))
        dmlp = (dx2 * gate2).astype(MXU_DTYPE)
        dact = _mm_nt(dmlp, wd[...])
        dg = dact * u * (sg * (1.0 + g * (1.0 - sg)))
        du = dact * silu_g
        dgu = jnp.concatenate([dg, du], axis=1).astype(MXU_DTYPE)
        dh = _mm(dgu, wgu[...])
        dx, dn, dsc, dsh = vjp_h(dh)
        dx1_ref[...] = dx2 + dx
        h_ref[...] = hb
        dgu_ref[...] = dgu
        act_ref[...] = act
        dmlp_ref[...] = dmlp
        acc_ref[0:1, :] += dn
        acc_ref[1:2, :] += dsc
        acc_ref[2:3, :] += dsh
        acc_ref[3:4, :] += jnp.sum(dx2 * mlp, axis=0, keepdims=True)
        acc_ref[4:5, :] += dfn
        acc_ref[5:6, :] += jnp.broadcast_to(loss, (1, D_MODEL))

    full = pl.BlockSpec((tm, D_MODEL), _row)
    vec = pl.BlockSpec((1, D_MODEL), _fixed)
    anyspec = pl.BlockSpec(memory_space=pl.ANY)
    return pl.pallas_call(
        body, name="mlp_loss", grid=(S // tm,),
        in_specs=[full, full, vec, vec, vec, vec, vec] + [anyspec] * n_pieces,
        out_specs=[full, full, pl.BlockSpec((tm, 2 * D_FF), _row), pl.BlockSpec((tm, D_FF), _row), full,
                   pl.BlockSpec((8, D_MODEL), _fixed)],
        out_shape=[jax.ShapeDtypeStruct((S, D_MODEL), F32), jax.ShapeDtypeStruct((S, D_MODEL), MXU_DTYPE),
                   jax.ShapeDtypeStruct((S, 2 * D_FF), MXU_DTYPE), jax.ShapeDtypeStruct((S, D_FF), MXU_DTYPE),
                   jax.ShapeDtypeStruct((S, D_MODEL), MXU_DTYPE), jax.ShapeDtypeStruct((8, D_MODEL), F32)],
        scratch_shapes=[pltpu.VMEM((2 * D_FF, D_MODEL), MXU_DTYPE), pltpu.VMEM((D_FF, D_MODEL), MXU_DTYPE),
                        pltpu.SemaphoreType.DMA((N_DEV * n_pieces,))],
        compiler_params=_cparams(VMEM_BIG),
    )(x1, tgt, norm2, scale2, shift2, gate2, fnorm, *w_gu, *w_d)


def _wgrad(a, g, tk, ts, name, xchg=None, g_cols=None):
    pieces = list(a) if isinstance(a, (list, tuple)) else [a]
    S = pieces[0].shape[0]
    K = sum(p.shape[1] for p in pieces)
    assert len(pieces) == 1 or tk == K
    N, col = (g.shape[1], 0) if g_cols is None else g_cols
    ns = S // ts
    n_a = len(pieces)

    def body(*refs):
        a_refs, (g_ref, o_ref, acc_ref) = refs[:n_a], refs[n_a:]
        s = pl.program_id(1)

        @pl.when(s == 0)
        def _():
            acc_ref[...] = jnp.zeros_like(acc_ref)

        a_blk = a_refs[0][...] if n_a == 1 else jnp.concatenate([r[...] for r in a_refs], axis=1)
        acc_ref[...] += _mm_tn(a_blk, g_ref[...])

        @pl.when(s == ns - 1)
        def _():
            o_ref[...] = acc_ref[...].astype(o_ref.dtype)

    if n_a == 1:
        in_specs = [pl.BlockSpec((ts, tk), lambda j, s: (s, j))]
    else:
        in_specs = [pl.BlockSpec((ts, p.shape[1]), lambda j, s: (s, 0)) for p in pieces]
    in_specs.append(pl.BlockSpec((ts, N), lambda j, s: (s, col)))
    out_spec = pl.BlockSpec((tk, N), lambda j, s: (j, 0))
    out_shape = jax.ShapeDtypeStruct((K, N), WIRE_DTYPE)
    scratch = [pltpu.VMEM((tk, N), F32)]
    args = (*pieces, g)
    if xchg is None:
        return pl.pallas_call(body, name=name, grid=(K // tk, ns), in_specs=in_specs, out_specs=out_spec,
                              out_shape=out_shape, scratch_shapes=scratch, compiler_params=_cparams(VMEM_BIG))(*args)
    (out,), x_out = _hosted_call(body, name, (K // tk, ns), in_specs, [out_spec], [out_shape], scratch, args, xchg,
                                 _cparams(VMEM_BIG))
    return out, x_out


SSD_CHUNKS_PER_STEP = 4
SSD_BWD_CHUNKS_PER_STEP = 2
ATTN_BLOCKS_PER_STEP = 4
MASKED = -1e30
QK_SCALE = HALF ** -0.5


def _attn_bias(buckets, rel_bias):
    def body(bk_ref, relb_ref, out_ref):
        bk = bk_ref[...]
        i = lax.broadcasted_iota(jnp.int32, (BLK, 2 * BLK), 0)
        j = lax.broadcasted_iota(jnp.int32, (BLK, 2 * BLK), 1)
        window = (j > i) & (j <= i + BLK)
        for h in range(N_HEADS):
            acc = jnp.zeros((BLK, 2 * BLK), F32)
            for b in range(N_BUCKETS):
                acc = jnp.where(bk == b, relb_ref[b, h], acc)
            out_ref[0, h] = jnp.where(window, acc, MASKED)
            out_ref[1, h] = jnp.where(window & (j >= BLK), acc, MASKED)

    return pl.pallas_call(
        body, name="attn_bias", out_shape=jax.ShapeDtypeStruct((2, N_HEADS, BLK, 2 * BLK), F32),
        in_specs=[pl.BlockSpec(memory_space=pltpu.VMEM), pl.BlockSpec(memory_space=pltpu.SMEM)],
    )(buckets, rel_bias)


def _attn_fwd(qkv, bias, sinks, xchg):
    S = qkv.shape[0]
    nb = S // BLK

    nq = ATTN_BLOCKS_PER_STEP if nb % ATTN_BLOCKS_PER_STEP == 0 else 1
    rows = nq * BLK

    def body(q_ref, kvp_ref, kvc_ref, bias_ref, sinks_ref, y_ref):
        i = pl.program_id(0)
        q = q_ref[...].astype(F32) * QK_SCALE
        kv = jnp.concatenate([kvp_ref[...], kvc_ref[...]], axis=0).astype(F32)
        k_lo, k_hi = _split_pair(kv[:, :LANE])
        v_lo, v_hi = _split_pair(kv[:, LANE:])
        bands = [[t[b * BLK:(b + 2) * BLK].astype(MXU_DTYPE) for t in (k_lo, k_hi, v_lo, v_hi)] for b in range(nq)]
        q_heads = [_split_heads(q[b * BLK:(b + 1) * BLK], 4) for b in range(nq)]
        first = [jnp.where(i == 0, 1, 0) if b == 0 else 0 for b in range(nq)]
        items = [(b, h) for b in range(nq) for h in range(N_HEADS)]
        s = [_mm_nt(q_heads[b][h].astype(MXU_DTYPE), bands[b][h // 4]) + bias_ref[first[b], h] for b, h in items]
        m = [jnp.maximum(jnp.max(s[n], axis=-1, keepdims=True), sinks_ref[h]) for n, (b, h) in enumerate(items)]
        p = [jnp.exp(s[n] - m[n]) for n in range(len(items))]
        rinv = [1.0 / (jnp.sum(p[n], axis=-1, keepdims=True) + jnp.exp(sinks_ref[h] - m[n]))
                for n, (b, h) in enumerate(items)]
        out = [_mm(p[n], bands[b][2 + h // 4]) * rinv[n] for n, (b, h) in enumerate(items)]
        y_ref[...] = jnp.concatenate([_join_heads(out[b * N_HEADS:(b + 1) * N_HEADS]) for b in range(nq)], axis=0)

    smem = pl.BlockSpec(memory_space=pltpu.SMEM)
    return _hosted_call(
        body, "attn_fwd", nb // nq,
        in_specs=[pl.BlockSpec((rows, ATTN_W), _row),
                  pl.BlockSpec((BLK, 2 * KV_W), lambda i: (jnp.maximum(i * nq - 1, 0), 2)),
                  pl.BlockSpec((rows, 2 * KV_W), lambda i: (i, 2)),
                  pl.BlockSpec((2, N_HEADS, BLK, 2 * BLK), lambda i: (0, 0, 0, 0)), smem],
        out_specs=[pl.BlockSpec((rows, ATTN_W), _row)],
        out_shape=[jax.ShapeDtypeStruct((S, ATTN_W), F32)],
        scratch_shapes=[],
        args=(qkv, qkv, qkv, bias, sinks), xchg=xchg, cparams=_cparams(),
    )


def _attn_bwd(qkv, y, dy, bias, sinks, xchg):
    S = qkv.shape[0]
    nb = S // BLK
    nq = ATTN_BLOCKS_PER_STEP if nb % ATTN_BLOCKS_PER_STEP == 0 else 1
    rows, n_steps = nq * BLK, nb // nq

    def body(q_ref, kvp_ref, kvc_ref, y_ref, dy_ref, bias_ref, sinks_ref, dq_ref, dkv_ref, dbias_ref, dsk_ref, carry_ref):
        i = pl.program_id(0)

        @pl.when(i == 0)
        def _():
            dbias_ref[...] = jnp.zeros_like(dbias_ref)
            dsk_ref[...] = jnp.zeros_like(dsk_ref)
            carry_ref[...] = jnp.zeros_like(carry_ref)

        q = q_ref[...].astype(F32) * QK_SCALE
        kv = jnp.concatenate([kvp_ref[...], kvc_ref[...]], axis=0).astype(F32)
        k_lo, k_hi = _split_pair(kv[:, :LANE])
        v_lo, v_hi = _split_pair(kv[:, LANE:])
        bands = [[t[b * BLK:(b + 2) * BLK].astype(MXU_DTYPE) for t in (k_lo, k_hi, v_lo, v_hi)] for b in range(nq)]
        rows_of = lambda ref, b: ref[b * BLK:(b + 1) * BLK, :]
        first = [jnp.where(i == n_steps - 1, 1, 0) if b == 0 else 0 for b in range(nq)]
        items = [(b, h) for b in range(nq) for h in range(N_HEADS)]
        at = lambda b, h: b * N_HEADS + h
        q_heads = [hd for b in range(nq) for hd in _split_heads(q[b * BLK:(b + 1) * BLK], 4)]
        y_heads = [hd for b in range(nq) for hd in _split_heads(rows_of(y_ref, b), 4)]
        dy_heads = [hd for b in range(nq) for hd in _split_heads(rows_of(dy_ref, b), 4)]
        qs = [q_heads[n].astype(MXU_DTYPE) for n in range(len(items))]
        s = [_mm_nt(qs[at(b, h)], bands[b][h // 4]) + bias_ref[first[b], h] for b, h in items]
        m = [jnp.maximum(jnp.max(s[at(b, h)], axis=-1, keepdims=True), sinks_ref[h]) for b, h in items]
        p = [jnp.exp(s[n] - m[n]) for n in range(len(items))]
        esink = [jnp.exp(sinks_ref[h] - m[at(b, h)]) for b, h in items]
        rinv = [1.0 / (jnp.sum(p[n], axis=-1, keepdims=True) + esink[n]) for n in range(len(items))]
        t = [dy_heads[n] * rinv[n] for n in range(len(items))]
        delta = [jnp.sum(t[n] * y_heads[n], axis=-1, keepdims=True) for n in range(len(items))]
        tb = [t[n].astype(MXU_DTYPE) for n in range(len(items))]
        dp = [_mm_nt(tb[at(b, h)], bands[b][2 + h // 4]) for b, h in items]
        ds = [p[n] * (dp[n] - delta[n]) for n in range(len(items))]
        for h in range(N_HEADS):
            ds_h, dsk_h = ds[at(0, h)], esink[at(0, h)] * delta[at(0, h)]
            for b in range(1, nq):
                ds_h = ds_h + ds[at(b, h)]
                dsk_h = dsk_h + esink[at(b, h)] * delta[at(b, h)]
            dbias_ref[h] += ds_h
            dsk_ref[h] -= dsk_h
        dsb = [ds[n].astype(MXU_DTYPE) for n in range(len(items))]
        pb = [p[n].astype(MXU_DTYPE) for n in range(len(items))]
        dq_heads = [_mm(dsb[at(b, h)], bands[b][h // 4]) * QK_SCALE for b, h in items]
        grp = lambda lst, b, g: jnp.concatenate(lst[at(b, 4 * g):at(b, 4 * g) + 4], axis=0)
        dk_pads = [[_mm_tn(grp(dsb, b, g), grp(qs, b, g)) for g in range(2)] for b in range(nq)]
        dv_pads = [[_mm_tn(grp(pb, b, g), grp(tb, b, g)) for g in range(2)] for b in range(nq)]
        dq_ref[...] = jnp.concatenate([_join_heads(dq_heads[b * N_HEADS:(b + 1) * N_HEADS]) for b in range(nq)],
                                      axis=0).astype(dq_ref.dtype)
        part = lambda b, lo: jnp.concatenate(
            [_join_pair(d[b][0][lo:lo + BLK], d[b][1][lo:lo + BLK]) for d in (dk_pads, dv_pads)], axis=1)
        dkv = [part(b, BLK) + (part(b + 1, 0) if b + 1 < nq else carry_ref[...]) for b in range(nq)]
        dkv_ref[...] = jnp.concatenate(dkv, axis=0).astype(dkv_ref.dtype)
        carry_ref[...] = part(0, 0)

    smem = pl.BlockSpec(memory_space=pltpu.SMEM)
    rev = lambda i: (n_steps - 1 - i, 0)
    return _hosted_call(
        body, "attn_bwd", n_steps,
        in_specs=[pl.BlockSpec((rows, ATTN_W), rev),
                  pl.BlockSpec((BLK, 2 * KV_W), lambda i: (jnp.maximum((n_steps - 1 - i) * nq - 1, 0), 2)),
                  pl.BlockSpec((rows, 2 * KV_W), lambda i: (n_steps - 1 - i, 2)),
                  pl.BlockSpec((rows, ATTN_W), rev), pl.BlockSpec((rows, ATTN_W), rev),
                  pl.BlockSpec((2, N_HEADS, BLK, 2 * BLK), lambda i: (0, 0, 0, 0)), smem],
        out_specs=[pl.BlockSpec((rows, ATTN_W), rev), pl.BlockSpec((rows, 2 * KV_W), rev),
                   pl.BlockSpec((N_HEADS, BLK, 2 * BLK), lambda i: (0, 0, 0)),
                   pl.BlockSpec((N_HEADS, BLK, 1), lambda i: (0, 0, 0))],
        out_shape=[jax.ShapeDtypeStruct((S, ATTN_W), MXU_DTYPE), jax.ShapeDtypeStruct((S, 2 * KV_W), MXU_DTYPE),
                   jax.ShapeDtypeStruct((N_HEADS, BLK, 2 * BLK), F32), jax.ShapeDtypeStruct((N_HEADS, BLK, 1), F32)],
        scratch_shapes=[pltpu.VMEM((BLK, 2 * KV_W), F32)],
        args=(qkv, qkv, qkv, y, dy, bias, sinks), xchg=xchg, cparams=_cparams(),
    )


def _attn_finish(dbias, dsk, buckets):
    def body(db_ref, dsk_ref, bk_ref, drel_ref, dsink_ref):
        bk = bk_ref[...]
        r = lax.broadcasted_iota(jnp.int32, (N_BUCKETS, LANE), 0)
        l = lax.broadcasted_iota(jnp.int32, (N_BUCKETS, LANE), 1)
        row = lax.broadcasted_iota(jnp.int32, (N_HEADS, LANE), 0)
        res = jnp.zeros((N_BUCKETS, LANE), F32)
        dsink = jnp.zeros((N_HEADS, LANE), F32)
        for h in range(N_HEADS):
            db = db_ref[h]
            for b in range(N_BUCKETS):
                v = jnp.sum(jnp.sum(jnp.where(bk == b, db, 0.0), axis=1, keepdims=True), axis=0, keepdims=True)
                res = res + jnp.where((r == b) & (l == h), v, 0.0)
            dsink = dsink + jnp.where(row == h, jnp.sum(dsk_ref[h], axis=0, keepdims=True), 0.0)
        drel_ref[...] = res
        dsink_ref[...] = dsink

    return pl.pallas_call(body, name="attn_finish",
                          out_shape=[jax.ShapeDtypeStruct((N_BUCKETS, LANE), F32),
                                     jax.ShapeDtypeStruct((N_HEADS, LANE), F32)])(dbias, dsk, buckets)


def _ssd_consts():
    r = lax.broadcasted_iota(jnp.int32, (BLK, BLK), 0)
    c = lax.broadcasted_iota(jnp.int32, (BLK, BLK), 1)
    causal = c <= r
    upper = (r <= c).astype(F32)
    last = r == BLK - 1
    head = lax.broadcasted_iota(jnp.int32, (N_HEADS, BLK), 0)
    return causal, upper, last, head


def _ssd_chunks(xs, bg, cg, dt_raw_t, prev0, dtb, alog, d_rows, consts):
    causal, upper, last, head = consts
    nq = len(xs)
    items = [(c, h) for c in range(nq) for h in range(N_HEADS)]
    at = lambda c, h: c * N_HEADS + h
    a_neg = -jnp.exp(alog)
    dt_t = [_softplus(dt_raw_t[c] + dtb) for c in range(nq)]
    acs_t = [_mm_hi(dt_t[c] * a_neg, upper) for c in range(nq)]
    cb = [[_mm_nt(cg[c][g], bg[c][g]) for g in range(2)] for c in range(nq)]
    pick = lambda t, h: jnp.sum(jnp.where(head == h, t, 0.0), axis=0, keepdims=True)
    dt_row = [pick(dt_t[c], h) for c, h in items]
    a_row = [pick(acs_t[c], h) for c, h in items]
    a_rb = [jnp.broadcast_to(a_row[n], (BLK, BLK)) for n in range(len(items))]
    a_b = [a_rb[n].T for n in range(len(items))]
    a_last = [jnp.sum(jnp.where(last, a_b[n], 0.0), axis=0, keepdims=True) for n in range(len(items))]
    w = [cb[c][h // 4] * jnp.exp(jnp.where(causal, a_b[at(c, h)] - a_rb[at(c, h)], -1e30)) * dt_row[at(c, h)]
         for c, h in items]
    f_b = [jnp.broadcast_to(dt_row[n] * jnp.exp(a_last[n] - a_row[n]), (BLK, BLK)).T for n in range(len(items))]
    y_in = [_mm(w[at(c, h)], xs[c][h]) for c, h in items]
    st = [_mm_tn(bg[c][h // 4], xs[c][h] * f_b[at(c, h)]) for c, h in items]
    e_b = [jnp.exp(a_b[n]) for n in range(len(items))]
    states = [list(prev0)]
    for c in range(nq):
        states.append([states[c][h] * jnp.exp(a_last[at(c, h)]) + st[at(c, h)] for h in range(N_HEADS)])
    y_off = [_mm(cg[c][h // 4], states[c][h]) * e_b[at(c, h)] for c, h in items]
    ys = [[y_in[at(c, h)] + y_off[at(c, h)] + d_rows[h] * xs[c][h] for h in range(N_HEADS)] for c in range(nq)]
    return ys, states


def _ssd_chunks_bwd(xs, bg, cg, dt_raw_t, prev, dtb, alog, d_rows, dys, dh_last, consts):
    causal, upper, last, head = consts
    nq = len(xs)
    items = [(c, h) for c in range(nq) for h in range(N_HEADS)]
    ni = len(items)
    at = lambda c, h: c * N_HEADS + h
    groups = [(c, g) for c in range(nq) for g in range(2)]
    lane = _lane_iota((BLK, BLK))
    lane_row = _lane_iota((1, BLK))
    a_neg = -jnp.exp(alog)
    pre_dt = [dt_raw_t[c] + dtb for c in range(nq)]
    dt_t = [_softplus(pre_dt[c]) for c in range(nq)]
    acs_t = [_mm_hi(dt_t[c] * a_neg, upper) for c in range(nq)]
    pick = lambda t, h: jnp.sum(jnp.where(head == h, t, 0.0), axis=0, keepdims=True)
    full_sum = lambda t: jnp.sum(jnp.sum(t, axis=1, keepdims=True), axis=0, keepdims=True)
    dt_row = [pick(dt_t[c], h) for c, h in items]
    a_row = [pick(acs_t[c], h) for c, h in items]
    a_rb = [jnp.broadcast_to(a_row[n], (BLK, BLK)) for n in range(ni)]
    a_b = [a_rb[n].T for n in range(ni)]
    a_last = [jnp.sum(jnp.where(last, a_b[n], 0.0), axis=0, keepdims=True) for n in range(ni)]
    lm = [jnp.exp(jnp.where(causal, a_b[n] - a_rb[n], -1e30)) for n in range(ni)]
    cgb = [[cg[c][g].astype(MXU_DTYPE) for g in range(2)] for c in range(nq)]
    bgb = [[bg[c][g].astype(MXU_DTYPE) for g in range(2)] for c in range(nq)]
    cb = [[_mm_nt(cgb[c][g], bgb[c][g]) for g in range(2)] for c in range(nq)]
    u = [cb[c][h // 4] * lm[at(c, h)] for c, h in items]
    w = [(u[n] * dt_row[n]).astype(MXU_DTYPE) for n in range(ni)]
    e_row = [jnp.exp(a_last[n] - a_row[n]) for n in range(ni)]
    f_row = [dt_row[n] * e_row[n] for n in range(ni)]
    f_b = [jnp.broadcast_to(f_row[n], (BLK, BLK)).T for n in range(ni)]
    e_b = [jnp.exp(a_b[n]) for n in range(ni)]
    el = [jnp.exp(a_last[n]) for n in range(ni)]
    xb = [xs[c][h].astype(MXU_DTYPE) for c, h in items]
    dyb = [dys[c][h].astype(MXU_DTYPE) for c, h in items]
    prevb = [prev[c][h].astype(MXU_DTYPE) for c, h in items]
    gmat = [_mm(cgb[c][h // 4], prevb[at(c, h)]) for c, h in items]
    dw = [_mm_nt(dyb[n], xb[n]) for n in range(ni)]
    dg = [dys[c][h] * e_b[at(c, h)] for c, h in items]
    dgb = [dg[n].astype(MXU_DTYPE) for n in range(ni)]
    from_y = [_mm_tn(cgb[c][h // 4], dgb[at(c, h)]) for c, h in items]
    dhs = [None] * ni
    dprev = [None] * ni
    for c in reversed(range(nq)):
        for h in range(N_HEADS):
            dhs[at(c, h)] = dh_last[h] if c == nq - 1 else dprev[at(c + 1, h)]
            dprev[at(c, h)] = from_y[at(c, h)] + dhs[at(c, h)] * el[at(c, h)]
    dstb = [dhs[n].astype(MXU_DTYPE) for n in range(ni)]
    dxf = [_mm(bgb[c][h // 4], dstb[at(c, h)]) for c, h in items]
    xfb = [(xs[c][h] * f_b[at(c, h)]).astype(MXU_DTYPE) for c, h in items]
    dxs = [_mm_tn(w[at(c, h)], dyb[at(c, h)]) + d_rows[h] * dys[c][h] + f_b[at(c, h)] * dxf[at(c, h)]
           for c, h in items]
    dd_item = [jnp.sum(dys[c][h] * xs[c][h], axis=0, keepdims=True) for c, h in items]
    dcg_h = [_mm_nt(dgb[n], prevb[n]) for n in range(ni)]
    dbg_h = [_mm_nt(xfb[n], dstb[n]) for n in range(ni)]
    zt = [dw[n] * u[n] for n in range(ni)]
    dseg = [zt[n] * dt_row[n] for n in range(ni)]
    dcb_h = [dw[n] * lm[n] * dt_row[n] for n in range(ni)]
    four = lambda lst, c, g: lst[at(c, 4 * g)] + lst[at(c, 4 * g + 1)] + lst[at(c, 4 * g + 2)] + lst[at(c, 4 * g + 3)]
    dcb = {(c, g): four(dcb_h, c, g).astype(MXU_DTYPE) for c, g in groups}
    dcg = [[four(dcg_h, c, g) + _mm(dcb[c, g], bgb[c][g]) for g in range(2)] for c in range(nq)]
    dbg = [[four(dbg_h, c, g) + _mm_tn(dcb[c, g], cgb[c][g]) for g in range(2)] for c in range(nq)]
    r1 = [jnp.sum(dg[n] * gmat[n] + dseg[n], axis=1, keepdims=True) for n in range(ni)]
    r2 = [jnp.sum(dxf[at(c, h)] * xs[c][h], axis=1, keepdims=True) for c, h in items]
    tt = [jnp.where(lane < HALF, jnp.broadcast_to(r1[n], (BLK, BLK)), jnp.broadcast_to(r2[n], (BLK, BLK))).T
          for n in range(ni)]
    r1_row = [tt[n][0:1, :] for n in range(ni)]
    r2_row = [tt[n][HALF:HALF + 1, :] for n in range(ni)]
    d_el = [full_sum(dhs[at(c, h)] * prev[c][h]) for c, h in items]
    da_last = [jnp.sum(r2_row[n] * f_row[n], axis=1, keepdims=True) + el[n] * d_el[n] for n in range(ni)]
    da_row = [r1_row[n] - jnp.sum(dseg[n], axis=0, keepdims=True) - r2_row[n] * f_row[n]
              + jnp.where(lane_row == BLK - 1, da_last[n], 0.0) for n in range(ni)]
    ddt_row = [jnp.sum(zt[n], axis=0, keepdims=True) + r2_row[n] * e_row[n] for n in range(ni)]
    draw, dalog = [], jnp.zeros((N_HEADS, BLK), F32)
    for c in range(nq):
        da_t = jnp.zeros((N_HEADS, BLK), F32)
        ddt_t = jnp.zeros((N_HEADS, BLK), F32)
        for h in range(N_HEADS):
            da_t = jnp.where(head == h, da_row[at(c, h)], da_t)
            ddt_t = jnp.where(head == h, ddt_row[at(c, h)], ddt_t)
        d_dta = _mm_hi(da_t, causal.astype(F32))
        dalog = dalog + d_dta * dt_t[c] * a_neg
        draw.append((ddt_t + d_dta * a_neg) * jax.nn.sigmoid(pre_dt[c]))
    ddtb = draw[0]
    for c in range(1, nq):
        ddtb = ddtb + draw[c]
    dd_rows = []
    for h in range(N_HEADS):
        t = dd_item[at(0, h)]
        for c in range(1, nq):
            t = t + dd_item[at(c, h)]
        dd_rows.append(t)
    return ([dxs[c * N_HEADS:(c + 1) * N_HEADS] for c in range(nq)], dbg, dcg, draw,
            [dprev[at(0, h)] for h in range(N_HEADS)], ddtb, dalog, dd_rows)


def _dt_rows(dt_blk):
    return dt_blk.T[:N_HEADS]


def _silu_grad(x):
    s = jax.nn.sigmoid(x)
    return s * (1.0 + x * (1.0 - s))


def _conv_pre(halo, blk, cw_ref, cb_ref):
    ext = jnp.concatenate([halo, blk], axis=0)
    taps = [pltpu.roll(ext, 3 - k, 0)[8:] for k in range(3)] + [blk]
    pre = cb_ref[...] + cw_ref[0:1, :] * taps[0]
    for k in range(1, 4):
        pre = pre + cw_ref[k:k + 1, :] * taps[k]
    return pre, taps


def _ssd_split(pre):
    heads = _split_heads(pre[:, :SSM_W], 4)
    pb = [pre[:, SSM_W + g * D_STATE:SSM_W + (g + 1) * D_STATE] for g in range(2)]
    pc = [pre[:, SSM_W + 2 * D_STATE + g * D_STATE:SSM_W + 2 * D_STATE + (g + 1) * D_STATE] for g in range(2)]
    return heads, pb, pc


def _ssd_fwd(xbc, dt_raw, conv_w, conv_b, dtb_row, alog_row, d_exp, xchg):
    S = xbc.shape[0]
    nc = S // BLK
    nq = SSD_CHUNKS_PER_STEP if nc % SSD_CHUNKS_PER_STEP == 0 else 1
    rows = nq * BLK

    def body(xbc_ref, halo_ref, dt_ref, cw_ref, cb_ref, dtb_ref, alog_ref, d_ref, y_ref, prev_ref, state_ref):
        i = pl.program_id(0)

        @pl.when(i == 0)
        def _():
            state_ref[...] = jnp.zeros_like(state_ref)

        halo = halo_ref[...] * jnp.where(i > 0, 1.0, 0.0)
        pre, _ = _conv_pre(halo, xbc_ref[...], cw_ref, cb_ref)
        xc = _silu(pre)
        split = [_ssd_split(xc[c * BLK:(c + 1) * BLK]) for c in range(nq)]
        dt_t = [_dt_rows(dt_ref[c * BLK:(c + 1) * BLK, :]) for c in range(nq)]
        prev0 = [state_ref[h] for h in range(N_HEADS)]
        d_rows = [d_ref[h:h + 1, :] for h in range(N_HEADS)]
        ys, states = _ssd_chunks([s[0] for s in split], [s[1] for s in split], [s[2] for s in split], dt_t, prev0,
                                 dtb_ref[...], alog_ref[...], d_rows, _ssd_consts())
        for h in range(N_HEADS):
            for c in range(nq):
                prev_ref[c, h] = states[c][h]
            state_ref[h] = states[nq][h]
        y_ref[...] = jnp.concatenate([_join_heads(ys[c]) for c in range(nq)], axis=0)

    vec = pl.BlockSpec((N_HEADS, LANE), _fixed)
    return _hosted_call(
        body, "ssd_fwd", nc // nq,
        in_specs=[pl.BlockSpec((rows, XBC_W), _row),
                  pl.BlockSpec((8, XBC_W), lambda i: (jnp.maximum(i * (rows // 8) - 1, 0), 0)),
                  pl.BlockSpec((rows, LANE), _row),
                  pl.BlockSpec((4, XBC_W), _fixed), pl.BlockSpec((1, XBC_W), _fixed), vec, vec,
                  pl.BlockSpec((N_HEADS, LANE), _fixed)],
        out_specs=[pl.BlockSpec((rows, SSM_W), _row),
                   pl.BlockSpec((nq, N_HEADS, D_STATE, LANE), lambda i: (i, 0, 0, 0))],
        out_shape=[jax.ShapeDtypeStruct((S, SSM_W), F32), jax.ShapeDtypeStruct((nc, N_HEADS, D_STATE, LANE), F32)],
        scratch_shapes=[pltpu.VMEM((N_HEADS, D_STATE, LANE), F32)],
        args=(xbc, xbc, dt_raw, conv_w, conv_b, dtb_row, alog_row, d_exp), xchg=xchg, cparams=_cparams(),
    )


def _ssd_bwd(xbc, dt_raw, prev_states, dy, conv_w, conv_b, dtb_row, alog_row, d_exp, xchg):
    S = xbc.shape[0]
    nc = S // BLK
    nq = SSD_BWD_CHUNKS_PER_STEP if nc % SSD_BWD_CHUNKS_PER_STEP == 0 else 1
    rows, n_steps = nq * BLK, nc // nq

    def body(xbc_ref, halo_ref, dt_ref, prev_ref, dy_ref, cw_ref, cb_ref, dtb_ref, alog_ref, d_ref,
             dxbc_ref, ddt_ref, dcw_ref, dvec_ref, dd_ref, gstate_ref, ghalo_ref):
        i = pl.program_id(0)

        @pl.when(i == 0)
        def _():
            gstate_ref[...] = jnp.zeros_like(gstate_ref)
            ghalo_ref[...] = jnp.zeros_like(ghalo_ref)
            dcw_ref[...] = jnp.zeros_like(dcw_ref)
            dvec_ref[...] = jnp.zeros_like(dvec_ref)
            dd_ref[...] = jnp.zeros_like(dd_ref)

        halo = halo_ref[...] * jnp.where(i < n_steps - 1, 1.0, 0.0)
        pre, taps = _conv_pre(halo, xbc_ref[...], cw_ref, cb_ref)
        xc = _silu(pre)
        split = [_ssd_split(xc[c * BLK:(c + 1) * BLK]) for c in range(nq)]
        dt_t = [_dt_rows(dt_ref[c * BLK:(c + 1) * BLK, :]) for c in range(nq)]
        prev = [[prev_ref[c, h] for h in range(N_HEADS)] for c in range(nq)]
        d_rows = [d_ref[h:h + 1, :] for h in range(N_HEADS)]
        dys = [_split_heads(dy_ref[c * BLK:(c + 1) * BLK, :], 4) for c in range(nq)]
        dh_last = [gstate_ref[h] for h in range(N_HEADS)]
        dheads, dpb, dpc, ddt_t, dprev0, ddtb, dalog, dd_rows = _ssd_chunks_bwd(
            [s[0] for s in split], [s[1] for s in split], [s[2] for s in split], dt_t, prev, dtb_ref[...],
            alog_ref[...], d_rows, dys, dh_last, _ssd_consts())
        for h in range(N_HEADS):
            gstate_ref[h] = dprev0[h]
            dd_ref[h:h + 1, :] += dd_rows[h]
        pad = jnp.zeros((BLK - N_HEADS, BLK), F32)
        ddt_ref[...] = jnp.concatenate([jnp.concatenate([ddt_t[c], pad], axis=0).T for c in range(nq)],
                                       axis=0).astype(ddt_ref.dtype)
        dvec_ref[0:N_HEADS, :] += ddtb
        dvec_ref[N_HEADS:, :] += dalog
        dxc = jnp.concatenate([jnp.concatenate([_join_heads(dheads[c])] + list(dpb[c]) + list(dpc[c]), axis=1)
                               for c in range(nq)], axis=0)
        dpre = dxc * _silu_grad(pre)
        zeros8 = jnp.zeros((8, XBC_W), F32)
        dpe = jnp.concatenate([zeros8, dpre, zeros8], axis=0)
        n_ext = 16 + rows
        dext = cw_ref[3:4, :] * dpe[:8 + rows]
        dcw_ref[3:4, :] += jnp.sum(dpre * taps[3], axis=0, keepdims=True)
        for k in range(3):
            dext = dext + cw_ref[k:k + 1, :] * pltpu.roll(dpe, n_ext - (3 - k), 0)[:8 + rows]
            dcw_ref[k:k + 1, :] += jnp.sum(dpre * taps[k], axis=0, keepdims=True)
        dcw_ref[4:5, :] += jnp.sum(dpre, axis=0, keepdims=True)
        dxbc_ref[...] = jnp.concatenate([dext[8:rows], dext[rows:] + ghalo_ref[...]], axis=0).astype(dxbc_ref.dtype)
        ghalo_ref[...] = dext[:8, :]

    vec = pl.BlockSpec((N_HEADS, LANE), _fixed)
    rev = lambda i: (n_steps - 1 - i, 0)
    return _hosted_call(
        body, "ssd_bwd", n_steps,
        in_specs=[pl.BlockSpec((rows, XBC_W), rev),
                  pl.BlockSpec((8, XBC_W), lambda i: (jnp.maximum((n_steps - 1 - i) * (rows // 8) - 1, 0), 0)),
                  pl.BlockSpec((rows, LANE), rev),
                  pl.BlockSpec((nq, N_HEADS, D_STATE, LANE), lambda i: (n_steps - 1 - i, 0, 0, 0)),
                  pl.BlockSpec((rows, SSM_W), rev),
                  pl.BlockSpec((4, XBC_W), _fixed), pl.BlockSpec((1, XBC_W), _fixed), vec, vec,
                  pl.BlockSpec((N_HEADS, LANE), _fixed)],
        out_specs=[pl.BlockSpec((rows, XBC_W), rev), pl.BlockSpec((rows, LANE), rev),
                   pl.BlockSpec((8, XBC_W), _fixed), pl.BlockSpec((2 * N_HEADS, LANE), _fixed),
                   pl.BlockSpec((N_HEADS, LANE), _fixed)],
        out_shape=[jax.ShapeDtypeStruct((S, XBC_W), MXU_DTYPE), jax.ShapeDtypeStruct((S, LANE), MXU_DTYPE),
                   jax.ShapeDtypeStruct((8, XBC_W), F32), jax.ShapeDtypeStruct((2 * N_HEADS, LANE), F32),
                   jax.ShapeDtypeStruct((N_HEADS, LANE), F32)],
        scratch_shapes=[pltpu.VMEM((N_HEADS, D_STATE, LANE), F32), pltpu.VMEM((8, XBC_W), F32)],
        args=(xbc, xbc, dt_raw, prev_states, dy, conv_w, conv_b, dtb_row, alog_row, d_exp), xchg=xchg,
        cparams=_cparams(VMEM_BIG),
    )


def _adamw_math(w, g, m, v):
    m = ADAM_B1 * m + (1.0 - ADAM_B1) * g
    v = ADAM_B2 * v + (1.0 - ADAM_B2) * jnp.square(g)
    m_hat = m / (1.0 - ADAM_B1 ** ADAM_STEP)
    v_hat = v / (1.0 - ADAM_B2 ** ADAM_STEP)
    delta = -ADAM_LR * (m_hat / (jnp.sqrt(v_hat) + ADAM_EPS) + ADAM_WD * w)
    return delta, m, v


def _reduce_adamw(parts, w, m, v, name):
    R, C = w.shape

    def body(p_ref, w_ref, m_ref, v_ref, g_ref, d_ref, nm_ref, nv_ref):
        g = p_ref[0].astype(F32)
        for i in range(1, N_DEV):
            g = g + p_ref[i].astype(F32)
        d, nm, nv = _adamw_math(w_ref[...], g, m_ref[...], v_ref[...])
        g_ref[...] = g
        d_ref[...] = d
        nm_ref[...] = nm
        nv_ref[...] = nv

    if R % 16 == 0:
        tr = max(t for t in range(16, 257, 16) if R % t == 0)
        n, blk, pblk = R // tr, pl.BlockSpec((tr, C), _row), pl.BlockSpec((N_DEV, tr, C), lambda i: (0, i, 0))
    else:
        tl = 256
        n, blk, pblk = C // tl, pl.BlockSpec((R, tl), lambda i: (0, i)), pl.BlockSpec((N_DEV, R, tl),
                                                                                      lambda i: (0, 0, i))
    return pl.pallas_call(
        body, name=name, grid=(n,), in_specs=[pblk, blk, blk, blk],
        out_specs=[blk] * 4, out_shape=[jax.ShapeDtypeStruct((R, C), F32)] * 4,
    )(parts, w, m, v)


def _reduce_adamw_hosting(parts_list, wmv_list, name, xchg):
    n_arr = len(parts_list)
    C = wmv_list[0][0].shape[1]
    tl = 256

    def body(*refs):
        p_refs, wmv_refs, o_refs = refs[:n_arr], refs[n_arr:4 * n_arr], refs[4 * n_arr:]
        for k in range(n_arr):
            g = p_refs[k][0].astype(F32)
            for i in range(1, N_DEV):
                g = g + p_refs[k][i].astype(F32)
            w_ref, m_ref, v_ref = wmv_refs[3 * k:3 * k + 3]
            d, nm, nv = _adamw_math(w_ref[...], g, m_ref[...], v_ref[...])
            for o, val in zip(o_refs[4 * k:4 * k + 4], (g, d, nm, nv)):
                o[...] = val

    in_specs = [pl.BlockSpec((N_DEV, w.shape[0], tl), lambda i: (0, 0, i)) for w, _, _ in wmv_list]
    in_specs += [pl.BlockSpec((w.shape[0], tl), lambda i: (0, i)) for w, _, _ in wmv_list for _ in range(3)]
    out_specs = [pl.BlockSpec((w.shape[0], tl), lambda i: (0, i)) for w, _, _ in wmv_list for _ in range(4)]
    out_shape = [jax.ShapeDtypeStruct(w.shape, F32) for w, _, _ in wmv_list for _ in range(4)]
    args = list(parts_list) + [a for wmv in wmv_list for a in wmv]
    outs, x_out = _hosted_call(body, name, C // tl, in_specs, out_specs, out_shape, [], args, xchg,
                               _cparams(VMEM_BIG))
    return [outs[4 * k:4 * k + 4] for k in range(n_arr)], x_out


_SMALL_NAMES = ("ada_b", "norm1", "conv_w", "conv_b", "dt_bias", "A_log", "D_skip", "sinks", "attn_out_norm",
                "ssm_out_norm", "norm2", "rel_bias", "final_norm")
N_MOD = 6 * D_MODEL


def _mod_row(a0, a1, a2):
    return jnp.concatenate([a0[2:3], a0[1:2], a1[0:1], a2[2:3], a2[1:2], a2[3:4]], axis=1)


def _small_update(gathered, params):
    n_g = len(gathered)
    flat = [a for name in _SMALL_NAMES for a in params[name]]

    def body(*refs):
        a0_ref, a1_ref, a2_ref, cw_ref, dv_ref, dd_ref, ds_ref, dr_ref, c_ref = refs[:n_g]
        wmv = refs[n_g:n_g + len(flat)]
        outs = refs[n_g + len(flat):]

        def total(ref):
            t = ref[0]
            for i in range(1, N_DEV):
                t = t + ref[i]
            return t

        t0, t1, t2, tcw, tdv, tdd, tds, tdr = [total(r) for r in (a0_ref, a1_ref, a2_ref, cw_ref, dv_ref, dd_ref,
                                                                   ds_ref, dr_ref)]
        r8 = lax.broadcasted_iota(jnp.int32, (N_HEADS, LANE), 0)
        l8 = lax.broadcasted_iota(jnp.int32, (N_HEADS, LANE), 1)

        def diag_row(t):
            return jnp.sum(jnp.where(r8 == l8, t, 0.0), axis=0, keepdims=True)[:, :N_HEADS]

        def lane_sums(t):
            return diag_row(jnp.broadcast_to(jnp.sum(t, axis=1, keepdims=True), (N_HEADS, LANE)))

        me = _lin(_my_pos())
        n_cw = XBC_W // N_DEV
        cw_mine = jnp.zeros((4, n_cw), F32)
        for j in range(N_DEV):
            cw_mine = cw_mine + tcw[0:4, j * n_cw:(j + 1) * n_cw] * jnp.where(me == j, 1.0, 0.0)
        grads = {
            "ada_b": _mod_row(t0, t1, t2), "norm1": t0[0:1], "conv_w": cw_mine, "conv_b": tcw[4:5],
            "dt_bias": lane_sums(tdv[:N_HEADS]), "A_log": lane_sums(tdv[N_HEADS:]), "D_skip": lane_sums(tdd),
            "sinks": diag_row(tds), "attn_out_norm": t1[1:2, :ATTN_W], "ssm_out_norm": t1[1:2, ATTN_W:],
            "norm2": t2[0:1], "rel_bias": tdr[:, :N_HEADS], "final_norm": t2[4:5],
        }
        for k, name in enumerate(_SMALL_NAMES):
            w_ref, m_ref, v_ref = wmv[3 * k:3 * k + 3]
            g = grads[name]
            d, nm, nv = _adamw_math(w_ref[...], g, m_ref[...], v_ref[...])
            for o, val in zip(outs[4 * k:4 * k + 4], (g, d, nm, nv)):
                o[...] = val
        loss_ref, call_ref, dmod_ref = outs[4 * len(_SMALL_NAMES):]
        loss_ref[...] = t2[5:6, 0:1]
        call_ref[...] = jnp.concatenate([c_ref[i] for i in range(N_DEV)], axis=0)
        dmod_ref[...] = jnp.concatenate([_mod_row(a0_ref[i], a1_ref[i], a2_ref[i]) for i in range(N_DEV)], axis=0)

    out_shape = [jax.ShapeDtypeStruct(params[name][0].shape, F32) for name in _SMALL_NAMES for _ in range(4)]
    out_shape += [jax.ShapeDtypeStruct((1, 1), F32), jax.ShapeDtypeStruct((N_DEV, D_MODEL), F32),
                  jax.ShapeDtypeStruct((N_DEV, N_MOD), F32)]
    res = pl.pallas_call(body, name="small_update", out_shape=out_shape)(*gathered, *flat)
    upd = {name: res[4 * k:4 * k + 4] for k, name in enumerate(_SMALL_NAMES)}
    loss, c_all, dmod_all = res[4 * len(_SMALL_NAMES):]
    return upd, loss, c_all, dmod_all


def _ada_w_update(c_all, dmod_all, w, m, v):
    chunk = w.shape[1]

    def body(c_ref, dm_ref, w_ref, m_ref, v_ref, g_ref, d_ref, nm_ref, nv_ref):
        me = _lin(_my_pos())
        dm = jnp.zeros((N_DEV, chunk), F32)
        for j in range(N_DEV):
            dm = dm + dm_ref[:, j * chunk:(j + 1) * chunk] * jnp.where(me == j, 1.0, 0.0)
        g = lax.dot_general(_silu(c_ref[...]), dm, (((0,), (0,)), ((), ())), precision=HI,
                            preferred_element_type=F32)
        d, nm, nv = _adamw_math(w_ref[...], g, m_ref[...], v_ref[...])
        g_ref[...] = g
        d_ref[...] = d
        nm_ref[...] = nm
        nv_ref[...] = nv

    tr = 256
    blk = pl.BlockSpec((tr, chunk), _row)
    return pl.pallas_call(
        body, name="ada_w_update", grid=(w.shape[0] // tr,),
        in_specs=[pl.BlockSpec((N_DEV, tr), lambda i: (0, i)), pl.BlockSpec(dmod_all.shape, _fixed), blk, blk, blk],
        out_specs=[blk] * 4, out_shape=[jax.ShapeDtypeStruct(w.shape, F32)] * 4,
    )(c_all, dmod_all, w, m, v)


def _local_step(x, tgt, c, mod, w_in, conv_w, w_o_mine, w_gu_mine, w_d_mine, p):
    S = x.shape[0]
    tm = min(512, S)
    tmm = min(256, S)
    tw = min(2048, S)
    shift1, scale1, gate1, shift2, scale2, gate2 = [mod[i:i + 1] for i in range(6)]
    buckets = jnp.asarray(_t5_bucket_table())
    per_head = lambda a: jnp.broadcast_to(a.reshape(N_HEADS, 1), (N_HEADS, LANE))
    dtb_row, alog_row, d_exp = per_head(p["dt_bias"]), per_head(p["A_log"]), per_head(p["D_skip"])
    sinks = p["sinks"].reshape(N_HEADS)

    d_cut, gu_cut = WD_CUT, WGU_CUTS
    (qkv, z, xbc, dt_raw), (g_d_a,) = _in_proj_fwd(x, p["norm1"], scale1, shift1, w_in, tm,
                                                   ([w_d_mine[:d_cut]], "two-level"))
    bias = _attn_bias(buckets, p["rel_bias"])
    (ya,), (g_gu_a,) = _attn_fwd(qkv, bias, sinks, ([w_gu_mine[:gu_cut[0]]], "two-level"))
    (ys, prev_states), (g_gu_b, g_o) = _ssd_fwd(xbc, dt_raw, conv_w, p["conv_b"], dtb_row, alog_row, d_exp,
                                                ([w_gu_mine[gu_cut[0]:gu_cut[1]], w_o_mine], "two-level"))
    w_o = g_o.reshape(D_MODEL, D_MODEL)
    x1, (g_gu_c, g_d_b) = _out_proj_fwd(x, ya, ys, z, p["attn_out_norm"], p["ssm_out_norm"], gate1, w_o, tm,
                                        ([w_gu_mine[gu_cut[1]:], w_d_mine[d_cut:]], "two-level"))
    dx1, h2, dgu, act, dmlp, acc2 = _mlp_loss(x1, tgt, p["norm2"], scale2, shift2, gate2, p["final_norm"],
                                              (g_gu_a, g_gu_b, g_gu_c), (g_d_a, g_d_b), tmm)
    g_w_gu = _wgrad(dgu, h2, 2 * D_FF // 4, tw, "wgrad_gate_up")
    g_w_d = _wgrad(act, dmlp, D_FF // 2, tw, "wgrad_down")
    (dya, dys, dz, u, dmix, acc1), (r_d,) = _out_proj_bwd(
        dx1, ya, ys, z, p["attn_out_norm"], p["ssm_out_norm"], gate1, w_o, tm,
        ([g_w_d.reshape(N_DEV, D_FF // N_DEV, D_MODEL)], True))
    g_w_o = _wgrad(u, dmix, D_MODEL, tw, "wgrad_out")
    (dq, dkv, dbias, dsk), (r_o,) = _attn_bwd(qkv, ya, dya, bias, sinks,
                                              ([g_w_o.reshape(N_DEV, D_MODEL // N_DEV, D_MODEL)], True))
    drel, dsink = _attn_finish(dbias, dsk, buckets)
    (dxbc, ddt, dcw, dvec, dd), (r_gu,) = _ssd_bwd(
        xbc, dt_raw, prev_states, dys, conv_w, p["conv_b"], dtb_row, alog_row, d_exp,
        ([g_w_gu.reshape(N_DEV, 2 * D_FF // N_DEV, D_MODEL)], True))
    gx, h1, acc0 = _in_proj_bwd(x, dx1, dq, dkv, dz, dxbc, ddt, p["norm1"], scale1, shift1, w_in, tm)
    dproj = (dq, dkv, dz, dxbc, ddt)
    half = D_MODEL // 2
    slots = lambda g: g[:IN_W].reshape(N_DEV, IN_W // N_DEV, half)
    g_in_a, gathered = _wgrad(dproj, h1, IN_PAD, tw, "wgrad_in_a",
                              ([acc0, acc1, acc2, dcw, dvec, dd, dsink, drel, c], False), g_cols=(half, 0))
    g_in_b, (r_in_a,) = _wgrad(dproj, h1, IN_PAD, tw, "wgrad_in_b", ([slots(g_in_a)], True), g_cols=(half, 1))
    return gx, (r_in_a, slots(g_in_b)), (r_o, r_gu, r_d), gathered


def kernel(x, c, ada_w, ada_b, norm1, w_in, conv_w, conv_b, dt_bias, A_log, D_skip, sinks, attn_out_norm, ssm_out_norm, w_o, norm2, w_gate_up, w_down, rel_bias, final_norm, loss_target, m_ada_w, m_ada_b, m_norm1, m_w_in, m_conv_w, m_conv_b, m_dt_bias, m_A_log, m_D_skip, m_sinks, m_attn_out_norm, m_ssm_out_norm, m_w_o, m_norm2, m_w_gate_up, m_w_down, m_rel_bias, m_final_norm, v_ada_w, v_ada_b, v_norm1, v_w_in, v_conv_w, v_conv_b, v_dt_bias, v_A_log, v_D_skip, v_sinks, v_attn_out_norm, v_ssm_out_norm, v_w_o, v_norm2, v_w_gate_up, v_w_down, v_rel_bias, v_final_norm):
    two_d = lambda a: a if a.ndim == 2 else a.reshape(-1, a.shape[-1])
    small_params = dict(
        ada_b=(ada_b, m_ada_b, v_ada_b), norm1=(norm1, m_norm1, v_norm1), conv_w=(conv_w, m_conv_w, v_conv_w),
        conv_b=(conv_b, m_conv_b, v_conv_b), dt_bias=(dt_bias, m_dt_bias, v_dt_bias), A_log=(A_log, m_A_log, v_A_log),
        D_skip=(D_skip, m_D_skip, v_D_skip), sinks=(sinks, m_sinks, v_sinks),
        attn_out_norm=(attn_out_norm, m_attn_out_norm, v_attn_out_norm),
        ssm_out_norm=(ssm_out_norm, m_ssm_out_norm, v_ssm_out_norm), norm2=(norm2, m_norm2, v_norm2),
        rel_bias=(rel_bias, m_rel_bias, v_rel_bias), final_norm=(final_norm, m_final_norm, v_final_norm))
    small_params = {k: tuple(two_d(a) for a in v) for k, v in small_params.items()}
    S = x.shape[1]
    xs, tgt = x.reshape(S, D_MODEL), loss_target.reshape(S, D_MODEL)
    ada_w2 = ada_w[0]
    chunk = ada_w2.shape[1]
    t_in = [jnp.transpose(a[0]) for a in (w_in, m_w_in, v_w_in)]
    t_gu = [jnp.transpose(a[0]) for a in (w_gate_up, m_w_gate_up, v_w_gate_up)]

    mod, (g_in, g_cw) = _mod_and_gather(c, ada_w2, ada_b.reshape(N_DEV, chunk), [t_in[0].astype(WIRE_DTYPE), conv_w[0]])
    mod = mod.reshape(6, D_MODEL)
    w_in_full = jnp.pad(g_in.reshape(IN_W, D_MODEL), ((0, IN_PAD - IN_W), (0, 0)))
    conv_w_full = jnp.transpose(g_cw, (1, 0, 2)).reshape(4, XBC_W)

    p = {k: v[0] for k, v in small_params.items()}
    gx, (r_in_a, gw_in_b), (r_o, r_gu, r_d), gathered = _local_step(
        xs, tgt, c, mod, w_in_full, conv_w_full, w_o[0].astype(WIRE_DTYPE), t_gu[0].astype(WIRE_DTYPE),
        w_down[0].astype(WIRE_DTYPE), p)

    (u_gu, u_d, u_o), (r_in_b,) = _reduce_adamw_hosting(
        [r_gu, r_d, r_o], [tuple(t_gu), (w_down[0], m_w_down[0], v_w_down[0]), (w_o[0], m_w_o[0], v_w_o[0])],
        "adamw_big", ([gw_in_b], True))
    r_in = jnp.concatenate([r_in_a, r_in_b], axis=2)

    small, loss, c_all, dmod_all = _small_update(gathered, small_params)

    big = {
        "ada_w": _ada_w_update(c_all, dmod_all, ada_w2, m_ada_w[0], v_ada_w[0]),
        "w_in": [jnp.transpose(a) for a in _reduce_adamw(r_in, *t_in, "adamw_w_in")],
        "w_o": u_o,
        "w_gate_up": [jnp.transpose(a) for a in u_gu],
        "w_down": u_d,
    }
    big.update(small)

    order = ['ada_w', 'ada_b', 'norm1', 'w_in', 'conv_w', 'conv_b', 'dt_bias', 'A_log', 'D_skip', 'sinks',
             'attn_out_norm', 'ssm_out_norm', 'w_o', 'norm2', 'w_gate_up', 'w_down', 'rel_bias', 'final_norm']
    shapes = dict(ada_w=ada_w.shape, ada_b=ada_b.shape, norm1=norm1.shape, w_in=w_in.shape, conv_w=conv_w.shape,
                  conv_b=conv_b.shape, dt_bias=dt_bias.shape, A_log=A_log.shape, D_skip=D_skip.shape,
                  sinks=sinks.shape, attn_out_norm=attn_out_norm.shape, ssm_out_norm=ssm_out_norm.shape,
                  w_o=w_o.shape, norm2=norm2.shape, w_gate_up=w_gate_up.shape, w_down=w_down.shape,
                  rel_bias=rel_bias.shape, final_norm=final_norm.shape)
    outs = [[], [], [], []]
    for name in order:
        for kind in range(4):
            outs[kind].append(big[name][kind].reshape(shapes[name]))
    return (loss.reshape(()), gx.reshape(x.shape), *outs[0], *outs[1], *outs[2], *outs[3])
```

```python
import functools

import numpy as np
import jax
import jax.numpy as jnp
from jax import lax
from jax.experimental import pallas as pl
from jax.experimental.pallas import tpu as pltpu

F32 = jnp.float32
MXU_DTYPE = jnp.bfloat16
WIRE_DTYPE = jnp.bfloat16
HI = lax.Precision.HIGHEST
MESH = pl.DeviceIdType.MESH
N_DEV = 8

D_MODEL = 1024
ATTN_W = 512
KV_W = 128
SSM_W = 512
XBC_W = 1024
N_HEADS = 8
D_STATE = 128
D_FF = 2816
IN_W = 2312
IN_PAD = 2432
BLK = 128
N_BUCKETS = 32
EPS = 1e-6
LANE = 128
HALF = 64

ADAM_LR, ADAM_B1, ADAM_B2, ADAM_EPS, ADAM_WD, ADAM_STEP = 0.001, 0.9, 0.999, 1e-08, 0.01, 10

VMEM_BIG = 56 * 1024 * 1024
WD_CUT = 288
WGU_CUTS = (240, 496)


def _cparams(vmem=None):
    if vmem is None:
        return pltpu.CompilerParams()
    return pltpu.CompilerParams(vmem_limit_bytes=vmem)


def _mm(a, b):
    return jnp.dot(a.astype(MXU_DTYPE), b.astype(MXU_DTYPE), preferred_element_type=F32)


def _mm_nt(a, b):
    return lax.dot_general(a.astype(MXU_DTYPE), b.astype(MXU_DTYPE), (((1,), (1,)), ((), ())),
                           preferred_element_type=F32)


def _mm_tn(a, b):
    return lax.dot_general(a.astype(MXU_DTYPE), b.astype(MXU_DTYPE), (((0,), (0,)), ((), ())),
                           preferred_element_type=F32)


def _mm_hi(a, b):
    return jnp.dot(a, b, precision=HI, preferred_element_type=F32)


def _silu(x):
    return x * jax.nn.sigmoid(x)


def _softplus(x):
    return jnp.maximum(x, 0.0) + jnp.log1p(jnp.exp(-jnp.abs(x)))


def _rms(x, g, n):
    return x * lax.rsqrt(jnp.sum(x * x, axis=-1, keepdims=True) * (1.0 / n) + EPS) * g


def _modnorm(x, g, scale, shift):
    return _rms(x, g, x.shape[-1]) * (1.0 + scale) + shift


def _modnorm_parts(x):
    r = lax.rsqrt(jnp.sum(x * x, axis=-1, keepdims=True) * (1.0 / x.shape[-1]) + EPS)
    return r, x * r


def _modnorm_bwd(r, xhat, g, scale, dy):
    dyg = dy * (g * (1.0 + scale))
    c = jnp.sum(dyg * xhat, axis=-1, keepdims=True) * (1.0 / xhat.shape[-1])
    dx = r * (dyg - xhat * c)
    ct = jnp.sum(dy * xhat, axis=0, keepdims=True)
    return dx, ct * (1.0 + scale), ct * g, jnp.sum(dy, axis=0, keepdims=True)


def _lane_iota(shape):
    return lax.broadcasted_iota(jnp.int32, shape, len(shape) - 1)


def _split_pair(t):
    lane = _lane_iota(t.shape)
    lo = jnp.where(lane < HALF, t, 0.0)
    hi = pltpu.roll(jnp.where(lane >= HALF, t, 0.0), HALF, 1)
    return lo, hi


def _join_pair(lo, hi):
    lane = _lane_iota(lo.shape)
    return jnp.where(lane < HALF, lo, pltpu.roll(hi, HALF, 1))


def _split_heads(t, n_pairs):
    out = []
    for p in range(n_pairs):
        out.extend(_split_pair(t[:, p * LANE:(p + 1) * LANE]))
    return out


def _join_heads(hs):
    return jnp.concatenate([_join_pair(hs[2 * p], hs[2 * p + 1]) for p in range(len(hs) // 2)], axis=1)


def _t5_bucket_table():
    dist = np.arange(BLK)[:, None] + BLK - np.arange(2 * BLK)[None, :]
    n = np.maximum(dist, 0)
    max_exact = N_BUCKETS // 2
    large = max_exact + (np.log(np.maximum(n, 1) / max_exact) / np.log(128 / max_exact)
                         * (N_BUCKETS - max_exact)).astype(np.int32)
    large = np.minimum(large, N_BUCKETS - 1)
    return np.where(n < max_exact, n, large).astype(np.int32)


def _my_pos():
    return lax.axis_index("x"), lax.axis_index("y"), lax.axis_index("c")


def _peer(k):
    x, y, c = _my_pos()
    return (1 - x if k & 4 else x, 1 - y if k & 2 else y, 1 - c if k & 1 else c)


def _lin(pos):
    return 4 * pos[0] + 2 * pos[1] + pos[2]


def _xchg_copies(ins, outs, sems, scatter):
    local_sem, send_sem, recv_sem = sems
    me = _lin(_my_pos())
    local, remote = [], []
    for a in range(len(ins)):
        src = ins[a].at[me] if scatter else ins[a]
        local.append(pltpu.make_async_copy(src, outs[a].at[me], local_sem.at[a]))
    for k in range(1, N_DEV):
        peer = _peer(k)
        for a in range(len(ins)):
            src = ins[a].at[_lin(peer)] if scatter else ins[a]
            remote.append(pltpu.make_async_remote_copy(src, outs[a].at[me], send_sem.at[a, k - 1],
                                                       recv_sem.at[a, k - 1], device_id=peer, device_id_type=MESH))
    return local, remote


def _xchg_start(ins, outs, sems, scatter):
    local, remote = _xchg_copies(ins, outs, sems, scatter)
    for cp in local + remote:
        cp.start()


def _xchg_wait(ins, outs, sems, scatter):
    local, remote = _xchg_copies(ins, outs, sems, scatter)
    for cp in local:
        cp.wait()
    for cp in remote:
        cp.wait_send()
        cp.wait_recv()


def _xchg_shapes(arrs, scatter):
    n = len(arrs)
    if scatter:
        out_shape = [jax.ShapeDtypeStruct(a.shape, a.dtype) for a in arrs]
    else:
        out_shape = [jax.ShapeDtypeStruct((N_DEV,) + a.shape, a.dtype) for a in arrs]
    sems = [pltpu.SemaphoreType.DMA((n,)), pltpu.SemaphoreType.DMA((n, N_DEV - 1)),
            pltpu.SemaphoreType.DMA((n, N_DEV - 1))]
    return out_shape, sems


_CHIPS = (2, 4, 6)


def _g2_sems(n):
    dma = pltpu.SemaphoreType.DMA
    return [dma((n,)), dma((n, N_DEV)), dma((n, N_DEV)), dma((n, len(_CHIPS))), dma((n, len(_CHIPS)))]


class _TwoLevelGather:
    def __init__(self, ins, outs, sems):
        self.ins, self.outs = ins, outs
        self.local_sem, self.send_sem, self.recv_sem, self.fsend_sem, self.frecv_sem = sems
        self.n = len(ins)

    def _direct(self, a, k):
        return pltpu.make_async_remote_copy(self.ins[a], self.outs[a].at[_lin(_my_pos())], self.send_sem.at[a, k],
                                            self.recv_sem.at[a, k], device_id=_peer(k), device_id_type=MESH)

    def _handed_on(self, a, j, origin):
        slot = self.outs[a].at[origin]
        return pltpu.make_async_remote_copy(slot, slot, self.fsend_sem.at[a, j], self.frecv_sem.at[a, j],
                                            device_id=_peer(1), device_id_type=MESH)

    def _local(self, a):
        return pltpu.make_async_copy(self.ins[a], self.outs[a].at[_lin(_my_pos())], self.local_sem.at[a])

    def start(self):
        for a in range(self.n):
            self._local(a).start()
        for k in (1,) + _CHIPS:
            for a in range(self.n):
                self._direct(a, k).start()

    def forward(self):
        for j, k in enumerate(_CHIPS):
            for a in range(self.n):
                self._direct(a, k).wait_recv()
                self._handed_on(a, j, _lin(_peer(k))).start()

    def finish(self):
        for a in range(self.n):
            self._direct(a, 1).wait_recv()
            for j, k in enumerate(_CHIPS):
                self._handed_on(a, j, _lin(_peer(k ^ 1))).wait_recv()
            self._local(a).wait()
            for k in (1,) + _CHIPS:
                self._direct(a, k).wait_send()
            for j, k in enumerate(_CHIPS):
                self._handed_on(a, j, _lin(_peer(k))).wait_send()


def _mod_and_gather(c, ada_w, ada_b8, arrs):
    n = len(arrs)
    chunk = ada_w.shape[1]
    out_shape = [jax.ShapeDtypeStruct((N_DEV, 1, chunk), F32)]
    out_shape += [jax.ShapeDtypeStruct((N_DEV,) + a.shape, a.dtype) for a in arrs]

    def modulation(c_ref, w_ref, b_ref, out_ref, cbuf, part, s1, r1, s2, r2):
        me = _lin(_my_pos())
        first = []
        for k in range(1, N_DEV):
            cp = pltpu.make_async_remote_copy(c_ref, cbuf.at[me], s1.at[k - 1], r1.at[k - 1],
                                              device_id=_peer(k), device_id_type=MESH)
            cp.start()
            first.append(cp)
        cbuf[me] = c_ref[...]
        for cp in first:
            cp.wait_send()
            cp.wait_recv()
        cond = _silu(jnp.concatenate([cbuf[i] for i in range(N_DEV)], axis=0))
        mod = _mm_hi(cond, w_ref[...]) + b_ref[pl.ds(me, 1), :]
        for j in range(N_DEV):
            part[j] = mod[j:j + 1, :]
        second = []
        for k in range(1, N_DEV):
            peer = _peer(k)
            cp = pltpu.make_async_remote_copy(part.at[_lin(peer)], out_ref.at[me], s2.at[k - 1], r2.at[k - 1],
                                              device_id=peer, device_id_type=MESH)
            cp.start()
            second.append(cp)
        out_ref[me] = part[me]
        for cp in second:
            cp.wait_send()
            cp.wait_recv()

    def body(*refs):
        c_ref, w_ref, b_ref = refs[:3]
        ins = refs[3:3 + n]
        mod_ref = refs[3 + n]
        outs = refs[4 + n:4 + 2 * n]
        cbuf, part, s1, r1, s2, r2 = refs[4 + 2 * n:10 + 2 * n]
        gather = _TwoLevelGather(ins, outs, refs[10 + 2 * n:])
        gather.start()
        modulation(c_ref, w_ref, b_ref, mod_ref, cbuf, part, s1, r1, s2, r2)
        gather.forward()
        gather.finish()

    hbm = pl.BlockSpec(memory_space=pltpu.HBM)
    vm = pl.BlockSpec(memory_space=pltpu.VMEM)
    dma = pltpu.SemaphoreType.DMA
    res = pl.pallas_call(
        body, name="mod_and_gather", out_shape=out_shape, in_specs=[vm, vm, vm] + [hbm] * n,
        out_specs=[vm] + [hbm] * n,
        scratch_shapes=[pltpu.VMEM((N_DEV, 1, D_MODEL), F32), pltpu.VMEM((N_DEV, 1, chunk), F32)]
        + [dma((N_DEV - 1,))] * 4 + _g2_sems(n),
    )(c, ada_w, ada_b8, *arrs)
    return res[0], res[1:]


def _hosted_call(body, name, grid, in_specs, out_specs, out_shape, scratch_shapes, args, xchg, cparams):
    arrs, scatter = xchg
    grid = (grid,) if isinstance(grid, int) else tuple(grid)
    n, n_in, n_out, n_scr = len(arrs), len(in_specs), len(out_specs), len(scratch_shapes)
    two_level = scatter == "two-level"
    x_shape, x_sems = _xchg_shapes(arrs, False if two_level else scatter)
    if two_level:
        x_sems = _g2_sems(n)
    n_steps = int(np.prod(grid))

    def hosted(*refs):
        ins, refs = refs[:n_in], refs[n_in:]
        x_in, refs = refs[:n], refs[n:]
        outs, refs = refs[:n_out], refs[n_out:]
        x_out, refs = refs[:n], refs[n:]
        scr, sems = refs[:n_scr], refs[n_scr:]
        step = pl.program_id(0)
        for d in range(1, len(grid)):
            step = step * grid[d] + pl.program_id(d)

        @pl.when(step == 0)
        def _():
            if two_level:
                _TwoLevelGather(x_in, x_out, sems).start()
            else:
                _xchg_start(x_in, x_out, sems, scatter)

        if two_level:
            @pl.when(step == (2 * n_steps) // 3)
            def _():
                _TwoLevelGather(x_in, x_out, sems).forward()

        body(*ins, *outs, *scr)

        @pl.when(step == n_steps - 1)
        def _():
            if two_level:
                _TwoLevelGather(x_in, x_out, sems).finish()
            else:
                _xchg_wait(x_in, x_out, sems, scatter)

    hbm = pl.BlockSpec(memory_space=pltpu.HBM)
    res = pl.pallas_call(
        hosted, name=name, grid=grid, in_specs=list(in_specs) + [hbm] * n,
        out_specs=list(out_specs) + [hbm] * n, out_shape=list(out_shape) + x_shape,
        scratch_shapes=list(scratch_shapes) + x_sems, compiler_params=cparams,
    )(*args, *arrs)
    return res[:n_out], res[n_out:]


def _row(i):
    return (i, 0)


def _fixed(i):
    return (0, 0)


def _in_proj_fwd(x, norm1, scale1, shift1, w_in, tm, xchg):
    S = x.shape[0]

    def body(x_ref, n_ref, sc_ref, sh_ref, w_ref, qkv_ref, z_ref, xbc_ref, dt_ref):
        h = _modnorm(x_ref[...], n_ref[...], sc_ref[...], sh_ref[...])
        p = _mm_nt(h, w_ref[...])
        qkv_ref[...] = p[:, :768].astype(qkv_ref.dtype)
        z_ref[...] = p[:, 768:1280]
        xbc_ref[...] = p[:, 1280:2304]
        dt_ref[...] = p[:, 2304:IN_PAD]

    vec = pl.BlockSpec((1, D_MODEL), _fixed)
    return _hosted_call(
        body, "in_proj_fwd", S // tm,
        in_specs=[pl.BlockSpec((tm, D_MODEL), _row), vec, vec, vec, pl.BlockSpec((IN_PAD, D_MODEL), _fixed)],
        out_specs=[pl.BlockSpec((tm, 768), _row), pl.BlockSpec((tm, SSM_W), _row),
                   pl.BlockSpec((tm, XBC_W), _row), pl.BlockSpec((tm, LANE), _row)],
        out_shape=[jax.ShapeDtypeStruct((S, 768), MXU_DTYPE), jax.ShapeDtypeStruct((S, SSM_W), F32),
                   jax.ShapeDtypeStruct((S, XBC_W), F32), jax.ShapeDtypeStruct((S, LANE), F32)],
        scratch_shapes=[], args=(x, norm1, scale1, shift1, w_in), xchg=xchg, cparams=_cparams(VMEM_BIG),
    )


def _in_proj_bwd(x, dx1, dq, dkv, dz, dxbc, ddt, norm1, scale1, shift1, w_in, tm):
    S = x.shape[0]

    def body(x_ref, dx1_ref, dq_ref, dkv_ref, dz_ref, dxbc_ref, ddt_ref, n_ref, sc_ref, sh_ref, w_ref,
             gx_ref, h_ref, acc_ref):
        @pl.when(pl.program_id(0) == 0)
        def _():
            acc_ref[...] = jnp.zeros_like(acc_ref)

        dp = jnp.concatenate([dq_ref[...], dkv_ref[...], dz_ref[...], dxbc_ref[...], ddt_ref[...]], axis=1)
        dh = _mm(dp, w_ref[...])
        r, xhat = _modnorm_parts(x_ref[...])
        dx, dn, dsc, dsh = _modnorm_bwd(r, xhat, n_ref[...], sc_ref[...], dh)
        gx_ref[...] = dx1_ref[...] + dx
        h_ref[...] = (xhat * n_ref[...] * (1.0 + sc_ref[...]) + sh_ref[...]).astype(h_ref.dtype)
        acc_ref[0:1, :] += dn
        acc_ref[1:2, :] += dsc
        acc_ref[2:3, :] += dsh

    vec = pl.BlockSpec((1, D_MODEL), _fixed)
    return pl.pallas_call(
        body, name="in_proj_bwd", grid=(S // tm,),
        in_specs=[pl.BlockSpec((tm, D_MODEL), _row), pl.BlockSpec((tm, D_MODEL), _row),
                  pl.BlockSpec((tm, ATTN_W), _row), pl.BlockSpec((tm, 2 * KV_W), _row),
                  pl.BlockSpec((tm, SSM_W), _row), pl.BlockSpec((tm, XBC_W), _row), pl.BlockSpec((tm, LANE), _row),
                  vec, vec, vec, pl.BlockSpec((IN_PAD, D_MODEL), _fixed)],
        out_specs=[pl.BlockSpec((tm, D_MODEL), _row), pl.BlockSpec((tm, D_MODEL), _row),
                   pl.BlockSpec((8, D_MODEL), _fixed)],
        out_shape=[jax.ShapeDtypeStruct((S, D_MODEL), F32), jax.ShapeDtypeStruct((S, D_MODEL), MXU_DTYPE),
                   jax.ShapeDtypeStruct((8, D_MODEL), F32)],
        compiler_params=_cparams(VMEM_BIG),
    )(x, dx1, dq, dkv, dz, dxbc, ddt, norm1, scale1, shift1, w_in)


def _out_stage(ya, ys0, ys1, z0, z1, an, sn0, sn1):
    half = SSM_W // 2
    a = _rms(ya, an, ATTN_W)
    g0 = _rms(ys0 * _silu(z0), sn0, half)
    g1 = _rms(ys1 * _silu(z1), sn1, half)
    return jnp.concatenate([a, g0, g1], axis=1)


def _out_stage_args(ya_ref, ys_ref, z_ref, an_ref, sn_ref):
    half = SSM_W // 2
    return (ya_ref[...], ys_ref[:, :half], ys_ref[:, half:], z_ref[:, :half], z_ref[:, half:],
            an_ref[...], sn_ref[:, :half], sn_ref[:, half:])


def _out_proj_fwd(x, ya, ys, z, an, sn, gate1, w_o, tm, xchg):
    S = x.shape[0]

    def body(x_ref, ya_ref, ys_ref, z_ref, an_ref, sn_ref, g_ref, w_ref, x1_ref):
        u = _out_stage(*_out_stage_args(ya_ref, ys_ref, z_ref, an_ref, sn_ref))
        x1_ref[...] = x_ref[...] + g_ref[...] * _mm(u, w_ref[...])

    half = pl.BlockSpec((tm, ATTN_W), _row)
    hvec = pl.BlockSpec((1, ATTN_W), _fixed)
    (x1,), x_out = _hosted_call(
        body, "out_proj_fwd", S // tm,
        in_specs=[pl.BlockSpec((tm, D_MODEL), _row), half, half, half, hvec, hvec,
                  pl.BlockSpec((1, D_MODEL), _fixed), pl.BlockSpec((D_MODEL, D_MODEL), _fixed)],
        out_specs=[pl.BlockSpec((tm, D_MODEL), _row)],
        out_shape=[jax.ShapeDtypeStruct((S, D_MODEL), F32)],
        scratch_shapes=[], args=(x, ya, ys, z, an, sn, gate1, w_o), xchg=xchg, cparams=_cparams(VMEM_BIG),
    )
    return x1, x_out


def _out_proj_bwd(dx1, ya, ys, z, an, sn, gate1, w_o, tm, xchg):
    S = dx1.shape[0]
    n_steps = S // tm

    def body(dx1_ref, ya_ref, ys_ref, z_ref, an_ref, sn_ref, g_ref, w_ref,
             dya_ref, dys_ref, dz_ref, gw_ref, acc_ref, gw_acc):
        i = pl.program_id(0)

        @pl.when(i == 0)
        def _():
            acc_ref[...] = jnp.zeros_like(acc_ref)
            gw_acc[...] = jnp.zeros_like(gw_acc)

        u, vjp = jax.vjp(_out_stage, *_out_stage_args(ya_ref, ys_ref, z_ref, an_ref, sn_ref))
        dx1 = dx1_ref[...]
        ub = u.astype(MXU_DTYPE)
        mix = _mm(ub, w_ref[...])
        dmix = dx1 * g_ref[...]
        dmixb = dmix.astype(MXU_DTYPE)
        du = _mm_nt(dmixb, w_ref[...])
        gw_acc[...] += _mm_tn(ub, dmixb)
        dya, dys0, dys1, dz0, dz1, dan, dsn0, dsn1 = vjp(du)
        dya_ref[...] = dya
        dys_ref[...] = jnp.concatenate([dys0, dys1], axis=1)
        dz_ref[...] = jnp.concatenate([dz0, dz1], axis=1).astype(dz_ref.dtype)
        acc_ref[0:1, :] += jnp.sum(dx1 * mix, axis=0, keepdims=True)
        acc_ref[1:2, :] += jnp.concatenate([dan, dsn0, dsn1], axis=1)

        @pl.when(i == n_steps - 1)
        def _():
            gw_ref[...] = gw_acc[...].astype(gw_ref.dtype)

    half = pl.BlockSpec((tm, ATTN_W), _row)
    hvec = pl.BlockSpec((1, ATTN_W), _fixed)
    full = pl.BlockSpec((tm, D_MODEL), _row)
    return _hosted_call(
        body, "out_proj_bwd", n_steps,
        in_specs=[full, half, half, half, hvec, hvec,
                  pl.BlockSpec((1, D_MODEL), _fixed), pl.BlockSpec((D_MODEL, D_MODEL), _fixed)],
        out_specs=[half, half, half, pl.BlockSpec((D_MODEL, D_MODEL), _fixed), pl.BlockSpec((8, D_MODEL), _fixed)],
        out_shape=[jax.ShapeDtypeStruct((S, ATTN_W), F32)] * 2 + [jax.ShapeDtypeStruct((S, ATTN_W), MXU_DTYPE),
                   jax.ShapeDtypeStruct((D_MODEL, D_MODEL), WIRE_DTYPE), jax.ShapeDtypeStruct((8, D_MODEL), F32)],
        scratch_shapes=[pltpu.VMEM((D_MODEL, D_MODEL), F32)],
        args=(dx1, ya, ys, z, an, sn, gate1, w_o), xchg=xchg, cparams=_cparams(VMEM_BIG),
    )


def _loss_rows(x2, fn, tgt):
    y = _rms(x2, fn, D_MODEL)
    per_row = jnp.sum(jnp.square(y - tgt), axis=1, keepdims=True)
    return jnp.sum(per_row, axis=0, keepdims=True) * (0.5 / D_MODEL)


def _mlp_loss(x1, tgt, norm2, scale2, shift2, gate2, fnorm, w_gu, w_d, tm):
    S = x1.shape[0]
    n_pieces = len(w_gu) + len(w_d)

    def body(*refs):
        x1_ref, t_ref, n_ref, sc_ref, sh_ref, g_ref, fn_ref = refs[:7]
        piece_refs = refs[7:7 + n_pieces]
        dx1_ref, h_ref, dgu_ref, act_ref, dmlp_ref, acc_ref, wgu, wd, wsem = refs[7 + n_pieces:]

        @pl.when(pl.program_id(0) == 0)
        def _():
            acc_ref[...] = jnp.zeros_like(acc_ref)
            copies = []
            for dst, pieces in ((wgu, piece_refs[:len(w_gu)]), (wd, piece_refs[len(w_gu):])):
                shard = sum(p.shape[1] for p in pieces)
                off = 0
                for p in pieces:
                    for j in range(N_DEV):
                        copies.append(pltpu.make_async_copy(p.at[j], dst.at[pl.ds(j * shard + off, p.shape[1])],
                                                            wsem.at[len(copies)]))
                    off += p.shape[1]
            for cp in copies:
                cp.start()
            for cp in copies:
                cp.wait()

        x1 = x1_ref[...]
        gate2 = g_ref[...]
        h, vjp_h = jax.vjp(_modnorm, x1, n_ref[...], sc_ref[...], sh_ref[...])
        hb = h.astype(MXU_DTYPE)
        gu = _mm_nt(hb, wgu[...])
        g, u = gu[:, :D_FF], gu[:, D_FF:]
        sg = jax.nn.sigmoid(g)
        silu_g = g * sg
        act = (silu_g * u).astype(MXU_DTYPE)
        mlp = _mm(act, wd[...])
        x2 = x1 + gate2 * mlp
        loss, vjp_loss = jax.vjp(_loss_rows, x2, fn_ref[...], t_ref[...])
        dx2, dfn, _ = vjp_loss(jnp.ones((1, 1), F32))
        dmlp = (dx2 * gate2).astype(MXU_DTYPE)
        dact = _mm_nt(dmlp, wd[...])
        dg = dact * u * (sg * (1.0 + g * (1.0 - sg)))
        du = dact * silu_g
        dgu = jnp.concatenate([dg, du], axis=1).astype(MXU_DTYPE)
        dh = _mm(dgu, wgu[...])
        dx, dn, dsc, dsh = vjp_h(dh)
        dx1_ref[...] = dx2 + dx
        h_ref[...] = hb
        dgu_ref[...] = dgu
        act_ref[...] = act
        dmlp_ref[...] = dmlp
        acc_ref[0:1, :] += dn
        acc_ref[1:2, :] += dsc
        acc_ref[2:3, :] += dsh
        acc_ref[3:4, :] += jnp.sum(dx2 * mlp, axis=0, keepdims=True)
        acc_ref[4:5, :] += dfn
        acc_ref[5:6, :] += jnp.broadcast_to(loss, (1, D_MODEL))

    full = pl.BlockSpec((tm, D_MODEL), _row)
    vec = pl.BlockSpec((1, D_MODEL), _fixed)
    anyspec = pl.BlockSpec(memory_space=pl.ANY)
    return pl.pallas_call(
        body, name="mlp_loss", grid=(S // tm,),
        in_specs=[full, full, vec, vec, vec, vec, vec] + [anyspec] * n_pieces,
        out_specs=[full, full, pl.BlockSpec((tm, 2 * D_FF), _row), pl.BlockSpec((tm, D_FF), _row), full,
                   pl.BlockSpec((8, D_MODEL), _fixed)],
        out_shape=[jax.ShapeDtypeStruct((S, D_MODEL), F32), jax.ShapeDtypeStruct((S, D_MODEL), MXU_DTYPE),
                   jax.ShapeDtypeStruct((S, 2 * D_FF), MXU_DTYPE), jax.ShapeDtypeStruct((S, D_FF), MXU_DTYPE),
                   jax.ShapeDtypeStruct((S, D_MODEL), MXU_DTYPE), jax.ShapeDtypeStruct((8, D_MODEL), F32)],
        scratch_shapes=[pltpu.VMEM((2 * D_FF, D_MODEL), MXU_DTYPE), pltpu.VMEM((D_FF, D_MODEL), MXU_DTYPE),
                        pltpu.SemaphoreType.DMA((N_DEV * n_pieces,))],
        compiler_params=_cparams(VMEM_BIG),
    )(x1, tgt, norm2, scale2, shift2, gate2, fnorm, *w_gu, *w_d)


def _wgrad(a, g, tk, ts, name, xchg=None, g_cols=None):
    pieces = list(a) if isinstance(a, (list, tuple)) else [a]
    S = pieces[0].shape[0]
    K = sum(p.shape[1] for p in pieces)
    assert len(pieces) == 1 or tk == K
    N, col = (g.shape[1], 0) if g_cols is None else g_cols
    ns = S // ts
    n_a = len(pieces)

    def body(*refs):
        a_refs, (g_ref, o_ref, acc_ref) = refs[:n_a], refs[n_a:]
        s = pl.program_id(1)

        @pl.when(s == 0)
        def _():
            acc_ref[...] = jnp.zeros_like(acc_ref)

        a_blk = a_refs[0][...] if n_a == 1 else jnp.concatenate([r[...] for r in a_refs], axis=1)
        acc_ref[...] += _mm_tn(a_blk, g_ref[...])

        @pl.when(s == ns - 1)
        def _():
            o_ref[...] = acc_ref[...].astype(o_ref.dtype)

    if n_a == 1:
        in_specs = [pl.BlockSpec((ts, tk), lambda j, s: (s, j))]
    else:
        in_specs = [pl.BlockSpec((ts, p.shape[1]), lambda j, s: (s, 0)) for p in pieces]
    in_specs.append(pl.BlockSpec((ts, N), lambda j, s: (s, col)))
    out_spec = pl.BlockSpec((tk, N), lambda j, s: (j, 0))
    out_shape = jax.ShapeDtypeStruct((K, N), WIRE_DTYPE)
    scratch = [pltpu.VMEM((tk, N), F32)]
    args = (*pieces, g)
    if xchg is None:
        return pl.pallas_call(body, name=name, grid=(K // tk, ns), in_specs=in_specs, out_specs=out_spec,
                              out_shape=out_shape, scratch_shapes=scratch, compiler_params=_cparams(VMEM_BIG))(*args)
    (out,), x_out = _hosted_call(body, name, (K // tk, ns), in_specs, [out_spec], [out_shape], scratch, args, xchg,
                                 _cparams(VMEM_BIG))
    return out, x_out


SSD_CHUNKS_PER_STEP = 4
SSD_BWD_CHUNKS_PER_STEP = 2
ATTN_BLOCKS_PER_STEP = 4
MASKED = -1e30
QK_SCALE = HALF ** -0.5


def _attn_bias(buckets, rel_bias):
    def body(bk_ref, relb_ref, out_ref):
        bk = bk_ref[...]
        i = lax.broadcasted_iota(jnp.int32, (BLK, 2 * BLK), 0)
        j = lax.broadcasted_iota(jnp.int32, (BLK, 2 * BLK), 1)
        window = (j > i) & (j <= i + BLK)
        for h in range(N_HEADS):
            acc = jnp.zeros((BLK, 2 * BLK), F32)
            for b in range(N_BUCKETS):
                acc = jnp.where(bk == b, relb_ref[b, h], acc)
            out_ref[0, h] = jnp.where(window, acc, MASKED)
            out_ref[1, h] = jnp.where(window & (j >= BLK), acc, MASKED)

    return pl.pallas_call(
        body, name="attn_bias", out_shape=jax.ShapeDtypeStruct((2, N_HEADS, BLK, 2 * BLK), F32),
        in_specs=[pl.BlockSpec(memory_space=pltpu.VMEM), pl.BlockSpec(memory_space=pltpu.SMEM)],
    )(buckets, rel_bias)


def _attn_fwd(qkv, bias, sinks, xchg):
    S = qkv.shape[0]
    nb = S // BLK

    nq = ATTN_BLOCKS_PER_STEP if nb % ATTN_BLOCKS_PER_STEP == 0 else 1
    rows = nq * BLK

    def body(q_ref, kvp_ref, kvc_ref, bias_ref, sinks_ref, y_ref):
        i = pl.program_id(0)
        q = q_ref[...].astype(F32) * QK_SCALE
        kv = jnp.concatenate([kvp_ref[...], kvc_ref[...]], axis=0).astype(F32)
        k_lo, k_hi = _split_pair(kv[:, :LANE])
        v_lo, v_hi = _split_pair(kv[:, LANE:])
        bands = [[t[b * BLK:(b + 2) * BLK].astype(MXU_DTYPE) for t in (k_lo, k_hi, v_lo, v_hi)] for b in range(nq)]
        q_heads = [_split_heads(q[b * BLK:(b + 1) * BLK], 4) for b in range(nq)]
        first = [jnp.where(i == 0, 1, 0) if b == 0 else 0 for b in range(nq)]
        items = [(b, h) for b in range(nq) for h in range(N_HEADS)]
        s = [_mm_nt(q_heads[b][h].astype(MXU_DTYPE), bands[b][h // 4]) + bias_ref[first[b], h] for b, h in items]
        m = [jnp.maximum(jnp.max(s[n], axis=-1, keepdims=True), sinks_ref[h]) for n, (b, h) in enumerate(items)]
        p = [jnp.exp(s[n] - m[n]) for n in range(len(items))]
        rinv = [1.0 / (jnp.sum(p[n], axis=-1, keepdims=True) + jnp.exp(sinks_ref[h] - m[n]))
                for n, (b, h) in enumerate(items)]
        out = [_mm(p[n], bands[b][2 + h // 4]) * rinv[n] for n, (b, h) in enumerate(items)]
        y_ref[...] = jnp.concatenate([_join_heads(out[b * N_HEADS:(b + 1) * N_HEADS]) for b in range(nq)], axis=0)

    smem = pl.BlockSpec(memory_space=pltpu.SMEM)
    return _hosted_call(
        body, "attn_fwd", nb // nq,
        in_specs=[pl.BlockSpec((rows, ATTN_W), _row),
                  pl.BlockSpec((BLK, 2 * KV_W), lambda i: (jnp.maximum(i * nq - 1, 0), 2)),
                  pl.BlockSpec((rows, 2 * KV_W), lambda i: (i, 2)),
                  pl.BlockSpec((2, N_HEADS, BLK, 2 * BLK), lambda i: (0, 0, 0, 0)), smem],
        out_specs=[pl.BlockSpec((rows, ATTN_W), _row)],
        out_shape=[jax.ShapeDtypeStruct((S, ATTN_W), F32)],
        scratch_shapes=[],
        args=(qkv, qkv, qkv, bias, sinks), xchg=xchg, cparams=_cparams(),
    )


def _attn_bwd(qkv, y, dy, bias, sinks, xchg):
    S = qkv.shape[0]
    nb = S // BLK
    nq = ATTN_BLOCKS_PER_STEP if nb % ATTN_BLOCKS_PER_STEP == 0 else 1
    rows, n_steps = nq * BLK, nb // nq

    def body(q_ref, kvp_ref, kvc_ref, y_ref, dy_ref, bias_ref, sinks_ref, dq_ref, dkv_ref, dbias_ref, dsk_ref, carry_ref):
        i = pl.program_id(0)

        @pl.when(i == 0)
        def _():
            dbias_ref[...] = jnp.zeros_like(dbias_ref)
            dsk_ref[...] = jnp.zeros_like(dsk_ref)
            carry_ref[...] = jnp.zeros_like(carry_ref)

        q = q_ref[...].astype(F32) * QK_SCALE
        kv = jnp.concatenate([kvp_ref[...], kvc_ref[...]], axis=0).astype(F32)
        k_lo, k_hi = _split_pair(kv[:, :LANE])
        v_lo, v_hi = _split_pair(kv[:, LANE:])
        bands = [[t[b * BLK:(b + 2) * BLK].astype(MXU_DTYPE) for t in (k_lo, k_hi, v_lo, v_hi)] for b in range(nq)]
        rows_of = lambda ref, b: ref[b * BLK:(b + 1) * BLK, :]
        first = [jnp.where(i == n_steps - 1, 1, 0) if b == 0 else 0 for b in range(nq)]
        items = [(b, h) for b in range(nq) for h in range(N_HEADS)]
        at = lambda b, h: b * N_HEADS + h
        q_heads = [hd for b in range(nq) for hd in _split_heads(q[b * BLK:(b + 1) * BLK], 4)]
        y_heads = [hd for b in range(nq) for hd in _split_heads(rows_of(y_ref, b), 4)]
        dy_heads = [hd for b in range(nq) for hd in _split_heads(rows_of(dy_ref, b), 4)]
        qs = [q_heads[n].astype(MXU_DTYPE) for n in range(len(items))]
        s = [_mm_nt(qs[at(b, h)], bands[b][h // 4]) + bias_ref[first[b], h] for b, h in items]
        m = [jnp.maximum(jnp.max(s[at(b, h)], axis=-1, keepdims=True), sinks_ref[h]) for b, h in items]
        p = [jnp.exp(s[n] - m[n]) for n in range(len(items))]
        esink = [jnp.exp(sinks_ref[h] - m[at(b, h)]) for b, h in items]
        rinv = [1.0 / (jnp.sum(p[n], axis=-1, keepdims=True) + esink[n]) for n in range(len(items))]
        t = [dy_heads[n] * rinv[n] for n in range(len(items))]
        delta = [jnp.sum(t[n] * y_heads[n], axis=-1, keepdims=True) for n in range(len(items))]
        tb = [t[n].astype(MXU_DTYPE) for n in range(len(items))]
        dp = [_mm_nt(tb[at(b, h)], bands[b][2 + h // 4]) for b, h in items]
        ds = [p[n] * (dp[n] - delta[n]) for n in range(len(items))]
        for h in range(N_HEADS):
            ds_h, dsk_h = ds[at(0, h)], esink[at(0, h)] * delta[at(0, h)]
            for b in range(1, nq):
                ds_h = ds_h + ds[at(b, h)]
                dsk_h = dsk_h + esink[at(b, h)] * delta[at(b, h)]
            dbias_ref[h] += ds_h
            dsk_ref[h] -= dsk_h
        dsb = [ds[n].astype(MXU_DTYPE) for n in range(len(items))]
        pb = [p[n].astype(MXU_DTYPE) for n in range(len(items))]
        dq_heads = [_mm(dsb[at(b, h)], bands[b][h // 4]) * QK_SCALE for b, h in items]
        grp = lambda lst, b, g: jnp.concatenate(lst[at(b, 4 * g):at(b, 4 * g) + 4], axis=0)
        dk_pads = [[_mm_tn(grp(dsb, b, g), grp(qs, b, g)) for g in range(2)] for b in range(nq)]
        dv_pads = [[_mm_tn(grp(pb, b, g), grp(tb, b, g)) for g in range(2)] for b in range(nq)]
        dq_ref[...] = jnp.concatenate([_join_heads(dq_heads[b * N_HEADS:(b + 1) * N_HEADS]) for b in range(nq)],
                                      axis=0).astype(dq_ref.dtype)
        part = lambda b, lo: jnp.concatenate(
            [_join_pair(d[b][0][lo:lo + BLK], d[b][1][lo:lo + BLK]) for d in (dk_pads, dv_pads)], axis=1)
        dkv = [part(b, BLK) + (part(b + 1, 0) if b + 1 < nq else carry_ref[...]) for b in range(nq)]
        dkv_ref[...] = jnp.concatenate(dkv, axis=0).astype(dkv_ref.dtype)
        carry_ref[...] = part(0, 0)

    smem = pl.BlockSpec(memory_space=pltpu.SMEM)
    rev = lambda i: (n_steps - 1 - i, 0)
    return _hosted_call(
        body, "attn_bwd", n_steps,
        in_specs=[pl.BlockSpec((rows, ATTN_W), rev),
                  pl.BlockSpec((BLK, 2 * KV_W), lambda i: (jnp.maximum((n_steps - 1 - i) * nq - 1, 0), 2)),
                  pl.BlockSpec((rows, 2 * KV_W), lambda i: (n_steps - 1 - i, 2)),
                  pl.BlockSpec((rows, ATTN_W), rev), pl.BlockSpec((rows, ATTN_W), rev),
                  pl.BlockSpec((2, N_HEADS, BLK, 2 * BLK), lambda i: (0, 0, 0, 0)), smem],
        out_specs=[pl.BlockSpec((rows, ATTN_W), rev), pl.BlockSpec((rows, 2 * KV_W), rev),
                   pl.BlockSpec((N_HEADS, BLK, 2 * BLK), lambda i: (0, 0, 0)),
                   pl.BlockSpec((N_HEADS, BLK, 1), lambda i: (0, 0, 0))],
        out_shape=[jax.ShapeDtypeStruct((S, ATTN_W), MXU_DTYPE), jax.ShapeDtypeStruct((S, 2 * KV_W), MXU_DTYPE),
                   jax.ShapeDtypeStruct((N_HEADS, BLK, 2 * BLK), F32), jax.ShapeDtypeStruct((N_HEADS, BLK, 1), F32)],
        scratch_shapes=[pltpu.VMEM((BLK, 2 * KV_W), F32)],
        args=(qkv, qkv, qkv, y, dy, bias, sinks), xchg=xchg, cparams=_cparams(),
    )


def _attn_finish(dbias, dsk, buckets):
    def body(db_ref, dsk_ref, bk_ref, drel_ref, dsink_ref):
        bk = bk_ref[...]
        r = lax.broadcasted_iota(jnp.int32, (N_BUCKETS, LANE), 0)
        l = lax.broadcasted_iota(jnp.int32, (N_BUCKETS, LANE), 1)
        row = lax.broadcasted_iota(jnp.int32, (N_HEADS, LANE), 0)
        res = jnp.zeros((N_BUCKETS, LANE), F32)
        dsink = jnp.zeros((N_HEADS, LANE), F32)
        for h in range(N_HEADS):
            db = db_ref[h]
            for b in range(N_BUCKETS):
                v = jnp.sum(jnp.sum(jnp.where(bk == b, db, 0.0), axis=1, keepdims=True), axis=0, keepdims=True)
                res = res + jnp.where((r == b) & (l == h), v, 0.0)
            dsink = dsink + jnp.where(row == h, jnp.sum(dsk_ref[h], axis=0, keepdims=True), 0.0)
        drel_ref[...] = res
        dsink_ref[...] = dsink

    return pl.pallas_call(body, name="attn_finish",
                          out_shape=[jax.ShapeDtypeStruct((N_BUCKETS, LANE), F32),
                                     jax.ShapeDtypeStruct((N_HEADS, LANE), F32)])(dbias, dsk, buckets)


def _ssd_consts():
    r = lax.broadcasted_iota(jnp.int32, (BLK, BLK), 0)
    c = lax.broadcasted_iota(jnp.int32, (BLK, BLK), 1)
    causal = c <= r
    upper = (r <= c).astype(F32)
    last = r == BLK - 1
    head = lax.broadcasted_iota(jnp.int32, (N_HEADS, BLK), 0)
    return causal, upper, last, head


def _ssd_chunks(xs, bg, cg, dt_raw_t, prev0, dtb, alog, d_rows, consts):
    causal, upper, last, head = consts
    nq = len(xs)
    items = [(c, h) for c in range(nq) for h in range(N_HEADS)]
    at = lambda c, h: c * N_HEADS + h
    a_neg = -jnp.exp(alog)
    dt_t = [_softplus(dt_raw_t[c] + dtb) for c in range(nq)]
    acs_t = [_mm_hi(dt_t[c] * a_neg, upper) for c in range(nq)]
    cb = [[_mm_nt(cg[c][g], bg[c][g]) for g in range(2)] for c in range(nq)]
    pick = lambda t, h: jnp.sum(jnp.where(head == h, t, 0.0), axis=0, keepdims=True)
    dt_row = [pick(dt_t[c], h) for c, h in items]
    a_row = [pick(acs_t[c], h) for c, h in items]
    a_rb = [jnp.broadcast_to(a_row[n], (BLK, BLK)) for n in range(len(items))]
    a_b = [a_rb[n].T for n in range(len(items))]
    a_last = [jnp.sum(jnp.where(last, a_b[n], 0.0), axis=0, keepdims=True) for n in range(len(items))]
    w = [cb[c][h // 4] * jnp.exp(jnp.where(causal, a_b[at(c, h)] - a_rb[at(c, h)], -1e30)) * dt_row[at(c, h)]
         for c, h in items]
    f_b = [jnp.broadcast_to(dt_row[n] * jnp.exp(a_last[n] - a_row[n]), (BLK, BLK)).T for n in range(len(items))]
    y_in = [_mm(w[at(c, h)], xs[c][h]) for c, h in items]
    st = [_mm_tn(bg[c][h // 4], xs[c][h] * f_b[at(c, h)]) for c, h in items]
    e_b = [jnp.exp(a_b[n]) for n in range(len(items))]
    states = [list(prev0)]
    for c in range(nq):
        states.append([states[c][h] * jnp.exp(a_last[at(c, h)]) + st[at(c, h)] for h in range(N_HEADS)])
    y_off = [_mm(cg[c][h // 4], states[c][h]) * e_b[at(c, h)] for c, h in items]
    ys = [[y_in[at(c, h)] + y_off[at(c, h)] + d_rows[h] * xs[c][h] for h in range(N_HEADS)] for c in range(nq)]
    return ys, states


def _ssd_chunks_bwd(xs, bg, cg, dt_raw_t, prev, dtb, alog, d_rows, dys, dh_last, consts):
    causal, upper, last, head = consts
    nq = len(xs)
    items = [(c, h) for c in range(nq) for h in range(N_HEADS)]
    ni = len(items)
    at = lambda c, h: c * N_HEADS + h
    groups = [(c, g) for c in range(nq) for g in range(2)]
    lane = _lane_iota((BLK, BLK))
    lane_row = _lane_iota((1, BLK))
    a_neg = -jnp.exp(alog)
    pre_dt = [dt_raw_t[c] + dtb for c in range(nq)]
    dt_t = [_softplus(pre_dt[c]) for c in range(nq)]
    acs_t = [_mm_hi(dt_t[c] * a_neg, upper) for c in range(nq)]
    pick = lambda t, h: jnp.sum(jnp.where(head == h, t, 0.0), axis=0, keepdims=True)
    full_sum = lambda t: jnp.sum(jnp.sum(t, axis=1, keepdims=True), axis=0, keepdims=True)
    dt_row = [pick(dt_t[c], h) for c, h in items]
    a_row = [pick(acs_t[c], h) for c, h in items]
    a_rb = [jnp.broadcast_to(a_row[n], (BLK, BLK)) for n in range(ni)]
    a_b = [a_rb[n].T for n in range(ni)]
    a_last = [jnp.sum(jnp.where(last, a_b[n], 0.0), axis=0, keepdims=True) for n in range(ni)]
    lm = [jnp.exp(jnp.where(causal, a_b[n] - a_rb[n], -1e30)) for n in range(ni)]
    cgb = [[cg[c][g].astype(MXU_DTYPE) for g in range(2)] for c in range(nq)]
    bgb = [[bg[c][g].astype(MXU_DTYPE) for g in range(2)] for c in range(nq)]
    cb = [[_mm_nt(cgb[c][g], bgb[c][g]) for g in range(2)] for c in range(nq)]
    u = [cb[c][h // 4] * lm[at(c, h)] for c, h in items]
    w = [(u[n] * dt_row[n]).astype(MXU_DTYPE) for n in range(ni)]
    e_row = [jnp.exp(a_last[n] - a_row[n]) for n in range(ni)]
    f_row = [dt_row[n] * e_row[n] for n in range(ni)]
    f_b = [jnp.broadcast_to(f_row[n], (BLK, BLK)).T for n in range(ni)]
    e_b = [jnp.exp(a_b[n]) for n in range(ni)]
    el = [jnp.exp(a_last[n]) for n in range(ni)]
    xb = [xs[c][h].astype(MXU_DTYPE) for c, h in items]
    dyb = [dys[c][h].astype(MXU_DTYPE) for c, h in items]
    prevb = [prev[c][h].astype(MXU_DTYPE) for c, h in items]
    gmat = [_mm(cgb[c][h // 4], prevb[at(c, h)]) for c, h in items]
    dw = [_mm_nt(dyb[n], xb[n]) for n in range(ni)]
    dg = [dys[c][h] * e_b[at(c, h)] for c, h in items]
    dgb = [dg[n].astype(MXU_DTYPE) for n in range(ni)]
    from_y = [_mm_tn(cgb[c][h // 4], dgb[at(c, h)]) for c, h in items]
    dhs = [None] * ni
    dprev = [None] * ni
    for c in reversed(range(nq)):
        for h in range(N_HEADS):
            dhs[at(c, h)] = dh_last[h] if c == nq - 1 else dprev[at(c + 1, h)]
            dprev[at(c, h)] = from_y[at(c, h)] + dhs[at(c, h)] * el[at(c, h)]
    dstb = [dhs[n].astype(MXU_DTYPE) for n in range(ni)]
    dxf = [_mm(bgb[c][h // 4], dstb[at(c, h)]) for c, h in items]
    xfb = [(xs[c][h] * f_b[at(c, h)]).astype(MXU_DTYPE) for c, h in items]
    dxs = [_mm_tn(w[at(c, h)], dyb[at(c, h)]) + d_rows[h] * dys[c][h] + f_b[at(c, h)] * dxf[at(c, h)]
           for c, h in items]
    dd_item = [jnp.sum(dys[c][h] * xs[c][h], axis=0, keepdims=True) for c, h in items]
    dcg_h = [_mm_nt(dgb[n], prevb[n]) for n in range(ni)]
    dbg_h = [_mm_nt(xfb[n], dstb[n]) for n in range(ni)]
    zt = [dw[n] * u[n] for n in range(ni)]
    dseg = [zt[n] * dt_row[n] for n in range(ni)]
    dcb_h = [dw[n] * lm[n] * dt_row[n] for n in range(ni)]
    four = lambda lst, c, g: lst[at(c, 4 * g)] + lst[at(c, 4 * g + 1)] + lst[at(c, 4 * g + 2)] + lst[at(c, 4 * g + 3)]
    dcb = {(c, g): four(dcb_h, c, g).astype(MXU_DTYPE) for c, g in groups}
    dcg = [[four(dcg_h, c, g) + _mm(dcb[c, g], bgb[c][g]) for g in range(2)] for c in range(nq)]
    dbg = [[four(dbg_h, c, g) + _mm_tn(dcb[c, g], cgb[c][g]) for g in range(2)] for c in range(nq)]
    r1 = [jnp.sum(dg[n] * gmat[n] + dseg[n], axis=1, keepdims=True) for n in range(ni)]
    r2 = [jnp.sum(dxf[at(c, h)] * xs[c][h], axis=1, keepdims=True) for c, h in items]
    tt = [jnp.where(lane < HALF, jnp.broadcast_to(r1[n], (BLK, BLK)), jnp.broadcast_to(r2[n], (BLK, BLK))).T
          for n in range(ni)]
    r1_row = [tt[n][0:1, :] for n in range(ni)]
    r2_row = [tt[n][HALF:HALF + 1, :] for n in range(ni)]
    d_el = [full_sum(dhs[at(c, h)] * prev[c][h]) for c, h in items]
    da_last = [jnp.sum(r2_row[n] * f_row[n], axis=1, keepdims=True) + el[n] * d_el[n] for n in range(ni)]
    da_row = [r1_row[n] - jnp.sum(dseg[n], axis=0, keepdims=True) - r2_row[n] * f_row[n]
              + jnp.where(lane_row == BLK - 1, da_last[n], 0.0) for n in range(ni)]
    ddt_row = [jnp.sum(zt[n], axis=0, keepdims=True) + r2_row[n] * e_row[n] for n in range(ni)]
    draw, dalog = [], jnp.zeros((N_HEADS, BLK), F32)
    for c in range(nq):
        da_t = jnp.zeros((N_HEADS, BLK), F32)
        ddt_t = jnp.zeros((N_HEADS, BLK), F32)
        for h in range(N_HEADS):
            da_t = jnp.where(head == h, da_row[at(c, h)], da_t)
            ddt_t = jnp.where(head == h, ddt_row[at(c, h)], ddt_t)
        d_dta = _mm_hi(da_t, causal.astype(F32))
        dalog = dalog + d_dta * dt_t[c] * a_neg
        draw.append((ddt_t + d_dta * a_neg) * jax.nn.sigmoid(pre_dt[c]))
    ddtb = draw[0]
    for c in range(1, nq):
        ddtb = ddtb + draw[c]
    dd_rows = []
    for h in range(N_HEADS):
        t = dd_item[at(0, h)]
        for c in range(1, nq):
            t = t + dd_item[at(c, h)]
        dd_rows.append(t)
    return ([dxs[c * N_HEADS:(c + 1) * N_HEADS] for c in range(nq)], dbg, dcg, draw,
            [dprev[at(0, h)] for h in range(N_HEADS)], ddtb, dalog, dd_rows)


def _dt_rows(dt_blk):
    return dt_blk.T[:N_HEADS]


def _silu_grad(x):
    s = jax.nn.sigmoid(x)
    return s * (1.0 + x * (1.0 - s))


def _conv_pre(halo, blk, cw_ref, cb_ref):
    ext = jnp.concatenate([halo, blk], axis=0)
    taps = [pltpu.roll(ext, 3 - k, 0)[8:] for k in range(3)] + [blk]
    pre = cb_ref[...] + cw_ref[0:1, :] * taps[0]
    for k in range(1, 4):
        pre = pre + cw_ref[k:k + 1, :] * taps[k]
    return pre, taps


def _ssd_split(pre):
    heads = _split_heads(pre[:, :SSM_W], 4)
    pb = [pre[:, SSM_W + g * D_STATE:SSM_W + (g + 1) * D_STATE] for g in range(2)]
    pc = [pre[:, SSM_W + 2 * D_STATE + g * D_STATE:SSM_W + 2 * D_STATE + (g + 1) * D_STATE] for g in range(2)]
    return heads, pb, pc


def _ssd_fwd(xbc, dt_raw, conv_w, conv_b, dtb_row, alog_row, d_exp, xchg):
    S = xbc.shape[0]
    nc = S // BLK
    nq = SSD_CHUNKS_PER_STEP if nc % SSD_CHUNKS_PER_STEP == 0 else 1
    rows = nq * BLK

    def body(xbc_ref, halo_ref, dt_ref, cw_ref, cb_ref, dtb_ref, alog_ref, d_ref, y_ref, prev_ref, state_ref):
        i = pl.program_id(0)

        @pl.when(i == 0)
        def _():
            state_ref[...] = jnp.zeros_like(state_ref)

        halo = halo_ref[...] * jnp.where(i > 0, 1.0, 0.0)
        pre, _ = _conv_pre(halo, xbc_ref[...], cw_ref, cb_ref)
        xc = _silu(pre)
        split = [_ssd_split(xc[c * BLK:(c + 1) * BLK]) for c in range(nq)]
        dt_t = [_dt_rows(dt_ref[c * BLK:(c + 1) * BLK, :]) for c in range(nq)]
        prev0 = [state_ref[h] for h in range(N_HEADS)]
        d_rows = [d_ref[h:h + 1, :] for h in range(N_HEADS)]
        ys, states = _ssd_chunks([s[0] for s in split], [s[1] for s in split], [s[2] for s in split], dt_t, prev0,
                                 dtb_ref[...], alog_ref[...], d_rows, _ssd_consts())
        for h in range(N_HEADS):
            for c in range(nq):
                prev_ref[c, h] = states[c][h]
            state_ref[h] = states[nq][h]
        y_ref[...] = jnp.concatenate([_join_heads(ys[c]) for c in range(nq)], axis=0)

    vec = pl.BlockSpec((N_HEADS, LANE), _fixed)
    return _hosted_call(
        body, "ssd_fwd", nc // nq,
        in_specs=[pl.BlockSpec((rows, XBC_W), _row),
                  pl.BlockSpec((8, XBC_W), lambda i: (jnp.maximum(i * (rows // 8) - 1, 0), 0)),
                  pl.BlockSpec((rows, LANE), _row),
                  pl.BlockSpec((4, XBC_W), _fixed), pl.BlockSpec((1, XBC_W), _fixed), vec, vec,
                  pl.BlockSpec((N_HEADS, LANE), _fixed)],
        out_specs=[pl.BlockSpec((rows, SSM_W), _row),
                   pl.BlockSpec((nq, N_HEADS, D_STATE, LANE), lambda i: (i, 0, 0, 0))],
        out_shape=[jax.ShapeDtypeStruct((S, SSM_W), F32), jax.ShapeDtypeStruct((nc, N_HEADS, D_STATE, LANE), F32)],
        scratch_shapes=[pltpu.VMEM((N_HEADS, D_STATE, LANE), F32)],
        args=(xbc, xbc, dt_raw, conv_w, conv_b, dtb_row, alog_row, d_exp), xchg=xchg, cparams=_cparams(),
    )


def _ssd_bwd(xbc, dt_raw, prev_states, dy, conv_w, conv_b, dtb_row, alog_row, d_exp, xchg):
    S = xbc.shape[0]
    nc = S // BLK
    nq = SSD_BWD_CHUNKS_PER_STEP if nc % SSD_BWD_CHUNKS_PER_STEP == 0 else 1
    rows, n_steps = nq * BLK, nc // nq

    def body(xbc_ref, halo_ref, dt_ref, prev_ref, dy_ref, cw_ref, cb_ref, dtb_ref, alog_ref, d_ref,
             dxbc_ref, ddt_ref, dcw_ref, dvec_ref, dd_ref, gstate_ref, ghalo_ref):
        i = pl.program_id(0)

        @pl.when(i == 0)
        def _():
            gstate_ref[...] = jnp.zeros_like(gstate_ref)
            ghalo_ref[...] = jnp.zeros_like(ghalo_ref)
            dcw_ref[...] = jnp.zeros_like(dcw_ref)
            dvec_ref[...] = jnp.zeros_like(dvec_ref)
            dd_ref[...] = jnp.zeros_like(dd_ref)

        halo = halo_ref[...] * jnp.where(i < n_steps - 1, 1.0, 0.0)
        pre, taps = _conv_pre(halo, xbc_ref[...], cw_ref, cb_ref)
        xc = _silu(pre)
        split = [_ssd_split(xc[c * BLK:(c + 1) * BLK]) for c in range(nq)]
        dt_t = [_dt_rows(dt_ref[c * BLK:(c + 1) * BLK, :]) for c in range(nq)]
        prev = [[prev_ref[c, h] for h in range(N_HEADS)] for c in range(nq)]
        d_rows = [d_ref[h:h + 1, :] for h in range(N_HEADS)]
        dys = [_split_heads(dy_ref[c * BLK:(c + 1) * BLK, :], 4) for c in range(nq)]
        dh_last = [gstate_ref[h] for h in range(N_HEADS)]
        dheads, dpb, dpc, ddt_t, dprev0, ddtb, dalog, dd_rows = _ssd_chunks_bwd(
            [s[0] for s in split], [s[1] for s in split], [s[2] for s in split], dt_t, prev, dtb_ref[...],
            alog_ref[...], d_rows, dys, dh_last, _ssd_consts())
        for h in range(N_HEADS):
            gstate_ref[h] = dprev0[h]
            dd_ref[h:h + 1, :] += dd_rows[h]
        pad = jnp.zeros((BLK - N_HEADS, BLK), F32)
        ddt_ref[...] = jnp.concatenate([jnp.concatenate([ddt_t[c], pad], axis=0).T for c in range(nq)],
                                       axis=0).astype(ddt_ref.dtype)
        dvec_ref[0:N_HEADS, :] += ddtb
        dvec_ref[N_HEADS:, :] += dalog
        dxc = jnp.concatenate([jnp.concatenate([_join_heads(dheads[c])] + list(dpb[c]) + list(dpc[c]), axis=1)
                               for c in range(nq)], axis=0)
        dpre = dxc * _silu_grad(pre)
        zeros8 = jnp.zeros((8, XBC_W), F32)
        dpe = jnp.concatenate([zeros8, dpre, zeros8], axis=0)
        n_ext = 16 + rows
        dext = cw_ref[3:4, :] * dpe[:8 + rows]
        dcw_ref[3:4, :] += jnp.sum(dpre * taps[3], axis=0, keepdims=True)
        for k in range(3):
            dext = dext + cw_ref[k:k + 1, :] * pltpu.roll(dpe, n_ext - (3 - k), 0)[:8 + rows]
            dcw_ref[k:k + 1, :] += jnp.sum(dpre * taps[k], axis=0, keepdims=True)
        dcw_ref[4:5, :] += jnp.sum(dpre, axis=0, keepdims=True)
        dxbc_ref[...] = jnp.concatenate([dext[8:rows], dext[rows:] + ghalo_ref[...]], axis=0).astype(dxbc_ref.dtype)
        ghalo_ref[...] = dext[:8, :]

    vec = pl.BlockSpec((N_HEADS, LANE), _fixed)
    rev = lambda i: (n_steps - 1 - i, 0)
    return _hosted_call(
        body, "ssd_bwd", n_steps,
        in_specs=[pl.BlockSpec((rows, XBC_W), rev),
                  pl.BlockSpec((8, XBC_W), lambda i: (jnp.maximum((n_steps - 1 - i) * (rows // 8) - 1, 0), 0)),
                  pl.BlockSpec((rows, LANE), rev),
                  pl.BlockSpec((nq, N_HEADS, D_STATE, LANE), lambda i: (n_steps - 1 - i, 0, 0, 0)),
                  pl.BlockSpec((rows, SSM_W), rev),
                  pl.BlockSpec((4, XBC_W), _fixed), pl.BlockSpec((1, XBC_W), _fixed), vec, vec,
                  pl.BlockSpec((N_HEADS, LANE), _fixed)],
        out_specs=[pl.BlockSpec((rows, XBC_W), rev), pl.BlockSpec((rows, LANE), rev),
                   pl.BlockSpec((8, XBC_W), _fixed), pl.BlockSpec((2 * N_HEADS, LANE), _fixed),
                   pl.BlockSpec((N_HEADS, LANE), _fixed)],
        out_shape=[jax.ShapeDtypeStruct((S, XBC_W), MXU_DTYPE), jax.ShapeDtypeStruct((S, LANE), MXU_DTYPE),
                   jax.ShapeDtypeStruct((8, XBC_W), F32), jax.ShapeDtypeStruct((2 * N_HEADS, LANE), F32),
                   jax.ShapeDtypeStruct((N_HEADS, LANE), F32)],
        scratch_shapes=[pltpu.VMEM((N_HEADS, D_STATE, LANE), F32), pltpu.VMEM((8, XBC_W), F32)],
        args=(xbc, xbc, dt_raw, prev_states, dy, conv_w, conv_b, dtb_row, alog_row, d_exp), xchg=xchg,
        cparams=_cparams(VMEM_BIG),
    )


def _adamw_math(w, g, m, v):
    m = ADAM_B1 * m + (1.0 - ADAM_B1) * g
    v = ADAM_B2 * v + (1.0 - ADAM_B2) * jnp.square(g)
    m_hat = m / (1.0 - ADAM_B1 ** ADAM_STEP)
    v_hat = v / (1.0 - ADAM_B2 ** ADAM_STEP)
    delta = -ADAM_LR * (m_hat / (jnp.sqrt(v_hat) + ADAM_EPS) + ADAM_WD * w)
    return delta, m, v


def _reduce_adamw(parts, w, m, v, name):
    R, C = w.shape

    def body(p_ref, w_ref, m_ref, v_ref, g_ref, d_ref, nm_ref, nv_ref):
        g = p_ref[0].astype(F32)
        for i in range(1, N_DEV):
            g = g + p_ref[i].astype(F32)
        d, nm, nv = _adamw_math(w_ref[...], g, m_ref[...], v_ref[...])
        g_ref[...] = g
        d_ref[...] = d
        nm_ref[...] = nm
        nv_ref[...] = nv

    if R % 16 == 0:
        tr = max(t for t in range(16, 257, 16) if R % t == 0)
        n, blk, pblk = R // tr, pl.BlockSpec((tr, C), _row), pl.BlockSpec((N_DEV, tr, C), lambda i: (0, i, 0))
    else:
        tl = 256
        n, blk, pblk = C // tl, pl.BlockSpec((R, tl), lambda i: (0, i)), pl.BlockSpec((N_DEV, R, tl),
                                                                                      lambda i: (0, 0, i))
    return pl.pallas_call(
        body, name=name, grid=(n,), in_specs=[pblk, blk, blk, blk],
        out_specs=[blk] * 4, out_shape=[jax.ShapeDtypeStruct((R, C), F32)] * 4,
    )(parts, w, m, v)


def _reduce_adamw_hosting(parts_list, wmv_list, name, xchg):
    n_arr = len(parts_list)
    C = wmv_list[0][0].shape[1]
    tl = 256

    def body(*refs):
        p_refs, wmv_refs, o_refs = refs[:n_arr], refs[n_arr:4 * n_arr], refs[4 * n_arr:]
        for k in range(n_arr):
            g = p_refs[k][0].astype(F32)
            for i in range(1, N_DEV):
                g = g + p_refs[k][i].astype(F32)
            w_ref, m_ref, v_ref = wmv_refs[3 * k:3 * k + 3]
            d, nm, nv = _adamw_math(w_ref[...], g, m_ref[...], v_ref[...])
            for o, val in zip(o_refs[4 * k:4 * k + 4], (g, d, nm, nv)):
                o[...] = val

    in_specs = [pl.BlockSpec((N_DEV, w.shape[0], tl), lambda i: (0, 0, i)) for w, _, _ in wmv_list]
    in_specs += [pl.BlockSpec((w.shape[0], tl), lambda i: (0, i)) for w, _, _ in wmv_list for _ in range(3)]
    out_specs = [pl.BlockSpec((w.shape[0], tl), lambda i: (0, i)) for w, _, _ in wmv_list for _ in range(4)]
    out_shape = [jax.ShapeDtypeStruct(w.shape, F32) for w, _, _ in wmv_list for _ in range(4)]
    args = list(parts_list) + [a for wmv in wmv_list for a in wmv]
    outs, x_out = _hosted_call(body, name, C // tl, in_specs, out_specs, out_shape, [], args, xchg,
                               _cparams(VMEM_BIG))
    return [outs[4 * k:4 * k + 4] for k in range(n_arr)], x_out


_SMALL_NAMES = ("ada_b", "norm1", "conv_w", "conv_b", "dt_bias", "A_log", "D_skip", "sinks", "attn_out_norm",
                "ssm_out_norm", "norm2", "rel_bias", "final_norm")
N_MOD = 6 * D_MODEL


def _mod_row(a0, a1, a2):
    return jnp.concatenate([a0[2:3], a0[1:2], a1[0:1], a2[2:3], a2[1:2], a2[3:4]], axis=1)


def _small_update(gathered, params):
    n_g = len(gathered)
    flat = [a for name in _SMALL_NAMES for a in params[name]]

    def body(*refs):
        a0_ref, a1_ref, a2_ref, cw_ref, dv_ref, dd_ref, ds_ref, dr_ref, c_ref = refs[:n_g]
        wmv = refs[n_g:n_g + len(flat)]
        outs = refs[n_g + len(flat):]

        def total(ref):
            t = ref[0]
            for i in range(1, N_DEV):
                t = t + ref[i]
            return t

        t0, t1, t2, tcw, tdv, tdd, tds, tdr = [total(r) for r in (a0_ref, a1_ref, a2_ref, cw_ref, dv_ref, dd_ref,
                                                                   ds_ref, dr_ref)]
        r8 = lax.broadcasted_iota(jnp.int32, (N_HEADS, LANE), 0)
        l8 = lax.broadcasted_iota(jnp.int32, (N_HEADS, LANE), 1)

        def diag_row(t):
            return jnp.sum(jnp.where(r8 == l8, t, 0.0), axis=0, keepdims=True)[:, :N_HEADS]

        def lane_sums(t):
            return diag_row(jnp.broadcast_to(jnp.sum(t, axis=1, keepdims=True), (N_HEADS, LANE)))

        me = _lin(_my_pos())
        n_cw = XBC_W // N_DEV
        cw_mine = jnp.zeros((4, n_cw), F32)
        for j in range(N_DEV):
            cw_mine = cw_mine + tcw[0:4, j * n_cw:(j + 1) * n_cw] * jnp.where(me == j, 1.0, 0.0)
        grads = {
            "ada_b": _mod_row(t0, t1, t2), "norm1": t0[0:1], "conv_w": cw_mine, "conv_b": tcw[4:5],
            "dt_bias": lane_sums(tdv[:N_HEADS]), "A_log": lane_sums(tdv[N_HEADS:]), "D_skip": lane_sums(tdd),
            "sinks": diag_row(tds), "attn_out_norm": t1[1:2, :ATTN_W], "ssm_out_norm": t1[1:2, ATTN_W:],
            "norm2": t2[0:1], "rel_bias": tdr[:, :N_HEADS], "final_norm": t2[4:5],
        }
        for k, name in enumerate(_SMALL_NAMES):
            w_ref, m_ref, v_ref = wmv[3 * k:3 * k + 3]
            g = grads[name]
            d, nm, nv = _adamw_math(w_ref[...], g, m_ref[...], v_ref[...])
            for o, val in zip(outs[4 * k:4 * k + 4], (g, d, nm, nv)):
                o[...] = val
        loss_ref, call_ref, dmod_ref = outs[4 * len(_SMALL_NAMES):]
        loss_ref[...] = t2[5:6, 0:1]
        call_ref[...] = jnp.concatenate([c_ref[i] for i in range(N_DEV)], axis=0)
        dmod_ref[...] = jnp.concatenate([_mod_row(a0_ref[i], a1_ref[i], a2_ref[i]) for i in range(N_DEV)], axis=0)

    out_shape = [jax.ShapeDtypeStruct(params[name][0].shape, F32) for name in _SMALL_NAMES for _ in range(4)]
    out_shape += [jax.ShapeDtypeStruct((1, 1), F32), jax.ShapeDtypeStruct((N_DEV, D_MODEL), F32),
                  jax.ShapeDtypeStruct((N_DEV, N_MOD), F32)]
    res = pl.pallas_call(body, name="small_update", out_shape=out_shape)(*gathered, *flat)
    upd = {name: res[4 * k:4 * k + 4] for k, name in enumerate(_SMALL_NAMES)}
    loss, c_all, dmod_all = res[4 * len(_SMALL_NAMES):]
    return upd, loss, c_all, dmod_all


def _ada_w_update(c_all, dmod_all, w, m, v):
    chunk = w.shape[1]

    def body(c_ref, dm_ref, w_ref, m_ref, v_ref, g_ref, d_ref, nm_ref, nv_ref):
        me = _lin(_my_pos())
        dm = jnp.zeros((N_DEV, chunk), F32)
        for j in range(N_DEV):
            dm = dm + dm_ref[:, j * chunk:(j + 1) * chunk] * jnp.where(me == j, 1.0, 0.0)
        g = lax.dot_general(_silu(c_ref[...]), dm, (((0,), (0,)), ((), ())), precision=HI,
                            preferred_element_type=F32)
        d, nm, nv = _adamw_math(w_ref[...], g, m_ref[...], v_ref[...])
        g_ref[...] = g
        d_ref[...] = d
        nm_ref[...] = nm
        nv_ref[...] = nv

    tr = 256
    blk = pl.BlockSpec((tr, chunk), _row)
    return pl.pallas_call(
        body, name="ada_w_update", grid=(w.shape[0] // tr,),
        in_specs=[pl.BlockSpec((N_DEV, tr), lambda i: (0, i)), pl.BlockSpec(dmod_all.shape, _fixed), blk, blk, blk],
        out_specs=[blk] * 4, out_shape=[jax.ShapeDtypeStruct(w.shape, F32)] * 4,
    )(c_all, dmod_all, w, m, v)


def _local_step(x, tgt, c, mod, w_in, conv_w, w_o_mine, w_gu_mine, w_d_mine, p):
    S = x.shape[0]
    tm = min(512, S)
    tmm = min(256, S)
    tw = min(2048, S)
    shift1, scale1, gate1, shift2, scale2, gate2 = [mod[i:i + 1] for i in range(6)]
    buckets = jnp.asarray(_t5_bucket_table())
    per_head = lambda a: jnp.broadcast_to(a.reshape(N_HEADS, 1), (N_HEADS, LANE))
    dtb_row, alog_row, d_exp = per_head(p["dt_bias"]), per_head(p["A_log"]), per_head(p["D_skip"])
    sinks = p["sinks"].reshape(N_HEADS)

    d_cut, gu_cut = WD_CUT, WGU_CUTS
    (qkv, z, xbc, dt_raw), (g_d_a,) = _in_proj_fwd(x, p["norm1"], scale1, shift1, w_in, tm,
                                                   ([w_d_mine[:d_cut]], "two-level"))
    bias = _attn_bias(buckets, p["rel_bias"])
    (ya,), (g_gu_a,) = _attn_fwd(qkv, bias, sinks, ([w_gu_mine[:gu_cut[0]]], "two-level"))
    (ys, prev_states), (g_gu_b, g_o) = _ssd_fwd(xbc, dt_raw, conv_w, p["conv_b"], dtb_row, alog_row, d_exp,
                                                ([w_gu_mine[gu_cut[0]:gu_cut[1]], w_o_mine], "two-level"))
    w_o = g_o.reshape(D_MODEL, D_MODEL)
    x1, (g_gu_c, g_d_b) = _out_proj_fwd(x, ya, ys, z, p["attn_out_norm"], p["ssm_out_norm"], gate1, w_o, tm,
                                        ([w_gu_mine[gu_cut[1]:], w_d_mine[d_cut:]], "two-level"))
    dx1, h2, dgu, act, dmlp, acc2 = _mlp_loss(x1, tgt, p["norm2"], scale2, shift2, gate2, p["final_norm"],
                                              (g_gu_a, g_gu_b, g_gu_c), (g_d_a, g_d_b), tmm)
    g_w_gu = _wgrad(dgu, h2, 2 * D_FF // 4, tw, "wgrad_gate_up")
    g_w_d = _wgrad(act, dmlp, D_FF // 2, tw, "wgrad_down")
    (dya, dys, dz, g_w_o, acc1), (r_d,) = _out_proj_bwd(
        dx1, ya, ys, z, p["attn_out_norm"], p["ssm_out_norm"], gate1, w_o, tm,
        ([g_w_d.reshape(N_DEV, D_FF // N_DEV, D_MODEL)], True))
    (dq, dkv, dbias, dsk), (r_o,) = _attn_bwd(qkv, ya, dya, bias, sinks,
                                              ([g_w_o.reshape(N_DEV, D_MODEL // N_DEV, D_MODEL)], True))
    drel, dsink = _attn_finish(dbias, dsk, buckets)
    (dxbc, ddt, dcw, dvec, dd), (r_gu,) = _ssd_bwd(
        xbc, dt_raw, prev_states, dys, conv_w, p["conv_b"], dtb_row, alog_row, d_exp,
        ([g_w_gu.reshape(N_DEV, 2 * D_FF // N_DEV, D_MODEL)], True))
    gx, h1, acc0 = _in_proj_bwd(x, dx1, dq, dkv, dz, dxbc, ddt, p["norm1"], scale1, shift1, w_in, tm)
    dproj = (dq, dkv, dz, dxbc, ddt)
    half = D_MODEL // 2
    slots = lambda g: g[:IN_W].reshape(N_DEV, IN_W // N_DEV, half)
    g_in_a, gathered = _wgrad(dproj, h1, IN_PAD, tw, "wgrad_in_a",
                              ([acc0, acc1, acc2, dcw, dvec, dd, dsink, drel, c], False), g_cols=(half, 0))
    g_in_b, (r_in_a,) = _wgrad(dproj, h1, IN_PAD, tw, "wgrad_in_b", ([slots(g_in_a)], True), g_cols=(half, 1))
    return gx, (r_in_a, slots(g_in_b)), (r_o, r_gu, r_d), gathered


def kernel(x, c, ada_w, ada_b, norm1, w_in, conv_w, conv_b, dt_bias, A_log, D_skip, sinks, attn_out_norm, ssm_out_norm, w_o, norm2, w_gate_up, w_down, rel_bias, final_norm, loss_target, m_ada_w, m_ada_b, m_norm1, m_w_in, m_conv_w, m_conv_b, m_dt_bias, m_A_log, m_D_skip, m_sinks, m_attn_out_norm, m_ssm_out_norm, m_w_o, m_norm2, m_w_gate_up, m_w_down, m_rel_bias, m_final_norm, v_ada_w, v_ada_b, v_norm1, v_w_in, v_conv_w, v_conv_b, v_dt_bias, v_A_log, v_D_skip, v_sinks, v_attn_out_norm, v_ssm_out_norm, v_w_o, v_norm2, v_w_gate_up, v_w_down, v_rel_bias, v_final_norm):
    two_d = lambda a: a if a.ndim == 2 else a.reshape(-1, a.shape[-1])
    small_params = dict(
        ada_b=(ada_b, m_ada_b, v_ada_b), norm1=(norm1, m_norm1, v_norm1), conv_w=(conv_w, m_conv_w, v_conv_w),
        conv_b=(conv_b, m_conv_b, v_conv_b), dt_bias=(dt_bias, m_dt_bias, v_dt_bias), A_log=(A_log, m_A_log, v_A_log),
        D_skip=(D_skip, m_D_skip, v_D_skip), sinks=(sinks, m_sinks, v_sinks),
        attn_out_norm=(attn_out_norm, m_attn_out_norm, v_attn_out_norm),
        ssm_out_norm=(ssm_out_norm, m_ssm_out_norm, v_ssm_out_norm), norm2=(norm2, m_norm2, v_norm2),
        rel_bias=(rel_bias, m_rel_bias, v_rel_bias), final_norm=(final_norm, m_final_norm, v_final_norm))
    small_params = {k: tuple(two_d(a) for a in v) for k, v in small_params.items()}
    S = x.shape[1]
    xs, tgt = x.reshape(S, D_MODEL), loss_target.reshape(S, D_MODEL)
    ada_w2 = ada_w[0]
    chunk = ada_w2.shape[1]
    t_in = [jnp.transpose(a[0]) for a in (w_in, m_w_in, v_w_in)]
    t_gu = [jnp.transpose(a[0]) for a in (w_gate_up, m_w_gate_up, v_w_gate_up)]

    mod, (g_in, g_cw) = _mod_and_gather(c, ada_w2, ada_b.reshape(N_DEV, chunk), [t_in[0].astype(WIRE_DTYPE), conv_w[0]])
    mod = mod.reshape(6, D_MODEL)
    w_in_full = jnp.pad(g_in.reshape(IN_W, D_MODEL), ((0, IN_PAD - IN_W), (0, 0)))
    conv_w_full = jnp.transpose(g_cw, (1, 0, 2)).reshape(4, XBC_W)

    p = {k: v[0] for k, v in small_params.items()}
    gx, (r_in_a, gw_in_b), (r_o, r_gu, r_d), gathered = _local_step(
        xs, tgt, c, mod, w_in_full, conv_w_full, w_o[0].astype(WIRE_DTYPE), t_gu[0].astype(WIRE_DTYPE),
        w_down[0].astype(WIRE_DTYPE), p)

    (u_gu, u_d, u_o), (r_in_b,) = _reduce_adamw_hosting(
        [r_gu, r_d, r_o], [tuple(t_gu), (w_down[0], m_w_down[0], v_w_down[0]), (w_o[0], m_w_o[0], v_w_o[0])],
        "adamw_big", ([gw_in_b], True))
    r_in = jnp.concatenate([r_in_a, r_in_b], axis=2)

    small, loss, c_all, dmod_all = _small_update(gathered, small_params)

    big = {
        "ada_w": _ada_w_update(c_all, dmod_all, ada_w2, m_ada_w[0], v_ada_w[0]),
        "w_in": [jnp.transpose(a) for a in _reduce_adamw(r_in, *t_in, "adamw_w_in")],
        "w_o": u_o,
        "w_gate_up": [jnp.transpose(a) for a in u_gu],
        "w_down": u_d,
    }
    big.update(small)

    order = ['ada_w', 'ada_b', 'norm1', 'w_in', 'conv_w', 'conv_b', 'dt_bias', 'A_log', 'D_skip', 'sinks',
             'attn_out_norm', 'ssm_out_norm', 'w_o', 'norm2', 'w_gate_up', 'w_down', 'rel_bias', 'final_norm']
    shapes = dict(ada_w=ada_w.shape, ada_b=ada_b.shape, norm1=norm1.shape, w_in=w_in.shape, conv_w=conv_w.shape,
                  conv_b=conv_b.shape, dt_bias=dt_bias.shape, A_log=A_log.shape, D_skip=D_skip.shape,
                  sinks=sinks.shape, attn_out_norm=attn_out_norm.shape, ssm_out_norm=ssm_out_norm.shape,
                  w_o=w_o.shape, norm2=norm2.shape, w_gate_up=w_gate_up.shape, w_down=w_down.shape,
                  rel_bias=rel_bias.shape, final_norm=final_norm.shape)
    outs = [[], [], [], []]
    for name in order:
        for kind in range(4):
            outs[kind].append(big[name][kind].reshape(shapes[name]))
    return (loss.reshape(()), gx.reshape(x.shape), *outs[0], *outs[1], *outs[2], *outs[3])
```

```python
import functools

import numpy as np
import jax
import jax.numpy as jnp
from jax import lax
from jax.experimental import pallas as pl
from jax.experimental.pallas import tpu as pltpu

F32 = jnp.float32
MXU_DTYPE = jnp.bfloat16
WIRE_DTYPE = jnp.bfloat16
HI = lax.Precision.HIGHEST
MESH = pl.DeviceIdType.MESH
N_DEV = 8

D_MODEL = 1024
ATTN_W = 512
KV_W = 128
SSM_W = 512
XBC_W = 1024
N_HEADS = 8
D_STATE = 128
D_FF = 2816
IN_W = 2312
IN_PAD = 2432
BLK = 128
N_BUCKETS = 32
EPS = 1e-6
LANE = 128
HALF = 64

ADAM_LR, ADAM_B1, ADAM_B2, ADAM_EPS, ADAM_WD, ADAM_STEP = 0.001, 0.9, 0.999, 1e-08, 0.01, 10

VMEM_BIG = 56 * 1024 * 1024
WD_CUT = 288
WGU_CUTS = (240, 496)


def _cparams(vmem=None):
    if vmem is None:
        return pltpu.CompilerParams()
    return pltpu.CompilerParams(vmem_limit_bytes=vmem)


def _mm(a, b):
    return jnp.dot(a.astype(MXU_DTYPE), b.astype(MXU_DTYPE), preferred_element_type=F32)


def _mm_nt(a, b):
    return lax.dot_general(a.astype(MXU_DTYPE), b.astype(MXU_DTYPE), (((1,), (1,)), ((), ())),
                           preferred_element_type=F32)


def _mm_tn(a, b):
    return lax.dot_general(a.astype(MXU_DTYPE), b.astype(MXU_DTYPE), (((0,), (0,)), ((), ())),
                           preferred_element_type=F32)


def _mm_hi(a, b):
    return jnp.dot(a, b, precision=HI, preferred_element_type=F32)


def _silu(x):
    return x * jax.nn.sigmoid(x)


def _softplus(x):
    return jnp.maximum(x, 0.0) + jnp.log1p(jnp.exp(-jnp.abs(x)))


def _rms(x, g, n):
    return x * lax.rsqrt(jnp.sum(x * x, axis=-1, keepdims=True) * (1.0 / n) + EPS) * g


def _modnorm(x, g, scale, shift):
    return _rms(x, g, x.shape[-1]) * (1.0 + scale) + shift


def _modnorm_parts(x):
    r = lax.rsqrt(jnp.sum(x * x, axis=-1, keepdims=True) * (1.0 / x.shape[-1]) + EPS)
    return r, x * r


def _modnorm_bwd(r, xhat, g, scale, dy):
    dyg = dy * (g * (1.0 + scale))
    c = jnp.sum(dyg * xhat, axis=-1, keepdims=True) * (1.0 / xhat.shape[-1])
    dx = r * (dyg - xhat * c)
    ct = jnp.sum(dy * xhat, axis=0, keepdims=True)
    return dx, ct * (1.0 + scale), ct * g, jnp.sum(dy, axis=0, keepdims=True)


def _lane_iota(shape):
    return lax.broadcasted_iota(jnp.int32, shape, len(shape) - 1)


def _split_pair(t):
    lane = _lane_iota(t.shape)
    lo = jnp.where(lane < HALF, t, 0.0)
    hi = pltpu.roll(jnp.where(lane >= HALF, t, 0.0), HALF, 1)
    return lo, hi


def _join_pair(lo, hi):
    lane = _lane_iota(lo.shape)
    return jnp.where(lane < HALF, lo, pltpu.roll(hi, HALF, 1))


def _split_heads(t, n_pairs):
    out = []
    for p in range(n_pairs):
        out.extend(_split_pair(t[:, p * LANE:(p + 1) * LANE]))
    return out


def _join_heads(hs):
    return jnp.concatenate([_join_pair(hs[2 * p], hs[2 * p + 1]) for p in range(len(hs) // 2)], axis=1)


def _t5_bucket_table():
    dist = np.arange(BLK)[:, None] + BLK - np.arange(2 * BLK)[None, :]
    n = np.maximum(dist, 0)
    max_exact = N_BUCKETS // 2
    large = max_exact + (np.log(np.maximum(n, 1) / max_exact) / np.log(128 / max_exact)
                         * (N_BUCKETS - max_exact)).astype(np.int32)
    large = np.minimum(large, N_BUCKETS - 1)
    return np.where(n < max_exact, n, large).astype(np.int32)


def _my_pos():
    return lax.axis_index("x"), lax.axis_index("y"), lax.axis_index("c")


def _peer(k):
    x, y, c = _my_pos()
    return (1 - x if k & 4 else x, 1 - y if k & 2 else y, 1 - c if k & 1 else c)


def _lin(pos):
    return 4 * pos[0] + 2 * pos[1] + pos[2]


def _xchg_copies(ins, outs, sems, scatter):
    local_sem, send_sem, recv_sem = sems
    me = _lin(_my_pos())
    local, remote = [], []
    for a in range(len(ins)):
        src = ins[a].at[me] if scatter else ins[a]
        local.append(pltpu.make_async_copy(src, outs[a].at[me], local_sem.at[a]))
    for k in range(1, N_DEV):
        peer = _peer(k)
        for a in range(len(ins)):
            src = ins[a].at[_lin(peer)] if scatter else ins[a]
            remote.append(pltpu.make_async_remote_copy(src, outs[a].at[me], send_sem.at[a, k - 1],
                                                       recv_sem.at[a, k - 1], device_id=peer, device_id_type=MESH))
    return local, remote


def _xchg_start(ins, outs, sems, scatter):
    local, remote = _xchg_copies(ins, outs, sems, scatter)
    for cp in local + remote:
        cp.start()


def _xchg_wait(ins, outs, sems, scatter):
    local, remote = _xchg_copies(ins, outs, sems, scatter)
    for cp in local:
        cp.wait()
    for cp in remote:
        cp.wait_send()
        cp.wait_recv()


def _xchg_shapes(arrs, scatter):
    n = len(arrs)
    if scatter:
        out_shape = [jax.ShapeDtypeStruct(a.shape, a.dtype) for a in arrs]
    else:
        out_shape = [jax.ShapeDtypeStruct((N_DEV,) + a.shape, a.dtype) for a in arrs]
    sems = [pltpu.SemaphoreType.DMA((n,)), pltpu.SemaphoreType.DMA((n, N_DEV - 1)),
            pltpu.SemaphoreType.DMA((n, N_DEV - 1))]
    return out_shape, sems


_CHIPS = (2, 4, 6)


def _g2_sems(n):
    dma = pltpu.SemaphoreType.DMA
    return [dma((n,)), dma((n, N_DEV)), dma((n, N_DEV)), dma((n, len(_CHIPS))), dma((n, len(_CHIPS)))]


class _TwoLevelGather:
    def __init__(self, ins, outs, sems):
        self.ins, self.outs = ins, outs
        self.local_sem, self.send_sem, self.recv_sem, self.fsend_sem, self.frecv_sem = sems
        self.n = len(ins)

    def _direct(self, a, k):
        return pltpu.make_async_remote_copy(self.ins[a], self.outs[a].at[_lin(_my_pos())], self.send_sem.at[a, k],
                                            self.recv_sem.at[a, k], device_id=_peer(k), device_id_type=MESH)

    def _handed_on(self, a, j, origin):
        slot = self.outs[a].at[origin]
        return pltpu.make_async_remote_copy(slot, slot, self.fsend_sem.at[a, j], self.frecv_sem.at[a, j],
                                            device_id=_peer(1), device_id_type=MESH)

    def _local(self, a):
        return pltpu.make_async_copy(self.ins[a], self.outs[a].at[_lin(_my_pos())], self.local_sem.at[a])

    def start(self):
        for a in range(self.n):
            self._local(a).start()
        for k in (1,) + _CHIPS:
            for a in range(self.n):
                self._direct(a, k).start()

    def forward(self):
        for j, k in enumerate(_CHIPS):
            for a in range(self.n):
                self._direct(a, k).wait_recv()
                self._handed_on(a, j, _lin(_peer(k))).start()

    def finish(self):
        for a in range(self.n):
            self._direct(a, 1).wait_recv()
            for j, k in enumerate(_CHIPS):
                self._handed_on(a, j, _lin(_peer(k ^ 1))).wait_recv()
            self._local(a).wait()
            for k in (1,) + _CHIPS:
                self._direct(a, k).wait_send()
            for j, k in enumerate(_CHIPS):
                self._handed_on(a, j, _lin(_peer(k))).wait_send()


def _mod_and_gather(c, ada_w, ada_b8, arrs):
    n = len(arrs)
    chunk = ada_w.shape[1]
    out_shape = [jax.ShapeDtypeStruct((N_DEV, 1, chunk), F32)]
    out_shape += [jax.ShapeDtypeStruct((N_DEV,) + a.shape, a.dtype) for a in arrs]

    def modulation(c_ref, w_ref, b_ref, out_ref, cbuf, part, s1, r1, s2, r2):
        me = _lin(_my_pos())
        first = []
        for k in range(1, N_DEV):
            cp = pltpu.make_async_remote_copy(c_ref, cbuf.at[me], s1.at[k - 1], r1.at[k - 1],
                                              device_id=_peer(k), device_id_type=MESH)
            cp.start()
            first.append(cp)
        cbuf[me] = c_ref[...]
        for cp in first:
            cp.wait_send()
            cp.wait_recv()
        cond = _silu(jnp.concatenate([cbuf[i] for i in range(N_DEV)], axis=0))
        mod = _mm_hi(cond, w_ref[...]) + b_ref[pl.ds(me, 1), :]
        for j in range(N_DEV):
            part[j] = mod[j:j + 1, :]
        second = []
        for k in range(1, N_DEV):
            peer = _peer(k)
            cp = pltpu.make_async_remote_copy(part.at[_lin(peer)], out_ref.at[me], s2.at[k - 1], r2.at[k - 1],
                                              device_id=peer, device_id_type=MESH)
            cp.start()
            second.append(cp)
        out_ref[me] = part[me]
        for cp in second:
            cp.wait_send()
            cp.wait_recv()

    def body(*refs):
        c_ref, w_ref, b_ref = refs[:3]
        ins = refs[3:3 + n]
        mod_ref = refs[3 + n]
        outs = refs[4 + n:4 + 2 * n]
        cbuf, part, s1, r1, s2, r2 = refs[4 + 2 * n:10 + 2 * n]
        gather = _TwoLevelGather(ins, outs, refs[10 + 2 * n:])
        gather.start()
        modulation(c_ref, w_ref, b_ref, mod_ref, cbuf, part, s1, r1, s2, r2)
        gather.forward()
        gather.finish()

    hbm = pl.BlockSpec(memory_space=pltpu.HBM)
    vm = pl.BlockSpec(memory_space=pltpu.VMEM)
    dma = pltpu.SemaphoreType.DMA
    res = pl.pallas_call(
        body, name="mod_and_gather", out_shape=out_shape, in_specs=[vm, vm, vm] + [hbm] * n,
        out_specs=[vm] + [hbm] * n,
        scratch_shapes=[pltpu.VMEM((N_DEV, 1, D_MODEL), F32), pltpu.VMEM((N_DEV, 1, chunk), F32)]
        + [dma((N_DEV - 1,))] * 4 + _g2_sems(n),
    )(c, ada_w, ada_b8, *arrs)
    return res[0], res[1:]


def _hosted_call(body, name, grid, in_specs, out_specs, out_shape, scratch_shapes, args, xchg, cparams):
    arrs, scatter = xchg
    grid = (grid,) if isinstance(grid, int) else tuple(grid)
    n, n_in, n_out, n_scr = len(arrs), len(in_specs), len(out_specs), len(scratch_shapes)
    two_level = scatter == "two-level"
    x_shape, x_sems = _xchg_shapes(arrs, False if two_level else scatter)
    if two_level:
        x_sems = _g2_sems(n)
    n_steps = int(np.prod(grid))

    def hosted(*refs):
        ins, refs = refs[:n_in], refs[n_in:]
        x_in, refs = refs[:n], refs[n:]
        outs, refs = refs[:n_out], refs[n_out:]
        x_out, refs = refs[:n], refs[n:]
        scr, sems = refs[:n_scr], refs[n_scr:]
        step = pl.program_id(0)
        for d in range(1, len(grid)):
            step = step * grid[d] + pl.program_id(d)

        @pl.when(step == 0)
        def _():
            if two_level:
                _TwoLevelGather(x_in, x_out, sems).start()
            else:
                _xchg_start(x_in, x_out, sems, scatter)

        if two_level:
            @pl.when(step == (2 * n_steps) // 3)
            def _():
                _TwoLevelGather(x_in, x_out, sems).forward()

        body(*ins, *outs, *scr)

        @pl.when(step == n_steps - 1)
        def _():
            if two_level:
                _TwoLevelGather(x_in, x_out, sems).finish()
            else:
                _xchg_wait(x_in, x_out, sems, scatter)

    hbm = pl.BlockSpec(memory_space=pltpu.HBM)
    res = pl.pallas_call(
        hosted, name=name, grid=grid, in_specs=list(in_specs) + [hbm] * n,
        out_specs=list(out_specs) + [hbm] * n, out_shape=list(out_shape) + x_shape,
        scratch_shapes=list(scratch_shapes) + x_sems, compiler_params=cparams,
    )(*args, *arrs)
    return res[:n_out], res[n_out:]


def _row(i):
    return (i, 0)


def _fixed(i):
    return (0, 0)


def _in_proj_fwd(x, norm1, scale1, shift1, w_in, tm, xchg):
    S = x.shape[0]

    def body(x_ref, n_ref, sc_ref, sh_ref, w_ref, qkv_ref, z_ref, xbc_ref, dt_ref):
        h = _modnorm(x_ref[...], n_ref[...], sc_ref[...], sh_ref[...])
        p = _mm_nt(h, w_ref[...])
        qkv_ref[...] = p[:, :768].astype(qkv_ref.dtype)
        z_ref[...] = p[:, 768:1280]
        xbc_ref[...] = p[:, 1280:2304]
        dt_ref[...] = p[:, 2304:IN_PAD]

    vec = pl.BlockSpec((1, D_MODEL), _fixed)
    return _hosted_call(
        body, "in_proj_fwd", S // tm,
        in_specs=[pl.BlockSpec((tm, D_MODEL), _row), vec, vec, vec, pl.BlockSpec((IN_PAD, D_MODEL), _fixed)],
        out_specs=[pl.BlockSpec((tm, 768), _row), pl.BlockSpec((tm, SSM_W), _row),
                   pl.BlockSpec((tm, XBC_W), _row), pl.BlockSpec((tm, LANE), _row)],
        out_shape=[jax.ShapeDtypeStruct((S, 768), MXU_DTYPE), jax.ShapeDtypeStruct((S, SSM_W), F32),
                   jax.ShapeDtypeStruct((S, XBC_W), F32), jax.ShapeDtypeStruct((S, LANE), F32)],
        scratch_shapes=[], args=(x, norm1, scale1, shift1, w_in), xchg=xchg, cparams=_cparams(VMEM_BIG),
    )


def _in_proj_bwd(x, dx1, dq, dkv, dz, dxbc, ddt, norm1, scale1, shift1, w_in, tm):
    S = x.shape[0]

    def body(x_ref, dx1_ref, dq_ref, dkv_ref, dz_ref, dxbc_ref, ddt_ref, n_ref, sc_ref, sh_ref, w_ref,
             gx_ref, h_ref, acc_ref):
        @pl.when(pl.program_id(0) == 0)
        def _():
            acc_ref[...] = jnp.zeros_like(acc_ref)

        dp = jnp.concatenate([dq_ref[...], dkv_ref[...], dz_ref[...], dxbc_ref[...], ddt_ref[...]], axis=1)
        dh = _mm(dp, w_ref[...])
        r, xhat = _modnorm_parts(x_ref[...])
        dx, dn, dsc, dsh = _modnorm_bwd(r, xhat, n_ref[...], sc_ref[...], dh)
        gx_ref[...] = dx1_ref[...] + dx
        h_ref[...] = (xhat * n_ref[...] * (1.0 + sc_ref[...]) + sh_ref[...]).astype(h_ref.dtype)
        acc_ref[0:1, :] += dn
        acc_ref[1:2, :] += dsc
        acc_ref[2:3, :] += dsh

    vec = pl.BlockSpec((1, D_MODEL), _fixed)
    return pl.pallas_call(
        body, name="in_proj_bwd", grid=(S // tm,),
        in_specs=[pl.BlockSpec((tm, D_MODEL), _row), pl.BlockSpec((tm, D_MODEL), _row),
                  pl.BlockSpec((tm, ATTN_W), _row), pl.BlockSpec((tm, 2 * KV_W), _row),
                  pl.BlockSpec((tm, SSM_W), _row), pl.BlockSpec((tm, XBC_W), _row), pl.BlockSpec((tm, LANE), _row),
                  vec, vec, vec, pl.BlockSpec((IN_PAD, D_MODEL), _fixed)],
        out_specs=[pl.BlockSpec((tm, D_MODEL), _row), pl.BlockSpec((tm, D_MODEL), _row),
                   pl.BlockSpec((8, D_MODEL), _fixed)],
        out_shape=[jax.ShapeDtypeStruct((S, D_MODEL), F32), jax.ShapeDtypeStruct((S, D_MODEL), MXU_DTYPE),
                   jax.ShapeDtypeStruct((8, D_MODEL), F32)],
        compiler_params=_cparams(VMEM_BIG),
    )(x, dx1, dq, dkv, dz, dxbc, ddt, norm1, scale1, shift1, w_in)


def _out_stage(ya, ys0, ys1, z0, z1, an, sn0, sn1):
    half = SSM_W // 2
    a = _rms(ya, an, ATTN_W)
    g0 = _rms(ys0 * _silu(z0), sn0, half)
    g1 = _rms(ys1 * _silu(z1), sn1, half)
    return jnp.concatenate([a, g0, g1], axis=1)


def _out_stage_args(ya_ref, ys_ref, z_ref, an_ref, sn_ref):
    half = SSM_W // 2
    return (ya_ref[...], ys_ref[:, :half], ys_ref[:, half:], z_ref[:, :half], z_ref[:, half:],
            an_ref[...], sn_ref[:, :half], sn_ref[:, half:])


def _out_proj_fwd(x, ya, ys, z, an, sn, gate1, w_o, tm, xchg):
    S = x.shape[0]

    def body(x_ref, ya_ref, ys_ref, z_ref, an_ref, sn_ref, g_ref, w_ref, x1_ref):
        u = _out_stage(*_out_stage_args(ya_ref, ys_ref, z_ref, an_ref, sn_ref))
        x1_ref[...] = x_ref[...] + g_ref[...] * _mm(u, w_ref[...])

    half = pl.BlockSpec((tm, ATTN_W), _row)
    hvec = pl.BlockSpec((1, ATTN_W), _fixed)
    (x1,), x_out = _hosted_call(
        body, "out_proj_fwd", S // tm,
        in_specs=[pl.BlockSpec((tm, D_MODEL), _row), half, half, half, hvec, hvec,
                  pl.BlockSpec((1, D_MODEL), _fixed), pl.BlockSpec((D_MODEL, D_MODEL), _fixed)],
        out_specs=[pl.BlockSpec((tm, D_MODEL), _row)],
        out_shape=[jax.ShapeDtypeStruct((S, D_MODEL), F32)],
        scratch_shapes=[], args=(x, ya, ys, z, an, sn, gate1, w_o), xchg=xchg, cparams=_cparams(VMEM_BIG),
    )
    return x1, x_out


def _out_proj_bwd(dx1, ya, ys, z, an, sn, gate1, w_o, tm, xchg):
    S = dx1.shape[0]
    n_steps = S // tm

    def body(dx1_ref, ya_ref, ys_ref, z_ref, an_ref, sn_ref, g_ref, w_ref,
             dya_ref, dys_ref, dz_ref, gw_ref, acc_ref, gw_acc):
        i = pl.program_id(0)

        @pl.when(i == 0)
        def _():
            acc_ref[...] = jnp.zeros_like(acc_ref)
            gw_acc[...] = jnp.zeros_like(gw_acc)

        u, vjp = jax.vjp(_out_stage, *_out_stage_args(ya_ref, ys_ref, z_ref, an_ref, sn_ref))
        dx1 = dx1_ref[...]
        ub = u.astype(MXU_DTYPE)
        mix = _mm(ub, w_ref[...])
        dmix = dx1 * g_ref[...]
        dmixb = dmix.astype(MXU_DTYPE)
        du = _mm_nt(dmixb, w_ref[...])
        gw_acc[...] += _mm_tn(ub, dmixb)
        dya, dys0, dys1, dz0, dz1, dan, dsn0, dsn1 = vjp(du)
        dya_ref[...] = dya
        dys_ref[...] = jnp.concatenate([dys0, dys1], axis=1)
        dz_ref[...] = jnp.concatenate([dz0, dz1], axis=1).astype(dz_ref.dtype)
        acc_ref[0:1, :] += jnp.sum(dx1 * mix, axis=0, keepdims=True)
        acc_ref[1:2, :] += jnp.concatenate([dan, dsn0, dsn1], axis=1)

        @pl.when(i == n_steps - 1)
        def _():
            gw_ref[...] = gw_acc[...].astype(gw_ref.dtype)

    half = pl.BlockSpec((tm, ATTN_W), _row)
    hvec = pl.BlockSpec((1, ATTN_W), _fixed)
    full = pl.BlockSpec((tm, D_MODEL), _row)
    return _hosted_call(
        body, "out_proj_bwd", n_steps,
        in_specs=[full, half, half, half, hvec, hvec,
                  pl.BlockSpec((1, D_MODEL), _fixed), pl.BlockSpec((D_MODEL, D_MODEL), _fixed)],
        out_specs=[half, half, half, pl.BlockSpec((D_MODEL, D_MODEL), _fixed), pl.BlockSpec((8, D_MODEL), _fixed)],
        out_shape=[jax.ShapeDtypeStruct((S, ATTN_W), F32)] * 2 + [jax.ShapeDtypeStruct((S, ATTN_W), MXU_DTYPE),
                   jax.ShapeDtypeStruct((D_MODEL, D_MODEL), WIRE_DTYPE), jax.ShapeDtypeStruct((8, D_MODEL), F32)],
        scratch_shapes=[pltpu.VMEM((D_MODEL, D_MODEL), F32)],
        args=(dx1, ya, ys, z, an, sn, gate1, w_o), xchg=xchg, cparams=_cparams(VMEM_BIG),
    )


def _loss_rows(x2, fn, tgt):
    y = _rms(x2, fn, D_MODEL)
    per_row = jnp.sum(jnp.square(y - tgt), axis=1, keepdims=True)
    return jnp.sum(per_row, axis=0, keepdims=True) * (0.5 / D_MODEL)


def _mlp_loss(x1, tgt, norm2, scale2, shift2, gate2, fnorm, w_gu, w_d, tm):
    S = x1.shape[0]
    n_pieces = len(w_gu) + len(w_d)

    def body(*refs):
        x1_ref, t_ref, n_ref, sc_ref, sh_ref, g_ref, fn_ref = refs[:7]
        piece_refs = refs[7:7 + n_pieces]
        dx1_ref, h_ref, dgu_ref, act_ref, dmlp_ref, acc_ref, wgu, wd, wsem = refs[7 + n_pieces:]

        @pl.when(pl.program_id(0) == 0)
        def _():
            acc_ref[...] = jnp.zeros_like(acc_ref)
            copies = []
            for dst, pieces in ((wgu, piece_refs[:len(w_gu)]), (wd, piece_refs[len(w_gu):])):
                shard = sum(p.shape[1] for p in pieces)
                off = 0
                for p in pieces:
                    for j in range(N_DEV):
                        copies.append(pltpu.make_async_copy(p.at[j], dst.at[pl.ds(j * shard + off, p.shape[1])],
                                                            wsem.at[len(copies)]))
                    off += p.shape[1]
            for cp in copies:
                cp.start()
            for cp in copies:
                cp.wait()

        x1 = x1_ref[...]
        gate2 = g_ref[...]
        h, vjp_h = jax.vjp(_modnorm, x1, n_ref[...], sc_ref[...], sh_ref[...])
        hb = h.astype(MXU_DTYPE)
        gu = _mm_nt(hb, wgu[...])
        g, u = gu[:, :D_FF], gu[:, D_FF:]
        sg = jax.nn.sigmoid(g)
        silu_g = g * sg
        act = (silu_g * u).astype(MXU_DTYPE)
        mlp = _mm(act, wd[...])
        x2 = x1 + gate2 * mlp
        loss, vjp_loss = jax.vjp(_loss_rows, x2, fn_ref[...], t_ref[...])
        dx2, dfn, _ = vjp_loss(jnp.ones((1, 1), F32))
        dmlp = (dx2 * gate2).astype(MXU_DTYPE)
        dact = _mm_nt(dmlp, wd[...])
        dg = dact * u * (sg * (1.0 + g * (1.0 - sg)))
        du = dact * silu_g
        dgu = jnp.concatenate([dg, du], axis=1).astype(MXU_DTYPE)
        dh = _mm(dgu, wgu[...])
        dx, dn, dsc, dsh = vjp_h(dh)
        dx1_ref[...] = dx2 + dx
        h_ref[...] = hb
        dgu_ref[...] = dgu
        act_ref[...] = act
        dmlp_ref[...] = dmlp
        acc_ref[0:1, :] += dn
        acc_ref[1:2, :] += dsc
        acc_ref[2:3, :] += dsh
        acc_ref[3:4, :] += jnp.sum(dx2 * mlp, axis=0, keepdims=True)
        acc_ref[4:5, :] += dfn
        acc_ref[5:6, :] += jnp.broadcast_to(loss, (1, D_MODEL))

    full = pl.BlockSpec((tm, D_MODEL), _row)
    vec = pl.BlockSpec((1, D_MODEL), _fixed)
    anyspec = pl.BlockSpec(memory_space=pl.ANY)
    return pl.pallas_call(
        body, name="mlp_loss", grid=(S // tm,),
        in_specs=[full, full, vec, vec, vec, vec, vec] + [anyspec] * n_pieces,
        out_specs=[full, full, pl.BlockSpec((tm, 2 * D_FF), _row), pl.BlockSpec((tm, D_FF), _row), full,
                   pl.BlockSpec((8, D_MODEL), _fixed)],
        out_shape=[jax.ShapeDtypeStruct((S, D_MODEL), F32), jax.ShapeDtypeStruct((S, D_MODEL), MXU_DTYPE),
                   jax.ShapeDtypeStruct((S, 2 * D_FF), MXU_DTYPE), jax.ShapeDtypeStruct((S, D_FF), MXU_DTYPE),
                   jax.ShapeDtypeStruct((S, D_MODEL), MXU_DTYPE), jax.ShapeDtypeStruct((8, D_MODEL), F32)],
        scratch_shapes=[pltpu.VMEM((2 * D_FF, D_MODEL), MXU_DTYPE), pltpu.VMEM((D_FF, D_MODEL), MXU_DTYPE),
                        pltpu.SemaphoreType.DMA((N_DEV * n_pieces,))],
        compiler_params=_cparams(VMEM_BIG),
    )(x1, tgt, norm2, scale2, shift2, gate2, fnorm, *w_gu, *w_d)


def _wgrad(a, g, tk, ts, name, xchg=None, g_cols=None):
    pieces = list(a) if isinstance(a, (list, tuple)) else [a]
    S = pieces[0].shape[0]
    K = sum(p.shape[1] for p in pieces)
    assert len(pieces) == 1 or tk == K
    N, col = (g.shape[1], 0) if g_cols is None else g_cols
    ns = S // ts
    n_a = len(pieces)

    def body(*refs):
        a_refs, (g_ref, o_ref, acc_ref) = refs[:n_a], refs[n_a:]
        s = pl.program_id(1)

        @pl.when(s == 0)
        def _():
            acc_ref[...] = jnp.zeros_like(acc_ref)

        a_blk = a_refs[0][...] if n_a == 1 else jnp.concatenate([r[...] for r in a_refs], axis=1)
        acc_ref[...] += _mm_tn(a_blk, g_ref[...])

        @pl.when(s == ns - 1)
        def _():
            o_ref[...] = acc_ref[...].astype(o_ref.dtype)

    if n_a == 1:
        in_specs = [pl.BlockSpec((ts, tk), lambda j, s: (s, j))]
    else:
        in_specs = [pl.BlockSpec((ts, p.shape[1]), lambda j, s: (s, 0)) for p in pieces]
    in_specs.append(pl.BlockSpec((ts, N), lambda j, s: (s, col)))
    out_spec = pl.BlockSpec((tk, N), lambda j, s: (j, 0))
    out_shape = jax.ShapeDtypeStruct((K, N), WIRE_DTYPE)
    scratch = [pltpu.VMEM((tk, N), F32)]
    args = (*pieces, g)
    if xchg is None:
        return pl.pallas_call(body, name=name, grid=(K // tk, ns), in_specs=in_specs, out_specs=out_spec,
                              out_shape=out_shape, scratch_shapes=scratch, compiler_params=_cparams(VMEM_BIG))(*args)
    (out,), x_out = _hosted_call(body, name, (K // tk, ns), in_specs, [out_spec], [out_shape], scratch, args, xchg,
                                 _cparams(VMEM_BIG))
    return out, x_out


SSD_CHUNKS_PER_STEP = 4
SSD_BWD_CHUNKS_PER_STEP = 4
ATTN_BLOCKS_PER_STEP = 4
MASKED = -1e30
QK_SCALE = HALF ** -0.5


def _attn_bias(buckets, rel_bias):
    def body(bk_ref, relb_ref, out_ref):
        bk = bk_ref[...]
        i = lax.broadcasted_iota(jnp.int32, (BLK, 2 * BLK), 0)
        j = lax.broadcasted_iota(jnp.int32, (BLK, 2 * BLK), 1)
        window = (j > i) & (j <= i + BLK)
        for h in range(N_HEADS):
            acc = jnp.zeros((BLK, 2 * BLK), F32)
            for b in range(N_BUCKETS):
                acc = jnp.where(bk == b, relb_ref[b, h], acc)
            out_ref[0, h] = jnp.where(window, acc, MASKED)
            out_ref[1, h] = jnp.where(window & (j >= BLK), acc, MASKED)

    return pl.pallas_call(
        body, name="attn_bias", out_shape=jax.ShapeDtypeStruct((2, N_HEADS, BLK, 2 * BLK), F32),
        in_specs=[pl.BlockSpec(memory_space=pltpu.VMEM), pl.BlockSpec(memory_space=pltpu.SMEM)],
    )(buckets, rel_bias)


def _attn_fwd(qkv, bias, sinks, xchg):
    S = qkv.shape[0]
    nb = S // BLK

    nq = ATTN_BLOCKS_PER_STEP if nb % ATTN_BLOCKS_PER_STEP == 0 else 1
    rows = nq * BLK

    def body(q_ref, kvp_ref, kvc_ref, bias_ref, sinks_ref, y_ref):
        i = pl.program_id(0)
        q = q_ref[...].astype(F32) * QK_SCALE
        kv = jnp.concatenate([kvp_ref[...], kvc_ref[...]], axis=0).astype(F32)
        k_lo, k_hi = _split_pair(kv[:, :LANE])
        v_lo, v_hi = _split_pair(kv[:, LANE:])
        bands = [[t[b * BLK:(b + 2) * BLK].astype(MXU_DTYPE) for t in (k_lo, k_hi, v_lo, v_hi)] for b in range(nq)]
        q_heads = [_split_heads(q[b * BLK:(b + 1) * BLK], 4) for b in range(nq)]
        first = [jnp.where(i == 0, 1, 0) if b == 0 else 0 for b in range(nq)]
        items = [(b, h) for b in range(nq) for h in range(N_HEADS)]
        s = [_mm_nt(q_heads[b][h].astype(MXU_DTYPE), bands[b][h // 4]) + bias_ref[first[b], h] for b, h in items]
        m = [jnp.maximum(jnp.max(s[n], axis=-1, keepdims=True), sinks_ref[h]) for n, (b, h) in enumerate(items)]
        p = [jnp.exp(s[n] - m[n]) for n in range(len(items))]
        rinv = [1.0 / (jnp.sum(p[n], axis=-1, keepdims=True) + jnp.exp(sinks_ref[h] - m[n]))
                for n, (b, h) in enumerate(items)]
        out = [_mm(p[n], bands[b][2 + h // 4]) * rinv[n] for n, (b, h) in enumerate(items)]
        y_ref[...] = jnp.concatenate([_join_heads(out[b * N_HEADS:(b + 1) * N_HEADS]) for b in range(nq)], axis=0)

    smem = pl.BlockSpec(memory_space=pltpu.SMEM)
    return _hosted_call(
        body, "attn_fwd", nb // nq,
        in_specs=[pl.BlockSpec((rows, ATTN_W), _row),
                  pl.BlockSpec((BLK, 2 * KV_W), lambda i: (jnp.maximum(i * nq - 1, 0), 2)),
                  pl.BlockSpec((rows, 2 * KV_W), lambda i: (i, 2)),
                  pl.BlockSpec((2, N_HEADS, BLK, 2 * BLK), lambda i: (0, 0, 0, 0)), smem],
        out_specs=[pl.BlockSpec((rows, ATTN_W), _row)],
        out_shape=[jax.ShapeDtypeStruct((S, ATTN_W), F32)],
        scratch_shapes=[],
        args=(qkv, qkv, qkv, bias, sinks), xchg=xchg, cparams=_cparams(),
    )


def _attn_bwd(qkv, y, dy, bias, sinks, xchg):
    S = qkv.shape[0]
    nb = S // BLK
    nq = ATTN_BLOCKS_PER_STEP if nb % ATTN_BLOCKS_PER_STEP == 0 else 1
    rows, n_steps = nq * BLK, nb // nq

    def body(q_ref, kvp_ref, kvc_ref, y_ref, dy_ref, bias_ref, sinks_ref, dq_ref, dkv_ref, dbias_ref, dsk_ref, carry_ref):
        i = pl.program_id(0)

        @pl.when(i == 0)
        def _():
            dbias_ref[...] = jnp.zeros_like(dbias_ref)
            dsk_ref[...] = jnp.zeros_like(dsk_ref)
            carry_ref[...] = jnp.zeros_like(carry_ref)

        q = q_ref[...].astype(F32) * QK_SCALE
        kv = jnp.concatenate([kvp_ref[...], kvc_ref[...]], axis=0).astype(F32)
        k_lo, k_hi = _split_pair(kv[:, :LANE])
        v_lo, v_hi = _split_pair(kv[:, LANE:])
        bands = [[t[b * BLK:(b + 2) * BLK].astype(MXU_DTYPE) for t in (k_lo, k_hi, v_lo, v_hi)] for b in range(nq)]
        rows_of = lambda ref, b: ref[b * BLK:(b + 1) * BLK, :]
        first = [jnp.where(i == n_steps - 1, 1, 0) if b == 0 else 0 for b in range(nq)]
        items = [(b, h) for b in range(nq) for h in range(N_HEADS)]
        at = lambda b, h: b * N_HEADS + h
        q_heads = [hd for b in range(nq) for hd in _split_heads(q[b * BLK:(b + 1) * BLK], 4)]
        y_heads = [hd for b in range(nq) for hd in _split_heads(rows_of(y_ref, b), 4)]
        dy_heads = [hd for b in range(nq) for hd in _split_heads(rows_of(dy_ref, b), 4)]
        qs = [q_heads[n].astype(MXU_DTYPE) for n in range(len(items))]
        s = [_mm_nt(qs[at(b, h)], bands[b][h // 4]) + bias_ref[first[b], h] for b, h in items]
        m = [jnp.maximum(jnp.max(s[at(b, h)], axis=-1, keepdims=True), sinks_ref[h]) for b, h in items]
        p = [jnp.exp(s[n] - m[n]) for n in range(len(items))]
        esink = [jnp.exp(sinks_ref[h] - m[at(b, h)]) for b, h in items]
        rinv = [1.0 / (jnp.sum(p[n], axis=-1, keepdims=True) + esink[n]) for n in range(len(items))]
        t = [dy_heads[n] * rinv[n] for n in range(len(items))]
        delta = [jnp.sum(t[n] * y_heads[n], axis=-1, keepdims=True) for n in range(len(items))]
        tb = [t[n].astype(MXU_DTYPE) for n in range(len(items))]
        dp = [_mm_nt(tb[at(b, h)], bands[b][2 + h // 4]) for b, h in items]
        ds = [p[n] * (dp[n] - delta[n]) for n in range(len(items))]
        for h in range(N_HEADS):
            ds_h, dsk_h = ds[at(0, h)], esink[at(0, h)] * delta[at(0, h)]
            for b in range(1, nq):
                ds_h = ds_h + ds[at(b, h)]
                dsk_h = dsk_h + esink[at(b, h)] * delta[at(b, h)]
            dbias_ref[h] += ds_h
            dsk_ref[h] -= dsk_h
        dsb = [ds[n].astype(MXU_DTYPE) for n in range(len(items))]
        pb = [p[n].astype(MXU_DTYPE) for n in range(len(items))]
        dq_heads = [_mm(dsb[at(b, h)], bands[b][h // 4]) * QK_SCALE for b, h in items]
        grp = lambda lst, b, g: jnp.concatenate(lst[at(b, 4 * g):at(b, 4 * g) + 4], axis=0)
        dk_pads = [[_mm_tn(grp(dsb, b, g), grp(qs, b, g)) for g in range(2)] for b in range(nq)]
        dv_pads = [[_mm_tn(grp(pb, b, g), grp(tb, b, g)) for g in range(2)] for b in range(nq)]
        dq_ref[...] = jnp.concatenate([_join_heads(dq_heads[b * N_HEADS:(b + 1) * N_HEADS]) for b in range(nq)],
                                      axis=0).astype(dq_ref.dtype)
        part = lambda b, lo: jnp.concatenate(
            [_join_pair(d[b][0][lo:lo + BLK], d[b][1][lo:lo + BLK]) for d in (dk_pads, dv_pads)], axis=1)
        dkv = [part(b, BLK) + (part(b + 1, 0) if b + 1 < nq else carry_ref[...]) for b in range(nq)]
        dkv_ref[...] = jnp.concatenate(dkv, axis=0).astype(dkv_ref.dtype)
        carry_ref[...] = part(0, 0)

    smem = pl.BlockSpec(memory_space=pltpu.SMEM)
    rev = lambda i: (n_steps - 1 - i, 0)
    return _hosted_call(
        body, "attn_bwd", n_steps,
        in_specs=[pl.BlockSpec((rows, ATTN_W), rev),
                  pl.BlockSpec((BLK, 2 * KV_W), lambda i: (jnp.maximum((n_steps - 1 - i) * nq - 1, 0), 2)),
                  pl.BlockSpec((rows, 2 * KV_W), lambda i: (n_steps - 1 - i, 2)),
                  pl.BlockSpec((rows, ATTN_W), rev), pl.BlockSpec((rows, ATTN_W), rev),
                  pl.BlockSpec((2, N_HEADS, BLK, 2 * BLK), lambda i: (0, 0, 0, 0)), smem],
        out_specs=[pl.BlockSpec((rows, ATTN_W), rev), pl.BlockSpec((rows, 2 * KV_W), rev),
                   pl.BlockSpec((N_HEADS, BLK, 2 * BLK), lambda i: (0, 0, 0)),
                   pl.BlockSpec((N_HEADS, BLK, 1), lambda i: (0, 0, 0))],
        out_shape=[jax.ShapeDtypeStruct((S, ATTN_W), MXU_DTYPE), jax.ShapeDtypeStruct((S, 2 * KV_W), MXU_DTYPE),
                   jax.ShapeDtypeStruct((N_HEADS, BLK, 2 * BLK), F32), jax.ShapeDtypeStruct((N_HEADS, BLK, 1), F32)],
        scratch_shapes=[pltpu.VMEM((BLK, 2 * KV_W), F32)],
        args=(qkv, qkv, qkv, y, dy, bias, sinks), xchg=xchg, cparams=_cparams(),
    )


def _attn_finish(dbias, dsk, buckets):
    def body(db_ref, dsk_ref, bk_ref, drel_ref, dsink_ref):
        bk = bk_ref[...]
        r = lax.broadcasted_iota(jnp.int32, (N_BUCKETS, LANE), 0)
        l = lax.broadcasted_iota(jnp.int32, (N_BUCKETS, LANE), 1)
        row = lax.broadcasted_iota(jnp.int32, (N_HEADS, LANE), 0)
        res = jnp.zeros((N_BUCKETS, LANE), F32)
        dsink = jnp.zeros((N_HEADS, LANE), F32)
        for h in range(N_HEADS):
            db = db_ref[h]
            for b in range(N_BUCKETS):
                v = jnp.sum(jnp.sum(jnp.where(bk == b, db, 0.0), axis=1, keepdims=True), axis=0, keepdims=True)
                res = res + jnp.where((r == b) & (l == h), v, 0.0)
            dsink = dsink + jnp.where(row == h, jnp.sum(dsk_ref[h], axis=0, keepdims=True), 0.0)
        drel_ref[...] = res
        dsink_ref[...] = dsink

    return pl.pallas_call(body, name="attn_finish",
                          out_shape=[jax.ShapeDtypeStruct((N_BUCKETS, LANE), F32),
                                     jax.ShapeDtypeStruct((N_HEADS, LANE), F32)])(dbias, dsk, buckets)


def _ssd_consts():
    r = lax.broadcasted_iota(jnp.int32, (BLK, BLK), 0)
    c = lax.broadcasted_iota(jnp.int32, (BLK, BLK), 1)
    causal = c <= r
    upper = (r <= c).astype(F32)
    last = r == BLK - 1
    head = lax.broadcasted_iota(jnp.int32, (N_HEADS, BLK), 0)
    return causal, upper, last, head


def _ssd_chunks(xs, bg, cg, dt_raw_t, prev0, dtb, alog, d_rows, consts):
    causal, upper, last, head = consts
    nq = len(xs)
    items = [(c, h) for c in range(nq) for h in range(N_HEADS)]
    at = lambda c, h: c * N_HEADS + h
    a_neg = -jnp.exp(alog)
    dt_t = [_softplus(dt_raw_t[c] + dtb) for c in range(nq)]
    acs_t = [_mm_hi(dt_t[c] * a_neg, upper) for c in range(nq)]
    cb = [[_mm_nt(cg[c][g], bg[c][g]) for g in range(2)] for c in range(nq)]
    pick = lambda t, h: jnp.sum(jnp.where(head == h, t, 0.0), axis=0, keepdims=True)
    dt_row = [pick(dt_t[c], h) for c, h in items]
    a_row = [pick(acs_t[c], h) for c, h in items]
    a_rb = [jnp.broadcast_to(a_row[n], (BLK, BLK)) for n in range(len(items))]
    a_b = [a_rb[n].T for n in range(len(items))]
    a_last = [jnp.sum(jnp.where(last, a_b[n], 0.0), axis=0, keepdims=True) for n in range(len(items))]
    w = [cb[c][h // 4] * jnp.exp(jnp.where(causal, a_b[at(c, h)] - a_rb[at(c, h)], -1e30)) * dt_row[at(c, h)]
         for c, h in items]
    f_b = [jnp.broadcast_to(dt_row[n] * jnp.exp(a_last[n] - a_row[n]), (BLK, BLK)).T for n in range(len(items))]
    y_in = [_mm(w[at(c, h)], xs[c][h]) for c, h in items]
    st = [_mm_tn(bg[c][h // 4], xs[c][h] * f_b[at(c, h)]) for c, h in items]
    e_b = [jnp.exp(a_b[n]) for n in range(len(items))]
    states = [list(prev0)]
    for c in range(nq):
        states.append([states[c][h] * jnp.exp(a_last[at(c, h)]) + st[at(c, h)] for h in range(N_HEADS)])
    y_off = [_mm(cg[c][h // 4], states[c][h]) * e_b[at(c, h)] for c, h in items]
    ys = [[y_in[at(c, h)] + y_off[at(c, h)] + d_rows[h] * xs[c][h] for h in range(N_HEADS)] for c in range(nq)]
    return ys, states


def _ssd_chunks_bwd(xs, bg, cg, dt_raw_t, prev, dtb, alog, d_rows, dys, dh_last, consts):
    causal, upper, last, head = consts
    nq = len(xs)
    items = [(c, h) for c in range(nq) for h in range(N_HEADS)]
    ni = len(items)
    at = lambda c, h: c * N_HEADS + h
    groups = [(c, g) for c in range(nq) for g in range(2)]
    lane = _lane_iota((BLK, BLK))
    lane_row = _lane_iota((1, BLK))
    a_neg = -jnp.exp(alog)
    pre_dt = [dt_raw_t[c] + dtb for c in range(nq)]
    dt_t = [_softplus(pre_dt[c]) for c in range(nq)]
    acs_t = [_mm_hi(dt_t[c] * a_neg, upper) for c in range(nq)]
    pick = lambda t, h: jnp.sum(jnp.where(head == h, t, 0.0), axis=0, keepdims=True)
    full_sum = lambda t: jnp.sum(jnp.sum(t, axis=1, keepdims=True), axis=0, keepdims=True)
    dt_row = [pick(dt_t[c], h) for c, h in items]
    a_row = [pick(acs_t[c], h) for c, h in items]
    a_rb = [jnp.broadcast_to(a_row[n], (BLK, BLK)) for n in range(ni)]
    a_b = [a_rb[n].T for n in range(ni)]
    a_last = [jnp.sum(jnp.where(last, a_b[n], 0.0), axis=0, keepdims=True) for n in range(ni)]
    lm = [jnp.exp(jnp.where(causal, a_b[n] - a_rb[n], -1e30)) for n in range(ni)]
    cgb = [[cg[c][g].astype(MXU_DTYPE) for g in range(2)] for c in range(nq)]
    bgb = [[bg[c][g].astype(MXU_DTYPE) for g in range(2)] for c in range(nq)]
    cb = [[_mm_nt(cgb[c][g], bgb[c][g]) for g in range(2)] for c in range(nq)]
    u = [cb[c][h // 4] * lm[at(c, h)] for c, h in items]
    w = [(u[n] * dt_row[n]).astype(MXU_DTYPE) for n in range(ni)]
    e_row = [jnp.exp(a_last[n] - a_row[n]) for n in range(ni)]
    f_row = [dt_row[n] * e_row[n] for n in range(ni)]
    f_b = [jnp.broadcast_to(f_row[n], (BLK, BLK)).T for n in range(ni)]
    e_b = [jnp.exp(a_b[n]) for n in range(ni)]
    el = [jnp.exp(a_last[n]) for n in range(ni)]
    xb = [xs[c][h].astype(MXU_DTYPE) for c, h in items]
    dyb = [dys[c][h].astype(MXU_DTYPE) for c, h in items]
    prevb = [prev[c][h].astype(MXU_DTYPE) for c, h in items]
    gmat = [_mm(cgb[c][h // 4], prevb[at(c, h)]) for c, h in items]
    dw = [_mm_nt(dyb[n], xb[n]) for n in range(ni)]
    dg = [dys[c][h] * e_b[at(c, h)] for c, h in items]
    dgb = [dg[n].astype(MXU_DTYPE) for n in range(ni)]
    from_y = [_mm_tn(cgb[c][h // 4], dgb[at(c, h)]) for c, h in items]
    dhs = [None] * ni
    dprev = [None] * ni
    for c in reversed(range(nq)):
        for h in range(N_HEADS):
            dhs[at(c, h)] = dh_last[h] if c == nq - 1 else dprev[at(c + 1, h)]
            dprev[at(c, h)] = from_y[at(c, h)] + dhs[at(c, h)] * el[at(c, h)]
    dstb = [dhs[n].astype(MXU_DTYPE) for n in range(ni)]
    dxf = [_mm(bgb[c][h // 4], dstb[at(c, h)]) for c, h in items]
    xfb = [(xs[c][h] * f_b[at(c, h)]).astype(MXU_DTYPE) for c, h in items]
    dxs = [_mm_tn(w[at(c, h)], dyb[at(c, h)]) + d_rows[h] * dys[c][h] + f_b[at(c, h)] * dxf[at(c, h)]
           for c, h in items]
    dd_item = [jnp.sum(dys[c][h] * xs[c][h], axis=0, keepdims=True) for c, h in items]
    dcg_h = [_mm_nt(dgb[n], prevb[n]) for n in range(ni)]
    dbg_h = [_mm_nt(xfb[n], dstb[n]) for n in range(ni)]
    zt = [dw[n] * u[n] for n in range(ni)]
    dseg = [zt[n] * dt_row[n] for n in range(ni)]
    dcb_h = [dw[n] * lm[n] * dt_row[n] for n in range(ni)]
    four = lambda lst, c, g: lst[at(c, 4 * g)] + lst[at(c, 4 * g + 1)] + lst[at(c, 4 * g + 2)] + lst[at(c, 4 * g + 3)]
    dcb = {(c, g): four(dcb_h, c, g).astype(MXU_DTYPE) for c, g in groups}
    dcg = [[four(dcg_h, c, g) + _mm(dcb[c, g], bgb[c][g]) for g in range(2)] for c in range(nq)]
    dbg = [[four(dbg_h, c, g) + _mm_tn(dcb[c, g], cgb[c][g]) for g in range(2)] for c in range(nq)]
    r1 = [jnp.sum(dg[n] * gmat[n] + dseg[n], axis=1, keepdims=True) for n in range(ni)]
    r2 = [jnp.sum(dxf[at(c, h)] * xs[c][h], axis=1, keepdims=True) for c, h in items]
    tt = [jnp.where(lane < HALF, jnp.broadcast_to(r1[n], (BLK, BLK)), jnp.broadcast_to(r2[n], (BLK, BLK))).T
          for n in range(ni)]
    r1_row = [tt[n][0:1, :] for n in range(ni)]
    r2_row = [tt[n][HALF:HALF + 1, :] for n in range(ni)]
    d_el = [full_sum(dhs[at(c, h)] * prev[c][h]) for c, h in items]
    da_last = [jnp.sum(r2_row[n] * f_row[n], axis=1, keepdims=True) + el[n] * d_el[n] for n in range(ni)]
    da_row = [r1_row[n] - jnp.sum(dseg[n], axis=0, keepdims=True) - r2_row[n] * f_row[n]
              + jnp.where(lane_row == BLK - 1, da_last[n], 0.0) for n in range(ni)]
    ddt_row = [jnp.sum(zt[n], axis=0, keepdims=True) + r2_row[n] * e_row[n] for n in range(ni)]
    draw, dalog = [], jnp.zeros((N_HEADS, BLK), F32)
    for c in range(nq):
        da_t = jnp.zeros((N_HEADS, BLK), F32)
        ddt_t = jnp.zeros((N_HEADS, BLK), F32)
        for h in range(N_HEADS):
            da_t = jnp.where(head == h, da_row[at(c, h)], da_t)
            ddt_t = jnp.where(head == h, ddt_row[at(c, h)], ddt_t)
        d_dta = _mm_hi(da_t, causal.astype(F32))
        dalog = dalog + d_dta * dt_t[c] * a_neg
        draw.append((ddt_t + d_dta * a_neg) * jax.nn.sigmoid(pre_dt[c]))
    ddtb = draw[0]
    for c in range(1, nq):
        ddtb = ddtb + draw[c]
    dd_rows = []
    for h in range(N_HEADS):
        t = dd_item[at(0, h)]
        for c in range(1, nq):
            t = t + dd_item[at(c, h)]
        dd_rows.append(t)
    return ([dxs[c * N_HEADS:(c + 1) * N_HEADS] for c in range(nq)], dbg, dcg, draw,
            [dprev[at(0, h)] for h in range(N_HEADS)], ddtb, dalog, dd_rows)


def _dt_rows(dt_blk):
    return dt_blk.T[:N_HEADS]


def _silu_grad(x):
    s = jax.nn.sigmoid(x)
    return s * (1.0 + x * (1.0 - s))


def _conv_pre(halo, blk, cw_ref, cb_ref):
    ext = jnp.concatenate([halo, blk], axis=0)
    taps = [pltpu.roll(ext, 3 - k, 0)[8:] for k in range(3)] + [blk]
    pre = cb_ref[...] + cw_ref[0:1, :] * taps[0]
    for k in range(1, 4):
        pre = pre + cw_ref[k:k + 1, :] * taps[k]
    return pre, taps


def _ssd_split(pre):
    heads = _split_heads(pre[:, :SSM_W], 4)
    pb = [pre[:, SSM_W + g * D_STATE:SSM_W + (g + 1) * D_STATE] for g in range(2)]
    pc = [pre[:, SSM_W + 2 * D_STATE + g * D_STATE:SSM_W + 2 * D_STATE + (g + 1) * D_STATE] for g in range(2)]
    return heads, pb, pc


def _ssd_fwd(xbc, dt_raw, conv_w, conv_b, dtb_row, alog_row, d_exp, xchg):
    S = xbc.shape[0]
    nc = S // BLK
    nq = SSD_CHUNKS_PER_STEP if nc % SSD_CHUNKS_PER_STEP == 0 else 1
    rows = nq * BLK

    def body(xbc_ref, halo_ref, dt_ref, cw_ref, cb_ref, dtb_ref, alog_ref, d_ref, y_ref, prev_ref, state_ref):
        i = pl.program_id(0)

        @pl.when(i == 0)
        def _():
            state_ref[...] = jnp.zeros_like(state_ref)

        halo = halo_ref[...] * jnp.where(i > 0, 1.0, 0.0)
        pre, _ = _conv_pre(halo, xbc_ref[...], cw_ref, cb_ref)
        xc = _silu(pre)
        split = [_ssd_split(xc[c * BLK:(c + 1) * BLK]) for c in range(nq)]
        dt_t = [_dt_rows(dt_ref[c * BLK:(c + 1) * BLK, :]) for c in range(nq)]
        prev0 = [state_ref[h] for h in range(N_HEADS)]
        d_rows = [d_ref[h:h + 1, :] for h in range(N_HEADS)]
        ys, states = _ssd_chunks([s[0] for s in split], [s[1] for s in split], [s[2] for s in split], dt_t, prev0,
                                 dtb_ref[...], alog_ref[...], d_rows, _ssd_consts())
        for h in range(N_HEADS):
            for c in range(nq):
                prev_ref[c, h] = states[c][h]
            state_ref[h] = states[nq][h]
        y_ref[...] = jnp.concatenate([_join_heads(ys[c]) for c in range(nq)], axis=0)

    vec = pl.BlockSpec((N_HEADS, LANE), _fixed)
    return _hosted_call(
        body, "ssd_fwd", nc // nq,
        in_specs=[pl.BlockSpec((rows, XBC_W), _row),
                  pl.BlockSpec((8, XBC_W), lambda i: (jnp.maximum(i * (rows // 8) - 1, 0), 0)),
                  pl.BlockSpec((rows, LANE), _row),
                  pl.BlockSpec((4, XBC_W), _fixed), pl.BlockSpec((1, XBC_W), _fixed), vec, vec,
                  pl.BlockSpec((N_HEADS, LANE), _fixed)],
        out_specs=[pl.BlockSpec((rows, SSM_W), _row),
                   pl.BlockSpec((nq, N_HEADS, D_STATE, LANE), lambda i: (i, 0, 0, 0))],
        out_shape=[jax.ShapeDtypeStruct((S, SSM_W), F32), jax.ShapeDtypeStruct((nc, N_HEADS, D_STATE, LANE), F32)],
        scratch_shapes=[pltpu.VMEM((N_HEADS, D_STATE, LANE), F32)],
        args=(xbc, xbc, dt_raw, conv_w, conv_b, dtb_row, alog_row, d_exp), xchg=xchg, cparams=_cparams(),
    )


def _ssd_bwd(xbc, dt_raw, prev_states, dy, conv_w, conv_b, dtb_row, alog_row, d_exp, xchg):
    S = xbc.shape[0]
    nc = S // BLK
    nq = SSD_BWD_CHUNKS_PER_STEP if nc % SSD_BWD_CHUNKS_PER_STEP == 0 else 1
    rows, n_steps = nq * BLK, nc // nq

    def body(xbc_ref, halo_ref, dt_ref, prev_ref, dy_ref, cw_ref, cb_ref, dtb_ref, alog_ref, d_ref,
             dxbc_ref, ddt_ref, dcw_ref, dvec_ref, dd_ref, gstate_ref, ghalo_ref):
        i = pl.program_id(0)

        @pl.when(i == 0)
        def _():
            gstate_ref[...] = jnp.zeros_like(gstate_ref)
            ghalo_ref[...] = jnp.zeros_like(ghalo_ref)
            dcw_ref[...] = jnp.zeros_like(dcw_ref)
            dvec_ref[...] = jnp.zeros_like(dvec_ref)
            dd_ref[...] = jnp.zeros_like(dd_ref)

        halo = halo_ref[...] * jnp.where(i < n_steps - 1, 1.0, 0.0)
        pre, taps = _conv_pre(halo, xbc_ref[...], cw_ref, cb_ref)
        xc = _silu(pre)
        split = [_ssd_split(xc[c * BLK:(c + 1) * BLK]) for c in range(nq)]
        dt_t = [_dt_rows(dt_ref[c * BLK:(c + 1) * BLK, :]) for c in range(nq)]
        prev = [[prev_ref[c, h] for h in range(N_HEADS)] for c in range(nq)]
        d_rows = [d_ref[h:h + 1, :] for h in range(N_HEADS)]
        dys = [_split_heads(dy_ref[c * BLK:(c + 1) * BLK, :], 4) for c in range(nq)]
        dh_last = [gstate_ref[h] for h in range(N_HEADS)]
        dheads, dpb, dpc, ddt_t, dprev0, ddtb, dalog, dd_rows = _ssd_chunks_bwd(
            [s[0] for s in split], [s[1] for s in split], [s[2] for s in split], dt_t, prev, dtb_ref[...],
            alog_ref[...], d_rows, dys, dh_last, _ssd_consts())
        for h in range(N_HEADS):
            gstate_ref[h] = dprev0[h]
            dd_ref[h:h + 1, :] += dd_rows[h]
        pad = jnp.zeros((BLK - N_HEADS, BLK), F32)
        ddt_ref[...] = jnp.concatenate([jnp.concatenate([ddt_t[c], pad], axis=0).T for c in range(nq)],
                                       axis=0).astype(ddt_ref.dtype)
        dvec_ref[0:N_HEADS, :] += ddtb
        dvec_ref[N_HEADS:, :] += dalog
        dxc = jnp.concatenate([jnp.concatenate([_join_heads(dheads[c])] + list(dpb[c]) + list(dpc[c]), axis=1)
                               for c in range(nq)], axis=0)
        dpre = dxc * _silu_grad(pre)
        zeros8 = jnp.zeros((8, XBC_W), F32)
        dpe = jnp.concatenate([zeros8, dpre, zeros8], axis=0)
        n_ext = 16 + rows
        dext = cw_ref[3:4, :] * dpe[:8 + rows]
        dcw_ref[3:4, :] += jnp.sum(dpre * taps[3], axis=0, keepdims=True)
        for k in range(3):
            dext = dext + cw_ref[k:k + 1, :] * pltpu.roll(dpe, n_ext - (3 - k), 0)[:8 + rows]
            dcw_ref[k:k + 1, :] += jnp.sum(dpre * taps[k], axis=0, keepdims=True)
        dcw_ref[4:5, :] += jnp.sum(dpre, axis=0, keepdims=True)
        dxbc_ref[...] = jnp.concatenate([dext[8:rows], dext[rows:] + ghalo_ref[...]], axis=0).astype(dxbc_ref.dtype)
        ghalo_ref[...] = dext[:8, :]

    vec = pl.BlockSpec((N_HEADS, LANE), _fixed)
    rev = lambda i: (n_steps - 1 - i, 0)
    return _hosted_call(
        body, "ssd_bwd", n_steps,
        in_specs=[pl.BlockSpec((rows, XBC_W), rev),
                  pl.BlockSpec((8, XBC_W), lambda i: (jnp.maximum((n_steps - 1 - i) * (rows // 8) - 1, 0), 0)),
                  pl.BlockSpec((rows, LANE), rev),
                  pl.BlockSpec((nq, N_HEADS, D_STATE, LANE), lambda i: (n_steps - 1 - i, 0, 0, 0)),
                  pl.BlockSpec((rows, SSM_W), rev),
                  pl.BlockSpec((4, XBC_W), _fixed), pl.BlockSpec((1, XBC_W), _fixed), vec, vec,
                  pl.BlockSpec((N_HEADS, LANE), _fixed)],
        out_specs=[pl.BlockSpec((rows, XBC_W), rev), pl.BlockSpec((rows, LANE), rev),
                   pl.BlockSpec((8, XBC_W), _fixed), pl.BlockSpec((2 * N_HEADS, LANE), _fixed),
                   pl.BlockSpec((N_HEADS, LANE), _fixed)],
        out_shape=[jax.ShapeDtypeStruct((S, XBC_W), MXU_DTYPE), jax.ShapeDtypeStruct((S, LANE), MXU_DTYPE),
                   jax.ShapeDtypeStruct((8, XBC_W), F32), jax.ShapeDtypeStruct((2 * N_HEADS, LANE), F32),
                   jax.ShapeDtypeStruct((N_HEADS, LANE), F32)],
        scratch_shapes=[pltpu.VMEM((N_HEADS, D_STATE, LANE), F32), pltpu.VMEM((8, XBC_W), F32)],
        args=(xbc, xbc, dt_raw, prev_states, dy, conv_w, conv_b, dtb_row, alog_row, d_exp), xchg=xchg,
        cparams=_cparams(VMEM_BIG),
    )


def _adamw_math(w, g, m, v):
    m = ADAM_B1 * m + (1.0 - ADAM_B1) * g
    v = ADAM_B2 * v + (1.0 - ADAM_B2) * jnp.square(g)
    m_hat = m / (1.0 - ADAM_B1 ** ADAM_STEP)
    v_hat = v / (1.0 - ADAM_B2 ** ADAM_STEP)
    delta = -ADAM_LR * (m_hat / (jnp.sqrt(v_hat) + ADAM_EPS) + ADAM_WD * w)
    return delta, m, v


def _reduce_adamw(parts, w, m, v, name):
    R, C = w.shape

    def body(p_ref, w_ref, m_ref, v_ref, g_ref, d_ref, nm_ref, nv_ref):
        g = p_ref[0].astype(F32)
        for i in range(1, N_DEV):
            g = g + p_ref[i].astype(F32)
        d, nm, nv = _adamw_math(w_ref[...], g, m_ref[...], v_ref[...])
        g_ref[...] = g
        d_ref[...] = d
        nm_ref[...] = nm
        nv_ref[...] = nv

    if R % 16 == 0:
        tr = max(t for t in range(16, 257, 16) if R % t == 0)
        n, blk, pblk = R // tr, pl.BlockSpec((tr, C), _row), pl.BlockSpec((N_DEV, tr, C), lambda i: (0, i, 0))
    else:
        tl = 256
        n, blk, pblk = C // tl, pl.BlockSpec((R, tl), lambda i: (0, i)), pl.BlockSpec((N_DEV, R, tl),
                                                                                      lambda i: (0, 0, i))
    return pl.pallas_call(
        body, name=name, grid=(n,), in_specs=[pblk, blk, blk, blk],
        out_specs=[blk] * 4, out_shape=[jax.ShapeDtypeStruct((R, C), F32)] * 4,
    )(parts, w, m, v)


def _reduce_adamw_hosting(parts_list, wmv_list, name, xchg):
    n_arr = len(parts_list)
    C = wmv_list[0][0].shape[1]
    tl = 256

    def body(*refs):
        p_refs, wmv_refs, o_refs = refs[:n_arr], refs[n_arr:4 * n_arr], refs[4 * n_arr:]
        for k in range(n_arr):
            g = p_refs[k][0].astype(F32)
            for i in range(1, N_DEV):
                g = g + p_refs[k][i].astype(F32)
            w_ref, m_ref, v_ref = wmv_refs[3 * k:3 * k + 3]
            d, nm, nv = _adamw_math(w_ref[...], g, m_ref[...], v_ref[...])
            for o, val in zip(o_refs[4 * k:4 * k + 4], (g, d, nm, nv)):
                o[...] = val

    in_specs = [pl.BlockSpec((N_DEV, w.shape[0], tl), lambda i: (0, 0, i)) for w, _, _ in wmv_list]
    in_specs += [pl.BlockSpec((w.shape[0], tl), lambda i: (0, i)) for w, _, _ in wmv_list for _ in range(3)]
    out_specs = [pl.BlockSpec((w.shape[0], tl), lambda i: (0, i)) for w, _, _ in wmv_list for _ in range(4)]
    out_shape = [jax.ShapeDtypeStruct(w.shape, F32) for w, _, _ in wmv_list for _ in range(4)]
    args = list(parts_list) + [a for wmv in wmv_list for a in wmv]
    outs, x_out = _hosted_call(body, name, C // tl, in_specs, out_specs, out_shape, [], args, xchg,
                               _cparams(VMEM_BIG))
    return [outs[4 * k:4 * k + 4] for k in range(n_arr)], x_out


_SMALL_NAMES = ("ada_b", "norm1", "conv_w", "conv_b", "dt_bias", "A_log", "D_skip", "sinks", "attn_out_norm",
                "ssm_out_norm", "norm2", "rel_bias", "final_norm")
N_MOD = 6 * D_MODEL


def _mod_row(a0, a1, a2):
    return jnp.concatenate([a0[2:3], a0[1:2], a1[0:1], a2[2:3], a2[1:2], a2[3:4]], axis=1)


def _small_update(gathered, params):
    n_g = len(gathered)
    flat = [a for name in _SMALL_NAMES for a in params[name]]

    def body(*refs):
        a0_ref, a1_ref, a2_ref, cw_ref, dv_ref, dd_ref, ds_ref, dr_ref, c_ref = refs[:n_g]
        wmv = refs[n_g:n_g + len(flat)]
        outs = refs[n_g + len(flat):]

        def total(ref):
            t = ref[0]
            for i in range(1, N_DEV):
                t = t + ref[i]
            return t

        t0, t1, t2, tcw, tdv, tdd, tds, tdr = [total(r) for r in (a0_ref, a1_ref, a2_ref, cw_ref, dv_ref, dd_ref,
                                                                   ds_ref, dr_ref)]
        r8 = lax.broadcasted_iota(jnp.int32, (N_HEADS, LANE), 0)
        l8 = lax.broadcasted_iota(jnp.int32, (N_HEADS, LANE), 1)

        def diag_row(t):
            return jnp.sum(jnp.where(r8 == l8, t, 0.0), axis=0, keepdims=True)[:, :N_HEADS]

        def lane_sums(t):
            return diag_row(jnp.broadcast_to(jnp.sum(t, axis=1, keepdims=True), (N_HEADS, LANE)))

        me = _lin(_my_pos())
        n_cw = XBC_W // N_DEV
        cw_mine = jnp.zeros((4, n_cw), F32)
        for j in range(N_DEV):
            cw_mine = cw_mine + tcw[0:4, j * n_cw:(j + 1) * n_cw] * jnp.where(me == j, 1.0, 0.0)
        grads = {
            "ada_b": _mod_row(t0, t1, t2), "norm1": t0[0:1], "conv_w": cw_mine, "conv_b": tcw[4:5],
            "dt_bias": lane_sums(tdv[:N_HEADS]), "A_log": lane_sums(tdv[N_HEADS:]), "D_skip": lane_sums(tdd),
            "sinks": diag_row(tds), "attn_out_norm": t1[1:2, :ATTN_W], "ssm_out_norm": t1[1:2, ATTN_W:],
            "norm2": t2[0:1], "rel_bias": tdr[:, :N_HEADS], "final_norm": t2[4:5],
        }
        for k, name in enumerate(_SMALL_NAMES):
            w_ref, m_ref, v_ref = wmv[3 * k:3 * k + 3]
            g = grads[name]
            d, nm, nv = _adamw_math(w_ref[...], g, m_ref[...], v_ref[...])
            for o, val in zip(outs[4 * k:4 * k + 4], (g, d, nm, nv)):
                o[...] = val
        loss_ref, call_ref, dmod_ref = outs[4 * len(_SMALL_NAMES):]
        loss_ref[...] = t2[5:6, 0:1]
        call_ref[...] = jnp.concatenate([c_ref[i] for i in range(N_DEV)], axis=0)
        dmod_ref[...] = jnp.concatenate([_mod_row(a0_ref[i], a1_ref[i], a2_ref[i]) for i in range(N_DEV)], axis=0)

    out_shape = [jax.ShapeDtypeStruct(params[name][0].shape, F32) for name in _SMALL_NAMES for _ in range(4)]
    out_shape += [jax.ShapeDtypeStruct((1, 1), F32), jax.ShapeDtypeStruct((N_DEV, D_MODEL), F32),
                  jax.ShapeDtypeStruct((N_DEV, N_MOD), F32)]
    res = pl.pallas_call(body, name="small_update", out_shape=out_shape)(*gathered, *flat)
    upd = {name: res[4 * k:4 * k + 4] for k, name in enumerate(_SMALL_NAMES)}
    loss, c_all, dmod_all = res[4 * len(_SMALL_NAMES):]
    return upd, loss, c_all, dmod_all


def _ada_w_update(c_all, dmod_all, w, m, v):
    chunk = w.shape[1]

    def body(c_ref, dm_ref, w_ref, m_ref, v_ref, g_ref, d_ref, nm_ref, nv_ref):
        me = _lin(_my_pos())
        dm = jnp.zeros((N_DEV, chunk), F32)
        for j in range(N_DEV):
            dm = dm + dm_ref[:, j * chunk:(j + 1) * chunk] * jnp.where(me == j, 1.0, 0.0)
        g = lax.dot_general(_silu(c_ref[...]), dm, (((0,), (0,)), ((), ())), precision=HI,
                            preferred_element_type=F32)
        d, nm, nv = _adamw_math(w_ref[...], g, m_ref[...], v_ref[...])
        g_ref[...] = g
        d_ref[...] = d
        nm_ref[...] = nm
        nv_ref[...] = nv

    tr = 256
    blk = pl.BlockSpec((tr, chunk), _row)
    return pl.pallas_call(
        body, name="ada_w_update", grid=(w.shape[0] // tr,),
        in_specs=[pl.BlockSpec((N_DEV, tr), lambda i: (0, i)), pl.BlockSpec(dmod_all.shape, _fixed), blk, blk, blk],
        out_specs=[blk] * 4, out_shape=[jax.ShapeDtypeStruct(w.shape, F32)] * 4,
    )(c_all, dmod_all, w, m, v)


def _local_step(x, tgt, c, mod, w_in, conv_w, w_o_mine, w_gu_mine, w_d_mine, p):
    S = x.shape[0]
    tm = min(512, S)
    tmm = min(256, S)
    tw = min(2048, S)
    shift1, scale1, gate1, shift2, scale2, gate2 = [mod[i:i + 1] for i in range(6)]
    buckets = jnp.asarray(_t5_bucket_table())
    per_head = lambda a: jnp.broadcast_to(a.reshape(N_HEADS, 1), (N_HEADS, LANE))
    dtb_row, alog_row, d_exp = per_head(p["dt_bias"]), per_head(p["A_log"]), per_head(p["D_skip"])
    sinks = p["sinks"].reshape(N_HEADS)

    d_cut, gu_cut = WD_CUT, WGU_CUTS
    (qkv, z, xbc, dt_raw), (g_d_a,) = _in_proj_fwd(x, p["norm1"], scale1, shift1, w_in, tm,
                                                   ([w_d_mine[:d_cut]], "two-level"))
    bias = _attn_bias(buckets, p["rel_bias"])
    (ya,), (g_gu_a,) = _attn_fwd(qkv, bias, sinks, ([w_gu_mine[:gu_cut[0]]], "two-level"))
    (ys, prev_states), (g_gu_b, g_o) = _ssd_fwd(xbc, dt_raw, conv_w, p["conv_b"], dtb_row, alog_row, d_exp,
                                                ([w_gu_mine[gu_cut[0]:gu_cut[1]], w_o_mine], "two-level"))
    w_o = g_o.reshape(D_MODEL, D_MODEL)
    x1, (g_gu_c, g_d_b) = _out_proj_fwd(x, ya, ys, z, p["attn_out_norm"], p["ssm_out_norm"], gate1, w_o, tm,
                                        ([w_gu_mine[gu_cut[1]:], w_d_mine[d_cut:]], "two-level"))
    dx1, h2, dgu, act, dmlp, acc2 = _mlp_loss(x1, tgt, p["norm2"], scale2, shift2, gate2, p["final_norm"],
                                              (g_gu_a, g_gu_b, g_gu_c), (g_d_a, g_d_b), tmm)
    g_w_gu = _wgrad(dgu, h2, 2 * D_FF // 4, tw, "wgrad_gate_up")
    g_w_d = _wgrad(act, dmlp, D_FF // 2, tw, "wgrad_down")
    (dya, dys, dz, g_w_o, acc1), (r_d,) = _out_proj_bwd(
        dx1, ya, ys, z, p["attn_out_norm"], p["ssm_out_norm"], gate1, w_o, tm,
        ([g_w_d.reshape(N_DEV, D_FF // N_DEV, D_MODEL)], True))
    (dq, dkv, dbias, dsk), (r_o,) = _attn_bwd(qkv, ya, dya, bias, sinks,
                                              ([g_w_o.reshape(N_DEV, D_MODEL // N_DEV, D_MODEL)], True))
    drel, dsink = _attn_finish(dbias, dsk, buckets)
    (dxbc, ddt, dcw, dvec, dd), (r_gu,) = _ssd_bwd(
        xbc, dt_raw, prev_states, dys, conv_w, p["conv_b"], dtb_row, alog_row, d_exp,
        ([g_w_gu.reshape(N_DEV, 2 * D_FF // N_DEV, D_MODEL)], True))
    gx, h1, acc0 = _in_proj_bwd(x, dx1, dq, dkv, dz, dxbc, ddt, p["norm1"], scale1, shift1, w_in, tm)
    dproj = (dq, dkv, dz, dxbc, ddt)
    half = D_MODEL // 2
    slots = lambda g: g[:IN_W].reshape(N_DEV, IN_W // N_DEV, half)
    g_in_a, gathered = _wgrad(dproj, h1, IN_PAD, tw, "wgrad_in_a",
                              ([acc0, acc1, acc2, dcw, dvec, dd, dsink, drel, c], False), g_cols=(half, 0))
    g_in_b, (r_in_a,) = _wgrad(dproj, h1, IN_PAD, tw, "wgrad_in_b", ([slots(g_in_a)], True), g_cols=(half, 1))
    return gx, (r_in_a, slots(g_in_b)), (r_o, r_gu, r_d), gathered


def kernel(x, c, ada_w, ada_b, norm1, w_in, conv_w, conv_b, dt_bias, A_log, D_skip, sinks, attn_out_norm, ssm_out_norm, w_o, norm2, w_gate_up, w_down, rel_bias, final_norm, loss_target, m_ada_w, m_ada_b, m_norm1, m_w_in, m_conv_w, m_conv_b, m_dt_bias, m_A_log, m_D_skip, m_sinks, m_attn_out_norm, m_ssm_out_norm, m_w_o, m_norm2, m_w_gate_up, m_w_down, m_rel_bias, m_final_norm, v_ada_w, v_ada_b, v_norm1, v_w_in, v_conv_w, v_conv_b, v_dt_bias, v_A_log, v_D_skip, v_sinks, v_attn_out_norm, v_ssm_out_norm, v_w_o, v_norm2, v_w_gate_up, v_w_down, v_rel_bias, v_final_norm):
    two_d = lambda a: a if a.ndim == 2 else a.reshape(-1, a.shape[-1])
    small_params = dict(
        ada_b=(ada_b, m_ada_b, v_ada_b), norm1=(norm1, m_norm1, v_norm1), conv_w=(conv_w, m_conv_w, v_conv_w),
        conv_b=(conv_b, m_conv_b, v_conv_b), dt_bias=(dt_bias, m_dt_bias, v_dt_bias), A_log=(A_log, m_A_log, v_A_log),
        D_skip=(D_skip, m_D_skip, v_D_skip), sinks=(sinks, m_sinks, v_sinks),
        attn_out_norm=(attn_out_norm, m_attn_out_norm, v_attn_out_norm),
        ssm_out_norm=(ssm_out_norm, m_ssm_out_norm, v_ssm_out_norm), norm2=(norm2, m_norm2, v_norm2),
        rel_bias=(rel_bias, m_rel_bias, v_rel_bias), final_norm=(final_norm, m_final_norm, v_final_norm))
    small_params = {k: tuple(two_d(a) for a in v) for k, v in small_params.items()}
    S = x.shape[1]
    xs, tgt = x.reshape(S, D_MODEL), loss_target.reshape(S, D_MODEL)
    ada_w2 = ada_w[0]
    chunk = ada_w2.shape[1]
    t_in = [jnp.transpose(a[0]) for a in (w_in, m_w_in, v_w_in)]
    t_gu = [jnp.transpose(a[0]) for a in (w_gate_up, m_w_gate_up, v_w_gate_up)]

    mod, (g_in, g_cw) = _mod_and_gather(c, ada_w2, ada_b.reshape(N_DEV, chunk), [t_in[0].astype(WIRE_DTYPE), conv_w[0]])
    mod = mod.reshape(6, D_MODEL)
    w_in_full = jnp.pad(g_in.reshape(IN_W, D_MODEL), ((0, IN_PAD - IN_W), (0, 0)))
    conv_w_full = jnp.transpose(g_cw, (1, 0, 2)).reshape(4, XBC_W)

    p = {k: v[0] for k, v in small_params.items()}
    gx, (r_in_a, gw_in_b), (r_o, r_gu, r_d), gathered = _local_step(
        xs, tgt, c, mod, w_in_full, conv_w_full, w_o[0].astype(WIRE_DTYPE), t_gu[0].astype(WIRE_DTYPE),
        w_down[0].astype(WIRE_DTYPE), p)

    (u_gu, u_d, u_o), (r_in_b,) = _reduce_adamw_hosting(
        [r_gu, r_d, r_o], [tuple(t_gu), (w_down[0], m_w_down[0], v_w_down[0]), (w_o[0], m_w_o[0], v_w_o[0])],
        "adamw_big", ([gw_in_b], True))
    r_in = jnp.concatenate([r_in_a, r_in_b], axis=2)

    small, loss, c_all, dmod_all = _small_update(gathered, small_params)

    big = {
        "ada_w": _ada_w_update(c_all, dmod_all, ada_w2, m_ada_w[0], v_ada_w[0]),
        "w_in": [jnp.transpose(a) for a in _reduce_adamw(r_in, *t_in, "adamw_w_in")],
        "w_o": u_o,
        "w_gate_up": [jnp.transpose(a) for a in u_gu],
        "w_down": u_d,
    }
    big.update(small)

    order = ['ada_w', 'ada_b', 'norm1', 'w_in', 'conv_w', 'conv_b', 'dt_bias', 'A_log', 'D_skip', 'sinks',
             'attn_out_norm', 'ssm_out_norm', 'w_o', 'norm2', 'w_gate_up', 'w_down', 'rel_bias', 'final_norm']
    shapes = dict(ada_w=ada_w.shape, ada_b=ada_b.shape, norm1=norm1.shape, w_in=w_in.shape, conv_w=conv_w.shape,
                  conv_b=conv_b.shape, dt_bias=dt_bias.shape, A_log=A_log.shape, D_skip=D_skip.shape,
                  sinks=sinks.shape, attn_out_norm=attn_out_norm.shape, ssm_out_norm=ssm_out_norm.shape,
                  w_o=w_o.shape, norm2=norm2.shape, w_gate_up=w_gate_up.shape, w_down=w_down.shape,
                  rel_bias=rel_bias.shape, final_norm=final_norm.shape)
    outs = [[], [], [], []]
    for name in order:
        for kind in range(4):
            outs[kind].append(big[name][kind].reshape(shapes[name]))
    return (loss.reshape(()), gx.reshape(x.shape), *outs[0], *outs[1], *outs[2], *outs[3])
```

```python
import functools

import numpy as np
import jax
import jax.numpy as jnp
from jax import lax
from jax.experimental import pallas as pl
from jax.experimental.pallas import tpu as pltpu

F32 = jnp.float32
MXU_DTYPE = jnp.bfloat16
WIRE_DTYPE = jnp.bfloat16
HI = lax.Precision.HIGHEST
MESH = pl.DeviceIdType.MESH
N_DEV = 8

D_MODEL = 1024
ATTN_W = 512
KV_W = 128
SSM_W = 512
XBC_W = 1024
N_HEADS = 8
D_STATE = 128
D_FF = 2816
IN_W = 2312
IN_PAD = 2432
BLK = 128
N_BUCKETS = 32
EPS = 1e-6
LANE = 128
HALF = 64

ADAM_LR, ADAM_B1, ADAM_B2, ADAM_EPS, ADAM_WD, ADAM_STEP = 0.001, 0.9, 0.999, 1e-08, 0.01, 10

VMEM_BIG = 56 * 1024 * 1024
WD_CUT = 288
WGU_CUTS = (240, 496)


def _cparams(vmem=None):
    if vmem is None:
        return pltpu.CompilerParams()
    return pltpu.CompilerParams(vmem_limit_bytes=vmem)


def _mm(a, b):
    return jnp.dot(a.astype(MXU_DTYPE), b.astype(MXU_DTYPE), preferred_element_type=F32)


def _mm_nt(a, b):
    return lax.dot_general(a.astype(MXU_DTYPE), b.astype(MXU_DTYPE), (((1,), (1,)), ((), ())),
                           preferred_element_type=F32)


def _mm_tn(a, b):
    return lax.dot_general(a.astype(MXU_DTYPE), b.astype(MXU_DTYPE), (((0,), (0,)), ((), ())),
                           preferred_element_type=F32)


def _mm_hi(a, b):
    return jnp.dot(a, b, precision=HI, preferred_element_type=F32)


def _silu(x):
    return x * jax.nn.sigmoid(x)


def _softplus(x):
    return jnp.maximum(x, 0.0) + jnp.log1p(jnp.exp(-jnp.abs(x)))


def _rms(x, g, n):
    return x * lax.rsqrt(jnp.sum(x * x, axis=-1, keepdims=True) * (1.0 / n) + EPS) * g


def _modnorm(x, g, scale, shift):
    return _rms(x, g, x.shape[-1]) * (1.0 + scale) + shift


def _modnorm_parts(x):
    r = lax.rsqrt(jnp.sum(x * x, axis=-1, keepdims=True) * (1.0 / x.shape[-1]) + EPS)
    return r, x * r


def _modnorm_bwd(r, xhat, g, scale, dy):
    dyg = dy * (g * (1.0 + scale))
    c = jnp.sum(dyg * xhat, axis=-1, keepdims=True) * (1.0 / xhat.shape[-1])
    dx = r * (dyg - xhat * c)
    ct = jnp.sum(dy * xhat, axis=0, keepdims=True)
    return dx, ct * (1.0 + scale), ct * g, jnp.sum(dy, axis=0, keepdims=True)


def _lane_iota(shape):
    return lax.broadcasted_iota(jnp.int32, shape, len(shape) - 1)


def _split_pair(t):
    lane = _lane_iota(t.shape)
    lo = jnp.where(lane < HALF, t, 0.0)
    hi = pltpu.roll(jnp.where(lane >= HALF, t, 0.0), HALF, 1)
    return lo, hi


def _join_pair(lo, hi):
    lane = _lane_iota(lo.shape)
    return jnp.where(lane < HALF, lo, pltpu.roll(hi, HALF, 1))


def _split_heads(t, n_pairs):
    out = []
    for p in range(n_pairs):
        out.extend(_split_pair(t[:, p * LANE:(p + 1) * LANE]))
    return out


def _join_heads(hs):
    return jnp.concatenate([_join_pair(hs[2 * p], hs[2 * p + 1]) for p in range(len(hs) // 2)], axis=1)


def _t5_bucket_table():
    dist = np.arange(BLK)[:, None] + BLK - np.arange(2 * BLK)[None, :]
    n = np.maximum(dist, 0)
    max_exact = N_BUCKETS // 2
    large = max_exact + (np.log(np.maximum(n, 1) / max_exact) / np.log(128 / max_exact)
                         * (N_BUCKETS - max_exact)).astype(np.int32)
    large = np.minimum(large, N_BUCKETS - 1)
    return np.where(n < max_exact, n, large).astype(np.int32)


def _my_pos():
    return lax.axis_index("x"), lax.axis_index("y"), lax.axis_index("c")


def _peer(k):
    x, y, c = _my_pos()
    return (1 - x if k & 4 else x, 1 - y if k & 2 else y, 1 - c if k & 1 else c)


def _lin(pos):
    return 4 * pos[0] + 2 * pos[1] + pos[2]


def _xchg_copies(ins, outs, sems, scatter):
    local_sem, send_sem, recv_sem = sems
    me = _lin(_my_pos())
    local, remote = [], []
    for a in range(len(ins)):
        src = ins[a].at[me] if scatter else ins[a]
        local.append(pltpu.make_async_copy(src, outs[a].at[me], local_sem.at[a]))
    for k in range(1, N_DEV):
        peer = _peer(k)
        for a in range(len(ins)):
            src = ins[a].at[_lin(peer)] if scatter else ins[a]
            remote.append(pltpu.make_async_remote_copy(src, outs[a].at[me], send_sem.at[a, k - 1],
                                                       recv_sem.at[a, k - 1], device_id=peer, device_id_type=MESH))
    return local, remote


def _xchg_start(ins, outs, sems, scatter):
    local, remote = _xchg_copies(ins, outs, sems, scatter)
    for cp in local + remote:
        cp.start()


def _xchg_wait(ins, outs, sems, scatter):
    local, remote = _xchg_copies(ins, outs, sems, scatter)
    for cp in local:
        cp.wait()
    for cp in remote:
        cp.wait_send()
        cp.wait_recv()


def _xchg_shapes(arrs, scatter):
    n = len(arrs)
    if scatter:
        out_shape = [jax.ShapeDtypeStruct(a.shape, a.dtype) for a in arrs]
    else:
        out_shape = [jax.ShapeDtypeStruct((N_DEV,) + a.shape, a.dtype) for a in arrs]
    sems = [pltpu.SemaphoreType.DMA((n,)), pltpu.SemaphoreType.DMA((n, N_DEV - 1)),
            pltpu.SemaphoreType.DMA((n, N_DEV - 1))]
    return out_shape, sems


_CHIPS = (2, 4, 6)


def _g2_sems(n):
    dma = pltpu.SemaphoreType.DMA
    return [dma((n,)), dma((n, N_DEV)), dma((n, N_DEV)), dma((n, len(_CHIPS))), dma((n, len(_CHIPS)))]


class _TwoLevelGather:
    def __init__(self, ins, outs, sems):
        self.ins, self.outs = ins, outs
        self.local_sem, self.send_sem, self.recv_sem, self.fsend_sem, self.frecv_sem = sems
        self.n = len(ins)

    def _direct(self, a, k):
        return pltpu.make_async_remote_copy(self.ins[a], self.outs[a].at[_lin(_my_pos())], self.send_sem.at[a, k],
                                            self.recv_sem.at[a, k], device_id=_peer(k), device_id_type=MESH)

    def _handed_on(self, a, j, origin):
        slot = self.outs[a].at[origin]
        return pltpu.make_async_remote_copy(slot, slot, self.fsend_sem.at[a, j], self.frecv_sem.at[a, j],
                                            device_id=_peer(1), device_id_type=MESH)

    def _local(self, a):
        return pltpu.make_async_copy(self.ins[a], self.outs[a].at[_lin(_my_pos())], self.local_sem.at[a])

    def start(self):
        for a in range(self.n):
            self._local(a).start()
        for k in (1,) + _CHIPS:
            for a in range(self.n):
                self._direct(a, k).start()

    def forward(self):
        for j, k in enumerate(_CHIPS):
            for a in range(self.n):
                self._direct(a, k).wait_recv()
                self._handed_on(a, j, _lin(_peer(k))).start()

    def finish(self):
        for a in range(self.n):
            self._direct(a, 1).wait_recv()
            for j, k in enumerate(_CHIPS):
                self._handed_on(a, j, _lin(_peer(k ^ 1))).wait_recv()
            self._local(a).wait()
            for k in (1,) + _CHIPS:
                self._direct(a, k).wait_send()
            for j, k in enumerate(_CHIPS):
                self._handed_on(a, j, _lin(_peer(k))).wait_send()


def _mod_and_gather(c, ada_w, ada_b8, arrs):
    n = len(arrs)
    chunk = ada_w.shape[1]
    out_shape = [jax.ShapeDtypeStruct((N_DEV, 1, chunk), F32)]
    out_shape += [jax.ShapeDtypeStruct((N_DEV,) + a.shape, a.dtype) for a in arrs]

    def modulation(c_ref, w_ref, b_ref, out_ref, cbuf, part, s1, r1, s2, r2):
        me = _lin(_my_pos())
        first = []
        for k in range(1, N_DEV):
            cp = pltpu.make_async_remote_copy(c_ref, cbuf.at[me], s1.at[k - 1], r1.at[k - 1],
                                              device_id=_peer(k), device_id_type=MESH)
            cp.start()
            first.append(cp)
        cbuf[me] = c_ref[...]
        for cp in first:
            cp.wait_send()
            cp.wait_recv()
        cond = _silu(jnp.concatenate([cbuf[i] for i in range(N_DEV)], axis=0))
        mod = _mm_hi(cond, w_ref[...]) + b_ref[pl.ds(me, 1), :]
        for j in range(N_DEV):
            part[j] = mod[j:j + 1, :]
        second = []
        for k in range(1, N_DEV):
            peer = _peer(k)
            cp = pltpu.make_async_remote_copy(part.at[_lin(peer)], out_ref.at[me], s2.at[k - 1], r2.at[k - 1],
                                              device_id=peer, device_id_type=MESH)
            cp.start()
            second.append(cp)
        out_ref[me] = part[me]
        for cp in second:
            cp.wait_send()
            cp.wait_recv()

    def body(*refs):
        c_ref, w_ref, b_ref = refs[:3]
        ins = refs[3:3 + n]
        mod_ref = refs[3 + n]
        outs = refs[4 + n:4 + 2 * n]
        cbuf, part, s1, r1, s2, r2 = refs[4 + 2 * n:10 + 2 * n]
        gather = _TwoLevelGather(ins, outs, refs[10 + 2 * n:])
        gather.start()
        modulation(c_ref, w_ref, b_ref, mod_ref, cbuf, part, s1, r1, s2, r2)
        gather.forward()
        gather.finish()

    hbm = pl.BlockSpec(memory_space=pltpu.HBM)
    vm = pl.BlockSpec(memory_space=pltpu.VMEM)
    dma = pltpu.SemaphoreType.DMA
    res = pl.pallas_call(
        body, name="mod_and_gather", out_shape=out_shape, in_specs=[vm, vm, vm] + [hbm] * n,
        out_specs=[vm] + [hbm] * n,
        scratch_shapes=[pltpu.VMEM((N_DEV, 1, D_MODEL), F32), pltpu.VMEM((N_DEV, 1, chunk), F32)]
        + [dma((N_DEV - 1,))] * 4 + _g2_sems(n),
    )(c, ada_w, ada_b8, *arrs)
    return res[0], res[1:]


def _hosted_call(body, name, grid, in_specs, out_specs, out_shape, scratch_shapes, args, xchg, cparams):
    xchgs = [xchg] if isinstance(xchg, tuple) else list(xchg)
    grid = (grid,) if isinstance(grid, int) else tuple(grid)
    n_in, n_out, n_scr = len(in_specs), len(out_specs), len(scratch_shapes)
    arrs = [a for group, _ in xchgs for a in group]
    n = len(arrs)
    x_shape, x_sems, sem_counts = [], [], []
    for group, mode in xchgs:
        shapes, sems = _xchg_shapes(group, False if mode == "two-level" else mode)
        if mode == "two-level":
            sems = _g2_sems(len(group))
        x_shape += shapes
        x_sems += sems
        sem_counts.append(len(sems))
    n_steps = int(np.prod(grid))

    def hosted(*refs):
        ins, refs = refs[:n_in], refs[n_in:]
        x_in, refs = refs[:n], refs[n:]
        outs, refs = refs[:n_out], refs[n_out:]
        x_out, refs = refs[:n], refs[n:]
        scr, sems = refs[:n_scr], refs[n_scr:]
        step = pl.program_id(0)
        for d in range(1, len(grid)):
            step = step * grid[d] + pl.program_id(d)
        parts, a0, s0 = [], 0, 0
        for (group, mode), ns in zip(xchgs, sem_counts):
            parts.append((x_in[a0:a0 + len(group)], x_out[a0:a0 + len(group)], sems[s0:s0 + ns], mode))
            a0, s0 = a0 + len(group), s0 + ns

        @pl.when(step == 0)
        def _():
            for gi, go, gs, mode in parts:
                if mode == "two-level":
                    _TwoLevelGather(gi, go, gs).start()
                else:
                    _xchg_start(gi, go, gs, mode)

        if any(mode == "two-level" for _, mode in xchgs):
            @pl.when(step == (2 * n_steps) // 3)
            def _():
                for gi, go, gs, mode in parts:
                    if mode == "two-level":
                        _TwoLevelGather(gi, go, gs).forward()

        body(*ins, *outs, *scr)

        @pl.when(step == n_steps - 1)
        def _():
            for gi, go, gs, mode in parts:
                if mode == "two-level":
                    _TwoLevelGather(gi, go, gs).finish()
                else:
                    _xchg_wait(gi, go, gs, mode)

    hbm = pl.BlockSpec(memory_space=pltpu.HBM)
    res = pl.pallas_call(
        hosted, name=name, grid=grid, in_specs=list(in_specs) + [hbm] * n,
        out_specs=list(out_specs) + [hbm] * n, out_shape=list(out_shape) + x_shape,
        scratch_shapes=list(scratch_shapes) + x_sems, compiler_params=cparams,
    )(*args, *arrs)
    return res[:n_out], res[n_out:]


def _row(i):
    return (i, 0)


def _fixed(i):
    return (0, 0)


def _in_proj_fwd(x, norm1, scale1, shift1, w_in, tm, xchg):
    S = x.shape[0]

    def body(x_ref, n_ref, sc_ref, sh_ref, w_ref, qkv_ref, z_ref, xbc_ref, dt_ref):
        h = _modnorm(x_ref[...], n_ref[...], sc_ref[...], sh_ref[...])
        p = _mm_nt(h, w_ref[...])
        qkv_ref[...] = p[:, :768].astype(qkv_ref.dtype)
        z_ref[...] = p[:, 768:1280]
        xbc_ref[...] = p[:, 1280:2304]
        dt_ref[...] = p[:, 2304:IN_PAD]

    vec = pl.BlockSpec((1, D_MODEL), _fixed)
    return _hosted_call(
        body, "in_proj_fwd", S // tm,
        in_specs=[pl.BlockSpec((tm, D_MODEL), _row), vec, vec, vec, pl.BlockSpec((IN_PAD, D_MODEL), _fixed)],
        out_specs=[pl.BlockSpec((tm, 768), _row), pl.BlockSpec((tm, SSM_W), _row),
                   pl.BlockSpec((tm, XBC_W), _row), pl.BlockSpec((tm, LANE), _row)],
        out_shape=[jax.ShapeDtypeStruct((S, 768), MXU_DTYPE), jax.ShapeDtypeStruct((S, SSM_W), F32),
                   jax.ShapeDtypeStruct((S, XBC_W), F32), jax.ShapeDtypeStruct((S, LANE), F32)],
        scratch_shapes=[], args=(x, norm1, scale1, shift1, w_in), xchg=xchg, cparams=_cparams(VMEM_BIG),
    )


def _in_proj_bwd(x, dx1, dq, dkv, dz, dxbc, ddt, norm1, scale1, shift1, w_in, tm):
    S = x.shape[0]

    n_steps = S // tm
    half_cols = D_MODEL // 2

    def body(x_ref, dx1_ref, dq_ref, dkv_ref, dz_ref, dxbc_ref, ddt_ref, n_ref, sc_ref, sh_ref, w_ref,
             gx_ref, h_ref, acc_ref, gw_ref, gw_acc):
        i = pl.program_id(0)

        @pl.when(i == 0)
        def _():
            acc_ref[...] = jnp.zeros_like(acc_ref)
            gw_acc[...] = jnp.zeros_like(gw_acc)

        halves = [pl.ds(k * (tm // 2), tm // 2) for k in range(2)]
        dp = [jnp.concatenate([r[rows, :] for r in (dq_ref, dkv_ref, dz_ref, dxbc_ref, ddt_ref)], axis=1)
              for rows in halves]
        dh = [_mm(dp[k], w_ref[...]) for k in range(2)]
        parts = [_modnorm_parts(x_ref[rows, :]) for rows in halves]
        hb = [(parts[k][1] * n_ref[...] * (1.0 + sc_ref[...]) + sh_ref[...]).astype(h_ref.dtype) for k in range(2)]
        gw_acc[...] += _mm_tn(dp[0], hb[0][:, :half_cols]) + _mm_tn(dp[1], hb[1][:, :half_cols])
        bwd = [_modnorm_bwd(parts[k][0], parts[k][1], n_ref[...], sc_ref[...], dh[k]) for k in range(2)]
        for k, rows in enumerate(halves):
            gx_ref[rows, :] = dx1_ref[rows, :] + bwd[k][0]
            h_ref[rows, :] = hb[k]
        acc_ref[0:1, :] += bwd[0][1] + bwd[1][1]
        acc_ref[1:2, :] += bwd[0][2] + bwd[1][2]
        acc_ref[2:3, :] += bwd[0][3] + bwd[1][3]

        @pl.when(i == n_steps - 1)
        def _():
            gw_ref[...] = gw_acc[...].astype(gw_ref.dtype)

    vec = pl.BlockSpec((1, D_MODEL), _fixed)
    return pl.pallas_call(
        body, name="in_proj_bwd", grid=(n_steps,),
        in_specs=[pl.BlockSpec((tm, D_MODEL), _row), pl.BlockSpec((tm, D_MODEL), _row),
                  pl.BlockSpec((tm, ATTN_W), _row), pl.BlockSpec((tm, 2 * KV_W), _row),
                  pl.BlockSpec((tm, SSM_W), _row), pl.BlockSpec((tm, XBC_W), _row), pl.BlockSpec((tm, LANE), _row),
                  vec, vec, vec, pl.BlockSpec((IN_PAD, D_MODEL), _fixed)],
        out_specs=[pl.BlockSpec((tm, D_MODEL), _row), pl.BlockSpec((tm, D_MODEL), _row),
                   pl.BlockSpec((8, D_MODEL), _fixed), pl.BlockSpec((IN_PAD, half_cols), _fixed)],
        out_shape=[jax.ShapeDtypeStruct((S, D_MODEL), F32), jax.ShapeDtypeStruct((S, D_MODEL), MXU_DTYPE),
                   jax.ShapeDtypeStruct((8, D_MODEL), F32), jax.ShapeDtypeStruct((IN_PAD, half_cols), WIRE_DTYPE)],
        scratch_shapes=[pltpu.VMEM((IN_PAD, half_cols), F32)],
        compiler_params=_cparams(VMEM_BIG),
    )(x, dx1, dq, dkv, dz, dxbc, ddt, norm1, scale1, shift1, w_in)


def _out_stage(ya, ys0, ys1, z0, z1, an, sn0, sn1):
    half = SSM_W // 2
    a = _rms(ya, an, ATTN_W)
    g0 = _rms(ys0 * _silu(z0), sn0, half)
    g1 = _rms(ys1 * _silu(z1), sn1, half)
    return jnp.concatenate([a, g0, g1], axis=1)


def _out_stage_args(ya_ref, ys_ref, z_ref, an_ref, sn_ref):
    half = SSM_W // 2
    return (ya_ref[...], ys_ref[:, :half], ys_ref[:, half:], z_ref[:, :half], z_ref[:, half:],
            an_ref[...], sn_ref[:, :half], sn_ref[:, half:])


def _out_proj_fwd(x, ya, ys, z, an, sn, gate1, w_o, tm, xchg):
    S = x.shape[0]

    def body(x_ref, ya_ref, ys_ref, z_ref, an_ref, sn_ref, g_ref, w_ref, x1_ref):
        u = _out_stage(*_out_stage_args(ya_ref, ys_ref, z_ref, an_ref, sn_ref))
        x1_ref[...] = x_ref[...] + g_ref[...] * _mm(u, w_ref[...])

    half = pl.BlockSpec((tm, ATTN_W), _row)
    hvec = pl.BlockSpec((1, ATTN_W), _fixed)
    (x1,), x_out = _hosted_call(
        body, "out_proj_fwd", S // tm,
        in_specs=[pl.BlockSpec((tm, D_MODEL), _row), half, half, half, hvec, hvec,
                  pl.BlockSpec((1, D_MODEL), _fixed), pl.BlockSpec((D_MODEL, D_MODEL), _fixed)],
        out_specs=[pl.BlockSpec((tm, D_MODEL), _row)],
        out_shape=[jax.ShapeDtypeStruct((S, D_MODEL), F32)],
        scratch_shapes=[], args=(x, ya, ys, z, an, sn, gate1, w_o), xchg=xchg, cparams=_cparams(VMEM_BIG),
    )
    return x1, x_out


def _out_proj_bwd(dx1, ya, ys, z, an, sn, gate1, w_o, tm, xchg):
    S = dx1.shape[0]
    n_steps = S // tm

    def body(dx1_ref, ya_ref, ys_ref, z_ref, an_ref, sn_ref, g_ref, w_ref,
             dya_ref, dys_ref, dz_ref, gw_ref, acc_ref, gw_acc):
        i = pl.program_id(0)

        @pl.when(i == 0)
        def _():
            acc_ref[...] = jnp.zeros_like(acc_ref)
            gw_acc[...] = jnp.zeros_like(gw_acc)

        u, vjp = jax.vjp(_out_stage, *_out_stage_args(ya_ref, ys_ref, z_ref, an_ref, sn_ref))
        dx1 = dx1_ref[...]
        ub = u.astype(MXU_DTYPE)
        mix = _mm(ub, w_ref[...])
        dmix = dx1 * g_ref[...]
        dmixb = dmix.astype(MXU_DTYPE)
        du = _mm_nt(dmixb, w_ref[...])
        gw_acc[...] += _mm_tn(ub, dmixb)
        dya, dys0, dys1, dz0, dz1, dan, dsn0, dsn1 = vjp(du)
        dya_ref[...] = dya
        dys_ref[...] = jnp.concatenate([dys0, dys1], axis=1)
        dz_ref[...] = jnp.concatenate([dz0, dz1], axis=1).astype(dz_ref.dtype)
        acc_ref[0:1, :] += jnp.sum(dx1 * mix, axis=0, keepdims=True)
        acc_ref[1:2, :] += jnp.concatenate([dan, dsn0, dsn1], axis=1)

        @pl.when(i == n_steps - 1)
        def _():
            gw_ref[...] = gw_acc[...].astype(gw_ref.dtype)

    half = pl.BlockSpec((tm, ATTN_W), _row)
    hvec = pl.BlockSpec((1, ATTN_W), _fixed)
    full = pl.BlockSpec((tm, D_MODEL), _row)
    return _hosted_call(
        body, "out_proj_bwd", n_steps,
        in_specs=[full, half, half, half, hvec, hvec,
                  pl.BlockSpec((1, D_MODEL), _fixed), pl.BlockSpec((D_MODEL, D_MODEL), _fixed)],
        out_specs=[half, half, half, pl.BlockSpec((D_MODEL, D_MODEL), _fixed), pl.BlockSpec((8, D_MODEL), _fixed)],
        out_shape=[jax.ShapeDtypeStruct((S, ATTN_W), F32)] * 2 + [jax.ShapeDtypeStruct((S, ATTN_W), MXU_DTYPE),
                   jax.ShapeDtypeStruct((D_MODEL, D_MODEL), WIRE_DTYPE), jax.ShapeDtypeStruct((8, D_MODEL), F32)],
        scratch_shapes=[pltpu.VMEM((D_MODEL, D_MODEL), F32)],
        args=(dx1, ya, ys, z, an, sn, gate1, w_o), xchg=xchg, cparams=_cparams(VMEM_BIG),
    )


def _loss_rows(x2, fn, tgt):
    y = _rms(x2, fn, D_MODEL)
    per_row = jnp.sum(jnp.square(y - tgt), axis=1, keepdims=True)
    return jnp.sum(per_row, axis=0, keepdims=True) * (0.5 / D_MODEL)


def _mlp_loss(x1, tgt, norm2, scale2, shift2, gate2, fnorm, w_gu, w_d, tm):
    S = x1.shape[0]
    n_pieces = len(w_gu) + len(w_d)

    def body(*refs):
        x1_ref, t_ref, n_ref, sc_ref, sh_ref, g_ref, fn_ref = refs[:7]
        piece_refs = refs[7:7 + n_pieces]
        dx1_ref, h_ref, dgu_ref, act_ref, dmlp_ref, acc_ref, wgu, wd, wsem = refs[7 + n_pieces:]

        @pl.when(pl.program_id(0) == 0)
        def _():
            acc_ref[...] = jnp.zeros_like(acc_ref)
            copies = []
            for dst, pieces in ((wgu, piece_refs[:len(w_gu)]), (wd, piece_refs[len(w_gu):])):
                shard = sum(p.shape[1] for p in pieces)
                off = 0
                for p in pieces:
                    for j in range(N_DEV):
                        copies.append(pltpu.make_async_copy(p.at[j], dst.at[pl.ds(j * shard + off, p.shape[1])],
                                                            wsem.at[len(copies)]))
                    off += p.shape[1]
            for cp in copies:
                cp.start()
            for cp in copies:
                cp.wait()

        x1 = x1_ref[...]
        gate2 = g_ref[...]
        h, vjp_h = jax.vjp(_modnorm, x1, n_ref[...], sc_ref[...], sh_ref[...])
        hb = h.astype(MXU_DTYPE)
        gu = _mm_nt(hb, wgu[...])
        g, u = gu[:, :D_FF], gu[:, D_FF:]
        sg = jax.nn.sigmoid(g)
        silu_g = g * sg
        act = (silu_g * u).astype(MXU_DTYPE)
        mlp = _mm(act, wd[...])
        x2 = x1 + gate2 * mlp
        loss, vjp_loss = jax.vjp(_loss_rows, x2, fn_ref[...], t_ref[...])
        dx2, dfn, _ = vjp_loss(jnp.ones((1, 1), F32))
        dmlp = (dx2 * gate2).astype(MXU_DTYPE)
        dact = _mm_nt(dmlp, wd[...])
        dg = dact * u * (sg * (1.0 + g * (1.0 - sg)))
        du = dact * silu_g
        dgu = jnp.concatenate([dg, du], axis=1).astype(MXU_DTYPE)
        dh = _mm(dgu, wgu[...])
        dx, dn, dsc, dsh = vjp_h(dh)
        dx1_ref[...] = dx2 + dx
        h_ref[...] = hb
        dgu_ref[...] = dgu
        act_ref[...] = act
        dmlp_ref[...] = dmlp
        acc_ref[0:1, :] += dn
        acc_ref[1:2, :] += dsc
        acc_ref[2:3, :] += dsh
        acc_ref[3:4, :] += jnp.sum(dx2 * mlp, axis=0, keepdims=True)
        acc_ref[4:5, :] += dfn
        acc_ref[5:6, :] += jnp.broadcast_to(loss, (1, D_MODEL))

    full = pl.BlockSpec((tm, D_MODEL), _row)
    vec = pl.BlockSpec((1, D_MODEL), _fixed)
    anyspec = pl.BlockSpec(memory_space=pl.ANY)
    return pl.pallas_call(
        body, name="mlp_loss", grid=(S // tm,),
        in_specs=[full, full, vec, vec, vec, vec, vec] + [anyspec] * n_pieces,
        out_specs=[full, full, pl.BlockSpec((tm, 2 * D_FF), _row), pl.BlockSpec((tm, D_FF), _row), full,
                   pl.BlockSpec((8, D_MODEL), _fixed)],
        out_shape=[jax.ShapeDtypeStruct((S, D_MODEL), F32), jax.ShapeDtypeStruct((S, D_MODEL), MXU_DTYPE),
                   jax.ShapeDtypeStruct((S, 2 * D_FF), MXU_DTYPE), jax.ShapeDtypeStruct((S, D_FF), MXU_DTYPE),
                   jax.ShapeDtypeStruct((S, D_MODEL), MXU_DTYPE), jax.ShapeDtypeStruct((8, D_MODEL), F32)],
        scratch_shapes=[pltpu.VMEM((2 * D_FF, D_MODEL), MXU_DTYPE), pltpu.VMEM((D_FF, D_MODEL), MXU_DTYPE),
                        pltpu.SemaphoreType.DMA((N_DEV * n_pieces,))],
        compiler_params=_cparams(VMEM_BIG),
    )(x1, tgt, norm2, scale2, shift2, gate2, fnorm, *w_gu, *w_d)


def _wgrad(a, g, tk, ts, name, xchg=None, g_cols=None):
    pieces = list(a) if isinstance(a, (list, tuple)) else [a]
    S = pieces[0].shape[0]
    K = sum(p.shape[1] for p in pieces)
    assert len(pieces) == 1 or tk == K
    N, col = (g.shape[1], 0) if g_cols is None else g_cols
    ns = S // ts
    n_a = len(pieces)

    def body(*refs):
        a_refs, (g_ref, o_ref, acc_ref) = refs[:n_a], refs[n_a:]
        s = pl.program_id(1)

        @pl.when(s == 0)
        def _():
            acc_ref[...] = jnp.zeros_like(acc_ref)

        a_blk = a_refs[0][...] if n_a == 1 else jnp.concatenate([r[...] for r in a_refs], axis=1)
        acc_ref[...] += _mm_tn(a_blk, g_ref[...])

        @pl.when(s == ns - 1)
        def _():
            o_ref[...] = acc_ref[...].astype(o_ref.dtype)

    if n_a == 1:
        in_specs = [pl.BlockSpec((ts, tk), lambda j, s: (s, j))]
    else:
        in_specs = [pl.BlockSpec((ts, p.shape[1]), lambda j, s: (s, 0)) for p in pieces]
    in_specs.append(pl.BlockSpec((ts, N), lambda j, s: (s, col)))
    out_spec = pl.BlockSpec((tk, N), lambda j, s: (j, 0))
    out_shape = jax.ShapeDtypeStruct((K, N), WIRE_DTYPE)
    scratch = [pltpu.VMEM((tk, N), F32)]
    args = (*pieces, g)
    if xchg is None:
        return pl.pallas_call(body, name=name, grid=(K // tk, ns), in_specs=in_specs, out_specs=out_spec,
                              out_shape=out_shape, scratch_shapes=scratch, compiler_params=_cparams(VMEM_BIG))(*args)
    (out,), x_out = _hosted_call(body, name, (K // tk, ns), in_specs, [out_spec], [out_shape], scratch, args, xchg,
                                 _cparams(VMEM_BIG))
    return out, x_out


SSD_CHUNKS_PER_STEP = 4
SSD_BWD_CHUNKS_PER_STEP = 4
ATTN_BLOCKS_PER_STEP = 4
MASKED = -1e30
QK_SCALE = HALF ** -0.5


def _attn_bias(buckets, rel_bias):
    def body(bk_ref, relb_ref, out_ref):
        bk = bk_ref[...]
        i = lax.broadcasted_iota(jnp.int32, (BLK, 2 * BLK), 0)
        j = lax.broadcasted_iota(jnp.int32, (BLK, 2 * BLK), 1)
        window = (j > i) & (j <= i + BLK)
        for h in range(N_HEADS):
            acc = jnp.zeros((BLK, 2 * BLK), F32)
            for b in range(N_BUCKETS):
                acc = jnp.where(bk == b, relb_ref[b, h], acc)
            out_ref[0, h] = jnp.where(window, acc, MASKED)
            out_ref[1, h] = jnp.where(window & (j >= BLK), acc, MASKED)

    return pl.pallas_call(
        body, name="attn_bias", out_shape=jax.ShapeDtypeStruct((2, N_HEADS, BLK, 2 * BLK), F32),
        in_specs=[pl.BlockSpec(memory_space=pltpu.VMEM), pl.BlockSpec(memory_space=pltpu.SMEM)],
    )(buckets, rel_bias)


def _attn_fwd(qkv, bias, sinks, xchg):
    S = qkv.shape[0]
    nb = S // BLK

    nq = ATTN_BLOCKS_PER_STEP if nb % ATTN_BLOCKS_PER_STEP == 0 else 1
    rows = nq * BLK

    def body(q_ref, kvp_ref, kvc_ref, bias_ref, sinks_ref, y_ref):
        i = pl.program_id(0)
        q = q_ref[...].astype(F32) * QK_SCALE
        kv = jnp.concatenate([kvp_ref[...], kvc_ref[...]], axis=0).astype(F32)
        k_lo, k_hi = _split_pair(kv[:, :LANE])
        v_lo, v_hi = _split_pair(kv[:, LANE:])
        bands = [[t[b * BLK:(b + 2) * BLK].astype(MXU_DTYPE) for t in (k_lo, k_hi, v_lo, v_hi)] for b in range(nq)]
        q_heads = [_split_heads(q[b * BLK:(b + 1) * BLK], 4) for b in range(nq)]
        first = [jnp.where(i == 0, 1, 0) if b == 0 else 0 for b in range(nq)]
        items = [(b, h) for b in range(nq) for h in range(N_HEADS)]
        s = [_mm_nt(q_heads[b][h].astype(MXU_DTYPE), bands[b][h // 4]) + bias_ref[first[b], h] for b, h in items]
        m = [jnp.maximum(jnp.max(s[n], axis=-1, keepdims=True), sinks_ref[h]) for n, (b, h) in enumerate(items)]
        p = [jnp.exp(s[n] - m[n]) for n in range(len(items))]
        rinv = [1.0 / (jnp.sum(p[n], axis=-1, keepdims=True) + jnp.exp(sinks_ref[h] - m[n]))
                for n, (b, h) in enumerate(items)]
        out = [_mm(p[n], bands[b][2 + h // 4]) * rinv[n] for n, (b, h) in enumerate(items)]
        y_ref[...] = jnp.concatenate([_join_heads(out[b * N_HEADS:(b + 1) * N_HEADS]) for b in range(nq)], axis=0)

    smem = pl.BlockSpec(memory_space=pltpu.SMEM)
    return _hosted_call(
        body, "attn_fwd", nb // nq,
        in_specs=[pl.BlockSpec((rows, ATTN_W), _row),
                  pl.BlockSpec((BLK, 2 * KV_W), lambda i: (jnp.maximum(i * nq - 1, 0), 2)),
                  pl.BlockSpec((rows, 2 * KV_W), lambda i: (i, 2)),
                  pl.BlockSpec((2, N_HEADS, BLK, 2 * BLK), lambda i: (0, 0, 0, 0)), smem],
        out_specs=[pl.BlockSpec((rows, ATTN_W), _row)],
        out_shape=[jax.ShapeDtypeStruct((S, ATTN_W), F32)],
        scratch_shapes=[],
        args=(qkv, qkv, qkv, bias, sinks), xchg=xchg, cparams=_cparams(),
    )


def _attn_bwd(qkv, y, dy, bias, sinks, xchg):
    S = qkv.shape[0]
    nb = S // BLK
    nq = ATTN_BLOCKS_PER_STEP if nb % ATTN_BLOCKS_PER_STEP == 0 else 1
    rows, n_steps = nq * BLK, nb // nq

    def body(q_ref, kvp_ref, kvc_ref, y_ref, dy_ref, bias_ref, sinks_ref, dq_ref, dkv_ref, dbias_ref, dsk_ref, carry_ref):
        i = pl.program_id(0)

        @pl.when(i == 0)
        def _():
            dbias_ref[...] = jnp.zeros_like(dbias_ref)
            dsk_ref[...] = jnp.zeros_like(dsk_ref)
            carry_ref[...] = jnp.zeros_like(carry_ref)

        q = q_ref[...].astype(F32) * QK_SCALE
        kv = jnp.concatenate([kvp_ref[...], kvc_ref[...]], axis=0).astype(F32)
        k_lo, k_hi = _split_pair(kv[:, :LANE])
        v_lo, v_hi = _split_pair(kv[:, LANE:])
        bands = [[t[b * BLK:(b + 2) * BLK].astype(MXU_DTYPE) for t in (k_lo, k_hi, v_lo, v_hi)] for b in range(nq)]
        rows_of = lambda ref, b: ref[b * BLK:(b + 1) * BLK, :]
        first = [jnp.where(i == n_steps - 1, 1, 0) if b == 0 else 0 for b in range(nq)]
        items = [(b, h) for b in range(nq) for h in range(N_HEADS)]
        at = lambda b, h: b * N_HEADS + h
        q_heads = [hd for b in range(nq) for hd in _split_heads(q[b * BLK:(b + 1) * BLK], 4)]
        y_heads = [hd for b in range(nq) for hd in _split_heads(rows_of(y_ref, b), 4)]
        dy_heads = [hd for b in range(nq) for hd in _split_heads(rows_of(dy_ref, b), 4)]
        qs = [q_heads[n].astype(MXU_DTYPE) for n in range(len(items))]
        s = [_mm_nt(qs[at(b, h)], bands[b][h // 4]) + bias_ref[first[b], h] for b, h in items]
        m = [jnp.maximum(jnp.max(s[at(b, h)], axis=-1, keepdims=True), sinks_ref[h]) for b, h in items]
        p = [jnp.exp(s[n] - m[n]) for n in range(len(items))]
        esink = [jnp.exp(sinks_ref[h] - m[at(b, h)]) for b, h in items]
        rinv = [1.0 / (jnp.sum(p[n], axis=-1, keepdims=True) + esink[n]) for n in range(len(items))]
        t = [dy_heads[n] * rinv[n] for n in range(len(items))]
        delta = [jnp.sum(t[n] * y_heads[n], axis=-1, keepdims=True) for n in range(len(items))]
        tb = [t[n].astype(MXU_DTYPE) for n in range(len(items))]
        dp = [_mm_nt(tb[at(b, h)], bands[b][2 + h // 4]) for b, h in items]
        ds = [p[n] * (dp[n] - delta[n]) for n in range(len(items))]
        for h in range(N_HEADS):
            ds_h, dsk_h = ds[at(0, h)], esink[at(0, h)] * delta[at(0, h)]
            for b in range(1, nq):
                ds_h = ds_h + ds[at(b, h)]
                dsk_h = dsk_h + esink[at(b, h)] * delta[at(b, h)]
            dbias_ref[h] += ds_h
            dsk_ref[h] -= dsk_h
        dsb = [ds[n].astype(MXU_DTYPE) for n in range(len(items))]
        pb = [p[n].astype(MXU_DTYPE) for n in range(len(items))]
        dq_heads = [_mm(dsb[at(b, h)], bands[b][h // 4]) * QK_SCALE for b, h in items]
        grp = lambda lst, b, g: jnp.concatenate(lst[at(b, 4 * g):at(b, 4 * g) + 4], axis=0)
        dk_pads = [[_mm_tn(grp(dsb, b, g), grp(qs, b, g)) for g in range(2)] for b in range(nq)]
        dv_pads = [[_mm_tn(grp(pb, b, g), grp(tb, b, g)) for g in range(2)] for b in range(nq)]
        dq_ref[...] = jnp.concatenate([_join_heads(dq_heads[b * N_HEADS:(b + 1) * N_HEADS]) for b in range(nq)],
                                      axis=0).astype(dq_ref.dtype)
        part = lambda b, lo: jnp.concatenate(
            [_join_pair(d[b][0][lo:lo + BLK], d[b][1][lo:lo + BLK]) for d in (dk_pads, dv_pads)], axis=1)
        dkv = [part(b, BLK) + (part(b + 1, 0) if b + 1 < nq else carry_ref[...]) for b in range(nq)]
        dkv_ref[...] = jnp.concatenate(dkv, axis=0).astype(dkv_ref.dtype)
        carry_ref[...] = part(0, 0)

    smem = pl.BlockSpec(memory_space=pltpu.SMEM)
    rev = lambda i: (n_steps - 1 - i, 0)
    return _hosted_call(
        body, "attn_bwd", n_steps,
        in_specs=[pl.BlockSpec((rows, ATTN_W), rev),
                  pl.BlockSpec((BLK, 2 * KV_W), lambda i: (jnp.maximum((n_steps - 1 - i) * nq - 1, 0), 2)),
                  pl.BlockSpec((rows, 2 * KV_W), lambda i: (n_steps - 1 - i, 2)),
                  pl.BlockSpec((rows, ATTN_W), rev), pl.BlockSpec((rows, ATTN_W), rev),
                  pl.BlockSpec((2, N_HEADS, BLK, 2 * BLK), lambda i: (0, 0, 0, 0)), smem],
        out_specs=[pl.BlockSpec((rows, ATTN_W), rev), pl.BlockSpec((rows, 2 * KV_W), rev),
                   pl.BlockSpec((N_HEADS, BLK, 2 * BLK), lambda i: (0, 0, 0)),
                   pl.BlockSpec((N_HEADS, BLK, 1), lambda i: (0, 0, 0))],
        out_shape=[jax.ShapeDtypeStruct((S, ATTN_W), MXU_DTYPE), jax.ShapeDtypeStruct((S, 2 * KV_W), MXU_DTYPE),
                   jax.ShapeDtypeStruct((N_HEADS, BLK, 2 * BLK), F32), jax.ShapeDtypeStruct((N_HEADS, BLK, 1), F32)],
        scratch_shapes=[pltpu.VMEM((BLK, 2 * KV_W), F32)],
        args=(qkv, qkv, qkv, y, dy, bias, sinks), xchg=xchg, cparams=_cparams(),
    )


def _attn_finish(dbias, dsk, buckets):
    def body(db_ref, dsk_ref, bk_ref, drel_ref, dsink_ref):
        bk = bk_ref[...]
        r = lax.broadcasted_iota(jnp.int32, (N_BUCKETS, LANE), 0)
        l = lax.broadcasted_iota(jnp.int32, (N_BUCKETS, LANE), 1)
        row = lax.broadcasted_iota(jnp.int32, (N_HEADS, LANE), 0)
        res = jnp.zeros((N_BUCKETS, LANE), F32)
        dsink = jnp.zeros((N_HEADS, LANE), F32)
        for h in range(N_HEADS):
            db = db_ref[h]
            for b in range(N_BUCKETS):
                v = jnp.sum(jnp.sum(jnp.where(bk == b, db, 0.0), axis=1, keepdims=True), axis=0, keepdims=True)
                res = res + jnp.where((r == b) & (l == h), v, 0.0)
            dsink = dsink + jnp.where(row == h, jnp.sum(dsk_ref[h], axis=0, keepdims=True), 0.0)
        drel_ref[...] = res
        dsink_ref[...] = dsink

    return pl.pallas_call(body, name="attn_finish",
                          out_shape=[jax.ShapeDtypeStruct((N_BUCKETS, LANE), F32),
                                     jax.ShapeDtypeStruct((N_HEADS, LANE), F32)])(dbias, dsk, buckets)


def _ssd_consts():
    r = lax.broadcasted_iota(jnp.int32, (BLK, BLK), 0)
    c = lax.broadcasted_iota(jnp.int32, (BLK, BLK), 1)
    causal = c <= r
    upper = (r <= c).astype(F32)
    last = r == BLK - 1
    head = lax.broadcasted_iota(jnp.int32, (N_HEADS, BLK), 0)
    return causal, upper, last, head


def _ssd_chunks(xs, bg, cg, dt_raw_t, prev0, dtb, alog, d_rows, consts):
    causal, upper, last, head = consts
    nq = len(xs)
    items = [(c, h) for c in range(nq) for h in range(N_HEADS)]
    at = lambda c, h: c * N_HEADS + h
    a_neg = -jnp.exp(alog)
    dt_t = [_softplus(dt_raw_t[c] + dtb) for c in range(nq)]
    acs_t = [_mm_hi(dt_t[c] * a_neg, upper) for c in range(nq)]
    cb = [[_mm_nt(cg[c][g], bg[c][g]) for g in range(2)] for c in range(nq)]
    pick = lambda t, h: jnp.sum(jnp.where(head == h, t, 0.0), axis=0, keepdims=True)
    dt_row = [pick(dt_t[c], h) for c, h in items]
    a_row = [pick(acs_t[c], h) for c, h in items]
    a_rb = [jnp.broadcast_to(a_row[n], (BLK, BLK)) for n in range(len(items))]
    a_b = [a_rb[n].T for n in range(len(items))]
    a_last = [jnp.sum(jnp.where(last, a_b[n], 0.0), axis=0, keepdims=True) for n in range(len(items))]
    w = [cb[c][h // 4] * jnp.exp(jnp.where(causal, a_b[at(c, h)] - a_rb[at(c, h)], -1e30)) * dt_row[at(c, h)]
         for c, h in items]
    f_b = [jnp.broadcast_to(dt_row[n] * jnp.exp(a_last[n] - a_row[n]), (BLK, BLK)).T for n in range(len(items))]
    y_in = [_mm(w[at(c, h)], xs[c][h]) for c, h in items]
    st = [_mm_tn(bg[c][h // 4], xs[c][h] * f_b[at(c, h)]) for c, h in items]
    e_b = [jnp.exp(a_b[n]) for n in range(len(items))]
    states = [list(prev0)]
    for c in range(nq):
        states.append([states[c][h] * jnp.exp(a_last[at(c, h)]) + st[at(c, h)] for h in range(N_HEADS)])
    y_off = [_mm(cg[c][h // 4], states[c][h]) * e_b[at(c, h)] for c, h in items]
    ys = [[y_in[at(c, h)] + y_off[at(c, h)] + d_rows[h] * xs[c][h] for h in range(N_HEADS)] for c in range(nq)]
    return ys, states


def _ssd_chunks_bwd(xs, bg, cg, dt_raw_t, prev, dtb, alog, d_rows, dys, dh_last, consts):
    causal, upper, last, head = consts
    nq = len(xs)
    items = [(c, h) for c in range(nq) for h in range(N_HEADS)]
    ni = len(items)
    at = lambda c, h: c * N_HEADS + h
    groups = [(c, g) for c in range(nq) for g in range(2)]
    lane = _lane_iota((BLK, BLK))
    lane_row = _lane_iota((1, BLK))
    a_neg = -jnp.exp(alog)
    pre_dt = [dt_raw_t[c] + dtb for c in range(nq)]
    dt_t = [_softplus(pre_dt[c]) for c in range(nq)]
    acs_t = [_mm_hi(dt_t[c] * a_neg, upper) for c in range(nq)]
    pick = lambda t, h: jnp.sum(jnp.where(head == h, t, 0.0), axis=0, keepdims=True)
    full_sum = lambda t: jnp.sum(jnp.sum(t, axis=1, keepdims=True), axis=0, keepdims=True)
    dt_row = [pick(dt_t[c], h) for c, h in items]
    a_row = [pick(acs_t[c], h) for c, h in items]
    a_rb = [jnp.broadcast_to(a_row[n], (BLK, BLK)) for n in range(ni)]
    a_b = [a_rb[n].T for n in range(ni)]
    a_last = [jnp.sum(jnp.where(last, a_b[n], 0.0), axis=0, keepdims=True) for n in range(ni)]
    lm = [jnp.exp(jnp.where(causal, a_b[n] - a_rb[n], -1e30)) for n in range(ni)]
    cgb = [[cg[c][g].astype(MXU_DTYPE) for g in range(2)] for c in range(nq)]
    bgb = [[bg[c][g].astype(MXU_DTYPE) for g in range(2)] for c in range(nq)]
    cb = [[_mm_nt(cgb[c][g], bgb[c][g]) for g in range(2)] for c in range(nq)]
    u = [cb[c][h // 4] * lm[at(c, h)] for c, h in items]
    w = [(u[n] * dt_row[n]).astype(MXU_DTYPE) for n in range(ni)]
    e_row = [jnp.exp(a_last[n] - a_row[n]) for n in range(ni)]
    f_row = [dt_row[n] * e_row[n] for n in range(ni)]
    f_b = [jnp.broadcast_to(f_row[n], (BLK, BLK)).T for n in range(ni)]
    e_b = [jnp.exp(a_b[n]) for n in range(ni)]
    el = [jnp.exp(a_last[n]) for n in range(ni)]
    xb = [xs[c][h].astype(MXU_DTYPE) for c, h in items]
    dyb = [dys[c][h].astype(MXU_DTYPE) for c, h in items]
    prevb = [prev[c][h].astype(MXU_DTYPE) for c, h in items]
    gmat = [_mm(cgb[c][h // 4], prevb[at(c, h)]) for c, h in items]
    dw = [_mm_nt(dyb[n], xb[n]) for n in range(ni)]
    dg = [dys[c][h] * e_b[at(c, h)] for c, h in items]
    dgb = [dg[n].astype(MXU_DTYPE) for n in range(ni)]
    from_y = [_mm_tn(cgb[c][h // 4], dgb[at(c, h)]) for c, h in items]
    dhs = [None] * ni
    dprev = [None] * ni
    for c in reversed(range(nq)):
        for h in range(N_HEADS):
            dhs[at(c, h)] = dh_last[h] if c == nq - 1 else dprev[at(c + 1, h)]
            dprev[at(c, h)] = from_y[at(c, h)] + dhs[at(c, h)] * el[at(c, h)]
    dstb = [dhs[n].astype(MXU_DTYPE) for n in range(ni)]
    dxf = [_mm(bgb[c][h // 4], dstb[at(c, h)]) for c, h in items]
    xfb = [(xs[c][h] * f_b[at(c, h)]).astype(MXU_DTYPE) for c, h in items]
    dxs = [_mm_tn(w[at(c, h)], dyb[at(c, h)]) + d_rows[h] * dys[c][h] + f_b[at(c, h)] * dxf[at(c, h)]
           for c, h in items]
    dd_item = [jnp.sum(dys[c][h] * xs[c][h], axis=0, keepdims=True) for c, h in items]
    dcg_h = [_mm_nt(dgb[n], prevb[n]) for n in range(ni)]
    dbg_h = [_mm_nt(xfb[n], dstb[n]) for n in range(ni)]
    zt = [dw[n] * u[n] for n in range(ni)]
    dseg = [zt[n] * dt_row[n] for n in range(ni)]
    dcb_h = [dw[n] * lm[n] * dt_row[n] for n in range(ni)]
    four = lambda lst, c, g: lst[at(c, 4 * g)] + lst[at(c, 4 * g + 1)] + lst[at(c, 4 * g + 2)] + lst[at(c, 4 * g + 3)]
    dcb = {(c, g): four(dcb_h, c, g).astype(MXU_DTYPE) for c, g in groups}
    dcg = [[four(dcg_h, c, g) + _mm(dcb[c, g], bgb[c][g]) for g in range(2)] for c in range(nq)]
    dbg = [[four(dbg_h, c, g) + _mm_tn(dcb[c, g], cgb[c][g]) for g in range(2)] for c in range(nq)]
    r1 = [jnp.sum(dg[n] * gmat[n] + dseg[n], axis=1, keepdims=True) for n in range(ni)]
    r2 = [jnp.sum(dxf[at(c, h)] * xs[c][h], axis=1, keepdims=True) for c, h in items]
    tt = [jnp.where(lane < HALF, jnp.broadcast_to(r1[n], (BLK, BLK)), jnp.broadcast_to(r2[n], (BLK, BLK))).T
          for n in range(ni)]
    r1_row = [tt[n][0:1, :] for n in range(ni)]
    r2_row = [tt[n][HALF:HALF + 1, :] for n in range(ni)]
    d_el = [full_sum(dhs[at(c, h)] * prev[c][h]) for c, h in items]
    da_last = [jnp.sum(r2_row[n] * f_row[n], axis=1, keepdims=True) + el[n] * d_el[n] for n in range(ni)]
    da_row = [r1_row[n] - jnp.sum(dseg[n], axis=0, keepdims=True) - r2_row[n] * f_row[n]
              + jnp.where(lane_row == BLK - 1, da_last[n], 0.0) for n in range(ni)]
    ddt_row = [jnp.sum(zt[n], axis=0, keepdims=True) + r2_row[n] * e_row[n] for n in range(ni)]
    draw, dalog = [], jnp.zeros((N_HEADS, BLK), F32)
    for c in range(nq):
        da_t = jnp.zeros((N_HEADS, BLK), F32)
        ddt_t = jnp.zeros((N_HEADS, BLK), F32)
        for h in range(N_HEADS):
            da_t = jnp.where(head == h, da_row[at(c, h)], da_t)
            ddt_t = jnp.where(head == h, ddt_row[at(c, h)], ddt_t)
        d_dta = _mm_hi(da_t, causal.astype(F32))
        dalog = dalog + d_dta * dt_t[c] * a_neg
        draw.append((ddt_t + d_dta * a_neg) * jax.nn.sigmoid(pre_dt[c]))
    ddtb = draw[0]
    for c in range(1, nq):
        ddtb = ddtb + draw[c]
    dd_rows = []
    for h in range(N_HEADS):
        t = dd_item[at(0, h)]
        for c in range(1, nq):
            t = t + dd_item[at(c, h)]
        dd_rows.append(t)
    return ([dxs[c * N_HEADS:(c + 1) * N_HEADS] for c in range(nq)], dbg, dcg, draw,
            [dprev[at(0, h)] for h in range(N_HEADS)], ddtb, dalog, dd_rows)


def _dt_rows(dt_blk):
    return dt_blk.T[:N_HEADS]


def _silu_grad(x):
    s = jax.nn.sigmoid(x)
    return s * (1.0 + x * (1.0 - s))


def _conv_pre(halo, blk, cw_ref, cb_ref):
    ext = jnp.concatenate([halo, blk], axis=0)
    taps = [pltpu.roll(ext, 3 - k, 0)[8:] for k in range(3)] + [blk]
    pre = cb_ref[...] + cw_ref[0:1, :] * taps[0]
    for k in range(1, 4):
        pre = pre + cw_ref[k:k + 1, :] * taps[k]
    return pre, taps


def _ssd_split(pre):
    heads = _split_heads(pre[:, :SSM_W], 4)
    pb = [pre[:, SSM_W + g * D_STATE:SSM_W + (g + 1) * D_STATE] for g in range(2)]
    pc = [pre[:, SSM_W + 2 * D_STATE + g * D_STATE:SSM_W + 2 * D_STATE + (g + 1) * D_STATE] for g in range(2)]
    return heads, pb, pc


def _ssd_fwd(xbc, dt_raw, conv_w, conv_b, dtb_row, alog_row, d_exp, xchg):
    S = xbc.shape[0]
    nc = S // BLK
    nq = SSD_CHUNKS_PER_STEP if nc % SSD_CHUNKS_PER_STEP == 0 else 1
    rows = nq * BLK

    def body(xbc_ref, halo_ref, dt_ref, cw_ref, cb_ref, dtb_ref, alog_ref, d_ref, y_ref, prev_ref, state_ref):
        i = pl.program_id(0)

        @pl.when(i == 0)
        def _():
            state_ref[...] = jnp.zeros_like(state_ref)

        halo = halo_ref[...] * jnp.where(i > 0, 1.0, 0.0)
        pre, _ = _conv_pre(halo, xbc_ref[...], cw_ref, cb_ref)
        xc = _silu(pre)
        split = [_ssd_split(xc[c * BLK:(c + 1) * BLK]) for c in range(nq)]
        dt_t = [_dt_rows(dt_ref[c * BLK:(c + 1) * BLK, :]) for c in range(nq)]
        prev0 = [state_ref[h] for h in range(N_HEADS)]
        d_rows = [d_ref[h:h + 1, :] for h in range(N_HEADS)]
        ys, states = _ssd_chunks([s[0] for s in split], [s[1] for s in split], [s[2] for s in split], dt_t, prev0,
                                 dtb_ref[...], alog_ref[...], d_rows, _ssd_consts())
        for h in range(N_HEADS):
            for c in range(nq):
                prev_ref[c, h] = states[c][h]
            state_ref[h] = states[nq][h]
        y_ref[...] = jnp.concatenate([_join_heads(ys[c]) for c in range(nq)], axis=0)

    vec = pl.BlockSpec((N_HEADS, LANE), _fixed)
    return _hosted_call(
        body, "ssd_fwd", nc // nq,
        in_specs=[pl.BlockSpec((rows, XBC_W), _row),
                  pl.BlockSpec((8, XBC_W), lambda i: (jnp.maximum(i * (rows // 8) - 1, 0), 0)),
                  pl.BlockSpec((rows, LANE), _row),
                  pl.BlockSpec((4, XBC_W), _fixed), pl.BlockSpec((1, XBC_W), _fixed), vec, vec,
                  pl.BlockSpec((N_HEADS, LANE), _fixed)],
        out_specs=[pl.BlockSpec((rows, SSM_W), _row),
                   pl.BlockSpec((nq, N_HEADS, D_STATE, LANE), lambda i: (i, 0, 0, 0))],
        out_shape=[jax.ShapeDtypeStruct((S, SSM_W), F32), jax.ShapeDtypeStruct((nc, N_HEADS, D_STATE, LANE), F32)],
        scratch_shapes=[pltpu.VMEM((N_HEADS, D_STATE, LANE), F32)],
        args=(xbc, xbc, dt_raw, conv_w, conv_b, dtb_row, alog_row, d_exp), xchg=xchg, cparams=_cparams(),
    )


def _ssd_bwd(xbc, dt_raw, prev_states, dy, conv_w, conv_b, dtb_row, alog_row, d_exp, xchg):
    S = xbc.shape[0]
    nc = S // BLK
    nq = SSD_BWD_CHUNKS_PER_STEP if nc % SSD_BWD_CHUNKS_PER_STEP == 0 else 1
    rows, n_steps = nq * BLK, nc // nq

    def body(xbc_ref, halo_ref, dt_ref, prev_ref, dy_ref, cw_ref, cb_ref, dtb_ref, alog_ref, d_ref,
             dxbc_ref, ddt_ref, dcw_ref, dvec_ref, dd_ref, gstate_ref, ghalo_ref):
        i = pl.program_id(0)

        @pl.when(i == 0)
        def _():
            gstate_ref[...] = jnp.zeros_like(gstate_ref)
            ghalo_ref[...] = jnp.zeros_like(ghalo_ref)
            dcw_ref[...] = jnp.zeros_like(dcw_ref)
            dvec_ref[...] = jnp.zeros_like(dvec_ref)
            dd_ref[...] = jnp.zeros_like(dd_ref)

        halo = halo_ref[...] * jnp.where(i < n_steps - 1, 1.0, 0.0)
        pre, taps = _conv_pre(halo, xbc_ref[...], cw_ref, cb_ref)
        xc = _silu(pre)
        split = [_ssd_split(xc[c * BLK:(c + 1) * BLK]) for c in range(nq)]
        dt_t = [_dt_rows(dt_ref[c * BLK:(c + 1) * BLK, :]) for c in range(nq)]
        prev = [[prev_ref[c, h] for h in range(N_HEADS)] for c in range(nq)]
        d_rows = [d_ref[h:h + 1, :] for h in range(N_HEADS)]
        dys = [_split_heads(dy_ref[c * BLK:(c + 1) * BLK, :], 4) for c in range(nq)]
        dh_last = [gstate_ref[h] for h in range(N_HEADS)]
        dheads, dpb, dpc, ddt_t, dprev0, ddtb, dalog, dd_rows = _ssd_chunks_bwd(
            [s[0] for s in split], [s[1] for s in split], [s[2] for s in split], dt_t, prev, dtb_ref[...],
            alog_ref[...], d_rows, dys, dh_last, _ssd_consts())
        for h in range(N_HEADS):
            gstate_ref[h] = dprev0[h]
            dd_ref[h:h + 1, :] += dd_rows[h]
        pad = jnp.zeros((BLK - N_HEADS, BLK), F32)
        ddt_ref[...] = jnp.concatenate([jnp.concatenate([ddt_t[c], pad], axis=0).T for c in range(nq)],
                                       axis=0).astype(ddt_ref.dtype)
        dvec_ref[0:N_HEADS, :] += ddtb
        dvec_ref[N_HEADS:, :] += dalog
        dxc = jnp.concatenate([jnp.concatenate([_join_heads(dheads[c])] + list(dpb[c]) + list(dpc[c]), axis=1)
                               for c in range(nq)], axis=0)
        dpre = dxc * _silu_grad(pre)
        zeros8 = jnp.zeros((8, XBC_W), F32)
        dpe = jnp.concatenate([zeros8, dpre, zeros8], axis=0)
        n_ext = 16 + rows
        dext = cw_ref[3:4, :] * dpe[:8 + rows]
        dcw_ref[3:4, :] += jnp.sum(dpre * taps[3], axis=0, keepdims=True)
        for k in range(3):
            dext = dext + cw_ref[k:k + 1, :] * pltpu.roll(dpe, n_ext - (3 - k), 0)[:8 + rows]
            dcw_ref[k:k + 1, :] += jnp.sum(dpre * taps[k], axis=0, keepdims=True)
        dcw_ref[4:5, :] += jnp.sum(dpre, axis=0, keepdims=True)
        dxbc_ref[...] = jnp.concatenate([dext[8:rows], dext[rows:] + ghalo_ref[...]], axis=0).astype(dxbc_ref.dtype)
        ghalo_ref[...] = dext[:8, :]

    vec = pl.BlockSpec((N_HEADS, LANE), _fixed)
    rev = lambda i: (n_steps - 1 - i, 0)
    return _hosted_call(
        body, "ssd_bwd", n_steps,
        in_specs=[pl.BlockSpec((rows, XBC_W), rev),
                  pl.BlockSpec((8, XBC_W), lambda i: (jnp.maximum((n_steps - 1 - i) * (rows // 8) - 1, 0), 0)),
                  pl.BlockSpec((rows, LANE), rev),
                  pl.BlockSpec((nq, N_HEADS, D_STATE, LANE), lambda i: (n_steps - 1 - i, 0, 0, 0)),
                  pl.BlockSpec((rows, SSM_W), rev),
                  pl.BlockSpec((4, XBC_W), _fixed), pl.BlockSpec((1, XBC_W), _fixed), vec, vec,
                  pl.BlockSpec((N_HEADS, LANE), _fixed)],
        out_specs=[pl.BlockSpec((rows, XBC_W), rev), pl.BlockSpec((rows, LANE), rev),
                   pl.BlockSpec((8, XBC_W), _fixed), pl.BlockSpec((2 * N_HEADS, LANE), _fixed),
                   pl.BlockSpec((N_HEADS, LANE), _fixed)],
        out_shape=[jax.ShapeDtypeStruct((S, XBC_W), MXU_DTYPE), jax.ShapeDtypeStruct((S, LANE), MXU_DTYPE),
                   jax.ShapeDtypeStruct((8, XBC_W), F32), jax.ShapeDtypeStruct((2 * N_HEADS, LANE), F32),
                   jax.ShapeDtypeStruct((N_HEADS, LANE), F32)],
        scratch_shapes=[pltpu.VMEM((N_HEADS, D_STATE, LANE), F32), pltpu.VMEM((8, XBC_W), F32)],
        args=(xbc, xbc, dt_raw, prev_states, dy, conv_w, conv_b, dtb_row, alog_row, d_exp), xchg=xchg,
        cparams=_cparams(VMEM_BIG),
    )


def _adamw_math(w, g, m, v):
    m = ADAM_B1 * m + (1.0 - ADAM_B1) * g
    v = ADAM_B2 * v + (1.0 - ADAM_B2) * jnp.square(g)
    m_hat = m / (1.0 - ADAM_B1 ** ADAM_STEP)
    v_hat = v / (1.0 - ADAM_B2 ** ADAM_STEP)
    delta = -ADAM_LR * (m_hat / (jnp.sqrt(v_hat) + ADAM_EPS) + ADAM_WD * w)
    return delta, m, v


def _reduce_adamw(parts, w, m, v, name):
    R, C = w.shape

    def body(p_ref, w_ref, m_ref, v_ref, g_ref, d_ref, nm_ref, nv_ref):
        g = p_ref[0].astype(F32)
        for i in range(1, N_DEV):
            g = g + p_ref[i].astype(F32)
        d, nm, nv = _adamw_math(w_ref[...], g, m_ref[...], v_ref[...])
        g_ref[...] = g
        d_ref[...] = d
        nm_ref[...] = nm
        nv_ref[...] = nv

    if R % 16 == 0:
        tr = max(t for t in range(16, 257, 16) if R % t == 0)
        n, blk, pblk = R // tr, pl.BlockSpec((tr, C), _row), pl.BlockSpec((N_DEV, tr, C), lambda i: (0, i, 0))
    else:
        tl = 256
        n, blk, pblk = C // tl, pl.BlockSpec((R, tl), lambda i: (0, i)), pl.BlockSpec((N_DEV, R, tl),
                                                                                      lambda i: (0, 0, i))
    return pl.pallas_call(
        body, name=name, grid=(n,), in_specs=[pblk, blk, blk, blk],
        out_specs=[blk] * 4, out_shape=[jax.ShapeDtypeStruct((R, C), F32)] * 4,
    )(parts, w, m, v)


def _reduce_adamw_hosting(parts_list, wmv_list, name, xchg):
    n_arr = len(parts_list)
    C = wmv_list[0][0].shape[1]
    tl = 256

    def body(*refs):
        p_refs, wmv_refs, o_refs = refs[:n_arr], refs[n_arr:4 * n_arr], refs[4 * n_arr:]
        for k in range(n_arr):
            g = p_refs[k][0].astype(F32)
            for i in range(1, N_DEV):
                g = g + p_refs[k][i].astype(F32)
            w_ref, m_ref, v_ref = wmv_refs[3 * k:3 * k + 3]
            d, nm, nv = _adamw_math(w_ref[...], g, m_ref[...], v_ref[...])
            for o, val in zip(o_refs[4 * k:4 * k + 4], (g, d, nm, nv)):
                o[...] = val

    in_specs = [pl.BlockSpec((N_DEV, w.shape[0], tl), lambda i: (0, 0, i)) for w, _, _ in wmv_list]
    in_specs += [pl.BlockSpec((w.shape[0], tl), lambda i: (0, i)) for w, _, _ in wmv_list for _ in range(3)]
    out_specs = [pl.BlockSpec((w.shape[0], tl), lambda i: (0, i)) for w, _, _ in wmv_list for _ in range(4)]
    out_shape = [jax.ShapeDtypeStruct(w.shape, F32) for w, _, _ in wmv_list for _ in range(4)]
    args = list(parts_list) + [a for wmv in wmv_list for a in wmv]
    outs, x_out = _hosted_call(body, name, C // tl, in_specs, out_specs, out_shape, [], args, xchg,
                               _cparams(VMEM_BIG))
    return [outs[4 * k:4 * k + 4] for k in range(n_arr)], x_out


_SMALL_NAMES = ("ada_b", "norm1", "conv_w", "conv_b", "dt_bias", "A_log", "D_skip", "sinks", "attn_out_norm",
                "ssm_out_norm", "norm2", "rel_bias", "final_norm")
N_MOD = 6 * D_MODEL


def _mod_row(a0, a1, a2):
    return jnp.concatenate([a0[2:3], a0[1:2], a1[0:1], a2[2:3], a2[1:2], a2[3:4]], axis=1)


def _small_update(gathered, params):
    n_g = len(gathered)
    flat = [a for name in _SMALL_NAMES for a in params[name]]

    def body(*refs):
        a0_ref, a1_ref, a2_ref, cw_ref, dv_ref, dd_ref, ds_ref, dr_ref, c_ref = refs[:n_g]
        wmv = refs[n_g:n_g + len(flat)]
        outs = refs[n_g + len(flat):]

        def total(ref):
            t = ref[0]
            for i in range(1, N_DEV):
                t = t + ref[i]
            return t

        t0, t1, t2, tcw, tdv, tdd, tds, tdr = [total(r) for r in (a0_ref, a1_ref, a2_ref, cw_ref, dv_ref, dd_ref,
                                                                   ds_ref, dr_ref)]
        r8 = lax.broadcasted_iota(jnp.int32, (N_HEADS, LANE), 0)
        l8 = lax.broadcasted_iota(jnp.int32, (N_HEADS, LANE), 1)

        def diag_row(t):
            return jnp.sum(jnp.where(r8 == l8, t, 0.0), axis=0, keepdims=True)[:, :N_HEADS]

        def lane_sums(t):
            return diag_row(jnp.broadcast_to(jnp.sum(t, axis=1, keepdims=True), (N_HEADS, LANE)))

        me = _lin(_my_pos())
        n_cw = XBC_W // N_DEV
        cw_mine = jnp.zeros((4, n_cw), F32)
        for j in range(N_DEV):
            cw_mine = cw_mine + tcw[0:4, j * n_cw:(j + 1) * n_cw] * jnp.where(me == j, 1.0, 0.0)
        grads = {
            "ada_b": _mod_row(t0, t1, t2), "norm1": t0[0:1], "conv_w": cw_mine, "conv_b": tcw[4:5],
            "dt_bias": lane_sums(tdv[:N_HEADS]), "A_log": lane_sums(tdv[N_HEADS:]), "D_skip": lane_sums(tdd),
            "sinks": diag_row(tds), "attn_out_norm": t1[1:2, :ATTN_W], "ssm_out_norm": t1[1:2, ATTN_W:],
            "norm2": t2[0:1], "rel_bias": tdr[:, :N_HEADS], "final_norm": t2[4:5],
        }
        for k, name in enumerate(_SMALL_NAMES):
            w_ref, m_ref, v_ref = wmv[3 * k:3 * k + 3]
            g = grads[name]
            d, nm, nv = _adamw_math(w_ref[...], g, m_ref[...], v_ref[...])
            for o, val in zip(outs[4 * k:4 * k + 4], (g, d, nm, nv)):
                o[...] = val
        loss_ref, call_ref, dmod_ref = outs[4 * len(_SMALL_NAMES):]
        loss_ref[...] = t2[5:6, 0:1]
        call_ref[...] = jnp.concatenate([c_ref[i] for i in range(N_DEV)], axis=0)
        dmod_ref[...] = jnp.concatenate([_mod_row(a0_ref[i], a1_ref[i], a2_ref[i]) for i in range(N_DEV)], axis=0)

    out_shape = [jax.ShapeDtypeStruct(params[name][0].shape, F32) for name in _SMALL_NAMES for _ in range(4)]
    out_shape += [jax.ShapeDtypeStruct((1, 1), F32), jax.ShapeDtypeStruct((N_DEV, D_MODEL), F32),
                  jax.ShapeDtypeStruct((N_DEV, N_MOD), F32)]
    res = pl.pallas_call(body, name="small_update", out_shape=out_shape)(*gathered, *flat)
    upd = {name: res[4 * k:4 * k + 4] for k, name in enumerate(_SMALL_NAMES)}
    loss, c_all, dmod_all = res[4 * len(_SMALL_NAMES):]
    return upd, loss, c_all, dmod_all


def _ada_w_update(c_all, dmod_all, w, m, v):
    chunk = w.shape[1]

    def body(c_ref, dm_ref, w_ref, m_ref, v_ref, g_ref, d_ref, nm_ref, nv_ref):
        me = _lin(_my_pos())
        dm = jnp.zeros((N_DEV, chunk), F32)
        for j in range(N_DEV):
            dm = dm + dm_ref[:, j * chunk:(j + 1) * chunk] * jnp.where(me == j, 1.0, 0.0)
        g = lax.dot_general(_silu(c_ref[...]), dm, (((0,), (0,)), ((), ())), precision=HI,
                            preferred_element_type=F32)
        d, nm, nv = _adamw_math(w_ref[...], g, m_ref[...], v_ref[...])
        g_ref[...] = g
        d_ref[...] = d
        nm_ref[...] = nm
        nv_ref[...] = nv

    tr = 256
    blk = pl.BlockSpec((tr, chunk), _row)
    return pl.pallas_call(
        body, name="ada_w_update", grid=(w.shape[0] // tr,),
        in_specs=[pl.BlockSpec((N_DEV, tr), lambda i: (0, i)), pl.BlockSpec(dmod_all.shape, _fixed), blk, blk, blk],
        out_specs=[blk] * 4, out_shape=[jax.ShapeDtypeStruct(w.shape, F32)] * 4,
    )(c_all, dmod_all, w, m, v)


def _local_step(x, tgt, c, mod, w_in, conv_w, w_o_mine, w_gu_mine, w_d_mine, p):
    S = x.shape[0]
    tm = min(512, S)
    tmm = min(256, S)
    tw = min(2048, S)
    shift1, scale1, gate1, shift2, scale2, gate2 = [mod[i:i + 1] for i in range(6)]
    buckets = jnp.asarray(_t5_bucket_table())
    per_head = lambda a: jnp.broadcast_to(a.reshape(N_HEADS, 1), (N_HEADS, LANE))
    dtb_row, alog_row, d_exp = per_head(p["dt_bias"]), per_head(p["A_log"]), per_head(p["D_skip"])
    sinks = p["sinks"].reshape(N_HEADS)

    d_cut, gu_cut = WD_CUT, WGU_CUTS
    (qkv, z, xbc, dt_raw), (g_d_a,) = _in_proj_fwd(x, p["norm1"], scale1, shift1, w_in, tm,
                                                   ([w_d_mine[:d_cut]], "two-level"))
    bias = _attn_bias(buckets, p["rel_bias"])
    (ya,), (g_gu_a,) = _attn_fwd(qkv, bias, sinks, ([w_gu_mine[:gu_cut[0]]], "two-level"))
    (ys, prev_states), (g_gu_b, g_o) = _ssd_fwd(xbc, dt_raw, conv_w, p["conv_b"], dtb_row, alog_row, d_exp,
                                                ([w_gu_mine[gu_cut[0]:gu_cut[1]], w_o_mine], "two-level"))
    w_o = g_o.reshape(D_MODEL, D_MODEL)
    x1, (g_gu_c, g_d_b) = _out_proj_fwd(x, ya, ys, z, p["attn_out_norm"], p["ssm_out_norm"], gate1, w_o, tm,
                                        ([w_gu_mine[gu_cut[1]:], w_d_mine[d_cut:]], "two-level"))
    dx1, h2, dgu, act, dmlp, acc2 = _mlp_loss(x1, tgt, p["norm2"], scale2, shift2, gate2, p["final_norm"],
                                              (g_gu_a, g_gu_b, g_gu_c), (g_d_a, g_d_b), tmm)
    g_w_gu = _wgrad(dgu, h2, 2 * D_FF // 4, tw, "wgrad_gate_up")
    g_w_d = _wgrad(act, dmlp, D_FF // 2, tw, "wgrad_down")
    (dya, dys, dz, g_w_o, acc1), (r_d,) = _out_proj_bwd(
        dx1, ya, ys, z, p["attn_out_norm"], p["ssm_out_norm"], gate1, w_o, tm,
        ([g_w_d.reshape(N_DEV, D_FF // N_DEV, D_MODEL)], True))
    (dq, dkv, dbias, dsk), (r_o,) = _attn_bwd(qkv, ya, dya, bias, sinks,
                                              ([g_w_o.reshape(N_DEV, D_MODEL // N_DEV, D_MODEL)], True))
    drel, dsink = _attn_finish(dbias, dsk, buckets)
    (dxbc, ddt, dcw, dvec, dd), (r_gu,) = _ssd_bwd(
        xbc, dt_raw, prev_states, dys, conv_w, p["conv_b"], dtb_row, alog_row, d_exp,
        ([g_w_gu.reshape(N_DEV, 2 * D_FF // N_DEV, D_MODEL)], True))
    gx, h1, acc0, g_in_a = _in_proj_bwd(x, dx1, dq, dkv, dz, dxbc, ddt, p["norm1"], scale1, shift1, w_in, tm)
    half = D_MODEL // 2
    slots = lambda g: g[:IN_W].reshape(N_DEV, IN_W // N_DEV, half)
    g_in_b, exchanged = _wgrad((dq, dkv, dz, dxbc, ddt), h1, IN_PAD, tw, "wgrad_in_b",
                               [([slots(g_in_a)], True), ([acc0, acc1, acc2, dcw, dvec, dd, dsink, drel, c], False)],
                               g_cols=(half, 1))
    return gx, (exchanged[0], slots(g_in_b)), (r_o, r_gu, r_d), exchanged[1:]


def kernel(x, c, ada_w, ada_b, norm1, w_in, conv_w, conv_b, dt_bias, A_log, D_skip, sinks, attn_out_norm, ssm_out_norm, w_o, norm2, w_gate_up, w_down, rel_bias, final_norm, loss_target, m_ada_w, m_ada_b, m_norm1, m_w_in, m_conv_w, m_conv_b, m_dt_bias, m_A_log, m_D_skip, m_sinks, m_attn_out_norm, m_ssm_out_norm, m_w_o, m_norm2, m_w_gate_up, m_w_down, m_rel_bias, m_final_norm, v_ada_w, v_ada_b, v_norm1, v_w_in, v_conv_w, v_conv_b, v_dt_bias, v_A_log, v_D_skip, v_sinks, v_attn_out_norm, v_ssm_out_norm, v_w_o, v_norm2, v_w_gate_up, v_w_down, v_rel_bias, v_final_norm):
    two_d = lambda a: a if a.ndim == 2 else a.reshape(-1, a.shape[-1])
    small_params = dict(
        ada_b=(ada_b, m_ada_b, v_ada_b), norm1=(norm1, m_norm1, v_norm1), conv_w=(conv_w, m_conv_w, v_conv_w),
        conv_b=(conv_b, m_conv_b, v_conv_b), dt_bias=(dt_bias, m_dt_bias, v_dt_bias), A_log=(A_log, m_A_log, v_A_log),
        D_skip=(D_skip, m_D_skip, v_D_skip), sinks=(sinks, m_sinks, v_sinks),
        attn_out_norm=(attn_out_norm, m_attn_out_norm, v_attn_out_norm),
        ssm_out_norm=(ssm_out_norm, m_ssm_out_norm, v_ssm_out_norm), norm2=(norm2, m_norm2, v_norm2),
        rel_bias=(rel_bias, m_rel_bias, v_rel_bias), final_norm=(final_norm, m_final_norm, v_final_norm))
    small_params = {k: tuple(two_d(a) for a in v) for k, v in small_params.items()}
    S = x.shape[1]
    xs, tgt = x.reshape(S, D_MODEL), loss_target.reshape(S, D_MODEL)
    ada_w2 = ada_w[0]
    chunk = ada_w2.shape[1]
    t_in = [jnp.transpose(a[0]) for a in (w_in, m_w_in, v_w_in)]
    t_gu = [jnp.transpose(a[0]) for a in (w_gate_up, m_w_gate_up, v_w_gate_up)]

    mod, (g_in, g_cw) = _mod_and_gather(c, ada_w2, ada_b.reshape(N_DEV, chunk), [t_in[0].astype(WIRE_DTYPE), conv_w[0]])
    mod = mod.reshape(6, D_MODEL)
    w_in_full = jnp.pad(g_in.reshape(IN_W, D_MODEL), ((0, IN_PAD - IN_W), (0, 0)))
    conv_w_full = jnp.transpose(g_cw, (1, 0, 2)).reshape(4, XBC_W)

    p = {k: v[0] for k, v in small_params.items()}
    gx, (r_in_a, gw_in_b), (r_o, r_gu, r_d), gathered = _local_step(
        xs, tgt, c, mod, w_in_full, conv_w_full, w_o[0].astype(WIRE_DTYPE), t_gu[0].astype(WIRE_DTYPE),
        w_down[0].astype(WIRE_DTYPE), p)

    (u_gu, u_d, u_o), (r_in_b,) = _reduce_adamw_hosting(
        [r_gu, r_d, r_o], [tuple(t_gu), (w_down[0], m_w_down[0], v_w_down[0]), (w_o[0], m_w_o[0], v_w_o[0])],
        "adamw_big", ([gw_in_b], True))
    r_in = jnp.concatenate([r_in_a, r_in_b], axis=2)

    small, loss, c_all, dmod_all = _small_update(gathered, small_params)

    big = {
        "ada_w": _ada_w_update(c_all, dmod_all, ada_w2, m_ada_w[0], v_ada_w[0]),
        "w_in": [jnp.transpose(a) for a in _reduce_adamw(r_in, *t_in, "adamw_w_in")],
        "w_o": u_o,
        "w_gate_up": [jnp.transpose(a) for a in u_gu],
        "w_down": u_d,
    }
    big.update(small)

    order = ['ada_w', 'ada_b', 'norm1', 'w_in', 'conv_w', 'conv_b', 'dt_bias', 'A_log', 'D_skip', 'sinks',
             'attn_out_norm', 'ssm_out_norm', 'w_o', 'norm2', 'w_gate_up', 'w_down', 'rel_bias', 'final_norm']
    shapes = dict(ada_w=ada_w.shape, ada_b=ada_b.shape, norm1=norm1.shape, w_in=w_in.shape, conv_w=conv_w.shape,
                  conv_b=conv_b.shape, dt_bias=dt_bias.shape, A_log=A_log.shape, D_skip=D_skip.shape,
                  sinks=sinks.shape, attn_out_norm=attn_out_norm.shape, ssm_out_norm=ssm_out_norm.shape,
                  w_o=w_o.shape, norm2=norm2.shape, w_gate_up=w_gate_up.shape, w_down=w_down.shape,
                  rel_bias=rel_bias.shape, final_norm=final_norm.shape)
    outs = [[], [], [], []]
    for name in order:
        for kind in range(4):
            outs[kind].append(big[name][kind].reshape(shapes[name]))
    return (loss.reshape(()), gx.reshape(x.shape), *outs[0], *outs[1], *outs[2], *outs[3])
```

```python
import numpy as np
import jax
import jax.numpy as jnp
from jax import lax
from jax.experimental import pallas as pl
from jax.experimental.pallas import tpu as pltpu

F32 = jnp.float32
MXU_DTYPE = jnp.bfloat16
WIRE_DTYPE = jnp.bfloat16
HI = lax.Precision.HIGHEST
MESH = pl.DeviceIdType.MESH
N_DEV = 8

D_MODEL = 1024
ATTN_W = 512
KV_W = 128
SSM_W = 512
XBC_W = 1024
N_HEADS = 8
D_STATE = 128
D_FF = 2816
IN_W = 2312
IN_PAD = 2432
BLK = 128
N_BUCKETS = 32
EPS = 1e-6
LANE = 128
HALF = 64

ADAM_LR, ADAM_B1, ADAM_B2, ADAM_EPS, ADAM_WD, ADAM_STEP = 0.001, 0.9, 0.999, 1e-08, 0.01, 10

VMEM_BIG = 56 * 1024 * 1024
WD_CUT = 288
WGU_CUTS = (240, 496)


def _cparams(vmem=None):
    if vmem is None:
        return pltpu.CompilerParams()
    return pltpu.CompilerParams(vmem_limit_bytes=vmem)


def _mm(a, b):
    return jnp.dot(a.astype(MXU_DTYPE), b.astype(MXU_DTYPE), preferred_element_type=F32)


def _mm_nt(a, b):
    return lax.dot_general(a.astype(MXU_DTYPE), b.astype(MXU_DTYPE), (((1,), (1,)), ((), ())),
                           preferred_element_type=F32)


def _mm_tn(a, b):
    return lax.dot_general(a.astype(MXU_DTYPE), b.astype(MXU_DTYPE), (((0,), (0,)), ((), ())),
                           preferred_element_type=F32)


def _mm_hi(a, b):
    return jnp.dot(a, b, precision=HI, preferred_element_type=F32)


def _silu(x):
    return x * jax.nn.sigmoid(x)


def _softplus(x):
    return jnp.maximum(x, 0.0) + jnp.log1p(jnp.exp(-jnp.abs(x)))


def _rms(x, g, n):
    return x * lax.rsqrt(jnp.sum(x * x, axis=-1, keepdims=True) * (1.0 / n) + EPS) * g


def _modnorm(x, g, scale, shift):
    return _rms(x, g, x.shape[-1]) * (1.0 + scale) + shift


def _modnorm_parts(x):
    r = lax.rsqrt(jnp.sum(x * x, axis=-1, keepdims=True) * (1.0 / x.shape[-1]) + EPS)
    return r, x * r


def _modnorm_bwd(r, xhat, g, scale, dy):
    dyg = dy * (g * (1.0 + scale))
    c = jnp.sum(dyg * xhat, axis=-1, keepdims=True) * (1.0 / xhat.shape[-1])
    dx = r * (dyg - xhat * c)
    ct = jnp.sum(dy * xhat, axis=0, keepdims=True)
    return dx, ct * (1.0 + scale), ct * g, jnp.sum(dy, axis=0, keepdims=True)


def _lane_iota(shape):
    return lax.broadcasted_iota(jnp.int32, shape, len(shape) - 1)


def _split_pair(t):
    lane = _lane_iota(t.shape)
    lo = jnp.where(lane < HALF, t, 0.0)
    hi = pltpu.roll(jnp.where(lane >= HALF, t, 0.0), HALF, 1)
    return lo, hi


def _join_pair(lo, hi):
    lane = _lane_iota(lo.shape)
    return jnp.where(lane < HALF, lo, pltpu.roll(hi, HALF, 1))


def _split_heads(t, n_pairs):
    out = []
    for p in range(n_pairs):
        out.extend(_split_pair(t[:, p * LANE:(p + 1) * LANE]))
    return out


def _join_heads(hs):
    return jnp.concatenate([_join_pair(hs[2 * p], hs[2 * p + 1]) for p in range(len(hs) // 2)], axis=1)


def _t5_bucket_table():
    dist = np.arange(BLK)[:, None] + BLK - np.arange(2 * BLK)[None, :]
    n = np.maximum(dist, 0)
    max_exact = N_BUCKETS // 2
    large = max_exact + (np.log(np.maximum(n, 1) / max_exact) / np.log(128 / max_exact)
                         * (N_BUCKETS - max_exact)).astype(np.int32)
    large = np.minimum(large, N_BUCKETS - 1)
    return np.where(n < max_exact, n, large).astype(np.int32)


def _my_pos():
    return lax.axis_index("x"), lax.axis_index("y"), lax.axis_index("c")


def _peer(k):
    x, y, c = _my_pos()
    return (1 - x if k & 4 else x, 1 - y if k & 2 else y, 1 - c if k & 1 else c)


def _lin(pos):
    return 4 * pos[0] + 2 * pos[1] + pos[2]


def _xchg_copies(ins, outs, sems, scatter):
    local_sem, send_sem, recv_sem = sems
    me = _lin(_my_pos())
    local, remote = [], []
    for a in range(len(ins)):
        src = ins[a].at[me] if scatter else ins[a]
        local.append(pltpu.make_async_copy(src, outs[a].at[me], local_sem.at[a]))
    for k in range(1, N_DEV):
        peer = _peer(k)
        for a in range(len(ins)):
            src = ins[a].at[_lin(peer)] if scatter else ins[a]
            remote.append(pltpu.make_async_remote_copy(src, outs[a].at[me], send_sem.at[a, k - 1],
                                                       recv_sem.at[a, k - 1], device_id=peer, device_id_type=MESH))
    return local, remote


def _xchg_start(ins, outs, sems, scatter):
    local, remote = _xchg_copies(ins, outs, sems, scatter)
    for cp in local + remote:
        cp.start()


def _xchg_wait(ins, outs, sems, scatter):
    local, remote = _xchg_copies(ins, outs, sems, scatter)
    for cp in local:
        cp.wait()
    for cp in remote:
        cp.wait_send()
        cp.wait_recv()


def _xchg_shapes(arrs, scatter):
    n = len(arrs)
    if scatter:
        out_shape = [jax.ShapeDtypeStruct(a.shape, a.dtype) for a in arrs]
    else:
        out_shape = [jax.ShapeDtypeStruct((N_DEV,) + a.shape, a.dtype) for a in arrs]
    sems = [pltpu.SemaphoreType.DMA((n,)), pltpu.SemaphoreType.DMA((n, N_DEV - 1)),
            pltpu.SemaphoreType.DMA((n, N_DEV - 1))]
    return out_shape, sems


_CHIPS = (2, 4, 6)


def _g2_sems(n):
    dma = pltpu.SemaphoreType.DMA
    return [dma((n,)), dma((n, N_DEV)), dma((n, N_DEV)), dma((n, len(_CHIPS))), dma((n, len(_CHIPS)))]


class _TwoLevelGather:
    def __init__(self, ins, outs, sems):
        self.ins, self.outs = ins, outs
        self.local_sem, self.send_sem, self.recv_sem, self.fsend_sem, self.frecv_sem = sems
        self.n = len(ins)

    def _direct(self, a, k):
        return pltpu.make_async_remote_copy(self.ins[a], self.outs[a].at[_lin(_my_pos())], self.send_sem.at[a, k],
                                            self.recv_sem.at[a, k], device_id=_peer(k), device_id_type=MESH)

    def _handed_on(self, a, j, origin):
        slot = self.outs[a].at[origin]
        return pltpu.make_async_remote_copy(slot, slot, self.fsend_sem.at[a, j], self.frecv_sem.at[a, j],
                                            device_id=_peer(1), device_id_type=MESH)

    def _local(self, a):
        return pltpu.make_async_copy(self.ins[a], self.outs[a].at[_lin(_my_pos())], self.local_sem.at[a])

    def start(self):
        for a in range(self.n):
            self._local(a).start()
        for k in (1,) + _CHIPS:
            for a in range(self.n):
                self._direct(a, k).start()

    def forward(self):
        for j, k in enumerate(_CHIPS):
            for a in range(self.n):
                self._direct(a, k).wait_recv()
                self._handed_on(a, j, _lin(_peer(k))).start()

    def finish(self):
        for a in range(self.n):
            self._direct(a, 1).wait_recv()
            for j, k in enumerate(_CHIPS):
                self._handed_on(a, j, _lin(_peer(k ^ 1))).wait_recv()
            self._local(a).wait()
            for k in (1,) + _CHIPS:
                self._direct(a, k).wait_send()
            for j, k in enumerate(_CHIPS):
                self._handed_on(a, j, _lin(_peer(k))).wait_send()


def _mod_and_gather(c, ada_w, ada_b8, arrs):
    n = len(arrs)
    chunk = ada_w.shape[1]
    out_shape = [jax.ShapeDtypeStruct((N_DEV, 1, chunk), F32)]
    out_shape += [jax.ShapeDtypeStruct((N_DEV,) + a.shape, a.dtype) for a in arrs]

    def modulation(c_ref, w_ref, b_ref, out_ref, cbuf, part, s1, r1, s2, r2):
        me = _lin(_my_pos())
        first = []
        for k in range(1, N_DEV):
            cp = pltpu.make_async_remote_copy(c_ref, cbuf.at[me], s1.at[k - 1], r1.at[k - 1],
                                              device_id=_peer(k), device_id_type=MESH)
            cp.start()
            first.append(cp)
        cbuf[me] = c_ref[...]
        for cp in first:
            cp.wait_send()
            cp.wait_recv()
        cond = _silu(jnp.concatenate([cbuf[i] for i in range(N_DEV)], axis=0))
        mod = _mm_hi(cond, w_ref[...]) + b_ref[pl.ds(me, 1), :]
        for j in range(N_DEV):
            part[j] = mod[j:j + 1, :]
        second = []
        for k in range(1, N_DEV):
            peer = _peer(k)
            cp = pltpu.make_async_remote_copy(part.at[_lin(peer)], out_ref.at[me], s2.at[k - 1], r2.at[k - 1],
                                              device_id=peer, device_id_type=MESH)
            cp.start()
            second.append(cp)
        out_ref[me] = part[me]
        for cp in second:
            cp.wait_send()
            cp.wait_recv()

    def body(*refs):
        c_ref, w_ref, b_ref = refs[:3]
        ins = refs[3:3 + n]
        mod_ref = refs[3 + n]
        outs = refs[4 + n:4 + 2 * n]
        cbuf, part, s1, r1, s2, r2 = refs[4 + 2 * n:10 + 2 * n]
        gather = _TwoLevelGather(ins, outs, refs[10 + 2 * n:])
        gather.start()
        modulation(c_ref, w_ref, b_ref, mod_ref, cbuf, part, s1, r1, s2, r2)
        gather.forward()
        gather.finish()

    hbm = pl.BlockSpec(memory_space=pltpu.HBM)
    vm = pl.BlockSpec(memory_space=pltpu.VMEM)
    dma = pltpu.SemaphoreType.DMA
    res = pl.pallas_call(
        body, name="mod_and_gather", out_shape=out_shape, in_specs=[vm, vm, vm] + [hbm] * n,
        out_specs=[vm] + [hbm] * n,
        scratch_shapes=[pltpu.VMEM((N_DEV, 1, D_MODEL), F32), pltpu.VMEM((N_DEV, 1, chunk), F32)]
        + [dma((N_DEV - 1,))] * 4 + _g2_sems(n),
    )(c, ada_w, ada_b8, *arrs)
    return res[0], res[1:]


def _hosted_call(body, name, grid, in_specs, out_specs, out_shape, scratch_shapes, args, xchg, cparams):
    xchgs = [xchg] if isinstance(xchg, tuple) else list(xchg)
    grid = (grid,) if isinstance(grid, int) else tuple(grid)
    n_in, n_out, n_scr = len(in_specs), len(out_specs), len(scratch_shapes)
    arrs = [a for group, _ in xchgs for a in group]
    n = len(arrs)
    x_shape, x_sems, sem_counts = [], [], []
    for group, mode in xchgs:
        shapes, sems = _xchg_shapes(group, False if mode == "two-level" else mode)
        if mode == "two-level":
            sems = _g2_sems(len(group))
        x_shape += shapes
        x_sems += sems
        sem_counts.append(len(sems))
    n_steps = int(np.prod(grid))

    def hosted(*refs):
        ins, refs = refs[:n_in], refs[n_in:]
        x_in, refs = refs[:n], refs[n:]
        outs, refs = refs[:n_out], refs[n_out:]
        x_out, refs = refs[:n], refs[n:]
        scr, sems = refs[:n_scr], refs[n_scr:]
        step = pl.program_id(0)
        for d in range(1, len(grid)):
            step = step * grid[d] + pl.program_id(d)
        parts, a0, s0 = [], 0, 0
        for (group, mode), ns in zip(xchgs, sem_counts):
            parts.append((x_in[a0:a0 + len(group)], x_out[a0:a0 + len(group)], sems[s0:s0 + ns], mode))
            a0, s0 = a0 + len(group), s0 + ns

        @pl.when(step == 0)
        def _():
            for gi, go, gs, mode in parts:
                if mode == "two-level":
                    _TwoLevelGather(gi, go, gs).start()
                else:
                    _xchg_start(gi, go, gs, mode)

        if any(mode == "two-level" for _, mode in xchgs):
            @pl.when(step == (2 * n_steps) // 3)
            def _():
                for gi, go, gs, mode in parts:
                    if mode == "two-level":
                        _TwoLevelGather(gi, go, gs).forward()

        body(*ins, *outs, *scr)

        @pl.when(step == n_steps - 1)
        def _():
            for gi, go, gs, mode in parts:
                if mode == "two-level":
                    _TwoLevelGather(gi, go, gs).finish()
                else:
                    _xchg_wait(gi, go, gs, mode)

    hbm = pl.BlockSpec(memory_space=pltpu.HBM)
    res = pl.pallas_call(
        hosted, name=name, grid=grid, in_specs=list(in_specs) + [hbm] * n,
        out_specs=list(out_specs) + [hbm] * n, out_shape=list(out_shape) + x_shape,
        scratch_shapes=list(scratch_shapes) + x_sems, compiler_params=cparams,
    )(*args, *arrs)
    return res[:n_out], res[n_out:]


def _row(i):
    return (i, 0)


def _fixed(i):
    return (0, 0)


def _in_proj_fwd(x, norm1, scale1, shift1, w_in, tm, xchg):
    S = x.shape[0]

    def body(x_ref, n_ref, sc_ref, sh_ref, w_ref, qkv_ref, z_ref, xbc_ref, dt_ref):
        h = _modnorm(x_ref[...], n_ref[...], sc_ref[...], sh_ref[...])
        p = _mm_nt(h, w_ref[...])
        qkv_ref[...] = p[:, :768].astype(qkv_ref.dtype)
        z_ref[...] = p[:, 768:1280]
        xbc_ref[...] = p[:, 1280:2304]
        dt_ref[...] = p[:, 2304:IN_PAD]

    vec = pl.BlockSpec((1, D_MODEL), _fixed)
    return _hosted_call(
        body, "in_proj_fwd", S // tm,
        in_specs=[pl.BlockSpec((tm, D_MODEL), _row), vec, vec, vec, pl.BlockSpec((IN_PAD, D_MODEL), _fixed)],
        out_specs=[pl.BlockSpec((tm, 768), _row), pl.BlockSpec((tm, SSM_W), _row),
                   pl.BlockSpec((tm, XBC_W), _row), pl.BlockSpec((tm, LANE), _row)],
        out_shape=[jax.ShapeDtypeStruct((S, 768), MXU_DTYPE), jax.ShapeDtypeStruct((S, SSM_W), F32),
                   jax.ShapeDtypeStruct((S, XBC_W), F32), jax.ShapeDtypeStruct((S, LANE), F32)],
        scratch_shapes=[], args=(x, norm1, scale1, shift1, w_in), xchg=xchg, cparams=_cparams(VMEM_BIG),
    )


def _in_proj_bwd(x, dx1, dq, dkv, dz, dxbc, ddt, norm1, scale1, shift1, w_in, tm):
    S = x.shape[0]

    n_steps = S // tm
    half_cols = D_MODEL // 2

    def body(x_ref, dx1_ref, dq_ref, dkv_ref, dz_ref, dxbc_ref, ddt_ref, n_ref, sc_ref, sh_ref, w_ref,
             gx_ref, h_ref, acc_ref, gw_ref, gw_acc):
        i = pl.program_id(0)

        @pl.when(i == 0)
        def _():
            acc_ref[...] = jnp.zeros_like(acc_ref)
            gw_acc[...] = jnp.zeros_like(gw_acc)

        halves = [pl.ds(k * (tm // 2), tm // 2) for k in range(2)]
        dp = [jnp.concatenate([r[rows, :] for r in (dq_ref, dkv_ref, dz_ref, dxbc_ref, ddt_ref)], axis=1)
              for rows in halves]
        dh = [_mm(dp[k], w_ref[...]) for k in range(2)]
        parts = [_modnorm_parts(x_ref[rows, :]) for rows in halves]
        hb = [(parts[k][1] * n_ref[...] * (1.0 + sc_ref[...]) + sh_ref[...]).astype(h_ref.dtype) for k in range(2)]
        gw_acc[...] += _mm_tn(dp[0], hb[0][:, :half_cols]) + _mm_tn(dp[1], hb[1][:, :half_cols])
        bwd = [_modnorm_bwd(parts[k][0], parts[k][1], n_ref[...], sc_ref[...], dh[k]) for k in range(2)]
        for k, rows in enumerate(halves):
            gx_ref[rows, :] = dx1_ref[rows, :] + bwd[k][0]
            h_ref[rows, :] = hb[k]
        acc_ref[0:1, :] += bwd[0][1] + bwd[1][1]
        acc_ref[1:2, :] += bwd[0][2] + bwd[1][2]
        acc_ref[2:3, :] += bwd[0][3] + bwd[1][3]

        @pl.when(i == n_steps - 1)
        def _():
            gw_ref[...] = gw_acc[...].astype(gw_ref.dtype)

    vec = pl.BlockSpec((1, D_MODEL), _fixed)
    return pl.pallas_call(
        body, name="in_proj_bwd", grid=(n_steps,),
        in_specs=[pl.BlockSpec((tm, D_MODEL), _row), pl.BlockSpec((tm, D_MODEL), _row),
                  pl.BlockSpec((tm, ATTN_W), _row), pl.BlockSpec((tm, 2 * KV_W), _row),
                  pl.BlockSpec((tm, SSM_W), _row), pl.BlockSpec((tm, XBC_W), _row), pl.BlockSpec((tm, LANE), _row),
                  vec, vec, vec, pl.BlockSpec((IN_PAD, D_MODEL), _fixed)],
        out_specs=[pl.BlockSpec((tm, D_MODEL), _row), pl.BlockSpec((tm, D_MODEL), _row),
                   pl.BlockSpec((8, D_MODEL), _fixed), pl.BlockSpec((IN_PAD, half_cols), _fixed)],
        out_shape=[jax.ShapeDtypeStruct((S, D_MODEL), F32), jax.ShapeDtypeStruct((S, D_MODEL), MXU_DTYPE),
                   jax.ShapeDtypeStruct((8, D_MODEL), F32), jax.ShapeDtypeStruct((IN_PAD, half_cols), WIRE_DTYPE)],
        scratch_shapes=[pltpu.VMEM((IN_PAD, half_cols), F32)],
        compiler_params=_cparams(VMEM_BIG),
    )(x, dx1, dq, dkv, dz, dxbc, ddt, norm1, scale1, shift1, w_in)


def _out_stage(ya, ys0, ys1, z0, z1, an, sn0, sn1):
    half = SSM_W // 2
    a = _rms(ya, an, ATTN_W)
    g0 = _rms(ys0 * _silu(z0), sn0, half)
    g1 = _rms(ys1 * _silu(z1), sn1, half)
    return jnp.concatenate([a, g0, g1], axis=1)


def _out_stage_args(ya_ref, ys_ref, z_ref, an_ref, sn_ref):
    half = SSM_W // 2
    return (ya_ref[...], ys_ref[:, :half], ys_ref[:, half:], z_ref[:, :half], z_ref[:, half:],
            an_ref[...], sn_ref[:, :half], sn_ref[:, half:])


def _out_proj_fwd(x, ya, ys, z, an, sn, gate1, w_o, tm, xchg):
    S = x.shape[0]

    def body(x_ref, ya_ref, ys_ref, z_ref, an_ref, sn_ref, g_ref, w_ref, x1_ref):
        u = _out_stage(*_out_stage_args(ya_ref, ys_ref, z_ref, an_ref, sn_ref))
        x1_ref[...] = x_ref[...] + g_ref[...] * _mm(u, w_ref[...])

    half = pl.BlockSpec((tm, ATTN_W), _row)
    hvec = pl.BlockSpec((1, ATTN_W), _fixed)
    (x1,), x_out = _hosted_call(
        body, "out_proj_fwd", S // tm,
        in_specs=[pl.BlockSpec((tm, D_MODEL), _row), half, half, half, hvec, hvec,
                  pl.BlockSpec((1, D_MODEL), _fixed), pl.BlockSpec((D_MODEL, D_MODEL), _fixed)],
        out_specs=[pl.BlockSpec((tm, D_MODEL), _row)],
        out_shape=[jax.ShapeDtypeStruct((S, D_MODEL), F32)],
        scratch_shapes=[], args=(x, ya, ys, z, an, sn, gate1, w_o), xchg=xchg, cparams=_cparams(VMEM_BIG),
    )
    return x1, x_out


def _out_proj_bwd(dx1, ya, ys, z, an, sn, gate1, w_o, tm, xchg):
    S = dx1.shape[0]
    n_steps = S // tm

    def body(dx1_ref, ya_ref, ys_ref, z_ref, an_ref, sn_ref, g_ref, w_ref,
             dya_ref, dys_ref, dz_ref, gw_ref, acc_ref, gw_acc):
        i = pl.program_id(0)

        @pl.when(i == 0)
        def _():
            acc_ref[...] = jnp.zeros_like(acc_ref)
            gw_acc[...] = jnp.zeros_like(gw_acc)

        u, vjp = jax.vjp(_out_stage, *_out_stage_args(ya_ref, ys_ref, z_ref, an_ref, sn_ref))
        dx1 = dx1_ref[...]
        ub = u.astype(MXU_DTYPE)
        mix = _mm(ub, w_ref[...])
        dmix = dx1 * g_ref[...]
        dmixb = dmix.astype(MXU_DTYPE)
        du = _mm_nt(dmixb, w_ref[...])
        gw_acc[...] += _mm_tn(ub, dmixb)
        dya, dys0, dys1, dz0, dz1, dan, dsn0, dsn1 = vjp(du)
        dya_ref[...] = dya
        dys_ref[...] = jnp.concatenate([dys0, dys1], axis=1)
        dz_ref[...] = jnp.concatenate([dz0, dz1], axis=1).astype(dz_ref.dtype)
        acc_ref[0:1, :] += jnp.sum(dx1 * mix, axis=0, keepdims=True)
        acc_ref[1:2, :] += jnp.concatenate([dan, dsn0, dsn1], axis=1)

        @pl.when(i == n_steps - 1)
        def _():
            gw_ref[...] = gw_acc[...].astype(gw_ref.dtype)

    half = pl.BlockSpec((tm, ATTN_W), _row)
    hvec = pl.BlockSpec((1, ATTN_W), _fixed)
    full = pl.BlockSpec((tm, D_MODEL), _row)
    return _hosted_call(
        body, "out_proj_bwd", n_steps,
        in_specs=[full, half, half, half, hvec, hvec,
                  pl.BlockSpec((1, D_MODEL), _fixed), pl.BlockSpec((D_MODEL, D_MODEL), _fixed)],
        out_specs=[half, half, half, pl.BlockSpec((D_MODEL, D_MODEL), _fixed), pl.BlockSpec((8, D_MODEL), _fixed)],
        out_shape=[jax.ShapeDtypeStruct((S, ATTN_W), F32)] * 2 + [jax.ShapeDtypeStruct((S, ATTN_W), MXU_DTYPE),
                   jax.ShapeDtypeStruct((D_MODEL, D_MODEL), WIRE_DTYPE), jax.ShapeDtypeStruct((8, D_MODEL), F32)],
        scratch_shapes=[pltpu.VMEM((D_MODEL, D_MODEL), F32)],
        args=(dx1, ya, ys, z, an, sn, gate1, w_o), xchg=xchg, cparams=_cparams(VMEM_BIG),
    )


def _loss_rows(x2, fn, tgt):
    y = _rms(x2, fn, D_MODEL)
    per_row = jnp.sum(jnp.square(y - tgt), axis=1, keepdims=True)
    return jnp.sum(per_row, axis=0, keepdims=True) * (0.5 / D_MODEL)


def _mlp_loss(x1, tgt, norm2, scale2, shift2, gate2, fnorm, w_gu, w_d, tm):
    S = x1.shape[0]
    n_pieces = len(w_gu) + len(w_d)

    def body(*refs):
        x1_ref, t_ref, n_ref, sc_ref, sh_ref, g_ref, fn_ref = refs[:7]
        piece_refs = refs[7:7 + n_pieces]
        dx1_ref, h_ref, dgu_ref, act_ref, dmlp_ref, acc_ref, wgu, wd, wsem = refs[7 + n_pieces:]

        @pl.when(pl.program_id(0) == 0)
        def _():
            acc_ref[...] = jnp.zeros_like(acc_ref)
            copies = []
            for dst, pieces in ((wgu, piece_refs[:len(w_gu)]), (wd, piece_refs[len(w_gu):])):
                shard = sum(p.shape[1] for p in pieces)
                off = 0
                for p in pieces:
                    for j in range(N_DEV):
                        copies.append(pltpu.make_async_copy(p.at[j], dst.at[pl.ds(j * shard + off, p.shape[1])],
                                                            wsem.at[len(copies)]))
                    off += p.shape[1]
            for cp in copies:
                cp.start()
            for cp in copies:
                cp.wait()

        x1 = x1_ref[...]
        gate2 = g_ref[...]
        h, vjp_h = jax.vjp(_modnorm, x1, n_ref[...], sc_ref[...], sh_ref[...])
        hb = h.astype(MXU_DTYPE)
        gu = _mm_nt(hb, wgu[...])
        g, u = gu[:, :D_FF], gu[:, D_FF:]
        sg = jax.nn.sigmoid(g)
        silu_g = g * sg
        act = (silu_g * u).astype(MXU_DTYPE)
        mlp = _mm(act, wd[...])
        x2 = x1 + gate2 * mlp
        loss, vjp_loss = jax.vjp(_loss_rows, x2, fn_ref[...], t_ref[...])
        dx2, dfn, _ = vjp_loss(jnp.ones((1, 1), F32))
        dmlp = (dx2 * gate2).astype(MXU_DTYPE)
        dact = _mm_nt(dmlp, wd[...])
        dg = dact * u * (sg * (1.0 + g * (1.0 - sg)))
        du = dact * silu_g
        dgu = jnp.concatenate([dg, du], axis=1).astype(MXU_DTYPE)
        dh = _mm(dgu, wgu[...])
        dx, dn, dsc, dsh = vjp_h(dh)
        dx1_ref[...] = dx2 + dx
        h_ref[...] = hb
        dgu_ref[...] = dgu
        act_ref[...] = act
        dmlp_ref[...] = dmlp
        acc_ref[0:1, :] += dn
        acc_ref[1:2, :] += dsc
        acc_ref[2:3, :] += dsh
        acc_ref[3:4, :] += jnp.sum(dx2 * mlp, axis=0, keepdims=True)
        acc_ref[4:5, :] += dfn
        acc_ref[5:6, :] += jnp.broadcast_to(loss, (1, D_MODEL))

    full = pl.BlockSpec((tm, D_MODEL), _row)
    vec = pl.BlockSpec((1, D_MODEL), _fixed)
    anyspec = pl.BlockSpec(memory_space=pl.ANY)
    return pl.pallas_call(
        body, name="mlp_loss", grid=(S // tm,),
        in_specs=[full, full, vec, vec, vec, vec, vec] + [anyspec] * n_pieces,
        out_specs=[full, full, pl.BlockSpec((tm, 2 * D_FF), _row), pl.BlockSpec((tm, D_FF), _row), full,
                   pl.BlockSpec((8, D_MODEL), _fixed)],
        out_shape=[jax.ShapeDtypeStruct((S, D_MODEL), F32), jax.ShapeDtypeStruct((S, D_MODEL), MXU_DTYPE),
                   jax.ShapeDtypeStruct((S, 2 * D_FF), MXU_DTYPE), jax.ShapeDtypeStruct((S, D_FF), MXU_DTYPE),
                   jax.ShapeDtypeStruct((S, D_MODEL), MXU_DTYPE), jax.ShapeDtypeStruct((8, D_MODEL), F32)],
        scratch_shapes=[pltpu.VMEM((2 * D_FF, D_MODEL), MXU_DTYPE), pltpu.VMEM((D_FF, D_MODEL), MXU_DTYPE),
                        pltpu.SemaphoreType.DMA((N_DEV * n_pieces,))],
        compiler_params=_cparams(VMEM_BIG),
    )(x1, tgt, norm2, scale2, shift2, gate2, fnorm, *w_gu, *w_d)


def _wgrad(a, g, tk, ts, name, xchg=None, g_cols=None):
    pieces = list(a) if isinstance(a, (list, tuple)) else [a]
    S = pieces[0].shape[0]
    K = sum(p.shape[1] for p in pieces)
    assert len(pieces) == 1 or tk == K
    N, col = (g.shape[1], 0) if g_cols is None else g_cols
    ns = S // ts
    n_a = len(pieces)

    def body(*refs):
        a_refs, (g_ref, o_ref, acc_ref) = refs[:n_a], refs[n_a:]
        s = pl.program_id(1)

        @pl.when(s == 0)
        def _():
            acc_ref[...] = jnp.zeros_like(acc_ref)

        a_blk = a_refs[0][...] if n_a == 1 else jnp.concatenate([r[...] for r in a_refs], axis=1)
        acc_ref[...] += _mm_tn(a_blk, g_ref[...])

        @pl.when(s == ns - 1)
        def _():
            o_ref[...] = acc_ref[...].astype(o_ref.dtype)

    if n_a == 1:
        in_specs = [pl.BlockSpec((ts, tk), lambda j, s: (s, j))]
    else:
        in_specs = [pl.BlockSpec((ts, p.shape[1]), lambda j, s: (s, 0)) for p in pieces]
    in_specs.append(pl.BlockSpec((ts, N), lambda j, s: (s, col)))
    out_spec = pl.BlockSpec((tk, N), lambda j, s: (j, 0))
    out_shape = jax.ShapeDtypeStruct((K, N), WIRE_DTYPE)
    scratch = [pltpu.VMEM((tk, N), F32)]
    args = (*pieces, g)
    if xchg is None:
        return pl.pallas_call(body, name=name, grid=(K // tk, ns), in_specs=in_specs, out_specs=out_spec,
                              out_shape=out_shape, scratch_shapes=scratch, compiler_params=_cparams(VMEM_BIG))(*args)
    (out,), x_out = _hosted_call(body, name, (K // tk, ns), in_specs, [out_spec], [out_shape], scratch, args, xchg,
                                 _cparams(VMEM_BIG))
    return out, x_out


SSD_CHUNKS_PER_STEP = 4
SSD_BWD_CHUNKS_PER_STEP = 4
ATTN_BLOCKS_PER_STEP = 4
MASKED = -1e30
QK_SCALE = HALF ** -0.5


def _attn_bias(buckets, rel_bias):
    def body(bk_ref, relb_ref, out_ref):
        bk = bk_ref[...]
        i = lax.broadcasted_iota(jnp.int32, (BLK, 2 * BLK), 0)
        j = lax.broadcasted_iota(jnp.int32, (BLK, 2 * BLK), 1)
        window = (j > i) & (j <= i + BLK)
        for h in range(N_HEADS):
            acc = jnp.zeros((BLK, 2 * BLK), F32)
            for b in range(N_BUCKETS):
                acc = jnp.where(bk == b, relb_ref[b, h], acc)
            out_ref[0, h] = jnp.where(window, acc, MASKED)
            out_ref[1, h] = jnp.where(window & (j >= BLK), acc, MASKED)

    return pl.pallas_call(
        body, name="attn_bias", out_shape=jax.ShapeDtypeStruct((2, N_HEADS, BLK, 2 * BLK), F32),
        in_specs=[pl.BlockSpec(memory_space=pltpu.VMEM), pl.BlockSpec(memory_space=pltpu.SMEM)],
    )(buckets, rel_bias)


def _attn_fwd(qkv, bias, sinks, xchg):
    S = qkv.shape[0]
    nb = S // BLK

    nq = ATTN_BLOCKS_PER_STEP if nb % ATTN_BLOCKS_PER_STEP == 0 else 1
    rows = nq * BLK

    def body(q_ref, kvp_ref, kvc_ref, bias_ref, sinks_ref, y_ref):
        i = pl.program_id(0)
        q = q_ref[...].astype(F32) * QK_SCALE
        kv = jnp.concatenate([kvp_ref[...], kvc_ref[...]], axis=0).astype(F32)
        k_lo, k_hi = _split_pair(kv[:, :LANE])
        v_lo, v_hi = _split_pair(kv[:, LANE:])
        bands = [[t[b * BLK:(b + 2) * BLK].astype(MXU_DTYPE) for t in (k_lo, k_hi, v_lo, v_hi)] for b in range(nq)]
        q_heads = [_split_heads(q[b * BLK:(b + 1) * BLK], 4) for b in range(nq)]
        first = [jnp.where(i == 0, 1, 0) if b == 0 else 0 for b in range(nq)]
        items = [(b, h) for b in range(nq) for h in range(N_HEADS)]
        s = [_mm_nt(q_heads[b][h].astype(MXU_DTYPE), bands[b][h // 4]) + bias_ref[first[b], h] for b, h in items]
        m = [jnp.maximum(jnp.max(s[n], axis=-1, keepdims=True), sinks_ref[h]) for n, (b, h) in enumerate(items)]
        p = [jnp.exp(s[n] - m[n]) for n in range(len(items))]
        rinv = [1.0 / (jnp.sum(p[n], axis=-1, keepdims=True) + jnp.exp(sinks_ref[h] - m[n]))
                for n, (b, h) in enumerate(items)]
        out = [_mm(p[n], bands[b][2 + h // 4]) * rinv[n] for n, (b, h) in enumerate(items)]
        y_ref[...] = jnp.concatenate([_join_heads(out[b * N_HEADS:(b + 1) * N_HEADS]) for b in range(nq)], axis=0)

    smem = pl.BlockSpec(memory_space=pltpu.SMEM)
    return _hosted_call(
        body, "attn_fwd", nb // nq,
        in_specs=[pl.BlockSpec((rows, ATTN_W), _row),
                  pl.BlockSpec((BLK, 2 * KV_W), lambda i: (jnp.maximum(i * nq - 1, 0), 2)),
                  pl.BlockSpec((rows, 2 * KV_W), lambda i: (i, 2)),
                  pl.BlockSpec((2, N_HEADS, BLK, 2 * BLK), lambda i: (0, 0, 0, 0)), smem],
        out_specs=[pl.BlockSpec((rows, ATTN_W), _row)],
        out_shape=[jax.ShapeDtypeStruct((S, ATTN_W), F32)],
        scratch_shapes=[],
        args=(qkv, qkv, qkv, bias, sinks), xchg=xchg, cparams=_cparams(),
    )


def _attn_bwd(qkv, y, dy, bias, sinks, xchg):
    S = qkv.shape[0]
    nb = S // BLK
    nq = ATTN_BLOCKS_PER_STEP if nb % ATTN_BLOCKS_PER_STEP == 0 else 1
    rows, n_steps = nq * BLK, nb // nq

    def body(q_ref, kvp_ref, kvc_ref, y_ref, dy_ref, bias_ref, sinks_ref, dq_ref, dkv_ref, dbias_ref, dsk_ref, carry_ref):
        i = pl.program_id(0)

        @pl.when(i == 0)
        def _():
            dbias_ref[...] = jnp.zeros_like(dbias_ref)
            dsk_ref[...] = jnp.zeros_like(dsk_ref)
            carry_ref[...] = jnp.zeros_like(carry_ref)

        q = q_ref[...].astype(F32) * QK_SCALE
        kv = jnp.concatenate([kvp_ref[...], kvc_ref[...]], axis=0).astype(F32)
        k_lo, k_hi = _split_pair(kv[:, :LANE])
        v_lo, v_hi = _split_pair(kv[:, LANE:])
        bands = [[t[b * BLK:(b + 2) * BLK].astype(MXU_DTYPE) for t in (k_lo, k_hi, v_lo, v_hi)] for b in range(nq)]
        rows_of = lambda ref, b: ref[b * BLK:(b + 1) * BLK, :]
        first = [jnp.where(i == n_steps - 1, 1, 0) if b == 0 else 0 for b in range(nq)]
        items = [(b, h) for b in range(nq) for h in range(N_HEADS)]
        at = lambda b, h: b * N_HEADS + h
        q_heads = [hd for b in range(nq) for hd in _split_heads(q[b * BLK:(b + 1) * BLK], 4)]
        y_heads = [hd for b in range(nq) for hd in _split_heads(rows_of(y_ref, b), 4)]
        dy_heads = [hd for b in range(nq) for hd in _split_heads(rows_of(dy_ref, b), 4)]
        qs = [q_heads[n].astype(MXU_DTYPE) for n in range(len(items))]
        s = [_mm_nt(qs[at(b, h)], bands[b][h // 4]) + bias_ref[first[b], h] for b, h in items]
        m = [jnp.maximum(jnp.max(s[at(b, h)], axis=-1, keepdims=True), sinks_ref[h]) for b, h in items]
        p = [jnp.exp(s[n] - m[n]) for n in range(len(items))]
        esink = [jnp.exp(sinks_ref[h] - m[at(b, h)]) for b, h in items]
        rinv = [1.0 / (jnp.sum(p[n], axis=-1, keepdims=True) + esink[n]) for n in range(len(items))]
        t = [dy_heads[n] * rinv[n] for n in range(len(items))]
        delta = [jnp.sum(t[n] * y_heads[n], axis=-1, keepdims=True) for n in range(len(items))]
        tb = [t[n].astype(MXU_DTYPE) for n in range(len(items))]
        dp = [_mm_nt(tb[at(b, h)], bands[b][2 + h // 4]) for b, h in items]
        ds = [p[n] * (dp[n] - delta[n]) for n in range(len(items))]
        for h in range(N_HEADS):
            ds_h, dsk_h = ds[at(0, h)], esink[at(0, h)] * delta[at(0, h)]
            for b in range(1, nq):
                ds_h = ds_h + ds[at(b, h)]
                dsk_h = dsk_h + esink[at(b, h)] * delta[at(b, h)]
            dbias_ref[h] += ds_h
            dsk_ref[h] -= dsk_h
        dsb = [ds[n].astype(MXU_DTYPE) for n in range(len(items))]
        pb = [p[n].astype(MXU_DTYPE) for n in range(len(items))]
        dq_heads = [_mm(dsb[at(b, h)], bands[b][h // 4]) * QK_SCALE for b, h in items]
        grp = lambda lst, b, g: jnp.concatenate(lst[at(b, 4 * g):at(b, 4 * g) + 4], axis=0)
        dk_pads = [[_mm_tn(grp(dsb, b, g), grp(qs, b, g)) for g in range(2)] for b in range(nq)]
        dv_pads = [[_mm_tn(grp(pb, b, g), grp(tb, b, g)) for g in range(2)] for b in range(nq)]
        dq_ref[...] = jnp.concatenate([_join_heads(dq_heads[b * N_HEADS:(b + 1) * N_HEADS]) for b in range(nq)],
                                      axis=0).astype(dq_ref.dtype)
        part = lambda b, lo: jnp.concatenate(
            [_join_pair(d[b][0][lo:lo + BLK], d[b][1][lo:lo + BLK]) for d in (dk_pads, dv_pads)], axis=1)
        dkv = [part(b, BLK) + (part(b + 1, 0) if b + 1 < nq else carry_ref[...]) for b in range(nq)]
        dkv_ref[...] = jnp.concatenate(dkv, axis=0).astype(dkv_ref.dtype)
        carry_ref[...] = part(0, 0)

    smem = pl.BlockSpec(memory_space=pltpu.SMEM)
    rev = lambda i: (n_steps - 1 - i, 0)
    return _hosted_call(
        body, "attn_bwd", n_steps,
        in_specs=[pl.BlockSpec((rows, ATTN_W), rev),
                  pl.BlockSpec((BLK, 2 * KV_W), lambda i: (jnp.maximum((n_steps - 1 - i) * nq - 1, 0), 2)),
                  pl.BlockSpec((rows, 2 * KV_W), lambda i: (n_steps - 1 - i, 2)),
                  pl.BlockSpec((rows, ATTN_W), rev), pl.BlockSpec((rows, ATTN_W), rev),
                  pl.BlockSpec((2, N_HEADS, BLK, 2 * BLK), lambda i: (0, 0, 0, 0)), smem],
        out_specs=[pl.BlockSpec((rows, ATTN_W), rev), pl.BlockSpec((rows, 2 * KV_W), rev),
                   pl.BlockSpec((N_HEADS, BLK, 2 * BLK), lambda i: (0, 0, 0)),
                   pl.BlockSpec((N_HEADS, BLK, 1), lambda i: (0, 0, 0))],
        out_shape=[jax.ShapeDtypeStruct((S, ATTN_W), MXU_DTYPE), jax.ShapeDtypeStruct((S, 2 * KV_W), MXU_DTYPE),
                   jax.ShapeDtypeStruct((N_HEADS, BLK, 2 * BLK), F32), jax.ShapeDtypeStruct((N_HEADS, BLK, 1), F32)],
        scratch_shapes=[pltpu.VMEM((BLK, 2 * KV_W), F32)],
        args=(qkv, qkv, qkv, y, dy, bias, sinks), xchg=xchg, cparams=_cparams(),
    )


def _attn_finish(dbias, dsk, buckets):
    def body(db_ref, dsk_ref, bk_ref, drel_ref, dsink_ref):
        bk = bk_ref[...]
        r = lax.broadcasted_iota(jnp.int32, (N_BUCKETS, LANE), 0)
        l = lax.broadcasted_iota(jnp.int32, (N_BUCKETS, LANE), 1)
        row = lax.broadcasted_iota(jnp.int32, (N_HEADS, LANE), 0)
        res = jnp.zeros((N_BUCKETS, LANE), F32)
        dsink = jnp.zeros((N_HEADS, LANE), F32)
        for h in range(N_HEADS):
            db = db_ref[h]
            for b in range(N_BUCKETS):
                v = jnp.sum(jnp.sum(jnp.where(bk == b, db, 0.0), axis=1, keepdims=True), axis=0, keepdims=True)
                res = res + jnp.where((r == b) & (l == h), v, 0.0)
            dsink = dsink + jnp.where(row == h, jnp.sum(dsk_ref[h], axis=0, keepdims=True), 0.0)
        drel_ref[...] = res
        dsink_ref[...] = dsink

    return pl.pallas_call(body, name="attn_finish",
                          out_shape=[jax.ShapeDtypeStruct((N_BUCKETS, LANE), F32),
                                     jax.ShapeDtypeStruct((N_HEADS, LANE), F32)])(dbias, dsk, buckets)


def _ssd_consts():
    r = lax.broadcasted_iota(jnp.int32, (BLK, BLK), 0)
    c = lax.broadcasted_iota(jnp.int32, (BLK, BLK), 1)
    causal = c <= r
    upper = (r <= c).astype(F32)
    last = r == BLK - 1
    head = lax.broadcasted_iota(jnp.int32, (N_HEADS, BLK), 0)
    return causal, upper, last, head


def _ssd_chunks(xs, bg, cg, dt_raw_t, prev0, dtb, alog, d_rows, consts):
    causal, upper, last, head = consts
    nq = len(xs)
    items = [(c, h) for c in range(nq) for h in range(N_HEADS)]
    at = lambda c, h: c * N_HEADS + h
    a_neg = -jnp.exp(alog)
    dt_t = [_softplus(dt_raw_t[c] + dtb) for c in range(nq)]
    acs_t = [_mm_hi(dt_t[c] * a_neg, upper) for c in range(nq)]
    cb = [[_mm_nt(cg[c][g], bg[c][g]) for g in range(2)] for c in range(nq)]
    pick = lambda t, h: jnp.sum(jnp.where(head == h, t, 0.0), axis=0, keepdims=True)
    dt_row = [pick(dt_t[c], h) for c, h in items]
    a_row = [pick(acs_t[c], h) for c, h in items]
    a_rb = [jnp.broadcast_to(a_row[n], (BLK, BLK)) for n in range(len(items))]
    a_b = [a_rb[n].T for n in range(len(items))]
    a_last = [jnp.sum(jnp.where(last, a_b[n], 0.0), axis=0, keepdims=True) for n in range(len(items))]
    w = [cb[c][h // 4] * jnp.exp(jnp.where(causal, a_b[at(c, h)] - a_rb[at(c, h)], -1e30)) * dt_row[at(c, h)]
         for c, h in items]
    f_b = [jnp.broadcast_to(dt_row[n] * jnp.exp(a_last[n] - a_row[n]), (BLK, BLK)).T for n in range(len(items))]
    y_in = [_mm(w[at(c, h)], xs[c][h]) for c, h in items]
    st = [_mm_tn(bg[c][h // 4], xs[c][h] * f_b[at(c, h)]) for c, h in items]
    e_b = [jnp.exp(a_b[n]) for n in range(len(items))]
    states = [list(prev0)]
    for c in range(nq):
        states.append([states[c][h] * jnp.exp(a_last[at(c, h)]) + st[at(c, h)] for h in range(N_HEADS)])
    y_off = [_mm(cg[c][h // 4], states[c][h]) * e_b[at(c, h)] for c, h in items]
    ys = [[y_in[at(c, h)] + y_off[at(c, h)] + d_rows[h] * xs[c][h] for h in range(N_HEADS)] for c in range(nq)]
    return ys, states


def _ssd_chunks_bwd(xs, bg, cg, dt_raw_t, prev, dtb, alog, d_rows, dys, dh_last, consts):
    causal, upper, last, head = consts
    nq = len(xs)
    items = [(c, h) for c in range(nq) for h in range(N_HEADS)]
    ni = len(items)
    at = lambda c, h: c * N_HEADS + h
    groups = [(c, g) for c in range(nq) for g in range(2)]
    lane = _lane_iota((BLK, BLK))
    lane_row = _lane_iota((1, BLK))
    a_neg = -jnp.exp(alog)
    pre_dt = [dt_raw_t[c] + dtb for c in range(nq)]
    dt_t = [_softplus(pre_dt[c]) for c in range(nq)]
    acs_t = [_mm_hi(dt_t[c] * a_neg, upper) for c in range(nq)]
    pick = lambda t, h: jnp.sum(jnp.where(head == h, t, 0.0), axis=0, keepdims=True)
    full_sum = lambda t: jnp.sum(jnp.sum(t, axis=1, keepdims=True), axis=0, keepdims=True)
    dt_row = [pick(dt_t[c], h) for c, h in items]
    a_row = [pick(acs_t[c], h) for c, h in items]
    a_rb = [jnp.broadcast_to(a_row[n], (BLK, BLK)) for n in range(ni)]
    a_b = [a_rb[n].T for n in range(ni)]
    a_last = [jnp.sum(jnp.where(last, a_b[n], 0.0), axis=0, keepdims=True) for n in range(ni)]
    lm = [jnp.exp(jnp.where(causal, a_b[n] - a_rb[n], -1e30)) for n in range(ni)]
    cgb = [[cg[c][g].astype(MXU_DTYPE) for g in range(2)] for c in range(nq)]
    bgb = [[bg[c][g].astype(MXU_DTYPE) for g in range(2)] for c in range(nq)]
    cb = [[_mm_nt(cgb[c][g], bgb[c][g]) for g in range(2)] for c in range(nq)]
    u = [cb[c][h // 4] * lm[at(c, h)] for c, h in items]
    w = [(u[n] * dt_row[n]).astype(MXU_DTYPE) for n in range(ni)]
    e_row = [jnp.exp(a_last[n] - a_row[n]) for n in range(ni)]
    f_row = [dt_row[n] * e_row[n] for n in range(ni)]
    f_b = [jnp.broadcast_to(f_row[n], (BLK, BLK)).T for n in range(ni)]
    e_b = [jnp.exp(a_b[n]) for n in range(ni)]
    el = [jnp.exp(a_last[n]) for n in range(ni)]
    xb = [xs[c][h].astype(MXU_DTYPE) for c, h in items]
    dyb = [dys[c][h].astype(MXU_DTYPE) for c, h in items]
    prevb = [prev[c][h].astype(MXU_DTYPE) for c, h in items]
    gmat = [_mm(cgb[c][h // 4], prevb[at(c, h)]) for c, h in items]
    dw = [_mm_nt(dyb[n], xb[n]) for n in range(ni)]
    dg = [dys[c][h] * e_b[at(c, h)] for c, h in items]
    dgb = [dg[n].astype(MXU_DTYPE) for n in range(ni)]
    from_y = [_mm_tn(cgb[c][h // 4], dgb[at(c, h)]) for c, h in items]
    dhs = [None] * ni
    dprev = [None] * ni
    for c in reversed(range(nq)):
        for h in range(N_HEADS):
            dhs[at(c, h)] = dh_last[h] if c == nq - 1 else dprev[at(c + 1, h)]
            dprev[at(c, h)] = from_y[at(c, h)] + dhs[at(c, h)] * el[at(c, h)]
    dstb = [dhs[n].astype(MXU_DTYPE) for n in range(ni)]
    dxf = [_mm(bgb[c][h // 4], dstb[at(c, h)]) for c, h in items]
    xfb = [(xs[c][h] * f_b[at(c, h)]).astype(MXU_DTYPE) for c, h in items]
    dxs = [_mm_tn(w[at(c, h)], dyb[at(c, h)]) + d_rows[h] * dys[c][h] + f_b[at(c, h)] * dxf[at(c, h)]
           for c, h in items]
    dd_item = [jnp.sum(dys[c][h] * xs[c][h], axis=0, keepdims=True) for c, h in items]
    dcg_h = [_mm_nt(dgb[n], prevb[n]) for n in range(ni)]
    dbg_h = [_mm_nt(xfb[n], dstb[n]) for n in range(ni)]
    zt = [dw[n] * u[n] for n in range(ni)]
    dseg = [zt[n] * dt_row[n] for n in range(ni)]
    dcb_h = [dw[n] * lm[n] * dt_row[n] for n in range(ni)]
    four = lambda lst, c, g: lst[at(c, 4 * g)] + lst[at(c, 4 * g + 1)] + lst[at(c, 4 * g + 2)] + lst[at(c, 4 * g + 3)]
    dcb = {(c, g): four(dcb_h, c, g).astype(MXU_DTYPE) for c, g in groups}
    dcg = [[four(dcg_h, c, g) + _mm(dcb[c, g], bgb[c][g]) for g in range(2)] for c in range(nq)]
    dbg = [[four(dbg_h, c, g) + _mm_tn(dcb[c, g], cgb[c][g]) for g in range(2)] for c in range(nq)]
    r1 = [jnp.sum(dg[n] * gmat[n] + dseg[n], axis=1, keepdims=True) for n in range(ni)]
    r2 = [jnp.sum(dxf[at(c, h)] * xs[c][h], axis=1, keepdims=True) for c, h in items]
    tt = [jnp.where(lane < HALF, jnp.broadcast_to(r1[n], (BLK, BLK)), jnp.broadcast_to(r2[n], (BLK, BLK))).T
          for n in range(ni)]
    r1_row = [tt[n][0:1, :] for n in range(ni)]
    r2_row = [tt[n][HALF:HALF + 1, :] for n in range(ni)]
    d_el = [full_sum(dhs[at(c, h)] * prev[c][h]) for c, h in items]
    da_last = [jnp.sum(r2_row[n] * f_row[n], axis=1, keepdims=True) + el[n] * d_el[n] for n in range(ni)]
    da_row = [r1_row[n] - jnp.sum(dseg[n], axis=0, keepdims=True) - r2_row[n] * f_row[n]
              + jnp.where(lane_row == BLK - 1, da_last[n], 0.0) for n in range(ni)]
    ddt_row = [jnp.sum(zt[n], axis=0, keepdims=True) + r2_row[n] * e_row[n] for n in range(ni)]
    draw, dalog = [], jnp.zeros((N_HEADS, BLK), F32)
    for c in range(nq):
        da_t = jnp.zeros((N_HEADS, BLK), F32)
        ddt_t = jnp.zeros((N_HEADS, BLK), F32)
        for h in range(N_HEADS):
            da_t = jnp.where(head == h, da_row[at(c, h)], da_t)
            ddt_t = jnp.where(head == h, ddt_row[at(c, h)], ddt_t)
        d_dta = _mm_hi(da_t, causal.astype(F32))
        dalog = dalog + d_dta * dt_t[c] * a_neg
        draw.append((ddt_t + d_dta * a_neg) * jax.nn.sigmoid(pre_dt[c]))
    ddtb = draw[0]
    for c in range(1, nq):
        ddtb = ddtb + draw[c]
    dd_rows = []
    for h in range(N_HEADS):
        t = dd_item[at(0, h)]
        for c in range(1, nq):
            t = t + dd_item[at(c, h)]
        dd_rows.append(t)
    return ([dxs[c * N_HEADS:(c + 1) * N_HEADS] for c in range(nq)], dbg, dcg, draw,
            [dprev[at(0, h)] for h in range(N_HEADS)], ddtb, dalog, dd_rows)


def _dt_rows(dt_blk):
    return dt_blk.T[:N_HEADS]


def _conv_pre(halo, blk, cw_ref, cb_ref):
    ext = jnp.concatenate([halo, blk], axis=0)
    taps = [pltpu.roll(ext, 3 - k, 0)[8:] for k in range(3)] + [blk]
    pre = cb_ref[...] + cw_ref[0:1, :] * taps[0]
    for k in range(1, 4):
        pre = pre + cw_ref[k:k + 1, :] * taps[k]
    return pre


def _ssd_split(pre):
    heads = _split_heads(pre[:, :SSM_W], 4)
    pb = [pre[:, SSM_W + g * D_STATE:SSM_W + (g + 1) * D_STATE] for g in range(2)]
    pc = [pre[:, SSM_W + 2 * D_STATE + g * D_STATE:SSM_W + 2 * D_STATE + (g + 1) * D_STATE] for g in range(2)]
    return heads, pb, pc


def _ssd_fwd(xbc, dt_raw, conv_w, conv_b, dtb_row, alog_row, d_exp, xchg):
    S = xbc.shape[0]
    nc = S // BLK
    nq = SSD_CHUNKS_PER_STEP if nc % SSD_CHUNKS_PER_STEP == 0 else 1
    rows = nq * BLK

    def body(xbc_ref, halo_ref, dt_ref, cw_ref, cb_ref, dtb_ref, alog_ref, d_ref, y_ref, prev_ref, pre_ref, state_ref):
        i = pl.program_id(0)

        @pl.when(i == 0)
        def _():
            state_ref[...] = jnp.zeros_like(state_ref)

        halo = halo_ref[...] * jnp.where(i > 0, 1.0, 0.0)
        pre = _conv_pre(halo, xbc_ref[...], cw_ref, cb_ref)
        pre_ref[...] = pre
        xc = _silu(pre)
        split = [_ssd_split(xc[c * BLK:(c + 1) * BLK]) for c in range(nq)]
        dt_t = [_dt_rows(dt_ref[c * BLK:(c + 1) * BLK, :]) for c in range(nq)]
        prev0 = [state_ref[h] for h in range(N_HEADS)]
        d_rows = [d_ref[h:h + 1, :] for h in range(N_HEADS)]
        ys, states = _ssd_chunks([s[0] for s in split], [s[1] for s in split], [s[2] for s in split], dt_t, prev0,
                                 dtb_ref[...], alog_ref[...], d_rows, _ssd_consts())
        for h in range(N_HEADS):
            for c in range(nq):
                prev_ref[c, h] = states[c][h]
            state_ref[h] = states[nq][h]
        y_ref[...] = jnp.concatenate([_join_heads(ys[c]) for c in range(nq)], axis=0)

    vec = pl.BlockSpec((N_HEADS, LANE), _fixed)
    return _hosted_call(
        body, "ssd_fwd", nc // nq,
        in_specs=[pl.BlockSpec((rows, XBC_W), _row),
                  pl.BlockSpec((8, XBC_W), lambda i: (jnp.maximum(i * (rows // 8) - 1, 0), 0)),
                  pl.BlockSpec((rows, LANE), _row),
                  pl.BlockSpec((4, XBC_W), _fixed), pl.BlockSpec((1, XBC_W), _fixed), vec, vec,
                  pl.BlockSpec((N_HEADS, LANE), _fixed)],
        out_specs=[pl.BlockSpec((rows, SSM_W), _row),
                   pl.BlockSpec((nq, N_HEADS, D_STATE, LANE), lambda i: (i, 0, 0, 0)),
                   pl.BlockSpec((rows, XBC_W), _row)],
        out_shape=[jax.ShapeDtypeStruct((S, SSM_W), F32), jax.ShapeDtypeStruct((nc, N_HEADS, D_STATE, LANE), F32),
                   jax.ShapeDtypeStruct((S, XBC_W), F32)],
        scratch_shapes=[pltpu.VMEM((N_HEADS, D_STATE, LANE), F32)],
        args=(xbc, xbc, dt_raw, conv_w, conv_b, dtb_row, alog_row, d_exp), xchg=xchg, cparams=_cparams(),
    )


def _ssd_bwd(xbc, pre_act, dt_raw, prev_states, dy, conv_w, dtb_row, alog_row, d_exp, xchg):
    S = xbc.shape[0]
    nc = S // BLK
    nq = SSD_BWD_CHUNKS_PER_STEP if nc % SSD_BWD_CHUNKS_PER_STEP == 0 else 1
    rows, n_steps = nq * BLK, nc // nq

    def body(xbc_ref, halo_ref, pre_ref, dt_ref, prev_ref, dy_ref, cw_ref, dtb_ref, alog_ref, d_ref,
             dxbc_ref, ddt_ref, dcw_ref, dvec_ref, dd_ref, gstate_ref, ghalo_ref):
        i = pl.program_id(0)

        @pl.when(i == 0)
        def _():
            gstate_ref[...] = jnp.zeros_like(gstate_ref)
            ghalo_ref[...] = jnp.zeros_like(ghalo_ref)
            dcw_ref[...] = jnp.zeros_like(dcw_ref)
            dvec_ref[...] = jnp.zeros_like(dvec_ref)
            dd_ref[...] = jnp.zeros_like(dd_ref)

        halo = halo_ref[...] * jnp.where(i < n_steps - 1, 1.0, 0.0)
        ext = jnp.concatenate([halo, xbc_ref[...]], axis=0)
        pre = pre_ref[...]
        sig = jax.nn.sigmoid(pre)
        xc = pre * sig
        split = [_ssd_split(xc[c * BLK:(c + 1) * BLK]) for c in range(nq)]
        dt_t = [_dt_rows(dt_ref[c * BLK:(c + 1) * BLK, :]) for c in range(nq)]
        prev = [[prev_ref[c, h] for h in range(N_HEADS)] for c in range(nq)]
        d_rows = [d_ref[h:h + 1, :] for h in range(N_HEADS)]
        dys = [_split_heads(dy_ref[c * BLK:(c + 1) * BLK, :], 4) for c in range(nq)]
        dh_last = [gstate_ref[h] for h in range(N_HEADS)]
        dheads, dpb, dpc, ddt_t, dprev0, ddtb, dalog, dd_rows = _ssd_chunks_bwd(
            [s[0] for s in split], [s[1] for s in split], [s[2] for s in split], dt_t, prev, dtb_ref[...],
            alog_ref[...], d_rows, dys, dh_last, _ssd_consts())
        for h in range(N_HEADS):
            gstate_ref[h] = dprev0[h]
            dd_ref[h:h + 1, :] += dd_rows[h]
        pad = jnp.zeros((BLK - N_HEADS, BLK), F32)
        ddt_ref[...] = jnp.concatenate([jnp.concatenate([ddt_t[c], pad], axis=0).T for c in range(nq)],
                                       axis=0).astype(ddt_ref.dtype)
        dvec_ref[0:N_HEADS, :] += ddtb
        dvec_ref[N_HEADS:, :] += dalog
        dxc = jnp.concatenate([jnp.concatenate([_join_heads(dheads[c])] + list(dpb[c]) + list(dpc[c]), axis=1)
                               for c in range(nq)], axis=0)
        dpre = dxc * (sig * (1.0 + pre * (1.0 - sig)))
        zeros8 = jnp.zeros((8, XBC_W), F32)
        dpe = jnp.concatenate([zeros8, dpre, zeros8], axis=0)
        n_ext = 16 + rows
        shifted = [pltpu.roll(dpe, n_ext - (3 - k), 0)[:8 + rows] for k in range(3)] + [dpe[:8 + rows]]
        dext = cw_ref[0:1, :] * shifted[0]
        for k in range(1, 4):
            dext = dext + cw_ref[k:k + 1, :] * shifted[k]
        for k in range(4):
            dcw_ref[k:k + 1, :] += jnp.sum(shifted[k] * ext, axis=0, keepdims=True)
        dcw_ref[4:5, :] += jnp.sum(dpre, axis=0, keepdims=True)
        dxbc_ref[...] = jnp.concatenate([dext[8:rows], dext[rows:] + ghalo_ref[...]], axis=0).astype(dxbc_ref.dtype)
        ghalo_ref[...] = dext[:8, :]

    vec = pl.BlockSpec((N_HEADS, LANE), _fixed)
    rev = lambda i: (n_steps - 1 - i, 0)
    return _hosted_call(
        body, "ssd_bwd", n_steps,
        in_specs=[pl.BlockSpec((rows, XBC_W), rev),
                  pl.BlockSpec((8, XBC_W), lambda i: (jnp.maximum((n_steps - 1 - i) * (rows // 8) - 1, 0), 0)),
                  pl.BlockSpec((rows, XBC_W), rev),
                  pl.BlockSpec((rows, LANE), rev),
                  pl.BlockSpec((nq, N_HEADS, D_STATE, LANE), lambda i: (n_steps - 1 - i, 0, 0, 0)),
                  pl.BlockSpec((rows, SSM_W), rev),
                  pl.BlockSpec((4, XBC_W), _fixed), vec, vec,
                  pl.BlockSpec((N_HEADS, LANE), _fixed)],
        out_specs=[pl.BlockSpec((rows, XBC_W), rev), pl.BlockSpec((rows, LANE), rev),
                   pl.BlockSpec((8, XBC_W), _fixed), pl.BlockSpec((2 * N_HEADS, LANE), _fixed),
                   pl.BlockSpec((N_HEADS, LANE), _fixed)],
        out_shape=[jax.ShapeDtypeStruct((S, XBC_W), MXU_DTYPE), jax.ShapeDtypeStruct((S, LANE), MXU_DTYPE),
                   jax.ShapeDtypeStruct((8, XBC_W), F32), jax.ShapeDtypeStruct((2 * N_HEADS, LANE), F32),
                   jax.ShapeDtypeStruct((N_HEADS, LANE), F32)],
        scratch_shapes=[pltpu.VMEM((N_HEADS, D_STATE, LANE), F32), pltpu.VMEM((8, XBC_W), F32)],
        args=(xbc, xbc, pre_act, dt_raw, prev_states, dy, conv_w, dtb_row, alog_row, d_exp), xchg=xchg,
        cparams=_cparams(VMEM_BIG),
    )


def _adamw_math(w, g, m, v):
    m = ADAM_B1 * m + (1.0 - ADAM_B1) * g
    v = ADAM_B2 * v + (1.0 - ADAM_B2) * jnp.square(g)
    m_hat = m / (1.0 - ADAM_B1 ** ADAM_STEP)
    v_hat = v / (1.0 - ADAM_B2 ** ADAM_STEP)
    delta = -ADAM_LR * (m_hat / (jnp.sqrt(v_hat) + ADAM_EPS) + ADAM_WD * w)
    return delta, m, v


def _reduce_adamw(parts, w, m, v, name):
    R, C = w.shape

    def body(p_ref, w_ref, m_ref, v_ref, g_ref, d_ref, nm_ref, nv_ref):
        g = p_ref[0].astype(F32)
        for i in range(1, N_DEV):
            g = g + p_ref[i].astype(F32)
        d, nm, nv = _adamw_math(w_ref[...], g, m_ref[...], v_ref[...])
        g_ref[...] = g
        d_ref[...] = d
        nm_ref[...] = nm
        nv_ref[...] = nv

    if R % 16 == 0:
        tr = max(t for t in range(16, 257, 16) if R % t == 0)
        n, blk, pblk = R // tr, pl.BlockSpec((tr, C), _row), pl.BlockSpec((N_DEV, tr, C), lambda i: (0, i, 0))
    else:
        tl = 256
        n, blk, pblk = C // tl, pl.BlockSpec((R, tl), lambda i: (0, i)), pl.BlockSpec((N_DEV, R, tl),
                                                                                      lambda i: (0, 0, i))
    return pl.pallas_call(
        body, name=name, grid=(n,), in_specs=[pblk, blk, blk, blk],
        out_specs=[blk] * 4, out_shape=[jax.ShapeDtypeStruct((R, C), F32)] * 4,
    )(parts, w, m, v)


def _reduce_adamw_hosting(parts_list, wmv_list, name, xchg):
    n_arr = len(parts_list)
    C = wmv_list[0][0].shape[1]
    tl = 256

    def body(*refs):
        p_refs, wmv_refs, o_refs = refs[:n_arr], refs[n_arr:4 * n_arr], refs[4 * n_arr:]
        for k in range(n_arr):
            g = p_refs[k][0].astype(F32)
            for i in range(1, N_DEV):
                g = g + p_refs[k][i].astype(F32)
            w_ref, m_ref, v_ref = wmv_refs[3 * k:3 * k + 3]
            d, nm, nv = _adamw_math(w_ref[...], g, m_ref[...], v_ref[...])
            for o, val in zip(o_refs[4 * k:4 * k + 4], (g, d, nm, nv)):
                o[...] = val

    in_specs = [pl.BlockSpec((N_DEV, w.shape[0], tl), lambda i: (0, 0, i)) for w, _, _ in wmv_list]
    in_specs += [pl.BlockSpec((w.shape[0], tl), lambda i: (0, i)) for w, _, _ in wmv_list for _ in range(3)]
    out_specs = [pl.BlockSpec((w.shape[0], tl), lambda i: (0, i)) for w, _, _ in wmv_list for _ in range(4)]
    out_shape = [jax.ShapeDtypeStruct(w.shape, F32) for w, _, _ in wmv_list for _ in range(4)]
    args = list(parts_list) + [a for wmv in wmv_list for a in wmv]
    outs, x_out = _hosted_call(body, name, C // tl, in_specs, out_specs, out_shape, [], args, xchg,
                               _cparams(VMEM_BIG))
    return [outs[4 * k:4 * k + 4] for k in range(n_arr)], x_out


_SMALL_NAMES = ("ada_b", "norm1", "conv_w", "conv_b", "dt_bias", "A_log", "D_skip", "sinks", "attn_out_norm",
                "ssm_out_norm", "norm2", "rel_bias", "final_norm")
N_MOD = 6 * D_MODEL


def _mod_row(a0, a1, a2):
    return jnp.concatenate([a0[2:3], a0[1:2], a1[0:1], a2[2:3], a2[1:2], a2[3:4]], axis=1)


def _small_update(gathered, params):
    n_g = len(gathered)
    flat = [a for name in _SMALL_NAMES for a in params[name]]

    def body(*refs):
        a0_ref, a1_ref, a2_ref, cw_ref, dv_ref, dd_ref, ds_ref, dr_ref, c_ref = refs[:n_g]
        wmv = refs[n_g:n_g + len(flat)]
        outs = refs[n_g + len(flat):]

        def total(ref):
            t = ref[0]
            for i in range(1, N_DEV):
                t = t + ref[i]
            return t

        t0, t1, t2, tcw, tdv, tdd, tds, tdr = [total(r) for r in (a0_ref, a1_ref, a2_ref, cw_ref, dv_ref, dd_ref,
                                                                   ds_ref, dr_ref)]
        r8 = lax.broadcasted_iota(jnp.int32, (N_HEADS, LANE), 0)
        l8 = lax.broadcasted_iota(jnp.int32, (N_HEADS, LANE), 1)

        def diag_row(t):
            return jnp.sum(jnp.where(r8 == l8, t, 0.0), axis=0, keepdims=True)[:, :N_HEADS]

        def lane_sums(t):
            return diag_row(jnp.broadcast_to(jnp.sum(t, axis=1, keepdims=True), (N_HEADS, LANE)))

        me = _lin(_my_pos())
        n_cw = XBC_W // N_DEV
        cw_mine = jnp.zeros((4, n_cw), F32)
        for j in range(N_DEV):
            cw_mine = cw_mine + tcw[0:4, j * n_cw:(j + 1) * n_cw] * jnp.where(me == j, 1.0, 0.0)
        grads = {
            "ada_b": _mod_row(t0, t1, t2), "norm1": t0[0:1], "conv_w": cw_mine, "conv_b": tcw[4:5],
            "dt_bias": lane_sums(tdv[:N_HEADS]), "A_log": lane_sums(tdv[N_HEADS:]), "D_skip": lane_sums(tdd),
            "sinks": diag_row(tds), "attn_out_norm": t1[1:2, :ATTN_W], "ssm_out_norm": t1[1:2, ATTN_W:],
            "norm2": t2[0:1], "rel_bias": tdr[:, :N_HEADS], "final_norm": t2[4:5],
        }
        for k, name in enumerate(_SMALL_NAMES):
            w_ref, m_ref, v_ref = wmv[3 * k:3 * k + 3]
            g = grads[name]
            d, nm, nv = _adamw_math(w_ref[...], g, m_ref[...], v_ref[...])
            for o, val in zip(outs[4 * k:4 * k + 4], (g, d, nm, nv)):
                o[...] = val
        loss_ref, call_ref, dmod_ref = outs[4 * len(_SMALL_NAMES):]
        loss_ref[...] = t2[5:6, 0:1]
        call_ref[...] = jnp.concatenate([c_ref[i] for i in range(N_DEV)], axis=0)
        dmod_ref[...] = jnp.concatenate([_mod_row(a0_ref[i], a1_ref[i], a2_ref[i]) for i in range(N_DEV)], axis=0)

    out_shape = [jax.ShapeDtypeStruct(params[name][0].shape, F32) for name in _SMALL_NAMES for _ in range(4)]
    out_shape += [jax.ShapeDtypeStruct((1, 1), F32), jax.ShapeDtypeStruct((N_DEV, D_MODEL), F32),
                  jax.ShapeDtypeStruct((N_DEV, N_MOD), F32)]
    res = pl.pallas_call(body, name="small_update", out_shape=out_shape)(*gathered, *flat)
    upd = {name: res[4 * k:4 * k + 4] for k, name in enumerate(_SMALL_NAMES)}
    loss, c_all, dmod_all = res[4 * len(_SMALL_NAMES):]
    return upd, loss, c_all, dmod_all


def _ada_w_update(c_all, dmod_all, w, m, v):
    chunk = w.shape[1]

    def body(c_ref, dm_ref, w_ref, m_ref, v_ref, g_ref, d_ref, nm_ref, nv_ref):
        me = _lin(_my_pos())
        dm = jnp.zeros((N_DEV, chunk), F32)
        for j in range(N_DEV):
            dm = dm + dm_ref[:, j * chunk:(j + 1) * chunk] * jnp.where(me == j, 1.0, 0.0)
        g = lax.dot_general(_silu(c_ref[...]), dm, (((0,), (0,)), ((), ())), precision=HI,
                            preferred_element_type=F32)
        d, nm, nv = _adamw_math(w_ref[...], g, m_ref[...], v_ref[...])
        g_ref[...] = g
        d_ref[...] = d
        nm_ref[...] = nm
        nv_ref[...] = nv

    tr = 256
    blk = pl.BlockSpec((tr, chunk), _row)
    return pl.pallas_call(
        body, name="ada_w_update", grid=(w.shape[0] // tr,),
        in_specs=[pl.BlockSpec((N_DEV, tr), lambda i: (0, i)), pl.BlockSpec(dmod_all.shape, _fixed), blk, blk, blk],
        out_specs=[blk] * 4, out_shape=[jax.ShapeDtypeStruct(w.shape, F32)] * 4,
    )(c_all, dmod_all, w, m, v)


def _local_step(x, tgt, c, mod, w_in, conv_w, w_o_mine, w_gu_mine, w_d_mine, p):
    S = x.shape[0]
    tm = min(512, S)
    tmm = min(256, S)
    tw = min(2048, S)
    shift1, scale1, gate1, shift2, scale2, gate2 = [mod[i:i + 1] for i in range(6)]
    buckets = jnp.asarray(_t5_bucket_table())
    per_head = lambda a: jnp.broadcast_to(a.reshape(N_HEADS, 1), (N_HEADS, LANE))
    dtb_row, alog_row, d_exp = per_head(p["dt_bias"]), per_head(p["A_log"]), per_head(p["D_skip"])
    sinks = p["sinks"].reshape(N_HEADS)

    d_cut, gu_cut = WD_CUT, WGU_CUTS
    (qkv, z, xbc, dt_raw), (g_d_a,) = _in_proj_fwd(x, p["norm1"], scale1, shift1, w_in, tm,
                                                   ([w_d_mine[:d_cut]], "two-level"))
    bias = _attn_bias(buckets, p["rel_bias"])
    (ya,), (g_gu_a,) = _attn_fwd(qkv, bias, sinks, ([w_gu_mine[:gu_cut[0]]], "two-level"))
    (ys, prev_states, pre_act), (g_gu_b, g_o) = _ssd_fwd(xbc, dt_raw, conv_w, p["conv_b"], dtb_row, alog_row, d_exp,
                                                ([w_gu_mine[gu_cut[0]:gu_cut[1]], w_o_mine], "two-level"))
    w_o = g_o.reshape(D_MODEL, D_MODEL)
    x1, (g_gu_c, g_d_b) = _out_proj_fwd(x, ya, ys, z, p["attn_out_norm"], p["ssm_out_norm"], gate1, w_o, tm,
                                        ([w_gu_mine[gu_cut[1]:], w_d_mine[d_cut:]], "two-level"))
    dx1, h2, dgu, act, dmlp, acc2 = _mlp_loss(x1, tgt, p["norm2"], scale2, shift2, gate2, p["final_norm"],
                                              (g_gu_a, g_gu_b, g_gu_c), (g_d_a, g_d_b), tmm)
    g_w_gu = _wgrad(dgu, h2, 2 * D_FF // 4, tw, "wgrad_gate_up")
    g_w_d = _wgrad(act, dmlp, D_FF // 2, tw, "wgrad_down")
    (dya, dys, dz, g_w_o, acc1), (r_d,) = _out_proj_bwd(
        dx1, ya, ys, z, p["attn_out_norm"], p["ssm_out_norm"], gate1, w_o, tm,
        ([g_w_d.reshape(N_DEV, D_FF // N_DEV, D_MODEL)], True))
    (dq, dkv, dbias, dsk), (r_o,) = _attn_bwd(qkv, ya, dya, bias, sinks,
                                              ([g_w_o.reshape(N_DEV, D_MODEL // N_DEV, D_MODEL)], True))
    drel, dsink = _attn_finish(dbias, dsk, buckets)
    (dxbc, ddt, dcw, dvec, dd), (r_gu,) = _ssd_bwd(
        xbc, pre_act, dt_raw, prev_states, dys, conv_w, dtb_row, alog_row, d_exp,
        ([g_w_gu.reshape(N_DEV, 2 * D_FF // N_DEV, D_MODEL)], True))
    gx, h1, acc0, g_in_a = _in_proj_bwd(x, dx1, dq, dkv, dz, dxbc, ddt, p["norm1"], scale1, shift1, w_in, tm)
    half = D_MODEL // 2
    slots = lambda g: g[:IN_W].reshape(N_DEV, IN_W // N_DEV, half)
    g_in_b, exchanged = _wgrad((dq, dkv, dz, dxbc, ddt), h1, IN_PAD, tw, "wgrad_in_b",
                               [([slots(g_in_a)], True), ([acc0, acc1, acc2, dcw, dvec, dd, dsink, drel, c], False)],
                               g_cols=(half, 1))
    return gx, (exchanged[0], slots(g_in_b)), (r_o, r_gu, r_d), exchanged[1:]


def kernel(x, c, ada_w, ada_b, norm1, w_in, conv_w, conv_b, dt_bias, A_log, D_skip, sinks, attn_out_norm, ssm_out_norm, w_o, norm2, w_gate_up, w_down, rel_bias, final_norm, loss_target, m_ada_w, m_ada_b, m_norm1, m_w_in, m_conv_w, m_conv_b, m_dt_bias, m_A_log, m_D_skip, m_sinks, m_attn_out_norm, m_ssm_out_norm, m_w_o, m_norm2, m_w_gate_up, m_w_down, m_rel_bias, m_final_norm, v_ada_w, v_ada_b, v_norm1, v_w_in, v_conv_w, v_conv_b, v_dt_bias, v_A_log, v_D_skip, v_sinks, v_attn_out_norm, v_ssm_out_norm, v_w_o, v_norm2, v_w_gate_up, v_w_down, v_rel_bias, v_final_norm):
    two_d = lambda a: a if a.ndim == 2 else a.reshape(-1, a.shape[-1])
    small_params = dict(
        ada_b=(ada_b, m_ada_b, v_ada_b), norm1=(norm1, m_norm1, v_norm1), conv_w=(conv_w, m_conv_w, v_conv_w),
        conv_b=(conv_b, m_conv_b, v_conv_b), dt_bias=(dt_bias, m_dt_bias, v_dt_bias), A_log=(A_log, m_A_log, v_A_log),
        D_skip=(D_skip, m_D_skip, v_D_skip), sinks=(sinks, m_sinks, v_sinks),
        attn_out_norm=(attn_out_norm, m_attn_out_norm, v_attn_out_norm),
        ssm_out_norm=(ssm_out_norm, m_ssm_out_norm, v_ssm_out_norm), norm2=(norm2, m_norm2, v_norm2),
        rel_bias=(rel_bias, m_rel_bias, v_rel_bias), final_norm=(final_norm, m_final_norm, v_final_norm))
    small_params = {k: tuple(two_d(a) for a in v) for k, v in small_params.items()}
    S = x.shape[1]
    xs, tgt = x.reshape(S, D_MODEL), loss_target.reshape(S, D_MODEL)
    ada_w2 = ada_w[0]
    chunk = ada_w2.shape[1]
    t_in = [jnp.transpose(a[0]) for a in (w_in, m_w_in, v_w_in)]
    t_gu = [jnp.transpose(a[0]) for a in (w_gate_up, m_w_gate_up, v_w_gate_up)]

    mod, (g_in, g_cw) = _mod_and_gather(c, ada_w2, ada_b.reshape(N_DEV, chunk), [t_in[0].astype(WIRE_DTYPE), conv_w[0]])
    mod = mod.reshape(6, D_MODEL)
    w_in_full = jnp.pad(g_in.reshape(IN_W, D_MODEL), ((0, IN_PAD - IN_W), (0, 0)))
    conv_w_full = jnp.transpose(g_cw, (1, 0, 2)).reshape(4, XBC_W)

    p = {k: v[0] for k, v in small_params.items()}
    gx, (r_in_a, gw_in_b), (r_o, r_gu, r_d), gathered = _local_step(
        xs, tgt, c, mod, w_in_full, conv_w_full, w_o[0].astype(WIRE_DTYPE), t_gu[0].astype(WIRE_DTYPE),
        w_down[0].astype(WIRE_DTYPE), p)

    (u_gu, u_d, u_o), (r_in_b,) = _reduce_adamw_hosting(
        [r_gu, r_d, r_o], [tuple(t_gu), (w_down[0], m_w_down[0], v_w_down[0]), (w_o[0], m_w_o[0], v_w_o[0])],
        "adamw_big", ([gw_in_b], True))
    r_in = jnp.concatenate([r_in_a, r_in_b], axis=2)

    small, loss, c_all, dmod_all = _small_update(gathered, small_params)

    big = {
        "ada_w": _ada_w_update(c_all, dmod_all, ada_w2, m_ada_w[0], v_ada_w[0]),
        "w_in": [jnp.transpose(a) for a in _reduce_adamw(r_in, *t_in, "adamw_w_in")],
        "w_o": u_o,
        "w_gate_up": [jnp.transpose(a) for a in u_gu],
        "w_down": u_d,
    }
    big.update(small)

    order = ['ada_w', 'ada_b', 'norm1', 'w_in', 'conv_w', 'conv_b', 'dt_bias', 'A_log', 'D_skip', 'sinks',
             'attn_out_norm', 'ssm_out_norm', 'w_o', 'norm2', 'w_gate_up', 'w_down', 'rel_bias', 'final_norm']
    shapes = dict(ada_w=ada_w.shape, ada_b=ada_b.shape, norm1=norm1.shape, w_in=w_in.shape, conv_w=conv_w.shape,
                  conv_b=conv_b.shape, dt_bias=dt_bias.shape, A_log=A_log.shape, D_skip=D_skip.shape,
                  sinks=sinks.shape, attn_out_norm=attn_out_norm.shape, ssm_out_norm=ssm_out_norm.shape,
                  w_o=w_o.shape, norm2=norm2.shape, w_gate_up=w_gate_up.shape, w_down=w_down.shape,
                  rel_bias=rel_bias.shape, final_norm=final_norm.shape)
    outs = [[], [], [], []]
    for name in order:
        for kind in range(4):
            outs[kind].append(big[name][kind].reshape(shapes[name]))
    return (loss.reshape(()), gx.reshape(x.shape), *outs[0], *outs[1], *outs[2], *outs[3])
```

```python
import numpy as np
import jax
import jax.numpy as jnp
from jax import lax
from jax.experimental import pallas as pl
from jax.experimental.pallas import tpu as pltpu

F32 = jnp.float32
MXU_DTYPE = jnp.bfloat16
WIRE_DTYPE = jnp.bfloat16
HI = lax.Precision.HIGHEST
MESH = pl.DeviceIdType.MESH
N_DEV = 8

D_MODEL = 1024
ATTN_W = 512
KV_W = 128
SSM_W = 512
XBC_W = 1024
N_HEADS = 8
D_STATE = 128
D_FF = 2816
IN_W = 2312
IN_PAD = 2432
BLK = 128
N_BUCKETS = 32
EPS = 1e-6
LANE = 128
HALF = 64

ADAM_LR, ADAM_B1, ADAM_B2, ADAM_EPS, ADAM_WD, ADAM_STEP = 0.001, 0.9, 0.999, 1e-08, 0.01, 10

VMEM_BIG = 56 * 1024 * 1024
WD_CUT = 288
WGU_CUTS = (240, 496)
GGU_CUT = 304


def _cparams(vmem=None):
    if vmem is None:
        return pltpu.CompilerParams()
    return pltpu.CompilerParams(vmem_limit_bytes=vmem)


def _mm(a, b):
    return jnp.dot(a.astype(MXU_DTYPE), b.astype(MXU_DTYPE), preferred_element_type=F32)


def _mm_nt(a, b):
    return lax.dot_general(a.astype(MXU_DTYPE), b.astype(MXU_DTYPE), (((1,), (1,)), ((), ())),
                           preferred_element_type=F32)


def _mm_tn(a, b):
    return lax.dot_general(a.astype(MXU_DTYPE), b.astype(MXU_DTYPE), (((0,), (0,)), ((), ())),
                           preferred_element_type=F32)


def _mm_hi(a, b):
    return jnp.dot(a, b, precision=HI, preferred_element_type=F32)


def _silu(x):
    return x * jax.nn.sigmoid(x)


def _softplus(x):
    return jnp.maximum(x, 0.0) + jnp.log1p(jnp.exp(-jnp.abs(x)))


def _rms(x, g, n):
    return x * lax.rsqrt(jnp.sum(x * x, axis=-1, keepdims=True) * (1.0 / n) + EPS) * g


def _modnorm(x, g, scale, shift):
    return _rms(x, g, x.shape[-1]) * (1.0 + scale) + shift


def _modnorm_parts(x):
    r = lax.rsqrt(jnp.sum(x * x, axis=-1, keepdims=True) * (1.0 / x.shape[-1]) + EPS)
    return r, x * r


def _modnorm_bwd(r, xhat, g, scale, dy):
    dyg = dy * (g * (1.0 + scale))
    c = jnp.sum(dyg * xhat, axis=-1, keepdims=True) * (1.0 / xhat.shape[-1])
    dx = r * (dyg - xhat * c)
    ct = jnp.sum(dy * xhat, axis=0, keepdims=True)
    return dx, ct * (1.0 + scale), ct * g, jnp.sum(dy, axis=0, keepdims=True)


def _lane_iota(shape):
    return lax.broadcasted_iota(jnp.int32, shape, len(shape) - 1)


def _split_pair(t):
    lane = _lane_iota(t.shape)
    lo = jnp.where(lane < HALF, t, 0.0)
    hi = pltpu.roll(jnp.where(lane >= HALF, t, 0.0), HALF, 1)
    return lo, hi


def _join_pair(lo, hi):
    lane = _lane_iota(lo.shape)
    return jnp.where(lane < HALF, lo, pltpu.roll(hi, HALF, 1))


def _split_heads(t, n_pairs):
    out = []
    for p in range(n_pairs):
        out.extend(_split_pair(t[:, p * LANE:(p + 1) * LANE]))
    return out


def _join_heads(hs):
    return jnp.concatenate([_join_pair(hs[2 * p], hs[2 * p + 1]) for p in range(len(hs) // 2)], axis=1)


def _t5_bucket_table():
    dist = np.arange(BLK)[:, None] + BLK - np.arange(2 * BLK)[None, :]
    n = np.maximum(dist, 0)
    max_exact = N_BUCKETS // 2
    large = max_exact + (np.log(np.maximum(n, 1) / max_exact) / np.log(128 / max_exact)
                         * (N_BUCKETS - max_exact)).astype(np.int32)
    large = np.minimum(large, N_BUCKETS - 1)
    return np.where(n < max_exact, n, large).astype(np.int32)


def _my_pos():
    return lax.axis_index("x"), lax.axis_index("y"), lax.axis_index("c")


def _peer(k):
    x, y, c = _my_pos()
    return (1 - x if k & 4 else x, 1 - y if k & 2 else y, 1 - c if k & 1 else c)


def _lin(pos):
    return 4 * pos[0] + 2 * pos[1] + pos[2]


def _xchg_copies(ins, outs, sems, scatter):
    local_sem, send_sem, recv_sem = sems
    me = _lin(_my_pos())

    def source(a, slot):
        if not scatter:
            return ins[a]
        if scatter is True:
            return ins[a].at[slot]
        return ins[a].at[slot, pl.ds(scatter[1], scatter[2])]

    local, remote = [], []
    for a in range(len(ins)):
        local.append(pltpu.make_async_copy(source(a, me), outs[a].at[me], local_sem.at[a]))
    for k in range(1, N_DEV):
        peer = _peer(k)
        for a in range(len(ins)):
            remote.append(pltpu.make_async_remote_copy(source(a, _lin(peer)), outs[a].at[me], send_sem.at[a, k - 1],
                                                       recv_sem.at[a, k - 1], device_id=peer, device_id_type=MESH))
    return local, remote


def _xchg_start(ins, outs, sems, scatter):
    local, remote = _xchg_copies(ins, outs, sems, scatter)
    for cp in local + remote:
        cp.start()


def _xchg_wait(ins, outs, sems, scatter):
    local, remote = _xchg_copies(ins, outs, sems, scatter)
    for cp in local:
        cp.wait()
    for cp in remote:
        cp.wait_send()
        cp.wait_recv()


def _xchg_shapes(arrs, scatter):
    n = len(arrs)
    if isinstance(scatter, tuple):
        out_shape = [jax.ShapeDtypeStruct((a.shape[0], scatter[2]) + a.shape[2:], a.dtype) for a in arrs]
    elif scatter:
        out_shape = [jax.ShapeDtypeStruct(a.shape, a.dtype) for a in arrs]
    else:
        out_shape = [jax.ShapeDtypeStruct((N_DEV,) + a.shape, a.dtype) for a in arrs]
    sems = [pltpu.SemaphoreType.DMA((n,)), pltpu.SemaphoreType.DMA((n, N_DEV - 1)),
            pltpu.SemaphoreType.DMA((n, N_DEV - 1))]
    return out_shape, sems


_CHIPS = (2, 4, 6)


def _g2_sems(n):
    dma = pltpu.SemaphoreType.DMA
    return [dma((n,)), dma((n, N_DEV)), dma((n, N_DEV)), dma((n, len(_CHIPS))), dma((n, len(_CHIPS)))]


class _TwoLevelGather:
    def __init__(self, ins, outs, sems):
        self.ins, self.outs = ins, outs
        self.local_sem, self.send_sem, self.recv_sem, self.fsend_sem, self.frecv_sem = sems
        self.n = len(ins)

    def _direct(self, a, k):
        return pltpu.make_async_remote_copy(self.ins[a], self.outs[a].at[_lin(_my_pos())], self.send_sem.at[a, k],
                                            self.recv_sem.at[a, k], device_id=_peer(k), device_id_type=MESH)

    def _handed_on(self, a, j, origin):
        slot = self.outs[a].at[origin]
        return pltpu.make_async_remote_copy(slot, slot, self.fsend_sem.at[a, j], self.frecv_sem.at[a, j],
                                            device_id=_peer(1), device_id_type=MESH)

    def _local(self, a):
        return pltpu.make_async_copy(self.ins[a], self.outs[a].at[_lin(_my_pos())], self.local_sem.at[a])

    def start(self):
        for a in range(self.n):
            self._local(a).start()
        for k in (1,) + _CHIPS:
            for a in range(self.n):
                self._direct(a, k).start()

    def forward(self):
        for j, k in enumerate(_CHIPS):
            for a in range(self.n):
                self._direct(a, k).wait_recv()
                self._handed_on(a, j, _lin(_peer(k))).start()

    def finish(self):
        for a in range(self.n):
            self._direct(a, 1).wait_recv()
            for j, k in enumerate(_CHIPS):
                self._handed_on(a, j, _lin(_peer(k ^ 1))).wait_recv()
            self._local(a).wait()
            for k in (1,) + _CHIPS:
                self._direct(a, k).wait_send()
            for j, k in enumerate(_CHIPS):
                self._handed_on(a, j, _lin(_peer(k))).wait_send()


def _mod_and_gather(c, ada_w, ada_b8, arrs):
    n = len(arrs)
    chunk = ada_w.shape[1]
    out_shape = [jax.ShapeDtypeStruct((N_DEV, 1, chunk), F32)]
    out_shape += [jax.ShapeDtypeStruct((N_DEV,) + a.shape, a.dtype) for a in arrs]

    def modulation(c_ref, w_ref, b_ref, out_ref, cbuf, part, s1, r1, s2, r2):
        me = _lin(_my_pos())
        first = []
        for k in range(1, N_DEV):
            cp = pltpu.make_async_remote_copy(c_ref, cbuf.at[me], s1.at[k - 1], r1.at[k - 1],
                                              device_id=_peer(k), device_id_type=MESH)
            cp.start()
            first.append(cp)
        cbuf[me] = c_ref[...]
        for cp in first:
            cp.wait_send()
            cp.wait_recv()
        cond = _silu(jnp.concatenate([cbuf[i] for i in range(N_DEV)], axis=0))
        mod = _mm_hi(cond, w_ref[...]) + b_ref[pl.ds(me, 1), :]
        for j in range(N_DEV):
            part[j] = mod[j:j + 1, :]
        second = []
        for k in range(1, N_DEV):
            peer = _peer(k)
            cp = pltpu.make_async_remote_copy(part.at[_lin(peer)], out_ref.at[me], s2.at[k - 1], r2.at[k - 1],
                                              device_id=peer, device_id_type=MESH)
            cp.start()
            second.append(cp)
        out_ref[me] = part[me]
        for cp in second:
            cp.wait_send()
            cp.wait_recv()

    def body(*refs):
        c_ref, w_ref, b_ref = refs[:3]
        ins = refs[3:3 + n]
        mod_ref = refs[3 + n]
        outs = refs[4 + n:4 + 2 * n]
        cbuf, part, s1, r1, s2, r2 = refs[4 + 2 * n:10 + 2 * n]
        gather = _TwoLevelGather(ins, outs, refs[10 + 2 * n:])
        gather.start()
        modulation(c_ref, w_ref, b_ref, mod_ref, cbuf, part, s1, r1, s2, r2)
        gather.forward()
        gather.finish()

    hbm = pl.BlockSpec(memory_space=pltpu.HBM)
    vm = pl.BlockSpec(memory_space=pltpu.VMEM)
    dma = pltpu.SemaphoreType.DMA
    res = pl.pallas_call(
        body, name="mod_and_gather", out_shape=out_shape, in_specs=[vm, vm, vm] + [hbm] * n,
        out_specs=[vm] + [hbm] * n,
        scratch_shapes=[pltpu.VMEM((N_DEV, 1, D_MODEL), F32), pltpu.VMEM((N_DEV, 1, chunk), F32)]
        + [dma((N_DEV - 1,))] * 4 + _g2_sems(n),
    )(c, ada_w, ada_b8, *arrs)
    return res[0], res[1:]


def _hosted_call(body, name, grid, in_specs, out_specs, out_shape, scratch_shapes, args, xchg, cparams):
    xchgs = [xchg] if isinstance(xchg, tuple) else list(xchg)
    grid = (grid,) if isinstance(grid, int) else tuple(grid)
    n_in, n_out, n_scr = len(in_specs), len(out_specs), len(scratch_shapes)
    arrs = [a for group, _ in xchgs for a in group]
    n = len(arrs)
    x_shape, x_sems, sem_counts = [], [], []
    for group, mode in xchgs:
        shapes, sems = _xchg_shapes(group, False if mode == "two-level" else mode)
        if mode == "two-level":
            sems = _g2_sems(len(group))
        x_shape += shapes
        x_sems += sems
        sem_counts.append(len(sems))
    n_steps = int(np.prod(grid))

    def hosted(*refs):
        ins, refs = refs[:n_in], refs[n_in:]
        x_in, refs = refs[:n], refs[n:]
        outs, refs = refs[:n_out], refs[n_out:]
        x_out, refs = refs[:n], refs[n:]
        scr, sems = refs[:n_scr], refs[n_scr:]
        step = pl.program_id(0)
        for d in range(1, len(grid)):
            step = step * grid[d] + pl.program_id(d)
        parts, a0, s0 = [], 0, 0
        for (group, mode), ns in zip(xchgs, sem_counts):
            parts.append((x_in[a0:a0 + len(group)], x_out[a0:a0 + len(group)], sems[s0:s0 + ns], mode))
            a0, s0 = a0 + len(group), s0 + ns

        @pl.when(step == 0)
        def _():
            for gi, go, gs, mode in parts:
                if mode == "two-level":
                    _TwoLevelGather(gi, go, gs).start()
                else:
                    _xchg_start(gi, go, gs, mode)

        if any(mode == "two-level" for _, mode in xchgs):
            @pl.when(step == (2 * n_steps) // 3)
            def _():
                for gi, go, gs, mode in parts:
                    if mode == "two-level":
                        _TwoLevelGather(gi, go, gs).forward()

        body(*ins, *outs, *scr)

        @pl.when(step == n_steps - 1)
        def _():
            for gi, go, gs, mode in parts:
                if mode == "two-level":
                    _TwoLevelGather(gi, go, gs).finish()
                else:
                    _xchg_wait(gi, go, gs, mode)

    hbm = pl.BlockSpec(memory_space=pltpu.HBM)
    res = pl.pallas_call(
        hosted, name=name, grid=grid, in_specs=list(in_specs) + [hbm] * n,
        out_specs=list(out_specs) + [hbm] * n, out_shape=list(out_shape) + x_shape,
        scratch_shapes=list(scratch_shapes) + x_sems, compiler_params=cparams,
    )(*args, *arrs)
    return res[:n_out], res[n_out:]


def _row(i):
    return (i, 0)


def _fixed(i):
    return (0, 0)


def _in_proj_fwd(x, norm1, scale1, shift1, w_in, tm, xchg):
    S = x.shape[0]

    def body(x_ref, n_ref, sc_ref, sh_ref, w_ref, qkv_ref, z_ref, xbc_ref, dt_ref):
        h = _modnorm(x_ref[...], n_ref[...], sc_ref[...], sh_ref[...])
        p = _mm_nt(h, w_ref[...])
        qkv_ref[...] = p[:, :768].astype(qkv_ref.dtype)
        z_ref[...] = p[:, 768:1280]
        xbc_ref[...] = p[:, 1280:2304]
        dt_ref[...] = p[:, 2304:IN_PAD]

    vec = pl.BlockSpec((1, D_MODEL), _fixed)
    return _hosted_call(
        body, "in_proj_fwd", S // tm,
        in_specs=[pl.BlockSpec((tm, D_MODEL), _row), vec, vec, vec, pl.BlockSpec((IN_PAD, D_MODEL), _fixed)],
        out_specs=[pl.BlockSpec((tm, 768), _row), pl.BlockSpec((tm, SSM_W), _row),
                   pl.BlockSpec((tm, XBC_W), _row), pl.BlockSpec((tm, LANE), _row)],
        out_shape=[jax.ShapeDtypeStruct((S, 768), MXU_DTYPE), jax.ShapeDtypeStruct((S, SSM_W), F32),
                   jax.ShapeDtypeStruct((S, XBC_W), F32), jax.ShapeDtypeStruct((S, LANE), F32)],
        scratch_shapes=[], args=(x, norm1, scale1, shift1, w_in), xchg=xchg, cparams=_cparams(VMEM_BIG),
    )


def _in_proj_bwd(x, dx1, dq, dkv, dz, dxbc, ddt, norm1, scale1, shift1, w_in, tm):
    S = x.shape[0]

    n_steps = S // tm
    half_cols = D_MODEL // 2

    def body(x_ref, dx1_ref, dq_ref, dkv_ref, dz_ref, dxbc_ref, ddt_ref, n_ref, sc_ref, sh_ref, w_ref,
             gx_ref, h_ref, acc_ref, gw_ref, gw_acc):
        i = pl.program_id(0)

        @pl.when(i == 0)
        def _():
            acc_ref[...] = jnp.zeros_like(acc_ref)
            gw_acc[...] = jnp.zeros_like(gw_acc)

        halves = [pl.ds(k * (tm // 2), tm // 2) for k in range(2)]
        dp = [jnp.concatenate([r[rows, :] for r in (dq_ref, dkv_ref, dz_ref, dxbc_ref, ddt_ref)], axis=1)
              for rows in halves]
        dh = [_mm(dp[k], w_ref[...]) for k in range(2)]
        parts = [_modnorm_parts(x_ref[rows, :]) for rows in halves]
        hb = [(parts[k][1] * n_ref[...] * (1.0 + sc_ref[...]) + sh_ref[...]).astype(h_ref.dtype) for k in range(2)]
        gw_acc[...] += _mm_tn(dp[0], hb[0][:, :half_cols]) + _mm_tn(dp[1], hb[1][:, :half_cols])
        bwd = [_modnorm_bwd(parts[k][0], parts[k][1], n_ref[...], sc_ref[...], dh[k]) for k in range(2)]
        for k, rows in enumerate(halves):
            gx_ref[rows, :] = dx1_ref[rows, :] + bwd[k][0]
            h_ref[rows, :] = hb[k]
        acc_ref[0:1, :] += bwd[0][1] + bwd[1][1]
        acc_ref[1:2, :] += bwd[0][2] + bwd[1][2]
        acc_ref[2:3, :] += bwd[0][3] + bwd[1][3]

        @pl.when(i == n_steps - 1)
        def _():
            gw_ref[...] = gw_acc[...].astype(gw_ref.dtype)

    vec = pl.BlockSpec((1, D_MODEL), _fixed)
    return pl.pallas_call(
        body, name="in_proj_bwd", grid=(n_steps,),
        in_specs=[pl.BlockSpec((tm, D_MODEL), _row), pl.BlockSpec((tm, D_MODEL), _row),
                  pl.BlockSpec((tm, ATTN_W), _row), pl.BlockSpec((tm, 2 * KV_W), _row),
                  pl.BlockSpec((tm, SSM_W), _row), pl.BlockSpec((tm, XBC_W), _row), pl.BlockSpec((tm, LANE), _row),
                  vec, vec, vec, pl.BlockSpec((IN_PAD, D_MODEL), _fixed)],
        out_specs=[pl.BlockSpec((tm, D_MODEL), _row), pl.BlockSpec((tm, D_MODEL), _row),
                   pl.BlockSpec((8, D_MODEL), _fixed), pl.BlockSpec((IN_PAD, half_cols), _fixed)],
        out_shape=[jax.ShapeDtypeStruct((S, D_MODEL), F32), jax.ShapeDtypeStruct((S, D_MODEL), MXU_DTYPE),
                   jax.ShapeDtypeStruct((8, D_MODEL), F32), jax.ShapeDtypeStruct((IN_PAD, half_cols), WIRE_DTYPE)],
        scratch_shapes=[pltpu.VMEM((IN_PAD, half_cols), F32)],
        compiler_params=_cparams(VMEM_BIG),
    )(x, dx1, dq, dkv, dz, dxbc, ddt, norm1, scale1, shift1, w_in)


def _out_stage(ya, ys0, ys1, z0, z1, an, sn0, sn1):
    half = SSM_W // 2
    a = _rms(ya, an, ATTN_W)
    g0 = _rms(ys0 * _silu(z0), sn0, half)
    g1 = _rms(ys1 * _silu(z1), sn1, half)
    return jnp.concatenate([a, g0, g1], axis=1)


def _out_stage_args(ya_ref, ys_ref, z_ref, an_ref, sn_ref):
    half = SSM_W // 2
    return (ya_ref[...], ys_ref[:, :half], ys_ref[:, half:], z_ref[:, :half], z_ref[:, half:],
            an_ref[...], sn_ref[:, :half], sn_ref[:, half:])


def _out_proj_fwd(x, ya, ys, z, an, sn, gate1, w_o, tm, xchg):
    S = x.shape[0]

    def body(x_ref, ya_ref, ys_ref, z_ref, an_ref, sn_ref, g_ref, w_ref, x1_ref):
        u = _out_stage(*_out_stage_args(ya_ref, ys_ref, z_ref, an_ref, sn_ref))
        x1_ref[...] = x_ref[...] + g_ref[...] * _mm(u, w_ref[...])

    half = pl.BlockSpec((tm, ATTN_W), _row)
    hvec = pl.BlockSpec((1, ATTN_W), _fixed)
    (x1,), x_out = _hosted_call(
        body, "out_proj_fwd", S // tm,
        in_specs=[pl.BlockSpec((tm, D_MODEL), _row), half, half, half, hvec, hvec,
                  pl.BlockSpec((1, D_MODEL), _fixed), pl.BlockSpec((D_MODEL, D_MODEL), _fixed)],
        out_specs=[pl.BlockSpec((tm, D_MODEL), _row)],
        out_shape=[jax.ShapeDtypeStruct((S, D_MODEL), F32)],
        scratch_shapes=[], args=(x, ya, ys, z, an, sn, gate1, w_o), xchg=xchg, cparams=_cparams(VMEM_BIG),
    )
    return x1, x_out


def _out_proj_bwd(dx1, ya, ys, z, an, sn, gate1, w_o, tm, xchg):
    S = dx1.shape[0]
    n_steps = S // tm

    def body(dx1_ref, ya_ref, ys_ref, z_ref, an_ref, sn_ref, g_ref, w_ref,
             dya_ref, dys_ref, dz_ref, gw_ref, acc_ref, gw_acc):
        i = pl.program_id(0)

        @pl.when(i == 0)
        def _():
            acc_ref[...] = jnp.zeros_like(acc_ref)
            gw_acc[...] = jnp.zeros_like(gw_acc)

        u, vjp = jax.vjp(_out_stage, *_out_stage_args(ya_ref, ys_ref, z_ref, an_ref, sn_ref))
        dx1 = dx1_ref[...]
        ub = u.astype(MXU_DTYPE)
        mix = _mm(ub, w_ref[...])
        dmix = dx1 * g_ref[...]
        dmixb = dmix.astype(MXU_DTYPE)
        du = _mm_nt(dmixb, w_ref[...])
        gw_acc[...] += _mm_tn(ub, dmixb)
        dya, dys0, dys1, dz0, dz1, dan, dsn0, dsn1 = vjp(du)
        dya_ref[...] = dya
        dys_ref[...] = jnp.concatenate([dys0, dys1], axis=1)
        dz_ref[...] = jnp.concatenate([dz0, dz1], axis=1).astype(dz_ref.dtype)
        acc_ref[0:1, :] += jnp.sum(dx1 * mix, axis=0, keepdims=True)
        acc_ref[1:2, :] += jnp.concatenate([dan, dsn0, dsn1], axis=1)

        @pl.when(i == n_steps - 1)
        def _():
            gw_ref[...] = gw_acc[...].astype(gw_ref.dtype)

    half = pl.BlockSpec((tm, ATTN_W), _row)
    hvec = pl.BlockSpec((1, ATTN_W), _fixed)
    full = pl.BlockSpec((tm, D_MODEL), _row)
    return _hosted_call(
        body, "out_proj_bwd", n_steps,
        in_specs=[full, half, half, half, hvec, hvec,
                  pl.BlockSpec((1, D_MODEL), _fixed), pl.BlockSpec((D_MODEL, D_MODEL), _fixed)],
        out_specs=[half, half, half, pl.BlockSpec((D_MODEL, D_MODEL), _fixed), pl.BlockSpec((8, D_MODEL), _fixed)],
        out_shape=[jax.ShapeDtypeStruct((S, ATTN_W), F32)] * 2 + [jax.ShapeDtypeStruct((S, ATTN_W), MXU_DTYPE),
                   jax.ShapeDtypeStruct((D_MODEL, D_MODEL), WIRE_DTYPE), jax.ShapeDtypeStruct((8, D_MODEL), F32)],
        scratch_shapes=[pltpu.VMEM((D_MODEL, D_MODEL), F32)],
        args=(dx1, ya, ys, z, an, sn, gate1, w_o), xchg=xchg, cparams=_cparams(VMEM_BIG),
    )


def _loss_rows(x2, fn, tgt):
    y = _rms(x2, fn, D_MODEL)
    per_row = jnp.sum(jnp.square(y - tgt), axis=1, keepdims=True)
    return jnp.sum(per_row, axis=0, keepdims=True) * (0.5 / D_MODEL)


def _mlp_loss(x1, tgt, norm2, scale2, shift2, gate2, fnorm, w_gu, w_d, tm):
    S = x1.shape[0]
    n_pieces = len(w_gu) + len(w_d)

    def body(*refs):
        x1_ref, t_ref, n_ref, sc_ref, sh_ref, g_ref, fn_ref = refs[:7]
        piece_refs = refs[7:7 + n_pieces]
        dx1_ref, h_ref, dgu_ref, act_ref, dmlp_ref, acc_ref, wgu, wd, wsem = refs[7 + n_pieces:]

        @pl.when(pl.program_id(0) == 0)
        def _():
            acc_ref[...] = jnp.zeros_like(acc_ref)
            copies = []
            for dst, pieces in ((wgu, piece_refs[:len(w_gu)]), (wd, piece_refs[len(w_gu):])):
                shard = sum(p.shape[1] for p in pieces)
                off = 0
                for p in pieces:
                    for j in range(N_DEV):
                        copies.append(pltpu.make_async_copy(p.at[j], dst.at[pl.ds(j * shard + off, p.shape[1])],
                                                            wsem.at[len(copies)]))
                    off += p.shape[1]
            for cp in copies:
                cp.start()
            for cp in copies:
                cp.wait()

        x1 = x1_ref[...]
        gate2 = g_ref[...]
        h, vjp_h = jax.vjp(_modnorm, x1, n_ref[...], sc_ref[...], sh_ref[...])
        hb = h.astype(MXU_DTYPE)
        gu = _mm_nt(hb, wgu[...])
        g, u = gu[:, :D_FF], gu[:, D_FF:]
        sg = jax.nn.sigmoid(g)
        silu_g = g * sg
        act = (silu_g * u).astype(MXU_DTYPE)
        mlp = _mm(act, wd[...])
        x2 = x1 + gate2 * mlp
        loss, vjp_loss = jax.vjp(_loss_rows, x2, fn_ref[...], t_ref[...])
        dx2, dfn, _ = vjp_loss(jnp.ones((1, 1), F32))
        dmlp = (dx2 * gate2).astype(MXU_DTYPE)
        dact = _mm_nt(dmlp, wd[...])
        dg = dact * u * (sg * (1.0 + g * (1.0 - sg)))
        du = dact * silu_g
        dgu = jnp.concatenate([dg, du], axis=1).astype(MXU_DTYPE)
        dh = _mm(dgu, wgu[...])
        dx, dn, dsc, dsh = vjp_h(dh)
        dx1_ref[...] = dx2 + dx
        h_ref[...] = hb
        dgu_ref[...] = dgu
        act_ref[...] = act
        dmlp_ref[...] = dmlp
        acc_ref[0:1, :] += dn
        acc_ref[1:2, :] += dsc
        acc_ref[2:3, :] += dsh
        acc_ref[3:4, :] += jnp.sum(dx2 * mlp, axis=0, keepdims=True)
        acc_ref[4:5, :] += dfn
        acc_ref[5:6, :] += jnp.broadcast_to(loss, (1, D_MODEL))

    full = pl.BlockSpec((tm, D_MODEL), _row)
    vec = pl.BlockSpec((1, D_MODEL), _fixed)
    anyspec = pl.BlockSpec(memory_space=pl.ANY)
    return pl.pallas_call(
        body, name="mlp_loss", grid=(S // tm,),
        in_specs=[full, full, vec, vec, vec, vec, vec] + [anyspec] * n_pieces,
        out_specs=[full, full, pl.BlockSpec((tm, 2 * D_FF), _row), pl.BlockSpec((tm, D_FF), _row), full,
                   pl.BlockSpec((8, D_MODEL), _fixed)],
        out_shape=[jax.ShapeDtypeStruct((S, D_MODEL), F32), jax.ShapeDtypeStruct((S, D_MODEL), MXU_DTYPE),
                   jax.ShapeDtypeStruct((S, 2 * D_FF), MXU_DTYPE), jax.ShapeDtypeStruct((S, D_FF), MXU_DTYPE),
                   jax.ShapeDtypeStruct((S, D_MODEL), MXU_DTYPE), jax.ShapeDtypeStruct((8, D_MODEL), F32)],
        scratch_shapes=[pltpu.VMEM((2 * D_FF, D_MODEL), MXU_DTYPE), pltpu.VMEM((D_FF, D_MODEL), MXU_DTYPE),
                        pltpu.SemaphoreType.DMA((N_DEV * n_pieces,))],
        compiler_params=_cparams(VMEM_BIG),
    )(x1, tgt, norm2, scale2, shift2, gate2, fnorm, *w_gu, *w_d)


def _wgrad(a, g, tk, ts, name, xchg=None, g_cols=None):
    pieces = list(a) if isinstance(a, (list, tuple)) else [a]
    S = pieces[0].shape[0]
    K = sum(p.shape[1] for p in pieces)
    assert len(pieces) == 1 or tk == K
    N, col = (g.shape[1], 0) if g_cols is None else g_cols
    ns = S // ts
    n_a = len(pieces)

    def body(*refs):
        a_refs, (g_ref, o_ref, acc_ref) = refs[:n_a], refs[n_a:]
        s = pl.program_id(1)

        @pl.when(s == 0)
        def _():
            acc_ref[...] = jnp.zeros_like(acc_ref)

        a_blk = a_refs[0][...] if n_a == 1 else jnp.concatenate([r[...] for r in a_refs], axis=1)
        acc_ref[...] += _mm_tn(a_blk, g_ref[...])

        @pl.when(s == ns - 1)
        def _():
            o_ref[...] = acc_ref[...].astype(o_ref.dtype)

    if n_a == 1:
        in_specs = [pl.BlockSpec((ts, tk), lambda j, s: (s, j))]
    else:
        in_specs = [pl.BlockSpec((ts, p.shape[1]), lambda j, s: (s, 0)) for p in pieces]
    in_specs.append(pl.BlockSpec((ts, N), lambda j, s: (s, col)))
    out_spec = pl.BlockSpec((tk, N), lambda j, s: (j, 0))
    out_shape = jax.ShapeDtypeStruct((K, N), WIRE_DTYPE)
    scratch = [pltpu.VMEM((tk, N), F32)]
    args = (*pieces, g)
    if xchg is None:
        return pl.pallas_call(body, name=name, grid=(K // tk, ns), in_specs=in_specs, out_specs=out_spec,
                              out_shape=out_shape, scratch_shapes=scratch, compiler_params=_cparams(VMEM_BIG))(*args)
    (out,), x_out = _hosted_call(body, name, (K // tk, ns), in_specs, [out_spec], [out_shape], scratch, args, xchg,
                                 _cparams(VMEM_BIG))
    return out, x_out


SSD_CHUNKS_PER_STEP = 4
SSD_BWD_CHUNKS_PER_STEP = 4
ATTN_BLOCKS_PER_STEP = 4
MASKED = -1e30
QK_SCALE = HALF ** -0.5


def _attn_bias(buckets, rel_bias):
    def body(bk_ref, relb_ref, out_ref):
        bk = bk_ref[...]
        i = lax.broadcasted_iota(jnp.int32, (BLK, 2 * BLK), 0)
        j = lax.broadcasted_iota(jnp.int32, (BLK, 2 * BLK), 1)
        window = (j > i) & (j <= i + BLK)
        for h in range(N_HEADS):
            acc = jnp.zeros((BLK, 2 * BLK), F32)
            for b in range(N_BUCKETS):
                acc = jnp.where(bk == b, relb_ref[b, h], acc)
            out_ref[0, h] = jnp.where(window, acc, MASKED)
            out_ref[1, h] = jnp.where(window & (j >= BLK), acc, MASKED)

    return pl.pallas_call(
        body, name="attn_bias", out_shape=jax.ShapeDtypeStruct((2, N_HEADS, BLK, 2 * BLK), F32),
        in_specs=[pl.BlockSpec(memory_space=pltpu.VMEM), pl.BlockSpec(memory_space=pltpu.SMEM)],
    )(buckets, rel_bias)


def _attn_fwd(qkv, bias, sinks, xchg):
    S = qkv.shape[0]
    nb = S // BLK

    nq = ATTN_BLOCKS_PER_STEP if nb % ATTN_BLOCKS_PER_STEP == 0 else 1
    rows = nq * BLK

    def body(q_ref, kvp_ref, kvc_ref, bias_ref, sinks_ref, y_ref):
        i = pl.program_id(0)
        q = q_ref[...].astype(F32) * QK_SCALE
        kv = jnp.concatenate([kvp_ref[...], kvc_ref[...]], axis=0).astype(F32)
        k_lo, k_hi = _split_pair(kv[:, :LANE])
        v_lo, v_hi = _split_pair(kv[:, LANE:])
        bands = [[t[b * BLK:(b + 2) * BLK].astype(MXU_DTYPE) for t in (k_lo, k_hi, v_lo, v_hi)] for b in range(nq)]
        q_heads = [_split_heads(q[b * BLK:(b + 1) * BLK], 4) for b in range(nq)]
        first = [jnp.where(i == 0, 1, 0) if b == 0 else 0 for b in range(nq)]
        items = [(b, h) for b in range(nq) for h in range(N_HEADS)]
        s = [_mm_nt(q_heads[b][h].astype(MXU_DTYPE), bands[b][h // 4]) + bias_ref[first[b], h] for b, h in items]
        m = [jnp.maximum(jnp.max(s[n], axis=-1, keepdims=True), sinks_ref[h]) for n, (b, h) in enumerate(items)]
        p = [jnp.exp(s[n] - m[n]) for n in range(len(items))]
        rinv = [1.0 / (jnp.sum(p[n], axis=-1, keepdims=True) + jnp.exp(sinks_ref[h] - m[n]))
                for n, (b, h) in enumerate(items)]
        out = [_mm(p[n], bands[b][2 + h // 4]) * rinv[n] for n, (b, h) in enumerate(items)]
        y_ref[...] = jnp.concatenate([_join_heads(out[b * N_HEADS:(b + 1) * N_HEADS]) for b in range(nq)], axis=0)

    smem = pl.BlockSpec(memory_space=pltpu.SMEM)
    return _hosted_call(
        body, "attn_fwd", nb // nq,
        in_specs=[pl.BlockSpec((rows, ATTN_W), _row),
                  pl.BlockSpec((BLK, 2 * KV_W), lambda i: (jnp.maximum(i * nq - 1, 0), 2)),
                  pl.BlockSpec((rows, 2 * KV_W), lambda i: (i, 2)),
                  pl.BlockSpec((2, N_HEADS, BLK, 2 * BLK), lambda i: (0, 0, 0, 0)), smem],
        out_specs=[pl.BlockSpec((rows, ATTN_W), _row)],
        out_shape=[jax.ShapeDtypeStruct((S, ATTN_W), F32)],
        scratch_shapes=[],
        args=(qkv, qkv, qkv, bias, sinks), xchg=xchg, cparams=_cparams(),
    )


def _attn_bwd(qkv, y, dy, bias, sinks, xchg):
    S = qkv.shape[0]
    nb = S // BLK
    nq = ATTN_BLOCKS_PER_STEP if nb % ATTN_BLOCKS_PER_STEP == 0 else 1
    rows, n_steps = nq * BLK, nb // nq

    def body(q_ref, kvp_ref, kvc_ref, y_ref, dy_ref, bias_ref, sinks_ref, dq_ref, dkv_ref, dbias_ref, dsk_ref, carry_ref):
        i = pl.program_id(0)

        @pl.when(i == 0)
        def _():
            dbias_ref[...] = jnp.zeros_like(dbias_ref)
            dsk_ref[...] = jnp.zeros_like(dsk_ref)
            carry_ref[...] = jnp.zeros_like(carry_ref)

        q = q_ref[...].astype(F32) * QK_SCALE
        kv = jnp.concatenate([kvp_ref[...], kvc_ref[...]], axis=0).astype(F32)
        k_lo, k_hi = _split_pair(kv[:, :LANE])
        v_lo, v_hi = _split_pair(kv[:, LANE:])
        bands = [[t[b * BLK:(b + 2) * BLK].astype(MXU_DTYPE) for t in (k_lo, k_hi, v_lo, v_hi)] for b in range(nq)]
        rows_of = lambda ref, b: ref[b * BLK:(b + 1) * BLK, :]
        first = [jnp.where(i == n_steps - 1, 1, 0) if b == 0 else 0 for b in range(nq)]
        items = [(b, h) for b in range(nq) for h in range(N_HEADS)]
        at = lambda b, h: b * N_HEADS + h
        q_heads = [hd for b in range(nq) for hd in _split_heads(q[b * BLK:(b + 1) * BLK], 4)]
        y_heads = [hd for b in range(nq) for hd in _split_heads(rows_of(y_ref, b), 4)]
        dy_heads = [hd for b in range(nq) for hd in _split_heads(rows_of(dy_ref, b), 4)]
        qs = [q_heads[n].astype(MXU_DTYPE) for n in range(len(items))]
        s = [_mm_nt(qs[at(b, h)], bands[b][h // 4]) + bias_ref[first[b], h] for b, h in items]
        m = [jnp.maximum(jnp.max(s[at(b, h)], axis=-1, keepdims=True), sinks_ref[h]) for b, h in items]
        p = [jnp.exp(s[n] - m[n]) for n in range(len(items))]
        esink = [jnp.exp(sinks_ref[h] - m[at(b, h)]) for b, h in items]
        rinv = [1.0 / (jnp.sum(p[n], axis=-1, keepdims=True) + esink[n]) for n in range(len(items))]
        t = [dy_heads[n] * rinv[n] for n in range(len(items))]
        delta = [jnp.sum(t[n] * y_heads[n], axis=-1, keepdims=True) for n in range(len(items))]
        tb = [t[n].astype(MXU_DTYPE) for n in range(len(items))]
        dp = [_mm_nt(tb[at(b, h)], bands[b][2 + h // 4]) for b, h in items]
        ds = [p[n] * (dp[n] - delta[n]) for n in range(len(items))]
        for h in range(N_HEADS):
            ds_h, dsk_h = ds[at(0, h)], esink[at(0, h)] * delta[at(0, h)]
            for b in range(1, nq):
                ds_h = ds_h + ds[at(b, h)]
                dsk_h = dsk_h + esink[at(b, h)] * delta[at(b, h)]
            dbias_ref[h] += ds_h
            dsk_ref[h] -= dsk_h
        dsb = [ds[n].astype(MXU_DTYPE) for n in range(len(items))]
        pb = [p[n].astype(MXU_DTYPE) for n in range(len(items))]
        dq_heads = [_mm(dsb[at(b, h)], bands[b][h // 4]) * QK_SCALE for b, h in items]
        grp = lambda lst, b, g: jnp.concatenate(lst[at(b, 4 * g):at(b, 4 * g) + 4], axis=0)
        dk_pads = [[_mm_tn(grp(dsb, b, g), grp(qs, b, g)) for g in range(2)] for b in range(nq)]
        dv_pads = [[_mm_tn(grp(pb, b, g), grp(tb, b, g)) for g in range(2)] for b in range(nq)]
        dq_ref[...] = jnp.concatenate([_join_heads(dq_heads[b * N_HEADS:(b + 1) * N_HEADS]) for b in range(nq)],
                                      axis=0).astype(dq_ref.dtype)
        part = lambda b, lo: jnp.concatenate(
            [_join_pair(d[b][0][lo:lo + BLK], d[b][1][lo:lo + BLK]) for d in (dk_pads, dv_pads)], axis=1)
        dkv = [part(b, BLK) + (part(b + 1, 0) if b + 1 < nq else carry_ref[...]) for b in range(nq)]
        dkv_ref[...] = jnp.concatenate(dkv, axis=0).astype(dkv_ref.dtype)
        carry_ref[...] = part(0, 0)

    smem = pl.BlockSpec(memory_space=pltpu.SMEM)
    rev = lambda i: (n_steps - 1 - i, 0)
    return _hosted_call(
        body, "attn_bwd", n_steps,
        in_specs=[pl.BlockSpec((rows, ATTN_W), rev),
                  pl.BlockSpec((BLK, 2 * KV_W), lambda i: (jnp.maximum((n_steps - 1 - i) * nq - 1, 0), 2)),
                  pl.BlockSpec((rows, 2 * KV_W), lambda i: (n_steps - 1 - i, 2)),
                  pl.BlockSpec((rows, ATTN_W), rev), pl.BlockSpec((rows, ATTN_W), rev),
                  pl.BlockSpec((2, N_HEADS, BLK, 2 * BLK), lambda i: (0, 0, 0, 0)), smem],
        out_specs=[pl.BlockSpec((rows, ATTN_W), rev), pl.BlockSpec((rows, 2 * KV_W), rev),
                   pl.BlockSpec((N_HEADS, BLK, 2 * BLK), lambda i: (0, 0, 0)),
                   pl.BlockSpec((N_HEADS, BLK, 1), lambda i: (0, 0, 0))],
        out_shape=[jax.ShapeDtypeStruct((S, ATTN_W), MXU_DTYPE), jax.ShapeDtypeStruct((S, 2 * KV_W), MXU_DTYPE),
                   jax.ShapeDtypeStruct((N_HEADS, BLK, 2 * BLK), F32), jax.ShapeDtypeStruct((N_HEADS, BLK, 1), F32)],
        scratch_shapes=[pltpu.VMEM((BLK, 2 * KV_W), F32)],
        args=(qkv, qkv, qkv, y, dy, bias, sinks), xchg=xchg, cparams=_cparams(),
    )


def _attn_finish(dbias, dsk, buckets):
    def body(db_ref, dsk_ref, bk_ref, drel_ref, dsink_ref):
        bk = bk_ref[...]
        r = lax.broadcasted_iota(jnp.int32, (N_BUCKETS, LANE), 0)
        l = lax.broadcasted_iota(jnp.int32, (N_BUCKETS, LANE), 1)
        row = lax.broadcasted_iota(jnp.int32, (N_HEADS, LANE), 0)
        res = jnp.zeros((N_BUCKETS, LANE), F32)
        dsink = jnp.zeros((N_HEADS, LANE), F32)
        for h in range(N_HEADS):
            db = db_ref[h]
            for b in range(N_BUCKETS):
                v = jnp.sum(jnp.sum(jnp.where(bk == b, db, 0.0), axis=1, keepdims=True), axis=0, keepdims=True)
                res = res + jnp.where((r == b) & (l == h), v, 0.0)
            dsink = dsink + jnp.where(row == h, jnp.sum(dsk_ref[h], axis=0, keepdims=True), 0.0)
        drel_ref[...] = res
        dsink_ref[...] = dsink

    return pl.pallas_call(body, name="attn_finish",
                          out_shape=[jax.ShapeDtypeStruct((N_BUCKETS, LANE), F32),
                                     jax.ShapeDtypeStruct((N_HEADS, LANE), F32)])(dbias, dsk, buckets)


def _ssd_consts():
    r = lax.broadcasted_iota(jnp.int32, (BLK, BLK), 0)
    c = lax.broadcasted_iota(jnp.int32, (BLK, BLK), 1)
    causal = c <= r
    upper = (r <= c).astype(F32)
    last = r == BLK - 1
    head = lax.broadcasted_iota(jnp.int32, (N_HEADS, BLK), 0)
    return causal, upper, last, head


def _ssd_chunks(xs, bg, cg, dt_raw_t, prev0, dtb, alog, d_rows, consts):
    causal, upper, last, head = consts
    nq = len(xs)
    items = [(c, h) for c in range(nq) for h in range(N_HEADS)]
    at = lambda c, h: c * N_HEADS + h
    a_neg = -jnp.exp(alog)
    dt_t = [_softplus(dt_raw_t[c] + dtb) for c in range(nq)]
    acs_t = [_mm_hi(dt_t[c] * a_neg, upper) for c in range(nq)]
    cb = [[_mm_nt(cg[c][g], bg[c][g]) for g in range(2)] for c in range(nq)]
    pick = lambda t, h: jnp.sum(jnp.where(head == h, t, 0.0), axis=0, keepdims=True)
    dt_row = [pick(dt_t[c], h) for c, h in items]
    a_row = [pick(acs_t[c], h) for c, h in items]
    a_rb = [jnp.broadcast_to(a_row[n], (BLK, BLK)) for n in range(len(items))]
    a_b = [a_rb[n].T for n in range(len(items))]
    a_last = [jnp.sum(jnp.where(last, a_b[n], 0.0), axis=0, keepdims=True) for n in range(len(items))]
    w = [cb[c][h // 4] * jnp.exp(jnp.where(causal, a_b[at(c, h)] - a_rb[at(c, h)], -1e30)) * dt_row[at(c, h)]
         for c, h in items]
    f_b = [jnp.broadcast_to(dt_row[n] * jnp.exp(a_last[n] - a_row[n]), (BLK, BLK)).T for n in range(len(items))]
    y_in = [_mm(w[at(c, h)], xs[c][h]) for c, h in items]
    st = [_mm_tn(bg[c][h // 4], xs[c][h] * f_b[at(c, h)]) for c, h in items]
    e_b = [jnp.exp(a_b[n]) for n in range(len(items))]
    states = [list(prev0)]
    for c in range(nq):
        states.append([states[c][h] * jnp.exp(a_last[at(c, h)]) + st[at(c, h)] for h in range(N_HEADS)])
    y_off = [_mm(cg[c][h // 4], states[c][h]) * e_b[at(c, h)] for c, h in items]
    ys = [[y_in[at(c, h)] + y_off[at(c, h)] + d_rows[h] * xs[c][h] for h in range(N_HEADS)] for c in range(nq)]
    return ys, states


def _ssd_chunks_bwd(xs, bg, cg, dt_raw_t, prev, dtb, alog, d_rows, dys, dh_last, consts):
    causal, upper, last, head = consts
    nq = len(xs)
    items = [(c, h) for c in range(nq) for h in range(N_HEADS)]
    ni = len(items)
    at = lambda c, h: c * N_HEADS + h
    groups = [(c, g) for c in range(nq) for g in range(2)]
    lane = _lane_iota((BLK, BLK))
    lane_row = _lane_iota((1, BLK))
    a_neg = -jnp.exp(alog)
    pre_dt = [dt_raw_t[c] + dtb for c in range(nq)]
    dt_t = [_softplus(pre_dt[c]) for c in range(nq)]
    acs_t = [_mm_hi(dt_t[c] * a_neg, upper) for c in range(nq)]
    pick = lambda t, h: jnp.sum(jnp.where(head == h, t, 0.0), axis=0, keepdims=True)
    full_sum = lambda t: jnp.sum(jnp.sum(t, axis=1, keepdims=True), axis=0, keepdims=True)
    dt_row = [pick(dt_t[c], h) for c, h in items]
    a_row = [pick(acs_t[c], h) for c, h in items]
    a_rb = [jnp.broadcast_to(a_row[n], (BLK, BLK)) for n in range(ni)]
    a_b = [a_rb[n].T for n in range(ni)]
    a_last = [jnp.sum(jnp.where(last, a_b[n], 0.0), axis=0, keepdims=True) for n in range(ni)]
    lm = [jnp.exp(jnp.where(causal, a_b[n] - a_rb[n], -1e30)) for n in range(ni)]
    cgb = [[cg[c][g].astype(MXU_DTYPE) for g in range(2)] for c in range(nq)]
    bgb = [[bg[c][g].astype(MXU_DTYPE) for g in range(2)] for c in range(nq)]
    cb = [[_mm_nt(cgb[c][g], bgb[c][g]) for g in range(2)] for c in range(nq)]
    u = [cb[c][h // 4] * lm[at(c, h)] for c, h in items]
    w = [(u[n] * dt_row[n]).astype(MXU_DTYPE) for n in range(ni)]
    e_row = [jnp.exp(a_last[n] - a_row[n]) for n in range(ni)]
    f_row = [dt_row[n] * e_row[n] for n in range(ni)]
    f_b = [jnp.broadcast_to(f_row[n], (BLK, BLK)).T for n in range(ni)]
    e_b = [jnp.exp(a_b[n]) for n in range(ni)]
    el = [jnp.exp(a_last[n]) for n in range(ni)]
    xb = [xs[c][h].astype(MXU_DTYPE) for c, h in items]
    dyb = [dys[c][h].astype(MXU_DTYPE) for c, h in items]
    prevb = [prev[c][h].astype(MXU_DTYPE) for c, h in items]
    gmat = [_mm(cgb[c][h // 4], prevb[at(c, h)]) for c, h in items]
    dw = [_mm_nt(dyb[n], xb[n]) for n in range(ni)]
    dg = [dys[c][h] * e_b[at(c, h)] for c, h in items]
    dgb = [dg[n].astype(MXU_DTYPE) for n in range(ni)]
    from_y = [_mm_tn(cgb[c][h // 4], dgb[at(c, h)]) for c, h in items]
    dhs = [None] * ni
    dprev = [None] * ni
    for c in reversed(range(nq)):
        for h in range(N_HEADS):
            dhs[at(c, h)] = dh_last[h] if c == nq - 1 else dprev[at(c + 1, h)]
            dprev[at(c, h)] = from_y[at(c, h)] + dhs[at(c, h)] * el[at(c, h)]
    dstb = [dhs[n].astype(MXU_DTYPE) for n in range(ni)]
    dxf = [_mm(bgb[c][h // 4], dstb[at(c, h)]) for c, h in items]
    xfb = [(xs[c][h] * f_b[at(c, h)]).astype(MXU_DTYPE) for c, h in items]
    dxs = [_mm_tn(w[at(c, h)], dyb[at(c, h)]) + d_rows[h] * dys[c][h] + f_b[at(c, h)] * dxf[at(c, h)]
           for c, h in items]
    dd_item = [jnp.sum(dys[c][h] * xs[c][h], axis=0, keepdims=True) for c, h in items]
    dcg_h = [_mm_nt(dgb[n], prevb[n]) for n in range(ni)]
    dbg_h = [_mm_nt(xfb[n], dstb[n]) for n in range(ni)]
    zt = [dw[n] * u[n] for n in range(ni)]
    dseg = [zt[n] * dt_row[n] for n in range(ni)]
    dcb_h = [dw[n] * lm[n] * dt_row[n] for n in range(ni)]
    four = lambda lst, c, g: lst[at(c, 4 * g)] + lst[at(c, 4 * g + 1)] + lst[at(c, 4 * g + 2)] + lst[at(c, 4 * g + 3)]
    dcb = {(c, g): four(dcb_h, c, g).astype(MXU_DTYPE) for c, g in groups}
    dcg = [[four(dcg_h, c, g) + _mm(dcb[c, g], bgb[c][g]) for g in range(2)] for c in range(nq)]
    dbg = [[four(dbg_h, c, g) + _mm_tn(dcb[c, g], cgb[c][g]) for g in range(2)] for c in range(nq)]
    r1 = [jnp.sum(dg[n] * gmat[n] + dseg[n], axis=1, keepdims=True) for n in range(ni)]
    r2 = [jnp.sum(dxf[at(c, h)] * xs[c][h], axis=1, keepdims=True) for c, h in items]
    tt = [jnp.where(lane < HALF, jnp.broadcast_to(r1[n], (BLK, BLK)), jnp.broadcast_to(r2[n], (BLK, BLK))).T
          for n in range(ni)]
    r1_row = [tt[n][0:1, :] for n in range(ni)]
    r2_row = [tt[n][HALF:HALF + 1, :] for n in range(ni)]
    d_el = [full_sum(dhs[at(c, h)] * prev[c][h]) for c, h in items]
    da_last = [jnp.sum(r2_row[n] * f_row[n], axis=1, keepdims=True) + el[n] * d_el[n] for n in range(ni)]
    da_row = [r1_row[n] - jnp.sum(dseg[n], axis=0, keepdims=True) - r2_row[n] * f_row[n]
              + jnp.where(lane_row == BLK - 1, da_last[n], 0.0) for n in range(ni)]
    ddt_row = [jnp.sum(zt[n], axis=0, keepdims=True) + r2_row[n] * e_row[n] for n in range(ni)]
    draw, dalog = [], jnp.zeros((N_HEADS, BLK), F32)
    for c in range(nq):
        da_t = jnp.zeros((N_HEADS, BLK), F32)
        ddt_t = jnp.zeros((N_HEADS, BLK), F32)
        for h in range(N_HEADS):
            da_t = jnp.where(head == h, da_row[at(c, h)], da_t)
            ddt_t = jnp.where(head == h, ddt_row[at(c, h)], ddt_t)
        d_dta = _mm_hi(da_t, causal.astype(F32))
        dalog = dalog + d_dta * dt_t[c] * a_neg
        draw.append((ddt_t + d_dta * a_neg) * jax.nn.sigmoid(pre_dt[c]))
    ddtb = draw[0]
    for c in range(1, nq):
        ddtb = ddtb + draw[c]
    dd_rows = []
    for h in range(N_HEADS):
        t = dd_item[at(0, h)]
        for c in range(1, nq):
            t = t + dd_item[at(c, h)]
        dd_rows.append(t)
    return ([dxs[c * N_HEADS:(c + 1) * N_HEADS] for c in range(nq)], dbg, dcg, draw,
            [dprev[at(0, h)] for h in range(N_HEADS)], ddtb, dalog, dd_rows)


def _dt_rows(dt_blk):
    return dt_blk.T[:N_HEADS]


def _conv_pre(halo, blk, cw_ref, cb_ref):
    ext = jnp.concatenate([halo, blk], axis=0)
    taps = [pltpu.roll(ext, 3 - k, 0)[8:] for k in range(3)] + [blk]
    pre = cb_ref[...] + cw_ref[0:1, :] * taps[0]
    for k in range(1, 4):
        pre = pre + cw_ref[k:k + 1, :] * taps[k]
    return pre


def _ssd_split(pre):
    heads = _split_heads(pre[:, :SSM_W], 4)
    pb = [pre[:, SSM_W + g * D_STATE:SSM_W + (g + 1) * D_STATE] for g in range(2)]
    pc = [pre[:, SSM_W + 2 * D_STATE + g * D_STATE:SSM_W + 2 * D_STATE + (g + 1) * D_STATE] for g in range(2)]
    return heads, pb, pc


def _ssd_fwd(xbc, dt_raw, conv_w, conv_b, dtb_row, alog_row, d_exp, xchg):
    S = xbc.shape[0]
    nc = S // BLK
    nq = SSD_CHUNKS_PER_STEP if nc % SSD_CHUNKS_PER_STEP == 0 else 1
    rows = nq * BLK

    def body(xbc_ref, halo_ref, dt_ref, cw_ref, cb_ref, dtb_ref, alog_ref, d_ref, y_ref, prev_ref, pre_ref, state_ref):
        i = pl.program_id(0)

        @pl.when(i == 0)
        def _():
            state_ref[...] = jnp.zeros_like(state_ref)

        halo = halo_ref[...] * jnp.where(i > 0, 1.0, 0.0)
        pre = _conv_pre(halo, xbc_ref[...], cw_ref, cb_ref)
        pre_ref[...] = pre
        xc = _silu(pre)
        split = [_ssd_split(xc[c * BLK:(c + 1) * BLK]) for c in range(nq)]
        dt_t = [_dt_rows(dt_ref[c * BLK:(c + 1) * BLK, :]) for c in range(nq)]
        prev0 = [state_ref[h] for h in range(N_HEADS)]
        d_rows = [d_ref[h:h + 1, :] for h in range(N_HEADS)]
        ys, states = _ssd_chunks([s[0] for s in split], [s[1] for s in split], [s[2] for s in split], dt_t, prev0,
                                 dtb_ref[...], alog_ref[...], d_rows, _ssd_consts())
        for h in range(N_HEADS):
            for c in range(nq):
                prev_ref[c, h] = states[c][h]
            state_ref[h] = states[nq][h]
        y_ref[...] = jnp.concatenate([_join_heads(ys[c]) for c in range(nq)], axis=0)

    vec = pl.BlockSpec((N_HEADS, LANE), _fixed)
    return _hosted_call(
        body, "ssd_fwd", nc // nq,
        in_specs=[pl.BlockSpec((rows, XBC_W), _row),
                  pl.BlockSpec((8, XBC_W), lambda i: (jnp.maximum(i * (rows // 8) - 1, 0), 0)),
                  pl.BlockSpec((rows, LANE), _row),
                  pl.BlockSpec((4, XBC_W), _fixed), pl.BlockSpec((1, XBC_W), _fixed), vec, vec,
                  pl.BlockSpec((N_HEADS, LANE), _fixed)],
        out_specs=[pl.BlockSpec((rows, SSM_W), _row),
                   pl.BlockSpec((nq, N_HEADS, D_STATE, LANE), lambda i: (i, 0, 0, 0)),
                   pl.BlockSpec((rows, XBC_W), _row)],
        out_shape=[jax.ShapeDtypeStruct((S, SSM_W), F32), jax.ShapeDtypeStruct((nc, N_HEADS, D_STATE, LANE), F32),
                   jax.ShapeDtypeStruct((S, XBC_W), F32)],
        scratch_shapes=[pltpu.VMEM((N_HEADS, D_STATE, LANE), F32)],
        args=(xbc, xbc, dt_raw, conv_w, conv_b, dtb_row, alog_row, d_exp), xchg=xchg, cparams=_cparams(),
    )


def _ssd_bwd(xbc, pre_act, dt_raw, prev_states, dy, conv_w, dtb_row, alog_row, d_exp, xchg):
    S = xbc.shape[0]
    nc = S // BLK
    nq = SSD_BWD_CHUNKS_PER_STEP if nc % SSD_BWD_CHUNKS_PER_STEP == 0 else 1
    rows, n_steps = nq * BLK, nc // nq

    def body(xbc_ref, halo_ref, pre_ref, dt_ref, prev_ref, dy_ref, cw_ref, dtb_ref, alog_ref, d_ref,
             dxbc_ref, ddt_ref, dcw_ref, dvec_ref, dd_ref, gstate_ref, ghalo_ref):
        i = pl.program_id(0)

        @pl.when(i == 0)
        def _():
            gstate_ref[...] = jnp.zeros_like(gstate_ref)
            ghalo_ref[...] = jnp.zeros_like(ghalo_ref)
            dcw_ref[...] = jnp.zeros_like(dcw_ref)
            dvec_ref[...] = jnp.zeros_like(dvec_ref)
            dd_ref[...] = jnp.zeros_like(dd_ref)

        halo = halo_ref[...] * jnp.where(i < n_steps - 1, 1.0, 0.0)
        ext = jnp.concatenate([halo, xbc_ref[...]], axis=0)
        pre = pre_ref[...]
        sig = jax.nn.sigmoid(pre)
        xc = pre * sig
        split = [_ssd_split(xc[c * BLK:(c + 1) * BLK]) for c in range(nq)]
        dt_t = [_dt_rows(dt_ref[c * BLK:(c + 1) * BLK, :]) for c in range(nq)]
        prev = [[prev_ref[c, h] for h in range(N_HEADS)] for c in range(nq)]
        d_rows = [d_ref[h:h + 1, :] for h in range(N_HEADS)]
        dys = [_split_heads(dy_ref[c * BLK:(c + 1) * BLK, :], 4) for c in range(nq)]
        dh_last = [gstate_ref[h] for h in range(N_HEADS)]
        dheads, dpb, dpc, ddt_t, dprev0, ddtb, dalog, dd_rows = _ssd_chunks_bwd(
            [s[0] for s in split], [s[1] for s in split], [s[2] for s in split], dt_t, prev, dtb_ref[...],
            alog_ref[...], d_rows, dys, dh_last, _ssd_consts())
        for h in range(N_HEADS):
            gstate_ref[h] = dprev0[h]
            dd_ref[h:h + 1, :] += dd_rows[h]
        pad = jnp.zeros((BLK - N_HEADS, BLK), F32)
        ddt_ref[...] = jnp.concatenate([jnp.concatenate([ddt_t[c], pad], axis=0).T for c in range(nq)],
                                       axis=0).astype(ddt_ref.dtype)
        dvec_ref[0:N_HEADS, :] += ddtb
        dvec_ref[N_HEADS:, :] += dalog
        dxc = jnp.concatenate([jnp.concatenate([_join_heads(dheads[c])] + list(dpb[c]) + list(dpc[c]), axis=1)
                               for c in range(nq)], axis=0)
        dpre = dxc * (sig * (1.0 + pre * (1.0 - sig)))
        zeros8 = jnp.zeros((8, XBC_W), F32)
        dpe = jnp.concatenate([zeros8, dpre, zeros8], axis=0)
        n_ext = 16 + rows
        shifted = [pltpu.roll(dpe, n_ext - (3 - k), 0)[:8 + rows] for k in range(3)] + [dpe[:8 + rows]]
        dext = cw_ref[0:1, :] * shifted[0]
        for k in range(1, 4):
            dext = dext + cw_ref[k:k + 1, :] * shifted[k]
        for k in range(4):
            dcw_ref[k:k + 1, :] += jnp.sum(shifted[k] * ext, axis=0, keepdims=True)
        dcw_ref[4:5, :] += jnp.sum(dpre, axis=0, keepdims=True)
        dxbc_ref[...] = jnp.concatenate([dext[8:rows], dext[rows:] + ghalo_ref[...]], axis=0).astype(dxbc_ref.dtype)
        ghalo_ref[...] = dext[:8, :]

    vec = pl.BlockSpec((N_HEADS, LANE), _fixed)
    rev = lambda i: (n_steps - 1 - i, 0)
    return _hosted_call(
        body, "ssd_bwd", n_steps,
        in_specs=[pl.BlockSpec((rows, XBC_W), rev),
                  pl.BlockSpec((8, XBC_W), lambda i: (jnp.maximum((n_steps - 1 - i) * (rows // 8) - 1, 0), 0)),
                  pl.BlockSpec((rows, XBC_W), rev),
                  pl.BlockSpec((rows, LANE), rev),
                  pl.BlockSpec((nq, N_HEADS, D_STATE, LANE), lambda i: (n_steps - 1 - i, 0, 0, 0)),
                  pl.BlockSpec((rows, SSM_W), rev),
                  pl.BlockSpec((4, XBC_W), _fixed), vec, vec,
                  pl.BlockSpec((N_HEADS, LANE), _fixed)],
        out_specs=[pl.BlockSpec((rows, XBC_W), rev), pl.BlockSpec((rows, LANE), rev),
                   pl.BlockSpec((8, XBC_W), _fixed), pl.BlockSpec((2 * N_HEADS, LANE), _fixed),
                   pl.BlockSpec((N_HEADS, LANE), _fixed)],
        out_shape=[jax.ShapeDtypeStruct((S, XBC_W), MXU_DTYPE), jax.ShapeDtypeStruct((S, LANE), MXU_DTYPE),
                   jax.ShapeDtypeStruct((8, XBC_W), F32), jax.ShapeDtypeStruct((2 * N_HEADS, LANE), F32),
                   jax.ShapeDtypeStruct((N_HEADS, LANE), F32)],
        scratch_shapes=[pltpu.VMEM((N_HEADS, D_STATE, LANE), F32), pltpu.VMEM((8, XBC_W), F32)],
        args=(xbc, xbc, pre_act, dt_raw, prev_states, dy, conv_w, dtb_row, alog_row, d_exp), xchg=xchg,
        cparams=_cparams(VMEM_BIG),
    )


def _adamw_math(w, g, m, v):
    m = ADAM_B1 * m + (1.0 - ADAM_B1) * g
    v = ADAM_B2 * v + (1.0 - ADAM_B2) * jnp.square(g)
    m_hat = m / (1.0 - ADAM_B1 ** ADAM_STEP)
    v_hat = v / (1.0 - ADAM_B2 ** ADAM_STEP)
    delta = -ADAM_LR * (m_hat / (jnp.sqrt(v_hat) + ADAM_EPS) + ADAM_WD * w)
    return delta, m, v


def _reduce_adamw(parts, w, m, v, name):
    R, C = w.shape

    def body(p_ref, w_ref, m_ref, v_ref, g_ref, d_ref, nm_ref, nv_ref):
        g = p_ref[0].astype(F32)
        for i in range(1, N_DEV):
            g = g + p_ref[i].astype(F32)
        d, nm, nv = _adamw_math(w_ref[...], g, m_ref[...], v_ref[...])
        g_ref[...] = g
        d_ref[...] = d
        nm_ref[...] = nm
        nv_ref[...] = nv

    if R % 16 == 0:
        tr = max(t for t in range(16, 257, 16) if R % t == 0)
        n, blk, pblk = R // tr, pl.BlockSpec((tr, C), _row), pl.BlockSpec((N_DEV, tr, C), lambda i: (0, i, 0))
    else:
        tl = 256
        n, blk, pblk = C // tl, pl.BlockSpec((R, tl), lambda i: (0, i)), pl.BlockSpec((N_DEV, R, tl),
                                                                                      lambda i: (0, 0, i))
    return pl.pallas_call(
        body, name=name, grid=(n,), in_specs=[pblk, blk, blk, blk],
        out_specs=[blk] * 4, out_shape=[jax.ShapeDtypeStruct((R, C), F32)] * 4,
    )(parts, w, m, v)


def _reduce_adamw_hosting(parts_list, wmv_list, name, xchg):
    n_arr = len(parts_list)
    pieces = [list(p) if isinstance(p, (tuple, list)) else [p] for p in parts_list]
    n_pieces = sum(len(p) for p in pieces)
    C = wmv_list[0][0].shape[1]
    tl = 256

    def total(ref):
        g = ref[0].astype(F32)
        for i in range(1, N_DEV):
            g = g + ref[i].astype(F32)
        return g

    def body(*refs):
        p_refs, wmv_refs, o_refs = refs[:n_pieces], refs[n_pieces:n_pieces + 3 * n_arr], refs[n_pieces + 3 * n_arr:]
        at = 0
        for k in range(n_arr):
            sums = [total(r) for r in p_refs[at:at + len(pieces[k])]]
            at += len(pieces[k])
            g = sums[0] if len(sums) == 1 else jnp.concatenate(sums, axis=0)
            w_ref, m_ref, v_ref = wmv_refs[3 * k:3 * k + 3]
            d, nm, nv = _adamw_math(w_ref[...], g, m_ref[...], v_ref[...])
            for o, val in zip(o_refs[4 * k:4 * k + 4], (g, d, nm, nv)):
                o[...] = val

    in_specs = [pl.BlockSpec((N_DEV, p.shape[1], tl), lambda i: (0, 0, i)) for group in pieces for p in group]
    in_specs += [pl.BlockSpec((w.shape[0], tl), lambda i: (0, i)) for w, _, _ in wmv_list for _ in range(3)]
    out_specs = [pl.BlockSpec((w.shape[0], tl), lambda i: (0, i)) for w, _, _ in wmv_list for _ in range(4)]
    out_shape = [jax.ShapeDtypeStruct(w.shape, F32) for w, _, _ in wmv_list for _ in range(4)]
    args = [p for group in pieces for p in group] + [a for wmv in wmv_list for a in wmv]
    outs, x_out = _hosted_call(body, name, C // tl, in_specs, out_specs, out_shape, [], args, xchg,
                               _cparams(VMEM_BIG))
    return [outs[4 * k:4 * k + 4] for k in range(n_arr)], x_out


_SMALL_NAMES = ("ada_b", "norm1", "conv_w", "conv_b", "dt_bias", "A_log", "D_skip", "sinks", "attn_out_norm",
                "ssm_out_norm", "norm2", "rel_bias", "final_norm")
N_MOD = 6 * D_MODEL


def _mod_row(a0, a1, a2):
    return jnp.concatenate([a0[2:3], a0[1:2], a1[0:1], a2[2:3], a2[1:2], a2[3:4]], axis=1)


def _small_update(gathered, params):
    n_g = len(gathered)
    flat = [a for name in _SMALL_NAMES for a in params[name]]

    def body(*refs):
        a0_ref, a1_ref, a2_ref, cw_ref, dv_ref, dd_ref, ds_ref, dr_ref, c_ref = refs[:n_g]
        wmv = refs[n_g:n_g + len(flat)]
        outs = refs[n_g + len(flat):]

        def total(ref):
            t = ref[0]
            for i in range(1, N_DEV):
                t = t + ref[i]
            return t

        t0, t1, t2, tcw, tdv, tdd, tds, tdr = [total(r) for r in (a0_ref, a1_ref, a2_ref, cw_ref, dv_ref, dd_ref,
                                                                   ds_ref, dr_ref)]
        r8 = lax.broadcasted_iota(jnp.int32, (N_HEADS, LANE), 0)
        l8 = lax.broadcasted_iota(jnp.int32, (N_HEADS, LANE), 1)

        def diag_row(t):
            return jnp.sum(jnp.where(r8 == l8, t, 0.0), axis=0, keepdims=True)[:, :N_HEADS]

        def lane_sums(t):
            return diag_row(jnp.broadcast_to(jnp.sum(t, axis=1, keepdims=True), (N_HEADS, LANE)))

        me = _lin(_my_pos())
        n_cw = XBC_W // N_DEV
        cw_mine = jnp.zeros((4, n_cw), F32)
        for j in range(N_DEV):
            cw_mine = cw_mine + tcw[0:4, j * n_cw:(j + 1) * n_cw] * jnp.where(me == j, 1.0, 0.0)
        grads = {
            "ada_b": _mod_row(t0, t1, t2), "norm1": t0[0:1], "conv_w": cw_mine, "conv_b": tcw[4:5],
            "dt_bias": lane_sums(tdv[:N_HEADS]), "A_log": lane_sums(tdv[N_HEADS:]), "D_skip": lane_sums(tdd),
            "sinks": diag_row(tds), "attn_out_norm": t1[1:2, :ATTN_W], "ssm_out_norm": t1[1:2, ATTN_W:],
            "norm2": t2[0:1], "rel_bias": tdr[:, :N_HEADS], "final_norm": t2[4:5],
        }
        for k, name in enumerate(_SMALL_NAMES):
            w_ref, m_ref, v_ref = wmv[3 * k:3 * k + 3]
            g = grads[name]
            d, nm, nv = _adamw_math(w_ref[...], g, m_ref[...], v_ref[...])
            for o, val in zip(outs[4 * k:4 * k + 4], (g, d, nm, nv)):
                o[...] = val
        loss_ref, call_ref, dmod_ref = outs[4 * len(_SMALL_NAMES):]
        loss_ref[...] = t2[5:6, 0:1]
        call_ref[...] = jnp.concatenate([c_ref[i] for i in range(N_DEV)], axis=0)
        dmod_ref[...] = jnp.concatenate([_mod_row(a0_ref[i], a1_ref[i], a2_ref[i]) for i in range(N_DEV)], axis=0)

    out_shape = [jax.ShapeDtypeStruct(params[name][0].shape, F32) for name in _SMALL_NAMES for _ in range(4)]
    out_shape += [jax.ShapeDtypeStruct((1, 1), F32), jax.ShapeDtypeStruct((N_DEV, D_MODEL), F32),
                  jax.ShapeDtypeStruct((N_DEV, N_MOD), F32)]
    res = pl.pallas_call(body, name="small_update", out_shape=out_shape)(*gathered, *flat)
    upd = {name: res[4 * k:4 * k + 4] for k, name in enumerate(_SMALL_NAMES)}
    loss, c_all, dmod_all = res[4 * len(_SMALL_NAMES):]
    return upd, loss, c_all, dmod_all


def _ada_w_update(c_all, dmod_all, w, m, v):
    chunk = w.shape[1]

    def body(c_ref, dm_ref, w_ref, m_ref, v_ref, g_ref, d_ref, nm_ref, nv_ref):
        me = _lin(_my_pos())
        dm = jnp.zeros((N_DEV, chunk), F32)
        for j in range(N_DEV):
            dm = dm + dm_ref[:, j * chunk:(j + 1) * chunk] * jnp.where(me == j, 1.0, 0.0)
        g = lax.dot_general(_silu(c_ref[...]), dm, (((0,), (0,)), ((), ())), precision=HI,
                            preferred_element_type=F32)
        d, nm, nv = _adamw_math(w_ref[...], g, m_ref[...], v_ref[...])
        g_ref[...] = g
        d_ref[...] = d
        nm_ref[...] = nm
        nv_ref[...] = nv

    tr = 256
    blk = pl.BlockSpec((tr, chunk), _row)
    return pl.pallas_call(
        body, name="ada_w_update", grid=(w.shape[0] // tr,),
        in_specs=[pl.BlockSpec((N_DEV, tr), lambda i: (0, i)), pl.BlockSpec(dmod_all.shape, _fixed), blk, blk, blk],
        out_specs=[blk] * 4, out_shape=[jax.ShapeDtypeStruct(w.shape, F32)] * 4,
    )(c_all, dmod_all, w, m, v)


def _local_step(x, tgt, c, mod, w_in, conv_w, w_o_mine, w_gu_mine, w_d_mine, p):
    S = x.shape[0]
    tm = min(512, S)
    tmm = min(256, S)
    tw = min(2048, S)
    shift1, scale1, gate1, shift2, scale2, gate2 = [mod[i:i + 1] for i in range(6)]
    buckets = jnp.asarray(_t5_bucket_table())
    per_head = lambda a: jnp.broadcast_to(a.reshape(N_HEADS, 1), (N_HEADS, LANE))
    dtb_row, alog_row, d_exp = per_head(p["dt_bias"]), per_head(p["A_log"]), per_head(p["D_skip"])
    sinks = p["sinks"].reshape(N_HEADS)

    d_cut, gu_cut = WD_CUT, WGU_CUTS
    (qkv, z, xbc, dt_raw), (g_d_a,) = _in_proj_fwd(x, p["norm1"], scale1, shift1, w_in, tm,
                                                   ([w_d_mine[:d_cut]], "two-level"))
    bias = _attn_bias(buckets, p["rel_bias"])
    (ya,), (g_gu_a,) = _attn_fwd(qkv, bias, sinks, ([w_gu_mine[:gu_cut[0]]], "two-level"))
    (ys, prev_states, pre_act), (g_gu_b, g_o) = _ssd_fwd(xbc, dt_raw, conv_w, p["conv_b"], dtb_row, alog_row, d_exp,
                                                ([w_gu_mine[gu_cut[0]:gu_cut[1]], w_o_mine], "two-level"))
    w_o = g_o.reshape(D_MODEL, D_MODEL)
    x1, (g_gu_c, g_d_b) = _out_proj_fwd(x, ya, ys, z, p["attn_out_norm"], p["ssm_out_norm"], gate1, w_o, tm,
                                        ([w_gu_mine[gu_cut[1]:], w_d_mine[d_cut:]], "two-level"))
    dx1, h2, dgu, act, dmlp, acc2 = _mlp_loss(x1, tgt, p["norm2"], scale2, shift2, gate2, p["final_norm"],
                                              (g_gu_a, g_gu_b, g_gu_c), (g_d_a, g_d_b), tmm)
    g_w_gu = _wgrad(dgu, h2, 2 * D_FF // 4, tw, "wgrad_gate_up")
    g_w_d = _wgrad(act, dmlp, D_FF // 2, tw, "wgrad_down")
    gu_slots = g_w_gu.reshape(N_DEV, 2 * D_FF // N_DEV, D_MODEL)
    (dya, dys, dz, g_w_o, acc1), (r_gu_a,) = _out_proj_bwd(
        dx1, ya, ys, z, p["attn_out_norm"], p["ssm_out_norm"], gate1, w_o, tm, ([gu_slots], ("rows", 0, GGU_CUT)))
    (dq, dkv, dbias, dsk), (r_d, r_o) = _attn_bwd(
        qkv, ya, dya, bias, sinks,
        ([g_w_d.reshape(N_DEV, D_FF // N_DEV, D_MODEL), g_w_o.reshape(N_DEV, D_MODEL // N_DEV, D_MODEL)], True))
    drel, dsink = _attn_finish(dbias, dsk, buckets)
    (dxbc, ddt, dcw, dvec, dd), (r_gu_b,) = _ssd_bwd(
        xbc, pre_act, dt_raw, prev_states, dys, conv_w, dtb_row, alog_row, d_exp,
        ([gu_slots], ("rows", GGU_CUT, 2 * D_FF // N_DEV - GGU_CUT)))
    r_gu = (r_gu_a, r_gu_b)
    gx, h1, acc0, g_in_a = _in_proj_bwd(x, dx1, dq, dkv, dz, dxbc, ddt, p["norm1"], scale1, shift1, w_in, tm)
    half = D_MODEL // 2
    slots = lambda g: g[:IN_W].reshape(N_DEV, IN_W // N_DEV, half)
    g_in_b, exchanged = _wgrad((dq, dkv, dz, dxbc, ddt), h1, IN_PAD, tw, "wgrad_in_b",
                               [([slots(g_in_a)], True), ([acc0, acc1, acc2, dcw, dvec, dd, dsink, drel, c], False)],
                               g_cols=(half, 1))
    return gx, (exchanged[0], slots(g_in_b)), (r_o, r_gu, r_d), exchanged[1:]


def kernel(x, c, ada_w, ada_b, norm1, w_in, conv_w, conv_b, dt_bias, A_log, D_skip, sinks, attn_out_norm, ssm_out_norm, w_o, norm2, w_gate_up, w_down, rel_bias, final_norm, loss_target, m_ada_w, m_ada_b, m_norm1, m_w_in, m_conv_w, m_conv_b, m_dt_bias, m_A_log, m_D_skip, m_sinks, m_attn_out_norm, m_ssm_out_norm, m_w_o, m_norm2, m_w_gate_up, m_w_down, m_rel_bias, m_final_norm, v_ada_w, v_ada_b, v_norm1, v_w_in, v_conv_w, v_conv_b, v_dt_bias, v_A_log, v_D_skip, v_sinks, v_attn_out_norm, v_ssm_out_norm, v_w_o, v_norm2, v_w_gate_up, v_w_down, v_rel_bias, v_final_norm):
    two_d = lambda a: a if a.ndim == 2 else a.reshape(-1, a.shape[-1])
    small_params = dict(
        ada_b=(ada_b, m_ada_b, v_ada_b), norm1=(norm1, m_norm1, v_norm1), conv_w=(conv_w, m_conv_w, v_conv_w),
        conv_b=(conv_b, m_conv_b, v_conv_b), dt_bias=(dt_bias, m_dt_bias, v_dt_bias), A_log=(A_log, m_A_log, v_A_log),
        D_skip=(D_skip, m_D_skip, v_D_skip), sinks=(sinks, m_sinks, v_sinks),
        attn_out_norm=(attn_out_norm, m_attn_out_norm, v_attn_out_norm),
        ssm_out_norm=(ssm_out_norm, m_ssm_out_norm, v_ssm_out_norm), norm2=(norm2, m_norm2, v_norm2),
        rel_bias=(rel_bias, m_rel_bias, v_rel_bias), final_norm=(final_norm, m_final_norm, v_final_norm))
    small_params = {k: tuple(two_d(a) for a in v) for k, v in small_params.items()}
    S = x.shape[1]
    xs, tgt = x.reshape(S, D_MODEL), loss_target.reshape(S, D_MODEL)
    ada_w2 = ada_w[0]
    chunk = ada_w2.shape[1]
    t_in = [jnp.transpose(a[0]) for a in (w_in, m_w_in, v_w_in)]
    t_gu = [jnp.transpose(a[0]) for a in (w_gate_up, m_w_gate_up, v_w_gate_up)]

    mod, (g_in, g_cw) = _mod_and_gather(c, ada_w2, ada_b.reshape(N_DEV, chunk), [t_in[0].astype(WIRE_DTYPE), conv_w[0]])
    mod = mod.reshape(6, D_MODEL)
    w_in_full = jnp.pad(g_in.reshape(IN_W, D_MODEL), ((0, IN_PAD - IN_W), (0, 0)))
    conv_w_full = jnp.transpose(g_cw, (1, 0, 2)).reshape(4, XBC_W)

    p = {k: v[0] for k, v in small_params.items()}
    gx, (r_in_a, gw_in_b), (r_o, r_gu, r_d), gathered = _local_step(
        xs, tgt, c, mod, w_in_full, conv_w_full, w_o[0].astype(WIRE_DTYPE), t_gu[0].astype(WIRE_DTYPE),
        w_down[0].astype(WIRE_DTYPE), p)

    (u_gu, u_d, u_o), (r_in_b,) = _reduce_adamw_hosting(
        [r_gu, r_d, r_o], [tuple(t_gu), (w_down[0], m_w_down[0], v_w_down[0]), (w_o[0], m_w_o[0], v_w_o[0])],
        "adamw_big", ([gw_in_b], True))
    r_in = jnp.concatenate([r_in_a, r_in_b], axis=2)

    small, loss, c_all, dmod_all = _small_update(gathered, small_params)

    big = {
        "ada_w": _ada_w_update(c_all, dmod_all, ada_w2, m_ada_w[0], v_ada_w[0]),
        "w_in": [jnp.transpose(a) for a in _reduce_adamw(r_in, *t_in, "adamw_w_in")],
        "w_o": u_o,
        "w_gate_up": [jnp.transpose(a) for a in u_gu],
        "w_down": u_d,
    }
    big.update(small)

    order = ['ada_w', 'ada_b', 'norm1', 'w_in', 'conv_w', 'conv_b', 'dt_bias', 'A_log', 'D_skip', 'sinks',
             'attn_out_norm', 'ssm_out_norm', 'w_o', 'norm2', 'w_gate_up', 'w_down', 'rel_bias', 'final_norm']
    shapes = dict(ada_w=ada_w.shape, ada_b=ada_b.shape, norm1=norm1.shape, w_in=w_in.shape, conv_w=conv_w.shape,
                  conv_b=conv_b.shape, dt_bias=dt_bias.shape, A_log=A_log.shape, D_skip=D_skip.shape,
                  sinks=sinks.shape, attn_out_norm=attn_out_norm.shape, ssm_out_norm=ssm_out_norm.shape,
                  w_o=w_o.shape, norm2=norm2.shape, w_gate_up=w_gate_up.shape, w_down=w_down.shape,
                  rel_bias=rel_bias.shape, final_norm=final_norm.shape)
    outs = [[], [], [], []]
    for name in order:
        for kind in range(4):
            outs[kind].append(big[name][kind].reshape(shapes[name]))
    return (loss.reshape(()), gx.reshape(x.shape), *outs[0], *outs[1], *outs[2], *outs[3])
```

```python
import numpy as np
import jax
import jax.numpy as jnp
from jax import lax
from jax.experimental import pallas as pl
from jax.experimental.pallas import tpu as pltpu

F32 = jnp.float32
MXU_DTYPE = jnp.bfloat16
WIRE_DTYPE = jnp.bfloat16
HI = lax.Precision.HIGHEST
MESH = pl.DeviceIdType.MESH
N_DEV = 8

D_MODEL = 1024
ATTN_W = 512
KV_W = 128
SSM_W = 512
XBC_W = 1024
N_HEADS = 8
D_STATE = 128
D_FF = 2816
IN_W = 2312
IN_PAD = 2432
BLK = 128
N_BUCKETS = 32
EPS = 1e-6
LANE = 128
HALF = 64

ADAM_LR, ADAM_B1, ADAM_B2, ADAM_EPS, ADAM_WD, ADAM_STEP = 0.001, 0.9, 0.999, 1e-08, 0.01, 10

VMEM_BIG = 56 * 1024 * 1024
WD_CUT = 288
WGU_CUTS = (240, 496)
GGU_CUT = 304


def _cparams(vmem=None):
    if vmem is None:
        return pltpu.CompilerParams()
    return pltpu.CompilerParams(vmem_limit_bytes=vmem)


def _mm(a, b):
    return jnp.dot(a.astype(MXU_DTYPE), b.astype(MXU_DTYPE), preferred_element_type=F32)


def _mm_nt(a, b):
    return lax.dot_general(a.astype(MXU_DTYPE), b.astype(MXU_DTYPE), (((1,), (1,)), ((), ())),
                           preferred_element_type=F32)


def _mm_tn(a, b):
    return lax.dot_general(a.astype(MXU_DTYPE), b.astype(MXU_DTYPE), (((0,), (0,)), ((), ())),
                           preferred_element_type=F32)


def _mm_hi(a, b):
    return jnp.dot(a, b, precision=HI, preferred_element_type=F32)


def _silu(x):
    return x * jax.nn.sigmoid(x)


def _softplus(x):
    return jnp.maximum(x, 0.0) + jnp.log1p(jnp.exp(-jnp.abs(x)))


def _rms(x, g, n):
    return x * lax.rsqrt(jnp.sum(x * x, axis=-1, keepdims=True) * (1.0 / n) + EPS) * g


def _modnorm(x, g, scale, shift):
    return _rms(x, g, x.shape[-1]) * (1.0 + scale) + shift


def _modnorm_parts(x):
    r = lax.rsqrt(jnp.sum(x * x, axis=-1, keepdims=True) * (1.0 / x.shape[-1]) + EPS)
    return r, x * r


def _modnorm_bwd(r, xhat, g, scale, dy):
    dyg = dy * (g * (1.0 + scale))
    c = jnp.sum(dyg * xhat, axis=-1, keepdims=True) * (1.0 / xhat.shape[-1])
    dx = r * (dyg - xhat * c)
    ct = jnp.sum(dy * xhat, axis=0, keepdims=True)
    return dx, ct * (1.0 + scale), ct * g, jnp.sum(dy, axis=0, keepdims=True)


def _lane_iota(shape):
    return lax.broadcasted_iota(jnp.int32, shape, len(shape) - 1)


def _split_pair(t):
    lane = _lane_iota(t.shape)
    lo = jnp.where(lane < HALF, t, 0.0)
    hi = pltpu.roll(jnp.where(lane >= HALF, t, 0.0), HALF, 1)
    return lo, hi


def _join_pair(lo, hi):
    lane = _lane_iota(lo.shape)
    return jnp.where(lane < HALF, lo, pltpu.roll(hi, HALF, 1))


def _split_heads(t, n_pairs):
    out = []
    for p in range(n_pairs):
        out.extend(_split_pair(t[:, p * LANE:(p + 1) * LANE]))
    return out


def _join_heads(hs):
    return jnp.concatenate([_join_pair(hs[2 * p], hs[2 * p + 1]) for p in range(len(hs) // 2)], axis=1)


def _t5_bucket_table():
    dist = np.arange(BLK)[:, None] + BLK - np.arange(2 * BLK)[None, :]
    n = np.maximum(dist, 0)
    max_exact = N_BUCKETS // 2
    large = max_exact + (np.log(np.maximum(n, 1) / max_exact) / np.log(128 / max_exact)
                         * (N_BUCKETS - max_exact)).astype(np.int32)
    large = np.minimum(large, N_BUCKETS - 1)
    return np.where(n < max_exact, n, large).astype(np.int32)


def _my_pos():
    return lax.axis_index("x"), lax.axis_index("y"), lax.axis_index("c")


def _peer(k):
    x, y, c = _my_pos()
    return (1 - x if k & 4 else x, 1 - y if k & 2 else y, 1 - c if k & 1 else c)


def _lin(pos):
    return 4 * pos[0] + 2 * pos[1] + pos[2]


def _xchg_copies(ins, outs, sems, scatter):
    local_sem, send_sem, recv_sem = sems
    me = _lin(_my_pos())

    def source(a, slot):
        if not scatter:
            return ins[a]
        if scatter is True:
            return ins[a].at[slot]
        return ins[a].at[slot, pl.ds(scatter[1], scatter[2])]

    local, remote = [], []
    for a in range(len(ins)):
        local.append(pltpu.make_async_copy(source(a, me), outs[a].at[me], local_sem.at[a]))
    for k in range(1, N_DEV):
        peer = _peer(k)
        for a in range(len(ins)):
            remote.append(pltpu.make_async_remote_copy(source(a, _lin(peer)), outs[a].at[me], send_sem.at[a, k - 1],
                                                       recv_sem.at[a, k - 1], device_id=peer, device_id_type=MESH))
    return local, remote


def _xchg_start(ins, outs, sems, scatter):
    local, remote = _xchg_copies(ins, outs, sems, scatter)
    for cp in local + remote:
        cp.start()


def _xchg_wait(ins, outs, sems, scatter):
    local, remote = _xchg_copies(ins, outs, sems, scatter)
    for cp in local:
        cp.wait()
    for cp in remote:
        cp.wait_send()
        cp.wait_recv()


def _xchg_shapes(arrs, scatter):
    n = len(arrs)
    if isinstance(scatter, tuple):
        out_shape = [jax.ShapeDtypeStruct((a.shape[0], scatter[2]) + a.shape[2:], a.dtype) for a in arrs]
    elif scatter:
        out_shape = [jax.ShapeDtypeStruct(a.shape, a.dtype) for a in arrs]
    else:
        out_shape = [jax.ShapeDtypeStruct((N_DEV,) + a.shape, a.dtype) for a in arrs]
    sems = [pltpu.SemaphoreType.DMA((n,)), pltpu.SemaphoreType.DMA((n, N_DEV - 1)),
            pltpu.SemaphoreType.DMA((n, N_DEV - 1))]
    return out_shape, sems


_CHIPS = (2, 4, 6)


def _g2_sems(n):
    dma = pltpu.SemaphoreType.DMA
    return [dma((n,)), dma((n, N_DEV)), dma((n, N_DEV)), dma((n, len(_CHIPS))), dma((n, len(_CHIPS)))]


class _TwoLevelGather:
    def __init__(self, ins, outs, sems, windows=None):
        self.ins, self.outs = ins, outs
        self.local_sem, self.send_sem, self.recv_sem, self.fsend_sem, self.frecv_sem = sems
        self.n = len(ins)
        self.windows = windows or [None] * self.n

    def _mine(self, a):
        w = self.windows[a]
        return self.ins[a] if w is None else self.ins[a].at[pl.ds(w[0], w[1])]

    def _direct(self, a, k):
        return pltpu.make_async_remote_copy(self._mine(a), self.outs[a].at[_lin(_my_pos())], self.send_sem.at[a, k],
                                            self.recv_sem.at[a, k], device_id=_peer(k), device_id_type=MESH)

    def _handed_on(self, a, j, origin):
        slot = self.outs[a].at[origin]
        return pltpu.make_async_remote_copy(slot, slot, self.fsend_sem.at[a, j], self.frecv_sem.at[a, j],
                                            device_id=_peer(1), device_id_type=MESH)

    def _local(self, a):
        return pltpu.make_async_copy(self._mine(a), self.outs[a].at[_lin(_my_pos())], self.local_sem.at[a])

    def start(self):
        for a in range(self.n):
            self._local(a).start()
        for k in (1,) + _CHIPS:
            for a in range(self.n):
                self._direct(a, k).start()

    def forward(self):
        for j, k in enumerate(_CHIPS):
            for a in range(self.n):
                self._direct(a, k).wait_recv()
                self._handed_on(a, j, _lin(_peer(k))).start()

    def finish(self):
        for a in range(self.n):
            self._direct(a, 1).wait_recv()
            for j, k in enumerate(_CHIPS):
                self._handed_on(a, j, _lin(_peer(k ^ 1))).wait_recv()
            self._local(a).wait()
            for k in (1,) + _CHIPS:
                self._direct(a, k).wait_send()
            for j, k in enumerate(_CHIPS):
                self._handed_on(a, j, _lin(_peer(k))).wait_send()


def _mod_and_gather(c, ada_w, ada_b8, arrs):
    n = len(arrs)
    chunk = ada_w.shape[1]
    out_shape = [jax.ShapeDtypeStruct((N_DEV, 1, chunk), F32)]
    out_shape += [jax.ShapeDtypeStruct((N_DEV,) + a.shape, a.dtype) for a in arrs]

    def modulation(c_ref, w_ref, b_ref, out_ref, cbuf, part, s1, r1, s2, r2):
        me = _lin(_my_pos())
        first = []
        for k in range(1, N_DEV):
            cp = pltpu.make_async_remote_copy(c_ref, cbuf.at[me], s1.at[k - 1], r1.at[k - 1],
                                              device_id=_peer(k), device_id_type=MESH)
            cp.start()
            first.append(cp)
        cbuf[me] = c_ref[...]
        for cp in first:
            cp.wait_send()
            cp.wait_recv()
        cond = _silu(jnp.concatenate([cbuf[i] for i in range(N_DEV)], axis=0))
        mod = _mm_hi(cond, w_ref[...]) + b_ref[pl.ds(me, 1), :]
        for j in range(N_DEV):
            part[j] = mod[j:j + 1, :]
        second = []
        for k in range(1, N_DEV):
            peer = _peer(k)
            cp = pltpu.make_async_remote_copy(part.at[_lin(peer)], out_ref.at[me], s2.at[k - 1], r2.at[k - 1],
                                              device_id=peer, device_id_type=MESH)
            cp.start()
            second.append(cp)
        out_ref[me] = part[me]
        for cp in second:
            cp.wait_send()
            cp.wait_recv()

    def body(*refs):
        c_ref, w_ref, b_ref = refs[:3]
        ins = refs[3:3 + n]
        mod_ref = refs[3 + n]
        outs = refs[4 + n:4 + 2 * n]
        cbuf, part, s1, r1, s2, r2 = refs[4 + 2 * n:10 + 2 * n]
        gather = _TwoLevelGather(ins, outs, refs[10 + 2 * n:])
        gather.start()
        modulation(c_ref, w_ref, b_ref, mod_ref, cbuf, part, s1, r1, s2, r2)
        gather.forward()
        gather.finish()

    hbm = pl.BlockSpec(memory_space=pltpu.HBM)
    vm = pl.BlockSpec(memory_space=pltpu.VMEM)
    dma = pltpu.SemaphoreType.DMA
    res = pl.pallas_call(
        body, name="mod_and_gather", out_shape=out_shape, in_specs=[vm, vm, vm] + [hbm] * n,
        out_specs=[vm] + [hbm] * n,
        scratch_shapes=[pltpu.VMEM((N_DEV, 1, D_MODEL), F32), pltpu.VMEM((N_DEV, 1, chunk), F32)]
        + [dma((N_DEV - 1,))] * 4 + _g2_sems(n),
    )(c, ada_w, ada_b8, *arrs)
    return res[0], res[1:]


def _hosted_call(body, name, grid, in_specs, out_specs, out_shape, scratch_shapes, args, xchg, cparams):
    xchgs = [xchg] if isinstance(xchg, tuple) else list(xchg)
    grid = (grid,) if isinstance(grid, int) else tuple(grid)
    n_in, n_out, n_scr = len(in_specs), len(out_specs), len(scratch_shapes)
    windows = [[(a[1], a[2]) if isinstance(a, tuple) else None for a in group] for group, _ in xchgs]
    xchgs = [([a[0] if isinstance(a, tuple) else a for a in group], mode) for group, mode in xchgs]
    arrs = [a for group, _ in xchgs for a in group]
    n = len(arrs)
    x_shape, x_sems, sem_counts = [], [], []
    for (group, mode), wins in zip(xchgs, windows):
        shapes, sems = _xchg_shapes(group, False if mode == "two-level" else mode)
        if mode == "two-level":
            sems = _g2_sems(len(group))
            shapes = [s if w is None else jax.ShapeDtypeStruct((N_DEV, w[1]) + a.shape[1:], a.dtype)
                      for s, w, a in zip(shapes, wins, group)]
        x_shape += shapes
        x_sems += sems
        sem_counts.append(len(sems))
    n_steps = int(np.prod(grid))

    def hosted(*refs):
        ins, refs = refs[:n_in], refs[n_in:]
        x_in, refs = refs[:n], refs[n:]
        outs, refs = refs[:n_out], refs[n_out:]
        x_out, refs = refs[:n], refs[n:]
        scr, sems = refs[:n_scr], refs[n_scr:]
        step = pl.program_id(0)
        for d in range(1, len(grid)):
            step = step * grid[d] + pl.program_id(d)
        parts, a0, s0 = [], 0, 0
        for (group, mode), ns, wins in zip(xchgs, sem_counts, windows):
            parts.append((x_in[a0:a0 + len(group)], x_out[a0:a0 + len(group)], sems[s0:s0 + ns], mode, wins))
            a0, s0 = a0 + len(group), s0 + ns

        @pl.when(step == 0)
        def _():
            for gi, go, gs, mode, wins in parts:
                if mode == "two-level":
                    _TwoLevelGather(gi, go, gs, wins).start()
                else:
                    _xchg_start(gi, go, gs, mode)

        if any(mode == "two-level" for _, mode in xchgs):
            @pl.when(step == (2 * n_steps) // 3)
            def _():
                for gi, go, gs, mode, wins in parts:
                    if mode == "two-level":
                        _TwoLevelGather(gi, go, gs, wins).forward()

        body(*ins, *outs, *scr)

        @pl.when(step == n_steps - 1)
        def _():
            for gi, go, gs, mode, wins in parts:
                if mode == "two-level":
                    _TwoLevelGather(gi, go, gs, wins).finish()
                else:
                    _xchg_wait(gi, go, gs, mode)

    hbm = pl.BlockSpec(memory_space=pltpu.HBM)
    res = pl.pallas_call(
        hosted, name=name, grid=grid, in_specs=list(in_specs) + [hbm] * n,
        out_specs=list(out_specs) + [hbm] * n, out_shape=list(out_shape) + x_shape,
        scratch_shapes=list(scratch_shapes) + x_sems, compiler_params=cparams,
    )(*args, *arrs)
    return res[:n_out], res[n_out:]


def _row(i):
    return (i, 0)


def _fixed(i):
    return (0, 0)


def _in_proj_fwd(x, norm1, scale1, shift1, w_in, tm, xchg):
    S = x.shape[0]

    def body(x_ref, n_ref, sc_ref, sh_ref, w_ref, qkv_ref, z_ref, xbc_ref, dt_ref):
        h = _modnorm(x_ref[...], n_ref[...], sc_ref[...], sh_ref[...])
        p = _mm_nt(h, w_ref[...])
        qkv_ref[...] = p[:, :768].astype(qkv_ref.dtype)
        z_ref[...] = p[:, 768:1280]
        xbc_ref[...] = p[:, 1280:2304]
        dt_ref[...] = p[:, 2304:IN_PAD]

    vec = pl.BlockSpec((1, D_MODEL), _fixed)
    return _hosted_call(
        body, "in_proj_fwd", S // tm,
        in_specs=[pl.BlockSpec((tm, D_MODEL), _row), vec, vec, vec, pl.BlockSpec((IN_PAD, D_MODEL), _fixed)],
        out_specs=[pl.BlockSpec((tm, 768), _row), pl.BlockSpec((tm, SSM_W), _row),
                   pl.BlockSpec((tm, XBC_W), _row), pl.BlockSpec((tm, LANE), _row)],
        out_shape=[jax.ShapeDtypeStruct((S, 768), MXU_DTYPE), jax.ShapeDtypeStruct((S, SSM_W), F32),
                   jax.ShapeDtypeStruct((S, XBC_W), F32), jax.ShapeDtypeStruct((S, LANE), F32)],
        scratch_shapes=[], args=(x, norm1, scale1, shift1, w_in), xchg=xchg, cparams=_cparams(VMEM_BIG),
    )


def _in_proj_bwd(x, dx1, dq, dkv, dz, dxbc, ddt, norm1, scale1, shift1, w_in, tm):
    S = x.shape[0]

    n_steps = S // tm
    half_cols = D_MODEL // 2

    def body(x_ref, dx1_ref, dq_ref, dkv_ref, dz_ref, dxbc_ref, ddt_ref, n_ref, sc_ref, sh_ref, w_ref,
             gx_ref, h_ref, acc_ref, gw_ref, gw_acc):
        i = pl.program_id(0)

        @pl.when(i == 0)
        def _():
            acc_ref[...] = jnp.zeros_like(acc_ref)
            gw_acc[...] = jnp.zeros_like(gw_acc)

        halves = [pl.ds(k * (tm // 2), tm // 2) for k in range(2)]
        dp = [jnp.concatenate([r[rows, :] for r in (dq_ref, dkv_ref, dz_ref, dxbc_ref, ddt_ref)], axis=1)
              for rows in halves]
        dh = [_mm(dp[k], w_ref[...]) for k in range(2)]
        parts = [_modnorm_parts(x_ref[rows, :]) for rows in halves]
        hb = [(parts[k][1] * n_ref[...] * (1.0 + sc_ref[...]) + sh_ref[...]).astype(h_ref.dtype) for k in range(2)]
        gw_acc[...] += _mm_tn(dp[0], hb[0][:, :half_cols]) + _mm_tn(dp[1], hb[1][:, :half_cols])
        bwd = [_modnorm_bwd(parts[k][0], parts[k][1], n_ref[...], sc_ref[...], dh[k]) for k in range(2)]
        for k, rows in enumerate(halves):
            gx_ref[rows, :] = dx1_ref[rows, :] + bwd[k][0]
            h_ref[rows, :] = hb[k]
        acc_ref[0:1, :] += bwd[0][1] + bwd[1][1]
        acc_ref[1:2, :] += bwd[0][2] + bwd[1][2]
        acc_ref[2:3, :] += bwd[0][3] + bwd[1][3]

        @pl.when(i == n_steps - 1)
        def _():
            gw_ref[...] = gw_acc[...].astype(gw_ref.dtype)

    vec = pl.BlockSpec((1, D_MODEL), _fixed)
    return pl.pallas_call(
        body, name="in_proj_bwd", grid=(n_steps,),
        in_specs=[pl.BlockSpec((tm, D_MODEL), _row), pl.BlockSpec((tm, D_MODEL), _row),
                  pl.BlockSpec((tm, ATTN_W), _row), pl.BlockSpec((tm, 2 * KV_W), _row),
                  pl.BlockSpec((tm, SSM_W), _row), pl.BlockSpec((tm, XBC_W), _row), pl.BlockSpec((tm, LANE), _row),
                  vec, vec, vec, pl.BlockSpec((IN_PAD, D_MODEL), _fixed)],
        out_specs=[pl.BlockSpec((tm, D_MODEL), _row), pl.BlockSpec((tm, D_MODEL), _row),
                   pl.BlockSpec((8, D_MODEL), _fixed), pl.BlockSpec((IN_PAD, half_cols), _fixed)],
        out_shape=[jax.ShapeDtypeStruct((S, D_MODEL), F32), jax.ShapeDtypeStruct((S, D_MODEL), MXU_DTYPE),
                   jax.ShapeDtypeStruct((8, D_MODEL), F32), jax.ShapeDtypeStruct((IN_PAD, half_cols), WIRE_DTYPE)],
        scratch_shapes=[pltpu.VMEM((IN_PAD, half_cols), F32)],
        compiler_params=_cparams(VMEM_BIG),
    )(x, dx1, dq, dkv, dz, dxbc, ddt, norm1, scale1, shift1, w_in)


def _out_stage(ya, ys0, ys1, z0, z1, an, sn0, sn1):
    half = SSM_W // 2
    a = _rms(ya, an, ATTN_W)
    g0 = _rms(ys0 * _silu(z0), sn0, half)
    g1 = _rms(ys1 * _silu(z1), sn1, half)
    return jnp.concatenate([a, g0, g1], axis=1)


def _out_stage_args(ya_ref, ys_ref, z_ref, an_ref, sn_ref):
    half = SSM_W // 2
    return (ya_ref[...], ys_ref[:, :half], ys_ref[:, half:], z_ref[:, :half], z_ref[:, half:],
            an_ref[...], sn_ref[:, :half], sn_ref[:, half:])


def _out_proj_fwd(x, ya, ys, z, an, sn, gate1, w_o, tm, xchg):
    S = x.shape[0]

    def body(x_ref, ya_ref, ys_ref, z_ref, an_ref, sn_ref, g_ref, w_ref, x1_ref):
        u = _out_stage(*_out_stage_args(ya_ref, ys_ref, z_ref, an_ref, sn_ref))
        x1_ref[...] = x_ref[...] + g_ref[...] * _mm(u, w_ref[...])

    half = pl.BlockSpec((tm, ATTN_W), _row)
    hvec = pl.BlockSpec((1, ATTN_W), _fixed)
    (x1,), x_out = _hosted_call(
        body, "out_proj_fwd", S // tm,
        in_specs=[pl.BlockSpec((tm, D_MODEL), _row), half, half, half, hvec, hvec,
                  pl.BlockSpec((1, D_MODEL), _fixed), pl.BlockSpec((D_MODEL, D_MODEL), _fixed)],
        out_specs=[pl.BlockSpec((tm, D_MODEL), _row)],
        out_shape=[jax.ShapeDtypeStruct((S, D_MODEL), F32)],
        scratch_shapes=[], args=(x, ya, ys, z, an, sn, gate1, w_o), xchg=xchg, cparams=_cparams(VMEM_BIG),
    )
    return x1, x_out


def _out_proj_bwd(dx1, ya, ys, z, an, sn, gate1, w_o, tm, xchg):
    S = dx1.shape[0]
    n_steps = S // tm

    def body(dx1_ref, ya_ref, ys_ref, z_ref, an_ref, sn_ref, g_ref, w_ref,
             dya_ref, dys_ref, dz_ref, gw_ref, acc_ref, gw_acc):
        i = pl.program_id(0)

        @pl.when(i == 0)
        def _():
            acc_ref[...] = jnp.zeros_like(acc_ref)
            gw_acc[...] = jnp.zeros_like(gw_acc)

        u, vjp = jax.vjp(_out_stage, *_out_stage_args(ya_ref, ys_ref, z_ref, an_ref, sn_ref))
        dx1 = dx1_ref[...]
        ub = u.astype(MXU_DTYPE)
        mix = _mm(ub, w_ref[...])
        dmix = dx1 * g_ref[...]
        dmixb = dmix.astype(MXU_DTYPE)
        du = _mm_nt(dmixb, w_ref[...])
        gw_acc[...] += _mm_tn(ub, dmixb)
        dya, dys0, dys1, dz0, dz1, dan, dsn0, dsn1 = vjp(du)
        dya_ref[...] = dya
        dys_ref[...] = jnp.concatenate([dys0, dys1], axis=1)
        dz_ref[...] = jnp.concatenate([dz0, dz1], axis=1).astype(dz_ref.dtype)
        acc_ref[0:1, :] += jnp.sum(dx1 * mix, axis=0, keepdims=True)
        acc_ref[1:2, :] += jnp.concatenate([dan, dsn0, dsn1], axis=1)

        @pl.when(i == n_steps - 1)
        def _():
            gw_ref[...] = gw_acc[...].astype(gw_ref.dtype)

    half = pl.BlockSpec((tm, ATTN_W), _row)
    hvec = pl.BlockSpec((1, ATTN_W), _fixed)
    full = pl.BlockSpec((tm, D_MODEL), _row)
    return _hosted_call(
        body, "out_proj_bwd", n_steps,
        in_specs=[full, half, half, half, hvec, hvec,
                  pl.BlockSpec((1, D_MODEL), _fixed), pl.BlockSpec((D_MODEL, D_MODEL), _fixed)],
        out_specs=[half, half, half, pl.BlockSpec((D_MODEL, D_MODEL), _fixed), pl.BlockSpec((8, D_MODEL), _fixed)],
        out_shape=[jax.ShapeDtypeStruct((S, ATTN_W), F32)] * 2 + [jax.ShapeDtypeStruct((S, ATTN_W), MXU_DTYPE),
                   jax.ShapeDtypeStruct((D_MODEL, D_MODEL), WIRE_DTYPE), jax.ShapeDtypeStruct((8, D_MODEL), F32)],
        scratch_shapes=[pltpu.VMEM((D_MODEL, D_MODEL), F32)],
        args=(dx1, ya, ys, z, an, sn, gate1, w_o), xchg=xchg, cparams=_cparams(VMEM_BIG),
    )


def _loss_rows(x2, fn, tgt):
    y = _rms(x2, fn, D_MODEL)
    per_row = jnp.sum(jnp.square(y - tgt), axis=1, keepdims=True)
    return jnp.sum(per_row, axis=0, keepdims=True) * (0.5 / D_MODEL)


def _mlp_loss(x1, tgt, norm2, scale2, shift2, gate2, fnorm, w_gu, w_d, tm):
    S = x1.shape[0]
    n_pieces = len(w_gu) + len(w_d)

    def body(*refs):
        x1_ref, t_ref, n_ref, sc_ref, sh_ref, g_ref, fn_ref = refs[:7]
        piece_refs = refs[7:7 + n_pieces]
        dx1_ref, h_ref, dgu_ref, act_ref, dmlp_ref, acc_ref, wgu, wd, wsem = refs[7 + n_pieces:]

        @pl.when(pl.program_id(0) == 0)
        def _():
            acc_ref[...] = jnp.zeros_like(acc_ref)
            copies = []
            for dst, pieces in ((wgu, piece_refs[:len(w_gu)]), (wd, piece_refs[len(w_gu):])):
                shard = sum(p.shape[1] for p in pieces)
                off = 0
                for p in pieces:
                    for j in range(N_DEV):
                        copies.append(pltpu.make_async_copy(p.at[j], dst.at[pl.ds(j * shard + off, p.shape[1])],
                                                            wsem.at[len(copies)]))
                    off += p.shape[1]
            for cp in copies:
                cp.start()
            for cp in copies:
                cp.wait()

        x1 = x1_ref[...]
        gate2 = g_ref[...]
        h, vjp_h = jax.vjp(_modnorm, x1, n_ref[...], sc_ref[...], sh_ref[...])
        hb = h.astype(MXU_DTYPE)
        gu = _mm_nt(hb, wgu[...])
        g, u = gu[:, :D_FF], gu[:, D_FF:]
        sg = jax.nn.sigmoid(g)
        silu_g = g * sg
        act = (silu_g * u).astype(MXU_DTYPE)
        mlp = _mm(act, wd[...])
        x2 = x1 + gate2 * mlp
        loss, vjp_loss = jax.vjp(_loss_rows, x2, fn_ref[...], t_ref[...])
        dx2, dfn, _ = vjp_loss(jnp.ones((1, 1), F32))
        dmlp = (dx2 * gate2).astype(MXU_DTYPE)
        dact = _mm_nt(dmlp, wd[...])
        dg = dact * u * (sg * (1.0 + g * (1.0 - sg)))
        du = dact * silu_g
        dgu = jnp.concatenate([dg, du], axis=1).astype(MXU_DTYPE)
        dh = _mm(dgu, wgu[...])
        dx, dn, dsc, dsh = vjp_h(dh)
        dx1_ref[...] = dx2 + dx
        h_ref[...] = hb
        dgu_ref[...] = dgu
        act_ref[...] = act
        dmlp_ref[...] = dmlp
        acc_ref[0:1, :] += dn
        acc_ref[1:2, :] += dsc
        acc_ref[2:3, :] += dsh
        acc_ref[3:4, :] += jnp.sum(dx2 * mlp, axis=0, keepdims=True)
        acc_ref[4:5, :] += dfn
        acc_ref[5:6, :] += jnp.broadcast_to(loss, (1, D_MODEL))

    full = pl.BlockSpec((tm, D_MODEL), _row)
    vec = pl.BlockSpec((1, D_MODEL), _fixed)
    anyspec = pl.BlockSpec(memory_space=pl.ANY)
    return pl.pallas_call(
        body, name="mlp_loss", grid=(S // tm,),
        in_specs=[full, full, vec, vec, vec, vec, vec] + [anyspec] * n_pieces,
        out_specs=[full, full, pl.BlockSpec((tm, 2 * D_FF), _row), pl.BlockSpec((tm, D_FF), _row), full,
                   pl.BlockSpec((8, D_MODEL), _fixed)],
        out_shape=[jax.ShapeDtypeStruct((S, D_MODEL), F32), jax.ShapeDtypeStruct((S, D_MODEL), MXU_DTYPE),
                   jax.ShapeDtypeStruct((S, 2 * D_FF), MXU_DTYPE), jax.ShapeDtypeStruct((S, D_FF), MXU_DTYPE),
                   jax.ShapeDtypeStruct((S, D_MODEL), MXU_DTYPE), jax.ShapeDtypeStruct((8, D_MODEL), F32)],
        scratch_shapes=[pltpu.VMEM((2 * D_FF, D_MODEL), MXU_DTYPE), pltpu.VMEM((D_FF, D_MODEL), MXU_DTYPE),
                        pltpu.SemaphoreType.DMA((N_DEV * n_pieces,))],
        compiler_params=_cparams(VMEM_BIG),
    )(x1, tgt, norm2, scale2, shift2, gate2, fnorm, *w_gu, *w_d)


def _wgrad(a, g, tk, ts, name, xchg=None, g_cols=None):
    pieces = list(a) if isinstance(a, (list, tuple)) else [a]
    S = pieces[0].shape[0]
    K = sum(p.shape[1] for p in pieces)
    assert len(pieces) == 1 or tk == K
    N, col = (g.shape[1], 0) if g_cols is None else g_cols
    ns = S // ts
    n_a = len(pieces)

    def body(*refs):
        a_refs, (g_ref, o_ref, acc_ref) = refs[:n_a], refs[n_a:]
        s = pl.program_id(1)

        @pl.when(s == 0)
        def _():
            acc_ref[...] = jnp.zeros_like(acc_ref)

        a_blk = a_refs[0][...] if n_a == 1 else jnp.concatenate([r[...] for r in a_refs], axis=1)
        acc_ref[...] += _mm_tn(a_blk, g_ref[...])

        @pl.when(s == ns - 1)
        def _():
            o_ref[...] = acc_ref[...].astype(o_ref.dtype)

    if n_a == 1:
        in_specs = [pl.BlockSpec((ts, tk), lambda j, s: (s, j))]
    else:
        in_specs = [pl.BlockSpec((ts, p.shape[1]), lambda j, s: (s, 0)) for p in pieces]
    in_specs.append(pl.BlockSpec((ts, N), lambda j, s: (s, col)))
    out_spec = pl.BlockSpec((tk, N), lambda j, s: (j, 0))
    out_shape = jax.ShapeDtypeStruct((K, N), WIRE_DTYPE)
    scratch = [pltpu.VMEM((tk, N), F32)]
    args = (*pieces, g)
    if xchg is None:
        return pl.pallas_call(body, name=name, grid=(K // tk, ns), in_specs=in_specs, out_specs=out_spec,
                              out_shape=out_shape, scratch_shapes=scratch, compiler_params=_cparams(VMEM_BIG))(*args)
    (out,), x_out = _hosted_call(body, name, (K // tk, ns), in_specs, [out_spec], [out_shape], scratch, args, xchg,
                                 _cparams(VMEM_BIG))
    return out, x_out


SSD_CHUNKS_PER_STEP = 4
SSD_BWD_CHUNKS_PER_STEP = 4
ATTN_BLOCKS_PER_STEP = 4
MASKED = -1e30
QK_SCALE = HALF ** -0.5


def _attn_bias(buckets, rel_bias):
    def body(bk_ref, relb_ref, out_ref):
        bk = bk_ref[...]
        i = lax.broadcasted_iota(jnp.int32, (BLK, 2 * BLK), 0)
        j = lax.broadcasted_iota(jnp.int32, (BLK, 2 * BLK), 1)
        window = (j > i) & (j <= i + BLK)
        for h in range(N_HEADS):
            acc = jnp.zeros((BLK, 2 * BLK), F32)
            for b in range(N_BUCKETS):
                acc = jnp.where(bk == b, relb_ref[b, h], acc)
            out_ref[0, h] = jnp.where(window, acc, MASKED)
            out_ref[1, h] = jnp.where(window & (j >= BLK), acc, MASKED)

    return pl.pallas_call(
        body, name="attn_bias", out_shape=jax.ShapeDtypeStruct((2, N_HEADS, BLK, 2 * BLK), F32),
        in_specs=[pl.BlockSpec(memory_space=pltpu.VMEM), pl.BlockSpec(memory_space=pltpu.SMEM)],
    )(buckets, rel_bias)


def _attn_fwd(qkv, bias, sinks, xchg):
    S = qkv.shape[0]
    nb = S // BLK

    nq = ATTN_BLOCKS_PER_STEP if nb % ATTN_BLOCKS_PER_STEP == 0 else 1
    rows = nq * BLK

    def body(q_ref, kvp_ref, kvc_ref, bias_ref, sinks_ref, y_ref):
        i = pl.program_id(0)
        q = q_ref[...].astype(F32) * QK_SCALE
        kv = jnp.concatenate([kvp_ref[...], kvc_ref[...]], axis=0).astype(F32)
        k_lo, k_hi = _split_pair(kv[:, :LANE])
        v_lo, v_hi = _split_pair(kv[:, LANE:])
        bands = [[t[b * BLK:(b + 2) * BLK].astype(MXU_DTYPE) for t in (k_lo, k_hi, v_lo, v_hi)] for b in range(nq)]
        q_heads = [_split_heads(q[b * BLK:(b + 1) * BLK], 4) for b in range(nq)]
        first = [jnp.where(i == 0, 1, 0) if b == 0 else 0 for b in range(nq)]
        items = [(b, h) for b in range(nq) for h in range(N_HEADS)]
        s = [_mm_nt(q_heads[b][h].astype(MXU_DTYPE), bands[b][h // 4]) + bias_ref[first[b], h] for b, h in items]
        m = [jnp.maximum(jnp.max(s[n], axis=-1, keepdims=True), sinks_ref[h]) for n, (b, h) in enumerate(items)]
        p = [jnp.exp(s[n] - m[n]) for n in range(len(items))]
        rinv = [1.0 / (jnp.sum(p[n], axis=-1, keepdims=True) + jnp.exp(sinks_ref[h] - m[n]))
                for n, (b, h) in enumerate(items)]
        out = [_mm(p[n], bands[b][2 + h // 4]) * rinv[n] for n, (b, h) in enumerate(items)]
        y_ref[...] = jnp.concatenate([_join_heads(out[b * N_HEADS:(b + 1) * N_HEADS]) for b in range(nq)], axis=0)

    smem = pl.BlockSpec(memory_space=pltpu.SMEM)
    return _hosted_call(
        body, "attn_fwd", nb // nq,
        in_specs=[pl.BlockSpec((rows, ATTN_W), _row),
                  pl.BlockSpec((BLK, 2 * KV_W), lambda i: (jnp.maximum(i * nq - 1, 0), 2)),
                  pl.BlockSpec((rows, 2 * KV_W), lambda i: (i, 2)),
                  pl.BlockSpec((2, N_HEADS, BLK, 2 * BLK), lambda i: (0, 0, 0, 0)), smem],
        out_specs=[pl.BlockSpec((rows, ATTN_W), _row)],
        out_shape=[jax.ShapeDtypeStruct((S, ATTN_W), F32)],
        scratch_shapes=[],
        args=(qkv, qkv, qkv, bias, sinks), xchg=xchg, cparams=_cparams(),
    )


def _attn_bwd(qkv, y, dy, bias, sinks, xchg):
    S = qkv.shape[0]
    nb = S // BLK
    nq = ATTN_BLOCKS_PER_STEP if nb % ATTN_BLOCKS_PER_STEP == 0 else 1
    rows, n_steps = nq * BLK, nb // nq

    def body(q_ref, kvp_ref, kvc_ref, y_ref, dy_ref, bias_ref, sinks_ref, dq_ref, dkv_ref, dbias_ref, dsk_ref, carry_ref):
        i = pl.program_id(0)

        @pl.when(i == 0)
        def _():
            dbias_ref[...] = jnp.zeros_like(dbias_ref)
            dsk_ref[...] = jnp.zeros_like(dsk_ref)
            carry_ref[...] = jnp.zeros_like(carry_ref)

        q = q_ref[...].astype(F32) * QK_SCALE
        kv = jnp.concatenate([kvp_ref[...], kvc_ref[...]], axis=0).astype(F32)
        k_lo, k_hi = _split_pair(kv[:, :LANE])
        v_lo, v_hi = _split_pair(kv[:, LANE:])
        bands = [[t[b * BLK:(b + 2) * BLK].astype(MXU_DTYPE) for t in (k_lo, k_hi, v_lo, v_hi)] for b in range(nq)]
        rows_of = lambda ref, b: ref[b * BLK:(b + 1) * BLK, :]
        first = [jnp.where(i == n_steps - 1, 1, 0) if b == 0 else 0 for b in range(nq)]
        items = [(b, h) for b in range(nq) for h in range(N_HEADS)]
        at = lambda b, h: b * N_HEADS + h
        q_heads = [hd for b in range(nq) for hd in _split_heads(q[b * BLK:(b + 1) * BLK], 4)]
        y_heads = [hd for b in range(nq) for hd in _split_heads(rows_of(y_ref, b), 4)]
        dy_heads = [hd for b in range(nq) for hd in _split_heads(rows_of(dy_ref, b), 4)]
        qs = [q_heads[n].astype(MXU_DTYPE) for n in range(len(items))]
        s = [_mm_nt(qs[at(b, h)], bands[b][h // 4]) + bias_ref[first[b], h] for b, h in items]
        m = [jnp.maximum(jnp.max(s[at(b, h)], axis=-1, keepdims=True), sinks_ref[h]) for b, h in items]
        p = [jnp.exp(s[n] - m[n]) for n in range(len(items))]
        esink = [jnp.exp(sinks_ref[h] - m[at(b, h)]) for b, h in items]
        rinv = [1.0 / (jnp.sum(p[n], axis=-1, keepdims=True) + esink[n]) for n in range(len(items))]
        t = [dy_heads[n] * rinv[n] for n in range(len(items))]
        delta = [jnp.sum(t[n] * y_heads[n], axis=-1, keepdims=True) for n in range(len(items))]
        tb = [t[n].astype(MXU_DTYPE) for n in range(len(items))]
        dp = [_mm_nt(tb[at(b, h)], bands[b][2 + h // 4]) for b, h in items]
        ds = [p[n] * (dp[n] - delta[n]) for n in range(len(items))]
        for h in range(N_HEADS):
            ds_h, dsk_h = ds[at(0, h)], esink[at(0, h)] * delta[at(0, h)]
            for b in range(1, nq):
                ds_h = ds_h + ds[at(b, h)]
                dsk_h = dsk_h + esink[at(b, h)] * delta[at(b, h)]
            dbias_ref[h] += ds_h
            dsk_ref[h] -= dsk_h
        dsb = [ds[n].astype(MXU_DTYPE) for n in range(len(items))]
        pb = [p[n].astype(MXU_DTYPE) for n in range(len(items))]
        dq_heads = [_mm(dsb[at(b, h)], bands[b][h // 4]) * QK_SCALE for b, h in items]
        grp = lambda lst, b, g: jnp.concatenate(lst[at(b, 4 * g):at(b, 4 * g) + 4], axis=0)
        dk_pads = [[_mm_tn(grp(dsb, b, g), grp(qs, b, g)) for g in range(2)] for b in range(nq)]
        dv_pads = [[_mm_tn(grp(pb, b, g), grp(tb, b, g)) for g in range(2)] for b in range(nq)]
        dq_ref[...] = jnp.concatenate([_join_heads(dq_heads[b * N_HEADS:(b + 1) * N_HEADS]) for b in range(nq)],
                                      axis=0).astype(dq_ref.dtype)
        part = lambda b, lo: jnp.concatenate(
            [_join_pair(d[b][0][lo:lo + BLK], d[b][1][lo:lo + BLK]) for d in (dk_pads, dv_pads)], axis=1)
        dkv = [part(b, BLK) + (part(b + 1, 0) if b + 1 < nq else carry_ref[...]) for b in range(nq)]
        dkv_ref[...] = jnp.concatenate(dkv, axis=0).astype(dkv_ref.dtype)
        carry_ref[...] = part(0, 0)

    smem = pl.BlockSpec(memory_space=pltpu.SMEM)
    rev = lambda i: (n_steps - 1 - i, 0)
    return _hosted_call(
        body, "attn_bwd", n_steps,
        in_specs=[pl.BlockSpec((rows, ATTN_W), rev),
                  pl.BlockSpec((BLK, 2 * KV_W), lambda i: (jnp.maximum((n_steps - 1 - i) * nq - 1, 0), 2)),
                  pl.BlockSpec((rows, 2 * KV_W), lambda i: (n_steps - 1 - i, 2)),
                  pl.BlockSpec((rows, ATTN_W), rev), pl.BlockSpec((rows, ATTN_W), rev),
                  pl.BlockSpec((2, N_HEADS, BLK, 2 * BLK), lambda i: (0, 0, 0, 0)), smem],
        out_specs=[pl.BlockSpec((rows, ATTN_W), rev), pl.BlockSpec((rows, 2 * KV_W), rev),
                   pl.BlockSpec((N_HEADS, BLK, 2 * BLK), lambda i: (0, 0, 0)),
                   pl.BlockSpec((N_HEADS, BLK, 1), lambda i: (0, 0, 0))],
        out_shape=[jax.ShapeDtypeStruct((S, ATTN_W), MXU_DTYPE), jax.ShapeDtypeStruct((S, 2 * KV_W), MXU_DTYPE),
                   jax.ShapeDtypeStruct((N_HEADS, BLK, 2 * BLK), F32), jax.ShapeDtypeStruct((N_HEADS, BLK, 1), F32)],
        scratch_shapes=[pltpu.VMEM((BLK, 2 * KV_W), F32)],
        args=(qkv, qkv, qkv, y, dy, bias, sinks), xchg=xchg, cparams=_cparams(),
    )


def _attn_finish(dbias, dsk, buckets):
    def body(db_ref, dsk_ref, bk_ref, drel_ref, dsink_ref):
        bk = bk_ref[...]
        r = lax.broadcasted_iota(jnp.int32, (N_BUCKETS, LANE), 0)
        l = lax.broadcasted_iota(jnp.int32, (N_BUCKETS, LANE), 1)
        row = lax.broadcasted_iota(jnp.int32, (N_HEADS, LANE), 0)
        res = jnp.zeros((N_BUCKETS, LANE), F32)
        dsink = jnp.zeros((N_HEADS, LANE), F32)
        for h in range(N_HEADS):
            db = db_ref[h]
            for b in range(N_BUCKETS):
                v = jnp.sum(jnp.sum(jnp.where(bk == b, db, 0.0), axis=1, keepdims=True), axis=0, keepdims=True)
                res = res + jnp.where((r == b) & (l == h), v, 0.0)
            dsink = dsink + jnp.where(row == h, jnp.sum(dsk_ref[h], axis=0, keepdims=True), 0.0)
        drel_ref[...] = res
        dsink_ref[...] = dsink

    return pl.pallas_call(body, name="attn_finish",
                          out_shape=[jax.ShapeDtypeStruct((N_BUCKETS, LANE), F32),
                                     jax.ShapeDtypeStruct((N_HEADS, LANE), F32)])(dbias, dsk, buckets)


def _ssd_consts():
    r = lax.broadcasted_iota(jnp.int32, (BLK, BLK), 0)
    c = lax.broadcasted_iota(jnp.int32, (BLK, BLK), 1)
    causal = c <= r
    upper = (r <= c).astype(F32)
    last = r == BLK - 1
    head = lax.broadcasted_iota(jnp.int32, (N_HEADS, BLK), 0)
    return causal, upper, last, head


def _ssd_chunks(xs, bg, cg, dt_raw_t, prev0, dtb, alog, d_rows, consts):
    causal, upper, last, head = consts
    nq = len(xs)
    items = [(c, h) for c in range(nq) for h in range(N_HEADS)]
    at = lambda c, h: c * N_HEADS + h
    a_neg = -jnp.exp(alog)
    dt_t = [_softplus(dt_raw_t[c] + dtb) for c in range(nq)]
    acs_t = [_mm_hi(dt_t[c] * a_neg, upper) for c in range(nq)]
    cb = [[_mm_nt(cg[c][g], bg[c][g]) for g in range(2)] for c in range(nq)]
    pick = lambda t, h: jnp.sum(jnp.where(head == h, t, 0.0), axis=0, keepdims=True)
    dt_row = [pick(dt_t[c], h) for c, h in items]
    a_row = [pick(acs_t[c], h) for c, h in items]
    a_rb = [jnp.broadcast_to(a_row[n], (BLK, BLK)) for n in range(len(items))]
    a_b = [a_rb[n].T for n in range(len(items))]
    a_last = [jnp.sum(jnp.where(last, a_b[n], 0.0), axis=0, keepdims=True) for n in range(len(items))]
    w = [cb[c][h // 4] * jnp.exp(jnp.where(causal, a_b[at(c, h)] - a_rb[at(c, h)], -1e30)) * dt_row[at(c, h)]
         for c, h in items]
    f_b = [jnp.broadcast_to(dt_row[n] * jnp.exp(a_last[n] - a_row[n]), (BLK, BLK)).T for n in range(len(items))]
    y_in = [_mm(w[at(c, h)], xs[c][h]) for c, h in items]
    st = [_mm_tn(bg[c][h // 4], xs[c][h] * f_b[at(c, h)]) for c, h in items]
    e_b = [jnp.exp(a_b[n]) for n in range(len(items))]
    states = [list(prev0)]
    for c in range(nq):
        states.append([states[c][h] * jnp.exp(a_last[at(c, h)]) + st[at(c, h)] for h in range(N_HEADS)])
    y_off = [_mm(cg[c][h // 4], states[c][h]) * e_b[at(c, h)] for c, h in items]
    ys = [[y_in[at(c, h)] + y_off[at(c, h)] + d_rows[h] * xs[c][h] for h in range(N_HEADS)] for c in range(nq)]
    return ys, states


def _ssd_chunks_bwd(xs, bg, cg, dt_raw_t, prev, dtb, alog, d_rows, dys, dh_last, consts):
    causal, upper, last, head = consts
    nq = len(xs)
    items = [(c, h) for c in range(nq) for h in range(N_HEADS)]
    ni = len(items)
    at = lambda c, h: c * N_HEADS + h
    groups = [(c, g) for c in range(nq) for g in range(2)]
    lane = _lane_iota((BLK, BLK))
    lane_row = _lane_iota((1, BLK))
    a_neg = -jnp.exp(alog)
    pre_dt = [dt_raw_t[c] + dtb for c in range(nq)]
    dt_t = [_softplus(pre_dt[c]) for c in range(nq)]
    acs_t = [_mm_hi(dt_t[c] * a_neg, upper) for c in range(nq)]
    pick = lambda t, h: jnp.sum(jnp.where(head == h, t, 0.0), axis=0, keepdims=True)
    full_sum = lambda t: jnp.sum(jnp.sum(t, axis=1, keepdims=True), axis=0, keepdims=True)
    dt_row = [pick(dt_t[c], h) for c, h in items]
    a_row = [pick(acs_t[c], h) for c, h in items]
    a_rb = [jnp.broadcast_to(a_row[n], (BLK, BLK)) for n in range(ni)]
    a_b = [a_rb[n].T for n in range(ni)]
    a_last = [jnp.sum(jnp.where(last, a_b[n], 0.0), axis=0, keepdims=True) for n in range(ni)]
    lm = [jnp.exp(jnp.where(causal, a_b[n] - a_rb[n], -1e30)) for n in range(ni)]
    cgb = [[cg[c][g].astype(MXU_DTYPE) for g in range(2)] for c in range(nq)]
    bgb = [[bg[c][g].astype(MXU_DTYPE) for g in range(2)] for c in range(nq)]
    cb = [[_mm_nt(cgb[c][g], bgb[c][g]) for g in range(2)] for c in range(nq)]
    u = [cb[c][h // 4] * lm[at(c, h)] for c, h in items]
    w = [(u[n] * dt_row[n]).astype(MXU_DTYPE) for n in range(ni)]
    e_row = [jnp.exp(a_last[n] - a_row[n]) for n in range(ni)]
    f_row = [dt_row[n] * e_row[n] for n in range(ni)]
    f_b = [jnp.broadcast_to(f_row[n], (BLK, BLK)).T for n in range(ni)]
    e_b = [jnp.exp(a_b[n]) for n in range(ni)]
    el = [jnp.exp(a_last[n]) for n in range(ni)]
    xb = [xs[c][h].astype(MXU_DTYPE) for c, h in items]
    dyb = [dys[c][h].astype(MXU_DTYPE) for c, h in items]
    prevb = [prev[c][h].astype(MXU_DTYPE) for c, h in items]
    gmat = [_mm(cgb[c][h // 4], prevb[at(c, h)]) for c, h in items]
    dw = [_mm_nt(dyb[n], xb[n]) for n in range(ni)]
    dg = [dys[c][h] * e_b[at(c, h)] for c, h in items]
    dgb = [dg[n].astype(MXU_DTYPE) for n in range(ni)]
    from_y = [_mm_tn(cgb[c][h // 4], dgb[at(c, h)]) for c, h in items]
    dhs = [None] * ni
    dprev = [None] * ni
    for c in reversed(range(nq)):
        for h in range(N_HEADS):
            dhs[at(c, h)] = dh_last[h] if c == nq - 1 else dprev[at(c + 1, h)]
            dprev[at(c, h)] = from_y[at(c, h)] + dhs[at(c, h)] * el[at(c, h)]
    dstb = [dhs[n].astype(MXU_DTYPE) for n in range(ni)]
    dxf = [_mm(bgb[c][h // 4], dstb[at(c, h)]) for c, h in items]
    xfb = [(xs[c][h] * f_b[at(c, h)]).astype(MXU_DTYPE) for c, h in items]
    dxs = [_mm_tn(w[at(c, h)], dyb[at(c, h)]) + d_rows[h] * dys[c][h] + f_b[at(c, h)] * dxf[at(c, h)]
           for c, h in items]
    dd_item = [jnp.sum(dys[c][h] * xs[c][h], axis=0, keepdims=True) for c, h in items]
    dcg_h = [_mm_nt(dgb[n], prevb[n]) for n in range(ni)]
    dbg_h = [_mm_nt(xfb[n], dstb[n]) for n in range(ni)]
    zt = [dw[n] * u[n] for n in range(ni)]
    dseg = [zt[n] * dt_row[n] for n in range(ni)]
    dcb_h = [dw[n] * lm[n] * dt_row[n] for n in range(ni)]
    four = lambda lst, c, g: lst[at(c, 4 * g)] + lst[at(c, 4 * g + 1)] + lst[at(c, 4 * g + 2)] + lst[at(c, 4 * g + 3)]
    dcb = {(c, g): four(dcb_h, c, g).astype(MXU_DTYPE) for c, g in groups}
    dcg = [[four(dcg_h, c, g) + _mm(dcb[c, g], bgb[c][g]) for g in range(2)] for c in range(nq)]
    dbg = [[four(dbg_h, c, g) + _mm_tn(dcb[c, g], cgb[c][g]) for g in range(2)] for c in range(nq)]
    r1 = [jnp.sum(dg[n] * gmat[n] + dseg[n], axis=1, keepdims=True) for n in range(ni)]
    r2 = [jnp.sum(dxf[at(c, h)] * xs[c][h], axis=1, keepdims=True) for c, h in items]
    tt = [jnp.where(lane < HALF, jnp.broadcast_to(r1[n], (BLK, BLK)), jnp.broadcast_to(r2[n], (BLK, BLK))).T
          for n in range(ni)]
    r1_row = [tt[n][0:1, :] for n in range(ni)]
    r2_row = [tt[n][HALF:HALF + 1, :] for n in range(ni)]
    d_el = [full_sum(dhs[at(c, h)] * prev[c][h]) for c, h in items]
    da_last = [jnp.sum(r2_row[n] * f_row[n], axis=1, keepdims=True) + el[n] * d_el[n] for n in range(ni)]
    da_row = [r1_row[n] - jnp.sum(dseg[n], axis=0, keepdims=True) - r2_row[n] * f_row[n]
              + jnp.where(lane_row == BLK - 1, da_last[n], 0.0) for n in range(ni)]
    ddt_row = [jnp.sum(zt[n], axis=0, keepdims=True) + r2_row[n] * e_row[n] for n in range(ni)]
    draw, dalog = [], jnp.zeros((N_HEADS, BLK), F32)
    for c in range(nq):
        da_t = jnp.zeros((N_HEADS, BLK), F32)
        ddt_t = jnp.zeros((N_HEADS, BLK), F32)
        for h in range(N_HEADS):
            da_t = jnp.where(head == h, da_row[at(c, h)], da_t)
            ddt_t = jnp.where(head == h, ddt_row[at(c, h)], ddt_t)
        d_dta = _mm_hi(da_t, causal.astype(F32))
        dalog = dalog + d_dta * dt_t[c] * a_neg
        draw.append((ddt_t + d_dta * a_neg) * jax.nn.sigmoid(pre_dt[c]))
    ddtb = draw[0]
    for c in range(1, nq):
        ddtb = ddtb + draw[c]
    dd_rows = []
    for h in range(N_HEADS):
        t = dd_item[at(0, h)]
        for c in range(1, nq):
            t = t + dd_item[at(c, h)]
        dd_rows.append(t)
    return ([dxs[c * N_HEADS:(c + 1) * N_HEADS] for c in range(nq)], dbg, dcg, draw,
            [dprev[at(0, h)] for h in range(N_HEADS)], ddtb, dalog, dd_rows)


def _dt_rows(dt_blk):
    return dt_blk.T[:N_HEADS]


def _conv_pre(halo, blk, cw_ref, cb_ref):
    ext = jnp.concatenate([halo, blk], axis=0)
    taps = [pltpu.roll(ext, 3 - k, 0)[8:] for k in range(3)] + [blk]
    pre = cb_ref[...] + cw_ref[0:1, :] * taps[0]
    for k in range(1, 4):
        pre = pre + cw_ref[k:k + 1, :] * taps[k]
    return pre


def _ssd_split(pre):
    heads = _split_heads(pre[:, :SSM_W], 4)
    pb = [pre[:, SSM_W + g * D_STATE:SSM_W + (g + 1) * D_STATE] for g in range(2)]
    pc = [pre[:, SSM_W + 2 * D_STATE + g * D_STATE:SSM_W + 2 * D_STATE + (g + 1) * D_STATE] for g in range(2)]
    return heads, pb, pc


def _ssd_fwd(xbc, dt_raw, conv_w, conv_b, dtb_row, alog_row, d_exp, xchg):
    S = xbc.shape[0]
    nc = S // BLK
    nq = SSD_CHUNKS_PER_STEP if nc % SSD_CHUNKS_PER_STEP == 0 else 1
    rows = nq * BLK

    def body(xbc_ref, halo_ref, dt_ref, cw_ref, cb_ref, dtb_ref, alog_ref, d_ref, y_ref, prev_ref, pre_ref, state_ref):
        i = pl.program_id(0)

        @pl.when(i == 0)
        def _():
            state_ref[...] = jnp.zeros_like(state_ref)

        halo = halo_ref[...] * jnp.where(i > 0, 1.0, 0.0)
        pre = _conv_pre(halo, xbc_ref[...], cw_ref, cb_ref)
        pre_ref[...] = pre
        xc = _silu(pre)
        split = [_ssd_split(xc[c * BLK:(c + 1) * BLK]) for c in range(nq)]
        dt_t = [_dt_rows(dt_ref[c * BLK:(c + 1) * BLK, :]) for c in range(nq)]
        prev0 = [state_ref[h] for h in range(N_HEADS)]
        d_rows = [d_ref[h:h + 1, :] for h in range(N_HEADS)]
        ys, states = _ssd_chunks([s[0] for s in split], [s[1] for s in split], [s[2] for s in split], dt_t, prev0,
                                 dtb_ref[...], alog_ref[...], d_rows, _ssd_consts())
        for h in range(N_HEADS):
            for c in range(nq):
                prev_ref[c, h] = states[c][h]
            state_ref[h] = states[nq][h]
        y_ref[...] = jnp.concatenate([_join_heads(ys[c]) for c in range(nq)], axis=0)

    vec = pl.BlockSpec((N_HEADS, LANE), _fixed)
    return _hosted_call(
        body, "ssd_fwd", nc // nq,
        in_specs=[pl.BlockSpec((rows, XBC_W), _row),
                  pl.BlockSpec((8, XBC_W), lambda i: (jnp.maximum(i * (rows // 8) - 1, 0), 0)),
                  pl.BlockSpec((rows, LANE), _row),
                  pl.BlockSpec((4, XBC_W), _fixed), pl.BlockSpec((1, XBC_W), _fixed), vec, vec,
                  pl.BlockSpec((N_HEADS, LANE), _fixed)],
        out_specs=[pl.BlockSpec((rows, SSM_W), _row),
                   pl.BlockSpec((nq, N_HEADS, D_STATE, LANE), lambda i: (i, 0, 0, 0)),
                   pl.BlockSpec((rows, XBC_W), _row)],
        out_shape=[jax.ShapeDtypeStruct((S, SSM_W), F32), jax.ShapeDtypeStruct((nc, N_HEADS, D_STATE, LANE), F32),
                   jax.ShapeDtypeStruct((S, XBC_W), F32)],
        scratch_shapes=[pltpu.VMEM((N_HEADS, D_STATE, LANE), F32)],
        args=(xbc, xbc, dt_raw, conv_w, conv_b, dtb_row, alog_row, d_exp), xchg=xchg, cparams=_cparams(),
    )


def _ssd_bwd(xbc, pre_act, dt_raw, prev_states, dy, conv_w, dtb_row, alog_row, d_exp, xchg):
    S = xbc.shape[0]
    nc = S // BLK
    nq = SSD_BWD_CHUNKS_PER_STEP if nc % SSD_BWD_CHUNKS_PER_STEP == 0 else 1
    rows, n_steps = nq * BLK, nc // nq

    def body(xbc_ref, halo_ref, pre_ref, dt_ref, prev_ref, dy_ref, cw_ref, dtb_ref, alog_ref, d_ref,
             dxbc_ref, ddt_ref, dcw_ref, dvec_ref, dd_ref, gstate_ref, ghalo_ref):
        i = pl.program_id(0)

        @pl.when(i == 0)
        def _():
            gstate_ref[...] = jnp.zeros_like(gstate_ref)
            ghalo_ref[...] = jnp.zeros_like(ghalo_ref)
            dcw_ref[...] = jnp.zeros_like(dcw_ref)
            dvec_ref[...] = jnp.zeros_like(dvec_ref)
            dd_ref[...] = jnp.zeros_like(dd_ref)

        halo = halo_ref[...] * jnp.where(i < n_steps - 1, 1.0, 0.0)
        ext = jnp.concatenate([halo, xbc_ref[...]], axis=0)
        pre = pre_ref[...]
        sig = jax.nn.sigmoid(pre)
        xc = pre * sig
        split = [_ssd_split(xc[c * BLK:(c + 1) * BLK]) for c in range(nq)]
        dt_t = [_dt_rows(dt_ref[c * BLK:(c + 1) * BLK, :]) for c in range(nq)]
        prev = [[prev_ref[c, h] for h in range(N_HEADS)] for c in range(nq)]
        d_rows = [d_ref[h:h + 1, :] for h in range(N_HEADS)]
        dys = [_split_heads(dy_ref[c * BLK:(c + 1) * BLK, :], 4) for c in range(nq)]
        dh_last = [gstate_ref[h] for h in range(N_HEADS)]
        dheads, dpb, dpc, ddt_t, dprev0, ddtb, dalog, dd_rows = _ssd_chunks_bwd(
            [s[0] for s in split], [s[1] for s in split], [s[2] for s in split], dt_t, prev, dtb_ref[...],
            alog_ref[...], d_rows, dys, dh_last, _ssd_consts())
        for h in range(N_HEADS):
            gstate_ref[h] = dprev0[h]
            dd_ref[h:h + 1, :] += dd_rows[h]
        pad = jnp.zeros((BLK - N_HEADS, BLK), F32)
        ddt_ref[...] = jnp.concatenate([jnp.concatenate([ddt_t[c], pad], axis=0).T for c in range(nq)],
                                       axis=0).astype(ddt_ref.dtype)
        dvec_ref[0:N_HEADS, :] += ddtb
        dvec_ref[N_HEADS:, :] += dalog
        dxc = jnp.concatenate([jnp.concatenate([_join_heads(dheads[c])] + list(dpb[c]) + list(dpc[c]), axis=1)
                               for c in range(nq)], axis=0)
        dpre = dxc * (sig * (1.0 + pre * (1.0 - sig)))
        zeros8 = jnp.zeros((8, XBC_W), F32)
        dpe = jnp.concatenate([zeros8, dpre, zeros8], axis=0)
        n_ext = 16 + rows
        shifted = [pltpu.roll(dpe, n_ext - (3 - k), 0)[:8 + rows] for k in range(3)] + [dpe[:8 + rows]]
        dext = cw_ref[0:1, :] * shifted[0]
        for k in range(1, 4):
            dext = dext + cw_ref[k:k + 1, :] * shifted[k]
        for k in range(4):
            dcw_ref[k:k + 1, :] += jnp.sum(shifted[k] * ext, axis=0, keepdims=True)
        dcw_ref[4:5, :] += jnp.sum(dpre, axis=0, keepdims=True)
        dxbc_ref[...] = jnp.concatenate([dext[8:rows], dext[rows:] + ghalo_ref[...]], axis=0).astype(dxbc_ref.dtype)
        ghalo_ref[...] = dext[:8, :]

    vec = pl.BlockSpec((N_HEADS, LANE), _fixed)
    rev = lambda i: (n_steps - 1 - i, 0)
    return _hosted_call(
        body, "ssd_bwd", n_steps,
        in_specs=[pl.BlockSpec((rows, XBC_W), rev),
                  pl.BlockSpec((8, XBC_W), lambda i: (jnp.maximum((n_steps - 1 - i) * (rows // 8) - 1, 0), 0)),
                  pl.BlockSpec((rows, XBC_W), rev),
                  pl.BlockSpec((rows, LANE), rev),
                  pl.BlockSpec((nq, N_HEADS, D_STATE, LANE), lambda i: (n_steps - 1 - i, 0, 0, 0)),
                  pl.BlockSpec((rows, SSM_W), rev),
                  pl.BlockSpec((4, XBC_W), _fixed), vec, vec,
                  pl.BlockSpec((N_HEADS, LANE), _fixed)],
        out_specs=[pl.BlockSpec((rows, XBC_W), rev), pl.BlockSpec((rows, LANE), rev),
                   pl.BlockSpec((8, XBC_W), _fixed), pl.BlockSpec((2 * N_HEADS, LANE), _fixed),
                   pl.BlockSpec((N_HEADS, LANE), _fixed)],
        out_shape=[jax.ShapeDtypeStruct((S, XBC_W), MXU_DTYPE), jax.ShapeDtypeStruct((S, LANE), MXU_DTYPE),
                   jax.ShapeDtypeStruct((8, XBC_W), F32), jax.ShapeDtypeStruct((2 * N_HEADS, LANE), F32),
                   jax.ShapeDtypeStruct((N_HEADS, LANE), F32)],
        scratch_shapes=[pltpu.VMEM((N_HEADS, D_STATE, LANE), F32), pltpu.VMEM((8, XBC_W), F32)],
        args=(xbc, xbc, pre_act, dt_raw, prev_states, dy, conv_w, dtb_row, alog_row, d_exp), xchg=xchg,
        cparams=_cparams(VMEM_BIG),
    )


def _adamw_math(w, g, m, v):
    m = ADAM_B1 * m + (1.0 - ADAM_B1) * g
    v = ADAM_B2 * v + (1.0 - ADAM_B2) * jnp.square(g)
    m_hat = m / (1.0 - ADAM_B1 ** ADAM_STEP)
    v_hat = v / (1.0 - ADAM_B2 ** ADAM_STEP)
    delta = -ADAM_LR * (m_hat / (jnp.sqrt(v_hat) + ADAM_EPS) + ADAM_WD * w)
    return delta, m, v


def _reduce_adamw(parts, w, m, v, name):
    R, C = w.shape

    def body(p_ref, w_ref, m_ref, v_ref, g_ref, d_ref, nm_ref, nv_ref):
        g = p_ref[0].astype(F32)
        for i in range(1, N_DEV):
            g = g + p_ref[i].astype(F32)
        d, nm, nv = _adamw_math(w_ref[...], g, m_ref[...], v_ref[...])
        g_ref[...] = g
        d_ref[...] = d
        nm_ref[...] = nm
        nv_ref[...] = nv

    if R % 16 == 0:
        tr = max(t for t in range(16, 257, 16) if R % t == 0)
        n, blk, pblk = R // tr, pl.BlockSpec((tr, C), _row), pl.BlockSpec((N_DEV, tr, C), lambda i: (0, i, 0))
    else:
        tl = 256
        n, blk, pblk = C // tl, pl.BlockSpec((R, tl), lambda i: (0, i)), pl.BlockSpec((N_DEV, R, tl),
                                                                                      lambda i: (0, 0, i))
    return pl.pallas_call(
        body, name=name, grid=(n,), in_specs=[pblk, blk, blk, blk],
        out_specs=[blk] * 4, out_shape=[jax.ShapeDtypeStruct((R, C), F32)] * 4,
    )(parts, w, m, v)


def _reduce_adamw_hosting(parts_list, wmv_list, name, xchg):
    n_arr = len(parts_list)
    pieces = [list(p) if isinstance(p, (tuple, list)) else [p] for p in parts_list]
    n_pieces = sum(len(p) for p in pieces)
    C = wmv_list[0][0].shape[1]
    tl = 256

    def total(ref):
        g = ref[0].astype(F32)
        for i in range(1, N_DEV):
            g = g + ref[i].astype(F32)
        return g

    def body(*refs):
        p_refs, wmv_refs, o_refs = refs[:n_pieces], refs[n_pieces:n_pieces + 3 * n_arr], refs[n_pieces + 3 * n_arr:]
        at = 0
        for k in range(n_arr):
            sums = [total(r) for r in p_refs[at:at + len(pieces[k])]]
            at += len(pieces[k])
            g = sums[0] if len(sums) == 1 else jnp.concatenate(sums, axis=0)
            w_ref, m_ref, v_ref = wmv_refs[3 * k:3 * k + 3]
            d, nm, nv = _adamw_math(w_ref[...], g, m_ref[...], v_ref[...])
            for o, val in zip(o_refs[4 * k:4 * k + 4], (g, d, nm, nv)):
                o[...] = val

    in_specs = [pl.BlockSpec((N_DEV, p.shape[1], tl), lambda i: (0, 0, i)) for group in pieces for p in group]
    in_specs += [pl.BlockSpec((w.shape[0], tl), lambda i: (0, i)) for w, _, _ in wmv_list for _ in range(3)]
    out_specs = [pl.BlockSpec((w.shape[0], tl), lambda i: (0, i)) for w, _, _ in wmv_list for _ in range(4)]
    out_shape = [jax.ShapeDtypeStruct(w.shape, F32) for w, _, _ in wmv_list for _ in range(4)]
    args = [p for group in pieces for p in group] + [a for wmv in wmv_list for a in wmv]
    outs, x_out = _hosted_call(body, name, C // tl, in_specs, out_specs, out_shape, [], args, xchg,
                               _cparams(VMEM_BIG))
    return [outs[4 * k:4 * k + 4] for k in range(n_arr)], x_out


_SMALL_NAMES = ("ada_b", "norm1", "conv_w", "conv_b", "dt_bias", "A_log", "D_skip", "sinks", "attn_out_norm",
                "ssm_out_norm", "norm2", "rel_bias", "final_norm")
N_MOD = 6 * D_MODEL


def _mod_row(a0, a1, a2):
    return jnp.concatenate([a0[2:3], a0[1:2], a1[0:1], a2[2:3], a2[1:2], a2[3:4]], axis=1)


def _small_update(gathered, params):
    n_g = len(gathered)
    flat = [a for name in _SMALL_NAMES for a in params[name]]

    def body(*refs):
        a0_ref, a1_ref, a2_ref, cw_ref, dv_ref, dd_ref, ds_ref, dr_ref, c_ref = refs[:n_g]
        wmv = refs[n_g:n_g + len(flat)]
        outs = refs[n_g + len(flat):]

        def total(ref):
            t = ref[0]
            for i in range(1, N_DEV):
                t = t + ref[i]
            return t

        t0, t1, t2, tcw, tdv, tdd, tds, tdr = [total(r) for r in (a0_ref, a1_ref, a2_ref, cw_ref, dv_ref, dd_ref,
                                                                   ds_ref, dr_ref)]
        r8 = lax.broadcasted_iota(jnp.int32, (N_HEADS, LANE), 0)
        l8 = lax.broadcasted_iota(jnp.int32, (N_HEADS, LANE), 1)

        def diag_row(t):
            return jnp.sum(jnp.where(r8 == l8, t, 0.0), axis=0, keepdims=True)[:, :N_HEADS]

        def lane_sums(t):
            return diag_row(jnp.broadcast_to(jnp.sum(t, axis=1, keepdims=True), (N_HEADS, LANE)))

        me = _lin(_my_pos())
        n_cw = XBC_W // N_DEV
        cw_mine = jnp.zeros((4, n_cw), F32)
        for j in range(N_DEV):
            cw_mine = cw_mine + tcw[0:4, j * n_cw:(j + 1) * n_cw] * jnp.where(me == j, 1.0, 0.0)
        grads = {
            "ada_b": _mod_row(t0, t1, t2), "norm1": t0[0:1], "conv_w": cw_mine, "conv_b": tcw[4:5],
            "dt_bias": lane_sums(tdv[:N_HEADS]), "A_log": lane_sums(tdv[N_HEADS:]), "D_skip": lane_sums(tdd),
            "sinks": diag_row(tds), "attn_out_norm": t1[1:2, :ATTN_W], "ssm_out_norm": t1[1:2, ATTN_W:],
            "norm2": t2[0:1], "rel_bias": tdr[:, :N_HEADS], "final_norm": t2[4:5],
        }
        for k, name in enumerate(_SMALL_NAMES):
            w_ref, m_ref, v_ref = wmv[3 * k:3 * k + 3]
            g = grads[name]
            d, nm, nv = _adamw_math(w_ref[...], g, m_ref[...], v_ref[...])
            for o, val in zip(outs[4 * k:4 * k + 4], (g, d, nm, nv)):
                o[...] = val
        loss_ref, call_ref, dmod_ref = outs[4 * len(_SMALL_NAMES):]
        loss_ref[...] = t2[5:6, 0:1]
        call_ref[...] = jnp.concatenate([c_ref[i] for i in range(N_DEV)], axis=0)
        dmod_ref[...] = jnp.concatenate([_mod_row(a0_ref[i], a1_ref[i], a2_ref[i]) for i in range(N_DEV)], axis=0)

    out_shape = [jax.ShapeDtypeStruct(params[name][0].shape, F32) for name in _SMALL_NAMES for _ in range(4)]
    out_shape += [jax.ShapeDtypeStruct((1, 1), F32), jax.ShapeDtypeStruct((N_DEV, D_MODEL), F32),
                  jax.ShapeDtypeStruct((N_DEV, N_MOD), F32)]
    res = pl.pallas_call(body, name="small_update", out_shape=out_shape)(*gathered, *flat)
    upd = {name: res[4 * k:4 * k + 4] for k, name in enumerate(_SMALL_NAMES)}
    loss, c_all, dmod_all = res[4 * len(_SMALL_NAMES):]
    return upd, loss, c_all, dmod_all


def _ada_w_update(c_all, dmod_all, w, m, v):
    chunk = w.shape[1]

    def body(c_ref, dm_ref, w_ref, m_ref, v_ref, g_ref, d_ref, nm_ref, nv_ref):
        me = _lin(_my_pos())
        dm = jnp.zeros((N_DEV, chunk), F32)
        for j in range(N_DEV):
            dm = dm + dm_ref[:, j * chunk:(j + 1) * chunk] * jnp.where(me == j, 1.0, 0.0)
        g = lax.dot_general(_silu(c_ref[...]), dm, (((0,), (0,)), ((), ())), precision=HI,
                            preferred_element_type=F32)
        d, nm, nv = _adamw_math(w_ref[...], g, m_ref[...], v_ref[...])
        g_ref[...] = g
        d_ref[...] = d
        nm_ref[...] = nm
        nv_ref[...] = nv

    tr = 256
    blk = pl.BlockSpec((tr, chunk), _row)
    return pl.pallas_call(
        body, name="ada_w_update", grid=(w.shape[0] // tr,),
        in_specs=[pl.BlockSpec((N_DEV, tr), lambda i: (0, i)), pl.BlockSpec(dmod_all.shape, _fixed), blk, blk, blk],
        out_specs=[blk] * 4, out_shape=[jax.ShapeDtypeStruct(w.shape, F32)] * 4,
    )(c_all, dmod_all, w, m, v)


def _local_step(x, tgt, c, mod, w_in, conv_w, w_o_mine, w_gu_mine, w_d_mine, p):
    S = x.shape[0]
    tm = min(512, S)
    tmm = min(256, S)
    tw = min(2048, S)
    shift1, scale1, gate1, shift2, scale2, gate2 = [mod[i:i + 1] for i in range(6)]
    buckets = jnp.asarray(_t5_bucket_table())
    per_head = lambda a: jnp.broadcast_to(a.reshape(N_HEADS, 1), (N_HEADS, LANE))
    dtb_row, alog_row, d_exp = per_head(p["dt_bias"]), per_head(p["A_log"]), per_head(p["D_skip"])
    sinks = p["sinks"].reshape(N_HEADS)

    d_cut, gu_cut = WD_CUT, WGU_CUTS
    n_d, n_gu = w_d_mine.shape[0], w_gu_mine.shape[0]
    (qkv, z, xbc, dt_raw), (g_d_a,) = _in_proj_fwd(x, p["norm1"], scale1, shift1, w_in, tm,
                                                   ([(w_d_mine, 0, d_cut)], "two-level"))
    bias = _attn_bias(buckets, p["rel_bias"])
    (ya,), (g_gu_a,) = _attn_fwd(qkv, bias, sinks, ([(w_gu_mine, 0, gu_cut[0])], "two-level"))
    (ys, prev_states, pre_act), (g_gu_b, g_o) = _ssd_fwd(
        xbc, dt_raw, conv_w, p["conv_b"], dtb_row, alog_row, d_exp,
        ([(w_gu_mine, gu_cut[0], gu_cut[1] - gu_cut[0]), w_o_mine], "two-level"))
    w_o = g_o.reshape(D_MODEL, D_MODEL)
    x1, (g_gu_c, g_d_b) = _out_proj_fwd(
        x, ya, ys, z, p["attn_out_norm"], p["ssm_out_norm"], gate1, w_o, tm,
        ([(w_gu_mine, gu_cut[1], n_gu - gu_cut[1]), (w_d_mine, d_cut, n_d - d_cut)], "two-level"))
    dx1, h2, dgu, act, dmlp, acc2 = _mlp_loss(x1, tgt, p["norm2"], scale2, shift2, gate2, p["final_norm"],
                                              (g_gu_a, g_gu_b, g_gu_c), (g_d_a, g_d_b), tmm)
    g_w_gu = _wgrad(dgu, h2, 2 * D_FF // 4, tw, "wgrad_gate_up")
    g_w_d = _wgrad(act, dmlp, D_FF // 2, tw, "wgrad_down")
    gu_slots = g_w_gu.reshape(N_DEV, 2 * D_FF // N_DEV, D_MODEL)
    (dya, dys, dz, g_w_o, acc1), (r_gu_a,) = _out_proj_bwd(
        dx1, ya, ys, z, p["attn_out_norm"], p["ssm_out_norm"], gate1, w_o, tm, ([gu_slots], ("rows", 0, GGU_CUT)))
    (dq, dkv, dbias, dsk), (r_d, r_o) = _attn_bwd(
        qkv, ya, dya, bias, sinks,
        ([g_w_d.reshape(N_DEV, D_FF // N_DEV, D_MODEL), g_w_o.reshape(N_DEV, D_MODEL // N_DEV, D_MODEL)], True))
    drel, dsink = _attn_finish(dbias, dsk, buckets)
    (dxbc, ddt, dcw, dvec, dd), (r_gu_b,) = _ssd_bwd(
        xbc, pre_act, dt_raw, prev_states, dys, conv_w, dtb_row, alog_row, d_exp,
        ([gu_slots], ("rows", GGU_CUT, 2 * D_FF // N_DEV - GGU_CUT)))
    r_gu = (r_gu_a, r_gu_b)
    gx, h1, acc0, g_in_a = _in_proj_bwd(x, dx1, dq, dkv, dz, dxbc, ddt, p["norm1"], scale1, shift1, w_in, tm)
    half = D_MODEL // 2
    slots = lambda g: g[:IN_W].reshape(N_DEV, IN_W // N_DEV, half)
    g_in_b, exchanged = _wgrad((dq, dkv, dz, dxbc, ddt), h1, IN_PAD, tw, "wgrad_in_b",
                               [([slots(g_in_a)], True), ([acc0, acc1, acc2, dcw, dvec, dd, dsink, drel, c], False)],
                               g_cols=(half, 1))
    return gx, (exchanged[0], slots(g_in_b)), (r_o, r_gu, r_d), exchanged[1:]


def kernel(x, c, ada_w, ada_b, norm1, w_in, conv_w, conv_b, dt_bias, A_log, D_skip, sinks, attn_out_norm, ssm_out_norm, w_o, norm2, w_gate_up, w_down, rel_bias, final_norm, loss_target, m_ada_w, m_ada_b, m_norm1, m_w_in, m_conv_w, m_conv_b, m_dt_bias, m_A_log, m_D_skip, m_sinks, m_attn_out_norm, m_ssm_out_norm, m_w_o, m_norm2, m_w_gate_up, m_w_down, m_rel_bias, m_final_norm, v_ada_w, v_ada_b, v_norm1, v_w_in, v_conv_w, v_conv_b, v_dt_bias, v_A_log, v_D_skip, v_sinks, v_attn_out_norm, v_ssm_out_norm, v_w_o, v_norm2, v_w_gate_up, v_w_down, v_rel_bias, v_final_norm):
    two_d = lambda a: a if a.ndim == 2 else a.reshape(-1, a.shape[-1])
    small_params = dict(
        ada_b=(ada_b, m_ada_b, v_ada_b), norm1=(norm1, m_norm1, v_norm1), conv_w=(conv_w, m_conv_w, v_conv_w),
        conv_b=(conv_b, m_conv_b, v_conv_b), dt_bias=(dt_bias, m_dt_bias, v_dt_bias), A_log=(A_log, m_A_log, v_A_log),
        D_skip=(D_skip, m_D_skip, v_D_skip), sinks=(sinks, m_sinks, v_sinks),
        attn_out_norm=(attn_out_norm, m_attn_out_norm, v_attn_out_norm),
        ssm_out_norm=(ssm_out_norm, m_ssm_out_norm, v_ssm_out_norm), norm2=(norm2, m_norm2, v_norm2),
        rel_bias=(rel_bias, m_rel_bias, v_rel_bias), final_norm=(final_norm, m_final_norm, v_final_norm))
    small_params = {k: tuple(two_d(a) for a in v) for k, v in small_params.items()}
    S = x.shape[1]
    xs, tgt = x.reshape(S, D_MODEL), loss_target.reshape(S, D_MODEL)
    ada_w2 = ada_w[0]
    chunk = ada_w2.shape[1]
    t_in = [jnp.transpose(a[0]) for a in (w_in, m_w_in, v_w_in)]
    t_gu = [jnp.transpose(a[0]) for a in (w_gate_up, m_w_gate_up, v_w_gate_up)]

    mod, (g_in, g_cw) = _mod_and_gather(c, ada_w2, ada_b.reshape(N_DEV, chunk), [t_in[0].astype(WIRE_DTYPE), conv_w[0]])
    mod = mod.reshape(6, D_MODEL)
    w_in_full = jnp.pad(g_in.reshape(IN_W, D_MODEL), ((0, IN_PAD - IN_W), (0, 0)))
    conv_w_full = jnp.transpose(g_cw, (1, 0, 2)).reshape(4, XBC_W)

    p = {k: v[0] for k, v in small_params.items()}
    gx, (r_in_a, gw_in_b), (r_o, r_gu, r_d), gathered = _local_step(
        xs, tgt, c, mod, w_in_full, conv_w_full, w_o[0].astype(WIRE_DTYPE), t_gu[0].astype(WIRE_DTYPE),
        w_down[0].astype(WIRE_DTYPE), p)

    (u_gu, u_d, u_o), (r_in_b,) = _reduce_adamw_hosting(
        [r_gu, r_d, r_o], [tuple(t_gu), (w_down[0], m_w_down[0], v_w_down[0]), (w_o[0], m_w_o[0], v_w_o[0])],
        "adamw_big", ([gw_in_b], True))
    r_in = jnp.concatenate([r_in_a, r_in_b], axis=2)

    small, loss, c_all, dmod_all = _small_update(gathered, small_params)

    big = {
        "ada_w": _ada_w_update(c_all, dmod_all, ada_w2, m_ada_w[0], v_ada_w[0]),
        "w_in": [jnp.transpose(a) for a in _reduce_adamw(r_in, *t_in, "adamw_w_in")],
        "w_o": u_o,
        "w_gate_up": [jnp.transpose(a) for a in u_gu],
        "w_down": u_d,
    }
    big.update(small)

    order = ['ada_w', 'ada_b', 'norm1', 'w_in', 'conv_w', 'conv_b', 'dt_bias', 'A_log', 'D_skip', 'sinks',
             'attn_out_norm', 'ssm_out_norm', 'w_o', 'norm2', 'w_gate_up', 'w_down', 'rel_bias', 'final_norm']
    shapes = dict(ada_w=ada_w.shape, ada_b=ada_b.shape, norm1=norm1.shape, w_in=w_in.shape, conv_w=conv_w.shape,
                  conv_b=conv_b.shape, dt_bias=dt_bias.shape, A_log=A_log.shape, D_skip=D_skip.shape,
                  sinks=sinks.shape, attn_out_norm=attn_out_norm.shape, ssm_out_norm=ssm_out_norm.shape,
                  w_o=w_o.shape, norm2=norm2.shape, w_gate_up=w_gate_up.shape, w_down=w_down.shape,
                  rel_bias=rel_bias.shape, final_norm=final_norm.shape)
    outs = [[], [], [], []]
    for name in order:
        for kind in range(4):
            outs[kind].append(big[name][kind].reshape(shapes[name]))
    return (loss.reshape(()), gx.reshape(x.shape), *outs[0], *outs[1], *outs[2], *outs[3])
```

```python
import numpy as np
import jax
import jax.numpy as jnp
from jax import lax
from jax.experimental import pallas as pl
from jax.experimental.pallas import tpu as pltpu

F32 = jnp.float32
MXU_DTYPE = jnp.bfloat16
WIRE_DTYPE = jnp.bfloat16
HI = lax.Precision.HIGHEST
MESH = pl.DeviceIdType.MESH
N_DEV = 8

D_MODEL = 1024
ATTN_W = 512
KV_W = 128
SSM_W = 512
XBC_W = 1024
N_HEADS = 8
D_STATE = 128
D_FF = 2816
IN_W = 2312
IN_PAD = 2432
BLK = 128
N_BUCKETS = 32
EPS = 1e-6
LANE = 128
HALF = 64

ADAM_LR, ADAM_B1, ADAM_B2, ADAM_EPS, ADAM_WD, ADAM_STEP = 0.001, 0.9, 0.999, 1e-08, 0.01, 10

VMEM_BIG = 56 * 1024 * 1024
WD_CUT = 288
WGU_CUTS = (240, 496)
GGU_CUT = 304


def _cparams(vmem=None):
    if vmem is None:
        return pltpu.CompilerParams()
    return pltpu.CompilerParams(vmem_limit_bytes=vmem)


def _mm(a, b):
    return jnp.dot(a.astype(MXU_DTYPE), b.astype(MXU_DTYPE), preferred_element_type=F32)


def _mm_nt(a, b):
    return lax.dot_general(a.astype(MXU_DTYPE), b.astype(MXU_DTYPE), (((1,), (1,)), ((), ())),
                           preferred_element_type=F32)


def _mm_tn(a, b):
    return lax.dot_general(a.astype(MXU_DTYPE), b.astype(MXU_DTYPE), (((0,), (0,)), ((), ())),
                           preferred_element_type=F32)


def _mm_hi(a, b):
    return jnp.dot(a, b, precision=HI, preferred_element_type=F32)


def _silu(x):
    return x * jax.nn.sigmoid(x)


def _softplus(x):
    return jnp.maximum(x, 0.0) + jnp.log1p(jnp.exp(-jnp.abs(x)))


def _rms(x, g, n):
    return x * lax.rsqrt(jnp.sum(x * x, axis=-1, keepdims=True) * (1.0 / n) + EPS) * g


def _modnorm(x, g, scale, shift):
    return _rms(x, g, x.shape[-1]) * (1.0 + scale) + shift


def _modnorm_parts(x):
    r = lax.rsqrt(jnp.sum(x * x, axis=-1, keepdims=True) * (1.0 / x.shape[-1]) + EPS)
    return r, x * r


def _modnorm_bwd(r, xhat, g, scale, dy):
    dyg = dy * (g * (1.0 + scale))
    c = jnp.sum(dyg * xhat, axis=-1, keepdims=True) * (1.0 / xhat.shape[-1])
    dx = r * (dyg - xhat * c)
    ct = jnp.sum(dy * xhat, axis=0, keepdims=True)
    return dx, ct * (1.0 + scale), ct * g, jnp.sum(dy, axis=0, keepdims=True)


def _lane_iota(shape):
    return lax.broadcasted_iota(jnp.int32, shape, len(shape) - 1)


def _split_pair(t):
    lane = _lane_iota(t.shape)
    lo = jnp.where(lane < HALF, t, 0.0)
    hi = pltpu.roll(jnp.where(lane >= HALF, t, 0.0), HALF, 1)
    return lo, hi


def _join_pair(lo, hi):
    lane = _lane_iota(lo.shape)
    return jnp.where(lane < HALF, lo, pltpu.roll(hi, HALF, 1))


def _split_heads(t, n_pairs):
    out = []
    for p in range(n_pairs):
        out.extend(_split_pair(t[:, p * LANE:(p + 1) * LANE]))
    return out


def _join_heads(hs):
    return jnp.concatenate([_join_pair(hs[2 * p], hs[2 * p + 1]) for p in range(len(hs) // 2)], axis=1)


def _t5_bucket_table():
    dist = np.arange(BLK)[:, None] + BLK - np.arange(2 * BLK)[None, :]
    n = np.maximum(dist, 0)
    max_exact = N_BUCKETS // 2
    large = max_exact + (np.log(np.maximum(n, 1) / max_exact) / np.log(128 / max_exact)
                         * (N_BUCKETS - max_exact)).astype(np.int32)
    large = np.minimum(large, N_BUCKETS - 1)
    return np.where(n < max_exact, n, large).astype(np.int32)


def _my_pos():
    return lax.axis_index("x"), lax.axis_index("y"), lax.axis_index("c")


def _peer(k):
    x, y, c = _my_pos()
    return (1 - x if k & 4 else x, 1 - y if k & 2 else y, 1 - c if k & 1 else c)


def _lin(pos):
    return 4 * pos[0] + 2 * pos[1] + pos[2]


def _xchg_copies(ins, outs, sems, scatter):
    local_sem, send_sem, recv_sem = sems
    me = _lin(_my_pos())

    def source(a, slot):
        if not scatter:
            return ins[a]
        if scatter is True:
            return ins[a].at[slot]
        return ins[a].at[slot, pl.ds(scatter[1], scatter[2])]

    local, remote = [], []
    for a in range(len(ins)):
        local.append(pltpu.make_async_copy(source(a, me), outs[a].at[me], local_sem.at[a]))
    for k in range(1, N_DEV):
        peer = _peer(k)
        for a in range(len(ins)):
            remote.append(pltpu.make_async_remote_copy(source(a, _lin(peer)), outs[a].at[me], send_sem.at[a, k - 1],
                                                       recv_sem.at[a, k - 1], device_id=peer, device_id_type=MESH))
    return local, remote


def _xchg_start(ins, outs, sems, scatter):
    local, remote = _xchg_copies(ins, outs, sems, scatter)
    for cp in local + remote:
        cp.start()


def _xchg_wait(ins, outs, sems, scatter):
    local, remote = _xchg_copies(ins, outs, sems, scatter)
    for cp in local:
        cp.wait()
    for cp in remote:
        cp.wait_send()
        cp.wait_recv()


def _xchg_shapes(arrs, scatter):
    n = len(arrs)
    if isinstance(scatter, tuple):
        out_shape = [jax.ShapeDtypeStruct((a.shape[0], scatter[2]) + a.shape[2:], a.dtype) for a in arrs]
    elif scatter:
        out_shape = [jax.ShapeDtypeStruct(a.shape, a.dtype) for a in arrs]
    else:
        out_shape = [jax.ShapeDtypeStruct((N_DEV,) + a.shape, a.dtype) for a in arrs]
    sems = [pltpu.SemaphoreType.DMA((n,)), pltpu.SemaphoreType.DMA((n, N_DEV - 1)),
            pltpu.SemaphoreType.DMA((n, N_DEV - 1))]
    return out_shape, sems


_CHIPS = (2, 4, 6)


def _g2_sems(n):
    dma = pltpu.SemaphoreType.DMA
    return [dma((n,)), dma((n, N_DEV)), dma((n, N_DEV)), dma((n, len(_CHIPS))), dma((n, len(_CHIPS)))]


class _TwoLevelGather:
    def __init__(self, ins, outs, sems, windows=None):
        self.ins, self.outs = ins, outs
        self.local_sem, self.send_sem, self.recv_sem, self.fsend_sem, self.frecv_sem = sems
        self.n = len(ins)
        self.windows = windows or [None] * self.n

    def _mine(self, a):
        w = self.windows[a]
        return self.ins[a] if w is None else self.ins[a].at[pl.ds(w[0], w[1])]

    def _direct(self, a, k):
        return pltpu.make_async_remote_copy(self._mine(a), self.outs[a].at[_lin(_my_pos())], self.send_sem.at[a, k],
                                            self.recv_sem.at[a, k], device_id=_peer(k), device_id_type=MESH)

    def _handed_on(self, a, j, origin):
        slot = self.outs[a].at[origin]
        return pltpu.make_async_remote_copy(slot, slot, self.fsend_sem.at[a, j], self.frecv_sem.at[a, j],
                                            device_id=_peer(1), device_id_type=MESH)

    def _local(self, a):
        return pltpu.make_async_copy(self._mine(a), self.outs[a].at[_lin(_my_pos())], self.local_sem.at[a])

    def start(self):
        for a in range(self.n):
            self._local(a).start()
        for k in (1,) + _CHIPS:
            for a in range(self.n):
                self._direct(a, k).start()

    def forward(self):
        for j, k in enumerate(_CHIPS):
            for a in range(self.n):
                self._direct(a, k).wait_recv()
                self._handed_on(a, j, _lin(_peer(k))).start()

    def finish(self):
        for a in range(self.n):
            self._direct(a, 1).wait_recv()
            for j, k in enumerate(_CHIPS):
                self._handed_on(a, j, _lin(_peer(k ^ 1))).wait_recv()
            self._local(a).wait()
            for k in (1,) + _CHIPS:
                self._direct(a, k).wait_send()
            for j, k in enumerate(_CHIPS):
                self._handed_on(a, j, _lin(_peer(k))).wait_send()


def _mod_and_gather(c, ada_w, ada_b8, arrs):
    n = len(arrs)
    chunk = ada_w.shape[1]
    out_shape = [jax.ShapeDtypeStruct((N_DEV, 1, chunk), F32)]
    out_shape += [jax.ShapeDtypeStruct((N_DEV,) + a.shape, a.dtype) for a in arrs]

    def modulation(c_ref, w_ref, b_ref, out_ref, cbuf, part, s1, r1, s2, r2):
        me = _lin(_my_pos())
        first = []
        for k in range(1, N_DEV):
            cp = pltpu.make_async_remote_copy(c_ref, cbuf.at[me], s1.at[k - 1], r1.at[k - 1],
                                              device_id=_peer(k), device_id_type=MESH)
            cp.start()
            first.append(cp)
        cbuf[me] = c_ref[...]
        for cp in first:
            cp.wait_send()
            cp.wait_recv()
        cond = _silu(jnp.concatenate([cbuf[i] for i in range(N_DEV)], axis=0))
        mod = _mm_hi(cond, w_ref[...]) + b_ref[pl.ds(me, 1), :]
        for j in range(N_DEV):
            part[j] = mod[j:j + 1, :]
        second = []
        for k in range(1, N_DEV):
            peer = _peer(k)
            cp = pltpu.make_async_remote_copy(part.at[_lin(peer)], out_ref.at[me], s2.at[k - 1], r2.at[k - 1],
                                              device_id=peer, device_id_type=MESH)
            cp.start()
            second.append(cp)
        out_ref[me] = part[me]
        for cp in second:
            cp.wait_send()
            cp.wait_recv()

    def body(*refs):
        c_ref, w_ref, b_ref = refs[:3]
        ins = refs[3:3 + n]
        mod_ref = refs[3 + n]
        outs = refs[4 + n:4 + 2 * n]
        cbuf, part, s1, r1, s2, r2 = refs[4 + 2 * n:10 + 2 * n]
        gather = _TwoLevelGather(ins, outs, refs[10 + 2 * n:])
        gather.start()
        modulation(c_ref, w_ref, b_ref, mod_ref, cbuf, part, s1, r1, s2, r2)
        gather.forward()
        gather.finish()

    hbm = pl.BlockSpec(memory_space=pltpu.HBM)
    vm = pl.BlockSpec(memory_space=pltpu.VMEM)
    dma = pltpu.SemaphoreType.DMA
    res = pl.pallas_call(
        body, name="mod_and_gather", out_shape=out_shape, in_specs=[vm, vm, vm] + [hbm] * n,
        out_specs=[vm] + [hbm] * n,
        scratch_shapes=[pltpu.VMEM((N_DEV, 1, D_MODEL), F32), pltpu.VMEM((N_DEV, 1, chunk), F32)]
        + [dma((N_DEV - 1,))] * 4 + _g2_sems(n),
    )(c, ada_w, ada_b8, *arrs)
    return res[0], res[1:]


def _hosted_call(body, name, grid, in_specs, out_specs, out_shape, scratch_shapes, args, xchg, cparams):
    xchgs = [xchg] if isinstance(xchg, tuple) else list(xchg)
    grid = (grid,) if isinstance(grid, int) else tuple(grid)
    n_in, n_out, n_scr = len(in_specs), len(out_specs), len(scratch_shapes)
    windows = [[(a[1], a[2]) if isinstance(a, tuple) else None for a in group] for group, _ in xchgs]
    xchgs = [([a[0] if isinstance(a, tuple) else a for a in group], mode) for group, mode in xchgs]
    arrs = [a for group, _ in xchgs for a in group]
    n = len(arrs)
    x_shape, x_sems, sem_counts = [], [], []
    for (group, mode), wins in zip(xchgs, windows):
        shapes, sems = _xchg_shapes(group, False if mode == "two-level" else mode)
        if mode == "two-level":
            sems = _g2_sems(len(group))
            shapes = [s if w is None else jax.ShapeDtypeStruct((N_DEV, w[1]) + a.shape[1:], a.dtype)
                      for s, w, a in zip(shapes, wins, group)]
        x_shape += shapes
        x_sems += sems
        sem_counts.append(len(sems))
    n_steps = int(np.prod(grid))

    def hosted(*refs):
        ins, refs = refs[:n_in], refs[n_in:]
        x_in, refs = refs[:n], refs[n:]
        outs, refs = refs[:n_out], refs[n_out:]
        x_out, refs = refs[:n], refs[n:]
        scr, sems = refs[:n_scr], refs[n_scr:]
        step = pl.program_id(0)
        for d in range(1, len(grid)):
            step = step * grid[d] + pl.program_id(d)
        parts, a0, s0 = [], 0, 0
        for (group, mode), ns, wins in zip(xchgs, sem_counts, windows):
            parts.append((x_in[a0:a0 + len(group)], x_out[a0:a0 + len(group)], sems[s0:s0 + ns], mode, wins))
            a0, s0 = a0 + len(group), s0 + ns

        @pl.when(step == 0)
        def _():
            for gi, go, gs, mode, wins in parts:
                if mode == "two-level":
                    _TwoLevelGather(gi, go, gs, wins).start()
                else:
                    _xchg_start(gi, go, gs, mode)

        if any(mode == "two-level" for _, mode in xchgs):
            @pl.when(step == (2 * n_steps) // 3)
            def _():
                for gi, go, gs, mode, wins in parts:
                    if mode == "two-level":
                        _TwoLevelGather(gi, go, gs, wins).forward()

        body(*ins, *outs, *scr)

        @pl.when(step == n_steps - 1)
        def _():
            for gi, go, gs, mode, wins in parts:
                if mode == "two-level":
                    _TwoLevelGather(gi, go, gs, wins).finish()
                else:
                    _xchg_wait(gi, go, gs, mode)

    hbm = pl.BlockSpec(memory_space=pltpu.HBM)
    res = pl.pallas_call(
        hosted, name=name, grid=grid, in_specs=list(in_specs) + [hbm] * n,
        out_specs=list(out_specs) + [hbm] * n, out_shape=list(out_shape) + x_shape,
        scratch_shapes=list(scratch_shapes) + x_sems, compiler_params=cparams,
    )(*args, *arrs)
    return res[:n_out], res[n_out:]


def _row(i):
    return (i, 0)


def _fixed(i):
    return (0, 0)


def _in_proj_fwd(x, norm1, scale1, shift1, w_in, tm, xchg):
    S = x.shape[0]

    def body(x_ref, n_ref, sc_ref, sh_ref, w_ref, qkv_ref, z_ref, xbc_ref, dt_ref):
        h = _modnorm(x_ref[...], n_ref[...], sc_ref[...], sh_ref[...])
        p = _mm_nt(h, w_ref[...])
        qkv_ref[...] = p[:, :768].astype(qkv_ref.dtype)
        z_ref[...] = p[:, 768:1280]
        xbc_ref[...] = p[:, 1280:2304]
        dt_ref[...] = p[:, 2304:IN_PAD]

    vec = pl.BlockSpec((1, D_MODEL), _fixed)
    return _hosted_call(
        body, "in_proj_fwd", S // tm,
        in_specs=[pl.BlockSpec((tm, D_MODEL), _row), vec, vec, vec, pl.BlockSpec((IN_PAD, D_MODEL), _fixed)],
        out_specs=[pl.BlockSpec((tm, 768), _row), pl.BlockSpec((tm, SSM_W), _row),
                   pl.BlockSpec((tm, XBC_W), _row), pl.BlockSpec((tm, LANE), _row)],
        out_shape=[jax.ShapeDtypeStruct((S, 768), MXU_DTYPE), jax.ShapeDtypeStruct((S, SSM_W), F32),
                   jax.ShapeDtypeStruct((S, XBC_W), F32), jax.ShapeDtypeStruct((S, LANE), F32)],
        scratch_shapes=[], args=(x, norm1, scale1, shift1, w_in), xchg=xchg, cparams=_cparams(VMEM_BIG),
    )


def _in_proj_bwd(x, dx1, dq, dkv, dz, dxbc, ddt, norm1, scale1, shift1, w_in, tm, xchg):
    S = x.shape[0]

    n_steps = S // tm
    half_cols = D_MODEL // 2

    def body(x_ref, dx1_ref, dq_ref, dkv_ref, dz_ref, dxbc_ref, ddt_ref, n_ref, sc_ref, sh_ref, w_ref,
             gx_ref, h_ref, acc_ref, gw_ref, gw_acc):
        i = pl.program_id(0)

        @pl.when(i == 0)
        def _():
            acc_ref[...] = jnp.zeros_like(acc_ref)
            gw_acc[...] = jnp.zeros_like(gw_acc)

        halves = [pl.ds(k * (tm // 2), tm // 2) for k in range(2)]
        dp = [jnp.concatenate([r[rows, :] for r in (dq_ref, dkv_ref, dz_ref, dxbc_ref, ddt_ref)], axis=1)
              for rows in halves]
        dh = [_mm(dp[k], w_ref[...]) for k in range(2)]
        parts = [_modnorm_parts(x_ref[rows, :]) for rows in halves]
        hb = [(parts[k][1] * n_ref[...] * (1.0 + sc_ref[...]) + sh_ref[...]).astype(h_ref.dtype) for k in range(2)]
        gw_acc[...] += _mm_tn(dp[0], hb[0][:, :half_cols]) + _mm_tn(dp[1], hb[1][:, :half_cols])
        bwd = [_modnorm_bwd(parts[k][0], parts[k][1], n_ref[...], sc_ref[...], dh[k]) for k in range(2)]
        for k, rows in enumerate(halves):
            gx_ref[rows, :] = dx1_ref[rows, :] + bwd[k][0]
            h_ref[rows, :] = hb[k]
        acc_ref[0:1, :] += bwd[0][1] + bwd[1][1]
        acc_ref[1:2, :] += bwd[0][2] + bwd[1][2]
        acc_ref[2:3, :] += bwd[0][3] + bwd[1][3]

        @pl.when(i == n_steps - 1)
        def _():
            gw_ref[...] = gw_acc[...].astype(gw_ref.dtype)

    vec = pl.BlockSpec((1, D_MODEL), _fixed)
    return _hosted_call(
        body, "in_proj_bwd", n_steps,
        in_specs=[pl.BlockSpec((tm, D_MODEL), _row), pl.BlockSpec((tm, D_MODEL), _row),
                  pl.BlockSpec((tm, ATTN_W), _row), pl.BlockSpec((tm, 2 * KV_W), _row),
                  pl.BlockSpec((tm, SSM_W), _row), pl.BlockSpec((tm, XBC_W), _row), pl.BlockSpec((tm, LANE), _row),
                  vec, vec, vec, pl.BlockSpec((IN_PAD, D_MODEL), _fixed)],
        out_specs=[pl.BlockSpec((tm, D_MODEL), _row), pl.BlockSpec((tm, D_MODEL), _row),
                   pl.BlockSpec((8, D_MODEL), _fixed), pl.BlockSpec((IN_PAD, half_cols), _fixed)],
        out_shape=[jax.ShapeDtypeStruct((S, D_MODEL), F32), jax.ShapeDtypeStruct((S, D_MODEL), MXU_DTYPE),
                   jax.ShapeDtypeStruct((8, D_MODEL), F32), jax.ShapeDtypeStruct((IN_PAD, half_cols), WIRE_DTYPE)],
        scratch_shapes=[pltpu.VMEM((IN_PAD, half_cols), F32)],
        args=(x, dx1, dq, dkv, dz, dxbc, ddt, norm1, scale1, shift1, w_in), xchg=xchg, cparams=_cparams(VMEM_BIG),
    )


def _out_stage(ya, ys0, ys1, z0, z1, an, sn0, sn1):
    half = SSM_W // 2
    a = _rms(ya, an, ATTN_W)
    g0 = _rms(ys0 * _silu(z0), sn0, half)
    g1 = _rms(ys1 * _silu(z1), sn1, half)
    return jnp.concatenate([a, g0, g1], axis=1)


def _out_stage_args(ya_ref, ys_ref, z_ref, an_ref, sn_ref):
    half = SSM_W // 2
    return (ya_ref[...], ys_ref[:, :half], ys_ref[:, half:], z_ref[:, :half], z_ref[:, half:],
            an_ref[...], sn_ref[:, :half], sn_ref[:, half:])


def _out_proj_fwd(x, ya, ys, z, an, sn, gate1, w_o, tm, xchg):
    S = x.shape[0]

    def body(x_ref, ya_ref, ys_ref, z_ref, an_ref, sn_ref, g_ref, w_ref, x1_ref):
        u = _out_stage(*_out_stage_args(ya_ref, ys_ref, z_ref, an_ref, sn_ref))
        x1_ref[...] = x_ref[...] + g_ref[...] * _mm(u, w_ref[...])

    half = pl.BlockSpec((tm, ATTN_W), _row)
    hvec = pl.BlockSpec((1, ATTN_W), _fixed)
    (x1,), x_out = _hosted_call(
        body, "out_proj_fwd", S // tm,
        in_specs=[pl.BlockSpec((tm, D_MODEL), _row), half, half, half, hvec, hvec,
                  pl.BlockSpec((1, D_MODEL), _fixed), pl.BlockSpec((D_MODEL, D_MODEL), _fixed)],
        out_specs=[pl.BlockSpec((tm, D_MODEL), _row)],
        out_shape=[jax.ShapeDtypeStruct((S, D_MODEL), F32)],
        scratch_shapes=[], args=(x, ya, ys, z, an, sn, gate1, w_o), xchg=xchg, cparams=_cparams(VMEM_BIG),
    )
    return x1, x_out


def _out_proj_bwd(dx1, ya, ys, z, an, sn, gate1, w_o, tm, xchg):
    S = dx1.shape[0]
    n_steps = S // tm

    def body(dx1_ref, ya_ref, ys_ref, z_ref, an_ref, sn_ref, g_ref, w_ref,
             dya_ref, dys_ref, dz_ref, gw_ref, acc_ref, gw_acc):
        i = pl.program_id(0)

        @pl.when(i == 0)
        def _():
            acc_ref[...] = jnp.zeros_like(acc_ref)
            gw_acc[...] = jnp.zeros_like(gw_acc)

        u, vjp = jax.vjp(_out_stage, *_out_stage_args(ya_ref, ys_ref, z_ref, an_ref, sn_ref))
        dx1 = dx1_ref[...]
        ub = u.astype(MXU_DTYPE)
        mix = _mm(ub, w_ref[...])
        dmix = dx1 * g_ref[...]
        dmixb = dmix.astype(MXU_DTYPE)
        du = _mm_nt(dmixb, w_ref[...])
        gw_acc[...] += _mm_tn(ub, dmixb)
        dya, dys0, dys1, dz0, dz1, dan, dsn0, dsn1 = vjp(du)
        dya_ref[...] = dya
        dys_ref[...] = jnp.concatenate([dys0, dys1], axis=1)
        dz_ref[...] = jnp.concatenate([dz0, dz1], axis=1).astype(dz_ref.dtype)
        acc_ref[0:1, :] += jnp.sum(dx1 * mix, axis=0, keepdims=True)
        acc_ref[1:2, :] += jnp.concatenate([dan, dsn0, dsn1], axis=1)

        @pl.when(i == n_steps - 1)
        def _():
            gw_ref[...] = gw_acc[...].astype(gw_ref.dtype)

    half = pl.BlockSpec((tm, ATTN_W), _row)
    hvec = pl.BlockSpec((1, ATTN_W), _fixed)
    full = pl.BlockSpec((tm, D_MODEL), _row)
    return _hosted_call(
        body, "out_proj_bwd", n_steps,
        in_specs=[full, half, half, half, hvec, hvec,
                  pl.BlockSpec((1, D_MODEL), _fixed), pl.BlockSpec((D_MODEL, D_MODEL), _fixed)],
        out_specs=[half, half, half, pl.BlockSpec((D_MODEL, D_MODEL), _fixed), pl.BlockSpec((8, D_MODEL), _fixed)],
        out_shape=[jax.ShapeDtypeStruct((S, ATTN_W), F32)] * 2 + [jax.ShapeDtypeStruct((S, ATTN_W), MXU_DTYPE),
                   jax.ShapeDtypeStruct((D_MODEL, D_MODEL), WIRE_DTYPE), jax.ShapeDtypeStruct((8, D_MODEL), F32)],
        scratch_shapes=[pltpu.VMEM((D_MODEL, D_MODEL), F32)],
        args=(dx1, ya, ys, z, an, sn, gate1, w_o), xchg=xchg, cparams=_cparams(VMEM_BIG),
    )


def _loss_rows(x2, fn, tgt):
    y = _rms(x2, fn, D_MODEL)
    per_row = jnp.sum(jnp.square(y - tgt), axis=1, keepdims=True)
    return jnp.sum(per_row, axis=0, keepdims=True) * (0.5 / D_MODEL)


def _mlp_loss(x1, tgt, norm2, scale2, shift2, gate2, fnorm, w_gu, w_d, tm):
    S = x1.shape[0]
    n_pieces = len(w_gu) + len(w_d)

    def body(*refs):
        x1_ref, t_ref, n_ref, sc_ref, sh_ref, g_ref, fn_ref = refs[:7]
        piece_refs = refs[7:7 + n_pieces]
        dx1_ref, h_ref, dgu_ref, act_ref, dmlp_ref, acc_ref, wgu, wd, wsem = refs[7 + n_pieces:]

        @pl.when(pl.program_id(0) == 0)
        def _():
            acc_ref[...] = jnp.zeros_like(acc_ref)
            copies = []
            for dst, pieces in ((wgu, piece_refs[:len(w_gu)]), (wd, piece_refs[len(w_gu):])):
                shard = sum(p.shape[1] for p in pieces)
                off = 0
                for p in pieces:
                    for j in range(N_DEV):
                        copies.append(pltpu.make_async_copy(p.at[j], dst.at[pl.ds(j * shard + off, p.shape[1])],
                                                            wsem.at[len(copies)]))
                    off += p.shape[1]
            for cp in copies:
                cp.start()
            for cp in copies:
                cp.wait()

        x1 = x1_ref[...]
        gate2 = g_ref[...]
        h, vjp_h = jax.vjp(_modnorm, x1, n_ref[...], sc_ref[...], sh_ref[...])
        hb = h.astype(MXU_DTYPE)
        gu = _mm_nt(hb, wgu[...])
        g, u = gu[:, :D_FF], gu[:, D_FF:]
        sg = jax.nn.sigmoid(g)
        silu_g = g * sg
        act = (silu_g * u).astype(MXU_DTYPE)
        mlp = _mm(act, wd[...])
        x2 = x1 + gate2 * mlp
        loss, vjp_loss = jax.vjp(_loss_rows, x2, fn_ref[...], t_ref[...])
        dx2, dfn, _ = vjp_loss(jnp.ones((1, 1), F32))
        dmlp = (dx2 * gate2).astype(MXU_DTYPE)
        dact = _mm_nt(dmlp, wd[...])
        dg = dact * u * (sg * (1.0 + g * (1.0 - sg)))
        du = dact * silu_g
        dgu = jnp.concatenate([dg, du], axis=1).astype(MXU_DTYPE)
        dh = _mm(dgu, wgu[...])
        dx, dn, dsc, dsh = vjp_h(dh)
        dx1_ref[...] = dx2 + dx
        h_ref[...] = hb
        dgu_ref[...] = dgu
        act_ref[...] = act
        dmlp_ref[...] = dmlp
        acc_ref[0:1, :] += dn
        acc_ref[1:2, :] += dsc
        acc_ref[2:3, :] += dsh
        acc_ref[3:4, :] += jnp.sum(dx2 * mlp, axis=0, keepdims=True)
        acc_ref[4:5, :] += dfn
        acc_ref[5:6, :] += jnp.broadcast_to(loss, (1, D_MODEL))

    full = pl.BlockSpec((tm, D_MODEL), _row)
    vec = pl.BlockSpec((1, D_MODEL), _fixed)
    anyspec = pl.BlockSpec(memory_space=pl.ANY)
    return pl.pallas_call(
        body, name="mlp_loss", grid=(S // tm,),
        in_specs=[full, full, vec, vec, vec, vec, vec] + [anyspec] * n_pieces,
        out_specs=[full, full, pl.BlockSpec((tm, 2 * D_FF), _row), pl.BlockSpec((tm, D_FF), _row), full,
                   pl.BlockSpec((8, D_MODEL), _fixed)],
        out_shape=[jax.ShapeDtypeStruct((S, D_MODEL), F32), jax.ShapeDtypeStruct((S, D_MODEL), MXU_DTYPE),
                   jax.ShapeDtypeStruct((S, 2 * D_FF), MXU_DTYPE), jax.ShapeDtypeStruct((S, D_FF), MXU_DTYPE),
                   jax.ShapeDtypeStruct((S, D_MODEL), MXU_DTYPE), jax.ShapeDtypeStruct((8, D_MODEL), F32)],
        scratch_shapes=[pltpu.VMEM((2 * D_FF, D_MODEL), MXU_DTYPE), pltpu.VMEM((D_FF, D_MODEL), MXU_DTYPE),
                        pltpu.SemaphoreType.DMA((N_DEV * n_pieces,))],
        compiler_params=_cparams(VMEM_BIG),
    )(x1, tgt, norm2, scale2, shift2, gate2, fnorm, *w_gu, *w_d)


def _wgrad(a, g, tk, ts, name, xchg=None, g_cols=None):
    pieces = list(a) if isinstance(a, (list, tuple)) else [a]
    S = pieces[0].shape[0]
    K = sum(p.shape[1] for p in pieces)
    assert len(pieces) == 1 or tk == K
    N, col = (g.shape[1], 0) if g_cols is None else g_cols
    ns = S // ts
    n_a = len(pieces)

    def body(*refs):
        a_refs, (g_ref, o_ref, acc_ref) = refs[:n_a], refs[n_a:]
        s = pl.program_id(1)

        @pl.when(s == 0)
        def _():
            acc_ref[...] = jnp.zeros_like(acc_ref)

        a_blk = a_refs[0][...] if n_a == 1 else jnp.concatenate([r[...] for r in a_refs], axis=1)
        acc_ref[...] += _mm_tn(a_blk, g_ref[...])

        @pl.when(s == ns - 1)
        def _():
            o_ref[...] = acc_ref[...].astype(o_ref.dtype)

    if n_a == 1:
        in_specs = [pl.BlockSpec((ts, tk), lambda j, s: (s, j))]
    else:
        in_specs = [pl.BlockSpec((ts, p.shape[1]), lambda j, s: (s, 0)) for p in pieces]
    in_specs.append(pl.BlockSpec((ts, N), lambda j, s: (s, col)))
    out_spec = pl.BlockSpec((tk, N), lambda j, s: (j, 0))
    out_shape = jax.ShapeDtypeStruct((K, N), WIRE_DTYPE)
    scratch = [pltpu.VMEM((tk, N), F32)]
    args = (*pieces, g)
    if xchg is None:
        return pl.pallas_call(body, name=name, grid=(K // tk, ns), in_specs=in_specs, out_specs=out_spec,
                              out_shape=out_shape, scratch_shapes=scratch, compiler_params=_cparams(VMEM_BIG))(*args)
    (out,), x_out = _hosted_call(body, name, (K // tk, ns), in_specs, [out_spec], [out_shape], scratch, args, xchg,
                                 _cparams(VMEM_BIG))
    return out, x_out


SSD_CHUNKS_PER_STEP = 4
SSD_BWD_CHUNKS_PER_STEP = 4
ATTN_BLOCKS_PER_STEP = 4
MASKED = -1e30
QK_SCALE = HALF ** -0.5


def _attn_bias(buckets, rel_bias):
    def body(bk_ref, relb_ref, out_ref):
        bk = bk_ref[...]
        i = lax.broadcasted_iota(jnp.int32, (BLK, 2 * BLK), 0)
        j = lax.broadcasted_iota(jnp.int32, (BLK, 2 * BLK), 1)
        window = (j > i) & (j <= i + BLK)
        for h in range(N_HEADS):
            acc = jnp.zeros((BLK, 2 * BLK), F32)
            for b in range(N_BUCKETS):
                acc = jnp.where(bk == b, relb_ref[b, h], acc)
            out_ref[0, h] = jnp.where(window, acc, MASKED)
            out_ref[1, h] = jnp.where(window & (j >= BLK), acc, MASKED)

    return pl.pallas_call(
        body, name="attn_bias", out_shape=jax.ShapeDtypeStruct((2, N_HEADS, BLK, 2 * BLK), F32),
        in_specs=[pl.BlockSpec(memory_space=pltpu.VMEM), pl.BlockSpec(memory_space=pltpu.SMEM)],
    )(buckets, rel_bias)


def _attn_fwd(qkv, bias, sinks, xchg):
    S = qkv.shape[0]
    nb = S // BLK

    nq = ATTN_BLOCKS_PER_STEP if nb % ATTN_BLOCKS_PER_STEP == 0 else 1
    rows = nq * BLK

    def body(q_ref, kvp_ref, kvc_ref, bias_ref, sinks_ref, y_ref):
        i = pl.program_id(0)
        q = q_ref[...].astype(F32) * QK_SCALE
        kv = jnp.concatenate([kvp_ref[...], kvc_ref[...]], axis=0).astype(F32)
        k_lo, k_hi = _split_pair(kv[:, :LANE])
        v_lo, v_hi = _split_pair(kv[:, LANE:])
        bands = [[t[b * BLK:(b + 2) * BLK].astype(MXU_DTYPE) for t in (k_lo, k_hi, v_lo, v_hi)] for b in range(nq)]
        q_heads = [_split_heads(q[b * BLK:(b + 1) * BLK], 4) for b in range(nq)]
        first = [jnp.where(i == 0, 1, 0) if b == 0 else 0 for b in range(nq)]
        items = [(b, h) for b in range(nq) for h in range(N_HEADS)]
        s = [_mm_nt(q_heads[b][h].astype(MXU_DTYPE), bands[b][h // 4]) + bias_ref[first[b], h] for b, h in items]
        m = [jnp.maximum(jnp.max(s[n], axis=-1, keepdims=True), sinks_ref[h]) for n, (b, h) in enumerate(items)]
        p = [jnp.exp(s[n] - m[n]) for n in range(len(items))]
        rinv = [1.0 / (jnp.sum(p[n], axis=-1, keepdims=True) + jnp.exp(sinks_ref[h] - m[n]))
                for n, (b, h) in enumerate(items)]
        out = [_mm(p[n], bands[b][2 + h // 4]) * rinv[n] for n, (b, h) in enumerate(items)]
        y_ref[...] = jnp.concatenate([_join_heads(out[b * N_HEADS:(b + 1) * N_HEADS]) for b in range(nq)], axis=0)

    smem = pl.BlockSpec(memory_space=pltpu.SMEM)
    return _hosted_call(
        body, "attn_fwd", nb // nq,
        in_specs=[pl.BlockSpec((rows, ATTN_W), _row),
                  pl.BlockSpec((BLK, 2 * KV_W), lambda i: (jnp.maximum(i * nq - 1, 0), 2)),
                  pl.BlockSpec((rows, 2 * KV_W), lambda i: (i, 2)),
                  pl.BlockSpec((2, N_HEADS, BLK, 2 * BLK), lambda i: (0, 0, 0, 0)), smem],
        out_specs=[pl.BlockSpec((rows, ATTN_W), _row)],
        out_shape=[jax.ShapeDtypeStruct((S, ATTN_W), F32)],
        scratch_shapes=[],
        args=(qkv, qkv, qkv, bias, sinks), xchg=xchg, cparams=_cparams(),
    )


def _attn_bwd(qkv, y, dy, bias, sinks, xchg):
    S = qkv.shape[0]
    nb = S // BLK
    nq = ATTN_BLOCKS_PER_STEP if nb % ATTN_BLOCKS_PER_STEP == 0 else 1
    rows, n_steps = nq * BLK, nb // nq

    def body(q_ref, kvp_ref, kvc_ref, y_ref, dy_ref, bias_ref, sinks_ref, dq_ref, dkv_ref, dbias_ref, dsk_ref, carry_ref):
        i = pl.program_id(0)

        @pl.when(i == 0)
        def _():
            dbias_ref[...] = jnp.zeros_like(dbias_ref)
            dsk_ref[...] = jnp.zeros_like(dsk_ref)
            carry_ref[...] = jnp.zeros_like(carry_ref)

        q = q_ref[...].astype(F32) * QK_SCALE
        kv = jnp.concatenate([kvp_ref[...], kvc_ref[...]], axis=0).astype(F32)
        k_lo, k_hi = _split_pair(kv[:, :LANE])
        v_lo, v_hi = _split_pair(kv[:, LANE:])
        bands = [[t[b * BLK:(b + 2) * BLK].astype(MXU_DTYPE) for t in (k_lo, k_hi, v_lo, v_hi)] for b in range(nq)]
        rows_of = lambda ref, b: ref[b * BLK:(b + 1) * BLK, :]
        first = [jnp.where(i == n_steps - 1, 1, 0) if b == 0 else 0 for b in range(nq)]
        items = [(b, h) for b in range(nq) for h in range(N_HEADS)]
        at = lambda b, h: b * N_HEADS + h
        q_heads = [hd for b in range(nq) for hd in _split_heads(q[b * BLK:(b + 1) * BLK], 4)]
        y_heads = [hd for b in range(nq) for hd in _split_heads(rows_of(y_ref, b), 4)]
        dy_heads = [hd for b in range(nq) for hd in _split_heads(rows_of(dy_ref, b), 4)]
        qs = [q_heads[n].astype(MXU_DTYPE) for n in range(len(items))]
        s = [_mm_nt(qs[at(b, h)], bands[b][h // 4]) + bias_ref[first[b], h] for b, h in items]
        m = [jnp.maximum(jnp.max(s[at(b, h)], axis=-1, keepdims=True), sinks_ref[h]) for b, h in items]
        p = [jnp.exp(s[n] - m[n]) for n in range(len(items))]
        esink = [jnp.exp(sinks_ref[h] - m[at(b, h)]) for b, h in items]
        rinv = [1.0 / (jnp.sum(p[n], axis=-1, keepdims=True) + esink[n]) for n in range(len(items))]
        t = [dy_heads[n] * rinv[n] for n in range(len(items))]
        delta = [jnp.sum(t[n] * y_heads[n], axis=-1, keepdims=True) for n in range(len(items))]
        tb = [t[n].astype(MXU_DTYPE) for n in range(len(items))]
        dp = [_mm_nt(tb[at(b, h)], bands[b][2 + h // 4]) for b, h in items]
        ds = [p[n] * (dp[n] - delta[n]) for n in range(len(items))]
        for h in range(N_HEADS):
            ds_h, dsk_h = ds[at(0, h)], esink[at(0, h)] * delta[at(0, h)]
            for b in range(1, nq):
                ds_h = ds_h + ds[at(b, h)]
                dsk_h = dsk_h + esink[at(b, h)] * delta[at(b, h)]
            dbias_ref[h] += ds_h
            dsk_ref[h] -= dsk_h
        dsb = [ds[n].astype(MXU_DTYPE) for n in range(len(items))]
        pb = [p[n].astype(MXU_DTYPE) for n in range(len(items))]
        dq_heads = [_mm(dsb[at(b, h)], bands[b][h // 4]) * QK_SCALE for b, h in items]
        grp = lambda lst, b, g: jnp.concatenate(lst[at(b, 4 * g):at(b, 4 * g) + 4], axis=0)
        dk_pads = [[_mm_tn(grp(dsb, b, g), grp(qs, b, g)) for g in range(2)] for b in range(nq)]
        dv_pads = [[_mm_tn(grp(pb, b, g), grp(tb, b, g)) for g in range(2)] for b in range(nq)]
        dq_ref[...] = jnp.concatenate([_join_heads(dq_heads[b * N_HEADS:(b + 1) * N_HEADS]) for b in range(nq)],
                                      axis=0).astype(dq_ref.dtype)
        part = lambda b, lo: jnp.concatenate(
            [_join_pair(d[b][0][lo:lo + BLK], d[b][1][lo:lo + BLK]) for d in (dk_pads, dv_pads)], axis=1)
        dkv = [part(b, BLK) + (part(b + 1, 0) if b + 1 < nq else carry_ref[...]) for b in range(nq)]
        dkv_ref[...] = jnp.concatenate(dkv, axis=0).astype(dkv_ref.dtype)
        carry_ref[...] = part(0, 0)

    smem = pl.BlockSpec(memory_space=pltpu.SMEM)
    rev = lambda i: (n_steps - 1 - i, 0)
    return _hosted_call(
        body, "attn_bwd", n_steps,
        in_specs=[pl.BlockSpec((rows, ATTN_W), rev),
                  pl.BlockSpec((BLK, 2 * KV_W), lambda i: (jnp.maximum((n_steps - 1 - i) * nq - 1, 0), 2)),
                  pl.BlockSpec((rows, 2 * KV_W), lambda i: (n_steps - 1 - i, 2)),
                  pl.BlockSpec((rows, ATTN_W), rev), pl.BlockSpec((rows, ATTN_W), rev),
                  pl.BlockSpec((2, N_HEADS, BLK, 2 * BLK), lambda i: (0, 0, 0, 0)), smem],
        out_specs=[pl.BlockSpec((rows, ATTN_W), rev), pl.BlockSpec((rows, 2 * KV_W), rev),
                   pl.BlockSpec((N_HEADS, BLK, 2 * BLK), lambda i: (0, 0, 0)),
                   pl.BlockSpec((N_HEADS, BLK, 1), lambda i: (0, 0, 0))],
        out_shape=[jax.ShapeDtypeStruct((S, ATTN_W), MXU_DTYPE), jax.ShapeDtypeStruct((S, 2 * KV_W), MXU_DTYPE),
                   jax.ShapeDtypeStruct((N_HEADS, BLK, 2 * BLK), F32), jax.ShapeDtypeStruct((N_HEADS, BLK, 1), F32)],
        scratch_shapes=[pltpu.VMEM((BLK, 2 * KV_W), F32)],
        args=(qkv, qkv, qkv, y, dy, bias, sinks), xchg=xchg, cparams=_cparams(),
    )


def _attn_finish(dbias, dsk, buckets):
    def body(db_ref, dsk_ref, bk_ref, drel_ref, dsink_ref):
        bk = bk_ref[...]
        r = lax.broadcasted_iota(jnp.int32, (N_BUCKETS, LANE), 0)
        l = lax.broadcasted_iota(jnp.int32, (N_BUCKETS, LANE), 1)
        row = lax.broadcasted_iota(jnp.int32, (N_HEADS, LANE), 0)
        res = jnp.zeros((N_BUCKETS, LANE), F32)
        dsink = jnp.zeros((N_HEADS, LANE), F32)
        for h in range(N_HEADS):
            db = db_ref[h]
            for b in range(N_BUCKETS):
                v = jnp.sum(jnp.sum(jnp.where(bk == b, db, 0.0), axis=1, keepdims=True), axis=0, keepdims=True)
                res = res + jnp.where((r == b) & (l == h), v, 0.0)
            dsink = dsink + jnp.where(row == h, jnp.sum(dsk_ref[h], axis=0, keepdims=True), 0.0)
        drel_ref[...] = res
        dsink_ref[...] = dsink

    return pl.pallas_call(body, name="attn_finish",
                          out_shape=[jax.ShapeDtypeStruct((N_BUCKETS, LANE), F32),
                                     jax.ShapeDtypeStruct((N_HEADS, LANE), F32)])(dbias, dsk, buckets)


def _ssd_consts():
    r = lax.broadcasted_iota(jnp.int32, (BLK, BLK), 0)
    c = lax.broadcasted_iota(jnp.int32, (BLK, BLK), 1)
    causal = c <= r
    upper = (r <= c).astype(F32)
    last = r == BLK - 1
    head = lax.broadcasted_iota(jnp.int32, (N_HEADS, BLK), 0)
    return causal, upper, last, head


def _ssd_chunks(xs, bg, cg, dt_raw_t, prev0, dtb, alog, d_rows, consts):
    causal, upper, last, head = consts
    nq = len(xs)
    items = [(c, h) for c in range(nq) for h in range(N_HEADS)]
    at = lambda c, h: c * N_HEADS + h
    a_neg = -jnp.exp(alog)
    dt_t = [_softplus(dt_raw_t[c] + dtb) for c in range(nq)]
    acs_t = [_mm_hi(dt_t[c] * a_neg, upper) for c in range(nq)]
    cb = [[_mm_nt(cg[c][g], bg[c][g]) for g in range(2)] for c in range(nq)]
    pick = lambda t, h: jnp.sum(jnp.where(head == h, t, 0.0), axis=0, keepdims=True)
    dt_row = [pick(dt_t[c], h) for c, h in items]
    a_row = [pick(acs_t[c], h) for c, h in items]
    a_rb = [jnp.broadcast_to(a_row[n], (BLK, BLK)) for n in range(len(items))]
    a_b = [a_rb[n].T for n in range(len(items))]
    a_last = [jnp.sum(jnp.where(last, a_b[n], 0.0), axis=0, keepdims=True) for n in range(len(items))]
    w = [cb[c][h // 4] * jnp.exp(jnp.where(causal, a_b[at(c, h)] - a_rb[at(c, h)], -1e30)) * dt_row[at(c, h)]
         for c, h in items]
    f_b = [jnp.broadcast_to(dt_row[n] * jnp.exp(a_last[n] - a_row[n]), (BLK, BLK)).T for n in range(len(items))]
    y_in = [_mm(w[at(c, h)], xs[c][h]) for c, h in items]
    st = [_mm_tn(bg[c][h // 4], xs[c][h] * f_b[at(c, h)]) for c, h in items]
    e_b = [jnp.exp(a_b[n]) for n in range(len(items))]
    states = [list(prev0)]
    for c in range(nq):
        states.append([states[c][h] * jnp.exp(a_last[at(c, h)]) + st[at(c, h)] for h in range(N_HEADS)])
    y_off = [_mm(cg[c][h // 4], states[c][h]) * e_b[at(c, h)] for c, h in items]
    ys = [[y_in[at(c, h)] + y_off[at(c, h)] + d_rows[h] * xs[c][h] for h in range(N_HEADS)] for c in range(nq)]
    return ys, states


def _ssd_chunks_bwd(xs, bg, cg, dt_raw_t, prev, dtb, alog, d_rows, dys, dh_last, consts):
    causal, upper, last, head = consts
    nq = len(xs)
    items = [(c, h) for c in range(nq) for h in range(N_HEADS)]
    ni = len(items)
    at = lambda c, h: c * N_HEADS + h
    groups = [(c, g) for c in range(nq) for g in range(2)]
    lane = _lane_iota((BLK, BLK))
    lane_row = _lane_iota((1, BLK))
    a_neg = -jnp.exp(alog)
    pre_dt = [dt_raw_t[c] + dtb for c in range(nq)]
    dt_t = [_softplus(pre_dt[c]) for c in range(nq)]
    acs_t = [_mm_hi(dt_t[c] * a_neg, upper) for c in range(nq)]
    pick = lambda t, h: jnp.sum(jnp.where(head == h, t, 0.0), axis=0, keepdims=True)
    full_sum = lambda t: jnp.sum(jnp.sum(t, axis=1, keepdims=True), axis=0, keepdims=True)
    dt_row = [pick(dt_t[c], h) for c, h in items]
    a_row = [pick(acs_t[c], h) for c, h in items]
    a_rb = [jnp.broadcast_to(a_row[n], (BLK, BLK)) for n in range(ni)]
    a_b = [a_rb[n].T for n in range(ni)]
    a_last = [jnp.sum(jnp.where(last, a_b[n], 0.0), axis=0, keepdims=True) for n in range(ni)]
    lm = [jnp.exp(jnp.where(causal, a_b[n] - a_rb[n], -1e30)) for n in range(ni)]
    cgb = [[cg[c][g].astype(MXU_DTYPE) for g in range(2)] for c in range(nq)]
    bgb = [[bg[c][g].astype(MXU_DTYPE) for g in range(2)] for c in range(nq)]
    cb = [[_mm_nt(cgb[c][g], bgb[c][g]) for g in range(2)] for c in range(nq)]
    u = [cb[c][h // 4] * lm[at(c, h)] for c, h in items]
    w = [(u[n] * dt_row[n]).astype(MXU_DTYPE) for n in range(ni)]
    e_row = [jnp.exp(a_last[n] - a_row[n]) for n in range(ni)]
    f_row = [dt_row[n] * e_row[n] for n in range(ni)]
    f_b = [jnp.broadcast_to(f_row[n], (BLK, BLK)).T for n in range(ni)]
    e_b = [jnp.exp(a_b[n]) for n in range(ni)]
    el = [jnp.exp(a_last[n]) for n in range(ni)]
    xb = [xs[c][h].astype(MXU_DTYPE) for c, h in items]
    dyb = [dys[c][h].astype(MXU_DTYPE) for c, h in items]
    prevb = [prev[c][h].astype(MXU_DTYPE) for c, h in items]
    gmat = [_mm(cgb[c][h // 4], prevb[at(c, h)]) for c, h in items]
    dw = [_mm_nt(dyb[n], xb[n]) for n in range(ni)]
    dg = [dys[c][h] * e_b[at(c, h)] for c, h in items]
    dgb = [dg[n].astype(MXU_DTYPE) for n in range(ni)]
    from_y = [_mm_tn(cgb[c][h // 4], dgb[at(c, h)]) for c, h in items]
    dhs = [None] * ni
    dprev = [None] * ni
    for c in reversed(range(nq)):
        for h in range(N_HEADS):
            dhs[at(c, h)] = dh_last[h] if c == nq - 1 else dprev[at(c + 1, h)]
            dprev[at(c, h)] = from_y[at(c, h)] + dhs[at(c, h)] * el[at(c, h)]
    dstb = [dhs[n].astype(MXU_DTYPE) for n in range(ni)]
    dxf = [_mm(bgb[c][h // 4], dstb[at(c, h)]) for c, h in items]
    xfb = [(xs[c][h] * f_b[at(c, h)]).astype(MXU_DTYPE) for c, h in items]
    dxs = [_mm_tn(w[at(c, h)], dyb[at(c, h)]) + d_rows[h] * dys[c][h] + f_b[at(c, h)] * dxf[at(c, h)]
           for c, h in items]
    dd_item = [jnp.sum(dys[c][h] * xs[c][h], axis=0, keepdims=True) for c, h in items]
    dcg_h = [_mm_nt(dgb[n], prevb[n]) for n in range(ni)]
    dbg_h = [_mm_nt(xfb[n], dstb[n]) for n in range(ni)]
    zt = [dw[n] * u[n] for n in range(ni)]
    dseg = [zt[n] * dt_row[n] for n in range(ni)]
    dcb_h = [dw[n] * lm[n] * dt_row[n] for n in range(ni)]
    four = lambda lst, c, g: lst[at(c, 4 * g)] + lst[at(c, 4 * g + 1)] + lst[at(c, 4 * g + 2)] + lst[at(c, 4 * g + 3)]
    dcb = {(c, g): four(dcb_h, c, g).astype(MXU_DTYPE) for c, g in groups}
    dcg = [[four(dcg_h, c, g) + _mm(dcb[c, g], bgb[c][g]) for g in range(2)] for c in range(nq)]
    dbg = [[four(dbg_h, c, g) + _mm_tn(dcb[c, g], cgb[c][g]) for g in range(2)] for c in range(nq)]
    r1 = [jnp.sum(dg[n] * gmat[n] + dseg[n], axis=1, keepdims=True) for n in range(ni)]
    r2 = [jnp.sum(dxf[at(c, h)] * xs[c][h], axis=1, keepdims=True) for c, h in items]
    tt = [jnp.where(lane < HALF, jnp.broadcast_to(r1[n], (BLK, BLK)), jnp.broadcast_to(r2[n], (BLK, BLK))).T
          for n in range(ni)]
    r1_row = [tt[n][0:1, :] for n in range(ni)]
    r2_row = [tt[n][HALF:HALF + 1, :] for n in range(ni)]
    d_el = [full_sum(dhs[at(c, h)] * prev[c][h]) for c, h in items]
    da_last = [jnp.sum(r2_row[n] * f_row[n], axis=1, keepdims=True) + el[n] * d_el[n] for n in range(ni)]
    da_row = [r1_row[n] - jnp.sum(dseg[n], axis=0, keepdims=True) - r2_row[n] * f_row[n]
              + jnp.where(lane_row == BLK - 1, da_last[n], 0.0) for n in range(ni)]
    ddt_row = [jnp.sum(zt[n], axis=0, keepdims=True) + r2_row[n] * e_row[n] for n in range(ni)]
    draw, dalog = [], jnp.zeros((N_HEADS, BLK), F32)
    for c in range(nq):
        da_t = jnp.zeros((N_HEADS, BLK), F32)
        ddt_t = jnp.zeros((N_HEADS, BLK), F32)
        for h in range(N_HEADS):
            da_t = jnp.where(head == h, da_row[at(c, h)], da_t)
            ddt_t = jnp.where(head == h, ddt_row[at(c, h)], ddt_t)
        d_dta = _mm_hi(da_t, causal.astype(F32))
        dalog = dalog + d_dta * dt_t[c] * a_neg
        draw.append((ddt_t + d_dta * a_neg) * jax.nn.sigmoid(pre_dt[c]))
    ddtb = draw[0]
    for c in range(1, nq):
        ddtb = ddtb + draw[c]
    dd_rows = []
    for h in range(N_HEADS):
        t = dd_item[at(0, h)]
        for c in range(1, nq):
            t = t + dd_item[at(c, h)]
        dd_rows.append(t)
    return ([dxs[c * N_HEADS:(c + 1) * N_HEADS] for c in range(nq)], dbg, dcg, draw,
            [dprev[at(0, h)] for h in range(N_HEADS)], ddtb, dalog, dd_rows)


def _dt_rows(dt_blk):
    return dt_blk.T[:N_HEADS]


def _conv_pre(halo, blk, cw_ref, cb_ref):
    ext = jnp.concatenate([halo, blk], axis=0)
    taps = [pltpu.roll(ext, 3 - k, 0)[8:] for k in range(3)] + [blk]
    pre = cb_ref[...] + cw_ref[0:1, :] * taps[0]
    for k in range(1, 4):
        pre = pre + cw_ref[k:k + 1, :] * taps[k]
    return pre


def _ssd_split(pre):
    heads = _split_heads(pre[:, :SSM_W], 4)
    pb = [pre[:, SSM_W + g * D_STATE:SSM_W + (g + 1) * D_STATE] for g in range(2)]
    pc = [pre[:, SSM_W + 2 * D_STATE + g * D_STATE:SSM_W + 2 * D_STATE + (g + 1) * D_STATE] for g in range(2)]
    return heads, pb, pc


def _ssd_fwd(xbc, dt_raw, conv_w, conv_b, dtb_row, alog_row, d_exp, xchg):
    S = xbc.shape[0]
    nc = S // BLK
    nq = SSD_CHUNKS_PER_STEP if nc % SSD_CHUNKS_PER_STEP == 0 else 1
    rows = nq * BLK

    def body(xbc_ref, halo_ref, dt_ref, cw_ref, cb_ref, dtb_ref, alog_ref, d_ref, y_ref, prev_ref, pre_ref, state_ref):
        i = pl.program_id(0)

        @pl.when(i == 0)
        def _():
            state_ref[...] = jnp.zeros_like(state_ref)

        halo = halo_ref[...] * jnp.where(i > 0, 1.0, 0.0)
        pre = _conv_pre(halo, xbc_ref[...], cw_ref, cb_ref)
        pre_ref[...] = pre
        xc = _silu(pre)
        split = [_ssd_split(xc[c * BLK:(c + 1) * BLK]) for c in range(nq)]
        dt_t = [_dt_rows(dt_ref[c * BLK:(c + 1) * BLK, :]) for c in range(nq)]
        prev0 = [state_ref[h] for h in range(N_HEADS)]
        d_rows = [d_ref[h:h + 1, :] for h in range(N_HEADS)]
        ys, states = _ssd_chunks([s[0] for s in split], [s[1] for s in split], [s[2] for s in split], dt_t, prev0,
                                 dtb_ref[...], alog_ref[...], d_rows, _ssd_consts())
        for h in range(N_HEADS):
            for c in range(nq):
                prev_ref[c, h] = states[c][h]
            state_ref[h] = states[nq][h]
        y_ref[...] = jnp.concatenate([_join_heads(ys[c]) for c in range(nq)], axis=0)

    vec = pl.BlockSpec((N_HEADS, LANE), _fixed)
    return _hosted_call(
        body, "ssd_fwd", nc // nq,
        in_specs=[pl.BlockSpec((rows, XBC_W), _row),
                  pl.BlockSpec((8, XBC_W), lambda i: (jnp.maximum(i * (rows // 8) - 1, 0), 0)),
                  pl.BlockSpec((rows, LANE), _row),
                  pl.BlockSpec((4, XBC_W), _fixed), pl.BlockSpec((1, XBC_W), _fixed), vec, vec,
                  pl.BlockSpec((N_HEADS, LANE), _fixed)],
        out_specs=[pl.BlockSpec((rows, SSM_W), _row),
                   pl.BlockSpec((nq, N_HEADS, D_STATE, LANE), lambda i: (i, 0, 0, 0)),
                   pl.BlockSpec((rows, XBC_W), _row)],
        out_shape=[jax.ShapeDtypeStruct((S, SSM_W), F32), jax.ShapeDtypeStruct((nc, N_HEADS, D_STATE, LANE), F32),
                   jax.ShapeDtypeStruct((S, XBC_W), F32)],
        scratch_shapes=[pltpu.VMEM((N_HEADS, D_STATE, LANE), F32)],
        args=(xbc, xbc, dt_raw, conv_w, conv_b, dtb_row, alog_row, d_exp), xchg=xchg, cparams=_cparams(),
    )


def _ssd_bwd(xbc, pre_act, dt_raw, prev_states, dy, conv_w, dtb_row, alog_row, d_exp, xchg):
    S = xbc.shape[0]
    nc = S // BLK
    nq = SSD_BWD_CHUNKS_PER_STEP if nc % SSD_BWD_CHUNKS_PER_STEP == 0 else 1
    rows, n_steps = nq * BLK, nc // nq

    def body(xbc_ref, halo_ref, pre_ref, dt_ref, prev_ref, dy_ref, cw_ref, dtb_ref, alog_ref, d_ref,
             dxbc_ref, ddt_ref, dcw_ref, dvec_ref, dd_ref, gstate_ref, ghalo_ref):
        i = pl.program_id(0)

        @pl.when(i == 0)
        def _():
            gstate_ref[...] = jnp.zeros_like(gstate_ref)
            ghalo_ref[...] = jnp.zeros_like(ghalo_ref)
            dcw_ref[...] = jnp.zeros_like(dcw_ref)
            dvec_ref[...] = jnp.zeros_like(dvec_ref)
            dd_ref[...] = jnp.zeros_like(dd_ref)

        halo = halo_ref[...] * jnp.where(i < n_steps - 1, 1.0, 0.0)
        ext = jnp.concatenate([halo, xbc_ref[...]], axis=0)
        pre = pre_ref[...]
        sig = jax.nn.sigmoid(pre)
        xc = pre * sig
        split = [_ssd_split(xc[c * BLK:(c + 1) * BLK]) for c in range(nq)]
        dt_t = [_dt_rows(dt_ref[c * BLK:(c + 1) * BLK, :]) for c in range(nq)]
        prev = [[prev_ref[c, h] for h in range(N_HEADS)] for c in range(nq)]
        d_rows = [d_ref[h:h + 1, :] for h in range(N_HEADS)]
        dys = [_split_heads(dy_ref[c * BLK:(c + 1) * BLK, :], 4) for c in range(nq)]
        dh_last = [gstate_ref[h] for h in range(N_HEADS)]
        dheads, dpb, dpc, ddt_t, dprev0, ddtb, dalog, dd_rows = _ssd_chunks_bwd(
            [s[0] for s in split], [s[1] for s in split], [s[2] for s in split], dt_t, prev, dtb_ref[...],
            alog_ref[...], d_rows, dys, dh_last, _ssd_consts())
        for h in range(N_HEADS):
            gstate_ref[h] = dprev0[h]
            dd_ref[h:h + 1, :] += dd_rows[h]
        pad = jnp.zeros((BLK - N_HEADS, BLK), F32)
        ddt_ref[...] = jnp.concatenate([jnp.concatenate([ddt_t[c], pad], axis=0).T for c in range(nq)],
                                       axis=0).astype(ddt_ref.dtype)
        dvec_ref[0:N_HEADS, :] += ddtb
        dvec_ref[N_HEADS:, :] += dalog
        dxc = jnp.concatenate([jnp.concatenate([_join_heads(dheads[c])] + list(dpb[c]) + list(dpc[c]), axis=1)
                               for c in range(nq)], axis=0)
        dpre = dxc * (sig * (1.0 + pre * (1.0 - sig)))
        zeros8 = jnp.zeros((8, XBC_W), F32)
        dpe = jnp.concatenate([zeros8, dpre, zeros8], axis=0)
        n_ext = 16 + rows
        shifted = [pltpu.roll(dpe, n_ext - (3 - k), 0)[:8 + rows] for k in range(3)] + [dpe[:8 + rows]]
        dext = cw_ref[0:1, :] * shifted[0]
        for k in range(1, 4):
            dext = dext + cw_ref[k:k + 1, :] * shifted[k]
        for k in range(4):
            dcw_ref[k:k + 1, :] += jnp.sum(shifted[k] * ext, axis=0, keepdims=True)
        dcw_ref[4:5, :] += jnp.sum(dpre, axis=0, keepdims=True)
        dxbc_ref[...] = jnp.concatenate([dext[8:rows], dext[rows:] + ghalo_ref[...]], axis=0).astype(dxbc_ref.dtype)
        ghalo_ref[...] = dext[:8, :]

    vec = pl.BlockSpec((N_HEADS, LANE), _fixed)
    rev = lambda i: (n_steps - 1 - i, 0)
    return _hosted_call(
        body, "ssd_bwd", n_steps,
        in_specs=[pl.BlockSpec((rows, XBC_W), rev),
                  pl.BlockSpec((8, XBC_W), lambda i: (jnp.maximum((n_steps - 1 - i) * (rows // 8) - 1, 0), 0)),
                  pl.BlockSpec((rows, XBC_W), rev),
                  pl.BlockSpec((rows, LANE), rev),
                  pl.BlockSpec((nq, N_HEADS, D_STATE, LANE), lambda i: (n_steps - 1 - i, 0, 0, 0)),
                  pl.BlockSpec((rows, SSM_W), rev),
                  pl.BlockSpec((4, XBC_W), _fixed), vec, vec,
                  pl.BlockSpec((N_HEADS, LANE), _fixed)],
        out_specs=[pl.BlockSpec((rows, XBC_W), rev), pl.BlockSpec((rows, LANE), rev),
                   pl.BlockSpec((8, XBC_W), _fixed), pl.BlockSpec((2 * N_HEADS, LANE), _fixed),
                   pl.BlockSpec((N_HEADS, LANE), _fixed)],
        out_shape=[jax.ShapeDtypeStruct((S, XBC_W), MXU_DTYPE), jax.ShapeDtypeStruct((S, LANE), MXU_DTYPE),
                   jax.ShapeDtypeStruct((8, XBC_W), F32), jax.ShapeDtypeStruct((2 * N_HEADS, LANE), F32),
                   jax.ShapeDtypeStruct((N_HEADS, LANE), F32)],
        scratch_shapes=[pltpu.VMEM((N_HEADS, D_STATE, LANE), F32), pltpu.VMEM((8, XBC_W), F32)],
        args=(xbc, xbc, pre_act, dt_raw, prev_states, dy, conv_w, dtb_row, alog_row, d_exp), xchg=xchg,
        cparams=_cparams(VMEM_BIG),
    )


def _adamw_math(w, g, m, v):
    m = ADAM_B1 * m + (1.0 - ADAM_B1) * g
    v = ADAM_B2 * v + (1.0 - ADAM_B2) * jnp.square(g)
    m_hat = m / (1.0 - ADAM_B1 ** ADAM_STEP)
    v_hat = v / (1.0 - ADAM_B2 ** ADAM_STEP)
    delta = -ADAM_LR * (m_hat / (jnp.sqrt(v_hat) + ADAM_EPS) + ADAM_WD * w)
    return delta, m, v


def _reduce_adamw(parts, w, m, v, name):
    R, C = w.shape

    def body(p_ref, w_ref, m_ref, v_ref, g_ref, d_ref, nm_ref, nv_ref):
        g = p_ref[0].astype(F32)
        for i in range(1, N_DEV):
            g = g + p_ref[i].astype(F32)
        d, nm, nv = _adamw_math(w_ref[...], g, m_ref[...], v_ref[...])
        g_ref[...] = g
        d_ref[...] = d
        nm_ref[...] = nm
        nv_ref[...] = nv

    if R % 16 == 0:
        tr = max(t for t in range(16, 257, 16) if R % t == 0)
        n, blk, pblk = R // tr, pl.BlockSpec((tr, C), _row), pl.BlockSpec((N_DEV, tr, C), lambda i: (0, i, 0))
    else:
        tl = 256
        n, blk, pblk = C // tl, pl.BlockSpec((R, tl), lambda i: (0, i)), pl.BlockSpec((N_DEV, R, tl),
                                                                                      lambda i: (0, 0, i))
    return pl.pallas_call(
        body, name=name, grid=(n,), in_specs=[pblk, blk, blk, blk],
        out_specs=[blk] * 4, out_shape=[jax.ShapeDtypeStruct((R, C), F32)] * 4,
    )(parts, w, m, v)


def _reduce_adamw_hosting(parts_list, wmv_list, name, xchg):
    n_arr = len(parts_list)
    pieces = [list(p) if isinstance(p, (tuple, list)) else [p] for p in parts_list]
    n_pieces = sum(len(p) for p in pieces)
    C = wmv_list[0][0].shape[1]
    tl = 256

    def total(ref):
        g = ref[0].astype(F32)
        for i in range(1, N_DEV):
            g = g + ref[i].astype(F32)
        return g

    def body(*refs):
        p_refs, wmv_refs, o_refs = refs[:n_pieces], refs[n_pieces:n_pieces + 3 * n_arr], refs[n_pieces + 3 * n_arr:]
        at = 0
        for k in range(n_arr):
            sums = [total(r) for r in p_refs[at:at + len(pieces[k])]]
            at += len(pieces[k])
            g = sums[0] if len(sums) == 1 else jnp.concatenate(sums, axis=0)
            w_ref, m_ref, v_ref = wmv_refs[3 * k:3 * k + 3]
            d, nm, nv = _adamw_math(w_ref[...], g, m_ref[...], v_ref[...])
            for o, val in zip(o_refs[4 * k:4 * k + 4], (g, d, nm, nv)):
                o[...] = val

    in_specs = [pl.BlockSpec((N_DEV, p.shape[1], tl), lambda i: (0, 0, i)) for group in pieces for p in group]
    in_specs += [pl.BlockSpec((w.shape[0], tl), lambda i: (0, i)) for w, _, _ in wmv_list for _ in range(3)]
    out_specs = [pl.BlockSpec((w.shape[0], tl), lambda i: (0, i)) for w, _, _ in wmv_list for _ in range(4)]
    out_shape = [jax.ShapeDtypeStruct(w.shape, F32) for w, _, _ in wmv_list for _ in range(4)]
    args = [p for group in pieces for p in group] + [a for wmv in wmv_list for a in wmv]
    outs, x_out = _hosted_call(body, name, C // tl, in_specs, out_specs, out_shape, [], args, xchg,
                               _cparams(VMEM_BIG))
    return [outs[4 * k:4 * k + 4] for k in range(n_arr)], x_out


_SMALL_NAMES = ("ada_b", "norm1", "conv_w", "conv_b", "dt_bias", "A_log", "D_skip", "sinks", "attn_out_norm",
                "ssm_out_norm", "norm2", "rel_bias", "final_norm")
N_MOD = 6 * D_MODEL


def _mod_row(a0, a1, a2):
    return jnp.concatenate([a0[2:3], a0[1:2], a1[0:1], a2[2:3], a2[1:2], a2[3:4]], axis=1)


def _small_update(gathered, params):
    n_g = len(gathered)
    flat = [a for name in _SMALL_NAMES for a in params[name]]

    def body(*refs):
        a0_ref, a1_ref, a2_ref, cw_ref, dv_ref, dd_ref, ds_ref, dr_ref, c_ref = refs[:n_g]
        wmv = refs[n_g:n_g + len(flat)]
        outs = refs[n_g + len(flat):]

        def total(ref):
            t = ref[0]
            for i in range(1, N_DEV):
                t = t + ref[i]
            return t

        t0, t1, t2, tcw, tdv, tdd, tds, tdr = [total(r) for r in (a0_ref, a1_ref, a2_ref, cw_ref, dv_ref, dd_ref,
                                                                   ds_ref, dr_ref)]
        r8 = lax.broadcasted_iota(jnp.int32, (N_HEADS, LANE), 0)
        l8 = lax.broadcasted_iota(jnp.int32, (N_HEADS, LANE), 1)

        def diag_row(t):
            return jnp.sum(jnp.where(r8 == l8, t, 0.0), axis=0, keepdims=True)[:, :N_HEADS]

        def lane_sums(t):
            return diag_row(jnp.broadcast_to(jnp.sum(t, axis=1, keepdims=True), (N_HEADS, LANE)))

        me = _lin(_my_pos())
        n_cw = XBC_W // N_DEV
        cw_mine = jnp.zeros((4, n_cw), F32)
        for j in range(N_DEV):
            cw_mine = cw_mine + tcw[0:4, j * n_cw:(j + 1) * n_cw] * jnp.where(me == j, 1.0, 0.0)
        grads = {
            "ada_b": _mod_row(t0, t1, t2), "norm1": t0[0:1], "conv_w": cw_mine, "conv_b": tcw[4:5],
            "dt_bias": lane_sums(tdv[:N_HEADS]), "A_log": lane_sums(tdv[N_HEADS:]), "D_skip": lane_sums(tdd),
            "sinks": diag_row(tds), "attn_out_norm": t1[1:2, :ATTN_W], "ssm_out_norm": t1[1:2, ATTN_W:],
            "norm2": t2[0:1], "rel_bias": tdr[:, :N_HEADS], "final_norm": t2[4:5],
        }
        for k, name in enumerate(_SMALL_NAMES):
            w_ref, m_ref, v_ref = wmv[3 * k:3 * k + 3]
            g = grads[name]
            d, nm, nv = _adamw_math(w_ref[...], g, m_ref[...], v_ref[...])
            for o, val in zip(outs[4 * k:4 * k + 4], (g, d, nm, nv)):
                o[...] = val
        loss_ref, call_ref, dmod_ref = outs[4 * len(_SMALL_NAMES):]
        loss_ref[...] = t2[5:6, 0:1]
        call_ref[...] = jnp.concatenate([c_ref[i] for i in range(N_DEV)], axis=0)
        dmod_ref[...] = jnp.concatenate([_mod_row(a0_ref[i], a1_ref[i], a2_ref[i]) for i in range(N_DEV)], axis=0)

    out_shape = [jax.ShapeDtypeStruct(params[name][0].shape, F32) for name in _SMALL_NAMES for _ in range(4)]
    out_shape += [jax.ShapeDtypeStruct((1, 1), F32), jax.ShapeDtypeStruct((N_DEV, D_MODEL), F32),
                  jax.ShapeDtypeStruct((N_DEV, N_MOD), F32)]
    res = pl.pallas_call(body, name="small_update", out_shape=out_shape)(*gathered, *flat)
    upd = {name: res[4 * k:4 * k + 4] for k, name in enumerate(_SMALL_NAMES)}
    loss, c_all, dmod_all = res[4 * len(_SMALL_NAMES):]
    return upd, loss, c_all, dmod_all


def _ada_w_update(c_all, dmod_all, w, m, v):
    chunk = w.shape[1]

    def body(c_ref, dm_ref, w_ref, m_ref, v_ref, g_ref, d_ref, nm_ref, nv_ref):
        me = _lin(_my_pos())
        dm = jnp.zeros((N_DEV, chunk), F32)
        for j in range(N_DEV):
            dm = dm + dm_ref[:, j * chunk:(j + 1) * chunk] * jnp.where(me == j, 1.0, 0.0)
        g = lax.dot_general(_silu(c_ref[...]), dm, (((0,), (0,)), ((), ())), precision=HI,
                            preferred_element_type=F32)
        d, nm, nv = _adamw_math(w_ref[...], g, m_ref[...], v_ref[...])
        g_ref[...] = g
        d_ref[...] = d
        nm_ref[...] = nm
        nv_ref[...] = nv

    tr = 256
    blk = pl.BlockSpec((tr, chunk), _row)
    return pl.pallas_call(
        body, name="ada_w_update", grid=(w.shape[0] // tr,),
        in_specs=[pl.BlockSpec((N_DEV, tr), lambda i: (0, i)), pl.BlockSpec(dmod_all.shape, _fixed), blk, blk, blk],
        out_specs=[blk] * 4, out_shape=[jax.ShapeDtypeStruct(w.shape, F32)] * 4,
    )(c_all, dmod_all, w, m, v)


def _local_step(x, tgt, c, mod, w_in, conv_w, w_o_mine, w_gu_mine, w_d_mine, p):
    S = x.shape[0]
    tm = min(512, S)
    tmm = min(256, S)
    tw = min(2048, S)
    shift1, scale1, gate1, shift2, scale2, gate2 = [mod[i:i + 1] for i in range(6)]
    buckets = jnp.asarray(_t5_bucket_table())
    per_head = lambda a: jnp.broadcast_to(a.reshape(N_HEADS, 1), (N_HEADS, LANE))
    dtb_row, alog_row, d_exp = per_head(p["dt_bias"]), per_head(p["A_log"]), per_head(p["D_skip"])
    sinks = p["sinks"].reshape(N_HEADS)

    d_cut, gu_cut = WD_CUT, WGU_CUTS
    n_d, n_gu = w_d_mine.shape[0], w_gu_mine.shape[0]
    (qkv, z, xbc, dt_raw), (g_d_a,) = _in_proj_fwd(x, p["norm1"], scale1, shift1, w_in, tm,
                                                   ([(w_d_mine, 0, d_cut)], "two-level"))
    bias = _attn_bias(buckets, p["rel_bias"])
    (ya,), (g_gu_a,) = _attn_fwd(qkv, bias, sinks, ([(w_gu_mine, 0, gu_cut[0])], "two-level"))
    (ys, prev_states, pre_act), (g_gu_b, g_o) = _ssd_fwd(
        xbc, dt_raw, conv_w, p["conv_b"], dtb_row, alog_row, d_exp,
        ([(w_gu_mine, gu_cut[0], gu_cut[1] - gu_cut[0]), w_o_mine], "two-level"))
    w_o = g_o.reshape(D_MODEL, D_MODEL)
    x1, (g_gu_c, g_d_b) = _out_proj_fwd(
        x, ya, ys, z, p["attn_out_norm"], p["ssm_out_norm"], gate1, w_o, tm,
        ([(w_gu_mine, gu_cut[1], n_gu - gu_cut[1]), (w_d_mine, d_cut, n_d - d_cut)], "two-level"))
    dx1, h2, dgu, act, dmlp, acc2 = _mlp_loss(x1, tgt, p["norm2"], scale2, shift2, gate2, p["final_norm"],
                                              (g_gu_a, g_gu_b, g_gu_c), (g_d_a, g_d_b), tmm)
    g_w_gu = _wgrad(dgu, h2, 2 * D_FF // 4, tw, "wgrad_gate_up")
    g_w_d = _wgrad(act, dmlp, D_FF // 2, tw, "wgrad_down")
    gu_slots = g_w_gu.reshape(N_DEV, 2 * D_FF // N_DEV, D_MODEL)
    (dya, dys, dz, g_w_o, acc1), (r_gu_a,) = _out_proj_bwd(
        dx1, ya, ys, z, p["attn_out_norm"], p["ssm_out_norm"], gate1, w_o, tm, ([gu_slots], ("rows", 0, GGU_CUT)))
    (dq, dkv, dbias, dsk), (r_d, r_o) = _attn_bwd(
        qkv, ya, dya, bias, sinks,
        ([g_w_d.reshape(N_DEV, D_FF // N_DEV, D_MODEL), g_w_o.reshape(N_DEV, D_MODEL // N_DEV, D_MODEL)], True))
    drel, dsink = _attn_finish(dbias, dsk, buckets)
    (dxbc, ddt, dcw, dvec, dd), (r_gu_b,) = _ssd_bwd(
        xbc, pre_act, dt_raw, prev_states, dys, conv_w, dtb_row, alog_row, d_exp,
        ([gu_slots], ("rows", GGU_CUT, 2 * D_FF // N_DEV - GGU_CUT)))
    r_gu = (r_gu_a, r_gu_b)
    (gx, h1, acc0, g_in_a), others = _in_proj_bwd(x, dx1, dq, dkv, dz, dxbc, ddt, p["norm1"], scale1, shift1, w_in, tm,
                                                  ([acc1, acc2, dcw, dvec, dd, dsink, drel, c], False))
    half = D_MODEL // 2
    slots = lambda g: g[:IN_W].reshape(N_DEV, IN_W // N_DEV, half)
    g_in_b, (r_in_a, acc0_all) = _wgrad((dq, dkv, dz, dxbc, ddt), h1, IN_PAD, tw, "wgrad_in_b",
                                        [([slots(g_in_a)], True), ([acc0], False)], g_cols=(half, 1))
    return gx, (r_in_a, slots(g_in_b)), (r_o, r_gu, r_d), [acc0_all] + list(others)


def kernel(x, c, ada_w, ada_b, norm1, w_in, conv_w, conv_b, dt_bias, A_log, D_skip, sinks, attn_out_norm, ssm_out_norm, w_o, norm2, w_gate_up, w_down, rel_bias, final_norm, loss_target, m_ada_w, m_ada_b, m_norm1, m_w_in, m_conv_w, m_conv_b, m_dt_bias, m_A_log, m_D_skip, m_sinks, m_attn_out_norm, m_ssm_out_norm, m_w_o, m_norm2, m_w_gate_up, m_w_down, m_rel_bias, m_final_norm, v_ada_w, v_ada_b, v_norm1, v_w_in, v_conv_w, v_conv_b, v_dt_bias, v_A_log, v_D_skip, v_sinks, v_attn_out_norm, v_ssm_out_norm, v_w_o, v_norm2, v_w_gate_up, v_w_down, v_rel_bias, v_final_norm):
    two_d = lambda a: a if a.ndim == 2 else a.reshape(-1, a.shape[-1])
    small_params = dict(
        ada_b=(ada_b, m_ada_b, v_ada_b), norm1=(norm1, m_norm1, v_norm1), conv_w=(conv_w, m_conv_w, v_conv_w),
        conv_b=(conv_b, m_conv_b, v_conv_b), dt_bias=(dt_bias, m_dt_bias, v_dt_bias), A_log=(A_log, m_A_log, v_A_log),
        D_skip=(D_skip, m_D_skip, v_D_skip), sinks=(sinks, m_sinks, v_sinks),
        attn_out_norm=(attn_out_norm, m_attn_out_norm, v_attn_out_norm),
        ssm_out_norm=(ssm_out_norm, m_ssm_out_norm, v_ssm_out_norm), norm2=(norm2, m_norm2, v_norm2),
        rel_bias=(rel_bias, m_rel_bias, v_rel_bias), final_norm=(final_norm, m_final_norm, v_final_norm))
    small_params = {k: tuple(two_d(a) for a in v) for k, v in small_params.items()}
    S = x.shape[1]
    xs, tgt = x.reshape(S, D_MODEL), loss_target.reshape(S, D_MODEL)
    ada_w2 = ada_w[0]
    chunk = ada_w2.shape[1]
    t_in = [jnp.transpose(a[0]) for a in (w_in, m_w_in, v_w_in)]
    t_gu = [jnp.transpose(a[0]) for a in (w_gate_up, m_w_gate_up, v_w_gate_up)]

    mod, (g_in, g_cw) = _mod_and_gather(c, ada_w2, ada_b.reshape(N_DEV, chunk), [t_in[0].astype(WIRE_DTYPE), conv_w[0]])
    mod = mod.reshape(6, D_MODEL)
    w_in_full = jnp.pad(g_in.reshape(IN_W, D_MODEL), ((0, IN_PAD - IN_W), (0, 0)))
    conv_w_full = jnp.transpose(g_cw, (1, 0, 2)).reshape(4, XBC_W)

    p = {k: v[0] for k, v in small_params.items()}
    gx, (r_in_a, gw_in_b), (r_o, r_gu, r_d), gathered = _local_step(
        xs, tgt, c, mod, w_in_full, conv_w_full, w_o[0].astype(WIRE_DTYPE), t_gu[0].astype(WIRE_DTYPE),
        w_down[0].astype(WIRE_DTYPE), p)

    (u_gu, u_d, u_o), (r_in_b,) = _reduce_adamw_hosting(
        [r_gu, r_d, r_o], [tuple(t_gu), (w_down[0], m_w_down[0], v_w_down[0]), (w_o[0], m_w_o[0], v_w_o[0])],
        "adamw_big", ([gw_in_b], True))
    r_in = jnp.concatenate([r_in_a, r_in_b], axis=2)

    small, loss, c_all, dmod_all = _small_update(gathered, small_params)

    big = {
        "ada_w": _ada_w_update(c_all, dmod_all, ada_w2, m_ada_w[0], v_ada_w[0]),
        "w_in": [jnp.transpose(a) for a in _reduce_adamw(r_in, *t_in, "adamw_w_in")],
        "w_o": u_o,
        "w_gate_up": [jnp.transpose(a) for a in u_gu],
        "w_down": u_d,
    }
    big.update(small)

    order = ['ada_w', 'ada_b', 'norm1', 'w_in', 'conv_w', 'conv_b', 'dt_bias', 'A_log', 'D_skip', 'sinks',
             'attn_out_norm', 'ssm_out_norm', 'w_o', 'norm2', 'w_gate_up', 'w_down', 'rel_bias', 'final_norm']
    shapes = dict(ada_w=ada_w.shape, ada_b=ada_b.shape, norm1=norm1.shape, w_in=w_in.shape, conv_w=conv_w.shape,
                  conv_b=conv_b.shape, dt_bias=dt_bias.shape, A_log=A_log.shape, D_skip=D_skip.shape,
                  sinks=sinks.shape, attn_out_norm=attn_out_norm.shape, ssm_out_norm=ssm_out_norm.shape,
                  w_o=w_o.shape, norm2=norm2.shape, w_gate_up=w_gate_up.shape, w_down=w_down.shape,
                  rel_bias=rel_bias.shape, final_norm=final_norm.shape)
    outs = [[], [], [], []]
    for name in order:
        for kind in range(4):
            outs[kind].append(big[name][kind].reshape(shapes[name]))
    return (loss.reshape(()), gx.reshape(x.shape), *outs[0], *outs[1], *outs[2], *outs[3])
```

```python
import numpy as np
import jax
import jax.numpy as jnp
from jax import lax
from jax.experimental import pallas as pl
from jax.experimental.pallas import tpu as pltpu

F32 = jnp.float32
MXU_DTYPE = jnp.bfloat16
WIRE_DTYPE = jnp.bfloat16
HI = lax.Precision.HIGHEST
MESH = pl.DeviceIdType.MESH
N_DEV = 8

D_MODEL = 1024
ATTN_W = 512
KV_W = 128
SSM_W = 512
XBC_W = 1024
N_HEADS = 8
D_STATE = 128
D_FF = 2816
IN_W = 2312
IN_PAD = 2432
BLK = 128
N_BUCKETS = 32
EPS = 1e-6
LANE = 128
HALF = 64

ADAM_LR, ADAM_B1, ADAM_B2, ADAM_EPS, ADAM_WD, ADAM_STEP = 0.001, 0.9, 0.999, 1e-08, 0.01, 10

VMEM_BIG = 56 * 1024 * 1024
WD_CUT = 288
WGU_CUTS = (240, 496)
GGU_CUT = 304


def _cparams(vmem=None):
    if vmem is None:
        return pltpu.CompilerParams()
    return pltpu.CompilerParams(vmem_limit_bytes=vmem)


def _mm(a, b):
    return jnp.dot(a.astype(MXU_DTYPE), b.astype(MXU_DTYPE), preferred_element_type=F32)


def _mm_nt(a, b):
    return lax.dot_general(a.astype(MXU_DTYPE), b.astype(MXU_DTYPE), (((1,), (1,)), ((), ())),
                           preferred_element_type=F32)


def _mm_tn(a, b):
    return lax.dot_general(a.astype(MXU_DTYPE), b.astype(MXU_DTYPE), (((0,), (0,)), ((), ())),
                           preferred_element_type=F32)


def _mm_hi(a, b):
    return jnp.dot(a, b, precision=HI, preferred_element_type=F32)


def _silu(x):
    return x * jax.nn.sigmoid(x)


def _softplus(x):
    return jnp.maximum(x, 0.0) + jnp.log1p(jnp.exp(-jnp.abs(x)))


def _rms(x, g, n):
    return x * lax.rsqrt(jnp.sum(x * x, axis=-1, keepdims=True) * (1.0 / n) + EPS) * g


def _modnorm(x, g, scale, shift):
    return _rms(x, g, x.shape[-1]) * (1.0 + scale) + shift


def _modnorm_parts(x):
    r = lax.rsqrt(jnp.sum(x * x, axis=-1, keepdims=True) * (1.0 / x.shape[-1]) + EPS)
    return r, x * r


def _modnorm_bwd(r, xhat, g, scale, dy):
    dyg = dy * (g * (1.0 + scale))
    c = jnp.sum(dyg * xhat, axis=-1, keepdims=True) * (1.0 / xhat.shape[-1])
    dx = r * (dyg - xhat * c)
    ct = jnp.sum(dy * xhat, axis=0, keepdims=True)
    return dx, ct * (1.0 + scale), ct * g, jnp.sum(dy, axis=0, keepdims=True)


def _lane_iota(shape):
    return lax.broadcasted_iota(jnp.int32, shape, len(shape) - 1)


def _split_pair(t):
    lane = _lane_iota(t.shape)
    lo = jnp.where(lane < HALF, t, 0.0)
    hi = pltpu.roll(jnp.where(lane >= HALF, t, 0.0), HALF, 1)
    return lo, hi


def _join_pair(lo, hi):
    lane = _lane_iota(lo.shape)
    return jnp.where(lane < HALF, lo, pltpu.roll(hi, HALF, 1))


def _split_heads(t, n_pairs):
    out = []
    for p in range(n_pairs):
        out.extend(_split_pair(t[:, p * LANE:(p + 1) * LANE]))
    return out


def _join_heads(hs):
    return jnp.concatenate([_join_pair(hs[2 * p], hs[2 * p + 1]) for p in range(len(hs) // 2)], axis=1)


def _t5_bucket_table():
    dist = np.arange(BLK)[:, None] + BLK - np.arange(2 * BLK)[None, :]
    n = np.maximum(dist, 0)
    max_exact = N_BUCKETS // 2
    large = max_exact + (np.log(np.maximum(n, 1) / max_exact) / np.log(128 / max_exact)
                         * (N_BUCKETS - max_exact)).astype(np.int32)
    large = np.minimum(large, N_BUCKETS - 1)
    return np.where(n < max_exact, n, large).astype(np.int32)


def _my_pos():
    return lax.axis_index("x"), lax.axis_index("y"), lax.axis_index("c")


def _peer(k):
    x, y, c = _my_pos()
    return (1 - x if k & 4 else x, 1 - y if k & 2 else y, 1 - c if k & 1 else c)


def _lin(pos):
    return 4 * pos[0] + 2 * pos[1] + pos[2]


def _xchg_copies(ins, outs, sems, scatter):
    local_sem, send_sem, recv_sem = sems
    me = _lin(_my_pos())

    def source(a, slot):
        if not scatter:
            return ins[a]
        if scatter is True:
            return ins[a].at[slot]
        return ins[a].at[slot, pl.ds(scatter[1], scatter[2])]

    local, remote = [], []
    for a in range(len(ins)):
        local.append(pltpu.make_async_copy(source(a, me), outs[a].at[me], local_sem.at[a]))
    for k in range(1, N_DEV):
        peer = _peer(k)
        for a in range(len(ins)):
            remote.append(pltpu.make_async_remote_copy(source(a, _lin(peer)), outs[a].at[me], send_sem.at[a, k - 1],
                                                       recv_sem.at[a, k - 1], device_id=peer, device_id_type=MESH))
    return local, remote


def _xchg_start(ins, outs, sems, scatter):
    local, remote = _xchg_copies(ins, outs, sems, scatter)
    for cp in local + remote:
        cp.start()


def _xchg_wait(ins, outs, sems, scatter):
    local, remote = _xchg_copies(ins, outs, sems, scatter)
    for cp in local:
        cp.wait()
    for cp in remote:
        cp.wait_send()
        cp.wait_recv()


def _xchg_shapes(arrs, scatter):
    n = len(arrs)
    if isinstance(scatter, tuple):
        out_shape = [jax.ShapeDtypeStruct((a.shape[0], scatter[2]) + a.shape[2:], a.dtype) for a in arrs]
    elif scatter:
        out_shape = [jax.ShapeDtypeStruct(a.shape, a.dtype) for a in arrs]
    else:
        out_shape = [jax.ShapeDtypeStruct((N_DEV,) + a.shape, a.dtype) for a in arrs]
    sems = [pltpu.SemaphoreType.DMA((n,)), pltpu.SemaphoreType.DMA((n, N_DEV - 1)),
            pltpu.SemaphoreType.DMA((n, N_DEV - 1))]
    return out_shape, sems


_CHIPS = (2, 4, 6)


def _g2_sems(n):
    dma = pltpu.SemaphoreType.DMA
    return [dma((n,)), dma((n, N_DEV)), dma((n, N_DEV)), dma((n, len(_CHIPS))), dma((n, len(_CHIPS)))]


class _TwoLevelGather:
    def __init__(self, ins, outs, sems, windows=None):
        self.ins, self.outs = ins, outs
        self.local_sem, self.send_sem, self.recv_sem, self.fsend_sem, self.frecv_sem = sems
        self.n = len(ins)
        self.windows = windows or [None] * self.n

    def _mine(self, a):
        w = self.windows[a]
        return self.ins[a] if w is None else self.ins[a].at[pl.ds(w[0], w[1])]

    def _direct(self, a, k):
        return pltpu.make_async_remote_copy(self._mine(a), self.outs[a].at[_lin(_my_pos())], self.send_sem.at[a, k],
                                            self.recv_sem.at[a, k], device_id=_peer(k), device_id_type=MESH)

    def _handed_on(self, a, j, origin):
        slot = self.outs[a].at[origin]
        return pltpu.make_async_remote_copy(slot, slot, self.fsend_sem.at[a, j], self.frecv_sem.at[a, j],
                                            device_id=_peer(1), device_id_type=MESH)

    def _local(self, a):
        return pltpu.make_async_copy(self._mine(a), self.outs[a].at[_lin(_my_pos())], self.local_sem.at[a])

    def start(self):
        for a in range(self.n):
            self._local(a).start()
        for k in (1,) + _CHIPS:
            for a in range(self.n):
                self._direct(a, k).start()

    def forward(self):
        for j, k in enumerate(_CHIPS):
            for a in range(self.n):
                self._direct(a, k).wait_recv()
                self._handed_on(a, j, _lin(_peer(k))).start()

    def finish(self):
        for a in range(self.n):
            self._direct(a, 1).wait_recv()
            for j, k in enumerate(_CHIPS):
                self._handed_on(a, j, _lin(_peer(k ^ 1))).wait_recv()
            self._local(a).wait()
            for k in (1,) + _CHIPS:
                self._direct(a, k).wait_send()
            for j, k in enumerate(_CHIPS):
                self._handed_on(a, j, _lin(_peer(k))).wait_send()


def _mod_and_gather(c, ada_w, ada_b8, arrs):
    n = len(arrs)
    chunk = ada_w.shape[1]
    out_shape = [jax.ShapeDtypeStruct((N_DEV, 1, chunk), F32)]
    out_shape += [jax.ShapeDtypeStruct((N_DEV,) + a.shape, a.dtype) for a in arrs]

    def modulation(c_ref, w_ref, b_ref, out_ref, cbuf, part, s1, r1, s2, r2):
        me = _lin(_my_pos())
        first = []
        for k in range(1, N_DEV):
            cp = pltpu.make_async_remote_copy(c_ref, cbuf.at[me], s1.at[k - 1], r1.at[k - 1],
                                              device_id=_peer(k), device_id_type=MESH)
            cp.start()
            first.append(cp)
        cbuf[me] = c_ref[...]
        for cp in first:
            cp.wait_send()
            cp.wait_recv()
        cond = _silu(jnp.concatenate([cbuf[i] for i in range(N_DEV)], axis=0))
        mod = _mm_hi(cond, w_ref[...]) + b_ref[pl.ds(me, 1), :]
        for j in range(N_DEV):
            part[j] = mod[j:j + 1, :]
        second = []
        for k in range(1, N_DEV):
            peer = _peer(k)
            cp = pltpu.make_async_remote_copy(part.at[_lin(peer)], out_ref.at[me], s2.at[k - 1], r2.at[k - 1],
                                              device_id=peer, device_id_type=MESH)
            cp.start()
            second.append(cp)
        out_ref[me] = part[me]
        for cp in second:
            cp.wait_send()
            cp.wait_recv()

    def body(*refs):
        c_ref, w_ref, b_ref = refs[:3]
        ins = refs[3:3 + n]
        mod_ref = refs[3 + n]
        outs = refs[4 + n:4 + 2 * n]
        cbuf, part, s1, r1, s2, r2 = refs[4 + 2 * n:10 + 2 * n]
        gather = _TwoLevelGather(ins, outs, refs[10 + 2 * n:])
        gather.start()
        modulation(c_ref, w_ref, b_ref, mod_ref, cbuf, part, s1, r1, s2, r2)
        gather.forward()
        gather.finish()

    hbm = pl.BlockSpec(memory_space=pltpu.HBM)
    vm = pl.BlockSpec(memory_space=pltpu.VMEM)
    dma = pltpu.SemaphoreType.DMA
    res = pl.pallas_call(
        body, name="mod_and_gather", out_shape=out_shape, in_specs=[vm, vm, vm] + [hbm] * n,
        out_specs=[vm] + [hbm] * n,
        scratch_shapes=[pltpu.VMEM((N_DEV, 1, D_MODEL), F32), pltpu.VMEM((N_DEV, 1, chunk), F32)]
        + [dma((N_DEV - 1,))] * 4 + _g2_sems(n),
    )(c, ada_w, ada_b8, *arrs)
    return res[0], res[1:]


def _hosted_call(body, name, grid, in_specs, out_specs, out_shape, scratch_shapes, args, xchg, cparams):
    xchgs = [xchg] if isinstance(xchg, tuple) else list(xchg)
    grid = (grid,) if isinstance(grid, int) else tuple(grid)
    n_in, n_out, n_scr = len(in_specs), len(out_specs), len(scratch_shapes)
    windows = [[(a[1], a[2]) if isinstance(a, tuple) else None for a in group] for group, _ in xchgs]
    xchgs = [([a[0] if isinstance(a, tuple) else a for a in group], mode) for group, mode in xchgs]
    arrs = [a for group, _ in xchgs for a in group]
    n = len(arrs)
    x_shape, x_sems, sem_counts = [], [], []
    for (group, mode), wins in zip(xchgs, windows):
        shapes, sems = _xchg_shapes(group, False if mode == "two-level" else mode)
        if mode == "two-level":
            sems = _g2_sems(len(group))
            shapes = [s if w is None else jax.ShapeDtypeStruct((N_DEV, w[1]) + a.shape[1:], a.dtype)
                      for s, w, a in zip(shapes, wins, group)]
        x_shape += shapes
        x_sems += sems
        sem_counts.append(len(sems))
    n_steps = int(np.prod(grid))

    def hosted(*refs):
        ins, refs = refs[:n_in], refs[n_in:]
        x_in, refs = refs[:n], refs[n:]
        outs, refs = refs[:n_out], refs[n_out:]
        x_out, refs = refs[:n], refs[n:]
        scr, sems = refs[:n_scr], refs[n_scr:]
        step = pl.program_id(0)
        for d in range(1, len(grid)):
            step = step * grid[d] + pl.program_id(d)
        parts, a0, s0 = [], 0, 0
        for (group, mode), ns, wins in zip(xchgs, sem_counts, windows):
            parts.append((x_in[a0:a0 + len(group)], x_out[a0:a0 + len(group)], sems[s0:s0 + ns], mode, wins))
            a0, s0 = a0 + len(group), s0 + ns

        @pl.when(step == 0)
        def _():
            for gi, go, gs, mode, wins in parts:
                if mode == "two-level":
                    _TwoLevelGather(gi, go, gs, wins).start()
                else:
                    _xchg_start(gi, go, gs, mode)

        if any(mode == "two-level" for _, mode in xchgs):
            @pl.when(step == (2 * n_steps) // 3)
            def _():
                for gi, go, gs, mode, wins in parts:
                    if mode == "two-level":
                        _TwoLevelGather(gi, go, gs, wins).forward()

        body(*ins, *outs, *scr)

        @pl.when(step == n_steps - 1)
        def _():
            for gi, go, gs, mode, wins in parts:
                if mode == "two-level":
                    _TwoLevelGather(gi, go, gs, wins).finish()
                else:
                    _xchg_wait(gi, go, gs, mode)

    hbm = pl.BlockSpec(memory_space=pltpu.HBM)
    res = pl.pallas_call(
        hosted, name=name, grid=grid, in_specs=list(in_specs) + [hbm] * n,
        out_specs=list(out_specs) + [hbm] * n, out_shape=list(out_shape) + x_shape,
        scratch_shapes=list(scratch_shapes) + x_sems, compiler_params=cparams,
    )(*args, *arrs)
    return res[:n_out], res[n_out:]


def _row(i):
    return (i, 0)


def _fixed(i):
    return (0, 0)


def _in_proj_fwd(x, norm1, scale1, shift1, w_in, tm, xchg):
    S = x.shape[0]

    def body(x_ref, n_ref, sc_ref, sh_ref, w_ref, qkv_ref, z_ref, xbc_ref, dt_ref):
        h = _modnorm(x_ref[...], n_ref[...], sc_ref[...], sh_ref[...])
        p = _mm_nt(h, w_ref[...])
        qkv_ref[...] = p[:, :768].astype(qkv_ref.dtype)
        z_ref[...] = p[:, 768:1280]
        xbc_ref[...] = p[:, 1280:2304]
        dt_ref[...] = p[:, 2304:IN_PAD]

    vec = pl.BlockSpec((1, D_MODEL), _fixed)
    return _hosted_call(
        body, "in_proj_fwd", S // tm,
        in_specs=[pl.BlockSpec((tm, D_MODEL), _row), vec, vec, vec, pl.BlockSpec((IN_PAD, D_MODEL), _fixed)],
        out_specs=[pl.BlockSpec((tm, 768), _row), pl.BlockSpec((tm, SSM_W), _row),
                   pl.BlockSpec((tm, XBC_W), _row), pl.BlockSpec((tm, LANE), _row)],
        out_shape=[jax.ShapeDtypeStruct((S, 768), MXU_DTYPE), jax.ShapeDtypeStruct((S, SSM_W), F32),
                   jax.ShapeDtypeStruct((S, XBC_W), F32), jax.ShapeDtypeStruct((S, LANE), F32)],
        scratch_shapes=[], args=(x, norm1, scale1, shift1, w_in), xchg=xchg, cparams=_cparams(VMEM_BIG),
    )


def _in_proj_bwd(x, dx1, dq, dkv, dz, dxbc, ddt, norm1, scale1, shift1, w_in, tm):
    S = x.shape[0]

    n_steps = S // tm
    half_cols = D_MODEL // 2

    def body(x_ref, dx1_ref, dq_ref, dkv_ref, dz_ref, dxbc_ref, ddt_ref, n_ref, sc_ref, sh_ref, w_ref,
             gx_ref, h_ref, acc_ref, gw_ref, gw_acc):
        i = pl.program_id(0)

        @pl.when(i == 0)
        def _():
            acc_ref[...] = jnp.zeros_like(acc_ref)
            gw_acc[...] = jnp.zeros_like(gw_acc)

        halves = [pl.ds(k * (tm // 2), tm // 2) for k in range(2)]
        dp = [jnp.concatenate([r[rows, :] for r in (dq_ref, dkv_ref, dz_ref, dxbc_ref, ddt_ref)], axis=1)
              for rows in halves]
        dh = [_mm(dp[k], w_ref[...]) for k in range(2)]
        parts = [_modnorm_parts(x_ref[rows, :]) for rows in halves]
        hb = [(parts[k][1] * n_ref[...] * (1.0 + sc_ref[...]) + sh_ref[...]).astype(h_ref.dtype) for k in range(2)]
        gw_acc[...] += _mm_tn(dp[0], hb[0][:, :half_cols]) + _mm_tn(dp[1], hb[1][:, :half_cols])
        bwd = [_modnorm_bwd(parts[k][0], parts[k][1], n_ref[...], sc_ref[...], dh[k]) for k in range(2)]
        for k, rows in enumerate(halves):
            gx_ref[rows, :] = dx1_ref[rows, :] + bwd[k][0]
            h_ref[rows, :] = hb[k]
        acc_ref[0:1, :] += bwd[0][1] + bwd[1][1]
        acc_ref[1:2, :] += bwd[0][2] + bwd[1][2]
        acc_ref[2:3, :] += bwd[0][3] + bwd[1][3]

        @pl.when(i == n_steps - 1)
        def _():
            gw_ref[...] = gw_acc[...].astype(gw_ref.dtype)

    vec = pl.BlockSpec((1, D_MODEL), _fixed)
    return pl.pallas_call(
        body, name="in_proj_bwd", grid=(n_steps,),
        in_specs=[pl.BlockSpec((tm, D_MODEL), _row), pl.BlockSpec((tm, D_MODEL), _row),
                  pl.BlockSpec((tm, ATTN_W), _row), pl.BlockSpec((tm, 2 * KV_W), _row),
                  pl.BlockSpec((tm, SSM_W), _row), pl.BlockSpec((tm, XBC_W), _row), pl.BlockSpec((tm, LANE), _row),
                  vec, vec, vec, pl.BlockSpec((IN_PAD, D_MODEL), _fixed)],
        out_specs=[pl.BlockSpec((tm, D_MODEL), _row), pl.BlockSpec((tm, D_MODEL), _row),
                   pl.BlockSpec((8, D_MODEL), _fixed), pl.BlockSpec((IN_PAD, half_cols), _fixed)],
        out_shape=[jax.ShapeDtypeStruct((S, D_MODEL), F32), jax.ShapeDtypeStruct((S, D_MODEL), MXU_DTYPE),
                   jax.ShapeDtypeStruct((8, D_MODEL), F32), jax.ShapeDtypeStruct((IN_PAD, half_cols), WIRE_DTYPE)],
        scratch_shapes=[pltpu.VMEM((IN_PAD, half_cols), F32)],
        compiler_params=_cparams(VMEM_BIG),
    )(x, dx1, dq, dkv, dz, dxbc, ddt, norm1, scale1, shift1, w_in)


def _out_stage(ya, ys0, ys1, z0, z1, an, sn0, sn1):
    half = SSM_W // 2
    a = _rms(ya, an, ATTN_W)
    g0 = _rms(ys0 * _silu(z0), sn0, half)
    g1 = _rms(ys1 * _silu(z1), sn1, half)
    return jnp.concatenate([a, g0, g1], axis=1)


def _out_stage_args(ya_ref, ys_ref, z_ref, an_ref, sn_ref):
    half = SSM_W // 2
    return (ya_ref[...], ys_ref[:, :half], ys_ref[:, half:], z_ref[:, :half], z_ref[:, half:],
            an_ref[...], sn_ref[:, :half], sn_ref[:, half:])


def _out_proj_fwd(x, ya, ys, z, an, sn, gate1, w_o, tm, xchg):
    S = x.shape[0]

    def body(x_ref, ya_ref, ys_ref, z_ref, an_ref, sn_ref, g_ref, w_ref, x1_ref):
        u = _out_stage(*_out_stage_args(ya_ref, ys_ref, z_ref, an_ref, sn_ref))
        x1_ref[...] = x_ref[...] + g_ref[...] * _mm(u, w_ref[...])

    half = pl.BlockSpec((tm, ATTN_W), _row)
    hvec = pl.BlockSpec((1, ATTN_W), _fixed)
    (x1,), x_out = _hosted_call(
        body, "out_proj_fwd", S // tm,
        in_specs=[pl.BlockSpec((tm, D_MODEL), _row), half, half, half, hvec, hvec,
                  pl.BlockSpec((1, D_MODEL), _fixed), pl.BlockSpec((D_MODEL, D_MODEL), _fixed)],
        out_specs=[pl.BlockSpec((tm, D_MODEL), _row)],
        out_shape=[jax.ShapeDtypeStruct((S, D_MODEL), F32)],
        scratch_shapes=[], args=(x, ya, ys, z, an, sn, gate1, w_o), xchg=xchg, cparams=_cparams(VMEM_BIG),
    )
    return x1, x_out


def _out_proj_bwd(dx1, ya, ys, z, an, sn, gate1, w_o, tm, xchg):
    S = dx1.shape[0]
    n_steps = S // tm

    def body(dx1_ref, ya_ref, ys_ref, z_ref, an_ref, sn_ref, g_ref, w_ref,
             dya_ref, dys_ref, dz_ref, gw_ref, acc_ref, gw_acc):
        i = pl.program_id(0)

        @pl.when(i == 0)
        def _():
            acc_ref[...] = jnp.zeros_like(acc_ref)
            gw_acc[...] = jnp.zeros_like(gw_acc)

        u, vjp = jax.vjp(_out_stage, *_out_stage_args(ya_ref, ys_ref, z_ref, an_ref, sn_ref))
        dx1 = dx1_ref[...]
        ub = u.astype(MXU_DTYPE)
        mix = _mm(ub, w_ref[...])
        dmix = dx1 * g_ref[...]
        dmixb = dmix.astype(MXU_DTYPE)
        du = _mm_nt(dmixb, w_ref[...])
        gw_acc[...] += _mm_tn(ub, dmixb)
        dya, dys0, dys1, dz0, dz1, dan, dsn0, dsn1 = vjp(du)
        dya_ref[...] = dya
        dys_ref[...] = jnp.concatenate([dys0, dys1], axis=1)
        dz_ref[...] = jnp.concatenate([dz0, dz1], axis=1).astype(dz_ref.dtype)
        acc_ref[0:1, :] += jnp.sum(dx1 * mix, axis=0, keepdims=True)
        acc_ref[1:2, :] += jnp.concatenate([dan, dsn0, dsn1], axis=1)

        @pl.when(i == n_steps - 1)
        def _():
            gw_ref[...] = gw_acc[...].astype(gw_ref.dtype)

    half = pl.BlockSpec((tm, ATTN_W), _row)
    hvec = pl.BlockSpec((1, ATTN_W), _fixed)
    full = pl.BlockSpec((tm, D_MODEL), _row)
    return _hosted_call(
        body, "out_proj_bwd", n_steps,
        in_specs=[full, half, half, half, hvec, hvec,
                  pl.BlockSpec((1, D_MODEL), _fixed), pl.BlockSpec((D_MODEL, D_MODEL), _fixed)],
        out_specs=[half, half, half, pl.BlockSpec((D_MODEL, D_MODEL), _fixed), pl.BlockSpec((8, D_MODEL), _fixed)],
        out_shape=[jax.ShapeDtypeStruct((S, ATTN_W), F32)] * 2 + [jax.ShapeDtypeStruct((S, ATTN_W), MXU_DTYPE),
                   jax.ShapeDtypeStruct((D_MODEL, D_MODEL), WIRE_DTYPE), jax.ShapeDtypeStruct((8, D_MODEL), F32)],
        scratch_shapes=[pltpu.VMEM((D_MODEL, D_MODEL), F32)],
        args=(dx1, ya, ys, z, an, sn, gate1, w_o), xchg=xchg, cparams=_cparams(VMEM_BIG),
    )


def _loss_rows(x2, fn, tgt):
    y = _rms(x2, fn, D_MODEL)
    per_row = jnp.sum(jnp.square(y - tgt), axis=1, keepdims=True)
    return jnp.sum(per_row, axis=0, keepdims=True) * (0.5 / D_MODEL)


def _mlp_loss(x1, tgt, norm2, scale2, shift2, gate2, fnorm, w_gu, w_d, tm):
    S = x1.shape[0]
    n_pieces = len(w_gu) + len(w_d)

    def body(*refs):
        x1_ref, t_ref, n_ref, sc_ref, sh_ref, g_ref, fn_ref = refs[:7]
        piece_refs = refs[7:7 + n_pieces]
        dx1_ref, h_ref, dgu_ref, act_ref, dmlp_ref, acc_ref, wgu, wd, wsem = refs[7 + n_pieces:]

        @pl.when(pl.program_id(0) == 0)
        def _():
            acc_ref[...] = jnp.zeros_like(acc_ref)
            copies = []
            for dst, pieces in ((wgu, piece_refs[:len(w_gu)]), (wd, piece_refs[len(w_gu):])):
                shard = sum(p.shape[1] for p in pieces)
                off = 0
                for p in pieces:
                    for j in range(N_DEV):
                        copies.append(pltpu.make_async_copy(p.at[j], dst.at[pl.ds(j * shard + off, p.shape[1])],
                                                            wsem.at[len(copies)]))
                    off += p.shape[1]
            for cp in copies:
                cp.start()
            for cp in copies:
                cp.wait()

        x1 = x1_ref[...]
        gate2 = g_ref[...]
        h, vjp_h = jax.vjp(_modnorm, x1, n_ref[...], sc_ref[...], sh_ref[...])
        hb = h.astype(MXU_DTYPE)
        gu = _mm_nt(hb, wgu[...])
        g, u = gu[:, :D_FF], gu[:, D_FF:]
        sg = jax.nn.sigmoid(g)
        silu_g = g * sg
        act = (silu_g * u).astype(MXU_DTYPE)
        mlp = _mm(act, wd[...])
        x2 = x1 + gate2 * mlp
        loss, vjp_loss = jax.vjp(_loss_rows, x2, fn_ref[...], t_ref[...])
        dx2, dfn, _ = vjp_loss(jnp.ones((1, 1), F32))
        dmlp = (dx2 * gate2).astype(MXU_DTYPE)
        dact = _mm_nt(dmlp, wd[...])
        dg = dact * u * (sg * (1.0 + g * (1.0 - sg)))
        du = dact * silu_g
        dgu = jnp.concatenate([dg, du], axis=1).astype(MXU_DTYPE)
        dh = _mm(dgu, wgu[...])
        dx, dn, dsc, dsh = vjp_h(dh)
        dx1_ref[...] = dx2 + dx
        h_ref[...] = hb
        dgu_ref[...] = dgu
        act_ref[...] = act
        dmlp_ref[...] = dmlp
        acc_ref[0:1, :] += dn
        acc_ref[1:2, :] += dsc
        acc_ref[2:3, :] += dsh
        acc_ref[3:4, :] += jnp.sum(dx2 * mlp, axis=0, keepdims=True)
        acc_ref[4:5, :] += dfn
        acc_ref[5:6, :] += jnp.broadcast_to(loss, (1, D_MODEL))

    full = pl.BlockSpec((tm, D_MODEL), _row)
    vec = pl.BlockSpec((1, D_MODEL), _fixed)
    anyspec = pl.BlockSpec(memory_space=pl.ANY)
    return pl.pallas_call(
        body, name="mlp_loss", grid=(S // tm,),
        in_specs=[full, full, vec, vec, vec, vec, vec] + [anyspec] * n_pieces,
        out_specs=[full, full, pl.BlockSpec((tm, 2 * D_FF), _row), pl.BlockSpec((tm, D_FF), _row), full,
                   pl.BlockSpec((8, D_MODEL), _fixed)],
        out_shape=[jax.ShapeDtypeStruct((S, D_MODEL), F32), jax.ShapeDtypeStruct((S, D_MODEL), MXU_DTYPE),
                   jax.ShapeDtypeStruct((S, 2 * D_FF), MXU_DTYPE), jax.ShapeDtypeStruct((S, D_FF), MXU_DTYPE),
                   jax.ShapeDtypeStruct((S, D_MODEL), MXU_DTYPE), jax.ShapeDtypeStruct((8, D_MODEL), F32)],
        scratch_shapes=[pltpu.VMEM((2 * D_FF, D_MODEL), MXU_DTYPE), pltpu.VMEM((D_FF, D_MODEL), MXU_DTYPE),
                        pltpu.SemaphoreType.DMA((N_DEV * n_pieces,))],
        compiler_params=_cparams(VMEM_BIG),
    )(x1, tgt, norm2, scale2, shift2, gate2, fnorm, *w_gu, *w_d)


def _wgrad(a, g, tk, ts, name, xchg=None, g_cols=None):
    pieces = list(a) if isinstance(a, (list, tuple)) else [a]
    S = pieces[0].shape[0]
    K = sum(p.shape[1] for p in pieces)
    assert len(pieces) == 1 or tk == K
    N, col = (g.shape[1], 0) if g_cols is None else g_cols
    ns = S // ts
    n_a = len(pieces)

    def body(*refs):
        a_refs, (g_ref, o_ref, acc_ref) = refs[:n_a], refs[n_a:]
        s = pl.program_id(1)

        @pl.when(s == 0)
        def _():
            acc_ref[...] = jnp.zeros_like(acc_ref)

        a_blk = a_refs[0][...] if n_a == 1 else jnp.concatenate([r[...] for r in a_refs], axis=1)
        acc_ref[...] += _mm_tn(a_blk, g_ref[...])

        @pl.when(s == ns - 1)
        def _():
            o_ref[...] = acc_ref[...].astype(o_ref.dtype)

    if n_a == 1:
        in_specs = [pl.BlockSpec((ts, tk), lambda j, s: (s, j))]
    else:
        in_specs = [pl.BlockSpec((ts, p.shape[1]), lambda j, s: (s, 0)) for p in pieces]
    in_specs.append(pl.BlockSpec((ts, N), lambda j, s: (s, col)))
    out_spec = pl.BlockSpec((tk, N), lambda j, s: (j, 0))
    out_shape = jax.ShapeDtypeStruct((K, N), WIRE_DTYPE)
    scratch = [pltpu.VMEM((tk, N), F32)]
    args = (*pieces, g)
    if xchg is None:
        return pl.pallas_call(body, name=name, grid=(K // tk, ns), in_specs=in_specs, out_specs=out_spec,
                              out_shape=out_shape, scratch_shapes=scratch, compiler_params=_cparams(VMEM_BIG))(*args)
    (out,), x_out = _hosted_call(body, name, (K // tk, ns), in_specs, [out_spec], [out_shape], scratch, args, xchg,
                                 _cparams(VMEM_BIG))
    return out, x_out


SSD_CHUNKS_PER_STEP = 4
SSD_BWD_CHUNKS_PER_STEP = 4
ATTN_BLOCKS_PER_STEP = 4
MASKED = -1e30
QK_SCALE = HALF ** -0.5


def _attn_bias(buckets, rel_bias):
    def body(bk_ref, relb_ref, out_ref):
        bk = bk_ref[...]
        i = lax.broadcasted_iota(jnp.int32, (BLK, 2 * BLK), 0)
        j = lax.broadcasted_iota(jnp.int32, (BLK, 2 * BLK), 1)
        window = (j > i) & (j <= i + BLK)
        for h in range(N_HEADS):
            acc = jnp.zeros((BLK, 2 * BLK), F32)
            for b in range(N_BUCKETS):
                acc = jnp.where(bk == b, relb_ref[b, h], acc)
            out_ref[0, h] = jnp.where(window, acc, MASKED)
            out_ref[1, h] = jnp.where(window & (j >= BLK), acc, MASKED)

    return pl.pallas_call(
        body, name="attn_bias", out_shape=jax.ShapeDtypeStruct((2, N_HEADS, BLK, 2 * BLK), F32),
        in_specs=[pl.BlockSpec(memory_space=pltpu.VMEM), pl.BlockSpec(memory_space=pltpu.SMEM)],
    )(buckets, rel_bias)


def _attn_fwd(qkv, bias, sinks, xchg):
    S = qkv.shape[0]
    nb = S // BLK

    nq = ATTN_BLOCKS_PER_STEP if nb % ATTN_BLOCKS_PER_STEP == 0 else 1
    rows = nq * BLK

    def body(q_ref, kvp_ref, kvc_ref, bias_ref, sinks_ref, y_ref):
        i = pl.program_id(0)
        q = q_ref[...].astype(F32) * QK_SCALE
        kv = jnp.concatenate([kvp_ref[...], kvc_ref[...]], axis=0).astype(F32)
        k_lo, k_hi = _split_pair(kv[:, :LANE])
        v_lo, v_hi = _split_pair(kv[:, LANE:])
        bands = [[t[b * BLK:(b + 2) * BLK].astype(MXU_DTYPE) for t in (k_lo, k_hi, v_lo, v_hi)] for b in range(nq)]
        q_heads = [_split_heads(q[b * BLK:(b + 1) * BLK], 4) for b in range(nq)]
        first = [jnp.where(i == 0, 1, 0) if b == 0 else 0 for b in range(nq)]
        items = [(b, h) for b in range(nq) for h in range(N_HEADS)]
        s = [_mm_nt(q_heads[b][h].astype(MXU_DTYPE), bands[b][h // 4]) + bias_ref[first[b], h] for b, h in items]
        m = [jnp.maximum(jnp.max(s[n], axis=-1, keepdims=True), sinks_ref[h]) for n, (b, h) in enumerate(items)]
        p = [jnp.exp(s[n] - m[n]) for n in range(len(items))]
        rinv = [1.0 / (jnp.sum(p[n], axis=-1, keepdims=True) + jnp.exp(sinks_ref[h] - m[n]))
                for n, (b, h) in enumerate(items)]
        out = [_mm(p[n], bands[b][2 + h // 4]) * rinv[n] for n, (b, h) in enumerate(items)]
        y_ref[...] = jnp.concatenate([_join_heads(out[b * N_HEADS:(b + 1) * N_HEADS]) for b in range(nq)], axis=0)

    smem = pl.BlockSpec(memory_space=pltpu.SMEM)
    return _hosted_call(
        body, "attn_fwd", nb // nq,
        in_specs=[pl.BlockSpec((rows, ATTN_W), _row),
                  pl.BlockSpec((BLK, 2 * KV_W), lambda i: (jnp.maximum(i * nq - 1, 0), 2)),
                  pl.BlockSpec((rows, 2 * KV_W), lambda i: (i, 2)),
                  pl.BlockSpec((2, N_HEADS, BLK, 2 * BLK), lambda i: (0, 0, 0, 0)), smem],
        out_specs=[pl.BlockSpec((rows, ATTN_W), _row)],
        out_shape=[jax.ShapeDtypeStruct((S, ATTN_W), F32)],
        scratch_shapes=[],
        args=(qkv, qkv, qkv, bias, sinks), xchg=xchg, cparams=_cparams(),
    )


def _attn_bwd(qkv, y, dy, bias, sinks, xchg):
    S = qkv.shape[0]
    nb = S // BLK
    nq = ATTN_BLOCKS_PER_STEP if nb % ATTN_BLOCKS_PER_STEP == 0 else 1
    rows, n_steps = nq * BLK, nb // nq

    def body(q_ref, kvp_ref, kvc_ref, y_ref, dy_ref, bias_ref, sinks_ref, dq_ref, dkv_ref, dbias_ref, dsk_ref, carry_ref):
        i = pl.program_id(0)

        @pl.when(i == 0)
        def _():
            dbias_ref[...] = jnp.zeros_like(dbias_ref)
            dsk_ref[...] = jnp.zeros_like(dsk_ref)
            carry_ref[...] = jnp.zeros_like(carry_ref)

        q = q_ref[...].astype(F32) * QK_SCALE
        kv = jnp.concatenate([kvp_ref[...], kvc_ref[...]], axis=0).astype(F32)
        k_lo, k_hi = _split_pair(kv[:, :LANE])
        v_lo, v_hi = _split_pair(kv[:, LANE:])
        bands = [[t[b * BLK:(b + 2) * BLK].astype(MXU_DTYPE) for t in (k_lo, k_hi, v_lo, v_hi)] for b in range(nq)]
        rows_of = lambda ref, b: ref[b * BLK:(b + 1) * BLK, :]
        first = [jnp.where(i == n_steps - 1, 1, 0) if b == 0 else 0 for b in range(nq)]
        items = [(b, h) for b in range(nq) for h in range(N_HEADS)]
        at = lambda b, h: b * N_HEADS + h
        q_heads = [hd for b in range(nq) for hd in _split_heads(q[b * BLK:(b + 1) * BLK], 4)]
        y_heads = [hd for b in range(nq) for hd in _split_heads(rows_of(y_ref, b), 4)]
        dy_heads = [hd for b in range(nq) for hd in _split_heads(rows_of(dy_ref, b), 4)]
        qs = [q_heads[n].astype(MXU_DTYPE) for n in range(len(items))]
        s = [_mm_nt(qs[at(b, h)], bands[b][h // 4]) + bias_ref[first[b], h] for b, h in items]
        m = [jnp.maximum(jnp.max(s[at(b, h)], axis=-1, keepdims=True), sinks_ref[h]) for b, h in items]
        p = [jnp.exp(s[n] - m[n]) for n in range(len(items))]
        esink = [jnp.exp(sinks_ref[h] - m[at(b, h)]) for b, h in items]
        rinv = [1.0 / (jnp.sum(p[n], axis=-1, keepdims=True) + esink[n]) for n in range(len(items))]
        t = [dy_heads[n] * rinv[n] for n in range(len(items))]
        delta = [jnp.sum(t[n] * y_heads[n], axis=-1, keepdims=True) for n in range(len(items))]
        tb = [t[n].astype(MXU_DTYPE) for n in range(len(items))]
        dp = [_mm_nt(tb[at(b, h)], bands[b][2 + h // 4]) for b, h in items]
        ds = [p[n] * (dp[n] - delta[n]) for n in range(len(items))]
        for h in range(N_HEADS):
            ds_h, dsk_h = ds[at(0, h)], esink[at(0, h)] * delta[at(0, h)]
            for b in range(1, nq):
                ds_h = ds_h + ds[at(b, h)]
                dsk_h = dsk_h + esink[at(b, h)] * delta[at(b, h)]
            dbias_ref[h] += ds_h
            dsk_ref[h] -= dsk_h
        dsb = [ds[n].astype(MXU_DTYPE) for n in range(len(items))]
        pb = [p[n].astype(MXU_DTYPE) for n in range(len(items))]
        dq_heads = [_mm(dsb[at(b, h)], bands[b][h // 4]) * QK_SCALE for b, h in items]
        grp = lambda lst, b, g: jnp.concatenate(lst[at(b, 4 * g):at(b, 4 * g) + 4], axis=0)
        dk_pads = [[_mm_tn(grp(dsb, b, g), grp(qs, b, g)) for g in range(2)] for b in range(nq)]
        dv_pads = [[_mm_tn(grp(pb, b, g), grp(tb, b, g)) for g in range(2)] for b in range(nq)]
        dq_ref[...] = jnp.concatenate([_join_heads(dq_heads[b * N_HEADS:(b + 1) * N_HEADS]) for b in range(nq)],
                                      axis=0).astype(dq_ref.dtype)
        part = lambda b, lo: jnp.concatenate(
            [_join_pair(d[b][0][lo:lo + BLK], d[b][1][lo:lo + BLK]) for d in (dk_pads, dv_pads)], axis=1)
        dkv = [part(b, BLK) + (part(b + 1, 0) if b + 1 < nq else carry_ref[...]) for b in range(nq)]
        dkv_ref[...] = jnp.concatenate(dkv, axis=0).astype(dkv_ref.dtype)
        carry_ref[...] = part(0, 0)

    smem = pl.BlockSpec(memory_space=pltpu.SMEM)
    rev = lambda i: (n_steps - 1 - i, 0)
    return _hosted_call(
        body, "attn_bwd", n_steps,
        in_specs=[pl.BlockSpec((rows, ATTN_W), rev),
                  pl.BlockSpec((BLK, 2 * KV_W), lambda i: (jnp.maximum((n_steps - 1 - i) * nq - 1, 0), 2)),
                  pl.BlockSpec((rows, 2 * KV_W), lambda i: (n_steps - 1 - i, 2)),
                  pl.BlockSpec((rows, ATTN_W), rev), pl.BlockSpec((rows, ATTN_W), rev),
                  pl.BlockSpec((2, N_HEADS, BLK, 2 * BLK), lambda i: (0, 0, 0, 0)), smem],
        out_specs=[pl.BlockSpec((rows, ATTN_W), rev), pl.BlockSpec((rows, 2 * KV_W), rev),
                   pl.BlockSpec((N_HEADS, BLK, 2 * BLK), lambda i: (0, 0, 0)),
                   pl.BlockSpec((N_HEADS, BLK, 1), lambda i: (0, 0, 0))],
        out_shape=[jax.ShapeDtypeStruct((S, ATTN_W), MXU_DTYPE), jax.ShapeDtypeStruct((S, 2 * KV_W), MXU_DTYPE),
                   jax.ShapeDtypeStruct((N_HEADS, BLK, 2 * BLK), F32), jax.ShapeDtypeStruct((N_HEADS, BLK, 1), F32)],
        scratch_shapes=[pltpu.VMEM((BLK, 2 * KV_W), F32)],
        args=(qkv, qkv, qkv, y, dy, bias, sinks), xchg=xchg, cparams=_cparams(),
    )


def _attn_finish(dbias, dsk, buckets):
    def body(db_ref, dsk_ref, bk_ref, drel_ref, dsink_ref):
        bk = bk_ref[...]
        r = lax.broadcasted_iota(jnp.int32, (N_BUCKETS, LANE), 0)
        l = lax.broadcasted_iota(jnp.int32, (N_BUCKETS, LANE), 1)
        row = lax.broadcasted_iota(jnp.int32, (N_HEADS, LANE), 0)
        res = jnp.zeros((N_BUCKETS, LANE), F32)
        dsink = jnp.zeros((N_HEADS, LANE), F32)
        for h in range(N_HEADS):
            db = db_ref[h]
            for b in range(N_BUCKETS):
                v = jnp.sum(jnp.sum(jnp.where(bk == b, db, 0.0), axis=1, keepdims=True), axis=0, keepdims=True)
                res = res + jnp.where((r == b) & (l == h), v, 0.0)
            dsink = dsink + jnp.where(row == h, jnp.sum(dsk_ref[h], axis=0, keepdims=True), 0.0)
        drel_ref[...] = res
        dsink_ref[...] = dsink

    return pl.pallas_call(body, name="attn_finish",
                          out_shape=[jax.ShapeDtypeStruct((N_BUCKETS, LANE), F32),
                                     jax.ShapeDtypeStruct((N_HEADS, LANE), F32)])(dbias, dsk, buckets)


def _ssd_consts():
    r = lax.broadcasted_iota(jnp.int32, (BLK, BLK), 0)
    c = lax.broadcasted_iota(jnp.int32, (BLK, BLK), 1)
    causal = c <= r
    upper = (r <= c).astype(F32)
    last = r == BLK - 1
    head = lax.broadcasted_iota(jnp.int32, (N_HEADS, BLK), 0)
    return causal, upper, last, head


def _ssd_chunks(xs, bg, cg, dt_raw_t, prev0, dtb, alog, d_rows, consts):
    causal, upper, last, head = consts
    nq = len(xs)
    items = [(c, h) for c in range(nq) for h in range(N_HEADS)]
    at = lambda c, h: c * N_HEADS + h
    a_neg = -jnp.exp(alog)
    dt_t = [_softplus(dt_raw_t[c] + dtb) for c in range(nq)]
    acs_t = [_mm_hi(dt_t[c] * a_neg, upper) for c in range(nq)]
    cb = [[_mm_nt(cg[c][g], bg[c][g]) for g in range(2)] for c in range(nq)]
    pick = lambda t, h: jnp.sum(jnp.where(head == h, t, 0.0), axis=0, keepdims=True)
    dt_row = [pick(dt_t[c], h) for c, h in items]
    a_row = [pick(acs_t[c], h) for c, h in items]
    a_rb = [jnp.broadcast_to(a_row[n], (BLK, BLK)) for n in range(len(items))]
    a_b = [a_rb[n].T for n in range(len(items))]
    a_last = [jnp.sum(jnp.where(last, a_b[n], 0.0), axis=0, keepdims=True) for n in range(len(items))]
    w = [cb[c][h // 4] * jnp.exp(jnp.where(causal, a_b[at(c, h)] - a_rb[at(c, h)], -1e30)) * dt_row[at(c, h)]
         for c, h in items]
    f_b = [jnp.broadcast_to(dt_row[n] * jnp.exp(a_last[n] - a_row[n]), (BLK, BLK)).T for n in range(len(items))]
    y_in = [_mm(w[at(c, h)], xs[c][h]) for c, h in items]
    st = [_mm_tn(bg[c][h // 4], xs[c][h] * f_b[at(c, h)]) for c, h in items]
    e_b = [jnp.exp(a_b[n]) for n in range(len(items))]
    states = [list(prev0)]
    for c in range(nq):
        states.append([states[c][h] * jnp.exp(a_last[at(c, h)]) + st[at(c, h)] for h in range(N_HEADS)])
    y_off = [_mm(cg[c][h // 4], states[c][h]) * e_b[at(c, h)] for c, h in items]
    ys = [[y_in[at(c, h)] + y_off[at(c, h)] + d_rows[h] * xs[c][h] for h in range(N_HEADS)] for c in range(nq)]
    return ys, states


def _ssd_chunks_bwd(xs, bg, cg, dt_raw_t, prev, dtb, alog, d_rows, dys, dh_last, consts):
    causal, upper, last, head = consts
    nq = len(xs)
    items = [(c, h) for c in range(nq) for h in range(N_HEADS)]
    ni = len(items)
    at = lambda c, h: c * N_HEADS + h
    groups = [(c, g) for c in range(nq) for g in range(2)]
    lane = _lane_iota((BLK, BLK))
    lane_row = _lane_iota((1, BLK))
    a_neg = -jnp.exp(alog)
    pre_dt = [dt_raw_t[c] + dtb for c in range(nq)]
    dt_t = [_softplus(pre_dt[c]) for c in range(nq)]
    acs_t = [_mm_hi(dt_t[c] * a_neg, upper) for c in range(nq)]
    pick = lambda t, h: jnp.sum(jnp.where(head == h, t, 0.0), axis=0, keepdims=True)
    full_sum = lambda t: jnp.sum(jnp.sum(t, axis=1, keepdims=True), axis=0, keepdims=True)
    dt_row = [pick(dt_t[c], h) for c, h in items]
    a_row = [pick(acs_t[c], h) for c, h in items]
    a_rb = [jnp.broadcast_to(a_row[n], (BLK, BLK)) for n in range(ni)]
    a_b = [a_rb[n].T for n in range(ni)]
    a_last = [jnp.sum(jnp.where(last, a_b[n], 0.0), axis=0, keepdims=True) for n in range(ni)]
    lm = [jnp.exp(jnp.where(causal, a_b[n] - a_rb[n], -1e30)) for n in range(ni)]
    cgb = [[cg[c][g].astype(MXU_DTYPE) for g in range(2)] for c in range(nq)]
    bgb = [[bg[c][g].astype(MXU_DTYPE) for g in range(2)] for c in range(nq)]
    cb = [[_mm_nt(cgb[c][g], bgb[c][g]) for g in range(2)] for c in range(nq)]
    u = [cb[c][h // 4] * lm[at(c, h)] for c, h in items]
    w = [(u[n] * dt_row[n]).astype(MXU_DTYPE) for n in range(ni)]
    e_row = [jnp.exp(a_last[n] - a_row[n]) for n in range(ni)]
    f_row = [dt_row[n] * e_row[n] for n in range(ni)]
    f_b = [jnp.broadcast_to(f_row[n], (BLK, BLK)).T for n in range(ni)]
    e_b = [jnp.exp(a_b[n]) for n in range(ni)]
    el = [jnp.exp(a_last[n]) for n in range(ni)]
    xb = [xs[c][h].astype(MXU_DTYPE) for c, h in items]
    dyb = [dys[c][h].astype(MXU_DTYPE) for c, h in items]
    prevb = [prev[c][h].astype(MXU_DTYPE) for c, h in items]
    gmat = [_mm(cgb[c][h // 4], prevb[at(c, h)]) for c, h in items]
    dw = [_mm_nt(dyb[n], xb[n]) for n in range(ni)]
    dg = [dys[c][h] * e_b[at(c, h)] for c, h in items]
    dgb = [dg[n].astype(MXU_DTYPE) for n in range(ni)]
    from_y = [_mm_tn(cgb[c][h // 4], dgb[at(c, h)]) for c, h in items]
    dhs = [None] * ni
    dprev = [None] * ni
    for c in reversed(range(nq)):
        for h in range(N_HEADS):
            dhs[at(c, h)] = dh_last[h] if c == nq - 1 else dprev[at(c + 1, h)]
            dprev[at(c, h)] = from_y[at(c, h)] + dhs[at(c, h)] * el[at(c, h)]
    dstb = [dhs[n].astype(MXU_DTYPE) for n in range(ni)]
    dxf = [_mm(bgb[c][h // 4], dstb[at(c, h)]) for c, h in items]
    xfb = [(xs[c][h] * f_b[at(c, h)]).astype(MXU_DTYPE) for c, h in items]
    dxs = [_mm_tn(w[at(c, h)], dyb[at(c, h)]) + d_rows[h] * dys[c][h] + f_b[at(c, h)] * dxf[at(c, h)]
           for c, h in items]
    dd_item = [jnp.sum(dys[c][h] * xs[c][h], axis=0, keepdims=True) for c, h in items]
    dcg_h = [_mm_nt(dgb[n], prevb[n]) for n in range(ni)]
    dbg_h = [_mm_nt(xfb[n], dstb[n]) for n in range(ni)]
    zt = [dw[n] * u[n] for n in range(ni)]
    dseg = [zt[n] * dt_row[n] for n in range(ni)]
    dcb_h = [dw[n] * lm[n] * dt_row[n] for n in range(ni)]
    four = lambda lst, c, g: lst[at(c, 4 * g)] + lst[at(c, 4 * g + 1)] + lst[at(c, 4 * g + 2)] + lst[at(c, 4 * g + 3)]
    dcb = {(c, g): four(dcb_h, c, g).astype(MXU_DTYPE) for c, g in groups}
    dcg = [[four(dcg_h, c, g) + _mm(dcb[c, g], bgb[c][g]) for g in range(2)] for c in range(nq)]
    dbg = [[four(dbg_h, c, g) + _mm_tn(dcb[c, g], cgb[c][g]) for g in range(2)] for c in range(nq)]
    r1 = [jnp.sum(dg[n] * gmat[n] + dseg[n], axis=1, keepdims=True) for n in range(ni)]
    r2 = [jnp.sum(dxf[at(c, h)] * xs[c][h], axis=1, keepdims=True) for c, h in items]
    tt = [jnp.where(lane < HALF, jnp.broadcast_to(r1[n], (BLK, BLK)), jnp.broadcast_to(r2[n], (BLK, BLK))).T
          for n in range(ni)]
    r1_row = [tt[n][0:1, :] for n in range(ni)]
    r2_row = [tt[n][HALF:HALF + 1, :] for n in range(ni)]
    d_el = [full_sum(dhs[at(c, h)] * prev[c][h]) for c, h in items]
    da_last = [jnp.sum(r2_row[n] * f_row[n], axis=1, keepdims=True) + el[n] * d_el[n] for n in range(ni)]
    da_row = [r1_row[n] - jnp.sum(dseg[n], axis=0, keepdims=True) - r2_row[n] * f_row[n]
              + jnp.where(lane_row == BLK - 1, da_last[n], 0.0) for n in range(ni)]
    ddt_row = [jnp.sum(zt[n], axis=0, keepdims=True) + r2_row[n] * e_row[n] for n in range(ni)]
    draw, dalog = [], jnp.zeros((N_HEADS, BLK), F32)
    for c in range(nq):
        da_t = jnp.zeros((N_HEADS, BLK), F32)
        ddt_t = jnp.zeros((N_HEADS, BLK), F32)
        for h in range(N_HEADS):
            da_t = jnp.where(head == h, da_row[at(c, h)], da_t)
            ddt_t = jnp.where(head == h, ddt_row[at(c, h)], ddt_t)
        d_dta = _mm_hi(da_t, causal.astype(F32))
        dalog = dalog + d_dta * dt_t[c] * a_neg
        draw.append((ddt_t + d_dta * a_neg) * jax.nn.sigmoid(pre_dt[c]))
    ddtb = draw[0]
    for c in range(1, nq):
        ddtb = ddtb + draw[c]
    dd_rows = []
    for h in range(N_HEADS):
        t = dd_item[at(0, h)]
        for c in range(1, nq):
            t = t + dd_item[at(c, h)]
        dd_rows.append(t)
    return ([dxs[c * N_HEADS:(c + 1) * N_HEADS] for c in range(nq)], dbg, dcg, draw,
            [dprev[at(0, h)] for h in range(N_HEADS)], ddtb, dalog, dd_rows)


def _dt_rows(dt_blk):
    return dt_blk.T[:N_HEADS]


def _conv_pre(halo, blk, cw_ref, cb_ref):
    ext = jnp.concatenate([halo, blk], axis=0)
    taps = [pltpu.roll(ext, 3 - k, 0)[8:] for k in range(3)] + [blk]
    pre = cb_ref[...] + cw_ref[0:1, :] * taps[0]
    for k in range(1, 4):
        pre = pre + cw_ref[k:k + 1, :] * taps[k]
    return pre


def _ssd_split(pre):
    heads = _split_heads(pre[:, :SSM_W], 4)
    pb = [pre[:, SSM_W + g * D_STATE:SSM_W + (g + 1) * D_STATE] for g in range(2)]
    pc = [pre[:, SSM_W + 2 * D_STATE + g * D_STATE:SSM_W + 2 * D_STATE + (g + 1) * D_STATE] for g in range(2)]
    return heads, pb, pc


def _ssd_fwd(xbc, dt_raw, conv_w, conv_b, dtb_row, alog_row, d_exp, xchg):
    S = xbc.shape[0]
    nc = S // BLK
    nq = SSD_CHUNKS_PER_STEP if nc % SSD_CHUNKS_PER_STEP == 0 else 1
    rows = nq * BLK

    def body(xbc_ref, halo_ref, dt_ref, cw_ref, cb_ref, dtb_ref, alog_ref, d_ref, y_ref, prev_ref, pre_ref, state_ref):
        i = pl.program_id(0)

        @pl.when(i == 0)
        def _():
            state_ref[...] = jnp.zeros_like(state_ref)

        halo = halo_ref[...] * jnp.where(i > 0, 1.0, 0.0)
        pre = _conv_pre(halo, xbc_ref[...], cw_ref, cb_ref)
        pre_ref[...] = pre
        xc = _silu(pre)
        split = [_ssd_split(xc[c * BLK:(c + 1) * BLK]) for c in range(nq)]
        dt_t = [_dt_rows(dt_ref[c * BLK:(c + 1) * BLK, :]) for c in range(nq)]
        prev0 = [state_ref[h] for h in range(N_HEADS)]
        d_rows = [d_ref[h:h + 1, :] for h in range(N_HEADS)]
        ys, states = _ssd_chunks([s[0] for s in split], [s[1] for s in split], [s[2] for s in split], dt_t, prev0,
                                 dtb_ref[...], alog_ref[...], d_rows, _ssd_consts())
        for h in range(N_HEADS):
            for c in range(nq):
                prev_ref[c, h] = states[c][h]
            state_ref[h] = states[nq][h]
        y_ref[...] = jnp.concatenate([_join_heads(ys[c]) for c in range(nq)], axis=0)

    vec = pl.BlockSpec((N_HEADS, LANE), _fixed)
    return _hosted_call(
        body, "ssd_fwd", nc // nq,
        in_specs=[pl.BlockSpec((rows, XBC_W), _row),
                  pl.BlockSpec((8, XBC_W), lambda i: (jnp.maximum(i * (rows // 8) - 1, 0), 0)),
                  pl.BlockSpec((rows, LANE), _row),
                  pl.BlockSpec((4, XBC_W), _fixed), pl.BlockSpec((1, XBC_W), _fixed), vec, vec,
                  pl.BlockSpec((N_HEADS, LANE), _fixed)],
        out_specs=[pl.BlockSpec((rows, SSM_W), _row),
                   pl.BlockSpec((nq, N_HEADS, D_STATE, LANE), lambda i: (i, 0, 0, 0)),
                   pl.BlockSpec((rows, XBC_W), _row)],
        out_shape=[jax.ShapeDtypeStruct((S, SSM_W), F32), jax.ShapeDtypeStruct((nc, N_HEADS, D_STATE, LANE), F32),
                   jax.ShapeDtypeStruct((S, XBC_W), F32)],
        scratch_shapes=[pltpu.VMEM((N_HEADS, D_STATE, LANE), F32)],
        args=(xbc, xbc, dt_raw, conv_w, conv_b, dtb_row, alog_row, d_exp), xchg=xchg, cparams=_cparams(),
    )


def _ssd_bwd(xbc, pre_act, dt_raw, prev_states, dy, conv_w, dtb_row, alog_row, d_exp, xchg):
    S = xbc.shape[0]
    nc = S // BLK
    nq = SSD_BWD_CHUNKS_PER_STEP if nc % SSD_BWD_CHUNKS_PER_STEP == 0 else 1
    rows, n_steps = nq * BLK, nc // nq

    def body(xbc_ref, halo_ref, pre_ref, dt_ref, prev_ref, dy_ref, cw_ref, dtb_ref, alog_ref, d_ref,
             dxbc_ref, ddt_ref, dcw_ref, dvec_ref, dd_ref, gstate_ref, ghalo_ref):
        i = pl.program_id(0)

        @pl.when(i == 0)
        def _():
            gstate_ref[...] = jnp.zeros_like(gstate_ref)
            ghalo_ref[...] = jnp.zeros_like(ghalo_ref)
            dcw_ref[...] = jnp.zeros_like(dcw_ref)
            dvec_ref[...] = jnp.zeros_like(dvec_ref)
            dd_ref[...] = jnp.zeros_like(dd_ref)

        halo = halo_ref[...] * jnp.where(i < n_steps - 1, 1.0, 0.0)
        ext = jnp.concatenate([halo, xbc_ref[...]], axis=0)
        pre = pre_ref[...]
        sig = jax.nn.sigmoid(pre)
        xc = pre * sig
        split = [_ssd_split(xc[c * BLK:(c + 1) * BLK]) for c in range(nq)]
        dt_t = [_dt_rows(dt_ref[c * BLK:(c + 1) * BLK, :]) for c in range(nq)]
        prev = [[prev_ref[c, h] for h in range(N_HEADS)] for c in range(nq)]
        d_rows = [d_ref[h:h + 1, :] for h in range(N_HEADS)]
        dys = [_split_heads(dy_ref[c * BLK:(c + 1) * BLK, :], 4) for c in range(nq)]
        dh_last = [gstate_ref[h] for h in range(N_HEADS)]
        dheads, dpb, dpc, ddt_t, dprev0, ddtb, dalog, dd_rows = _ssd_chunks_bwd(
            [s[0] for s in split], [s[1] for s in split], [s[2] for s in split], dt_t, prev, dtb_ref[...],
            alog_ref[...], d_rows, dys, dh_last, _ssd_consts())
        for h in range(N_HEADS):
            gstate_ref[h] = dprev0[h]
            dd_ref[h:h + 1, :] += dd_rows[h]
        pad = jnp.zeros((BLK - N_HEADS, BLK), F32)
        ddt_ref[...] = jnp.concatenate([jnp.concatenate([ddt_t[c], pad], axis=0).T for c in range(nq)],
                                       axis=0).astype(ddt_ref.dtype)
        dvec_ref[0:N_HEADS, :] += ddtb
        dvec_ref[N_HEADS:, :] += dalog
        dxc = jnp.concatenate([jnp.concatenate([_join_heads(dheads[c])] + list(dpb[c]) + list(dpc[c]), axis=1)
                               for c in range(nq)], axis=0)
        dpre = dxc * (sig * (1.0 + pre * (1.0 - sig)))
        zeros8 = jnp.zeros((8, XBC_W), F32)
        dpe = jnp.concatenate([zeros8, dpre, zeros8], axis=0)
        n_ext = 16 + rows
        shifted = [pltpu.roll(dpe, n_ext - (3 - k), 0)[:8 + rows] for k in range(3)] + [dpe[:8 + rows]]
        dext = cw_ref[0:1, :] * shifted[0]
        for k in range(1, 4):
            dext = dext + cw_ref[k:k + 1, :] * shifted[k]
        for k in range(4):
            dcw_ref[k:k + 1, :] += jnp.sum(shifted[k] * ext, axis=0, keepdims=True)
        dcw_ref[4:5, :] += jnp.sum(dpre, axis=0, keepdims=True)
        dxbc_ref[...] = jnp.concatenate([dext[8:rows], dext[rows:] + ghalo_ref[...]], axis=0).astype(dxbc_ref.dtype)
        ghalo_ref[...] = dext[:8, :]

    vec = pl.BlockSpec((N_HEADS, LANE), _fixed)
    rev = lambda i: (n_steps - 1 - i, 0)
    return _hosted_call(
        body, "ssd_bwd", n_steps,
        in_specs=[pl.BlockSpec((rows, XBC_W), rev),
                  pl.BlockSpec((8, XBC_W), lambda i: (jnp.maximum((n_steps - 1 - i) * (rows // 8) - 1, 0), 0)),
                  pl.BlockSpec((rows, XBC_W), rev),
                  pl.BlockSpec((rows, LANE), rev),
                  pl.BlockSpec((nq, N_HEADS, D_STATE, LANE), lambda i: (n_steps - 1 - i, 0, 0, 0)),
                  pl.BlockSpec((rows, SSM_W), rev),
                  pl.BlockSpec((4, XBC_W), _fixed), vec, vec,
                  pl.BlockSpec((N_HEADS, LANE), _fixed)],
        out_specs=[pl.BlockSpec((rows, XBC_W), rev), pl.BlockSpec((rows, LANE), rev),
                   pl.BlockSpec((8, XBC_W), _fixed), pl.BlockSpec((2 * N_HEADS, LANE), _fixed),
                   pl.BlockSpec((N_HEADS, LANE), _fixed)],
        out_shape=[jax.ShapeDtypeStruct((S, XBC_W), MXU_DTYPE), jax.ShapeDtypeStruct((S, LANE), MXU_DTYPE),
                   jax.ShapeDtypeStruct((8, XBC_W), F32), jax.ShapeDtypeStruct((2 * N_HEADS, LANE), F32),
                   jax.ShapeDtypeStruct((N_HEADS, LANE), F32)],
        scratch_shapes=[pltpu.VMEM((N_HEADS, D_STATE, LANE), F32), pltpu.VMEM((8, XBC_W), F32)],
        args=(xbc, xbc, pre_act, dt_raw, prev_states, dy, conv_w, dtb_row, alog_row, d_exp), xchg=xchg,
        cparams=_cparams(VMEM_BIG),
    )


def _adamw_math(w, g, m, v):
    m = ADAM_B1 * m + (1.0 - ADAM_B1) * g
    v = ADAM_B2 * v + (1.0 - ADAM_B2) * jnp.square(g)
    m_hat = m / (1.0 - ADAM_B1 ** ADAM_STEP)
    v_hat = v / (1.0 - ADAM_B2 ** ADAM_STEP)
    delta = -ADAM_LR * (m_hat / (jnp.sqrt(v_hat) + ADAM_EPS) + ADAM_WD * w)
    return delta, m, v


def _reduce_adamw_halves(part_a, part_b, w, m, v, name):
    R, C = w.shape
    tl = 256
    n = C // tl

    def body(a_ref, b_ref, w_ref, m_ref, v_ref, g_ref, d_ref, nm_ref, nv_ref):
        ga, gb = a_ref[0].astype(F32), b_ref[0].astype(F32)
        for i in range(1, N_DEV):
            ga, gb = ga + a_ref[i].astype(F32), gb + b_ref[i].astype(F32)
        first = jnp.where(pl.program_id(0) < n // 2, 1.0, 0.0)
        g = ga * first + gb * (1.0 - first)
        d, nm, nv = _adamw_math(w_ref[...], g, m_ref[...], v_ref[...])
        g_ref[...] = g
        d_ref[...] = d
        nm_ref[...] = nm
        nv_ref[...] = nv

    blk = pl.BlockSpec((R, tl), lambda i: (0, i))
    return pl.pallas_call(
        body, name=name, grid=(n,),
        in_specs=[pl.BlockSpec((N_DEV, R, tl), lambda i: (0, 0, jnp.minimum(i, n // 2 - 1))),
                  pl.BlockSpec((N_DEV, R, tl), lambda i: (0, 0, jnp.maximum(i - n // 2, 0))), blk, blk, blk],
        out_specs=[blk] * 4, out_shape=[jax.ShapeDtypeStruct((R, C), F32)] * 4,
    )(part_a, part_b, w, m, v)


def _reduce_adamw_hosting(parts_list, wmv_list, name, xchg):
    n_arr = len(parts_list)
    pieces = [list(p) if isinstance(p, (tuple, list)) else [p] for p in parts_list]
    n_pieces = sum(len(p) for p in pieces)
    C = wmv_list[0][0].shape[1]
    tl = 256

    def total(ref):
        g = ref[0].astype(F32)
        for i in range(1, N_DEV):
            g = g + ref[i].astype(F32)
        return g

    def body(*refs):
        p_refs, wmv_refs, o_refs = refs[:n_pieces], refs[n_pieces:n_pieces + 3 * n_arr], refs[n_pieces + 3 * n_arr:]
        at = 0
        for k in range(n_arr):
            sums = [total(r) for r in p_refs[at:at + len(pieces[k])]]
            at += len(pieces[k])
            g = sums[0] if len(sums) == 1 else jnp.concatenate(sums, axis=0)
            w_ref, m_ref, v_ref = wmv_refs[3 * k:3 * k + 3]
            d, nm, nv = _adamw_math(w_ref[...], g, m_ref[...], v_ref[...])
            for o, val in zip(o_refs[4 * k:4 * k + 4], (g, d, nm, nv)):
                o[...] = val

    in_specs = [pl.BlockSpec((N_DEV, p.shape[1], tl), lambda i: (0, 0, i)) for group in pieces for p in group]
    in_specs += [pl.BlockSpec((w.shape[0], tl), lambda i: (0, i)) for w, _, _ in wmv_list for _ in range(3)]
    out_specs = [pl.BlockSpec((w.shape[0], tl), lambda i: (0, i)) for w, _, _ in wmv_list for _ in range(4)]
    out_shape = [jax.ShapeDtypeStruct(w.shape, F32) for w, _, _ in wmv_list for _ in range(4)]
    args = [p for group in pieces for p in group] + [a for wmv in wmv_list for a in wmv]
    outs, x_out = _hosted_call(body, name, C // tl, in_specs, out_specs, out_shape, [], args, xchg,
                               _cparams(VMEM_BIG))
    return [outs[4 * k:4 * k + 4] for k in range(n_arr)], x_out


_SMALL_NAMES = ("ada_b", "norm1", "conv_w", "conv_b", "dt_bias", "A_log", "D_skip", "sinks", "attn_out_norm",
                "ssm_out_norm", "norm2", "rel_bias", "final_norm")
N_MOD = 6 * D_MODEL


def _mod_row(a0, a1, a2):
    return jnp.concatenate([a0[2:3], a0[1:2], a1[0:1], a2[2:3], a2[1:2], a2[3:4]], axis=1)


def _small_update(gathered, params):
    n_g = len(gathered)
    flat = [a for name in _SMALL_NAMES for a in params[name]]

    def body(*refs):
        a0_ref, a1_ref, a2_ref, cw_ref, dv_ref, dd_ref, ds_ref, dr_ref, c_ref = refs[:n_g]
        wmv = refs[n_g:n_g + len(flat)]
        outs = refs[n_g + len(flat):]

        def total(ref):
            t = ref[0]
            for i in range(1, N_DEV):
                t = t + ref[i]
            return t

        t0, t1, t2, tcw, tdv, tdd, tds, tdr = [total(r) for r in (a0_ref, a1_ref, a2_ref, cw_ref, dv_ref, dd_ref,
                                                                   ds_ref, dr_ref)]
        r8 = lax.broadcasted_iota(jnp.int32, (N_HEADS, LANE), 0)
        l8 = lax.broadcasted_iota(jnp.int32, (N_HEADS, LANE), 1)

        def diag_row(t):
            return jnp.sum(jnp.where(r8 == l8, t, 0.0), axis=0, keepdims=True)[:, :N_HEADS]

        def lane_sums(t):
            return diag_row(jnp.broadcast_to(jnp.sum(t, axis=1, keepdims=True), (N_HEADS, LANE)))

        me = _lin(_my_pos())
        n_cw = XBC_W // N_DEV
        cw_mine = jnp.zeros((4, n_cw), F32)
        for j in range(N_DEV):
            cw_mine = cw_mine + tcw[0:4, j * n_cw:(j + 1) * n_cw] * jnp.where(me == j, 1.0, 0.0)
        grads = {
            "ada_b": _mod_row(t0, t1, t2), "norm1": t0[0:1], "conv_w": cw_mine, "conv_b": tcw[4:5],
            "dt_bias": lane_sums(tdv[:N_HEADS]), "A_log": lane_sums(tdv[N_HEADS:]), "D_skip": lane_sums(tdd),
            "sinks": diag_row(tds), "attn_out_norm": t1[1:2, :ATTN_W], "ssm_out_norm": t1[1:2, ATTN_W:],
            "norm2": t2[0:1], "rel_bias": tdr[:, :N_HEADS], "final_norm": t2[4:5],
        }
        for k, name in enumerate(_SMALL_NAMES):
            w_ref, m_ref, v_ref = wmv[3 * k:3 * k + 3]
            g = grads[name]
            d, nm, nv = _adamw_math(w_ref[...], g, m_ref[...], v_ref[...])
            for o, val in zip(outs[4 * k:4 * k + 4], (g, d, nm, nv)):
                o[...] = val
        loss_ref, call_ref, dmod_ref = outs[4 * len(_SMALL_NAMES):]
        loss_ref[...] = t2[5:6, 0:1]
        call_ref[...] = jnp.concatenate([c_ref[i] for i in range(N_DEV)], axis=0)
        dmod_ref[...] = jnp.concatenate([_mod_row(a0_ref[i], a1_ref[i], a2_ref[i]) for i in range(N_DEV)], axis=0)

    out_shape = [jax.ShapeDtypeStruct(params[name][0].shape, F32) for name in _SMALL_NAMES for _ in range(4)]
    out_shape += [jax.ShapeDtypeStruct((1, 1), F32), jax.ShapeDtypeStruct((N_DEV, D_MODEL), F32),
                  jax.ShapeDtypeStruct((N_DEV, N_MOD), F32)]
    res = pl.pallas_call(body, name="small_update", out_shape=out_shape)(*gathered, *flat)
    upd = {name: res[4 * k:4 * k + 4] for k, name in enumerate(_SMALL_NAMES)}
    loss, c_all, dmod_all = res[4 * len(_SMALL_NAMES):]
    return upd, loss, c_all, dmod_all


def _ada_w_update(c_all, dmod_all, w, m, v):
    chunk = w.shape[1]

    def body(c_ref, dm_ref, w_ref, m_ref, v_ref, g_ref, d_ref, nm_ref, nv_ref):
        me = _lin(_my_pos())
        dm = jnp.zeros((N_DEV, chunk), F32)
        for j in range(N_DEV):
            dm = dm + dm_ref[:, j * chunk:(j + 1) * chunk] * jnp.where(me == j, 1.0, 0.0)
        g = lax.dot_general(_silu(c_ref[...]), dm, (((0,), (0,)), ((), ())), precision=HI,
                            preferred_element_type=F32)
        d, nm, nv = _adamw_math(w_ref[...], g, m_ref[...], v_ref[...])
        g_ref[...] = g
        d_ref[...] = d
        nm_ref[...] = nm
        nv_ref[...] = nv

    tr = 256
    blk = pl.BlockSpec((tr, chunk), _row)
    return pl.pallas_call(
        body, name="ada_w_update", grid=(w.shape[0] // tr,),
        in_specs=[pl.BlockSpec((N_DEV, tr), lambda i: (0, i)), pl.BlockSpec(dmod_all.shape, _fixed), blk, blk, blk],
        out_specs=[blk] * 4, out_shape=[jax.ShapeDtypeStruct(w.shape, F32)] * 4,
    )(c_all, dmod_all, w, m, v)


def _local_step(x, tgt, c, mod, w_in, conv_w, w_o_mine, w_gu_mine, w_d_mine, p):
    S = x.shape[0]
    tm = min(512, S)
    tmm = min(256, S)
    tw = min(2048, S)
    shift1, scale1, gate1, shift2, scale2, gate2 = [mod[i:i + 1] for i in range(6)]
    buckets = jnp.asarray(_t5_bucket_table())
    per_head = lambda a: jnp.broadcast_to(a.reshape(N_HEADS, 1), (N_HEADS, LANE))
    dtb_row, alog_row, d_exp = per_head(p["dt_bias"]), per_head(p["A_log"]), per_head(p["D_skip"])
    sinks = p["sinks"].reshape(N_HEADS)

    d_cut, gu_cut = WD_CUT, WGU_CUTS
    n_d, n_gu = w_d_mine.shape[0], w_gu_mine.shape[0]
    (qkv, z, xbc, dt_raw), (g_d_a,) = _in_proj_fwd(x, p["norm1"], scale1, shift1, w_in, tm,
                                                   ([(w_d_mine, 0, d_cut)], "two-level"))
    bias = _attn_bias(buckets, p["rel_bias"])
    (ya,), (g_gu_a,) = _attn_fwd(qkv, bias, sinks, ([(w_gu_mine, 0, gu_cut[0])], "two-level"))
    (ys, prev_states, pre_act), (g_gu_b, g_o) = _ssd_fwd(
        xbc, dt_raw, conv_w, p["conv_b"], dtb_row, alog_row, d_exp,
        ([(w_gu_mine, gu_cut[0], gu_cut[1] - gu_cut[0]), w_o_mine], "two-level"))
    w_o = g_o.reshape(D_MODEL, D_MODEL)
    x1, (g_gu_c, g_d_b) = _out_proj_fwd(
        x, ya, ys, z, p["attn_out_norm"], p["ssm_out_norm"], gate1, w_o, tm,
        ([(w_gu_mine, gu_cut[1], n_gu - gu_cut[1]), (w_d_mine, d_cut, n_d - d_cut)], "two-level"))
    dx1, h2, dgu, act, dmlp, acc2 = _mlp_loss(x1, tgt, p["norm2"], scale2, shift2, gate2, p["final_norm"],
                                              (g_gu_a, g_gu_b, g_gu_c), (g_d_a, g_d_b), tmm)
    g_w_gu = _wgrad(dgu, h2, 2 * D_FF // 4, tw, "wgrad_gate_up")
    g_w_d = _wgrad(act, dmlp, D_FF // 2, tw, "wgrad_down")
    gu_slots = g_w_gu.reshape(N_DEV, 2 * D_FF // N_DEV, D_MODEL)
    (dya, dys, dz, g_w_o, acc1), (r_gu_a,) = _out_proj_bwd(
        dx1, ya, ys, z, p["attn_out_norm"], p["ssm_out_norm"], gate1, w_o, tm, ([gu_slots], ("rows", 0, GGU_CUT)))
    (dq, dkv, dbias, dsk), (r_d, r_o) = _attn_bwd(
        qkv, ya, dya, bias, sinks,
        ([g_w_d.reshape(N_DEV, D_FF // N_DEV, D_MODEL), g_w_o.reshape(N_DEV, D_MODEL // N_DEV, D_MODEL)], True))
    drel, dsink = _attn_finish(dbias, dsk, buckets)
    (dxbc, ddt, dcw, dvec, dd), (r_gu_b,) = _ssd_bwd(
        xbc, pre_act, dt_raw, prev_states, dys, conv_w, dtb_row, alog_row, d_exp,
        ([gu_slots], ("rows", GGU_CUT, 2 * D_FF // N_DEV - GGU_CUT)))
    r_gu = (r_gu_a, r_gu_b)
    gx, h1, acc0, g_in_a = _in_proj_bwd(x, dx1, dq, dkv, dz, dxbc, ddt, p["norm1"], scale1, shift1, w_in, tm)
    half = D_MODEL // 2
    slots = lambda g: g[:IN_W].reshape(N_DEV, IN_W // N_DEV, half)
    g_in_b, exchanged = _wgrad((dq, dkv, dz, dxbc, ddt), h1, IN_PAD, tw, "wgrad_in_b",
                               [([slots(g_in_a)], True), ([acc0, acc1, acc2, dcw, dvec, dd, dsink, drel, c], False)],
                               g_cols=(half, 1))
    return gx, (exchanged[0], slots(g_in_b)), (r_o, r_gu, r_d), exchanged[1:]


def kernel(x, c, ada_w, ada_b, norm1, w_in, conv_w, conv_b, dt_bias, A_log, D_skip, sinks, attn_out_norm, ssm_out_norm, w_o, norm2, w_gate_up, w_down, rel_bias, final_norm, loss_target, m_ada_w, m_ada_b, m_norm1, m_w_in, m_conv_w, m_conv_b, m_dt_bias, m_A_log, m_D_skip, m_sinks, m_attn_out_norm, m_ssm_out_norm, m_w_o, m_norm2, m_w_gate_up, m_w_down, m_rel_bias, m_final_norm, v_ada_w, v_ada_b, v_norm1, v_w_in, v_conv_w, v_conv_b, v_dt_bias, v_A_log, v_D_skip, v_sinks, v_attn_out_norm, v_ssm_out_norm, v_w_o, v_norm2, v_w_gate_up, v_w_down, v_rel_bias, v_final_norm):
    two_d = lambda a: a if a.ndim == 2 else a.reshape(-1, a.shape[-1])
    small_params = dict(
        ada_b=(ada_b, m_ada_b, v_ada_b), norm1=(norm1, m_norm1, v_norm1), conv_w=(conv_w, m_conv_w, v_conv_w),
        conv_b=(conv_b, m_conv_b, v_conv_b), dt_bias=(dt_bias, m_dt_bias, v_dt_bias), A_log=(A_log, m_A_log, v_A_log),
        D_skip=(D_skip, m_D_skip, v_D_skip), sinks=(sinks, m_sinks, v_sinks),
        attn_out_norm=(attn_out_norm, m_attn_out_norm, v_attn_out_norm),
        ssm_out_norm=(ssm_out_norm, m_ssm_out_norm, v_ssm_out_norm), norm2=(norm2, m_norm2, v_norm2),
        rel_bias=(rel_bias, m_rel_bias, v_rel_bias), final_norm=(final_norm, m_final_norm, v_final_norm))
    small_params = {k: tuple(two_d(a) for a in v) for k, v in small_params.items()}
    S = x.shape[1]
    xs, tgt = x.reshape(S, D_MODEL), loss_target.reshape(S, D_MODEL)
    ada_w2 = ada_w[0]
    chunk = ada_w2.shape[1]
    t_in = [jnp.transpose(a[0]) for a in (w_in, m_w_in, v_w_in)]
    t_gu = [jnp.transpose(a[0]) for a in (w_gate_up, m_w_gate_up, v_w_gate_up)]

    mod, (g_in, g_cw) = _mod_and_gather(c, ada_w2, ada_b.reshape(N_DEV, chunk), [t_in[0].astype(WIRE_DTYPE), conv_w[0]])
    mod = mod.reshape(6, D_MODEL)
    w_in_full = jnp.pad(g_in.reshape(IN_W, D_MODEL), ((0, IN_PAD - IN_W), (0, 0)))
    conv_w_full = jnp.transpose(g_cw, (1, 0, 2)).reshape(4, XBC_W)

    p = {k: v[0] for k, v in small_params.items()}
    gx, (r_in_a, gw_in_b), (r_o, r_gu, r_d), gathered = _local_step(
        xs, tgt, c, mod, w_in_full, conv_w_full, w_o[0].astype(WIRE_DTYPE), t_gu[0].astype(WIRE_DTYPE),
        w_down[0].astype(WIRE_DTYPE), p)

    (u_gu, u_d, u_o), (r_in_b,) = _reduce_adamw_hosting(
        [r_gu, r_d, r_o], [tuple(t_gu), (w_down[0], m_w_down[0], v_w_down[0]), (w_o[0], m_w_o[0], v_w_o[0])],
        "adamw_big", ([gw_in_b], True))

    small, loss, c_all, dmod_all = _small_update(gathered, small_params)

    big = {
        "ada_w": _ada_w_update(c_all, dmod_all, ada_w2, m_ada_w[0], v_ada_w[0]),
        "w_in": [jnp.transpose(a) for a in _reduce_adamw_halves(r_in_a, r_in_b, *t_in, "adamw_w_in")],
        "w_o": u_o,
        "w_gate_up": [jnp.transpose(a) for a in u_gu],
        "w_down": u_d,
    }
    big.update(small)

    order = ['ada_w', 'ada_b', 'norm1', 'w_in', 'conv_w', 'conv_b', 'dt_bias', 'A_log', 'D_skip', 'sinks',
             'attn_out_norm', 'ssm_out_norm', 'w_o', 'norm2', 'w_gate_up', 'w_down', 'rel_bias', 'final_norm']
    shapes = dict(ada_w=ada_w.shape, ada_b=ada_b.shape, norm1=norm1.shape, w_in=w_in.shape, conv_w=conv_w.shape,
                  conv_b=conv_b.shape, dt_bias=dt_bias.shape, A_log=A_log.shape, D_skip=D_skip.shape,
                  sinks=sinks.shape, attn_out_norm=attn_out_norm.shape, ssm_out_norm=ssm_out_norm.shape,
                  w_o=w_o.shape, norm2=norm2.shape, w_gate_up=w_gate_up.shape, w_down=w_down.shape,
                  rel_bias=rel_bias.shape, final_norm=final_norm.shape)
    outs = [[], [], [], []]
    for name in order:
        for kind in range(4):
            outs[kind].append(big[name][kind].reshape(shapes[name]))
    return (loss.reshape(()), gx.reshape(x.shape), *outs[0], *outs[1], *outs[2], *outs[3])
```

```python
import numpy as np
import jax
import jax.numpy as jnp
from jax import lax
from jax.experimental import pallas as pl
from jax.experimental.pallas import tpu as pltpu

F32 = jnp.float32
MXU_DTYPE = jnp.bfloat16
WIRE_DTYPE = jnp.bfloat16
HI = lax.Precision.HIGHEST
MESH = pl.DeviceIdType.MESH
N_DEV = 8

D_MODEL = 1024
ATTN_W = 512
KV_W = 128
SSM_W = 512
XBC_W = 1024
N_HEADS = 8
D_STATE = 128
D_FF = 2816
IN_W = 2312
IN_PAD = 2432
BLK = 128
N_BUCKETS = 32
EPS = 1e-6
LANE = 128
HALF = 64

ADAM_LR, ADAM_B1, ADAM_B2, ADAM_EPS, ADAM_WD, ADAM_STEP = 0.001, 0.9, 0.999, 1e-08, 0.01, 10

VMEM_BIG = 56 * 1024 * 1024
WD_CUT = 288
WGU_CUTS = (240, 496)
GGU_CUT = 304


def _cparams(vmem=None):
    if vmem is None:
        return pltpu.CompilerParams()
    return pltpu.CompilerParams(vmem_limit_bytes=vmem)


def _mm(a, b):
    return jnp.dot(a.astype(MXU_DTYPE), b.astype(MXU_DTYPE), preferred_element_type=F32)


def _mm_nt(a, b):
    return lax.dot_general(a.astype(MXU_DTYPE), b.astype(MXU_DTYPE), (((1,), (1,)), ((), ())),
                           preferred_element_type=F32)


def _mm_tn(a, b):
    return lax.dot_general(a.astype(MXU_DTYPE), b.astype(MXU_DTYPE), (((0,), (0,)), ((), ())),
                           preferred_element_type=F32)


def _mm_hi(a, b):
    return jnp.dot(a, b, precision=HI, preferred_element_type=F32)


def _silu(x):
    return x * jax.nn.sigmoid(x)


def _softplus(x):
    return jnp.maximum(x, 0.0) + jnp.log1p(jnp.exp(-jnp.abs(x)))


def _rms(x, g, n):
    return x * lax.rsqrt(jnp.sum(x * x, axis=-1, keepdims=True) * (1.0 / n) + EPS) * g


def _modnorm(x, g, scale, shift):
    return _rms(x, g, x.shape[-1]) * (1.0 + scale) + shift


def _modnorm_parts(x):
    r = lax.rsqrt(jnp.sum(x * x, axis=-1, keepdims=True) * (1.0 / x.shape[-1]) + EPS)
    return r, x * r


def _modnorm_bwd(r, xhat, g, scale, dy):
    dyg = dy * (g * (1.0 + scale))
    c = jnp.sum(dyg * xhat, axis=-1, keepdims=True) * (1.0 / xhat.shape[-1])
    dx = r * (dyg - xhat * c)
    ct = jnp.sum(dy * xhat, axis=0, keepdims=True)
    return dx, ct * (1.0 + scale), ct * g, jnp.sum(dy, axis=0, keepdims=True)


def _lane_iota(shape):
    return lax.broadcasted_iota(jnp.int32, shape, len(shape) - 1)


def _split_pair(t):
    lane = _lane_iota(t.shape)
    lo = jnp.where(lane < HALF, t, 0.0)
    hi = pltpu.roll(jnp.where(lane >= HALF, t, 0.0), HALF, 1)
    return lo, hi


def _join_pair(lo, hi):
    lane = _lane_iota(lo.shape)
    return jnp.where(lane < HALF, lo, pltpu.roll(hi, HALF, 1))


def _split_heads(t, n_pairs):
    out = []
    for p in range(n_pairs):
        out.extend(_split_pair(t[:, p * LANE:(p + 1) * LANE]))
    return out


def _join_heads(hs):
    return jnp.concatenate([_join_pair(hs[2 * p], hs[2 * p + 1]) for p in range(len(hs) // 2)], axis=1)


def _t5_bucket_table():
    dist = np.arange(BLK)[:, None] + BLK - np.arange(2 * BLK)[None, :]
    n = np.maximum(dist, 0)
    max_exact = N_BUCKETS // 2
    large = max_exact + (np.log(np.maximum(n, 1) / max_exact) / np.log(128 / max_exact)
                         * (N_BUCKETS - max_exact)).astype(np.int32)
    large = np.minimum(large, N_BUCKETS - 1)
    return np.where(n < max_exact, n, large).astype(np.int32)


def _my_pos():
    return lax.axis_index("x"), lax.axis_index("y"), lax.axis_index("c")


def _peer(k):
    x, y, c = _my_pos()
    return (1 - x if k & 4 else x, 1 - y if k & 2 else y, 1 - c if k & 1 else c)


def _lin(pos):
    return 4 * pos[0] + 2 * pos[1] + pos[2]


def _xchg_copies(ins, outs, sems, scatter):
    local_sem, send_sem, recv_sem = sems
    me = _lin(_my_pos())

    def source(a, slot):
        if not scatter:
            return ins[a]
        if scatter is True:
            return ins[a].at[slot]
        return ins[a].at[slot, pl.ds(scatter[1], scatter[2])]

    local, remote = [], []
    for a in range(len(ins)):
        local.append(pltpu.make_async_copy(source(a, me), outs[a].at[me], local_sem.at[a]))
    for k in range(1, N_DEV):
        peer = _peer(k)
        for a in range(len(ins)):
            remote.append(pltpu.make_async_remote_copy(source(a, _lin(peer)), outs[a].at[me], send_sem.at[a, k - 1],
                                                       recv_sem.at[a, k - 1], device_id=peer, device_id_type=MESH))
    return local, remote


def _xchg_start(ins, outs, sems, scatter):
    local, remote = _xchg_copies(ins, outs, sems, scatter)
    for cp in local + remote:
        cp.start()


def _xchg_wait(ins, outs, sems, scatter):
    local, remote = _xchg_copies(ins, outs, sems, scatter)
    for cp in local:
        cp.wait()
    for cp in remote:
        cp.wait_send()
        cp.wait_recv()


def _xchg_shapes(arrs, scatter):
    n = len(arrs)
    if isinstance(scatter, tuple):
        out_shape = [jax.ShapeDtypeStruct((a.shape[0], scatter[2]) + a.shape[2:], a.dtype) for a in arrs]
    elif scatter:
        out_shape = [jax.ShapeDtypeStruct(a.shape, a.dtype) for a in arrs]
    else:
        out_shape = [jax.ShapeDtypeStruct((N_DEV,) + a.shape, a.dtype) for a in arrs]
    sems = [pltpu.SemaphoreType.DMA((n,)), pltpu.SemaphoreType.DMA((n, N_DEV - 1)),
            pltpu.SemaphoreType.DMA((n, N_DEV - 1))]
    return out_shape, sems


_CHIPS = (2, 4, 6)


def _g2_sems(n):
    dma = pltpu.SemaphoreType.DMA
    return [dma((n,)), dma((n, N_DEV)), dma((n, N_DEV)), dma((n, len(_CHIPS))), dma((n, len(_CHIPS)))]


class _TwoLevelGather:
    def __init__(self, ins, outs, sems, windows=None):
        self.ins, self.outs = ins, outs
        self.local_sem, self.send_sem, self.recv_sem, self.fsend_sem, self.frecv_sem = sems
        self.n = len(ins)
        self.windows = windows or [None] * self.n

    def _mine(self, a):
        w = self.windows[a]
        return self.ins[a] if w is None else self.ins[a].at[pl.ds(w[0], w[1])]

    def _direct(self, a, k):
        return pltpu.make_async_remote_copy(self._mine(a), self.outs[a].at[_lin(_my_pos())], self.send_sem.at[a, k],
                                            self.recv_sem.at[a, k], device_id=_peer(k), device_id_type=MESH)

    def _handed_on(self, a, j, origin):
        slot = self.outs[a].at[origin]
        return pltpu.make_async_remote_copy(slot, slot, self.fsend_sem.at[a, j], self.frecv_sem.at[a, j],
                                            device_id=_peer(1), device_id_type=MESH)

    def _local(self, a):
        return pltpu.make_async_copy(self._mine(a), self.outs[a].at[_lin(_my_pos())], self.local_sem.at[a])

    def start(self):
        for a in range(self.n):
            self._local(a).start()
        for k in (1,) + _CHIPS:
            for a in range(self.n):
                self._direct(a, k).start()

    def forward(self):
        for j, k in enumerate(_CHIPS):
            for a in range(self.n):
                self._direct(a, k).wait_recv()
                self._handed_on(a, j, _lin(_peer(k))).start()

    def finish(self):
        for a in range(self.n):
            self._direct(a, 1).wait_recv()
            for j, k in enumerate(_CHIPS):
                self._handed_on(a, j, _lin(_peer(k ^ 1))).wait_recv()
            self._local(a).wait()
            for k in (1,) + _CHIPS:
                self._direct(a, k).wait_send()
            for j, k in enumerate(_CHIPS):
                self._handed_on(a, j, _lin(_peer(k))).wait_send()


def _mod_and_gather(c, ada_w, ada_b8, arrs):
    n = len(arrs)
    chunk = ada_w.shape[1]
    out_shape = [jax.ShapeDtypeStruct((N_DEV, 1, chunk), F32)]
    out_shape += [jax.ShapeDtypeStruct((N_DEV,) + a.shape, a.dtype) for a in arrs]

    def modulation(c_ref, w_ref, b_ref, out_ref, cbuf, part, s1, r1, s2, r2):
        me = _lin(_my_pos())
        first = []
        for k in range(1, N_DEV):
            cp = pltpu.make_async_remote_copy(c_ref, cbuf.at[me], s1.at[k - 1], r1.at[k - 1],
                                              device_id=_peer(k), device_id_type=MESH)
            cp.start()
            first.append(cp)
        cbuf[me] = c_ref[...]
        for cp in first:
            cp.wait_send()
            cp.wait_recv()
        cond = _silu(jnp.concatenate([cbuf[i] for i in range(N_DEV)], axis=0))
        mod = _mm_hi(cond, w_ref[...]) + b_ref[pl.ds(me, 1), :]
        for j in range(N_DEV):
            part[j] = mod[j:j + 1, :]
        second = []
        for k in range(1, N_DEV):
            peer = _peer(k)
            cp = pltpu.make_async_remote_copy(part.at[_lin(peer)], out_ref.at[me], s2.at[k - 1], r2.at[k - 1],
                                              device_id=peer, device_id_type=MESH)
            cp.start()
            second.append(cp)
        out_ref[me] = part[me]
        for cp in second:
            cp.wait_send()
            cp.wait_recv()

    def body(*refs):
        c_ref, w_ref, b_ref = refs[:3]
        ins = refs[3:3 + n]
        mod_ref = refs[3 + n]
        outs = refs[4 + n:4 + 2 * n]
        cbuf, part, s1, r1, s2, r2 = refs[4 + 2 * n:10 + 2 * n]
        gather = _TwoLevelGather(ins, outs, refs[10 + 2 * n:])
        gather.start()
        modulation(c_ref, w_ref, b_ref, mod_ref, cbuf, part, s1, r1, s2, r2)
        gather.forward()
        gather.finish()

    hbm = pl.BlockSpec(memory_space=pltpu.HBM)
    vm = pl.BlockSpec(memory_space=pltpu.VMEM)
    dma = pltpu.SemaphoreType.DMA
    res = pl.pallas_call(
        body, name="mod_and_gather", out_shape=out_shape, in_specs=[vm, vm, vm] + [hbm] * n,
        out_specs=[vm] + [hbm] * n,
        scratch_shapes=[pltpu.VMEM((N_DEV, 1, D_MODEL), F32), pltpu.VMEM((N_DEV, 1, chunk), F32)]
        + [dma((N_DEV - 1,))] * 4 + _g2_sems(n),
    )(c, ada_w, ada_b8, *arrs)
    return res[0], res[1:]


def _hosted_call(body, name, grid, in_specs, out_specs, out_shape, scratch_shapes, args, xchg, cparams):
    xchgs = [xchg] if isinstance(xchg, tuple) else list(xchg)
    grid = (grid,) if isinstance(grid, int) else tuple(grid)
    n_in, n_out, n_scr = len(in_specs), len(out_specs), len(scratch_shapes)
    windows = [[(a[1], a[2]) if isinstance(a, tuple) else None for a in group] for group, _ in xchgs]
    xchgs = [([a[0] if isinstance(a, tuple) else a for a in group], mode) for group, mode in xchgs]
    arrs = [a for group, _ in xchgs for a in group]
    n = len(arrs)
    x_shape, x_sems, sem_counts = [], [], []
    for (group, mode), wins in zip(xchgs, windows):
        shapes, sems = _xchg_shapes(group, False if mode == "two-level" else mode)
        if mode == "two-level":
            sems = _g2_sems(len(group))
            shapes = [s if w is None else jax.ShapeDtypeStruct((N_DEV, w[1]) + a.shape[1:], a.dtype)
                      for s, w, a in zip(shapes, wins, group)]
        x_shape += shapes
        x_sems += sems
        sem_counts.append(len(sems))
    n_steps = int(np.prod(grid))

    def hosted(*refs):
        ins, refs = refs[:n_in], refs[n_in:]
        x_in, refs = refs[:n], refs[n:]
        outs, refs = refs[:n_out], refs[n_out:]
        x_out, refs = refs[:n], refs[n:]
        scr, sems = refs[:n_scr], refs[n_scr:]
        step = pl.program_id(0)
        for d in range(1, len(grid)):
            step = step * grid[d] + pl.program_id(d)
        parts, a0, s0 = [], 0, 0
        for (group, mode), ns, wins in zip(xchgs, sem_counts, windows):
            parts.append((x_in[a0:a0 + len(group)], x_out[a0:a0 + len(group)], sems[s0:s0 + ns], mode, wins))
            a0, s0 = a0 + len(group), s0 + ns

        @pl.when(step == 0)
        def _():
            for gi, go, gs, mode, wins in parts:
                if mode == "two-level":
                    _TwoLevelGather(gi, go, gs, wins).start()
                else:
                    _xchg_start(gi, go, gs, mode)

        if any(mode == "two-level" for _, mode in xchgs):
            @pl.when(step == (2 * n_steps) // 3)
            def _():
                for gi, go, gs, mode, wins in parts:
                    if mode == "two-level":
                        _TwoLevelGather(gi, go, gs, wins).forward()

        body(*ins, *outs, *scr)

        @pl.when(step == n_steps - 1)
        def _():
            for gi, go, gs, mode, wins in parts:
                if mode == "two-level":
                    _TwoLevelGather(gi, go, gs, wins).finish()
                else:
                    _xchg_wait(gi, go, gs, mode)

    hbm = pl.BlockSpec(memory_space=pltpu.HBM)
    res = pl.pallas_call(
        hosted, name=name, grid=grid, in_specs=list(in_specs) + [hbm] * n,
        out_specs=list(out_specs) + [hbm] * n, out_shape=list(out_shape) + x_shape,
        scratch_shapes=list(scratch_shapes) + x_sems, compiler_params=cparams,
    )(*args, *arrs)
    return res[:n_out], res[n_out:]


def _row(i):
    return (i, 0)


def _fixed(i):
    return (0, 0)


def _in_proj_fwd(x, norm1, scale1, shift1, w_in, tm, xchg):
    S = x.shape[0]

    def body(x_ref, n_ref, sc_ref, sh_ref, w_ref, qkv_ref, z_ref, xbc_ref, dt_ref):
        h = _modnorm(x_ref[...], n_ref[...], sc_ref[...], sh_ref[...])
        p = _mm_nt(h, w_ref[...])
        qkv_ref[...] = p[:, :768].astype(qkv_ref.dtype)
        z_ref[...] = p[:, 768:1280]
        xbc_ref[...] = p[:, 1280:2304]
        dt_ref[...] = p[:, 2304:IN_PAD]

    vec = pl.BlockSpec((1, D_MODEL), _fixed)
    return _hosted_call(
        body, "in_proj_fwd", S // tm,
        in_specs=[pl.BlockSpec((tm, D_MODEL), _row), vec, vec, vec, pl.BlockSpec((IN_PAD, D_MODEL), _fixed)],
        out_specs=[pl.BlockSpec((tm, 768), _row), pl.BlockSpec((tm, SSM_W), _row),
                   pl.BlockSpec((tm, XBC_W), _row), pl.BlockSpec((tm, LANE), _row)],
        out_shape=[jax.ShapeDtypeStruct((S, 768), MXU_DTYPE), jax.ShapeDtypeStruct((S, SSM_W), F32),
                   jax.ShapeDtypeStruct((S, XBC_W), F32), jax.ShapeDtypeStruct((S, LANE), F32)],
        scratch_shapes=[], args=(x, norm1, scale1, shift1, w_in), xchg=xchg, cparams=_cparams(VMEM_BIG),
    )


def _in_proj_bwd(x, dx1, dq, dkv, dz, dxbc, ddt, norm1, scale1, shift1, w_in, tm):
    S = x.shape[0]

    n_steps = S // tm
    half_cols = D_MODEL // 2

    def body(x_ref, dx1_ref, dq_ref, dkv_ref, dz_ref, dxbc_ref, ddt_ref, n_ref, sc_ref, sh_ref, w_ref,
             gx_ref, h_ref, acc_ref, gw_ref, gw_acc):
        i = pl.program_id(0)

        @pl.when(i == 0)
        def _():
            acc_ref[...] = jnp.zeros_like(acc_ref)
            gw_acc[...] = jnp.zeros_like(gw_acc)

        halves = [pl.ds(k * (tm // 2), tm // 2) for k in range(2)]
        dp = [jnp.concatenate([r[rows, :] for r in (dq_ref, dkv_ref, dz_ref, dxbc_ref, ddt_ref)], axis=1)
              for rows in halves]
        dh = [_mm(dp[k], w_ref[...]) for k in range(2)]
        parts = [_modnorm_parts(x_ref[rows, :]) for rows in halves]
        hb = [(parts[k][1] * n_ref[...] * (1.0 + sc_ref[...]) + sh_ref[...]).astype(h_ref.dtype) for k in range(2)]
        gw_acc[...] += _mm_tn(dp[0], hb[0][:, :half_cols]) + _mm_tn(dp[1], hb[1][:, :half_cols])
        bwd = [_modnorm_bwd(parts[k][0], parts[k][1], n_ref[...], sc_ref[...], dh[k]) for k in range(2)]
        for k, rows in enumerate(halves):
            gx_ref[rows, :] = dx1_ref[rows, :] + bwd[k][0]
            h_ref[rows, :] = hb[k]
        acc_ref[0:1, :] += bwd[0][1] + bwd[1][1]
        acc_ref[1:2, :] += bwd[0][2] + bwd[1][2]
        acc_ref[2:3, :] += bwd[0][3] + bwd[1][3]

        @pl.when(i == n_steps - 1)
        def _():
            gw_ref[...] = gw_acc[...].astype(gw_ref.dtype)

    vec = pl.BlockSpec((1, D_MODEL), _fixed)
    return pl.pallas_call(
        body, name="in_proj_bwd", grid=(n_steps,),
        in_specs=[pl.BlockSpec((tm, D_MODEL), _row), pl.BlockSpec((tm, D_MODEL), _row),
                  pl.BlockSpec((tm, ATTN_W), _row), pl.BlockSpec((tm, 2 * KV_W), _row),
                  pl.BlockSpec((tm, SSM_W), _row), pl.BlockSpec((tm, XBC_W), _row), pl.BlockSpec((tm, LANE), _row),
                  vec, vec, vec, pl.BlockSpec((IN_PAD, D_MODEL), _fixed)],
        out_specs=[pl.BlockSpec((tm, D_MODEL), _row), pl.BlockSpec((tm, D_MODEL), _row),
                   pl.BlockSpec((8, D_MODEL), _fixed), pl.BlockSpec((IN_PAD, half_cols), _fixed)],
        out_shape=[jax.ShapeDtypeStruct((S, D_MODEL), F32), jax.ShapeDtypeStruct((S, D_MODEL), MXU_DTYPE),
                   jax.ShapeDtypeStruct((8, D_MODEL), F32), jax.ShapeDtypeStruct((IN_PAD, half_cols), WIRE_DTYPE)],
        scratch_shapes=[pltpu.VMEM((IN_PAD, half_cols), F32)],
        compiler_params=_cparams(VMEM_BIG),
    )(x, dx1, dq, dkv, dz, dxbc, ddt, norm1, scale1, shift1, w_in)


def _out_stage(ya, ys0, ys1, z0, z1, an, sn0, sn1):
    half = SSM_W // 2
    a = _rms(ya, an, ATTN_W)
    g0 = _rms(ys0 * _silu(z0), sn0, half)
    g1 = _rms(ys1 * _silu(z1), sn1, half)
    return jnp.concatenate([a, g0, g1], axis=1)


def _out_stage_args(ya_ref, ys_ref, z_ref, an_ref, sn_ref):
    half = SSM_W // 2
    return (ya_ref[...], ys_ref[:, :half], ys_ref[:, half:], z_ref[:, :half], z_ref[:, half:],
            an_ref[...], sn_ref[:, :half], sn_ref[:, half:])


def _out_proj_fwd(x, ya, ys, z, an, sn, gate1, w_o, tm, xchg):
    S = x.shape[0]

    def body(x_ref, ya_ref, ys_ref, z_ref, an_ref, sn_ref, g_ref, w_ref, x1_ref):
        u = _out_stage(*_out_stage_args(ya_ref, ys_ref, z_ref, an_ref, sn_ref))
        x1_ref[...] = x_ref[...] + g_ref[...] * _mm(u, w_ref[...])

    half = pl.BlockSpec((tm, ATTN_W), _row)
    hvec = pl.BlockSpec((1, ATTN_W), _fixed)
    (x1,), x_out = _hosted_call(
        body, "out_proj_fwd", S // tm,
        in_specs=[pl.BlockSpec((tm, D_MODEL), _row), half, half, half, hvec, hvec,
                  pl.BlockSpec((1, D_MODEL), _fixed), pl.BlockSpec((D_MODEL, D_MODEL), _fixed)],
        out_specs=[pl.BlockSpec((tm, D_MODEL), _row)],
        out_shape=[jax.ShapeDtypeStruct((S, D_MODEL), F32)],
        scratch_shapes=[], args=(x, ya, ys, z, an, sn, gate1, w_o), xchg=xchg, cparams=_cparams(VMEM_BIG),
    )
    return x1, x_out


def _out_proj_bwd(dx1, ya, ys, z, an, sn, gate1, w_o, tm, xchg):
    S = dx1.shape[0]
    n_steps = S // tm

    def body(dx1_ref, ya_ref, ys_ref, z_ref, an_ref, sn_ref, g_ref, w_ref,
             dya_ref, dys_ref, dz_ref, gw_ref, acc_ref, gw_acc):
        i = pl.program_id(0)

        @pl.when(i == 0)
        def _():
            acc_ref[...] = jnp.zeros_like(acc_ref)
            gw_acc[...] = jnp.zeros_like(gw_acc)

        u, vjp = jax.vjp(_out_stage, *_out_stage_args(ya_ref, ys_ref, z_ref, an_ref, sn_ref))
        dx1 = dx1_ref[...]
        ub = u.astype(MXU_DTYPE)
        mix = _mm(ub, w_ref[...])
        dmix = dx1 * g_ref[...]
        dmixb = dmix.astype(MXU_DTYPE)
        du = _mm_nt(dmixb, w_ref[...])
        gw_acc[...] += _mm_tn(ub, dmixb)
        dya, dys0, dys1, dz0, dz1, dan, dsn0, dsn1 = vjp(du)
        dya_ref[...] = dya
        dys_ref[...] = jnp.concatenate([dys0, dys1], axis=1)
        dz_ref[...] = jnp.concatenate([dz0, dz1], axis=1).astype(dz_ref.dtype)
        acc_ref[0:1, :] += jnp.sum(dx1 * mix, axis=0, keepdims=True)
        acc_ref[1:2, :] += jnp.concatenate([dan, dsn0, dsn1], axis=1)

        @pl.when(i == n_steps - 1)
        def _():
            gw_ref[...] = gw_acc[...].astype(gw_ref.dtype)

    half = pl.BlockSpec((tm, ATTN_W), _row)
    hvec = pl.BlockSpec((1, ATTN_W), _fixed)
    full = pl.BlockSpec((tm, D_MODEL), _row)
    return _hosted_call(
        body, "out_proj_bwd", n_steps,
        in_specs=[full, half, half, half, hvec, hvec,
                  pl.BlockSpec((1, D_MODEL), _fixed), pl.BlockSpec((D_MODEL, D_MODEL), _fixed)],
        out_specs=[half, half, half, pl.BlockSpec((D_MODEL, D_MODEL), _fixed), pl.BlockSpec((8, D_MODEL), _fixed)],
        out_shape=[jax.ShapeDtypeStruct((S, ATTN_W), F32)] * 2 + [jax.ShapeDtypeStruct((S, ATTN_W), MXU_DTYPE),
                   jax.ShapeDtypeStruct((D_MODEL, D_MODEL), WIRE_DTYPE), jax.ShapeDtypeStruct((8, D_MODEL), F32)],
        scratch_shapes=[pltpu.VMEM((D_MODEL, D_MODEL), F32)],
        args=(dx1, ya, ys, z, an, sn, gate1, w_o), xchg=xchg, cparams=_cparams(VMEM_BIG),
    )


def _loss_rows(x2, fn, tgt):
    y = _rms(x2, fn, D_MODEL)
    per_row = jnp.sum(jnp.square(y - tgt), axis=1, keepdims=True)
    return jnp.sum(per_row, axis=0, keepdims=True) * (0.5 / D_MODEL)


def _mlp_loss(x1, tgt, norm2, scale2, shift2, gate2, fnorm, w_gu, w_d, tm):
    S = x1.shape[0]
    n_pieces = len(w_gu) + len(w_d)

    def body(*refs):
        x1_ref, t_ref, n_ref, sc_ref, sh_ref, g_ref, fn_ref = refs[:7]
        piece_refs = refs[7:7 + n_pieces]
        dx1_ref, h_ref, dgu_ref, act_ref, dmlp_ref, acc_ref, wgu, wd, wsem = refs[7 + n_pieces:]

        @pl.when(pl.program_id(0) == 0)
        def _():
            acc_ref[...] = jnp.zeros_like(acc_ref)
            copies = []
            for dst, pieces in ((wgu, piece_refs[:len(w_gu)]), (wd, piece_refs[len(w_gu):])):
                shard = sum(p.shape[1] for p in pieces)
                off = 0
                for p in pieces:
                    for j in range(N_DEV):
                        copies.append(pltpu.make_async_copy(p.at[j], dst.at[pl.ds(j * shard + off, p.shape[1])],
                                                            wsem.at[len(copies)]))
                    off += p.shape[1]
            for cp in copies:
                cp.start()
            for cp in copies:
                cp.wait()

        x1 = x1_ref[...]
        gate2 = g_ref[...]
        h, vjp_h = jax.vjp(_modnorm, x1, n_ref[...], sc_ref[...], sh_ref[...])
        hb = h.astype(MXU_DTYPE)
        gu = _mm_nt(hb, wgu[...])
        g, u = gu[:, :D_FF], gu[:, D_FF:]
        sg = jax.nn.sigmoid(g)
        silu_g = g * sg
        act = (silu_g * u).astype(MXU_DTYPE)
        mlp = _mm(act, wd[...])
        x2 = x1 + gate2 * mlp
        loss, vjp_loss = jax.vjp(_loss_rows, x2, fn_ref[...], t_ref[...])
        dx2, dfn, _ = vjp_loss(jnp.ones((1, 1), F32))
        dmlp = (dx2 * gate2).astype(MXU_DTYPE)
        dact = _mm_nt(dmlp, wd[...])
        dg = dact * u * (sg * (1.0 + g * (1.0 - sg)))
        du = dact * silu_g
        dgu = jnp.concatenate([dg, du], axis=1).astype(MXU_DTYPE)
        dh = _mm(dgu, wgu[...])
        dx, dn, dsc, dsh = vjp_h(dh)
        dx1_ref[...] = dx2 + dx
        h_ref[...] = hb
        dgu_ref[...] = dgu
        act_ref[...] = act
        dmlp_ref[...] = dmlp
        acc_ref[0:1, :] += dn
        acc_ref[1:2, :] += dsc
        acc_ref[2:3, :] += dsh
        acc_ref[3:4, :] += jnp.sum(dx2 * mlp, axis=0, keepdims=True)
        acc_ref[4:5, :] += dfn
        acc_ref[5:6, :] += jnp.broadcast_to(loss, (1, D_MODEL))

    full = pl.BlockSpec((tm, D_MODEL), _row)
    vec = pl.BlockSpec((1, D_MODEL), _fixed)
    anyspec = pl.BlockSpec(memory_space=pl.ANY)
    return pl.pallas_call(
        body, name="mlp_loss", grid=(S // tm,),
        in_specs=[full, full, vec, vec, vec, vec, vec] + [anyspec] * n_pieces,
        out_specs=[full, full, pl.BlockSpec((tm, 2 * D_FF), _row), pl.BlockSpec((tm, D_FF), _row), full,
                   pl.BlockSpec((8, D_MODEL), _fixed)],
        out_shape=[jax.ShapeDtypeStruct((S, D_MODEL), F32), jax.ShapeDtypeStruct((S, D_MODEL), MXU_DTYPE),
                   jax.ShapeDtypeStruct((S, 2 * D_FF), MXU_DTYPE), jax.ShapeDtypeStruct((S, D_FF), MXU_DTYPE),
                   jax.ShapeDtypeStruct((S, D_MODEL), MXU_DTYPE), jax.ShapeDtypeStruct((8, D_MODEL), F32)],
        scratch_shapes=[pltpu.VMEM((2 * D_FF, D_MODEL), MXU_DTYPE), pltpu.VMEM((D_FF, D_MODEL), MXU_DTYPE),
                        pltpu.SemaphoreType.DMA((N_DEV * n_pieces,))],
        compiler_params=_cparams(VMEM_BIG),
    )(x1, tgt, norm2, scale2, shift2, gate2, fnorm, *w_gu, *w_d)


def _wgrad(a, g, tk, ts, name, xchg=None, g_cols=None):
    pieces = list(a) if isinstance(a, (list, tuple)) else [a]
    S = pieces[0].shape[0]
    K = sum(p.shape[1] for p in pieces)
    assert len(pieces) == 1 or tk == K
    N, col = (g.shape[1], 0) if g_cols is None else g_cols
    ns = S // ts
    n_a = len(pieces)

    def body(*refs):
        a_refs, (g_ref, o_ref, acc_ref) = refs[:n_a], refs[n_a:]
        s = pl.program_id(1)

        @pl.when(s == 0)
        def _():
            acc_ref[...] = jnp.zeros_like(acc_ref)

        a_blk = a_refs[0][...] if n_a == 1 else jnp.concatenate([r[...] for r in a_refs], axis=1)
        acc_ref[...] += _mm_tn(a_blk, g_ref[...])

        @pl.when(s == ns - 1)
        def _():
            o_ref[...] = acc_ref[...].astype(o_ref.dtype)

    if n_a == 1:
        in_specs = [pl.BlockSpec((ts, tk), lambda j, s: (s, j))]
    else:
        in_specs = [pl.BlockSpec((ts, p.shape[1]), lambda j, s: (s, 0)) for p in pieces]
    in_specs.append(pl.BlockSpec((ts, N), lambda j, s: (s, col)))
    out_spec = pl.BlockSpec((tk, N), lambda j, s: (j, 0))
    out_shape = jax.ShapeDtypeStruct((K, N), WIRE_DTYPE)
    scratch = [pltpu.VMEM((tk, N), F32)]
    args = (*pieces, g)
    if xchg is None:
        return pl.pallas_call(body, name=name, grid=(K // tk, ns), in_specs=in_specs, out_specs=out_spec,
                              out_shape=out_shape, scratch_shapes=scratch, compiler_params=_cparams(VMEM_BIG))(*args)
    (out,), x_out = _hosted_call(body, name, (K // tk, ns), in_specs, [out_spec], [out_shape], scratch, args, xchg,
                                 _cparams(VMEM_BIG))
    return out, x_out


SSD_CHUNKS_PER_STEP = 4
SSD_BWD_CHUNKS_PER_STEP = 4
ATTN_BLOCKS_PER_STEP = 4
MASKED = -1e30
QK_SCALE = HALF ** -0.5


def _attn_bias(buckets, rel_bias):
    def body(bk_ref, relb_ref, out_ref):
        bk = bk_ref[...]
        i = lax.broadcasted_iota(jnp.int32, (BLK, 2 * BLK), 0)
        j = lax.broadcasted_iota(jnp.int32, (BLK, 2 * BLK), 1)
        window = (j > i) & (j <= i + BLK)
        for h in range(N_HEADS):
            acc = jnp.zeros((BLK, 2 * BLK), F32)
            for b in range(N_BUCKETS):
                acc = jnp.where(bk == b, relb_ref[b, h], acc)
            out_ref[0, h] = jnp.where(window, acc, MASKED)
            out_ref[1, h] = jnp.where(window & (j >= BLK), acc, MASKED)

    return pl.pallas_call(
        body, name="attn_bias", out_shape=jax.ShapeDtypeStruct((2, N_HEADS, BLK, 2 * BLK), F32),
        in_specs=[pl.BlockSpec(memory_space=pltpu.VMEM), pl.BlockSpec(memory_space=pltpu.SMEM)],
    )(buckets, rel_bias)


def _attn_fwd(qkv, bias, sinks, xchg):
    S = qkv.shape[0]
    nb = S // BLK

    nq = ATTN_BLOCKS_PER_STEP if nb % ATTN_BLOCKS_PER_STEP == 0 else 1
    rows = nq * BLK

    def body(q_ref, kvp_ref, kvc_ref, bias_ref, sinks_ref, y_ref):
        i = pl.program_id(0)
        q = q_ref[...].astype(F32) * QK_SCALE
        kv = jnp.concatenate([kvp_ref[...], kvc_ref[...]], axis=0).astype(F32)
        k_lo, k_hi = _split_pair(kv[:, :LANE])
        v_lo, v_hi = _split_pair(kv[:, LANE:])
        bands = [[t[b * BLK:(b + 2) * BLK].astype(MXU_DTYPE) for t in (k_lo, k_hi, v_lo, v_hi)] for b in range(nq)]
        q_heads = [_split_heads(q[b * BLK:(b + 1) * BLK], 4) for b in range(nq)]
        first = [jnp.where(i == 0, 1, 0) if b == 0 else 0 for b in range(nq)]
        items = [(b, h) for b in range(nq) for h in range(N_HEADS)]
        s = [_mm_nt(q_heads[b][h].astype(MXU_DTYPE), bands[b][h // 4]) + bias_ref[first[b], h] for b, h in items]
        m = [jnp.maximum(jnp.max(s[n], axis=-1, keepdims=True), sinks_ref[h]) for n, (b, h) in enumerate(items)]
        p = [jnp.exp(s[n] - m[n]) for n in range(len(items))]
        rinv = [1.0 / (jnp.sum(p[n], axis=-1, keepdims=True) + jnp.exp(sinks_ref[h] - m[n]))
                for n, (b, h) in enumerate(items)]
        out = [_mm(p[n], bands[b][2 + h // 4]) * rinv[n] for n, (b, h) in enumerate(items)]
        y_ref[...] = jnp.concatenate([_join_heads(out[b * N_HEADS:(b + 1) * N_HEADS]) for b in range(nq)], axis=0)

    smem = pl.BlockSpec(memory_space=pltpu.SMEM)
    return _hosted_call(
        body, "attn_fwd", nb // nq,
        in_specs=[pl.BlockSpec((rows, ATTN_W), _row),
                  pl.BlockSpec((BLK, 2 * KV_W), lambda i: (jnp.maximum(i * nq - 1, 0), 2)),
                  pl.BlockSpec((rows, 2 * KV_W), lambda i: (i, 2)),
                  pl.BlockSpec((2, N_HEADS, BLK, 2 * BLK), lambda i: (0, 0, 0, 0)), smem],
        out_specs=[pl.BlockSpec((rows, ATTN_W), _row)],
        out_shape=[jax.ShapeDtypeStruct((S, ATTN_W), F32)],
        scratch_shapes=[],
        args=(qkv, qkv, qkv, bias, sinks), xchg=xchg, cparams=_cparams(),
    )


def _attn_bwd(qkv, y, dy, bias, sinks, xchg):
    S = qkv.shape[0]
    nb = S // BLK
    nq = ATTN_BLOCKS_PER_STEP if nb % ATTN_BLOCKS_PER_STEP == 0 else 1
    rows, n_steps = nq * BLK, nb // nq

    def body(q_ref, kvp_ref, kvc_ref, y_ref, dy_ref, bias_ref, sinks_ref, dq_ref, dkv_ref, dbias_ref, dsk_ref, carry_ref):
        i = pl.program_id(0)

        @pl.when(i == 0)
        def _():
            dbias_ref[...] = jnp.zeros_like(dbias_ref)
            dsk_ref[...] = jnp.zeros_like(dsk_ref)
            carry_ref[...] = jnp.zeros_like(carry_ref)

        q = q_ref[...].astype(F32) * QK_SCALE
        kv = jnp.concatenate([kvp_ref[...], kvc_ref[...]], axis=0).astype(F32)
        k_lo, k_hi = _split_pair(kv[:, :LANE])
        v_lo, v_hi = _split_pair(kv[:, LANE:])
        bands = [[t[b * BLK:(b + 2) * BLK].astype(MXU_DTYPE) for t in (k_lo, k_hi, v_lo, v_hi)] for b in range(nq)]
        rows_of = lambda ref, b: ref[b * BLK:(b + 1) * BLK, :]
        first = [jnp.where(i == n_steps - 1, 1, 0) if b == 0 else 0 for b in range(nq)]
        items = [(b, h) for b in range(nq) for h in range(N_HEADS)]
        at = lambda b, h: b * N_HEADS + h
        q_heads = [hd for b in range(nq) for hd in _split_heads(q[b * BLK:(b + 1) * BLK], 4)]
        y_heads = [hd for b in range(nq) for hd in _split_heads(rows_of(y_ref, b), 4)]
        dy_heads = [hd for b in range(nq) for hd in _split_heads(rows_of(dy_ref, b), 4)]
        qs = [q_heads[n].astype(MXU_DTYPE) for n in range(len(items))]
        s = [_mm_nt(qs[at(b, h)], bands[b][h // 4]) + bias_ref[first[b], h] for b, h in items]
        m = [jnp.maximum(jnp.max(s[at(b, h)], axis=-1, keepdims=True), sinks_ref[h]) for b, h in items]
        p = [jnp.exp(s[n] - m[n]) for n in range(len(items))]
        esink = [jnp.exp(sinks_ref[h] - m[at(b, h)]) for b, h in items]
        rinv = [1.0 / (jnp.sum(p[n], axis=-1, keepdims=True) + esink[n]) for n in range(len(items))]
        t = [dy_heads[n] * rinv[n] for n in range(len(items))]
        delta = [jnp.sum(t[n] * y_heads[n], axis=-1, keepdims=True) for n in range(len(items))]
        tb = [t[n].astype(MXU_DTYPE) for n in range(len(items))]
        dp = [_mm_nt(tb[at(b, h)], bands[b][2 + h // 4]) for b, h in items]
        ds = [p[n] * (dp[n] - delta[n]) for n in range(len(items))]
        for h in range(N_HEADS):
            ds_h, dsk_h = ds[at(0, h)], esink[at(0, h)] * delta[at(0, h)]
            for b in range(1, nq):
                ds_h = ds_h + ds[at(b, h)]
                dsk_h = dsk_h + esink[at(b, h)] * delta[at(b, h)]
            dbias_ref[h] += ds_h
            dsk_ref[h] -= dsk_h
        dsb = [ds[n].astype(MXU_DTYPE) for n in range(len(items))]
        pb = [p[n].astype(MXU_DTYPE) for n in range(len(items))]
        dq_heads = [_mm(dsb[at(b, h)], bands[b][h // 4]) * QK_SCALE for b, h in items]
        grp = lambda lst, b, g: jnp.concatenate(lst[at(b, 4 * g):at(b, 4 * g) + 4], axis=0)
        dk_pads = [[_mm_tn(grp(dsb, b, g), grp(qs, b, g)) for g in range(2)] for b in range(nq)]
        dv_pads = [[_mm_tn(grp(pb, b, g), grp(tb, b, g)) for g in range(2)] for b in range(nq)]
        dq_ref[...] = jnp.concatenate([_join_heads(dq_heads[b * N_HEADS:(b + 1) * N_HEADS]) for b in range(nq)],
                                      axis=0).astype(dq_ref.dtype)
        part = lambda b, lo: jnp.concatenate(
            [_join_pair(d[b][0][lo:lo + BLK], d[b][1][lo:lo + BLK]) for d in (dk_pads, dv_pads)], axis=1)
        dkv = [part(b, BLK) + (part(b + 1, 0) if b + 1 < nq else carry_ref[...]) for b in range(nq)]
        dkv_ref[...] = jnp.concatenate(dkv, axis=0).astype(dkv_ref.dtype)
        carry_ref[...] = part(0, 0)

    smem = pl.BlockSpec(memory_space=pltpu.SMEM)
    rev = lambda i: (n_steps - 1 - i, 0)
    return _hosted_call(
        body, "attn_bwd", n_steps,
        in_specs=[pl.BlockSpec((rows, ATTN_W), rev),
                  pl.BlockSpec((BLK, 2 * KV_W), lambda i: (jnp.maximum((n_steps - 1 - i) * nq - 1, 0), 2)),
                  pl.BlockSpec((rows, 2 * KV_W), lambda i: (n_steps - 1 - i, 2)),
                  pl.BlockSpec((rows, ATTN_W), rev), pl.BlockSpec((rows, ATTN_W), rev),
                  pl.BlockSpec((2, N_HEADS, BLK, 2 * BLK), lambda i: (0, 0, 0, 0)), smem],
        out_specs=[pl.BlockSpec((rows, ATTN_W), rev), pl.BlockSpec((rows, 2 * KV_W), rev),
                   pl.BlockSpec((N_HEADS, BLK, 2 * BLK), lambda i: (0, 0, 0)),
                   pl.BlockSpec((N_HEADS, BLK, 1), lambda i: (0, 0, 0))],
        out_shape=[jax.ShapeDtypeStruct((S, ATTN_W), MXU_DTYPE), jax.ShapeDtypeStruct((S, 2 * KV_W), MXU_DTYPE),
                   jax.ShapeDtypeStruct((N_HEADS, BLK, 2 * BLK), F32), jax.ShapeDtypeStruct((N_HEADS, BLK, 1), F32)],
        scratch_shapes=[pltpu.VMEM((BLK, 2 * KV_W), F32)],
        args=(qkv, qkv, qkv, y, dy, bias, sinks), xchg=xchg, cparams=_cparams(),
    )


def _attn_finish(dbias, dsk, buckets):
    def body(db_ref, dsk_ref, bk_ref, drel_ref, dsink_ref):
        bk = bk_ref[...]
        r = lax.broadcasted_iota(jnp.int32, (N_BUCKETS, LANE), 0)
        l = lax.broadcasted_iota(jnp.int32, (N_BUCKETS, LANE), 1)
        row = lax.broadcasted_iota(jnp.int32, (N_HEADS, LANE), 0)
        res = jnp.zeros((N_BUCKETS, LANE), F32)
        dsink = jnp.zeros((N_HEADS, LANE), F32)
        for h in range(N_HEADS):
            db = db_ref[h]
            for b in range(N_BUCKETS):
                v = jnp.sum(jnp.sum(jnp.where(bk == b, db, 0.0), axis=1, keepdims=True), axis=0, keepdims=True)
                res = res + jnp.where((r == b) & (l == h), v, 0.0)
            dsink = dsink + jnp.where(row == h, jnp.sum(dsk_ref[h], axis=0, keepdims=True), 0.0)
        drel_ref[...] = res
        dsink_ref[...] = dsink

    return pl.pallas_call(body, name="attn_finish",
                          out_shape=[jax.ShapeDtypeStruct((N_BUCKETS, LANE), F32),
                                     jax.ShapeDtypeStruct((N_HEADS, LANE), F32)])(dbias, dsk, buckets)


def _ssd_consts():
    r = lax.broadcasted_iota(jnp.int32, (BLK, BLK), 0)
    c = lax.broadcasted_iota(jnp.int32, (BLK, BLK), 1)
    causal = c <= r
    upper = (r <= c).astype(F32)
    last = r == BLK - 1
    head = lax.broadcasted_iota(jnp.int32, (N_HEADS, BLK), 0)
    return causal, upper, last, head


def _ssd_chunks(xs, bg, cg, dt_raw_t, prev0, dtb, alog, d_rows, consts):
    causal, upper, last, head = consts
    nq = len(xs)
    items = [(c, h) for c in range(nq) for h in range(N_HEADS)]
    at = lambda c, h: c * N_HEADS + h
    a_neg = -jnp.exp(alog)
    dt_t = [_softplus(dt_raw_t[c] + dtb) for c in range(nq)]
    acs_t = [_mm_hi(dt_t[c] * a_neg, upper) for c in range(nq)]
    cb = [[_mm_nt(cg[c][g], bg[c][g]) for g in range(2)] for c in range(nq)]
    pick = lambda t, h: jnp.sum(jnp.where(head == h, t, 0.0), axis=0, keepdims=True)
    dt_row = [pick(dt_t[c], h) for c, h in items]
    a_row = [pick(acs_t[c], h) for c, h in items]
    a_rb = [jnp.broadcast_to(a_row[n], (BLK, BLK)) for n in range(len(items))]
    a_b = [a_rb[n].T for n in range(len(items))]
    a_last = [jnp.sum(jnp.where(last, a_b[n], 0.0), axis=0, keepdims=True) for n in range(len(items))]
    w = [cb[c][h // 4] * jnp.exp(jnp.where(causal, a_b[at(c, h)] - a_rb[at(c, h)], -1e30)) * dt_row[at(c, h)]
         for c, h in items]
    f_b = [jnp.broadcast_to(dt_row[n] * jnp.exp(a_last[n] - a_row[n]), (BLK, BLK)).T for n in range(len(items))]
    y_in = [_mm(w[at(c, h)], xs[c][h]) for c, h in items]
    st = [_mm_tn(bg[c][h // 4], xs[c][h] * f_b[at(c, h)]) for c, h in items]
    e_b = [jnp.exp(a_b[n]) for n in range(len(items))]
    states = [list(prev0)]
    for c in range(nq):
        states.append([states[c][h] * jnp.exp(a_last[at(c, h)]) + st[at(c, h)] for h in range(N_HEADS)])
    y_off = [_mm(cg[c][h // 4], states[c][h]) * e_b[at(c, h)] for c, h in items]
    ys = [[y_in[at(c, h)] + y_off[at(c, h)] + d_rows[h] * xs[c][h] for h in range(N_HEADS)] for c in range(nq)]
    return ys, states


def _ssd_chunks_bwd(xs, bg, cg, dt_raw_t, prev, dtb, alog, d_rows, dys, dh_last, consts):
    causal, upper, last, head = consts
    nq = len(xs)
    items = [(c, h) for c in range(nq) for h in range(N_HEADS)]
    ni = len(items)
    at = lambda c, h: c * N_HEADS + h
    groups = [(c, g) for c in range(nq) for g in range(2)]
    lane = _lane_iota((BLK, BLK))
    lane_row = _lane_iota((1, BLK))
    a_neg = -jnp.exp(alog)
    pre_dt = [dt_raw_t[c] + dtb for c in range(nq)]
    dt_t = [_softplus(pre_dt[c]) for c in range(nq)]
    acs_t = [_mm_hi(dt_t[c] * a_neg, upper) for c in range(nq)]
    pick = lambda t, h: jnp.sum(jnp.where(head == h, t, 0.0), axis=0, keepdims=True)
    full_sum = lambda t: jnp.sum(jnp.sum(t, axis=1, keepdims=True), axis=0, keepdims=True)
    dt_row = [pick(dt_t[c], h) for c, h in items]
    a_row = [pick(acs_t[c], h) for c, h in items]
    a_rb = [jnp.broadcast_to(a_row[n], (BLK, BLK)) for n in range(ni)]
    a_b = [a_rb[n].T for n in range(ni)]
    a_last = [jnp.sum(jnp.where(last, a_b[n], 0.0), axis=0, keepdims=True) for n in range(ni)]
    lm = [jnp.exp(jnp.where(causal, a_b[n] - a_rb[n], -1e30)) for n in range(ni)]
    cgb = [[cg[c][g].astype(MXU_DTYPE) for g in range(2)] for c in range(nq)]
    bgb = [[bg[c][g].astype(MXU_DTYPE) for g in range(2)] for c in range(nq)]
    cb = [[_mm_nt(cgb[c][g], bgb[c][g]) for g in range(2)] for c in range(nq)]
    u = [cb[c][h // 4] * lm[at(c, h)] for c, h in items]
    w = [(u[n] * dt_row[n]).astype(MXU_DTYPE) for n in range(ni)]
    e_row = [jnp.exp(a_last[n] - a_row[n]) for n in range(ni)]
    f_row = [dt_row[n] * e_row[n] for n in range(ni)]
    f_b = [jnp.broadcast_to(f_row[n], (BLK, BLK)).T for n in range(ni)]
    e_b = [jnp.exp(a_b[n]) for n in range(ni)]
    el = [jnp.exp(a_last[n]) for n in range(ni)]
    xb = [xs[c][h].astype(MXU_DTYPE) for c, h in items]
    dyb = [dys[c][h].astype(MXU_DTYPE) for c, h in items]
    prevb = [prev[c][h].astype(MXU_DTYPE) for c, h in items]
    gmat = [_mm(cgb[c][h // 4], prevb[at(c, h)]) for c, h in items]
    dw = [_mm_nt(dyb[n], xb[n]) for n in range(ni)]
    dg = [dys[c][h] * e_b[at(c, h)] for c, h in items]
    dgb = [dg[n].astype(MXU_DTYPE) for n in range(ni)]
    from_y = [_mm_tn(cgb[c][h // 4], dgb[at(c, h)]) for c, h in items]
    dhs = [None] * ni
    dprev = [None] * ni
    for c in reversed(range(nq)):
        for h in range(N_HEADS):
            dhs[at(c, h)] = dh_last[h] if c == nq - 1 else dprev[at(c + 1, h)]
            dprev[at(c, h)] = from_y[at(c, h)] + dhs[at(c, h)] * el[at(c, h)]
    dstb = [dhs[n].astype(MXU_DTYPE) for n in range(ni)]
    dxf = [_mm(bgb[c][h // 4], dstb[at(c, h)]) for c, h in items]
    xfb = [(xs[c][h] * f_b[at(c, h)]).astype(MXU_DTYPE) for c, h in items]
    dxs = [_mm_tn(w[at(c, h)], dyb[at(c, h)]) + d_rows[h] * dys[c][h] + f_b[at(c, h)] * dxf[at(c, h)]
           for c, h in items]
    dd_item = [jnp.sum(dys[c][h] * xs[c][h], axis=0, keepdims=True) for c, h in items]
    dcg_h = [_mm_nt(dgb[n], prevb[n]) for n in range(ni)]
    dbg_h = [_mm_nt(xfb[n], dstb[n]) for n in range(ni)]
    zt = [dw[n] * u[n] for n in range(ni)]
    dseg = [zt[n] * dt_row[n] for n in range(ni)]
    dcb_h = [dw[n] * lm[n] * dt_row[n] for n in range(ni)]
    four = lambda lst, c, g: lst[at(c, 4 * g)] + lst[at(c, 4 * g + 1)] + lst[at(c, 4 * g + 2)] + lst[at(c, 4 * g + 3)]
    dcb = {(c, g): four(dcb_h, c, g).astype(MXU_DTYPE) for c, g in groups}
    dcg = [[four(dcg_h, c, g) + _mm(dcb[c, g], bgb[c][g]) for g in range(2)] for c in range(nq)]
    dbg = [[four(dbg_h, c, g) + _mm_tn(dcb[c, g], cgb[c][g]) for g in range(2)] for c in range(nq)]
    r1 = [jnp.sum(dg[n] * gmat[n] + dseg[n], axis=1, keepdims=True) for n in range(ni)]
    r2 = [jnp.sum(dxf[at(c, h)] * xs[c][h], axis=1, keepdims=True) for c, h in items]
    tt = [jnp.where(lane < HALF, jnp.broadcast_to(r1[n], (BLK, BLK)), jnp.broadcast_to(r2[n], (BLK, BLK))).T
          for n in range(ni)]
    r1_row = [tt[n][0:1, :] for n in range(ni)]
    r2_row = [tt[n][HALF:HALF + 1, :] for n in range(ni)]
    d_el = [full_sum(dhs[at(c, h)] * prev[c][h]) for c, h in items]
    da_last = [jnp.sum(r2_row[n] * f_row[n], axis=1, keepdims=True) + el[n] * d_el[n] for n in range(ni)]
    da_row = [r1_row[n] - jnp.sum(dseg[n], axis=0, keepdims=True) - r2_row[n] * f_row[n]
              + jnp.where(lane_row == BLK - 1, da_last[n], 0.0) for n in range(ni)]
    ddt_row = [jnp.sum(zt[n], axis=0, keepdims=True) + r2_row[n] * e_row[n] for n in range(ni)]
    draw, dalog = [], jnp.zeros((N_HEADS, BLK), F32)
    for c in range(nq):
        da_t = jnp.zeros((N_HEADS, BLK), F32)
        ddt_t = jnp.zeros((N_HEADS, BLK), F32)
        for h in range(N_HEADS):
            da_t = jnp.where(head == h, da_row[at(c, h)], da_t)
            ddt_t = jnp.where(head == h, ddt_row[at(c, h)], ddt_t)
        d_dta = _mm_hi(da_t, causal.astype(F32))
        dalog = dalog + d_dta * dt_t[c] * a_neg
        draw.append((ddt_t + d_dta * a_neg) * jax.nn.sigmoid(pre_dt[c]))
    ddtb = draw[0]
    for c in range(1, nq):
        ddtb = ddtb + draw[c]
    dd_rows = []
    for h in range(N_HEADS):
        t = dd_item[at(0, h)]
        for c in range(1, nq):
            t = t + dd_item[at(c, h)]
        dd_rows.append(t)
    return ([dxs[c * N_HEADS:(c + 1) * N_HEADS] for c in range(nq)], dbg, dcg, draw,
            [dprev[at(0, h)] for h in range(N_HEADS)], ddtb, dalog, dd_rows)


def _dt_rows(dt_blk):
    return dt_blk.T[:N_HEADS]


def _conv_pre(halo, blk, cw_ref, cb_ref):
    ext = jnp.concatenate([halo, blk], axis=0)
    taps = [pltpu.roll(ext, 3 - k, 0)[8:] for k in range(3)] + [blk]
    pre = cb_ref[...] + cw_ref[0:1, :] * taps[0]
    for k in range(1, 4):
        pre = pre + cw_ref[k:k + 1, :] * taps[k]
    return pre


def _ssd_split(pre):
    heads = _split_heads(pre[:, :SSM_W], 4)
    pb = [pre[:, SSM_W + g * D_STATE:SSM_W + (g + 1) * D_STATE] for g in range(2)]
    pc = [pre[:, SSM_W + 2 * D_STATE + g * D_STATE:SSM_W + 2 * D_STATE + (g + 1) * D_STATE] for g in range(2)]
    return heads, pb, pc


def _ssd_fwd(xbc, dt_raw, conv_w, conv_b, dtb_row, alog_row, d_exp, xchg):
    S = xbc.shape[0]
    nc = S // BLK
    nq = SSD_CHUNKS_PER_STEP if nc % SSD_CHUNKS_PER_STEP == 0 else 1
    rows = nq * BLK

    def body(xbc_ref, halo_ref, dt_ref, cw_ref, cb_ref, dtb_ref, alog_ref, d_ref, y_ref, prev_ref, pre_ref, state_ref):
        i = pl.program_id(0)

        @pl.when(i == 0)
        def _():
            state_ref[...] = jnp.zeros_like(state_ref)

        halo = halo_ref[...] * jnp.where(i > 0, 1.0, 0.0)
        pre = _conv_pre(halo, xbc_ref[...], cw_ref, cb_ref)
        pre_ref[...] = pre
        xc = _silu(pre)
        split = [_ssd_split(xc[c * BLK:(c + 1) * BLK]) for c in range(nq)]
        dt_t = [_dt_rows(dt_ref[c * BLK:(c + 1) * BLK, :]) for c in range(nq)]
        prev0 = [state_ref[h] for h in range(N_HEADS)]
        d_rows = [d_ref[h:h + 1, :] for h in range(N_HEADS)]
        ys, states = _ssd_chunks([s[0] for s in split], [s[1] for s in split], [s[2] for s in split], dt_t, prev0,
                                 dtb_ref[...], alog_ref[...], d_rows, _ssd_consts())
        for h in range(N_HEADS):
            for c in range(nq):
                prev_ref[c, h] = states[c][h]
            state_ref[h] = states[nq][h]
        y_ref[...] = jnp.concatenate([_join_heads(ys[c]) for c in range(nq)], axis=0)

    vec = pl.BlockSpec((N_HEADS, LANE), _fixed)
    return _hosted_call(
        body, "ssd_fwd", nc // nq,
        in_specs=[pl.BlockSpec((rows, XBC_W), _row),
                  pl.BlockSpec((8, XBC_W), lambda i: (jnp.maximum(i * (rows // 8) - 1, 0), 0)),
                  pl.BlockSpec((rows, LANE), _row),
                  pl.BlockSpec((4, XBC_W), _fixed), pl.BlockSpec((1, XBC_W), _fixed), vec, vec,
                  pl.BlockSpec((N_HEADS, LANE), _fixed)],
        out_specs=[pl.BlockSpec((rows, SSM_W), _row),
                   pl.BlockSpec((nq, N_HEADS, D_STATE, LANE), lambda i: (i, 0, 0, 0)),
                   pl.BlockSpec((rows, XBC_W), _row)],
        out_shape=[jax.ShapeDtypeStruct((S, SSM_W), F32), jax.ShapeDtypeStruct((nc, N_HEADS, D_STATE, LANE), F32),
                   jax.ShapeDtypeStruct((S, XBC_W), F32)],
        scratch_shapes=[pltpu.VMEM((N_HEADS, D_STATE, LANE), F32)],
        args=(xbc, xbc, dt_raw, conv_w, conv_b, dtb_row, alog_row, d_exp), xchg=xchg, cparams=_cparams(),
    )


def _ssd_bwd(xbc, pre_act, dt_raw, prev_states, dy, conv_w, dtb_row, alog_row, d_exp, xchg):
    S = xbc.shape[0]
    nc = S // BLK
    nq = SSD_BWD_CHUNKS_PER_STEP if nc % SSD_BWD_CHUNKS_PER_STEP == 0 else 1
    rows, n_steps = nq * BLK, nc // nq

    def body(xbc_ref, halo_ref, pre_ref, dt_ref, prev_ref, dy_ref, cw_ref, dtb_ref, alog_ref, d_ref,
             dxbc_ref, ddt_ref, dcw_ref, dvec_ref, dd_ref, gstate_ref, ghalo_ref):
        i = pl.program_id(0)

        @pl.when(i == 0)
        def _():
            gstate_ref[...] = jnp.zeros_like(gstate_ref)
            ghalo_ref[...] = jnp.zeros_like(ghalo_ref)
            dcw_ref[...] = jnp.zeros_like(dcw_ref)
            dvec_ref[...] = jnp.zeros_like(dvec_ref)
            dd_ref[...] = jnp.zeros_like(dd_ref)

        halo = halo_ref[...] * jnp.where(i < n_steps - 1, 1.0, 0.0)
        ext = jnp.concatenate([halo, xbc_ref[...]], axis=0)
        pre = pre_ref[...]
        sig = jax.nn.sigmoid(pre)
        xc = pre * sig
        split = [_ssd_split(xc[c * BLK:(c + 1) * BLK]) for c in range(nq)]
        dt_t = [_dt_rows(dt_ref[c * BLK:(c + 1) * BLK, :]) for c in range(nq)]
        prev = [[prev_ref[c, h] for h in range(N_HEADS)] for c in range(nq)]
        d_rows = [d_ref[h:h + 1, :] for h in range(N_HEADS)]
        dys = [_split_heads(dy_ref[c * BLK:(c + 1) * BLK, :], 4) for c in range(nq)]
        dh_last = [gstate_ref[h] for h in range(N_HEADS)]
        dheads, dpb, dpc, ddt_t, dprev0, ddtb, dalog, dd_rows = _ssd_chunks_bwd(
            [s[0] for s in split], [s[1] for s in split], [s[2] for s in split], dt_t, prev, dtb_ref[...],
            alog_ref[...], d_rows, dys, dh_last, _ssd_consts())
        for h in range(N_HEADS):
            gstate_ref[h] = dprev0[h]
            dd_ref[h:h + 1, :] += dd_rows[h]
        pad = jnp.zeros((BLK - N_HEADS, BLK), F32)
        ddt_ref[...] = jnp.concatenate([jnp.concatenate([ddt_t[c], pad], axis=0).T for c in range(nq)],
                                       axis=0).astype(ddt_ref.dtype)
        dvec_ref[0:N_HEADS, :] += ddtb
        dvec_ref[N_HEADS:, :] += dalog
        dxc = jnp.concatenate([jnp.concatenate([_join_heads(dheads[c])] + list(dpb[c]) + list(dpc[c]), axis=1)
                               for c in range(nq)], axis=0)
        dpre = dxc * (sig * (1.0 + pre * (1.0 - sig)))
        zeros8 = jnp.zeros((8, XBC_W), F32)
        dpe = jnp.concatenate([zeros8, dpre, zeros8], axis=0)
        n_ext = 16 + rows
        shifted = [pltpu.roll(dpe, n_ext - (3 - k), 0)[:8 + rows] for k in range(3)] + [dpe[:8 + rows]]
        dext = cw_ref[0:1, :] * shifted[0]
        for k in range(1, 4):
            dext = dext + cw_ref[k:k + 1, :] * shifted[k]
        for k in range(4):
            dcw_ref[k:k + 1, :] += jnp.sum(shifted[k] * ext, axis=0, keepdims=True)
        dcw_ref[4:5, :] += jnp.sum(dpre, axis=0, keepdims=True)
        dxbc_ref[...] = jnp.concatenate([dext[8:rows], dext[rows:] + ghalo_ref[...]], axis=0).astype(dxbc_ref.dtype)
        ghalo_ref[...] = dext[:8, :]

    vec = pl.BlockSpec((N_HEADS, LANE), _fixed)
    rev = lambda i: (n_steps - 1 - i, 0)
    return _hosted_call(
        body, "ssd_bwd", n_steps,
        in_specs=[pl.BlockSpec((rows, XBC_W), rev),
                  pl.BlockSpec((8, XBC_W), lambda i: (jnp.maximum((n_steps - 1 - i) * (rows // 8) - 1, 0), 0)),
                  pl.BlockSpec((rows, XBC_W), rev),
                  pl.BlockSpec((rows, LANE), rev),
                  pl.BlockSpec((nq, N_HEADS, D_STATE, LANE), lambda i: (n_steps - 1 - i, 0, 0, 0)),
                  pl.BlockSpec((rows, SSM_W), rev),
                  pl.BlockSpec((4, XBC_W), _fixed), vec, vec,
                  pl.BlockSpec((N_HEADS, LANE), _fixed)],
        out_specs=[pl.BlockSpec((rows, XBC_W), rev), pl.BlockSpec((rows, LANE), rev),
                   pl.BlockSpec((8, XBC_W), _fixed), pl.BlockSpec((2 * N_HEADS, LANE), _fixed),
                   pl.BlockSpec((N_HEADS, LANE), _fixed)],
        out_shape=[jax.ShapeDtypeStruct((S, XBC_W), MXU_DTYPE), jax.ShapeDtypeStruct((S, LANE), MXU_DTYPE),
                   jax.ShapeDtypeStruct((8, XBC_W), F32), jax.ShapeDtypeStruct((2 * N_HEADS, LANE), F32),
                   jax.ShapeDtypeStruct((N_HEADS, LANE), F32)],
        scratch_shapes=[pltpu.VMEM((N_HEADS, D_STATE, LANE), F32), pltpu.VMEM((8, XBC_W), F32)],
        args=(xbc, xbc, pre_act, dt_raw, prev_states, dy, conv_w, dtb_row, alog_row, d_exp), xchg=xchg,
        cparams=_cparams(VMEM_BIG),
    )


def _adamw_math(w, g, m, v):
    m = ADAM_B1 * m + (1.0 - ADAM_B1) * g
    v = ADAM_B2 * v + (1.0 - ADAM_B2) * jnp.square(g)
    m_hat = m / (1.0 - ADAM_B1 ** ADAM_STEP)
    v_hat = v / (1.0 - ADAM_B2 ** ADAM_STEP)
    delta = -ADAM_LR * (m_hat / (jnp.sqrt(v_hat) + ADAM_EPS) + ADAM_WD * w)
    return delta, m, v


def _reduce_adamw_halves(part_a, part_b, w, m, v, name):
    R, C = w.shape
    tl = 256
    n = C // tl

    def body(a_ref, b_ref, w_ref, m_ref, v_ref, g_ref, d_ref, nm_ref, nv_ref):
        ga, gb = a_ref[0].astype(F32), b_ref[0].astype(F32)
        for i in range(1, N_DEV):
            ga, gb = ga + a_ref[i].astype(F32), gb + b_ref[i].astype(F32)
        first = jnp.where(pl.program_id(0) < n // 2, 1.0, 0.0)
        g = ga * first + gb * (1.0 - first)
        d, nm, nv = _adamw_math(w_ref[...], g, m_ref[...], v_ref[...])
        g_ref[...] = g
        d_ref[...] = d
        nm_ref[...] = nm
        nv_ref[...] = nv

    blk = pl.BlockSpec((R, tl), lambda i: (0, i))
    return pl.pallas_call(
        body, name=name, grid=(n,),
        in_specs=[pl.BlockSpec((N_DEV, R, tl), lambda i: (0, 0, jnp.minimum(i, n // 2 - 1))),
                  pl.BlockSpec((N_DEV, R, tl), lambda i: (0, 0, jnp.maximum(i - n // 2, 0))), blk, blk, blk],
        out_specs=[blk] * 4, out_shape=[jax.ShapeDtypeStruct((R, C), F32)] * 4,
    )(part_a, part_b, w, m, v)


def _reduce_adamw_hosting(parts_list, wmv_list, name, xchg):
    n_arr = len(parts_list)
    pieces = [list(p) if isinstance(p, (tuple, list)) else [p] for p in parts_list]
    n_pieces = sum(len(p) for p in pieces)
    C = wmv_list[0][0].shape[1]
    tl = 256

    def total(ref):
        g = ref[0].astype(F32)
        for i in range(1, N_DEV):
            g = g + ref[i].astype(F32)
        return g

    def body(*refs):
        p_refs, wmv_refs, o_refs = refs[:n_pieces], refs[n_pieces:n_pieces + 3 * n_arr], refs[n_pieces + 3 * n_arr:]
        at = 0
        for k in range(n_arr):
            sums = [total(r) for r in p_refs[at:at + len(pieces[k])]]
            at += len(pieces[k])
            g = sums[0] if len(sums) == 1 else jnp.concatenate(sums, axis=0)
            w_ref, m_ref, v_ref = wmv_refs[3 * k:3 * k + 3]
            d, nm, nv = _adamw_math(w_ref[...], g, m_ref[...], v_ref[...])
            for o, val in zip(o_refs[4 * k:4 * k + 4], (g, d, nm, nv)):
                o[...] = val

    in_specs = [pl.BlockSpec((N_DEV, p.shape[1], tl), lambda i: (0, 0, i)) for group in pieces for p in group]
    in_specs += [pl.BlockSpec((w.shape[0], tl), lambda i: (0, i)) for w, _, _ in wmv_list for _ in range(3)]
    out_specs = [pl.BlockSpec((w.shape[0], tl), lambda i: (0, i)) for w, _, _ in wmv_list for _ in range(4)]
    out_shape = [jax.ShapeDtypeStruct(w.shape, F32) for w, _, _ in wmv_list for _ in range(4)]
    args = [p for group in pieces for p in group] + [a for wmv in wmv_list for a in wmv]
    outs, x_out = _hosted_call(body, name, C // tl, in_specs, out_specs, out_shape, [], args, xchg,
                               _cparams(VMEM_BIG))
    return [outs[4 * k:4 * k + 4] for k in range(n_arr)], x_out


_SMALL_NAMES = ("ada_b", "norm1", "conv_w", "conv_b", "dt_bias", "A_log", "D_skip", "sinks", "attn_out_norm",
                "ssm_out_norm", "norm2", "rel_bias", "final_norm")
N_MOD = 6 * D_MODEL


def _mod_row(a0, a1, a2):
    return jnp.concatenate([a0[2:3], a0[1:2], a1[0:1], a2[2:3], a2[1:2], a2[3:4]], axis=1)


def _small_update(gathered, params):
    n_g = len(gathered)
    flat = [a for name in _SMALL_NAMES for a in params[name]]

    def body(*refs):
        a0_ref, a1_ref, a2_ref, cw_ref, dv_ref, dd_ref, ds_ref, dr_ref, c_ref = refs[:n_g]
        wmv = refs[n_g:n_g + len(flat)]
        outs = refs[n_g + len(flat):]

        def total(ref):
            t = ref[0]
            for i in range(1, N_DEV):
                t = t + ref[i]
            return t

        t0, t1, t2, tcw, tdv, tdd, tds, tdr = [total(r) for r in (a0_ref, a1_ref, a2_ref, cw_ref, dv_ref, dd_ref,
                                                                   ds_ref, dr_ref)]
        r8 = lax.broadcasted_iota(jnp.int32, (N_HEADS, LANE), 0)
        l8 = lax.broadcasted_iota(jnp.int32, (N_HEADS, LANE), 1)

        def diag_row(t):
            return jnp.sum(jnp.where(r8 == l8, t, 0.0), axis=0, keepdims=True)[:, :N_HEADS]

        def lane_sums(t):
            return diag_row(jnp.broadcast_to(jnp.sum(t, axis=1, keepdims=True), (N_HEADS, LANE)))

        me = _lin(_my_pos())
        n_cw = XBC_W // N_DEV
        cw_mine = jnp.zeros((4, n_cw), F32)
        for j in range(N_DEV):
            cw_mine = cw_mine + tcw[0:4, j * n_cw:(j + 1) * n_cw] * jnp.where(me == j, 1.0, 0.0)
        grads = {
            "ada_b": _mod_row(t0, t1, t2), "norm1": t0[0:1], "conv_w": cw_mine, "conv_b": tcw[4:5],
            "dt_bias": lane_sums(tdv[:N_HEADS]), "A_log": lane_sums(tdv[N_HEADS:]), "D_skip": lane_sums(tdd),
            "sinks": diag_row(tds), "attn_out_norm": t1[1:2, :ATTN_W], "ssm_out_norm": t1[1:2, ATTN_W:],
            "norm2": t2[0:1], "rel_bias": tdr[:, :N_HEADS], "final_norm": t2[4:5],
        }
        for k, name in enumerate(_SMALL_NAMES):
            w_ref, m_ref, v_ref = wmv[3 * k:3 * k + 3]
            g = grads[name]
            d, nm, nv = _adamw_math(w_ref[...], g, m_ref[...], v_ref[...])
            for o, val in zip(outs[4 * k:4 * k + 4], (g, d, nm, nv)):
                o[...] = val
        loss_ref, call_ref, dmod_ref = outs[4 * len(_SMALL_NAMES):]
        loss_ref[...] = t2[5:6, 0:1]
        call_ref[...] = jnp.concatenate([c_ref[i] for i in range(N_DEV)], axis=0)
        dmod_ref[...] = jnp.concatenate([_mod_row(a0_ref[i], a1_ref[i], a2_ref[i]) for i in range(N_DEV)], axis=0)

    out_shape = [jax.ShapeDtypeStruct(params[name][0].shape, F32) for name in _SMALL_NAMES for _ in range(4)]
    out_shape += [jax.ShapeDtypeStruct((1, 1), F32), jax.ShapeDtypeStruct((N_DEV, D_MODEL), F32),
                  jax.ShapeDtypeStruct((N_DEV, N_MOD), F32)]
    res = pl.pallas_call(body, name="small_update", out_shape=out_shape)(*gathered, *flat)
    upd = {name: res[4 * k:4 * k + 4] for k, name in enumerate(_SMALL_NAMES)}
    loss, c_all, dmod_all = res[4 * len(_SMALL_NAMES):]
    return upd, loss, c_all, dmod_all


def _ada_w_update(c_all, dmod_all, w, m, v):
    chunk = w.shape[1]

    def body(c_ref, dm_ref, w_ref, m_ref, v_ref, g_ref, d_ref, nm_ref, nv_ref):
        me = _lin(_my_pos())
        dm = jnp.zeros((N_DEV, chunk), F32)
        for j in range(N_DEV):
            dm = dm + dm_ref[:, j * chunk:(j + 1) * chunk] * jnp.where(me == j, 1.0, 0.0)
        g = lax.dot_general(_silu(c_ref[...]), dm, (((0,), (0,)), ((), ())), precision=HI,
                            preferred_element_type=F32)
        d, nm, nv = _adamw_math(w_ref[...], g, m_ref[...], v_ref[...])
        g_ref[...] = g
        d_ref[...] = d
        nm_ref[...] = nm
        nv_ref[...] = nv

    tr = 256
    blk = pl.BlockSpec((tr, chunk), _row)
    return pl.pallas_call(
        body, name="ada_w_update", grid=(w.shape[0] // tr,),
        in_specs=[pl.BlockSpec((N_DEV, tr), lambda i: (0, i)), pl.BlockSpec(dmod_all.shape, _fixed), blk, blk, blk],
        out_specs=[blk] * 4, out_shape=[jax.ShapeDtypeStruct(w.shape, F32)] * 4,
    )(c_all, dmod_all, w, m, v)


def _local_step(x, tgt, c, mod, w_in, conv_w, w_o_mine, w_gu_mine, w_d_mine, p):
    S = x.shape[0]
    tm = min(512, S)
    tmm = min(256, S)
    tw = min(2048, S)
    shift1, scale1, gate1, shift2, scale2, gate2 = [mod[i:i + 1] for i in range(6)]
    buckets = jnp.asarray(_t5_bucket_table())
    per_head = lambda a: jnp.broadcast_to(a.reshape(N_HEADS, 1), (N_HEADS, LANE))
    dtb_row, alog_row, d_exp = per_head(p["dt_bias"]), per_head(p["A_log"]), per_head(p["D_skip"])
    sinks = p["sinks"].reshape(N_HEADS)

    d_cut, gu_cut = WD_CUT, WGU_CUTS
    n_d, n_gu = w_d_mine.shape[0], w_gu_mine.shape[0]
    (qkv, z, xbc, dt_raw), (g_d_a,) = _in_proj_fwd(x, p["norm1"], scale1, shift1, w_in, tm,
                                                   ([(w_d_mine, 0, d_cut)], "two-level"))
    bias = _attn_bias(buckets, p["rel_bias"])
    (ya,), (g_gu_a,) = _attn_fwd(qkv, bias, sinks, ([(w_gu_mine, 0, gu_cut[0])], "two-level"))
    (ys, prev_states, pre_act), (g_gu_b, g_o) = _ssd_fwd(
        xbc, dt_raw, conv_w, p["conv_b"], dtb_row, alog_row, d_exp,
        ([(w_gu_mine, gu_cut[0], gu_cut[1] - gu_cut[0]), w_o_mine], "two-level"))
    w_o = g_o.reshape(D_MODEL, D_MODEL)
    x1, (g_gu_c, g_d_b) = _out_proj_fwd(
        x, ya, ys, z, p["attn_out_norm"], p["ssm_out_norm"], gate1, w_o, tm,
        ([(w_gu_mine, gu_cut[1], n_gu - gu_cut[1]), (w_d_mine, d_cut, n_d - d_cut)], "two-level"))
    dx1, h2, dgu, act, dmlp, acc2 = _mlp_loss(x1, tgt, p["norm2"], scale2, shift2, gate2, p["final_norm"],
                                              (g_gu_a, g_gu_b, g_gu_c), (g_d_a, g_d_b), tmm)
    g_w_gu = _wgrad(dgu, h2, 2 * D_FF // 4, tw, "wgrad_gate_up")
    g_w_d = _wgrad(act, dmlp, D_FF // 2, tw, "wgrad_down")
    gu_slots = g_w_gu.reshape(N_DEV, 2 * D_FF // N_DEV, D_MODEL)
    (dya, dys, dz, g_w_o, acc1), (r_gu_a,) = _out_proj_bwd(
        dx1, ya, ys, z, p["attn_out_norm"], p["ssm_out_norm"], gate1, w_o, tm, ([gu_slots], ("rows", 0, GGU_CUT)))
    (dq, dkv, dbias, dsk), (r_d, r_o) = _attn_bwd(
        qkv, ya, dya, bias, sinks,
        ([g_w_d.reshape(N_DEV, D_FF // N_DEV, D_MODEL), g_w_o.reshape(N_DEV, D_MODEL // N_DEV, D_MODEL)], True))
    drel, dsink = _attn_finish(dbias, dsk, buckets)
    (dxbc, ddt, dcw, dvec, dd), (r_gu_b, *early) = _ssd_bwd(
        xbc, pre_act, dt_raw, prev_states, dys, conv_w, dtb_row, alog_row, d_exp,
        [([gu_slots], ("rows", GGU_CUT, 2 * D_FF // N_DEV - GGU_CUT)), ([acc1, acc2, dsink, drel, c], False)])
    r_gu = (r_gu_a, r_gu_b)
    gx, h1, acc0, g_in_a = _in_proj_bwd(x, dx1, dq, dkv, dz, dxbc, ddt, p["norm1"], scale1, shift1, w_in, tm)
    half = D_MODEL // 2
    slots = lambda g: g[:IN_W].reshape(N_DEV, IN_W // N_DEV, half)
    g_in_b, (r_in_a, *late) = _wgrad((dq, dkv, dz, dxbc, ddt), h1, IN_PAD, tw, "wgrad_in_b",
                                     [([slots(g_in_a)], True), ([acc0, dcw, dvec, dd], False)], g_cols=(half, 1))
    gathered = (late[0], early[0], early[1], late[1], late[2], late[3], early[2], early[3], early[4])
    return gx, (r_in_a, slots(g_in_b)), (r_o, r_gu, r_d), gathered


def kernel(x, c, ada_w, ada_b, norm1, w_in, conv_w, conv_b, dt_bias, A_log, D_skip, sinks, attn_out_norm, ssm_out_norm, w_o, norm2, w_gate_up, w_down, rel_bias, final_norm, loss_target, m_ada_w, m_ada_b, m_norm1, m_w_in, m_conv_w, m_conv_b, m_dt_bias, m_A_log, m_D_skip, m_sinks, m_attn_out_norm, m_ssm_out_norm, m_w_o, m_norm2, m_w_gate_up, m_w_down, m_rel_bias, m_final_norm, v_ada_w, v_ada_b, v_norm1, v_w_in, v_conv_w, v_conv_b, v_dt_bias, v_A_log, v_D_skip, v_sinks, v_attn_out_norm, v_ssm_out_norm, v_w_o, v_norm2, v_w_gate_up, v_w_down, v_rel_bias, v_final_norm):
    two_d = lambda a: a if a.ndim == 2 else a.reshape(-1, a.shape[-1])
    small_params = dict(
        ada_b=(ada_b, m_ada_b, v_ada_b), norm1=(norm1, m_norm1, v_norm1), conv_w=(conv_w, m_conv_w, v_conv_w),
        conv_b=(conv_b, m_conv_b, v_conv_b), dt_bias=(dt_bias, m_dt_bias, v_dt_bias), A_log=(A_log, m_A_log, v_A_log),
        D_skip=(D_skip, m_D_skip, v_D_skip), sinks=(sinks, m_sinks, v_sinks),
        attn_out_norm=(attn_out_norm, m_attn_out_norm, v_attn_out_norm),
        ssm_out_norm=(ssm_out_norm, m_ssm_out_norm, v_ssm_out_norm), norm2=(norm2, m_norm2, v_norm2),
        rel_bias=(rel_bias, m_rel_bias, v_rel_bias), final_norm=(final_norm, m_final_norm, v_final_norm))
    small_params = {k: tuple(two_d(a) for a in v) for k, v in small_params.items()}
    S = x.shape[1]
    xs, tgt = x.reshape(S, D_MODEL), loss_target.reshape(S, D_MODEL)
    ada_w2 = ada_w[0]
    chunk = ada_w2.shape[1]
    t_in = [jnp.transpose(a[0]) for a in (w_in, m_w_in, v_w_in)]
    t_gu = [jnp.transpose(a[0]) for a in (w_gate_up, m_w_gate_up, v_w_gate_up)]

    mod, (g_in, g_cw) = _mod_and_gather(c, ada_w2, ada_b.reshape(N_DEV, chunk), [t_in[0].astype(WIRE_DTYPE), conv_w[0]])
    mod = mod.reshape(6, D_MODEL)
    w_in_full = jnp.pad(g_in.reshape(IN_W, D_MODEL), ((0, IN_PAD - IN_W), (0, 0)))
    conv_w_full = jnp.transpose(g_cw, (1, 0, 2)).reshape(4, XBC_W)

    p = {k: v[0] for k, v in small_params.items()}
    gx, (r_in_a, gw_in_b), (r_o, r_gu, r_d), gathered = _local_step(
        xs, tgt, c, mod, w_in_full, conv_w_full, w_o[0].astype(WIRE_DTYPE), t_gu[0].astype(WIRE_DTYPE),
        w_down[0].astype(WIRE_DTYPE), p)

    (u_gu, u_d, u_o), (r_in_b,) = _reduce_adamw_hosting(
        [r_gu, r_d, r_o], [tuple(t_gu), (w_down[0], m_w_down[0], v_w_down[0]), (w_o[0], m_w_o[0], v_w_o[0])],
        "adamw_big", ([gw_in_b], True))

    small, loss, c_all, dmod_all = _small_update(gathered, small_params)

    big = {
        "ada_w": _ada_w_update(c_all, dmod_all, ada_w2, m_ada_w[0], v_ada_w[0]),
        "w_in": [jnp.transpose(a) for a in _reduce_adamw_halves(r_in_a, r_in_b, *t_in, "adamw_w_in")],
        "w_o": u_o,
        "w_gate_up": [jnp.transpose(a) for a in u_gu],
        "w_down": u_d,
    }
    big.update(small)

    order = ['ada_w', 'ada_b', 'norm1', 'w_in', 'conv_w', 'conv_b', 'dt_bias', 'A_log', 'D_skip', 'sinks',
             'attn_out_norm', 'ssm_out_norm', 'w_o', 'norm2', 'w_gate_up', 'w_down', 'rel_bias', 'final_norm']
    shapes = dict(ada_w=ada_w.shape, ada_b=ada_b.shape, norm1=norm1.shape, w_in=w_in.shape, conv_w=conv_w.shape,
                  conv_b=conv_b.shape, dt_bias=dt_bias.shape, A_log=A_log.shape, D_skip=D_skip.shape,
                  sinks=sinks.shape, attn_out_norm=attn_out_norm.shape, ssm_out_norm=ssm_out_norm.shape,
                  w_o=w_o.shape, norm2=norm2.shape, w_gate_up=w_gate_up.shape, w_down=w_down.shape,
                  rel_bias=rel_bias.shape, final_norm=final_norm.shape)
    outs = [[], [], [], []]
    for name in order:
        for kind in range(4):
            outs[kind].append(big[name][kind].reshape(shapes[name]))
    return (loss.reshape(()), gx.reshape(x.shape), *outs[0], *outs[1], *outs[2], *outs[3])
```

```python
import numpy as np
import jax
import jax.numpy as jnp
from jax import lax
from jax.experimental import pallas as pl
from jax.experimental.pallas import tpu as pltpu

F32 = jnp.float32
MXU_DTYPE = jnp.bfloat16
WIRE_DTYPE = jnp.bfloat16
HI = lax.Precision.HIGHEST
MESH = pl.DeviceIdType.MESH
N_DEV = 8

D_MODEL = 1024
ATTN_W = 512
KV_W = 128
SSM_W = 512
XBC_W = 1024
N_HEADS = 8
D_STATE = 128
D_FF = 2816
IN_W = 2312
IN_PAD = 2432
BLK = 128
N_BUCKETS = 32
EPS = 1e-6
LANE = 128
HALF = 64

ADAM_LR, ADAM_B1, ADAM_B2, ADAM_EPS, ADAM_WD, ADAM_STEP = 0.001, 0.9, 0.999, 1e-08, 0.01, 10

VMEM_BIG = 56 * 1024 * 1024
WD_CUT = 288
WGU_CUTS = (240, 496)
GGU_CUT = 304
GIN_CUT = 640


def _cparams(vmem=None):
    if vmem is None:
        return pltpu.CompilerParams()
    return pltpu.CompilerParams(vmem_limit_bytes=vmem)


def _mm(a, b):
    return jnp.dot(a.astype(MXU_DTYPE), b.astype(MXU_DTYPE), preferred_element_type=F32)


def _mm_nt(a, b):
    return lax.dot_general(a.astype(MXU_DTYPE), b.astype(MXU_DTYPE), (((1,), (1,)), ((), ())),
                           preferred_element_type=F32)


def _mm_tn(a, b):
    return lax.dot_general(a.astype(MXU_DTYPE), b.astype(MXU_DTYPE), (((0,), (0,)), ((), ())),
                           preferred_element_type=F32)


def _mm_hi(a, b):
    return jnp.dot(a, b, precision=HI, preferred_element_type=F32)


def _silu(x):
    return x * jax.nn.sigmoid(x)


def _softplus(x):
    return jnp.maximum(x, 0.0) + jnp.log1p(jnp.exp(-jnp.abs(x)))


def _rms(x, g, n):
    return x * lax.rsqrt(jnp.sum(x * x, axis=-1, keepdims=True) * (1.0 / n) + EPS) * g


def _modnorm(x, g, scale, shift):
    return _rms(x, g, x.shape[-1]) * (1.0 + scale) + shift


def _modnorm_parts(x):
    r = lax.rsqrt(jnp.sum(x * x, axis=-1, keepdims=True) * (1.0 / x.shape[-1]) + EPS)
    return r, x * r


def _modnorm_bwd(r, xhat, g, scale, dy):
    dyg = dy * (g * (1.0 + scale))
    c = jnp.sum(dyg * xhat, axis=-1, keepdims=True) * (1.0 / xhat.shape[-1])
    dx = r * (dyg - xhat * c)
    ct = jnp.sum(dy * xhat, axis=0, keepdims=True)
    return dx, ct * (1.0 + scale), ct * g, jnp.sum(dy, axis=0, keepdims=True)


def _lane_iota(shape):
    return lax.broadcasted_iota(jnp.int32, shape, len(shape) - 1)


def _split_pair(t):
    lane = _lane_iota(t.shape)
    lo = jnp.where(lane < HALF, t, 0.0)
    hi = pltpu.roll(jnp.where(lane >= HALF, t, 0.0), HALF, 1)
    return lo, hi


def _join_pair(lo, hi):
    lane = _lane_iota(lo.shape)
    return jnp.where(lane < HALF, lo, pltpu.roll(hi, HALF, 1))


def _split_heads(t, n_pairs):
    out = []
    for p in range(n_pairs):
        out.extend(_split_pair(t[:, p * LANE:(p + 1) * LANE]))
    return out


def _join_heads(hs):
    return jnp.concatenate([_join_pair(hs[2 * p], hs[2 * p + 1]) for p in range(len(hs) // 2)], axis=1)


def _t5_bucket_table():
    dist = np.arange(BLK)[:, None] + BLK - np.arange(2 * BLK)[None, :]
    n = np.maximum(dist, 0)
    max_exact = N_BUCKETS // 2
    large = max_exact + (np.log(np.maximum(n, 1) / max_exact) / np.log(128 / max_exact)
                         * (N_BUCKETS - max_exact)).astype(np.int32)
    large = np.minimum(large, N_BUCKETS - 1)
    return np.where(n < max_exact, n, large).astype(np.int32)


def _my_pos():
    return lax.axis_index("x"), lax.axis_index("y"), lax.axis_index("c")


def _peer(k):
    x, y, c = _my_pos()
    return (1 - x if k & 4 else x, 1 - y if k & 2 else y, 1 - c if k & 1 else c)


def _lin(pos):
    return 4 * pos[0] + 2 * pos[1] + pos[2]


def _xchg_copies(ins, outs, sems, scatter):
    local_sem, send_sem, recv_sem = sems
    me = _lin(_my_pos())

    def source(a, slot):
        if not scatter:
            return ins[a]
        if scatter is True:
            return ins[a].at[slot]
        return ins[a].at[slot, pl.ds(scatter[1], scatter[2])]

    local, remote = [], []
    for a in range(len(ins)):
        local.append(pltpu.make_async_copy(source(a, me), outs[a].at[me], local_sem.at[a]))
    for k in range(1, N_DEV):
        peer = _peer(k)
        for a in range(len(ins)):
            remote.append(pltpu.make_async_remote_copy(source(a, _lin(peer)), outs[a].at[me], send_sem.at[a, k - 1],
                                                       recv_sem.at[a, k - 1], device_id=peer, device_id_type=MESH))
    return local, remote


def _xchg_start(ins, outs, sems, scatter):
    local, remote = _xchg_copies(ins, outs, sems, scatter)
    for cp in local + remote:
        cp.start()


def _xchg_wait(ins, outs, sems, scatter):
    local, remote = _xchg_copies(ins, outs, sems, scatter)
    for cp in local:
        cp.wait()
    for cp in remote:
        cp.wait_send()
        cp.wait_recv()


def _xchg_shapes(arrs, scatter):
    n = len(arrs)
    if isinstance(scatter, tuple):
        out_shape = [jax.ShapeDtypeStruct((a.shape[0], scatter[2]) + a.shape[2:], a.dtype) for a in arrs]
    elif scatter:
        out_shape = [jax.ShapeDtypeStruct(a.shape, a.dtype) for a in arrs]
    else:
        out_shape = [jax.ShapeDtypeStruct((N_DEV,) + a.shape, a.dtype) for a in arrs]
    sems = [pltpu.SemaphoreType.DMA((n,)), pltpu.SemaphoreType.DMA((n, N_DEV - 1)),
            pltpu.SemaphoreType.DMA((n, N_DEV - 1))]
    return out_shape, sems


_CHIPS = (2, 4, 6)


def _g2_sems(n):
    dma = pltpu.SemaphoreType.DMA
    return [dma((n,)), dma((n, N_DEV)), dma((n, N_DEV)), dma((n, len(_CHIPS))), dma((n, len(_CHIPS)))]


class _TwoLevelGather:
    def __init__(self, ins, outs, sems, windows=None):
        self.ins, self.outs = ins, outs
        self.local_sem, self.send_sem, self.recv_sem, self.fsend_sem, self.frecv_sem = sems
        self.n = len(ins)
        self.windows = windows or [None] * self.n

    def _mine(self, a):
        w = self.windows[a]
        return self.ins[a] if w is None else self.ins[a].at[pl.ds(w[0], w[1])]

    def _direct(self, a, k):
        return pltpu.make_async_remote_copy(self._mine(a), self.outs[a].at[_lin(_my_pos())], self.send_sem.at[a, k],
                                            self.recv_sem.at[a, k], device_id=_peer(k), device_id_type=MESH)

    def _handed_on(self, a, j, origin):
        slot = self.outs[a].at[origin]
        return pltpu.make_async_remote_copy(slot, slot, self.fsend_sem.at[a, j], self.frecv_sem.at[a, j],
                                            device_id=_peer(1), device_id_type=MESH)

    def _local(self, a):
        return pltpu.make_async_copy(self._mine(a), self.outs[a].at[_lin(_my_pos())], self.local_sem.at[a])

    def start(self):
        for a in range(self.n):
            self._local(a).start()
        for k in (1,) + _CHIPS:
            for a in range(self.n):
                self._direct(a, k).start()

    def forward(self):
        for j, k in enumerate(_CHIPS):
            for a in range(self.n):
                self._direct(a, k).wait_recv()
                self._handed_on(a, j, _lin(_peer(k))).start()

    def finish(self):
        for a in range(self.n):
            self._direct(a, 1).wait_recv()
            for j, k in enumerate(_CHIPS):
                self._handed_on(a, j, _lin(_peer(k ^ 1))).wait_recv()
            self._local(a).wait()
            for k in (1,) + _CHIPS:
                self._direct(a, k).wait_send()
            for j, k in enumerate(_CHIPS):
                self._handed_on(a, j, _lin(_peer(k))).wait_send()


def _mod_and_gather(c, ada_w, ada_b8, arrs):
    n = len(arrs)
    chunk = ada_w.shape[1]
    out_shape = [jax.ShapeDtypeStruct((N_DEV, 1, chunk), F32)]
    out_shape += [jax.ShapeDtypeStruct((N_DEV,) + a.shape, a.dtype) for a in arrs]

    def modulation(c_ref, w_ref, b_ref, out_ref, cbuf, part, s1, r1, s2, r2):
        me = _lin(_my_pos())
        first = []
        for k in range(1, N_DEV):
            cp = pltpu.make_async_remote_copy(c_ref, cbuf.at[me], s1.at[k - 1], r1.at[k - 1],
                                              device_id=_peer(k), device_id_type=MESH)
            cp.start()
            first.append(cp)
        cbuf[me] = c_ref[...]
        for cp in first:
            cp.wait_send()
            cp.wait_recv()
        cond = _silu(jnp.concatenate([cbuf[i] for i in range(N_DEV)], axis=0))
        mod = _mm_hi(cond, w_ref[...]) + b_ref[pl.ds(me, 1), :]
        for j in range(N_DEV):
            part[j] = mod[j:j + 1, :]
        second = []
        for k in range(1, N_DEV):
            peer = _peer(k)
            cp = pltpu.make_async_remote_copy(part.at[_lin(peer)], out_ref.at[me], s2.at[k - 1], r2.at[k - 1],
                                              device_id=peer, device_id_type=MESH)
            cp.start()
            second.append(cp)
        out_ref[me] = part[me]
        for cp in second:
            cp.wait_send()
            cp.wait_recv()

    def body(*refs):
        c_ref, w_ref, b_ref = refs[:3]
        ins = refs[3:3 + n]
        mod_ref = refs[3 + n]
        outs = refs[4 + n:4 + 2 * n]
        cbuf, part, s1, r1, s2, r2 = refs[4 + 2 * n:10 + 2 * n]
        gather = _TwoLevelGather(ins, outs, refs[10 + 2 * n:])
        gather.start()
        modulation(c_ref, w_ref, b_ref, mod_ref, cbuf, part, s1, r1, s2, r2)
        gather.forward()
        gather.finish()

    hbm = pl.BlockSpec(memory_space=pltpu.HBM)
    vm = pl.BlockSpec(memory_space=pltpu.VMEM)
    dma = pltpu.SemaphoreType.DMA
    res = pl.pallas_call(
        body, name="mod_and_gather", out_shape=out_shape, in_specs=[vm, vm, vm] + [hbm] * n,
        out_specs=[vm] + [hbm] * n,
        scratch_shapes=[pltpu.VMEM((N_DEV, 1, D_MODEL), F32), pltpu.VMEM((N_DEV, 1, chunk), F32)]
        + [dma((N_DEV - 1,))] * 4 + _g2_sems(n),
    )(c, ada_w, ada_b8, *arrs)
    return res[0], res[1:]


def _hosted_call(body, name, grid, in_specs, out_specs, out_shape, scratch_shapes, args, xchg, cparams):
    xchgs = [xchg] if isinstance(xchg, tuple) else list(xchg)
    grid = (grid,) if isinstance(grid, int) else tuple(grid)
    n_in, n_out, n_scr = len(in_specs), len(out_specs), len(scratch_shapes)
    windows = [[(a[1], a[2]) if isinstance(a, tuple) else None for a in group] for group, _ in xchgs]
    xchgs = [([a[0] if isinstance(a, tuple) else a for a in group], mode) for group, mode in xchgs]
    arrs = [a for group, _ in xchgs for a in group]
    n = len(arrs)
    x_shape, x_sems, sem_counts = [], [], []
    for (group, mode), wins in zip(xchgs, windows):
        shapes, sems = _xchg_shapes(group, False if mode == "two-level" else mode)
        if mode == "two-level":
            sems = _g2_sems(len(group))
            shapes = [s if w is None else jax.ShapeDtypeStruct((N_DEV, w[1]) + a.shape[1:], a.dtype)
                      for s, w, a in zip(shapes, wins, group)]
        x_shape += shapes
        x_sems += sems
        sem_counts.append(len(sems))
    n_steps = int(np.prod(grid))

    def hosted(*refs):
        ins, refs = refs[:n_in], refs[n_in:]
        x_in, refs = refs[:n], refs[n:]
        outs, refs = refs[:n_out], refs[n_out:]
        x_out, refs = refs[:n], refs[n:]
        scr, sems = refs[:n_scr], refs[n_scr:]
        step = pl.program_id(0)
        for d in range(1, len(grid)):
            step = step * grid[d] + pl.program_id(d)
        parts, a0, s0 = [], 0, 0
        for (group, mode), ns, wins in zip(xchgs, sem_counts, windows):
            parts.append((x_in[a0:a0 + len(group)], x_out[a0:a0 + len(group)], sems[s0:s0 + ns], mode, wins))
            a0, s0 = a0 + len(group), s0 + ns

        @pl.when(step == 0)
        def _():
            for gi, go, gs, mode, wins in parts:
                if mode == "two-level":
                    _TwoLevelGather(gi, go, gs, wins).start()
                else:
                    _xchg_start(gi, go, gs, mode)

        if any(mode == "two-level" for _, mode in xchgs):
            @pl.when(step == (2 * n_steps) // 3)
            def _():
                for gi, go, gs, mode, wins in parts:
                    if mode == "two-level":
                        _TwoLevelGather(gi, go, gs, wins).forward()

        body(*ins, *outs, *scr)

        @pl.when(step == n_steps - 1)
        def _():
            for gi, go, gs, mode, wins in parts:
                if mode == "two-level":
                    _TwoLevelGather(gi, go, gs, wins).finish()
                else:
                    _xchg_wait(gi, go, gs, mode)

    hbm = pl.BlockSpec(memory_space=pltpu.HBM)
    res = pl.pallas_call(
        hosted, name=name, grid=grid, in_specs=list(in_specs) + [hbm] * n,
        out_specs=list(out_specs) + [hbm] * n, out_shape=list(out_shape) + x_shape,
        scratch_shapes=list(scratch_shapes) + x_sems, compiler_params=cparams,
    )(*args, *arrs)
    return res[:n_out], res[n_out:]


def _row(i):
    return (i, 0)


def _fixed(i):
    return (0, 0)


def _in_proj_fwd(x, norm1, scale1, shift1, w_in, tm, xchg):
    S = x.shape[0]

    def body(x_ref, n_ref, sc_ref, sh_ref, w_ref, qkv_ref, z_ref, xbc_ref, dt_ref):
        h = _modnorm(x_ref[...], n_ref[...], sc_ref[...], sh_ref[...])
        p = _mm_nt(h, w_ref[...])
        qkv_ref[...] = p[:, :768].astype(qkv_ref.dtype)
        z_ref[...] = p[:, 768:1280]
        xbc_ref[...] = p[:, 1280:2304]
        dt_ref[...] = p[:, 2304:IN_PAD]

    vec = pl.BlockSpec((1, D_MODEL), _fixed)
    return _hosted_call(
        body, "in_proj_fwd", S // tm,
        in_specs=[pl.BlockSpec((tm, D_MODEL), _row), vec, vec, vec, pl.BlockSpec((IN_PAD, D_MODEL), _fixed)],
        out_specs=[pl.BlockSpec((tm, 768), _row), pl.BlockSpec((tm, SSM_W), _row),
                   pl.BlockSpec((tm, XBC_W), _row), pl.BlockSpec((tm, LANE), _row)],
        out_shape=[jax.ShapeDtypeStruct((S, 768), MXU_DTYPE), jax.ShapeDtypeStruct((S, SSM_W), F32),
                   jax.ShapeDtypeStruct((S, XBC_W), F32), jax.ShapeDtypeStruct((S, LANE), F32)],
        scratch_shapes=[], args=(x, norm1, scale1, shift1, w_in), xchg=xchg, cparams=_cparams(VMEM_BIG),
    )


def _in_proj_bwd(x, dx1, dq, dkv, dz, dxbc, ddt, norm1, scale1, shift1, w_in, tm):
    S = x.shape[0]

    n_steps = S // tm

    def body(x_ref, dx1_ref, dq_ref, dkv_ref, dz_ref, dxbc_ref, ddt_ref, n_ref, sc_ref, sh_ref, w_ref,
             gx_ref, h_ref, acc_ref, gw_ref, gw_acc):
        i = pl.program_id(0)

        @pl.when(i == 0)
        def _():
            acc_ref[...] = jnp.zeros_like(acc_ref)
            gw_acc[...] = jnp.zeros_like(gw_acc)

        halves = [pl.ds(k * (tm // 2), tm // 2) for k in range(2)]
        dp = [jnp.concatenate([r[rows, :] for r in (dq_ref, dkv_ref, dz_ref, dxbc_ref, ddt_ref)], axis=1)
              for rows in halves]
        dh = [_mm(dp[k], w_ref[...]) for k in range(2)]
        parts = [_modnorm_parts(x_ref[rows, :]) for rows in halves]
        hb = [(parts[k][1] * n_ref[...] * (1.0 + sc_ref[...]) + sh_ref[...]).astype(h_ref.dtype) for k in range(2)]
        gw_acc[...] += _mm_tn(dp[0], hb[0][:, :GIN_CUT]) + _mm_tn(dp[1], hb[1][:, :GIN_CUT])
        bwd = [_modnorm_bwd(parts[k][0], parts[k][1], n_ref[...], sc_ref[...], dh[k]) for k in range(2)]
        for k, rows in enumerate(halves):
            gx_ref[rows, :] = dx1_ref[rows, :] + bwd[k][0]
            h_ref[rows, :] = hb[k][:, GIN_CUT:]
        acc_ref[0:1, :] += bwd[0][1] + bwd[1][1]
        acc_ref[1:2, :] += bwd[0][2] + bwd[1][2]
        acc_ref[2:3, :] += bwd[0][3] + bwd[1][3]

        @pl.when(i == n_steps - 1)
        def _():
            gw_ref[...] = gw_acc[...].astype(gw_ref.dtype)

    vec = pl.BlockSpec((1, D_MODEL), _fixed)
    return pl.pallas_call(
        body, name="in_proj_bwd", grid=(n_steps,),
        in_specs=[pl.BlockSpec((tm, D_MODEL), _row), pl.BlockSpec((tm, D_MODEL), _row),
                  pl.BlockSpec((tm, ATTN_W), _row), pl.BlockSpec((tm, 2 * KV_W), _row),
                  pl.BlockSpec((tm, SSM_W), _row), pl.BlockSpec((tm, XBC_W), _row), pl.BlockSpec((tm, LANE), _row),
                  vec, vec, vec, pl.BlockSpec((IN_PAD, D_MODEL), _fixed)],
        out_specs=[pl.BlockSpec((tm, D_MODEL), _row), pl.BlockSpec((tm, D_MODEL - GIN_CUT), _row),
                   pl.BlockSpec((8, D_MODEL), _fixed), pl.BlockSpec((IN_PAD, GIN_CUT), _fixed)],
        out_shape=[jax.ShapeDtypeStruct((S, D_MODEL), F32), jax.ShapeDtypeStruct((S, D_MODEL - GIN_CUT), MXU_DTYPE),
                   jax.ShapeDtypeStruct((8, D_MODEL), F32), jax.ShapeDtypeStruct((IN_PAD, GIN_CUT), WIRE_DTYPE)],
        scratch_shapes=[pltpu.VMEM((IN_PAD, GIN_CUT), F32)],
        compiler_params=_cparams(VMEM_BIG),
    )(x, dx1, dq, dkv, dz, dxbc, ddt, norm1, scale1, shift1, w_in)


def _out_stage(ya, ys0, ys1, z0, z1, an, sn0, sn1):
    half = SSM_W // 2
    a = _rms(ya, an, ATTN_W)
    g0 = _rms(ys0 * _silu(z0), sn0, half)
    g1 = _rms(ys1 * _silu(z1), sn1, half)
    return jnp.concatenate([a, g0, g1], axis=1)


def _out_stage_args(ya_ref, ys_ref, z_ref, an_ref, sn_ref):
    half = SSM_W // 2
    return (ya_ref[...], ys_ref[:, :half], ys_ref[:, half:], z_ref[:, :half], z_ref[:, half:],
            an_ref[...], sn_ref[:, :half], sn_ref[:, half:])


def _out_proj_fwd(x, ya, ys, z, an, sn, gate1, w_o, tm, xchg):
    S = x.shape[0]

    def body(x_ref, ya_ref, ys_ref, z_ref, an_ref, sn_ref, g_ref, w_ref, x1_ref):
        u = _out_stage(*_out_stage_args(ya_ref, ys_ref, z_ref, an_ref, sn_ref))
        x1_ref[...] = x_ref[...] + g_ref[...] * _mm(u, w_ref[...])

    half = pl.BlockSpec((tm, ATTN_W), _row)
    hvec = pl.BlockSpec((1, ATTN_W), _fixed)
    (x1,), x_out = _hosted_call(
        body, "out_proj_fwd", S // tm,
        in_specs=[pl.BlockSpec((tm, D_MODEL), _row), half, half, half, hvec, hvec,
                  pl.BlockSpec((1, D_MODEL), _fixed), pl.BlockSpec((D_MODEL, D_MODEL), _fixed)],
        out_specs=[pl.BlockSpec((tm, D_MODEL), _row)],
        out_shape=[jax.ShapeDtypeStruct((S, D_MODEL), F32)],
        scratch_shapes=[], args=(x, ya, ys, z, an, sn, gate1, w_o), xchg=xchg, cparams=_cparams(VMEM_BIG),
    )
    return x1, x_out


def _out_proj_bwd(dx1, ya, ys, z, an, sn, gate1, w_o, tm, xchg):
    S = dx1.shape[0]
    n_steps = S // tm

    def body(dx1_ref, ya_ref, ys_ref, z_ref, an_ref, sn_ref, g_ref, w_ref,
             dya_ref, dys_ref, dz_ref, gw_ref, acc_ref, gw_acc):
        i = pl.program_id(0)

        @pl.when(i == 0)
        def _():
            acc_ref[...] = jnp.zeros_like(acc_ref)
            gw_acc[...] = jnp.zeros_like(gw_acc)

        u, vjp = jax.vjp(_out_stage, *_out_stage_args(ya_ref, ys_ref, z_ref, an_ref, sn_ref))
        dx1 = dx1_ref[...]
        ub = u.astype(MXU_DTYPE)
        mix = _mm(ub, w_ref[...])
        dmix = dx1 * g_ref[...]
        dmixb = dmix.astype(MXU_DTYPE)
        du = _mm_nt(dmixb, w_ref[...])
        gw_acc[...] += _mm_tn(ub, dmixb)
        dya, dys0, dys1, dz0, dz1, dan, dsn0, dsn1 = vjp(du)
        dya_ref[...] = dya
        dys_ref[...] = jnp.concatenate([dys0, dys1], axis=1)
        dz_ref[...] = jnp.concatenate([dz0, dz1], axis=1).astype(dz_ref.dtype)
        acc_ref[0:1, :] += jnp.sum(dx1 * mix, axis=0, keepdims=True)
        acc_ref[1:2, :] += jnp.concatenate([dan, dsn0, dsn1], axis=1)

        @pl.when(i == n_steps - 1)
        def _():
            gw_ref[...] = gw_acc[...].astype(gw_ref.dtype)

    half = pl.BlockSpec((tm, ATTN_W), _row)
    hvec = pl.BlockSpec((1, ATTN_W), _fixed)
    full = pl.BlockSpec((tm, D_MODEL), _row)
    return _hosted_call(
        body, "out_proj_bwd", n_steps,
        in_specs=[full, half, half, half, hvec, hvec,
                  pl.BlockSpec((1, D_MODEL), _fixed), pl.BlockSpec((D_MODEL, D_MODEL), _fixed)],
        out_specs=[half, half, half, pl.BlockSpec((D_MODEL, D_MODEL), _fixed), pl.BlockSpec((8, D_MODEL), _fixed)],
        out_shape=[jax.ShapeDtypeStruct((S, ATTN_W), F32)] * 2 + [jax.ShapeDtypeStruct((S, ATTN_W), MXU_DTYPE),
                   jax.ShapeDtypeStruct((D_MODEL, D_MODEL), WIRE_DTYPE), jax.ShapeDtypeStruct((8, D_MODEL), F32)],
        scratch_shapes=[pltpu.VMEM((D_MODEL, D_MODEL), F32)],
        args=(dx1, ya, ys, z, an, sn, gate1, w_o), xchg=xchg, cparams=_cparams(VMEM_BIG),
    )


def _loss_rows(x2, fn, tgt):
    y = _rms(x2, fn, D_MODEL)
    per_row = jnp.sum(jnp.square(y - tgt), axis=1, keepdims=True)
    return jnp.sum(per_row, axis=0, keepdims=True) * (0.5 / D_MODEL)


def _mlp_loss(x1, tgt, norm2, scale2, shift2, gate2, fnorm, w_gu, w_d, tm):
    S = x1.shape[0]
    n_pieces = len(w_gu) + len(w_d)

    def body(*refs):
        x1_ref, t_ref, n_ref, sc_ref, sh_ref, g_ref, fn_ref = refs[:7]
        piece_refs = refs[7:7 + n_pieces]
        dx1_ref, h_ref, dgu_ref, act_ref, dmlp_ref, acc_ref, wgu, wd, wsem = refs[7 + n_pieces:]

        @pl.when(pl.program_id(0) == 0)
        def _():
            acc_ref[...] = jnp.zeros_like(acc_ref)
            copies = []
            for dst, pieces in ((wgu, piece_refs[:len(w_gu)]), (wd, piece_refs[len(w_gu):])):
                shard = sum(p.shape[1] for p in pieces)
                off = 0
                for p in pieces:
                    for j in range(N_DEV):
                        copies.append(pltpu.make_async_copy(p.at[j], dst.at[pl.ds(j * shard + off, p.shape[1])],
                                                            wsem.at[len(copies)]))
                    off += p.shape[1]
            for cp in copies:
                cp.start()
            for cp in copies:
                cp.wait()

        x1 = x1_ref[...]
        gate2 = g_ref[...]
        h, vjp_h = jax.vjp(_modnorm, x1, n_ref[...], sc_ref[...], sh_ref[...])
        hb = h.astype(MXU_DTYPE)
        gu = _mm_nt(hb, wgu[...])
        g, u = gu[:, :D_FF], gu[:, D_FF:]
        sg = jax.nn.sigmoid(g)
        silu_g = g * sg
        act = (silu_g * u).astype(MXU_DTYPE)
        mlp = _mm(act, wd[...])
        x2 = x1 + gate2 * mlp
        loss, vjp_loss = jax.vjp(_loss_rows, x2, fn_ref[...], t_ref[...])
        dx2, dfn, _ = vjp_loss(jnp.ones((1, 1), F32))
        dmlp = (dx2 * gate2).astype(MXU_DTYPE)
        dact = _mm_nt(dmlp, wd[...])
        dg = dact * u * (sg * (1.0 + g * (1.0 - sg)))
        du = dact * silu_g
        dgu = jnp.concatenate([dg, du], axis=1).astype(MXU_DTYPE)
        dh = _mm(dgu, wgu[...])
        dx, dn, dsc, dsh = vjp_h(dh)
        dx1_ref[...] = dx2 + dx
        h_ref[...] = hb
        dgu_ref[...] = dgu
        act_ref[...] = act
        dmlp_ref[...] = dmlp
        acc_ref[0:1, :] += dn
        acc_ref[1:2, :] += dsc
        acc_ref[2:3, :] += dsh
        acc_ref[3:4, :] += jnp.sum(dx2 * mlp, axis=0, keepdims=True)
        acc_ref[4:5, :] += dfn
        acc_ref[5:6, :] += jnp.broadcast_to(loss, (1, D_MODEL))

    full = pl.BlockSpec((tm, D_MODEL), _row)
    vec = pl.BlockSpec((1, D_MODEL), _fixed)
    anyspec = pl.BlockSpec(memory_space=pl.ANY)
    return pl.pallas_call(
        body, name="mlp_loss", grid=(S // tm,),
        in_specs=[full, full, vec, vec, vec, vec, vec] + [anyspec] * n_pieces,
        out_specs=[full, full, pl.BlockSpec((tm, 2 * D_FF), _row), pl.BlockSpec((tm, D_FF), _row), full,
                   pl.BlockSpec((8, D_MODEL), _fixed)],
        out_shape=[jax.ShapeDtypeStruct((S, D_MODEL), F32), jax.ShapeDtypeStruct((S, D_MODEL), MXU_DTYPE),
                   jax.ShapeDtypeStruct((S, 2 * D_FF), MXU_DTYPE), jax.ShapeDtypeStruct((S, D_FF), MXU_DTYPE),
                   jax.ShapeDtypeStruct((S, D_MODEL), MXU_DTYPE), jax.ShapeDtypeStruct((8, D_MODEL), F32)],
        scratch_shapes=[pltpu.VMEM((2 * D_FF, D_MODEL), MXU_DTYPE), pltpu.VMEM((D_FF, D_MODEL), MXU_DTYPE),
                        pltpu.SemaphoreType.DMA((N_DEV * n_pieces,))],
        compiler_params=_cparams(VMEM_BIG),
    )(x1, tgt, norm2, scale2, shift2, gate2, fnorm, *w_gu, *w_d)


def _wgrad(a, g, tk, ts, name, xchg=None):
    pieces = list(a) if isinstance(a, (list, tuple)) else [a]
    S = pieces[0].shape[0]
    K = sum(p.shape[1] for p in pieces)
    assert len(pieces) == 1 or tk == K
    N = g.shape[1]
    ns = S // ts
    n_a = len(pieces)

    def body(*refs):
        a_refs, (g_ref, o_ref, acc_ref) = refs[:n_a], refs[n_a:]
        s = pl.program_id(1)

        @pl.when(s == 0)
        def _():
            acc_ref[...] = jnp.zeros_like(acc_ref)

        a_blk = a_refs[0][...] if n_a == 1 else jnp.concatenate([r[...] for r in a_refs], axis=1)
        acc_ref[...] += _mm_tn(a_blk, g_ref[...])

        @pl.when(s == ns - 1)
        def _():
            o_ref[...] = acc_ref[...].astype(o_ref.dtype)

    if n_a == 1:
        in_specs = [pl.BlockSpec((ts, tk), lambda j, s: (s, j))]
    else:
        in_specs = [pl.BlockSpec((ts, p.shape[1]), lambda j, s: (s, 0)) for p in pieces]
    in_specs.append(pl.BlockSpec((ts, N), lambda j, s: (s, 0)))
    out_spec = pl.BlockSpec((tk, N), lambda j, s: (j, 0))
    out_shape = jax.ShapeDtypeStruct((K, N), WIRE_DTYPE)
    scratch = [pltpu.VMEM((tk, N), F32)]
    args = (*pieces, g)
    if xchg is None:
        return pl.pallas_call(body, name=name, grid=(K // tk, ns), in_specs=in_specs, out_specs=out_spec,
                              out_shape=out_shape, scratch_shapes=scratch, compiler_params=_cparams(VMEM_BIG))(*args)
    (out,), x_out = _hosted_call(body, name, (K // tk, ns), in_specs, [out_spec], [out_shape], scratch, args, xchg,
                                 _cparams(VMEM_BIG))
    return out, x_out


SSD_CHUNKS_PER_STEP = 4
SSD_BWD_CHUNKS_PER_STEP = 4
ATTN_BLOCKS_PER_STEP = 4
MASKED = -1e30
QK_SCALE = HALF ** -0.5


def _attn_bias(buckets, rel_bias):
    def body(bk_ref, relb_ref, out_ref):
        bk = bk_ref[...]
        i = lax.broadcasted_iota(jnp.int32, (BLK, 2 * BLK), 0)
        j = lax.broadcasted_iota(jnp.int32, (BLK, 2 * BLK), 1)
        window = (j > i) & (j <= i + BLK)
        for h in range(N_HEADS):
            acc = jnp.zeros((BLK, 2 * BLK), F32)
            for b in range(N_BUCKETS):
                acc = jnp.where(bk == b, relb_ref[b, h], acc)
            out_ref[0, h] = jnp.where(window, acc, MASKED)
            out_ref[1, h] = jnp.where(window & (j >= BLK), acc, MASKED)

    return pl.pallas_call(
        body, name="attn_bias", out_shape=jax.ShapeDtypeStruct((2, N_HEADS, BLK, 2 * BLK), F32),
        in_specs=[pl.BlockSpec(memory_space=pltpu.VMEM), pl.BlockSpec(memory_space=pltpu.SMEM)],
    )(buckets, rel_bias)


def _attn_fwd(qkv, bias, sinks, xchg):
    S = qkv.shape[0]
    nb = S // BLK

    nq = ATTN_BLOCKS_PER_STEP if nb % ATTN_BLOCKS_PER_STEP == 0 else 1
    rows = nq * BLK

    def body(q_ref, kvp_ref, kvc_ref, bias_ref, sinks_ref, y_ref):
        i = pl.program_id(0)
        q = q_ref[...].astype(F32) * QK_SCALE
        kv = jnp.concatenate([kvp_ref[...], kvc_ref[...]], axis=0).astype(F32)
        k_lo, k_hi = _split_pair(kv[:, :LANE])
        v_lo, v_hi = _split_pair(kv[:, LANE:])
        bands = [[t[b * BLK:(b + 2) * BLK].astype(MXU_DTYPE) for t in (k_lo, k_hi, v_lo, v_hi)] for b in range(nq)]
        q_heads = [_split_heads(q[b * BLK:(b + 1) * BLK], 4) for b in range(nq)]
        first = [jnp.where(i == 0, 1, 0) if b == 0 else 0 for b in range(nq)]
        items = [(b, h) for b in range(nq) for h in range(N_HEADS)]
        s = [_mm_nt(q_heads[b][h].astype(MXU_DTYPE), bands[b][h // 4]) + bias_ref[first[b], h] for b, h in items]
        m = [jnp.maximum(jnp.max(s[n], axis=-1, keepdims=True), sinks_ref[h]) for n, (b, h) in enumerate(items)]
        p = [jnp.exp(s[n] - m[n]) for n in range(len(items))]
        rinv = [1.0 / (jnp.sum(p[n], axis=-1, keepdims=True) + jnp.exp(sinks_ref[h] - m[n]))
                for n, (b, h) in enumerate(items)]
        out = [_mm(p[n], bands[b][2 + h // 4]) * rinv[n] for n, (b, h) in enumerate(items)]
        y_ref[...] = jnp.concatenate([_join_heads(out[b * N_HEADS:(b + 1) * N_HEADS]) for b in range(nq)], axis=0)

    smem = pl.BlockSpec(memory_space=pltpu.SMEM)
    return _hosted_call(
        body, "attn_fwd", nb // nq,
        in_specs=[pl.BlockSpec((rows, ATTN_W), _row),
                  pl.BlockSpec((BLK, 2 * KV_W), lambda i: (jnp.maximum(i * nq - 1, 0), 2)),
                  pl.BlockSpec((rows, 2 * KV_W), lambda i: (i, 2)),
                  pl.BlockSpec((2, N_HEADS, BLK, 2 * BLK), lambda i: (0, 0, 0, 0)), smem],
        out_specs=[pl.BlockSpec((rows, ATTN_W), _row)],
        out_shape=[jax.ShapeDtypeStruct((S, ATTN_W), F32)],
        scratch_shapes=[],
        args=(qkv, qkv, qkv, bias, sinks), xchg=xchg, cparams=_cparams(),
    )


def _attn_bwd(qkv, y, dy, bias, sinks, xchg):
    S = qkv.shape[0]
    nb = S // BLK
    nq = ATTN_BLOCKS_PER_STEP if nb % ATTN_BLOCKS_PER_STEP == 0 else 1
    rows, n_steps = nq * BLK, nb // nq

    def body(q_ref, kvp_ref, kvc_ref, y_ref, dy_ref, bias_ref, sinks_ref, dq_ref, dkv_ref, dbias_ref, dsk_ref, carry_ref):
        i = pl.program_id(0)

        @pl.when(i == 0)
        def _():
            dbias_ref[...] = jnp.zeros_like(dbias_ref)
            dsk_ref[...] = jnp.zeros_like(dsk_ref)
            carry_ref[...] = jnp.zeros_like(carry_ref)

        q = q_ref[...].astype(F32) * QK_SCALE
        kv = jnp.concatenate([kvp_ref[...], kvc_ref[...]], axis=0).astype(F32)
        k_lo, k_hi = _split_pair(kv[:, :LANE])
        v_lo, v_hi = _split_pair(kv[:, LANE:])
        bands = [[t[b * BLK:(b + 2) * BLK].astype(MXU_DTYPE) for t in (k_lo, k_hi, v_lo, v_hi)] for b in range(nq)]
        rows_of = lambda ref, b: ref[b * BLK:(b + 1) * BLK, :]
        first = [jnp.where(i == n_steps - 1, 1, 0) if b == 0 else 0 for b in range(nq)]
        items = [(b, h) for b in range(nq) for h in range(N_HEADS)]
        at = lambda b, h: b * N_HEADS + h
        q_heads = [hd for b in range(nq) for hd in _split_heads(q[b * BLK:(b + 1) * BLK], 4)]
        y_heads = [hd for b in range(nq) for hd in _split_heads(rows_of(y_ref, b), 4)]
        dy_heads = [hd for b in range(nq) for hd in _split_heads(rows_of(dy_ref, b), 4)]
        qs = [q_heads[n].astype(MXU_DTYPE) for n in range(len(items))]
        s = [_mm_nt(qs[at(b, h)], bands[b][h // 4]) + bias_ref[first[b], h] for b, h in items]
        m = [jnp.maximum(jnp.max(s[at(b, h)], axis=-1, keepdims=True), sinks_ref[h]) for b, h in items]
        p = [jnp.exp(s[n] - m[n]) for n in range(len(items))]
        esink = [jnp.exp(sinks_ref[h] - m[at(b, h)]) for b, h in items]
        rinv = [1.0 / (jnp.sum(p[n], axis=-1, keepdims=True) + esink[n]) for n in range(len(items))]
        t = [dy_heads[n] * rinv[n] for n in range(len(items))]
        delta = [jnp.sum(t[n] * y_heads[n], axis=-1, keepdims=True) for n in range(len(items))]
        tb = [t[n].astype(MXU_DTYPE) for n in range(len(items))]
        dp = [_mm_nt(tb[at(b, h)], bands[b][2 + h // 4]) for b, h in items]
        ds = [p[n] * (dp[n] - delta[n]) for n in range(len(items))]
        for h in range(N_HEADS):
            ds_h, dsk_h = ds[at(0, h)], esink[at(0, h)] * delta[at(0, h)]
            for b in range(1, nq):
                ds_h = ds_h + ds[at(b, h)]
                dsk_h = dsk_h + esink[at(b, h)] * delta[at(b, h)]
            dbias_ref[h] += ds_h
            dsk_ref[h] -= dsk_h
        dsb = [ds[n].astype(MXU_DTYPE) for n in range(len(items))]
        pb = [p[n].astype(MXU_DTYPE) for n in range(len(items))]
        dq_heads = [_mm(dsb[at(b, h)], bands[b][h // 4]) * QK_SCALE for b, h in items]
        grp = lambda lst, b, g: jnp.concatenate(lst[at(b, 4 * g):at(b, 4 * g) + 4], axis=0)
        dk_pads = [[_mm_tn(grp(dsb, b, g), grp(qs, b, g)) for g in range(2)] for b in range(nq)]
        dv_pads = [[_mm_tn(grp(pb, b, g), grp(tb, b, g)) for g in range(2)] for b in range(nq)]
        dq_ref[...] = jnp.concatenate([_join_heads(dq_heads[b * N_HEADS:(b + 1) * N_HEADS]) for b in range(nq)],
                                      axis=0).astype(dq_ref.dtype)
        part = lambda b, lo: jnp.concatenate(
            [_join_pair(d[b][0][lo:lo + BLK], d[b][1][lo:lo + BLK]) for d in (dk_pads, dv_pads)], axis=1)
        dkv = [part(b, BLK) + (part(b + 1, 0) if b + 1 < nq else carry_ref[...]) for b in range(nq)]
        dkv_ref[...] = jnp.concatenate(dkv, axis=0).astype(dkv_ref.dtype)
        carry_ref[...] = part(0, 0)

    smem = pl.BlockSpec(memory_space=pltpu.SMEM)
    rev = lambda i: (n_steps - 1 - i, 0)
    return _hosted_call(
        body, "attn_bwd", n_steps,
        in_specs=[pl.BlockSpec((rows, ATTN_W), rev),
                  pl.BlockSpec((BLK, 2 * KV_W), lambda i: (jnp.maximum((n_steps - 1 - i) * nq - 1, 0), 2)),
                  pl.BlockSpec((rows, 2 * KV_W), lambda i: (n_steps - 1 - i, 2)),
                  pl.BlockSpec((rows, ATTN_W), rev), pl.BlockSpec((rows, ATTN_W), rev),
                  pl.BlockSpec((2, N_HEADS, BLK, 2 * BLK), lambda i: (0, 0, 0, 0)), smem],
        out_specs=[pl.BlockSpec((rows, ATTN_W), rev), pl.BlockSpec((rows, 2 * KV_W), rev),
                   pl.BlockSpec((N_HEADS, BLK, 2 * BLK), lambda i: (0, 0, 0)),
                   pl.BlockSpec((N_HEADS, BLK, 1), lambda i: (0, 0, 0))],
        out_shape=[jax.ShapeDtypeStruct((S, ATTN_W), MXU_DTYPE), jax.ShapeDtypeStruct((S, 2 * KV_W), MXU_DTYPE),
                   jax.ShapeDtypeStruct((N_HEADS, BLK, 2 * BLK), F32), jax.ShapeDtypeStruct((N_HEADS, BLK, 1), F32)],
        scratch_shapes=[pltpu.VMEM((BLK, 2 * KV_W), F32)],
        args=(qkv, qkv, qkv, y, dy, bias, sinks), xchg=xchg, cparams=_cparams(),
    )


def _attn_finish(dbias, dsk, buckets):
    def body(db_ref, dsk_ref, bk_ref, drel_ref, dsink_ref):
        bk = bk_ref[...]
        r = lax.broadcasted_iota(jnp.int32, (N_BUCKETS, LANE), 0)
        l = lax.broadcasted_iota(jnp.int32, (N_BUCKETS, LANE), 1)
        row = lax.broadcasted_iota(jnp.int32, (N_HEADS, LANE), 0)
        res = jnp.zeros((N_BUCKETS, LANE), F32)
        dsink = jnp.zeros((N_HEADS, LANE), F32)
        for h in range(N_HEADS):
            db = db_ref[h]
            for b in range(N_BUCKETS):
                v = jnp.sum(jnp.sum(jnp.where(bk == b, db, 0.0), axis=1, keepdims=True), axis=0, keepdims=True)
                res = res + jnp.where((r == b) & (l == h), v, 0.0)
            dsink = dsink + jnp.where(row == h, jnp.sum(dsk_ref[h], axis=0, keepdims=True), 0.0)
        drel_ref[...] = res
        dsink_ref[...] = dsink

    return pl.pallas_call(body, name="attn_finish",
                          out_shape=[jax.ShapeDtypeStruct((N_BUCKETS, LANE), F32),
                                     jax.ShapeDtypeStruct((N_HEADS, LANE), F32)])(dbias, dsk, buckets)


def _ssd_consts():
    r = lax.broadcasted_iota(jnp.int32, (BLK, BLK), 0)
    c = lax.broadcasted_iota(jnp.int32, (BLK, BLK), 1)
    causal = c <= r
    upper = (r <= c).astype(F32)
    last = r == BLK - 1
    head = lax.broadcasted_iota(jnp.int32, (N_HEADS, BLK), 0)
    return causal, upper, last, head


def _ssd_chunks(xs, bg, cg, dt_raw_t, prev0, dtb, alog, d_rows, consts):
    causal, upper, last, head = consts
    nq = len(xs)
    items = [(c, h) for c in range(nq) for h in range(N_HEADS)]
    at = lambda c, h: c * N_HEADS + h
    a_neg = -jnp.exp(alog)
    dt_t = [_softplus(dt_raw_t[c] + dtb) for c in range(nq)]
    acs_t = [_mm_hi(dt_t[c] * a_neg, upper) for c in range(nq)]
    cb = [[_mm_nt(cg[c][g], bg[c][g]) for g in range(2)] for c in range(nq)]
    pick = lambda t, h: jnp.sum(jnp.where(head == h, t, 0.0), axis=0, keepdims=True)
    dt_row = [pick(dt_t[c], h) for c, h in items]
    a_row = [pick(acs_t[c], h) for c, h in items]
    a_rb = [jnp.broadcast_to(a_row[n], (BLK, BLK)) for n in range(len(items))]
    a_b = [a_rb[n].T for n in range(len(items))]
    a_last = [jnp.sum(jnp.where(last, a_b[n], 0.0), axis=0, keepdims=True) for n in range(len(items))]
    w = [cb[c][h // 4] * jnp.exp(jnp.where(causal, a_b[at(c, h)] - a_rb[at(c, h)], -1e30)) * dt_row[at(c, h)]
         for c, h in items]
    f_b = [jnp.broadcast_to(dt_row[n] * jnp.exp(a_last[n] - a_row[n]), (BLK, BLK)).T for n in range(len(items))]
    y_in = [_mm(w[at(c, h)], xs[c][h]) for c, h in items]
    st = [_mm_tn(bg[c][h // 4], xs[c][h] * f_b[at(c, h)]) for c, h in items]
    e_b = [jnp.exp(a_b[n]) for n in range(len(items))]
    states = [list(prev0)]
    for c in range(nq):
        states.append([states[c][h] * jnp.exp(a_last[at(c, h)]) + st[at(c, h)] for h in range(N_HEADS)])
    y_off = [_mm(cg[c][h // 4], states[c][h]) * e_b[at(c, h)] for c, h in items]
    ys = [[y_in[at(c, h)] + y_off[at(c, h)] + d_rows[h] * xs[c][h] for h in range(N_HEADS)] for c in range(nq)]
    return ys, states


def _ssd_chunks_bwd(xs, bg, cg, dt_raw_t, prev, dtb, alog, d_rows, dys, dh_last, consts):
    causal, upper, last, head = consts
    nq = len(xs)
    items = [(c, h) for c in range(nq) for h in range(N_HEADS)]
    ni = len(items)
    at = lambda c, h: c * N_HEADS + h
    groups = [(c, g) for c in range(nq) for g in range(2)]
    lane = _lane_iota((BLK, BLK))
    lane_row = _lane_iota((1, BLK))
    a_neg = -jnp.exp(alog)
    pre_dt = [dt_raw_t[c] + dtb for c in range(nq)]
    dt_t = [_softplus(pre_dt[c]) for c in range(nq)]
    acs_t = [_mm_hi(dt_t[c] * a_neg, upper) for c in range(nq)]
    pick = lambda t, h: jnp.sum(jnp.where(head == h, t, 0.0), axis=0, keepdims=True)
    full_sum = lambda t: jnp.sum(jnp.sum(t, axis=1, keepdims=True), axis=0, keepdims=True)
    dt_row = [pick(dt_t[c], h) for c, h in items]
    a_row = [pick(acs_t[c], h) for c, h in items]
    a_rb = [jnp.broadcast_to(a_row[n], (BLK, BLK)) for n in range(ni)]
    a_b = [a_rb[n].T for n in range(ni)]
    a_last = [jnp.sum(jnp.where(last, a_b[n], 0.0), axis=0, keepdims=True) for n in range(ni)]
    lm = [jnp.exp(jnp.where(causal, a_b[n] - a_rb[n], -1e30)) for n in range(ni)]
    cgb = [[cg[c][g].astype(MXU_DTYPE) for g in range(2)] for c in range(nq)]
    bgb = [[bg[c][g].astype(MXU_DTYPE) for g in range(2)] for c in range(nq)]
    cb = [[_mm_nt(cgb[c][g], bgb[c][g]) for g in range(2)] for c in range(nq)]
    u = [cb[c][h // 4] * lm[at(c, h)] for c, h in items]
    w = [(u[n] * dt_row[n]).astype(MXU_DTYPE) for n in range(ni)]
    e_row = [jnp.exp(a_last[n] - a_row[n]) for n in range(ni)]
    f_row = [dt_row[n] * e_row[n] for n in range(ni)]
    f_b = [jnp.broadcast_to(f_row[n], (BLK, BLK)).T for n in range(ni)]
    e_b = [jnp.exp(a_b[n]) for n in range(ni)]
    el = [jnp.exp(a_last[n]) for n in range(ni)]
    xb = [xs[c][h].astype(MXU_DTYPE) for c, h in items]
    dyb = [dys[c][h].astype(MXU_DTYPE) for c, h in items]
    prevb = [prev[c][h].astype(MXU_DTYPE) for c, h in items]
    gmat = [_mm(cgb[c][h // 4], prevb[at(c, h)]) for c, h in items]
    dw = [_mm_nt(dyb[n], xb[n]) for n in range(ni)]
    dg = [dys[c][h] * e_b[at(c, h)] for c, h in items]
    dgb = [dg[n].astype(MXU_DTYPE) for n in range(ni)]
    from_y = [_mm_tn(cgb[c][h // 4], dgb[at(c, h)]) for c, h in items]
    dhs = [None] * ni
    dprev = [None] * ni
    for c in reversed(range(nq)):
        for h in range(N_HEADS):
            dhs[at(c, h)] = dh_last[h] if c == nq - 1 else dprev[at(c + 1, h)]
            dprev[at(c, h)] = from_y[at(c, h)] + dhs[at(c, h)] * el[at(c, h)]
    dstb = [dhs[n].astype(MXU_DTYPE) for n in range(ni)]
    dxf = [_mm(bgb[c][h // 4], dstb[at(c, h)]) for c, h in items]
    xfb = [(xs[c][h] * f_b[at(c, h)]).astype(MXU_DTYPE) for c, h in items]
    dxs = [_mm_tn(w[at(c, h)], dyb[at(c, h)]) + d_rows[h] * dys[c][h] + f_b[at(c, h)] * dxf[at(c, h)]
           for c, h in items]
    dd_item = [jnp.sum(dys[c][h] * xs[c][h], axis=0, keepdims=True) for c, h in items]
    dcg_h = [_mm_nt(dgb[n], prevb[n]) for n in range(ni)]
    dbg_h = [_mm_nt(xfb[n], dstb[n]) for n in range(ni)]
    zt = [dw[n] * u[n] for n in range(ni)]
    dseg = [zt[n] * dt_row[n] for n in range(ni)]
    dcb_h = [dw[n] * lm[n] * dt_row[n] for n in range(ni)]
    four = lambda lst, c, g: lst[at(c, 4 * g)] + lst[at(c, 4 * g + 1)] + lst[at(c, 4 * g + 2)] + lst[at(c, 4 * g + 3)]
    dcb = {(c, g): four(dcb_h, c, g).astype(MXU_DTYPE) for c, g in groups}
    dcg = [[four(dcg_h, c, g) + _mm(dcb[c, g], bgb[c][g]) for g in range(2)] for c in range(nq)]
    dbg = [[four(dbg_h, c, g) + _mm_tn(dcb[c, g], cgb[c][g]) for g in range(2)] for c in range(nq)]
    r1 = [jnp.sum(dg[n] * gmat[n] + dseg[n], axis=1, keepdims=True) for n in range(ni)]
    r2 = [jnp.sum(dxf[at(c, h)] * xs[c][h], axis=1, keepdims=True) for c, h in items]
    tt = [jnp.where(lane < HALF, jnp.broadcast_to(r1[n], (BLK, BLK)), jnp.broadcast_to(r2[n], (BLK, BLK))).T
          for n in range(ni)]
    r1_row = [tt[n][0:1, :] for n in range(ni)]
    r2_row = [tt[n][HALF:HALF + 1, :] for n in range(ni)]
    d_el = [full_sum(dhs[at(c, h)] * prev[c][h]) for c, h in items]
    da_last = [jnp.sum(r2_row[n] * f_row[n], axis=1, keepdims=True) + el[n] * d_el[n] for n in range(ni)]
    da_row = [r1_row[n] - jnp.sum(dseg[n], axis=0, keepdims=True) - r2_row[n] * f_row[n]
              + jnp.where(lane_row == BLK - 1, da_last[n], 0.0) for n in range(ni)]
    ddt_row = [jnp.sum(zt[n], axis=0, keepdims=True) + r2_row[n] * e_row[n] for n in range(ni)]
    draw, dalog = [], jnp.zeros((N_HEADS, BLK), F32)
    for c in range(nq):
        da_t = jnp.zeros((N_HEADS, BLK), F32)
        ddt_t = jnp.zeros((N_HEADS, BLK), F32)
        for h in range(N_HEADS):
            da_t = jnp.where(head == h, da_row[at(c, h)], da_t)
            ddt_t = jnp.where(head == h, ddt_row[at(c, h)], ddt_t)
        d_dta = _mm_hi(da_t, causal.astype(F32))
        dalog = dalog + d_dta * dt_t[c] * a_neg
        draw.append((ddt_t + d_dta * a_neg) * jax.nn.sigmoid(pre_dt[c]))
    ddtb = draw[0]
    for c in range(1, nq):
        ddtb = ddtb + draw[c]
    dd_rows = []
    for h in range(N_HEADS):
        t = dd_item[at(0, h)]
        for c in range(1, nq):
            t = t + dd_item[at(c, h)]
        dd_rows.append(t)
    return ([dxs[c * N_HEADS:(c + 1) * N_HEADS] for c in range(nq)], dbg, dcg, draw,
            [dprev[at(0, h)] for h in range(N_HEADS)], ddtb, dalog, dd_rows)


def _dt_rows(dt_blk):
    return dt_blk.T[:N_HEADS]


def _conv_pre(halo, blk, cw_ref, cb_ref):
    ext = jnp.concatenate([halo, blk], axis=0)
    taps = [pltpu.roll(ext, 3 - k, 0)[8:] for k in range(3)] + [blk]
    pre = cb_ref[...] + cw_ref[0:1, :] * taps[0]
    for k in range(1, 4):
        pre = pre + cw_ref[k:k + 1, :] * taps[k]
    return pre


def _ssd_split(pre):
    heads = _split_heads(pre[:, :SSM_W], 4)
    pb = [pre[:, SSM_W + g * D_STATE:SSM_W + (g + 1) * D_STATE] for g in range(2)]
    pc = [pre[:, SSM_W + 2 * D_STATE + g * D_STATE:SSM_W + 2 * D_STATE + (g + 1) * D_STATE] for g in range(2)]
    return heads, pb, pc


def _ssd_fwd(xbc, dt_raw, conv_w, conv_b, dtb_row, alog_row, d_exp, xchg):
    S = xbc.shape[0]
    nc = S // BLK
    nq = SSD_CHUNKS_PER_STEP if nc % SSD_CHUNKS_PER_STEP == 0 else 1
    rows = nq * BLK

    def body(xbc_ref, halo_ref, dt_ref, cw_ref, cb_ref, dtb_ref, alog_ref, d_ref, y_ref, prev_ref, pre_ref, state_ref):
        i = pl.program_id(0)

        @pl.when(i == 0)
        def _():
            state_ref[...] = jnp.zeros_like(state_ref)

        halo = halo_ref[...] * jnp.where(i > 0, 1.0, 0.0)
        pre = _conv_pre(halo, xbc_ref[...], cw_ref, cb_ref)
        pre_ref[...] = pre
        xc = _silu(pre)
        split = [_ssd_split(xc[c * BLK:(c + 1) * BLK]) for c in range(nq)]
        dt_t = [_dt_rows(dt_ref[c * BLK:(c + 1) * BLK, :]) for c in range(nq)]
        prev0 = [state_ref[h] for h in range(N_HEADS)]
        d_rows = [d_ref[h:h + 1, :] for h in range(N_HEADS)]
        ys, states = _ssd_chunks([s[0] for s in split], [s[1] for s in split], [s[2] for s in split], dt_t, prev0,
                                 dtb_ref[...], alog_ref[...], d_rows, _ssd_consts())
        for h in range(N_HEADS):
            for c in range(nq):
                prev_ref[c, h] = states[c][h]
            state_ref[h] = states[nq][h]
        y_ref[...] = jnp.concatenate([_join_heads(ys[c]) for c in range(nq)], axis=0)

    vec = pl.BlockSpec((N_HEADS, LANE), _fixed)
    return _hosted_call(
        body, "ssd_fwd", nc // nq,
        in_specs=[pl.BlockSpec((rows, XBC_W), _row),
                  pl.BlockSpec((8, XBC_W), lambda i: (jnp.maximum(i * (rows // 8) - 1, 0), 0)),
                  pl.BlockSpec((rows, LANE), _row),
                  pl.BlockSpec((4, XBC_W), _fixed), pl.BlockSpec((1, XBC_W), _fixed), vec, vec,
                  pl.BlockSpec((N_HEADS, LANE), _fixed)],
        out_specs=[pl.BlockSpec((rows, SSM_W), _row),
                   pl.BlockSpec((nq, N_HEADS, D_STATE, LANE), lambda i: (i, 0, 0, 0)),
                   pl.BlockSpec((rows, XBC_W), _row)],
        out_shape=[jax.ShapeDtypeStruct((S, SSM_W), F32), jax.ShapeDtypeStruct((nc, N_HEADS, D_STATE, LANE), F32),
                   jax.ShapeDtypeStruct((S, XBC_W), F32)],
        scratch_shapes=[pltpu.VMEM((N_HEADS, D_STATE, LANE), F32)],
        args=(xbc, xbc, dt_raw, conv_w, conv_b, dtb_row, alog_row, d_exp), xchg=xchg, cparams=_cparams(),
    )


def _ssd_bwd(xbc, pre_act, dt_raw, prev_states, dy, conv_w, dtb_row, alog_row, d_exp, xchg):
    S = xbc.shape[0]
    nc = S // BLK
    nq = SSD_BWD_CHUNKS_PER_STEP if nc % SSD_BWD_CHUNKS_PER_STEP == 0 else 1
    rows, n_steps = nq * BLK, nc // nq

    def body(xbc_ref, halo_ref, pre_ref, dt_ref, prev_ref, dy_ref, cw_ref, dtb_ref, alog_ref, d_ref,
             dxbc_ref, ddt_ref, dcw_ref, dvec_ref, dd_ref, gstate_ref, ghalo_ref):
        i = pl.program_id(0)

        @pl.when(i == 0)
        def _():
            gstate_ref[...] = jnp.zeros_like(gstate_ref)
            ghalo_ref[...] = jnp.zeros_like(ghalo_ref)
            dcw_ref[...] = jnp.zeros_like(dcw_ref)
            dvec_ref[...] = jnp.zeros_like(dvec_ref)
            dd_ref[...] = jnp.zeros_like(dd_ref)

        halo = halo_ref[...] * jnp.where(i < n_steps - 1, 1.0, 0.0)
        ext = jnp.concatenate([halo, xbc_ref[...]], axis=0)
        pre = pre_ref[...]
        sig = jax.nn.sigmoid(pre)
        xc = pre * sig
        split = [_ssd_split(xc[c * BLK:(c + 1) * BLK]) for c in range(nq)]
        dt_t = [_dt_rows(dt_ref[c * BLK:(c + 1) * BLK, :]) for c in range(nq)]
        prev = [[prev_ref[c, h] for h in range(N_HEADS)] for c in range(nq)]
        d_rows = [d_ref[h:h + 1, :] for h in range(N_HEADS)]
        dys = [_split_heads(dy_ref[c * BLK:(c + 1) * BLK, :], 4) for c in range(nq)]
        dh_last = [gstate_ref[h] for h in range(N_HEADS)]
        dheads, dpb, dpc, ddt_t, dprev0, ddtb, dalog, dd_rows = _ssd_chunks_bwd(
            [s[0] for s in split], [s[1] for s in split], [s[2] for s in split], dt_t, prev, dtb_ref[...],
            alog_ref[...], d_rows, dys, dh_last, _ssd_consts())
        for h in range(N_HEADS):
            gstate_ref[h] = dprev0[h]
            dd_ref[h:h + 1, :] += dd_rows[h]
        pad = jnp.zeros((BLK - N_HEADS, BLK), F32)
        ddt_ref[...] = jnp.concatenate([jnp.concatenate([ddt_t[c], pad], axis=0).T for c in range(nq)],
                                       axis=0).astype(ddt_ref.dtype)
        dvec_ref[0:N_HEADS, :] += ddtb
        dvec_ref[N_HEADS:, :] += dalog
        dxc = jnp.concatenate([jnp.concatenate([_join_heads(dheads[c])] + list(dpb[c]) + list(dpc[c]), axis=1)
                               for c in range(nq)], axis=0)
        dpre = dxc * (sig * (1.0 + pre * (1.0 - sig)))
        zeros8 = jnp.zeros((8, XBC_W), F32)
        dpe = jnp.concatenate([zeros8, dpre, zeros8], axis=0)
        n_ext = 16 + rows
        shifted = [pltpu.roll(dpe, n_ext - (3 - k), 0)[:8 + rows] for k in range(3)] + [dpe[:8 + rows]]
        dext = cw_ref[0:1, :] * shifted[0]
        for k in range(1, 4):
            dext = dext + cw_ref[k:k + 1, :] * shifted[k]
        for k in range(4):
            dcw_ref[k:k + 1, :] += jnp.sum(shifted[k] * ext, axis=0, keepdims=True)
        dcw_ref[4:5, :] += jnp.sum(dpre, axis=0, keepdims=True)
        dxbc_ref[...] = jnp.concatenate([dext[8:rows], dext[rows:] + ghalo_ref[...]], axis=0).astype(dxbc_ref.dtype)
        ghalo_ref[...] = dext[:8, :]

    vec = pl.BlockSpec((N_HEADS, LANE), _fixed)
    rev = lambda i: (n_steps - 1 - i, 0)
    return _hosted_call(
        body, "ssd_bwd", n_steps,
        in_specs=[pl.BlockSpec((rows, XBC_W), rev),
                  pl.BlockSpec((8, XBC_W), lambda i: (jnp.maximum((n_steps - 1 - i) * (rows // 8) - 1, 0), 0)),
                  pl.BlockSpec((rows, XBC_W), rev),
                  pl.BlockSpec((rows, LANE), rev),
                  pl.BlockSpec((nq, N_HEADS, D_STATE, LANE), lambda i: (n_steps - 1 - i, 0, 0, 0)),
                  pl.BlockSpec((rows, SSM_W), rev),
                  pl.BlockSpec((4, XBC_W), _fixed), vec, vec,
                  pl.BlockSpec((N_HEADS, LANE), _fixed)],
        out_specs=[pl.BlockSpec((rows, XBC_W), rev), pl.BlockSpec((rows, LANE), rev),
                   pl.BlockSpec((8, XBC_W), _fixed), pl.BlockSpec((2 * N_HEADS, LANE), _fixed),
                   pl.BlockSpec((N_HEADS, LANE), _fixed)],
        out_shape=[jax.ShapeDtypeStruct((S, XBC_W), MXU_DTYPE), jax.ShapeDtypeStruct((S, LANE), MXU_DTYPE),
                   jax.ShapeDtypeStruct((8, XBC_W), F32), jax.ShapeDtypeStruct((2 * N_HEADS, LANE), F32),
                   jax.ShapeDtypeStruct((N_HEADS, LANE), F32)],
        scratch_shapes=[pltpu.VMEM((N_HEADS, D_STATE, LANE), F32), pltpu.VMEM((8, XBC_W), F32)],
        args=(xbc, xbc, pre_act, dt_raw, prev_states, dy, conv_w, dtb_row, alog_row, d_exp), xchg=xchg,
        cparams=_cparams(VMEM_BIG),
    )


def _adamw_math(w, g, m, v):
    m = ADAM_B1 * m + (1.0 - ADAM_B1) * g
    v = ADAM_B2 * v + (1.0 - ADAM_B2) * jnp.square(g)
    m_hat = m / (1.0 - ADAM_B1 ** ADAM_STEP)
    v_hat = v / (1.0 - ADAM_B2 ** ADAM_STEP)
    delta = -ADAM_LR * (m_hat / (jnp.sqrt(v_hat) + ADAM_EPS) + ADAM_WD * w)
    return delta, m, v


def _reduce_adamw_halves(part_a, part_b, w, m, v, name):
    R, C = w.shape
    tl = LANE
    n, n_a = C // tl, part_a.shape[2] // tl
    assert part_a.shape[2] % tl == 0 and part_a.shape[2] + part_b.shape[2] == C

    def body(a_ref, b_ref, w_ref, m_ref, v_ref, g_ref, d_ref, nm_ref, nv_ref):
        ga, gb = a_ref[0].astype(F32), b_ref[0].astype(F32)
        for i in range(1, N_DEV):
            ga, gb = ga + a_ref[i].astype(F32), gb + b_ref[i].astype(F32)
        first = jnp.where(pl.program_id(0) < n_a, 1.0, 0.0)
        g = ga * first + gb * (1.0 - first)
        d, nm, nv = _adamw_math(w_ref[...], g, m_ref[...], v_ref[...])
        g_ref[...] = g
        d_ref[...] = d
        nm_ref[...] = nm
        nv_ref[...] = nv

    blk = pl.BlockSpec((R, tl), lambda i: (0, i))
    return pl.pallas_call(
        body, name=name, grid=(n,),
        in_specs=[pl.BlockSpec((N_DEV, R, tl), lambda i: (0, 0, jnp.minimum(i, n_a - 1))),
                  pl.BlockSpec((N_DEV, R, tl), lambda i: (0, 0, jnp.maximum(i - n_a, 0))), blk, blk, blk],
        out_specs=[blk] * 4, out_shape=[jax.ShapeDtypeStruct((R, C), F32)] * 4,
    )(part_a, part_b, w, m, v)


def _reduce_adamw_hosting(parts_list, wmv_list, name, xchg):
    n_arr = len(parts_list)
    pieces = [list(p) if isinstance(p, (tuple, list)) else [p] for p in parts_list]
    n_pieces = sum(len(p) for p in pieces)
    C = wmv_list[0][0].shape[1]
    tl = 256

    def total(ref):
        g = ref[0].astype(F32)
        for i in range(1, N_DEV):
            g = g + ref[i].astype(F32)
        return g

    def body(*refs):
        p_refs, wmv_refs, o_refs = refs[:n_pieces], refs[n_pieces:n_pieces + 3 * n_arr], refs[n_pieces + 3 * n_arr:]
        at = 0
        for k in range(n_arr):
            sums = [total(r) for r in p_refs[at:at + len(pieces[k])]]
            at += len(pieces[k])
            g = sums[0] if len(sums) == 1 else jnp.concatenate(sums, axis=0)
            w_ref, m_ref, v_ref = wmv_refs[3 * k:3 * k + 3]
            d, nm, nv = _adamw_math(w_ref[...], g, m_ref[...], v_ref[...])
            for o, val in zip(o_refs[4 * k:4 * k + 4], (g, d, nm, nv)):
                o[...] = val

    in_specs = [pl.BlockSpec((N_DEV, p.shape[1], tl), lambda i: (0, 0, i)) for group in pieces for p in group]
    in_specs += [pl.BlockSpec((w.shape[0], tl), lambda i: (0, i)) for w, _, _ in wmv_list for _ in range(3)]
    out_specs = [pl.BlockSpec((w.shape[0], tl), lambda i: (0, i)) for w, _, _ in wmv_list for _ in range(4)]
    out_shape = [jax.ShapeDtypeStruct(w.shape, F32) for w, _, _ in wmv_list for _ in range(4)]
    args = [p for group in pieces for p in group] + [a for wmv in wmv_list for a in wmv]
    outs, x_out = _hosted_call(body, name, C // tl, in_specs, out_specs, out_shape, [], args, xchg,
                               _cparams(VMEM_BIG))
    return [outs[4 * k:4 * k + 4] for k in range(n_arr)], x_out


_SMALL_NAMES = ("ada_b", "norm1", "conv_w", "conv_b", "dt_bias", "A_log", "D_skip", "sinks", "attn_out_norm",
                "ssm_out_norm", "norm2", "rel_bias", "final_norm")
N_MOD = 6 * D_MODEL


def _mod_row(a0, a1, a2):
    return jnp.concatenate([a0[2:3], a0[1:2], a1[0:1], a2[2:3], a2[1:2], a2[3:4]], axis=1)


def _small_update(gathered, params):
    n_g = len(gathered)
    flat = [a for name in _SMALL_NAMES for a in params[name]]

    def body(*refs):
        a0_ref, a1_ref, a2_ref, cw_ref, dv_ref, dd_ref, ds_ref, dr_ref, c_ref = refs[:n_g]
        wmv = refs[n_g:n_g + len(flat)]
        outs = refs[n_g + len(flat):]

        def total(ref):
            t = ref[0]
            for i in range(1, N_DEV):
                t = t + ref[i]
            return t

        t0, t1, t2, tcw, tdv, tdd, tds, tdr = [total(r) for r in (a0_ref, a1_ref, a2_ref, cw_ref, dv_ref, dd_ref,
                                                                   ds_ref, dr_ref)]
        r8 = lax.broadcasted_iota(jnp.int32, (N_HEADS, LANE), 0)
        l8 = lax.broadcasted_iota(jnp.int32, (N_HEADS, LANE), 1)

        def diag_row(t):
            return jnp.sum(jnp.where(r8 == l8, t, 0.0), axis=0, keepdims=True)[:, :N_HEADS]

        def lane_sums(t):
            return diag_row(jnp.broadcast_to(jnp.sum(t, axis=1, keepdims=True), (N_HEADS, LANE)))

        me = _lin(_my_pos())
        n_cw = XBC_W // N_DEV
        cw_mine = jnp.zeros((4, n_cw), F32)
        for j in range(N_DEV):
            cw_mine = cw_mine + tcw[0:4, j * n_cw:(j + 1) * n_cw] * jnp.where(me == j, 1.0, 0.0)
        grads = {
            "ada_b": _mod_row(t0, t1, t2), "norm1": t0[0:1], "conv_w": cw_mine, "conv_b": tcw[4:5],
            "dt_bias": lane_sums(tdv[:N_HEADS]), "A_log": lane_sums(tdv[N_HEADS:]), "D_skip": lane_sums(tdd),
            "sinks": diag_row(tds), "attn_out_norm": t1[1:2, :ATTN_W], "ssm_out_norm": t1[1:2, ATTN_W:],
            "norm2": t2[0:1], "rel_bias": tdr[:, :N_HEADS], "final_norm": t2[4:5],
        }
        for k, name in enumerate(_SMALL_NAMES):
            w_ref, m_ref, v_ref = wmv[3 * k:3 * k + 3]
            g = grads[name]
            d, nm, nv = _adamw_math(w_ref[...], g, m_ref[...], v_ref[...])
            for o, val in zip(outs[4 * k:4 * k + 4], (g, d, nm, nv)):
                o[...] = val
        loss_ref, call_ref, dmod_ref = outs[4 * len(_SMALL_NAMES):]
        loss_ref[...] = t2[5:6, 0:1]
        call_ref[...] = jnp.concatenate([c_ref[i] for i in range(N_DEV)], axis=0)
        dmod_ref[...] = jnp.concatenate([_mod_row(a0_ref[i], a1_ref[i], a2_ref[i]) for i in range(N_DEV)], axis=0)

    out_shape = [jax.ShapeDtypeStruct(params[name][0].shape, F32) for name in _SMALL_NAMES for _ in range(4)]
    out_shape += [jax.ShapeDtypeStruct((1, 1), F32), jax.ShapeDtypeStruct((N_DEV, D_MODEL), F32),
                  jax.ShapeDtypeStruct((N_DEV, N_MOD), F32)]
    res = pl.pallas_call(body, name="small_update", out_shape=out_shape)(*gathered, *flat)
    upd = {name: res[4 * k:4 * k + 4] for k, name in enumerate(_SMALL_NAMES)}
    loss, c_all, dmod_all = res[4 * len(_SMALL_NAMES):]
    return upd, loss, c_all, dmod_all


def _ada_w_update(c_all, dmod_all, w, m, v):
    chunk = w.shape[1]

    def body(c_ref, dm_ref, w_ref, m_ref, v_ref, g_ref, d_ref, nm_ref, nv_ref):
        me = _lin(_my_pos())
        dm = jnp.zeros((N_DEV, chunk), F32)
        for j in range(N_DEV):
            dm = dm + dm_ref[:, j * chunk:(j + 1) * chunk] * jnp.where(me == j, 1.0, 0.0)
        g = lax.dot_general(_silu(c_ref[...]), dm, (((0,), (0,)), ((), ())), precision=HI,
                            preferred_element_type=F32)
        d, nm, nv = _adamw_math(w_ref[...], g, m_ref[...], v_ref[...])
        g_ref[...] = g
        d_ref[...] = d
        nm_ref[...] = nm
        nv_ref[...] = nv

    tr = 256
    blk = pl.BlockSpec((tr, chunk), _row)
    return pl.pallas_call(
        body, name="ada_w_update", grid=(w.shape[0] // tr,),
        in_specs=[pl.BlockSpec((N_DEV, tr), lambda i: (0, i)), pl.BlockSpec(dmod_all.shape, _fixed), blk, blk, blk],
        out_specs=[blk] * 4, out_shape=[jax.ShapeDtypeStruct(w.shape, F32)] * 4,
    )(c_all, dmod_all, w, m, v)


def _local_step(x, tgt, c, mod, w_in, conv_w, w_o_mine, w_gu_mine, w_d_mine, p):
    S = x.shape[0]
    tm = min(512, S)
    tmm = min(256, S)
    tw = min(2048, S)
    shift1, scale1, gate1, shift2, scale2, gate2 = [mod[i:i + 1] for i in range(6)]
    buckets = jnp.asarray(_t5_bucket_table())
    per_head = lambda a: jnp.broadcast_to(a.reshape(N_HEADS, 1), (N_HEADS, LANE))
    dtb_row, alog_row, d_exp = per_head(p["dt_bias"]), per_head(p["A_log"]), per_head(p["D_skip"])
    sinks = p["sinks"].reshape(N_HEADS)

    d_cut, gu_cut = WD_CUT, WGU_CUTS
    n_d, n_gu = w_d_mine.shape[0], w_gu_mine.shape[0]
    (qkv, z, xbc, dt_raw), (g_d_a,) = _in_proj_fwd(x, p["norm1"], scale1, shift1, w_in, tm,
                                                   ([(w_d_mine, 0, d_cut)], "two-level"))
    bias = _attn_bias(buckets, p["rel_bias"])
    (ya,), (g_gu_a,) = _attn_fwd(qkv, bias, sinks, ([(w_gu_mine, 0, gu_cut[0])], "two-level"))
    (ys, prev_states, pre_act), (g_gu_b, g_o) = _ssd_fwd(
        xbc, dt_raw, conv_w, p["conv_b"], dtb_row, alog_row, d_exp,
        ([(w_gu_mine, gu_cut[0], gu_cut[1] - gu_cut[0]), w_o_mine], "two-level"))
    w_o = g_o.reshape(D_MODEL, D_MODEL)
    x1, (g_gu_c, g_d_b) = _out_proj_fwd(
        x, ya, ys, z, p["attn_out_norm"], p["ssm_out_norm"], gate1, w_o, tm,
        ([(w_gu_mine, gu_cut[1], n_gu - gu_cut[1]), (w_d_mine, d_cut, n_d - d_cut)], "two-level"))
    dx1, h2, dgu, act, dmlp, acc2 = _mlp_loss(x1, tgt, p["norm2"], scale2, shift2, gate2, p["final_norm"],
                                              (g_gu_a, g_gu_b, g_gu_c), (g_d_a, g_d_b), tmm)
    g_w_gu = _wgrad(dgu, h2, 2 * D_FF // 4, tw, "wgrad_gate_up")
    g_w_d = _wgrad(act, dmlp, D_FF // 2, tw, "wgrad_down")
    gu_slots = g_w_gu.reshape(N_DEV, 2 * D_FF // N_DEV, D_MODEL)
    (dya, dys, dz, g_w_o, acc1), (r_gu_a,) = _out_proj_bwd(
        dx1, ya, ys, z, p["attn_out_norm"], p["ssm_out_norm"], gate1, w_o, tm, ([gu_slots], ("rows", 0, GGU_CUT)))
    (dq, dkv, dbias, dsk), (r_d, r_o) = _attn_bwd(
        qkv, ya, dya, bias, sinks,
        ([g_w_d.reshape(N_DEV, D_FF // N_DEV, D_MODEL), g_w_o.reshape(N_DEV, D_MODEL // N_DEV, D_MODEL)], True))
    drel, dsink = _attn_finish(dbias, dsk, buckets)
    (dxbc, ddt, dcw, dvec, dd), (r_gu_b, *early) = _ssd_bwd(
        xbc, pre_act, dt_raw, prev_states, dys, conv_w, dtb_row, alog_row, d_exp,
        [([gu_slots], ("rows", GGU_CUT, 2 * D_FF // N_DEV - GGU_CUT)), ([acc1, acc2, dsink, drel, c], False)])
    r_gu = (r_gu_a, r_gu_b)
    gx, h1_b, acc0, g_in_a = _in_proj_bwd(x, dx1, dq, dkv, dz, dxbc, ddt, p["norm1"], scale1, shift1, w_in, tm)
    slots = lambda g: g[:IN_W].reshape(N_DEV, IN_W // N_DEV, g.shape[1])
    g_in_b, (r_in_a, *late) = _wgrad((dq, dkv, dz, dxbc, ddt), h1_b, IN_PAD, tw, "wgrad_in_b",
                                     [([slots(g_in_a)], True), ([acc0, dcw, dvec, dd], False)])
    gathered = (late[0], early[0], early[1], late[1], late[2], late[3], early[2], early[3], early[4])
    return gx, (r_in_a, slots(g_in_b)), (r_o, r_gu, r_d), gathered


def kernel(x, c, ada_w, ada_b, norm1, w_in, conv_w, conv_b, dt_bias, A_log, D_skip, sinks, attn_out_norm, ssm_out_norm, w_o, norm2, w_gate_up, w_down, rel_bias, final_norm, loss_target, m_ada_w, m_ada_b, m_norm1, m_w_in, m_conv_w, m_conv_b, m_dt_bias, m_A_log, m_D_skip, m_sinks, m_attn_out_norm, m_ssm_out_norm, m_w_o, m_norm2, m_w_gate_up, m_w_down, m_rel_bias, m_final_norm, v_ada_w, v_ada_b, v_norm1, v_w_in, v_conv_w, v_conv_b, v_dt_bias, v_A_log, v_D_skip, v_sinks, v_attn_out_norm, v_ssm_out_norm, v_w_o, v_norm2, v_w_gate_up, v_w_down, v_rel_bias, v_final_norm):
    two_d = lambda a: a if a.ndim == 2 else a.reshape(-1, a.shape[-1])
    small_params = dict(
        ada_b=(ada_b, m_ada_b, v_ada_b), norm1=(norm1, m_norm1, v_norm1), conv_w=(conv_w, m_conv_w, v_conv_w),
        conv_b=(conv_b, m_conv_b, v_conv_b), dt_bias=(dt_bias, m_dt_bias, v_dt_bias), A_log=(A_log, m_A_log, v_A_log),
        D_skip=(D_skip, m_D_skip, v_D_skip), sinks=(sinks, m_sinks, v_sinks),
        attn_out_norm=(attn_out_norm, m_attn_out_norm, v_attn_out_norm),
        ssm_out_norm=(ssm_out_norm, m_ssm_out_norm, v_ssm_out_norm), norm2=(norm2, m_norm2, v_norm2),
        rel_bias=(rel_bias, m_rel_bias, v_rel_bias), final_norm=(final_norm, m_final_norm, v_final_norm))
    small_params = {k: tuple(two_d(a) for a in v) for k, v in small_params.items()}
    S = x.shape[1]
    xs, tgt = x.reshape(S, D_MODEL), loss_target.reshape(S, D_MODEL)
    ada_w2 = ada_w[0]
    chunk = ada_w2.shape[1]
    t_in = [jnp.transpose(a[0]) for a in (w_in, m_w_in, v_w_in)]
    t_gu = [jnp.transpose(a[0]) for a in (w_gate_up, m_w_gate_up, v_w_gate_up)]

    mod, (g_in, g_cw) = _mod_and_gather(c, ada_w2, ada_b.reshape(N_DEV, chunk), [t_in[0].astype(WIRE_DTYPE), conv_w[0]])
    mod = mod.reshape(6, D_MODEL)
    w_in_full = jnp.pad(g_in.reshape(IN_W, D_MODEL), ((0, IN_PAD - IN_W), (0, 0)))
    conv_w_full = jnp.transpose(g_cw, (1, 0, 2)).reshape(4, XBC_W)

    p = {k: v[0] for k, v in small_params.items()}
    gx, (r_in_a, gw_in_b), (r_o, r_gu, r_d), gathered = _local_step(
        xs, tgt, c, mod, w_in_full, conv_w_full, w_o[0].astype(WIRE_DTYPE), t_gu[0].astype(WIRE_DTYPE),
        w_down[0].astype(WIRE_DTYPE), p)

    (u_gu, u_d, u_o), (r_in_b,) = _reduce_adamw_hosting(
        [r_gu, r_d, r_o], [tuple(t_gu), (w_down[0], m_w_down[0], v_w_down[0]), (w_o[0], m_w_o[0], v_w_o[0])],
        "adamw_big", ([gw_in_b], True))

    small, loss, c_all, dmod_all = _small_update(gathered, small_params)

    big = {
        "ada_w": _ada_w_update(c_all, dmod_all, ada_w2, m_ada_w[0], v_ada_w[0]),
        "w_in": [jnp.transpose(a) for a in _reduce_adamw_halves(r_in_a, r_in_b, *t_in, "adamw_w_in")],
        "w_o": u_o,
        "w_gate_up": [jnp.transpose(a) for a in u_gu],
        "w_down": u_d,
    }
    big.update(small)

    order = ['ada_w', 'ada_b', 'norm1', 'w_in', 'conv_w', 'conv_b', 'dt_bias', 'A_log', 'D_skip', 'sinks',
             'attn_out_norm', 'ssm_out_norm', 'w_o', 'norm2', 'w_gate_up', 'w_down', 'rel_bias', 'final_norm']
    shapes = dict(ada_w=ada_w.shape, ada_b=ada_b.shape, norm1=norm1.shape, w_in=w_in.shape, conv_w=conv_w.shape,
                  conv_b=conv_b.shape, dt_bias=dt_bias.shape, A_log=A_log.shape, D_skip=D_skip.shape,
                  sinks=sinks.shape, attn_out_norm=attn_out_norm.shape, ssm_out_norm=ssm_out_norm.shape,
                  w_o=w_o.shape, norm2=norm2.shape, w_gate_up=w_gate_up.shape, w_down=w_down.shape,
                  rel_bias=rel_bias.shape, final_norm=final_norm.shape)
    outs = [[], [], [], []]
    for name in order:
        for kind in range(4):
            outs[kind].append(big[name][kind].reshape(shapes[name]))
    return (loss.reshape(()), gx.reshape(x.shape), *outs[0], *outs[1], *outs[2], *outs[3])
```

```python
import numpy as np
import jax
import jax.numpy as jnp
from jax import lax
from jax.experimental import pallas as pl
from jax.experimental.pallas import tpu as pltpu

F32 = jnp.float32
MXU_DTYPE = jnp.bfloat16
WIRE_DTYPE = jnp.bfloat16
HI = lax.Precision.HIGHEST
MESH = pl.DeviceIdType.MESH
N_DEV = 8

D_MODEL = 1024
ATTN_W = 512
KV_W = 128
SSM_W = 512
XBC_W = 1024
N_HEADS = 8
D_STATE = 128
D_FF = 2816
IN_W = 2312
IN_PAD = 2432
BLK = 128
N_BUCKETS = 32
EPS = 1e-6
LANE = 128
HALF = 64

ADAM_LR, ADAM_B1, ADAM_B2, ADAM_EPS, ADAM_WD, ADAM_STEP = 0.001, 0.9, 0.999, 1e-08, 0.01, 10

VMEM_BIG = 56 * 1024 * 1024
WD_CUT = 288
WGU_CUTS = (240, 496)
GGU_CUT = 304


def _cparams(vmem=None):
    if vmem is None:
        return pltpu.CompilerParams()
    return pltpu.CompilerParams(vmem_limit_bytes=vmem)


def _mm(a, b):
    return jnp.dot(a.astype(MXU_DTYPE), b.astype(MXU_DTYPE), preferred_element_type=F32)


def _mm_nt(a, b):
    return lax.dot_general(a.astype(MXU_DTYPE), b.astype(MXU_DTYPE), (((1,), (1,)), ((), ())),
                           preferred_element_type=F32)


def _mm_tn(a, b):
    return lax.dot_general(a.astype(MXU_DTYPE), b.astype(MXU_DTYPE), (((0,), (0,)), ((), ())),
                           preferred_element_type=F32)


def _mm_hi(a, b):
    return jnp.dot(a, b, precision=HI, preferred_element_type=F32)


def _silu(x):
    return x * jax.nn.sigmoid(x)


def _softplus(x):
    return jnp.maximum(x, 0.0) + jnp.log1p(jnp.exp(-jnp.abs(x)))


def _rms(x, g, n):
    return x * lax.rsqrt(jnp.sum(x * x, axis=-1, keepdims=True) * (1.0 / n) + EPS) * g


def _modnorm(x, g, scale, shift):
    return _rms(x, g, x.shape[-1]) * (1.0 + scale) + shift


def _modnorm_parts(x):
    r = lax.rsqrt(jnp.sum(x * x, axis=-1, keepdims=True) * (1.0 / x.shape[-1]) + EPS)
    return r, x * r


def _modnorm_bwd(r, xhat, g, scale, dy):
    dyg = dy * (g * (1.0 + scale))
    c = jnp.sum(dyg * xhat, axis=-1, keepdims=True) * (1.0 / xhat.shape[-1])
    dx = r * (dyg - xhat * c)
    ct = jnp.sum(dy * xhat, axis=0, keepdims=True)
    return dx, ct * (1.0 + scale), ct * g, jnp.sum(dy, axis=0, keepdims=True)


def _lane_iota(shape):
    return lax.broadcasted_iota(jnp.int32, shape, len(shape) - 1)


def _split_pair(t):
    lane = _lane_iota(t.shape)
    lo = jnp.where(lane < HALF, t, 0.0)
    hi = pltpu.roll(jnp.where(lane >= HALF, t, 0.0), HALF, 1)
    return lo, hi


def _join_pair(lo, hi):
    lane = _lane_iota(lo.shape)
    return jnp.where(lane < HALF, lo, pltpu.roll(hi, HALF, 1))


def _split_heads(t, n_pairs):
    out = []
    for p in range(n_pairs):
        out.extend(_split_pair(t[:, p * LANE:(p + 1) * LANE]))
    return out


def _join_heads(hs):
    return jnp.concatenate([_join_pair(hs[2 * p], hs[2 * p + 1]) for p in range(len(hs) // 2)], axis=1)


def _t5_bucket_table():
    dist = np.arange(BLK)[:, None] + BLK - np.arange(2 * BLK)[None, :]
    n = np.maximum(dist, 0)
    max_exact = N_BUCKETS // 2
    large = max_exact + (np.log(np.maximum(n, 1) / max_exact) / np.log(128 / max_exact)
                         * (N_BUCKETS - max_exact)).astype(np.int32)
    large = np.minimum(large, N_BUCKETS - 1)
    return np.where(n < max_exact, n, large).astype(np.int32)


def _my_pos():
    return lax.axis_index("x"), lax.axis_index("y"), lax.axis_index("c")


def _peer(k):
    x, y, c = _my_pos()
    return (1 - x if k & 4 else x, 1 - y if k & 2 else y, 1 - c if k & 1 else c)


def _lin(pos):
    return 4 * pos[0] + 2 * pos[1] + pos[2]


def _xchg_copies(ins, outs, sems, scatter):
    local_sem, send_sem, recv_sem = sems
    me = _lin(_my_pos())

    def source(a, slot):
        if not scatter:
            return ins[a]
        if scatter is True:
            return ins[a].at[slot]
        return ins[a].at[slot, pl.ds(scatter[1], scatter[2])]

    local, remote = [], []
    for a in range(len(ins)):
        local.append(pltpu.make_async_copy(source(a, me), outs[a].at[me], local_sem.at[a]))
    for k in range(1, N_DEV):
        peer = _peer(k)
        for a in range(len(ins)):
            remote.append(pltpu.make_async_remote_copy(source(a, _lin(peer)), outs[a].at[me], send_sem.at[a, k - 1],
                                                       recv_sem.at[a, k - 1], device_id=peer, device_id_type=MESH))
    return local, remote


def _xchg_start(ins, outs, sems, scatter):
    local, remote = _xchg_copies(ins, outs, sems, scatter)
    for cp in local + remote:
        cp.start()


def _xchg_wait(ins, outs, sems, scatter):
    local, remote = _xchg_copies(ins, outs, sems, scatter)
    for cp in local:
        cp.wait()
    for cp in remote:
        cp.wait_send()
        cp.wait_recv()


def _xchg_shapes(arrs, scatter):
    n = len(arrs)
    if isinstance(scatter, tuple):
        out_shape = [jax.ShapeDtypeStruct((a.shape[0], scatter[2]) + a.shape[2:], a.dtype) for a in arrs]
    elif scatter:
        out_shape = [jax.ShapeDtypeStruct(a.shape, a.dtype) for a in arrs]
    else:
        out_shape = [jax.ShapeDtypeStruct((N_DEV,) + a.shape, a.dtype) for a in arrs]
    sems = [pltpu.SemaphoreType.DMA((n,)), pltpu.SemaphoreType.DMA((n, N_DEV - 1)),
            pltpu.SemaphoreType.DMA((n, N_DEV - 1))]
    return out_shape, sems


_CHIPS = (2, 4, 6)


def _g2_sems(n):
    dma = pltpu.SemaphoreType.DMA
    return [dma((n,)), dma((n, N_DEV)), dma((n, N_DEV)), dma((n, len(_CHIPS))), dma((n, len(_CHIPS)))]


class _TwoLevelGather:
    def __init__(self, ins, outs, sems, windows=None):
        self.ins, self.outs = ins, outs
        self.local_sem, self.send_sem, self.recv_sem, self.fsend_sem, self.frecv_sem = sems
        self.n = len(ins)
        self.windows = windows or [None] * self.n

    def _mine(self, a):
        w = self.windows[a]
        return self.ins[a] if w is None else self.ins[a].at[pl.ds(w[0], w[1])]

    def _direct(self, a, k):
        return pltpu.make_async_remote_copy(self._mine(a), self.outs[a].at[_lin(_my_pos())], self.send_sem.at[a, k],
                                            self.recv_sem.at[a, k], device_id=_peer(k), device_id_type=MESH)

    def _handed_on(self, a, j, origin):
        slot = self.outs[a].at[origin]
        return pltpu.make_async_remote_copy(slot, slot, self.fsend_sem.at[a, j], self.frecv_sem.at[a, j],
                                            device_id=_peer(1), device_id_type=MESH)

    def _local(self, a):
        return pltpu.make_async_copy(self._mine(a), self.outs[a].at[_lin(_my_pos())], self.local_sem.at[a])

    def start(self):
        for a in range(self.n):
            self._local(a).start()
        for k in (1,) + _CHIPS:
            for a in range(self.n):
                self._direct(a, k).start()

    def forward(self):
        for j, k in enumerate(_CHIPS):
            for a in range(self.n):
                self._direct(a, k).wait_recv()
                self._handed_on(a, j, _lin(_peer(k))).start()

    def finish(self):
        for a in range(self.n):
            self._direct(a, 1).wait_recv()
            for j, k in enumerate(_CHIPS):
                self._handed_on(a, j, _lin(_peer(k ^ 1))).wait_recv()
            self._local(a).wait()
            for k in (1,) + _CHIPS:
                self._direct(a, k).wait_send()
            for j, k in enumerate(_CHIPS):
                self._handed_on(a, j, _lin(_peer(k))).wait_send()


N_SCATTERED = 2 + len(_CHIPS)


class _TwoLevelScatter:
    def __init__(self, src, out, stage, buf_a, buf_b, sems):
        self.src, self.out, self.stage, self.buf_a, self.buf_b = src, out, stage, buf_a, buf_b
        self.local_sem, self.dsend, self.drecv, self.csend, self.crecv = sems

    def _own(self):
        return pltpu.make_async_copy(self.src.at[_lin(_my_pos())], self.out.at[0], self.local_sem.at[0])

    def _to_core(self, j):
        q = 0 if j == 0 else _CHIPS[j - 1]
        dst = self.out.at[1] if j == 0 else self.stage.at[j - 1]
        return pltpu.make_async_remote_copy(self.src.at[_lin(_peer(q ^ 1))], dst, self.dsend.at[j], self.drecv.at[j],
                                            device_id=_peer(1), device_id_type=MESH)

    def _loads(self, j):
        mine = self.src.at[_lin(_peer(_CHIPS[j]))]
        return (pltpu.make_async_copy(self.stage.at[j], self.buf_a.at[j], self.local_sem.at[1 + 2 * j]),
                pltpu.make_async_copy(mine, self.buf_b.at[j], self.local_sem.at[2 + 2 * j]))

    def _to_chip(self, j):
        return pltpu.make_async_remote_copy(self.buf_a.at[j], self.out.at[2 + j], self.csend.at[j], self.crecv.at[j],
                                            device_id=_peer(_CHIPS[j]), device_id_type=MESH)

    def start(self):
        self._own().start()
        for j in range(1 + len(_CHIPS)):
            self._to_core(j).start()

    def forward(self):
        for j in range(len(_CHIPS)):
            self._to_core(j + 1).wait_recv()
            for cp in self._loads(j):
                cp.start()
        for j in range(len(_CHIPS)):
            for cp in self._loads(j):
                cp.wait()
            self.buf_a[j] = (self.buf_a[j].astype(F32) + self.buf_b[j].astype(F32)).astype(self.buf_a.dtype)
            self._to_chip(j).start()

    def finish(self):
        self._own().wait()
        self._to_core(0).wait_recv()
        for j in range(1 + len(_CHIPS)):
            self._to_core(j).wait_send()
        for j in range(len(_CHIPS)):
            self._to_chip(j).wait_send()
            self._to_chip(j).wait_recv()


def _s2_shapes(a):
    dma = pltpu.SemaphoreType.DMA
    n_c = len(_CHIPS)
    piece = a.shape[1:]
    return (jax.ShapeDtypeStruct((N_SCATTERED,) + piece, a.dtype), jax.ShapeDtypeStruct((n_c,) + piece, a.dtype),
            [pltpu.VMEM((n_c,) + piece, a.dtype)] * 2,
            [dma((1 + 2 * n_c,)), dma((1 + n_c,)), dma((1 + n_c,)), dma((n_c,)), dma((n_c,))])


def _mod_and_gather(c, ada_w, ada_b8, arrs):
    n = len(arrs)
    chunk = ada_w.shape[1]
    out_shape = [jax.ShapeDtypeStruct((N_DEV, 1, chunk), F32)]
    out_shape += [jax.ShapeDtypeStruct((N_DEV,) + a.shape, a.dtype) for a in arrs]

    def modulation(c_ref, w_ref, b_ref, out_ref, cbuf, part, s1, r1, s2, r2):
        me = _lin(_my_pos())
        first = []
        for k in range(1, N_DEV):
            cp = pltpu.make_async_remote_copy(c_ref, cbuf.at[me], s1.at[k - 1], r1.at[k - 1],
                                              device_id=_peer(k), device_id_type=MESH)
            cp.start()
            first.append(cp)
        cbuf[me] = c_ref[...]
        for cp in first:
            cp.wait_send()
            cp.wait_recv()
        cond = _silu(jnp.concatenate([cbuf[i] for i in range(N_DEV)], axis=0))
        mod = _mm_hi(cond, w_ref[...]) + b_ref[pl.ds(me, 1), :]
        for j in range(N_DEV):
            part[j] = mod[j:j + 1, :]
        second = []
        for k in range(1, N_DEV):
            peer = _peer(k)
            cp = pltpu.make_async_remote_copy(part.at[_lin(peer)], out_ref.at[me], s2.at[k - 1], r2.at[k - 1],
                                              device_id=peer, device_id_type=MESH)
            cp.start()
            second.append(cp)
        out_ref[me] = part[me]
        for cp in second:
            cp.wait_send()
            cp.wait_recv()

    def body(*refs):
        c_ref, w_ref, b_ref = refs[:3]
        ins = refs[3:3 + n]
        mod_ref = refs[3 + n]
        outs = refs[4 + n:4 + 2 * n]
        cbuf, part, s1, r1, s2, r2 = refs[4 + 2 * n:10 + 2 * n]
        gather = _TwoLevelGather(ins, outs, refs[10 + 2 * n:])
        gather.start()
        modulation(c_ref, w_ref, b_ref, mod_ref, cbuf, part, s1, r1, s2, r2)
        gather.forward()
        gather.finish()

    hbm = pl.BlockSpec(memory_space=pltpu.HBM)
    vm = pl.BlockSpec(memory_space=pltpu.VMEM)
    dma = pltpu.SemaphoreType.DMA
    res = pl.pallas_call(
        body, name="mod_and_gather", out_shape=out_shape, in_specs=[vm, vm, vm] + [hbm] * n,
        out_specs=[vm] + [hbm] * n,
        scratch_shapes=[pltpu.VMEM((N_DEV, 1, D_MODEL), F32), pltpu.VMEM((N_DEV, 1, chunk), F32)]
        + [dma((N_DEV - 1,))] * 4 + _g2_sems(n),
    )(c, ada_w, ada_b8, *arrs)
    return res[0], res[1:]


def _hosted_call(body, name, grid, in_specs, out_specs, out_shape, scratch_shapes, args, xchg, cparams):
    xchgs = [xchg] if isinstance(xchg, tuple) else list(xchg)
    grid = (grid,) if isinstance(grid, int) else tuple(grid)
    n_in, n_out, n_scr = len(in_specs), len(out_specs), len(scratch_shapes)
    windows = [[(a[1], a[2]) if isinstance(a, tuple) else None for a in group] for group, _ in xchgs]
    xchgs = [([a[0] if isinstance(a, tuple) else a for a in group], mode) for group, mode in xchgs]
    arrs = [a for group, _ in xchgs for a in group]
    n = len(arrs)
    x_shape, x_sems, sem_counts, stage_shape, stage_bufs = [], [], [], [], []
    for (group, mode), wins in zip(xchgs, windows):
        if mode == "two-level scatter":
            (a,) = group
            res_shape, stage, bufs, sems = _s2_shapes(a)
            shapes = [res_shape]
            stage_shape.append(stage)
            stage_bufs += bufs
        else:
            shapes, sems = _xchg_shapes(group, False if mode == "two-level" else mode)
        if mode == "two-level":
            sems = _g2_sems(len(group))
            shapes = [s if w is None else jax.ShapeDtypeStruct((N_DEV, w[1]) + a.shape[1:], a.dtype)
                      for s, w, a in zip(shapes, wins, group)]
        x_shape += shapes
        x_sems += sems
        sem_counts.append(len(sems))
    n_stage = len(stage_shape)
    n_steps = int(np.prod(grid))
    staged = ("two-level", "two-level scatter")

    def hosted(*refs):
        ins, refs = refs[:n_in], refs[n_in:]
        x_in, refs = refs[:n], refs[n:]
        outs, refs = refs[:n_out], refs[n_out:]
        x_out, refs = refs[:n], refs[n:]
        stages, refs = refs[:n_stage], refs[n_stage:]
        scr, refs = refs[:n_scr], refs[n_scr:]
        bufs, sems = refs[:2 * n_stage], refs[2 * n_stage:]
        step = pl.program_id(0)
        for d in range(1, len(grid)):
            step = step * grid[d] + pl.program_id(d)
        parts, a0, s0, t0 = [], 0, 0, 0
        for (group, mode), ns, wins in zip(xchgs, sem_counts, windows):
            gi, go, gs = x_in[a0:a0 + len(group)], x_out[a0:a0 + len(group)], sems[s0:s0 + ns]
            if mode == "two-level":
                parts.append((mode, _TwoLevelGather(gi, go, gs, wins)))
            elif mode == "two-level scatter":
                parts.append((mode, _TwoLevelScatter(gi[0], go[0], stages[t0], bufs[2 * t0], bufs[2 * t0 + 1], gs)))
                t0 += 1
            else:
                parts.append((mode, (gi, go, gs, mode)))
            a0, s0 = a0 + len(group), s0 + ns

        @pl.when(step == 0)
        def _():
            for mode, x in parts:
                if mode in staged:
                    x.start()
                else:
                    _xchg_start(*x)

        for kind, at in (("two-level", (2 * n_steps) // 3), ("two-level scatter", n_steps // 4)):
            if any(mode == kind for mode, _ in parts):
                @pl.when(step == at)
                def _():
                    for mode, x in parts:
                        if mode == kind:
                            x.forward()

        body(*ins, *outs, *scr)

        @pl.when(step == n_steps - 1)
        def _():
            for mode, x in parts:
                if mode in staged:
                    x.finish()
                else:
                    _xchg_wait(*x)

    hbm = pl.BlockSpec(memory_space=pltpu.HBM)
    res = pl.pallas_call(
        hosted, name=name, grid=grid, in_specs=list(in_specs) + [hbm] * n,
        out_specs=list(out_specs) + [hbm] * (n + n_stage), out_shape=list(out_shape) + x_shape + stage_shape,
        scratch_shapes=list(scratch_shapes) + stage_bufs + x_sems, compiler_params=cparams,
    )(*args, *arrs)
    return res[:n_out], res[n_out:n_out + n]


def _row(i):
    return (i, 0)


def _fixed(i):
    return (0, 0)


def _in_proj_fwd(x, norm1, scale1, shift1, w_in, tm, xchg):
    S = x.shape[0]

    def body(x_ref, n_ref, sc_ref, sh_ref, w_ref, qkv_ref, z_ref, xbc_ref, dt_ref):
        h = _modnorm(x_ref[...], n_ref[...], sc_ref[...], sh_ref[...])
        p = _mm_nt(h, w_ref[...])
        qkv_ref[...] = p[:, :768].astype(qkv_ref.dtype)
        z_ref[...] = p[:, 768:1280]
        xbc_ref[...] = p[:, 1280:2304]
        dt_ref[...] = p[:, 2304:IN_PAD]

    vec = pl.BlockSpec((1, D_MODEL), _fixed)
    return _hosted_call(
        body, "in_proj_fwd", S // tm,
        in_specs=[pl.BlockSpec((tm, D_MODEL), _row), vec, vec, vec, pl.BlockSpec((IN_PAD, D_MODEL), _fixed)],
        out_specs=[pl.BlockSpec((tm, 768), _row), pl.BlockSpec((tm, SSM_W), _row),
                   pl.BlockSpec((tm, XBC_W), _row), pl.BlockSpec((tm, LANE), _row)],
        out_shape=[jax.ShapeDtypeStruct((S, 768), MXU_DTYPE), jax.ShapeDtypeStruct((S, SSM_W), F32),
                   jax.ShapeDtypeStruct((S, XBC_W), F32), jax.ShapeDtypeStruct((S, LANE), F32)],
        scratch_shapes=[], args=(x, norm1, scale1, shift1, w_in), xchg=xchg, cparams=_cparams(VMEM_BIG),
    )


def _in_proj_bwd(x, dx1, dq, dkv, dz, dxbc, ddt, norm1, scale1, shift1, w_in, tm):
    S = x.shape[0]

    n_steps = S // tm
    half_cols = D_MODEL // 2

    def body(x_ref, dx1_ref, dq_ref, dkv_ref, dz_ref, dxbc_ref, ddt_ref, n_ref, sc_ref, sh_ref, w_ref,
             gx_ref, h_ref, acc_ref, gw_ref, gw_acc):
        i = pl.program_id(0)

        @pl.when(i == 0)
        def _():
            acc_ref[...] = jnp.zeros_like(acc_ref)
            gw_acc[...] = jnp.zeros_like(gw_acc)

        halves = [pl.ds(k * (tm // 2), tm // 2) for k in range(2)]
        dp = [jnp.concatenate([r[rows, :] for r in (dq_ref, dkv_ref, dz_ref, dxbc_ref, ddt_ref)], axis=1)
              for rows in halves]
        dh = [_mm(dp[k], w_ref[...]) for k in range(2)]
        parts = [_modnorm_parts(x_ref[rows, :]) for rows in halves]
        hb = [(parts[k][1] * n_ref[...] * (1.0 + sc_ref[...]) + sh_ref[...]).astype(h_ref.dtype) for k in range(2)]
        gw_acc[...] += _mm_tn(dp[0], hb[0][:, :half_cols]) + _mm_tn(dp[1], hb[1][:, :half_cols])
        bwd = [_modnorm_bwd(parts[k][0], parts[k][1], n_ref[...], sc_ref[...], dh[k]) for k in range(2)]
        for k, rows in enumerate(halves):
            gx_ref[rows, :] = dx1_ref[rows, :] + bwd[k][0]
            h_ref[rows, :] = hb[k]
        acc_ref[0:1, :] += bwd[0][1] + bwd[1][1]
        acc_ref[1:2, :] += bwd[0][2] + bwd[1][2]
        acc_ref[2:3, :] += bwd[0][3] + bwd[1][3]

        @pl.when(i == n_steps - 1)
        def _():
            gw_ref[...] = gw_acc[...].astype(gw_ref.dtype)

    vec = pl.BlockSpec((1, D_MODEL), _fixed)
    return pl.pallas_call(
        body, name="in_proj_bwd", grid=(n_steps,),
        in_specs=[pl.BlockSpec((tm, D_MODEL), _row), pl.BlockSpec((tm, D_MODEL), _row),
                  pl.BlockSpec((tm, ATTN_W), _row), pl.BlockSpec((tm, 2 * KV_W), _row),
                  pl.BlockSpec((tm, SSM_W), _row), pl.BlockSpec((tm, XBC_W), _row), pl.BlockSpec((tm, LANE), _row),
                  vec, vec, vec, pl.BlockSpec((IN_PAD, D_MODEL), _fixed)],
        out_specs=[pl.BlockSpec((tm, D_MODEL), _row), pl.BlockSpec((tm, D_MODEL), _row),
                   pl.BlockSpec((8, D_MODEL), _fixed), pl.BlockSpec((IN_PAD, half_cols), _fixed)],
        out_shape=[jax.ShapeDtypeStruct((S, D_MODEL), F32), jax.ShapeDtypeStruct((S, D_MODEL), MXU_DTYPE),
                   jax.ShapeDtypeStruct((8, D_MODEL), F32), jax.ShapeDtypeStruct((IN_PAD, half_cols), WIRE_DTYPE)],
        scratch_shapes=[pltpu.VMEM((IN_PAD, half_cols), F32)],
        compiler_params=_cparams(VMEM_BIG),
    )(x, dx1, dq, dkv, dz, dxbc, ddt, norm1, scale1, shift1, w_in)


def _out_stage(ya, ys0, ys1, z0, z1, an, sn0, sn1):
    half = SSM_W // 2
    a = _rms(ya, an, ATTN_W)
    g0 = _rms(ys0 * _silu(z0), sn0, half)
    g1 = _rms(ys1 * _silu(z1), sn1, half)
    return jnp.concatenate([a, g0, g1], axis=1)


def _out_stage_args(ya_ref, ys_ref, z_ref, an_ref, sn_ref):
    half = SSM_W // 2
    return (ya_ref[...], ys_ref[:, :half], ys_ref[:, half:], z_ref[:, :half], z_ref[:, half:],
            an_ref[...], sn_ref[:, :half], sn_ref[:, half:])


def _out_proj_fwd(x, ya, ys, z, an, sn, gate1, w_o, tm, xchg):
    S = x.shape[0]

    def body(x_ref, ya_ref, ys_ref, z_ref, an_ref, sn_ref, g_ref, w_ref, x1_ref):
        u = _out_stage(*_out_stage_args(ya_ref, ys_ref, z_ref, an_ref, sn_ref))
        x1_ref[...] = x_ref[...] + g_ref[...] * _mm(u, w_ref[...])

    half = pl.BlockSpec((tm, ATTN_W), _row)
    hvec = pl.BlockSpec((1, ATTN_W), _fixed)
    (x1,), x_out = _hosted_call(
        body, "out_proj_fwd", S // tm,
        in_specs=[pl.BlockSpec((tm, D_MODEL), _row), half, half, half, hvec, hvec,
                  pl.BlockSpec((1, D_MODEL), _fixed), pl.BlockSpec((D_MODEL, D_MODEL), _fixed)],
        out_specs=[pl.BlockSpec((tm, D_MODEL), _row)],
        out_shape=[jax.ShapeDtypeStruct((S, D_MODEL), F32)],
        scratch_shapes=[], args=(x, ya, ys, z, an, sn, gate1, w_o), xchg=xchg, cparams=_cparams(VMEM_BIG),
    )
    return x1, x_out


def _out_proj_bwd(dx1, ya, ys, z, an, sn, gate1, w_o, tm, xchg):
    S = dx1.shape[0]
    n_steps = S // tm

    def body(dx1_ref, ya_ref, ys_ref, z_ref, an_ref, sn_ref, g_ref, w_ref,
             dya_ref, dys_ref, dz_ref, gw_ref, acc_ref, gw_acc):
        i = pl.program_id(0)

        @pl.when(i == 0)
        def _():
            acc_ref[...] = jnp.zeros_like(acc_ref)
            gw_acc[...] = jnp.zeros_like(gw_acc)

        u, vjp = jax.vjp(_out_stage, *_out_stage_args(ya_ref, ys_ref, z_ref, an_ref, sn_ref))
        dx1 = dx1_ref[...]
        ub = u.astype(MXU_DTYPE)
        mix = _mm(ub, w_ref[...])
        dmix = dx1 * g_ref[...]
        dmixb = dmix.astype(MXU_DTYPE)
        du = _mm_nt(dmixb, w_ref[...])
        gw_acc[...] += _mm_tn(ub, dmixb)
        dya, dys0, dys1, dz0, dz1, dan, dsn0, dsn1 = vjp(du)
        dya_ref[...] = dya
        dys_ref[...] = jnp.concatenate([dys0, dys1], axis=1)
        dz_ref[...] = jnp.concatenate([dz0, dz1], axis=1).astype(dz_ref.dtype)
        acc_ref[0:1, :] += jnp.sum(dx1 * mix, axis=0, keepdims=True)
        acc_ref[1:2, :] += jnp.concatenate([dan, dsn0, dsn1], axis=1)

        @pl.when(i == n_steps - 1)
        def _():
            gw_ref[...] = gw_acc[...].astype(gw_ref.dtype)

    half = pl.BlockSpec((tm, ATTN_W), _row)
    hvec = pl.BlockSpec((1, ATTN_W), _fixed)
    full = pl.BlockSpec((tm, D_MODEL), _row)
    return _hosted_call(
        body, "out_proj_bwd", n_steps,
        in_specs=[full, half, half, half, hvec, hvec,
                  pl.BlockSpec((1, D_MODEL), _fixed), pl.BlockSpec((D_MODEL, D_MODEL), _fixed)],
        out_specs=[half, half, half, pl.BlockSpec((D_MODEL, D_MODEL), _fixed), pl.BlockSpec((8, D_MODEL), _fixed)],
        out_shape=[jax.ShapeDtypeStruct((S, ATTN_W), F32)] * 2 + [jax.ShapeDtypeStruct((S, ATTN_W), MXU_DTYPE),
                   jax.ShapeDtypeStruct((D_MODEL, D_MODEL), WIRE_DTYPE), jax.ShapeDtypeStruct((8, D_MODEL), F32)],
        scratch_shapes=[pltpu.VMEM((D_MODEL, D_MODEL), F32)],
        args=(dx1, ya, ys, z, an, sn, gate1, w_o), xchg=xchg, cparams=_cparams(VMEM_BIG),
    )


def _loss_rows(x2, fn, tgt):
    y = _rms(x2, fn, D_MODEL)
    per_row = jnp.sum(jnp.square(y - tgt), axis=1, keepdims=True)
    return jnp.sum(per_row, axis=0, keepdims=True) * (0.5 / D_MODEL)


def _mlp_loss(x1, tgt, norm2, scale2, shift2, gate2, fnorm, w_gu, w_d, tm):
    S = x1.shape[0]
    n_pieces = len(w_gu) + len(w_d)

    def body(*refs):
        x1_ref, t_ref, n_ref, sc_ref, sh_ref, g_ref, fn_ref = refs[:7]
        piece_refs = refs[7:7 + n_pieces]
        dx1_ref, h_ref, dgu_ref, act_ref, dmlp_ref, acc_ref, wgu, wd, wsem = refs[7 + n_pieces:]

        @pl.when(pl.program_id(0) == 0)
        def _():
            acc_ref[...] = jnp.zeros_like(acc_ref)
            copies = []
            for dst, pieces in ((wgu, piece_refs[:len(w_gu)]), (wd, piece_refs[len(w_gu):])):
                shard = sum(p.shape[1] for p in pieces)
                off = 0
                for p in pieces:
                    for j in range(N_DEV):
                        copies.append(pltpu.make_async_copy(p.at[j], dst.at[pl.ds(j * shard + off, p.shape[1])],
                                                            wsem.at[len(copies)]))
                    off += p.shape[1]
            for cp in copies:
                cp.start()
            for cp in copies:
                cp.wait()

        x1 = x1_ref[...]
        gate2 = g_ref[...]
        h, vjp_h = jax.vjp(_modnorm, x1, n_ref[...], sc_ref[...], sh_ref[...])
        hb = h.astype(MXU_DTYPE)
        gu = _mm_nt(hb, wgu[...])
        g, u = gu[:, :D_FF], gu[:, D_FF:]
        sg = jax.nn.sigmoid(g)
        silu_g = g * sg
        act = (silu_g * u).astype(MXU_DTYPE)
        mlp = _mm(act, wd[...])
        x2 = x1 + gate2 * mlp
        loss, vjp_loss = jax.vjp(_loss_rows, x2, fn_ref[...], t_ref[...])
        dx2, dfn, _ = vjp_loss(jnp.ones((1, 1), F32))
        dmlp = (dx2 * gate2).astype(MXU_DTYPE)
        dact = _mm_nt(dmlp, wd[...])
        dg = dact * u * (sg * (1.0 + g * (1.0 - sg)))
        du = dact * silu_g
        dgu = jnp.concatenate([dg, du], axis=1).astype(MXU_DTYPE)
        dh = _mm(dgu, wgu[...])
        dx, dn, dsc, dsh = vjp_h(dh)
        dx1_ref[...] = dx2 + dx
        h_ref[...] = hb
        dgu_ref[...] = dgu
        act_ref[...] = act
        dmlp_ref[...] = dmlp
        acc_ref[0:1, :] += dn
        acc_ref[1:2, :] += dsc
        acc_ref[2:3, :] += dsh
        acc_ref[3:4, :] += jnp.sum(dx2 * mlp, axis=0, keepdims=True)
        acc_ref[4:5, :] += dfn
        acc_ref[5:6, :] += jnp.broadcast_to(loss, (1, D_MODEL))

    full = pl.BlockSpec((tm, D_MODEL), _row)
    vec = pl.BlockSpec((1, D_MODEL), _fixed)
    anyspec = pl.BlockSpec(memory_space=pl.ANY)
    return pl.pallas_call(
        body, name="mlp_loss", grid=(S // tm,),
        in_specs=[full, full, vec, vec, vec, vec, vec] + [anyspec] * n_pieces,
        out_specs=[full, full, pl.BlockSpec((tm, 2 * D_FF), _row), pl.BlockSpec((tm, D_FF), _row), full,
                   pl.BlockSpec((8, D_MODEL), _fixed)],
        out_shape=[jax.ShapeDtypeStruct((S, D_MODEL), F32), jax.ShapeDtypeStruct((S, D_MODEL), MXU_DTYPE),
                   jax.ShapeDtypeStruct((S, 2 * D_FF), MXU_DTYPE), jax.ShapeDtypeStruct((S, D_FF), MXU_DTYPE),
                   jax.ShapeDtypeStruct((S, D_MODEL), MXU_DTYPE), jax.ShapeDtypeStruct((8, D_MODEL), F32)],
        scratch_shapes=[pltpu.VMEM((2 * D_FF, D_MODEL), MXU_DTYPE), pltpu.VMEM((D_FF, D_MODEL), MXU_DTYPE),
                        pltpu.SemaphoreType.DMA((N_DEV * n_pieces,))],
        compiler_params=_cparams(VMEM_BIG),
    )(x1, tgt, norm2, scale2, shift2, gate2, fnorm, *w_gu, *w_d)


def _wgrad(a, g, tk, ts, name, xchg=None, g_cols=None):
    pieces = list(a) if isinstance(a, (list, tuple)) else [a]
    S = pieces[0].shape[0]
    K = sum(p.shape[1] for p in pieces)
    assert len(pieces) == 1 or tk == K
    N, col = (g.shape[1], 0) if g_cols is None else g_cols
    ns = S // ts
    n_a = len(pieces)

    def body(*refs):
        a_refs, (g_ref, o_ref, acc_ref) = refs[:n_a], refs[n_a:]
        s = pl.program_id(1)

        @pl.when(s == 0)
        def _():
            acc_ref[...] = jnp.zeros_like(acc_ref)

        a_blk = a_refs[0][...] if n_a == 1 else jnp.concatenate([r[...] for r in a_refs], axis=1)
        acc_ref[...] += _mm_tn(a_blk, g_ref[...])

        @pl.when(s == ns - 1)
        def _():
            o_ref[...] = acc_ref[...].astype(o_ref.dtype)

    if n_a == 1:
        in_specs = [pl.BlockSpec((ts, tk), lambda j, s: (s, j))]
    else:
        in_specs = [pl.BlockSpec((ts, p.shape[1]), lambda j, s: (s, 0)) for p in pieces]
    in_specs.append(pl.BlockSpec((ts, N), lambda j, s: (s, col)))
    out_spec = pl.BlockSpec((tk, N), lambda j, s: (j, 0))
    out_shape = jax.ShapeDtypeStruct((K, N), WIRE_DTYPE)
    scratch = [pltpu.VMEM((tk, N), F32)]
    args = (*pieces, g)
    if xchg is None:
        return pl.pallas_call(body, name=name, grid=(K // tk, ns), in_specs=in_specs, out_specs=out_spec,
                              out_shape=out_shape, scratch_shapes=scratch, compiler_params=_cparams(VMEM_BIG))(*args)
    (out,), x_out = _hosted_call(body, name, (K // tk, ns), in_specs, [out_spec], [out_shape], scratch, args, xchg,
                                 _cparams(VMEM_BIG))
    return out, x_out


SSD_CHUNKS_PER_STEP = 4
SSD_BWD_CHUNKS_PER_STEP = 4
ATTN_BLOCKS_PER_STEP = 4
MASKED = -1e30
QK_SCALE = HALF ** -0.5


def _attn_bias(buckets, rel_bias):
    def body(bk_ref, relb_ref, out_ref):
        bk = bk_ref[...]
        i = lax.broadcasted_iota(jnp.int32, (BLK, 2 * BLK), 0)
        j = lax.broadcasted_iota(jnp.int32, (BLK, 2 * BLK), 1)
        window = (j > i) & (j <= i + BLK)
        for h in range(N_HEADS):
            acc = jnp.zeros((BLK, 2 * BLK), F32)
            for b in range(N_BUCKETS):
                acc = jnp.where(bk == b, relb_ref[b, h], acc)
            out_ref[0, h] = jnp.where(window, acc, MASKED)
            out_ref[1, h] = jnp.where(window & (j >= BLK), acc, MASKED)

    return pl.pallas_call(
        body, name="attn_bias", out_shape=jax.ShapeDtypeStruct((2, N_HEADS, BLK, 2 * BLK), F32),
        in_specs=[pl.BlockSpec(memory_space=pltpu.VMEM), pl.BlockSpec(memory_space=pltpu.SMEM)],
    )(buckets, rel_bias)


def _attn_fwd(qkv, bias, sinks, xchg):
    S = qkv.shape[0]
    nb = S // BLK

    nq = ATTN_BLOCKS_PER_STEP if nb % ATTN_BLOCKS_PER_STEP == 0 else 1
    rows = nq * BLK

    def body(q_ref, kvp_ref, kvc_ref, bias_ref, sinks_ref, y_ref):
        i = pl.program_id(0)
        q = q_ref[...].astype(F32) * QK_SCALE
        kv = jnp.concatenate([kvp_ref[...], kvc_ref[...]], axis=0).astype(F32)
        k_lo, k_hi = _split_pair(kv[:, :LANE])
        v_lo, v_hi = _split_pair(kv[:, LANE:])
        bands = [[t[b * BLK:(b + 2) * BLK].astype(MXU_DTYPE) for t in (k_lo, k_hi, v_lo, v_hi)] for b in range(nq)]
        q_heads = [_split_heads(q[b * BLK:(b + 1) * BLK], 4) for b in range(nq)]
        first = [jnp.where(i == 0, 1, 0) if b == 0 else 0 for b in range(nq)]
        items = [(b, h) for b in range(nq) for h in range(N_HEADS)]
        s = [_mm_nt(q_heads[b][h].astype(MXU_DTYPE), bands[b][h // 4]) + bias_ref[first[b], h] for b, h in items]
        m = [jnp.maximum(jnp.max(s[n], axis=-1, keepdims=True), sinks_ref[h]) for n, (b, h) in enumerate(items)]
        p = [jnp.exp(s[n] - m[n]) for n in range(len(items))]
        rinv = [1.0 / (jnp.sum(p[n], axis=-1, keepdims=True) + jnp.exp(sinks_ref[h] - m[n]))
                for n, (b, h) in enumerate(items)]
        out = [_mm(p[n], bands[b][2 + h // 4]) * rinv[n] for n, (b, h) in enumerate(items)]
        y_ref[...] = jnp.concatenate([_join_heads(out[b * N_HEADS:(b + 1) * N_HEADS]) for b in range(nq)], axis=0)

    smem = pl.BlockSpec(memory_space=pltpu.SMEM)
    return _hosted_call(
        body, "attn_fwd", nb // nq,
        in_specs=[pl.BlockSpec((rows, ATTN_W), _row),
                  pl.BlockSpec((BLK, 2 * KV_W), lambda i: (jnp.maximum(i * nq - 1, 0), 2)),
                  pl.BlockSpec((rows, 2 * KV_W), lambda i: (i, 2)),
                  pl.BlockSpec((2, N_HEADS, BLK, 2 * BLK), lambda i: (0, 0, 0, 0)), smem],
        out_specs=[pl.BlockSpec((rows, ATTN_W), _row)],
        out_shape=[jax.ShapeDtypeStruct((S, ATTN_W), F32)],
        scratch_shapes=[],
        args=(qkv, qkv, qkv, bias, sinks), xchg=xchg, cparams=_cparams(),
    )


def _attn_bwd(qkv, y, dy, bias, sinks, xchg):
    S = qkv.shape[0]
    nb = S // BLK
    nq = ATTN_BLOCKS_PER_STEP if nb % ATTN_BLOCKS_PER_STEP == 0 else 1
    rows, n_steps = nq * BLK, nb // nq

    def body(q_ref, kvp_ref, kvc_ref, y_ref, dy_ref, bias_ref, sinks_ref, dq_ref, dkv_ref, dbias_ref, dsk_ref, carry_ref):
        i = pl.program_id(0)

        @pl.when(i == 0)
        def _():
            dbias_ref[...] = jnp.zeros_like(dbias_ref)
            dsk_ref[...] = jnp.zeros_like(dsk_ref)
            carry_ref[...] = jnp.zeros_like(carry_ref)

        q = q_ref[...].astype(F32) * QK_SCALE
        kv = jnp.concatenate([kvp_ref[...], kvc_ref[...]], axis=0).astype(F32)
        k_lo, k_hi = _split_pair(kv[:, :LANE])
        v_lo, v_hi = _split_pair(kv[:, LANE:])
        bands = [[t[b * BLK:(b + 2) * BLK].astype(MXU_DTYPE) for t in (k_lo, k_hi, v_lo, v_hi)] for b in range(nq)]
        rows_of = lambda ref, b: ref[b * BLK:(b + 1) * BLK, :]
        first = [jnp.where(i == n_steps - 1, 1, 0) if b == 0 else 0 for b in range(nq)]
        items = [(b, h) for b in range(nq) for h in range(N_HEADS)]
        at = lambda b, h: b * N_HEADS + h
        q_heads = [hd for b in range(nq) for hd in _split_heads(q[b * BLK:(b + 1) * BLK], 4)]
        y_heads = [hd for b in range(nq) for hd in _split_heads(rows_of(y_ref, b), 4)]
        dy_heads = [hd for b in range(nq) for hd in _split_heads(rows_of(dy_ref, b), 4)]
        qs = [q_heads[n].astype(MXU_DTYPE) for n in range(len(items))]
        s = [_mm_nt(qs[at(b, h)], bands[b][h // 4]) + bias_ref[first[b], h] for b, h in items]
        m = [jnp.maximum(jnp.max(s[at(b, h)], axis=-1, keepdims=True), sinks_ref[h]) for b, h in items]
        p = [jnp.exp(s[n] - m[n]) for n in range(len(items))]
        esink = [jnp.exp(sinks_ref[h] - m[at(b, h)]) for b, h in items]
        rinv = [1.0 / (jnp.sum(p[n], axis=-1, keepdims=True) + esink[n]) for n in range(len(items))]
        t = [dy_heads[n] * rinv[n] for n in range(len(items))]
        delta = [jnp.sum(t[n] * y_heads[n], axis=-1, keepdims=True) for n in range(len(items))]
        tb = [t[n].astype(MXU_DTYPE) for n in range(len(items))]
        dp = [_mm_nt(tb[at(b, h)], bands[b][2 + h // 4]) for b, h in items]
        ds = [p[n] * (dp[n] - delta[n]) for n in range(len(items))]
        for h in range(N_HEADS):
            ds_h, dsk_h = ds[at(0, h)], esink[at(0, h)] * delta[at(0, h)]
            for b in range(1, nq):
                ds_h = ds_h + ds[at(b, h)]
                dsk_h = dsk_h + esink[at(b, h)] * delta[at(b, h)]
            dbias_ref[h] += ds_h
            dsk_ref[h] -= dsk_h
        dsb = [ds[n].astype(MXU_DTYPE) for n in range(len(items))]
        pb = [p[n].astype(MXU_DTYPE) for n in range(len(items))]
        dq_heads = [_mm(dsb[at(b, h)], bands[b][h // 4]) * QK_SCALE for b, h in items]
        grp = lambda lst, b, g: jnp.concatenate(lst[at(b, 4 * g):at(b, 4 * g) + 4], axis=0)
        dk_pads = [[_mm_tn(grp(dsb, b, g), grp(qs, b, g)) for g in range(2)] for b in range(nq)]
        dv_pads = [[_mm_tn(grp(pb, b, g), grp(tb, b, g)) for g in range(2)] for b in range(nq)]
        dq_ref[...] = jnp.concatenate([_join_heads(dq_heads[b * N_HEADS:(b + 1) * N_HEADS]) for b in range(nq)],
                                      axis=0).astype(dq_ref.dtype)
        part = lambda b, lo: jnp.concatenate(
            [_join_pair(d[b][0][lo:lo + BLK], d[b][1][lo:lo + BLK]) for d in (dk_pads, dv_pads)], axis=1)
        dkv = [part(b, BLK) + (part(b + 1, 0) if b + 1 < nq else carry_ref[...]) for b in range(nq)]
        dkv_ref[...] = jnp.concatenate(dkv, axis=0).astype(dkv_ref.dtype)
        carry_ref[...] = part(0, 0)

    smem = pl.BlockSpec(memory_space=pltpu.SMEM)
    rev = lambda i: (n_steps - 1 - i, 0)
    return _hosted_call(
        body, "attn_bwd", n_steps,
        in_specs=[pl.BlockSpec((rows, ATTN_W), rev),
                  pl.BlockSpec((BLK, 2 * KV_W), lambda i: (jnp.maximum((n_steps - 1 - i) * nq - 1, 0), 2)),
                  pl.BlockSpec((rows, 2 * KV_W), lambda i: (n_steps - 1 - i, 2)),
                  pl.BlockSpec((rows, ATTN_W), rev), pl.BlockSpec((rows, ATTN_W), rev),
                  pl.BlockSpec((2, N_HEADS, BLK, 2 * BLK), lambda i: (0, 0, 0, 0)), smem],
        out_specs=[pl.BlockSpec((rows, ATTN_W), rev), pl.BlockSpec((rows, 2 * KV_W), rev),
                   pl.BlockSpec((N_HEADS, BLK, 2 * BLK), lambda i: (0, 0, 0)),
                   pl.BlockSpec((N_HEADS, BLK, 1), lambda i: (0, 0, 0))],
        out_shape=[jax.ShapeDtypeStruct((S, ATTN_W), MXU_DTYPE), jax.ShapeDtypeStruct((S, 2 * KV_W), MXU_DTYPE),
                   jax.ShapeDtypeStruct((N_HEADS, BLK, 2 * BLK), F32), jax.ShapeDtypeStruct((N_HEADS, BLK, 1), F32)],
        scratch_shapes=[pltpu.VMEM((BLK, 2 * KV_W), F32)],
        args=(qkv, qkv, qkv, y, dy, bias, sinks), xchg=xchg, cparams=_cparams(),
    )


def _attn_finish(dbias, dsk, buckets):
    def body(db_ref, dsk_ref, bk_ref, drel_ref, dsink_ref):
        bk = bk_ref[...]
        r = lax.broadcasted_iota(jnp.int32, (N_BUCKETS, LANE), 0)
        l = lax.broadcasted_iota(jnp.int32, (N_BUCKETS, LANE), 1)
        row = lax.broadcasted_iota(jnp.int32, (N_HEADS, LANE), 0)
        res = jnp.zeros((N_BUCKETS, LANE), F32)
        dsink = jnp.zeros((N_HEADS, LANE), F32)
        for h in range(N_HEADS):
            db = db_ref[h]
            for b in range(N_BUCKETS):
                v = jnp.sum(jnp.sum(jnp.where(bk == b, db, 0.0), axis=1, keepdims=True), axis=0, keepdims=True)
                res = res + jnp.where((r == b) & (l == h), v, 0.0)
            dsink = dsink + jnp.where(row == h, jnp.sum(dsk_ref[h], axis=0, keepdims=True), 0.0)
        drel_ref[...] = res
        dsink_ref[...] = dsink

    return pl.pallas_call(body, name="attn_finish",
                          out_shape=[jax.ShapeDtypeStruct((N_BUCKETS, LANE), F32),
                                     jax.ShapeDtypeStruct((N_HEADS, LANE), F32)])(dbias, dsk, buckets)


def _ssd_consts():
    r = lax.broadcasted_iota(jnp.int32, (BLK, BLK), 0)
    c = lax.broadcasted_iota(jnp.int32, (BLK, BLK), 1)
    causal = c <= r
    upper = (r <= c).astype(F32)
    last = r == BLK - 1
    head = lax.broadcasted_iota(jnp.int32, (N_HEADS, BLK), 0)
    return causal, upper, last, head


def _ssd_chunks(xs, bg, cg, dt_raw_t, prev0, dtb, alog, d_rows, consts):
    causal, upper, last, head = consts
    nq = len(xs)
    items = [(c, h) for c in range(nq) for h in range(N_HEADS)]
    at = lambda c, h: c * N_HEADS + h
    a_neg = -jnp.exp(alog)
    dt_t = [_softplus(dt_raw_t[c] + dtb) for c in range(nq)]
    acs_t = [_mm_hi(dt_t[c] * a_neg, upper) for c in range(nq)]
    cb = [[_mm_nt(cg[c][g], bg[c][g]) for g in range(2)] for c in range(nq)]
    pick = lambda t, h: jnp.sum(jnp.where(head == h, t, 0.0), axis=0, keepdims=True)
    dt_row = [pick(dt_t[c], h) for c, h in items]
    a_row = [pick(acs_t[c], h) for c, h in items]
    a_rb = [jnp.broadcast_to(a_row[n], (BLK, BLK)) for n in range(len(items))]
    a_b = [a_rb[n].T for n in range(len(items))]
    a_last = [jnp.sum(jnp.where(last, a_b[n], 0.0), axis=0, keepdims=True) for n in range(len(items))]
    w = [cb[c][h // 4] * jnp.exp(jnp.where(causal, a_b[at(c, h)] - a_rb[at(c, h)], -1e30)) * dt_row[at(c, h)]
         for c, h in items]
    f_b = [jnp.broadcast_to(dt_row[n] * jnp.exp(a_last[n] - a_row[n]), (BLK, BLK)).T for n in range(len(items))]
    y_in = [_mm(w[at(c, h)], xs[c][h]) for c, h in items]
    st = [_mm_tn(bg[c][h // 4], xs[c][h] * f_b[at(c, h)]) for c, h in items]
    e_b = [jnp.exp(a_b[n]) for n in range(len(items))]
    states = [list(prev0)]
    for c in range(nq):
        states.append([states[c][h] * jnp.exp(a_last[at(c, h)]) + st[at(c, h)] for h in range(N_HEADS)])
    y_off = [_mm(cg[c][h // 4], states[c][h]) * e_b[at(c, h)] for c, h in items]
    ys = [[y_in[at(c, h)] + y_off[at(c, h)] + d_rows[h] * xs[c][h] for h in range(N_HEADS)] for c in range(nq)]
    return ys, states


def _ssd_chunks_bwd(xs, bg, cg, dt_raw_t, prev, dtb, alog, d_rows, dys, dh_last, consts):
    causal, upper, last, head = consts
    nq = len(xs)
    items = [(c, h) for c in range(nq) for h in range(N_HEADS)]
    ni = len(items)
    at = lambda c, h: c * N_HEADS + h
    groups = [(c, g) for c in range(nq) for g in range(2)]
    lane = _lane_iota((BLK, BLK))
    lane_row = _lane_iota((1, BLK))
    a_neg = -jnp.exp(alog)
    pre_dt = [dt_raw_t[c] + dtb for c in range(nq)]
    dt_t = [_softplus(pre_dt[c]) for c in range(nq)]
    acs_t = [_mm_hi(dt_t[c] * a_neg, upper) for c in range(nq)]
    pick = lambda t, h: jnp.sum(jnp.where(head == h, t, 0.0), axis=0, keepdims=True)
    full_sum = lambda t: jnp.sum(jnp.sum(t, axis=1, keepdims=True), axis=0, keepdims=True)
    dt_row = [pick(dt_t[c], h) for c, h in items]
    a_row = [pick(acs_t[c], h) for c, h in items]
    a_rb = [jnp.broadcast_to(a_row[n], (BLK, BLK)) for n in range(ni)]
    a_b = [a_rb[n].T for n in range(ni)]
    a_last = [jnp.sum(jnp.where(last, a_b[n], 0.0), axis=0, keepdims=True) for n in range(ni)]
    lm = [jnp.exp(jnp.where(causal, a_b[n] - a_rb[n], -1e30)) for n in range(ni)]
    cgb = [[cg[c][g].astype(MXU_DTYPE) for g in range(2)] for c in range(nq)]
    bgb = [[bg[c][g].astype(MXU_DTYPE) for g in range(2)] for c in range(nq)]
    cb = [[_mm_nt(cgb[c][g], bgb[c][g]) for g in range(2)] for c in range(nq)]
    u = [cb[c][h // 4] * lm[at(c, h)] for c, h in items]
    w = [(u[n] * dt_row[n]).astype(MXU_DTYPE) for n in range(ni)]
    e_row = [jnp.exp(a_last[n] - a_row[n]) for n in range(ni)]
    f_row = [dt_row[n] * e_row[n] for n in range(ni)]
    f_b = [jnp.broadcast_to(f_row[n], (BLK, BLK)).T for n in range(ni)]
    e_b = [jnp.exp(a_b[n]) for n in range(ni)]
    el = [jnp.exp(a_last[n]) for n in range(ni)]
    xb = [xs[c][h].astype(MXU_DTYPE) for c, h in items]
    dyb = [dys[c][h].astype(MXU_DTYPE) for c, h in items]
    prevb = [prev[c][h].astype(MXU_DTYPE) for c, h in items]
    gmat = [_mm(cgb[c][h // 4], prevb[at(c, h)]) for c, h in items]
    dw = [_mm_nt(dyb[n], xb[n]) for n in range(ni)]
    dg = [dys[c][h] * e_b[at(c, h)] for c, h in items]
    dgb = [dg[n].astype(MXU_DTYPE) for n in range(ni)]
    from_y = [_mm_tn(cgb[c][h // 4], dgb[at(c, h)]) for c, h in items]
    dhs = [None] * ni
    dprev = [None] * ni
    for c in reversed(range(nq)):
        for h in range(N_HEADS):
            dhs[at(c, h)] = dh_last[h] if c == nq - 1 else dprev[at(c + 1, h)]
            dprev[at(c, h)] = from_y[at(c, h)] + dhs[at(c, h)] * el[at(c, h)]
    dstb = [dhs[n].astype(MXU_DTYPE) for n in range(ni)]
    dxf = [_mm(bgb[c][h // 4], dstb[at(c, h)]) for c, h in items]
    xfb = [(xs[c][h] * f_b[at(c, h)]).astype(MXU_DTYPE) for c, h in items]
    dxs = [_mm_tn(w[at(c, h)], dyb[at(c, h)]) + d_rows[h] * dys[c][h] + f_b[at(c, h)] * dxf[at(c, h)]
           for c, h in items]
    dd_item = [jnp.sum(dys[c][h] * xs[c][h], axis=0, keepdims=True) for c, h in items]
    dcg_h = [_mm_nt(dgb[n], prevb[n]) for n in range(ni)]
    dbg_h = [_mm_nt(xfb[n], dstb[n]) for n in range(ni)]
    zt = [dw[n] * u[n] for n in range(ni)]
    dseg = [zt[n] * dt_row[n] for n in range(ni)]
    dcb_h = [dw[n] * lm[n] * dt_row[n] for n in range(ni)]
    four = lambda lst, c, g: lst[at(c, 4 * g)] + lst[at(c, 4 * g + 1)] + lst[at(c, 4 * g + 2)] + lst[at(c, 4 * g + 3)]
    dcb = {(c, g): four(dcb_h, c, g).astype(MXU_DTYPE) for c, g in groups}
    dcg = [[four(dcg_h, c, g) + _mm(dcb[c, g], bgb[c][g]) for g in range(2)] for c in range(nq)]
    dbg = [[four(dbg_h, c, g) + _mm_tn(dcb[c, g], cgb[c][g]) for g in range(2)] for c in range(nq)]
    r1 = [jnp.sum(dg[n] * gmat[n] + dseg[n], axis=1, keepdims=True) for n in range(ni)]
    r2 = [jnp.sum(dxf[at(c, h)] * xs[c][h], axis=1, keepdims=True) for c, h in items]
    tt = [jnp.where(lane < HALF, jnp.broadcast_to(r1[n], (BLK, BLK)), jnp.broadcast_to(r2[n], (BLK, BLK))).T
          for n in range(ni)]
    r1_row = [tt[n][0:1, :] for n in range(ni)]
    r2_row = [tt[n][HALF:HALF + 1, :] for n in range(ni)]
    d_el = [full_sum(dhs[at(c, h)] * prev[c][h]) for c, h in items]
    da_last = [jnp.sum(r2_row[n] * f_row[n], axis=1, keepdims=True) + el[n] * d_el[n] for n in range(ni)]
    da_row = [r1_row[n] - jnp.sum(dseg[n], axis=0, keepdims=True) - r2_row[n] * f_row[n]
              + jnp.where(lane_row == BLK - 1, da_last[n], 0.0) for n in range(ni)]
    ddt_row = [jnp.sum(zt[n], axis=0, keepdims=True) + r2_row[n] * e_row[n] for n in range(ni)]
    draw, dalog = [], jnp.zeros((N_HEADS, BLK), F32)
    for c in range(nq):
        da_t = jnp.zeros((N_HEADS, BLK), F32)
        ddt_t = jnp.zeros((N_HEADS, BLK), F32)
        for h in range(N_HEADS):
            da_t = jnp.where(head == h, da_row[at(c, h)], da_t)
            ddt_t = jnp.where(head == h, ddt_row[at(c, h)], ddt_t)
        d_dta = _mm_hi(da_t, causal.astype(F32))
        dalog = dalog + d_dta * dt_t[c] * a_neg
        draw.append((ddt_t + d_dta * a_neg) * jax.nn.sigmoid(pre_dt[c]))
    ddtb = draw[0]
    for c in range(1, nq):
        ddtb = ddtb + draw[c]
    dd_rows = []
    for h in range(N_HEADS):
        t = dd_item[at(0, h)]
        for c in range(1, nq):
            t = t + dd_item[at(c, h)]
        dd_rows.append(t)
    return ([dxs[c * N_HEADS:(c + 1) * N_HEADS] for c in range(nq)], dbg, dcg, draw,
            [dprev[at(0, h)] for h in range(N_HEADS)], ddtb, dalog, dd_rows)


def _dt_rows(dt_blk):
    return dt_blk.T[:N_HEADS]


def _conv_pre(halo, blk, cw_ref, cb_ref):
    ext = jnp.concatenate([halo, blk], axis=0)
    taps = [pltpu.roll(ext, 3 - k, 0)[8:] for k in range(3)] + [blk]
    pre = cb_ref[...] + cw_ref[0:1, :] * taps[0]
    for k in range(1, 4):
        pre = pre + cw_ref[k:k + 1, :] * taps[k]
    return pre


def _ssd_split(pre):
    heads = _split_heads(pre[:, :SSM_W], 4)
    pb = [pre[:, SSM_W + g * D_STATE:SSM_W + (g + 1) * D_STATE] for g in range(2)]
    pc = [pre[:, SSM_W + 2 * D_STATE + g * D_STATE:SSM_W + 2 * D_STATE + (g + 1) * D_STATE] for g in range(2)]
    return heads, pb, pc


def _ssd_fwd(xbc, dt_raw, conv_w, conv_b, dtb_row, alog_row, d_exp, xchg):
    S = xbc.shape[0]
    nc = S // BLK
    nq = SSD_CHUNKS_PER_STEP if nc % SSD_CHUNKS_PER_STEP == 0 else 1
    rows = nq * BLK

    def body(xbc_ref, halo_ref, dt_ref, cw_ref, cb_ref, dtb_ref, alog_ref, d_ref, y_ref, prev_ref, pre_ref, state_ref):
        i = pl.program_id(0)

        @pl.when(i == 0)
        def _():
            state_ref[...] = jnp.zeros_like(state_ref)

        halo = halo_ref[...] * jnp.where(i > 0, 1.0, 0.0)
        pre = _conv_pre(halo, xbc_ref[...], cw_ref, cb_ref)
        pre_ref[...] = pre
        xc = _silu(pre)
        split = [_ssd_split(xc[c * BLK:(c + 1) * BLK]) for c in range(nq)]
        dt_t = [_dt_rows(dt_ref[c * BLK:(c + 1) * BLK, :]) for c in range(nq)]
        prev0 = [state_ref[h] for h in range(N_HEADS)]
        d_rows = [d_ref[h:h + 1, :] for h in range(N_HEADS)]
        ys, states = _ssd_chunks([s[0] for s in split], [s[1] for s in split], [s[2] for s in split], dt_t, prev0,
                                 dtb_ref[...], alog_ref[...], d_rows, _ssd_consts())
        for h in range(N_HEADS):
            for c in range(nq):
                prev_ref[c, h] = states[c][h]
            state_ref[h] = states[nq][h]
        y_ref[...] = jnp.concatenate([_join_heads(ys[c]) for c in range(nq)], axis=0)

    vec = pl.BlockSpec((N_HEADS, LANE), _fixed)
    return _hosted_call(
        body, "ssd_fwd", nc // nq,
        in_specs=[pl.BlockSpec((rows, XBC_W), _row),
                  pl.BlockSpec((8, XBC_W), lambda i: (jnp.maximum(i * (rows // 8) - 1, 0), 0)),
                  pl.BlockSpec((rows, LANE), _row),
                  pl.BlockSpec((4, XBC_W), _fixed), pl.BlockSpec((1, XBC_W), _fixed), vec, vec,
                  pl.BlockSpec((N_HEADS, LANE), _fixed)],
        out_specs=[pl.BlockSpec((rows, SSM_W), _row),
                   pl.BlockSpec((nq, N_HEADS, D_STATE, LANE), lambda i: (i, 0, 0, 0)),
                   pl.BlockSpec((rows, XBC_W), _row)],
        out_shape=[jax.ShapeDtypeStruct((S, SSM_W), F32), jax.ShapeDtypeStruct((nc, N_HEADS, D_STATE, LANE), F32),
                   jax.ShapeDtypeStruct((S, XBC_W), F32)],
        scratch_shapes=[pltpu.VMEM((N_HEADS, D_STATE, LANE), F32)],
        args=(xbc, xbc, dt_raw, conv_w, conv_b, dtb_row, alog_row, d_exp), xchg=xchg, cparams=_cparams(),
    )


def _ssd_bwd(xbc, pre_act, dt_raw, prev_states, dy, conv_w, dtb_row, alog_row, d_exp, xchg):
    S = xbc.shape[0]
    nc = S // BLK
    nq = SSD_BWD_CHUNKS_PER_STEP if nc % SSD_BWD_CHUNKS_PER_STEP == 0 else 1
    rows, n_steps = nq * BLK, nc // nq

    def body(xbc_ref, halo_ref, pre_ref, dt_ref, prev_ref, dy_ref, cw_ref, dtb_ref, alog_ref, d_ref,
             dxbc_ref, ddt_ref, dcw_ref, dvec_ref, dd_ref, gstate_ref, ghalo_ref):
        i = pl.program_id(0)

        @pl.when(i == 0)
        def _():
            gstate_ref[...] = jnp.zeros_like(gstate_ref)
            ghalo_ref[...] = jnp.zeros_like(ghalo_ref)
            dcw_ref[...] = jnp.zeros_like(dcw_ref)
            dvec_ref[...] = jnp.zeros_like(dvec_ref)
            dd_ref[...] = jnp.zeros_like(dd_ref)

        halo = halo_ref[...] * jnp.where(i < n_steps - 1, 1.0, 0.0)
        ext = jnp.concatenate([halo, xbc_ref[...]], axis=0)
        pre = pre_ref[...]
        sig = jax.nn.sigmoid(pre)
        xc = pre * sig
        split = [_ssd_split(xc[c * BLK:(c + 1) * BLK]) for c in range(nq)]
        dt_t = [_dt_rows(dt_ref[c * BLK:(c + 1) * BLK, :]) for c in range(nq)]
        prev = [[prev_ref[c, h] for h in range(N_HEADS)] for c in range(nq)]
        d_rows = [d_ref[h:h + 1, :] for h in range(N_HEADS)]
        dys = [_split_heads(dy_ref[c * BLK:(c + 1) * BLK, :], 4) for c in range(nq)]
        dh_last = [gstate_ref[h] for h in range(N_HEADS)]
        dheads, dpb, dpc, ddt_t, dprev0, ddtb, dalog, dd_rows = _ssd_chunks_bwd(
            [s[0] for s in split], [s[1] for s in split], [s[2] for s in split], dt_t, prev, dtb_ref[...],
            alog_ref[...], d_rows, dys, dh_last, _ssd_consts())
        for h in range(N_HEADS):
            gstate_ref[h] = dprev0[h]
            dd_ref[h:h + 1, :] += dd_rows[h]
        pad = jnp.zeros((BLK - N_HEADS, BLK), F32)
        ddt_ref[...] = jnp.concatenate([jnp.concatenate([ddt_t[c], pad], axis=0).T for c in range(nq)],
                                       axis=0).astype(ddt_ref.dtype)
        dvec_ref[0:N_HEADS, :] += ddtb
        dvec_ref[N_HEADS:, :] += dalog
        dxc = jnp.concatenate([jnp.concatenate([_join_heads(dheads[c])] + list(dpb[c]) + list(dpc[c]), axis=1)
                               for c in range(nq)], axis=0)
        dpre = dxc * (sig * (1.0 + pre * (1.0 - sig)))
        zeros8 = jnp.zeros((8, XBC_W), F32)
        dpe = jnp.concatenate([zeros8, dpre, zeros8], axis=0)
        n_ext = 16 + rows
        shifted = [pltpu.roll(dpe, n_ext - (3 - k), 0)[:8 + rows] for k in range(3)] + [dpe[:8 + rows]]
        dext = cw_ref[0:1, :] * shifted[0]
        for k in range(1, 4):
            dext = dext + cw_ref[k:k + 1, :] * shifted[k]
        for k in range(4):
            dcw_ref[k:k + 1, :] += jnp.sum(shifted[k] * ext, axis=0, keepdims=True)
        dcw_ref[4:5, :] += jnp.sum(dpre, axis=0, keepdims=True)
        dxbc_ref[...] = jnp.concatenate([dext[8:rows], dext[rows:] + ghalo_ref[...]], axis=0).astype(dxbc_ref.dtype)
        ghalo_ref[...] = dext[:8, :]

    vec = pl.BlockSpec((N_HEADS, LANE), _fixed)
    rev = lambda i: (n_steps - 1 - i, 0)
    return _hosted_call(
        body, "ssd_bwd", n_steps,
        in_specs=[pl.BlockSpec((rows, XBC_W), rev),
                  pl.BlockSpec((8, XBC_W), lambda i: (jnp.maximum((n_steps - 1 - i) * (rows // 8) - 1, 0), 0)),
                  pl.BlockSpec((rows, XBC_W), rev),
                  pl.BlockSpec((rows, LANE), rev),
                  pl.BlockSpec((nq, N_HEADS, D_STATE, LANE), lambda i: (n_steps - 1 - i, 0, 0, 0)),
                  pl.BlockSpec((rows, SSM_W), rev),
                  pl.BlockSpec((4, XBC_W), _fixed), vec, vec,
                  pl.BlockSpec((N_HEADS, LANE), _fixed)],
        out_specs=[pl.BlockSpec((rows, XBC_W), rev), pl.BlockSpec((rows, LANE), rev),
                   pl.BlockSpec((8, XBC_W), _fixed), pl.BlockSpec((2 * N_HEADS, LANE), _fixed),
                   pl.BlockSpec((N_HEADS, LANE), _fixed)],
        out_shape=[jax.ShapeDtypeStruct((S, XBC_W), MXU_DTYPE), jax.ShapeDtypeStruct((S, LANE), MXU_DTYPE),
                   jax.ShapeDtypeStruct((8, XBC_W), F32), jax.ShapeDtypeStruct((2 * N_HEADS, LANE), F32),
                   jax.ShapeDtypeStruct((N_HEADS, LANE), F32)],
        scratch_shapes=[pltpu.VMEM((N_HEADS, D_STATE, LANE), F32), pltpu.VMEM((8, XBC_W), F32)],
        args=(xbc, xbc, pre_act, dt_raw, prev_states, dy, conv_w, dtb_row, alog_row, d_exp), xchg=xchg,
        cparams=_cparams(VMEM_BIG),
    )


def _adamw_math(w, g, m, v):
    m = ADAM_B1 * m + (1.0 - ADAM_B1) * g
    v = ADAM_B2 * v + (1.0 - ADAM_B2) * jnp.square(g)
    m_hat = m / (1.0 - ADAM_B1 ** ADAM_STEP)
    v_hat = v / (1.0 - ADAM_B2 ** ADAM_STEP)
    delta = -ADAM_LR * (m_hat / (jnp.sqrt(v_hat) + ADAM_EPS) + ADAM_WD * w)
    return delta, m, v


def _reduce_adamw_halves(part_a, part_b, w, m, v, name):
    R, C = w.shape
    P = part_a.shape[0]
    tl = 256
    n = C // tl

    def body(a_ref, b_ref, w_ref, m_ref, v_ref, g_ref, d_ref, nm_ref, nv_ref):
        ga, gb = a_ref[0].astype(F32), b_ref[0].astype(F32)
        for i in range(1, P):
            ga, gb = ga + a_ref[i].astype(F32), gb + b_ref[i].astype(F32)
        first = jnp.where(pl.program_id(0) < n // 2, 1.0, 0.0)
        g = ga * first + gb * (1.0 - first)
        d, nm, nv = _adamw_math(w_ref[...], g, m_ref[...], v_ref[...])
        g_ref[...] = g
        d_ref[...] = d
        nm_ref[...] = nm
        nv_ref[...] = nv

    blk = pl.BlockSpec((R, tl), lambda i: (0, i))
    return pl.pallas_call(
        body, name=name, grid=(n,),
        in_specs=[pl.BlockSpec((P, R, tl), lambda i: (0, 0, jnp.minimum(i, n // 2 - 1))),
                  pl.BlockSpec((P, R, tl), lambda i: (0, 0, jnp.maximum(i - n // 2, 0))), blk, blk, blk],
        out_specs=[blk] * 4, out_shape=[jax.ShapeDtypeStruct((R, C), F32)] * 4,
    )(part_a, part_b, w, m, v)


def _reduce_adamw_hosting(parts_list, wmv_list, name, xchg):
    n_arr = len(parts_list)
    pieces = [list(p) if isinstance(p, (tuple, list)) else [p] for p in parts_list]
    n_pieces = sum(len(p) for p in pieces)
    C = wmv_list[0][0].shape[1]
    tl = 256

    def total(ref):
        g = ref[0].astype(F32)
        for i in range(1, N_DEV):
            g = g + ref[i].astype(F32)
        return g

    def body(*refs):
        p_refs, wmv_refs, o_refs = refs[:n_pieces], refs[n_pieces:n_pieces + 3 * n_arr], refs[n_pieces + 3 * n_arr:]
        at = 0
        for k in range(n_arr):
            sums = [total(r) for r in p_refs[at:at + len(pieces[k])]]
            at += len(pieces[k])
            g = sums[0] if len(sums) == 1 else jnp.concatenate(sums, axis=0)
            w_ref, m_ref, v_ref = wmv_refs[3 * k:3 * k + 3]
            d, nm, nv = _adamw_math(w_ref[...], g, m_ref[...], v_ref[...])
            for o, val in zip(o_refs[4 * k:4 * k + 4], (g, d, nm, nv)):
                o[...] = val

    in_specs = [pl.BlockSpec((N_DEV, p.shape[1], tl), lambda i: (0, 0, i)) for group in pieces for p in group]
    in_specs += [pl.BlockSpec((w.shape[0], tl), lambda i: (0, i)) for w, _, _ in wmv_list for _ in range(3)]
    out_specs = [pl.BlockSpec((w.shape[0], tl), lambda i: (0, i)) for w, _, _ in wmv_list for _ in range(4)]
    out_shape = [jax.ShapeDtypeStruct(w.shape, F32) for w, _, _ in wmv_list for _ in range(4)]
    args = [p for group in pieces for p in group] + [a for wmv in wmv_list for a in wmv]
    outs, x_out = _hosted_call(body, name, C // tl, in_specs, out_specs, out_shape, [], args, xchg,
                               _cparams(VMEM_BIG))
    return [outs[4 * k:4 * k + 4] for k in range(n_arr)], x_out


_SMALL_NAMES = ("ada_b", "norm1", "conv_w", "conv_b", "dt_bias", "A_log", "D_skip", "sinks", "attn_out_norm",
                "ssm_out_norm", "norm2", "rel_bias", "final_norm")
N_MOD = 6 * D_MODEL


def _mod_row(a0, a1, a2):
    return jnp.concatenate([a0[2:3], a0[1:2], a1[0:1], a2[2:3], a2[1:2], a2[3:4]], axis=1)


def _small_update(gathered, params):
    n_g = len(gathered)
    flat = [a for name in _SMALL_NAMES for a in params[name]]

    def body(*refs):
        a0_ref, a1_ref, a2_ref, cw_ref, dv_ref, dd_ref, ds_ref, dr_ref, c_ref = refs[:n_g]
        wmv = refs[n_g:n_g + len(flat)]
        outs = refs[n_g + len(flat):]

        def total(ref):
            t = ref[0]
            for i in range(1, N_DEV):
                t = t + ref[i]
            return t

        t0, t1, t2, tcw, tdv, tdd, tds, tdr = [total(r) for r in (a0_ref, a1_ref, a2_ref, cw_ref, dv_ref, dd_ref,
                                                                   ds_ref, dr_ref)]
        r8 = lax.broadcasted_iota(jnp.int32, (N_HEADS, LANE), 0)
        l8 = lax.broadcasted_iota(jnp.int32, (N_HEADS, LANE), 1)

        def diag_row(t):
            return jnp.sum(jnp.where(r8 == l8, t, 0.0), axis=0, keepdims=True)[:, :N_HEADS]

        def lane_sums(t):
            return diag_row(jnp.broadcast_to(jnp.sum(t, axis=1, keepdims=True), (N_HEADS, LANE)))

        me = _lin(_my_pos())
        n_cw = XBC_W // N_DEV
        cw_mine = jnp.zeros((4, n_cw), F32)
        for j in range(N_DEV):
            cw_mine = cw_mine + tcw[0:4, j * n_cw:(j + 1) * n_cw] * jnp.where(me == j, 1.0, 0.0)
        grads = {
            "ada_b": _mod_row(t0, t1, t2), "norm1": t0[0:1], "conv_w": cw_mine, "conv_b": tcw[4:5],
            "dt_bias": lane_sums(tdv[:N_HEADS]), "A_log": lane_sums(tdv[N_HEADS:]), "D_skip": lane_sums(tdd),
            "sinks": diag_row(tds), "attn_out_norm": t1[1:2, :ATTN_W], "ssm_out_norm": t1[1:2, ATTN_W:],
            "norm2": t2[0:1], "rel_bias": tdr[:, :N_HEADS], "final_norm": t2[4:5],
        }
        for k, name in enumerate(_SMALL_NAMES):
            w_ref, m_ref, v_ref = wmv[3 * k:3 * k + 3]
            g = grads[name]
            d, nm, nv = _adamw_math(w_ref[...], g, m_ref[...], v_ref[...])
            for o, val in zip(outs[4 * k:4 * k + 4], (g, d, nm, nv)):
                o[...] = val
        loss_ref, call_ref, dmod_ref = outs[4 * len(_SMALL_NAMES):]
        loss_ref[...] = t2[5:6, 0:1]
        call_ref[...] = jnp.concatenate([c_ref[i] for i in range(N_DEV)], axis=0)
        dmod_ref[...] = jnp.concatenate([_mod_row(a0_ref[i], a1_ref[i], a2_ref[i]) for i in range(N_DEV)], axis=0)

    out_shape = [jax.ShapeDtypeStruct(params[name][0].shape, F32) for name in _SMALL_NAMES for _ in range(4)]
    out_shape += [jax.ShapeDtypeStruct((1, 1), F32), jax.ShapeDtypeStruct((N_DEV, D_MODEL), F32),
                  jax.ShapeDtypeStruct((N_DEV, N_MOD), F32)]
    res = pl.pallas_call(body, name="small_update", out_shape=out_shape)(*gathered, *flat)
    upd = {name: res[4 * k:4 * k + 4] for k, name in enumerate(_SMALL_NAMES)}
    loss, c_all, dmod_all = res[4 * len(_SMALL_NAMES):]
    return upd, loss, c_all, dmod_all


def _ada_w_update(c_all, dmod_all, w, m, v):
    chunk = w.shape[1]

    def body(c_ref, dm_ref, w_ref, m_ref, v_ref, g_ref, d_ref, nm_ref, nv_ref):
        me = _lin(_my_pos())
        dm = jnp.zeros((N_DEV, chunk), F32)
        for j in range(N_DEV):
            dm = dm + dm_ref[:, j * chunk:(j + 1) * chunk] * jnp.where(me == j, 1.0, 0.0)
        g = lax.dot_general(_silu(c_ref[...]), dm, (((0,), (0,)), ((), ())), precision=HI,
                            preferred_element_type=F32)
        d, nm, nv = _adamw_math(w_ref[...], g, m_ref[...], v_ref[...])
        g_ref[...] = g
        d_ref[...] = d
        nm_ref[...] = nm
        nv_ref[...] = nv

    tr = 256
    blk = pl.BlockSpec((tr, chunk), _row)
    return pl.pallas_call(
        body, name="ada_w_update", grid=(w.shape[0] // tr,),
        in_specs=[pl.BlockSpec((N_DEV, tr), lambda i: (0, i)), pl.BlockSpec(dmod_all.shape, _fixed), blk, blk, blk],
        out_specs=[blk] * 4, out_shape=[jax.ShapeDtypeStruct(w.shape, F32)] * 4,
    )(c_all, dmod_all, w, m, v)


def _local_step(x, tgt, c, mod, w_in, conv_w, w_o_mine, w_gu_mine, w_d_mine, p):
    S = x.shape[0]
    tm = min(512, S)
    tmm = min(256, S)
    tw = min(2048, S)
    shift1, scale1, gate1, shift2, scale2, gate2 = [mod[i:i + 1] for i in range(6)]
    buckets = jnp.asarray(_t5_bucket_table())
    per_head = lambda a: jnp.broadcast_to(a.reshape(N_HEADS, 1), (N_HEADS, LANE))
    dtb_row, alog_row, d_exp = per_head(p["dt_bias"]), per_head(p["A_log"]), per_head(p["D_skip"])
    sinks = p["sinks"].reshape(N_HEADS)

    d_cut, gu_cut = WD_CUT, WGU_CUTS
    n_d, n_gu = w_d_mine.shape[0], w_gu_mine.shape[0]
    (qkv, z, xbc, dt_raw), (g_d_a,) = _in_proj_fwd(x, p["norm1"], scale1, shift1, w_in, tm,
                                                   ([(w_d_mine, 0, d_cut)], "two-level"))
    bias = _attn_bias(buckets, p["rel_bias"])
    (ya,), (g_gu_a,) = _attn_fwd(qkv, bias, sinks, ([(w_gu_mine, 0, gu_cut[0])], "two-level"))
    (ys, prev_states, pre_act), (g_gu_b, g_o) = _ssd_fwd(
        xbc, dt_raw, conv_w, p["conv_b"], dtb_row, alog_row, d_exp,
        ([(w_gu_mine, gu_cut[0], gu_cut[1] - gu_cut[0]), w_o_mine], "two-level"))
    w_o = g_o.reshape(D_MODEL, D_MODEL)
    x1, (g_gu_c, g_d_b) = _out_proj_fwd(
        x, ya, ys, z, p["attn_out_norm"], p["ssm_out_norm"], gate1, w_o, tm,
        ([(w_gu_mine, gu_cut[1], n_gu - gu_cut[1]), (w_d_mine, d_cut, n_d - d_cut)], "two-level"))
    dx1, h2, dgu, act, dmlp, acc2 = _mlp_loss(x1, tgt, p["norm2"], scale2, shift2, gate2, p["final_norm"],
                                              (g_gu_a, g_gu_b, g_gu_c), (g_d_a, g_d_b), tmm)
    g_w_gu = _wgrad(dgu, h2, 2 * D_FF // 4, tw, "wgrad_gate_up")
    g_w_d = _wgrad(act, dmlp, D_FF // 2, tw, "wgrad_down")
    gu_slots = g_w_gu.reshape(N_DEV, 2 * D_FF // N_DEV, D_MODEL)
    (dya, dys, dz, g_w_o, acc1), (r_gu_a,) = _out_proj_bwd(
        dx1, ya, ys, z, p["attn_out_norm"], p["ssm_out_norm"], gate1, w_o, tm, ([gu_slots], ("rows", 0, GGU_CUT)))
    (dq, dkv, dbias, dsk), (r_d, r_o) = _attn_bwd(
        qkv, ya, dya, bias, sinks,
        ([g_w_d.reshape(N_DEV, D_FF // N_DEV, D_MODEL), g_w_o.reshape(N_DEV, D_MODEL // N_DEV, D_MODEL)], True))
    drel, dsink = _attn_finish(dbias, dsk, buckets)
    (dxbc, ddt, dcw, dvec, dd), (r_gu_b, *early) = _ssd_bwd(
        xbc, pre_act, dt_raw, prev_states, dys, conv_w, dtb_row, alog_row, d_exp,
        [([gu_slots], ("rows", GGU_CUT, 2 * D_FF // N_DEV - GGU_CUT)), ([acc1, acc2, dsink, drel, c], False)])
    r_gu = (r_gu_a, r_gu_b)
    gx, h1, acc0, g_in_a = _in_proj_bwd(x, dx1, dq, dkv, dz, dxbc, ddt, p["norm1"], scale1, shift1, w_in, tm)
    half = D_MODEL // 2
    slots = lambda g: g[:IN_W].reshape(N_DEV, IN_W // N_DEV, half)
    g_in_b, (r_in_a, *late) = _wgrad((dq, dkv, dz, dxbc, ddt), h1, IN_PAD, tw, "wgrad_in_b",
                                     [([slots(g_in_a)], "two-level scatter"), ([acc0, dcw, dvec, dd], False)],
                                     g_cols=(half, 1))
    gathered = (late[0], early[0], early[1], late[1], late[2], late[3], early[2], early[3], early[4])
    return gx, (r_in_a, slots(g_in_b)), (r_o, r_gu, r_d), gathered


def kernel(x, c, ada_w, ada_b, norm1, w_in, conv_w, conv_b, dt_bias, A_log, D_skip, sinks, attn_out_norm, ssm_out_norm, w_o, norm2, w_gate_up, w_down, rel_bias, final_norm, loss_target, m_ada_w, m_ada_b, m_norm1, m_w_in, m_conv_w, m_conv_b, m_dt_bias, m_A_log, m_D_skip, m_sinks, m_attn_out_norm, m_ssm_out_norm, m_w_o, m_norm2, m_w_gate_up, m_w_down, m_rel_bias, m_final_norm, v_ada_w, v_ada_b, v_norm1, v_w_in, v_conv_w, v_conv_b, v_dt_bias, v_A_log, v_D_skip, v_sinks, v_attn_out_norm, v_ssm_out_norm, v_w_o, v_norm2, v_w_gate_up, v_w_down, v_rel_bias, v_final_norm):
    two_d = lambda a: a if a.ndim == 2 else a.reshape(-1, a.shape[-1])
    small_params = dict(
        ada_b=(ada_b, m_ada_b, v_ada_b), norm1=(norm1, m_norm1, v_norm1), conv_w=(conv_w, m_conv_w, v_conv_w),
        conv_b=(conv_b, m_conv_b, v_conv_b), dt_bias=(dt_bias, m_dt_bias, v_dt_bias), A_log=(A_log, m_A_log, v_A_log),
        D_skip=(D_skip, m_D_skip, v_D_skip), sinks=(sinks, m_sinks, v_sinks),
        attn_out_norm=(attn_out_norm, m_attn_out_norm, v_attn_out_norm),
        ssm_out_norm=(ssm_out_norm, m_ssm_out_norm, v_ssm_out_norm), norm2=(norm2, m_norm2, v_norm2),
        rel_bias=(rel_bias, m_rel_bias, v_rel_bias), final_norm=(final_norm, m_final_norm, v_final_norm))
    small_params = {k: tuple(two_d(a) for a in v) for k, v in small_params.items()}
    S = x.shape[1]
    xs, tgt = x.reshape(S, D_MODEL), loss_target.reshape(S, D_MODEL)
    ada_w2 = ada_w[0]
    chunk = ada_w2.shape[1]
    t_in = [jnp.transpose(a[0]) for a in (w_in, m_w_in, v_w_in)]
    t_gu = [jnp.transpose(a[0]) for a in (w_gate_up, m_w_gate_up, v_w_gate_up)]

    mod, (g_in, g_cw) = _mod_and_gather(c, ada_w2, ada_b.reshape(N_DEV, chunk), [t_in[0].astype(WIRE_DTYPE), conv_w[0]])
    mod = mod.reshape(6, D_MODEL)
    w_in_full = jnp.pad(g_in.reshape(IN_W, D_MODEL), ((0, IN_PAD - IN_W), (0, 0)))
    conv_w_full = jnp.transpose(g_cw, (1, 0, 2)).reshape(4, XBC_W)

    p = {k: v[0] for k, v in small_params.items()}
    gx, (r_in_a, gw_in_b), (r_o, r_gu, r_d), gathered = _local_step(
        xs, tgt, c, mod, w_in_full, conv_w_full, w_o[0].astype(WIRE_DTYPE), t_gu[0].astype(WIRE_DTYPE),
        w_down[0].astype(WIRE_DTYPE), p)

    (u_gu, u_d, u_o), (r_in_b,) = _reduce_adamw_hosting(
        [r_gu, r_d, r_o], [tuple(t_gu), (w_down[0], m_w_down[0], v_w_down[0]), (w_o[0], m_w_o[0], v_w_o[0])],
        "adamw_big", ([gw_in_b], "two-level scatter"))

    small, loss, c_all, dmod_all = _small_update(gathered, small_params)

    big = {
        "ada_w": _ada_w_update(c_all, dmod_all, ada_w2, m_ada_w[0], v_ada_w[0]),
        "w_in": [jnp.transpose(a) for a in _reduce_adamw_halves(r_in_a, r_in_b, *t_in, "adamw_w_in")],
        "w_o": u_o,
        "w_gate_up": [jnp.transpose(a) for a in u_gu],
        "w_down": u_d,
    }
    big.update(small)

    order = ['ada_w', 'ada_b', 'norm1', 'w_in', 'conv_w', 'conv_b', 'dt_bias', 'A_log', 'D_skip', 'sinks',
             'attn_out_norm', 'ssm_out_norm', 'w_o', 'norm2', 'w_gate_up', 'w_down', 'rel_bias', 'final_norm']
    shapes = dict(ada_w=ada_w.shape, ada_b=ada_b.shape, norm1=norm1.shape, w_in=w_in.shape, conv_w=conv_w.shape,
                  conv_b=conv_b.shape, dt_bias=dt_bias.shape, A_log=A_log.shape, D_skip=D_skip.shape,
                  sinks=sinks.shape, attn_out_norm=attn_out_norm.shape, ssm_out_norm=ssm_out_norm.shape,
                  w_o=w_o.shape, norm2=norm2.shape, w_gate_up=w_gate_up.shape, w_down=w_down.shape,
                  rel_bias=rel_bias.shape, final_norm=final_norm.shape)
    outs = [[], [], [], []]
    for name in order:
        for kind in range(4):
            outs[kind].append(big[name][kind].reshape(shapes[name]))
    return (loss.reshape(()), gx.reshape(x.shape), *outs[0], *outs[1], *outs[2], *outs[3])
```

```python
import numpy as np
import jax
import jax.numpy as jnp
from jax import lax
from jax.experimental import pallas as pl
from jax.experimental.pallas import tpu as pltpu

F32 = jnp.float32
MXU_DTYPE = jnp.bfloat16
WIRE_DTYPE = jnp.bfloat16
HI = lax.Precision.HIGHEST
MESH = pl.DeviceIdType.MESH
N_DEV = 8

D_MODEL = 1024
ATTN_W = 512
KV_W = 128
SSM_W = 512
XBC_W = 1024
N_HEADS = 8
D_STATE = 128
D_FF = 2816
IN_W = 2312
IN_PAD = 2432
BLK = 128
N_BUCKETS = 32
EPS = 1e-6
LANE = 128
HALF = 64

ADAM_LR, ADAM_B1, ADAM_B2, ADAM_EPS, ADAM_WD, ADAM_STEP = 0.001, 0.9, 0.999, 1e-08, 0.01, 10

VMEM_BIG = 56 * 1024 * 1024
WD_CUT = 288
WGU_CUTS = (240, 496)
GGU_CUT = 304


def _cparams(vmem=None):
    if vmem is None:
        return pltpu.CompilerParams()
    return pltpu.CompilerParams(vmem_limit_bytes=vmem)


def _mm(a, b):
    return jnp.dot(a.astype(MXU_DTYPE), b.astype(MXU_DTYPE), preferred_element_type=F32)


def _mm_nt(a, b):
    return lax.dot_general(a.astype(MXU_DTYPE), b.astype(MXU_DTYPE), (((1,), (1,)), ((), ())),
                           preferred_element_type=F32)


def _mm_tn(a, b):
    return lax.dot_general(a.astype(MXU_DTYPE), b.astype(MXU_DTYPE), (((0,), (0,)), ((), ())),
                           preferred_element_type=F32)


def _mm_hi(a, b):
    return jnp.dot(a, b, precision=HI, preferred_element_type=F32)


def _silu(x):
    return x * jax.nn.sigmoid(x)


def _softplus(x):
    return jnp.maximum(x, 0.0) + jnp.log1p(jnp.exp(-jnp.abs(x)))


def _rms(x, g, n):
    return x * lax.rsqrt(jnp.sum(x * x, axis=-1, keepdims=True) * (1.0 / n) + EPS) * g


def _modnorm(x, g, scale, shift):
    return _rms(x, g, x.shape[-1]) * (1.0 + scale) + shift


def _modnorm_parts(x):
    r = lax.rsqrt(jnp.sum(x * x, axis=-1, keepdims=True) * (1.0 / x.shape[-1]) + EPS)
    return r, x * r


def _modnorm_bwd(r, xhat, g, scale, dy):
    dyg = dy * (g * (1.0 + scale))
    c = jnp.sum(dyg * xhat, axis=-1, keepdims=True) * (1.0 / xhat.shape[-1])
    dx = r * (dyg - xhat * c)
    ct = jnp.sum(dy * xhat, axis=0, keepdims=True)
    return dx, ct * (1.0 + scale), ct * g, jnp.sum(dy, axis=0, keepdims=True)


def _lane_iota(shape):
    return lax.broadcasted_iota(jnp.int32, shape, len(shape) - 1)


def _split_pair(t):
    lane = _lane_iota(t.shape)
    lo = jnp.where(lane < HALF, t, 0.0)
    hi = pltpu.roll(jnp.where(lane >= HALF, t, 0.0), HALF, 1)
    return lo, hi


def _join_pair(lo, hi):
    lane = _lane_iota(lo.shape)
    return jnp.where(lane < HALF, lo, pltpu.roll(hi, HALF, 1))


def _split_heads(t, n_pairs):
    out = []
    for p in range(n_pairs):
        out.extend(_split_pair(t[:, p * LANE:(p + 1) * LANE]))
    return out


def _join_heads(hs):
    return jnp.concatenate([_join_pair(hs[2 * p], hs[2 * p + 1]) for p in range(len(hs) // 2)], axis=1)


def _t5_bucket_table():
    dist = np.arange(BLK)[:, None] + BLK - np.arange(2 * BLK)[None, :]
    n = np.maximum(dist, 0)
    max_exact = N_BUCKETS // 2
    large = max_exact + (np.log(np.maximum(n, 1) / max_exact) / np.log(128 / max_exact)
                         * (N_BUCKETS - max_exact)).astype(np.int32)
    large = np.minimum(large, N_BUCKETS - 1)
    return np.where(n < max_exact, n, large).astype(np.int32)


def _my_pos():
    return lax.axis_index("x"), lax.axis_index("y"), lax.axis_index("c")


def _peer(k):
    x, y, c = _my_pos()
    return (1 - x if k & 4 else x, 1 - y if k & 2 else y, 1 - c if k & 1 else c)


def _lin(pos):
    return 4 * pos[0] + 2 * pos[1] + pos[2]


def _xchg_copies(ins, outs, sems, scatter):
    local_sem, send_sem, recv_sem = sems
    me = _lin(_my_pos())

    def source(a, slot):
        if not scatter:
            return ins[a]
        if scatter is True:
            return ins[a].at[slot]
        return ins[a].at[slot, pl.ds(scatter[1], scatter[2])]

    local, remote = [], []
    for a in range(len(ins)):
        local.append(pltpu.make_async_copy(source(a, me), outs[a].at[me], local_sem.at[a]))
    for k in range(1, N_DEV):
        peer = _peer(k)
        for a in range(len(ins)):
            remote.append(pltpu.make_async_remote_copy(source(a, _lin(peer)), outs[a].at[me], send_sem.at[a, k - 1],
                                                       recv_sem.at[a, k - 1], device_id=peer, device_id_type=MESH))
    return local, remote


def _xchg_start(ins, outs, sems, scatter):
    local, remote = _xchg_copies(ins, outs, sems, scatter)
    for cp in local + remote:
        cp.start()


def _xchg_wait(ins, outs, sems, scatter):
    local, remote = _xchg_copies(ins, outs, sems, scatter)
    for cp in local:
        cp.wait()
    for cp in remote:
        cp.wait_send()
        cp.wait_recv()


def _xchg_shapes(arrs, scatter):
    n = len(arrs)
    if isinstance(scatter, tuple):
        out_shape = [jax.ShapeDtypeStruct((a.shape[0], scatter[2]) + a.shape[2:], a.dtype) for a in arrs]
    elif scatter:
        out_shape = [jax.ShapeDtypeStruct(a.shape, a.dtype) for a in arrs]
    else:
        out_shape = [jax.ShapeDtypeStruct((N_DEV,) + a.shape, a.dtype) for a in arrs]
    sems = [pltpu.SemaphoreType.DMA((n,)), pltpu.SemaphoreType.DMA((n, N_DEV - 1)),
            pltpu.SemaphoreType.DMA((n, N_DEV - 1))]
    return out_shape, sems


_CHIPS = (2, 4, 6)


def _g2_sems(n):
    dma = pltpu.SemaphoreType.DMA
    return [dma((n,)), dma((n, N_DEV)), dma((n, N_DEV)), dma((n, len(_CHIPS))), dma((n, len(_CHIPS)))]


class _TwoLevelGather:
    def __init__(self, ins, outs, sems, windows=None):
        self.ins, self.outs = ins, outs
        self.local_sem, self.send_sem, self.recv_sem, self.fsend_sem, self.frecv_sem = sems
        self.n = len(ins)
        self.windows = windows or [None] * self.n

    def _mine(self, a):
        w = self.windows[a]
        return self.ins[a] if w is None else self.ins[a].at[pl.ds(w[0], w[1])]

    def _direct(self, a, k):
        return pltpu.make_async_remote_copy(self._mine(a), self.outs[a].at[_lin(_my_pos())], self.send_sem.at[a, k],
                                            self.recv_sem.at[a, k], device_id=_peer(k), device_id_type=MESH)

    def _handed_on(self, a, j, origin):
        slot = self.outs[a].at[origin]
        return pltpu.make_async_remote_copy(slot, slot, self.fsend_sem.at[a, j], self.frecv_sem.at[a, j],
                                            device_id=_peer(1), device_id_type=MESH)

    def _local(self, a):
        return pltpu.make_async_copy(self._mine(a), self.outs[a].at[_lin(_my_pos())], self.local_sem.at[a])

    def start(self):
        for a in range(self.n):
            self._local(a).start()
        for k in (1,) + _CHIPS:
            for a in range(self.n):
                self._direct(a, k).start()

    def forward(self):
        for j, k in enumerate(_CHIPS):
            for a in range(self.n):
                self._direct(a, k).wait_recv()
                self._handed_on(a, j, _lin(_peer(k))).start()

    def finish(self):
        for a in range(self.n):
            self._direct(a, 1).wait_recv()
            for j, k in enumerate(_CHIPS):
                self._handed_on(a, j, _lin(_peer(k ^ 1))).wait_recv()
            self._local(a).wait()
            for k in (1,) + _CHIPS:
                self._direct(a, k).wait_send()
            for j, k in enumerate(_CHIPS):
                self._handed_on(a, j, _lin(_peer(k))).wait_send()


N_SCATTERED = 2 + len(_CHIPS)


class _TwoLevelScatter:
    def __init__(self, src, out, stage, buf_a, buf_b, sems):
        self.src, self.out, self.stage, self.buf_a, self.buf_b = src, out, stage, buf_a, buf_b
        self.local_sem, self.dsend, self.drecv, self.csend, self.crecv = sems

    def _own(self):
        return pltpu.make_async_copy(self.src.at[_lin(_my_pos())], self.out.at[0], self.local_sem.at[0])

    def _to_core(self, j):
        q = 0 if j == 0 else _CHIPS[j - 1]
        dst = self.out.at[1] if j == 0 else self.stage.at[j - 1]
        return pltpu.make_async_remote_copy(self.src.at[_lin(_peer(q ^ 1))], dst, self.dsend.at[j], self.drecv.at[j],
                                            device_id=_peer(1), device_id_type=MESH)

    def _loads(self, j):
        mine = self.src.at[_lin(_peer(_CHIPS[j]))]
        return (pltpu.make_async_copy(self.stage.at[j], self.buf_a.at[j], self.local_sem.at[1 + 2 * j]),
                pltpu.make_async_copy(mine, self.buf_b.at[j], self.local_sem.at[2 + 2 * j]))

    def _to_chip(self, j):
        return pltpu.make_async_remote_copy(self.buf_a.at[j], self.out.at[2 + j], self.csend.at[j], self.crecv.at[j],
                                            device_id=_peer(_CHIPS[j]), device_id_type=MESH)

    def start(self):
        self._own().start()
        for j in range(1 + len(_CHIPS)):
            self._to_core(j).start()

    def forward(self):
        for j in range(len(_CHIPS)):
            self._to_core(j + 1).wait_recv()
            for cp in self._loads(j):
                cp.start()
        for j in range(len(_CHIPS)):
            for cp in self._loads(j):
                cp.wait()
            self.buf_a[j] = (self.buf_a[j].astype(F32) + self.buf_b[j].astype(F32)).astype(self.buf_a.dtype)
            self._to_chip(j).start()

    def finish(self):
        self._own().wait()
        self._to_core(0).wait_recv()
        for j in range(1 + len(_CHIPS)):
            self._to_core(j).wait_send()
        for j in range(len(_CHIPS)):
            self._to_chip(j).wait_send()
            self._to_chip(j).wait_recv()


def _s2_shapes(a):
    dma = pltpu.SemaphoreType.DMA
    n_c = len(_CHIPS)
    piece = a.shape[1:]
    return (jax.ShapeDtypeStruct((N_SCATTERED,) + piece, a.dtype), jax.ShapeDtypeStruct((n_c,) + piece, a.dtype),
            [pltpu.VMEM((n_c,) + piece, a.dtype)] * 2,
            [dma((1 + 2 * n_c,)), dma((1 + n_c,)), dma((1 + n_c,)), dma((n_c,)), dma((n_c,))])


def _mod_and_gather(c, ada_w, ada_b8, arrs):
    n = len(arrs)
    chunk = ada_w.shape[1]
    out_shape = [jax.ShapeDtypeStruct((N_DEV, 1, chunk), F32)]
    out_shape += [jax.ShapeDtypeStruct((N_DEV,) + a.shape, a.dtype) for a in arrs]

    def modulation(c_ref, w_ref, b_ref, out_ref, cbuf, part, s1, r1, s2, r2):
        me = _lin(_my_pos())
        first = []
        for k in range(1, N_DEV):
            cp = pltpu.make_async_remote_copy(c_ref, cbuf.at[me], s1.at[k - 1], r1.at[k - 1],
                                              device_id=_peer(k), device_id_type=MESH)
            cp.start()
            first.append(cp)
        cbuf[me] = c_ref[...]
        for cp in first:
            cp.wait_send()
            cp.wait_recv()
        cond = _silu(jnp.concatenate([cbuf[i] for i in range(N_DEV)], axis=0))
        mod = _mm_hi(cond, w_ref[...]) + b_ref[pl.ds(me, 1), :]
        for j in range(N_DEV):
            part[j] = mod[j:j + 1, :]
        second = []
        for k in range(1, N_DEV):
            peer = _peer(k)
            cp = pltpu.make_async_remote_copy(part.at[_lin(peer)], out_ref.at[me], s2.at[k - 1], r2.at[k - 1],
                                              device_id=peer, device_id_type=MESH)
            cp.start()
            second.append(cp)
        out_ref[me] = part[me]
        for cp in second:
            cp.wait_send()
            cp.wait_recv()

    def body(*refs):
        c_ref, w_ref, b_ref = refs[:3]
        ins = refs[3:3 + n]
        mod_ref = refs[3 + n]
        outs = refs[4 + n:4 + 2 * n]
        cbuf, part, s1, r1, s2, r2 = refs[4 + 2 * n:10 + 2 * n]
        gather = _TwoLevelGather(ins, outs, refs[10 + 2 * n:])
        gather.start()
        modulation(c_ref, w_ref, b_ref, mod_ref, cbuf, part, s1, r1, s2, r2)
        gather.forward()
        gather.finish()

    hbm = pl.BlockSpec(memory_space=pltpu.HBM)
    vm = pl.BlockSpec(memory_space=pltpu.VMEM)
    dma = pltpu.SemaphoreType.DMA
    res = pl.pallas_call(
        body, name="mod_and_gather", out_shape=out_shape, in_specs=[vm, vm, vm] + [hbm] * n,
        out_specs=[vm] + [hbm] * n,
        scratch_shapes=[pltpu.VMEM((N_DEV, 1, D_MODEL), F32), pltpu.VMEM((N_DEV, 1, chunk), F32)]
        + [dma((N_DEV - 1,))] * 4 + _g2_sems(n),
    )(c, ada_w, ada_b8, *arrs)
    return res[0], res[1:]


def _hosted_call(body, name, grid, in_specs, out_specs, out_shape, scratch_shapes, args, xchg, cparams):
    xchgs = [xchg] if isinstance(xchg, tuple) else list(xchg)
    grid = (grid,) if isinstance(grid, int) else tuple(grid)
    n_in, n_out, n_scr = len(in_specs), len(out_specs), len(scratch_shapes)
    windows = [[(a[1], a[2]) if isinstance(a, tuple) else None for a in group] for group, _ in xchgs]
    xchgs = [([a[0] if isinstance(a, tuple) else a for a in group], mode) for group, mode in xchgs]
    arrs = [a for group, _ in xchgs for a in group]
    n = len(arrs)
    x_shape, x_sems, sem_counts, stage_shape, stage_bufs = [], [], [], [], []
    for (group, mode), wins in zip(xchgs, windows):
        if mode == "two-level scatter":
            (a,) = group
            res_shape, stage, bufs, sems = _s2_shapes(a)
            shapes = [res_shape]
            stage_shape.append(stage)
            stage_bufs += bufs
        else:
            shapes, sems = _xchg_shapes(group, False if mode == "two-level" else mode)
        if mode == "two-level":
            sems = _g2_sems(len(group))
            shapes = [s if w is None else jax.ShapeDtypeStruct((N_DEV, w[1]) + a.shape[1:], a.dtype)
                      for s, w, a in zip(shapes, wins, group)]
        x_shape += shapes
        x_sems += sems
        sem_counts.append(len(sems))
    n_stage = len(stage_shape)
    n_steps = int(np.prod(grid))
    staged = ("two-level", "two-level scatter")

    def hosted(*refs):
        ins, refs = refs[:n_in], refs[n_in:]
        x_in, refs = refs[:n], refs[n:]
        outs, refs = refs[:n_out], refs[n_out:]
        x_out, refs = refs[:n], refs[n:]
        stages, refs = refs[:n_stage], refs[n_stage:]
        scr, refs = refs[:n_scr], refs[n_scr:]
        bufs, sems = refs[:2 * n_stage], refs[2 * n_stage:]
        step = pl.program_id(0)
        for d in range(1, len(grid)):
            step = step * grid[d] + pl.program_id(d)
        parts, a0, s0, t0 = [], 0, 0, 0
        for (group, mode), ns, wins in zip(xchgs, sem_counts, windows):
            gi, go, gs = x_in[a0:a0 + len(group)], x_out[a0:a0 + len(group)], sems[s0:s0 + ns]
            if mode == "two-level":
                parts.append((mode, _TwoLevelGather(gi, go, gs, wins)))
            elif mode == "two-level scatter":
                parts.append((mode, _TwoLevelScatter(gi[0], go[0], stages[t0], bufs[2 * t0], bufs[2 * t0 + 1], gs)))
                t0 += 1
            else:
                parts.append((mode, (gi, go, gs, mode)))
            a0, s0 = a0 + len(group), s0 + ns

        @pl.when(step == 0)
        def _():
            for mode, x in parts:
                if mode in staged:
                    x.start()
                else:
                    _xchg_start(*x)

        for kind, at in (("two-level", (2 * n_steps) // 3), ("two-level scatter", n_steps // 4)):
            if any(mode == kind for mode, _ in parts):
                @pl.when(step == at)
                def _():
                    for mode, x in parts:
                        if mode == kind:
                            x.forward()

        body(*ins, *outs, *scr)

        @pl.when(step == n_steps - 1)
        def _():
            for mode, x in parts:
                if mode in staged:
                    x.finish()
                else:
                    _xchg_wait(*x)

    hbm = pl.BlockSpec(memory_space=pltpu.HBM)
    res = pl.pallas_call(
        hosted, name=name, grid=grid, in_specs=list(in_specs) + [hbm] * n,
        out_specs=list(out_specs) + [hbm] * (n + n_stage), out_shape=list(out_shape) + x_shape + stage_shape,
        scratch_shapes=list(scratch_shapes) + stage_bufs + x_sems, compiler_params=cparams,
    )(*args, *arrs)
    return res[:n_out], res[n_out:n_out + n]


def _row(i):
    return (i, 0)


def _fixed(i):
    return (0, 0)


def _in_proj_fwd(x, norm1, scale1, shift1, w_in, tm, xchg):
    S = x.shape[0]

    def body(x_ref, n_ref, sc_ref, sh_ref, w_ref, qkv_ref, z_ref, xbc_ref, dt_ref):
        h = _modnorm(x_ref[...], n_ref[...], sc_ref[...], sh_ref[...])
        p = _mm_nt(h, w_ref[...])
        qkv_ref[...] = p[:, :768].astype(qkv_ref.dtype)
        z_ref[...] = p[:, 768:1280]
        xbc_ref[...] = p[:, 1280:2304]
        dt_ref[...] = p[:, 2304:IN_PAD]

    vec = pl.BlockSpec((1, D_MODEL), _fixed)
    return _hosted_call(
        body, "in_proj_fwd", S // tm,
        in_specs=[pl.BlockSpec((tm, D_MODEL), _row), vec, vec, vec, pl.BlockSpec((IN_PAD, D_MODEL), _fixed)],
        out_specs=[pl.BlockSpec((tm, 768), _row), pl.BlockSpec((tm, SSM_W), _row),
                   pl.BlockSpec((tm, XBC_W), _row), pl.BlockSpec((tm, LANE), _row)],
        out_shape=[jax.ShapeDtypeStruct((S, 768), MXU_DTYPE), jax.ShapeDtypeStruct((S, SSM_W), F32),
                   jax.ShapeDtypeStruct((S, XBC_W), F32), jax.ShapeDtypeStruct((S, LANE), F32)],
        scratch_shapes=[], args=(x, norm1, scale1, shift1, w_in), xchg=xchg, cparams=_cparams(VMEM_BIG),
    )


def _in_proj_bwd(x, dx1, dq, dkv, dz, dxbc, ddt, norm1, scale1, shift1, w_in, tm):
    S = x.shape[0]

    n_steps = S // tm
    half_cols = D_MODEL // 2

    def body(x_ref, dx1_ref, dq_ref, dkv_ref, dz_ref, dxbc_ref, ddt_ref, n_ref, sc_ref, sh_ref, w_ref,
             gx_ref, h_ref, acc_ref, gw_ref, gw_acc):
        i = pl.program_id(0)

        @pl.when(i == 0)
        def _():
            acc_ref[...] = jnp.zeros_like(acc_ref)
            gw_acc[...] = jnp.zeros_like(gw_acc)

        halves = [pl.ds(k * (tm // 2), tm // 2) for k in range(2)]
        dp = [jnp.concatenate([r[rows, :] for r in (dq_ref, dkv_ref, dz_ref, dxbc_ref, ddt_ref)], axis=1)
              for rows in halves]
        dh = [_mm(dp[k], w_ref[...]) for k in range(2)]
        parts = [_modnorm_parts(x_ref[rows, :]) for rows in halves]
        hb = [(parts[k][1] * n_ref[...] * (1.0 + sc_ref[...]) + sh_ref[...]).astype(h_ref.dtype) for k in range(2)]
        gw_acc[...] += _mm_tn(dp[0], hb[0][:, :half_cols]) + _mm_tn(dp[1], hb[1][:, :half_cols])
        bwd = [_modnorm_bwd(parts[k][0], parts[k][1], n_ref[...], sc_ref[...], dh[k]) for k in range(2)]
        for k, rows in enumerate(halves):
            gx_ref[rows, :] = dx1_ref[rows, :] + bwd[k][0]
            h_ref[rows, :] = hb[k]
        acc_ref[0:1, :] += bwd[0][1] + bwd[1][1]
        acc_ref[1:2, :] += bwd[0][2] + bwd[1][2]
        acc_ref[2:3, :] += bwd[0][3] + bwd[1][3]

        @pl.when(i == n_steps - 1)
        def _():
            gw_ref[...] = gw_acc[...].astype(gw_ref.dtype)

    vec = pl.BlockSpec((1, D_MODEL), _fixed)
    return pl.pallas_call(
        body, name="in_proj_bwd", grid=(n_steps,),
        in_specs=[pl.BlockSpec((tm, D_MODEL), _row), pl.BlockSpec((tm, D_MODEL), _row),
                  pl.BlockSpec((tm, ATTN_W), _row), pl.BlockSpec((tm, 2 * KV_W), _row),
                  pl.BlockSpec((tm, SSM_W), _row), pl.BlockSpec((tm, XBC_W), _row), pl.BlockSpec((tm, LANE), _row),
                  vec, vec, vec, pl.BlockSpec((IN_PAD, D_MODEL), _fixed)],
        out_specs=[pl.BlockSpec((tm, D_MODEL), _row), pl.BlockSpec((tm, D_MODEL), _row),
                   pl.BlockSpec((8, D_MODEL), _fixed), pl.BlockSpec((IN_PAD, half_cols), _fixed)],
        out_shape=[jax.ShapeDtypeStruct((S, D_MODEL), F32), jax.ShapeDtypeStruct((S, D_MODEL), MXU_DTYPE),
                   jax.ShapeDtypeStruct((8, D_MODEL), F32), jax.ShapeDtypeStruct((IN_PAD, half_cols), WIRE_DTYPE)],
        scratch_shapes=[pltpu.VMEM((IN_PAD, half_cols), F32)],
        compiler_params=_cparams(VMEM_BIG),
    )(x, dx1, dq, dkv, dz, dxbc, ddt, norm1, scale1, shift1, w_in)


def _out_stage(ya, ys0, ys1, z0, z1, an, sn0, sn1):
    half = SSM_W // 2
    a = _rms(ya, an, ATTN_W)
    g0 = _rms(ys0 * _silu(z0), sn0, half)
    g1 = _rms(ys1 * _silu(z1), sn1, half)
    return jnp.concatenate([a, g0, g1], axis=1)


def _out_stage_args(ya_ref, ys_ref, z_ref, an_ref, sn_ref):
    half = SSM_W // 2
    return (ya_ref[...], ys_ref[:, :half], ys_ref[:, half:], z_ref[:, :half], z_ref[:, half:],
            an_ref[...], sn_ref[:, :half], sn_ref[:, half:])


def _out_proj_fwd(x, ya, ys, z, an, sn, gate1, w_o, tm, xchg):
    S = x.shape[0]

    def body(x_ref, ya_ref, ys_ref, z_ref, an_ref, sn_ref, g_ref, w_ref, x1_ref):
        u = _out_stage(*_out_stage_args(ya_ref, ys_ref, z_ref, an_ref, sn_ref))
        x1_ref[...] = x_ref[...] + g_ref[...] * _mm(u, w_ref[...])

    half = pl.BlockSpec((tm, ATTN_W), _row)
    hvec = pl.BlockSpec((1, ATTN_W), _fixed)
    (x1,), x_out = _hosted_call(
        body, "out_proj_fwd", S // tm,
        in_specs=[pl.BlockSpec((tm, D_MODEL), _row), half, half, half, hvec, hvec,
                  pl.BlockSpec((1, D_MODEL), _fixed), pl.BlockSpec((D_MODEL, D_MODEL), _fixed)],
        out_specs=[pl.BlockSpec((tm, D_MODEL), _row)],
        out_shape=[jax.ShapeDtypeStruct((S, D_MODEL), F32)],
        scratch_shapes=[], args=(x, ya, ys, z, an, sn, gate1, w_o), xchg=xchg, cparams=_cparams(VMEM_BIG),
    )
    return x1, x_out


def _out_proj_bwd(dx1, ya, ys, z, an, sn, gate1, w_o, tm, xchg):
    S = dx1.shape[0]
    n_steps = S // tm

    def body(dx1_ref, ya_ref, ys_ref, z_ref, an_ref, sn_ref, g_ref, w_ref,
             dya_ref, dys_ref, dz_ref, gw_ref, acc_ref, gw_acc):
        i = pl.program_id(0)

        @pl.when(i == 0)
        def _():
            acc_ref[...] = jnp.zeros_like(acc_ref)
            gw_acc[...] = jnp.zeros_like(gw_acc)

        u, vjp = jax.vjp(_out_stage, *_out_stage_args(ya_ref, ys_ref, z_ref, an_ref, sn_ref))
        dx1 = dx1_ref[...]
        ub = u.astype(MXU_DTYPE)
        mix = _mm(ub, w_ref[...])
        dmix = dx1 * g_ref[...]
        dmixb = dmix.astype(MXU_DTYPE)
        du = _mm_nt(dmixb, w_ref[...])
        gw_acc[...] += _mm_tn(ub, dmixb)
        dya, dys0, dys1, dz0, dz1, dan, dsn0, dsn1 = vjp(du)
        dya_ref[...] = dya
        dys_ref[...] = jnp.concatenate([dys0, dys1], axis=1)
        dz_ref[...] = jnp.concatenate([dz0, dz1], axis=1).astype(dz_ref.dtype)
        acc_ref[0:1, :] += jnp.sum(dx1 * mix, axis=0, keepdims=True)
        acc_ref[1:2, :] += jnp.concatenate([dan, dsn0, dsn1], axis=1)

        @pl.when(i == n_steps - 1)
        def _():
            gw_ref[...] = gw_acc[...].astype(gw_ref.dtype)

    half = pl.BlockSpec((tm, ATTN_W), _row)
    hvec = pl.BlockSpec((1, ATTN_W), _fixed)
    full = pl.BlockSpec((tm, D_MODEL), _row)
    return _hosted_call(
        body, "out_proj_bwd", n_steps,
        in_specs=[full, half, half, half, hvec, hvec,
                  pl.BlockSpec((1, D_MODEL), _fixed), pl.BlockSpec((D_MODEL, D_MODEL), _fixed)],
        out_specs=[half, half, half, pl.BlockSpec((D_MODEL, D_MODEL), _fixed), pl.BlockSpec((8, D_MODEL), _fixed)],
        out_shape=[jax.ShapeDtypeStruct((S, ATTN_W), F32)] * 2 + [jax.ShapeDtypeStruct((S, ATTN_W), MXU_DTYPE),
                   jax.ShapeDtypeStruct((D_MODEL, D_MODEL), WIRE_DTYPE), jax.ShapeDtypeStruct((8, D_MODEL), F32)],
        scratch_shapes=[pltpu.VMEM((D_MODEL, D_MODEL), F32)],
        args=(dx1, ya, ys, z, an, sn, gate1, w_o), xchg=xchg, cparams=_cparams(VMEM_BIG),
    )


def _loss_rows(x2, fn, tgt):
    y = _rms(x2, fn, D_MODEL)
    per_row = jnp.sum(jnp.square(y - tgt), axis=1, keepdims=True)
    return jnp.sum(per_row, axis=0, keepdims=True) * (0.5 / D_MODEL)


def _mlp_loss(x1, tgt, norm2, scale2, shift2, gate2, fnorm, w_gu, w_d, tm):
    S = x1.shape[0]
    n_pieces = len(w_gu) + len(w_d)

    def body(*refs):
        x1_ref, t_ref, n_ref, sc_ref, sh_ref, g_ref, fn_ref = refs[:7]
        piece_refs = refs[7:7 + n_pieces]
        dx1_ref, h_ref, dgu_ref, act_ref, dmlp_ref, acc_ref, wgu, wd, wsem = refs[7 + n_pieces:]

        @pl.when(pl.program_id(0) == 0)
        def _():
            acc_ref[...] = jnp.zeros_like(acc_ref)
            copies = []
            for dst, pieces in ((wgu, piece_refs[:len(w_gu)]), (wd, piece_refs[len(w_gu):])):
                shard = sum(p.shape[1] for p in pieces)
                off = 0
                for p in pieces:
                    for j in range(N_DEV):
                        copies.append(pltpu.make_async_copy(p.at[j], dst.at[pl.ds(j * shard + off, p.shape[1])],
                                                            wsem.at[len(copies)]))
                    off += p.shape[1]
            for cp in copies:
                cp.start()
            for cp in copies:
                cp.wait()

        x1 = x1_ref[...]
        gate2 = g_ref[...]
        h, vjp_h = jax.vjp(_modnorm, x1, n_ref[...], sc_ref[...], sh_ref[...])
        hb = h.astype(MXU_DTYPE)
        gu = _mm_nt(hb, wgu[...])
        g, u = gu[:, :D_FF], gu[:, D_FF:]
        sg = jax.nn.sigmoid(g)
        silu_g = g * sg
        act = (silu_g * u).astype(MXU_DTYPE)
        mlp = _mm(act, wd[...])
        x2 = x1 + gate2 * mlp
        loss, vjp_loss = jax.vjp(_loss_rows, x2, fn_ref[...], t_ref[...])
        dx2, dfn, _ = vjp_loss(jnp.ones((1, 1), F32))
        dmlp = (dx2 * gate2).astype(MXU_DTYPE)
        dact = _mm_nt(dmlp, wd[...])
        dg = dact * u * (sg * (1.0 + g * (1.0 - sg)))
        du = dact * silu_g
        dgu = jnp.concatenate([dg, du], axis=1).astype(MXU_DTYPE)
        dh = _mm(dgu, wgu[...])
        dx, dn, dsc, dsh = vjp_h(dh)
        dx1_ref[...] = dx2 + dx
        h_ref[...] = hb
        dgu_ref[...] = dgu
        act_ref[...] = act
        dmlp_ref[...] = dmlp
        acc_ref[0:1, :] += dn
        acc_ref[1:2, :] += dsc
        acc_ref[2:3, :] += dsh
        acc_ref[3:4, :] += jnp.sum(dx2 * mlp, axis=0, keepdims=True)
        acc_ref[4:5, :] += dfn
        acc_ref[5:6, :] += jnp.broadcast_to(loss, (1, D_MODEL))

    full = pl.BlockSpec((tm, D_MODEL), _row)
    vec = pl.BlockSpec((1, D_MODEL), _fixed)
    anyspec = pl.BlockSpec(memory_space=pl.ANY)
    return pl.pallas_call(
        body, name="mlp_loss", grid=(S // tm,),
        in_specs=[full, full, vec, vec, vec, vec, vec] + [anyspec] * n_pieces,
        out_specs=[full, full, pl.BlockSpec((tm, 2 * D_FF), _row), pl.BlockSpec((tm, D_FF), _row), full,
                   pl.BlockSpec((8, D_MODEL), _fixed)],
        out_shape=[jax.ShapeDtypeStruct((S, D_MODEL), F32), jax.ShapeDtypeStruct((S, D_MODEL), MXU_DTYPE),
                   jax.ShapeDtypeStruct((S, 2 * D_FF), MXU_DTYPE), jax.ShapeDtypeStruct((S, D_FF), MXU_DTYPE),
                   jax.ShapeDtypeStruct((S, D_MODEL), MXU_DTYPE), jax.ShapeDtypeStruct((8, D_MODEL), F32)],
        scratch_shapes=[pltpu.VMEM((2 * D_FF, D_MODEL), MXU_DTYPE), pltpu.VMEM((D_FF, D_MODEL), MXU_DTYPE),
                        pltpu.SemaphoreType.DMA((N_DEV * n_pieces,))],
        compiler_params=_cparams(VMEM_BIG),
    )(x1, tgt, norm2, scale2, shift2, gate2, fnorm, *w_gu, *w_d)


def _wgrad(a, g, tk, ts, name, xchg=None, g_cols=None):
    pieces = list(a) if isinstance(a, (list, tuple)) else [a]
    S = pieces[0].shape[0]
    K = sum(p.shape[1] for p in pieces)
    assert len(pieces) == 1 or tk == K
    N, col = (g.shape[1], 0) if g_cols is None else g_cols
    ns = S // ts
    n_a = len(pieces)

    def body(*refs):
        a_refs, (g_ref, o_ref, acc_ref) = refs[:n_a], refs[n_a:]
        s = pl.program_id(1)

        @pl.when(s == 0)
        def _():
            acc_ref[...] = jnp.zeros_like(acc_ref)

        a_blk = a_refs[0][...] if n_a == 1 else jnp.concatenate([r[...] for r in a_refs], axis=1)
        acc_ref[...] += _mm_tn(a_blk, g_ref[...])

        @pl.when(s == ns - 1)
        def _():
            o_ref[...] = acc_ref[...].astype(o_ref.dtype)

    if n_a == 1:
        in_specs = [pl.BlockSpec((ts, tk), lambda j, s: (s, j))]
    else:
        in_specs = [pl.BlockSpec((ts, p.shape[1]), lambda j, s: (s, 0)) for p in pieces]
    in_specs.append(pl.BlockSpec((ts, N), lambda j, s: (s, col)))
    out_spec = pl.BlockSpec((tk, N), lambda j, s: (j, 0))
    out_shape = jax.ShapeDtypeStruct((K, N), WIRE_DTYPE)
    scratch = [pltpu.VMEM((tk, N), F32)]
    args = (*pieces, g)
    if xchg is None:
        return pl.pallas_call(body, name=name, grid=(K // tk, ns), in_specs=in_specs, out_specs=out_spec,
                              out_shape=out_shape, scratch_shapes=scratch, compiler_params=_cparams(VMEM_BIG))(*args)
    (out,), x_out = _hosted_call(body, name, (K // tk, ns), in_specs, [out_spec], [out_shape], scratch, args, xchg,
                                 _cparams(VMEM_BIG))
    return out, x_out


SSD_CHUNKS_PER_STEP = 4
SSD_BWD_CHUNKS_PER_STEP = 4
ATTN_BLOCKS_PER_STEP = 4
MASKED = -1e30
QK_SCALE = HALF ** -0.5


def _attn_bias(buckets, rel_bias):
    def body(bk_ref, relb_ref, out_ref):
        bk = bk_ref[...]
        i = lax.broadcasted_iota(jnp.int32, (BLK, 2 * BLK), 0)
        j = lax.broadcasted_iota(jnp.int32, (BLK, 2 * BLK), 1)
        window = (j > i) & (j <= i + BLK)
        for h in range(N_HEADS):
            acc = jnp.zeros((BLK, 2 * BLK), F32)
            for b in range(N_BUCKETS):
                acc = jnp.where(bk == b, relb_ref[b, h], acc)
            out_ref[0, h] = jnp.where(window, acc, MASKED)
            out_ref[1, h] = jnp.where(window & (j >= BLK), acc, MASKED)

    return pl.pallas_call(
        body, name="attn_bias", out_shape=jax.ShapeDtypeStruct((2, N_HEADS, BLK, 2 * BLK), F32),
        in_specs=[pl.BlockSpec(memory_space=pltpu.VMEM), pl.BlockSpec(memory_space=pltpu.SMEM)],
    )(buckets, rel_bias)


def _attn_fwd(qkv, bias, sinks, xchg):
    S = qkv.shape[0]
    nb = S // BLK

    nq = ATTN_BLOCKS_PER_STEP if nb % ATTN_BLOCKS_PER_STEP == 0 else 1
    rows = nq * BLK

    def body(q_ref, kvp_ref, kvc_ref, bias_ref, sinks_ref, y_ref):
        i = pl.program_id(0)
        q = q_ref[...].astype(F32) * QK_SCALE
        kv = jnp.concatenate([kvp_ref[...], kvc_ref[...]], axis=0).astype(F32)
        k_lo, k_hi = _split_pair(kv[:, :LANE])
        v_lo, v_hi = _split_pair(kv[:, LANE:])
        bands = [[t[b * BLK:(b + 2) * BLK].astype(MXU_DTYPE) for t in (k_lo, k_hi, v_lo, v_hi)] for b in range(nq)]
        q_heads = [_split_heads(q[b * BLK:(b + 1) * BLK], 4) for b in range(nq)]
        first = [jnp.where(i == 0, 1, 0) if b == 0 else 0 for b in range(nq)]
        items = [(b, h) for b in range(nq) for h in range(N_HEADS)]
        s = [_mm_nt(q_heads[b][h].astype(MXU_DTYPE), bands[b][h // 4]) + bias_ref[first[b], h] for b, h in items]
        m = [jnp.maximum(jnp.max(s[n], axis=-1, keepdims=True), sinks_ref[h]) for n, (b, h) in enumerate(items)]
        p = [jnp.exp(s[n] - m[n]) for n in range(len(items))]
        rinv = [1.0 / (jnp.sum(p[n], axis=-1, keepdims=True) + jnp.exp(sinks_ref[h] - m[n]))
                for n, (b, h) in enumerate(items)]
        out = [_mm(p[n], bands[b][2 + h // 4]) * rinv[n] for n, (b, h) in enumerate(items)]
        y_ref[...] = jnp.concatenate([_join_heads(out[b * N_HEADS:(b + 1) * N_HEADS]) for b in range(nq)], axis=0)

    smem = pl.BlockSpec(memory_space=pltpu.SMEM)
    return _hosted_call(
        body, "attn_fwd", nb // nq,
        in_specs=[pl.BlockSpec((rows, ATTN_W), _row),
                  pl.BlockSpec((BLK, 2 * KV_W), lambda i: (jnp.maximum(i * nq - 1, 0), 2)),
                  pl.BlockSpec((rows, 2 * KV_W), lambda i: (i, 2)),
                  pl.BlockSpec((2, N_HEADS, BLK, 2 * BLK), lambda i: (0, 0, 0, 0)), smem],
        out_specs=[pl.BlockSpec((rows, ATTN_W), _row)],
        out_shape=[jax.ShapeDtypeStruct((S, ATTN_W), F32)],
        scratch_shapes=[],
        args=(qkv, qkv, qkv, bias, sinks), xchg=xchg, cparams=_cparams(),
    )


def _attn_bwd(qkv, y, dy, bias, sinks, xchg):
    S = qkv.shape[0]
    nb = S // BLK
    nq = ATTN_BLOCKS_PER_STEP if nb % ATTN_BLOCKS_PER_STEP == 0 else 1
    rows, n_steps = nq * BLK, nb // nq

    def body(q_ref, kvp_ref, kvc_ref, y_ref, dy_ref, bias_ref, sinks_ref, dq_ref, dkv_ref, dbias_ref, dsk_ref, carry_ref):
        i = pl.program_id(0)

        @pl.when(i == 0)
        def _():
            dbias_ref[...] = jnp.zeros_like(dbias_ref)
            dsk_ref[...] = jnp.zeros_like(dsk_ref)
            carry_ref[...] = jnp.zeros_like(carry_ref)

        q = q_ref[...].astype(F32) * QK_SCALE
        kv = jnp.concatenate([kvp_ref[...], kvc_ref[...]], axis=0).astype(F32)
        k_lo, k_hi = _split_pair(kv[:, :LANE])
        v_lo, v_hi = _split_pair(kv[:, LANE:])
        bands = [[t[b * BLK:(b + 2) * BLK].astype(MXU_DTYPE) for t in (k_lo, k_hi, v_lo, v_hi)] for b in range(nq)]
        rows_of = lambda ref, b: ref[b * BLK:(b + 1) * BLK, :]
        first = [jnp.where(i == n_steps - 1, 1, 0) if b == 0 else 0 for b in range(nq)]
        items = [(b, h) for b in range(nq) for h in range(N_HEADS)]
        at = lambda b, h: b * N_HEADS + h
        q_heads = [hd for b in range(nq) for hd in _split_heads(q[b * BLK:(b + 1) * BLK], 4)]
        y_heads = [hd for b in range(nq) for hd in _split_heads(rows_of(y_ref, b), 4)]
        dy_heads = [hd for b in range(nq) for hd in _split_heads(rows_of(dy_ref, b), 4)]
        qs = [q_heads[n].astype(MXU_DTYPE) for n in range(len(items))]
        s = [_mm_nt(qs[at(b, h)], bands[b][h // 4]) + bias_ref[first[b], h] for b, h in items]
        m = [jnp.maximum(jnp.max(s[at(b, h)], axis=-1, keepdims=True), sinks_ref[h]) for b, h in items]
        p = [jnp.exp(s[n] - m[n]) for n in range(len(items))]
        esink = [jnp.exp(sinks_ref[h] - m[at(b, h)]) for b, h in items]
        rinv = [1.0 / (jnp.sum(p[n], axis=-1, keepdims=True) + esink[n]) for n in range(len(items))]
        t = [dy_heads[n] * rinv[n] for n in range(len(items))]
        delta = [jnp.sum(t[n] * y_heads[n], axis=-1, keepdims=True) for n in range(len(items))]
        tb = [t[n].astype(MXU_DTYPE) for n in range(len(items))]
        dp = [_mm_nt(tb[at(b, h)], bands[b][2 + h // 4]) for b, h in items]
        ds = [p[n] * (dp[n] - delta[n]) for n in range(len(items))]
        for h in range(N_HEADS):
            ds_h, dsk_h = ds[at(0, h)], esink[at(0, h)] * delta[at(0, h)]
            for b in range(1, nq):
                ds_h = ds_h + ds[at(b, h)]
                dsk_h = dsk_h + esink[at(b, h)] * delta[at(b, h)]
            dbias_ref[h] += ds_h
            dsk_ref[h] -= dsk_h
        dsb = [ds[n].astype(MXU_DTYPE) for n in range(len(items))]
        pb = [p[n].astype(MXU_DTYPE) for n in range(len(items))]
        dq_heads = [_mm(dsb[at(b, h)], bands[b][h // 4]) * QK_SCALE for b, h in items]
        grp = lambda lst, b, g: jnp.concatenate(lst[at(b, 4 * g):at(b, 4 * g) + 4], axis=0)
        dk_pads = [[_mm_tn(grp(dsb, b, g), grp(qs, b, g)) for g in range(2)] for b in range(nq)]
        dv_pads = [[_mm_tn(grp(pb, b, g), grp(tb, b, g)) for g in range(2)] for b in range(nq)]
        dq_ref[...] = jnp.concatenate([_join_heads(dq_heads[b * N_HEADS:(b + 1) * N_HEADS]) for b in range(nq)],
                                      axis=0).astype(dq_ref.dtype)
        part = lambda b, lo: jnp.concatenate(
            [_join_pair(d[b][0][lo:lo + BLK], d[b][1][lo:lo + BLK]) for d in (dk_pads, dv_pads)], axis=1)
        dkv = [part(b, BLK) + (part(b + 1, 0) if b + 1 < nq else carry_ref[...]) for b in range(nq)]
        dkv_ref[...] = jnp.concatenate(dkv, axis=0).astype(dkv_ref.dtype)
        carry_ref[...] = part(0, 0)

    smem = pl.BlockSpec(memory_space=pltpu.SMEM)
    rev = lambda i: (n_steps - 1 - i, 0)
    return _hosted_call(
        body, "attn_bwd", n_steps,
        in_specs=[pl.BlockSpec((rows, ATTN_W), rev),
                  pl.BlockSpec((BLK, 2 * KV_W), lambda i: (jnp.maximum((n_steps - 1 - i) * nq - 1, 0), 2)),
                  pl.BlockSpec((rows, 2 * KV_W), lambda i: (n_steps - 1 - i, 2)),
                  pl.BlockSpec((rows, ATTN_W), rev), pl.BlockSpec((rows, ATTN_W), rev),
                  pl.BlockSpec((2, N_HEADS, BLK, 2 * BLK), lambda i: (0, 0, 0, 0)), smem],
        out_specs=[pl.BlockSpec((rows, ATTN_W), rev), pl.BlockSpec((rows, 2 * KV_W), rev),
                   pl.BlockSpec((N_HEADS, BLK, 2 * BLK), lambda i: (0, 0, 0)),
                   pl.BlockSpec((N_HEADS, BLK, 1), lambda i: (0, 0, 0))],
        out_shape=[jax.ShapeDtypeStruct((S, ATTN_W), MXU_DTYPE), jax.ShapeDtypeStruct((S, 2 * KV_W), MXU_DTYPE),
                   jax.ShapeDtypeStruct((N_HEADS, BLK, 2 * BLK), F32), jax.ShapeDtypeStruct((N_HEADS, BLK, 1), F32)],
        scratch_shapes=[pltpu.VMEM((BLK, 2 * KV_W), F32)],
        args=(qkv, qkv, qkv, y, dy, bias, sinks), xchg=xchg, cparams=_cparams(),
    )


def _attn_finish(dbias, dsk, buckets):
    def body(db_ref, dsk_ref, bk_ref, drel_ref, dsink_ref):
        bk = bk_ref[...]
        r = lax.broadcasted_iota(jnp.int32, (N_BUCKETS, LANE), 0)
        l = lax.broadcasted_iota(jnp.int32, (N_BUCKETS, LANE), 1)
        row = lax.broadcasted_iota(jnp.int32, (N_HEADS, LANE), 0)
        res = jnp.zeros((N_BUCKETS, LANE), F32)
        dsink = jnp.zeros((N_HEADS, LANE), F32)
        for h in range(N_HEADS):
            db = db_ref[h]
            for b in range(N_BUCKETS):
                v = jnp.sum(jnp.sum(jnp.where(bk == b, db, 0.0), axis=1, keepdims=True), axis=0, keepdims=True)
                res = res + jnp.where((r == b) & (l == h), v, 0.0)
            dsink = dsink + jnp.where(row == h, jnp.sum(dsk_ref[h], axis=0, keepdims=True), 0.0)
        drel_ref[...] = res
        dsink_ref[...] = dsink

    return pl.pallas_call(body, name="attn_finish",
                          out_shape=[jax.ShapeDtypeStruct((N_BUCKETS, LANE), F32),
                                     jax.ShapeDtypeStruct((N_HEADS, LANE), F32)])(dbias, dsk, buckets)


def _ssd_consts():
    r = lax.broadcasted_iota(jnp.int32, (BLK, BLK), 0)
    c = lax.broadcasted_iota(jnp.int32, (BLK, BLK), 1)
    causal = c <= r
    upper = (r <= c).astype(F32)
    last = r == BLK - 1
    head = lax.broadcasted_iota(jnp.int32, (N_HEADS, BLK), 0)
    return causal, upper, last, head


def _ssd_chunks(xs, bg, cg, dt_raw_t, prev0, dtb, alog, d_rows, consts):
    causal, upper, last, head = consts
    nq = len(xs)
    items = [(c, h) for c in range(nq) for h in range(N_HEADS)]
    at = lambda c, h: c * N_HEADS + h
    a_neg = -jnp.exp(alog)
    dt_t = [_softplus(dt_raw_t[c] + dtb) for c in range(nq)]
    acs_t = [_mm_hi(dt_t[c] * a_neg, upper) for c in range(nq)]
    cb = [[_mm_nt(cg[c][g], bg[c][g]) for g in range(2)] for c in range(nq)]
    pick = lambda t, h: jnp.sum(jnp.where(head == h, t, 0.0), axis=0, keepdims=True)
    dt_row = [pick(dt_t[c], h) for c, h in items]
    a_row = [pick(acs_t[c], h) for c, h in items]
    a_rb = [jnp.broadcast_to(a_row[n], (BLK, BLK)) for n in range(len(items))]
    a_b = [a_rb[n].T for n in range(len(items))]
    a_last = [jnp.sum(jnp.where(last, a_b[n], 0.0), axis=0, keepdims=True) for n in range(len(items))]
    w = [cb[c][h // 4] * jnp.exp(jnp.where(causal, a_b[at(c, h)] - a_rb[at(c, h)], -1e30)) * dt_row[at(c, h)]
         for c, h in items]
    f_b = [jnp.broadcast_to(dt_row[n] * jnp.exp(a_last[n] - a_row[n]), (BLK, BLK)).T for n in range(len(items))]
    y_in = [_mm(w[at(c, h)], xs[c][h]) for c, h in items]
    st = [_mm_tn(bg[c][h // 4], xs[c][h] * f_b[at(c, h)]) for c, h in items]
    e_b = [jnp.exp(a_b[n]) for n in range(len(items))]
    states = [list(prev0)]
    for c in range(nq):
        states.append([states[c][h] * jnp.exp(a_last[at(c, h)]) + st[at(c, h)] for h in range(N_HEADS)])
    y_off = [_mm(cg[c][h // 4], states[c][h]) * e_b[at(c, h)] for c, h in items]
    ys = [[y_in[at(c, h)] + y_off[at(c, h)] + d_rows[h] * xs[c][h] for h in range(N_HEADS)] for c in range(nq)]
    return ys, states


def _ssd_chunks_bwd(xs, bg, cg, dt_raw_t, prev, dtb, alog, d_rows, dys, dh_last, consts):
    causal, upper, last, head = consts
    nq = len(xs)
    items = [(c, h) for c in range(nq) for h in range(N_HEADS)]
    ni = len(items)
    at = lambda c, h: c * N_HEADS + h
    groups = [(c, g) for c in range(nq) for g in range(2)]
    lane = _lane_iota((BLK, BLK))
    lane_row = _lane_iota((1, BLK))
    a_neg = -jnp.exp(alog)
    pre_dt = [dt_raw_t[c] + dtb for c in range(nq)]
    dt_t = [_softplus(pre_dt[c]) for c in range(nq)]
    acs_t = [_mm_hi(dt_t[c] * a_neg, upper) for c in range(nq)]
    pick = lambda t, h: jnp.sum(jnp.where(head == h, t, 0.0), axis=0, keepdims=True)
    full_sum = lambda t: jnp.sum(jnp.sum(t, axis=1, keepdims=True), axis=0, keepdims=True)
    dt_row = [pick(dt_t[c], h) for c, h in items]
    a_row = [pick(acs_t[c], h) for c, h in items]
    a_rb = [jnp.broadcast_to(a_row[n], (BLK, BLK)) for n in range(ni)]
    a_b = [a_rb[n].T for n in range(ni)]
    a_last = [jnp.sum(jnp.where(last, a_b[n], 0.0), axis=0, keepdims=True) for n in range(ni)]
    lm = [jnp.exp(jnp.where(causal, a_b[n] - a_rb[n], -1e30)) for n in range(ni)]
    cgb = [[cg[c][g].astype(MXU_DTYPE) for g in range(2)] for c in range(nq)]
    bgb = [[bg[c][g].astype(MXU_DTYPE) for g in range(2)] for c in range(nq)]
    cb = [[_mm_nt(cgb[c][g], bgb[c][g]) for g in range(2)] for c in range(nq)]
    u = [cb[c][h // 4] * lm[at(c, h)] for c, h in items]
    w = [(u[n] * dt_row[n]).astype(MXU_DTYPE) for n in range(ni)]
    e_row = [jnp.exp(a_last[n] - a_row[n]) for n in range(ni)]
    f_row = [dt_row[n] * e_row[n] for n in range(ni)]
    f_b = [jnp.broadcast_to(f_row[n], (BLK, BLK)).T for n in range(ni)]
    e_b = [jnp.exp(a_b[n]) for n in range(ni)]
    el = [jnp.exp(a_last[n]) for n in range(ni)]
    xb = [xs[c][h].astype(MXU_DTYPE) for c, h in items]
    dyb = [dys[c][h].astype(MXU_DTYPE) for c, h in items]
    prevb = [prev[c][h].astype(MXU_DTYPE) for c, h in items]
    gmat = [_mm(cgb[c][h // 4], prevb[at(c, h)]) for c, h in items]
    dw = [_mm_nt(dyb[n], xb[n]) for n in range(ni)]
    dg = [dys[c][h] * e_b[at(c, h)] for c, h in items]
    dgb = [dg[n].astype(MXU_DTYPE) for n in range(ni)]
    from_y = [_mm_tn(cgb[c][h // 4], dgb[at(c, h)]) for c, h in items]
    dhs = [None] * ni
    dprev = [None] * ni
    for c in reversed(range(nq)):
        for h in range(N_HEADS):
            dhs[at(c, h)] = dh_last[h] if c == nq - 1 else dprev[at(c + 1, h)]
            dprev[at(c, h)] = from_y[at(c, h)] + dhs[at(c, h)] * el[at(c, h)]
    dstb = [dhs[n].astype(MXU_DTYPE) for n in range(ni)]
    dxf = [_mm(bgb[c][h // 4], dstb[at(c, h)]) for c, h in items]
    xfb = [(xs[c][h] * f_b[at(c, h)]).astype(MXU_DTYPE) for c, h in items]
    dxs = [_mm_tn(w[at(c, h)], dyb[at(c, h)]) + d_rows[h] * dys[c][h] + f_b[at(c, h)] * dxf[at(c, h)]
           for c, h in items]
    dd_item = [jnp.sum(dys[c][h] * xs[c][h], axis=0, keepdims=True) for c, h in items]
    dcg_h = [_mm_nt(dgb[n], prevb[n]) for n in range(ni)]
    dbg_h = [_mm_nt(xfb[n], dstb[n]) for n in range(ni)]
    zt = [dw[n] * u[n] for n in range(ni)]
    dseg = [zt[n] * dt_row[n] for n in range(ni)]
    dcb_h = [dw[n] * lm[n] * dt_row[n] for n in range(ni)]
    four = lambda lst, c, g: lst[at(c, 4 * g)] + lst[at(c, 4 * g + 1)] + lst[at(c, 4 * g + 2)] + lst[at(c, 4 * g + 3)]
    dcb = {(c, g): four(dcb_h, c, g).astype(MXU_DTYPE) for c, g in groups}
    dcg = [[four(dcg_h, c, g) + _mm(dcb[c, g], bgb[c][g]) for g in range(2)] for c in range(nq)]
    dbg = [[four(dbg_h, c, g) + _mm_tn(dcb[c, g], cgb[c][g]) for g in range(2)] for c in range(nq)]
    r1 = [jnp.sum(dg[n] * gmat[n] + dseg[n], axis=1, keepdims=True) for n in range(ni)]
    r2 = [jnp.sum(dxf[at(c, h)] * xs[c][h], axis=1, keepdims=True) for c, h in items]
    tt = [jnp.where(lane < HALF, jnp.broadcast_to(r1[n], (BLK, BLK)), jnp.broadcast_to(r2[n], (BLK, BLK))).T
          for n in range(ni)]
    r1_row = [tt[n][0:1, :] for n in range(ni)]
    r2_row = [tt[n][HALF:HALF + 1, :] for n in range(ni)]
    d_el = [full_sum(dhs[at(c, h)] * prev[c][h]) for c, h in items]
    da_last = [jnp.sum(r2_row[n] * f_row[n], axis=1, keepdims=True) + el[n] * d_el[n] for n in range(ni)]
    da_row = [r1_row[n] - jnp.sum(dseg[n], axis=0, keepdims=True) - r2_row[n] * f_row[n]
              + jnp.where(lane_row == BLK - 1, da_last[n], 0.0) for n in range(ni)]
    ddt_row = [jnp.sum(zt[n], axis=0, keepdims=True) + r2_row[n] * e_row[n] for n in range(ni)]
    draw, dalog = [], jnp.zeros((N_HEADS, BLK), F32)
    for c in range(nq):
        da_t = jnp.zeros((N_HEADS, BLK), F32)
        ddt_t = jnp.zeros((N_HEADS, BLK), F32)
        for h in range(N_HEADS):
            da_t = jnp.where(head == h, da_row[at(c, h)], da_t)
            ddt_t = jnp.where(head == h, ddt_row[at(c, h)], ddt_t)
        d_dta = _mm_hi(da_t, causal.astype(F32))
        dalog = dalog + d_dta * dt_t[c] * a_neg
        draw.append((ddt_t + d_dta * a_neg) * jax.nn.sigmoid(pre_dt[c]))
    ddtb = draw[0]
    for c in range(1, nq):
        ddtb = ddtb + draw[c]
    dd_rows = []
    for h in range(N_HEADS):
        t = dd_item[at(0, h)]
        for c in range(1, nq):
            t = t + dd_item[at(c, h)]
        dd_rows.append(t)
    return ([dxs[c * N_HEADS:(c + 1) * N_HEADS] for c in range(nq)], dbg, dcg, draw,
            [dprev[at(0, h)] for h in range(N_HEADS)], ddtb, dalog, dd_rows)


def _dt_rows(dt_blk):
    return dt_blk.T[:N_HEADS]


def _conv_pre(halo, blk, cw_ref, cb_ref):
    ext = jnp.concatenate([halo, blk], axis=0)
    taps = [pltpu.roll(ext, 3 - k, 0)[8:] for k in range(3)] + [blk]
    pre = cb_ref[...] + cw_ref[0:1, :] * taps[0]
    for k in range(1, 4):
        pre = pre + cw_ref[k:k + 1, :] * taps[k]
    return pre


def _ssd_split(pre):
    heads = _split_heads(pre[:, :SSM_W], 4)
    pb = [pre[:, SSM_W + g * D_STATE:SSM_W + (g + 1) * D_STATE] for g in range(2)]
    pc = [pre[:, SSM_W + 2 * D_STATE + g * D_STATE:SSM_W + 2 * D_STATE + (g + 1) * D_STATE] for g in range(2)]
    return heads, pb, pc


def _ssd_fwd(xbc, dt_raw, conv_w, conv_b, dtb_row, alog_row, d_exp, xchg):
    S = xbc.shape[0]
    nc = S // BLK
    nq = SSD_CHUNKS_PER_STEP if nc % SSD_CHUNKS_PER_STEP == 0 else 1
    rows = nq * BLK

    def body(xbc_ref, halo_ref, dt_ref, cw_ref, cb_ref, dtb_ref, alog_ref, d_ref, y_ref, prev_ref, pre_ref, state_ref):
        i = pl.program_id(0)

        @pl.when(i == 0)
        def _():
            state_ref[...] = jnp.zeros_like(state_ref)

        halo = halo_ref[...] * jnp.where(i > 0, 1.0, 0.0)
        pre = _conv_pre(halo, xbc_ref[...], cw_ref, cb_ref)
        pre_ref[...] = pre
        xc = _silu(pre)
        split = [_ssd_split(xc[c * BLK:(c + 1) * BLK]) for c in range(nq)]
        dt_t = [_dt_rows(dt_ref[c * BLK:(c + 1) * BLK, :]) for c in range(nq)]
        prev0 = [state_ref[h] for h in range(N_HEADS)]
        d_rows = [d_ref[h:h + 1, :] for h in range(N_HEADS)]
        ys, states = _ssd_chunks([s[0] for s in split], [s[1] for s in split], [s[2] for s in split], dt_t, prev0,
                                 dtb_ref[...], alog_ref[...], d_rows, _ssd_consts())
        for h in range(N_HEADS):
            for c in range(nq):
                prev_ref[c, h] = states[c][h]
            state_ref[h] = states[nq][h]
        y_ref[...] = jnp.concatenate([_join_heads(ys[c]) for c in range(nq)], axis=0)

    vec = pl.BlockSpec((N_HEADS, LANE), _fixed)
    return _hosted_call(
        body, "ssd_fwd", nc // nq,
        in_specs=[pl.BlockSpec((rows, XBC_W), _row),
                  pl.BlockSpec((8, XBC_W), lambda i: (jnp.maximum(i * (rows // 8) - 1, 0), 0)),
                  pl.BlockSpec((rows, LANE), _row),
                  pl.BlockSpec((4, XBC_W), _fixed), pl.BlockSpec((1, XBC_W), _fixed), vec, vec,
                  pl.BlockSpec((N_HEADS, LANE), _fixed)],
        out_specs=[pl.BlockSpec((rows, SSM_W), _row),
                   pl.BlockSpec((nq, N_HEADS, D_STATE, LANE), lambda i: (i, 0, 0, 0)),
                   pl.BlockSpec((rows, XBC_W), _row)],
        out_shape=[jax.ShapeDtypeStruct((S, SSM_W), F32), jax.ShapeDtypeStruct((nc, N_HEADS, D_STATE, LANE), F32),
                   jax.ShapeDtypeStruct((S, XBC_W), F32)],
        scratch_shapes=[pltpu.VMEM((N_HEADS, D_STATE, LANE), F32)],
        args=(xbc, xbc, dt_raw, conv_w, conv_b, dtb_row, alog_row, d_exp), xchg=xchg, cparams=_cparams(),
    )


def _ssd_bwd(xbc, pre_act, dt_raw, prev_states, dy, conv_w, dtb_row, alog_row, d_exp, xchg):
    S = xbc.shape[0]
    nc = S // BLK
    nq = SSD_BWD_CHUNKS_PER_STEP if nc % SSD_BWD_CHUNKS_PER_STEP == 0 else 1
    rows, n_steps = nq * BLK, nc // nq

    def body(xbc_ref, halo_ref, pre_ref, dt_ref, prev_ref, dy_ref, cw_ref, dtb_ref, alog_ref, d_ref,
             dxbc_ref, ddt_ref, dcw_ref, dvec_ref, dd_ref, gstate_ref, ghalo_ref):
        i = pl.program_id(0)

        @pl.when(i == 0)
        def _():
            gstate_ref[...] = jnp.zeros_like(gstate_ref)
            ghalo_ref[...] = jnp.zeros_like(ghalo_ref)
            dcw_ref[...] = jnp.zeros_like(dcw_ref)
            dvec_ref[...] = jnp.zeros_like(dvec_ref)
            dd_ref[...] = jnp.zeros_like(dd_ref)

        halo = halo_ref[...] * jnp.where(i < n_steps - 1, 1.0, 0.0)
        ext = jnp.concatenate([halo, xbc_ref[...]], axis=0)
        pre = pre_ref[...]
        sig = jax.nn.sigmoid(pre)
        xc = pre * sig
        split = [_ssd_split(xc[c * BLK:(c + 1) * BLK]) for c in range(nq)]
        dt_t = [_dt_rows(dt_ref[c * BLK:(c + 1) * BLK, :]) for c in range(nq)]
        prev = [[prev_ref[c, h] for h in range(N_HEADS)] for c in range(nq)]
        d_rows = [d_ref[h:h + 1, :] for h in range(N_HEADS)]
        dys = [_split_heads(dy_ref[c * BLK:(c + 1) * BLK, :], 4) for c in range(nq)]
        dh_last = [gstate_ref[h] for h in range(N_HEADS)]
        dheads, dpb, dpc, ddt_t, dprev0, ddtb, dalog, dd_rows = _ssd_chunks_bwd(
            [s[0] for s in split], [s[1] for s in split], [s[2] for s in split], dt_t, prev, dtb_ref[...],
            alog_ref[...], d_rows, dys, dh_last, _ssd_consts())
        for h in range(N_HEADS):
            gstate_ref[h] = dprev0[h]
            dd_ref[h:h + 1, :] += dd_rows[h]
        pad = jnp.zeros((BLK - N_HEADS, BLK), F32)
        ddt_ref[...] = jnp.concatenate([jnp.concatenate([ddt_t[c], pad], axis=0).T for c in range(nq)],
                                       axis=0).astype(ddt_ref.dtype)
        dvec_ref[0:N_HEADS, :] += ddtb
        dvec_ref[N_HEADS:, :] += dalog
        dxc = jnp.concatenate([jnp.concatenate([_join_heads(dheads[c])] + list(dpb[c]) + list(dpc[c]), axis=1)
                               for c in range(nq)], axis=0)
        dpre = dxc * (sig * (1.0 + pre * (1.0 - sig)))
        zeros8 = jnp.zeros((8, XBC_W), F32)
        dpe = jnp.concatenate([zeros8, dpre, zeros8], axis=0)
        n_ext = 16 + rows
        shifted = [pltpu.roll(dpe, n_ext - (3 - k), 0)[:8 + rows] for k in range(3)] + [dpe[:8 + rows]]
        dext = cw_ref[0:1, :] * shifted[0]
        for k in range(1, 4):
            dext = dext + cw_ref[k:k + 1, :] * shifted[k]
        for k in range(4):
            dcw_ref[k:k + 1, :] += jnp.sum(shifted[k] * ext, axis=0, keepdims=True)
        dcw_ref[4:5, :] += jnp.sum(dpre, axis=0, keepdims=True)
        dxbc_ref[...] = jnp.concatenate([dext[8:rows], dext[rows:] + ghalo_ref[...]], axis=0).astype(dxbc_ref.dtype)
        ghalo_ref[...] = dext[:8, :]

    vec = pl.BlockSpec((N_HEADS, LANE), _fixed)
    rev = lambda i: (n_steps - 1 - i, 0)
    return _hosted_call(
        body, "ssd_bwd", n_steps,
        in_specs=[pl.BlockSpec((rows, XBC_W), rev),
                  pl.BlockSpec((8, XBC_W), lambda i: (jnp.maximum((n_steps - 1 - i) * (rows // 8) - 1, 0), 0)),
                  pl.BlockSpec((rows, XBC_W), rev),
                  pl.BlockSpec((rows, LANE), rev),
                  pl.BlockSpec((nq, N_HEADS, D_STATE, LANE), lambda i: (n_steps - 1 - i, 0, 0, 0)),
                  pl.BlockSpec((rows, SSM_W), rev),
                  pl.BlockSpec((4, XBC_W), _fixed), vec, vec,
                  pl.BlockSpec((N_HEADS, LANE), _fixed)],
        out_specs=[pl.BlockSpec((rows, XBC_W), rev), pl.BlockSpec((rows, LANE), rev),
                   pl.BlockSpec((8, XBC_W), _fixed), pl.BlockSpec((2 * N_HEADS, LANE), _fixed),
                   pl.BlockSpec((N_HEADS, LANE), _fixed)],
        out_shape=[jax.ShapeDtypeStruct((S, XBC_W), MXU_DTYPE), jax.ShapeDtypeStruct((S, LANE), MXU_DTYPE),
                   jax.ShapeDtypeStruct((8, XBC_W), F32), jax.ShapeDtypeStruct((2 * N_HEADS, LANE), F32),
                   jax.ShapeDtypeStruct((N_HEADS, LANE), F32)],
        scratch_shapes=[pltpu.VMEM((N_HEADS, D_STATE, LANE), F32), pltpu.VMEM((8, XBC_W), F32)],
        args=(xbc, xbc, pre_act, dt_raw, prev_states, dy, conv_w, dtb_row, alog_row, d_exp), xchg=xchg,
        cparams=_cparams(VMEM_BIG),
    )


def _adamw_math(w, g, m, v):
    m = ADAM_B1 * m + (1.0 - ADAM_B1) * g
    v = ADAM_B2 * v + (1.0 - ADAM_B2) * jnp.square(g)
    m_hat = m / (1.0 - ADAM_B1 ** ADAM_STEP)
    v_hat = v / (1.0 - ADAM_B2 ** ADAM_STEP)
    delta = -ADAM_LR * (m_hat / (jnp.sqrt(v_hat) + ADAM_EPS) + ADAM_WD * w)
    return delta, m, v


def _reduce_adamw_halves(part_a, part_b, w, m, v, name):
    R, C = w.shape
    P = part_a.shape[0]
    tl = 256
    n = C // tl

    def body(a_ref, b_ref, w_ref, m_ref, v_ref, g_ref, d_ref, nm_ref, nv_ref):
        ga, gb = a_ref[0].astype(F32), b_ref[0].astype(F32)
        for i in range(1, P):
            ga, gb = ga + a_ref[i].astype(F32), gb + b_ref[i].astype(F32)
        first = jnp.where(pl.program_id(0) < n // 2, 1.0, 0.0)
        g = ga * first + gb * (1.0 - first)
        d, nm, nv = _adamw_math(w_ref[...], g, m_ref[...], v_ref[...])
        g_ref[...] = g
        d_ref[...] = d
        nm_ref[...] = nm
        nv_ref[...] = nv

    blk = pl.BlockSpec((R, tl), lambda i: (0, i))
    return pl.pallas_call(
        body, name=name, grid=(n,),
        in_specs=[pl.BlockSpec((P, R, tl), lambda i: (0, 0, jnp.minimum(i, n // 2 - 1))),
                  pl.BlockSpec((P, R, tl), lambda i: (0, 0, jnp.maximum(i - n // 2, 0))), blk, blk, blk],
        out_specs=[blk] * 4, out_shape=[jax.ShapeDtypeStruct((R, C), F32)] * 4,
    )(part_a, part_b, w, m, v)


def _reduce_adamw_hosting(parts_list, wmv_list, name, xchg):
    n_arr = len(parts_list)
    pieces = [list(p) if isinstance(p, (tuple, list)) else [p] for p in parts_list]
    n_pieces = sum(len(p) for p in pieces)
    C = wmv_list[0][0].shape[1]
    tl = 256

    def total(ref):
        g = ref[0].astype(F32)
        for i in range(1, N_DEV):
            g = g + ref[i].astype(F32)
        return g

    def body(*refs):
        p_refs, wmv_refs, o_refs = refs[:n_pieces], refs[n_pieces:n_pieces + 3 * n_arr], refs[n_pieces + 3 * n_arr:]
        at = 0
        for k in range(n_arr):
            sums = [total(r) for r in p_refs[at:at + len(pieces[k])]]
            at += len(pieces[k])
            g = sums[0] if len(sums) == 1 else jnp.concatenate(sums, axis=0)
            w_ref, m_ref, v_ref = wmv_refs[3 * k:3 * k + 3]
            d, nm, nv = _adamw_math(w_ref[...], g, m_ref[...], v_ref[...])
            for o, val in zip(o_refs[4 * k:4 * k + 4], (g, d, nm, nv)):
                o[...] = val

    in_specs = [pl.BlockSpec((N_DEV, p.shape[1], tl), lambda i: (0, 0, i)) for group in pieces for p in group]
    in_specs += [pl.BlockSpec((w.shape[0], tl), lambda i: (0, i)) for w, _, _ in wmv_list for _ in range(3)]
    out_specs = [pl.BlockSpec((w.shape[0], tl), lambda i: (0, i)) for w, _, _ in wmv_list for _ in range(4)]
    out_shape = [jax.ShapeDtypeStruct(w.shape, F32) for w, _, _ in wmv_list for _ in range(4)]
    args = [p for group in pieces for p in group] + [a for wmv in wmv_list for a in wmv]
    outs, x_out = _hosted_call(body, name, C // tl, in_specs, out_specs, out_shape, [], args, xchg,
                               _cparams(VMEM_BIG))
    return [outs[4 * k:4 * k + 4] for k in range(n_arr)], x_out


_SMALL_NAMES = ("ada_b", "norm1", "conv_w", "conv_b", "dt_bias", "A_log", "D_skip", "sinks", "attn_out_norm",
                "ssm_out_norm", "norm2", "rel_bias", "final_norm")
N_MOD = 6 * D_MODEL


def _mod_row(a0, a1, a2):
    return jnp.concatenate([a0[2:3], a0[1:2], a1[0:1], a2[2:3], a2[1:2], a2[3:4]], axis=1)


def _small_update(gathered, params):
    n_g = len(gathered)
    flat = [a for name in _SMALL_NAMES for a in params[name]]

    def body(*refs):
        a0_ref, a1_ref, a2_ref, cw_ref, dv_ref, dd_ref, ds_ref, dr_ref, c_ref = refs[:n_g]
        wmv = refs[n_g:n_g + len(flat)]
        outs = refs[n_g + len(flat):]

        def total(ref):
            t = ref[0]
            for i in range(1, N_DEV):
                t = t + ref[i]
            return t

        t0, t1, t2, tcw, tdv, tdd, tds, tdr = [total(r) for r in (a0_ref, a1_ref, a2_ref, cw_ref, dv_ref, dd_ref,
                                                                   ds_ref, dr_ref)]
        r8 = lax.broadcasted_iota(jnp.int32, (N_HEADS, LANE), 0)
        l8 = lax.broadcasted_iota(jnp.int32, (N_HEADS, LANE), 1)

        def diag_row(t):
            return jnp.sum(jnp.where(r8 == l8, t, 0.0), axis=0, keepdims=True)[:, :N_HEADS]

        def lane_sums(t):
            return diag_row(jnp.broadcast_to(jnp.sum(t, axis=1, keepdims=True), (N_HEADS, LANE)))

        me = _lin(_my_pos())
        n_cw = XBC_W // N_DEV
        cw_mine = jnp.zeros((4, n_cw), F32)
        for j in range(N_DEV):
            cw_mine = cw_mine + tcw[0:4, j * n_cw:(j + 1) * n_cw] * jnp.where(me == j, 1.0, 0.0)
        grads = {
            "ada_b": _mod_row(t0, t1, t2), "norm1": t0[0:1], "conv_w": cw_mine, "conv_b": tcw[4:5],
            "dt_bias": lane_sums(tdv[:N_HEADS]), "A_log": lane_sums(tdv[N_HEADS:]), "D_skip": lane_sums(tdd),
            "sinks": diag_row(tds), "attn_out_norm": t1[1:2, :ATTN_W], "ssm_out_norm": t1[1:2, ATTN_W:],
            "norm2": t2[0:1], "rel_bias": tdr[:, :N_HEADS], "final_norm": t2[4:5],
        }
        for k, name in enumerate(_SMALL_NAMES):
            w_ref, m_ref, v_ref = wmv[3 * k:3 * k + 3]
            g = grads[name]
            d, nm, nv = _adamw_math(w_ref[...], g, m_ref[...], v_ref[...])
            for o, val in zip(outs[4 * k:4 * k + 4], (g, d, nm, nv)):
                o[...] = val
        loss_ref, call_ref, dmod_ref = outs[4 * len(_SMALL_NAMES):]
        loss_ref[...] = t2[5:6, 0:1]
        call_ref[...] = jnp.concatenate([c_ref[i] for i in range(N_DEV)], axis=0)
        dmod_ref[...] = jnp.concatenate([_mod_row(a0_ref[i], a1_ref[i], a2_ref[i]) for i in range(N_DEV)], axis=0)

    out_shape = [jax.ShapeDtypeStruct(params[name][0].shape, F32) for name in _SMALL_NAMES for _ in range(4)]
    out_shape += [jax.ShapeDtypeStruct((1, 1), F32), jax.ShapeDtypeStruct((N_DEV, D_MODEL), F32),
                  jax.ShapeDtypeStruct((N_DEV, N_MOD), F32)]
    res = pl.pallas_call(body, name="small_update", out_shape=out_shape)(*gathered, *flat)
    upd = {name: res[4 * k:4 * k + 4] for k, name in enumerate(_SMALL_NAMES)}
    loss, c_all, dmod_all = res[4 * len(_SMALL_NAMES):]
    return upd, loss, c_all, dmod_all


def _ada_w_update(c_all, dmod_all, w, m, v):
    chunk = w.shape[1]

    def body(c_ref, dm_ref, w_ref, m_ref, v_ref, g_ref, d_ref, nm_ref, nv_ref):
        me = _lin(_my_pos())
        dm = jnp.zeros((N_DEV, chunk), F32)
        for j in range(N_DEV):
            dm = dm + dm_ref[:, j * chunk:(j + 1) * chunk] * jnp.where(me == j, 1.0, 0.0)
        g = lax.dot_general(_silu(c_ref[...]), dm, (((0,), (0,)), ((), ())), precision=HI,
                            preferred_element_type=F32)
        d, nm, nv = _adamw_math(w_ref[...], g, m_ref[...], v_ref[...])
        g_ref[...] = g
        d_ref[...] = d
        nm_ref[...] = nm
        nv_ref[...] = nv

    tr = 256
    blk = pl.BlockSpec((tr, chunk), _row)
    return pl.pallas_call(
        body, name="ada_w_update", grid=(w.shape[0] // tr,),
        in_specs=[pl.BlockSpec((N_DEV, tr), lambda i: (0, i)), pl.BlockSpec(dmod_all.shape, _fixed), blk, blk, blk],
        out_specs=[blk] * 4, out_shape=[jax.ShapeDtypeStruct(w.shape, F32)] * 4,
    )(c_all, dmod_all, w, m, v)


def _local_step(x, tgt, c, mod, w_in, conv_w, w_o_mine, w_gu_mine, w_d_mine, p):
    S = x.shape[0]
    tm = min(512, S)
    tmm = min(256, S)
    tw = min(2048, S)
    shift1, scale1, gate1, shift2, scale2, gate2 = [mod[i:i + 1] for i in range(6)]
    buckets = jnp.asarray(_t5_bucket_table())
    per_head = lambda a: jnp.broadcast_to(a.reshape(N_HEADS, 1), (N_HEADS, LANE))
    dtb_row, alog_row, d_exp = per_head(p["dt_bias"]), per_head(p["A_log"]), per_head(p["D_skip"])
    sinks = p["sinks"].reshape(N_HEADS)

    d_cut, gu_cut = WD_CUT, WGU_CUTS
    n_d, n_gu = w_d_mine.shape[0], w_gu_mine.shape[0]
    (qkv, z, xbc, dt_raw), (g_d_a,) = _in_proj_fwd(x, p["norm1"], scale1, shift1, w_in, tm,
                                                   ([(w_d_mine, 0, d_cut)], "two-level"))
    bias = _attn_bias(buckets, p["rel_bias"])
    (ya,), (g_gu_a,) = _attn_fwd(qkv, bias, sinks, ([(w_gu_mine, 0, gu_cut[0])], "two-level"))
    (ys, prev_states, pre_act), (g_gu_b, g_o) = _ssd_fwd(
        xbc, dt_raw, conv_w, p["conv_b"], dtb_row, alog_row, d_exp,
        ([(w_gu_mine, gu_cut[0], gu_cut[1] - gu_cut[0]), w_o_mine], "two-level"))
    w_o = g_o.reshape(D_MODEL, D_MODEL)
    x1, (g_gu_c, g_d_b) = _out_proj_fwd(
        x, ya, ys, z, p["attn_out_norm"], p["ssm_out_norm"], gate1, w_o, tm,
        ([(w_gu_mine, gu_cut[1], n_gu - gu_cut[1]), (w_d_mine, d_cut, n_d - d_cut)], "two-level"))
    dx1, h2, dgu, act, dmlp, acc2 = _mlp_loss(x1, tgt, p["norm2"], scale2, shift2, gate2, p["final_norm"],
                                              (g_gu_a, g_gu_b, g_gu_c), (g_d_a, g_d_b), tmm)
    g_w_gu = _wgrad(dgu, h2, 2 * D_FF // 4, tw, "wgrad_gate_up")
    g_w_d = _wgrad(act, dmlp, D_FF // 2, tw, "wgrad_down")
    gu_slots = g_w_gu.reshape(N_DEV, 2 * D_FF // N_DEV, D_MODEL)
    (dya, dys, dz, g_w_o, acc1), (r_gu_a,) = _out_proj_bwd(
        dx1, ya, ys, z, p["attn_out_norm"], p["ssm_out_norm"], gate1, w_o, tm, ([gu_slots], ("rows", 0, GGU_CUT)))
    (dq, dkv, dbias, dsk), (r_d, r_o) = _attn_bwd(
        qkv, ya, dya, bias, sinks,
        ([g_w_d.reshape(N_DEV, D_FF // N_DEV, D_MODEL), g_w_o.reshape(N_DEV, D_MODEL // N_DEV, D_MODEL)], True))
    drel, dsink = _attn_finish(dbias, dsk, buckets)
    (dxbc, ddt, dcw, dvec, dd), (r_gu_b, *early) = _ssd_bwd(
        xbc, pre_act, dt_raw, prev_states, dys, conv_w, dtb_row, alog_row, d_exp,
        [([gu_slots], ("rows", GGU_CUT, 2 * D_FF // N_DEV - GGU_CUT)), ([acc1, acc2, dsink, drel, c], False)])
    r_gu = (r_gu_a, r_gu_b)
    gx, h1, acc0, g_in_a = _in_proj_bwd(x, dx1, dq, dkv, dz, dxbc, ddt, p["norm1"], scale1, shift1, w_in, tm)
    half = D_MODEL // 2
    slots = lambda g: g[:IN_W].reshape(N_DEV, IN_W // N_DEV, half)
    g_in_b, (r_in_a,) = _wgrad((dq, dkv, dz, dxbc, ddt), h1, IN_PAD, tw, "wgrad_in_b",
                               ([slots(g_in_a)], "two-level scatter"), g_cols=(half, 1))
    return gx, (r_in_a, slots(g_in_b)), (r_o, r_gu, r_d), early, (acc0, dcw, dvec, dd)


def kernel(x, c, ada_w, ada_b, norm1, w_in, conv_w, conv_b, dt_bias, A_log, D_skip, sinks, attn_out_norm, ssm_out_norm, w_o, norm2, w_gate_up, w_down, rel_bias, final_norm, loss_target, m_ada_w, m_ada_b, m_norm1, m_w_in, m_conv_w, m_conv_b, m_dt_bias, m_A_log, m_D_skip, m_sinks, m_attn_out_norm, m_ssm_out_norm, m_w_o, m_norm2, m_w_gate_up, m_w_down, m_rel_bias, m_final_norm, v_ada_w, v_ada_b, v_norm1, v_w_in, v_conv_w, v_conv_b, v_dt_bias, v_A_log, v_D_skip, v_sinks, v_attn_out_norm, v_ssm_out_norm, v_w_o, v_norm2, v_w_gate_up, v_w_down, v_rel_bias, v_final_norm):
    two_d = lambda a: a if a.ndim == 2 else a.reshape(-1, a.shape[-1])
    small_params = dict(
        ada_b=(ada_b, m_ada_b, v_ada_b), norm1=(norm1, m_norm1, v_norm1), conv_w=(conv_w, m_conv_w, v_conv_w),
        conv_b=(conv_b, m_conv_b, v_conv_b), dt_bias=(dt_bias, m_dt_bias, v_dt_bias), A_log=(A_log, m_A_log, v_A_log),
        D_skip=(D_skip, m_D_skip, v_D_skip), sinks=(sinks, m_sinks, v_sinks),
        attn_out_norm=(attn_out_norm, m_attn_out_norm, v_attn_out_norm),
        ssm_out_norm=(ssm_out_norm, m_ssm_out_norm, v_ssm_out_norm), norm2=(norm2, m_norm2, v_norm2),
        rel_bias=(rel_bias, m_rel_bias, v_rel_bias), final_norm=(final_norm, m_final_norm, v_final_norm))
    small_params = {k: tuple(two_d(a) for a in v) for k, v in small_params.items()}
    S = x.shape[1]
    xs, tgt = x.reshape(S, D_MODEL), loss_target.reshape(S, D_MODEL)
    ada_w2 = ada_w[0]
    chunk = ada_w2.shape[1]
    t_in = [jnp.transpose(a[0]) for a in (w_in, m_w_in, v_w_in)]
    t_gu = [jnp.transpose(a[0]) for a in (w_gate_up, m_w_gate_up, v_w_gate_up)]

    mod, (g_in, g_cw) = _mod_and_gather(c, ada_w2, ada_b.reshape(N_DEV, chunk), [t_in[0].astype(WIRE_DTYPE), conv_w[0]])
    mod = mod.reshape(6, D_MODEL)
    w_in_full = jnp.pad(g_in.reshape(IN_W, D_MODEL), ((0, IN_PAD - IN_W), (0, 0)))
    conv_w_full = jnp.transpose(g_cw, (1, 0, 2)).reshape(4, XBC_W)

    p = {k: v[0] for k, v in small_params.items()}
    gx, (r_in_a, gw_in_b), (r_o, r_gu, r_d), early, late_blocks = _local_step(
        xs, tgt, c, mod, w_in_full, conv_w_full, w_o[0].astype(WIRE_DTYPE), t_gu[0].astype(WIRE_DTYPE),
        w_down[0].astype(WIRE_DTYPE), p)

    (u_gu, u_d, u_o), (r_in_b, *late) = _reduce_adamw_hosting(
        [r_gu, r_d, r_o], [tuple(t_gu), (w_down[0], m_w_down[0], v_w_down[0]), (w_o[0], m_w_o[0], v_w_o[0])],
        "adamw_big", [([gw_in_b], "two-level scatter"), (list(late_blocks), False)])
    gathered = (late[0], early[0], early[1], late[1], late[2], late[3], early[2], early[3], early[4])

    small, loss, c_all, dmod_all = _small_update(gathered, small_params)

    big = {
        "ada_w": _ada_w_update(c_all, dmod_all, ada_w2, m_ada_w[0], v_ada_w[0]),
        "w_in": [jnp.transpose(a) for a in _reduce_adamw_halves(r_in_a, r_in_b, *t_in, "adamw_w_in")],
        "w_o": u_o,
        "w_gate_up": [jnp.transpose(a) for a in u_gu],
        "w_down": u_d,
    }
    big.update(small)

    order = ['ada_w', 'ada_b', 'norm1', 'w_in', 'conv_w', 'conv_b', 'dt_bias', 'A_log', 'D_skip', 'sinks',
             'attn_out_norm', 'ssm_out_norm', 'w_o', 'norm2', 'w_gate_up', 'w_down', 'rel_bias', 'final_norm']
    shapes = dict(ada_w=ada_w.shape, ada_b=ada_b.shape, norm1=norm1.shape, w_in=w_in.shape, conv_w=conv_w.shape,
                  conv_b=conv_b.shape, dt_bias=dt_bias.shape, A_log=A_log.shape, D_skip=D_skip.shape,
                  sinks=sinks.shape, attn_out_norm=attn_out_norm.shape, ssm_out_norm=ssm_out_norm.shape,
                  w_o=w_o.shape, norm2=norm2.shape, w_gate_up=w_gate_up.shape, w_down=w_down.shape,
                  rel_bias=rel_bias.shape, final_norm=final_norm.shape)
    outs = [[], [], [], []]
    for name in order:
        for kind in range(4):
            outs[kind].append(big[name][kind].reshape(shapes[name]))
    return (loss.reshape(()), gx.reshape(x.shape), *outs[0], *outs[1], *outs[2], *outs[3])
```

```python
import numpy as np
import jax
import jax.numpy as jnp
from jax import lax
from jax.experimental import pallas as pl
from jax.experimental.pallas import tpu as pltpu

F32 = jnp.float32
MXU_DTYPE = jnp.bfloat16
WIRE_DTYPE = jnp.bfloat16
HI = lax.Precision.HIGHEST
MESH = pl.DeviceIdType.MESH
N_DEV = 8

D_MODEL = 1024
ATTN_W = 512
KV_W = 128
SSM_W = 512
XBC_W = 1024
N_HEADS = 8
D_STATE = 128
D_FF = 2816
IN_W = 2312
IN_PAD = 2432
BLK = 128
N_BUCKETS = 32
EPS = 1e-6
LANE = 128
HALF = 64

ADAM_LR, ADAM_B1, ADAM_B2, ADAM_EPS, ADAM_WD, ADAM_STEP = 0.001, 0.9, 0.999, 1e-08, 0.01, 10

VMEM_BIG = 56 * 1024 * 1024
WD_CUT = 288
WGU_CUTS = (240, 496)
GGU_CUT = 304


def _cparams(vmem=None):
    if vmem is None:
        return pltpu.CompilerParams()
    return pltpu.CompilerParams(vmem_limit_bytes=vmem)


def _mm(a, b):
    return jnp.dot(a.astype(MXU_DTYPE), b.astype(MXU_DTYPE), preferred_element_type=F32)


def _mm_nt(a, b):
    return lax.dot_general(a.astype(MXU_DTYPE), b.astype(MXU_DTYPE), (((1,), (1,)), ((), ())),
                           preferred_element_type=F32)


def _mm_tn(a, b):
    return lax.dot_general(a.astype(MXU_DTYPE), b.astype(MXU_DTYPE), (((0,), (0,)), ((), ())),
                           preferred_element_type=F32)


def _mm_hi(a, b):
    return jnp.dot(a, b, precision=HI, preferred_element_type=F32)


def _silu(x):
    return x * jax.nn.sigmoid(x)


def _softplus(x):
    return jnp.maximum(x, 0.0) + jnp.log1p(jnp.exp(-jnp.abs(x)))


def _rms(x, g, n):
    return x * lax.rsqrt(jnp.sum(x * x, axis=-1, keepdims=True) * (1.0 / n) + EPS) * g


def _modnorm(x, g, scale, shift):
    return _rms(x, g, x.shape[-1]) * (1.0 + scale) + shift


def _modnorm_parts(x):
    r = lax.rsqrt(jnp.sum(x * x, axis=-1, keepdims=True) * (1.0 / x.shape[-1]) + EPS)
    return r, x * r


def _modnorm_bwd(r, xhat, g, scale, dy):
    dyg = dy * (g * (1.0 + scale))
    c = jnp.sum(dyg * xhat, axis=-1, keepdims=True) * (1.0 / xhat.shape[-1])
    dx = r * (dyg - xhat * c)
    ct = jnp.sum(dy * xhat, axis=0, keepdims=True)
    return dx, ct * (1.0 + scale), ct * g, jnp.sum(dy, axis=0, keepdims=True)


def _lane_iota(shape):
    return lax.broadcasted_iota(jnp.int32, shape, len(shape) - 1)


def _split_pair(t):
    lane = _lane_iota(t.shape)
    lo = jnp.where(lane < HALF, t, 0.0)
    hi = pltpu.roll(jnp.where(lane >= HALF, t, 0.0), HALF, 1)
    return lo, hi


def _join_pair(lo, hi):
    lane = _lane_iota(lo.shape)
    return jnp.where(lane < HALF, lo, pltpu.roll(hi, HALF, 1))


def _split_heads(t, n_pairs):
    out = []
    for p in range(n_pairs):
        out.extend(_split_pair(t[:, p * LANE:(p + 1) * LANE]))
    return out


def _join_heads(hs):
    return jnp.concatenate([_join_pair(hs[2 * p], hs[2 * p + 1]) for p in range(len(hs) // 2)], axis=1)


def _t5_bucket_table():
    dist = np.arange(BLK)[:, None] + BLK - np.arange(2 * BLK)[None, :]
    n = np.maximum(dist, 0)
    max_exact = N_BUCKETS // 2
    large = max_exact + (np.log(np.maximum(n, 1) / max_exact) / np.log(128 / max_exact)
                         * (N_BUCKETS - max_exact)).astype(np.int32)
    large = np.minimum(large, N_BUCKETS - 1)
    return np.where(n < max_exact, n, large).astype(np.int32)


def _my_pos():
    return lax.axis_index("x"), lax.axis_index("y"), lax.axis_index("c")


def _peer(k):
    x, y, c = _my_pos()
    return (1 - x if k & 4 else x, 1 - y if k & 2 else y, 1 - c if k & 1 else c)


def _lin(pos):
    return 4 * pos[0] + 2 * pos[1] + pos[2]


def _xchg_copies(ins, outs, sems, scatter):
    local_sem, send_sem, recv_sem = sems
    me = _lin(_my_pos())

    def source(a, slot):
        if not scatter:
            return ins[a]
        if scatter is True:
            return ins[a].at[slot]
        return ins[a].at[slot, pl.ds(scatter[1], scatter[2])]

    local, remote = [], []
    for a in range(len(ins)):
        local.append(pltpu.make_async_copy(source(a, me), outs[a].at[me], local_sem.at[a]))
    for k in range(1, N_DEV):
        peer = _peer(k)
        for a in range(len(ins)):
            remote.append(pltpu.make_async_remote_copy(source(a, _lin(peer)), outs[a].at[me], send_sem.at[a, k - 1],
                                                       recv_sem.at[a, k - 1], device_id=peer, device_id_type=MESH))
    return local, remote


def _xchg_start(ins, outs, sems, scatter):
    local, remote = _xchg_copies(ins, outs, sems, scatter)
    for cp in local + remote:
        cp.start()


def _xchg_wait(ins, outs, sems, scatter):
    local, remote = _xchg_copies(ins, outs, sems, scatter)
    for cp in local:
        cp.wait()
    for cp in remote:
        cp.wait_send()
        cp.wait_recv()


def _xchg_shapes(arrs, scatter):
    n = len(arrs)
    if isinstance(scatter, tuple):
        out_shape = [jax.ShapeDtypeStruct((a.shape[0], scatter[2]) + a.shape[2:], a.dtype) for a in arrs]
    elif scatter:
        out_shape = [jax.ShapeDtypeStruct(a.shape, a.dtype) for a in arrs]
    else:
        out_shape = [jax.ShapeDtypeStruct((N_DEV,) + a.shape, a.dtype) for a in arrs]
    sems = [pltpu.SemaphoreType.DMA((n,)), pltpu.SemaphoreType.DMA((n, N_DEV - 1)),
            pltpu.SemaphoreType.DMA((n, N_DEV - 1))]
    return out_shape, sems


_CHIPS = (2, 4, 6)


def _g2_sems(n):
    dma = pltpu.SemaphoreType.DMA
    return [dma((n,)), dma((n, N_DEV, N_PIECES)), dma((n, N_DEV, N_PIECES)), dma((n, len(_CHIPS))),
            dma((n, len(_CHIPS)))]


class _TwoLevelGather:
    def __init__(self, ins, outs, sems, windows=None):
        self.ins, self.outs = ins, outs
        self.local_sem, self.send_sem, self.recv_sem, self.fsend_sem, self.frecv_sem = sems
        self.n = len(ins)
        self.windows = windows or [None] * self.n

    def _mine(self, a):
        w = self.windows[a]
        return self.ins[a] if w is None else self.ins[a].at[pl.ds(w[0], w[1])]

    def _direct(self, a, k):
        w = self.windows[a]
        w0, rows = (0, self.ins[a].shape[0]) if w is None else w
        pieces = _row_pieces(rows) if k in _CHIPS else [(0, rows)]
        me = _lin(_my_pos())
        return [pltpu.make_async_remote_copy(self.ins[a].at[pl.ds(w0 + r0, nr)], self.outs[a].at[me, pl.ds(r0, nr)],
                                             self.send_sem.at[a, k, i], self.recv_sem.at[a, k, i],
                                             device_id=_peer(k), device_id_type=MESH)
                for i, (r0, nr) in enumerate(pieces)]

    def _handed_on(self, a, j, origin):
        slot = self.outs[a].at[origin]
        return pltpu.make_async_remote_copy(slot, slot, self.fsend_sem.at[a, j], self.frecv_sem.at[a, j],
                                            device_id=_peer(1), device_id_type=MESH)

    def _local(self, a):
        return pltpu.make_async_copy(self._mine(a), self.outs[a].at[_lin(_my_pos())], self.local_sem.at[a])

    def start(self):
        for a in range(self.n):
            self._local(a).start()
        for k in (1,) + _CHIPS:
            for a in range(self.n):
                for cp in self._direct(a, k):
                    cp.start()

    def forward(self):
        for j, k in enumerate(_CHIPS):
            for a in range(self.n):
                for cp in self._direct(a, k):
                    cp.wait_recv()
                self._handed_on(a, j, _lin(_peer(k))).start()

    def finish(self):
        for a in range(self.n):
            for cp in self._direct(a, 1):
                cp.wait_recv()
            for j, k in enumerate(_CHIPS):
                self._handed_on(a, j, _lin(_peer(k ^ 1))).wait_recv()
            self._local(a).wait()
            for k in (1,) + _CHIPS:
                for cp in self._direct(a, k):
                    cp.wait_send()
            for j, k in enumerate(_CHIPS):
                self._handed_on(a, j, _lin(_peer(k))).wait_send()


N_SCATTERED = 2 + len(_CHIPS)
N_PIECES = 4
PIECE_ALIGN = 16


def _row_pieces(rows):
    step = -(-rows // (N_PIECES * PIECE_ALIGN)) * PIECE_ALIGN
    if step * (N_PIECES - 1) >= rows:
        return [(0, rows)]
    return [(i * step, min(step, rows - i * step)) for i in range(N_PIECES)]


class _TwoLevelScatter:
    def __init__(self, src, out, stage, buf_a, buf_b, sems):
        self.src, self.out, self.stage, self.buf_a, self.buf_b = src, out, stage, buf_a, buf_b
        self.local_sem, self.dsend, self.drecv, self.csend, self.crecv = sems

    def _own(self):
        return pltpu.make_async_copy(self.src.at[_lin(_my_pos())], self.out.at[0], self.local_sem.at[0])

    def _to_core(self, j):
        q = 0 if j == 0 else _CHIPS[j - 1]
        dst = self.out.at[1] if j == 0 else self.stage.at[j - 1]
        return pltpu.make_async_remote_copy(self.src.at[_lin(_peer(q ^ 1))], dst, self.dsend.at[j], self.drecv.at[j],
                                            device_id=_peer(1), device_id_type=MESH)

    def _loads(self, j):
        mine = self.src.at[_lin(_peer(_CHIPS[j]))]
        return (pltpu.make_async_copy(self.stage.at[j], self.buf_a.at[j], self.local_sem.at[1 + 2 * j]),
                pltpu.make_async_copy(mine, self.buf_b.at[j], self.local_sem.at[2 + 2 * j]))

    def _to_chip(self, j):
        return [pltpu.make_async_remote_copy(self.buf_a.at[j, pl.ds(r0, nr)], self.out.at[2 + j, pl.ds(r0, nr)],
                                             self.csend.at[j, i], self.crecv.at[j, i],
                                             device_id=_peer(_CHIPS[j]), device_id_type=MESH)
                for i, (r0, nr) in enumerate(_row_pieces(self.src.shape[1]))]

    def start(self):
        self._own().start()
        for j in range(1 + len(_CHIPS)):
            self._to_core(j).start()

    def forward(self):
        for j in range(len(_CHIPS)):
            self._to_core(j + 1).wait_recv()
            for cp in self._loads(j):
                cp.start()
        for j in range(len(_CHIPS)):
            for cp in self._loads(j):
                cp.wait()
            self.buf_a[j] = (self.buf_a[j].astype(F32) + self.buf_b[j].astype(F32)).astype(self.buf_a.dtype)
            for cp in self._to_chip(j):
                cp.start()

    def finish(self):
        self._own().wait()
        self._to_core(0).wait_recv()
        for j in range(1 + len(_CHIPS)):
            self._to_core(j).wait_send()
        for j in range(len(_CHIPS)):
            for cp in self._to_chip(j):
                cp.wait_send()
                cp.wait_recv()


def _s2_shapes(a):
    dma = pltpu.SemaphoreType.DMA
    n_c = len(_CHIPS)
    piece = a.shape[1:]
    return (jax.ShapeDtypeStruct((N_SCATTERED,) + piece, a.dtype), jax.ShapeDtypeStruct((n_c,) + piece, a.dtype),
            [pltpu.VMEM((n_c,) + piece, a.dtype)] * 2,
            [dma((1 + 2 * n_c,)), dma((1 + n_c,)), dma((1 + n_c,)), dma((n_c, N_PIECES)), dma((n_c, N_PIECES))])


def _mod_and_gather(c, ada_w, ada_b8, arrs):
    n = len(arrs)
    chunk = ada_w.shape[1]
    out_shape = [jax.ShapeDtypeStruct((N_DEV, 1, chunk), F32)]
    out_shape += [jax.ShapeDtypeStruct((N_DEV,) + a.shape, a.dtype) for a in arrs]

    def modulation(c_ref, w_ref, b_ref, out_ref, cbuf, part, s1, r1, s2, r2):
        me = _lin(_my_pos())
        first = []
        for k in range(1, N_DEV):
            cp = pltpu.make_async_remote_copy(c_ref, cbuf.at[me], s1.at[k - 1], r1.at[k - 1],
                                              device_id=_peer(k), device_id_type=MESH)
            cp.start()
            first.append(cp)
        cbuf[me] = c_ref[...]
        for cp in first:
            cp.wait_send()
            cp.wait_recv()
        cond = _silu(jnp.concatenate([cbuf[i] for i in range(N_DEV)], axis=0))
        mod = _mm_hi(cond, w_ref[...]) + b_ref[pl.ds(me, 1), :]
        for j in range(N_DEV):
            part[j] = mod[j:j + 1, :]
        second = []
        for k in range(1, N_DEV):
            peer = _peer(k)
            cp = pltpu.make_async_remote_copy(part.at[_lin(peer)], out_ref.at[me], s2.at[k - 1], r2.at[k - 1],
                                              device_id=peer, device_id_type=MESH)
            cp.start()
            second.append(cp)
        out_ref[me] = part[me]
        for cp in second:
            cp.wait_send()
            cp.wait_recv()

    def body(*refs):
        c_ref, w_ref, b_ref = refs[:3]
        ins = refs[3:3 + n]
        mod_ref = refs[3 + n]
        outs = refs[4 + n:4 + 2 * n]
        cbuf, part, s1, r1, s2, r2 = refs[4 + 2 * n:10 + 2 * n]
        gather = _TwoLevelGather(ins, outs, refs[10 + 2 * n:])
        gather.start()
        modulation(c_ref, w_ref, b_ref, mod_ref, cbuf, part, s1, r1, s2, r2)
        gather.forward()
        gather.finish()

    hbm = pl.BlockSpec(memory_space=pltpu.HBM)
    vm = pl.BlockSpec(memory_space=pltpu.VMEM)
    dma = pltpu.SemaphoreType.DMA
    res = pl.pallas_call(
        body, name="mod_and_gather", out_shape=out_shape, in_specs=[vm, vm, vm] + [hbm] * n,
        out_specs=[vm] + [hbm] * n,
        scratch_shapes=[pltpu.VMEM((N_DEV, 1, D_MODEL), F32), pltpu.VMEM((N_DEV, 1, chunk), F32)]
        + [dma((N_DEV - 1,))] * 4 + _g2_sems(n),
    )(c, ada_w, ada_b8, *arrs)
    return res[0], res[1:]


def _hosted_call(body, name, grid, in_specs, out_specs, out_shape, scratch_shapes, args, xchg, cparams):
    xchgs = [xchg] if isinstance(xchg, tuple) else list(xchg)
    grid = (grid,) if isinstance(grid, int) else tuple(grid)
    n_in, n_out, n_scr = len(in_specs), len(out_specs), len(scratch_shapes)
    windows = [[(a[1], a[2]) if isinstance(a, tuple) else None for a in group] for group, _ in xchgs]
    xchgs = [([a[0] if isinstance(a, tuple) else a for a in group], mode) for group, mode in xchgs]
    arrs = [a for group, _ in xchgs for a in group]
    n = len(arrs)
    x_shape, x_sems, sem_counts, stage_shape, stage_bufs = [], [], [], [], []
    for (group, mode), wins in zip(xchgs, windows):
        if mode == "two-level scatter":
            (a,) = group
            res_shape, stage, bufs, sems = _s2_shapes(a)
            shapes = [res_shape]
            stage_shape.append(stage)
            stage_bufs += bufs
        else:
            shapes, sems = _xchg_shapes(group, False if mode == "two-level" else mode)
        if mode == "two-level":
            sems = _g2_sems(len(group))
            shapes = [s if w is None else jax.ShapeDtypeStruct((N_DEV, w[1]) + a.shape[1:], a.dtype)
                      for s, w, a in zip(shapes, wins, group)]
        x_shape += shapes
        x_sems += sems
        sem_counts.append(len(sems))
    n_stage = len(stage_shape)
    n_steps = int(np.prod(grid))
    staged = ("two-level", "two-level scatter")

    def hosted(*refs):
        ins, refs = refs[:n_in], refs[n_in:]
        x_in, refs = refs[:n], refs[n:]
        outs, refs = refs[:n_out], refs[n_out:]
        x_out, refs = refs[:n], refs[n:]
        stages, refs = refs[:n_stage], refs[n_stage:]
        scr, refs = refs[:n_scr], refs[n_scr:]
        bufs, sems = refs[:2 * n_stage], refs[2 * n_stage:]
        step = pl.program_id(0)
        for d in range(1, len(grid)):
            step = step * grid[d] + pl.program_id(d)
        parts, a0, s0, t0 = [], 0, 0, 0
        for (group, mode), ns, wins in zip(xchgs, sem_counts, windows):
            gi, go, gs = x_in[a0:a0 + len(group)], x_out[a0:a0 + len(group)], sems[s0:s0 + ns]
            if mode == "two-level":
                parts.append((mode, _TwoLevelGather(gi, go, gs, wins)))
            elif mode == "two-level scatter":
                parts.append((mode, _TwoLevelScatter(gi[0], go[0], stages[t0], bufs[2 * t0], bufs[2 * t0 + 1], gs)))
                t0 += 1
            else:
                parts.append((mode, (gi, go, gs, mode)))
            a0, s0 = a0 + len(group), s0 + ns

        @pl.when(step == 0)
        def _():
            for mode, x in parts:
                if mode in staged:
                    x.start()
                else:
                    _xchg_start(*x)

        for kind, at in (("two-level", (2 * n_steps) // 3), ("two-level scatter", n_steps // 4)):
            if any(mode == kind for mode, _ in parts):
                @pl.when(step == at)
                def _():
                    for mode, x in parts:
                        if mode == kind:
                            x.forward()

        body(*ins, *outs, *scr)

        @pl.when(step == n_steps - 1)
        def _():
            for mode, x in parts:
                if mode in staged:
                    x.finish()
                else:
                    _xchg_wait(*x)

    hbm = pl.BlockSpec(memory_space=pltpu.HBM)
    res = pl.pallas_call(
        hosted, name=name, grid=grid, in_specs=list(in_specs) + [hbm] * n,
        out_specs=list(out_specs) + [hbm] * (n + n_stage), out_shape=list(out_shape) + x_shape + stage_shape,
        scratch_shapes=list(scratch_shapes) + stage_bufs + x_sems, compiler_params=cparams,
    )(*args, *arrs)
    return res[:n_out], res[n_out:n_out + n]


def _row(i):
    return (i, 0)


def _fixed(i):
    return (0, 0)


def _in_proj_fwd(x, norm1, scale1, shift1, w_in, tm, xchg):
    S = x.shape[0]

    def body(x_ref, n_ref, sc_ref, sh_ref, w_ref, qkv_ref, z_ref, xbc_ref, dt_ref):
        h = _modnorm(x_ref[...], n_ref[...], sc_ref[...], sh_ref[...])
        p = _mm_nt(h, w_ref[...])
        qkv_ref[...] = p[:, :768].astype(qkv_ref.dtype)
        z_ref[...] = p[:, 768:1280]
        xbc_ref[...] = p[:, 1280:2304]
        dt_ref[...] = p[:, 2304:IN_PAD]

    vec = pl.BlockSpec((1, D_MODEL), _fixed)
    return _hosted_call(
        body, "in_proj_fwd", S // tm,
        in_specs=[pl.BlockSpec((tm, D_MODEL), _row), vec, vec, vec, pl.BlockSpec((IN_PAD, D_MODEL), _fixed)],
        out_specs=[pl.BlockSpec((tm, 768), _row), pl.BlockSpec((tm, SSM_W), _row),
                   pl.BlockSpec((tm, XBC_W), _row), pl.BlockSpec((tm, LANE), _row)],
        out_shape=[jax.ShapeDtypeStruct((S, 768), MXU_DTYPE), jax.ShapeDtypeStruct((S, SSM_W), F32),
                   jax.ShapeDtypeStruct((S, XBC_W), F32), jax.ShapeDtypeStruct((S, LANE), F32)],
        scratch_shapes=[], args=(x, norm1, scale1, shift1, w_in), xchg=xchg, cparams=_cparams(VMEM_BIG),
    )


def _in_proj_bwd(x, dx1, dq, dkv, dz, dxbc, ddt, norm1, scale1, shift1, w_in, tm):
    S = x.shape[0]

    n_steps = S // tm
    half_cols = D_MODEL // 2

    def body(x_ref, dx1_ref, dq_ref, dkv_ref, dz_ref, dxbc_ref, ddt_ref, n_ref, sc_ref, sh_ref, w_ref,
             gx_ref, h_ref, acc_ref, gw_ref, gw_acc):
        i = pl.program_id(0)

        @pl.when(i == 0)
        def _():
            acc_ref[...] = jnp.zeros_like(acc_ref)
            gw_acc[...] = jnp.zeros_like(gw_acc)

        halves = [pl.ds(k * (tm // 2), tm // 2) for k in range(2)]
        dp = [jnp.concatenate([r[rows, :] for r in (dq_ref, dkv_ref, dz_ref, dxbc_ref, ddt_ref)], axis=1)
              for rows in halves]
        dh = [_mm(dp[k], w_ref[...]) for k in range(2)]
        parts = [_modnorm_parts(x_ref[rows, :]) for rows in halves]
        hb = [(parts[k][1] * n_ref[...] * (1.0 + sc_ref[...]) + sh_ref[...]).astype(h_ref.dtype) for k in range(2)]
        gw_acc[...] += _mm_tn(dp[0], hb[0][:, :half_cols]) + _mm_tn(dp[1], hb[1][:, :half_cols])
        bwd = [_modnorm_bwd(parts[k][0], parts[k][1], n_ref[...], sc_ref[...], dh[k]) for k in range(2)]
        for k, rows in enumerate(halves):
            gx_ref[rows, :] = dx1_ref[rows, :] + bwd[k][0]
            h_ref[rows, :] = hb[k]
        acc_ref[0:1, :] += bwd[0][1] + bwd[1][1]
        acc_ref[1:2, :] += bwd[0][2] + bwd[1][2]
        acc_ref[2:3, :] += bwd[0][3] + bwd[1][3]

        @pl.when(i == n_steps - 1)
        def _():
            gw_ref[...] = gw_acc[...].astype(gw_ref.dtype)

    vec = pl.BlockSpec((1, D_MODEL), _fixed)
    return pl.pallas_call(
        body, name="in_proj_bwd", grid=(n_steps,),
        in_specs=[pl.BlockSpec((tm, D_MODEL), _row), pl.BlockSpec((tm, D_MODEL), _row),
                  pl.BlockSpec((tm, ATTN_W), _row), pl.BlockSpec((tm, 2 * KV_W), _row),
                  pl.BlockSpec((tm, SSM_W), _row), pl.BlockSpec((tm, XBC_W), _row), pl.BlockSpec((tm, LANE), _row),
                  vec, vec, vec, pl.BlockSpec((IN_PAD, D_MODEL), _fixed)],
        out_specs=[pl.BlockSpec((tm, D_MODEL), _row), pl.BlockSpec((tm, D_MODEL), _row),
                   pl.BlockSpec((8, D_MODEL), _fixed), pl.BlockSpec((IN_PAD, half_cols), _fixed)],
        out_shape=[jax.ShapeDtypeStruct((S, D_MODEL), F32), jax.ShapeDtypeStruct((S, D_MODEL), MXU_DTYPE),
                   jax.ShapeDtypeStruct((8, D_MODEL), F32), jax.ShapeDtypeStruct((IN_PAD, half_cols), WIRE_DTYPE)],
        scratch_shapes=[pltpu.VMEM((IN_PAD, half_cols), F32)],
        compiler_params=_cparams(VMEM_BIG),
    )(x, dx1, dq, dkv, dz, dxbc, ddt, norm1, scale1, shift1, w_in)


def _out_stage(ya, ys0, ys1, z0, z1, an, sn0, sn1):
    half = SSM_W // 2
    a = _rms(ya, an, ATTN_W)
    g0 = _rms(ys0 * _silu(z0), sn0, half)
    g1 = _rms(ys1 * _silu(z1), sn1, half)
    return jnp.concatenate([a, g0, g1], axis=1)


def _out_stage_args(ya_ref, ys_ref, z_ref, an_ref, sn_ref):
    half = SSM_W // 2
    return (ya_ref[...], ys_ref[:, :half], ys_ref[:, half:], z_ref[:, :half], z_ref[:, half:],
            an_ref[...], sn_ref[:, :half], sn_ref[:, half:])


def _out_proj_fwd(x, ya, ys, z, an, sn, gate1, w_o, tm, xchg):
    S = x.shape[0]

    def body(x_ref, ya_ref, ys_ref, z_ref, an_ref, sn_ref, g_ref, w_ref, x1_ref):
        u = _out_stage(*_out_stage_args(ya_ref, ys_ref, z_ref, an_ref, sn_ref))
        x1_ref[...] = x_ref[...] + g_ref[...] * _mm(u, w_ref[...])

    half = pl.BlockSpec((tm, ATTN_W), _row)
    hvec = pl.BlockSpec((1, ATTN_W), _fixed)
    (x1,), x_out = _hosted_call(
        body, "out_proj_fwd", S // tm,
        in_specs=[pl.BlockSpec((tm, D_MODEL), _row), half, half, half, hvec, hvec,
                  pl.BlockSpec((1, D_MODEL), _fixed), pl.BlockSpec((D_MODEL, D_MODEL), _fixed)],
        out_specs=[pl.BlockSpec((tm, D_MODEL), _row)],
        out_shape=[jax.ShapeDtypeStruct((S, D_MODEL), F32)],
        scratch_shapes=[], args=(x, ya, ys, z, an, sn, gate1, w_o), xchg=xchg, cparams=_cparams(VMEM_BIG),
    )
    return x1, x_out


def _out_proj_bwd(dx1, ya, ys, z, an, sn, gate1, w_o, tm, xchg):
    S = dx1.shape[0]
    n_steps = S // tm

    def body(dx1_ref, ya_ref, ys_ref, z_ref, an_ref, sn_ref, g_ref, w_ref,
             dya_ref, dys_ref, dz_ref, gw_ref, acc_ref, gw_acc):
        i = pl.program_id(0)

        @pl.when(i == 0)
        def _():
            acc_ref[...] = jnp.zeros_like(acc_ref)
            gw_acc[...] = jnp.zeros_like(gw_acc)

        u, vjp = jax.vjp(_out_stage, *_out_stage_args(ya_ref, ys_ref, z_ref, an_ref, sn_ref))
        dx1 = dx1_ref[...]
        ub = u.astype(MXU_DTYPE)
        mix = _mm(ub, w_ref[...])
        dmix = dx1 * g_ref[...]
        dmixb = dmix.astype(MXU_DTYPE)
        du = _mm_nt(dmixb, w_ref[...])
        gw_acc[...] += _mm_tn(ub, dmixb)
        dya, dys0, dys1, dz0, dz1, dan, dsn0, dsn1 = vjp(du)
        dya_ref[...] = dya
        dys_ref[...] = jnp.concatenate([dys0, dys1], axis=1)
        dz_ref[...] = jnp.concatenate([dz0, dz1], axis=1).astype(dz_ref.dtype)
        acc_ref[0:1, :] += jnp.sum(dx1 * mix, axis=0, keepdims=True)
        acc_ref[1:2, :] += jnp.concatenate([dan, dsn0, dsn1], axis=1)

        @pl.when(i == n_steps - 1)
        def _():
            gw_ref[...] = gw_acc[...].astype(gw_ref.dtype)

    half = pl.BlockSpec((tm, ATTN_W), _row)
    hvec = pl.BlockSpec((1, ATTN_W), _fixed)
    full = pl.BlockSpec((tm, D_MODEL), _row)
    return _hosted_call(
        body, "out_proj_bwd", n_steps,
        in_specs=[full, half, half, half, hvec, hvec,
                  pl.BlockSpec((1, D_MODEL), _fixed), pl.BlockSpec((D_MODEL, D_MODEL), _fixed)],
        out_specs=[half, half, half, pl.BlockSpec((D_MODEL, D_MODEL), _fixed), pl.BlockSpec((8, D_MODEL), _fixed)],
        out_shape=[jax.ShapeDtypeStruct((S, ATTN_W), F32)] * 2 + [jax.ShapeDtypeStruct((S, ATTN_W), MXU_DTYPE),
                   jax.ShapeDtypeStruct((D_MODEL, D_MODEL), WIRE_DTYPE), jax.ShapeDtypeStruct((8, D_MODEL), F32)],
        scratch_shapes=[pltpu.VMEM((D_MODEL, D_MODEL), F32)],
        args=(dx1, ya, ys, z, an, sn, gate1, w_o), xchg=xchg, cparams=_cparams(VMEM_BIG),
    )


def _loss_rows(x2, fn, tgt):
    y = _rms(x2, fn, D_MODEL)
    per_row = jnp.sum(jnp.square(y - tgt), axis=1, keepdims=True)
    return jnp.sum(per_row, axis=0, keepdims=True) * (0.5 / D_MODEL)


def _mlp_loss(x1, tgt, norm2, scale2, shift2, gate2, fnorm, w_gu, w_d, tm):
    S = x1.shape[0]
    n_pieces = len(w_gu) + len(w_d)

    def body(*refs):
        x1_ref, t_ref, n_ref, sc_ref, sh_ref, g_ref, fn_ref = refs[:7]
        piece_refs = refs[7:7 + n_pieces]
        dx1_ref, h_ref, dgu_ref, act_ref, dmlp_ref, acc_ref, wgu, wd, wsem = refs[7 + n_pieces:]

        @pl.when(pl.program_id(0) == 0)
        def _():
            acc_ref[...] = jnp.zeros_like(acc_ref)
            copies = []
            for dst, pieces in ((wgu, piece_refs[:len(w_gu)]), (wd, piece_refs[len(w_gu):])):
                shard = sum(p.shape[1] for p in pieces)
                off = 0
                for p in pieces:
                    for j in range(N_DEV):
                        copies.append(pltpu.make_async_copy(p.at[j], dst.at[pl.ds(j * shard + off, p.shape[1])],
                                                            wsem.at[len(copies)]))
                    off += p.shape[1]
            for cp in copies:
                cp.start()
            for cp in copies:
                cp.wait()

        x1 = x1_ref[...]
        gate2 = g_ref[...]
        h, vjp_h = jax.vjp(_modnorm, x1, n_ref[...], sc_ref[...], sh_ref[...])
        hb = h.astype(MXU_DTYPE)
        gu = _mm_nt(hb, wgu[...])
        g, u = gu[:, :D_FF], gu[:, D_FF:]
        sg = jax.nn.sigmoid(g)
        silu_g = g * sg
        act = (silu_g * u).astype(MXU_DTYPE)
        mlp = _mm(act, wd[...])
        x2 = x1 + gate2 * mlp
        loss, vjp_loss = jax.vjp(_loss_rows, x2, fn_ref[...], t_ref[...])
        dx2, dfn, _ = vjp_loss(jnp.ones((1, 1), F32))
        dmlp = (dx2 * gate2).astype(MXU_DTYPE)
        dact = _mm_nt(dmlp, wd[...])
        dg = dact * u * (sg * (1.0 + g * (1.0 - sg)))
        du = dact * silu_g
        dgu = jnp.concatenate([dg, du], axis=1).astype(MXU_DTYPE)
        dh = _mm(dgu, wgu[...])
        dx, dn, dsc, dsh = vjp_h(dh)
        dx1_ref[...] = dx2 + dx
        h_ref[...] = hb
        dgu_ref[...] = dgu
        act_ref[...] = act
        dmlp_ref[...] = dmlp
        acc_ref[0:1, :] += dn
        acc_ref[1:2, :] += dsc
        acc_ref[2:3, :] += dsh
        acc_ref[3:4, :] += jnp.sum(dx2 * mlp, axis=0, keepdims=True)
        acc_ref[4:5, :] += dfn
        acc_ref[5:6, :] += jnp.broadcast_to(loss, (1, D_MODEL))

    full = pl.BlockSpec((tm, D_MODEL), _row)
    vec = pl.BlockSpec((1, D_MODEL), _fixed)
    anyspec = pl.BlockSpec(memory_space=pl.ANY)
    return pl.pallas_call(
        body, name="mlp_loss", grid=(S // tm,),
        in_specs=[full, full, vec, vec, vec, vec, vec] + [anyspec] * n_pieces,
        out_specs=[full, full, pl.BlockSpec((tm, 2 * D_FF), _row), pl.BlockSpec((tm, D_FF), _row), full,
                   pl.BlockSpec((8, D_MODEL), _fixed)],
        out_shape=[jax.ShapeDtypeStruct((S, D_MODEL), F32), jax.ShapeDtypeStruct((S, D_MODEL), MXU_DTYPE),
                   jax.ShapeDtypeStruct((S, 2 * D_FF), MXU_DTYPE), jax.ShapeDtypeStruct((S, D_FF), MXU_DTYPE),
                   jax.ShapeDtypeStruct((S, D_MODEL), MXU_DTYPE), jax.ShapeDtypeStruct((8, D_MODEL), F32)],
        scratch_shapes=[pltpu.VMEM((2 * D_FF, D_MODEL), MXU_DTYPE), pltpu.VMEM((D_FF, D_MODEL), MXU_DTYPE),
                        pltpu.SemaphoreType.DMA((N_DEV * n_pieces,))],
        compiler_params=_cparams(VMEM_BIG),
    )(x1, tgt, norm2, scale2, shift2, gate2, fnorm, *w_gu, *w_d)


def _wgrad(a, g, tk, ts, name, xchg=None, g_cols=None):
    pieces = list(a) if isinstance(a, (list, tuple)) else [a]
    S = pieces[0].shape[0]
    K = sum(p.shape[1] for p in pieces)
    assert len(pieces) == 1 or tk == K
    N, col = (g.shape[1], 0) if g_cols is None else g_cols
    ns = S // ts
    n_a = len(pieces)

    def body(*refs):
        a_refs, (g_ref, o_ref, acc_ref) = refs[:n_a], refs[n_a:]
        s = pl.program_id(1)

        @pl.when(s == 0)
        def _():
            acc_ref[...] = jnp.zeros_like(acc_ref)

        a_blk = a_refs[0][...] if n_a == 1 else jnp.concatenate([r[...] for r in a_refs], axis=1)
        acc_ref[...] += _mm_tn(a_blk, g_ref[...])

        @pl.when(s == ns - 1)
        def _():
            o_ref[...] = acc_ref[...].astype(o_ref.dtype)

    if n_a == 1:
        in_specs = [pl.BlockSpec((ts, tk), lambda j, s: (s, j))]
    else:
        in_specs = [pl.BlockSpec((ts, p.shape[1]), lambda j, s: (s, 0)) for p in pieces]
    in_specs.append(pl.BlockSpec((ts, N), lambda j, s: (s, col)))
    out_spec = pl.BlockSpec((tk, N), lambda j, s: (j, 0))
    out_shape = jax.ShapeDtypeStruct((K, N), WIRE_DTYPE)
    scratch = [pltpu.VMEM((tk, N), F32)]
    args = (*pieces, g)
    if xchg is None:
        return pl.pallas_call(body, name=name, grid=(K // tk, ns), in_specs=in_specs, out_specs=out_spec,
                              out_shape=out_shape, scratch_shapes=scratch, compiler_params=_cparams(VMEM_BIG))(*args)
    (out,), x_out = _hosted_call(body, name, (K // tk, ns), in_specs, [out_spec], [out_shape], scratch, args, xchg,
                                 _cparams(VMEM_BIG))
    return out, x_out


SSD_CHUNKS_PER_STEP = 4
SSD_BWD_CHUNKS_PER_STEP = 4
ATTN_BLOCKS_PER_STEP = 4
MASKED = -1e30
QK_SCALE = HALF ** -0.5


def _attn_bias(buckets, rel_bias):
    def body(bk_ref, relb_ref, out_ref):
        bk = bk_ref[...]
        i = lax.broadcasted_iota(jnp.int32, (BLK, 2 * BLK), 0)
        j = lax.broadcasted_iota(jnp.int32, (BLK, 2 * BLK), 1)
        window = (j > i) & (j <= i + BLK)
        for h in range(N_HEADS):
            acc = jnp.zeros((BLK, 2 * BLK), F32)
            for b in range(N_BUCKETS):
                acc = jnp.where(bk == b, relb_ref[b, h], acc)
            out_ref[0, h] = jnp.where(window, acc, MASKED)
            out_ref[1, h] = jnp.where(window & (j >= BLK), acc, MASKED)

    return pl.pallas_call(
        body, name="attn_bias", out_shape=jax.ShapeDtypeStruct((2, N_HEADS, BLK, 2 * BLK), F32),
        in_specs=[pl.BlockSpec(memory_space=pltpu.VMEM), pl.BlockSpec(memory_space=pltpu.SMEM)],
    )(buckets, rel_bias)


def _attn_fwd(qkv, bias, sinks, xchg):
    S = qkv.shape[0]
    nb = S // BLK

    nq = ATTN_BLOCKS_PER_STEP if nb % ATTN_BLOCKS_PER_STEP == 0 else 1
    rows = nq * BLK

    def body(q_ref, kvp_ref, kvc_ref, bias_ref, sinks_ref, y_ref):
        i = pl.program_id(0)
        q = q_ref[...].astype(F32) * QK_SCALE
        kv = jnp.concatenate([kvp_ref[...], kvc_ref[...]], axis=0).astype(F32)
        k_lo, k_hi = _split_pair(kv[:, :LANE])
        v_lo, v_hi = _split_pair(kv[:, LANE:])
        bands = [[t[b * BLK:(b + 2) * BLK].astype(MXU_DTYPE) for t in (k_lo, k_hi, v_lo, v_hi)] for b in range(nq)]
        q_heads = [_split_heads(q[b * BLK:(b + 1) * BLK], 4) for b in range(nq)]
        first = [jnp.where(i == 0, 1, 0) if b == 0 else 0 for b in range(nq)]
        items = [(b, h) for b in range(nq) for h in range(N_HEADS)]
        s = [_mm_nt(q_heads[b][h].astype(MXU_DTYPE), bands[b][h // 4]) + bias_ref[first[b], h] for b, h in items]
        m = [jnp.maximum(jnp.max(s[n], axis=-1, keepdims=True), sinks_ref[h]) for n, (b, h) in enumerate(items)]
        p = [jnp.exp(s[n] - m[n]) for n in range(len(items))]
        rinv = [1.0 / (jnp.sum(p[n], axis=-1, keepdims=True) + jnp.exp(sinks_ref[h] - m[n]))
                for n, (b, h) in enumerate(items)]
        out = [_mm(p[n], bands[b][2 + h // 4]) * rinv[n] for n, (b, h) in enumerate(items)]
        y_ref[...] = jnp.concatenate([_join_heads(out[b * N_HEADS:(b + 1) * N_HEADS]) for b in range(nq)], axis=0)

    smem = pl.BlockSpec(memory_space=pltpu.SMEM)
    return _hosted_call(
        body, "attn_fwd", nb // nq,
        in_specs=[pl.BlockSpec((rows, ATTN_W), _row),
                  pl.BlockSpec((BLK, 2 * KV_W), lambda i: (jnp.maximum(i * nq - 1, 0), 2)),
                  pl.BlockSpec((rows, 2 * KV_W), lambda i: (i, 2)),
                  pl.BlockSpec((2, N_HEADS, BLK, 2 * BLK), lambda i: (0, 0, 0, 0)), smem],
        out_specs=[pl.BlockSpec((rows, ATTN_W), _row)],
        out_shape=[jax.ShapeDtypeStruct((S, ATTN_W), F32)],
        scratch_shapes=[],
        args=(qkv, qkv, qkv, bias, sinks), xchg=xchg, cparams=_cparams(),
    )


def _attn_bwd(qkv, y, dy, bias, sinks, xchg):
    S = qkv.shape[0]
    nb = S // BLK
    nq = ATTN_BLOCKS_PER_STEP if nb % ATTN_BLOCKS_PER_STEP == 0 else 1
    rows, n_steps = nq * BLK, nb // nq

    def body(q_ref, kvp_ref, kvc_ref, y_ref, dy_ref, bias_ref, sinks_ref, dq_ref, dkv_ref, dbias_ref, dsk_ref, carry_ref):
        i = pl.program_id(0)

        @pl.when(i == 0)
        def _():
            dbias_ref[...] = jnp.zeros_like(dbias_ref)
            dsk_ref[...] = jnp.zeros_like(dsk_ref)
            carry_ref[...] = jnp.zeros_like(carry_ref)

        q = q_ref[...].astype(F32) * QK_SCALE
        kv = jnp.concatenate([kvp_ref[...], kvc_ref[...]], axis=0).astype(F32)
        k_lo, k_hi = _split_pair(kv[:, :LANE])
        v_lo, v_hi = _split_pair(kv[:, LANE:])
        bands = [[t[b * BLK:(b + 2) * BLK].astype(MXU_DTYPE) for t in (k_lo, k_hi, v_lo, v_hi)] for b in range(nq)]
        rows_of = lambda ref, b: ref[b * BLK:(b + 1) * BLK, :]
        first = [jnp.where(i == n_steps - 1, 1, 0) if b == 0 else 0 for b in range(nq)]
        items = [(b, h) for b in range(nq) for h in range(N_HEADS)]
        at = lambda b, h: b * N_HEADS + h
        q_heads = [hd for b in range(nq) for hd in _split_heads(q[b * BLK:(b + 1) * BLK], 4)]
        y_heads = [hd for b in range(nq) for hd in _split_heads(rows_of(y_ref, b), 4)]
        dy_heads = [hd for b in range(nq) for hd in _split_heads(rows_of(dy_ref, b), 4)]
        qs = [q_heads[n].astype(MXU_DTYPE) for n in range(len(items))]
        s = [_mm_nt(qs[at(b, h)], bands[b][h // 4]) + bias_ref[first[b], h] for b, h in items]
        m = [jnp.maximum(jnp.max(s[at(b, h)], axis=-1, keepdims=True), sinks_ref[h]) for b, h in items]
        p = [jnp.exp(s[n] - m[n]) for n in range(len(items))]
        esink = [jnp.exp(sinks_ref[h] - m[at(b, h)]) for b, h in items]
        rinv = [1.0 / (jnp.sum(p[n], axis=-1, keepdims=True) + esink[n]) for n in range(len(items))]
        t = [dy_heads[n] * rinv[n] for n in range(len(items))]
        delta = [jnp.sum(t[n] * y_heads[n], axis=-1, keepdims=True) for n in range(len(items))]
        tb = [t[n].astype(MXU_DTYPE) for n in range(len(items))]
        dp = [_mm_nt(tb[at(b, h)], bands[b][2 + h // 4]) for b, h in items]
        ds = [p[n] * (dp[n] - delta[n]) for n in range(len(items))]
        for h in range(N_HEADS):
            ds_h, dsk_h = ds[at(0, h)], esink[at(0, h)] * delta[at(0, h)]
            for b in range(1, nq):
                ds_h = ds_h + ds[at(b, h)]
                dsk_h = dsk_h + esink[at(b, h)] * delta[at(b, h)]
            dbias_ref[h] += ds_h
            dsk_ref[h] -= dsk_h
        dsb = [ds[n].astype(MXU_DTYPE) for n in range(len(items))]
        pb = [p[n].astype(MXU_DTYPE) for n in range(len(items))]
        dq_heads = [_mm(dsb[at(b, h)], bands[b][h // 4]) * QK_SCALE for b, h in items]
        grp = lambda lst, b, g: jnp.concatenate(lst[at(b, 4 * g):at(b, 4 * g) + 4], axis=0)
        dk_pads = [[_mm_tn(grp(dsb, b, g), grp(qs, b, g)) for g in range(2)] for b in range(nq)]
        dv_pads = [[_mm_tn(grp(pb, b, g), grp(tb, b, g)) for g in range(2)] for b in range(nq)]
        dq_ref[...] = jnp.concatenate([_join_heads(dq_heads[b * N_HEADS:(b + 1) * N_HEADS]) for b in range(nq)],
                                      axis=0).astype(dq_ref.dtype)
        part = lambda b, lo: jnp.concatenate(
            [_join_pair(d[b][0][lo:lo + BLK], d[b][1][lo:lo + BLK]) for d in (dk_pads, dv_pads)], axis=1)
        dkv = [part(b, BLK) + (part(b + 1, 0) if b + 1 < nq else carry_ref[...]) for b in range(nq)]
        dkv_ref[...] = jnp.concatenate(dkv, axis=0).astype(dkv_ref.dtype)
        carry_ref[...] = part(0, 0)

    smem = pl.BlockSpec(memory_space=pltpu.SMEM)
    rev = lambda i: (n_steps - 1 - i, 0)
    return _hosted_call(
        body, "attn_bwd", n_steps,
        in_specs=[pl.BlockSpec((rows, ATTN_W), rev),
                  pl.BlockSpec((BLK, 2 * KV_W), lambda i: (jnp.maximum((n_steps - 1 - i) * nq - 1, 0), 2)),
                  pl.BlockSpec((rows, 2 * KV_W), lambda i: (n_steps - 1 - i, 2)),
                  pl.BlockSpec((rows, ATTN_W), rev), pl.BlockSpec((rows, ATTN_W), rev),
                  pl.BlockSpec((2, N_HEADS, BLK, 2 * BLK), lambda i: (0, 0, 0, 0)), smem],
        out_specs=[pl.BlockSpec((rows, ATTN_W), rev), pl.BlockSpec((rows, 2 * KV_W), rev),
                   pl.BlockSpec((N_HEADS, BLK, 2 * BLK), lambda i: (0, 0, 0)),
                   pl.BlockSpec((N_HEADS, BLK, 1), lambda i: (0, 0, 0))],
        out_shape=[jax.ShapeDtypeStruct((S, ATTN_W), MXU_DTYPE), jax.ShapeDtypeStruct((S, 2 * KV_W), MXU_DTYPE),
                   jax.ShapeDtypeStruct((N_HEADS, BLK, 2 * BLK), F32), jax.ShapeDtypeStruct((N_HEADS, BLK, 1), F32)],
        scratch_shapes=[pltpu.VMEM((BLK, 2 * KV_W), F32)],
        args=(qkv, qkv, qkv, y, dy, bias, sinks), xchg=xchg, cparams=_cparams(),
    )


def _attn_finish(dbias, dsk, buckets):
    def body(db_ref, dsk_ref, bk_ref, drel_ref, dsink_ref):
        bk = bk_ref[...]
        r = lax.broadcasted_iota(jnp.int32, (N_BUCKETS, LANE), 0)
        l = lax.broadcasted_iota(jnp.int32, (N_BUCKETS, LANE), 1)
        row = lax.broadcasted_iota(jnp.int32, (N_HEADS, LANE), 0)
        res = jnp.zeros((N_BUCKETS, LANE), F32)
        dsink = jnp.zeros((N_HEADS, LANE), F32)
        for h in range(N_HEADS):
            db = db_ref[h]
            for b in range(N_BUCKETS):
                v = jnp.sum(jnp.sum(jnp.where(bk == b, db, 0.0), axis=1, keepdims=True), axis=0, keepdims=True)
                res = res + jnp.where((r == b) & (l == h), v, 0.0)
            dsink = dsink + jnp.where(row == h, jnp.sum(dsk_ref[h], axis=0, keepdims=True), 0.0)
        drel_ref[...] = res
        dsink_ref[...] = dsink

    return pl.pallas_call(body, name="attn_finish",
                          out_shape=[jax.ShapeDtypeStruct((N_BUCKETS, LANE), F32),
                                     jax.ShapeDtypeStruct((N_HEADS, LANE), F32)])(dbias, dsk, buckets)


def _ssd_consts():
    r = lax.broadcasted_iota(jnp.int32, (BLK, BLK), 0)
    c = lax.broadcasted_iota(jnp.int32, (BLK, BLK), 1)
    causal = c <= r
    upper = (r <= c).astype(F32)
    last = r == BLK - 1
    head = lax.broadcasted_iota(jnp.int32, (N_HEADS, BLK), 0)
    return causal, upper, last, head


def _ssd_chunks(xs, bg, cg, dt_raw_t, prev0, dtb, alog, d_rows, consts):
    causal, upper, last, head = consts
    nq = len(xs)
    items = [(c, h) for c in range(nq) for h in range(N_HEADS)]
    at = lambda c, h: c * N_HEADS + h
    a_neg = -jnp.exp(alog)
    dt_t = [_softplus(dt_raw_t[c] + dtb) for c in range(nq)]
    acs_t = [_mm_hi(dt_t[c] * a_neg, upper) for c in range(nq)]
    cb = [[_mm_nt(cg[c][g], bg[c][g]) for g in range(2)] for c in range(nq)]
    pick = lambda t, h: jnp.sum(jnp.where(head == h, t, 0.0), axis=0, keepdims=True)
    dt_row = [pick(dt_t[c], h) for c, h in items]
    a_row = [pick(acs_t[c], h) for c, h in items]
    a_rb = [jnp.broadcast_to(a_row[n], (BLK, BLK)) for n in range(len(items))]
    a_b = [a_rb[n].T for n in range(len(items))]
    a_last = [jnp.sum(jnp.where(last, a_b[n], 0.0), axis=0, keepdims=True) for n in range(len(items))]
    w = [cb[c][h // 4] * jnp.exp(jnp.where(causal, a_b[at(c, h)] - a_rb[at(c, h)], -1e30)) * dt_row[at(c, h)]
         for c, h in items]
    f_b = [jnp.broadcast_to(dt_row[n] * jnp.exp(a_last[n] - a_row[n]), (BLK, BLK)).T for n in range(len(items))]
    y_in = [_mm(w[at(c, h)], xs[c][h]) for c, h in items]
    st = [_mm_tn(bg[c][h // 4], xs[c][h] * f_b[at(c, h)]) for c, h in items]
    e_b = [jnp.exp(a_b[n]) for n in range(len(items))]
    states = [list(prev0)]
    for c in range(nq):
        states.append([states[c][h] * jnp.exp(a_last[at(c, h)]) + st[at(c, h)] for h in range(N_HEADS)])
    y_off = [_mm(cg[c][h // 4], states[c][h]) * e_b[at(c, h)] for c, h in items]
    ys = [[y_in[at(c, h)] + y_off[at(c, h)] + d_rows[h] * xs[c][h] for h in range(N_HEADS)] for c in range(nq)]
    return ys, states


def _ssd_chunks_bwd(xs, bg, cg, dt_raw_t, prev, dtb, alog, d_rows, dys, dh_last, consts):
    causal, upper, last, head = consts
    nq = len(xs)
    items = [(c, h) for c in range(nq) for h in range(N_HEADS)]
    ni = len(items)
    at = lambda c, h: c * N_HEADS + h
    groups = [(c, g) for c in range(nq) for g in range(2)]
    lane = _lane_iota((BLK, BLK))
    lane_row = _lane_iota((1, BLK))
    a_neg = -jnp.exp(alog)
    pre_dt = [dt_raw_t[c] + dtb for c in range(nq)]
    dt_t = [_softplus(pre_dt[c]) for c in range(nq)]
    acs_t = [_mm_hi(dt_t[c] * a_neg, upper) for c in range(nq)]
    pick = lambda t, h: jnp.sum(jnp.where(head == h, t, 0.0), axis=0, keepdims=True)
    full_sum = lambda t: jnp.sum(jnp.sum(t, axis=1, keepdims=True), axis=0, keepdims=True)
    dt_row = [pick(dt_t[c], h) for c, h in items]
    a_row = [pick(acs_t[c], h) for c, h in items]
    a_rb = [jnp.broadcast_to(a_row[n], (BLK, BLK)) for n in range(ni)]
    a_b = [a_rb[n].T for n in range(ni)]
    a_last = [jnp.sum(jnp.where(last, a_b[n], 0.0), axis=0, keepdims=True) for n in range(ni)]
    lm = [jnp.exp(jnp.where(causal, a_b[n] - a_rb[n], -1e30)) for n in range(ni)]
    cgb = [[cg[c][g].astype(MXU_DTYPE) for g in range(2)] for c in range(nq)]
    bgb = [[bg[c][g].astype(MXU_DTYPE) for g in range(2)] for c in range(nq)]
    cb = [[_mm_nt(cgb[c][g], bgb[c][g]) for g in range(2)] for c in range(nq)]
    u = [cb[c][h // 4] * lm[at(c, h)] for c, h in items]
    w = [(u[n] * dt_row[n]).astype(MXU_DTYPE) for n in range(ni)]
    e_row = [jnp.exp(a_last[n] - a_row[n]) for n in range(ni)]
    f_row = [dt_row[n] * e_row[n] for n in range(ni)]
    f_b = [jnp.broadcast_to(f_row[n], (BLK, BLK)).T for n in range(ni)]
    e_b = [jnp.exp(a_b[n]) for n in range(ni)]
    el = [jnp.exp(a_last[n]) for n in range(ni)]
    xb = [xs[c][h].astype(MXU_DTYPE) for c, h in items]
    dyb = [dys[c][h].astype(MXU_DTYPE) for c, h in items]
    prevb = [prev[c][h].astype(MXU_DTYPE) for c, h in items]
    gmat = [_mm(cgb[c][h // 4], prevb[at(c, h)]) for c, h in items]
    dw = [_mm_nt(dyb[n], xb[n]) for n in range(ni)]
    dg = [dys[c][h] * e_b[at(c, h)] for c, h in items]
    dgb = [dg[n].astype(MXU_DTYPE) for n in range(ni)]
    from_y = [_mm_tn(cgb[c][h // 4], dgb[at(c, h)]) for c, h in items]
    dhs = [None] * ni
    dprev = [None] * ni
    for c in reversed(range(nq)):
        for h in range(N_HEADS):
            dhs[at(c, h)] = dh_last[h] if c == nq - 1 else dprev[at(c + 1, h)]
            dprev[at(c, h)] = from_y[at(c, h)] + dhs[at(c, h)] * el[at(c, h)]
    dstb = [dhs[n].astype(MXU_DTYPE) for n in range(ni)]
    dxf = [_mm(bgb[c][h // 4], dstb[at(c, h)]) for c, h in items]
    xfb = [(xs[c][h] * f_b[at(c, h)]).astype(MXU_DTYPE) for c, h in items]
    dxs = [_mm_tn(w[at(c, h)], dyb[at(c, h)]) + d_rows[h] * dys[c][h] + f_b[at(c, h)] * dxf[at(c, h)]
           for c, h in items]
    dd_item = [jnp.sum(dys[c][h] * xs[c][h], axis=0, keepdims=True) for c, h in items]
    dcg_h = [_mm_nt(dgb[n], prevb[n]) for n in range(ni)]
    dbg_h = [_mm_nt(xfb[n], dstb[n]) for n in range(ni)]
    zt = [dw[n] * u[n] for n in range(ni)]
    dseg = [zt[n] * dt_row[n] for n in range(ni)]
    dcb_h = [dw[n] * lm[n] * dt_row[n] for n in range(ni)]
    four = lambda lst, c, g: lst[at(c, 4 * g)] + lst[at(c, 4 * g + 1)] + lst[at(c, 4 * g + 2)] + lst[at(c, 4 * g + 3)]
    dcb = {(c, g): four(dcb_h, c, g).astype(MXU_DTYPE) for c, g in groups}
    dcg = [[four(dcg_h, c, g) + _mm(dcb[c, g], bgb[c][g]) for g in range(2)] for c in range(nq)]
    dbg = [[four(dbg_h, c, g) + _mm_tn(dcb[c, g], cgb[c][g]) for g in range(2)] for c in range(nq)]
    r1 = [jnp.sum(dg[n] * gmat[n] + dseg[n], axis=1, keepdims=True) for n in range(ni)]
    r2 = [jnp.sum(dxf[at(c, h)] * xs[c][h], axis=1, keepdims=True) for c, h in items]
    tt = [jnp.where(lane < HALF, jnp.broadcast_to(r1[n], (BLK, BLK)), jnp.broadcast_to(r2[n], (BLK, BLK))).T
          for n in range(ni)]
    r1_row = [tt[n][0:1, :] for n in range(ni)]
    r2_row = [tt[n][HALF:HALF + 1, :] for n in range(ni)]
    d_el = [full_sum(dhs[at(c, h)] * prev[c][h]) for c, h in items]
    da_last = [jnp.sum(r2_row[n] * f_row[n], axis=1, keepdims=True) + el[n] * d_el[n] for n in range(ni)]
    da_row = [r1_row[n] - jnp.sum(dseg[n], axis=0, keepdims=True) - r2_row[n] * f_row[n]
              + jnp.where(lane_row == BLK - 1, da_last[n], 0.0) for n in range(ni)]
    ddt_row = [jnp.sum(zt[n], axis=0, keepdims=True) + r2_row[n] * e_row[n] for n in range(ni)]
    draw, dalog = [], jnp.zeros((N_HEADS, BLK), F32)
    for c in range(nq):
        da_t = jnp.zeros((N_HEADS, BLK), F32)
        ddt_t = jnp.zeros((N_HEADS, BLK), F32)
        for h in range(N_HEADS):
            da_t = jnp.where(head == h, da_row[at(c, h)], da_t)
            ddt_t = jnp.where(head == h, ddt_row[at(c, h)], ddt_t)
        d_dta = _mm_hi(da_t, causal.astype(F32))
        dalog = dalog + d_dta * dt_t[c] * a_neg
        draw.append((ddt_t + d_dta * a_neg) * jax.nn.sigmoid(pre_dt[c]))
    ddtb = draw[0]
    for c in range(1, nq):
        ddtb = ddtb + draw[c]
    dd_rows = []
    for h in range(N_HEADS):
        t = dd_item[at(0, h)]
        for c in range(1, nq):
            t = t + dd_item[at(c, h)]
        dd_rows.append(t)
    return ([dxs[c * N_HEADS:(c + 1) * N_HEADS] for c in range(nq)], dbg, dcg, draw,
            [dprev[at(0, h)] for h in range(N_HEADS)], ddtb, dalog, dd_rows)


def _dt_rows(dt_blk):
    return dt_blk.T[:N_HEADS]


def _conv_pre(halo, blk, cw_ref, cb_ref):
    ext = jnp.concatenate([halo, blk], axis=0)
    taps = [pltpu.roll(ext, 3 - k, 0)[8:] for k in range(3)] + [blk]
    pre = cb_ref[...] + cw_ref[0:1, :] * taps[0]
    for k in range(1, 4):
        pre = pre + cw_ref[k:k + 1, :] * taps[k]
    return pre


def _ssd_split(pre):
    heads = _split_heads(pre[:, :SSM_W], 4)
    pb = [pre[:, SSM_W + g * D_STATE:SSM_W + (g + 1) * D_STATE] for g in range(2)]
    pc = [pre[:, SSM_W + 2 * D_STATE + g * D_STATE:SSM_W + 2 * D_STATE + (g + 1) * D_STATE] for g in range(2)]
    return heads, pb, pc


def _ssd_fwd(xbc, dt_raw, conv_w, conv_b, dtb_row, alog_row, d_exp, xchg):
    S = xbc.shape[0]
    nc = S // BLK
    nq = SSD_CHUNKS_PER_STEP if nc % SSD_CHUNKS_PER_STEP == 0 else 1
    rows = nq * BLK

    def body(xbc_ref, halo_ref, dt_ref, cw_ref, cb_ref, dtb_ref, alog_ref, d_ref, y_ref, prev_ref, pre_ref, state_ref):
        i = pl.program_id(0)

        @pl.when(i == 0)
        def _():
            state_ref[...] = jnp.zeros_like(state_ref)

        halo = halo_ref[...] * jnp.where(i > 0, 1.0, 0.0)
        pre = _conv_pre(halo, xbc_ref[...], cw_ref, cb_ref)
        pre_ref[...] = pre
        xc = _silu(pre)
        split = [_ssd_split(xc[c * BLK:(c + 1) * BLK]) for c in range(nq)]
        dt_t = [_dt_rows(dt_ref[c * BLK:(c + 1) * BLK, :]) for c in range(nq)]
        prev0 = [state_ref[h] for h in range(N_HEADS)]
        d_rows = [d_ref[h:h + 1, :] for h in range(N_HEADS)]
        ys, states = _ssd_chunks([s[0] for s in split], [s[1] for s in split], [s[2] for s in split], dt_t, prev0,
                                 dtb_ref[...], alog_ref[...], d_rows, _ssd_consts())
        for h in range(N_HEADS):
            for c in range(nq):
                prev_ref[c, h] = states[c][h]
            state_ref[h] = states[nq][h]
        y_ref[...] = jnp.concatenate([_join_heads(ys[c]) for c in range(nq)], axis=0)

    vec = pl.BlockSpec((N_HEADS, LANE), _fixed)
    return _hosted_call(
        body, "ssd_fwd", nc // nq,
        in_specs=[pl.BlockSpec((rows, XBC_W), _row),
                  pl.BlockSpec((8, XBC_W), lambda i: (jnp.maximum(i * (rows // 8) - 1, 0), 0)),
                  pl.BlockSpec((rows, LANE), _row),
                  pl.BlockSpec((4, XBC_W), _fixed), pl.BlockSpec((1, XBC_W), _fixed), vec, vec,
                  pl.BlockSpec((N_HEADS, LANE), _fixed)],
        out_specs=[pl.BlockSpec((rows, SSM_W), _row),
                   pl.BlockSpec((nq, N_HEADS, D_STATE, LANE), lambda i: (i, 0, 0, 0)),
                   pl.BlockSpec((rows, XBC_W), _row)],
        out_shape=[jax.ShapeDtypeStruct((S, SSM_W), F32), jax.ShapeDtypeStruct((nc, N_HEADS, D_STATE, LANE), F32),
                   jax.ShapeDtypeStruct((S, XBC_W), F32)],
        scratch_shapes=[pltpu.VMEM((N_HEADS, D_STATE, LANE), F32)],
        args=(xbc, xbc, dt_raw, conv_w, conv_b, dtb_row, alog_row, d_exp), xchg=xchg, cparams=_cparams(),
    )


def _ssd_bwd(xbc, pre_act, dt_raw, prev_states, dy, conv_w, dtb_row, alog_row, d_exp, xchg):
    S = xbc.shape[0]
    nc = S // BLK
    nq = SSD_BWD_CHUNKS_PER_STEP if nc % SSD_BWD_CHUNKS_PER_STEP == 0 else 1
    rows, n_steps = nq * BLK, nc // nq

    def body(xbc_ref, halo_ref, pre_ref, dt_ref, prev_ref, dy_ref, cw_ref, dtb_ref, alog_ref, d_ref,
             dxbc_ref, ddt_ref, dcw_ref, dvec_ref, dd_ref, gstate_ref, ghalo_ref):
        i = pl.program_id(0)

        @pl.when(i == 0)
        def _():
            gstate_ref[...] = jnp.zeros_like(gstate_ref)
            ghalo_ref[...] = jnp.zeros_like(ghalo_ref)
            dcw_ref[...] = jnp.zeros_like(dcw_ref)
            dvec_ref[...] = jnp.zeros_like(dvec_ref)
            dd_ref[...] = jnp.zeros_like(dd_ref)

        halo = halo_ref[...] * jnp.where(i < n_steps - 1, 1.0, 0.0)
        ext = jnp.concatenate([halo, xbc_ref[...]], axis=0)
        pre = pre_ref[...]
        sig = jax.nn.sigmoid(pre)
        xc = pre * sig
        split = [_ssd_split(xc[c * BLK:(c + 1) * BLK]) for c in range(nq)]
        dt_t = [_dt_rows(dt_ref[c * BLK:(c + 1) * BLK, :]) for c in range(nq)]
        prev = [[prev_ref[c, h] for h in range(N_HEADS)] for c in range(nq)]
        d_rows = [d_ref[h:h + 1, :] for h in range(N_HEADS)]
        dys = [_split_heads(dy_ref[c * BLK:(c + 1) * BLK, :], 4) for c in range(nq)]
        dh_last = [gstate_ref[h] for h in range(N_HEADS)]
        dheads, dpb, dpc, ddt_t, dprev0, ddtb, dalog, dd_rows = _ssd_chunks_bwd(
            [s[0] for s in split], [s[1] for s in split], [s[2] for s in split], dt_t, prev, dtb_ref[...],
            alog_ref[...], d_rows, dys, dh_last, _ssd_consts())
        for h in range(N_HEADS):
            gstate_ref[h] = dprev0[h]
            dd_ref[h:h + 1, :] += dd_rows[h]
        pad = jnp.zeros((BLK - N_HEADS, BLK), F32)
        ddt_ref[...] = jnp.concatenate([jnp.concatenate([ddt_t[c], pad], axis=0).T for c in range(nq)],
                                       axis=0).astype(ddt_ref.dtype)
        dvec_ref[0:N_HEADS, :] += ddtb
        dvec_ref[N_HEADS:, :] += dalog
        dxc = jnp.concatenate([jnp.concatenate([_join_heads(dheads[c])] + list(dpb[c]) + list(dpc[c]), axis=1)
                               for c in range(nq)], axis=0)
        dpre = dxc * (sig * (1.0 + pre * (1.0 - sig)))
        zeros8 = jnp.zeros((8, XBC_W), F32)
        dpe = jnp.concatenate([zeros8, dpre, zeros8], axis=0)
        n_ext = 16 + rows
        shifted = [pltpu.roll(dpe, n_ext - (3 - k), 0)[:8 + rows] for k in range(3)] + [dpe[:8 + rows]]
        dext = cw_ref[0:1, :] * shifted[0]
        for k in range(1, 4):
            dext = dext + cw_ref[k:k + 1, :] * shifted[k]
        for k in range(4):
            dcw_ref[k:k + 1, :] += jnp.sum(shifted[k] * ext, axis=0, keepdims=True)
        dcw_ref[4:5, :] += jnp.sum(dpre, axis=0, keepdims=True)
        dxbc_ref[...] = jnp.concatenate([dext[8:rows], dext[rows:] + ghalo_ref[...]], axis=0).astype(dxbc_ref.dtype)
        ghalo_ref[...] = dext[:8, :]

    vec = pl.BlockSpec((N_HEADS, LANE), _fixed)
    rev = lambda i: (n_steps - 1 - i, 0)
    return _hosted_call(
        body, "ssd_bwd", n_steps,
        in_specs=[pl.BlockSpec((rows, XBC_W), rev),
                  pl.BlockSpec((8, XBC_W), lambda i: (jnp.maximum((n_steps - 1 - i) * (rows // 8) - 1, 0), 0)),
                  pl.BlockSpec((rows, XBC_W), rev),
                  pl.BlockSpec((rows, LANE), rev),
                  pl.BlockSpec((nq, N_HEADS, D_STATE, LANE), lambda i: (n_steps - 1 - i, 0, 0, 0)),
                  pl.BlockSpec((rows, SSM_W), rev),
                  pl.BlockSpec((4, XBC_W), _fixed), vec, vec,
                  pl.BlockSpec((N_HEADS, LANE), _fixed)],
        out_specs=[pl.BlockSpec((rows, XBC_W), rev), pl.BlockSpec((rows, LANE), rev),
                   pl.BlockSpec((8, XBC_W), _fixed), pl.BlockSpec((2 * N_HEADS, LANE), _fixed),
                   pl.BlockSpec((N_HEADS, LANE), _fixed)],
        out_shape=[jax.ShapeDtypeStruct((S, XBC_W), MXU_DTYPE), jax.ShapeDtypeStruct((S, LANE), MXU_DTYPE),
                   jax.ShapeDtypeStruct((8, XBC_W), F32), jax.ShapeDtypeStruct((2 * N_HEADS, LANE), F32),
                   jax.ShapeDtypeStruct((N_HEADS, LANE), F32)],
        scratch_shapes=[pltpu.VMEM((N_HEADS, D_STATE, LANE), F32), pltpu.VMEM((8, XBC_W), F32)],
        args=(xbc, xbc, pre_act, dt_raw, prev_states, dy, conv_w, dtb_row, alog_row, d_exp), xchg=xchg,
        cparams=_cparams(VMEM_BIG),
    )


def _adamw_math(w, g, m, v):
    m = ADAM_B1 * m + (1.0 - ADAM_B1) * g
    v = ADAM_B2 * v + (1.0 - ADAM_B2) * jnp.square(g)
    m_hat = m / (1.0 - ADAM_B1 ** ADAM_STEP)
    v_hat = v / (1.0 - ADAM_B2 ** ADAM_STEP)
    delta = -ADAM_LR * (m_hat / (jnp.sqrt(v_hat) + ADAM_EPS) + ADAM_WD * w)
    return delta, m, v


def _reduce_adamw_halves(part_a, part_b, w, m, v, name):
    R, C = w.shape
    P = part_a.shape[0]
    tl = 256
    n = C // tl

    def body(a_ref, b_ref, w_ref, m_ref, v_ref, g_ref, d_ref, nm_ref, nv_ref):
        ga, gb = a_ref[0].astype(F32), b_ref[0].astype(F32)
        for i in range(1, P):
            ga, gb = ga + a_ref[i].astype(F32), gb + b_ref[i].astype(F32)
        first = jnp.where(pl.program_id(0) < n // 2, 1.0, 0.0)
        g = ga * first + gb * (1.0 - first)
        d, nm, nv = _adamw_math(w_ref[...], g, m_ref[...], v_ref[...])
        g_ref[...] = g
        d_ref[...] = d
        nm_ref[...] = nm
        nv_ref[...] = nv

    blk = pl.BlockSpec((R, tl), lambda i: (0, i))
    return pl.pallas_call(
        body, name=name, grid=(n,),
        in_specs=[pl.BlockSpec((P, R, tl), lambda i: (0, 0, jnp.minimum(i, n // 2 - 1))),
                  pl.BlockSpec((P, R, tl), lambda i: (0, 0, jnp.maximum(i - n // 2, 0))), blk, blk, blk],
        out_specs=[blk] * 4, out_shape=[jax.ShapeDtypeStruct((R, C), F32)] * 4,
    )(part_a, part_b, w, m, v)


def _reduce_adamw_hosting(parts_list, wmv_list, name, xchg):
    n_arr = len(parts_list)
    pieces = [list(p) if isinstance(p, (tuple, list)) else [p] for p in parts_list]
    n_pieces = sum(len(p) for p in pieces)
    C = wmv_list[0][0].shape[1]
    tl = 256

    def total(ref):
        g = ref[0].astype(F32)
        for i in range(1, N_DEV):
            g = g + ref[i].astype(F32)
        return g

    def body(*refs):
        p_refs, wmv_refs, o_refs = refs[:n_pieces], refs[n_pieces:n_pieces + 3 * n_arr], refs[n_pieces + 3 * n_arr:]
        at = 0
        for k in range(n_arr):
            sums = [total(r) for r in p_refs[at:at + len(pieces[k])]]
            at += len(pieces[k])
            g = sums[0] if len(sums) == 1 else jnp.concatenate(sums, axis=0)
            w_ref, m_ref, v_ref = wmv_refs[3 * k:3 * k + 3]
            d, nm, nv = _adamw_math(w_ref[...], g, m_ref[...], v_ref[...])
            for o, val in zip(o_refs[4 * k:4 * k + 4], (g, d, nm, nv)):
                o[...] = val

    in_specs = [pl.BlockSpec((N_DEV, p.shape[1], tl), lambda i: (0, 0, i)) for group in pieces for p in group]
    in_specs += [pl.BlockSpec((w.shape[0], tl), lambda i: (0, i)) for w, _, _ in wmv_list for _ in range(3)]
    out_specs = [pl.BlockSpec((w.shape[0], tl), lambda i: (0, i)) for w, _, _ in wmv_list for _ in range(4)]
    out_shape = [jax.ShapeDtypeStruct(w.shape, F32) for w, _, _ in wmv_list for _ in range(4)]
    args = [p for group in pieces for p in group] + [a for wmv in wmv_list for a in wmv]
    outs, x_out = _hosted_call(body, name, C // tl, in_specs, out_specs, out_shape, [], args, xchg,
                               _cparams(VMEM_BIG))
    return [outs[4 * k:4 * k + 4] for k in range(n_arr)], x_out


_SMALL_NAMES = ("ada_b", "norm1", "conv_w", "conv_b", "dt_bias", "A_log", "D_skip", "sinks", "attn_out_norm",
                "ssm_out_norm", "norm2", "rel_bias", "final_norm")
N_MOD = 6 * D_MODEL


def _mod_row(a0, a1, a2):
    return jnp.concatenate([a0[2:3], a0[1:2], a1[0:1], a2[2:3], a2[1:2], a2[3:4]], axis=1)


def _small_update(gathered, params):
    n_g = len(gathered)
    flat = [a for name in _SMALL_NAMES for a in params[name]]

    def body(*refs):
        a0_ref, a1_ref, a2_ref, cw_ref, dv_ref, dd_ref, ds_ref, dr_ref, c_ref = refs[:n_g]
        wmv = refs[n_g:n_g + len(flat)]
        outs = refs[n_g + len(flat):]

        def total(ref):
            t = ref[0]
            for i in range(1, N_DEV):
                t = t + ref[i]
            return t

        t0, t1, t2, tcw, tdv, tdd, tds, tdr = [total(r) for r in (a0_ref, a1_ref, a2_ref, cw_ref, dv_ref, dd_ref,
                                                                   ds_ref, dr_ref)]
        r8 = lax.broadcasted_iota(jnp.int32, (N_HEADS, LANE), 0)
        l8 = lax.broadcasted_iota(jnp.int32, (N_HEADS, LANE), 1)

        def diag_row(t):
            return jnp.sum(jnp.where(r8 == l8, t, 0.0), axis=0, keepdims=True)[:, :N_HEADS]

        def lane_sums(t):
            return diag_row(jnp.broadcast_to(jnp.sum(t, axis=1, keepdims=True), (N_HEADS, LANE)))

        me = _lin(_my_pos())
        n_cw = XBC_W // N_DEV
        cw_mine = jnp.zeros((4, n_cw), F32)
        for j in range(N_DEV):
            cw_mine = cw_mine + tcw[0:4, j * n_cw:(j + 1) * n_cw] * jnp.where(me == j, 1.0, 0.0)
        grads = {
            "ada_b": _mod_row(t0, t1, t2), "norm1": t0[0:1], "conv_w": cw_mine, "conv_b": tcw[4:5],
            "dt_bias": lane_sums(tdv[:N_HEADS]), "A_log": lane_sums(tdv[N_HEADS:]), "D_skip": lane_sums(tdd),
            "sinks": diag_row(tds), "attn_out_norm": t1[1:2, :ATTN_W], "ssm_out_norm": t1[1:2, ATTN_W:],
            "norm2": t2[0:1], "rel_bias": tdr[:, :N_HEADS], "final_norm": t2[4:5],
        }
        for k, name in enumerate(_SMALL_NAMES):
            w_ref, m_ref, v_ref = wmv[3 * k:3 * k + 3]
            g = grads[name]
            d, nm, nv = _adamw_math(w_ref[...], g, m_ref[...], v_ref[...])
            for o, val in zip(outs[4 * k:4 * k + 4], (g, d, nm, nv)):
                o[...] = val
        loss_ref, call_ref, dmod_ref = outs[4 * len(_SMALL_NAMES):]
        loss_ref[...] = t2[5:6, 0:1]
        call_ref[...] = jnp.concatenate([c_ref[i] for i in range(N_DEV)], axis=0)
        dmod_ref[...] = jnp.concatenate([_mod_row(a0_ref[i], a1_ref[i], a2_ref[i]) for i in range(N_DEV)], axis=0)

    out_shape = [jax.ShapeDtypeStruct(params[name][0].shape, F32) for name in _SMALL_NAMES for _ in range(4)]
    out_shape += [jax.ShapeDtypeStruct((1, 1), F32), jax.ShapeDtypeStruct((N_DEV, D_MODEL), F32),
                  jax.ShapeDtypeStruct((N_DEV, N_MOD), F32)]
    res = pl.pallas_call(body, name="small_update", out_shape=out_shape)(*gathered, *flat)
    upd = {name: res[4 * k:4 * k + 4] for k, name in enumerate(_SMALL_NAMES)}
    loss, c_all, dmod_all = res[4 * len(_SMALL_NAMES):]
    return upd, loss, c_all, dmod_all


def _ada_w_update(c_all, dmod_all, w, m, v):
    chunk = w.shape[1]

    def body(c_ref, dm_ref, w_ref, m_ref, v_ref, g_ref, d_ref, nm_ref, nv_ref):
        me = _lin(_my_pos())
        dm = jnp.zeros((N_DEV, chunk), F32)
        for j in range(N_DEV):
            dm = dm + dm_ref[:, j * chunk:(j + 1) * chunk] * jnp.where(me == j, 1.0, 0.0)
        g = lax.dot_general(_silu(c_ref[...]), dm, (((0,), (0,)), ((), ())), precision=HI,
                            preferred_element_type=F32)
        d, nm, nv = _adamw_math(w_ref[...], g, m_ref[...], v_ref[...])
        g_ref[...] = g
        d_ref[...] = d
        nm_ref[...] = nm
        nv_ref[...] = nv

    tr = 256
    blk = pl.BlockSpec((tr, chunk), _row)
    return pl.pallas_call(
        body, name="ada_w_update", grid=(w.shape[0] // tr,),
        in_specs=[pl.BlockSpec((N_DEV, tr), lambda i: (0, i)), pl.BlockSpec(dmod_all.shape, _fixed), blk, blk, blk],
        out_specs=[blk] * 4, out_shape=[jax.ShapeDtypeStruct(w.shape, F32)] * 4,
    )(c_all, dmod_all, w, m, v)


def _local_step(x, tgt, c, mod, w_in, conv_w, w_o_mine, w_gu_mine, w_d_mine, p):
    S = x.shape[0]
    tm = min(512, S)
    tmm = min(256, S)
    tw = min(2048, S)
    shift1, scale1, gate1, shift2, scale2, gate2 = [mod[i:i + 1] for i in range(6)]
    buckets = jnp.asarray(_t5_bucket_table())
    per_head = lambda a: jnp.broadcast_to(a.reshape(N_HEADS, 1), (N_HEADS, LANE))
    dtb_row, alog_row, d_exp = per_head(p["dt_bias"]), per_head(p["A_log"]), per_head(p["D_skip"])
    sinks = p["sinks"].reshape(N_HEADS)

    d_cut, gu_cut = WD_CUT, WGU_CUTS
    n_d, n_gu = w_d_mine.shape[0], w_gu_mine.shape[0]
    (qkv, z, xbc, dt_raw), (g_d_a,) = _in_proj_fwd(x, p["norm1"], scale1, shift1, w_in, tm,
                                                   ([(w_d_mine, 0, d_cut)], "two-level"))
    bias = _attn_bias(buckets, p["rel_bias"])
    (ya,), (g_gu_a,) = _attn_fwd(qkv, bias, sinks, ([(w_gu_mine, 0, gu_cut[0])], "two-level"))
    (ys, prev_states, pre_act), (g_gu_b, g_o) = _ssd_fwd(
        xbc, dt_raw, conv_w, p["conv_b"], dtb_row, alog_row, d_exp,
        ([(w_gu_mine, gu_cut[0], gu_cut[1] - gu_cut[0]), w_o_mine], "two-level"))
    w_o = g_o.reshape(D_MODEL, D_MODEL)
    x1, (g_gu_c, g_d_b) = _out_proj_fwd(
        x, ya, ys, z, p["attn_out_norm"], p["ssm_out_norm"], gate1, w_o, tm,
        ([(w_gu_mine, gu_cut[1], n_gu - gu_cut[1]), (w_d_mine, d_cut, n_d - d_cut)], "two-level"))
    dx1, h2, dgu, act, dmlp, acc2 = _mlp_loss(x1, tgt, p["norm2"], scale2, shift2, gate2, p["final_norm"],
                                              (g_gu_a, g_gu_b, g_gu_c), (g_d_a, g_d_b), tmm)
    g_w_gu = _wgrad(dgu, h2, 2 * D_FF // 4, tw, "wgrad_gate_up")
    g_w_d = _wgrad(act, dmlp, D_FF // 2, tw, "wgrad_down")
    gu_slots = g_w_gu.reshape(N_DEV, 2 * D_FF // N_DEV, D_MODEL)
    (dya, dys, dz, g_w_o, acc1), (r_gu_a,) = _out_proj_bwd(
        dx1, ya, ys, z, p["attn_out_norm"], p["ssm_out_norm"], gate1, w_o, tm, ([gu_slots], ("rows", 0, GGU_CUT)))
    (dq, dkv, dbias, dsk), (r_d, r_o) = _attn_bwd(
        qkv, ya, dya, bias, sinks,
        ([g_w_d.reshape(N_DEV, D_FF // N_DEV, D_MODEL), g_w_o.reshape(N_DEV, D_MODEL // N_DEV, D_MODEL)], True))
    drel, dsink = _attn_finish(dbias, dsk, buckets)
    (dxbc, ddt, dcw, dvec, dd), (r_gu_b, *early) = _ssd_bwd(
        xbc, pre_act, dt_raw, prev_states, dys, conv_w, dtb_row, alog_row, d_exp,
        [([gu_slots], ("rows", GGU_CUT, 2 * D_FF // N_DEV - GGU_CUT)), ([acc1, acc2, dsink, drel, c], False)])
    r_gu = (r_gu_a, r_gu_b)
    gx, h1, acc0, g_in_a = _in_proj_bwd(x, dx1, dq, dkv, dz, dxbc, ddt, p["norm1"], scale1, shift1, w_in, tm)
    half = D_MODEL // 2
    slots = lambda g: g[:IN_W].reshape(N_DEV, IN_W // N_DEV, half)
    g_in_b, (r_in_a,) = _wgrad((dq, dkv, dz, dxbc, ddt), h1, IN_PAD, tw, "wgrad_in_b",
                               ([slots(g_in_a)], "two-level scatter"), g_cols=(half, 1))
    return gx, (r_in_a, slots(g_in_b)), (r_o, r_gu, r_d), early, (acc0, dcw, dvec, dd)


def kernel(x, c, ada_w, ada_b, norm1, w_in, conv_w, conv_b, dt_bias, A_log, D_skip, sinks, attn_out_norm, ssm_out_norm, w_o, norm2, w_gate_up, w_down, rel_bias, final_norm, loss_target, m_ada_w, m_ada_b, m_norm1, m_w_in, m_conv_w, m_conv_b, m_dt_bias, m_A_log, m_D_skip, m_sinks, m_attn_out_norm, m_ssm_out_norm, m_w_o, m_norm2, m_w_gate_up, m_w_down, m_rel_bias, m_final_norm, v_ada_w, v_ada_b, v_norm1, v_w_in, v_conv_w, v_conv_b, v_dt_bias, v_A_log, v_D_skip, v_sinks, v_attn_out_norm, v_ssm_out_norm, v_w_o, v_norm2, v_w_gate_up, v_w_down, v_rel_bias, v_final_norm):
    two_d = lambda a: a if a.ndim == 2 else a.reshape(-1, a.shape[-1])
    small_params = dict(
        ada_b=(ada_b, m_ada_b, v_ada_b), norm1=(norm1, m_norm1, v_norm1), conv_w=(conv_w, m_conv_w, v_conv_w),
        conv_b=(conv_b, m_conv_b, v_conv_b), dt_bias=(dt_bias, m_dt_bias, v_dt_bias), A_log=(A_log, m_A_log, v_A_log),
        D_skip=(D_skip, m_D_skip, v_D_skip), sinks=(sinks, m_sinks, v_sinks),
        attn_out_norm=(attn_out_norm, m_attn_out_norm, v_attn_out_norm),
        ssm_out_norm=(ssm_out_norm, m_ssm_out_norm, v_ssm_out_norm), norm2=(norm2, m_norm2, v_norm2),
        rel_bias=(rel_bias, m_rel_bias, v_rel_bias), final_norm=(final_norm, m_final_norm, v_final_norm))
    small_params = {k: tuple(two_d(a) for a in v) for k, v in small_params.items()}
    S = x.shape[1]
    xs, tgt = x.reshape(S, D_MODEL), loss_target.reshape(S, D_MODEL)
    ada_w2 = ada_w[0]
    chunk = ada_w2.shape[1]
    t_in = [jnp.transpose(a[0]) for a in (w_in, m_w_in, v_w_in)]
    t_gu = [jnp.transpose(a[0]) for a in (w_gate_up, m_w_gate_up, v_w_gate_up)]

    mod, (g_in, g_cw) = _mod_and_gather(c, ada_w2, ada_b.reshape(N_DEV, chunk), [t_in[0].astype(WIRE_DTYPE), conv_w[0]])
    mod = mod.reshape(6, D_MODEL)
    w_in_full = jnp.pad(g_in.reshape(IN_W, D_MODEL), ((0, IN_PAD - IN_W), (0, 0)))
    conv_w_full = jnp.transpose(g_cw, (1, 0, 2)).reshape(4, XBC_W)

    p = {k: v[0] for k, v in small_params.items()}
    gx, (r_in_a, gw_in_b), (r_o, r_gu, r_d), early, late_blocks = _local_step(
        xs, tgt, c, mod, w_in_full, conv_w_full, w_o[0].astype(WIRE_DTYPE), t_gu[0].astype(WIRE_DTYPE),
        w_down[0].astype(WIRE_DTYPE), p)

    (u_gu, u_d, u_o), (r_in_b, *late) = _reduce_adamw_hosting(
        [r_gu, r_d, r_o], [tuple(t_gu), (w_down[0], m_w_down[0], v_w_down[0]), (w_o[0], m_w_o[0], v_w_o[0])],
        "adamw_big", [([gw_in_b], "two-level scatter"), (list(late_blocks), False)])
    gathered = (late[0], early[0], early[1], late[1], late[2], late[3], early[2], early[3], early[4])

    small, loss, c_all, dmod_all = _small_update(gathered, small_params)

    big = {
        "ada_w": _ada_w_update(c_all, dmod_all, ada_w2, m_ada_w[0], v_ada_w[0]),
        "w_in": [jnp.transpose(a) for a in _reduce_adamw_halves(r_in_a, r_in_b, *t_in, "adamw_w_in")],
        "w_o": u_o,
        "w_gate_up": [jnp.transpose(a) for a in u_gu],
        "w_down": u_d,
    }
    big.update(small)

    order = ['ada_w', 'ada_b', 'norm1', 'w_in', 'conv_w', 'conv_b', 'dt_bias', 'A_log', 'D_skip', 'sinks',
             'attn_out_norm', 'ssm_out_norm', 'w_o', 'norm2', 'w_gate_up', 'w_down', 'rel_bias', 'final_norm']
    shapes = dict(ada_w=ada_w.shape, ada_b=ada_b.shape, norm1=norm1.shape, w_in=w_in.shape, conv_w=conv_w.shape,
                  conv_b=conv_b.shape, dt_bias=dt_bias.shape, A_log=A_log.shape, D_skip=D_skip.shape,
                  sinks=sinks.shape, attn_out_norm=attn_out_norm.shape, ssm_out_norm=ssm_out_norm.shape,
                  w_o=w_o.shape, norm2=norm2.shape, w_gate_up=w_gate_up.shape, w_down=w_down.shape,
                  rel_bias=rel_bias.shape, final_norm=final_norm.shape)
    outs = [[], [], [], []]
    for name in order:
        for kind in range(4):
            outs[kind].append(big[name][kind].reshape(shapes[name]))
    return (loss.reshape(()), gx.reshape(x.shape), *outs[0], *outs[1], *outs[2], *outs[3])
```

```python
import numpy as np
import jax
import jax.numpy as jnp
from jax import lax
from jax.experimental import pallas as pl
from jax.experimental.pallas import tpu as pltpu

F32 = jnp.float32
MXU_DTYPE = jnp.bfloat16
WIRE_DTYPE = jnp.bfloat16
HI = lax.Precision.HIGHEST
MESH = pl.DeviceIdType.MESH
N_DEV = 8

D_MODEL = 1024
ATTN_W = 512
KV_W = 128
SSM_W = 512
XBC_W = 1024
N_HEADS = 8
D_STATE = 128
D_FF = 2816
IN_W = 2312
IN_PAD = 2432
BLK = 128
N_BUCKETS = 32
EPS = 1e-6
LANE = 128
HALF = 64

ADAM_LR, ADAM_B1, ADAM_B2, ADAM_EPS, ADAM_WD, ADAM_STEP = 0.001, 0.9, 0.999, 1e-08, 0.01, 10

VMEM_BIG = 56 * 1024 * 1024
WD_CUT = 288
WGU_CUTS = (240, 496)
GGU_CUT = 304


def _cparams(vmem=None):
    if vmem is None:
        return pltpu.CompilerParams()
    return pltpu.CompilerParams(vmem_limit_bytes=vmem)


def _mm(a, b):
    return jnp.dot(a.astype(MXU_DTYPE), b.astype(MXU_DTYPE), preferred_element_type=F32)


def _mm_nt(a, b):
    return lax.dot_general(a.astype(MXU_DTYPE), b.astype(MXU_DTYPE), (((1,), (1,)), ((), ())),
                           preferred_element_type=F32)


def _mm_tn(a, b):
    return lax.dot_general(a.astype(MXU_DTYPE), b.astype(MXU_DTYPE), (((0,), (0,)), ((), ())),
                           preferred_element_type=F32)


def _mm_hi(a, b):
    return jnp.dot(a, b, precision=HI, preferred_element_type=F32)


def _silu(x):
    return x * jax.nn.sigmoid(x)


def _softplus(x):
    return jnp.maximum(x, 0.0) + jnp.log1p(jnp.exp(-jnp.abs(x)))


def _rms(x, g, n):
    return x * lax.rsqrt(jnp.sum(x * x, axis=-1, keepdims=True) * (1.0 / n) + EPS) * g


def _modnorm(x, g, scale, shift):
    return _rms(x, g, x.shape[-1]) * (1.0 + scale) + shift


def _modnorm_parts(x):
    r = lax.rsqrt(jnp.sum(x * x, axis=-1, keepdims=True) * (1.0 / x.shape[-1]) + EPS)
    return r, x * r


def _modnorm_bwd(r, xhat, g, scale, dy):
    dyg = dy * (g * (1.0 + scale))
    c = jnp.sum(dyg * xhat, axis=-1, keepdims=True) * (1.0 / xhat.shape[-1])
    dx = r * (dyg - xhat * c)
    ct = jnp.sum(dy * xhat, axis=0, keepdims=True)
    return dx, ct * (1.0 + scale), ct * g, jnp.sum(dy, axis=0, keepdims=True)


def _lane_iota(shape):
    return lax.broadcasted_iota(jnp.int32, shape, len(shape) - 1)


def _split_pair(t):
    lane = _lane_iota(t.shape)
    lo = jnp.where(lane < HALF, t, 0.0)
    hi = pltpu.roll(jnp.where(lane >= HALF, t, 0.0), HALF, 1)
    return lo, hi


def _join_pair(lo, hi):
    lane = _lane_iota(lo.shape)
    return jnp.where(lane < HALF, lo, pltpu.roll(hi, HALF, 1))


def _split_heads(t, n_pairs):
    out = []
    for p in range(n_pairs):
        out.extend(_split_pair(t[:, p * LANE:(p + 1) * LANE]))
    return out


def _join_heads(hs):
    return jnp.concatenate([_join_pair(hs[2 * p], hs[2 * p + 1]) for p in range(len(hs) // 2)], axis=1)


def _t5_bucket_table():
    dist = np.arange(BLK)[:, None] + BLK - np.arange(2 * BLK)[None, :]
    n = np.maximum(dist, 0)
    max_exact = N_BUCKETS // 2
    large = max_exact + (np.log(np.maximum(n, 1) / max_exact) / np.log(128 / max_exact)
                         * (N_BUCKETS - max_exact)).astype(np.int32)
    large = np.minimum(large, N_BUCKETS - 1)
    return np.where(n < max_exact, n, large).astype(np.int32)


def _my_pos():
    return lax.axis_index("x"), lax.axis_index("y"), lax.axis_index("c")


def _peer(k):
    x, y, c = _my_pos()
    return (1 - x if k & 4 else x, 1 - y if k & 2 else y, 1 - c if k & 1 else c)


def _lin(pos):
    return 4 * pos[0] + 2 * pos[1] + pos[2]


def _xchg_copies(ins, outs, sems, scatter):
    local_sem, send_sem, recv_sem = sems
    me = _lin(_my_pos())

    def source(a, slot):
        if not scatter:
            return ins[a]
        if scatter is True:
            return ins[a].at[slot]
        return ins[a].at[slot, pl.ds(scatter[1], scatter[2])]

    local, remote = [], []
    for a in range(len(ins)):
        local.append(pltpu.make_async_copy(source(a, me), outs[a].at[me], local_sem.at[a]))
    for k in range(1, N_DEV):
        peer = _peer(k)
        for a in range(len(ins)):
            remote.append(pltpu.make_async_remote_copy(source(a, _lin(peer)), outs[a].at[me], send_sem.at[a, k - 1],
                                                       recv_sem.at[a, k - 1], device_id=peer, device_id_type=MESH))
    return local, remote


def _xchg_start(ins, outs, sems, scatter):
    local, remote = _xchg_copies(ins, outs, sems, scatter)
    for cp in local + remote:
        cp.start()


def _xchg_wait(ins, outs, sems, scatter):
    local, remote = _xchg_copies(ins, outs, sems, scatter)
    for cp in local:
        cp.wait()
    for cp in remote:
        cp.wait_send()
        cp.wait_recv()


def _xchg_shapes(arrs, scatter):
    n = len(arrs)
    if isinstance(scatter, tuple):
        out_shape = [jax.ShapeDtypeStruct((a.shape[0], scatter[2]) + a.shape[2:], a.dtype) for a in arrs]
    elif scatter:
        out_shape = [jax.ShapeDtypeStruct(a.shape, a.dtype) for a in arrs]
    else:
        out_shape = [jax.ShapeDtypeStruct((N_DEV,) + a.shape, a.dtype) for a in arrs]
    sems = [pltpu.SemaphoreType.DMA((n,)), pltpu.SemaphoreType.DMA((n, N_DEV - 1)),
            pltpu.SemaphoreType.DMA((n, N_DEV - 1))]
    return out_shape, sems


_CHIPS = (2, 4, 6)


def _g2_sems(n):
    dma = pltpu.SemaphoreType.DMA
    return [dma((n,)), dma((n, N_DEV)), dma((n, N_DEV)), dma((n, len(_CHIPS))), dma((n, len(_CHIPS)))]


class _TwoLevelGather:
    def __init__(self, ins, outs, sems, windows=None):
        self.ins, self.outs = ins, outs
        self.local_sem, self.send_sem, self.recv_sem, self.fsend_sem, self.frecv_sem = sems
        self.n = len(ins)
        self.windows = windows or [None] * self.n

    def _mine(self, a):
        w = self.windows[a]
        return self.ins[a] if w is None else self.ins[a].at[pl.ds(w[0], w[1])]

    def _direct(self, a, k):
        return pltpu.make_async_remote_copy(self._mine(a), self.outs[a].at[_lin(_my_pos())], self.send_sem.at[a, k],
                                            self.recv_sem.at[a, k], device_id=_peer(k), device_id_type=MESH)

    def _handed_on(self, a, j, origin):
        slot = self.outs[a].at[origin]
        return pltpu.make_async_remote_copy(slot, slot, self.fsend_sem.at[a, j], self.frecv_sem.at[a, j],
                                            device_id=_peer(1), device_id_type=MESH)

    def _local(self, a):
        return pltpu.make_async_copy(self._mine(a), self.outs[a].at[_lin(_my_pos())], self.local_sem.at[a])

    def start(self):
        for a in range(self.n):
            self._local(a).start()
        for k in (1,) + _CHIPS:
            for a in range(self.n):
                self._direct(a, k).start()

    def forward(self):
        for j, k in enumerate(_CHIPS):
            for a in range(self.n):
                self._direct(a, k).wait_recv()
                self._handed_on(a, j, _lin(_peer(k))).start()

    def finish(self):
        for a in range(self.n):
            self._direct(a, 1).wait_recv()
            for j, k in enumerate(_CHIPS):
                self._handed_on(a, j, _lin(_peer(k ^ 1))).wait_recv()
            self._local(a).wait()
            for k in (1,) + _CHIPS:
                self._direct(a, k).wait_send()
            for j, k in enumerate(_CHIPS):
                self._handed_on(a, j, _lin(_peer(k))).wait_send()


N_SCATTERED = 2 + len(_CHIPS)


class _TwoLevelScatter:
    def __init__(self, src, out, stage, buf_a, buf_b, sems):
        self.src, self.out, self.stage, self.buf_a, self.buf_b = src, out, stage, buf_a, buf_b
        self.local_sem, self.dsend, self.drecv, self.csend, self.crecv = sems

    def _own(self):
        return pltpu.make_async_copy(self.src.at[_lin(_my_pos())], self.out.at[0], self.local_sem.at[0])

    def _to_core(self, j):
        q = 0 if j == 0 else _CHIPS[j - 1]
        dst = self.out.at[1] if j == 0 else self.stage.at[j - 1]
        return pltpu.make_async_remote_copy(self.src.at[_lin(_peer(q ^ 1))], dst, self.dsend.at[j], self.drecv.at[j],
                                            device_id=_peer(1), device_id_type=MESH)

    def _loads(self, j):
        mine = self.src.at[_lin(_peer(_CHIPS[j]))]
        return (pltpu.make_async_copy(self.stage.at[j], self.buf_a.at[j], self.local_sem.at[1 + 2 * j]),
                pltpu.make_async_copy(mine, self.buf_b.at[j], self.local_sem.at[2 + 2 * j]))

    def _store(self, j):
        return pltpu.make_async_copy(self.buf_a.at[j], self.stage.at[j], self.local_sem.at[1 + 2 * len(_CHIPS) + j])

    def _to_chip(self, j):
        return pltpu.make_async_remote_copy(self.stage.at[j], self.out.at[2 + j], self.csend.at[j], self.crecv.at[j],
                                            device_id=_peer(_CHIPS[j]), device_id_type=MESH)

    def start(self):
        self._own().start()
        for j in range(1 + len(_CHIPS)):
            self._to_core(j).start()

    def forward(self):
        for j in range(len(_CHIPS)):
            self._to_core(j + 1).wait_recv()
            for cp in self._loads(j):
                cp.start()
        for j in range(len(_CHIPS)):
            for cp in self._loads(j):
                cp.wait()
            self.buf_a[j] = (self.buf_a[j].astype(F32) + self.buf_b[j].astype(F32)).astype(self.buf_a.dtype)
            self._store(j).start()
        for j in range(len(_CHIPS)):
            self._store(j).wait()
            self._to_chip(j).start()

    def finish(self):
        self._own().wait()
        self._to_core(0).wait_recv()
        for j in range(1 + len(_CHIPS)):
            self._to_core(j).wait_send()
        for j in range(len(_CHIPS)):
            self._to_chip(j).wait_send()
            self._to_chip(j).wait_recv()


def _s2_shapes(a):
    dma = pltpu.SemaphoreType.DMA
    n_c = len(_CHIPS)
    piece = a.shape[1:]
    return (jax.ShapeDtypeStruct((N_SCATTERED,) + piece, a.dtype), jax.ShapeDtypeStruct((n_c,) + piece, a.dtype),
            [pltpu.VMEM((n_c,) + piece, a.dtype)] * 2,
            [dma((1 + 3 * n_c,)), dma((1 + n_c,)), dma((1 + n_c,)), dma((n_c,)), dma((n_c,))])


def _mod_and_gather(c, ada_w, ada_b8, arrs):
    n = len(arrs)
    chunk = ada_w.shape[1]
    out_shape = [jax.ShapeDtypeStruct((N_DEV, 1, chunk), F32)]
    out_shape += [jax.ShapeDtypeStruct((N_DEV,) + a.shape, a.dtype) for a in arrs]

    def modulation(c_ref, w_ref, b_ref, out_ref, cbuf, part, s1, r1, s2, r2):
        me = _lin(_my_pos())
        first = []
        for k in range(1, N_DEV):
            cp = pltpu.make_async_remote_copy(c_ref, cbuf.at[me], s1.at[k - 1], r1.at[k - 1],
                                              device_id=_peer(k), device_id_type=MESH)
            cp.start()
            first.append(cp)
        cbuf[me] = c_ref[...]
        for cp in first:
            cp.wait_send()
            cp.wait_recv()
        cond = _silu(jnp.concatenate([cbuf[i] for i in range(N_DEV)], axis=0))
        mod = _mm_hi(cond, w_ref[...]) + b_ref[pl.ds(me, 1), :]
        for j in range(N_DEV):
            part[j] = mod[j:j + 1, :]
        second = []
        for k in range(1, N_DEV):
            peer = _peer(k)
            cp = pltpu.make_async_remote_copy(part.at[_lin(peer)], out_ref.at[me], s2.at[k - 1], r2.at[k - 1],
                                              device_id=peer, device_id_type=MESH)
            cp.start()
            second.append(cp)
        out_ref[me] = part[me]
        for cp in second:
            cp.wait_send()
            cp.wait_recv()

    def body(*refs):
        c_ref, w_ref, b_ref = refs[:3]
        ins = refs[3:3 + n]
        mod_ref = refs[3 + n]
        outs = refs[4 + n:4 + 2 * n]
        cbuf, part, s1, r1, s2, r2 = refs[4 + 2 * n:10 + 2 * n]
        gather = _TwoLevelGather(ins, outs, refs[10 + 2 * n:])
        gather.start()
        modulation(c_ref, w_ref, b_ref, mod_ref, cbuf, part, s1, r1, s2, r2)
        gather.forward()
        gather.finish()

    hbm = pl.BlockSpec(memory_space=pltpu.HBM)
    vm = pl.BlockSpec(memory_space=pltpu.VMEM)
    dma = pltpu.SemaphoreType.DMA
    res = pl.pallas_call(
        body, name="mod_and_gather", out_shape=out_shape, in_specs=[vm, vm, vm] + [hbm] * n,
        out_specs=[vm] + [hbm] * n,
        scratch_shapes=[pltpu.VMEM((N_DEV, 1, D_MODEL), F32), pltpu.VMEM((N_DEV, 1, chunk), F32)]
        + [dma((N_DEV - 1,))] * 4 + _g2_sems(n),
    )(c, ada_w, ada_b8, *arrs)
    return res[0], res[1:]


def _hosted_call(body, name, grid, in_specs, out_specs, out_shape, scratch_shapes, args, xchg, cparams):
    xchgs = [xchg] if isinstance(xchg, tuple) else list(xchg)
    grid = (grid,) if isinstance(grid, int) else tuple(grid)
    n_in, n_out, n_scr = len(in_specs), len(out_specs), len(scratch_shapes)
    windows = [[(a[1], a[2]) if isinstance(a, tuple) else None for a in group] for group, _ in xchgs]
    xchgs = [([a[0] if isinstance(a, tuple) else a for a in group], mode) for group, mode in xchgs]
    arrs = [a for group, _ in xchgs for a in group]
    n = len(arrs)
    x_shape, x_sems, sem_counts, stage_shape, stage_bufs = [], [], [], [], []
    for (group, mode), wins in zip(xchgs, windows):
        if mode == "two-level scatter":
            (a,) = group
            res_shape, stage, bufs, sems = _s2_shapes(a)
            shapes = [res_shape]
            stage_shape.append(stage)
            stage_bufs += bufs
        else:
            shapes, sems = _xchg_shapes(group, False if mode == "two-level" else mode)
        if mode == "two-level":
            sems = _g2_sems(len(group))
            shapes = [s if w is None else jax.ShapeDtypeStruct((N_DEV, w[1]) + a.shape[1:], a.dtype)
                      for s, w, a in zip(shapes, wins, group)]
        x_shape += shapes
        x_sems += sems
        sem_counts.append(len(sems))
    n_stage = len(stage_shape)
    n_steps = int(np.prod(grid))
    staged = ("two-level", "two-level scatter")

    def hosted(*refs):
        ins, refs = refs[:n_in], refs[n_in:]
        x_in, refs = refs[:n], refs[n:]
        outs, refs = refs[:n_out], refs[n_out:]
        x_out, refs = refs[:n], refs[n:]
        stages, refs = refs[:n_stage], refs[n_stage:]
        scr, refs = refs[:n_scr], refs[n_scr:]
        bufs, sems = refs[:2 * n_stage], refs[2 * n_stage:]
        step = pl.program_id(0)
        for d in range(1, len(grid)):
            step = step * grid[d] + pl.program_id(d)
        parts, a0, s0, t0 = [], 0, 0, 0
        for (group, mode), ns, wins in zip(xchgs, sem_counts, windows):
            gi, go, gs = x_in[a0:a0 + len(group)], x_out[a0:a0 + len(group)], sems[s0:s0 + ns]
            if mode == "two-level":
                parts.append((mode, _TwoLevelGather(gi, go, gs, wins)))
            elif mode == "two-level scatter":
                parts.append((mode, _TwoLevelScatter(gi[0], go[0], stages[t0], bufs[2 * t0], bufs[2 * t0 + 1], gs)))
                t0 += 1
            else:
                parts.append((mode, (gi, go, gs, mode)))
            a0, s0 = a0 + len(group), s0 + ns

        @pl.when(step == 0)
        def _():
            for mode, x in parts:
                if mode in staged:
                    x.start()
                else:
                    _xchg_start(*x)

        for kind, at in (("two-level", (2 * n_steps) // 3), ("two-level scatter", n_steps // 4)):
            if any(mode == kind for mode, _ in parts):
                @pl.when(step == at)
                def _():
                    for mode, x in parts:
                        if mode == kind:
                            x.forward()

        body(*ins, *outs, *scr)

        @pl.when(step == n_steps - 1)
        def _():
            for mode, x in parts:
                if mode in staged:
                    x.finish()
                else:
                    _xchg_wait(*x)

    hbm = pl.BlockSpec(memory_space=pltpu.HBM)
    res = pl.pallas_call(
        hosted, name=name, grid=grid, in_specs=list(in_specs) + [hbm] * n,
        out_specs=list(out_specs) + [hbm] * (n + n_stage), out_shape=list(out_shape) + x_shape + stage_shape,
        scratch_shapes=list(scratch_shapes) + stage_bufs + x_sems, compiler_params=cparams,
    )(*args, *arrs)
    return res[:n_out], res[n_out:n_out + n]


def _row(i):
    return (i, 0)


def _fixed(i):
    return (0, 0)


def _in_proj_fwd(x, norm1, scale1, shift1, w_in, tm, xchg):
    S = x.shape[0]

    def body(x_ref, n_ref, sc_ref, sh_ref, w_ref, qkv_ref, z_ref, xbc_ref, dt_ref):
        h = _modnorm(x_ref[...], n_ref[...], sc_ref[...], sh_ref[...])
        p = _mm_nt(h, w_ref[...])
        qkv_ref[...] = p[:, :768].astype(qkv_ref.dtype)
        z_ref[...] = p[:, 768:1280]
        xbc_ref[...] = p[:, 1280:2304]
        dt_ref[...] = p[:, 2304:IN_PAD]

    vec = pl.BlockSpec((1, D_MODEL), _fixed)
    return _hosted_call(
        body, "in_proj_fwd", S // tm,
        in_specs=[pl.BlockSpec((tm, D_MODEL), _row), vec, vec, vec, pl.BlockSpec((IN_PAD, D_MODEL), _fixed)],
        out_specs=[pl.BlockSpec((tm, 768), _row), pl.BlockSpec((tm, SSM_W), _row),
                   pl.BlockSpec((tm, XBC_W), _row), pl.BlockSpec((tm, LANE), _row)],
        out_shape=[jax.ShapeDtypeStruct((S, 768), MXU_DTYPE), jax.ShapeDtypeStruct((S, SSM_W), F32),
                   jax.ShapeDtypeStruct((S, XBC_W), F32), jax.ShapeDtypeStruct((S, LANE), F32)],
        scratch_shapes=[], args=(x, norm1, scale1, shift1, w_in), xchg=xchg, cparams=_cparams(VMEM_BIG),
    )


def _in_proj_bwd(x, dx1, dq, dkv, dz, dxbc, ddt, norm1, scale1, shift1, w_in, tm):
    S = x.shape[0]

    n_steps = S // tm
    half_cols = D_MODEL // 2

    def body(x_ref, dx1_ref, dq_ref, dkv_ref, dz_ref, dxbc_ref, ddt_ref, n_ref, sc_ref, sh_ref, w_ref,
             gx_ref, h_ref, acc_ref, gw_ref, gw_acc):
        i = pl.program_id(0)

        @pl.when(i == 0)
        def _():
            acc_ref[...] = jnp.zeros_like(acc_ref)
            gw_acc[...] = jnp.zeros_like(gw_acc)

        halves = [pl.ds(k * (tm // 2), tm // 2) for k in range(2)]
        dp = [jnp.concatenate([r[rows, :] for r in (dq_ref, dkv_ref, dz_ref, dxbc_ref, ddt_ref)], axis=1)
              for rows in halves]
        dh = [_mm(dp[k], w_ref[...]) for k in range(2)]
        parts = [_modnorm_parts(x_ref[rows, :]) for rows in halves]
        hb = [(parts[k][1] * n_ref[...] * (1.0 + sc_ref[...]) + sh_ref[...]).astype(h_ref.dtype) for k in range(2)]
        gw_acc[...] += _mm_tn(dp[0], hb[0][:, :half_cols]) + _mm_tn(dp[1], hb[1][:, :half_cols])
        bwd = [_modnorm_bwd(parts[k][0], parts[k][1], n_ref[...], sc_ref[...], dh[k]) for k in range(2)]
        for k, rows in enumerate(halves):
            gx_ref[rows, :] = dx1_ref[rows, :] + bwd[k][0]
            h_ref[rows, :] = hb[k]
        acc_ref[0:1, :] += bwd[0][1] + bwd[1][1]
        acc_ref[1:2, :] += bwd[0][2] + bwd[1][2]
        acc_ref[2:3, :] += bwd[0][3] + bwd[1][3]

        @pl.when(i == n_steps - 1)
        def _():
            gw_ref[...] = gw_acc[...].astype(gw_ref.dtype)

    vec = pl.BlockSpec((1, D_MODEL), _fixed)
    return pl.pallas_call(
        body, name="in_proj_bwd", grid=(n_steps,),
        in_specs=[pl.BlockSpec((tm, D_MODEL), _row), pl.BlockSpec((tm, D_MODEL), _row),
                  pl.BlockSpec((tm, ATTN_W), _row), pl.BlockSpec((tm, 2 * KV_W), _row),
                  pl.BlockSpec((tm, SSM_W), _row), pl.BlockSpec((tm, XBC_W), _row), pl.BlockSpec((tm, LANE), _row),
                  vec, vec, vec, pl.BlockSpec((IN_PAD, D_MODEL), _fixed)],
        out_specs=[pl.BlockSpec((tm, D_MODEL), _row), pl.BlockSpec((tm, D_MODEL), _row),
                   pl.BlockSpec((8, D_MODEL), _fixed), pl.BlockSpec((IN_PAD, half_cols), _fixed)],
        out_shape=[jax.ShapeDtypeStruct((S, D_MODEL), F32), jax.ShapeDtypeStruct((S, D_MODEL), MXU_DTYPE),
                   jax.ShapeDtypeStruct((8, D_MODEL), F32), jax.ShapeDtypeStruct((IN_PAD, half_cols), WIRE_DTYPE)],
        scratch_shapes=[pltpu.VMEM((IN_PAD, half_cols), F32)],
        compiler_params=_cparams(VMEM_BIG),
    )(x, dx1, dq, dkv, dz, dxbc, ddt, norm1, scale1, shift1, w_in)


def _out_stage(ya, ys0, ys1, z0, z1, an, sn0, sn1):
    half = SSM_W // 2
    a = _rms(ya, an, ATTN_W)
    g0 = _rms(ys0 * _silu(z0), sn0, half)
    g1 = _rms(ys1 * _silu(z1), sn1, half)
    return jnp.concatenate([a, g0, g1], axis=1)


def _out_stage_args(ya_ref, ys_ref, z_ref, an_ref, sn_ref):
    half = SSM_W // 2
    return (ya_ref[...], ys_ref[:, :half], ys_ref[:, half:], z_ref[:, :half], z_ref[:, half:],
            an_ref[...], sn_ref[:, :half], sn_ref[:, half:])


def _out_proj_fwd(x, ya, ys, z, an, sn, gate1, w_o, tm, xchg):
    S = x.shape[0]

    def body(x_ref, ya_ref, ys_ref, z_ref, an_ref, sn_ref, g_ref, w_ref, x1_ref):
        u = _out_stage(*_out_stage_args(ya_ref, ys_ref, z_ref, an_ref, sn_ref))
        x1_ref[...] = x_ref[...] + g_ref[...] * _mm(u, w_ref[...])

    half = pl.BlockSpec((tm, ATTN_W), _row)
    hvec = pl.BlockSpec((1, ATTN_W), _fixed)
    (x1,), x_out = _hosted_call(
        body, "out_proj_fwd", S // tm,
        in_specs=[pl.BlockSpec((tm, D_MODEL), _row), half, half, half, hvec, hvec,
                  pl.BlockSpec((1, D_MODEL), _fixed), pl.BlockSpec((D_MODEL, D_MODEL), _fixed)],
        out_specs=[pl.BlockSpec((tm, D_MODEL), _row)],
        out_shape=[jax.ShapeDtypeStruct((S, D_MODEL), F32)],
        scratch_shapes=[], args=(x, ya, ys, z, an, sn, gate1, w_o), xchg=xchg, cparams=_cparams(VMEM_BIG),
    )
    return x1, x_out


def _out_proj_bwd(dx1, ya, ys, z, an, sn, gate1, w_o, tm, xchg):
    S = dx1.shape[0]
    n_steps = S // tm

    def body(dx1_ref, ya_ref, ys_ref, z_ref, an_ref, sn_ref, g_ref, w_ref,
             dya_ref, dys_ref, dz_ref, gw_ref, acc_ref, gw_acc):
        i = pl.program_id(0)

        @pl.when(i == 0)
        def _():
            acc_ref[...] = jnp.zeros_like(acc_ref)
            gw_acc[...] = jnp.zeros_like(gw_acc)

        u, vjp = jax.vjp(_out_stage, *_out_stage_args(ya_ref, ys_ref, z_ref, an_ref, sn_ref))
        dx1 = dx1_ref[...]
        ub = u.astype(MXU_DTYPE)
        mix = _mm(ub, w_ref[...])
        dmix = dx1 * g_ref[...]
        dmixb = dmix.astype(MXU_DTYPE)
        du = _mm_nt(dmixb, w_ref[...])
        gw_acc[...] += _mm_tn(ub, dmixb)
        dya, dys0, dys1, dz0, dz1, dan, dsn0, dsn1 = vjp(du)
        dya_ref[...] = dya
        dys_ref[...] = jnp.concatenate([dys0, dys1], axis=1)
        dz_ref[...] = jnp.concatenate([dz0, dz1], axis=1).astype(dz_ref.dtype)
        acc_ref[0:1, :] += jnp.sum(dx1 * mix, axis=0, keepdims=True)
        acc_ref[1:2, :] += jnp.concatenate([dan, dsn0, dsn1], axis=1)

        @pl.when(i == n_steps - 1)
        def _():
            gw_ref[...] = gw_acc[...].astype(gw_ref.dtype)

    half = pl.BlockSpec((tm, ATTN_W), _row)
    hvec = pl.BlockSpec((1, ATTN_W), _fixed)
    full = pl.BlockSpec((tm, D_MODEL), _row)
    return _hosted_call(
        body, "out_proj_bwd", n_steps,
        in_specs=[full, half, half, half, hvec, hvec,
                  pl.BlockSpec((1, D_MODEL), _fixed), pl.BlockSpec((D_MODEL, D_MODEL), _fixed)],
        out_specs=[half, half, half, pl.BlockSpec((D_MODEL, D_MODEL), _fixed), pl.BlockSpec((8, D_MODEL), _fixed)],
        out_shape=[jax.ShapeDtypeStruct((S, ATTN_W), F32)] * 2 + [jax.ShapeDtypeStruct((S, ATTN_W), MXU_DTYPE),
                   jax.ShapeDtypeStruct((D_MODEL, D_MODEL), WIRE_DTYPE), jax.ShapeDtypeStruct((8, D_MODEL), F32)],
        scratch_shapes=[pltpu.VMEM((D_MODEL, D_MODEL), F32)],
        args=(dx1, ya, ys, z, an, sn, gate1, w_o), xchg=xchg, cparams=_cparams(VMEM_BIG),
    )


def _loss_rows(x2, fn, tgt):
    y = _rms(x2, fn, D_MODEL)
    per_row = jnp.sum(jnp.square(y - tgt), axis=1, keepdims=True)
    return jnp.sum(per_row, axis=0, keepdims=True) * (0.5 / D_MODEL)


def _mlp_loss(x1, tgt, norm2, scale2, shift2, gate2, fnorm, w_gu, w_d, tm):
    S = x1.shape[0]
    n_pieces = len(w_gu) + len(w_d)

    def body(*refs):
        x1_ref, t_ref, n_ref, sc_ref, sh_ref, g_ref, fn_ref = refs[:7]
        piece_refs = refs[7:7 + n_pieces]
        dx1_ref, h_ref, dgu_ref, act_ref, dmlp_ref, acc_ref, wgu, wd, wsem = refs[7 + n_pieces:]

        @pl.when(pl.program_id(0) == 0)
        def _():
            acc_ref[...] = jnp.zeros_like(acc_ref)
            copies = []
            for dst, pieces in ((wgu, piece_refs[:len(w_gu)]), (wd, piece_refs[len(w_gu):])):
                shard = sum(p.shape[1] for p in pieces)
                off = 0
                for p in pieces:
                    for j in range(N_DEV):
                        copies.append(pltpu.make_async_copy(p.at[j], dst.at[pl.ds(j * shard + off, p.shape[1])],
                                                            wsem.at[len(copies)]))
                    off += p.shape[1]
            for cp in copies:
                cp.start()
            for cp in copies:
                cp.wait()

        x1 = x1_ref[...]
        gate2 = g_ref[...]
        h, vjp_h = jax.vjp(_modnorm, x1, n_ref[...], sc_ref[...], sh_ref[...])
        hb = h.astype(MXU_DTYPE)
        gu = _mm_nt(hb, wgu[...])
        g, u = gu[:, :D_FF], gu[:, D_FF:]
        sg = jax.nn.sigmoid(g)
        silu_g = g * sg
        act = (silu_g * u).astype(MXU_DTYPE)
        mlp = _mm(act, wd[...])
        x2 = x1 + gate2 * mlp
        loss, vjp_loss = jax.vjp(_loss_rows, x2, fn_ref[...], t_ref[...])
        dx2, dfn, _ = vjp_loss(jnp.ones((1, 1), F32))
        dmlp = (dx2 * gate2).astype(MXU_DTYPE)
        dact = _mm_nt(dmlp, wd[...])
        dg = dact * u * (sg * (1.0 + g * (1.0 - sg)))
        du = dact * silu_g
        dgu = jnp.concatenate([dg, du], axis=1).astype(MXU_DTYPE)
        dh = _mm(dgu, wgu[...])
        dx, dn, dsc, dsh = vjp_h(dh)
        dx1_ref[...] = dx2 + dx
        h_ref[...] = hb
        dgu_ref[...] = dgu
        act_ref[...] = act
        dmlp_ref[...] = dmlp
        acc_ref[0:1, :] += dn
        acc_ref[1:2, :] += dsc
        acc_ref[2:3, :] += dsh
        acc_ref[3:4, :] += jnp.sum(dx2 * mlp, axis=0, keepdims=True)
        acc_ref[4:5, :] += dfn
        acc_ref[5:6, :] += jnp.broadcast_to(loss, (1, D_MODEL))

    full = pl.BlockSpec((tm, D_MODEL), _row)
    vec = pl.BlockSpec((1, D_MODEL), _fixed)
    anyspec = pl.BlockSpec(memory_space=pl.ANY)
    return pl.pallas_call(
        body, name="mlp_loss", grid=(S // tm,),
        in_specs=[full, full, vec, vec, vec, vec, vec] + [anyspec] * n_pieces,
        out_specs=[full, full, pl.BlockSpec((tm, 2 * D_FF), _row), pl.BlockSpec((tm, D_FF), _row), full,
                   pl.BlockSpec((8, D_MODEL), _fixed)],
        out_shape=[jax.ShapeDtypeStruct((S, D_MODEL), F32), jax.ShapeDtypeStruct((S, D_MODEL), MXU_DTYPE),
                   jax.ShapeDtypeStruct((S, 2 * D_FF), MXU_DTYPE), jax.ShapeDtypeStruct((S, D_FF), MXU_DTYPE),
                   jax.ShapeDtypeStruct((S, D_MODEL), MXU_DTYPE), jax.ShapeDtypeStruct((8, D_MODEL), F32)],
        scratch_shapes=[pltpu.VMEM((2 * D_FF, D_MODEL), MXU_DTYPE), pltpu.VMEM((D_FF, D_MODEL), MXU_DTYPE),
                        pltpu.SemaphoreType.DMA((N_DEV * n_pieces,))],
        compiler_params=_cparams(VMEM_BIG),
    )(x1, tgt, norm2, scale2, shift2, gate2, fnorm, *w_gu, *w_d)


def _wgrad(a, g, tk, ts, name, xchg=None, g_cols=None):
    pieces = list(a) if isinstance(a, (list, tuple)) else [a]
    S = pieces[0].shape[0]
    K = sum(p.shape[1] for p in pieces)
    assert len(pieces) == 1 or tk == K
    N, col = (g.shape[1], 0) if g_cols is None else g_cols
    ns = S // ts
    n_a = len(pieces)

    def body(*refs):
        a_refs, (g_ref, o_ref, acc_ref) = refs[:n_a], refs[n_a:]
        s = pl.program_id(1)

        @pl.when(s == 0)
        def _():
            acc_ref[...] = jnp.zeros_like(acc_ref)

        a_blk = a_refs[0][...] if n_a == 1 else jnp.concatenate([r[...] for r in a_refs], axis=1)
        acc_ref[...] += _mm_tn(a_blk, g_ref[...])

        @pl.when(s == ns - 1)
        def _():
            o_ref[...] = acc_ref[...].astype(o_ref.dtype)

    if n_a == 1:
        in_specs = [pl.BlockSpec((ts, tk), lambda j, s: (s, j))]
    else:
        in_specs = [pl.BlockSpec((ts, p.shape[1]), lambda j, s: (s, 0)) for p in pieces]
    in_specs.append(pl.BlockSpec((ts, N), lambda j, s: (s, col)))
    out_spec = pl.BlockSpec((tk, N), lambda j, s: (j, 0))
    out_shape = jax.ShapeDtypeStruct((K, N), WIRE_DTYPE)
    scratch = [pltpu.VMEM((tk, N), F32)]
    args = (*pieces, g)
    if xchg is None:
        return pl.pallas_call(body, name=name, grid=(K // tk, ns), in_specs=in_specs, out_specs=out_spec,
                              out_shape=out_shape, scratch_shapes=scratch, compiler_params=_cparams(VMEM_BIG))(*args)
    (out,), x_out = _hosted_call(body, name, (K // tk, ns), in_specs, [out_spec], [out_shape], scratch, args, xchg,
                                 _cparams(VMEM_BIG))
    return out, x_out


SSD_CHUNKS_PER_STEP = 4
SSD_BWD_CHUNKS_PER_STEP = 4
ATTN_BLOCKS_PER_STEP = 4
MASKED = -1e30
QK_SCALE = HALF ** -0.5


def _attn_bias(buckets, rel_bias):
    def body(bk_ref, relb_ref, out_ref):
        bk = bk_ref[...]
        i = lax.broadcasted_iota(jnp.int32, (BLK, 2 * BLK), 0)
        j = lax.broadcasted_iota(jnp.int32, (BLK, 2 * BLK), 1)
        window = (j > i) & (j <= i + BLK)
        for h in range(N_HEADS):
            acc = jnp.zeros((BLK, 2 * BLK), F32)
            for b in range(N_BUCKETS):
                acc = jnp.where(bk == b, relb_ref[b, h], acc)
            out_ref[0, h] = jnp.where(window, acc, MASKED)
            out_ref[1, h] = jnp.where(window & (j >= BLK), acc, MASKED)

    return pl.pallas_call(
        body, name="attn_bias", out_shape=jax.ShapeDtypeStruct((2, N_HEADS, BLK, 2 * BLK), F32),
        in_specs=[pl.BlockSpec(memory_space=pltpu.VMEM), pl.BlockSpec(memory_space=pltpu.SMEM)],
    )(buckets, rel_bias)


def _attn_fwd(qkv, bias, sinks, xchg):
    S = qkv.shape[0]
    nb = S // BLK

    nq = ATTN_BLOCKS_PER_STEP if nb % ATTN_BLOCKS_PER_STEP == 0 else 1
    rows = nq * BLK

    def body(q_ref, kvp_ref, kvc_ref, bias_ref, sinks_ref, y_ref):
        i = pl.program_id(0)
        q = q_ref[...].astype(F32) * QK_SCALE
        kv = jnp.concatenate([kvp_ref[...], kvc_ref[...]], axis=0).astype(F32)
        k_lo, k_hi = _split_pair(kv[:, :LANE])
        v_lo, v_hi = _split_pair(kv[:, LANE:])
        bands = [[t[b * BLK:(b + 2) * BLK].astype(MXU_DTYPE) for t in (k_lo, k_hi, v_lo, v_hi)] for b in range(nq)]
        q_heads = [_split_heads(q[b * BLK:(b + 1) * BLK], 4) for b in range(nq)]
        first = [jnp.where(i == 0, 1, 0) if b == 0 else 0 for b in range(nq)]
        items = [(b, h) for b in range(nq) for h in range(N_HEADS)]
        s = [_mm_nt(q_heads[b][h].astype(MXU_DTYPE), bands[b][h // 4]) + bias_ref[first[b], h] for b, h in items]
        m = [jnp.maximum(jnp.max(s[n], axis=-1, keepdims=True), sinks_ref[h]) for n, (b, h) in enumerate(items)]
        p = [jnp.exp(s[n] - m[n]) for n in range(len(items))]
        rinv = [1.0 / (jnp.sum(p[n], axis=-1, keepdims=True) + jnp.exp(sinks_ref[h] - m[n]))
                for n, (b, h) in enumerate(items)]
        out = [_mm(p[n], bands[b][2 + h // 4]) * rinv[n] for n, (b, h) in enumerate(items)]
        y_ref[...] = jnp.concatenate([_join_heads(out[b * N_HEADS:(b + 1) * N_HEADS]) for b in range(nq)], axis=0)

    smem = pl.BlockSpec(memory_space=pltpu.SMEM)
    return _hosted_call(
        body, "attn_fwd", nb // nq,
        in_specs=[pl.BlockSpec((rows, ATTN_W), _row),
                  pl.BlockSpec((BLK, 2 * KV_W), lambda i: (jnp.maximum(i * nq - 1, 0), 2)),
                  pl.BlockSpec((rows, 2 * KV_W), lambda i: (i, 2)),
                  pl.BlockSpec((2, N_HEADS, BLK, 2 * BLK), lambda i: (0, 0, 0, 0)), smem],
        out_specs=[pl.BlockSpec((rows, ATTN_W), _row)],
        out_shape=[jax.ShapeDtypeStruct((S, ATTN_W), F32)],
        scratch_shapes=[],
        args=(qkv, qkv, qkv, bias, sinks), xchg=xchg, cparams=_cparams(),
    )


def _attn_bwd(qkv, y, dy, bias, sinks, xchg):
    S = qkv.shape[0]
    nb = S // BLK
    nq = ATTN_BLOCKS_PER_STEP if nb % ATTN_BLOCKS_PER_STEP == 0 else 1
    rows, n_steps = nq * BLK, nb // nq

    def body(q_ref, kvp_ref, kvc_ref, y_ref, dy_ref, bias_ref, sinks_ref, dq_ref, dkv_ref, dbias_ref, dsk_ref, carry_ref):
        i = pl.program_id(0)

        @pl.when(i == 0)
        def _():
            dbias_ref[...] = jnp.zeros_like(dbias_ref)
            dsk_ref[...] = jnp.zeros_like(dsk_ref)
            carry_ref[...] = jnp.zeros_like(carry_ref)

        q = q_ref[...].astype(F32) * QK_SCALE
        kv = jnp.concatenate([kvp_ref[...], kvc_ref[...]], axis=0).astype(F32)
        k_lo, k_hi = _split_pair(kv[:, :LANE])
        v_lo, v_hi = _split_pair(kv[:, LANE:])
        bands = [[t[b * BLK:(b + 2) * BLK].astype(MXU_DTYPE) for t in (k_lo, k_hi, v_lo, v_hi)] for b in range(nq)]
        rows_of = lambda ref, b: ref[b * BLK:(b + 1) * BLK, :]
        first = [jnp.where(i == n_steps - 1, 1, 0) if b == 0 else 0 for b in range(nq)]
        items = [(b, h) for b in range(nq) for h in range(N_HEADS)]
        at = lambda b, h: b * N_HEADS + h
        q_heads = [hd for b in range(nq) for hd in _split_heads(q[b * BLK:(b + 1) * BLK], 4)]
        y_heads = [hd for b in range(nq) for hd in _split_heads(rows_of(y_ref, b), 4)]
        dy_heads = [hd for b in range(nq) for hd in _split_heads(rows_of(dy_ref, b), 4)]
        qs = [q_heads[n].astype(MXU_DTYPE) for n in range(len(items))]
        s = [_mm_nt(qs[at(b, h)], bands[b][h // 4]) + bias_ref[first[b], h] for b, h in items]
        m = [jnp.maximum(jnp.max(s[at(b, h)], axis=-1, keepdims=True), sinks_ref[h]) for b, h in items]
        p = [jnp.exp(s[n] - m[n]) for n in range(len(items))]
        esink = [jnp.exp(sinks_ref[h] - m[at(b, h)]) for b, h in items]
        rinv = [1.0 / (jnp.sum(p[n], axis=-1, keepdims=True) + esink[n]) for n in range(len(items))]
        t = [dy_heads[n] * rinv[n] for n in range(len(items))]
        delta = [jnp.sum(t[n] * y_heads[n], axis=-1, keepdims=True) for n in range(len(items))]
        tb = [t[n].astype(MXU_DTYPE) for n in range(len(items))]
        dp = [_mm_nt(tb[at(b, h)], bands[b][2 + h // 4]) for b, h in items]
        ds = [p[n] * (dp[n] - delta[n]) for n in range(len(items))]
        for h in range(N_HEADS):
            ds_h, dsk_h = ds[at(0, h)], esink[at(0, h)] * delta[at(0, h)]
            for b in range(1, nq):
                ds_h = ds_h + ds[at(b, h)]
                dsk_h = dsk_h + esink[at(b, h)] * delta[at(b, h)]
            dbias_ref[h] += ds_h
            dsk_ref[h] -= dsk_h
        dsb = [ds[n].astype(MXU_DTYPE) for n in range(len(items))]
        pb = [p[n].astype(MXU_DTYPE) for n in range(len(items))]
        dq_heads = [_mm(dsb[at(b, h)], bands[b][h // 4]) * QK_SCALE for b, h in items]
        grp = lambda lst, b, g: jnp.concatenate(lst[at(b, 4 * g):at(b, 4 * g) + 4], axis=0)
        dk_pads = [[_mm_tn(grp(dsb, b, g), grp(qs, b, g)) for g in range(2)] for b in range(nq)]
        dv_pads = [[_mm_tn(grp(pb, b, g), grp(tb, b, g)) for g in range(2)] for b in range(nq)]
        dq_ref[...] = jnp.concatenate([_join_heads(dq_heads[b * N_HEADS:(b + 1) * N_HEADS]) for b in range(nq)],
                                      axis=0).astype(dq_ref.dtype)
        part = lambda b, lo: jnp.concatenate(
            [_join_pair(d[b][0][lo:lo + BLK], d[b][1][lo:lo + BLK]) for d in (dk_pads, dv_pads)], axis=1)
        dkv = [part(b, BLK) + (part(b + 1, 0) if b + 1 < nq else carry_ref[...]) for b in range(nq)]
        dkv_ref[...] = jnp.concatenate(dkv, axis=0).astype(dkv_ref.dtype)
        carry_ref[...] = part(0, 0)

    smem = pl.BlockSpec(memory_space=pltpu.SMEM)
    rev = lambda i: (n_steps - 1 - i, 0)
    return _hosted_call(
        body, "attn_bwd", n_steps,
        in_specs=[pl.BlockSpec((rows, ATTN_W), rev),
                  pl.BlockSpec((BLK, 2 * KV_W), lambda i: (jnp.maximum((n_steps - 1 - i) * nq - 1, 0), 2)),
                  pl.BlockSpec((rows, 2 * KV_W), lambda i: (n_steps - 1 - i, 2)),
                  pl.BlockSpec((rows, ATTN_W), rev), pl.BlockSpec((rows, ATTN_W), rev),
                  pl.BlockSpec((2, N_HEADS, BLK, 2 * BLK), lambda i: (0, 0, 0, 0)), smem],
        out_specs=[pl.BlockSpec((rows, ATTN_W), rev), pl.BlockSpec((rows, 2 * KV_W), rev),
                   pl.BlockSpec((N_HEADS, BLK, 2 * BLK), lambda i: (0, 0, 0)),
                   pl.BlockSpec((N_HEADS, BLK, 1), lambda i: (0, 0, 0))],
        out_shape=[jax.ShapeDtypeStruct((S, ATTN_W), MXU_DTYPE), jax.ShapeDtypeStruct((S, 2 * KV_W), MXU_DTYPE),
                   jax.ShapeDtypeStruct((N_HEADS, BLK, 2 * BLK), F32), jax.ShapeDtypeStruct((N_HEADS, BLK, 1), F32)],
        scratch_shapes=[pltpu.VMEM((BLK, 2 * KV_W), F32)],
        args=(qkv, qkv, qkv, y, dy, bias, sinks), xchg=xchg, cparams=_cparams(),
    )


def _attn_finish(dbias, dsk, buckets):
    def body(db_ref, dsk_ref, bk_ref, drel_ref, dsink_ref):
        bk = bk_ref[...]
        r = lax.broadcasted_iota(jnp.int32, (N_BUCKETS, LANE), 0)
        l = lax.broadcasted_iota(jnp.int32, (N_BUCKETS, LANE), 1)
        row = lax.broadcasted_iota(jnp.int32, (N_HEADS, LANE), 0)
        res = jnp.zeros((N_BUCKETS, LANE), F32)
        dsink = jnp.zeros((N_HEADS, LANE), F32)
        for h in range(N_HEADS):
            db = db_ref[h]
            for b in range(N_BUCKETS):
                v = jnp.sum(jnp.sum(jnp.where(bk == b, db, 0.0), axis=1, keepdims=True), axis=0, keepdims=True)
                res = res + jnp.where((r == b) & (l == h), v, 0.0)
            dsink = dsink + jnp.where(row == h, jnp.sum(dsk_ref[h], axis=0, keepdims=True), 0.0)
        drel_ref[...] = res
        dsink_ref[...] = dsink

    return pl.pallas_call(body, name="attn_finish",
                          out_shape=[jax.ShapeDtypeStruct((N_BUCKETS, LANE), F32),
                                     jax.ShapeDtypeStruct((N_HEADS, LANE), F32)])(dbias, dsk, buckets)


def _ssd_consts():
    r = lax.broadcasted_iota(jnp.int32, (BLK, BLK), 0)
    c = lax.broadcasted_iota(jnp.int32, (BLK, BLK), 1)
    causal = c <= r
    upper = (r <= c).astype(F32)
    last = r == BLK - 1
    head = lax.broadcasted_iota(jnp.int32, (N_HEADS, BLK), 0)
    return causal, upper, last, head


def _ssd_chunks(xs, bg, cg, dt_raw_t, prev0, dtb, alog, d_rows, consts):
    causal, upper, last, head = consts
    nq = len(xs)
    items = [(c, h) for c in range(nq) for h in range(N_HEADS)]
    at = lambda c, h: c * N_HEADS + h
    a_neg = -jnp.exp(alog)
    dt_t = [_softplus(dt_raw_t[c] + dtb) for c in range(nq)]
    acs_t = [_mm_hi(dt_t[c] * a_neg, upper) for c in range(nq)]
    cb = [[_mm_nt(cg[c][g], bg[c][g]) for g in range(2)] for c in range(nq)]
    pick = lambda t, h: jnp.sum(jnp.where(head == h, t, 0.0), axis=0, keepdims=True)
    dt_row = [pick(dt_t[c], h) for c, h in items]
    a_row = [pick(acs_t[c], h) for c, h in items]
    a_rb = [jnp.broadcast_to(a_row[n], (BLK, BLK)) for n in range(len(items))]
    a_b = [a_rb[n].T for n in range(len(items))]
    a_last = [jnp.sum(jnp.where(last, a_b[n], 0.0), axis=0, keepdims=True) for n in range(len(items))]
    w = [cb[c][h // 4] * jnp.exp(jnp.where(causal, a_b[at(c, h)] - a_rb[at(c, h)], -1e30)) * dt_row[at(c, h)]
         for c, h in items]
    f_b = [jnp.broadcast_to(dt_row[n] * jnp.exp(a_last[n] - a_row[n]), (BLK, BLK)).T for n in range(len(items))]
    y_in = [_mm(w[at(c, h)], xs[c][h]) for c, h in items]
    st = [_mm_tn(bg[c][h // 4], xs[c][h] * f_b[at(c, h)]) for c, h in items]
    e_b = [jnp.exp(a_b[n]) for n in range(len(items))]
    states = [list(prev0)]
    for c in range(nq):
        states.append([states[c][h] * jnp.exp(a_last[at(c, h)]) + st[at(c, h)] for h in range(N_HEADS)])
    y_off = [_mm(cg[c][h // 4], states[c][h]) * e_b[at(c, h)] for c, h in items]
    ys = [[y_in[at(c, h)] + y_off[at(c, h)] + d_rows[h] * xs[c][h] for h in range(N_HEADS)] for c in range(nq)]
    return ys, states


def _ssd_chunks_bwd(xs, bg, cg, dt_raw_t, prev, dtb, alog, d_rows, dys, dh_last, consts):
    causal, upper, last, head = consts
    nq = len(xs)
    items = [(c, h) for c in range(nq) for h in range(N_HEADS)]
    ni = len(items)
    at = lambda c, h: c * N_HEADS + h
    groups = [(c, g) for c in range(nq) for g in range(2)]
    lane = _lane_iota((BLK, BLK))
    lane_row = _lane_iota((1, BLK))
    a_neg = -jnp.exp(alog)
    pre_dt = [dt_raw_t[c] + dtb for c in range(nq)]
    dt_t = [_softplus(pre_dt[c]) for c in range(nq)]
    acs_t = [_mm_hi(dt_t[c] * a_neg, upper) for c in range(nq)]
    pick = lambda t, h: jnp.sum(jnp.where(head == h, t, 0.0), axis=0, keepdims=True)
    full_sum = lambda t: jnp.sum(jnp.sum(t, axis=1, keepdims=True), axis=0, keepdims=True)
    dt_row = [pick(dt_t[c], h) for c, h in items]
    a_row = [pick(acs_t[c], h) for c, h in items]
    a_rb = [jnp.broadcast_to(a_row[n], (BLK, BLK)) for n in range(ni)]
    a_b = [a_rb[n].T for n in range(ni)]
    a_last = [jnp.sum(jnp.where(last, a_b[n], 0.0), axis=0, keepdims=True) for n in range(ni)]
    lm = [jnp.exp(jnp.where(causal, a_b[n] - a_rb[n], -1e30)) for n in range(ni)]
    cgb = [[cg[c][g].astype(MXU_DTYPE) for g in range(2)] for c in range(nq)]
    bgb = [[bg[c][g].astype(MXU_DTYPE) for g in range(2)] for c in range(nq)]
    cb = [[_mm_nt(cgb[c][g], bgb[c][g]) for g in range(2)] for c in range(nq)]
    u = [cb[c][h // 4] * lm[at(c, h)] for c, h in items]
    w = [(u[n] * dt_row[n]).astype(MXU_DTYPE) for n in range(ni)]
    e_row = [jnp.exp(a_last[n] - a_row[n]) for n in range(ni)]
    f_row = [dt_row[n] * e_row[n] for n in range(ni)]
    f_b = [jnp.broadcast_to(f_row[n], (BLK, BLK)).T for n in range(ni)]
    e_b = [jnp.exp(a_b[n]) for n in range(ni)]
    el = [jnp.exp(a_last[n]) for n in range(ni)]
    xb = [xs[c][h].astype(MXU_DTYPE) for c, h in items]
    dyb = [dys[c][h].astype(MXU_DTYPE) for c, h in items]
    prevb = [prev[c][h].astype(MXU_DTYPE) for c, h in items]
    gmat = [_mm(cgb[c][h // 4], prevb[at(c, h)]) for c, h in items]
    dw = [_mm_nt(dyb[n], xb[n]) for n in range(ni)]
    dg = [dys[c][h] * e_b[at(c, h)] for c, h in items]
    dgb = [dg[n].astype(MXU_DTYPE) for n in range(ni)]
    from_y = [_mm_tn(cgb[c][h // 4], dgb[at(c, h)]) for c, h in items]
    dhs = [None] * ni
    dprev = [None] * ni
    for c in reversed(range(nq)):
        for h in range(N_HEADS):
            dhs[at(c, h)] = dh_last[h] if c == nq - 1 else dprev[at(c + 1, h)]
            dprev[at(c, h)] = from_y[at(c, h)] + dhs[at(c, h)] * el[at(c, h)]
    dstb = [dhs[n].astype(MXU_DTYPE) for n in range(ni)]
    dxf = [_mm(bgb[c][h // 4], dstb[at(c, h)]) for c, h in items]
    xfb = [(xs[c][h] * f_b[at(c, h)]).astype(MXU_DTYPE) for c, h in items]
    dxs = [_mm_tn(w[at(c, h)], dyb[at(c, h)]) + d_rows[h] * dys[c][h] + f_b[at(c, h)] * dxf[at(c, h)]
           for c, h in items]
    dd_item = [jnp.sum(dys[c][h] * xs[c][h], axis=0, keepdims=True) for c, h in items]
    dcg_h = [_mm_nt(dgb[n], prevb[n]) for n in range(ni)]
    dbg_h = [_mm_nt(xfb[n], dstb[n]) for n in range(ni)]
    zt = [dw[n] * u[n] for n in range(ni)]
    dseg = [zt[n] * dt_row[n] for n in range(ni)]
    dcb_h = [dw[n] * lm[n] * dt_row[n] for n in range(ni)]
    four = lambda lst, c, g: lst[at(c, 4 * g)] + lst[at(c, 4 * g + 1)] + lst[at(c, 4 * g + 2)] + lst[at(c, 4 * g + 3)]
    dcb = {(c, g): four(dcb_h, c, g).astype(MXU_DTYPE) for c, g in groups}
    dcg = [[four(dcg_h, c, g) + _mm(dcb[c, g], bgb[c][g]) for g in range(2)] for c in range(nq)]
    dbg = [[four(dbg_h, c, g) + _mm_tn(dcb[c, g], cgb[c][g]) for g in range(2)] for c in range(nq)]
    r1 = [jnp.sum(dg[n] * gmat[n] + dseg[n], axis=1, keepdims=True) for n in range(ni)]
    r2 = [jnp.sum(dxf[at(c, h)] * xs[c][h], axis=1, keepdims=True) for c, h in items]
    tt = [jnp.where(lane < HALF, jnp.broadcast_to(r1[n], (BLK, BLK)), jnp.broadcast_to(r2[n], (BLK, BLK))).T
          for n in range(ni)]
    r1_row = [tt[n][0:1, :] for n in range(ni)]
    r2_row = [tt[n][HALF:HALF + 1, :] for n in range(ni)]
    d_el = [full_sum(dhs[at(c, h)] * prev[c][h]) for c, h in items]
    da_last = [jnp.sum(r2_row[n] * f_row[n], axis=1, keepdims=True) + el[n] * d_el[n] for n in range(ni)]
    da_row = [r1_row[n] - jnp.sum(dseg[n], axis=0, keepdims=True) - r2_row[n] * f_row[n]
              + jnp.where(lane_row == BLK - 1, da_last[n], 0.0) for n in range(ni)]
    ddt_row = [jnp.sum(zt[n], axis=0, keepdims=True) + r2_row[n] * e_row[n] for n in range(ni)]
    draw, dalog = [], jnp.zeros((N_HEADS, BLK), F32)
    for c in range(nq):
        da_t = jnp.zeros((N_HEADS, BLK), F32)
        ddt_t = jnp.zeros((N_HEADS, BLK), F32)
        for h in range(N_HEADS):
            da_t = jnp.where(head == h, da_row[at(c, h)], da_t)
            ddt_t = jnp.where(head == h, ddt_row[at(c, h)], ddt_t)
        d_dta = _mm_hi(da_t, causal.astype(F32))
        dalog = dalog + d_dta * dt_t[c] * a_neg
        draw.append((ddt_t + d_dta * a_neg) * jax.nn.sigmoid(pre_dt[c]))
    ddtb = draw[0]
    for c in range(1, nq):
        ddtb = ddtb + draw[c]
    dd_rows = []
    for h in range(N_HEADS):
        t = dd_item[at(0, h)]
        for c in range(1, nq):
            t = t + dd_item[at(c, h)]
        dd_rows.append(t)
    return ([dxs[c * N_HEADS:(c + 1) * N_HEADS] for c in range(nq)], dbg, dcg, draw,
            [dprev[at(0, h)] for h in range(N_HEADS)], ddtb, dalog, dd_rows)


def _dt_rows(dt_blk):
    return dt_blk.T[:N_HEADS]


def _conv_pre(halo, blk, cw_ref, cb_ref):
    ext = jnp.concatenate([halo, blk], axis=0)
    taps = [pltpu.roll(ext, 3 - k, 0)[8:] for k in range(3)] + [blk]
    pre = cb_ref[...] + cw_ref[0:1, :] * taps[0]
    for k in range(1, 4):
        pre = pre + cw_ref[k:k + 1, :] * taps[k]
    return pre


def _ssd_split(pre):
    heads = _split_heads(pre[:, :SSM_W], 4)
    pb = [pre[:, SSM_W + g * D_STATE:SSM_W + (g + 1) * D_STATE] for g in range(2)]
    pc = [pre[:, SSM_W + 2 * D_STATE + g * D_STATE:SSM_W + 2 * D_STATE + (g + 1) * D_STATE] for g in range(2)]
    return heads, pb, pc


def _ssd_fwd(xbc, dt_raw, conv_w, conv_b, dtb_row, alog_row, d_exp, xchg):
    S = xbc.shape[0]
    nc = S // BLK
    nq = SSD_CHUNKS_PER_STEP if nc % SSD_CHUNKS_PER_STEP == 0 else 1
    rows = nq * BLK

    def body(xbc_ref, halo_ref, dt_ref, cw_ref, cb_ref, dtb_ref, alog_ref, d_ref, y_ref, prev_ref, pre_ref, state_ref):
        i = pl.program_id(0)

        @pl.when(i == 0)
        def _():
            state_ref[...] = jnp.zeros_like(state_ref)

        halo = halo_ref[...] * jnp.where(i > 0, 1.0, 0.0)
        pre = _conv_pre(halo, xbc_ref[...], cw_ref, cb_ref)
        pre_ref[...] = pre
        xc = _silu(pre)
        split = [_ssd_split(xc[c * BLK:(c + 1) * BLK]) for c in range(nq)]
        dt_t = [_dt_rows(dt_ref[c * BLK:(c + 1) * BLK, :]) for c in range(nq)]
        prev0 = [state_ref[h] for h in range(N_HEADS)]
        d_rows = [d_ref[h:h + 1, :] for h in range(N_HEADS)]
        ys, states = _ssd_chunks([s[0] for s in split], [s[1] for s in split], [s[2] for s in split], dt_t, prev0,
                                 dtb_ref[...], alog_ref[...], d_rows, _ssd_consts())
        for h in range(N_HEADS):
            for c in range(nq):
                prev_ref[c, h] = states[c][h]
            state_ref[h] = states[nq][h]
        y_ref[...] = jnp.concatenate([_join_heads(ys[c]) for c in range(nq)], axis=0)

    vec = pl.BlockSpec((N_HEADS, LANE), _fixed)
    return _hosted_call(
        body, "ssd_fwd", nc // nq,
        in_specs=[pl.BlockSpec((rows, XBC_W), _row),
                  pl.BlockSpec((8, XBC_W), lambda i: (jnp.maximum(i * (rows // 8) - 1, 0), 0)),
                  pl.BlockSpec((rows, LANE), _row),
                  pl.BlockSpec((4, XBC_W), _fixed), pl.BlockSpec((1, XBC_W), _fixed), vec, vec,
                  pl.BlockSpec((N_HEADS, LANE), _fixed)],
        out_specs=[pl.BlockSpec((rows, SSM_W), _row),
                   pl.BlockSpec((nq, N_HEADS, D_STATE, LANE), lambda i: (i, 0, 0, 0)),
                   pl.BlockSpec((rows, XBC_W), _row)],
        out_shape=[jax.ShapeDtypeStruct((S, SSM_W), F32), jax.ShapeDtypeStruct((nc, N_HEADS, D_STATE, LANE), F32),
                   jax.ShapeDtypeStruct((S, XBC_W), F32)],
        scratch_shapes=[pltpu.VMEM((N_HEADS, D_STATE, LANE), F32)],
        args=(xbc, xbc, dt_raw, conv_w, conv_b, dtb_row, alog_row, d_exp), xchg=xchg, cparams=_cparams(),
    )


def _ssd_bwd(xbc, pre_act, dt_raw, prev_states, dy, conv_w, dtb_row, alog_row, d_exp, xchg):
    S = xbc.shape[0]
    nc = S // BLK
    nq = SSD_BWD_CHUNKS_PER_STEP if nc % SSD_BWD_CHUNKS_PER_STEP == 0 else 1
    rows, n_steps = nq * BLK, nc // nq

    def body(xbc_ref, halo_ref, pre_ref, dt_ref, prev_ref, dy_ref, cw_ref, dtb_ref, alog_ref, d_ref,
             dxbc_ref, ddt_ref, dcw_ref, dvec_ref, dd_ref, gstate_ref, ghalo_ref):
        i = pl.program_id(0)

        @pl.when(i == 0)
        def _():
            gstate_ref[...] = jnp.zeros_like(gstate_ref)
            ghalo_ref[...] = jnp.zeros_like(ghalo_ref)
            dcw_ref[...] = jnp.zeros_like(dcw_ref)
            dvec_ref[...] = jnp.zeros_like(dvec_ref)
            dd_ref[...] = jnp.zeros_like(dd_ref)

        halo = halo_ref[...] * jnp.where(i < n_steps - 1, 1.0, 0.0)
        ext = jnp.concatenate([halo, xbc_ref[...]], axis=0)
        pre = pre_ref[...]
        sig = jax.nn.sigmoid(pre)
        xc = pre * sig
        split = [_ssd_split(xc[c * BLK:(c + 1) * BLK]) for c in range(nq)]
        dt_t = [_dt_rows(dt_ref[c * BLK:(c + 1) * BLK, :]) for c in range(nq)]
        prev = [[prev_ref[c, h] for h in range(N_HEADS)] for c in range(nq)]
        d_rows = [d_ref[h:h + 1, :] for h in range(N_HEADS)]
        dys = [_split_heads(dy_ref[c * BLK:(c + 1) * BLK, :], 4) for c in range(nq)]
        dh_last = [gstate_ref[h] for h in range(N_HEADS)]
        dheads, dpb, dpc, ddt_t, dprev0, ddtb, dalog, dd_rows = _ssd_chunks_bwd(
            [s[0] for s in split], [s[1] for s in split], [s[2] for s in split], dt_t, prev, dtb_ref[...],
            alog_ref[...], d_rows, dys, dh_last, _ssd_consts())
        for h in range(N_HEADS):
            gstate_ref[h] = dprev0[h]
            dd_ref[h:h + 1, :] += dd_rows[h]
        pad = jnp.zeros((BLK - N_HEADS, BLK), F32)
        ddt_ref[...] = jnp.concatenate([jnp.concatenate([ddt_t[c], pad], axis=0).T for c in range(nq)],
                                       axis=0).astype(ddt_ref.dtype)
        dvec_ref[0:N_HEADS, :] += ddtb
        dvec_ref[N_HEADS:, :] += dalog
        dxc = jnp.concatenate([jnp.concatenate([_join_heads(dheads[c])] + list(dpb[c]) + list(dpc[c]), axis=1)
                               for c in range(nq)], axis=0)
        dpre = dxc * (sig * (1.0 + pre * (1.0 - sig)))
        zeros8 = jnp.zeros((8, XBC_W), F32)
        dpe = jnp.concatenate([zeros8, dpre, zeros8], axis=0)
        n_ext = 16 + rows
        shifted = [pltpu.roll(dpe, n_ext - (3 - k), 0)[:8 + rows] for k in range(3)] + [dpe[:8 + rows]]
        dext = cw_ref[0:1, :] * shifted[0]
        for k in range(1, 4):
            dext = dext + cw_ref[k:k + 1, :] * shifted[k]
        for k in range(4):
            dcw_ref[k:k + 1, :] += jnp.sum(shifted[k] * ext, axis=0, keepdims=True)
        dcw_ref[4:5, :] += jnp.sum(dpre, axis=0, keepdims=True)
        dxbc_ref[...] = jnp.concatenate([dext[8:rows], dext[rows:] + ghalo_ref[...]], axis=0).astype(dxbc_ref.dtype)
        ghalo_ref[...] = dext[:8, :]

    vec = pl.BlockSpec((N_HEADS, LANE), _fixed)
    rev = lambda i: (n_steps - 1 - i, 0)
    return _hosted_call(
        body, "ssd_bwd", n_steps,
        in_specs=[pl.BlockSpec((rows, XBC_W), rev),
                  pl.BlockSpec((8, XBC_W), lambda i: (jnp.maximum((n_steps - 1 - i) * (rows // 8) - 1, 0), 0)),
                  pl.BlockSpec((rows, XBC_W), rev),
                  pl.BlockSpec((rows, LANE), rev),
                  pl.BlockSpec((nq, N_HEADS, D_STATE, LANE), lambda i: (n_steps - 1 - i, 0, 0, 0)),
                  pl.BlockSpec((rows, SSM_W), rev),
                  pl.BlockSpec((4, XBC_W), _fixed), vec, vec,
                  pl.BlockSpec((N_HEADS, LANE), _fixed)],
        out_specs=[pl.BlockSpec((rows, XBC_W), rev), pl.BlockSpec((rows, LANE), rev),
                   pl.BlockSpec((8, XBC_W), _fixed), pl.BlockSpec((2 * N_HEADS, LANE), _fixed),
                   pl.BlockSpec((N_HEADS, LANE), _fixed)],
        out_shape=[jax.ShapeDtypeStruct((S, XBC_W), MXU_DTYPE), jax.ShapeDtypeStruct((S, LANE), MXU_DTYPE),
                   jax.ShapeDtypeStruct((8, XBC_W), F32), jax.ShapeDtypeStruct((2 * N_HEADS, LANE), F32),
                   jax.ShapeDtypeStruct((N_HEADS, LANE), F32)],
        scratch_shapes=[pltpu.VMEM((N_HEADS, D_STATE, LANE), F32), pltpu.VMEM((8, XBC_W), F32)],
        args=(xbc, xbc, pre_act, dt_raw, prev_states, dy, conv_w, dtb_row, alog_row, d_exp), xchg=xchg,
        cparams=_cparams(VMEM_BIG),
    )


def _adamw_math(w, g, m, v):
    m = ADAM_B1 * m + (1.0 - ADAM_B1) * g
    v = ADAM_B2 * v + (1.0 - ADAM_B2) * jnp.square(g)
    m_hat = m / (1.0 - ADAM_B1 ** ADAM_STEP)
    v_hat = v / (1.0 - ADAM_B2 ** ADAM_STEP)
    delta = -ADAM_LR * (m_hat / (jnp.sqrt(v_hat) + ADAM_EPS) + ADAM_WD * w)
    return delta, m, v


def _reduce_adamw_halves(part_a, part_b, w, m, v, name):
    R, C = w.shape
    P = part_a.shape[0]
    tl = 256
    n = C // tl

    def body(a_ref, b_ref, w_ref, m_ref, v_ref, g_ref, d_ref, nm_ref, nv_ref):
        ga, gb = a_ref[0].astype(F32), b_ref[0].astype(F32)
        for i in range(1, P):
            ga, gb = ga + a_ref[i].astype(F32), gb + b_ref[i].astype(F32)
        first = jnp.where(pl.program_id(0) < n // 2, 1.0, 0.0)
        g = ga * first + gb * (1.0 - first)
        d, nm, nv = _adamw_math(w_ref[...], g, m_ref[...], v_ref[...])
        g_ref[...] = g
        d_ref[...] = d
        nm_ref[...] = nm
        nv_ref[...] = nv

    blk = pl.BlockSpec((R, tl), lambda i: (0, i))
    return pl.pallas_call(
        body, name=name, grid=(n,),
        in_specs=[pl.BlockSpec((P, R, tl), lambda i: (0, 0, jnp.minimum(i, n // 2 - 1))),
                  pl.BlockSpec((P, R, tl), lambda i: (0, 0, jnp.maximum(i - n // 2, 0))), blk, blk, blk],
        out_specs=[blk] * 4, out_shape=[jax.ShapeDtypeStruct((R, C), F32)] * 4,
    )(part_a, part_b, w, m, v)


def _reduce_adamw_hosting(parts_list, wmv_list, name, xchg):
    n_arr = len(parts_list)
    pieces = [list(p) if isinstance(p, (tuple, list)) else [p] for p in parts_list]
    n_pieces = sum(len(p) for p in pieces)
    C = wmv_list[0][0].shape[1]
    tl = 256

    def total(ref):
        g = ref[0].astype(F32)
        for i in range(1, N_DEV):
            g = g + ref[i].astype(F32)
        return g

    def body(*refs):
        p_refs, wmv_refs, o_refs = refs[:n_pieces], refs[n_pieces:n_pieces + 3 * n_arr], refs[n_pieces + 3 * n_arr:]
        at = 0
        for k in range(n_arr):
            sums = [total(r) for r in p_refs[at:at + len(pieces[k])]]
            at += len(pieces[k])
            g = sums[0] if len(sums) == 1 else jnp.concatenate(sums, axis=0)
            w_ref, m_ref, v_ref = wmv_refs[3 * k:3 * k + 3]
            d, nm, nv = _adamw_math(w_ref[...], g, m_ref[...], v_ref[...])
            for o, val in zip(o_refs[4 * k:4 * k + 4], (g, d, nm, nv)):
                o[...] = val

    in_specs = [pl.BlockSpec((N_DEV, p.shape[1], tl), lambda i: (0, 0, i)) for group in pieces for p in group]
    in_specs += [pl.BlockSpec((w.shape[0], tl), lambda i: (0, i)) for w, _, _ in wmv_list for _ in range(3)]
    out_specs = [pl.BlockSpec((w.shape[0], tl), lambda i: (0, i)) for w, _, _ in wmv_list for _ in range(4)]
    out_shape = [jax.ShapeDtypeStruct(w.shape, F32) for w, _, _ in wmv_list for _ in range(4)]
    args = [p for group in pieces for p in group] + [a for wmv in wmv_list for a in wmv]
    outs, x_out = _hosted_call(body, name, C // tl, in_specs, out_specs, out_shape, [], args, xchg,
                               _cparams(VMEM_BIG))
    return [outs[4 * k:4 * k + 4] for k in range(n_arr)], x_out


_SMALL_NAMES = ("ada_b", "norm1", "conv_w", "conv_b", "dt_bias", "A_log", "D_skip", "sinks", "attn_out_norm",
                "ssm_out_norm", "norm2", "rel_bias", "final_norm")
N_MOD = 6 * D_MODEL


def _mod_row(a0, a1, a2):
    return jnp.concatenate([a0[2:3], a0[1:2], a1[0:1], a2[2:3], a2[1:2], a2[3:4]], axis=1)


def _small_update(gathered, params):
    n_g = len(gathered)
    flat = [a for name in _SMALL_NAMES for a in params[name]]

    def body(*refs):
        a0_ref, a1_ref, a2_ref, cw_ref, dv_ref, dd_ref, ds_ref, dr_ref, c_ref = refs[:n_g]
        wmv = refs[n_g:n_g + len(flat)]
        outs = refs[n_g + len(flat):]

        def total(ref):
            t = ref[0]
            for i in range(1, N_DEV):
                t = t + ref[i]
            return t

        t0, t1, t2, tcw, tdv, tdd, tds, tdr = [total(r) for r in (a0_ref, a1_ref, a2_ref, cw_ref, dv_ref, dd_ref,
                                                                   ds_ref, dr_ref)]
        r8 = lax.broadcasted_iota(jnp.int32, (N_HEADS, LANE), 0)
        l8 = lax.broadcasted_iota(jnp.int32, (N_HEADS, LANE), 1)

        def diag_row(t):
            return jnp.sum(jnp.where(r8 == l8, t, 0.0), axis=0, keepdims=True)[:, :N_HEADS]

        def lane_sums(t):
            return diag_row(jnp.broadcast_to(jnp.sum(t, axis=1, keepdims=True), (N_HEADS, LANE)))

        me = _lin(_my_pos())
        n_cw = XBC_W // N_DEV
        cw_mine = jnp.zeros((4, n_cw), F32)
        for j in range(N_DEV):
            cw_mine = cw_mine + tcw[0:4, j * n_cw:(j + 1) * n_cw] * jnp.where(me == j, 1.0, 0.0)
        grads = {
            "ada_b": _mod_row(t0, t1, t2), "norm1": t0[0:1], "conv_w": cw_mine, "conv_b": tcw[4:5],
            "dt_bias": lane_sums(tdv[:N_HEADS]), "A_log": lane_sums(tdv[N_HEADS:]), "D_skip": lane_sums(tdd),
            "sinks": diag_row(tds), "attn_out_norm": t1[1:2, :ATTN_W], "ssm_out_norm": t1[1:2, ATTN_W:],
            "norm2": t2[0:1], "rel_bias": tdr[:, :N_HEADS], "final_norm": t2[4:5],
        }
        for k, name in enumerate(_SMALL_NAMES):
            w_ref, m_ref, v_ref = wmv[3 * k:3 * k + 3]
            g = grads[name]
            d, nm, nv = _adamw_math(w_ref[...], g, m_ref[...], v_ref[...])
            for o, val in zip(outs[4 * k:4 * k + 4], (g, d, nm, nv)):
                o[...] = val
        loss_ref, call_ref, dmod_ref = outs[4 * len(_SMALL_NAMES):]
        loss_ref[...] = t2[5:6, 0:1]
        call_ref[...] = jnp.concatenate([c_ref[i] for i in range(N_DEV)], axis=0)
        dmod_ref[...] = jnp.concatenate([_mod_row(a0_ref[i], a1_ref[i], a2_ref[i]) for i in range(N_DEV)], axis=0)

    out_shape = [jax.ShapeDtypeStruct(params[name][0].shape, F32) for name in _SMALL_NAMES for _ in range(4)]
    out_shape += [jax.ShapeDtypeStruct((1, 1), F32), jax.ShapeDtypeStruct((N_DEV, D_MODEL), F32),
                  jax.ShapeDtypeStruct((N_DEV, N_MOD), F32)]
    res = pl.pallas_call(body, name="small_update", out_shape=out_shape)(*gathered, *flat)
    upd = {name: res[4 * k:4 * k + 4] for k, name in enumerate(_SMALL_NAMES)}
    loss, c_all, dmod_all = res[4 * len(_SMALL_NAMES):]
    return upd, loss, c_all, dmod_all


def _ada_w_update(c_all, dmod_all, w, m, v):
    chunk = w.shape[1]

    def body(c_ref, dm_ref, w_ref, m_ref, v_ref, g_ref, d_ref, nm_ref, nv_ref):
        me = _lin(_my_pos())
        dm = jnp.zeros((N_DEV, chunk), F32)
        for j in range(N_DEV):
            dm = dm + dm_ref[:, j * chunk:(j + 1) * chunk] * jnp.where(me == j, 1.0, 0.0)
        g = lax.dot_general(_silu(c_ref[...]), dm, (((0,), (0,)), ((), ())), precision=HI,
                            preferred_element_type=F32)
        d, nm, nv = _adamw_math(w_ref[...], g, m_ref[...], v_ref[...])
        g_ref[...] = g
        d_ref[...] = d
        nm_ref[...] = nm
        nv_ref[...] = nv

    tr = 256
    blk = pl.BlockSpec((tr, chunk), _row)
    return pl.pallas_call(
        body, name="ada_w_update", grid=(w.shape[0] // tr,),
        in_specs=[pl.BlockSpec((N_DEV, tr), lambda i: (0, i)), pl.BlockSpec(dmod_all.shape, _fixed), blk, blk, blk],
        out_specs=[blk] * 4, out_shape=[jax.ShapeDtypeStruct(w.shape, F32)] * 4,
    )(c_all, dmod_all, w, m, v)


def _local_step(x, tgt, c, mod, w_in, conv_w, w_o_mine, w_gu_mine, w_d_mine, p):
    S = x.shape[0]
    tm = min(512, S)
    tmm = min(256, S)
    tw = min(2048, S)
    shift1, scale1, gate1, shift2, scale2, gate2 = [mod[i:i + 1] for i in range(6)]
    buckets = jnp.asarray(_t5_bucket_table())
    per_head = lambda a: jnp.broadcast_to(a.reshape(N_HEADS, 1), (N_HEADS, LANE))
    dtb_row, alog_row, d_exp = per_head(p["dt_bias"]), per_head(p["A_log"]), per_head(p["D_skip"])
    sinks = p["sinks"].reshape(N_HEADS)

    d_cut, gu_cut = WD_CUT, WGU_CUTS
    n_d, n_gu = w_d_mine.shape[0], w_gu_mine.shape[0]
    (qkv, z, xbc, dt_raw), (g_d_a,) = _in_proj_fwd(x, p["norm1"], scale1, shift1, w_in, tm,
                                                   ([(w_d_mine, 0, d_cut)], "two-level"))
    bias = _attn_bias(buckets, p["rel_bias"])
    (ya,), (g_gu_a,) = _attn_fwd(qkv, bias, sinks, ([(w_gu_mine, 0, gu_cut[0])], "two-level"))
    (ys, prev_states, pre_act), (g_gu_b, g_o) = _ssd_fwd(
        xbc, dt_raw, conv_w, p["conv_b"], dtb_row, alog_row, d_exp,
        ([(w_gu_mine, gu_cut[0], gu_cut[1] - gu_cut[0]), w_o_mine], "two-level"))
    w_o = g_o.reshape(D_MODEL, D_MODEL)
    x1, (g_gu_c, g_d_b) = _out_proj_fwd(
        x, ya, ys, z, p["attn_out_norm"], p["ssm_out_norm"], gate1, w_o, tm,
        ([(w_gu_mine, gu_cut[1], n_gu - gu_cut[1]), (w_d_mine, d_cut, n_d - d_cut)], "two-level"))
    dx1, h2, dgu, act, dmlp, acc2 = _mlp_loss(x1, tgt, p["norm2"], scale2, shift2, gate2, p["final_norm"],
                                              (g_gu_a, g_gu_b, g_gu_c), (g_d_a, g_d_b), tmm)
    g_w_gu = _wgrad(dgu, h2, 2 * D_FF // 4, tw, "wgrad_gate_up")
    g_w_d = _wgrad(act, dmlp, D_FF // 2, tw, "wgrad_down")
    gu_slots = g_w_gu.reshape(N_DEV, 2 * D_FF // N_DEV, D_MODEL)
    (dya, dys, dz, g_w_o, acc1), (r_gu_a,) = _out_proj_bwd(
        dx1, ya, ys, z, p["attn_out_norm"], p["ssm_out_norm"], gate1, w_o, tm, ([gu_slots], ("rows", 0, GGU_CUT)))
    (dq, dkv, dbias, dsk), (r_d, r_o) = _attn_bwd(
        qkv, ya, dya, bias, sinks,
        ([g_w_d.reshape(N_DEV, D_FF // N_DEV, D_MODEL), g_w_o.reshape(N_DEV, D_MODEL // N_DEV, D_MODEL)], True))
    drel, dsink = _attn_finish(dbias, dsk, buckets)
    (dxbc, ddt, dcw, dvec, dd), (r_gu_b, *early) = _ssd_bwd(
        xbc, pre_act, dt_raw, prev_states, dys, conv_w, dtb_row, alog_row, d_exp,
        [([gu_slots], ("rows", GGU_CUT, 2 * D_FF // N_DEV - GGU_CUT)), ([acc1, acc2, dsink, drel, c], False)])
    r_gu = (r_gu_a, r_gu_b)
    gx, h1, acc0, g_in_a = _in_proj_bwd(x, dx1, dq, dkv, dz, dxbc, ddt, p["norm1"], scale1, shift1, w_in, tm)
    half = D_MODEL // 2
    slots = lambda g: g[:IN_W].reshape(N_DEV, IN_W // N_DEV, half)
    g_in_b, (r_in_a,) = _wgrad((dq, dkv, dz, dxbc, ddt), h1, IN_PAD, tw, "wgrad_in_b",
                               ([slots(g_in_a)], "two-level scatter"), g_cols=(half, 1))
    return gx, (r_in_a, slots(g_in_b)), (r_o, r_gu, r_d), early, (acc0, dcw, dvec, dd)


def kernel(x, c, ada_w, ada_b, norm1, w_in, conv_w, conv_b, dt_bias, A_log, D_skip, sinks, attn_out_norm, ssm_out_norm, w_o, norm2, w_gate_up, w_down, rel_bias, final_norm, loss_target, m_ada_w, m_ada_b, m_norm1, m_w_in, m_conv_w, m_conv_b, m_dt_bias, m_A_log, m_D_skip, m_sinks, m_attn_out_norm, m_ssm_out_norm, m_w_o, m_norm2, m_w_gate_up, m_w_down, m_rel_bias, m_final_norm, v_ada_w, v_ada_b, v_norm1, v_w_in, v_conv_w, v_conv_b, v_dt_bias, v_A_log, v_D_skip, v_sinks, v_attn_out_norm, v_ssm_out_norm, v_w_o, v_norm2, v_w_gate_up, v_w_down, v_rel_bias, v_final_norm):
    two_d = lambda a: a if a.ndim == 2 else a.reshape(-1, a.shape[-1])
    small_params = dict(
        ada_b=(ada_b, m_ada_b, v_ada_b), norm1=(norm1, m_norm1, v_norm1), conv_w=(conv_w, m_conv_w, v_conv_w),
        conv_b=(conv_b, m_conv_b, v_conv_b), dt_bias=(dt_bias, m_dt_bias, v_dt_bias), A_log=(A_log, m_A_log, v_A_log),
        D_skip=(D_skip, m_D_skip, v_D_skip), sinks=(sinks, m_sinks, v_sinks),
        attn_out_norm=(attn_out_norm, m_attn_out_norm, v_attn_out_norm),
        ssm_out_norm=(ssm_out_norm, m_ssm_out_norm, v_ssm_out_norm), norm2=(norm2, m_norm2, v_norm2),
        rel_bias=(rel_bias, m_rel_bias, v_rel_bias), final_norm=(final_norm, m_final_norm, v_final_norm))
    small_params = {k: tuple(two_d(a) for a in v) for k, v in small_params.items()}
    S = x.shape[1]
    xs, tgt = x.reshape(S, D_MODEL), loss_target.reshape(S, D_MODEL)
    ada_w2 = ada_w[0]
    chunk = ada_w2.shape[1]
    t_in = [jnp.transpose(a[0]) for a in (w_in, m_w_in, v_w_in)]
    t_gu = [jnp.transpose(a[0]) for a in (w_gate_up, m_w_gate_up, v_w_gate_up)]

    mod, (g_in, g_cw) = _mod_and_gather(c, ada_w2, ada_b.reshape(N_DEV, chunk), [t_in[0].astype(WIRE_DTYPE), conv_w[0]])
    mod = mod.reshape(6, D_MODEL)
    w_in_full = jnp.pad(g_in.reshape(IN_W, D_MODEL), ((0, IN_PAD - IN_W), (0, 0)))
    conv_w_full = jnp.transpose(g_cw, (1, 0, 2)).reshape(4, XBC_W)

    p = {k: v[0] for k, v in small_params.items()}
    gx, (r_in_a, gw_in_b), (r_o, r_gu, r_d), early, late_blocks = _local_step(
        xs, tgt, c, mod, w_in_full, conv_w_full, w_o[0].astype(WIRE_DTYPE), t_gu[0].astype(WIRE_DTYPE),
        w_down[0].astype(WIRE_DTYPE), p)

    (u_gu, u_d, u_o), (r_in_b, *late) = _reduce_adamw_hosting(
        [r_gu, r_d, r_o], [tuple(t_gu), (w_down[0], m_w_down[0], v_w_down[0]), (w_o[0], m_w_o[0], v_w_o[0])],
        "adamw_big", [([gw_in_b], "two-level scatter"), (list(late_blocks), False)])
    gathered = (late[0], early[0], early[1], late[1], late[2], late[3], early[2], early[3], early[4])

    small, loss, c_all, dmod_all = _small_update(gathered, small_params)

    big = {
        "ada_w": _ada_w_update(c_all, dmod_all, ada_w2, m_ada_w[0], v_ada_w[0]),
        "w_in": [jnp.transpose(a) for a in _reduce_adamw_halves(r_in_a, r_in_b, *t_in, "adamw_w_in")],
        "w_o": u_o,
        "w_gate_up": [jnp.transpose(a) for a in u_gu],
        "w_down": u_d,
    }
    big.update(small)

    order = ['ada_w', 'ada_b', 'norm1', 'w_in', 'conv_w', 'conv_b', 'dt_bias', 'A_log', 'D_skip', 'sinks',
             'attn_out_norm', 'ssm_out_norm', 'w_o', 'norm2', 'w_gate_up', 'w_down', 'rel_bias', 'final_norm']
    shapes = dict(ada_w=ada_w.shape, ada_b=ada_b.shape, norm1=norm1.shape, w_in=w_in.shape, conv_w=conv_w.shape,
                  conv_b=conv_b.shape, dt_bias=dt_bias.shape, A_log=A_log.shape, D_skip=D_skip.shape,
                  sinks=sinks.shape, attn_out_norm=attn_out_norm.shape, ssm_out_norm=ssm_out_norm.shape,
                  w_o=w_o.shape, norm2=norm2.shape, w_gate_up=w_gate_up.shape, w_down=w_down.shape,
                  rel_bias=rel_bias.shape, final_norm=final_norm.shape)
    outs = [[], [], [], []]
    for name in order:
        for kind in range(4):
            outs[kind].append(big[name][kind].reshape(shapes[name]))
    return (loss.reshape(()), gx.reshape(x.shape), *outs[0], *outs[1], *outs[2], *outs[3])
```

```python
import numpy as np
import jax
import jax.numpy as jnp
from jax import lax
from jax.experimental import pallas as pl
from jax.experimental.pallas import tpu as pltpu

F32 = jnp.float32
MXU_DTYPE = jnp.bfloat16
WIRE_DTYPE = jnp.bfloat16
HI = lax.Precision.HIGHEST
MESH = pl.DeviceIdType.MESH
N_DEV = 8

D_MODEL = 1024
ATTN_W = 512
KV_W = 128
SSM_W = 512
XBC_W = 1024
N_HEADS = 8
D_STATE = 128
D_FF = 2816
IN_W = 2312
IN_PAD = 2432
BLK = 128
N_BUCKETS = 32
EPS = 1e-6
LANE = 128
HALF = 64

ADAM_LR, ADAM_B1, ADAM_B2, ADAM_EPS, ADAM_WD, ADAM_STEP = 0.001, 0.9, 0.999, 1e-08, 0.01, 10

VMEM_BIG = 56 * 1024 * 1024
WD_CUT = 288
WGU_CUTS = (240, 496)
GGU_CUT = 304


def _cparams(vmem=None):
    if vmem is None:
        return pltpu.CompilerParams()
    return pltpu.CompilerParams(vmem_limit_bytes=vmem)


def _mm(a, b):
    return jnp.dot(a.astype(MXU_DTYPE), b.astype(MXU_DTYPE), preferred_element_type=F32)


def _mm_nt(a, b):
    return lax.dot_general(a.astype(MXU_DTYPE), b.astype(MXU_DTYPE), (((1,), (1,)), ((), ())),
                           preferred_element_type=F32)


def _mm_tn(a, b):
    return lax.dot_general(a.astype(MXU_DTYPE), b.astype(MXU_DTYPE), (((0,), (0,)), ((), ())),
                           preferred_element_type=F32)


def _mm_hi(a, b):
    return jnp.dot(a, b, precision=HI, preferred_element_type=F32)


def _silu(x):
    return x * jax.nn.sigmoid(x)


def _softplus(x):
    return jnp.maximum(x, 0.0) + jnp.log1p(jnp.exp(-jnp.abs(x)))


def _rms(x, g, n):
    return x * lax.rsqrt(jnp.sum(x * x, axis=-1, keepdims=True) * (1.0 / n) + EPS) * g


def _modnorm(x, g, scale, shift):
    return _rms(x, g, x.shape[-1]) * (1.0 + scale) + shift


def _modnorm_parts(x):
    r = lax.rsqrt(jnp.sum(x * x, axis=-1, keepdims=True) * (1.0 / x.shape[-1]) + EPS)
    return r, x * r


def _modnorm_bwd(r, xhat, g, scale, dy):
    dyg = dy * (g * (1.0 + scale))
    c = jnp.sum(dyg * xhat, axis=-1, keepdims=True) * (1.0 / xhat.shape[-1])
    dx = r * (dyg - xhat * c)
    ct = jnp.sum(dy * xhat, axis=0, keepdims=True)
    return dx, ct * (1.0 + scale), ct * g, jnp.sum(dy, axis=0, keepdims=True)


def _lane_iota(shape):
    return lax.broadcasted_iota(jnp.int32, shape, len(shape) - 1)


def _split_pair(t):
    lane = _lane_iota(t.shape)
    lo = jnp.where(lane < HALF, t, 0.0)
    hi = pltpu.roll(jnp.where(lane >= HALF, t, 0.0), HALF, 1)
    return lo, hi


def _join_pair(lo, hi):
    lane = _lane_iota(lo.shape)
    return jnp.where(lane < HALF, lo, pltpu.roll(hi, HALF, 1))


def _split_heads(t, n_pairs):
    out = []
    for p in range(n_pairs):
        out.extend(_split_pair(t[:, p * LANE:(p + 1) * LANE]))
    return out


def _join_heads(hs):
    return jnp.concatenate([_join_pair(hs[2 * p], hs[2 * p + 1]) for p in range(len(hs) // 2)], axis=1)


def _t5_bucket_table():
    dist = np.arange(BLK)[:, None] + BLK - np.arange(2 * BLK)[None, :]
    n = np.maximum(dist, 0)
    max_exact = N_BUCKETS // 2
    large = max_exact + (np.log(np.maximum(n, 1) / max_exact) / np.log(128 / max_exact)
                         * (N_BUCKETS - max_exact)).astype(np.int32)
    large = np.minimum(large, N_BUCKETS - 1)
    return np.where(n < max_exact, n, large).astype(np.int32)


def _my_pos():
    return lax.axis_index("x"), lax.axis_index("y"), lax.axis_index("c")


def _peer(k):
    x, y, c = _my_pos()
    return (1 - x if k & 4 else x, 1 - y if k & 2 else y, 1 - c if k & 1 else c)


def _lin(pos):
    return 4 * pos[0] + 2 * pos[1] + pos[2]


def _xchg_copies(ins, outs, sems, scatter):
    local_sem, send_sem, recv_sem = sems
    me = _lin(_my_pos())

    def source(a, slot):
        if not scatter:
            return ins[a]
        if scatter is True:
            return ins[a].at[slot]
        return ins[a].at[slot, pl.ds(scatter[1], scatter[2])]

    local, remote = [], []
    for a in range(len(ins)):
        local.append(pltpu.make_async_copy(source(a, me), outs[a].at[me], local_sem.at[a]))
    for k in range(1, N_DEV):
        peer = _peer(k)
        for a in range(len(ins)):
            remote.append(pltpu.make_async_remote_copy(source(a, _lin(peer)), outs[a].at[me], send_sem.at[a, k - 1],
                                                       recv_sem.at[a, k - 1], device_id=peer, device_id_type=MESH))
    return local, remote


def _xchg_start(ins, outs, sems, scatter):
    local, remote = _xchg_copies(ins, outs, sems, scatter)
    for cp in local + remote:
        cp.start()


def _xchg_wait(ins, outs, sems, scatter):
    local, remote = _xchg_copies(ins, outs, sems, scatter)
    for cp in local:
        cp.wait()
    for cp in remote:
        cp.wait_send()
        cp.wait_recv()


def _xchg_shapes(arrs, scatter):
    n = len(arrs)
    if isinstance(scatter, tuple):
        out_shape = [jax.ShapeDtypeStruct((a.shape[0], scatter[2]) + a.shape[2:], a.dtype) for a in arrs]
    elif scatter:
        out_shape = [jax.ShapeDtypeStruct(a.shape, a.dtype) for a in arrs]
    else:
        out_shape = [jax.ShapeDtypeStruct((N_DEV,) + a.shape, a.dtype) for a in arrs]
    sems = [pltpu.SemaphoreType.DMA((n,)), pltpu.SemaphoreType.DMA((n, N_DEV - 1)),
            pltpu.SemaphoreType.DMA((n, N_DEV - 1))]
    return out_shape, sems


_CHIPS = (2, 4, 6)


def _g2_sems(n):
    dma = pltpu.SemaphoreType.DMA
    return [dma((n,)), dma((n, N_DEV)), dma((n, N_DEV)), dma((n, len(_CHIPS))), dma((n, len(_CHIPS)))]


class _TwoLevelGather:
    def __init__(self, ins, outs, sems, windows=None):
        self.ins, self.outs = ins, outs
        self.local_sem, self.send_sem, self.recv_sem, self.fsend_sem, self.frecv_sem = sems
        self.n = len(ins)
        self.windows = windows or [None] * self.n

    def _mine(self, a):
        w = self.windows[a]
        return self.ins[a] if w is None else self.ins[a].at[pl.ds(w[0], w[1])]

    def _direct(self, a, k):
        return pltpu.make_async_remote_copy(self._mine(a), self.outs[a].at[_lin(_my_pos())], self.send_sem.at[a, k],
                                            self.recv_sem.at[a, k], device_id=_peer(k), device_id_type=MESH)

    def _handed_on(self, a, j, origin):
        slot = self.outs[a].at[origin]
        return pltpu.make_async_remote_copy(slot, slot, self.fsend_sem.at[a, j], self.frecv_sem.at[a, j],
                                            device_id=_peer(1), device_id_type=MESH)

    def _local(self, a):
        return pltpu.make_async_copy(self._mine(a), self.outs[a].at[_lin(_my_pos())], self.local_sem.at[a])

    def start(self):
        for a in range(self.n):
            self._local(a).start()
        for k in (1,) + _CHIPS:
            for a in range(self.n):
                self._direct(a, k).start()

    def forward(self):
        for j, k in enumerate(_CHIPS):
            for a in range(self.n):
                self._direct(a, k).wait_recv()
                self._handed_on(a, j, _lin(_peer(k))).start()

    def finish(self):
        for a in range(self.n):
            self._direct(a, 1).wait_recv()
            for j, k in enumerate(_CHIPS):
                self._handed_on(a, j, _lin(_peer(k ^ 1))).wait_recv()
            self._local(a).wait()
            for k in (1,) + _CHIPS:
                self._direct(a, k).wait_send()
            for j, k in enumerate(_CHIPS):
                self._handed_on(a, j, _lin(_peer(k))).wait_send()


N_SCATTERED = 2 + len(_CHIPS)


class _TwoLevelScatter:
    def __init__(self, src, out, stage, buf_a, buf_b, sems):
        self.src, self.out, self.stage, self.buf_a, self.buf_b = src, out, stage, buf_a, buf_b
        self.local_sem, self.dsend, self.drecv, self.csend, self.crecv = sems

    def _own(self):
        return pltpu.make_async_copy(self.src.at[_lin(_my_pos())], self.out.at[0], self.local_sem.at[0])

    def _to_core(self, j):
        q = 0 if j == 0 else _CHIPS[j - 1]
        dst = self.out.at[1] if j == 0 else self.stage.at[j - 1]
        return pltpu.make_async_remote_copy(self.src.at[_lin(_peer(q ^ 1))], dst, self.dsend.at[j], self.drecv.at[j],
                                            device_id=_peer(1), device_id_type=MESH)

    def _loads(self, j):
        mine = self.src.at[_lin(_peer(_CHIPS[j]))]
        return (pltpu.make_async_copy(self.stage.at[j], self.buf_a.at[j], self.local_sem.at[1 + 2 * j]),
                pltpu.make_async_copy(mine, self.buf_b.at[j], self.local_sem.at[2 + 2 * j]))

    def _to_chip(self, j):
        return pltpu.make_async_remote_copy(self.buf_a.at[j], self.out.at[2 + j], self.csend.at[j], self.crecv.at[j],
                                            device_id=_peer(_CHIPS[j]), device_id_type=MESH)

    def start(self):
        self._own().start()
        for j in range(1 + len(_CHIPS)):
            self._to_core(j).start()

    def forward(self):
        for j in range(len(_CHIPS)):
            self._to_core(j + 1).wait_recv()
            for cp in self._loads(j):
                cp.start()
        for j in range(len(_CHIPS)):
            for cp in self._loads(j):
                cp.wait()
            self.buf_a[j] = (self.buf_a[j].astype(F32) + self.buf_b[j].astype(F32)).astype(self.buf_a.dtype)
            self._to_chip(j).start()

    def finish(self):
        self._own().wait()
        self._to_core(0).wait_recv()
        for j in range(1 + len(_CHIPS)):
            self._to_core(j).wait_send()
        for j in range(len(_CHIPS)):
            self._to_chip(j).wait_send()
            self._to_chip(j).wait_recv()


def _s2_shapes(a):
    dma = pltpu.SemaphoreType.DMA
    n_c = len(_CHIPS)
    piece = a.shape[1:]
    return (jax.ShapeDtypeStruct((N_SCATTERED,) + piece, a.dtype), jax.ShapeDtypeStruct((n_c,) + piece, a.dtype),
            [pltpu.VMEM((n_c,) + piece, a.dtype)] * 2,
            [dma((1 + 2 * n_c,)), dma((1 + n_c,)), dma((1 + n_c,)), dma((n_c,)), dma((n_c,))])


def _mod_and_gather(c, ada_w, ada_b8, arrs):
    n = len(arrs)
    chunk = ada_w.shape[1]
    out_shape = [jax.ShapeDtypeStruct((N_DEV, 1, chunk), F32)]
    out_shape += [jax.ShapeDtypeStruct((N_DEV,) + a.shape, a.dtype) for a in arrs]

    def modulation(c_ref, w_ref, b_ref, out_ref, cbuf, part, s1, r1, s2, r2):
        me = _lin(_my_pos())
        first = []
        for k in range(1, N_DEV):
            cp = pltpu.make_async_remote_copy(c_ref, cbuf.at[me], s1.at[k - 1], r1.at[k - 1],
                                              device_id=_peer(k), device_id_type=MESH)
            cp.start()
            first.append(cp)
        cbuf[me] = c_ref[...]
        for cp in first:
            cp.wait_send()
            cp.wait_recv()
        cond = _silu(jnp.concatenate([cbuf[i] for i in range(N_DEV)], axis=0))
        mod = _mm_hi(cond, w_ref[...]) + b_ref[pl.ds(me, 1), :]
        for j in range(N_DEV):
            part[j] = mod[j:j + 1, :]
        second = []
        for k in range(1, N_DEV):
            peer = _peer(k)
            cp = pltpu.make_async_remote_copy(part.at[_lin(peer)], out_ref.at[me], s2.at[k - 1], r2.at[k - 1],
                                              device_id=peer, device_id_type=MESH)
            cp.start()
            second.append(cp)
        out_ref[me] = part[me]
        for cp in second:
            cp.wait_send()
            cp.wait_recv()

    def body(*refs):
        c_ref, w_ref, b_ref = refs[:3]
        ins = refs[3:3 + n]
        mod_ref = refs[3 + n]
        outs = refs[4 + n:4 + 2 * n]
        cbuf, part, s1, r1, s2, r2 = refs[4 + 2 * n:10 + 2 * n]
        gather = _TwoLevelGather(ins, outs, refs[10 + 2 * n:])
        gather.start()
        modulation(c_ref, w_ref, b_ref, mod_ref, cbuf, part, s1, r1, s2, r2)
        gather.forward()
        gather.finish()

    hbm = pl.BlockSpec(memory_space=pltpu.HBM)
    vm = pl.BlockSpec(memory_space=pltpu.VMEM)
    dma = pltpu.SemaphoreType.DMA
    res = pl.pallas_call(
        body, name="mod_and_gather", out_shape=out_shape, in_specs=[vm, vm, vm] + [hbm] * n,
        out_specs=[vm] + [hbm] * n,
        scratch_shapes=[pltpu.VMEM((N_DEV, 1, D_MODEL), F32), pltpu.VMEM((N_DEV, 1, chunk), F32)]
        + [dma((N_DEV - 1,))] * 4 + _g2_sems(n),
    )(c, ada_w, ada_b8, *arrs)
    return res[0], res[1:]


def _hosted_call(body, name, grid, in_specs, out_specs, out_shape, scratch_shapes, args, xchg, cparams):
    xchgs = [xchg] if isinstance(xchg, tuple) else list(xchg)
    grid = (grid,) if isinstance(grid, int) else tuple(grid)
    n_in, n_out, n_scr = len(in_specs), len(out_specs), len(scratch_shapes)
    windows = [[(a[1], a[2]) if isinstance(a, tuple) else None for a in group] for group, _ in xchgs]
    xchgs = [([a[0] if isinstance(a, tuple) else a for a in group], mode) for group, mode in xchgs]
    arrs = [a for group, _ in xchgs for a in group]
    n = len(arrs)
    x_shape, x_sems, sem_counts, stage_shape, stage_bufs = [], [], [], [], []
    for (group, mode), wins in zip(xchgs, windows):
        if mode == "two-level scatter":
            (a,) = group
            res_shape, stage, bufs, sems = _s2_shapes(a)
            shapes = [res_shape]
            stage_shape.append(stage)
            stage_bufs += bufs
        else:
            shapes, sems = _xchg_shapes(group, False if mode == "two-level" else mode)
        if mode == "two-level":
            sems = _g2_sems(len(group))
            shapes = [s if w is None else jax.ShapeDtypeStruct((N_DEV, w[1]) + a.shape[1:], a.dtype)
                      for s, w, a in zip(shapes, wins, group)]
        x_shape += shapes
        x_sems += sems
        sem_counts.append(len(sems))
    n_stage = len(stage_shape)
    n_steps = int(np.prod(grid))
    staged = ("two-level", "two-level scatter")

    def hosted(*refs):
        ins, refs = refs[:n_in], refs[n_in:]
        x_in, refs = refs[:n], refs[n:]
        outs, refs = refs[:n_out], refs[n_out:]
        x_out, refs = refs[:n], refs[n:]
        stages, refs = refs[:n_stage], refs[n_stage:]
        scr, refs = refs[:n_scr], refs[n_scr:]
        bufs, sems = refs[:2 * n_stage], refs[2 * n_stage:]
        step = pl.program_id(0)
        for d in range(1, len(grid)):
            step = step * grid[d] + pl.program_id(d)
        parts, a0, s0, t0 = [], 0, 0, 0
        for (group, mode), ns, wins in zip(xchgs, sem_counts, windows):
            gi, go, gs = x_in[a0:a0 + len(group)], x_out[a0:a0 + len(group)], sems[s0:s0 + ns]
            if mode == "two-level":
                parts.append((mode, _TwoLevelGather(gi, go, gs, wins)))
            elif mode == "two-level scatter":
                parts.append((mode, _TwoLevelScatter(gi[0], go[0], stages[t0], bufs[2 * t0], bufs[2 * t0 + 1], gs)))
                t0 += 1
            else:
                parts.append((mode, (gi, go, gs, mode)))
            a0, s0 = a0 + len(group), s0 + ns

        @pl.when(step == 0)
        def _():
            for mode, x in parts:
                if mode in staged:
                    x.start()
                else:
                    _xchg_start(*x)

        for kind, at in (("two-level", (2 * n_steps) // 3), ("two-level scatter", 0)):
            if any(mode == kind for mode, _ in parts):
                @pl.when(step == at)
                def _():
                    for mode, x in parts:
                        if mode == kind:
                            x.forward()

        body(*ins, *outs, *scr)

        @pl.when(step == n_steps - 1)
        def _():
            for mode, x in parts:
                if mode in staged:
                    x.finish()
                else:
                    _xchg_wait(*x)

    hbm = pl.BlockSpec(memory_space=pltpu.HBM)
    res = pl.pallas_call(
        hosted, name=name, grid=grid, in_specs=list(in_specs) + [hbm] * n,
        out_specs=list(out_specs) + [hbm] * (n + n_stage), out_shape=list(out_shape) + x_shape + stage_shape,
        scratch_shapes=list(scratch_shapes) + stage_bufs + x_sems, compiler_params=cparams,
    )(*args, *arrs)
    return res[:n_out], res[n_out:n_out + n]


def _row(i):
    return (i, 0)


def _fixed(i):
    return (0, 0)


def _in_proj_fwd(x, norm1, scale1, shift1, w_in, tm, xchg):
    S = x.shape[0]

    def body(x_ref, n_ref, sc_ref, sh_ref, w_ref, qkv_ref, z_ref, xbc_ref, dt_ref):
        h = _modnorm(x_ref[...], n_ref[...], sc_ref[...], sh_ref[...])
        p = _mm_nt(h, w_ref[...])
        qkv_ref[...] = p[:, :768].astype(qkv_ref.dtype)
        z_ref[...] = p[:, 768:1280]
        xbc_ref[...] = p[:, 1280:2304]
        dt_ref[...] = p[:, 2304:IN_PAD]

    vec = pl.BlockSpec((1, D_MODEL), _fixed)
    return _hosted_call(
        body, "in_proj_fwd", S // tm,
        in_specs=[pl.BlockSpec((tm, D_MODEL), _row), vec, vec, vec, pl.BlockSpec((IN_PAD, D_MODEL), _fixed)],
        out_specs=[pl.BlockSpec((tm, 768), _row), pl.BlockSpec((tm, SSM_W), _row),
                   pl.BlockSpec((tm, XBC_W), _row), pl.BlockSpec((tm, LANE), _row)],
        out_shape=[jax.ShapeDtypeStruct((S, 768), MXU_DTYPE), jax.ShapeDtypeStruct((S, SSM_W), F32),
                   jax.ShapeDtypeStruct((S, XBC_W), F32), jax.ShapeDtypeStruct((S, LANE), F32)],
        scratch_shapes=[], args=(x, norm1, scale1, shift1, w_in), xchg=xchg, cparams=_cparams(VMEM_BIG),
    )


def _in_proj_bwd(x, dx1, dq, dkv, dz, dxbc, ddt, norm1, scale1, shift1, w_in, tm):
    S = x.shape[0]

    n_steps = S // tm
    half_cols = D_MODEL // 2

    def body(x_ref, dx1_ref, dq_ref, dkv_ref, dz_ref, dxbc_ref, ddt_ref, n_ref, sc_ref, sh_ref, w_ref,
             gx_ref, h_ref, acc_ref, gw_ref, gw_acc):
        i = pl.program_id(0)

        @pl.when(i == 0)
        def _():
            acc_ref[...] = jnp.zeros_like(acc_ref)
            gw_acc[...] = jnp.zeros_like(gw_acc)

        halves = [pl.ds(k * (tm // 2), tm // 2) for k in range(2)]
        dp = [jnp.concatenate([r[rows, :] for r in (dq_ref, dkv_ref, dz_ref, dxbc_ref, ddt_ref)], axis=1)
              for rows in halves]
        dh = [_mm(dp[k], w_ref[...]) for k in range(2)]
        parts = [_modnorm_parts(x_ref[rows, :]) for rows in halves]
        hb = [(parts[k][1] * n_ref[...] * (1.0 + sc_ref[...]) + sh_ref[...]).astype(h_ref.dtype) for k in range(2)]
        gw_acc[...] += _mm_tn(dp[0], hb[0][:, :half_cols]) + _mm_tn(dp[1], hb[1][:, :half_cols])
        bwd = [_modnorm_bwd(parts[k][0], parts[k][1], n_ref[...], sc_ref[...], dh[k]) for k in range(2)]
        for k, rows in enumerate(halves):
            gx_ref[rows, :] = dx1_ref[rows, :] + bwd[k][0]
            h_ref[rows, :] = hb[k]
        acc_ref[0:1, :] += bwd[0][1] + bwd[1][1]
        acc_ref[1:2, :] += bwd[0][2] + bwd[1][2]
        acc_ref[2:3, :] += bwd[0][3] + bwd[1][3]

        @pl.when(i == n_steps - 1)
        def _():
            gw_ref[...] = gw_acc[...].astype(gw_ref.dtype)

    vec = pl.BlockSpec((1, D_MODEL), _fixed)
    return pl.pallas_call(
        body, name="in_proj_bwd", grid=(n_steps,),
        in_specs=[pl.BlockSpec((tm, D_MODEL), _row), pl.BlockSpec((tm, D_MODEL), _row),
                  pl.BlockSpec((tm, ATTN_W), _row), pl.BlockSpec((tm, 2 * KV_W), _row),
                  pl.BlockSpec((tm, SSM_W), _row), pl.BlockSpec((tm, XBC_W), _row), pl.BlockSpec((tm, LANE), _row),
                  vec, vec, vec, pl.BlockSpec((IN_PAD, D_MODEL), _fixed)],
        out_specs=[pl.BlockSpec((tm, D_MODEL), _row), pl.BlockSpec((tm, D_MODEL), _row),
                   pl.BlockSpec((8, D_MODEL), _fixed), pl.BlockSpec((IN_PAD, half_cols), _fixed)],
        out_shape=[jax.ShapeDtypeStruct((S, D_MODEL), F32), jax.ShapeDtypeStruct((S, D_MODEL), MXU_DTYPE),
                   jax.ShapeDtypeStruct((8, D_MODEL), F32), jax.ShapeDtypeStruct((IN_PAD, half_cols), WIRE_DTYPE)],
        scratch_shapes=[pltpu.VMEM((IN_PAD, half_cols), F32)],
        compiler_params=_cparams(VMEM_BIG),
    )(x, dx1, dq, dkv, dz, dxbc, ddt, norm1, scale1, shift1, w_in)


def _out_stage(ya, ys0, ys1, z0, z1, an, sn0, sn1):
    half = SSM_W // 2
    a = _rms(ya, an, ATTN_W)
    g0 = _rms(ys0 * _silu(z0), sn0, half)
    g1 = _rms(ys1 * _silu(z1), sn1, half)
    return jnp.concatenate([a, g0, g1], axis=1)


def _out_stage_args(ya_ref, ys_ref, z_ref, an_ref, sn_ref):
    half = SSM_W // 2
    return (ya_ref[...], ys_ref[:, :half], ys_ref[:, half:], z_ref[:, :half], z_ref[:, half:],
            an_ref[...], sn_ref[:, :half], sn_ref[:, half:])


def _out_proj_fwd(x, ya, ys, z, an, sn, gate1, w_o, tm, xchg):
    S = x.shape[0]

    def body(x_ref, ya_ref, ys_ref, z_ref, an_ref, sn_ref, g_ref, w_ref, x1_ref):
        u = _out_stage(*_out_stage_args(ya_ref, ys_ref, z_ref, an_ref, sn_ref))
        x1_ref[...] = x_ref[...] + g_ref[...] * _mm(u, w_ref[...])

    half = pl.BlockSpec((tm, ATTN_W), _row)
    hvec = pl.BlockSpec((1, ATTN_W), _fixed)
    (x1,), x_out = _hosted_call(
        body, "out_proj_fwd", S // tm,
        in_specs=[pl.BlockSpec((tm, D_MODEL), _row), half, half, half, hvec, hvec,
                  pl.BlockSpec((1, D_MODEL), _fixed), pl.BlockSpec((D_MODEL, D_MODEL), _fixed)],
        out_specs=[pl.BlockSpec((tm, D_MODEL), _row)],
        out_shape=[jax.ShapeDtypeStruct((S, D_MODEL), F32)],
        scratch_shapes=[], args=(x, ya, ys, z, an, sn, gate1, w_o), xchg=xchg, cparams=_cparams(VMEM_BIG),
    )
    return x1, x_out


def _out_proj_bwd(dx1, ya, ys, z, an, sn, gate1, w_o, tm, xchg):
    S = dx1.shape[0]
    n_steps = S // tm

    def body(dx1_ref, ya_ref, ys_ref, z_ref, an_ref, sn_ref, g_ref, w_ref,
             dya_ref, dys_ref, dz_ref, gw_ref, acc_ref, gw_acc):
        i = pl.program_id(0)

        @pl.when(i == 0)
        def _():
            acc_ref[...] = jnp.zeros_like(acc_ref)
            gw_acc[...] = jnp.zeros_like(gw_acc)

        u, vjp = jax.vjp(_out_stage, *_out_stage_args(ya_ref, ys_ref, z_ref, an_ref, sn_ref))
        dx1 = dx1_ref[...]
        ub = u.astype(MXU_DTYPE)
        mix = _mm(ub, w_ref[...])
        dmix = dx1 * g_ref[...]
        dmixb = dmix.astype(MXU_DTYPE)
        du = _mm_nt(dmixb, w_ref[...])
        gw_acc[...] += _mm_tn(ub, dmixb)
        dya, dys0, dys1, dz0, dz1, dan, dsn0, dsn1 = vjp(du)
        dya_ref[...] = dya
        dys_ref[...] = jnp.concatenate([dys0, dys1], axis=1)
        dz_ref[...] = jnp.concatenate([dz0, dz1], axis=1).astype(dz_ref.dtype)
        acc_ref[0:1, :] += jnp.sum(dx1 * mix, axis=0, keepdims=True)
        acc_ref[1:2, :] += jnp.concatenate([dan, dsn0, dsn1], axis=1)

        @pl.when(i == n_steps - 1)
        def _():
            gw_ref[...] = gw_acc[...].astype(gw_ref.dtype)

    half = pl.BlockSpec((tm, ATTN_W), _row)
    hvec = pl.BlockSpec((1, ATTN_W), _fixed)
    full = pl.BlockSpec((tm, D_MODEL), _row)
    return _hosted_call(
        body, "out_proj_bwd", n_steps,
        in_specs=[full, half, half, half, hvec, hvec,
                  pl.BlockSpec((1, D_MODEL), _fixed), pl.BlockSpec((D_MODEL, D_MODEL), _fixed)],
        out_specs=[half, half, half, pl.BlockSpec((D_MODEL, D_MODEL), _fixed), pl.BlockSpec((8, D_MODEL), _fixed)],
        out_shape=[jax.ShapeDtypeStruct((S, ATTN_W), F32)] * 2 + [jax.ShapeDtypeStruct((S, ATTN_W), MXU_DTYPE),
                   jax.ShapeDtypeStruct((D_MODEL, D_MODEL), WIRE_DTYPE), jax.ShapeDtypeStruct((8, D_MODEL), F32)],
        scratch_shapes=[pltpu.VMEM((D_MODEL, D_MODEL), F32)],
        args=(dx1, ya, ys, z, an, sn, gate1, w_o), xchg=xchg, cparams=_cparams(VMEM_BIG),
    )


def _loss_rows(x2, fn, tgt):
    y = _rms(x2, fn, D_MODEL)
    per_row = jnp.sum(jnp.square(y - tgt), axis=1, keepdims=True)
    return jnp.sum(per_row, axis=0, keepdims=True) * (0.5 / D_MODEL)


def _mlp_loss(x1, tgt, norm2, scale2, shift2, gate2, fnorm, w_gu, w_d, tm):
    S = x1.shape[0]
    n_pieces = len(w_gu) + len(w_d)

    def body(*refs):
        x1_ref, t_ref, n_ref, sc_ref, sh_ref, g_ref, fn_ref = refs[:7]
        piece_refs = refs[7:7 + n_pieces]
        dx1_ref, h_ref, dgu_ref, act_ref, dmlp_ref, acc_ref, wgu, wd, wsem = refs[7 + n_pieces:]

        @pl.when(pl.program_id(0) == 0)
        def _():
            acc_ref[...] = jnp.zeros_like(acc_ref)
            copies = []
            for dst, pieces in ((wgu, piece_refs[:len(w_gu)]), (wd, piece_refs[len(w_gu):])):
                shard = sum(p.shape[1] for p in pieces)
                off = 0
                for p in pieces:
                    for j in range(N_DEV):
                        copies.append(pltpu.make_async_copy(p.at[j], dst.at[pl.ds(j * shard + off, p.shape[1])],
                                                            wsem.at[len(copies)]))
                    off += p.shape[1]
            for cp in copies:
                cp.start()
            for cp in copies:
                cp.wait()

        x1 = x1_ref[...]
        gate2 = g_ref[...]
        h, vjp_h = jax.vjp(_modnorm, x1, n_ref[...], sc_ref[...], sh_ref[...])
        hb = h.astype(MXU_DTYPE)
        gu = _mm_nt(hb, wgu[...])
        g, u = gu[:, :D_FF], gu[:, D_FF:]
        sg = jax.nn.sigmoid(g)
        silu_g = g * sg
        act = (silu_g * u).astype(MXU_DTYPE)
        mlp = _mm(act, wd[...])
        x2 = x1 + gate2 * mlp
        loss, vjp_loss = jax.vjp(_loss_rows, x2, fn_ref[...], t_ref[...])
        dx2, dfn, _ = vjp_loss(jnp.ones((1, 1), F32))
        dmlp = (dx2 * gate2).astype(MXU_DTYPE)
        dact = _mm_nt(dmlp, wd[...])
        dg = dact * u * (sg * (1.0 + g * (1.0 - sg)))
        du = dact * silu_g
        dgu = jnp.concatenate([dg, du], axis=1).astype(MXU_DTYPE)
        dh = _mm(dgu, wgu[...])
        dx, dn, dsc, dsh = vjp_h(dh)
        dx1_ref[...] = dx2 + dx
        h_ref[...] = hb
        dgu_ref[...] = dgu
        act_ref[...] = act
        dmlp_ref[...] = dmlp
        acc_ref[0:1, :] += dn
        acc_ref[1:2, :] += dsc
        acc_ref[2:3, :] += dsh
        acc_ref[3:4, :] += jnp.sum(dx2 * mlp, axis=0, keepdims=True)
        acc_ref[4:5, :] += dfn
        acc_ref[5:6, :] += jnp.broadcast_to(loss, (1, D_MODEL))

    full = pl.BlockSpec((tm, D_MODEL), _row)
    vec = pl.BlockSpec((1, D_MODEL), _fixed)
    anyspec = pl.BlockSpec(memory_space=pl.ANY)
    return pl.pallas_call(
        body, name="mlp_loss", grid=(S // tm,),
        in_specs=[full, full, vec, vec, vec, vec, vec] + [anyspec] * n_pieces,
        out_specs=[full, full, pl.BlockSpec((tm, 2 * D_FF), _row), pl.BlockSpec((tm, D_FF), _row), full,
                   pl.BlockSpec((8, D_MODEL), _fixed)],
        out_shape=[jax.ShapeDtypeStruct((S, D_MODEL), F32), jax.ShapeDtypeStruct((S, D_MODEL), MXU_DTYPE),
                   jax.ShapeDtypeStruct((S, 2 * D_FF), MXU_DTYPE), jax.ShapeDtypeStruct((S, D_FF), MXU_DTYPE),
                   jax.ShapeDtypeStruct((S, D_MODEL), MXU_DTYPE), jax.ShapeDtypeStruct((8, D_MODEL), F32)],
        scratch_shapes=[pltpu.VMEM((2 * D_FF, D_MODEL), MXU_DTYPE), pltpu.VMEM((D_FF, D_MODEL), MXU_DTYPE),
                        pltpu.SemaphoreType.DMA((N_DEV * n_pieces,))],
        compiler_params=_cparams(VMEM_BIG),
    )(x1, tgt, norm2, scale2, shift2, gate2, fnorm, *w_gu, *w_d)


def _wgrad(a, g, tk, ts, name, xchg=None, g_cols=None):
    pieces = list(a) if isinstance(a, (list, tuple)) else [a]
    S = pieces[0].shape[0]
    K = sum(p.shape[1] for p in pieces)
    assert len(pieces) == 1 or tk == K
    N, col = (g.shape[1], 0) if g_cols is None else g_cols
    ns = S // ts
    n_a = len(pieces)

    def body(*refs):
        a_refs, (g_ref, o_ref, acc_ref) = refs[:n_a], refs[n_a:]
        s = pl.program_id(1)

        @pl.when(s == 0)
        def _():
            acc_ref[...] = jnp.zeros_like(acc_ref)

        a_blk = a_refs[0][...] if n_a == 1 else jnp.concatenate([r[...] for r in a_refs], axis=1)
        acc_ref[...] += _mm_tn(a_blk, g_ref[...])

        @pl.when(s == ns - 1)
        def _():
            o_ref[...] = acc_ref[...].astype(o_ref.dtype)

    if n_a == 1:
        in_specs = [pl.BlockSpec((ts, tk), lambda j, s: (s, j))]
    else:
        in_specs = [pl.BlockSpec((ts, p.shape[1]), lambda j, s: (s, 0)) for p in pieces]
    in_specs.append(pl.BlockSpec((ts, N), lambda j, s: (s, col)))
    out_spec = pl.BlockSpec((tk, N), lambda j, s: (j, 0))
    out_shape = jax.ShapeDtypeStruct((K, N), WIRE_DTYPE)
    scratch = [pltpu.VMEM((tk, N), F32)]
    args = (*pieces, g)
    if xchg is None:
        return pl.pallas_call(body, name=name, grid=(K // tk, ns), in_specs=in_specs, out_specs=out_spec,
                              out_shape=out_shape, scratch_shapes=scratch, compiler_params=_cparams(VMEM_BIG))(*args)
    (out,), x_out = _hosted_call(body, name, (K // tk, ns), in_specs, [out_spec], [out_shape], scratch, args, xchg,
                                 _cparams(VMEM_BIG))
    return out, x_out


SSD_CHUNKS_PER_STEP = 4
SSD_BWD_CHUNKS_PER_STEP = 4
ATTN_BLOCKS_PER_STEP = 4
MASKED = -1e30
QK_SCALE = HALF ** -0.5


def _attn_bias(buckets, rel_bias):
    def body(bk_ref, relb_ref, out_ref):
        bk = bk_ref[...]
        i = lax.broadcasted_iota(jnp.int32, (BLK, 2 * BLK), 0)
        j = lax.broadcasted_iota(jnp.int32, (BLK, 2 * BLK), 1)
        window = (j > i) & (j <= i + BLK)
        for h in range(N_HEADS):
            acc = jnp.zeros((BLK, 2 * BLK), F32)
            for b in range(N_BUCKETS):
                acc = jnp.where(bk == b, relb_ref[b, h], acc)
            out_ref[0, h] = jnp.where(window, acc, MASKED)
            out_ref[1, h] = jnp.where(window & (j >= BLK), acc, MASKED)

    return pl.pallas_call(
        body, name="attn_bias", out_shape=jax.ShapeDtypeStruct((2, N_HEADS, BLK, 2 * BLK), F32),
        in_specs=[pl.BlockSpec(memory_space=pltpu.VMEM), pl.BlockSpec(memory_space=pltpu.SMEM)],
    )(buckets, rel_bias)


def _attn_fwd(qkv, bias, sinks, xchg):
    S = qkv.shape[0]
    nb = S // BLK

    nq = ATTN_BLOCKS_PER_STEP if nb % ATTN_BLOCKS_PER_STEP == 0 else 1
    rows = nq * BLK

    def body(q_ref, kvp_ref, kvc_ref, bias_ref, sinks_ref, y_ref):
        i = pl.program_id(0)
        q = q_ref[...].astype(F32) * QK_SCALE
        kv = jnp.concatenate([kvp_ref[...], kvc_ref[...]], axis=0).astype(F32)
        k_lo, k_hi = _split_pair(kv[:, :LANE])
        v_lo, v_hi = _split_pair(kv[:, LANE:])
        bands = [[t[b * BLK:(b + 2) * BLK].astype(MXU_DTYPE) for t in (k_lo, k_hi, v_lo, v_hi)] for b in range(nq)]
        q_heads = [_split_heads(q[b * BLK:(b + 1) * BLK], 4) for b in range(nq)]
        first = [jnp.where(i == 0, 1, 0) if b == 0 else 0 for b in range(nq)]
        items = [(b, h) for b in range(nq) for h in range(N_HEADS)]
        s = [_mm_nt(q_heads[b][h].astype(MXU_DTYPE), bands[b][h // 4]) + bias_ref[first[b], h] for b, h in items]
        m = [jnp.maximum(jnp.max(s[n], axis=-1, keepdims=True), sinks_ref[h]) for n, (b, h) in enumerate(items)]
        p = [jnp.exp(s[n] - m[n]) for n in range(len(items))]
        rinv = [1.0 / (jnp.sum(p[n], axis=-1, keepdims=True) + jnp.exp(sinks_ref[h] - m[n]))
                for n, (b, h) in enumerate(items)]
        out = [_mm(p[n], bands[b][2 + h // 4]) * rinv[n] for n, (b, h) in enumerate(items)]
        y_ref[...] = jnp.concatenate([_join_heads(out[b * N_HEADS:(b + 1) * N_HEADS]) for b in range(nq)], axis=0)

    smem = pl.BlockSpec(memory_space=pltpu.SMEM)
    return _hosted_call(
        body, "attn_fwd", nb // nq,
        in_specs=[pl.BlockSpec((rows, ATTN_W), _row),
                  pl.BlockSpec((BLK, 2 * KV_W), lambda i: (jnp.maximum(i * nq - 1, 0), 2)),
                  pl.BlockSpec((rows, 2 * KV_W), lambda i: (i, 2)),
                  pl.BlockSpec((2, N_HEADS, BLK, 2 * BLK), lambda i: (0, 0, 0, 0)), smem],
        out_specs=[pl.BlockSpec((rows, ATTN_W), _row)],
        out_shape=[jax.ShapeDtypeStruct((S, ATTN_W), F32)],
        scratch_shapes=[],
        args=(qkv, qkv, qkv, bias, sinks), xchg=xchg, cparams=_cparams(),
    )


def _attn_bwd(qkv, y, dy, bias, sinks, xchg):
    S = qkv.shape[0]
    nb = S // BLK
    nq = ATTN_BLOCKS_PER_STEP if nb % ATTN_BLOCKS_PER_STEP == 0 else 1
    rows, n_steps = nq * BLK, nb // nq

    def body(q_ref, kvp_ref, kvc_ref, y_ref, dy_ref, bias_ref, sinks_ref, dq_ref, dkv_ref, dbias_ref, dsk_ref, carry_ref):
        i = pl.program_id(0)

        @pl.when(i == 0)
        def _():
            dbias_ref[...] = jnp.zeros_like(dbias_ref)
            dsk_ref[...] = jnp.zeros_like(dsk_ref)
            carry_ref[...] = jnp.zeros_like(carry_ref)

        q = q_ref[...].astype(F32) * QK_SCALE
        kv = jnp.concatenate([kvp_ref[...], kvc_ref[...]], axis=0).astype(F32)
        k_lo, k_hi = _split_pair(kv[:, :LANE])
        v_lo, v_hi = _split_pair(kv[:, LANE:])
        bands = [[t[b * BLK:(b + 2) * BLK].astype(MXU_DTYPE) for t in (k_lo, k_hi, v_lo, v_hi)] for b in range(nq)]
        rows_of = lambda ref, b: ref[b * BLK:(b + 1) * BLK, :]
        first = [jnp.where(i == n_steps - 1, 1, 0) if b == 0 else 0 for b in range(nq)]
        items = [(b, h) for b in range(nq) for h in range(N_HEADS)]
        at = lambda b, h: b * N_HEADS + h
        q_heads = [hd for b in range(nq) for hd in _split_heads(q[b * BLK:(b + 1) * BLK], 4)]
        y_heads = [hd for b in range(nq) for hd in _split_heads(rows_of(y_ref, b), 4)]
        dy_heads = [hd for b in range(nq) for hd in _split_heads(rows_of(dy_ref, b), 4)]
        qs = [q_heads[n].astype(MXU_DTYPE) for n in range(len(items))]
        s = [_mm_nt(qs[at(b, h)], bands[b][h // 4]) + bias_ref[first[b], h] for b, h in items]
        m = [jnp.maximum(jnp.max(s[at(b, h)], axis=-1, keepdims=True), sinks_ref[h]) for b, h in items]
        p = [jnp.exp(s[n] - m[n]) for n in range(len(items))]
        esink = [jnp.exp(sinks_ref[h] - m[at(b, h)]) for b, h in items]
        rinv = [1.0 / (jnp.sum(p[n], axis=-1, keepdims=True) + esink[n]) for n in range(len(items))]
        t = [dy_heads[n] * rinv[n] for n in range(len(items))]
        delta = [jnp.sum(t[n] * y_heads[n], axis=-1, keepdims=True) for n in range(len(items))]
        tb = [t[n].astype(MXU_DTYPE) for n in range(len(items))]
        dp = [_mm_nt(tb[at(b, h)], bands[b][2 + h // 4]) for b, h in items]
        ds = [p[n] * (dp[n] - delta[n]) for n in range(len(items))]
        for h in range(N_HEADS):
            ds_h, dsk_h = ds[at(0, h)], esink[at(0, h)] * delta[at(0, h)]
            for b in range(1, nq):
                ds_h = ds_h + ds[at(b, h)]
                dsk_h = dsk_h + esink[at(b, h)] * delta[at(b, h)]
            dbias_ref[h] += ds_h
            dsk_ref[h] -= dsk_h
        dsb = [ds[n].astype(MXU_DTYPE) for n in range(len(items))]
        pb = [p[n].astype(MXU_DTYPE) for n in range(len(items))]
        dq_heads = [_mm(dsb[at(b, h)], bands[b][h // 4]) * QK_SCALE for b, h in items]
        grp = lambda lst, b, g: jnp.concatenate(lst[at(b, 4 * g):at(b, 4 * g) + 4], axis=0)
        dk_pads = [[_mm_tn(grp(dsb, b, g), grp(qs, b, g)) for g in range(2)] for b in range(nq)]
        dv_pads = [[_mm_tn(grp(pb, b, g), grp(tb, b, g)) for g in range(2)] for b in range(nq)]
        dq_ref[...] = jnp.concatenate([_join_heads(dq_heads[b * N_HEADS:(b + 1) * N_HEADS]) for b in range(nq)],
                                      axis=0).astype(dq_ref.dtype)
        part = lambda b, lo: jnp.concatenate(
            [_join_pair(d[b][0][lo:lo + BLK], d[b][1][lo:lo + BLK]) for d in (dk_pads, dv_pads)], axis=1)
        dkv = [part(b, BLK) + (part(b + 1, 0) if b + 1 < nq else carry_ref[...]) for b in range(nq)]
        dkv_ref[...] = jnp.concatenate(dkv, axis=0).astype(dkv_ref.dtype)
        carry_ref[...] = part(0, 0)

    smem = pl.BlockSpec(memory_space=pltpu.SMEM)
    rev = lambda i: (n_steps - 1 - i, 0)
    return _hosted_call(
        body, "attn_bwd", n_steps,
        in_specs=[pl.BlockSpec((rows, ATTN_W), rev),
                  pl.BlockSpec((BLK, 2 * KV_W), lambda i: (jnp.maximum((n_steps - 1 - i) * nq - 1, 0), 2)),
                  pl.BlockSpec((rows, 2 * KV_W), lambda i: (n_steps - 1 - i, 2)),
                  pl.BlockSpec((rows, ATTN_W), rev), pl.BlockSpec((rows, ATTN_W), rev),
                  pl.BlockSpec((2, N_HEADS, BLK, 2 * BLK), lambda i: (0, 0, 0, 0)), smem],
        out_specs=[pl.BlockSpec((rows, ATTN_W), rev), pl.BlockSpec((rows, 2 * KV_W), rev),
                   pl.BlockSpec((N_HEADS, BLK, 2 * BLK), lambda i: (0, 0, 0)),
                   pl.BlockSpec((N_HEADS, BLK, 1), lambda i: (0, 0, 0))],
        out_shape=[jax.ShapeDtypeStruct((S, ATTN_W), MXU_DTYPE), jax.ShapeDtypeStruct((S, 2 * KV_W), MXU_DTYPE),
                   jax.ShapeDtypeStruct((N_HEADS, BLK, 2 * BLK), F32), jax.ShapeDtypeStruct((N_HEADS, BLK, 1), F32)],
        scratch_shapes=[pltpu.VMEM((BLK, 2 * KV_W), F32)],
        args=(qkv, qkv, qkv, y, dy, bias, sinks), xchg=xchg, cparams=_cparams(),
    )


def _attn_finish(dbias, dsk, buckets):
    def body(db_ref, dsk_ref, bk_ref, drel_ref, dsink_ref):
        bk = bk_ref[...]
        r = lax.broadcasted_iota(jnp.int32, (N_BUCKETS, LANE), 0)
        l = lax.broadcasted_iota(jnp.int32, (N_BUCKETS, LANE), 1)
        row = lax.broadcasted_iota(jnp.int32, (N_HEADS, LANE), 0)
        res = jnp.zeros((N_BUCKETS, LANE), F32)
        dsink = jnp.zeros((N_HEADS, LANE), F32)
        for h in range(N_HEADS):
            db = db_ref[h]
            for b in range(N_BUCKETS):
                v = jnp.sum(jnp.sum(jnp.where(bk == b, db, 0.0), axis=1, keepdims=True), axis=0, keepdims=True)
                res = res + jnp.where((r == b) & (l == h), v, 0.0)
            dsink = dsink + jnp.where(row == h, jnp.sum(dsk_ref[h], axis=0, keepdims=True), 0.0)
        drel_ref[...] = res
        dsink_ref[...] = dsink

    return pl.pallas_call(body, name="attn_finish",
                          out_shape=[jax.ShapeDtypeStruct((N_BUCKETS, LANE), F32),
                                     jax.ShapeDtypeStruct((N_HEADS, LANE), F32)])(dbias, dsk, buckets)


def _ssd_consts():
    r = lax.broadcasted_iota(jnp.int32, (BLK, BLK), 0)
    c = lax.broadcasted_iota(jnp.int32, (BLK, BLK), 1)
    causal = c <= r
    upper = (r <= c).astype(F32)
    last = r == BLK - 1
    head = lax.broadcasted_iota(jnp.int32, (N_HEADS, BLK), 0)
    return causal, upper, last, head


def _ssd_chunks(xs, bg, cg, dt_raw_t, prev0, dtb, alog, d_rows, consts):
    causal, upper, last, head = consts
    nq = len(xs)
    items = [(c, h) for c in range(nq) for h in range(N_HEADS)]
    at = lambda c, h: c * N_HEADS + h
    a_neg = -jnp.exp(alog)
    dt_t = [_softplus(dt_raw_t[c] + dtb) for c in range(nq)]
    acs_t = [_mm_hi(dt_t[c] * a_neg, upper) for c in range(nq)]
    cb = [[_mm_nt(cg[c][g], bg[c][g]) for g in range(2)] for c in range(nq)]
    pick = lambda t, h: jnp.sum(jnp.where(head == h, t, 0.0), axis=0, keepdims=True)
    dt_row = [pick(dt_t[c], h) for c, h in items]
    a_row = [pick(acs_t[c], h) for c, h in items]
    a_rb = [jnp.broadcast_to(a_row[n], (BLK, BLK)) for n in range(len(items))]
    a_b = [a_rb[n].T for n in range(len(items))]
    a_last = [jnp.sum(jnp.where(last, a_b[n], 0.0), axis=0, keepdims=True) for n in range(len(items))]
    w = [cb[c][h // 4] * jnp.exp(jnp.where(causal, a_b[at(c, h)] - a_rb[at(c, h)], -1e30)) * dt_row[at(c, h)]
         for c, h in items]
    f_b = [jnp.broadcast_to(dt_row[n] * jnp.exp(a_last[n] - a_row[n]), (BLK, BLK)).T for n in range(len(items))]
    y_in = [_mm(w[at(c, h)], xs[c][h]) for c, h in items]
    st = [_mm_tn(bg[c][h // 4], xs[c][h] * f_b[at(c, h)]) for c, h in items]
    e_b = [jnp.exp(a_b[n]) for n in range(len(items))]
    states = [list(prev0)]
    for c in range(nq):
        states.append([states[c][h] * jnp.exp(a_last[at(c, h)]) + st[at(c, h)] for h in range(N_HEADS)])
    y_off = [_mm(cg[c][h // 4], states[c][h]) * e_b[at(c, h)] for c, h in items]
    ys = [[y_in[at(c, h)] + y_off[at(c, h)] + d_rows[h] * xs[c][h] for h in range(N_HEADS)] for c in range(nq)]
    return ys, states


def _ssd_chunks_bwd(xs, bg, cg, dt_raw_t, prev, dtb, alog, d_rows, dys, dh_last, consts):
    causal, upper, last, head = consts
    nq = len(xs)
    items = [(c, h) for c in range(nq) for h in range(N_HEADS)]
    ni = len(items)
    at = lambda c, h: c * N_HEADS + h
    groups = [(c, g) for c in range(nq) for g in range(2)]
    lane = _lane_iota((BLK, BLK))
    lane_row = _lane_iota((1, BLK))
    a_neg = -jnp.exp(alog)
    pre_dt = [dt_raw_t[c] + dtb for c in range(nq)]
    dt_t = [_softplus(pre_dt[c]) for c in range(nq)]
    acs_t = [_mm_hi(dt_t[c] * a_neg, upper) for c in range(nq)]
    pick = lambda t, h: jnp.sum(jnp.where(head == h, t, 0.0), axis=0, keepdims=True)
    full_sum = lambda t: jnp.sum(jnp.sum(t, axis=1, keepdims=True), axis=0, keepdims=True)
    dt_row = [pick(dt_t[c], h) for c, h in items]
    a_row = [pick(acs_t[c], h) for c, h in items]
    a_rb = [jnp.broadcast_to(a_row[n], (BLK, BLK)) for n in range(ni)]
    a_b = [a_rb[n].T for n in range(ni)]
    a_last = [jnp.sum(jnp.where(last, a_b[n], 0.0), axis=0, keepdims=True) for n in range(ni)]
    lm = [jnp.exp(jnp.where(causal, a_b[n] - a_rb[n], -1e30)) for n in range(ni)]
    cgb = [[cg[c][g].astype(MXU_DTYPE) for g in range(2)] for c in range(nq)]
    bgb = [[bg[c][g].astype(MXU_DTYPE) for g in range(2)] for c in range(nq)]
    cb = [[_mm_nt(cgb[c][g], bgb[c][g]) for g in range(2)] for c in range(nq)]
    u = [cb[c][h // 4] * lm[at(c, h)] for c, h in items]
    w = [(u[n] * dt_row[n]).astype(MXU_DTYPE) for n in range(ni)]
    e_row = [jnp.exp(a_last[n] - a_row[n]) for n in range(ni)]
    f_row = [dt_row[n] * e_row[n] for n in range(ni)]
    f_b = [jnp.broadcast_to(f_row[n], (BLK, BLK)).T for n in range(ni)]
    e_b = [jnp.exp(a_b[n]) for n in range(ni)]
    el = [jnp.exp(a_last[n]) for n in range(ni)]
    xb = [xs[c][h].astype(MXU_DTYPE) for c, h in items]
    dyb = [dys[c][h].astype(MXU_DTYPE) for c, h in items]
    prevb = [prev[c][h].astype(MXU_DTYPE) for c, h in items]
    gmat = [_mm(cgb[c][h // 4], prevb[at(c, h)]) for c, h in items]
    dw = [_mm_nt(dyb[n], xb[n]) for n in range(ni)]
    dg = [dys[c][h] * e_b[at(c, h)] for c, h in items]
    dgb = [dg[n].astype(MXU_DTYPE) for n in range(ni)]
    from_y = [_mm_tn(cgb[c][h // 4], dgb[at(c, h)]) for c, h in items]
    dhs = [None] * ni
    dprev = [None] * ni
    for c in reversed(range(nq)):
        for h in range(N_HEADS):
            dhs[at(c, h)] = dh_last[h] if c == nq - 1 else dprev[at(c + 1, h)]
            dprev[at(c, h)] = from_y[at(c, h)] + dhs[at(c, h)] * el[at(c, h)]
    dstb = [dhs[n].astype(MXU_DTYPE) for n in range(ni)]
    dxf = [_mm(bgb[c][h // 4], dstb[at(c, h)]) for c, h in items]
    xfb = [(xs[c][h] * f_b[at(c, h)]).astype(MXU_DTYPE) for c, h in items]
    dxs = [_mm_tn(w[at(c, h)], dyb[at(c, h)]) + d_rows[h] * dys[c][h] + f_b[at(c, h)] * dxf[at(c, h)]
           for c, h in items]
    dd_item = [jnp.sum(dys[c][h] * xs[c][h], axis=0, keepdims=True) for c, h in items]
    dcg_h = [_mm_nt(dgb[n], prevb[n]) for n in range(ni)]
    dbg_h = [_mm_nt(xfb[n], dstb[n]) for n in range(ni)]
    zt = [dw[n] * u[n] for n in range(ni)]
    dseg = [zt[n] * dt_row[n] for n in range(ni)]
    dcb_h = [dw[n] * lm[n] * dt_row[n] for n in range(ni)]
    four = lambda lst, c, g: lst[at(c, 4 * g)] + lst[at(c, 4 * g + 1)] + lst[at(c, 4 * g + 2)] + lst[at(c, 4 * g + 3)]
    dcb = {(c, g): four(dcb_h, c, g).astype(MXU_DTYPE) for c, g in groups}
    dcg = [[four(dcg_h, c, g) + _mm(dcb[c, g], bgb[c][g]) for g in range(2)] for c in range(nq)]
    dbg = [[four(dbg_h, c, g) + _mm_tn(dcb[c, g], cgb[c][g]) for g in range(2)] for c in range(nq)]
    r1 = [jnp.sum(dg[n] * gmat[n] + dseg[n], axis=1, keepdims=True) for n in range(ni)]
    r2 = [jnp.sum(dxf[at(c, h)] * xs[c][h], axis=1, keepdims=True) for c, h in items]
    tt = [jnp.where(lane < HALF, jnp.broadcast_to(r1[n], (BLK, BLK)), jnp.broadcast_to(r2[n], (BLK, BLK))).T
          for n in range(ni)]
    r1_row = [tt[n][0:1, :] for n in range(ni)]
    r2_row = [tt[n][HALF:HALF + 1, :] for n in range(ni)]
    d_el = [full_sum(dhs[at(c, h)] * prev[c][h]) for c, h in items]
    da_last = [jnp.sum(r2_row[n] * f_row[n], axis=1, keepdims=True) + el[n] * d_el[n] for n in range(ni)]
    da_row = [r1_row[n] - jnp.sum(dseg[n], axis=0, keepdims=True) - r2_row[n] * f_row[n]
              + jnp.where(lane_row == BLK - 1, da_last[n], 0.0) for n in range(ni)]
    ddt_row = [jnp.sum(zt[n], axis=0, keepdims=True) + r2_row[n] * e_row[n] for n in range(ni)]
    draw, dalog = [], jnp.zeros((N_HEADS, BLK), F32)
    for c in range(nq):
        da_t = jnp.zeros((N_HEADS, BLK), F32)
        ddt_t = jnp.zeros((N_HEADS, BLK), F32)
        for h in range(N_HEADS):
            da_t = jnp.where(head == h, da_row[at(c, h)], da_t)
            ddt_t = jnp.where(head == h, ddt_row[at(c, h)], ddt_t)
        d_dta = _mm_hi(da_t, causal.astype(F32))
        dalog = dalog + d_dta * dt_t[c] * a_neg
        draw.append((ddt_t + d_dta * a_neg) * jax.nn.sigmoid(pre_dt[c]))
    ddtb = draw[0]
    for c in range(1, nq):
        ddtb = ddtb + draw[c]
    dd_rows = []
    for h in range(N_HEADS):
        t = dd_item[at(0, h)]
        for c in range(1, nq):
            t = t + dd_item[at(c, h)]
        dd_rows.append(t)
    return ([dxs[c * N_HEADS:(c + 1) * N_HEADS] for c in range(nq)], dbg, dcg, draw,
            [dprev[at(0, h)] for h in range(N_HEADS)], ddtb, dalog, dd_rows)


def _dt_rows(dt_blk):
    return dt_blk.T[:N_HEADS]


def _conv_pre(halo, blk, cw_ref, cb_ref):
    ext = jnp.concatenate([halo, blk], axis=0)
    taps = [pltpu.roll(ext, 3 - k, 0)[8:] for k in range(3)] + [blk]
    pre = cb_ref[...] + cw_ref[0:1, :] * taps[0]
    for k in range(1, 4):
        pre = pre + cw_ref[k:k + 1, :] * taps[k]
    return pre


def _ssd_split(pre):
    heads = _split_heads(pre[:, :SSM_W], 4)
    pb = [pre[:, SSM_W + g * D_STATE:SSM_W + (g + 1) * D_STATE] for g in range(2)]
    pc = [pre[:, SSM_W + 2 * D_STATE + g * D_STATE:SSM_W + 2 * D_STATE + (g + 1) * D_STATE] for g in range(2)]
    return heads, pb, pc


def _ssd_fwd(xbc, dt_raw, conv_w, conv_b, dtb_row, alog_row, d_exp, xchg):
    S = xbc.shape[0]
    nc = S // BLK
    nq = SSD_CHUNKS_PER_STEP if nc % SSD_CHUNKS_PER_STEP == 0 else 1
    rows = nq * BLK

    def body(xbc_ref, halo_ref, dt_ref, cw_ref, cb_ref, dtb_ref, alog_ref, d_ref, y_ref, prev_ref, pre_ref, state_ref):
        i = pl.program_id(0)

        @pl.when(i == 0)
        def _():
            state_ref[...] = jnp.zeros_like(state_ref)

        halo = halo_ref[...] * jnp.where(i > 0, 1.0, 0.0)
        pre = _conv_pre(halo, xbc_ref[...], cw_ref, cb_ref)
        pre_ref[...] = pre
        xc = _silu(pre)
        split = [_ssd_split(xc[c * BLK:(c + 1) * BLK]) for c in range(nq)]
        dt_t = [_dt_rows(dt_ref[c * BLK:(c + 1) * BLK, :]) for c in range(nq)]
        prev0 = [state_ref[h] for h in range(N_HEADS)]
        d_rows = [d_ref[h:h + 1, :] for h in range(N_HEADS)]
        ys, states = _ssd_chunks([s[0] for s in split], [s[1] for s in split], [s[2] for s in split], dt_t, prev0,
                                 dtb_ref[...], alog_ref[...], d_rows, _ssd_consts())
        for h in range(N_HEADS):
            for c in range(nq):
                prev_ref[c, h] = states[c][h]
            state_ref[h] = states[nq][h]
        y_ref[...] = jnp.concatenate([_join_heads(ys[c]) for c in range(nq)], axis=0)

    vec = pl.BlockSpec((N_HEADS, LANE), _fixed)
    return _hosted_call(
        body, "ssd_fwd", nc // nq,
        in_specs=[pl.BlockSpec((rows, XBC_W), _row),
                  pl.BlockSpec((8, XBC_W), lambda i: (jnp.maximum(i * (rows // 8) - 1, 0), 0)),
                  pl.BlockSpec((rows, LANE), _row),
                  pl.BlockSpec((4, XBC_W), _fixed), pl.BlockSpec((1, XBC_W), _fixed), vec, vec,
                  pl.BlockSpec((N_HEADS, LANE), _fixed)],
        out_specs=[pl.BlockSpec((rows, SSM_W), _row),
                   pl.BlockSpec((nq, N_HEADS, D_STATE, LANE), lambda i: (i, 0, 0, 0)),
                   pl.BlockSpec((rows, XBC_W), _row)],
        out_shape=[jax.ShapeDtypeStruct((S, SSM_W), F32), jax.ShapeDtypeStruct((nc, N_HEADS, D_STATE, LANE), F32),
                   jax.ShapeDtypeStruct((S, XBC_W), F32)],
        scratch_shapes=[pltpu.VMEM((N_HEADS, D_STATE, LANE), F32)],
        args=(xbc, xbc, dt_raw, conv_w, conv_b, dtb_row, alog_row, d_exp), xchg=xchg, cparams=_cparams(),
    )


def _ssd_bwd(xbc, pre_act, dt_raw, prev_states, dy, conv_w, dtb_row, alog_row, d_exp, xchg):
    S = xbc.shape[0]
    nc = S // BLK
    nq = SSD_BWD_CHUNKS_PER_STEP if nc % SSD_BWD_CHUNKS_PER_STEP == 0 else 1
    rows, n_steps = nq * BLK, nc // nq

    def body(xbc_ref, halo_ref, pre_ref, dt_ref, prev_ref, dy_ref, cw_ref, dtb_ref, alog_ref, d_ref,
             dxbc_ref, ddt_ref, dcw_ref, dvec_ref, dd_ref, gstate_ref, ghalo_ref):
        i = pl.program_id(0)

        @pl.when(i == 0)
        def _():
            gstate_ref[...] = jnp.zeros_like(gstate_ref)
            ghalo_ref[...] = jnp.zeros_like(ghalo_ref)
            dcw_ref[...] = jnp.zeros_like(dcw_ref)
            dvec_ref[...] = jnp.zeros_like(dvec_ref)
            dd_ref[...] = jnp.zeros_like(dd_ref)

        halo = halo_ref[...] * jnp.where(i < n_steps - 1, 1.0, 0.0)
        ext = jnp.concatenate([halo, xbc_ref[...]], axis=0)
        pre = pre_ref[...]
        sig = jax.nn.sigmoid(pre)
        xc = pre * sig
        split = [_ssd_split(xc[c * BLK:(c + 1) * BLK]) for c in range(nq)]
        dt_t = [_dt_rows(dt_ref[c * BLK:(c + 1) * BLK, :]) for c in range(nq)]
        prev = [[prev_ref[c, h] for h in range(N_HEADS)] for c in range(nq)]
        d_rows = [d_ref[h:h + 1, :] for h in range(N_HEADS)]
        dys = [_split_heads(dy_ref[c * BLK:(c + 1) * BLK, :], 4) for c in range(nq)]
        dh_last = [gstate_ref[h] for h in range(N_HEADS)]
        dheads, dpb, dpc, ddt_t, dprev0, ddtb, dalog, dd_rows = _ssd_chunks_bwd(
            [s[0] for s in split], [s[1] for s in split], [s[2] for s in split], dt_t, prev, dtb_ref[...],
            alog_ref[...], d_rows, dys, dh_last, _ssd_consts())
        for h in range(N_HEADS):
            gstate_ref[h] = dprev0[h]
            dd_ref[h:h + 1, :] += dd_rows[h]
        pad = jnp.zeros((BLK - N_HEADS, BLK), F32)
        ddt_ref[...] = jnp.concatenate([jnp.concatenate([ddt_t[c], pad], axis=0).T for c in range(nq)],
                                       axis=0).astype(ddt_ref.dtype)
        dvec_ref[0:N_HEADS, :] += ddtb
        dvec_ref[N_HEADS:, :] += dalog
        dxc = jnp.concatenate([jnp.concatenate([_join_heads(dheads[c])] + list(dpb[c]) + list(dpc[c]), axis=1)
                               for c in range(nq)], axis=0)
        dpre = dxc * (sig * (1.0 + pre * (1.0 - sig)))
        zeros8 = jnp.zeros((8, XBC_W), F32)
        dpe = jnp.concatenate([zeros8, dpre, zeros8], axis=0)
        n_ext = 16 + rows
        shifted = [pltpu.roll(dpe, n_ext - (3 - k), 0)[:8 + rows] for k in range(3)] + [dpe[:8 + rows]]
        dext = cw_ref[0:1, :] * shifted[0]
        for k in range(1, 4):
            dext = dext + cw_ref[k:k + 1, :] * shifted[k]
        for k in range(4):
            dcw_ref[k:k + 1, :] += jnp.sum(shifted[k] * ext, axis=0, keepdims=True)
        dcw_ref[4:5, :] += jnp.sum(dpre, axis=0, keepdims=True)
        dxbc_ref[...] = jnp.concatenate([dext[8:rows], dext[rows:] + ghalo_ref[...]], axis=0).astype(dxbc_ref.dtype)
        ghalo_ref[...] = dext[:8, :]

    vec = pl.BlockSpec((N_HEADS, LANE), _fixed)
    rev = lambda i: (n_steps - 1 - i, 0)
    return _hosted_call(
        body, "ssd_bwd", n_steps,
        in_specs=[pl.BlockSpec((rows, XBC_W), rev),
                  pl.BlockSpec((8, XBC_W), lambda i: (jnp.maximum((n_steps - 1 - i) * (rows // 8) - 1, 0), 0)),
                  pl.BlockSpec((rows, XBC_W), rev),
                  pl.BlockSpec((rows, LANE), rev),
                  pl.BlockSpec((nq, N_HEADS, D_STATE, LANE), lambda i: (n_steps - 1 - i, 0, 0, 0)),
                  pl.BlockSpec((rows, SSM_W), rev),
                  pl.BlockSpec((4, XBC_W), _fixed), vec, vec,
                  pl.BlockSpec((N_HEADS, LANE), _fixed)],
        out_specs=[pl.BlockSpec((rows, XBC_W), rev), pl.BlockSpec((rows, LANE), rev),
                   pl.BlockSpec((8, XBC_W), _fixed), pl.BlockSpec((2 * N_HEADS, LANE), _fixed),
                   pl.BlockSpec((N_HEADS, LANE), _fixed)],
        out_shape=[jax.ShapeDtypeStruct((S, XBC_W), MXU_DTYPE), jax.ShapeDtypeStruct((S, LANE), MXU_DTYPE),
                   jax.ShapeDtypeStruct((8, XBC_W), F32), jax.ShapeDtypeStruct((2 * N_HEADS, LANE), F32),
                   jax.ShapeDtypeStruct((N_HEADS, LANE), F32)],
        scratch_shapes=[pltpu.VMEM((N_HEADS, D_STATE, LANE), F32), pltpu.VMEM((8, XBC_W), F32)],
        args=(xbc, xbc, pre_act, dt_raw, prev_states, dy, conv_w, dtb_row, alog_row, d_exp), xchg=xchg,
        cparams=_cparams(VMEM_BIG),
    )


def _adamw_math(w, g, m, v):
    m = ADAM_B1 * m + (1.0 - ADAM_B1) * g
    v = ADAM_B2 * v + (1.0 - ADAM_B2) * jnp.square(g)
    m_hat = m / (1.0 - ADAM_B1 ** ADAM_STEP)
    v_hat = v / (1.0 - ADAM_B2 ** ADAM_STEP)
    delta = -ADAM_LR * (m_hat / (jnp.sqrt(v_hat) + ADAM_EPS) + ADAM_WD * w)
    return delta, m, v


def _reduce_adamw_halves(part_a, part_b, w, m, v, name):
    R, C = w.shape
    P = part_a.shape[0]
    tl = 256
    n = C // tl

    def body(a_ref, b_ref, w_ref, m_ref, v_ref, g_ref, d_ref, nm_ref, nv_ref):
        ga, gb = a_ref[0].astype(F32), b_ref[0].astype(F32)
        for i in range(1, P):
            ga, gb = ga + a_ref[i].astype(F32), gb + b_ref[i].astype(F32)
        first = jnp.where(pl.program_id(0) < n // 2, 1.0, 0.0)
        g = ga * first + gb * (1.0 - first)
        d, nm, nv = _adamw_math(w_ref[...], g, m_ref[...], v_ref[...])
        g_ref[...] = g
        d_ref[...] = d
        nm_ref[...] = nm
        nv_ref[...] = nv

    blk = pl.BlockSpec((R, tl), lambda i: (0, i))
    return pl.pallas_call(
        body, name=name, grid=(n,),
        in_specs=[pl.BlockSpec((P, R, tl), lambda i: (0, 0, jnp.minimum(i, n // 2 - 1))),
                  pl.BlockSpec((P, R, tl), lambda i: (0, 0, jnp.maximum(i - n // 2, 0))), blk, blk, blk],
        out_specs=[blk] * 4, out_shape=[jax.ShapeDtypeStruct((R, C), F32)] * 4,
    )(part_a, part_b, w, m, v)


def _reduce_adamw_hosting(parts_list, wmv_list, name, xchg):
    n_arr = len(parts_list)
    pieces = [list(p) if isinstance(p, (tuple, list)) else [p] for p in parts_list]
    n_pieces = sum(len(p) for p in pieces)
    C = wmv_list[0][0].shape[1]
    tl = 256

    def total(ref):
        g = ref[0].astype(F32)
        for i in range(1, N_DEV):
            g = g + ref[i].astype(F32)
        return g

    def body(*refs):
        p_refs, wmv_refs, o_refs = refs[:n_pieces], refs[n_pieces:n_pieces + 3 * n_arr], refs[n_pieces + 3 * n_arr:]
        at = 0
        for k in range(n_arr):
            sums = [total(r) for r in p_refs[at:at + len(pieces[k])]]
            at += len(pieces[k])
            g = sums[0] if len(sums) == 1 else jnp.concatenate(sums, axis=0)
            w_ref, m_ref, v_ref = wmv_refs[3 * k:3 * k + 3]
            d, nm, nv = _adamw_math(w_ref[...], g, m_ref[...], v_ref[...])
            for o, val in zip(o_refs[4 * k:4 * k + 4], (g, d, nm, nv)):
                o[...] = val

    in_specs = [pl.BlockSpec((N_DEV, p.shape[1], tl), lambda i: (0, 0, i)) for group in pieces for p in group]
    in_specs += [pl.BlockSpec((w.shape[0], tl), lambda i: (0, i)) for w, _, _ in wmv_list for _ in range(3)]
    out_specs = [pl.BlockSpec((w.shape[0], tl), lambda i: (0, i)) for w, _, _ in wmv_list for _ in range(4)]
    out_shape = [jax.ShapeDtypeStruct(w.shape, F32) for w, _, _ in wmv_list for _ in range(4)]
    args = [p for group in pieces for p in group] + [a for wmv in wmv_list for a in wmv]
    outs, x_out = _hosted_call(body, name, C // tl, in_specs, out_specs, out_shape, [], args, xchg,
                               _cparams(VMEM_BIG))
    return [outs[4 * k:4 * k + 4] for k in range(n_arr)], x_out


_SMALL_NAMES = ("ada_b", "norm1", "conv_w", "conv_b", "dt_bias", "A_log", "D_skip", "sinks", "attn_out_norm",
                "ssm_out_norm", "norm2", "rel_bias", "final_norm")
N_MOD = 6 * D_MODEL


def _mod_row(a0, a1, a2):
    return jnp.concatenate([a0[2:3], a0[1:2], a1[0:1], a2[2:3], a2[1:2], a2[3:4]], axis=1)


def _small_update(gathered, params):
    n_g = len(gathered)
    flat = [a for name in _SMALL_NAMES for a in params[name]]

    def body(*refs):
        a0_ref, a1_ref, a2_ref, cw_ref, dv_ref, dd_ref, ds_ref, dr_ref, c_ref = refs[:n_g]
        wmv = refs[n_g:n_g + len(flat)]
        outs = refs[n_g + len(flat):]

        def total(ref):
            t = ref[0]
            for i in range(1, N_DEV):
                t = t + ref[i]
            return t

        t0, t1, t2, tcw, tdv, tdd, tds, tdr = [total(r) for r in (a0_ref, a1_ref, a2_ref, cw_ref, dv_ref, dd_ref,
                                                                   ds_ref, dr_ref)]
        r8 = lax.broadcasted_iota(jnp.int32, (N_HEADS, LANE), 0)
        l8 = lax.broadcasted_iota(jnp.int32, (N_HEADS, LANE), 1)

        def diag_row(t):
            return jnp.sum(jnp.where(r8 == l8, t, 0.0), axis=0, keepdims=True)[:, :N_HEADS]

        def lane_sums(t):
            return diag_row(jnp.broadcast_to(jnp.sum(t, axis=1, keepdims=True), (N_HEADS, LANE)))

        me = _lin(_my_pos())
        n_cw = XBC_W // N_DEV
        cw_mine = jnp.zeros((4, n_cw), F32)
        for j in range(N_DEV):
            cw_mine = cw_mine + tcw[0:4, j * n_cw:(j + 1) * n_cw] * jnp.where(me == j, 1.0, 0.0)
        grads = {
            "ada_b": _mod_row(t0, t1, t2), "norm1": t0[0:1], "conv_w": cw_mine, "conv_b": tcw[4:5],
            "dt_bias": lane_sums(tdv[:N_HEADS]), "A_log": lane_sums(tdv[N_HEADS:]), "D_skip": lane_sums(tdd),
            "sinks": diag_row(tds), "attn_out_norm": t1[1:2, :ATTN_W], "ssm_out_norm": t1[1:2, ATTN_W:],
            "norm2": t2[0:1], "rel_bias": tdr[:, :N_HEADS], "final_norm": t2[4:5],
        }
        for k, name in enumerate(_SMALL_NAMES):
            w_ref, m_ref, v_ref = wmv[3 * k:3 * k + 3]
            g = grads[name]
            d, nm, nv = _adamw_math(w_ref[...], g, m_ref[...], v_ref[...])
            for o, val in zip(outs[4 * k:4 * k + 4], (g, d, nm, nv)):
                o[...] = val
        loss_ref, call_ref, dmod_ref = outs[4 * len(_SMALL_NAMES):]
        loss_ref[...] = t2[5:6, 0:1]
        call_ref[...] = jnp.concatenate([c_ref[i] for i in range(N_DEV)], axis=0)
        dmod_ref[...] = jnp.concatenate([_mod_row(a0_ref[i], a1_ref[i], a2_ref[i]) for i in range(N_DEV)], axis=0)

    out_shape = [jax.ShapeDtypeStruct(params[name][0].shape, F32) for name in _SMALL_NAMES for _ in range(4)]
    out_shape += [jax.ShapeDtypeStruct((1, 1), F32), jax.ShapeDtypeStruct((N_DEV, D_MODEL), F32),
                  jax.ShapeDtypeStruct((N_DEV, N_MOD), F32)]
    res = pl.pallas_call(body, name="small_update", out_shape=out_shape)(*gathered, *flat)
    upd = {name: res[4 * k:4 * k + 4] for k, name in enumerate(_SMALL_NAMES)}
    loss, c_all, dmod_all = res[4 * len(_SMALL_NAMES):]
    return upd, loss, c_all, dmod_all


def _ada_w_update(c_all, dmod_all, w, m, v):
    chunk = w.shape[1]

    def body(c_ref, dm_ref, w_ref, m_ref, v_ref, g_ref, d_ref, nm_ref, nv_ref):
        me = _lin(_my_pos())
        dm = jnp.zeros((N_DEV, chunk), F32)
        for j in range(N_DEV):
            dm = dm + dm_ref[:, j * chunk:(j + 1) * chunk] * jnp.where(me == j, 1.0, 0.0)
        g = lax.dot_general(_silu(c_ref[...]), dm, (((0,), (0,)), ((), ())), precision=HI,
                            preferred_element_type=F32)
        d, nm, nv = _adamw_math(w_ref[...], g, m_ref[...], v_ref[...])
        g_ref[...] = g
        d_ref[...] = d
        nm_ref[...] = nm
        nv_ref[...] = nv

    tr = 256
    blk = pl.BlockSpec((tr, chunk), _row)
    return pl.pallas_call(
        body, name="ada_w_update", grid=(w.shape[0] // tr,),
        in_specs=[pl.BlockSpec((N_DEV, tr), lambda i: (0, i)), pl.BlockSpec(dmod_all.shape, _fixed), blk, blk, blk],
        out_specs=[blk] * 4, out_shape=[jax.ShapeDtypeStruct(w.shape, F32)] * 4,
    )(c_all, dmod_all, w, m, v)


def _local_step(x, tgt, c, mod, w_in, conv_w, w_o_mine, w_gu_mine, w_d_mine, p):
    S = x.shape[0]
    tm = min(512, S)
    tmm = min(256, S)
    tw = min(2048, S)
    shift1, scale1, gate1, shift2, scale2, gate2 = [mod[i:i + 1] for i in range(6)]
    buckets = jnp.asarray(_t5_bucket_table())
    per_head = lambda a: jnp.broadcast_to(a.reshape(N_HEADS, 1), (N_HEADS, LANE))
    dtb_row, alog_row, d_exp = per_head(p["dt_bias"]), per_head(p["A_log"]), per_head(p["D_skip"])
    sinks = p["sinks"].reshape(N_HEADS)

    d_cut, gu_cut = WD_CUT, WGU_CUTS
    n_d, n_gu = w_d_mine.shape[0], w_gu_mine.shape[0]
    (qkv, z, xbc, dt_raw), (g_d_a,) = _in_proj_fwd(x, p["norm1"], scale1, shift1, w_in, tm,
                                                   ([(w_d_mine, 0, d_cut)], "two-level"))
    bias = _attn_bias(buckets, p["rel_bias"])
    (ya,), (g_gu_a,) = _attn_fwd(qkv, bias, sinks, ([(w_gu_mine, 0, gu_cut[0])], "two-level"))
    (ys, prev_states, pre_act), (g_gu_b, g_o) = _ssd_fwd(
        xbc, dt_raw, conv_w, p["conv_b"], dtb_row, alog_row, d_exp,
        ([(w_gu_mine, gu_cut[0], gu_cut[1] - gu_cut[0]), w_o_mine], "two-level"))
    w_o = g_o.reshape(D_MODEL, D_MODEL)
    x1, (g_gu_c, g_d_b) = _out_proj_fwd(
        x, ya, ys, z, p["attn_out_norm"], p["ssm_out_norm"], gate1, w_o, tm,
        ([(w_gu_mine, gu_cut[1], n_gu - gu_cut[1]), (w_d_mine, d_cut, n_d - d_cut)], "two-level"))
    dx1, h2, dgu, act, dmlp, acc2 = _mlp_loss(x1, tgt, p["norm2"], scale2, shift2, gate2, p["final_norm"],
                                              (g_gu_a, g_gu_b, g_gu_c), (g_d_a, g_d_b), tmm)
    g_w_gu = _wgrad(dgu, h2, 2 * D_FF // 4, tw, "wgrad_gate_up")
    g_w_d = _wgrad(act, dmlp, D_FF // 2, tw, "wgrad_down")
    gu_slots = g_w_gu.reshape(N_DEV, 2 * D_FF // N_DEV, D_MODEL)
    (dya, dys, dz, g_w_o, acc1), (r_gu_a,) = _out_proj_bwd(
        dx1, ya, ys, z, p["attn_out_norm"], p["ssm_out_norm"], gate1, w_o, tm, ([gu_slots], ("rows", 0, GGU_CUT)))
    (dq, dkv, dbias, dsk), (r_d, r_o) = _attn_bwd(
        qkv, ya, dya, bias, sinks,
        ([g_w_d.reshape(N_DEV, D_FF // N_DEV, D_MODEL), g_w_o.reshape(N_DEV, D_MODEL // N_DEV, D_MODEL)], True))
    drel, dsink = _attn_finish(dbias, dsk, buckets)
    (dxbc, ddt, dcw, dvec, dd), (r_gu_b, *early) = _ssd_bwd(
        xbc, pre_act, dt_raw, prev_states, dys, conv_w, dtb_row, alog_row, d_exp,
        [([gu_slots], ("rows", GGU_CUT, 2 * D_FF // N_DEV - GGU_CUT)), ([acc1, acc2, dsink, drel, c], False)])
    r_gu = (r_gu_a, r_gu_b)
    gx, h1, acc0, g_in_a = _in_proj_bwd(x, dx1, dq, dkv, dz, dxbc, ddt, p["norm1"], scale1, shift1, w_in, tm)
    half = D_MODEL // 2
    slots = lambda g: g[:IN_W].reshape(N_DEV, IN_W // N_DEV, half)
    g_in_b, (r_in_a,) = _wgrad((dq, dkv, dz, dxbc, ddt), h1, IN_PAD, tw, "wgrad_in_b",
                               ([slots(g_in_a)], "two-level scatter"), g_cols=(half, 1))
    return gx, (r_in_a, slots(g_in_b)), (r_o, r_gu, r_d), early, (acc0, dcw, dvec, dd)


def kernel(x, c, ada_w, ada_b, norm1, w_in, conv_w, conv_b, dt_bias, A_log, D_skip, sinks, attn_out_norm, ssm_out_norm, w_o, norm2, w_gate_up, w_down, rel_bias, final_norm, loss_target, m_ada_w, m_ada_b, m_norm1, m_w_in, m_conv_w, m_conv_b, m_dt_bias, m_A_log, m_D_skip, m_sinks, m_attn_out_norm, m_ssm_out_norm, m_w_o, m_norm2, m_w_gate_up, m_w_down, m_rel_bias, m_final_norm, v_ada_w, v_ada_b, v_norm1, v_w_in, v_conv_w, v_conv_b, v_dt_bias, v_A_log, v_D_skip, v_sinks, v_attn_out_norm, v_ssm_out_norm, v_w_o, v_norm2, v_w_gate_up, v_w_down, v_rel_bias, v_final_norm):
    two_d = lambda a: a if a.ndim == 2 else a.reshape(-1, a.shape[-1])
    small_params = dict(
        ada_b=(ada_b, m_ada_b, v_ada_b), norm1=(norm1, m_norm1, v_norm1), conv_w=(conv_w, m_conv_w, v_conv_w),
        conv_b=(conv_b, m_conv_b, v_conv_b), dt_bias=(dt_bias, m_dt_bias, v_dt_bias), A_log=(A_log, m_A_log, v_A_log),
        D_skip=(D_skip, m_D_skip, v_D_skip), sinks=(sinks, m_sinks, v_sinks),
        attn_out_norm=(attn_out_norm, m_attn_out_norm, v_attn_out_norm),
        ssm_out_norm=(ssm_out_norm, m_ssm_out_norm, v_ssm_out_norm), norm2=(norm2, m_norm2, v_norm2),
        rel_bias=(rel_bias, m_rel_bias, v_rel_bias), final_norm=(final_norm, m_final_norm, v_final_norm))
    small_params = {k: tuple(two_d(a) for a in v) for k, v in small_params.items()}
    S = x.shape[1]
    xs, tgt = x.reshape(S, D_MODEL), loss_target.reshape(S, D_MODEL)
    ada_w2 = ada_w[0]
    chunk = ada_w2.shape[1]
    t_in = [jnp.transpose(a[0]) for a in (w_in, m_w_in, v_w_in)]
    t_gu = [jnp.transpose(a[0]) for a in (w_gate_up, m_w_gate_up, v_w_gate_up)]

    mod, (g_in, g_cw) = _mod_and_gather(c, ada_w2, ada_b.reshape(N_DEV, chunk), [t_in[0].astype(WIRE_DTYPE), conv_w[0]])
    mod = mod.reshape(6, D_MODEL)
    w_in_full = jnp.pad(g_in.reshape(IN_W, D_MODEL), ((0, IN_PAD - IN_W), (0, 0)))
    conv_w_full = jnp.transpose(g_cw, (1, 0, 2)).reshape(4, XBC_W)

    p = {k: v[0] for k, v in small_params.items()}
    gx, (r_in_a, gw_in_b), (r_o, r_gu, r_d), early, late_blocks = _local_step(
        xs, tgt, c, mod, w_in_full, conv_w_full, w_o[0].astype(WIRE_DTYPE), t_gu[0].astype(WIRE_DTYPE),
        w_down[0].astype(WIRE_DTYPE), p)

    (u_gu, u_d, u_o), (r_in_b, *late) = _reduce_adamw_hosting(
        [r_gu, r_d, r_o], [tuple(t_gu), (w_down[0], m_w_down[0], v_w_down[0]), (w_o[0], m_w_o[0], v_w_o[0])],
        "adamw_big", [([gw_in_b], "two-level scatter"), (list(late_blocks), False)])
    gathered = (late[0], early[0], early[1], late[1], late[2], late[3], early[2], early[3], early[4])

    small, loss, c_all, dmod_all = _small_update(gathered, small_params)

    big = {
        "ada_w": _ada_w_update(c_all, dmod_all, ada_w2, m_ada_w[0], v_ada_w[0]),
        "w_in": [jnp.transpose(a) for a in _reduce_adamw_halves(r_in_a, r_in_b, *t_in, "adamw_w_in")],
        "w_o": u_o,
        "w_gate_up": [jnp.transpose(a) for a in u_gu],
        "w_down": u_d,
    }
    big.update(small)

    order = ['ada_w', 'ada_b', 'norm1', 'w_in', 'conv_w', 'conv_b', 'dt_bias', 'A_log', 'D_skip', 'sinks',
             'attn_out_norm', 'ssm_out_norm', 'w_o', 'norm2', 'w_gate_up', 'w_down', 'rel_bias', 'final_norm']
    shapes = dict(ada_w=ada_w.shape, ada_b=ada_b.shape, norm1=norm1.shape, w_in=w_in.shape, conv_w=conv_w.shape,
                  conv_b=conv_b.shape, dt_bias=dt_bias.shape, A_log=A_log.shape, D_skip=D_skip.shape,
                  sinks=sinks.shape, attn_out_norm=attn_out_norm.shape, ssm_out_norm=ssm_out_norm.shape,
                  w_o=w_o.shape, norm2=norm2.shape, w_gate_up=w_gate_up.shape, w_down=w_down.shape,
                  rel_bias=rel_bias.shape, final_norm=final_norm.shape)
    outs = [[], [], [], []]
    for name in order:
        for kind in range(4):
            outs[kind].append(big[name][kind].reshape(shapes[name]))
    return (loss.reshape(()), gx.reshape(x.shape), *outs[0], *outs[1], *outs[2], *outs[3])
```

```python
import numpy as np
import jax
import jax.numpy as jnp
from jax import lax
from jax.experimental import pallas as pl
from jax.experimental.pallas import tpu as pltpu

F32 = jnp.float32
MXU_DTYPE = jnp.bfloat16
WIRE_DTYPE = jnp.bfloat16
HI = lax.Precision.HIGHEST
MESH = pl.DeviceIdType.MESH
N_DEV = 8

D_MODEL = 1024
ATTN_W = 512
KV_W = 128
SSM_W = 512
XBC_W = 1024
N_HEADS = 8
D_STATE = 128
D_FF = 2816
IN_W = 2312
IN_PAD = 2432
BLK = 128
N_BUCKETS = 32
EPS = 1e-6
LANE = 128
HALF = 64

ADAM_LR, ADAM_B1, ADAM_B2, ADAM_EPS, ADAM_WD, ADAM_STEP = 0.001, 0.9, 0.999, 1e-08, 0.01, 10

VMEM_BIG = 56 * 1024 * 1024
WD_CUT = 288
WGU_CUTS = (240, 496)
GGU_CUT = 304


def _cparams(vmem=None):
    if vmem is None:
        return pltpu.CompilerParams()
    return pltpu.CompilerParams(vmem_limit_bytes=vmem)


def _mm(a, b):
    return jnp.dot(a.astype(MXU_DTYPE), b.astype(MXU_DTYPE), preferred_element_type=F32)


def _mm_nt(a, b):
    return lax.dot_general(a.astype(MXU_DTYPE), b.astype(MXU_DTYPE), (((1,), (1,)), ((), ())),
                           preferred_element_type=F32)


def _mm_tn(a, b):
    return lax.dot_general(a.astype(MXU_DTYPE), b.astype(MXU_DTYPE), (((0,), (0,)), ((), ())),
                           preferred_element_type=F32)


def _mm_hi(a, b):
    return jnp.dot(a, b, precision=HI, preferred_element_type=F32)


def _silu(x):
    return x * jax.nn.sigmoid(x)


def _softplus(x):
    return jnp.maximum(x, 0.0) + jnp.log1p(jnp.exp(-jnp.abs(x)))


def _rms(x, g, n):
    return x * lax.rsqrt(jnp.sum(x * x, axis=-1, keepdims=True) * (1.0 / n) + EPS) * g


def _modnorm(x, g, scale, shift):
    return _rms(x, g, x.shape[-1]) * (1.0 + scale) + shift


def _modnorm_parts(x):
    r = lax.rsqrt(jnp.sum(x * x, axis=-1, keepdims=True) * (1.0 / x.shape[-1]) + EPS)
    return r, x * r


def _modnorm_bwd(r, xhat, g, scale, dy):
    dyg = dy * (g * (1.0 + scale))
    c = jnp.sum(dyg * xhat, axis=-1, keepdims=True) * (1.0 / xhat.shape[-1])
    dx = r * (dyg - xhat * c)
    ct = jnp.sum(dy * xhat, axis=0, keepdims=True)
    return dx, ct * (1.0 + scale), ct * g, jnp.sum(dy, axis=0, keepdims=True)


def _lane_iota(shape):
    return lax.broadcasted_iota(jnp.int32, shape, len(shape) - 1)


def _split_pair(t):
    lane = _lane_iota(t.shape)
    lo = jnp.where(lane < HALF, t, 0.0)
    hi = pltpu.roll(jnp.where(lane >= HALF, t, 0.0), HALF, 1)
    return lo, hi


def _join_pair(lo, hi):
    lane = _lane_iota(lo.shape)
    return jnp.where(lane < HALF, lo, pltpu.roll(hi, HALF, 1))


def _split_heads(t, n_pairs):
    out = []
    for p in range(n_pairs):
        out.extend(_split_pair(t[:, p * LANE:(p + 1) * LANE]))
    return out


def _join_heads(hs):
    return jnp.concatenate([_join_pair(hs[2 * p], hs[2 * p + 1]) for p in range(len(hs) // 2)], axis=1)


def _t5_bucket_table():
    dist = np.arange(BLK)[:, None] + BLK - np.arange(2 * BLK)[None, :]
    n = np.maximum(dist, 0)
    max_exact = N_BUCKETS // 2
    large = max_exact + (np.log(np.maximum(n, 1) / max_exact) / np.log(128 / max_exact)
                         * (N_BUCKETS - max_exact)).astype(np.int32)
    large = np.minimum(large, N_BUCKETS - 1)
    return np.where(n < max_exact, n, large).astype(np.int32)


def _my_pos():
    return lax.axis_index("x"), lax.axis_index("y"), lax.axis_index("c")


def _peer(k):
    x, y, c = _my_pos()
    return (1 - x if k & 4 else x, 1 - y if k & 2 else y, 1 - c if k & 1 else c)


def _lin(pos):
    return 4 * pos[0] + 2 * pos[1] + pos[2]


def _xchg_copies(ins, outs, sems, scatter):
    local_sem, send_sem, recv_sem = sems
    me = _lin(_my_pos())

    def source(a, slot):
        if not scatter:
            return ins[a]
        if scatter is True:
            return ins[a].at[slot]
        return ins[a].at[slot, pl.ds(scatter[1], scatter[2])]

    local, remote = [], []
    for a in range(len(ins)):
        local.append(pltpu.make_async_copy(source(a, me), outs[a].at[me], local_sem.at[a]))
    for k in range(1, N_DEV):
        peer = _peer(k)
        for a in range(len(ins)):
            remote.append(pltpu.make_async_remote_copy(source(a, _lin(peer)), outs[a].at[me], send_sem.at[a, k - 1],
                                                       recv_sem.at[a, k - 1], device_id=peer, device_id_type=MESH))
    return local, remote


def _xchg_start(ins, outs, sems, scatter):
    local, remote = _xchg_copies(ins, outs, sems, scatter)
    for cp in local + remote:
        cp.start()


def _xchg_wait(ins, outs, sems, scatter):
    local, remote = _xchg_copies(ins, outs, sems, scatter)
    for cp in local:
        cp.wait()
    for cp in remote:
        cp.wait_send()
        cp.wait_recv()


def _xchg_shapes(arrs, scatter):
    n = len(arrs)
    if isinstance(scatter, tuple):
        out_shape = [jax.ShapeDtypeStruct((a.shape[0], scatter[2]) + a.shape[2:], a.dtype) for a in arrs]
    elif scatter:
        out_shape = [jax.ShapeDtypeStruct(a.shape, a.dtype) for a in arrs]
    else:
        out_shape = [jax.ShapeDtypeStruct((N_DEV,) + a.shape, a.dtype) for a in arrs]
    sems = [pltpu.SemaphoreType.DMA((n,)), pltpu.SemaphoreType.DMA((n, N_DEV - 1)),
            pltpu.SemaphoreType.DMA((n, N_DEV - 1))]
    return out_shape, sems


_CHIPS = (2, 4, 6)


def _g2_sems(n):
    dma = pltpu.SemaphoreType.DMA
    return [dma((n,)), dma((n, N_DEV)), dma((n, N_DEV)), dma((n, len(_CHIPS))), dma((n, len(_CHIPS)))]


class _TwoLevelGather:
    def __init__(self, ins, outs, sems, windows=None):
        self.ins, self.outs = ins, outs
        self.local_sem, self.send_sem, self.recv_sem, self.fsend_sem, self.frecv_sem = sems
        self.n = len(ins)
        self.windows = windows or [None] * self.n

    def _mine(self, a):
        w = self.windows[a]
        return self.ins[a] if w is None else self.ins[a].at[pl.ds(w[0], w[1])]

    def _direct(self, a, k):
        return pltpu.make_async_remote_copy(self._mine(a), self.outs[a].at[_lin(_my_pos())], self.send_sem.at[a, k],
                                            self.recv_sem.at[a, k], device_id=_peer(k), device_id_type=MESH)

    def _handed_on(self, a, j, origin):
        slot = self.outs[a].at[origin]
        return pltpu.make_async_remote_copy(slot, slot, self.fsend_sem.at[a, j], self.frecv_sem.at[a, j],
                                            device_id=_peer(1), device_id_type=MESH)

    def _local(self, a):
        return pltpu.make_async_copy(self._mine(a), self.outs[a].at[_lin(_my_pos())], self.local_sem.at[a])

    def start(self):
        for a in range(self.n):
            self._local(a).start()
        for k in (1,) + _CHIPS:
            for a in range(self.n):
                self._direct(a, k).start()

    def forward(self):
        for j, k in enumerate(_CHIPS):
            for a in range(self.n):
                self._direct(a, k).wait_recv()
                self._handed_on(a, j, _lin(_peer(k))).start()

    def finish(self):
        for a in range(self.n):
            self._direct(a, 1).wait_recv()
            for j, k in enumerate(_CHIPS):
                self._handed_on(a, j, _lin(_peer(k ^ 1))).wait_recv()
            self._local(a).wait()
            for k in (1,) + _CHIPS:
                self._direct(a, k).wait_send()
            for j, k in enumerate(_CHIPS):
                self._handed_on(a, j, _lin(_peer(k))).wait_send()


N_SCATTERED = 2 + len(_CHIPS)


class _TwoLevelScatter:
    def __init__(self, src, out, stage, buf_a, buf_b, sems):
        self.src, self.out, self.stage, self.buf_a, self.buf_b = src, out, stage, buf_a, buf_b
        self.local_sem, self.dsend, self.drecv, self.csend, self.crecv = sems

    def _own(self):
        return pltpu.make_async_copy(self.src.at[_lin(_my_pos())], self.out.at[0], self.local_sem.at[0])

    def _to_core(self, j):
        q = 0 if j == 0 else _CHIPS[j - 1]
        dst = self.out.at[1] if j == 0 else self.stage.at[j - 1]
        return pltpu.make_async_remote_copy(self.src.at[_lin(_peer(q ^ 1))], dst, self.dsend.at[j], self.drecv.at[j],
                                            device_id=_peer(1), device_id_type=MESH)

    def _loads(self, j):
        mine = self.src.at[_lin(_peer(_CHIPS[j]))]
        return (pltpu.make_async_copy(self.stage.at[j], self.buf_a.at[j], self.local_sem.at[1 + 2 * j]),
                pltpu.make_async_copy(mine, self.buf_b.at[j], self.local_sem.at[2 + 2 * j]))

    def _to_chip(self, j):
        return pltpu.make_async_remote_copy(self.buf_a.at[j], self.out.at[2 + j], self.csend.at[j], self.crecv.at[j],
                                            device_id=_peer(_CHIPS[j]), device_id_type=MESH)

    def start(self):
        self._own().start()
        for j in range(1 + len(_CHIPS)):
            self._to_core(j).start()

    def forward(self):
        for j in range(len(_CHIPS)):
            self._to_core(j + 1).wait_recv()
            for cp in self._loads(j):
                cp.start()
        for j in range(len(_CHIPS)):
            for cp in self._loads(j):
                cp.wait()
            self.buf_a[j] = (self.buf_a[j].astype(F32) + self.buf_b[j].astype(F32)).astype(self.buf_a.dtype)
            self._to_chip(j).start()

    def finish(self):
        self._own().wait()
        self._to_core(0).wait_recv()
        for j in range(1 + len(_CHIPS)):
            self._to_core(j).wait_send()
        for j in range(len(_CHIPS)):
            self._to_chip(j).wait_send()
            self._to_chip(j).wait_recv()


def _s2_shapes(a):
    dma = pltpu.SemaphoreType.DMA
    n_c = len(_CHIPS)
    piece = a.shape[1:]
    return (jax.ShapeDtypeStruct((N_SCATTERED,) + piece, a.dtype), jax.ShapeDtypeStruct((n_c,) + piece, a.dtype),
            [pltpu.VMEM((n_c,) + piece, a.dtype)] * 2,
            [dma((1 + 2 * n_c,)), dma((1 + n_c,)), dma((1 + n_c,)), dma((n_c,)), dma((n_c,))])


def _mod_and_gather(c, ada_w, ada_b8, arrs):
    n = len(arrs)
    chunk = ada_w.shape[1]
    out_shape = [jax.ShapeDtypeStruct((N_DEV, 1, chunk), F32)]
    out_shape += [jax.ShapeDtypeStruct((N_DEV,) + a.shape, a.dtype) for a in arrs]

    def modulation(c_ref, w_ref, b_ref, out_ref, cbuf, part, s1, r1, s2, r2):
        me = _lin(_my_pos())
        first = []
        for k in range(1, N_DEV):
            cp = pltpu.make_async_remote_copy(c_ref, cbuf.at[me], s1.at[k - 1], r1.at[k - 1],
                                              device_id=_peer(k), device_id_type=MESH)
            cp.start()
            first.append(cp)
        cbuf[me] = c_ref[...]
        for cp in first:
            cp.wait_send()
            cp.wait_recv()
        cond = _silu(jnp.concatenate([cbuf[i] for i in range(N_DEV)], axis=0))
        mod = _mm_hi(cond, w_ref[...]) + b_ref[pl.ds(me, 1), :]
        for j in range(N_DEV):
            part[j] = mod[j:j + 1, :]
        second = []
        for k in range(1, N_DEV):
            peer = _peer(k)
            cp = pltpu.make_async_remote_copy(part.at[_lin(peer)], out_ref.at[me], s2.at[k - 1], r2.at[k - 1],
                                              device_id=peer, device_id_type=MESH)
            cp.start()
            second.append(cp)
        out_ref[me] = part[me]
        for cp in second:
            cp.wait_send()
            cp.wait_recv()

    def body(*refs):
        c_ref, w_ref, b_ref = refs[:3]
        ins = refs[3:3 + n]
        mod_ref = refs[3 + n]
        outs = refs[4 + n:4 + 2 * n]
        cbuf, part, s1, r1, s2, r2 = refs[4 + 2 * n:10 + 2 * n]
        gather = _TwoLevelGather(ins, outs, refs[10 + 2 * n:])
        gather.start()
        modulation(c_ref, w_ref, b_ref, mod_ref, cbuf, part, s1, r1, s2, r2)
        gather.forward()
        gather.finish()

    hbm = pl.BlockSpec(memory_space=pltpu.HBM)
    vm = pl.BlockSpec(memory_space=pltpu.VMEM)
    dma = pltpu.SemaphoreType.DMA
    res = pl.pallas_call(
        body, name="mod_and_gather", out_shape=out_shape, in_specs=[vm, vm, vm] + [hbm] * n,
        out_specs=[vm] + [hbm] * n,
        scratch_shapes=[pltpu.VMEM((N_DEV, 1, D_MODEL), F32), pltpu.VMEM((N_DEV, 1, chunk), F32)]
        + [dma((N_DEV - 1,))] * 4 + _g2_sems(n),
    )(c, ada_w, ada_b8, *arrs)
    return res[0], res[1:]


def _hosted_call(body, name, grid, in_specs, out_specs, out_shape, scratch_shapes, args, xchg, cparams):
    xchgs = [xchg] if isinstance(xchg, tuple) else list(xchg)
    grid = (grid,) if isinstance(grid, int) else tuple(grid)
    n_in, n_out, n_scr = len(in_specs), len(out_specs), len(scratch_shapes)
    windows = [[(a[1], a[2]) if isinstance(a, tuple) else None for a in group] for group, _ in xchgs]
    xchgs = [([a[0] if isinstance(a, tuple) else a for a in group], mode) for group, mode in xchgs]
    arrs = [a for group, _ in xchgs for a in group]
    n = len(arrs)
    x_shape, x_sems, sem_counts, stage_shape, stage_bufs = [], [], [], [], []
    for (group, mode), wins in zip(xchgs, windows):
        if mode == "two-level scatter":
            (a,) = group
            res_shape, stage, bufs, sems = _s2_shapes(a)
            shapes = [res_shape]
            stage_shape.append(stage)
            stage_bufs += bufs
        else:
            shapes, sems = _xchg_shapes(group, False if mode == "two-level" else mode)
        if mode == "two-level":
            sems = _g2_sems(len(group))
            shapes = [s if w is None else jax.ShapeDtypeStruct((N_DEV, w[1]) + a.shape[1:], a.dtype)
                      for s, w, a in zip(shapes, wins, group)]
        x_shape += shapes
        x_sems += sems
        sem_counts.append(len(sems))
    n_stage = len(stage_shape)
    n_steps = int(np.prod(grid))
    staged = ("two-level", "two-level scatter")

    def hosted(*refs):
        ins, refs = refs[:n_in], refs[n_in:]
        x_in, refs = refs[:n], refs[n:]
        outs, refs = refs[:n_out], refs[n_out:]
        x_out, refs = refs[:n], refs[n:]
        stages, refs = refs[:n_stage], refs[n_stage:]
        scr, refs = refs[:n_scr], refs[n_scr:]
        bufs, sems = refs[:2 * n_stage], refs[2 * n_stage:]
        step = pl.program_id(0)
        for d in range(1, len(grid)):
            step = step * grid[d] + pl.program_id(d)
        parts, a0, s0, t0 = [], 0, 0, 0
        for (group, mode), ns, wins in zip(xchgs, sem_counts, windows):
            gi, go, gs = x_in[a0:a0 + len(group)], x_out[a0:a0 + len(group)], sems[s0:s0 + ns]
            if mode == "two-level":
                parts.append((mode, _TwoLevelGather(gi, go, gs, wins)))
            elif mode == "two-level scatter":
                parts.append((mode, _TwoLevelScatter(gi[0], go[0], stages[t0], bufs[2 * t0], bufs[2 * t0 + 1], gs)))
                t0 += 1
            else:
                parts.append((mode, (gi, go, gs, mode)))
            a0, s0 = a0 + len(group), s0 + ns

        @pl.when(step == 0)
        def _():
            for mode, x in parts:
                if mode in staged:
                    x.start()
                else:
                    _xchg_start(*x)

        for kind, at in (("two-level", (2 * n_steps) // 3), ("two-level scatter", n_steps // 4)):
            if any(mode == kind for mode, _ in parts):
                @pl.when(step == at)
                def _():
                    for mode, x in parts:
                        if mode == kind:
                            x.forward()

        body(*ins, *outs, *scr)

        @pl.when(step == n_steps - 1)
        def _():
            for mode, x in parts:
                if mode in staged:
                    x.finish()
                else:
                    _xchg_wait(*x)

    hbm = pl.BlockSpec(memory_space=pltpu.HBM)
    res = pl.pallas_call(
        hosted, name=name, grid=grid, in_specs=list(in_specs) + [hbm] * n,
        out_specs=list(out_specs) + [hbm] * (n + n_stage), out_shape=list(out_shape) + x_shape + stage_shape,
        scratch_shapes=list(scratch_shapes) + stage_bufs + x_sems, compiler_params=cparams,
    )(*args, *arrs)
    return res[:n_out], res[n_out:n_out + n]


def _row(i):
    return (i, 0)


def _fixed(i):
    return (0, 0)


def _in_proj_fwd(x, norm1, scale1, shift1, w_in, tm, xchg):
    S = x.shape[0]

    def body(x_ref, n_ref, sc_ref, sh_ref, w_ref, qkv_ref, z_ref, xbc_ref, dt_ref):
        h = _modnorm(x_ref[...], n_ref[...], sc_ref[...], sh_ref[...])
        p = _mm_nt(h, w_ref[...])
        qkv_ref[...] = p[:, :768].astype(qkv_ref.dtype)
        z_ref[...] = p[:, 768:1280]
        xbc_ref[...] = p[:, 1280:2304]
        dt_ref[...] = p[:, 2304:IN_PAD]

    vec = pl.BlockSpec((1, D_MODEL), _fixed)
    return _hosted_call(
        body, "in_proj_fwd", S // tm,
        in_specs=[pl.BlockSpec((tm, D_MODEL), _row), vec, vec, vec, pl.BlockSpec((IN_PAD, D_MODEL), _fixed)],
        out_specs=[pl.BlockSpec((tm, 768), _row), pl.BlockSpec((tm, SSM_W), _row),
                   pl.BlockSpec((tm, XBC_W), _row), pl.BlockSpec((tm, LANE), _row)],
        out_shape=[jax.ShapeDtypeStruct((S, 768), MXU_DTYPE), jax.ShapeDtypeStruct((S, SSM_W), F32),
                   jax.ShapeDtypeStruct((S, XBC_W), F32), jax.ShapeDtypeStruct((S, LANE), F32)],
        scratch_shapes=[], args=(x, norm1, scale1, shift1, w_in), xchg=xchg, cparams=_cparams(VMEM_BIG),
    )


def _in_proj_bwd(x, dx1, dq, dkv, dz, dxbc, ddt, norm1, scale1, shift1, w_in, tm):
    S = x.shape[0]

    n_steps = S // tm
    half_cols = D_MODEL // 2

    def body(x_ref, dx1_ref, dq_ref, dkv_ref, dz_ref, dxbc_ref, ddt_ref, n_ref, sc_ref, sh_ref, w_ref,
             gx_ref, h_ref, acc_ref, gw_ref, gw_acc):
        i = pl.program_id(0)

        @pl.when(i == 0)
        def _():
            acc_ref[...] = jnp.zeros_like(acc_ref)
            gw_acc[...] = jnp.zeros_like(gw_acc)

        halves = [pl.ds(k * (tm // 2), tm // 2) for k in range(2)]
        dp = [jnp.concatenate([r[rows, :] for r in (dq_ref, dkv_ref, dz_ref, dxbc_ref, ddt_ref)], axis=1)
              for rows in halves]
        dh = [_mm(dp[k], w_ref[...]) for k in range(2)]
        parts = [_modnorm_parts(x_ref[rows, :]) for rows in halves]
        hb = [(parts[k][1] * n_ref[...] * (1.0 + sc_ref[...]) + sh_ref[...]).astype(h_ref.dtype) for k in range(2)]
        gw_acc[...] += _mm_tn(dp[0], hb[0][:, :half_cols]) + _mm_tn(dp[1], hb[1][:, :half_cols])
        bwd = [_modnorm_bwd(parts[k][0], parts[k][1], n_ref[...], sc_ref[...], dh[k]) for k in range(2)]
        for k, rows in enumerate(halves):
            gx_ref[rows, :] = dx1_ref[rows, :] + bwd[k][0]
            h_ref[rows, :] = hb[k]
        acc_ref[0:1, :] += bwd[0][1] + bwd[1][1]
        acc_ref[1:2, :] += bwd[0][2] + bwd[1][2]
        acc_ref[2:3, :] += bwd[0][3] + bwd[1][3]

        @pl.when(i == n_steps - 1)
        def _():
            gw_ref[...] = gw_acc[...].astype(gw_ref.dtype)

    vec = pl.BlockSpec((1, D_MODEL), _fixed)
    return pl.pallas_call(
        body, name="in_proj_bwd", grid=(n_steps,),
        in_specs=[pl.BlockSpec((tm, D_MODEL), _row), pl.BlockSpec((tm, D_MODEL), _row),
                  pl.BlockSpec((tm, ATTN_W), _row), pl.BlockSpec((tm, 2 * KV_W), _row),
                  pl.BlockSpec((tm, SSM_W), _row), pl.BlockSpec((tm, XBC_W), _row), pl.BlockSpec((tm, LANE), _row),
                  vec, vec, vec, pl.BlockSpec((IN_PAD, D_MODEL), _fixed)],
        out_specs=[pl.BlockSpec((tm, D_MODEL), _row), pl.BlockSpec((tm, D_MODEL), _row),
                   pl.BlockSpec((8, D_MODEL), _fixed), pl.BlockSpec((IN_PAD, half_cols), _fixed)],
        out_shape=[jax.ShapeDtypeStruct((S, D_MODEL), F32), jax.ShapeDtypeStruct((S, D_MODEL), MXU_DTYPE),
                   jax.ShapeDtypeStruct((8, D_MODEL), F32), jax.ShapeDtypeStruct((IN_PAD, half_cols), WIRE_DTYPE)],
        scratch_shapes=[pltpu.VMEM((IN_PAD, half_cols), F32)],
        compiler_params=_cparams(VMEM_BIG),
    )(x, dx1, dq, dkv, dz, dxbc, ddt, norm1, scale1, shift1, w_in)


def _out_stage(ya, ys0, ys1, z0, z1, an, sn0, sn1):
    half = SSM_W // 2
    a = _rms(ya, an, ATTN_W)
    g0 = _rms(ys0 * _silu(z0), sn0, half)
    g1 = _rms(ys1 * _silu(z1), sn1, half)
    return jnp.concatenate([a, g0, g1], axis=1)


def _out_stage_args(ya_ref, ys_ref, z_ref, an_ref, sn_ref):
    half = SSM_W // 2
    return (ya_ref[...], ys_ref[:, :half], ys_ref[:, half:], z_ref[:, :half], z_ref[:, half:],
            an_ref[...], sn_ref[:, :half], sn_ref[:, half:])


def _out_proj_fwd(x, ya, ys, z, an, sn, gate1, w_o, tm, xchg):
    S = x.shape[0]

    def body(x_ref, ya_ref, ys_ref, z_ref, an_ref, sn_ref, g_ref, w_ref, x1_ref):
        u = _out_stage(*_out_stage_args(ya_ref, ys_ref, z_ref, an_ref, sn_ref))
        x1_ref[...] = x_ref[...] + g_ref[...] * _mm(u, w_ref[...])

    half = pl.BlockSpec((tm, ATTN_W), _row)
    hvec = pl.BlockSpec((1, ATTN_W), _fixed)
    (x1,), x_out = _hosted_call(
        body, "out_proj_fwd", S // tm,
        in_specs=[pl.BlockSpec((tm, D_MODEL), _row), half, half, half, hvec, hvec,
                  pl.BlockSpec((1, D_MODEL), _fixed), pl.BlockSpec((D_MODEL, D_MODEL), _fixed)],
        out_specs=[pl.BlockSpec((tm, D_MODEL), _row)],
        out_shape=[jax.ShapeDtypeStruct((S, D_MODEL), F32)],
        scratch_shapes=[], args=(x, ya, ys, z, an, sn, gate1, w_o), xchg=xchg, cparams=_cparams(VMEM_BIG),
    )
    return x1, x_out


def _out_proj_bwd(dx1, ya, ys, z, an, sn, gate1, w_o, tm, xchg):
    S = dx1.shape[0]
    n_steps = S // tm

    def body(dx1_ref, ya_ref, ys_ref, z_ref, an_ref, sn_ref, g_ref, w_ref,
             dya_ref, dys_ref, dz_ref, gw_ref, acc_ref, gw_acc):
        i = pl.program_id(0)

        @pl.when(i == 0)
        def _():
            acc_ref[...] = jnp.zeros_like(acc_ref)
            gw_acc[...] = jnp.zeros_like(gw_acc)

        u, vjp = jax.vjp(_out_stage, *_out_stage_args(ya_ref, ys_ref, z_ref, an_ref, sn_ref))
        dx1 = dx1_ref[...]
        ub = u.astype(MXU_DTYPE)
        mix = _mm(ub, w_ref[...])
        dmix = dx1 * g_ref[...]
        dmixb = dmix.astype(MXU_DTYPE)
        du = _mm_nt(dmixb, w_ref[...])
        gw_acc[...] += _mm_tn(ub, dmixb)
        dya, dys0, dys1, dz0, dz1, dan, dsn0, dsn1 = vjp(du)
        dya_ref[...] = dya
        dys_ref[...] = jnp.concatenate([dys0, dys1], axis=1)
        dz_ref[...] = jnp.concatenate([dz0, dz1], axis=1).astype(dz_ref.dtype)
        acc_ref[0:1, :] += jnp.sum(dx1 * mix, axis=0, keepdims=True)
        acc_ref[1:2, :] += jnp.concatenate([dan, dsn0, dsn1], axis=1)

        @pl.when(i == n_steps - 1)
        def _():
            gw_ref[...] = gw_acc[...].astype(gw_ref.dtype)

    half = pl.BlockSpec((tm, ATTN_W), _row)
    hvec = pl.BlockSpec((1, ATTN_W), _fixed)
    full = pl.BlockSpec((tm, D_MODEL), _row)
    return _hosted_call(
        body, "out_proj_bwd", n_steps,
        in_specs=[full, half, half, half, hvec, hvec,
                  pl.BlockSpec((1, D_MODEL), _fixed), pl.BlockSpec((D_MODEL, D_MODEL), _fixed)],
        out_specs=[half, half, half, pl.BlockSpec((D_MODEL, D_MODEL), _fixed), pl.BlockSpec((8, D_MODEL), _fixed)],
        out_shape=[jax.ShapeDtypeStruct((S, ATTN_W), F32)] * 2 + [jax.ShapeDtypeStruct((S, ATTN_W), MXU_DTYPE),
                   jax.ShapeDtypeStruct((D_MODEL, D_MODEL), WIRE_DTYPE), jax.ShapeDtypeStruct((8, D_MODEL), F32)],
        scratch_shapes=[pltpu.VMEM((D_MODEL, D_MODEL), F32)],
        args=(dx1, ya, ys, z, an, sn, gate1, w_o), xchg=xchg, cparams=_cparams(VMEM_BIG),
    )


def _loss_rows(x2, fn, tgt):
    y = _rms(x2, fn, D_MODEL)
    per_row = jnp.sum(jnp.square(y - tgt), axis=1, keepdims=True)
    return jnp.sum(per_row, axis=0, keepdims=True) * (0.5 / D_MODEL)


def _mlp_loss(x1, tgt, norm2, scale2, shift2, gate2, fnorm, w_gu, w_d, tm):
    S = x1.shape[0]
    n_pieces = len(w_gu) + len(w_d)

    def body(*refs):
        x1_ref, t_ref, n_ref, sc_ref, sh_ref, g_ref, fn_ref = refs[:7]
        piece_refs = refs[7:7 + n_pieces]
        dx1_ref, h_ref, dgu_ref, act_ref, dmlp_ref, acc_ref, wgu, wd, wsem = refs[7 + n_pieces:]

        @pl.when(pl.program_id(0) == 0)
        def _():
            acc_ref[...] = jnp.zeros_like(acc_ref)
            copies = []
            for dst, pieces in ((wgu, piece_refs[:len(w_gu)]), (wd, piece_refs[len(w_gu):])):
                shard = sum(p.shape[1] for p in pieces)
                off = 0
                for p in pieces:
                    for j in range(N_DEV):
                        copies.append(pltpu.make_async_copy(p.at[j], dst.at[pl.ds(j * shard + off, p.shape[1])],
                                                            wsem.at[len(copies)]))
                    off += p.shape[1]
            for cp in copies:
                cp.start()
            for cp in copies:
                cp.wait()

        x1 = x1_ref[...]
        gate2 = g_ref[...]
        h, vjp_h = jax.vjp(_modnorm, x1, n_ref[...], sc_ref[...], sh_ref[...])
        hb = h.astype(MXU_DTYPE)
        gu = _mm_nt(hb, wgu[...])
        g, u = gu[:, :D_FF], gu[:, D_FF:]
        sg = jax.nn.sigmoid(g)
        silu_g = g * sg
        act = (silu_g * u).astype(MXU_DTYPE)
        mlp = _mm(act, wd[...])
        x2 = x1 + gate2 * mlp
        loss, vjp_loss = jax.vjp(_loss_rows, x2, fn_ref[...], t_ref[...])
        dx2, dfn, _ = vjp_loss(jnp.ones((1, 1), F32))
        dmlp = (dx2 * gate2).astype(MXU_DTYPE)
        dact = _mm_nt(dmlp, wd[...])
        dg = dact * u * (sg * (1.0 + g * (1.0 - sg)))
        du = dact * silu_g
        dgu = jnp.concatenate([dg, du], axis=1).astype(MXU_DTYPE)
        dh = _mm(dgu, wgu[...])
        dx, dn, dsc, dsh = vjp_h(dh)
        dx1_ref[...] = dx2 + dx
        h_ref[...] = hb
        dgu_ref[...] = dgu
        act_ref[...] = act
        dmlp_ref[...] = dmlp
        acc_ref[0:1, :] += dn
        acc_ref[1:2, :] += dsc
        acc_ref[2:3, :] += dsh
        acc_ref[3:4, :] += jnp.sum(dx2 * mlp, axis=0, keepdims=True)
        acc_ref[4:5, :] += dfn
        acc_ref[5:6, :] += jnp.broadcast_to(loss, (1, D_MODEL))

    full = pl.BlockSpec((tm, D_MODEL), _row)
    vec = pl.BlockSpec((1, D_MODEL), _fixed)
    anyspec = pl.BlockSpec(memory_space=pl.ANY)
    return pl.pallas_call(
        body, name="mlp_loss", grid=(S // tm,),
        in_specs=[full, full, vec, vec, vec, vec, vec] + [anyspec] * n_pieces,
        out_specs=[full, full, pl.BlockSpec((tm, 2 * D_FF), _row), pl.BlockSpec((tm, D_FF), _row), full,
                   pl.BlockSpec((8, D_MODEL), _fixed)],
        out_shape=[jax.ShapeDtypeStruct((S, D_MODEL), F32), jax.ShapeDtypeStruct((S, D_MODEL), MXU_DTYPE),
                   jax.ShapeDtypeStruct((S, 2 * D_FF), MXU_DTYPE), jax.ShapeDtypeStruct((S, D_FF), MXU_DTYPE),
                   jax.ShapeDtypeStruct((S, D_MODEL), MXU_DTYPE), jax.ShapeDtypeStruct((8, D_MODEL), F32)],
        scratch_shapes=[pltpu.VMEM((2 * D_FF, D_MODEL), MXU_DTYPE), pltpu.VMEM((D_FF, D_MODEL), MXU_DTYPE),
                        pltpu.SemaphoreType.DMA((N_DEV * n_pieces,))],
        compiler_params=_cparams(VMEM_BIG),
    )(x1, tgt, norm2, scale2, shift2, gate2, fnorm, *w_gu, *w_d)


def _wgrad(a, g, tk, ts, name, xchg=None, g_cols=None):
    pieces = list(a) if isinstance(a, (list, tuple)) else [a]
    S = pieces[0].shape[0]
    K = sum(p.shape[1] for p in pieces)
    assert len(pieces) == 1 or tk == K
    N, col = (g.shape[1], 0) if g_cols is None else g_cols
    ns = S // ts
    n_a = len(pieces)

    def body(*refs):
        a_refs, (g_ref, o_ref, acc_ref) = refs[:n_a], refs[n_a:]
        s = pl.program_id(1)

        @pl.when(s == 0)
        def _():
            acc_ref[...] = jnp.zeros_like(acc_ref)

        a_blk = a_refs[0][...] if n_a == 1 else jnp.concatenate([r[...] for r in a_refs], axis=1)
        acc_ref[...] += _mm_tn(a_blk, g_ref[...])

        @pl.when(s == ns - 1)
        def _():
            o_ref[...] = acc_ref[...].astype(o_ref.dtype)

    if n_a == 1:
        in_specs = [pl.BlockSpec((ts, tk), lambda j, s: (s, j))]
    else:
        in_specs = [pl.BlockSpec((ts, p.shape[1]), lambda j, s: (s, 0)) for p in pieces]
    in_specs.append(pl.BlockSpec((ts, N), lambda j, s: (s, col)))
    out_spec = pl.BlockSpec((tk, N), lambda j, s: (j, 0))
    out_shape = jax.ShapeDtypeStruct((K, N), WIRE_DTYPE)
    scratch = [pltpu.VMEM((tk, N), F32)]
    args = (*pieces, g)
    if xchg is None:
        return pl.pallas_call(body, name=name, grid=(K // tk, ns), in_specs=in_specs, out_specs=out_spec,
                              out_shape=out_shape, scratch_shapes=scratch, compiler_params=_cparams(VMEM_BIG))(*args)
    (out,), x_out = _hosted_call(body, name, (K // tk, ns), in_specs, [out_spec], [out_shape], scratch, args, xchg,
                                 _cparams(VMEM_BIG))
    return out, x_out


SSD_CHUNKS_PER_STEP = 4
SSD_BWD_CHUNKS_PER_STEP = 4
ATTN_BLOCKS_PER_STEP = 4
MASKED = -1e30
QK_SCALE = HALF ** -0.5


def _attn_bias(buckets, rel_bias):
    def body(bk_ref, relb_ref, out_ref):
        bk = bk_ref[...]
        i = lax.broadcasted_iota(jnp.int32, (BLK, 2 * BLK), 0)
        j = lax.broadcasted_iota(jnp.int32, (BLK, 2 * BLK), 1)
        window = (j > i) & (j <= i + BLK)
        for h in range(N_HEADS):
            acc = jnp.zeros((BLK, 2 * BLK), F32)
            for b in range(N_BUCKETS):
                acc = jnp.where(bk == b, relb_ref[b, h], acc)
            out_ref[0, h] = jnp.where(window, acc, MASKED)
            out_ref[1, h] = jnp.where(window & (j >= BLK), acc, MASKED)

    return pl.pallas_call(
        body, name="attn_bias", out_shape=jax.ShapeDtypeStruct((2, N_HEADS, BLK, 2 * BLK), F32),
        in_specs=[pl.BlockSpec(memory_space=pltpu.VMEM), pl.BlockSpec(memory_space=pltpu.SMEM)],
    )(buckets, rel_bias)


def _attn_fwd(qkv, bias, sinks, xchg):
    S = qkv.shape[0]
    nb = S // BLK

    nq = ATTN_BLOCKS_PER_STEP if nb % ATTN_BLOCKS_PER_STEP == 0 else 1
    rows = nq * BLK

    def body(q_ref, kvp_ref, kvc_ref, bias_ref, sinks_ref, y_ref):
        i = pl.program_id(0)
        q = q_ref[...].astype(F32) * QK_SCALE
        kv = jnp.concatenate([kvp_ref[...], kvc_ref[...]], axis=0).astype(F32)
        k_lo, k_hi = _split_pair(kv[:, :LANE])
        v_lo, v_hi = _split_pair(kv[:, LANE:])
        bands = [[t[b * BLK:(b + 2) * BLK].astype(MXU_DTYPE) for t in (k_lo, k_hi, v_lo, v_hi)] for b in range(nq)]
        q_heads = [_split_heads(q[b * BLK:(b + 1) * BLK], 4) for b in range(nq)]
        first = [jnp.where(i == 0, 1, 0) if b == 0 else 0 for b in range(nq)]
        items = [(b, h) for b in range(nq) for h in range(N_HEADS)]
        s = [_mm_nt(q_heads[b][h].astype(MXU_DTYPE), bands[b][h // 4]) + bias_ref[first[b], h] for b, h in items]
        m = [jnp.maximum(jnp.max(s[n], axis=-1, keepdims=True), sinks_ref[h]) for n, (b, h) in enumerate(items)]
        p = [jnp.exp(s[n] - m[n]) for n in range(len(items))]
        rinv = [1.0 / (jnp.sum(p[n], axis=-1, keepdims=True) + jnp.exp(sinks_ref[h] - m[n]))
                for n, (b, h) in enumerate(items)]
        out = [_mm(p[n], bands[b][2 + h // 4]) * rinv[n] for n, (b, h) in enumerate(items)]
        y_ref[...] = jnp.concatenate([_join_heads(out[b * N_HEADS:(b + 1) * N_HEADS]) for b in range(nq)], axis=0)

    smem = pl.BlockSpec(memory_space=pltpu.SMEM)
    return _hosted_call(
        body, "attn_fwd", nb // nq,
        in_specs=[pl.BlockSpec((rows, ATTN_W), _row),
                  pl.BlockSpec((BLK, 2 * KV_W), lambda i: (jnp.maximum(i * nq - 1, 0), 2)),
                  pl.BlockSpec((rows, 2 * KV_W), lambda i: (i, 2)),
                  pl.BlockSpec((2, N_HEADS, BLK, 2 * BLK), lambda i: (0, 0, 0, 0)), smem],
        out_specs=[pl.BlockSpec((rows, ATTN_W), _row)],
        out_shape=[jax.ShapeDtypeStruct((S, ATTN_W), F32)],
        scratch_shapes=[],
        args=(qkv, qkv, qkv, bias, sinks), xchg=xchg, cparams=_cparams(),
    )


def _attn_bwd(qkv, y, dy, bias, sinks, xchg):
    S = qkv.shape[0]
    nb = S // BLK
    nq = ATTN_BLOCKS_PER_STEP if nb % ATTN_BLOCKS_PER_STEP == 0 else 1
    rows, n_steps = nq * BLK, nb // nq

    def body(q_ref, kvp_ref, kvc_ref, y_ref, dy_ref, bias_ref, sinks_ref, dq_ref, dkv_ref, dbias_ref, dsk_ref, carry_ref):
        i = pl.program_id(0)

        @pl.when(i == 0)
        def _():
            dbias_ref[...] = jnp.zeros_like(dbias_ref)
            dsk_ref[...] = jnp.zeros_like(dsk_ref)
            carry_ref[...] = jnp.zeros_like(carry_ref)

        q = q_ref[...].astype(F32) * QK_SCALE
        kv = jnp.concatenate([kvp_ref[...], kvc_ref[...]], axis=0).astype(F32)
        k_lo, k_hi = _split_pair(kv[:, :LANE])
        v_lo, v_hi = _split_pair(kv[:, LANE:])
        bands = [[t[b * BLK:(b + 2) * BLK].astype(MXU_DTYPE) for t in (k_lo, k_hi, v_lo, v_hi)] for b in range(nq)]
        rows_of = lambda ref, b: ref[b * BLK:(b + 1) * BLK, :]
        first = [jnp.where(i == n_steps - 1, 1, 0) if b == 0 else 0 for b in range(nq)]
        items = [(b, h) for b in range(nq) for h in range(N_HEADS)]
        at = lambda b, h: b * N_HEADS + h
        q_heads = [hd for b in range(nq) for hd in _split_heads(q[b * BLK:(b + 1) * BLK], 4)]
        y_heads = [hd for b in range(nq) for hd in _split_heads(rows_of(y_ref, b), 4)]
        dy_heads = [hd for b in range(nq) for hd in _split_heads(rows_of(dy_ref, b), 4)]
        qs = [q_heads[n].astype(MXU_DTYPE) for n in range(len(items))]
        s = [_mm_nt(qs[at(b, h)], bands[b][h // 4]) + bias_ref[first[b], h] for b, h in items]
        m = [jnp.maximum(jnp.max(s[at(b, h)], axis=-1, keepdims=True), sinks_ref[h]) for b, h in items]
        p = [jnp.exp(s[n] - m[n]) for n in range(len(items))]
        esink = [jnp.exp(sinks_ref[h] - m[at(b, h)]) for b, h in items]
        rinv = [1.0 / (jnp.sum(p[n], axis=-1, keepdims=True) + esink[n]) for n in range(len(items))]
        t = [dy_heads[n] * rinv[n] for n in range(len(items))]
        delta = [jnp.sum(t[n] * y_heads[n], axis=-1, keepdims=True) for n in range(len(items))]
        tb = [t[n].astype(MXU_DTYPE) for n in range(len(items))]
        dp = [_mm_nt(tb[at(b, h)], bands[b][2 + h // 4]) for b, h in items]
        ds = [p[n] * (dp[n] - delta[n]) for n in range(len(items))]
        for h in range(N_HEADS):
            ds_h, dsk_h = ds[at(0, h)], esink[at(0, h)] * delta[at(0, h)]
            for b in range(1, nq):
                ds_h = ds_h + ds[at(b, h)]
                dsk_h = dsk_h + esink[at(b, h)] * delta[at(b, h)]
            dbias_ref[h] += ds_h
            dsk_ref[h] -= dsk_h
        dsb = [ds[n].astype(MXU_DTYPE) for n in range(len(items))]
        pb = [p[n].astype(MXU_DTYPE) for n in range(len(items))]
        dq_heads = [_mm(dsb[at(b, h)], bands[b][h // 4]) * QK_SCALE for b, h in items]
        grp = lambda lst, b, g: jnp.concatenate(lst[at(b, 4 * g):at(b, 4 * g) + 4], axis=0)
        dk_pads = [[_mm_tn(grp(dsb, b, g), grp(qs, b, g)) for g in range(2)] for b in range(nq)]
        dv_pads = [[_mm_tn(grp(pb, b, g), grp(tb, b, g)) for g in range(2)] for b in range(nq)]
        dq_ref[...] = jnp.concatenate([_join_heads(dq_heads[b * N_HEADS:(b + 1) * N_HEADS]) for b in range(nq)],
                                      axis=0).astype(dq_ref.dtype)
        part = lambda b, lo: jnp.concatenate(
            [_join_pair(d[b][0][lo:lo + BLK], d[b][1][lo:lo + BLK]) for d in (dk_pads, dv_pads)], axis=1)
        dkv = [part(b, BLK) + (part(b + 1, 0) if b + 1 < nq else carry_ref[...]) for b in range(nq)]
        dkv_ref[...] = jnp.concatenate(dkv, axis=0).astype(dkv_ref.dtype)
        carry_ref[...] = part(0, 0)

    smem = pl.BlockSpec(memory_space=pltpu.SMEM)
    rev = lambda i: (n_steps - 1 - i, 0)
    return _hosted_call(
        body, "attn_bwd", n_steps,
        in_specs=[pl.BlockSpec((rows, ATTN_W), rev),
                  pl.BlockSpec((BLK, 2 * KV_W), lambda i: (jnp.maximum((n_steps - 1 - i) * nq - 1, 0), 2)),
                  pl.BlockSpec((rows, 2 * KV_W), lambda i: (n_steps - 1 - i, 2)),
                  pl.BlockSpec((rows, ATTN_W), rev), pl.BlockSpec((rows, ATTN_W), rev),
                  pl.BlockSpec((2, N_HEADS, BLK, 2 * BLK), lambda i: (0, 0, 0, 0)), smem],
        out_specs=[pl.BlockSpec((rows, ATTN_W), rev), pl.BlockSpec((rows, 2 * KV_W), rev),
                   pl.BlockSpec((N_HEADS, BLK, 2 * BLK), lambda i: (0, 0, 0)),
                   pl.BlockSpec((N_HEADS, BLK, 1), lambda i: (0, 0, 0))],
        out_shape=[jax.ShapeDtypeStruct((S, ATTN_W), MXU_DTYPE), jax.ShapeDtypeStruct((S, 2 * KV_W), MXU_DTYPE),
                   jax.ShapeDtypeStruct((N_HEADS, BLK, 2 * BLK), F32), jax.ShapeDtypeStruct((N_HEADS, BLK, 1), F32)],
        scratch_shapes=[pltpu.VMEM((BLK, 2 * KV_W), F32)],
        args=(qkv, qkv, qkv, y, dy, bias, sinks), xchg=xchg, cparams=_cparams(),
    )


def _attn_finish(dbias, dsk, buckets):
    def body(db_ref, dsk_ref, bk_ref, drel_ref, dsink_ref):
        bk = bk_ref[...]
        r = lax.broadcasted_iota(jnp.int32, (N_BUCKETS, LANE), 0)
        l = lax.broadcasted_iota(jnp.int32, (N_BUCKETS, LANE), 1)
        row = lax.broadcasted_iota(jnp.int32, (N_HEADS, LANE), 0)
        res = jnp.zeros((N_BUCKETS, LANE), F32)
        dsink = jnp.zeros((N_HEADS, LANE), F32)
        for h in range(N_HEADS):
            db = db_ref[h]
            for b in range(N_BUCKETS):
                v = jnp.sum(jnp.sum(jnp.where(bk == b, db, 0.0), axis=1, keepdims=True), axis=0, keepdims=True)
                res = res + jnp.where((r == b) & (l == h), v, 0.0)
            dsink = dsink + jnp.where(row == h, jnp.sum(dsk_ref[h], axis=0, keepdims=True), 0.0)
        drel_ref[...] = res
        dsink_ref[...] = dsink

    return pl.pallas_call(body, name="attn_finish",
                          out_shape=[jax.ShapeDtypeStruct((N_BUCKETS, LANE), F32),
                                     jax.ShapeDtypeStruct((N_HEADS, LANE), F32)])(dbias, dsk, buckets)


def _ssd_consts():
    r = lax.broadcasted_iota(jnp.int32, (BLK, BLK), 0)
    c = lax.broadcasted_iota(jnp.int32, (BLK, BLK), 1)
    causal = c <= r
    upper = (r <= c).astype(F32)
    last = r == BLK - 1
    head = lax.broadcasted_iota(jnp.int32, (N_HEADS, BLK), 0)
    return causal, upper, last, head


def _ssd_chunks(xs, bg, cg, dt_raw_t, prev0, dtb, alog, d_rows, consts):
    causal, upper, last, head = consts
    nq = len(xs)
    items = [(c, h) for c in range(nq) for h in range(N_HEADS)]
    at = lambda c, h: c * N_HEADS + h
    a_neg = -jnp.exp(alog)
    dt_t = [_softplus(dt_raw_t[c] + dtb) for c in range(nq)]
    acs_t = [_mm_hi(dt_t[c] * a_neg, upper) for c in range(nq)]
    cb = [[_mm_nt(cg[c][g], bg[c][g]) for g in range(2)] for c in range(nq)]
    pick = lambda t, h: jnp.sum(jnp.where(head == h, t, 0.0), axis=0, keepdims=True)
    dt_row = [pick(dt_t[c], h) for c, h in items]
    a_row = [pick(acs_t[c], h) for c, h in items]
    a_rb = [jnp.broadcast_to(a_row[n], (BLK, BLK)) for n in range(len(items))]
    a_b = [a_rb[n].T for n in range(len(items))]
    a_last = [jnp.sum(jnp.where(last, a_b[n], 0.0), axis=0, keepdims=True) for n in range(len(items))]
    w = [cb[c][h // 4] * jnp.exp(jnp.where(causal, a_b[at(c, h)] - a_rb[at(c, h)], -1e30)) * dt_row[at(c, h)]
         for c, h in items]
    f_b = [jnp.broadcast_to(dt_row[n] * jnp.exp(a_last[n] - a_row[n]), (BLK, BLK)).T for n in range(len(items))]
    y_in = [_mm(w[at(c, h)], xs[c][h]) for c, h in items]
    st = [_mm_tn(bg[c][h // 4], xs[c][h] * f_b[at(c, h)]) for c, h in items]
    e_b = [jnp.exp(a_b[n]) for n in range(len(items))]
    states = [list(prev0)]
    for c in range(nq):
        states.append([states[c][h] * jnp.exp(a_last[at(c, h)]) + st[at(c, h)] for h in range(N_HEADS)])
    y_off = [_mm(cg[c][h // 4], states[c][h]) * e_b[at(c, h)] for c, h in items]
    ys = [[y_in[at(c, h)] + y_off[at(c, h)] + d_rows[h] * xs[c][h] for h in range(N_HEADS)] for c in range(nq)]
    return ys, states


def _ssd_chunks_bwd(xs, bg, cg, dt_raw_t, prev, dtb, alog, d_rows, dys, dh_last, consts):
    causal, upper, last, head = consts
    nq = len(xs)
    items = [(c, h) for c in range(nq) for h in range(N_HEADS)]
    ni = len(items)
    at = lambda c, h: c * N_HEADS + h
    groups = [(c, g) for c in range(nq) for g in range(2)]
    lane = _lane_iota((BLK, BLK))
    lane_row = _lane_iota((1, BLK))
    a_neg = -jnp.exp(alog)
    pre_dt = [dt_raw_t[c] + dtb for c in range(nq)]
    dt_t = [_softplus(pre_dt[c]) for c in range(nq)]
    acs_t = [_mm_hi(dt_t[c] * a_neg, upper) for c in range(nq)]
    pick = lambda t, h: jnp.sum(jnp.where(head == h, t, 0.0), axis=0, keepdims=True)
    full_sum = lambda t: jnp.sum(jnp.sum(t, axis=1, keepdims=True), axis=0, keepdims=True)
    dt_row = [pick(dt_t[c], h) for c, h in items]
    a_row = [pick(acs_t[c], h) for c, h in items]
    a_rb = [jnp.broadcast_to(a_row[n], (BLK, BLK)) for n in range(ni)]
    a_b = [a_rb[n].T for n in range(ni)]
    a_last = [jnp.sum(jnp.where(last, a_b[n], 0.0), axis=0, keepdims=True) for n in range(ni)]
    lm = [jnp.exp(jnp.where(causal, a_b[n] - a_rb[n], -1e30)) for n in range(ni)]
    cgb = [[cg[c][g].astype(MXU_DTYPE) for g in range(2)] for c in range(nq)]
    bgb = [[bg[c][g].astype(MXU_DTYPE) for g in range(2)] for c in range(nq)]
    cb = [[_mm_nt(cgb[c][g], bgb[c][g]) for g in range(2)] for c in range(nq)]
    u = [cb[c][h // 4] * lm[at(c, h)] for c, h in items]
    w = [(u[n] * dt_row[n]).astype(MXU_DTYPE) for n in range(ni)]
    e_row = [jnp.exp(a_last[n] - a_row[n]) for n in range(ni)]
    f_row = [dt_row[n] * e_row[n] for n in range(ni)]
    f_b = [jnp.broadcast_to(f_row[n], (BLK, BLK)).T for n in range(ni)]
    e_b = [jnp.exp(a_b[n]) for n in range(ni)]
    el = [jnp.exp(a_last[n]) for n in range(ni)]
    xb = [xs[c][h].astype(MXU_DTYPE) for c, h in items]
    dyb = [dys[c][h].astype(MXU_DTYPE) for c, h in items]
    prevb = [prev[c][h].astype(MXU_DTYPE) for c, h in items]
    gmat = [_mm(cgb[c][h // 4], prevb[at(c, h)]) for c, h in items]
    dw = [_mm_nt(dyb[n], xb[n]) for n in range(ni)]
    dg = [dys[c][h] * e_b[at(c, h)] for c, h in items]
    dgb = [dg[n].astype(MXU_DTYPE) for n in range(ni)]
    from_y = [_mm_tn(cgb[c][h // 4], dgb[at(c, h)]) for c, h in items]
    dhs = [None] * ni
    dprev = [None] * ni
    for c in reversed(range(nq)):
        for h in range(N_HEADS):
            dhs[at(c, h)] = dh_last[h] if c == nq - 1 else dprev[at(c + 1, h)]
            dprev[at(c, h)] = from_y[at(c, h)] + dhs[at(c, h)] * el[at(c, h)]
    dstb = [dhs[n].astype(MXU_DTYPE) for n in range(ni)]
    dxf = [_mm(bgb[c][h // 4], dstb[at(c, h)]) for c, h in items]
    xfb = [(xs[c][h] * f_b[at(c, h)]).astype(MXU_DTYPE) for c, h in items]
    dxs = [_mm_tn(w[at(c, h)], dyb[at(c, h)]) + d_rows[h] * dys[c][h] + f_b[at(c, h)] * dxf[at(c, h)]
           for c, h in items]
    dd_item = [jnp.sum(dys[c][h] * xs[c][h], axis=0, keepdims=True) for c, h in items]
    dcg_h = [_mm_nt(dgb[n], prevb[n]) for n in range(ni)]
    dbg_h = [_mm_nt(xfb[n], dstb[n]) for n in range(ni)]
    zt = [dw[n] * u[n] for n in range(ni)]
    dseg = [zt[n] * dt_row[n] for n in range(ni)]
    dcb_h = [dw[n] * lm[n] * dt_row[n] for n in range(ni)]
    four = lambda lst, c, g: lst[at(c, 4 * g)] + lst[at(c, 4 * g + 1)] + lst[at(c, 4 * g + 2)] + lst[at(c, 4 * g + 3)]
    dcb = {(c, g): four(dcb_h, c, g).astype(MXU_DTYPE) for c, g in groups}
    dcg = [[four(dcg_h, c, g) + _mm(dcb[c, g], bgb[c][g]) for g in range(2)] for c in range(nq)]
    dbg = [[four(dbg_h, c, g) + _mm_tn(dcb[c, g], cgb[c][g]) for g in range(2)] for c in range(nq)]
    r1 = [jnp.sum(dg[n] * gmat[n] + dseg[n], axis=1, keepdims=True) for n in range(ni)]
    r2 = [jnp.sum(dxf[at(c, h)] * xs[c][h], axis=1, keepdims=True) for c, h in items]
    tt = [jnp.where(lane < HALF, jnp.broadcast_to(r1[n], (BLK, BLK)), jnp.broadcast_to(r2[n], (BLK, BLK))).T
          for n in range(ni)]
    r1_row = [tt[n][0:1, :] for n in range(ni)]
    r2_row = [tt[n][HALF:HALF + 1, :] for n in range(ni)]
    d_el = [full_sum(dhs[at(c, h)] * prev[c][h]) for c, h in items]
    da_last = [jnp.sum(r2_row[n] * f_row[n], axis=1, keepdims=True) + el[n] * d_el[n] for n in range(ni)]
    da_row = [r1_row[n] - jnp.sum(dseg[n], axis=0, keepdims=True) - r2_row[n] * f_row[n]
              + jnp.where(lane_row == BLK - 1, da_last[n], 0.0) for n in range(ni)]
    ddt_row = [jnp.sum(zt[n], axis=0, keepdims=True) + r2_row[n] * e_row[n] for n in range(ni)]
    draw, dalog = [], jnp.zeros((N_HEADS, BLK), F32)
    for c in range(nq):
        da_t = jnp.zeros((N_HEADS, BLK), F32)
        ddt_t = jnp.zeros((N_HEADS, BLK), F32)
        for h in range(N_HEADS):
            da_t = jnp.where(head == h, da_row[at(c, h)], da_t)
            ddt_t = jnp.where(head == h, ddt_row[at(c, h)], ddt_t)
        d_dta = _mm_hi(da_t, causal.astype(F32))
        dalog = dalog + d_dta * dt_t[c] * a_neg
        draw.append((ddt_t + d_dta * a_neg) * jax.nn.sigmoid(pre_dt[c]))
    ddtb = draw[0]
    for c in range(1, nq):
        ddtb = ddtb + draw[c]
    dd_rows = []
    for h in range(N_HEADS):
        t = dd_item[at(0, h)]
        for c in range(1, nq):
            t = t + dd_item[at(c, h)]
        dd_rows.append(t)
    return ([dxs[c * N_HEADS:(c + 1) * N_HEADS] for c in range(nq)], dbg, dcg, draw,
            [dprev[at(0, h)] for h in range(N_HEADS)], ddtb, dalog, dd_rows)


def _dt_rows(dt_blk):
    return dt_blk.T[:N_HEADS]


def _conv_pre(halo, blk, cw_ref, cb_ref):
    ext = jnp.concatenate([halo, blk], axis=0)
    taps = [pltpu.roll(ext, 3 - k, 0)[8:] for k in range(3)] + [blk]
    pre = cb_ref[...] + cw_ref[0:1, :] * taps[0]
    for k in range(1, 4):
        pre = pre + cw_ref[k:k + 1, :] * taps[k]
    return pre


def _ssd_split(pre):
    heads = _split_heads(pre[:, :SSM_W], 4)
    pb = [pre[:, SSM_W + g * D_STATE:SSM_W + (g + 1) * D_STATE] for g in range(2)]
    pc = [pre[:, SSM_W + 2 * D_STATE + g * D_STATE:SSM_W + 2 * D_STATE + (g + 1) * D_STATE] for g in range(2)]
    return heads, pb, pc


def _ssd_fwd(xbc, dt_raw, conv_w, conv_b, dtb_row, alog_row, d_exp, xchg):
    S = xbc.shape[0]
    nc = S // BLK
    nq = SSD_CHUNKS_PER_STEP if nc % SSD_CHUNKS_PER_STEP == 0 else 1
    rows = nq * BLK

    def body(xbc_ref, halo_ref, dt_ref, cw_ref, cb_ref, dtb_ref, alog_ref, d_ref, y_ref, prev_ref, pre_ref, state_ref):
        i = pl.program_id(0)

        @pl.when(i == 0)
        def _():
            state_ref[...] = jnp.zeros_like(state_ref)

        halo = halo_ref[...] * jnp.where(i > 0, 1.0, 0.0)
        pre = _conv_pre(halo, xbc_ref[...], cw_ref, cb_ref)
        pre_ref[...] = pre
        xc = _silu(pre)
        split = [_ssd_split(xc[c * BLK:(c + 1) * BLK]) for c in range(nq)]
        dt_t = [_dt_rows(dt_ref[c * BLK:(c + 1) * BLK, :]) for c in range(nq)]
        prev0 = [state_ref[h] for h in range(N_HEADS)]
        d_rows = [d_ref[h:h + 1, :] for h in range(N_HEADS)]
        ys, states = _ssd_chunks([s[0] for s in split], [s[1] for s in split], [s[2] for s in split], dt_t, prev0,
                                 dtb_ref[...], alog_ref[...], d_rows, _ssd_consts())
        for h in range(N_HEADS):
            for c in range(nq):
                prev_ref[c, h] = states[c][h]
            state_ref[h] = states[nq][h]
        y_ref[...] = jnp.concatenate([_join_heads(ys[c]) for c in range(nq)], axis=0)

    vec = pl.BlockSpec((N_HEADS, LANE), _fixed)
    return _hosted_call(
        body, "ssd_fwd", nc // nq,
        in_specs=[pl.BlockSpec((rows, XBC_W), _row),
                  pl.BlockSpec((8, XBC_W), lambda i: (jnp.maximum(i * (rows // 8) - 1, 0), 0)),
                  pl.BlockSpec((rows, LANE), _row),
                  pl.BlockSpec((4, XBC_W), _fixed), pl.BlockSpec((1, XBC_W), _fixed), vec, vec,
                  pl.BlockSpec((N_HEADS, LANE), _fixed)],
        out_specs=[pl.BlockSpec((rows, SSM_W), _row),
                   pl.BlockSpec((nq, N_HEADS, D_STATE, LANE), lambda i: (i, 0, 0, 0)),
                   pl.BlockSpec((rows, XBC_W), _row)],
        out_shape=[jax.ShapeDtypeStruct((S, SSM_W), F32), jax.ShapeDtypeStruct((nc, N_HEADS, D_STATE, LANE), F32),
                   jax.ShapeDtypeStruct((S, XBC_W), F32)],
        scratch_shapes=[pltpu.VMEM((N_HEADS, D_STATE, LANE), F32)],
        args=(xbc, xbc, dt_raw, conv_w, conv_b, dtb_row, alog_row, d_exp), xchg=xchg, cparams=_cparams(),
    )


def _ssd_bwd(xbc, pre_act, dt_raw, prev_states, dy, conv_w, dtb_row, alog_row, d_exp, xchg):
    S = xbc.shape[0]
    nc = S // BLK
    nq = SSD_BWD_CHUNKS_PER_STEP if nc % SSD_BWD_CHUNKS_PER_STEP == 0 else 1
    rows, n_steps = nq * BLK, nc // nq

    def body(xbc_ref, halo_ref, pre_ref, dt_ref, prev_ref, dy_ref, cw_ref, dtb_ref, alog_ref, d_ref,
             dxbc_ref, ddt_ref, dcw_ref, dvec_ref, dd_ref, gstate_ref, ghalo_ref):
        i = pl.program_id(0)

        @pl.when(i == 0)
        def _():
            gstate_ref[...] = jnp.zeros_like(gstate_ref)
            ghalo_ref[...] = jnp.zeros_like(ghalo_ref)
            dcw_ref[...] = jnp.zeros_like(dcw_ref)
            dvec_ref[...] = jnp.zeros_like(dvec_ref)
            dd_ref[...] = jnp.zeros_like(dd_ref)

        halo = halo_ref[...] * jnp.where(i < n_steps - 1, 1.0, 0.0)
        ext = jnp.concatenate([halo, xbc_ref[...]], axis=0)
        pre = pre_ref[...]
        sig = jax.nn.sigmoid(pre)
        xc = pre * sig
        split = [_ssd_split(xc[c * BLK:(c + 1) * BLK]) for c in range(nq)]
        dt_t = [_dt_rows(dt_ref[c * BLK:(c + 1) * BLK, :]) for c in range(nq)]
        prev = [[prev_ref[c, h] for h in range(N_HEADS)] for c in range(nq)]
        d_rows = [d_ref[h:h + 1, :] for h in range(N_HEADS)]
        dys = [_split_heads(dy_ref[c * BLK:(c + 1) * BLK, :], 4) for c in range(nq)]
        dh_last = [gstate_ref[h] for h in range(N_HEADS)]
        dheads, dpb, dpc, ddt_t, dprev0, ddtb, dalog, dd_rows = _ssd_chunks_bwd(
            [s[0] for s in split], [s[1] for s in split], [s[2] for s in split], dt_t, prev, dtb_ref[...],
            alog_ref[...], d_rows, dys, dh_last, _ssd_consts())
        for h in range(N_HEADS):
            gstate_ref[h] = dprev0[h]
            dd_ref[h:h + 1, :] += dd_rows[h]
        pad = jnp.zeros((BLK - N_HEADS, BLK), F32)
        ddt_ref[...] = jnp.concatenate([jnp.concatenate([ddt_t[c], pad], axis=0).T for c in range(nq)],
                                       axis=0).astype(ddt_ref.dtype)
        dvec_ref[0:N_HEADS, :] += ddtb
        dvec_ref[N_HEADS:, :] += dalog
        dxc = jnp.concatenate([jnp.concatenate([_join_heads(dheads[c])] + list(dpb[c]) + list(dpc[c]), axis=1)
                               for c in range(nq)], axis=0)
        dpre = dxc * (sig * (1.0 + pre * (1.0 - sig)))
        zeros8 = jnp.zeros((8, XBC_W), F32)
        dpe = jnp.concatenate([zeros8, dpre, zeros8], axis=0)
        n_ext = 16 + rows
        shifted = [pltpu.roll(dpe, n_ext - (3 - k), 0)[:8 + rows] for k in range(3)] + [dpe[:8 + rows]]
        dext = cw_ref[0:1, :] * shifted[0]
        for k in range(1, 4):
            dext = dext + cw_ref[k:k + 1, :] * shifted[k]
        for k in range(4):
            dcw_ref[k:k + 1, :] += jnp.sum(shifted[k] * ext, axis=0, keepdims=True)
        dcw_ref[4:5, :] += jnp.sum(dpre, axis=0, keepdims=True)
        dxbc_ref[...] = jnp.concatenate([dext[8:rows], dext[rows:] + ghalo_ref[...]], axis=0).astype(dxbc_ref.dtype)
        ghalo_ref[...] = dext[:8, :]

    vec = pl.BlockSpec((N_HEADS, LANE), _fixed)
    rev = lambda i: (n_steps - 1 - i, 0)
    return _hosted_call(
        body, "ssd_bwd", n_steps,
        in_specs=[pl.BlockSpec((rows, XBC_W), rev),
                  pl.BlockSpec((8, XBC_W), lambda i: (jnp.maximum((n_steps - 1 - i) * (rows // 8) - 1, 0), 0)),
                  pl.BlockSpec((rows, XBC_W), rev),
                  pl.BlockSpec((rows, LANE), rev),
                  pl.BlockSpec((nq, N_HEADS, D_STATE, LANE), lambda i: (n_steps - 1 - i, 0, 0, 0)),
                  pl.BlockSpec((rows, SSM_W), rev),
                  pl.BlockSpec((4, XBC_W), _fixed), vec, vec,
                  pl.BlockSpec((N_HEADS, LANE), _fixed)],
        out_specs=[pl.BlockSpec((rows, XBC_W), rev), pl.BlockSpec((rows, LANE), rev),
                   pl.BlockSpec((8, XBC_W), _fixed), pl.BlockSpec((2 * N_HEADS, LANE), _fixed),
                   pl.BlockSpec((N_HEADS, LANE), _fixed)],
        out_shape=[jax.ShapeDtypeStruct((S, XBC_W), MXU_DTYPE), jax.ShapeDtypeStruct((S, LANE), MXU_DTYPE),
                   jax.ShapeDtypeStruct((8, XBC_W), F32), jax.ShapeDtypeStruct((2 * N_HEADS, LANE), F32),
                   jax.ShapeDtypeStruct((N_HEADS, LANE), F32)],
        scratch_shapes=[pltpu.VMEM((N_HEADS, D_STATE, LANE), F32), pltpu.VMEM((8, XBC_W), F32)],
        args=(xbc, xbc, pre_act, dt_raw, prev_states, dy, conv_w, dtb_row, alog_row, d_exp), xchg=xchg,
        cparams=_cparams(VMEM_BIG),
    )


def _adamw_math(w, g, m, v):
    m = ADAM_B1 * m + (1.0 - ADAM_B1) * g
    v = ADAM_B2 * v + (1.0 - ADAM_B2) * jnp.square(g)
    m_hat = m / (1.0 - ADAM_B1 ** ADAM_STEP)
    v_hat = v / (1.0 - ADAM_B2 ** ADAM_STEP)
    delta = -ADAM_LR * (m_hat / (jnp.sqrt(v_hat) + ADAM_EPS) + ADAM_WD * w)
    return delta, m, v


def _reduce_adamw_halves(part_a, part_b, w, m, v, name):
    R, C = w.shape
    P = part_a.shape[0]
    tl = 256
    n = C // tl

    def body(a_ref, b_ref, w_ref, m_ref, v_ref, g_ref, d_ref, nm_ref, nv_ref):
        ga, gb = a_ref[0].astype(F32), b_ref[0].astype(F32)
        for i in range(1, P):
            ga, gb = ga + a_ref[i].astype(F32), gb + b_ref[i].astype(F32)
        first = jnp.where(pl.program_id(0) < n // 2, 1.0, 0.0)
        g = ga * first + gb * (1.0 - first)
        d, nm, nv = _adamw_math(w_ref[...], g, m_ref[...], v_ref[...])
        g_ref[...] = g
        d_ref[...] = d
        nm_ref[...] = nm
        nv_ref[...] = nv

    blk = pl.BlockSpec((R, tl), lambda i: (0, i))
    return pl.pallas_call(
        body, name=name, grid=(n,),
        in_specs=[pl.BlockSpec((P, R, tl), lambda i: (0, 0, jnp.minimum(i, n // 2 - 1))),
                  pl.BlockSpec((P, R, tl), lambda i: (0, 0, jnp.maximum(i - n // 2, 0))), blk, blk, blk],
        out_specs=[blk] * 4, out_shape=[jax.ShapeDtypeStruct((R, C), F32)] * 4,
    )(part_a, part_b, w, m, v)


def _reduce_adamw_hosting(parts_list, wmv_list, name, xchg):
    n_arr = len(parts_list)
    pieces = [list(p) if isinstance(p, (tuple, list)) else [p] for p in parts_list]
    n_pieces = sum(len(p) for p in pieces)
    C = wmv_list[0][0].shape[1]
    tl = 256

    def total(ref):
        g = ref[0].astype(F32)
        for i in range(1, N_DEV):
            g = g + ref[i].astype(F32)
        return g

    def body(*refs):
        p_refs, wmv_refs, o_refs = refs[:n_pieces], refs[n_pieces:n_pieces + 3 * n_arr], refs[n_pieces + 3 * n_arr:]
        at = 0
        for k in range(n_arr):
            sums = [total(r) for r in p_refs[at:at + len(pieces[k])]]
            at += len(pieces[k])
            g = sums[0] if len(sums) == 1 else jnp.concatenate(sums, axis=0)
            w_ref, m_ref, v_ref = wmv_refs[3 * k:3 * k + 3]
            d, nm, nv = _adamw_math(w_ref[...], g, m_ref[...], v_ref[...])
            for o, val in zip(o_refs[4 * k:4 * k + 4], (g, d, nm, nv)):
                o[...] = val

    in_specs = [pl.BlockSpec((N_DEV, p.shape[1], tl), lambda i: (0, 0, i)) for group in pieces for p in group]
    in_specs += [pl.BlockSpec((w.shape[0], tl), lambda i: (0, i)) for w, _, _ in wmv_list for _ in range(3)]
    out_specs = [pl.BlockSpec((w.shape[0], tl), lambda i: (0, i)) for w, _, _ in wmv_list for _ in range(4)]
    out_shape = [jax.ShapeDtypeStruct(w.shape, F32) for w, _, _ in wmv_list for _ in range(4)]
    args = [p for group in pieces for p in group] + [a for wmv in wmv_list for a in wmv]
    outs, x_out = _hosted_call(body, name, C // tl, in_specs, out_specs, out_shape, [], args, xchg,
                               _cparams(VMEM_BIG))
    return [outs[4 * k:4 * k + 4] for k in range(n_arr)], x_out


_SMALL_NAMES = ("ada_b", "norm1", "conv_w", "conv_b", "dt_bias", "A_log", "D_skip", "sinks", "attn_out_norm",
                "ssm_out_norm", "norm2", "rel_bias", "final_norm")
N_MOD = 6 * D_MODEL


def _mod_row(a0, a1, a2):
    return jnp.concatenate([a0[2:3], a0[1:2], a1[0:1], a2[2:3], a2[1:2], a2[3:4]], axis=1)


def _small_update(gathered, params):
    n_g = len(gathered)
    flat = [a for name in _SMALL_NAMES for a in params[name]]

    def body(*refs):
        a0_ref, a1_ref, a2_ref, cw_ref, dv_ref, dd_ref, ds_ref, dr_ref, c_ref = refs[:n_g]
        wmv = refs[n_g:n_g + len(flat)]
        outs = refs[n_g + len(flat):]

        def total(ref):
            t = ref[0]
            for i in range(1, N_DEV):
                t = t + ref[i]
            return t

        t0, t1, t2, tcw, tdv, tdd, tds, tdr = [total(r) for r in (a0_ref, a1_ref, a2_ref, cw_ref, dv_ref, dd_ref,
                                                                   ds_ref, dr_ref)]
        r8 = lax.broadcasted_iota(jnp.int32, (N_HEADS, LANE), 0)
        l8 = lax.broadcasted_iota(jnp.int32, (N_HEADS, LANE), 1)

        def diag_row(t):
            return jnp.sum(jnp.where(r8 == l8, t, 0.0), axis=0, keepdims=True)[:, :N_HEADS]

        def lane_sums(t):
            return diag_row(jnp.broadcast_to(jnp.sum(t, axis=1, keepdims=True), (N_HEADS, LANE)))

        me = _lin(_my_pos())
        n_cw = XBC_W // N_DEV
        cw_mine = jnp.zeros((4, n_cw), F32)
        for j in range(N_DEV):
            cw_mine = cw_mine + tcw[0:4, j * n_cw:(j + 1) * n_cw] * jnp.where(me == j, 1.0, 0.0)
        grads = {
            "ada_b": _mod_row(t0, t1, t2), "norm1": t0[0:1], "conv_w": cw_mine, "conv_b": tcw[4:5],
            "dt_bias": lane_sums(tdv[:N_HEADS]), "A_log": lane_sums(tdv[N_HEADS:]), "D_skip": lane_sums(tdd),
            "sinks": diag_row(tds), "attn_out_norm": t1[1:2, :ATTN_W], "ssm_out_norm": t1[1:2, ATTN_W:],
            "norm2": t2[0:1], "rel_bias": tdr[:, :N_HEADS], "final_norm": t2[4:5],
        }
        for k, name in enumerate(_SMALL_NAMES):
            w_ref, m_ref, v_ref = wmv[3 * k:3 * k + 3]
            g = grads[name]
            d, nm, nv = _adamw_math(w_ref[...], g, m_ref[...], v_ref[...])
            for o, val in zip(outs[4 * k:4 * k + 4], (g, d, nm, nv)):
                o[...] = val
        loss_ref, call_ref, dmod_ref = outs[4 * len(_SMALL_NAMES):]
        loss_ref[...] = t2[5:6, 0:1]
        call_ref[...] = jnp.concatenate([c_ref[i] for i in range(N_DEV)], axis=0)
        dmod_ref[...] = jnp.concatenate([_mod_row(a0_ref[i], a1_ref[i], a2_ref[i]) for i in range(N_DEV)], axis=0)

    out_shape = [jax.ShapeDtypeStruct(params[name][0].shape, F32) for name in _SMALL_NAMES for _ in range(4)]
    out_shape += [jax.ShapeDtypeStruct((1, 1), F32), jax.ShapeDtypeStruct((N_DEV, D_MODEL), F32),
                  jax.ShapeDtypeStruct((N_DEV, N_MOD), F32)]
    res = pl.pallas_call(body, name="small_update", out_shape=out_shape)(*gathered, *flat)
    upd = {name: res[4 * k:4 * k + 4] for k, name in enumerate(_SMALL_NAMES)}
    loss, c_all, dmod_all = res[4 * len(_SMALL_NAMES):]
    return upd, loss, c_all, dmod_all


def _ada_w_update(c_all, dmod_all, w, m, v):
    chunk = w.shape[1]

    def body(c_ref, dm_ref, w_ref, m_ref, v_ref, g_ref, d_ref, nm_ref, nv_ref):
        me = _lin(_my_pos())
        dm = jnp.zeros((N_DEV, chunk), F32)
        for j in range(N_DEV):
            dm = dm + dm_ref[:, j * chunk:(j + 1) * chunk] * jnp.where(me == j, 1.0, 0.0)
        g = lax.dot_general(_silu(c_ref[...]), dm, (((0,), (0,)), ((), ())), precision=HI,
                            preferred_element_type=F32)
        d, nm, nv = _adamw_math(w_ref[...], g, m_ref[...], v_ref[...])
        g_ref[...] = g
        d_ref[...] = d
        nm_ref[...] = nm
        nv_ref[...] = nv

    tr = 256
    blk = pl.BlockSpec((tr, chunk), _row)
    return pl.pallas_call(
        body, name="ada_w_update", grid=(w.shape[0] // tr,),
        in_specs=[pl.BlockSpec((N_DEV, tr), lambda i: (0, i)), pl.BlockSpec(dmod_all.shape, _fixed), blk, blk, blk],
        out_specs=[blk] * 4, out_shape=[jax.ShapeDtypeStruct(w.shape, F32)] * 4,
    )(c_all, dmod_all, w, m, v)


def _local_step(x, tgt, c, mod, w_in, conv_w, w_o_mine, w_gu_mine, w_d_mine, p):
    S = x.shape[0]
    tm = min(512, S)
    tmm = min(256, S)
    tw = min(2048, S)
    shift1, scale1, gate1, shift2, scale2, gate2 = [mod[i:i + 1] for i in range(6)]
    buckets = jnp.asarray(_t5_bucket_table())
    per_head = lambda a: jnp.broadcast_to(a.reshape(N_HEADS, 1), (N_HEADS, LANE))
    dtb_row, alog_row, d_exp = per_head(p["dt_bias"]), per_head(p["A_log"]), per_head(p["D_skip"])
    sinks = p["sinks"].reshape(N_HEADS)

    d_cut, gu_cut = WD_CUT, WGU_CUTS
    n_d, n_gu = w_d_mine.shape[0], w_gu_mine.shape[0]
    (qkv, z, xbc, dt_raw), (g_d_a,) = _in_proj_fwd(x, p["norm1"], scale1, shift1, w_in, tm,
                                                   ([(w_d_mine, 0, d_cut)], "two-level"))
    bias = _attn_bias(buckets, p["rel_bias"])
    (ya,), (g_gu_a,) = _attn_fwd(qkv, bias, sinks, ([(w_gu_mine, 0, gu_cut[0])], "two-level"))
    (ys, prev_states, pre_act), (g_gu_b, g_o) = _ssd_fwd(
        xbc, dt_raw, conv_w, p["conv_b"], dtb_row, alog_row, d_exp,
        ([(w_gu_mine, gu_cut[0], gu_cut[1] - gu_cut[0]), w_o_mine], "two-level"))
    w_o = g_o.reshape(D_MODEL, D_MODEL)
    x1, (g_gu_c, g_d_b) = _out_proj_fwd(
        x, ya, ys, z, p["attn_out_norm"], p["ssm_out_norm"], gate1, w_o, tm,
        ([(w_gu_mine, gu_cut[1], n_gu - gu_cut[1]), (w_d_mine, d_cut, n_d - d_cut)], "two-level"))
    dx1, h2, dgu, act, dmlp, acc2 = _mlp_loss(x1, tgt, p["norm2"], scale2, shift2, gate2, p["final_norm"],
                                              (g_gu_a, g_gu_b, g_gu_c), (g_d_a, g_d_b), tmm)
    g_w_gu = _wgrad(dgu, h2, 2 * D_FF // 4, tw, "wgrad_gate_up")
    g_w_d = _wgrad(act, dmlp, D_FF // 2, tw, "wgrad_down")
    gu_slots = g_w_gu.reshape(N_DEV, 2 * D_FF // N_DEV, D_MODEL)
    (dya, dys, dz, g_w_o, acc1), (r_gu_a,) = _out_proj_bwd(
        dx1, ya, ys, z, p["attn_out_norm"], p["ssm_out_norm"], gate1, w_o, tm, ([gu_slots], ("rows", 0, GGU_CUT)))
    (dq, dkv, dbias, dsk), (r_d, r_o) = _attn_bwd(
        qkv, ya, dya, bias, sinks,
        ([g_w_d.reshape(N_DEV, D_FF // N_DEV, D_MODEL), g_w_o.reshape(N_DEV, D_MODEL // N_DEV, D_MODEL)], True))
    drel, dsink = _attn_finish(dbias, dsk, buckets)
    (dxbc, ddt, dcw, dvec, dd), (r_gu_b, *early) = _ssd_bwd(
        xbc, pre_act, dt_raw, prev_states, dys, conv_w, dtb_row, alog_row, d_exp,
        [([gu_slots], ("rows", GGU_CUT, 2 * D_FF // N_DEV - GGU_CUT)), ([acc1, acc2, dsink, drel, c], False)])
    r_gu = (r_gu_a, r_gu_b)
    gx, h1, acc0, g_in_a = _in_proj_bwd(x, dx1, dq, dkv, dz, dxbc, ddt, p["norm1"], scale1, shift1, w_in, tm)
    half = D_MODEL // 2
    slots = lambda g: g[:IN_W].reshape(N_DEV, IN_W // N_DEV, half)
    g_in_b, (r_in_a,) = _wgrad((dq, dkv, dz, dxbc, ddt), h1, IN_PAD, tw, "wgrad_in_b",
                               ([slots(g_in_a)], "two-level scatter"), g_cols=(half, 1))
    return gx, (r_in_a, slots(g_in_b)), (r_o, r_gu, r_d), early, (acc0, dcw, dvec, dd)


def kernel(x, c, ada_w, ada_b, norm1, w_in, conv_w, conv_b, dt_bias, A_log, D_skip, sinks, attn_out_norm, ssm_out_norm, w_o, norm2, w_gate_up, w_down, rel_bias, final_norm, loss_target, m_ada_w, m_ada_b, m_norm1, m_w_in, m_conv_w, m_conv_b, m_dt_bias, m_A_log, m_D_skip, m_sinks, m_attn_out_norm, m_ssm_out_norm, m_w_o, m_norm2, m_w_gate_up, m_w_down, m_rel_bias, m_final_norm, v_ada_w, v_ada_b, v_norm1, v_w_in, v_conv_w, v_conv_b, v_dt_bias, v_A_log, v_D_skip, v_sinks, v_attn_out_norm, v_ssm_out_norm, v_w_o, v_norm2, v_w_gate_up, v_w_down, v_rel_bias, v_final_norm):
    two_d = lambda a: a if a.ndim == 2 else a.reshape(-1, a.shape[-1])
    small_params = dict(
        ada_b=(ada_b, m_ada_b, v_ada_b), norm1=(norm1, m_norm1, v_norm1), conv_w=(conv_w, m_conv_w, v_conv_w),
        conv_b=(conv_b, m_conv_b, v_conv_b), dt_bias=(dt_bias, m_dt_bias, v_dt_bias), A_log=(A_log, m_A_log, v_A_log),
        D_skip=(D_skip, m_D_skip, v_D_skip), sinks=(sinks, m_sinks, v_sinks),
        attn_out_norm=(attn_out_norm, m_attn_out_norm, v_attn_out_norm),
        ssm_out_norm=(ssm_out_norm, m_ssm_out_norm, v_ssm_out_norm), norm2=(norm2, m_norm2, v_norm2),
        rel_bias=(rel_bias, m_rel_bias, v_rel_bias), final_norm=(final_norm, m_final_norm, v_final_norm))
    small_params = {k: tuple(two_d(a) for a in v) for k, v in small_params.items()}
    S = x.shape[1]
    xs, tgt = x.reshape(S, D_MODEL), loss_target.reshape(S, D_MODEL)
    ada_w2 = ada_w[0]
    chunk = ada_w2.shape[1]
    t_in = [jnp.transpose(a[0]) for a in (w_in, m_w_in, v_w_in)]
    t_gu = [jnp.transpose(a[0]) for a in (w_gate_up, m_w_gate_up, v_w_gate_up)]

    mod, (g_in, g_cw) = _mod_and_gather(c, ada_w2, ada_b.reshape(N_DEV, chunk), [t_in[0].astype(WIRE_DTYPE), conv_w[0]])
    mod = mod.reshape(6, D_MODEL)
    w_in_full = jnp.pad(g_in.reshape(IN_W, D_MODEL), ((0, IN_PAD - IN_W), (0, 0)))
    conv_w_full = jnp.transpose(g_cw, (1, 0, 2)).reshape(4, XBC_W)

    p = {k: v[0] for k, v in small_params.items()}
    gx, (r_in_a, gw_in_b), (r_o, r_gu, r_d), early, late_blocks = _local_step(
        xs, tgt, c, mod, w_in_full, conv_w_full, w_o[0].astype(WIRE_DTYPE), t_gu[0].astype(WIRE_DTYPE),
        w_down[0].astype(WIRE_DTYPE), p)

    _, (r_in_b, *late) = _hosted_call(
        lambda: None, "scatter_in_b", 1, [], [], [], [], (),
        [([gw_in_b], "two-level scatter"), (list(late_blocks), False)], _cparams())
    (u_gu, u_d, u_o), _ = _reduce_adamw_hosting(
        [r_gu, r_d, r_o], [tuple(t_gu), (w_down[0], m_w_down[0], v_w_down[0]), (w_o[0], m_w_o[0], v_w_o[0])],
        "adamw_big", [])
    gathered = (late[0], early[0], early[1], late[1], late[2], late[3], early[2], early[3], early[4])

    small, loss, c_all, dmod_all = _small_update(gathered, small_params)

    big = {
        "ada_w": _ada_w_update(c_all, dmod_all, ada_w2, m_ada_w[0], v_ada_w[0]),
        "w_in": [jnp.transpose(a) for a in _reduce_adamw_halves(r_in_a, r_in_b, *t_in, "adamw_w_in")],
        "w_o": u_o,
        "w_gate_up": [jnp.transpose(a) for a in u_gu],
        "w_down": u_d,
    }
    big.update(small)

    order = ['ada_w', 'ada_b', 'norm1', 'w_in', 'conv_w', 'conv_b', 'dt_bias', 'A_log', 'D_skip', 'sinks',
             'attn_out_norm', 'ssm_out_norm', 'w_o', 'norm2', 'w_gate_up', 'w_down', 'rel_bias', 'final_norm']
    shapes = dict(ada_w=ada_w.shape, ada_b=ada_b.shape, norm1=norm1.shape, w_in=w_in.shape, conv_w=conv_w.shape,
                  conv_b=conv_b.shape, dt_bias=dt_bias.shape, A_log=A_log.shape, D_skip=D_skip.shape,
                  sinks=sinks.shape, attn_out_norm=attn_out_norm.shape, ssm_out_norm=ssm_out_norm.shape,
                  w_o=w_o.shape, norm2=norm2.shape, w_gate_up=w_gate_up.shape, w_down=w_down.shape,
                  rel_bias=rel_bias.shape, final_norm=final_norm.shape)
    outs = [[], [], [], []]
    for name in order:
        for kind in range(4):
            outs[kind].append(big[name][kind].reshape(shapes[name]))
    return (loss.reshape(()), gx.reshape(x.shape), *outs[0], *outs[1], *outs[2], *outs[3])
```

```python
import numpy as np
import jax
import jax.numpy as jnp
from jax import lax
from jax.experimental import pallas as pl
from jax.experimental.pallas import tpu as pltpu

F32 = jnp.float32
MXU_DTYPE = jnp.bfloat16
WIRE_DTYPE = jnp.bfloat16
HI = lax.Precision.HIGHEST
MESH = pl.DeviceIdType.MESH
N_DEV = 8

D_MODEL = 1024
ATTN_W = 512
KV_W = 128
SSM_W = 512
XBC_W = 1024
N_HEADS = 8
D_STATE = 128
D_FF = 2816
IN_W = 2312
IN_PAD = 2432
BLK = 128
N_BUCKETS = 32
EPS = 1e-6
LANE = 128
HALF = 64

ADAM_LR, ADAM_B1, ADAM_B2, ADAM_EPS, ADAM_WD, ADAM_STEP = 0.001, 0.9, 0.999, 1e-08, 0.01, 10

VMEM_BIG = 56 * 1024 * 1024
WD_CUT = 288
WGU_CUTS = (240, 496)
GGU_CUT = 304


def _cparams(vmem=None):
    if vmem is None:
        return pltpu.CompilerParams()
    return pltpu.CompilerParams(vmem_limit_bytes=vmem)


def _mm(a, b):
    return jnp.dot(a.astype(MXU_DTYPE), b.astype(MXU_DTYPE), preferred_element_type=F32)


def _mm_nt(a, b):
    return lax.dot_general(a.astype(MXU_DTYPE), b.astype(MXU_DTYPE), (((1,), (1,)), ((), ())),
                           preferred_element_type=F32)


def _mm_tn(a, b):
    return lax.dot_general(a.astype(MXU_DTYPE), b.astype(MXU_DTYPE), (((0,), (0,)), ((), ())),
                           preferred_element_type=F32)


def _mm_hi(a, b):
    return jnp.dot(a, b, precision=HI, preferred_element_type=F32)


def _silu(x):
    return x * jax.nn.sigmoid(x)


def _softplus(x):
    return jnp.maximum(x, 0.0) + jnp.log1p(jnp.exp(-jnp.abs(x)))


def _rms(x, g, n):
    return x * lax.rsqrt(jnp.sum(x * x, axis=-1, keepdims=True) * (1.0 / n) + EPS) * g


def _modnorm(x, g, scale, shift):
    return _rms(x, g, x.shape[-1]) * (1.0 + scale) + shift


def _modnorm_parts(x):
    r = lax.rsqrt(jnp.sum(x * x, axis=-1, keepdims=True) * (1.0 / x.shape[-1]) + EPS)
    return r, x * r


def _modnorm_bwd(r, xhat, g, scale, dy):
    dyg = dy * (g * (1.0 + scale))
    c = jnp.sum(dyg * xhat, axis=-1, keepdims=True) * (1.0 / xhat.shape[-1])
    dx = r * (dyg - xhat * c)
    ct = jnp.sum(dy * xhat, axis=0, keepdims=True)
    return dx, ct * (1.0 + scale), ct * g, jnp.sum(dy, axis=0, keepdims=True)


def _lane_iota(shape):
    return lax.broadcasted_iota(jnp.int32, shape, len(shape) - 1)


def _split_pair(t):
    lane = _lane_iota(t.shape)
    lo = jnp.where(lane < HALF, t, 0.0)
    hi = pltpu.roll(jnp.where(lane >= HALF, t, 0.0), HALF, 1)
    return lo, hi


def _join_pair(lo, hi):
    lane = _lane_iota(lo.shape)
    return jnp.where(lane < HALF, lo, pltpu.roll(hi, HALF, 1))


def _split_heads(t, n_pairs):
    out = []
    for p in range(n_pairs):
        out.extend(_split_pair(t[:, p * LANE:(p + 1) * LANE]))
    return out


def _join_heads(hs):
    return jnp.concatenate([_join_pair(hs[2 * p], hs[2 * p + 1]) for p in range(len(hs) // 2)], axis=1)


def _t5_bucket_table():
    dist = np.arange(BLK)[:, None] + BLK - np.arange(2 * BLK)[None, :]
    n = np.maximum(dist, 0)
    max_exact = N_BUCKETS // 2
    large = max_exact + (np.log(np.maximum(n, 1) / max_exact) / np.log(128 / max_exact)
                         * (N_BUCKETS - max_exact)).astype(np.int32)
    large = np.minimum(large, N_BUCKETS - 1)
    return np.where(n < max_exact, n, large).astype(np.int32)


def _my_pos():
    return lax.axis_index("x"), lax.axis_index("y"), lax.axis_index("c")


def _peer(k):
    x, y, c = _my_pos()
    return (1 - x if k & 4 else x, 1 - y if k & 2 else y, 1 - c if k & 1 else c)


def _lin(pos):
    return 4 * pos[0] + 2 * pos[1] + pos[2]


def _xchg_copies(ins, outs, sems, scatter):
    local_sem, send_sem, recv_sem = sems
    me = _lin(_my_pos())

    def source(a, slot):
        if not scatter:
            return ins[a]
        if scatter is True:
            return ins[a].at[slot]
        return ins[a].at[slot, pl.ds(scatter[1], scatter[2])]

    local, remote = [], []
    for a in range(len(ins)):
        local.append(pltpu.make_async_copy(source(a, me), outs[a].at[me], local_sem.at[a]))
    for k in range(1, N_DEV):
        peer = _peer(k)
        for a in range(len(ins)):
            remote.append(pltpu.make_async_remote_copy(source(a, _lin(peer)), outs[a].at[me], send_sem.at[a, k - 1],
                                                       recv_sem.at[a, k - 1], device_id=peer, device_id_type=MESH))
    return local, remote


def _xchg_start(ins, outs, sems, scatter):
    local, remote = _xchg_copies(ins, outs, sems, scatter)
    for cp in local + remote:
        cp.start()


def _xchg_wait(ins, outs, sems, scatter):
    local, remote = _xchg_copies(ins, outs, sems, scatter)
    for cp in local:
        cp.wait()
    for cp in remote:
        cp.wait_send()
        cp.wait_recv()


def _xchg_shapes(arrs, scatter):
    n = len(arrs)
    if isinstance(scatter, tuple):
        out_shape = [jax.ShapeDtypeStruct((a.shape[0], scatter[2]) + a.shape[2:], a.dtype) for a in arrs]
    elif scatter:
        out_shape = [jax.ShapeDtypeStruct(a.shape, a.dtype) for a in arrs]
    else:
        out_shape = [jax.ShapeDtypeStruct((N_DEV,) + a.shape, a.dtype) for a in arrs]
    sems = [pltpu.SemaphoreType.DMA((n,)), pltpu.SemaphoreType.DMA((n, N_DEV - 1)),
            pltpu.SemaphoreType.DMA((n, N_DEV - 1))]
    return out_shape, sems


_CHIPS = (2, 4, 6)


def _g2_sems(n):
    dma = pltpu.SemaphoreType.DMA
    return [dma((n,)), dma((n, N_DEV)), dma((n, N_DEV)), dma((n, len(_CHIPS))), dma((n, len(_CHIPS)))]


class _TwoLevelGather:
    def __init__(self, ins, outs, sems, windows=None):
        self.ins, self.outs = ins, outs
        self.local_sem, self.send_sem, self.recv_sem, self.fsend_sem, self.frecv_sem = sems
        self.n = len(ins)
        self.windows = windows or [None] * self.n

    def _mine(self, a):
        w = self.windows[a]
        return self.ins[a] if w is None else self.ins[a].at[pl.ds(w[0], w[1])]

    def _direct(self, a, k):
        return pltpu.make_async_remote_copy(self._mine(a), self.outs[a].at[_lin(_my_pos())], self.send_sem.at[a, k],
                                            self.recv_sem.at[a, k], device_id=_peer(k), device_id_type=MESH)

    def _handed_on(self, a, j, origin):
        slot = self.outs[a].at[origin]
        return pltpu.make_async_remote_copy(slot, slot, self.fsend_sem.at[a, j], self.frecv_sem.at[a, j],
                                            device_id=_peer(1), device_id_type=MESH)

    def _local(self, a):
        return pltpu.make_async_copy(self._mine(a), self.outs[a].at[_lin(_my_pos())], self.local_sem.at[a])

    def start(self):
        for a in range(self.n):
            self._local(a).start()
        for k in (1,) + _CHIPS:
            for a in range(self.n):
                self._direct(a, k).start()

    def forward(self):
        for j, k in enumerate(_CHIPS):
            for a in range(self.n):
                self._direct(a, k).wait_recv()
                self._handed_on(a, j, _lin(_peer(k))).start()

    def finish(self):
        for a in range(self.n):
            self._direct(a, 1).wait_recv()
            for j, k in enumerate(_CHIPS):
                self._handed_on(a, j, _lin(_peer(k ^ 1))).wait_recv()
            self._local(a).wait()
            for k in (1,) + _CHIPS:
                self._direct(a, k).wait_send()
            for j, k in enumerate(_CHIPS):
                self._handed_on(a, j, _lin(_peer(k))).wait_send()


N_SCATTERED = 2 + len(_CHIPS)


class _TwoLevelScatter:
    def __init__(self, src, out, stage, buf_a, buf_b, sems):
        self.src, self.out, self.stage, self.buf_a, self.buf_b = src, out, stage, buf_a, buf_b
        self.local_sem, self.dsend, self.drecv, self.csend, self.crecv = sems

    def _own(self):
        return pltpu.make_async_copy(self.src.at[_lin(_my_pos())], self.out.at[0], self.local_sem.at[0])

    def _to_core(self, j):
        q = 0 if j == 0 else _CHIPS[j - 1]
        dst = self.out.at[1] if j == 0 else self.stage.at[j - 1]
        return pltpu.make_async_remote_copy(self.src.at[_lin(_peer(q ^ 1))], dst, self.dsend.at[j], self.drecv.at[j],
                                            device_id=_peer(1), device_id_type=MESH)

    def _loads(self, j):
        mine = self.src.at[_lin(_peer(_CHIPS[j]))]
        return (pltpu.make_async_copy(self.stage.at[j], self.buf_a.at[j], self.local_sem.at[1 + 2 * j]),
                pltpu.make_async_copy(mine, self.buf_b.at[j], self.local_sem.at[2 + 2 * j]))

    def _to_chip(self, j):
        return pltpu.make_async_remote_copy(self.buf_a.at[j], self.out.at[2 + j], self.csend.at[j], self.crecv.at[j],
                                            device_id=_peer(_CHIPS[j]), device_id_type=MESH)

    def start(self):
        self._own().start()
        for j in range(1 + len(_CHIPS)):
            self._to_core(j).start()

    def forward(self):
        for j in range(len(_CHIPS)):
            self._to_core(j + 1).wait_recv()
            for cp in self._loads(j):
                cp.start()
        for j in range(len(_CHIPS)):
            for cp in self._loads(j):
                cp.wait()
            self.buf_a[j] = (self.buf_a[j].astype(F32) + self.buf_b[j].astype(F32)).astype(self.buf_a.dtype)
            self._to_chip(j).start()

    def finish(self):
        self._own().wait()
        self._to_core(0).wait_recv()
        for j in range(1 + len(_CHIPS)):
            self._to_core(j).wait_send()
        for j in range(len(_CHIPS)):
            self._to_chip(j).wait_send()
            self._to_chip(j).wait_recv()


def _s2_shapes(a):
    dma = pltpu.SemaphoreType.DMA
    n_c = len(_CHIPS)
    piece = a.shape[1:]
    return (jax.ShapeDtypeStruct((N_SCATTERED,) + piece, a.dtype), jax.ShapeDtypeStruct((n_c,) + piece, a.dtype),
            [pltpu.VMEM((n_c,) + piece, a.dtype)] * 2,
            [dma((1 + 2 * n_c,)), dma((1 + n_c,)), dma((1 + n_c,)), dma((n_c,)), dma((n_c,))])


def _mod_and_gather(c, ada_w, ada_b8, arrs):
    n = len(arrs)
    chunk = ada_w.shape[1]
    out_shape = [jax.ShapeDtypeStruct((N_DEV, 1, chunk), F32)]
    out_shape += [jax.ShapeDtypeStruct((N_DEV,) + a.shape, a.dtype) for a in arrs]

    def modulation(c_ref, w_ref, b_ref, out_ref, cbuf, part, s1, r1, s2, r2):
        me = _lin(_my_pos())
        first = []
        for k in range(1, N_DEV):
            cp = pltpu.make_async_remote_copy(c_ref, cbuf.at[me], s1.at[k - 1], r1.at[k - 1],
                                              device_id=_peer(k), device_id_type=MESH)
            cp.start()
            first.append(cp)
        cbuf[me] = c_ref[...]
        for cp in first:
            cp.wait_send()
            cp.wait_recv()
        cond = _silu(jnp.concatenate([cbuf[i] for i in range(N_DEV)], axis=0))
        mod = _mm_hi(cond, w_ref[...]) + b_ref[pl.ds(me, 1), :]
        for j in range(N_DEV):
            part[j] = mod[j:j + 1, :]
        second = []
        for k in range(1, N_DEV):
            peer = _peer(k)
            cp = pltpu.make_async_remote_copy(part.at[_lin(peer)], out_ref.at[me], s2.at[k - 1], r2.at[k - 1],
                                              device_id=peer, device_id_type=MESH)
            cp.start()
            second.append(cp)
        out_ref[me] = part[me]
        for cp in second:
            cp.wait_send()
            cp.wait_recv()

    def body(*refs):
        c_ref, w_ref, b_ref = refs[:3]
        ins = refs[3:3 + n]
        mod_ref = refs[3 + n]
        outs = refs[4 + n:4 + 2 * n]
        cbuf, part, s1, r1, s2, r2 = refs[4 + 2 * n:10 + 2 * n]
        gather = _TwoLevelGather(ins, outs, refs[10 + 2 * n:])
        gather.start()
        modulation(c_ref, w_ref, b_ref, mod_ref, cbuf, part, s1, r1, s2, r2)
        gather.forward()
        gather.finish()

    hbm = pl.BlockSpec(memory_space=pltpu.HBM)
    vm = pl.BlockSpec(memory_space=pltpu.VMEM)
    dma = pltpu.SemaphoreType.DMA
    res = pl.pallas_call(
        body, name="mod_and_gather", out_shape=out_shape, in_specs=[vm, vm, vm] + [hbm] * n,
        out_specs=[vm] + [hbm] * n,
        scratch_shapes=[pltpu.VMEM((N_DEV, 1, D_MODEL), F32), pltpu.VMEM((N_DEV, 1, chunk), F32)]
        + [dma((N_DEV - 1,))] * 4 + _g2_sems(n),
    )(c, ada_w, ada_b8, *arrs)
    return res[0], res[1:]


def _hosted_call(body, name, grid, in_specs, out_specs, out_shape, scratch_shapes, args, xchg, cparams):
    xchgs = [xchg] if isinstance(xchg, tuple) else list(xchg)
    grid = (grid,) if isinstance(grid, int) else tuple(grid)
    n_in, n_out, n_scr = len(in_specs), len(out_specs), len(scratch_shapes)
    windows = [[(a[1], a[2]) if isinstance(a, tuple) else None for a in group] for group, _ in xchgs]
    xchgs = [([a[0] if isinstance(a, tuple) else a for a in group], mode) for group, mode in xchgs]
    arrs = [a for group, _ in xchgs for a in group]
    n = len(arrs)
    x_shape, x_sems, sem_counts, stage_shape, stage_bufs = [], [], [], [], []
    for (group, mode), wins in zip(xchgs, windows):
        if mode == "two-level scatter":
            (a,) = group
            res_shape, stage, bufs, sems = _s2_shapes(a)
            shapes = [res_shape]
            stage_shape.append(stage)
            stage_bufs += bufs
        else:
            shapes, sems = _xchg_shapes(group, False if mode == "two-level" else mode)
        if mode == "two-level":
            sems = _g2_sems(len(group))
            shapes = [s if w is None else jax.ShapeDtypeStruct((N_DEV, w[1]) + a.shape[1:], a.dtype)
                      for s, w, a in zip(shapes, wins, group)]
        x_shape += shapes
        x_sems += sems
        sem_counts.append(len(sems))
    n_stage = len(stage_shape)
    n_steps = int(np.prod(grid))
    staged = ("two-level", "two-level scatter")

    def hosted(*refs):
        ins, refs = refs[:n_in], refs[n_in:]
        x_in, refs = refs[:n], refs[n:]
        outs, refs = refs[:n_out], refs[n_out:]
        x_out, refs = refs[:n], refs[n:]
        stages, refs = refs[:n_stage], refs[n_stage:]
        scr, refs = refs[:n_scr], refs[n_scr:]
        bufs, sems = refs[:2 * n_stage], refs[2 * n_stage:]
        step = pl.program_id(0)
        for d in range(1, len(grid)):
            step = step * grid[d] + pl.program_id(d)
        parts, a0, s0, t0 = [], 0, 0, 0
        for (group, mode), ns, wins in zip(xchgs, sem_counts, windows):
            gi, go, gs = x_in[a0:a0 + len(group)], x_out[a0:a0 + len(group)], sems[s0:s0 + ns]
            if mode == "two-level":
                parts.append((mode, _TwoLevelGather(gi, go, gs, wins)))
            elif mode == "two-level scatter":
                parts.append((mode, _TwoLevelScatter(gi[0], go[0], stages[t0], bufs[2 * t0], bufs[2 * t0 + 1], gs)))
                t0 += 1
            else:
                parts.append((mode, (gi, go, gs, mode)))
            a0, s0 = a0 + len(group), s0 + ns

        @pl.when(step == 0)
        def _():
            for mode, x in parts:
                if mode in staged:
                    x.start()
                else:
                    _xchg_start(*x)

        for kind, at in (("two-level", (2 * n_steps) // 3), ("two-level scatter", n_steps // 4)):
            if any(mode == kind for mode, _ in parts):
                @pl.when(step == at)
                def _():
                    for mode, x in parts:
                        if mode == kind:
                            x.forward()

        body(*ins, *outs, *scr)

        @pl.when(step == n_steps - 1)
        def _():
            for mode, x in parts:
                if mode in staged:
                    x.finish()
                else:
                    _xchg_wait(*x)

    hbm = pl.BlockSpec(memory_space=pltpu.HBM)
    res = pl.pallas_call(
        hosted, name=name, grid=grid, in_specs=list(in_specs) + [hbm] * n,
        out_specs=list(out_specs) + [hbm] * (n + n_stage), out_shape=list(out_shape) + x_shape + stage_shape,
        scratch_shapes=list(scratch_shapes) + stage_bufs + x_sems, compiler_params=cparams,
    )(*args, *arrs)
    return res[:n_out], res[n_out:n_out + n]


def _row(i):
    return (i, 0)


def _fixed(i):
    return (0, 0)


def _in_proj_fwd(x, norm1, scale1, shift1, w_in, tm, xchg):
    S = x.shape[0]

    def body(x_ref, n_ref, sc_ref, sh_ref, w_ref, qkv_ref, z_ref, xbc_ref, dt_ref):
        h = _modnorm(x_ref[...], n_ref[...], sc_ref[...], sh_ref[...])
        p = _mm_nt(h, w_ref[...])
        qkv_ref[...] = p[:, :768].astype(qkv_ref.dtype)
        z_ref[...] = p[:, 768:1280]
        xbc_ref[...] = p[:, 1280:2304]
        dt_ref[...] = p[:, 2304:IN_PAD]

    vec = pl.BlockSpec((1, D_MODEL), _fixed)
    return _hosted_call(
        body, "in_proj_fwd", S // tm,
        in_specs=[pl.BlockSpec((tm, D_MODEL), _row), vec, vec, vec, pl.BlockSpec((IN_PAD, D_MODEL), _fixed)],
        out_specs=[pl.BlockSpec((tm, 768), _row), pl.BlockSpec((tm, SSM_W), _row),
                   pl.BlockSpec((tm, XBC_W), _row), pl.BlockSpec((tm, LANE), _row)],
        out_shape=[jax.ShapeDtypeStruct((S, 768), MXU_DTYPE), jax.ShapeDtypeStruct((S, SSM_W), F32),
                   jax.ShapeDtypeStruct((S, XBC_W), F32), jax.ShapeDtypeStruct((S, LANE), F32)],
        scratch_shapes=[], args=(x, norm1, scale1, shift1, w_in), xchg=xchg, cparams=_cparams(VMEM_BIG),
    )


def _in_proj_bwd(x, dx1, dq, dkv, dz, dxbc, ddt, norm1, scale1, shift1, w_in, tm):
    S = x.shape[0]

    n_steps = S // tm

    def body(x_ref, dx1_ref, dq_ref, dkv_ref, dz_ref, dxbc_ref, ddt_ref, n_ref, sc_ref, sh_ref, w_ref,
             gx_ref, h_ref, acc_ref):
        i = pl.program_id(0)

        @pl.when(i == 0)
        def _():
            acc_ref[...] = jnp.zeros_like(acc_ref)

        halves = [pl.ds(k * (tm // 2), tm // 2) for k in range(2)]
        dp = [jnp.concatenate([r[rows, :] for r in (dq_ref, dkv_ref, dz_ref, dxbc_ref, ddt_ref)], axis=1)
              for rows in halves]
        dh = [_mm(dp[k], w_ref[...]) for k in range(2)]
        parts = [_modnorm_parts(x_ref[rows, :]) for rows in halves]
        hb = [(parts[k][1] * n_ref[...] * (1.0 + sc_ref[...]) + sh_ref[...]).astype(h_ref.dtype) for k in range(2)]
        bwd = [_modnorm_bwd(parts[k][0], parts[k][1], n_ref[...], sc_ref[...], dh[k]) for k in range(2)]
        for k, rows in enumerate(halves):
            gx_ref[rows, :] = dx1_ref[rows, :] + bwd[k][0]
            h_ref[rows, :] = hb[k]
        acc_ref[0:1, :] += bwd[0][1] + bwd[1][1]
        acc_ref[1:2, :] += bwd[0][2] + bwd[1][2]
        acc_ref[2:3, :] += bwd[0][3] + bwd[1][3]

    vec = pl.BlockSpec((1, D_MODEL), _fixed)
    return pl.pallas_call(
        body, name="in_proj_bwd", grid=(n_steps,),
        in_specs=[pl.BlockSpec((tm, D_MODEL), _row), pl.BlockSpec((tm, D_MODEL), _row),
                  pl.BlockSpec((tm, ATTN_W), _row), pl.BlockSpec((tm, 2 * KV_W), _row),
                  pl.BlockSpec((tm, SSM_W), _row), pl.BlockSpec((tm, XBC_W), _row), pl.BlockSpec((tm, LANE), _row),
                  vec, vec, vec, pl.BlockSpec((IN_PAD, D_MODEL), _fixed)],
        out_specs=[pl.BlockSpec((tm, D_MODEL), _row), pl.BlockSpec((tm, D_MODEL), _row),
                   pl.BlockSpec((8, D_MODEL), _fixed)],
        out_shape=[jax.ShapeDtypeStruct((S, D_MODEL), F32), jax.ShapeDtypeStruct((S, D_MODEL), MXU_DTYPE),
                   jax.ShapeDtypeStruct((8, D_MODEL), F32)],
        compiler_params=_cparams(VMEM_BIG),
    )(x, dx1, dq, dkv, dz, dxbc, ddt, norm1, scale1, shift1, w_in)


def _out_stage(ya, ys0, ys1, z0, z1, an, sn0, sn1):
    half = SSM_W // 2
    a = _rms(ya, an, ATTN_W)
    g0 = _rms(ys0 * _silu(z0), sn0, half)
    g1 = _rms(ys1 * _silu(z1), sn1, half)
    return jnp.concatenate([a, g0, g1], axis=1)


def _out_stage_args(ya_ref, ys_ref, z_ref, an_ref, sn_ref):
    half = SSM_W // 2
    return (ya_ref[...], ys_ref[:, :half], ys_ref[:, half:], z_ref[:, :half], z_ref[:, half:],
            an_ref[...], sn_ref[:, :half], sn_ref[:, half:])


def _out_proj_fwd(x, ya, ys, z, an, sn, gate1, w_o, tm, xchg):
    S = x.shape[0]

    def body(x_ref, ya_ref, ys_ref, z_ref, an_ref, sn_ref, g_ref, w_ref, x1_ref):
        u = _out_stage(*_out_stage_args(ya_ref, ys_ref, z_ref, an_ref, sn_ref))
        x1_ref[...] = x_ref[...] + g_ref[...] * _mm(u, w_ref[...])

    half = pl.BlockSpec((tm, ATTN_W), _row)
    hvec = pl.BlockSpec((1, ATTN_W), _fixed)
    (x1,), x_out = _hosted_call(
        body, "out_proj_fwd", S // tm,
        in_specs=[pl.BlockSpec((tm, D_MODEL), _row), half, half, half, hvec, hvec,
                  pl.BlockSpec((1, D_MODEL), _fixed), pl.BlockSpec((D_MODEL, D_MODEL), _fixed)],
        out_specs=[pl.BlockSpec((tm, D_MODEL), _row)],
        out_shape=[jax.ShapeDtypeStruct((S, D_MODEL), F32)],
        scratch_shapes=[], args=(x, ya, ys, z, an, sn, gate1, w_o), xchg=xchg, cparams=_cparams(VMEM_BIG),
    )
    return x1, x_out


def _out_proj_bwd(dx1, ya, ys, z, an, sn, gate1, w_o, tm, xchg):
    S = dx1.shape[0]
    n_steps = S // tm

    def body(dx1_ref, ya_ref, ys_ref, z_ref, an_ref, sn_ref, g_ref, w_ref,
             dya_ref, dys_ref, dz_ref, gw_ref, acc_ref, gw_acc):
        i = pl.program_id(0)

        @pl.when(i == 0)
        def _():
            acc_ref[...] = jnp.zeros_like(acc_ref)
            gw_acc[...] = jnp.zeros_like(gw_acc)

        u, vjp = jax.vjp(_out_stage, *_out_stage_args(ya_ref, ys_ref, z_ref, an_ref, sn_ref))
        dx1 = dx1_ref[...]
        ub = u.astype(MXU_DTYPE)
        mix = _mm(ub, w_ref[...])
        dmix = dx1 * g_ref[...]
        dmixb = dmix.astype(MXU_DTYPE)
        du = _mm_nt(dmixb, w_ref[...])
        gw_acc[...] += _mm_tn(ub, dmixb)
        dya, dys0, dys1, dz0, dz1, dan, dsn0, dsn1 = vjp(du)
        dya_ref[...] = dya
        dys_ref[...] = jnp.concatenate([dys0, dys1], axis=1)
        dz_ref[...] = jnp.concatenate([dz0, dz1], axis=1).astype(dz_ref.dtype)
        acc_ref[0:1, :] += jnp.sum(dx1 * mix, axis=0, keepdims=True)
        acc_ref[1:2, :] += jnp.concatenate([dan, dsn0, dsn1], axis=1)

        @pl.when(i == n_steps - 1)
        def _():
            gw_ref[...] = gw_acc[...].astype(gw_ref.dtype)

    half = pl.BlockSpec((tm, ATTN_W), _row)
    hvec = pl.BlockSpec((1, ATTN_W), _fixed)
    full = pl.BlockSpec((tm, D_MODEL), _row)
    return _hosted_call(
        body, "out_proj_bwd", n_steps,
        in_specs=[full, half, half, half, hvec, hvec,
                  pl.BlockSpec((1, D_MODEL), _fixed), pl.BlockSpec((D_MODEL, D_MODEL), _fixed)],
        out_specs=[half, half, half, pl.BlockSpec((D_MODEL, D_MODEL), _fixed), pl.BlockSpec((8, D_MODEL), _fixed)],
        out_shape=[jax.ShapeDtypeStruct((S, ATTN_W), F32)] * 2 + [jax.ShapeDtypeStruct((S, ATTN_W), MXU_DTYPE),
                   jax.ShapeDtypeStruct((D_MODEL, D_MODEL), WIRE_DTYPE), jax.ShapeDtypeStruct((8, D_MODEL), F32)],
        scratch_shapes=[pltpu.VMEM((D_MODEL, D_MODEL), F32)],
        args=(dx1, ya, ys, z, an, sn, gate1, w_o), xchg=xchg, cparams=_cparams(VMEM_BIG),
    )


def _loss_rows(x2, fn, tgt):
    y = _rms(x2, fn, D_MODEL)
    per_row = jnp.sum(jnp.square(y - tgt), axis=1, keepdims=True)
    return jnp.sum(per_row, axis=0, keepdims=True) * (0.5 / D_MODEL)


def _mlp_loss(x1, tgt, norm2, scale2, shift2, gate2, fnorm, w_gu, w_d, tm):
    S = x1.shape[0]
    n_pieces = len(w_gu) + len(w_d)

    def body(*refs):
        x1_ref, t_ref, n_ref, sc_ref, sh_ref, g_ref, fn_ref = refs[:7]
        piece_refs = refs[7:7 + n_pieces]
        dx1_ref, h_ref, dgu_ref, act_ref, dmlp_ref, acc_ref, wgu, wd, wsem = refs[7 + n_pieces:]

        @pl.when(pl.program_id(0) == 0)
        def _():
            acc_ref[...] = jnp.zeros_like(acc_ref)
            copies = []
            for dst, pieces in ((wgu, piece_refs[:len(w_gu)]), (wd, piece_refs[len(w_gu):])):
                shard = sum(p.shape[1] for p in pieces)
                off = 0
                for p in pieces:
                    for j in range(N_DEV):
                        copies.append(pltpu.make_async_copy(p.at[j], dst.at[pl.ds(j * shard + off, p.shape[1])],
                                                            wsem.at[len(copies)]))
                    off += p.shape[1]
            for cp in copies:
                cp.start()
            for cp in copies:
                cp.wait()

        x1 = x1_ref[...]
        gate2 = g_ref[...]
        h, vjp_h = jax.vjp(_modnorm, x1, n_ref[...], sc_ref[...], sh_ref[...])
        hb = h.astype(MXU_DTYPE)
        gu = _mm_nt(hb, wgu[...])
        g, u = gu[:, :D_FF], gu[:, D_FF:]
        sg = jax.nn.sigmoid(g)
        silu_g = g * sg
        act = (silu_g * u).astype(MXU_DTYPE)
        mlp = _mm(act, wd[...])
        x2 = x1 + gate2 * mlp
        loss, vjp_loss = jax.vjp(_loss_rows, x2, fn_ref[...], t_ref[...])
        dx2, dfn, _ = vjp_loss(jnp.ones((1, 1), F32))
        dmlp = (dx2 * gate2).astype(MXU_DTYPE)
        dact = _mm_nt(dmlp, wd[...])
        dg = dact * u * (sg * (1.0 + g * (1.0 - sg)))
        du = dact * silu_g
        dgu = jnp.concatenate([dg, du], axis=1).astype(MXU_DTYPE)
        dh = _mm(dgu, wgu[...])
        dx, dn, dsc, dsh = vjp_h(dh)
        dx1_ref[...] = dx2 + dx
        h_ref[...] = hb
        dgu_ref[...] = dgu
        act_ref[...] = act
        dmlp_ref[...] = dmlp
        acc_ref[0:1, :] += dn
        acc_ref[1:2, :] += dsc
        acc_ref[2:3, :] += dsh
        acc_ref[3:4, :] += jnp.sum(dx2 * mlp, axis=0, keepdims=True)
        acc_ref[4:5, :] += dfn
        acc_ref[5:6, :] += jnp.broadcast_to(loss, (1, D_MODEL))

    full = pl.BlockSpec((tm, D_MODEL), _row)
    vec = pl.BlockSpec((1, D_MODEL), _fixed)
    anyspec = pl.BlockSpec(memory_space=pl.ANY)
    return pl.pallas_call(
        body, name="mlp_loss", grid=(S // tm,),
        in_specs=[full, full, vec, vec, vec, vec, vec] + [anyspec] * n_pieces,
        out_specs=[full, full, pl.BlockSpec((tm, 2 * D_FF), _row), pl.BlockSpec((tm, D_FF), _row), full,
                   pl.BlockSpec((8, D_MODEL), _fixed)],
        out_shape=[jax.ShapeDtypeStruct((S, D_MODEL), F32), jax.ShapeDtypeStruct((S, D_MODEL), MXU_DTYPE),
                   jax.ShapeDtypeStruct((S, 2 * D_FF), MXU_DTYPE), jax.ShapeDtypeStruct((S, D_FF), MXU_DTYPE),
                   jax.ShapeDtypeStruct((S, D_MODEL), MXU_DTYPE), jax.ShapeDtypeStruct((8, D_MODEL), F32)],
        scratch_shapes=[pltpu.VMEM((2 * D_FF, D_MODEL), MXU_DTYPE), pltpu.VMEM((D_FF, D_MODEL), MXU_DTYPE),
                        pltpu.SemaphoreType.DMA((N_DEV * n_pieces,))],
        compiler_params=_cparams(VMEM_BIG),
    )(x1, tgt, norm2, scale2, shift2, gate2, fnorm, *w_gu, *w_d)


def _wgrad(a, g, tk, ts, name, xchg=None, g_cols=None):
    pieces = list(a) if isinstance(a, (list, tuple)) else [a]
    S = pieces[0].shape[0]
    K = sum(p.shape[1] for p in pieces)
    assert len(pieces) == 1 or tk == K
    N, col = (g.shape[1], 0) if g_cols is None else g_cols
    ns = S // ts
    n_a = len(pieces)

    def body(*refs):
        a_refs, (g_ref, o_ref, acc_ref) = refs[:n_a], refs[n_a:]
        s = pl.program_id(1)

        @pl.when(s == 0)
        def _():
            acc_ref[...] = jnp.zeros_like(acc_ref)

        a_blk = a_refs[0][...] if n_a == 1 else jnp.concatenate([r[...] for r in a_refs], axis=1)
        acc_ref[...] += _mm_tn(a_blk, g_ref[...])

        @pl.when(s == ns - 1)
        def _():
            o_ref[...] = acc_ref[...].astype(o_ref.dtype)

    if n_a == 1:
        in_specs = [pl.BlockSpec((ts, tk), lambda j, s: (s, j))]
    else:
        in_specs = [pl.BlockSpec((ts, p.shape[1]), lambda j, s: (s, 0)) for p in pieces]
    in_specs.append(pl.BlockSpec((ts, N), lambda j, s: (s, col)))
    out_spec = pl.BlockSpec((tk, N), lambda j, s: (j, 0))
    out_shape = jax.ShapeDtypeStruct((K, N), WIRE_DTYPE)
    scratch = [pltpu.VMEM((tk, N), F32)]
    args = (*pieces, g)
    if xchg is None:
        return pl.pallas_call(body, name=name, grid=(K // tk, ns), in_specs=in_specs, out_specs=out_spec,
                              out_shape=out_shape, scratch_shapes=scratch, compiler_params=_cparams(VMEM_BIG))(*args)
    (out,), x_out = _hosted_call(body, name, (K // tk, ns), in_specs, [out_spec], [out_shape], scratch, args, xchg,
                                 _cparams(VMEM_BIG))
    return out, x_out


SSD_CHUNKS_PER_STEP = 4
SSD_BWD_CHUNKS_PER_STEP = 4
ATTN_BLOCKS_PER_STEP = 4
MASKED = -1e30
QK_SCALE = HALF ** -0.5


def _attn_bias(buckets, rel_bias):
    def body(bk_ref, relb_ref, out_ref):
        bk = bk_ref[...]
        i = lax.broadcasted_iota(jnp.int32, (BLK, 2 * BLK), 0)
        j = lax.broadcasted_iota(jnp.int32, (BLK, 2 * BLK), 1)
        window = (j > i) & (j <= i + BLK)
        for h in range(N_HEADS):
            acc = jnp.zeros((BLK, 2 * BLK), F32)
            for b in range(N_BUCKETS):
                acc = jnp.where(bk == b, relb_ref[b, h], acc)
            out_ref[0, h] = jnp.where(window, acc, MASKED)
            out_ref[1, h] = jnp.where(window & (j >= BLK), acc, MASKED)

    return pl.pallas_call(
        body, name="attn_bias", out_shape=jax.ShapeDtypeStruct((2, N_HEADS, BLK, 2 * BLK), F32),
        in_specs=[pl.BlockSpec(memory_space=pltpu.VMEM), pl.BlockSpec(memory_space=pltpu.SMEM)],
    )(buckets, rel_bias)


def _attn_fwd(qkv, bias, sinks, xchg):
    S = qkv.shape[0]
    nb = S // BLK

    nq = ATTN_BLOCKS_PER_STEP if nb % ATTN_BLOCKS_PER_STEP == 0 else 1
    rows = nq * BLK

    def body(q_ref, kvp_ref, kvc_ref, bias_ref, sinks_ref, y_ref):
        i = pl.program_id(0)
        q = q_ref[...].astype(F32) * QK_SCALE
        kv = jnp.concatenate([kvp_ref[...], kvc_ref[...]], axis=0).astype(F32)
        k_lo, k_hi = _split_pair(kv[:, :LANE])
        v_lo, v_hi = _split_pair(kv[:, LANE:])
        bands = [[t[b * BLK:(b + 2) * BLK].astype(MXU_DTYPE) for t in (k_lo, k_hi, v_lo, v_hi)] for b in range(nq)]
        q_heads = [_split_heads(q[b * BLK:(b + 1) * BLK], 4) for b in range(nq)]
        first = [jnp.where(i == 0, 1, 0) if b == 0 else 0 for b in range(nq)]
        items = [(b, h) for b in range(nq) for h in range(N_HEADS)]
        s = [_mm_nt(q_heads[b][h].astype(MXU_DTYPE), bands[b][h // 4]) + bias_ref[first[b], h] for b, h in items]
        m = [jnp.maximum(jnp.max(s[n], axis=-1, keepdims=True), sinks_ref[h]) for n, (b, h) in enumerate(items)]
        p = [jnp.exp(s[n] - m[n]) for n in range(len(items))]
        rinv = [1.0 / (jnp.sum(p[n], axis=-1, keepdims=True) + jnp.exp(sinks_ref[h] - m[n]))
                for n, (b, h) in enumerate(items)]
        out = [_mm(p[n], bands[b][2 + h // 4]) * rinv[n] for n, (b, h) in enumerate(items)]
        y_ref[...] = jnp.concatenate([_join_heads(out[b * N_HEADS:(b + 1) * N_HEADS]) for b in range(nq)], axis=0)

    smem = pl.BlockSpec(memory_space=pltpu.SMEM)
    return _hosted_call(
        body, "attn_fwd", nb // nq,
        in_specs=[pl.BlockSpec((rows, ATTN_W), _row),
                  pl.BlockSpec((BLK, 2 * KV_W), lambda i: (jnp.maximum(i * nq - 1, 0), 2)),
                  pl.BlockSpec((rows, 2 * KV_W), lambda i: (i, 2)),
                  pl.BlockSpec((2, N_HEADS, BLK, 2 * BLK), lambda i: (0, 0, 0, 0)), smem],
        out_specs=[pl.BlockSpec((rows, ATTN_W), _row)],
        out_shape=[jax.ShapeDtypeStruct((S, ATTN_W), F32)],
        scratch_shapes=[],
        args=(qkv, qkv, qkv, bias, sinks), xchg=xchg, cparams=_cparams(),
    )


def _attn_bwd(qkv, y, dy, bias, sinks, xchg):
    S = qkv.shape[0]
    nb = S // BLK
    nq = ATTN_BLOCKS_PER_STEP if nb % ATTN_BLOCKS_PER_STEP == 0 else 1
    rows, n_steps = nq * BLK, nb // nq

    def body(q_ref, kvp_ref, kvc_ref, y_ref, dy_ref, bias_ref, sinks_ref, dq_ref, dkv_ref, dbias_ref, dsk_ref, carry_ref):
        i = pl.program_id(0)

        @pl.when(i == 0)
        def _():
            dbias_ref[...] = jnp.zeros_like(dbias_ref)
            dsk_ref[...] = jnp.zeros_like(dsk_ref)
            carry_ref[...] = jnp.zeros_like(carry_ref)

        q = q_ref[...].astype(F32) * QK_SCALE
        kv = jnp.concatenate([kvp_ref[...], kvc_ref[...]], axis=0).astype(F32)
        k_lo, k_hi = _split_pair(kv[:, :LANE])
        v_lo, v_hi = _split_pair(kv[:, LANE:])
        bands = [[t[b * BLK:(b + 2) * BLK].astype(MXU_DTYPE) for t in (k_lo, k_hi, v_lo, v_hi)] for b in range(nq)]
        rows_of = lambda ref, b: ref[b * BLK:(b + 1) * BLK, :]
        first = [jnp.where(i == n_steps - 1, 1, 0) if b == 0 else 0 for b in range(nq)]
        items = [(b, h) for b in range(nq) for h in range(N_HEADS)]
        at = lambda b, h: b * N_HEADS + h
        q_heads = [hd for b in range(nq) for hd in _split_heads(q[b * BLK:(b + 1) * BLK], 4)]
        y_heads = [hd for b in range(nq) for hd in _split_heads(rows_of(y_ref, b), 4)]
        dy_heads = [hd for b in range(nq) for hd in _split_heads(rows_of(dy_ref, b), 4)]
        qs = [q_heads[n].astype(MXU_DTYPE) for n in range(len(items))]
        s = [_mm_nt(qs[at(b, h)], bands[b][h // 4]) + bias_ref[first[b], h] for b, h in items]
        m = [jnp.maximum(jnp.max(s[at(b, h)], axis=-1, keepdims=True), sinks_ref[h]) for b, h in items]
        p = [jnp.exp(s[n] - m[n]) for n in range(len(items))]
        esink = [jnp.exp(sinks_ref[h] - m[at(b, h)]) for b, h in items]
        rinv = [1.0 / (jnp.sum(p[n], axis=-1, keepdims=True) + esink[n]) for n in range(len(items))]
        t = [dy_heads[n] * rinv[n] for n in range(len(items))]
        delta = [jnp.sum(t[n] * y_heads[n], axis=-1, keepdims=True) for n in range(len(items))]
        tb = [t[n].astype(MXU_DTYPE) for n in range(len(items))]
        dp = [_mm_nt(tb[at(b, h)], bands[b][2 + h // 4]) for b, h in items]
        ds = [p[n] * (dp[n] - delta[n]) for n in range(len(items))]
        for h in range(N_HEADS):
            ds_h, dsk_h = ds[at(0, h)], esink[at(0, h)] * delta[at(0, h)]
            for b in range(1, nq):
                ds_h = ds_h + ds[at(b, h)]
                dsk_h = dsk_h + esink[at(b, h)] * delta[at(b, h)]
            dbias_ref[h] += ds_h
            dsk_ref[h] -= dsk_h
        dsb = [ds[n].astype(MXU_DTYPE) for n in range(len(items))]
        pb = [p[n].astype(MXU_DTYPE) for n in range(len(items))]
        dq_heads = [_mm(dsb[at(b, h)], bands[b][h // 4]) * QK_SCALE for b, h in items]
        grp = lambda lst, b, g: jnp.concatenate(lst[at(b, 4 * g):at(b, 4 * g) + 4], axis=0)
        dk_pads = [[_mm_tn(grp(dsb, b, g), grp(qs, b, g)) for g in range(2)] for b in range(nq)]
        dv_pads = [[_mm_tn(grp(pb, b, g), grp(tb, b, g)) for g in range(2)] for b in range(nq)]
        dq_ref[...] = jnp.concatenate([_join_heads(dq_heads[b * N_HEADS:(b + 1) * N_HEADS]) for b in range(nq)],
                                      axis=0).astype(dq_ref.dtype)
        part = lambda b, lo: jnp.concatenate(
            [_join_pair(d[b][0][lo:lo + BLK], d[b][1][lo:lo + BLK]) for d in (dk_pads, dv_pads)], axis=1)
        dkv = [part(b, BLK) + (part(b + 1, 0) if b + 1 < nq else carry_ref[...]) for b in range(nq)]
        dkv_ref[...] = jnp.concatenate(dkv, axis=0).astype(dkv_ref.dtype)
        carry_ref[...] = part(0, 0)

    smem = pl.BlockSpec(memory_space=pltpu.SMEM)
    rev = lambda i: (n_steps - 1 - i, 0)
    return _hosted_call(
        body, "attn_bwd", n_steps,
        in_specs=[pl.BlockSpec((rows, ATTN_W), rev),
                  pl.BlockSpec((BLK, 2 * KV_W), lambda i: (jnp.maximum((n_steps - 1 - i) * nq - 1, 0), 2)),
                  pl.BlockSpec((rows, 2 * KV_W), lambda i: (n_steps - 1 - i, 2)),
                  pl.BlockSpec((rows, ATTN_W), rev), pl.BlockSpec((rows, ATTN_W), rev),
                  pl.BlockSpec((2, N_HEADS, BLK, 2 * BLK), lambda i: (0, 0, 0, 0)), smem],
        out_specs=[pl.BlockSpec((rows, ATTN_W), rev), pl.BlockSpec((rows, 2 * KV_W), rev),
                   pl.BlockSpec((N_HEADS, BLK, 2 * BLK), lambda i: (0, 0, 0)),
                   pl.BlockSpec((N_HEADS, BLK, 1), lambda i: (0, 0, 0))],
        out_shape=[jax.ShapeDtypeStruct((S, ATTN_W), MXU_DTYPE), jax.ShapeDtypeStruct((S, 2 * KV_W), MXU_DTYPE),
                   jax.ShapeDtypeStruct((N_HEADS, BLK, 2 * BLK), F32), jax.ShapeDtypeStruct((N_HEADS, BLK, 1), F32)],
        scratch_shapes=[pltpu.VMEM((BLK, 2 * KV_W), F32)],
        args=(qkv, qkv, qkv, y, dy, bias, sinks), xchg=xchg, cparams=_cparams(),
    )


def _attn_finish(dbias, dsk, buckets):
    def body(db_ref, dsk_ref, bk_ref, drel_ref, dsink_ref):
        bk = bk_ref[...]
        r = lax.broadcasted_iota(jnp.int32, (N_BUCKETS, LANE), 0)
        l = lax.broadcasted_iota(jnp.int32, (N_BUCKETS, LANE), 1)
        row = lax.broadcasted_iota(jnp.int32, (N_HEADS, LANE), 0)
        res = jnp.zeros((N_BUCKETS, LANE), F32)
        dsink = jnp.zeros((N_HEADS, LANE), F32)
        for h in range(N_HEADS):
            db = db_ref[h]
            for b in range(N_BUCKETS):
                v = jnp.sum(jnp.sum(jnp.where(bk == b, db, 0.0), axis=1, keepdims=True), axis=0, keepdims=True)
                res = res + jnp.where((r == b) & (l == h), v, 0.0)
            dsink = dsink + jnp.where(row == h, jnp.sum(dsk_ref[h], axis=0, keepdims=True), 0.0)
        drel_ref[...] = res
        dsink_ref[...] = dsink

    return pl.pallas_call(body, name="attn_finish",
                          out_shape=[jax.ShapeDtypeStruct((N_BUCKETS, LANE), F32),
                                     jax.ShapeDtypeStruct((N_HEADS, LANE), F32)])(dbias, dsk, buckets)


def _ssd_consts():
    r = lax.broadcasted_iota(jnp.int32, (BLK, BLK), 0)
    c = lax.broadcasted_iota(jnp.int32, (BLK, BLK), 1)
    causal = c <= r
    upper = (r <= c).astype(F32)
    last = r == BLK - 1
    head = lax.broadcasted_iota(jnp.int32, (N_HEADS, BLK), 0)
    return causal, upper, last, head


def _ssd_chunks(xs, bg, cg, dt_raw_t, prev0, dtb, alog, d_rows, consts):
    causal, upper, last, head = consts
    nq = len(xs)
    items = [(c, h) for c in range(nq) for h in range(N_HEADS)]
    at = lambda c, h: c * N_HEADS + h
    a_neg = -jnp.exp(alog)
    dt_t = [_softplus(dt_raw_t[c] + dtb) for c in range(nq)]
    acs_t = [_mm_hi(dt_t[c] * a_neg, upper) for c in range(nq)]
    cb = [[_mm_nt(cg[c][g], bg[c][g]) for g in range(2)] for c in range(nq)]
    pick = lambda t, h: jnp.sum(jnp.where(head == h, t, 0.0), axis=0, keepdims=True)
    dt_row = [pick(dt_t[c], h) for c, h in items]
    a_row = [pick(acs_t[c], h) for c, h in items]
    a_rb = [jnp.broadcast_to(a_row[n], (BLK, BLK)) for n in range(len(items))]
    a_b = [a_rb[n].T for n in range(len(items))]
    a_last = [jnp.sum(jnp.where(last, a_b[n], 0.0), axis=0, keepdims=True) for n in range(len(items))]
    w = [cb[c][h // 4] * jnp.exp(jnp.where(causal, a_b[at(c, h)] - a_rb[at(c, h)], -1e30)) * dt_row[at(c, h)]
         for c, h in items]
    f_b = [jnp.broadcast_to(dt_row[n] * jnp.exp(a_last[n] - a_row[n]), (BLK, BLK)).T for n in range(len(items))]
    y_in = [_mm(w[at(c, h)], xs[c][h]) for c, h in items]
    st = [_mm_tn(bg[c][h // 4], xs[c][h] * f_b[at(c, h)]) for c, h in items]
    e_b = [jnp.exp(a_b[n]) for n in range(len(items))]
    states = [list(prev0)]
    for c in range(nq):
        states.append([states[c][h] * jnp.exp(a_last[at(c, h)]) + st[at(c, h)] for h in range(N_HEADS)])
    y_off = [_mm(cg[c][h // 4], states[c][h]) * e_b[at(c, h)] for c, h in items]
    ys = [[y_in[at(c, h)] + y_off[at(c, h)] + d_rows[h] * xs[c][h] for h in range(N_HEADS)] for c in range(nq)]
    return ys, states


def _ssd_chunks_bwd(xs, bg, cg, dt_raw_t, prev, dtb, alog, d_rows, dys, dh_last, consts):
    causal, upper, last, head = consts
    nq = len(xs)
    items = [(c, h) for c in range(nq) for h in range(N_HEADS)]
    ni = len(items)
    at = lambda c, h: c * N_HEADS + h
    groups = [(c, g) for c in range(nq) for g in range(2)]
    lane = _lane_iota((BLK, BLK))
    lane_row = _lane_iota((1, BLK))
    a_neg = -jnp.exp(alog)
    pre_dt = [dt_raw_t[c] + dtb for c in range(nq)]
    dt_t = [_softplus(pre_dt[c]) for c in range(nq)]
    acs_t = [_mm_hi(dt_t[c] * a_neg, upper) for c in range(nq)]
    pick = lambda t, h: jnp.sum(jnp.where(head == h, t, 0.0), axis=0, keepdims=True)
    full_sum = lambda t: jnp.sum(jnp.sum(t, axis=1, keepdims=True), axis=0, keepdims=True)
    dt_row = [pick(dt_t[c], h) for c, h in items]
    a_row = [pick(acs_t[c], h) for c, h in items]
    a_rb = [jnp.broadcast_to(a_row[n], (BLK, BLK)) for n in range(ni)]
    a_b = [a_rb[n].T for n in range(ni)]
    a_last = [jnp.sum(jnp.where(last, a_b[n], 0.0), axis=0, keepdims=True) for n in range(ni)]
    lm = [jnp.exp(jnp.where(causal, a_b[n] - a_rb[n], -1e30)) for n in range(ni)]
    cgb = [[cg[c][g].astype(MXU_DTYPE) for g in range(2)] for c in range(nq)]
    bgb = [[bg[c][g].astype(MXU_DTYPE) for g in range(2)] for c in range(nq)]
    cb = [[_mm_nt(cgb[c][g], bgb[c][g]) for g in range(2)] for c in range(nq)]
    u = [cb[c][h // 4] * lm[at(c, h)] for c, h in items]
    w = [(u[n] * dt_row[n]).astype(MXU_DTYPE) for n in range(ni)]
    e_row = [jnp.exp(a_last[n] - a_row[n]) for n in range(ni)]
    f_row = [dt_row[n] * e_row[n] for n in range(ni)]
    f_b = [jnp.broadcast_to(f_row[n], (BLK, BLK)).T for n in range(ni)]
    e_b = [jnp.exp(a_b[n]) for n in range(ni)]
    el = [jnp.exp(a_last[n]) for n in range(ni)]
    xb = [xs[c][h].astype(MXU_DTYPE) for c, h in items]
    dyb = [dys[c][h].astype(MXU_DTYPE) for c, h in items]
    prevb = [prev[c][h].astype(MXU_DTYPE) for c, h in items]
    gmat = [_mm(cgb[c][h // 4], prevb[at(c, h)]) for c, h in items]
    dw = [_mm_nt(dyb[n], xb[n]) for n in range(ni)]
    dg = [dys[c][h] * e_b[at(c, h)] for c, h in items]
    dgb = [dg[n].astype(MXU_DTYPE) for n in range(ni)]
    from_y = [_mm_tn(cgb[c][h // 4], dgb[at(c, h)]) for c, h in items]
    dhs = [None] * ni
    dprev = [None] * ni
    for c in reversed(range(nq)):
        for h in range(N_HEADS):
            dhs[at(c, h)] = dh_last[h] if c == nq - 1 else dprev[at(c + 1, h)]
            dprev[at(c, h)] = from_y[at(c, h)] + dhs[at(c, h)] * el[at(c, h)]
    dstb = [dhs[n].astype(MXU_DTYPE) for n in range(ni)]
    dxf = [_mm(bgb[c][h // 4], dstb[at(c, h)]) for c, h in items]
    xfb = [(xs[c][h] * f_b[at(c, h)]).astype(MXU_DTYPE) for c, h in items]
    dxs = [_mm_tn(w[at(c, h)], dyb[at(c, h)]) + d_rows[h] * dys[c][h] + f_b[at(c, h)] * dxf[at(c, h)]
           for c, h in items]
    dd_item = [jnp.sum(dys[c][h] * xs[c][h], axis=0, keepdims=True) for c, h in items]
    dcg_h = [_mm_nt(dgb[n], prevb[n]) for n in range(ni)]
    dbg_h = [_mm_nt(xfb[n], dstb[n]) for n in range(ni)]
    zt = [dw[n] * u[n] for n in range(ni)]
    dseg = [zt[n] * dt_row[n] for n in range(ni)]
    dcb_h = [dw[n] * lm[n] * dt_row[n] for n in range(ni)]
    four = lambda lst, c, g: lst[at(c, 4 * g)] + lst[at(c, 4 * g + 1)] + lst[at(c, 4 * g + 2)] + lst[at(c, 4 * g + 3)]
    dcb = {(c, g): four(dcb_h, c, g).astype(MXU_DTYPE) for c, g in groups}
    dcg = [[four(dcg_h, c, g) + _mm(dcb[c, g], bgb[c][g]) for g in range(2)] for c in range(nq)]
    dbg = [[four(dbg_h, c, g) + _mm_tn(dcb[c, g], cgb[c][g]) for g in range(2)] for c in range(nq)]
    r1 = [jnp.sum(dg[n] * gmat[n] + dseg[n], axis=1, keepdims=True) for n in range(ni)]
    r2 = [jnp.sum(dxf[at(c, h)] * xs[c][h], axis=1, keepdims=True) for c, h in items]
    tt = [jnp.where(lane < HALF, jnp.broadcast_to(r1[n], (BLK, BLK)), jnp.broadcast_to(r2[n], (BLK, BLK))).T
          for n in range(ni)]
    r1_row = [tt[n][0:1, :] for n in range(ni)]
    r2_row = [tt[n][HALF:HALF + 1, :] for n in range(ni)]
    d_el = [full_sum(dhs[at(c, h)] * prev[c][h]) for c, h in items]
    da_last = [jnp.sum(r2_row[n] * f_row[n], axis=1, keepdims=True) + el[n] * d_el[n] for n in range(ni)]
    da_row = [r1_row[n] - jnp.sum(dseg[n], axis=0, keepdims=True) - r2_row[n] * f_row[n]
              + jnp.where(lane_row == BLK - 1, da_last[n], 0.0) for n in range(ni)]
    ddt_row = [jnp.sum(zt[n], axis=0, keepdims=True) + r2_row[n] * e_row[n] for n in range(ni)]
    draw, dalog = [], jnp.zeros((N_HEADS, BLK), F32)
    for c in range(nq):
        da_t = jnp.zeros((N_HEADS, BLK), F32)
        ddt_t = jnp.zeros((N_HEADS, BLK), F32)
        for h in range(N_HEADS):
            da_t = jnp.where(head == h, da_row[at(c, h)], da_t)
            ddt_t = jnp.where(head == h, ddt_row[at(c, h)], ddt_t)
        d_dta = _mm_hi(da_t, causal.astype(F32))
        dalog = dalog + d_dta * dt_t[c] * a_neg
        draw.append((ddt_t + d_dta * a_neg) * jax.nn.sigmoid(pre_dt[c]))
    ddtb = draw[0]
    for c in range(1, nq):
        ddtb = ddtb + draw[c]
    dd_rows = []
    for h in range(N_HEADS):
        t = dd_item[at(0, h)]
        for c in range(1, nq):
            t = t + dd_item[at(c, h)]
        dd_rows.append(t)
    return ([dxs[c * N_HEADS:(c + 1) * N_HEADS] for c in range(nq)], dbg, dcg, draw,
            [dprev[at(0, h)] for h in range(N_HEADS)], ddtb, dalog, dd_rows)


def _dt_rows(dt_blk):
    return dt_blk.T[:N_HEADS]


def _conv_pre(halo, blk, cw_ref, cb_ref):
    ext = jnp.concatenate([halo, blk], axis=0)
    taps = [pltpu.roll(ext, 3 - k, 0)[8:] for k in range(3)] + [blk]
    pre = cb_ref[...] + cw_ref[0:1, :] * taps[0]
    for k in range(1, 4):
        pre = pre + cw_ref[k:k + 1, :] * taps[k]
    return pre


def _ssd_split(pre):
    heads = _split_heads(pre[:, :SSM_W], 4)
    pb = [pre[:, SSM_W + g * D_STATE:SSM_W + (g + 1) * D_STATE] for g in range(2)]
    pc = [pre[:, SSM_W + 2 * D_STATE + g * D_STATE:SSM_W + 2 * D_STATE + (g + 1) * D_STATE] for g in range(2)]
    return heads, pb, pc


def _ssd_fwd(xbc, dt_raw, conv_w, conv_b, dtb_row, alog_row, d_exp, xchg):
    S = xbc.shape[0]
    nc = S // BLK
    nq = SSD_CHUNKS_PER_STEP if nc % SSD_CHUNKS_PER_STEP == 0 else 1
    rows = nq * BLK

    def body(xbc_ref, halo_ref, dt_ref, cw_ref, cb_ref, dtb_ref, alog_ref, d_ref, y_ref, prev_ref, pre_ref, state_ref):
        i = pl.program_id(0)

        @pl.when(i == 0)
        def _():
            state_ref[...] = jnp.zeros_like(state_ref)

        halo = halo_ref[...] * jnp.where(i > 0, 1.0, 0.0)
        pre = _conv_pre(halo, xbc_ref[...], cw_ref, cb_ref)
        pre_ref[...] = pre
        xc = _silu(pre)
        split = [_ssd_split(xc[c * BLK:(c + 1) * BLK]) for c in range(nq)]
        dt_t = [_dt_rows(dt_ref[c * BLK:(c + 1) * BLK, :]) for c in range(nq)]
        prev0 = [state_ref[h] for h in range(N_HEADS)]
        d_rows = [d_ref[h:h + 1, :] for h in range(N_HEADS)]
        ys, states = _ssd_chunks([s[0] for s in split], [s[1] for s in split], [s[2] for s in split], dt_t, prev0,
                                 dtb_ref[...], alog_ref[...], d_rows, _ssd_consts())
        for h in range(N_HEADS):
            for c in range(nq):
                prev_ref[c, h] = states[c][h]
            state_ref[h] = states[nq][h]
        y_ref[...] = jnp.concatenate([_join_heads(ys[c]) for c in range(nq)], axis=0)

    vec = pl.BlockSpec((N_HEADS, LANE), _fixed)
    return _hosted_call(
        body, "ssd_fwd", nc // nq,
        in_specs=[pl.BlockSpec((rows, XBC_W), _row),
                  pl.BlockSpec((8, XBC_W), lambda i: (jnp.maximum(i * (rows // 8) - 1, 0), 0)),
                  pl.BlockSpec((rows, LANE), _row),
                  pl.BlockSpec((4, XBC_W), _fixed), pl.BlockSpec((1, XBC_W), _fixed), vec, vec,
                  pl.BlockSpec((N_HEADS, LANE), _fixed)],
        out_specs=[pl.BlockSpec((rows, SSM_W), _row),
                   pl.BlockSpec((nq, N_HEADS, D_STATE, LANE), lambda i: (i, 0, 0, 0)),
                   pl.BlockSpec((rows, XBC_W), _row)],
        out_shape=[jax.ShapeDtypeStruct((S, SSM_W), F32), jax.ShapeDtypeStruct((nc, N_HEADS, D_STATE, LANE), F32),
                   jax.ShapeDtypeStruct((S, XBC_W), F32)],
        scratch_shapes=[pltpu.VMEM((N_HEADS, D_STATE, LANE), F32)],
        args=(xbc, xbc, dt_raw, conv_w, conv_b, dtb_row, alog_row, d_exp), xchg=xchg, cparams=_cparams(),
    )


def _ssd_bwd(xbc, pre_act, dt_raw, prev_states, dy, conv_w, dtb_row, alog_row, d_exp, xchg):
    S = xbc.shape[0]
    nc = S // BLK
    nq = SSD_BWD_CHUNKS_PER_STEP if nc % SSD_BWD_CHUNKS_PER_STEP == 0 else 1
    rows, n_steps = nq * BLK, nc // nq

    def body(xbc_ref, halo_ref, pre_ref, dt_ref, prev_ref, dy_ref, cw_ref, dtb_ref, alog_ref, d_ref,
             dxbc_ref, ddt_ref, dcw_ref, dvec_ref, dd_ref, gstate_ref, ghalo_ref):
        i = pl.program_id(0)

        @pl.when(i == 0)
        def _():
            gstate_ref[...] = jnp.zeros_like(gstate_ref)
            ghalo_ref[...] = jnp.zeros_like(ghalo_ref)
            dcw_ref[...] = jnp.zeros_like(dcw_ref)
            dvec_ref[...] = jnp.zeros_like(dvec_ref)
            dd_ref[...] = jnp.zeros_like(dd_ref)

        halo = halo_ref[...] * jnp.where(i < n_steps - 1, 1.0, 0.0)
        ext = jnp.concatenate([halo, xbc_ref[...]], axis=0)
        pre = pre_ref[...]
        sig = jax.nn.sigmoid(pre)
        xc = pre * sig
        split = [_ssd_split(xc[c * BLK:(c + 1) * BLK]) for c in range(nq)]
        dt_t = [_dt_rows(dt_ref[c * BLK:(c + 1) * BLK, :]) for c in range(nq)]
        prev = [[prev_ref[c, h] for h in range(N_HEADS)] for c in range(nq)]
        d_rows = [d_ref[h:h + 1, :] for h in range(N_HEADS)]
        dys = [_split_heads(dy_ref[c * BLK:(c + 1) * BLK, :], 4) for c in range(nq)]
        dh_last = [gstate_ref[h] for h in range(N_HEADS)]
        dheads, dpb, dpc, ddt_t, dprev0, ddtb, dalog, dd_rows = _ssd_chunks_bwd(
            [s[0] for s in split], [s[1] for s in split], [s[2] for s in split], dt_t, prev, dtb_ref[...],
            alog_ref[...], d_rows, dys, dh_last, _ssd_consts())
        for h in range(N_HEADS):
            gstate_ref[h] = dprev0[h]
            dd_ref[h:h + 1, :] += dd_rows[h]
        pad = jnp.zeros((BLK - N_HEADS, BLK), F32)
        ddt_ref[...] = jnp.concatenate([jnp.concatenate([ddt_t[c], pad], axis=0).T for c in range(nq)],
                                       axis=0).astype(ddt_ref.dtype)
        dvec_ref[0:N_HEADS, :] += ddtb
        dvec_ref[N_HEADS:, :] += dalog
        dxc = jnp.concatenate([jnp.concatenate([_join_heads(dheads[c])] + list(dpb[c]) + list(dpc[c]), axis=1)
                               for c in range(nq)], axis=0)
        dpre = dxc * (sig * (1.0 + pre * (1.0 - sig)))
        zeros8 = jnp.zeros((8, XBC_W), F32)
        dpe = jnp.concatenate([zeros8, dpre, zeros8], axis=0)
        n_ext = 16 + rows
        shifted = [pltpu.roll(dpe, n_ext - (3 - k), 0)[:8 + rows] for k in range(3)] + [dpe[:8 + rows]]
        dext = cw_ref[0:1, :] * shifted[0]
        for k in range(1, 4):
            dext = dext + cw_ref[k:k + 1, :] * shifted[k]
        for k in range(4):
            dcw_ref[k:k + 1, :] += jnp.sum(shifted[k] * ext, axis=0, keepdims=True)
        dcw_ref[4:5, :] += jnp.sum(dpre, axis=0, keepdims=True)
        dxbc_ref[...] = jnp.concatenate([dext[8:rows], dext[rows:] + ghalo_ref[...]], axis=0).astype(dxbc_ref.dtype)
        ghalo_ref[...] = dext[:8, :]

    vec = pl.BlockSpec((N_HEADS, LANE), _fixed)
    rev = lambda i: (n_steps - 1 - i, 0)
    return _hosted_call(
        body, "ssd_bwd", n_steps,
        in_specs=[pl.BlockSpec((rows, XBC_W), rev),
                  pl.BlockSpec((8, XBC_W), lambda i: (jnp.maximum((n_steps - 1 - i) * (rows // 8) - 1, 0), 0)),
                  pl.BlockSpec((rows, XBC_W), rev),
                  pl.BlockSpec((rows, LANE), rev),
                  pl.BlockSpec((nq, N_HEADS, D_STATE, LANE), lambda i: (n_steps - 1 - i, 0, 0, 0)),
                  pl.BlockSpec((rows, SSM_W), rev),
                  pl.BlockSpec((4, XBC_W), _fixed), vec, vec,
                  pl.BlockSpec((N_HEADS, LANE), _fixed)],
        out_specs=[pl.BlockSpec((rows, XBC_W), rev), pl.BlockSpec((rows, LANE), rev),
                   pl.BlockSpec((8, XBC_W), _fixed), pl.BlockSpec((2 * N_HEADS, LANE), _fixed),
                   pl.BlockSpec((N_HEADS, LANE), _fixed)],
        out_shape=[jax.ShapeDtypeStruct((S, XBC_W), MXU_DTYPE), jax.ShapeDtypeStruct((S, LANE), MXU_DTYPE),
                   jax.ShapeDtypeStruct((8, XBC_W), F32), jax.ShapeDtypeStruct((2 * N_HEADS, LANE), F32),
                   jax.ShapeDtypeStruct((N_HEADS, LANE), F32)],
        scratch_shapes=[pltpu.VMEM((N_HEADS, D_STATE, LANE), F32), pltpu.VMEM((8, XBC_W), F32)],
        args=(xbc, xbc, pre_act, dt_raw, prev_states, dy, conv_w, dtb_row, alog_row, d_exp), xchg=xchg,
        cparams=_cparams(VMEM_BIG),
    )


def _adamw_math(w, g, m, v):
    m = ADAM_B1 * m + (1.0 - ADAM_B1) * g
    v = ADAM_B2 * v + (1.0 - ADAM_B2) * jnp.square(g)
    m_hat = m / (1.0 - ADAM_B1 ** ADAM_STEP)
    v_hat = v / (1.0 - ADAM_B2 ** ADAM_STEP)
    delta = -ADAM_LR * (m_hat / (jnp.sqrt(v_hat) + ADAM_EPS) + ADAM_WD * w)
    return delta, m, v


def _reduce_adamw_halves(part_a, part_b, w, m, v, name):
    R, C = w.shape
    P = part_a.shape[0]
    tl = 256
    n = C // tl

    def body(a_ref, b_ref, w_ref, m_ref, v_ref, g_ref, d_ref, nm_ref, nv_ref):
        ga, gb = a_ref[0].astype(F32), b_ref[0].astype(F32)
        for i in range(1, P):
            ga, gb = ga + a_ref[i].astype(F32), gb + b_ref[i].astype(F32)
        first = jnp.where(pl.program_id(0) < n // 2, 1.0, 0.0)
        g = ga * first + gb * (1.0 - first)
        d, nm, nv = _adamw_math(w_ref[...], g, m_ref[...], v_ref[...])
        g_ref[...] = g
        d_ref[...] = d
        nm_ref[...] = nm
        nv_ref[...] = nv

    blk = pl.BlockSpec((R, tl), lambda i: (0, i))
    return pl.pallas_call(
        body, name=name, grid=(n,),
        in_specs=[pl.BlockSpec((P, R, tl), lambda i: (0, 0, jnp.minimum(i, n // 2 - 1))),
                  pl.BlockSpec((P, R, tl), lambda i: (0, 0, jnp.maximum(i - n // 2, 0))), blk, blk, blk],
        out_specs=[blk] * 4, out_shape=[jax.ShapeDtypeStruct((R, C), F32)] * 4,
    )(part_a, part_b, w, m, v)


def _reduce_adamw_hosting(parts_list, wmv_list, name, xchg):
    n_arr = len(parts_list)
    pieces = [list(p) if isinstance(p, (tuple, list)) else [p] for p in parts_list]
    n_pieces = sum(len(p) for p in pieces)
    C = wmv_list[0][0].shape[1]
    tl = 256

    def total(ref):
        g = ref[0].astype(F32)
        for i in range(1, N_DEV):
            g = g + ref[i].astype(F32)
        return g

    def body(*refs):
        p_refs, wmv_refs, o_refs = refs[:n_pieces], refs[n_pieces:n_pieces + 3 * n_arr], refs[n_pieces + 3 * n_arr:]
        at = 0
        for k in range(n_arr):
            sums = [total(r) for r in p_refs[at:at + len(pieces[k])]]
            at += len(pieces[k])
            g = sums[0] if len(sums) == 1 else jnp.concatenate(sums, axis=0)
            w_ref, m_ref, v_ref = wmv_refs[3 * k:3 * k + 3]
            d, nm, nv = _adamw_math(w_ref[...], g, m_ref[...], v_ref[...])
            for o, val in zip(o_refs[4 * k:4 * k + 4], (g, d, nm, nv)):
                o[...] = val

    in_specs = [pl.BlockSpec((N_DEV, p.shape[1], tl), lambda i: (0, 0, i)) for group in pieces for p in group]
    in_specs += [pl.BlockSpec((w.shape[0], tl), lambda i: (0, i)) for w, _, _ in wmv_list for _ in range(3)]
    out_specs = [pl.BlockSpec((w.shape[0], tl), lambda i: (0, i)) for w, _, _ in wmv_list for _ in range(4)]
    out_shape = [jax.ShapeDtypeStruct(w.shape, F32) for w, _, _ in wmv_list for _ in range(4)]
    args = [p for group in pieces for p in group] + [a for wmv in wmv_list for a in wmv]
    outs, x_out = _hosted_call(body, name, C // tl, in_specs, out_specs, out_shape, [], args, xchg,
                               _cparams(VMEM_BIG))
    return [outs[4 * k:4 * k + 4] for k in range(n_arr)], x_out


_SMALL_NAMES = ("ada_b", "norm1", "conv_w", "conv_b", "dt_bias", "A_log", "D_skip", "sinks", "attn_out_norm",
                "ssm_out_norm", "norm2", "rel_bias", "final_norm")
N_MOD = 6 * D_MODEL


def _mod_row(a0, a1, a2):
    return jnp.concatenate([a0[2:3], a0[1:2], a1[0:1], a2[2:3], a2[1:2], a2[3:4]], axis=1)


def _small_update(gathered, params):
    n_g = len(gathered)
    flat = [a for name in _SMALL_NAMES for a in params[name]]

    def body(*refs):
        a0_ref, a1_ref, a2_ref, cw_ref, dv_ref, dd_ref, ds_ref, dr_ref, c_ref = refs[:n_g]
        wmv = refs[n_g:n_g + len(flat)]
        outs = refs[n_g + len(flat):]

        def total(ref):
            t = ref[0]
            for i in range(1, N_DEV):
                t = t + ref[i]
            return t

        t0, t1, t2, tcw, tdv, tdd, tds, tdr = [total(r) for r in (a0_ref, a1_ref, a2_ref, cw_ref, dv_ref, dd_ref,
                                                                   ds_ref, dr_ref)]
        r8 = lax.broadcasted_iota(jnp.int32, (N_HEADS, LANE), 0)
        l8 = lax.broadcasted_iota(jnp.int32, (N_HEADS, LANE), 1)

        def diag_row(t):
            return jnp.sum(jnp.where(r8 == l8, t, 0.0), axis=0, keepdims=True)[:, :N_HEADS]

        def lane_sums(t):
            return diag_row(jnp.broadcast_to(jnp.sum(t, axis=1, keepdims=True), (N_HEADS, LANE)))

        me = _lin(_my_pos())
        n_cw = XBC_W // N_DEV
        cw_mine = jnp.zeros((4, n_cw), F32)
        for j in range(N_DEV):
            cw_mine = cw_mine + tcw[0:4, j * n_cw:(j + 1) * n_cw] * jnp.where(me == j, 1.0, 0.0)
        grads = {
            "ada_b": _mod_row(t0, t1, t2), "norm1": t0[0:1], "conv_w": cw_mine, "conv_b": tcw[4:5],
            "dt_bias": lane_sums(tdv[:N_HEADS]), "A_log": lane_sums(tdv[N_HEADS:]), "D_skip": lane_sums(tdd),
            "sinks": diag_row(tds), "attn_out_norm": t1[1:2, :ATTN_W], "ssm_out_norm": t1[1:2, ATTN_W:],
            "norm2": t2[0:1], "rel_bias": tdr[:, :N_HEADS], "final_norm": t2[4:5],
        }
        for k, name in enumerate(_SMALL_NAMES):
            w_ref, m_ref, v_ref = wmv[3 * k:3 * k + 3]
            g = grads[name]
            d, nm, nv = _adamw_math(w_ref[...], g, m_ref[...], v_ref[...])
            for o, val in zip(outs[4 * k:4 * k + 4], (g, d, nm, nv)):
                o[...] = val
        loss_ref, call_ref, dmod_ref = outs[4 * len(_SMALL_NAMES):]
        loss_ref[...] = t2[5:6, 0:1]
        call_ref[...] = jnp.concatenate([c_ref[i] for i in range(N_DEV)], axis=0)
        dmod_ref[...] = jnp.concatenate([_mod_row(a0_ref[i], a1_ref[i], a2_ref[i]) for i in range(N_DEV)], axis=0)

    out_shape = [jax.ShapeDtypeStruct(params[name][0].shape, F32) for name in _SMALL_NAMES for _ in range(4)]
    out_shape += [jax.ShapeDtypeStruct((1, 1), F32), jax.ShapeDtypeStruct((N_DEV, D_MODEL), F32),
                  jax.ShapeDtypeStruct((N_DEV, N_MOD), F32)]
    res = pl.pallas_call(body, name="small_update", out_shape=out_shape)(*gathered, *flat)
    upd = {name: res[4 * k:4 * k + 4] for k, name in enumerate(_SMALL_NAMES)}
    loss, c_all, dmod_all = res[4 * len(_SMALL_NAMES):]
    return upd, loss, c_all, dmod_all


def _ada_w_update(c_all, dmod_all, w, m, v):
    chunk = w.shape[1]

    def body(c_ref, dm_ref, w_ref, m_ref, v_ref, g_ref, d_ref, nm_ref, nv_ref):
        me = _lin(_my_pos())
        dm = jnp.zeros((N_DEV, chunk), F32)
        for j in range(N_DEV):
            dm = dm + dm_ref[:, j * chunk:(j + 1) * chunk] * jnp.where(me == j, 1.0, 0.0)
        g = lax.dot_general(_silu(c_ref[...]), dm, (((0,), (0,)), ((), ())), precision=HI,
                            preferred_element_type=F32)
        d, nm, nv = _adamw_math(w_ref[...], g, m_ref[...], v_ref[...])
        g_ref[...] = g
        d_ref[...] = d
        nm_ref[...] = nm
        nv_ref[...] = nv

    tr = 256
    blk = pl.BlockSpec((tr, chunk), _row)
    return pl.pallas_call(
        body, name="ada_w_update", grid=(w.shape[0] // tr,),
        in_specs=[pl.BlockSpec((N_DEV, tr), lambda i: (0, i)), pl.BlockSpec(dmod_all.shape, _fixed), blk, blk, blk],
        out_specs=[blk] * 4, out_shape=[jax.ShapeDtypeStruct(w.shape, F32)] * 4,
    )(c_all, dmod_all, w, m, v)


def _local_step(x, tgt, c, mod, w_in, conv_w, w_o_mine, w_gu_mine, w_d_mine, p):
    S = x.shape[0]
    tm = min(512, S)
    tmm = min(256, S)
    tw = min(2048, S)
    shift1, scale1, gate1, shift2, scale2, gate2 = [mod[i:i + 1] for i in range(6)]
    buckets = jnp.asarray(_t5_bucket_table())
    per_head = lambda a: jnp.broadcast_to(a.reshape(N_HEADS, 1), (N_HEADS, LANE))
    dtb_row, alog_row, d_exp = per_head(p["dt_bias"]), per_head(p["A_log"]), per_head(p["D_skip"])
    sinks = p["sinks"].reshape(N_HEADS)

    d_cut, gu_cut = WD_CUT, WGU_CUTS
    n_d, n_gu = w_d_mine.shape[0], w_gu_mine.shape[0]
    (qkv, z, xbc, dt_raw), (g_d_a,) = _in_proj_fwd(x, p["norm1"], scale1, shift1, w_in, tm,
                                                   ([(w_d_mine, 0, d_cut)], "two-level"))
    bias = _attn_bias(buckets, p["rel_bias"])
    (ya,), (g_gu_a,) = _attn_fwd(qkv, bias, sinks, ([(w_gu_mine, 0, gu_cut[0])], "two-level"))
    (ys, prev_states, pre_act), (g_gu_b, g_o) = _ssd_fwd(
        xbc, dt_raw, conv_w, p["conv_b"], dtb_row, alog_row, d_exp,
        ([(w_gu_mine, gu_cut[0], gu_cut[1] - gu_cut[0]), w_o_mine], "two-level"))
    w_o = g_o.reshape(D_MODEL, D_MODEL)
    x1, (g_gu_c, g_d_b) = _out_proj_fwd(
        x, ya, ys, z, p["attn_out_norm"], p["ssm_out_norm"], gate1, w_o, tm,
        ([(w_gu_mine, gu_cut[1], n_gu - gu_cut[1]), (w_d_mine, d_cut, n_d - d_cut)], "two-level"))
    dx1, h2, dgu, act, dmlp, acc2 = _mlp_loss(x1, tgt, p["norm2"], scale2, shift2, gate2, p["final_norm"],
                                              (g_gu_a, g_gu_b, g_gu_c), (g_d_a, g_d_b), tmm)
    g_w_gu = _wgrad(dgu, h2, 2 * D_FF // 4, tw, "wgrad_gate_up")
    g_w_d = _wgrad(act, dmlp, D_FF // 2, tw, "wgrad_down")
    gu_slots = g_w_gu.reshape(N_DEV, 2 * D_FF // N_DEV, D_MODEL)
    (dya, dys, dz, g_w_o, acc1), (r_gu_a,) = _out_proj_bwd(
        dx1, ya, ys, z, p["attn_out_norm"], p["ssm_out_norm"], gate1, w_o, tm, ([gu_slots], ("rows", 0, GGU_CUT)))
    (dq, dkv, dbias, dsk), (r_d, r_o) = _attn_bwd(
        qkv, ya, dya, bias, sinks,
        ([g_w_d.reshape(N_DEV, D_FF // N_DEV, D_MODEL), g_w_o.reshape(N_DEV, D_MODEL // N_DEV, D_MODEL)], True))
    drel, dsink = _attn_finish(dbias, dsk, buckets)
    (dxbc, ddt, dcw, dvec, dd), (r_gu_b, *early) = _ssd_bwd(
        xbc, pre_act, dt_raw, prev_states, dys, conv_w, dtb_row, alog_row, d_exp,
        [([gu_slots], ("rows", GGU_CUT, 2 * D_FF // N_DEV - GGU_CUT)), ([acc1, acc2, dsink, drel, c], False)])
    r_gu = (r_gu_a, r_gu_b)
    gx, h1, acc0 = _in_proj_bwd(x, dx1, dq, dkv, dz, dxbc, ddt, p["norm1"], scale1, shift1, w_in, tm)
    half = D_MODEL // 2
    slots = lambda g: g[:IN_W].reshape(N_DEV, IN_W // N_DEV, half)
    g_in_a = _wgrad((dq, dkv, dz, dxbc, ddt), h1, IN_PAD, tw, "wgrad_in_a", g_cols=(half, 0))
    g_in_b, (r_in_a,) = _wgrad((dq, dkv, dz, dxbc, ddt), h1, IN_PAD, tw, "wgrad_in_b",
                               ([slots(g_in_a)], "two-level scatter"), g_cols=(half, 1))
    return gx, (r_in_a, slots(g_in_b)), (r_o, r_gu, r_d), early, (acc0, dcw, dvec, dd)


def kernel(x, c, ada_w, ada_b, norm1, w_in, conv_w, conv_b, dt_bias, A_log, D_skip, sinks, attn_out_norm, ssm_out_norm, w_o, norm2, w_gate_up, w_down, rel_bias, final_norm, loss_target, m_ada_w, m_ada_b, m_norm1, m_w_in, m_conv_w, m_conv_b, m_dt_bias, m_A_log, m_D_skip, m_sinks, m_attn_out_norm, m_ssm_out_norm, m_w_o, m_norm2, m_w_gate_up, m_w_down, m_rel_bias, m_final_norm, v_ada_w, v_ada_b, v_norm1, v_w_in, v_conv_w, v_conv_b, v_dt_bias, v_A_log, v_D_skip, v_sinks, v_attn_out_norm, v_ssm_out_norm, v_w_o, v_norm2, v_w_gate_up, v_w_down, v_rel_bias, v_final_norm):
    two_d = lambda a: a if a.ndim == 2 else a.reshape(-1, a.shape[-1])
    small_params = dict(
        ada_b=(ada_b, m_ada_b, v_ada_b), norm1=(norm1, m_norm1, v_norm1), conv_w=(conv_w, m_conv_w, v_conv_w),
        conv_b=(conv_b, m_conv_b, v_conv_b), dt_bias=(dt_bias, m_dt_bias, v_dt_bias), A_log=(A_log, m_A_log, v_A_log),
        D_skip=(D_skip, m_D_skip, v_D_skip), sinks=(sinks, m_sinks, v_sinks),
        attn_out_norm=(attn_out_norm, m_attn_out_norm, v_attn_out_norm),
        ssm_out_norm=(ssm_out_norm, m_ssm_out_norm, v_ssm_out_norm), norm2=(norm2, m_norm2, v_norm2),
        rel_bias=(rel_bias, m_rel_bias, v_rel_bias), final_norm=(final_norm, m_final_norm, v_final_norm))
    small_params = {k: tuple(two_d(a) for a in v) for k, v in small_params.items()}
    S = x.shape[1]
    xs, tgt = x.reshape(S, D_MODEL), loss_target.reshape(S, D_MODEL)
    ada_w2 = ada_w[0]
    chunk = ada_w2.shape[1]
    t_in = [jnp.transpose(a[0]) for a in (w_in, m_w_in, v_w_in)]
    t_gu = [jnp.transpose(a[0]) for a in (w_gate_up, m_w_gate_up, v_w_gate_up)]

    mod, (g_in, g_cw) = _mod_and_gather(c, ada_w2, ada_b.reshape(N_DEV, chunk), [t_in[0].astype(WIRE_DTYPE), conv_w[0]])
    mod = mod.reshape(6, D_MODEL)
    w_in_full = jnp.pad(g_in.reshape(IN_W, D_MODEL), ((0, IN_PAD - IN_W), (0, 0)))
    conv_w_full = jnp.transpose(g_cw, (1, 0, 2)).reshape(4, XBC_W)

    p = {k: v[0] for k, v in small_params.items()}
    gx, (r_in_a, gw_in_b), (r_o, r_gu, r_d), early, late_blocks = _local_step(
        xs, tgt, c, mod, w_in_full, conv_w_full, w_o[0].astype(WIRE_DTYPE), t_gu[0].astype(WIRE_DTYPE),
        w_down[0].astype(WIRE_DTYPE), p)

    _, (r_in_b, *late) = _hosted_call(
        lambda: None, "scatter_in_b", 1, [], [], [], [], (),
        [([gw_in_b], "two-level scatter"), (list(late_blocks), False)], _cparams())
    (u_gu, u_d, u_o), _ = _reduce_adamw_hosting(
        [r_gu, r_d, r_o], [tuple(t_gu), (w_down[0], m_w_down[0], v_w_down[0]), (w_o[0], m_w_o[0], v_w_o[0])],
        "adamw_big", [])
    gathered = (late[0], early[0], early[1], late[1], late[2], late[3], early[2], early[3], early[4])

    small, loss, c_all, dmod_all = _small_update(gathered, small_params)

    big = {
        "ada_w": _ada_w_update(c_all, dmod_all, ada_w2, m_ada_w[0], v_ada_w[0]),
        "w_in": [jnp.transpose(a) for a in _reduce_adamw_halves(r_in_a, r_in_b, *t_in, "adamw_w_in")],
        "w_o": u_o,
        "w_gate_up": [jnp.transpose(a) for a in u_gu],
        "w_down": u_d,
    }
    big.update(small)

    order = ['ada_w', 'ada_b', 'norm1', 'w_in', 'conv_w', 'conv_b', 'dt_bias', 'A_log', 'D_skip', 'sinks',
             'attn_out_norm', 'ssm_out_norm', 'w_o', 'norm2', 'w_gate_up', 'w_down', 'rel_bias', 'final_norm']
    shapes = dict(ada_w=ada_w.shape, ada_b=ada_b.shape, norm1=norm1.shape, w_in=w_in.shape, conv_w=conv_w.shape,
                  conv_b=conv_b.shape, dt_bias=dt_bias.shape, A_log=A_log.shape, D_skip=D_skip.shape,
                  sinks=sinks.shape, attn_out_norm=attn_out_norm.shape, ssm_out_norm=ssm_out_norm.shape,
                  w_o=w_o.shape, norm2=norm2.shape, w_gate_up=w_gate_up.shape, w_down=w_down.shape,
                  rel_bias=rel_bias.shape, final_norm=final_norm.shape)
    outs = [[], [], [], []]
    for name in order:
        for kind in range(4):
            outs[kind].append(big[name][kind].reshape(shapes[name]))
    return (loss.reshape(()), gx.reshape(x.shape), *outs[0], *outs[1], *outs[2], *outs[3])
```

```python
import numpy as np
import jax
import jax.numpy as jnp
from jax import lax
from jax.experimental import pallas as pl
from jax.experimental.pallas import tpu as pltpu

F32 = jnp.float32
MXU_DTYPE = jnp.bfloat16
WIRE_DTYPE = jnp.bfloat16
HI = lax.Precision.HIGHEST
MESH = pl.DeviceIdType.MESH
N_DEV = 8

D_MODEL = 1024
ATTN_W = 512
KV_W = 128
SSM_W = 512
XBC_W = 1024
N_HEADS = 8
D_STATE = 128
D_FF = 2816
IN_W = 2312
IN_PAD = 2432
BLK = 128
N_BUCKETS = 32
EPS = 1e-6
LANE = 128
HALF = 64

ADAM_LR, ADAM_B1, ADAM_B2, ADAM_EPS, ADAM_WD, ADAM_STEP = 0.001, 0.9, 0.999, 1e-08, 0.01, 10

VMEM_BIG = 56 * 1024 * 1024
WD_CUT = 288
WGU_CUTS = (240, 496)
GGU_CUT = 304


def _cparams(vmem=None):
    if vmem is None:
        return pltpu.CompilerParams()
    return pltpu.CompilerParams(vmem_limit_bytes=vmem)


def _mm(a, b):
    return jnp.dot(a.astype(MXU_DTYPE), b.astype(MXU_DTYPE), preferred_element_type=F32)


def _mm_nt(a, b):
    return lax.dot_general(a.astype(MXU_DTYPE), b.astype(MXU_DTYPE), (((1,), (1,)), ((), ())),
                           preferred_element_type=F32)


def _mm_tn(a, b):
    return lax.dot_general(a.astype(MXU_DTYPE), b.astype(MXU_DTYPE), (((0,), (0,)), ((), ())),
                           preferred_element_type=F32)


def _mm_hi(a, b):
    return jnp.dot(a, b, precision=HI, preferred_element_type=F32)


def _silu(x):
    return x * jax.nn.sigmoid(x)


def _softplus(x):
    return jnp.maximum(x, 0.0) + jnp.log1p(jnp.exp(-jnp.abs(x)))


def _rms(x, g, n):
    return x * lax.rsqrt(jnp.sum(x * x, axis=-1, keepdims=True) * (1.0 / n) + EPS) * g


def _modnorm(x, g, scale, shift):
    return _rms(x, g, x.shape[-1]) * (1.0 + scale) + shift


def _modnorm_parts(x):
    r = lax.rsqrt(jnp.sum(x * x, axis=-1, keepdims=True) * (1.0 / x.shape[-1]) + EPS)
    return r, x * r


def _modnorm_bwd(r, xhat, g, scale, dy):
    dyg = dy * (g * (1.0 + scale))
    c = jnp.sum(dyg * xhat, axis=-1, keepdims=True) * (1.0 / xhat.shape[-1])
    dx = r * (dyg - xhat * c)
    ct = jnp.sum(dy * xhat, axis=0, keepdims=True)
    return dx, ct * (1.0 + scale), ct * g, jnp.sum(dy, axis=0, keepdims=True)


def _lane_iota(shape):
    return lax.broadcasted_iota(jnp.int32, shape, len(shape) - 1)


def _split_pair(t):
    lane = _lane_iota(t.shape)
    lo = jnp.where(lane < HALF, t, 0.0)
    hi = pltpu.roll(jnp.where(lane >= HALF, t, 0.0), HALF, 1)
    return lo, hi


def _join_pair(lo, hi):
    lane = _lane_iota(lo.shape)
    return jnp.where(lane < HALF, lo, pltpu.roll(hi, HALF, 1))


def _split_heads(t, n_pairs):
    out = []
    for p in range(n_pairs):
        out.extend(_split_pair(t[:, p * LANE:(p + 1) * LANE]))
    return out


def _join_heads(hs):
    return jnp.concatenate([_join_pair(hs[2 * p], hs[2 * p + 1]) for p in range(len(hs) // 2)], axis=1)


def _t5_bucket_table():
    dist = np.arange(BLK)[:, None] + BLK - np.arange(2 * BLK)[None, :]
    n = np.maximum(dist, 0)
    max_exact = N_BUCKETS // 2
    large = max_exact + (np.log(np.maximum(n, 1) / max_exact) / np.log(128 / max_exact)
                         * (N_BUCKETS - max_exact)).astype(np.int32)
    large = np.minimum(large, N_BUCKETS - 1)
    return np.where(n < max_exact, n, large).astype(np.int32)


def _my_pos():
    return lax.axis_index("x"), lax.axis_index("y"), lax.axis_index("c")


def _peer(k):
    x, y, c = _my_pos()
    return (1 - x if k & 4 else x, 1 - y if k & 2 else y, 1 - c if k & 1 else c)


def _lin(pos):
    return 4 * pos[0] + 2 * pos[1] + pos[2]


def _xchg_copies(ins, outs, sems, scatter):
    local_sem, send_sem, recv_sem = sems
    me = _lin(_my_pos())

    def source(a, slot):
        if not scatter:
            return ins[a]
        if scatter is True:
            return ins[a].at[slot]
        return ins[a].at[slot, pl.ds(scatter[1], scatter[2])]

    local, remote = [], []
    for a in range(len(ins)):
        local.append(pltpu.make_async_copy(source(a, me), outs[a].at[me], local_sem.at[a]))
    for k in range(1, N_DEV):
        peer = _peer(k)
        for a in range(len(ins)):
            remote.append(pltpu.make_async_remote_copy(source(a, _lin(peer)), outs[a].at[me], send_sem.at[a, k - 1],
                                                       recv_sem.at[a, k - 1], device_id=peer, device_id_type=MESH))
    return local, remote


def _xchg_start(ins, outs, sems, scatter):
    local, remote = _xchg_copies(ins, outs, sems, scatter)
    for cp in local + remote:
        cp.start()


def _xchg_wait(ins, outs, sems, scatter):
    local, remote = _xchg_copies(ins, outs, sems, scatter)
    for cp in local:
        cp.wait()
    for cp in remote:
        cp.wait_send()
        cp.wait_recv()


def _xchg_shapes(arrs, scatter):
    n = len(arrs)
    if isinstance(scatter, tuple):
        out_shape = [jax.ShapeDtypeStruct((a.shape[0], scatter[2]) + a.shape[2:], a.dtype) for a in arrs]
    elif scatter:
        out_shape = [jax.ShapeDtypeStruct(a.shape, a.dtype) for a in arrs]
    else:
        out_shape = [jax.ShapeDtypeStruct((N_DEV,) + a.shape, a.dtype) for a in arrs]
    sems = [pltpu.SemaphoreType.DMA((n,)), pltpu.SemaphoreType.DMA((n, N_DEV - 1)),
            pltpu.SemaphoreType.DMA((n, N_DEV - 1))]
    return out_shape, sems


_CHIPS = (2, 4, 6)


def _g2_sems(n):
    dma = pltpu.SemaphoreType.DMA
    return [dma((n,)), dma((n, N_DEV)), dma((n, N_DEV)), dma((n, len(_CHIPS))), dma((n, len(_CHIPS)))]


class _TwoLevelGather:
    def __init__(self, ins, outs, sems, windows=None):
        self.ins, self.outs = ins, outs
        self.local_sem, self.send_sem, self.recv_sem, self.fsend_sem, self.frecv_sem = sems
        self.n = len(ins)
        self.windows = windows or [None] * self.n

    def _mine(self, a):
        w = self.windows[a]
        return self.ins[a] if w is None else self.ins[a].at[pl.ds(w[0], w[1])]

    def _direct(self, a, k):
        return pltpu.make_async_remote_copy(self._mine(a), self.outs[a].at[_lin(_my_pos())], self.send_sem.at[a, k],
                                            self.recv_sem.at[a, k], device_id=_peer(k), device_id_type=MESH)

    def _handed_on(self, a, j, origin):
        slot = self.outs[a].at[origin]
        return pltpu.make_async_remote_copy(slot, slot, self.fsend_sem.at[a, j], self.frecv_sem.at[a, j],
                                            device_id=_peer(1), device_id_type=MESH)

    def _local(self, a):
        return pltpu.make_async_copy(self._mine(a), self.outs[a].at[_lin(_my_pos())], self.local_sem.at[a])

    def start(self):
        for a in range(self.n):
            self._local(a).start()
        for k in (1,) + _CHIPS:
            for a in range(self.n):
                self._direct(a, k).start()

    def forward(self):
        for j, k in enumerate(_CHIPS):
            for a in range(self.n):
                self._direct(a, k).wait_recv()
                self._handed_on(a, j, _lin(_peer(k))).start()

    def finish(self):
        for a in range(self.n):
            self._direct(a, 1).wait_recv()
            for j, k in enumerate(_CHIPS):
                self._handed_on(a, j, _lin(_peer(k ^ 1))).wait_recv()
            self._local(a).wait()
            for k in (1,) + _CHIPS:
                self._direct(a, k).wait_send()
            for j, k in enumerate(_CHIPS):
                self._handed_on(a, j, _lin(_peer(k))).wait_send()


N_SCATTERED = 2 + len(_CHIPS)


class _TwoLevelScatter:
    def __init__(self, src, out, stage, buf_a, buf_b, sems):
        self.src, self.out, self.stage, self.buf_a, self.buf_b = src, out, stage, buf_a, buf_b
        self.local_sem, self.dsend, self.drecv, self.csend, self.crecv = sems

    def _own(self):
        return pltpu.make_async_copy(self.src.at[_lin(_my_pos())], self.out.at[0], self.local_sem.at[0])

    def _to_core(self, j):
        q = 0 if j == 0 else _CHIPS[j - 1]
        dst = self.out.at[1] if j == 0 else self.stage.at[j - 1]
        return pltpu.make_async_remote_copy(self.src.at[_lin(_peer(q ^ 1))], dst, self.dsend.at[j], self.drecv.at[j],
                                            device_id=_peer(1), device_id_type=MESH)

    def _loads(self, j):
        mine = self.src.at[_lin(_peer(_CHIPS[j]))]
        return (pltpu.make_async_copy(self.stage.at[j], self.buf_a.at[j], self.local_sem.at[1 + 2 * j]),
                pltpu.make_async_copy(mine, self.buf_b.at[j], self.local_sem.at[2 + 2 * j]))

    def _to_chip(self, j):
        return pltpu.make_async_remote_copy(self.buf_a.at[j], self.out.at[2 + j], self.csend.at[j], self.crecv.at[j],
                                            device_id=_peer(_CHIPS[j]), device_id_type=MESH)

    def start(self):
        self._own().start()
        for j in range(1 + len(_CHIPS)):
            self._to_core(j).start()

    def forward(self):
        for j in range(len(_CHIPS)):
            self._to_core(j + 1).wait_recv()
            for cp in self._loads(j):
                cp.start()
        for j in range(len(_CHIPS)):
            for cp in self._loads(j):
                cp.wait()
            self.buf_a[j] = (self.buf_a[j].astype(F32) + self.buf_b[j].astype(F32)).astype(self.buf_a.dtype)
            self._to_chip(j).start()

    def finish(self):
        self._own().wait()
        self._to_core(0).wait_recv()
        for j in range(1 + len(_CHIPS)):
            self._to_core(j).wait_send()
        for j in range(len(_CHIPS)):
            self._to_chip(j).wait_send()
            self._to_chip(j).wait_recv()


def _s2_shapes(a):
    dma = pltpu.SemaphoreType.DMA
    n_c = len(_CHIPS)
    piece = a.shape[1:]
    return (jax.ShapeDtypeStruct((N_SCATTERED,) + piece, a.dtype), jax.ShapeDtypeStruct((n_c,) + piece, a.dtype),
            [pltpu.VMEM((n_c,) + piece, a.dtype)] * 2,
            [dma((1 + 2 * n_c,)), dma((1 + n_c,)), dma((1 + n_c,)), dma((n_c,)), dma((n_c,))])


def _mod_and_gather(c, ada_w, ada_b8, arrs):
    n = len(arrs)
    chunk = ada_w.shape[1]
    out_shape = [jax.ShapeDtypeStruct((N_DEV, 1, chunk), F32)]
    out_shape += [jax.ShapeDtypeStruct((N_DEV,) + a.shape, a.dtype) for a in arrs]

    def modulation(c_ref, w_ref, b_ref, out_ref, cbuf, part, s1, r1, s2, r2):
        me = _lin(_my_pos())
        first = []
        for k in range(1, N_DEV):
            cp = pltpu.make_async_remote_copy(c_ref, cbuf.at[me], s1.at[k - 1], r1.at[k - 1],
                                              device_id=_peer(k), device_id_type=MESH)
            cp.start()
            first.append(cp)
        cbuf[me] = c_ref[...]
        for cp in first:
            cp.wait_send()
            cp.wait_recv()
        cond = _silu(jnp.concatenate([cbuf[i] for i in range(N_DEV)], axis=0))
        mod = _mm_hi(cond, w_ref[...]) + b_ref[pl.ds(me, 1), :]
        for j in range(N_DEV):
            part[j] = mod[j:j + 1, :]
        second = []
        for k in range(1, N_DEV):
            peer = _peer(k)
            cp = pltpu.make_async_remote_copy(part.at[_lin(peer)], out_ref.at[me], s2.at[k - 1], r2.at[k - 1],
                                              device_id=peer, device_id_type=MESH)
            cp.start()
            second.append(cp)
        out_ref[me] = part[me]
        for cp in second:
            cp.wait_send()
            cp.wait_recv()

    def body(*refs):
        c_ref, w_ref, b_ref = refs[:3]
        ins = refs[3:3 + n]
        mod_ref = refs[3 + n]
        outs = refs[4 + n:4 + 2 * n]
        cbuf, part, s1, r1, s2, r2 = refs[4 + 2 * n:10 + 2 * n]
        gather = _TwoLevelGather(ins, outs, refs[10 + 2 * n:])
        gather.start()
        modulation(c_ref, w_ref, b_ref, mod_ref, cbuf, part, s1, r1, s2, r2)
        gather.forward()
        gather.finish()

    hbm = pl.BlockSpec(memory_space=pltpu.HBM)
    vm = pl.BlockSpec(memory_space=pltpu.VMEM)
    dma = pltpu.SemaphoreType.DMA
    res = pl.pallas_call(
        body, name="mod_and_gather", out_shape=out_shape, in_specs=[vm, vm, vm] + [hbm] * n,
        out_specs=[vm] + [hbm] * n,
        scratch_shapes=[pltpu.VMEM((N_DEV, 1, D_MODEL), F32), pltpu.VMEM((N_DEV, 1, chunk), F32)]
        + [dma((N_DEV - 1,))] * 4 + _g2_sems(n),
    )(c, ada_w, ada_b8, *arrs)
    return res[0], res[1:]


def _hosted_call(body, name, grid, in_specs, out_specs, out_shape, scratch_shapes, args, xchg, cparams):
    xchgs = [xchg] if isinstance(xchg, tuple) else list(xchg)
    grid = (grid,) if isinstance(grid, int) else tuple(grid)
    n_in, n_out, n_scr = len(in_specs), len(out_specs), len(scratch_shapes)
    windows = [[(a[1], a[2]) if isinstance(a, tuple) else None for a in group] for group, _ in xchgs]
    xchgs = [([a[0] if isinstance(a, tuple) else a for a in group], mode) for group, mode in xchgs]
    arrs = [a for group, _ in xchgs for a in group]
    n = len(arrs)
    x_shape, x_sems, sem_counts, stage_shape, stage_bufs = [], [], [], [], []
    for (group, mode), wins in zip(xchgs, windows):
        if mode == "two-level scatter":
            (a,) = group
            res_shape, stage, bufs, sems = _s2_shapes(a)
            shapes = [res_shape]
            stage_shape.append(stage)
            stage_bufs += bufs
        else:
            shapes, sems = _xchg_shapes(group, False if mode == "two-level" else mode)
        if mode == "two-level":
            sems = _g2_sems(len(group))
            shapes = [s if w is None else jax.ShapeDtypeStruct((N_DEV, w[1]) + a.shape[1:], a.dtype)
                      for s, w, a in zip(shapes, wins, group)]
        x_shape += shapes
        x_sems += sems
        sem_counts.append(len(sems))
    n_stage = len(stage_shape)
    n_steps = int(np.prod(grid))
    staged = ("two-level", "two-level scatter")

    def hosted(*refs):
        ins, refs = refs[:n_in], refs[n_in:]
        x_in, refs = refs[:n], refs[n:]
        outs, refs = refs[:n_out], refs[n_out:]
        x_out, refs = refs[:n], refs[n:]
        stages, refs = refs[:n_stage], refs[n_stage:]
        scr, refs = refs[:n_scr], refs[n_scr:]
        bufs, sems = refs[:2 * n_stage], refs[2 * n_stage:]
        step = pl.program_id(0)
        for d in range(1, len(grid)):
            step = step * grid[d] + pl.program_id(d)
        parts, a0, s0, t0 = [], 0, 0, 0
        for (group, mode), ns, wins in zip(xchgs, sem_counts, windows):
            gi, go, gs = x_in[a0:a0 + len(group)], x_out[a0:a0 + len(group)], sems[s0:s0 + ns]
            if mode == "two-level":
                parts.append((mode, _TwoLevelGather(gi, go, gs, wins)))
            elif mode == "two-level scatter":
                parts.append((mode, _TwoLevelScatter(gi[0], go[0], stages[t0], bufs[2 * t0], bufs[2 * t0 + 1], gs)))
                t0 += 1
            else:
                parts.append((mode, (gi, go, gs, mode)))
            a0, s0 = a0 + len(group), s0 + ns

        @pl.when(step == 0)
        def _():
            for mode, x in parts:
                if mode in staged:
                    x.start()
                else:
                    _xchg_start(*x)

        for kind, at in (("two-level", (2 * n_steps) // 3), ("two-level scatter", n_steps // 4)):
            if any(mode == kind for mode, _ in parts):
                @pl.when(step == at)
                def _():
                    for mode, x in parts:
                        if mode == kind:
                            x.forward()

        body(*ins, *outs, *scr)

        @pl.when(step == n_steps - 1)
        def _():
            for mode, x in parts:
                if mode in staged:
                    x.finish()
                else:
                    _xchg_wait(*x)

    hbm = pl.BlockSpec(memory_space=pltpu.HBM)
    res = pl.pallas_call(
        hosted, name=name, grid=grid, in_specs=list(in_specs) + [hbm] * n,
        out_specs=list(out_specs) + [hbm] * (n + n_stage), out_shape=list(out_shape) + x_shape + stage_shape,
        scratch_shapes=list(scratch_shapes) + stage_bufs + x_sems, compiler_params=cparams,
    )(*args, *arrs)
    return res[:n_out], res[n_out:n_out + n]


def _row(i):
    return (i, 0)


def _fixed(i):
    return (0, 0)


def _in_proj_fwd(x, norm1, scale1, shift1, w_in, tm, xchg):
    S = x.shape[0]

    def body(x_ref, n_ref, sc_ref, sh_ref, w_ref, qkv_ref, z_ref, xbc_ref, dt_ref):
        h = _modnorm(x_ref[...], n_ref[...], sc_ref[...], sh_ref[...])
        p = _mm_nt(h, w_ref[...])
        qkv_ref[...] = p[:, :768].astype(qkv_ref.dtype)
        z_ref[...] = p[:, 768:1280]
        xbc_ref[...] = p[:, 1280:2304]
        dt_ref[...] = p[:, 2304:IN_PAD]

    vec = pl.BlockSpec((1, D_MODEL), _fixed)
    return _hosted_call(
        body, "in_proj_fwd", S // tm,
        in_specs=[pl.BlockSpec((tm, D_MODEL), _row), vec, vec, vec, pl.BlockSpec((IN_PAD, D_MODEL), _fixed)],
        out_specs=[pl.BlockSpec((tm, 768), _row), pl.BlockSpec((tm, SSM_W), _row),
                   pl.BlockSpec((tm, XBC_W), _row), pl.BlockSpec((tm, LANE), _row)],
        out_shape=[jax.ShapeDtypeStruct((S, 768), MXU_DTYPE), jax.ShapeDtypeStruct((S, SSM_W), F32),
                   jax.ShapeDtypeStruct((S, XBC_W), F32), jax.ShapeDtypeStruct((S, LANE), F32)],
        scratch_shapes=[], args=(x, norm1, scale1, shift1, w_in), xchg=xchg, cparams=_cparams(VMEM_BIG),
    )


def _in_proj_bwd(x, dx1, dq, dkv, dz, dxbc, ddt, norm1, scale1, shift1, w_in, tm):
    S = x.shape[0]

    n_steps = S // tm
    half_cols = D_MODEL // 2

    def body(x_ref, dx1_ref, dq_ref, dkv_ref, dz_ref, dxbc_ref, ddt_ref, n_ref, sc_ref, sh_ref, w_ref,
             gx_ref, h_ref, acc_ref, gw_ref, gw_acc):
        i = pl.program_id(0)

        @pl.when(i == 0)
        def _():
            acc_ref[...] = jnp.zeros_like(acc_ref)
            gw_acc[...] = jnp.zeros_like(gw_acc)

        halves = [pl.ds(k * (tm // 2), tm // 2) for k in range(2)]
        dp = [jnp.concatenate([r[rows, :] for r in (dq_ref, dkv_ref, dz_ref, dxbc_ref, ddt_ref)], axis=1)
              for rows in halves]
        dh = [_mm(dp[k], w_ref[...]) for k in range(2)]
        parts = [_modnorm_parts(x_ref[rows, :]) for rows in halves]
        hb = [(parts[k][1] * n_ref[...] * (1.0 + sc_ref[...]) + sh_ref[...]).astype(h_ref.dtype) for k in range(2)]
        gw_acc[...] += _mm_tn(dp[0], hb[0][:, :half_cols]) + _mm_tn(dp[1], hb[1][:, :half_cols])
        bwd = [_modnorm_bwd(parts[k][0], parts[k][1], n_ref[...], sc_ref[...], dh[k]) for k in range(2)]
        for k, rows in enumerate(halves):
            gx_ref[rows, :] = dx1_ref[rows, :] + bwd[k][0]
            h_ref[rows, :] = hb[k]
        acc_ref[0:1, :] += bwd[0][1] + bwd[1][1]
        acc_ref[1:2, :] += bwd[0][2] + bwd[1][2]
        acc_ref[2:3, :] += bwd[0][3] + bwd[1][3]

        @pl.when(i == n_steps - 1)
        def _():
            gw_ref[...] = gw_acc[...].astype(gw_ref.dtype)

    vec = pl.BlockSpec((1, D_MODEL), _fixed)
    return pl.pallas_call(
        body, name="in_proj_bwd", grid=(n_steps,),
        in_specs=[pl.BlockSpec((tm, D_MODEL), _row), pl.BlockSpec((tm, D_MODEL), _row),
                  pl.BlockSpec((tm, ATTN_W), _row), pl.BlockSpec((tm, 2 * KV_W), _row),
                  pl.BlockSpec((tm, SSM_W), _row), pl.BlockSpec((tm, XBC_W), _row), pl.BlockSpec((tm, LANE), _row),
                  vec, vec, vec, pl.BlockSpec((IN_PAD, D_MODEL), _fixed)],
        out_specs=[pl.BlockSpec((tm, D_MODEL), _row), pl.BlockSpec((tm, D_MODEL), _row),
                   pl.BlockSpec((8, D_MODEL), _fixed), pl.BlockSpec((IN_PAD, half_cols), _fixed)],
        out_shape=[jax.ShapeDtypeStruct((S, D_MODEL), F32), jax.ShapeDtypeStruct((S, D_MODEL), MXU_DTYPE),
                   jax.ShapeDtypeStruct((8, D_MODEL), F32), jax.ShapeDtypeStruct((IN_PAD, half_cols), WIRE_DTYPE)],
        scratch_shapes=[pltpu.VMEM((IN_PAD, half_cols), F32)],
        compiler_params=_cparams(VMEM_BIG),
    )(x, dx1, dq, dkv, dz, dxbc, ddt, norm1, scale1, shift1, w_in)


def _out_stage(ya, ys0, ys1, z0, z1, an, sn0, sn1):
    half = SSM_W // 2
    a = _rms(ya, an, ATTN_W)
    g0 = _rms(ys0 * _silu(z0), sn0, half)
    g1 = _rms(ys1 * _silu(z1), sn1, half)
    return jnp.concatenate([a, g0, g1], axis=1)


def _out_stage_args(ya_ref, ys_ref, z_ref, an_ref, sn_ref):
    half = SSM_W // 2
    return (ya_ref[...], ys_ref[:, :half], ys_ref[:, half:], z_ref[:, :half], z_ref[:, half:],
            an_ref[...], sn_ref[:, :half], sn_ref[:, half:])


def _out_proj_fwd(x, ya, ys, z, an, sn, gate1, w_o, tm, xchg):
    S = x.shape[0]

    def body(x_ref, ya_ref, ys_ref, z_ref, an_ref, sn_ref, g_ref, w_ref, x1_ref):
        u = _out_stage(*_out_stage_args(ya_ref, ys_ref, z_ref, an_ref, sn_ref))
        x1_ref[...] = x_ref[...] + g_ref[...] * _mm(u, w_ref[...])

    half = pl.BlockSpec((tm, ATTN_W), _row)
    hvec = pl.BlockSpec((1, ATTN_W), _fixed)
    (x1,), x_out = _hosted_call(
        body, "out_proj_fwd", S // tm,
        in_specs=[pl.BlockSpec((tm, D_MODEL), _row), half, half, half, hvec, hvec,
                  pl.BlockSpec((1, D_MODEL), _fixed), pl.BlockSpec((D_MODEL, D_MODEL), _fixed)],
        out_specs=[pl.BlockSpec((tm, D_MODEL), _row)],
        out_shape=[jax.ShapeDtypeStruct((S, D_MODEL), F32)],
        scratch_shapes=[], args=(x, ya, ys, z, an, sn, gate1, w_o), xchg=xchg, cparams=_cparams(VMEM_BIG),
    )
    return x1, x_out


def _out_proj_bwd(dx1, ya, ys, z, an, sn, gate1, w_o, tm, xchg):
    S = dx1.shape[0]
    n_steps = S // tm

    def body(dx1_ref, ya_ref, ys_ref, z_ref, an_ref, sn_ref, g_ref, w_ref,
             dya_ref, dys_ref, dz_ref, gw_ref, acc_ref, gw_acc):
        i = pl.program_id(0)

        @pl.when(i == 0)
        def _():
            acc_ref[...] = jnp.zeros_like(acc_ref)
            gw_acc[...] = jnp.zeros_like(gw_acc)

        u, vjp = jax.vjp(_out_stage, *_out_stage_args(ya_ref, ys_ref, z_ref, an_ref, sn_ref))
        dx1 = dx1_ref[...]
        ub = u.astype(MXU_DTYPE)
        mix = _mm(ub, w_ref[...])
        dmix = dx1 * g_ref[...]
        dmixb = dmix.astype(MXU_DTYPE)
        du = _mm_nt(dmixb, w_ref[...])
        gw_acc[...] += _mm_tn(ub, dmixb)
        dya, dys0, dys1, dz0, dz1, dan, dsn0, dsn1 = vjp(du)
        dya_ref[...] = dya
        dys_ref[...] = jnp.concatenate([dys0, dys1], axis=1)
        dz_ref[...] = jnp.concatenate([dz0, dz1], axis=1).astype(dz_ref.dtype)
        acc_ref[0:1, :] += jnp.sum(dx1 * mix, axis=0, keepdims=True)
        acc_ref[1:2, :] += jnp.concatenate([dan, dsn0, dsn1], axis=1)

        @pl.when(i == n_steps - 1)
        def _():
            gw_ref[...] = gw_acc[...].astype(gw_ref.dtype)

    half = pl.BlockSpec((tm, ATTN_W), _row)
    hvec = pl.BlockSpec((1, ATTN_W), _fixed)
    full = pl.BlockSpec((tm, D_MODEL), _row)
    return _hosted_call(
        body, "out_proj_bwd", n_steps,
        in_specs=[full, half, half, half, hvec, hvec,
                  pl.BlockSpec((1, D_MODEL), _fixed), pl.BlockSpec((D_MODEL, D_MODEL), _fixed)],
        out_specs=[half, half, half, pl.BlockSpec((D_MODEL, D_MODEL), _fixed), pl.BlockSpec((8, D_MODEL), _fixed)],
        out_shape=[jax.ShapeDtypeStruct((S, ATTN_W), F32)] * 2 + [jax.ShapeDtypeStruct((S, ATTN_W), MXU_DTYPE),
                   jax.ShapeDtypeStruct((D_MODEL, D_MODEL), WIRE_DTYPE), jax.ShapeDtypeStruct((8, D_MODEL), F32)],
        scratch_shapes=[pltpu.VMEM((D_MODEL, D_MODEL), F32)],
        args=(dx1, ya, ys, z, an, sn, gate1, w_o), xchg=xchg, cparams=_cparams(VMEM_BIG),
    )


def _loss_rows(x2, fn, tgt):
    y = _rms(x2, fn, D_MODEL)
    per_row = jnp.sum(jnp.square(y - tgt), axis=1, keepdims=True)
    return jnp.sum(per_row, axis=0, keepdims=True) * (0.5 / D_MODEL)


def _mlp_loss(x1, tgt, norm2, scale2, shift2, gate2, fnorm, w_gu, w_d, tm):
    S = x1.shape[0]
    n_pieces = len(w_gu) + len(w_d)

    def body(*refs):
        x1_ref, t_ref, n_ref, sc_ref, sh_ref, g_ref, fn_ref = refs[:7]
        piece_refs = refs[7:7 + n_pieces]
        dx1_ref, h_ref, dgu_ref, act_ref, dmlp_ref, acc_ref, wgu, wd, wsem = refs[7 + n_pieces:]

        @pl.when(pl.program_id(0) == 0)
        def _():
            acc_ref[...] = jnp.zeros_like(acc_ref)
            copies = []
            for dst, pieces in ((wgu, piece_refs[:len(w_gu)]), (wd, piece_refs[len(w_gu):])):
                shard = sum(p.shape[1] for p in pieces)
                off = 0
                for p in pieces:
                    for j in range(N_DEV):
                        copies.append(pltpu.make_async_copy(p.at[j], dst.at[pl.ds(j * shard + off, p.shape[1])],
                                                            wsem.at[len(copies)]))
                    off += p.shape[1]
            for cp in copies:
                cp.start()
            for cp in copies:
                cp.wait()

        x1 = x1_ref[...]
        gate2 = g_ref[...]
        h, vjp_h = jax.vjp(_modnorm, x1, n_ref[...], sc_ref[...], sh_ref[...])
        hb = h.astype(MXU_DTYPE)
        gu = _mm_nt(hb, wgu[...])
        g, u = gu[:, :D_FF], gu[:, D_FF:]
        sg = jax.nn.sigmoid(g)
        silu_g = g * sg
        act = (silu_g * u).astype(MXU_DTYPE)
        mlp = _mm(act, wd[...])
        x2 = x1 + gate2 * mlp
        loss, vjp_loss = jax.vjp(_loss_rows, x2, fn_ref[...], t_ref[...])
        dx2, dfn, _ = vjp_loss(jnp.ones((1, 1), F32))
        dmlp = (dx2 * gate2).astype(MXU_DTYPE)
        dact = _mm_nt(dmlp, wd[...])
        dg = dact * u * (sg * (1.0 + g * (1.0 - sg)))
        du = dact * silu_g
        dgu = jnp.concatenate([dg, du], axis=1).astype(MXU_DTYPE)
        dh = _mm(dgu, wgu[...])
        dx, dn, dsc, dsh = vjp_h(dh)
        dx1_ref[...] = dx2 + dx
        h_ref[...] = hb
        dgu_ref[...] = dgu
        act_ref[...] = act
        dmlp_ref[...] = dmlp
        acc_ref[0:1, :] += dn
        acc_ref[1:2, :] += dsc
        acc_ref[2:3, :] += dsh
        acc_ref[3:4, :] += jnp.sum(dx2 * mlp, axis=0, keepdims=True)
        acc_ref[4:5, :] += dfn
        acc_ref[5:6, :] += jnp.broadcast_to(loss, (1, D_MODEL))

    full = pl.BlockSpec((tm, D_MODEL), _row)
    vec = pl.BlockSpec((1, D_MODEL), _fixed)
    anyspec = pl.BlockSpec(memory_space=pl.ANY)
    return pl.pallas_call(
        body, name="mlp_loss", grid=(S // tm,),
        in_specs=[full, full, vec, vec, vec, vec, vec] + [anyspec] * n_pieces,
        out_specs=[full, full, pl.BlockSpec((tm, 2 * D_FF), _row), pl.BlockSpec((tm, D_FF), _row), full,
                   pl.BlockSpec((8, D_MODEL), _fixed)],
        out_shape=[jax.ShapeDtypeStruct((S, D_MODEL), F32), jax.ShapeDtypeStruct((S, D_MODEL), MXU_DTYPE),
                   jax.ShapeDtypeStruct((S, 2 * D_FF), MXU_DTYPE), jax.ShapeDtypeStruct((S, D_FF), MXU_DTYPE),
                   jax.ShapeDtypeStruct((S, D_MODEL), MXU_DTYPE), jax.ShapeDtypeStruct((8, D_MODEL), F32)],
        scratch_shapes=[pltpu.VMEM((2 * D_FF, D_MODEL), MXU_DTYPE), pltpu.VMEM((D_FF, D_MODEL), MXU_DTYPE),
                        pltpu.SemaphoreType.DMA((N_DEV * n_pieces,))],
        compiler_params=_cparams(VMEM_BIG),
    )(x1, tgt, norm2, scale2, shift2, gate2, fnorm, *w_gu, *w_d)


def _wgrad(a, g, tk, ts, name, xchg=None, g_cols=None):
    pieces = list(a) if isinstance(a, (list, tuple)) else [a]
    S = pieces[0].shape[0]
    K = sum(p.shape[1] for p in pieces)
    assert len(pieces) == 1 or tk == K
    N, col = (g.shape[1], 0) if g_cols is None else g_cols
    ns = S // ts
    n_a = len(pieces)

    def body(*refs):
        a_refs, (g_ref, o_ref, acc_ref) = refs[:n_a], refs[n_a:]
        s = pl.program_id(1)

        @pl.when(s == 0)
        def _():
            acc_ref[...] = jnp.zeros_like(acc_ref)

        a_blk = a_refs[0][...] if n_a == 1 else jnp.concatenate([r[...] for r in a_refs], axis=1)
        acc_ref[...] += _mm_tn(a_blk, g_ref[...])

        @pl.when(s == ns - 1)
        def _():
            o_ref[...] = acc_ref[...].astype(o_ref.dtype)

    if n_a == 1:
        in_specs = [pl.BlockSpec((ts, tk), lambda j, s: (s, j))]
    else:
        in_specs = [pl.BlockSpec((ts, p.shape[1]), lambda j, s: (s, 0)) for p in pieces]
    in_specs.append(pl.BlockSpec((ts, N), lambda j, s: (s, col)))
    out_spec = pl.BlockSpec((tk, N), lambda j, s: (j, 0))
    out_shape = jax.ShapeDtypeStruct((K, N), WIRE_DTYPE)
    scratch = [pltpu.VMEM((tk, N), F32)]
    args = (*pieces, g)
    if xchg is None:
        return pl.pallas_call(body, name=name, grid=(K // tk, ns), in_specs=in_specs, out_specs=out_spec,
                              out_shape=out_shape, scratch_shapes=scratch, compiler_params=_cparams(VMEM_BIG))(*args)
    (out,), x_out = _hosted_call(body, name, (K // tk, ns), in_specs, [out_spec], [out_shape], scratch, args, xchg,
                                 _cparams(VMEM_BIG))
    return out, x_out


SSD_CHUNKS_PER_STEP = 4
SSD_BWD_CHUNKS_PER_STEP = 4
ATTN_BLOCKS_PER_STEP = 4
MASKED = -1e30
QK_SCALE = HALF ** -0.5


def _attn_bias(buckets, rel_bias):
    def body(bk_ref, relb_ref, out_ref):
        bk = bk_ref[...]
        i = lax.broadcasted_iota(jnp.int32, (BLK, 2 * BLK), 0)
        j = lax.broadcasted_iota(jnp.int32, (BLK, 2 * BLK), 1)
        window = (j > i) & (j <= i + BLK)
        for h in range(N_HEADS):
            acc = jnp.zeros((BLK, 2 * BLK), F32)
            for b in range(N_BUCKETS):
                acc = jnp.where(bk == b, relb_ref[b, h], acc)
            out_ref[0, h] = jnp.where(window, acc, MASKED)
            out_ref[1, h] = jnp.where(window & (j >= BLK), acc, MASKED)

    return pl.pallas_call(
        body, name="attn_bias", out_shape=jax.ShapeDtypeStruct((2, N_HEADS, BLK, 2 * BLK), F32),
        in_specs=[pl.BlockSpec(memory_space=pltpu.VMEM), pl.BlockSpec(memory_space=pltpu.SMEM)],
    )(buckets, rel_bias)


def _attn_fwd(qkv, bias, sinks, xchg):
    S = qkv.shape[0]
    nb = S // BLK

    nq = ATTN_BLOCKS_PER_STEP if nb % ATTN_BLOCKS_PER_STEP == 0 else 1
    rows = nq * BLK

    def body(q_ref, kvp_ref, kvc_ref, bias_ref, sinks_ref, y_ref):
        i = pl.program_id(0)
        q = q_ref[...].astype(F32) * QK_SCALE
        kv = jnp.concatenate([kvp_ref[...], kvc_ref[...]], axis=0).astype(F32)
        k_lo, k_hi = _split_pair(kv[:, :LANE])
        v_lo, v_hi = _split_pair(kv[:, LANE:])
        bands = [[t[b * BLK:(b + 2) * BLK].astype(MXU_DTYPE) for t in (k_lo, k_hi, v_lo, v_hi)] for b in range(nq)]
        q_heads = [_split_heads(q[b * BLK:(b + 1) * BLK], 4) for b in range(nq)]
        first = [jnp.where(i == 0, 1, 0) if b == 0 else 0 for b in range(nq)]
        items = [(b, h) for b in range(nq) for h in range(N_HEADS)]
        s = [_mm_nt(q_heads[b][h].astype(MXU_DTYPE), bands[b][h // 4]) + bias_ref[first[b], h] for b, h in items]
        m = [jnp.maximum(jnp.max(s[n], axis=-1, keepdims=True), sinks_ref[h]) for n, (b, h) in enumerate(items)]
        p = [jnp.exp(s[n] - m[n]) for n in range(len(items))]
        rinv = [1.0 / (jnp.sum(p[n], axis=-1, keepdims=True) + jnp.exp(sinks_ref[h] - m[n]))
                for n, (b, h) in enumerate(items)]
        out = [_mm(p[n], bands[b][2 + h // 4]) * rinv[n] for n, (b, h) in enumerate(items)]
        y_ref[...] = jnp.concatenate([_join_heads(out[b * N_HEADS:(b + 1) * N_HEADS]) for b in range(nq)], axis=0)

    smem = pl.BlockSpec(memory_space=pltpu.SMEM)
    return _hosted_call(
        body, "attn_fwd", nb // nq,
        in_specs=[pl.BlockSpec((rows, ATTN_W), _row),
                  pl.BlockSpec((BLK, 2 * KV_W), lambda i: (jnp.maximum(i * nq - 1, 0), 2)),
                  pl.BlockSpec((rows, 2 * KV_W), lambda i: (i, 2)),
                  pl.BlockSpec((2, N_HEADS, BLK, 2 * BLK), lambda i: (0, 0, 0, 0)), smem],
        out_specs=[pl.BlockSpec((rows, ATTN_W), _row)],
        out_shape=[jax.ShapeDtypeStruct((S, ATTN_W), F32)],
        scratch_shapes=[],
        args=(qkv, qkv, qkv, bias, sinks), xchg=xchg, cparams=_cparams(),
    )


def _attn_bwd(qkv, y, dy, bias, sinks, xchg):
    S = qkv.shape[0]
    nb = S // BLK
    nq = ATTN_BLOCKS_PER_STEP if nb % ATTN_BLOCKS_PER_STEP == 0 else 1
    rows, n_steps = nq * BLK, nb // nq

    def body(q_ref, kvp_ref, kvc_ref, y_ref, dy_ref, bias_ref, sinks_ref, dq_ref, dkv_ref, dbias_ref, dsk_ref, carry_ref):
        i = pl.program_id(0)

        @pl.when(i == 0)
        def _():
            dbias_ref[...] = jnp.zeros_like(dbias_ref)
            dsk_ref[...] = jnp.zeros_like(dsk_ref)
            carry_ref[...] = jnp.zeros_like(carry_ref)

        q = q_ref[...].astype(F32) * QK_SCALE
        kv = jnp.concatenate([kvp_ref[...], kvc_ref[...]], axis=0).astype(F32)
        k_lo, k_hi = _split_pair(kv[:, :LANE])
        v_lo, v_hi = _split_pair(kv[:, LANE:])
        bands = [[t[b * BLK:(b + 2) * BLK].astype(MXU_DTYPE) for t in (k_lo, k_hi, v_lo, v_hi)] for b in range(nq)]
        rows_of = lambda ref, b: ref[b * BLK:(b + 1) * BLK, :]
        first = [jnp.where(i == n_steps - 1, 1, 0) if b == 0 else 0 for b in range(nq)]
        items = [(b, h) for b in range(nq) for h in range(N_HEADS)]
        at = lambda b, h: b * N_HEADS + h
        q_heads = [hd for b in range(nq) for hd in _split_heads(q[b * BLK:(b + 1) * BLK], 4)]
        y_heads = [hd for b in range(nq) for hd in _split_heads(rows_of(y_ref, b), 4)]
        dy_heads = [hd for b in range(nq) for hd in _split_heads(rows_of(dy_ref, b), 4)]
        qs = [q_heads[n].astype(MXU_DTYPE) for n in range(len(items))]
        s = [_mm_nt(qs[at(b, h)], bands[b][h // 4]) + bias_ref[first[b], h] for b, h in items]
        m = [jnp.maximum(jnp.max(s[at(b, h)], axis=-1, keepdims=True), sinks_ref[h]) for b, h in items]
        p = [jnp.exp(s[n] - m[n]) for n in range(len(items))]
        esink = [jnp.exp(sinks_ref[h] - m[at(b, h)]) for b, h in items]
        rinv = [1.0 / (jnp.sum(p[n], axis=-1, keepdims=True) + esink[n]) for n in range(len(items))]
        t = [dy_heads[n] * rinv[n] for n in range(len(items))]
        delta = [jnp.sum(t[n] * y_heads[n], axis=-1, keepdims=True) for n in range(len(items))]
        tb = [t[n].astype(MXU_DTYPE) for n in range(len(items))]
        dp = [_mm_nt(tb[at(b, h)], bands[b][2 + h // 4]) for b, h in items]
        ds = [p[n] * (dp[n] - delta[n]) for n in range(len(items))]
        for h in range(N_HEADS):
            ds_h, dsk_h = ds[at(0, h)], esink[at(0, h)] * delta[at(0, h)]
            for b in range(1, nq):
                ds_h = ds_h + ds[at(b, h)]
                dsk_h = dsk_h + esink[at(b, h)] * delta[at(b, h)]
            dbias_ref[h] += ds_h
            dsk_ref[h] -= dsk_h
        dsb = [ds[n].astype(MXU_DTYPE) for n in range(len(items))]
        pb = [p[n].astype(MXU_DTYPE) for n in range(len(items))]
        dq_heads = [_mm(dsb[at(b, h)], bands[b][h // 4]) * QK_SCALE for b, h in items]
        grp = lambda lst, b, g: jnp.concatenate(lst[at(b, 4 * g):at(b, 4 * g) + 4], axis=0)
        dk_pads = [[_mm_tn(grp(dsb, b, g), grp(qs, b, g)) for g in range(2)] for b in range(nq)]
        dv_pads = [[_mm_tn(grp(pb, b, g), grp(tb, b, g)) for g in range(2)] for b in range(nq)]
        dq_ref[...] = jnp.concatenate([_join_heads(dq_heads[b * N_HEADS:(b + 1) * N_HEADS]) for b in range(nq)],
                                      axis=0).astype(dq_ref.dtype)
        part = lambda b, lo: jnp.concatenate(
            [_join_pair(d[b][0][lo:lo + BLK], d[b][1][lo:lo + BLK]) for d in (dk_pads, dv_pads)], axis=1)
        dkv = [part(b, BLK) + (part(b + 1, 0) if b + 1 < nq else carry_ref[...]) for b in range(nq)]
        dkv_ref[...] = jnp.concatenate(dkv, axis=0).astype(dkv_ref.dtype)
        carry_ref[...] = part(0, 0)

    smem = pl.BlockSpec(memory_space=pltpu.SMEM)
    rev = lambda i: (n_steps - 1 - i, 0)
    return _hosted_call(
        body, "attn_bwd", n_steps,
        in_specs=[pl.BlockSpec((rows, ATTN_W), rev),
                  pl.BlockSpec((BLK, 2 * KV_W), lambda i: (jnp.maximum((n_steps - 1 - i) * nq - 1, 0), 2)),
                  pl.BlockSpec((rows, 2 * KV_W), lambda i: (n_steps - 1 - i, 2)),
                  pl.BlockSpec((rows, ATTN_W), rev), pl.BlockSpec((rows, ATTN_W), rev),
                  pl.BlockSpec((2, N_HEADS, BLK, 2 * BLK), lambda i: (0, 0, 0, 0)), smem],
        out_specs=[pl.BlockSpec((rows, ATTN_W), rev), pl.BlockSpec((rows, 2 * KV_W), rev),
                   pl.BlockSpec((N_HEADS, BLK, 2 * BLK), lambda i: (0, 0, 0)),
                   pl.BlockSpec((N_HEADS, BLK, 1), lambda i: (0, 0, 0))],
        out_shape=[jax.ShapeDtypeStruct((S, ATTN_W), MXU_DTYPE), jax.ShapeDtypeStruct((S, 2 * KV_W), MXU_DTYPE),
                   jax.ShapeDtypeStruct((N_HEADS, BLK, 2 * BLK), F32), jax.ShapeDtypeStruct((N_HEADS, BLK, 1), F32)],
        scratch_shapes=[pltpu.VMEM((BLK, 2 * KV_W), F32)],
        args=(qkv, qkv, qkv, y, dy, bias, sinks), xchg=xchg, cparams=_cparams(),
    )


def _attn_finish(dbias, dsk, buckets):
    def body(db_ref, dsk_ref, bk_ref, drel_ref, dsink_ref):
        bk = bk_ref[...]
        r = lax.broadcasted_iota(jnp.int32, (N_BUCKETS, LANE), 0)
        l = lax.broadcasted_iota(jnp.int32, (N_BUCKETS, LANE), 1)
        row = lax.broadcasted_iota(jnp.int32, (N_HEADS, LANE), 0)
        res = jnp.zeros((N_BUCKETS, LANE), F32)
        dsink = jnp.zeros((N_HEADS, LANE), F32)
        for h in range(N_HEADS):
            db = db_ref[h]
            for b in range(N_BUCKETS):
                v = jnp.sum(jnp.sum(jnp.where(bk == b, db, 0.0), axis=1, keepdims=True), axis=0, keepdims=True)
                res = res + jnp.where((r == b) & (l == h), v, 0.0)
            dsink = dsink + jnp.where(row == h, jnp.sum(dsk_ref[h], axis=0, keepdims=True), 0.0)
        drel_ref[...] = res
        dsink_ref[...] = dsink

    return pl.pallas_call(body, name="attn_finish",
                          out_shape=[jax.ShapeDtypeStruct((N_BUCKETS, LANE), F32),
                                     jax.ShapeDtypeStruct((N_HEADS, LANE), F32)])(dbias, dsk, buckets)


def _ssd_consts():
    r = lax.broadcasted_iota(jnp.int32, (BLK, BLK), 0)
    c = lax.broadcasted_iota(jnp.int32, (BLK, BLK), 1)
    causal = c <= r
    upper = (r <= c).astype(F32)
    last = r == BLK - 1
    head = lax.broadcasted_iota(jnp.int32, (N_HEADS, BLK), 0)
    return causal, upper, last, head


def _ssd_chunks(xs, bg, cg, dt_raw_t, prev0, dtb, alog, d_rows, consts):
    causal, upper, last, head = consts
    nq = len(xs)
    items = [(c, h) for c in range(nq) for h in range(N_HEADS)]
    at = lambda c, h: c * N_HEADS + h
    a_neg = -jnp.exp(alog)
    dt_t = [_softplus(dt_raw_t[c] + dtb) for c in range(nq)]
    acs_t = [_mm_hi(dt_t[c] * a_neg, upper) for c in range(nq)]
    cb = [[_mm_nt(cg[c][g], bg[c][g]) for g in range(2)] for c in range(nq)]
    pick = lambda t, h: jnp.sum(jnp.where(head == h, t, 0.0), axis=0, keepdims=True)
    dt_row = [pick(dt_t[c], h) for c, h in items]
    a_row = [pick(acs_t[c], h) for c, h in items]
    a_rb = [jnp.broadcast_to(a_row[n], (BLK, BLK)) for n in range(len(items))]
    a_b = [a_rb[n].T for n in range(len(items))]
    a_last = [jnp.sum(jnp.where(last, a_b[n], 0.0), axis=0, keepdims=True) for n in range(len(items))]
    w = [cb[c][h // 4] * jnp.exp(jnp.where(causal, a_b[at(c, h)] - a_rb[at(c, h)], -1e30)) * dt_row[at(c, h)]
         for c, h in items]
    f_b = [jnp.broadcast_to(dt_row[n] * jnp.exp(a_last[n] - a_row[n]), (BLK, BLK)).T for n in range(len(items))]
    y_in = [_mm(w[at(c, h)], xs[c][h]) for c, h in items]
    st = [_mm_tn(bg[c][h // 4], xs[c][h] * f_b[at(c, h)]) for c, h in items]
    e_b = [jnp.exp(a_b[n]) for n in range(len(items))]
    states = [list(prev0)]
    for c in range(nq):
        states.append([states[c][h] * jnp.exp(a_last[at(c, h)]) + st[at(c, h)] for h in range(N_HEADS)])
    y_off = [_mm(cg[c][h // 4], states[c][h]) * e_b[at(c, h)] for c, h in items]
    ys = [[y_in[at(c, h)] + y_off[at(c, h)] + d_rows[h] * xs[c][h] for h in range(N_HEADS)] for c in range(nq)]
    return ys, states


def _ssd_chunks_bwd(xs, bg, cg, dt_raw_t, prev, dtb, alog, d_rows, dys, dh_last, consts):
    causal, upper, last, head = consts
    nq = len(xs)
    items = [(c, h) for c in range(nq) for h in range(N_HEADS)]
    ni = len(items)
    at = lambda c, h: c * N_HEADS + h
    groups = [(c, g) for c in range(nq) for g in range(2)]
    lane = _lane_iota((BLK, BLK))
    lane_row = _lane_iota((1, BLK))
    a_neg = -jnp.exp(alog)
    pre_dt = [dt_raw_t[c] + dtb for c in range(nq)]
    dt_t = [_softplus(pre_dt[c]) for c in range(nq)]
    acs_t = [_mm_hi(dt_t[c] * a_neg, upper) for c in range(nq)]
    pick = lambda t, h: jnp.sum(jnp.where(head == h, t, 0.0), axis=0, keepdims=True)
    full_sum = lambda t: jnp.sum(jnp.sum(t, axis=1, keepdims=True), axis=0, keepdims=True)
    dt_row = [pick(dt_t[c], h) for c, h in items]
    a_row = [pick(acs_t[c], h) for c, h in items]
    a_rb = [jnp.broadcast_to(a_row[n], (BLK, BLK)) for n in range(ni)]
    a_b = [a_rb[n].T for n in range(ni)]
    a_last = [jnp.sum(jnp.where(last, a_b[n], 0.0), axis=0, keepdims=True) for n in range(ni)]
    lm = [jnp.exp(jnp.where(causal, a_b[n] - a_rb[n], -1e30)) for n in range(ni)]
    cgb = [[cg[c][g].astype(MXU_DTYPE) for g in range(2)] for c in range(nq)]
    bgb = [[bg[c][g].astype(MXU_DTYPE) for g in range(2)] for c in range(nq)]
    cb = [[_mm_nt(cgb[c][g], bgb[c][g]) for g in range(2)] for c in range(nq)]
    u = [cb[c][h // 4] * lm[at(c, h)] for c, h in items]
    w = [(u[n] * dt_row[n]).astype(MXU_DTYPE) for n in range(ni)]
    e_row = [jnp.exp(a_last[n] - a_row[n]) for n in range(ni)]
    f_row = [dt_row[n] * e_row[n] for n in range(ni)]
    f_b = [jnp.broadcast_to(f_row[n], (BLK, BLK)).T for n in range(ni)]
    e_b = [jnp.exp(a_b[n]) for n in range(ni)]
    el = [jnp.exp(a_last[n]) for n in range(ni)]
    xb = [xs[c][h].astype(MXU_DTYPE) for c, h in items]
    dyb = [dys[c][h].astype(MXU_DTYPE) for c, h in items]
    prevb = [prev[c][h].astype(MXU_DTYPE) for c, h in items]
    gmat = [_mm(cgb[c][h // 4], prevb[at(c, h)]) for c, h in items]
    dw = [_mm_nt(dyb[n], xb[n]) for n in range(ni)]
    dg = [dys[c][h] * e_b[at(c, h)] for c, h in items]
    dgb = [dg[n].astype(MXU_DTYPE) for n in range(ni)]
    from_y = [_mm_tn(cgb[c][h // 4], dgb[at(c, h)]) for c, h in items]
    dhs = [None] * ni
    dprev = [None] * ni
    for c in reversed(range(nq)):
        for h in range(N_HEADS):
            dhs[at(c, h)] = dh_last[h] if c == nq - 1 else dprev[at(c + 1, h)]
            dprev[at(c, h)] = from_y[at(c, h)] + dhs[at(c, h)] * el[at(c, h)]
    dstb = [dhs[n].astype(MXU_DTYPE) for n in range(ni)]
    dxf = [_mm(bgb[c][h // 4], dstb[at(c, h)]) for c, h in items]
    xfb = [(xs[c][h] * f_b[at(c, h)]).astype(MXU_DTYPE) for c, h in items]
    dxs = [_mm_tn(w[at(c, h)], dyb[at(c, h)]) + d_rows[h] * dys[c][h] + f_b[at(c, h)] * dxf[at(c, h)]
           for c, h in items]
    dd_item = [jnp.sum(dys[c][h] * xs[c][h], axis=0, keepdims=True) for c, h in items]
    dcg_h = [_mm_nt(dgb[n], prevb[n]) for n in range(ni)]
    dbg_h = [_mm_nt(xfb[n], dstb[n]) for n in range(ni)]
    zt = [dw[n] * u[n] for n in range(ni)]
    dseg = [zt[n] * dt_row[n] for n in range(ni)]
    dcb_h = [dw[n] * lm[n] * dt_row[n] for n in range(ni)]
    four = lambda lst, c, g: lst[at(c, 4 * g)] + lst[at(c, 4 * g + 1)] + lst[at(c, 4 * g + 2)] + lst[at(c, 4 * g + 3)]
    dcb = {(c, g): four(dcb_h, c, g).astype(MXU_DTYPE) for c, g in groups}
    dcg = [[four(dcg_h, c, g) + _mm(dcb[c, g], bgb[c][g]) for g in range(2)] for c in range(nq)]
    dbg = [[four(dbg_h, c, g) + _mm_tn(dcb[c, g], cgb[c][g]) for g in range(2)] for c in range(nq)]
    r1 = [jnp.sum(dg[n] * gmat[n] + dseg[n], axis=1, keepdims=True) for n in range(ni)]
    r2 = [jnp.sum(dxf[at(c, h)] * xs[c][h], axis=1, keepdims=True) for c, h in items]
    tt = [jnp.where(lane < HALF, jnp.broadcast_to(r1[n], (BLK, BLK)), jnp.broadcast_to(r2[n], (BLK, BLK))).T
          for n in range(ni)]
    r1_row = [tt[n][0:1, :] for n in range(ni)]
    r2_row = [tt[n][HALF:HALF + 1, :] for n in range(ni)]
    d_el = [full_sum(dhs[at(c, h)] * prev[c][h]) for c, h in items]
    da_last = [jnp.sum(r2_row[n] * f_row[n], axis=1, keepdims=True) + el[n] * d_el[n] for n in range(ni)]
    da_row = [r1_row[n] - jnp.sum(dseg[n], axis=0, keepdims=True) - r2_row[n] * f_row[n]
              + jnp.where(lane_row == BLK - 1, da_last[n], 0.0) for n in range(ni)]
    ddt_row = [jnp.sum(zt[n], axis=0, keepdims=True) + r2_row[n] * e_row[n] for n in range(ni)]
    draw, dalog = [], jnp.zeros((N_HEADS, BLK), F32)
    for c in range(nq):
        da_t = jnp.zeros((N_HEADS, BLK), F32)
        ddt_t = jnp.zeros((N_HEADS, BLK), F32)
        for h in range(N_HEADS):
            da_t = jnp.where(head == h, da_row[at(c, h)], da_t)
            ddt_t = jnp.where(head == h, ddt_row[at(c, h)], ddt_t)
        d_dta = _mm_hi(da_t, causal.astype(F32))
        dalog = dalog + d_dta * dt_t[c] * a_neg
        draw.append((ddt_t + d_dta * a_neg) * jax.nn.sigmoid(pre_dt[c]))
    ddtb = draw[0]
    for c in range(1, nq):
        ddtb = ddtb + draw[c]
    dd_rows = []
    for h in range(N_HEADS):
        t = dd_item[at(0, h)]
        for c in range(1, nq):
            t = t + dd_item[at(c, h)]
        dd_rows.append(t)
    return ([dxs[c * N_HEADS:(c + 1) * N_HEADS] for c in range(nq)], dbg, dcg, draw,
            [dprev[at(0, h)] for h in range(N_HEADS)], ddtb, dalog, dd_rows)


def _dt_rows(dt_blk):
    return dt_blk.T[:N_HEADS]


def _conv_pre(halo, blk, cw_ref, cb_ref):
    ext = jnp.concatenate([halo, blk], axis=0)
    taps = [pltpu.roll(ext, 3 - k, 0)[8:] for k in range(3)] + [blk]
    pre = cb_ref[...] + cw_ref[0:1, :] * taps[0]
    for k in range(1, 4):
        pre = pre + cw_ref[k:k + 1, :] * taps[k]
    return pre


def _ssd_split(pre):
    heads = _split_heads(pre[:, :SSM_W], 4)
    pb = [pre[:, SSM_W + g * D_STATE:SSM_W + (g + 1) * D_STATE] for g in range(2)]
    pc = [pre[:, SSM_W + 2 * D_STATE + g * D_STATE:SSM_W + 2 * D_STATE + (g + 1) * D_STATE] for g in range(2)]
    return heads, pb, pc


def _ssd_fwd(xbc, dt_raw, conv_w, conv_b, dtb_row, alog_row, d_exp, xchg):
    S = xbc.shape[0]
    nc = S // BLK
    nq = SSD_CHUNKS_PER_STEP if nc % SSD_CHUNKS_PER_STEP == 0 else 1
    rows = nq * BLK

    def body(xbc_ref, halo_ref, dt_ref, cw_ref, cb_ref, dtb_ref, alog_ref, d_ref, y_ref, prev_ref, pre_ref, state_ref):
        i = pl.program_id(0)

        @pl.when(i == 0)
        def _():
            state_ref[...] = jnp.zeros_like(state_ref)

        halo = halo_ref[...] * jnp.where(i > 0, 1.0, 0.0)
        pre = _conv_pre(halo, xbc_ref[...], cw_ref, cb_ref)
        pre_ref[...] = pre
        xc = _silu(pre)
        split = [_ssd_split(xc[c * BLK:(c + 1) * BLK]) for c in range(nq)]
        dt_t = [_dt_rows(dt_ref[c * BLK:(c + 1) * BLK, :]) for c in range(nq)]
        prev0 = [state_ref[h] for h in range(N_HEADS)]
        d_rows = [d_ref[h:h + 1, :] for h in range(N_HEADS)]
        ys, states = _ssd_chunks([s[0] for s in split], [s[1] for s in split], [s[2] for s in split], dt_t, prev0,
                                 dtb_ref[...], alog_ref[...], d_rows, _ssd_consts())
        for h in range(N_HEADS):
            for c in range(nq):
                prev_ref[c, h] = states[c][h]
            state_ref[h] = states[nq][h]
        y_ref[...] = jnp.concatenate([_join_heads(ys[c]) for c in range(nq)], axis=0)

    vec = pl.BlockSpec((N_HEADS, LANE), _fixed)
    return _hosted_call(
        body, "ssd_fwd", nc // nq,
        in_specs=[pl.BlockSpec((rows, XBC_W), _row),
                  pl.BlockSpec((8, XBC_W), lambda i: (jnp.maximum(i * (rows // 8) - 1, 0), 0)),
                  pl.BlockSpec((rows, LANE), _row),
                  pl.BlockSpec((4, XBC_W), _fixed), pl.BlockSpec((1, XBC_W), _fixed), vec, vec,
                  pl.BlockSpec((N_HEADS, LANE), _fixed)],
        out_specs=[pl.BlockSpec((rows, SSM_W), _row),
                   pl.BlockSpec((nq, N_HEADS, D_STATE, LANE), lambda i: (i, 0, 0, 0)),
                   pl.BlockSpec((rows, XBC_W), _row)],
        out_shape=[jax.ShapeDtypeStruct((S, SSM_W), F32), jax.ShapeDtypeStruct((nc, N_HEADS, D_STATE, LANE), F32),
                   jax.ShapeDtypeStruct((S, XBC_W), F32)],
        scratch_shapes=[pltpu.VMEM((N_HEADS, D_STATE, LANE), F32)],
        args=(xbc, xbc, dt_raw, conv_w, conv_b, dtb_row, alog_row, d_exp), xchg=xchg, cparams=_cparams(),
    )


def _ssd_bwd(xbc, pre_act, dt_raw, prev_states, dy, conv_w, dtb_row, alog_row, d_exp, xchg):
    S = xbc.shape[0]
    nc = S // BLK
    nq = SSD_BWD_CHUNKS_PER_STEP if nc % SSD_BWD_CHUNKS_PER_STEP == 0 else 1
    rows, n_steps = nq * BLK, nc // nq

    def body(xbc_ref, halo_ref, pre_ref, dt_ref, prev_ref, dy_ref, cw_ref, dtb_ref, alog_ref, d_ref,
             dxbc_ref, ddt_ref, dcw_ref, dvec_ref, dd_ref, gstate_ref, ghalo_ref):
        i = pl.program_id(0)

        @pl.when(i == 0)
        def _():
            gstate_ref[...] = jnp.zeros_like(gstate_ref)
            ghalo_ref[...] = jnp.zeros_like(ghalo_ref)
            dcw_ref[...] = jnp.zeros_like(dcw_ref)
            dvec_ref[...] = jnp.zeros_like(dvec_ref)
            dd_ref[...] = jnp.zeros_like(dd_ref)

        halo = halo_ref[...] * jnp.where(i < n_steps - 1, 1.0, 0.0)
        ext = jnp.concatenate([halo, xbc_ref[...]], axis=0)
        pre = pre_ref[...]
        sig = jax.nn.sigmoid(pre)
        xc = pre * sig
        split = [_ssd_split(xc[c * BLK:(c + 1) * BLK]) for c in range(nq)]
        dt_t = [_dt_rows(dt_ref[c * BLK:(c + 1) * BLK, :]) for c in range(nq)]
        prev = [[prev_ref[c, h] for h in range(N_HEADS)] for c in range(nq)]
        d_rows = [d_ref[h:h + 1, :] for h in range(N_HEADS)]
        dys = [_split_heads(dy_ref[c * BLK:(c + 1) * BLK, :], 4) for c in range(nq)]
        dh_last = [gstate_ref[h] for h in range(N_HEADS)]
        dheads, dpb, dpc, ddt_t, dprev0, ddtb, dalog, dd_rows = _ssd_chunks_bwd(
            [s[0] for s in split], [s[1] for s in split], [s[2] for s in split], dt_t, prev, dtb_ref[...],
            alog_ref[...], d_rows, dys, dh_last, _ssd_consts())
        for h in range(N_HEADS):
            gstate_ref[h] = dprev0[h]
            dd_ref[h:h + 1, :] += dd_rows[h]
        pad = jnp.zeros((BLK - N_HEADS, BLK), F32)
        ddt_ref[...] = jnp.concatenate([jnp.concatenate([ddt_t[c], pad], axis=0).T for c in range(nq)],
                                       axis=0).astype(ddt_ref.dtype)
        dvec_ref[0:N_HEADS, :] += ddtb
        dvec_ref[N_HEADS:, :] += dalog
        dxc = jnp.concatenate([jnp.concatenate([_join_heads(dheads[c])] + list(dpb[c]) + list(dpc[c]), axis=1)
                               for c in range(nq)], axis=0)
        dpre = dxc * (sig * (1.0 + pre * (1.0 - sig)))
        zeros8 = jnp.zeros((8, XBC_W), F32)
        dpe = jnp.concatenate([zeros8, dpre, zeros8], axis=0)
        n_ext = 16 + rows
        shifted = [pltpu.roll(dpe, n_ext - (3 - k), 0)[:8 + rows] for k in range(3)] + [dpe[:8 + rows]]
        dext = cw_ref[0:1, :] * shifted[0]
        for k in range(1, 4):
            dext = dext + cw_ref[k:k + 1, :] * shifted[k]
        for k in range(4):
            dcw_ref[k:k + 1, :] += jnp.sum(shifted[k] * ext, axis=0, keepdims=True)
        dcw_ref[4:5, :] += jnp.sum(dpre, axis=0, keepdims=True)
        dxbc_ref[...] = jnp.concatenate([dext[8:rows], dext[rows:] + ghalo_ref[...]], axis=0).astype(dxbc_ref.dtype)
        ghalo_ref[...] = dext[:8, :]

    vec = pl.BlockSpec((N_HEADS, LANE), _fixed)
    rev = lambda i: (n_steps - 1 - i, 0)
    return _hosted_call(
        body, "ssd_bwd", n_steps,
        in_specs=[pl.BlockSpec((rows, XBC_W), rev),
                  pl.BlockSpec((8, XBC_W), lambda i: (jnp.maximum((n_steps - 1 - i) * (rows // 8) - 1, 0), 0)),
                  pl.BlockSpec((rows, XBC_W), rev),
                  pl.BlockSpec((rows, LANE), rev),
                  pl.BlockSpec((nq, N_HEADS, D_STATE, LANE), lambda i: (n_steps - 1 - i, 0, 0, 0)),
                  pl.BlockSpec((rows, SSM_W), rev),
                  pl.BlockSpec((4, XBC_W), _fixed), vec, vec,
                  pl.BlockSpec((N_HEADS, LANE), _fixed)],
        out_specs=[pl.BlockSpec((rows, XBC_W), rev), pl.BlockSpec((rows, LANE), rev),
                   pl.BlockSpec((8, XBC_W), _fixed), pl.BlockSpec((2 * N_HEADS, LANE), _fixed),
                   pl.BlockSpec((N_HEADS, LANE), _fixed)],
        out_shape=[jax.ShapeDtypeStruct((S, XBC_W), MXU_DTYPE), jax.ShapeDtypeStruct((S, LANE), MXU_DTYPE),
                   jax.ShapeDtypeStruct((8, XBC_W), F32), jax.ShapeDtypeStruct((2 * N_HEADS, LANE), F32),
                   jax.ShapeDtypeStruct((N_HEADS, LANE), F32)],
        scratch_shapes=[pltpu.VMEM((N_HEADS, D_STATE, LANE), F32), pltpu.VMEM((8, XBC_W), F32)],
        args=(xbc, xbc, pre_act, dt_raw, prev_states, dy, conv_w, dtb_row, alog_row, d_exp), xchg=xchg,
        cparams=_cparams(VMEM_BIG),
    )


def _adamw_math(w, g, m, v):
    m = ADAM_B1 * m + (1.0 - ADAM_B1) * g
    v = ADAM_B2 * v + (1.0 - ADAM_B2) * jnp.square(g)
    m_hat = m / (1.0 - ADAM_B1 ** ADAM_STEP)
    v_hat = v / (1.0 - ADAM_B2 ** ADAM_STEP)
    delta = -ADAM_LR * (m_hat / (jnp.sqrt(v_hat) + ADAM_EPS) + ADAM_WD * w)
    return delta, m, v


def _reduce_adamw_halves(part_a, part_b, w, m, v, name):
    R, C = w.shape
    P = part_a.shape[0]
    tl = 256
    n = C // tl

    def body(a_ref, b_ref, w_ref, m_ref, v_ref, g_ref, d_ref, nm_ref, nv_ref):
        ga, gb = a_ref[0].astype(F32), b_ref[0].astype(F32)
        for i in range(1, P):
            ga, gb = ga + a_ref[i].astype(F32), gb + b_ref[i].astype(F32)
        first = jnp.where(pl.program_id(0) < n // 2, 1.0, 0.0)
        g = ga * first + gb * (1.0 - first)
        d, nm, nv = _adamw_math(w_ref[...], g, m_ref[...], v_ref[...])
        g_ref[...] = g
        d_ref[...] = d
        nm_ref[...] = nm
        nv_ref[...] = nv

    blk = pl.BlockSpec((R, tl), lambda i: (0, i))
    return pl.pallas_call(
        body, name=name, grid=(n,),
        in_specs=[pl.BlockSpec((P, R, tl), lambda i: (0, 0, jnp.minimum(i, n // 2 - 1))),
                  pl.BlockSpec((P, R, tl), lambda i: (0, 0, jnp.maximum(i - n // 2, 0))), blk, blk, blk],
        out_specs=[blk] * 4, out_shape=[jax.ShapeDtypeStruct((R, C), F32)] * 4,
    )(part_a, part_b, w, m, v)


def _reduce_adamw_hosting(parts_list, wmv_list, name, xchg):
    n_arr = len(parts_list)
    pieces = [list(p) if isinstance(p, (tuple, list)) else [p] for p in parts_list]
    n_pieces = sum(len(p) for p in pieces)
    C = wmv_list[0][0].shape[1]
    tl = 256

    def total(ref):
        g = ref[0].astype(F32)
        for i in range(1, N_DEV):
            g = g + ref[i].astype(F32)
        return g

    def body(*refs):
        p_refs, wmv_refs, o_refs = refs[:n_pieces], refs[n_pieces:n_pieces + 3 * n_arr], refs[n_pieces + 3 * n_arr:]
        at = 0
        for k in range(n_arr):
            sums = [total(r) for r in p_refs[at:at + len(pieces[k])]]
            at += len(pieces[k])
            g = sums[0] if len(sums) == 1 else jnp.concatenate(sums, axis=0)
            w_ref, m_ref, v_ref = wmv_refs[3 * k:3 * k + 3]
            d, nm, nv = _adamw_math(w_ref[...], g, m_ref[...], v_ref[...])
            for o, val in zip(o_refs[4 * k:4 * k + 4], (g, d, nm, nv)):
                o[...] = val

    in_specs = [pl.BlockSpec((N_DEV, p.shape[1], tl), lambda i: (0, 0, i)) for group in pieces for p in group]
    in_specs += [pl.BlockSpec((w.shape[0], tl), lambda i: (0, i)) for w, _, _ in wmv_list for _ in range(3)]
    out_specs = [pl.BlockSpec((w.shape[0], tl), lambda i: (0, i)) for w, _, _ in wmv_list for _ in range(4)]
    out_shape = [jax.ShapeDtypeStruct(w.shape, F32) for w, _, _ in wmv_list for _ in range(4)]
    args = [p for group in pieces for p in group] + [a for wmv in wmv_list for a in wmv]
    outs, x_out = _hosted_call(body, name, C // tl, in_specs, out_specs, out_shape, [], args, xchg,
                               _cparams(VMEM_BIG))
    return [outs[4 * k:4 * k + 4] for k in range(n_arr)], x_out


_SMALL_NAMES = ("ada_b", "norm1", "conv_w", "conv_b", "dt_bias", "A_log", "D_skip", "sinks", "attn_out_norm",
                "ssm_out_norm", "norm2", "rel_bias", "final_norm")
N_MOD = 6 * D_MODEL


def _mod_row(a0, a1, a2):
    return jnp.concatenate([a0[2:3], a0[1:2], a1[0:1], a2[2:3], a2[1:2], a2[3:4]], axis=1)


def _small_update(gathered, params):
    n_g = len(gathered)
    flat = [a for name in _SMALL_NAMES for a in params[name]]

    def body(*refs):
        a0_ref, a1_ref, a2_ref, cw_ref, dv_ref, dd_ref, ds_ref, dr_ref, c_ref = refs[:n_g]
        wmv = refs[n_g:n_g + len(flat)]
        outs = refs[n_g + len(flat):]

        def total(ref):
            t = ref[0]
            for i in range(1, N_DEV):
                t = t + ref[i]
            return t

        t0, t1, t2, tcw, tdv, tdd, tds, tdr = [total(r) for r in (a0_ref, a1_ref, a2_ref, cw_ref, dv_ref, dd_ref,
                                                                   ds_ref, dr_ref)]
        r8 = lax.broadcasted_iota(jnp.int32, (N_HEADS, LANE), 0)
        l8 = lax.broadcasted_iota(jnp.int32, (N_HEADS, LANE), 1)

        def diag_row(t):
            return jnp.sum(jnp.where(r8 == l8, t, 0.0), axis=0, keepdims=True)[:, :N_HEADS]

        def lane_sums(t):
            return diag_row(jnp.broadcast_to(jnp.sum(t, axis=1, keepdims=True), (N_HEADS, LANE)))

        me = _lin(_my_pos())
        n_cw = XBC_W // N_DEV
        cw_mine = jnp.zeros((4, n_cw), F32)
        for j in range(N_DEV):
            cw_mine = cw_mine + tcw[0:4, j * n_cw:(j + 1) * n_cw] * jnp.where(me == j, 1.0, 0.0)
        grads = {
            "ada_b": _mod_row(t0, t1, t2), "norm1": t0[0:1], "conv_w": cw_mine, "conv_b": tcw[4:5],
            "dt_bias": lane_sums(tdv[:N_HEADS]), "A_log": lane_sums(tdv[N_HEADS:]), "D_skip": lane_sums(tdd),
            "sinks": diag_row(tds), "attn_out_norm": t1[1:2, :ATTN_W], "ssm_out_norm": t1[1:2, ATTN_W:],
            "norm2": t2[0:1], "rel_bias": tdr[:, :N_HEADS], "final_norm": t2[4:5],
        }
        for k, name in enumerate(_SMALL_NAMES):
            w_ref, m_ref, v_ref = wmv[3 * k:3 * k + 3]
            g = grads[name]
            d, nm, nv = _adamw_math(w_ref[...], g, m_ref[...], v_ref[...])
            for o, val in zip(outs[4 * k:4 * k + 4], (g, d, nm, nv)):
                o[...] = val
        loss_ref, call_ref, dmod_ref = outs[4 * len(_SMALL_NAMES):]
        loss_ref[...] = t2[5:6, 0:1]
        call_ref[...] = jnp.concatenate([c_ref[i] for i in range(N_DEV)], axis=0)
        dmod_ref[...] = jnp.concatenate([_mod_row(a0_ref[i], a1_ref[i], a2_ref[i]) for i in range(N_DEV)], axis=0)

    out_shape = [jax.ShapeDtypeStruct(params[name][0].shape, F32) for name in _SMALL_NAMES for _ in range(4)]
    out_shape += [jax.ShapeDtypeStruct((1, 1), F32), jax.ShapeDtypeStruct((N_DEV, D_MODEL), F32),
                  jax.ShapeDtypeStruct((N_DEV, N_MOD), F32)]
    res = pl.pallas_call(body, name="small_update", out_shape=out_shape)(*gathered, *flat)
    upd = {name: res[4 * k:4 * k + 4] for k, name in enumerate(_SMALL_NAMES)}
    loss, c_all, dmod_all = res[4 * len(_SMALL_NAMES):]
    return upd, loss, c_all, dmod_all


def _ada_w_update(c_all, dmod_all, w, m, v):
    chunk = w.shape[1]

    def body(c_ref, dm_ref, w_ref, m_ref, v_ref, g_ref, d_ref, nm_ref, nv_ref):
        me = _lin(_my_pos())
        dm = jnp.zeros((N_DEV, chunk), F32)
        for j in range(N_DEV):
            dm = dm + dm_ref[:, j * chunk:(j + 1) * chunk] * jnp.where(me == j, 1.0, 0.0)
        g = lax.dot_general(_silu(c_ref[...]), dm, (((0,), (0,)), ((), ())), precision=HI,
                            preferred_element_type=F32)
        d, nm, nv = _adamw_math(w_ref[...], g, m_ref[...], v_ref[...])
        g_ref[...] = g
        d_ref[...] = d
        nm_ref[...] = nm
        nv_ref[...] = nv

    tr = 256
    blk = pl.BlockSpec((tr, chunk), _row)
    return pl.pallas_call(
        body, name="ada_w_update", grid=(w.shape[0] // tr,),
        in_specs=[pl.BlockSpec((N_DEV, tr), lambda i: (0, i)), pl.BlockSpec(dmod_all.shape, _fixed), blk, blk, blk],
        out_specs=[blk] * 4, out_shape=[jax.ShapeDtypeStruct(w.shape, F32)] * 4,
    )(c_all, dmod_all, w, m, v)


def _local_step(x, tgt, c, mod, w_in, conv_w, w_o_mine, w_gu_mine, w_d_mine, p):
    S = x.shape[0]
    tm = min(512, S)
    tmm = min(256, S)
    tw = min(2048, S)
    shift1, scale1, gate1, shift2, scale2, gate2 = [mod[i:i + 1] for i in range(6)]
    buckets = jnp.asarray(_t5_bucket_table())
    per_head = lambda a: jnp.broadcast_to(a.reshape(N_HEADS, 1), (N_HEADS, LANE))
    dtb_row, alog_row, d_exp = per_head(p["dt_bias"]), per_head(p["A_log"]), per_head(p["D_skip"])
    sinks = p["sinks"].reshape(N_HEADS)

    d_cut, gu_cut = WD_CUT, WGU_CUTS
    n_d, n_gu = w_d_mine.shape[0], w_gu_mine.shape[0]
    (qkv, z, xbc, dt_raw), (g_d_a,) = _in_proj_fwd(x, p["norm1"], scale1, shift1, w_in, tm,
                                                   ([(w_d_mine, 0, d_cut)], "two-level"))
    bias = _attn_bias(buckets, p["rel_bias"])
    (ya,), (g_gu_a,) = _attn_fwd(qkv, bias, sinks, ([(w_gu_mine, 0, gu_cut[0])], "two-level"))
    (ys, prev_states, pre_act), (g_gu_b, g_o) = _ssd_fwd(
        xbc, dt_raw, conv_w, p["conv_b"], dtb_row, alog_row, d_exp,
        ([(w_gu_mine, gu_cut[0], gu_cut[1] - gu_cut[0]), w_o_mine], "two-level"))
    w_o = g_o.reshape(D_MODEL, D_MODEL)
    x1, (g_gu_c, g_d_b) = _out_proj_fwd(
        x, ya, ys, z, p["attn_out_norm"], p["ssm_out_norm"], gate1, w_o, tm,
        ([(w_gu_mine, gu_cut[1], n_gu - gu_cut[1]), (w_d_mine, d_cut, n_d - d_cut)], "two-level"))
    dx1, h2, dgu, act, dmlp, acc2 = _mlp_loss(x1, tgt, p["norm2"], scale2, shift2, gate2, p["final_norm"],
                                              (g_gu_a, g_gu_b, g_gu_c), (g_d_a, g_d_b), tmm)
    g_w_gu = _wgrad(dgu, h2, 2 * D_FF // 4, tw, "wgrad_gate_up")
    g_w_d = _wgrad(act, dmlp, D_FF // 2, tw, "wgrad_down")
    gu_slots = g_w_gu.reshape(N_DEV, 2 * D_FF // N_DEV, D_MODEL)
    (dya, dys, dz, g_w_o, acc1), (r_gu_a,) = _out_proj_bwd(
        dx1, ya, ys, z, p["attn_out_norm"], p["ssm_out_norm"], gate1, w_o, tm, ([gu_slots], ("rows", 0, GGU_CUT)))
    (dq, dkv, dbias, dsk), (r_d, r_o) = _attn_bwd(
        qkv, ya, dya, bias, sinks,
        ([g_w_d.reshape(N_DEV, D_FF // N_DEV, D_MODEL), g_w_o.reshape(N_DEV, D_MODEL // N_DEV, D_MODEL)], True))
    drel, dsink = _attn_finish(dbias, dsk, buckets)
    (dxbc, ddt, dcw, dvec, dd), (r_gu_b, *early) = _ssd_bwd(
        xbc, pre_act, dt_raw, prev_states, dys, conv_w, dtb_row, alog_row, d_exp,
        [([gu_slots], ("rows", GGU_CUT, 2 * D_FF // N_DEV - GGU_CUT)), ([acc1, acc2, dsink, drel, c], False)])
    r_gu = (r_gu_a, r_gu_b)
    gx, h1, acc0, g_in_a = _in_proj_bwd(x, dx1, dq, dkv, dz, dxbc, ddt, p["norm1"], scale1, shift1, w_in, tm)
    half = D_MODEL // 2
    slots = lambda g: g[:IN_W].reshape(N_DEV, IN_W // N_DEV, half)
    g_in_b, (r_in_a, *late) = _wgrad((dq, dkv, dz, dxbc, ddt), h1, IN_PAD, tw, "wgrad_in_b",
                                     [([slots(g_in_a)], "two-level scatter"), ([acc0, dcw, dvec, dd], False)],
                                     g_cols=(half, 1))
    gathered = (late[0], early[0], early[1], late[1], late[2], late[3], early[2], early[3], early[4])
    return gx, (r_in_a, slots(g_in_b)), (r_o, r_gu, r_d), gathered


def kernel(x, c, ada_w, ada_b, norm1, w_in, conv_w, conv_b, dt_bias, A_log, D_skip, sinks, attn_out_norm, ssm_out_norm, w_o, norm2, w_gate_up, w_down, rel_bias, final_norm, loss_target, m_ada_w, m_ada_b, m_norm1, m_w_in, m_conv_w, m_conv_b, m_dt_bias, m_A_log, m_D_skip, m_sinks, m_attn_out_norm, m_ssm_out_norm, m_w_o, m_norm2, m_w_gate_up, m_w_down, m_rel_bias, m_final_norm, v_ada_w, v_ada_b, v_norm1, v_w_in, v_conv_w, v_conv_b, v_dt_bias, v_A_log, v_D_skip, v_sinks, v_attn_out_norm, v_ssm_out_norm, v_w_o, v_norm2, v_w_gate_up, v_w_down, v_rel_bias, v_final_norm):
    two_d = lambda a: a if a.ndim == 2 else a.reshape(-1, a.shape[-1])
    small_params = dict(
        ada_b=(ada_b, m_ada_b, v_ada_b), norm1=(norm1, m_norm1, v_norm1), conv_w=(conv_w, m_conv_w, v_conv_w),
        conv_b=(conv_b, m_conv_b, v_conv_b), dt_bias=(dt_bias, m_dt_bias, v_dt_bias), A_log=(A_log, m_A_log, v_A_log),
        D_skip=(D_skip, m_D_skip, v_D_skip), sinks=(sinks, m_sinks, v_sinks),
        attn_out_norm=(attn_out_norm, m_attn_out_norm, v_attn_out_norm),
        ssm_out_norm=(ssm_out_norm, m_ssm_out_norm, v_ssm_out_norm), norm2=(norm2, m_norm2, v_norm2),
        rel_bias=(rel_bias, m_rel_bias, v_rel_bias), final_norm=(final_norm, m_final_norm, v_final_norm))
    small_params = {k: tuple(two_d(a) for a in v) for k, v in small_params.items()}
    S = x.shape[1]
    xs, tgt = x.reshape(S, D_MODEL), loss_target.reshape(S, D_MODEL)
    ada_w2 = ada_w[0]
    chunk = ada_w2.shape[1]
    t_in = [jnp.transpose(a[0]) for a in (w_in, m_w_in, v_w_in)]
    t_gu = [jnp.transpose(a[0]) for a in (w_gate_up, m_w_gate_up, v_w_gate_up)]

    mod, (g_in, g_cw) = _mod_and_gather(c, ada_w2, ada_b.reshape(N_DEV, chunk), [t_in[0].astype(WIRE_DTYPE), conv_w[0]])
    mod = mod.reshape(6, D_MODEL)
    w_in_full = jnp.pad(g_in.reshape(IN_W, D_MODEL), ((0, IN_PAD - IN_W), (0, 0)))
    conv_w_full = jnp.transpose(g_cw, (1, 0, 2)).reshape(4, XBC_W)

    p = {k: v[0] for k, v in small_params.items()}
    gx, (r_in_a, gw_in_b), (r_o, r_gu, r_d), gathered = _local_step(
        xs, tgt, c, mod, w_in_full, conv_w_full, w_o[0].astype(WIRE_DTYPE), t_gu[0].astype(WIRE_DTYPE),
        w_down[0].astype(WIRE_DTYPE), p)

    _, (r_in_b,) = _hosted_call(lambda: None, "scatter_in_b", 1, [], [], [], [], (),
                                ([gw_in_b], "two-level scatter"), _cparams())
    (u_gu, u_d, u_o), _ = _reduce_adamw_hosting(
        [r_gu, r_d, r_o], [tuple(t_gu), (w_down[0], m_w_down[0], v_w_down[0]), (w_o[0], m_w_o[0], v_w_o[0])],
        "adamw_big", [])

    small, loss, c_all, dmod_all = _small_update(gathered, small_params)

    big = {
        "ada_w": _ada_w_update(c_all, dmod_all, ada_w2, m_ada_w[0], v_ada_w[0]),
        "w_in": [jnp.transpose(a) for a in _reduce_adamw_halves(r_in_a, r_in_b, *t_in, "adamw_w_in")],
        "w_o": u_o,
        "w_gate_up": [jnp.transpose(a) for a in u_gu],
        "w_down": u_d,
    }
    big.update(small)

    order = ['ada_w', 'ada_b', 'norm1', 'w_in', 'conv_w', 'conv_b', 'dt_bias', 'A_log', 'D_skip', 'sinks',
             'attn_out_norm', 'ssm_out_norm', 'w_o', 'norm2', 'w_gate_up', 'w_down', 'rel_bias', 'final_norm']
    shapes = dict(ada_w=ada_w.shape, ada_b=ada_b.shape, norm1=norm1.shape, w_in=w_in.shape, conv_w=conv_w.shape,
                  conv_b=conv_b.shape, dt_bias=dt_bias.shape, A_log=A_log.shape, D_skip=D_skip.shape,
                  sinks=sinks.shape, attn_out_norm=attn_out_norm.shape, ssm_out_norm=ssm_out_norm.shape,
                  w_o=w_o.shape, norm2=norm2.shape, w_gate_up=w_gate_up.shape, w_down=w_down.shape,
                  rel_bias=rel_bias.shape, final_norm=final_norm.shape)
    outs = [[], [], [], []]
    for name in order:
        for kind in range(4):
            outs[kind].append(big[name][kind].reshape(shapes[name]))
    return (loss.reshape(()), gx.reshape(x.shape), *outs[0], *outs[1], *outs[2], *outs[3])
```
